```python
import math
import jax, jax.numpy as jnp
from jax import lax
import numpy as np


D_MODEL = 1024
BATCH = 8
SEQ = 4096
DEPTH = 1

D_INNER = 2 * D_MODEL
ATTN_HEADS = 16
ATTN_HEAD_DIM = 64
D_ATTN = ATTN_HEADS * ATTN_HEAD_DIM
DILATED_PATTERNS = ((128, 1), (512, 4), (2048, 16))
ATTN_BLOCK = 128
D_SSM = D_INNER - D_ATTN
SSM_HEAD_DIM = 64
SSM_HEADS = D_SSM // SSM_HEAD_DIM
SSM_GROUPS = 2
D_STATE = 128
CONV_K = 4
CHUNK = 128
D_CONV = D_SSM + 2 * SSM_GROUPS * D_STATE
D_IN_PROJ = 4 * D_ATTN + D_SSM + D_CONV + SSM_HEADS
NORM_EPS = 1e-6
DT_MIN = 1e-3
DT_MAX = 1e-1

kernel_name = 'hymba_dilated_attn_mamba2_hybrid'


def rms_norm(x, w):
    xf = x.astype(jnp.float32)
    y = xf * lax.rsqrt(jnp.mean(xf * xf, axis=-1, keepdims=True) + NORM_EPS)
    return (y * w.astype(jnp.float32)).astype(x.dtype)


def dilated_window_attention(q, k, v, window, dilation):
    b, s, h, dh = q.shape
    n = s // dilation
    nb = -(-n // ATTN_BLOCK)
    n_pad = nb * ATTN_BLOCK
    span = window // dilation

    def to_sub(t):
        t = t.reshape(b, n, dilation, h, dh).transpose(0, 2, 3, 1, 4)
        t = jnp.pad(t, ((0, 0), (0, 0), (0, 0), (0, n_pad - n), (0, 0)))
        return t.reshape(b, dilation, h, nb, ATTN_BLOCK, dh)

    def with_prev(t):
        prev = jnp.pad(t, ((0, 0), (0, 0), (0, 0), (1, 0), (0, 0), (0, 0)))[:, :, :, :-1]
        return jnp.concatenate([prev, t], axis=4)

    qb = to_sub(q)
    kw = with_prev(to_sub(k))
    vw = with_prev(to_sub(v))
    scores = jnp.einsum('bdhnqc,bdhnkc->bdhnqk', qb, kw, preferred_element_type=jnp.float32)

    qi = jnp.arange(ATTN_BLOCK)[:, None]
    kj = jnp.arange(2 * ATTN_BLOCK)[None, :]
    dist = ATTN_BLOCK + qi - kj
    key_idx = jnp.arange(nb)[:, None, None] * ATTN_BLOCK - ATTN_BLOCK + kj[None]
    valid = (dist >= 0) & (dist <= span) & (key_idx >= 0)
    scores = jnp.where(valid, scores, -jnp.inf)

    m = jnp.max(scores, axis=-1, keepdims=True)
    p = jnp.exp(scores - m)
    l = jnp.sum(p, axis=-1, keepdims=True)
    o = jnp.einsum('bdhnqk,bdhnkc->bdhnqc', p, vw.astype(jnp.float32)) / l
    lse = (m + jnp.log(l))[..., 0]

    o = o.reshape(b, dilation, h, n_pad, dh)[:, :, :, :n].transpose(0, 3, 1, 2, 4).reshape(b, s, h, dh)
    lse = lse.reshape(b, dilation, h, n_pad)[:, :, :, :n].transpose(0, 3, 1, 2).reshape(b, s, h)
    return o, lse


def mixture_of_dilations(q, k, v):
    outs, lses = [], []
    for window, dilation in DILATED_PATTERNS:
        o, lse = dilated_window_attention(q, k, v, window, dilation)
        outs.append(o)
        lses.append(lse)
    weights = jax.nn.softmax(jnp.stack(lses), axis=0)
    return jnp.einsum('pbsh,pbshd->bshd', weights, jnp.stack(outs))


def causal_depthwise_conv(x, w, bias):
    c = x.shape[-1]
    y = lax.conv_general_dilated(
        x, w[:, None, :].astype(x.dtype), window_strides=(1,), padding=[(CONV_K - 1, 0)],
        dimension_numbers=('NWC', 'WIO', 'NWC'), feature_group_count=c)
    return y + bias.astype(x.dtype)


def ssd_chunked(x, dt, a, b_mat, c_mat):
    bsz, s, h, p = x.shape
    g = SSM_GROUPS
    e = h // g
    n = b_mat.shape[-1]
    nc = s // CHUNK
    xdt = (x.astype(jnp.float32) * dt[..., None]).reshape(bsz, nc, CHUNK, g, e, p)
    a_dt = (dt * a).reshape(bsz, nc, CHUNK, g, e).transpose(0, 3, 4, 1, 2)
    bc = b_mat.astype(jnp.float32).reshape(bsz, nc, CHUNK, g, n)
    cc = c_mat.astype(jnp.float32).reshape(bsz, nc, CHUNK, g, n)
    a_cs = jnp.cumsum(a_dt, axis=-1)

    causal = jnp.tril(jnp.ones((CHUNK, CHUNK), dtype=bool))
    seg = a_cs[..., :, None] - a_cs[..., None, :]
    decay = jnp.exp(jnp.where(causal, seg, -jnp.inf))
    cb = jnp.einsum('bclgn,bcsgn->bgcls', cc, bc)
    y_diag = jnp.einsum('bgecls,bcsgep->bclgep', cb[:, :, None] * decay, xdt)

    decay_states = jnp.exp(a_cs[..., -1:] - a_cs)
    states = jnp.einsum('bclgn,bgecl,bclgep->bcgepn', bc, decay_states, xdt)

    chunk_decay = jnp.exp(a_cs[..., -1])

    def step(h_prev, inp):
        st, dec = inp
        return h_prev * dec[..., None, None] + st, h_prev

    init = jnp.zeros((bsz, g, e, p, n), jnp.float32)
    _, prev = lax.scan(step, init, (states.transpose(1, 0, 2, 3, 4, 5), chunk_decay.transpose(3, 0, 1, 2)))
    prev = prev.transpose(1, 0, 2, 3, 4, 5)

    y_off = jnp.einsum('bclgn,bcgepn,bgecl->bclgep', cc, prev, jnp.exp(a_cs))
    return (y_diag + y_off).reshape(bsz, s, h, p)


def gated_group_rms_norm(y, z, w):
    yz = y.astype(jnp.float32) * jax.nn.silu(z.astype(jnp.float32))
    shp = yz.shape
    yz = yz.reshape(shp[:-1] + (SSM_GROUPS, shp[-1] // SSM_GROUPS))
    yz = yz * lax.rsqrt(jnp.mean(yz * yz, axis=-1, keepdims=True) + NORM_EPS)
    return yz.reshape(shp) * w.astype(jnp.float32)


def hybrid_layer(hid, norm_pre_w, w_in, conv_w, conv_b, dt_bias, a_log, d_skip, ssm_norm_w, w_out, norm_post_w):
    bsz, s, _ = hid.shape
    u = rms_norm(hid, norm_pre_w)
    proj = jnp.einsum('bsd,de->bse', u, w_in.astype(u.dtype))
    sizes = [D_ATTN, D_ATTN, D_ATTN, D_ATTN, D_SSM, D_CONV]
    q, k, v, g_attn, z, xbc, dt_raw = jnp.split(proj, np.cumsum(sizes).tolist(), axis=-1)

    q = q.reshape(bsz, s, ATTN_HEADS, ATTN_HEAD_DIM) * (ATTN_HEAD_DIM ** -0.5)
    k = k.reshape(bsz, s, ATTN_HEADS, ATTN_HEAD_DIM)
    v = v.reshape(bsz, s, ATTN_HEADS, ATTN_HEAD_DIM)
    attn = mixture_of_dilations(q, k, v).reshape(bsz, s, D_ATTN)
    attn = attn * jax.nn.silu(g_attn.astype(jnp.float32))

    xbc = jax.nn.silu(causal_depthwise_conv(xbc, conv_w, conv_b))
    xs, b_mat, c_mat = jnp.split(xbc, [D_SSM, D_SSM + SSM_GROUPS * D_STATE], axis=-1)
    xs = xs.reshape(bsz, s, SSM_HEADS, SSM_HEAD_DIM)
    b_mat = b_mat.reshape(bsz, s, SSM_GROUPS, D_STATE)
    c_mat = c_mat.reshape(bsz, s, SSM_GROUPS, D_STATE)
    dt = jax.nn.softplus(dt_raw.astype(jnp.float32) + dt_bias.astype(jnp.float32))
    a = -jnp.exp(a_log.astype(jnp.float32))
    y = ssd_chunked(xs, dt, a, b_mat, c_mat) + d_skip.astype(jnp.float32)[:, None] * xs.astype(jnp.float32)
    y = gated_group_rms_norm(y.reshape(bsz, s, D_SSM), z, ssm_norm_w)

    mix = jnp.concatenate([attn, y], axis=-1).astype(hid.dtype)
    out = jnp.einsum('bse,ed->bsd', mix, w_out.astype(mix.dtype))
    return hid + rms_norm(out, norm_post_w)


def _fwd_setup_inputs(seed: int = 0) -> dict:
    key = jax.random.key(seed)
    ks = jax.random.split(key, 12)
    f32 = jnp.float32
    x = jax.random.normal(ks[0], (BATCH, SEQ, D_MODEL), f32)
    norm_pre_w = 1.0 + 0.1 * jax.random.normal(ks[1], (DEPTH, D_MODEL), f32)
    w_in = jax.random.normal(ks[2], (DEPTH, D_MODEL, D_IN_PROJ), f32) * D_MODEL ** -0.5
    conv_w = jax.random.normal(ks[3], (DEPTH, CONV_K, D_CONV), f32) * CONV_K ** -0.5
    conv_b = 0.02 * jax.random.normal(ks[4], (DEPTH, D_CONV), f32)
    dt0 = jnp.exp(jax.random.uniform(ks[5], (DEPTH, SSM_HEADS), f32, math.log(DT_MIN), math.log(DT_MAX)))
    dt_bias = dt0 + jnp.log(-jnp.expm1(-dt0))
    a_log = jnp.log(jax.random.uniform(ks[6], (DEPTH, SSM_HEADS), f32, 1.0, 16.0))
    d_skip = 1.0 + 0.1 * jax.random.normal(ks[7], (DEPTH, SSM_HEADS), f32)
    ssm_norm_w = 1.0 + 0.1 * jax.random.normal(ks[8], (DEPTH, D_SSM), f32)
    w_out = jax.random.normal(ks[9], (DEPTH, D_INNER, D_MODEL), f32) * D_INNER ** -0.5
    norm_post_w = 1.0 + 0.1 * jax.random.normal(ks[10], (DEPTH, D_MODEL), f32)
    return {'x': x, 'norm_pre_w': norm_pre_w, 'w_in': w_in, 'conv_w': conv_w, 'conv_b': conv_b,
            'dt_bias': dt_bias, 'a_log': a_log, 'd_skip': d_skip, 'ssm_norm_w': ssm_norm_w,
            'w_out': w_out, 'norm_post_w': norm_post_w}


def _fwd_reference(x, norm_pre_w, w_in, conv_w, conv_b, dt_bias, a_log, d_skip, ssm_norm_w, w_out, norm_post_w):
    hid = x
    for layer in range(DEPTH):
        hid = hybrid_layer(hid, norm_pre_w[layer], w_in[layer], conv_w[layer], conv_b[layer],
                           dt_bias[layer], a_log[layer], d_skip[layer], ssm_norm_w[layer],
                           w_out[layer], norm_post_w[layer])
    return hid


import jax as _jax
import jax.numpy as _jnp

TWIN_FORMAT = 'train_step'
FWD_PARAMS = ['x', 'norm_pre_w', 'w_in', 'conv_w', 'conv_b', 'dt_bias', 'a_log', 'd_skip', 'ssm_norm_w', 'w_out', 'norm_post_w']
TWIN_WEIGHTS = ['norm_pre_w', 'w_in', 'conv_w', 'conv_b', 'dt_bias', 'a_log', 'd_skip', 'ssm_norm_w', 'w_out', 'norm_post_w']
TWIN_DIFF_INPUT = 'x'
TWIN_INPUTS = ['x', 'norm_pre_w', 'w_in', 'conv_w', 'conv_b', 'dt_bias', 'a_log', 'd_skip', 'ssm_norm_w', 'w_out', 'norm_post_w', 'loss_target', 'm_norm_pre_w', 'm_w_in', 'm_conv_w', 'm_conv_b', 'm_dt_bias', 'm_a_log', 'm_d_skip', 'm_ssm_norm_w', 'm_w_out', 'm_norm_post_w', 'v_norm_pre_w', 'v_w_in', 'v_conv_w', 'v_conv_b', 'v_dt_bias', 'v_a_log', 'v_d_skip', 'v_ssm_norm_w', 'v_w_out', 'v_norm_post_w']
TWIN_OUTPUTS = ['loss', 'grad_x', 'grad_norm_pre_w', 'grad_w_in', 'grad_conv_w', 'grad_conv_b', 'grad_dt_bias', 'grad_a_log', 'grad_d_skip', 'grad_ssm_norm_w', 'grad_w_out', 'grad_norm_post_w', 'delta_norm_pre_w', 'delta_w_in', 'delta_conv_w', 'delta_conv_b', 'delta_dt_bias', 'delta_a_log', 'delta_d_skip', 'delta_ssm_norm_w', 'delta_w_out', 'delta_norm_post_w', 'new_m_norm_pre_w', 'new_m_w_in', 'new_m_conv_w', 'new_m_conv_b', 'new_m_dt_bias', 'new_m_a_log', 'new_m_d_skip', 'new_m_ssm_norm_w', 'new_m_w_out', 'new_m_norm_post_w', 'new_v_norm_pre_w', 'new_v_w_in', 'new_v_conv_w', 'new_v_conv_b', 'new_v_dt_bias', 'new_v_a_log', 'new_v_d_skip', 'new_v_ssm_norm_w', 'new_v_w_out', 'new_v_norm_post_w']
TWIN_LEAF_KINDS = {'loss': 'loss', 'grad_x': 'grad_x', 'grad_norm_pre_w': 'grad_w', 'grad_w_in': 'grad_w', 'grad_conv_w': 'grad_w', 'grad_conv_b': 'grad_w', 'grad_dt_bias': 'grad_w', 'grad_a_log': 'grad_w', 'grad_d_skip': 'grad_w', 'grad_ssm_norm_w': 'grad_w', 'grad_w_out': 'grad_w', 'grad_norm_post_w': 'grad_w', 'delta_norm_pre_w': 'delta_w', 'delta_w_in': 'delta_w', 'delta_conv_w': 'delta_w', 'delta_conv_b': 'delta_w', 'delta_dt_bias': 'delta_w', 'delta_a_log': 'delta_w', 'delta_d_skip': 'delta_w', 'delta_ssm_norm_w': 'delta_w', 'delta_w_out': 'delta_w', 'delta_norm_post_w': 'delta_w', 'new_m_norm_pre_w': 'new_m', 'new_m_w_in': 'new_m', 'new_m_conv_w': 'new_m', 'new_m_conv_b': 'new_m', 'new_m_dt_bias': 'new_m', 'new_m_a_log': 'new_m', 'new_m_d_skip': 'new_m', 'new_m_ssm_norm_w': 'new_m', 'new_m_w_out': 'new_m', 'new_m_norm_post_w': 'new_m', 'new_v_norm_pre_w': 'new_v', 'new_v_w_in': 'new_v', 'new_v_conv_w': 'new_v', 'new_v_conv_b': 'new_v', 'new_v_dt_bias': 'new_v', 'new_v_a_log': 'new_v', 'new_v_d_skip': 'new_v', 'new_v_ssm_norm_w': 'new_v', 'new_v_w_out': 'new_v', 'new_v_norm_post_w': 'new_v'}


def _forward(args):
    return _fwd_reference(*[args[k] for k in FWD_PARAMS])


def _output_shape():
    out = _jax.eval_shape(lambda: _forward(_fwd_setup_inputs(0)))
    return out.shape, out.dtype

N_MICROBATCH = 1
ADAM_LR = 0.001
ADAM_B1 = 0.9
ADAM_B2 = 0.999
ADAM_EPS = 1e-08
ADAM_WD = 0.01
ADAM_STEP = 10
PER_EXAMPLE_BATCH_AXIS = {'x': 0, 'loss_target': 0}
SHARED_INPUTS = []
_WEIGHT_DTYPES = {'norm_pre_w': _jnp.float32, 'w_in': _jnp.float32, 'conv_w': _jnp.float32, 'conv_b': _jnp.float32, 'dt_bias': _jnp.float32, 'a_log': _jnp.float32, 'd_skip': _jnp.float32, 'ssm_norm_w': _jnp.float32, 'w_out': _jnp.float32, 'norm_post_w': _jnp.float32}
MOMENT_SCALE = {'norm_pre_w': 5.015503e-01, 'w_in': 1.795040e-01, 'conv_w': 2.833967e-01, 'conv_b': 6.123499e-01, 'dt_bias': 6.942603e-01, 'a_log': 1.711131e+00, 'd_skip': 2.019790e+00, 'ssm_norm_w': 4.164788e-01, 'w_out': 4.073794e-01, 'norm_post_w': 3.228916e+01}


def _to_microbatches(a, axis):
    t = _jnp.moveaxis(a, axis, 0)
    t = t.reshape((N_MICROBATCH, t.shape[0] // N_MICROBATCH) + t.shape[1:])
    return _jnp.moveaxis(t, 1, axis + 1)


def setup_inputs(seed: int = 0) -> dict:
    inp = _fwd_setup_inputs(seed)
    key = _jax.random.fold_in(_jax.random.key(seed), 7919)
    shape, _ = _output_shape()
    out = dict(inp)
    out["loss_target"] = _jax.random.normal(_jax.random.fold_in(key, 0), shape, _jnp.float32)
    for i, name in enumerate(TWIN_WEIGHTS):
        w = inp[name].astype(_jnp.float32)
        if MOMENT_SCALE is None:
            s = _jnp.sqrt(_jnp.mean(_jnp.square(w)) + 1e-30)
        else:
            s = MOMENT_SCALE[name]
        km, kv = _jax.random.split(_jax.random.fold_in(key, i + 1))
        out[name] = w
        out["m_" + name] = s * _jax.random.normal(km, w.shape, _jnp.float32)
        out["v_" + name] = (s * s) * _jax.random.uniform(kv, w.shape, _jnp.float32, 0.5, 1.5)
    if N_MICROBATCH > 1:
        for name, axis in PER_EXAMPLE_BATCH_AXIS.items():
            out[name] = _to_microbatches(out[name], axis)
    return {'x': out['x'], 'norm_pre_w': out['norm_pre_w'], 'w_in': out['w_in'], 'conv_w': out['conv_w'], 'conv_b': out['conv_b'], 'dt_bias': out['dt_bias'], 'a_log': out['a_log'], 'd_skip': out['d_skip'], 'ssm_norm_w': out['ssm_norm_w'], 'w_out': out['w_out'], 'norm_post_w': out['norm_post_w'], 'loss_target': out['loss_target'], 'm_norm_pre_w': out['m_norm_pre_w'], 'm_w_in': out['m_w_in'], 'm_conv_w': out['m_conv_w'], 'm_conv_b': out['m_conv_b'], 'm_dt_bias': out['m_dt_bias'], 'm_a_log': out['m_a_log'], 'm_d_skip': out['m_d_skip'], 'm_ssm_norm_w': out['m_ssm_norm_w'], 'm_w_out': out['m_w_out'], 'm_norm_post_w': out['m_norm_post_w'], 'v_norm_pre_w': out['v_norm_pre_w'], 'v_w_in': out['v_w_in'], 'v_conv_w': out['v_conv_w'], 'v_conv_b': out['v_conv_b'], 'v_dt_bias': out['v_dt_bias'], 'v_a_log': out['v_a_log'], 'v_d_skip': out['v_d_skip'], 'v_ssm_norm_w': out['v_ssm_norm_w'], 'v_w_out': out['v_w_out'], 'v_norm_post_w': out['v_norm_post_w']}


def _loss(weights, diff, rest, loss_target):
    with _jax.named_scope("forward"):
        args = {**rest, TWIN_DIFF_INPUT: diff, **{k: w.astype(_WEIGHT_DTYPES[k]) for k, w in weights.items()}}
        y = _forward(args)
    with _jax.named_scope("loss_head"):
        err = _jnp.square(y.astype(_jnp.float32) - loss_target)
        return 0.5 * _jnp.sum(_jnp.mean(err, axis=-1)) if err.ndim else 0.5 * err


def _adamw(w, g, m, v):
    m = ADAM_B1 * m + (1.0 - ADAM_B1) * g
    v = ADAM_B2 * v + (1.0 - ADAM_B2) * _jnp.square(g)
    m_hat = m / (1.0 - ADAM_B1 ** ADAM_STEP)
    v_hat = v / (1.0 - ADAM_B2 ** ADAM_STEP)
    delta = -ADAM_LR * (m_hat / (_jnp.sqrt(v_hat) + ADAM_EPS) + ADAM_WD * w)
    return delta, m, v


def reference(x, norm_pre_w, w_in, conv_w, conv_b, dt_bias, a_log, d_skip, ssm_norm_w, w_out, norm_post_w, loss_target, m_norm_pre_w, m_w_in, m_conv_w, m_conv_b, m_dt_bias, m_a_log, m_d_skip, m_ssm_norm_w, m_w_out, m_norm_post_w, v_norm_pre_w, v_w_in, v_conv_w, v_conv_b, v_dt_bias, v_a_log, v_d_skip, v_ssm_norm_w, v_w_out, v_norm_post_w):
    given = dict(x=x, norm_pre_w=norm_pre_w, w_in=w_in, conv_w=conv_w, conv_b=conv_b, dt_bias=dt_bias, a_log=a_log, d_skip=d_skip, ssm_norm_w=ssm_norm_w, w_out=w_out, norm_post_w=norm_post_w, loss_target=loss_target, m_norm_pre_w=m_norm_pre_w, m_w_in=m_w_in, m_conv_w=m_conv_w, m_conv_b=m_conv_b, m_dt_bias=m_dt_bias, m_a_log=m_a_log, m_d_skip=m_d_skip, m_ssm_norm_w=m_ssm_norm_w, m_w_out=m_w_out, m_norm_post_w=m_norm_post_w, v_norm_pre_w=v_norm_pre_w, v_w_in=v_w_in, v_conv_w=v_conv_w, v_conv_b=v_conv_b, v_dt_bias=v_dt_bias, v_a_log=v_a_log, v_d_skip=v_d_skip, v_ssm_norm_w=v_ssm_norm_w, v_w_out=v_w_out, v_norm_post_w=v_norm_post_w)
    weights = {n: given[n] for n in TWIN_WEIGHTS}
    shared = {n: given[n] for n in SHARED_INPUTS}
    per_example = {n: given[n] for n in ['x']}
    grad_fn = _jax.value_and_grad(_loss, argnums=(0, 1))

    def one_microbatch(ex, loss_target):
        ex = dict(ex)
        diff = ex.pop(TWIN_DIFF_INPUT)
        return grad_fn(weights, diff, {**shared, **ex}, loss_target)

    if N_MICROBATCH == 1:
        loss, (grad_w, grad_x) = one_microbatch(per_example, given["loss_target"])
    else:
        def body(carry, xs):
            loss_sum, grad_sum = carry
            l_k, (gw_k, gx_k) = one_microbatch(xs[0], xs[1])
            with _jax.named_scope("update"):
                return (loss_sum + l_k, _jax.tree.map(_jnp.add, grad_sum, gw_k)), gx_k

        init = (_jnp.zeros((), _jnp.float32), _jax.tree.map(_jnp.zeros_like, weights))
        (loss, grad_w), grad_x = _jax.lax.scan(body, init, (per_example, given["loss_target"]))
    with _jax.named_scope("update"):
        delta_w, new_m, new_v = {}, {}, {}
        for n in TWIN_WEIGHTS:
            delta_w[n], new_m[n], new_v[n] = _adamw(weights[n], grad_w[n], given["m_" + n], given["v_" + n])
    return (loss, grad_x, *[grad_w[n] for n in TWIN_WEIGHTS], *[delta_w[n] for n in TWIN_WEIGHTS],
            *[new_m[n] for n in TWIN_WEIGHTS], *[new_v[n] for n in TWIN_WEIGHTS])
```

```python
import functools

import jax
import jax.numpy as jnp
from jax import lax
from jax.experimental import pallas as pl
from jax.experimental.pallas import tpu as pltpu

F32 = jnp.float32
BF16 = jnp.bfloat16
HI = lax.Precision.HIGHEST
MESH = pl.DeviceIdType.MESH
SDS = jax.ShapeDtypeStruct
ANY = pl.BlockSpec(memory_space=pl.ANY)

S = 4096
D = 1024
DP = 7168
SHARD = 1668
OFF_G, OFF_Z = 3072, 4096
NH = 16
CH = 128
NC = S // CH
EPS = 1e-6
NEG = -1e30
LANE = 128
VMEM_LIMIT = 48 * 1024 * 1024

ADAM_LR, ADAM_B1, ADAM_B2, ADAM_EPS, ADAM_WD, ADAM_STEP = 0.001, 0.9, 0.999, 1e-08, 0.01, 10


def _cp(sem, **kw):
    return pltpu.CompilerParams(dimension_semantics=sem, vmem_limit_bytes=VMEM_LIMIT, **kw)


def _dot(a, b):
    return jnp.dot(a, b, preferred_element_type=F32)


def _dot_nt(a, b):
    return lax.dot_general(a, b, (((1,), (1,)), ((), ())), preferred_element_type=F32)


def _dot_tn(a, b):
    return lax.dot_general(a, b, (((0,), (0,)), ((), ())), preferred_element_type=F32)


def _dot_hi(a, b):
    return jnp.dot(a, b, precision=HI, preferred_element_type=F32)


def _sigmoid(v):
    return 1.0 / (1.0 + jnp.exp(-v))


def _iota(shape, dim):
    return lax.broadcasted_iota(jnp.int32, shape, dim)


def _inproj_fwd(x, nw, w_all):
    tm, tn = 512, 1024

    def body(x_ref, nw_ref, w_ref, proj_ref, u_ref):
        @pl.when(pl.program_id(1) == 0)
        def _():
            xf = x_ref[...]
            r = lax.rsqrt(jnp.mean(xf * xf, axis=-1, keepdims=True) + EPS)
            u_ref[...] = (xf * r * nw_ref[...]).astype(BF16)

        proj_ref[...] = _dot(u_ref[...], w_ref[...])

    return pl.pallas_call(
        body, name="inproj_fwd", grid=(S // tm, DP // tn),
        in_specs=[pl.BlockSpec((tm, D), lambda i, j: (i, 0)), pl.BlockSpec((1, D), lambda i, j: (0, 0)),
                  pl.BlockSpec((D, tn), lambda i, j: (0, j))],
        out_specs=[pl.BlockSpec((tm, tn), lambda i, j: (i, j)), pl.BlockSpec((tm, D), lambda i, j: (i, 0))],
        out_shape=[SDS((S, DP), F32), SDS((S, D), BF16)],
        compiler_params=_cp(("parallel", "arbitrary")),
    )(x, nw, w_all)


def _rows(r, d):
    return pl.ds(r, CH, stride=d) if d > 1 else pl.ds(0, CH)


def _for_residues(d, fn):
    if d <= 4:
        for r in range(d):
            fn(r)
    else:
        lax.fori_loop(0, d, lambda r, c: (fn(r), c)[1], 0)


def _attn_masks(i):
    row, col = _iota((CH, CH), 0), _iota((CH, CH), 1)
    return col < 64, (col >= row) & (i > 0), col <= row


def _attn_fwd(proj, d):
    rb = CH * d
    nb = S // rb

    def body(q_ref, kp_ref, kc_ref, vp_ref, vc_ref, o_ref, l_ref):
        lane_a, mask_p, mask_c = _attn_masks(pl.program_id(1))

        def residue(r):
            sl = _rows(r, d)
            q = q_ref[sl, :] * 0.125
            kp, kc = kp_ref[sl, :].astype(BF16), kc_ref[sl, :].astype(BF16)
            vp, vc = vp_ref[sl, :].astype(BF16), vc_ref[sl, :].astype(BF16)
            outs = []
            for head_a in (True, False):
                lane = lane_a if head_a else jnp.logical_not(lane_a)
                qh = jnp.where(lane, q, 0.0).astype(BF16)
                sp = jnp.where(mask_p, _dot_nt(qh, kp), NEG)
                sc = jnp.where(mask_c, _dot_nt(qh, kc), NEG)
                m = jnp.maximum(jnp.max(sp, axis=1, keepdims=True), jnp.max(sc, axis=1, keepdims=True))
                pp, pc = jnp.exp(sp - m), jnp.exp(sc - m)
                l = jnp.sum(pp, axis=1, keepdims=True) + jnp.sum(pc, axis=1, keepdims=True)
                o = (_dot(pp.astype(BF16), vp) + _dot(pc.astype(BF16), vc)) / l
                outs.append((o, m + jnp.log(l)))
            o_ref[sl, :] = jnp.where(lane_a, outs[0][0], outs[1][0])
            l_ref[sl, :] = jnp.where(lane_a, outs[0][1], outs[1][1])

        _for_residues(d, residue)

    def spec(off, prev):
        if prev:
            return pl.BlockSpec((rb, LANE), lambda c, i: (jnp.maximum(i - 1, 0), off + c))
        return pl.BlockSpec((rb, LANE), lambda c, i: (i, off + c))

    out_spec = pl.BlockSpec((rb, LANE), lambda c, i: (i, c))
    return pl.pallas_call(
        body, name=f"attn_fwd_d{d}", grid=(NH // 2, nb),
        in_specs=[spec(0, False), spec(8, True), spec(8, False), spec(16, True), spec(16, False)],
        out_specs=[out_spec, out_spec],
        out_shape=[SDS((S, D), F32), SDS((S, D), F32)],
        compiler_params=_cp(("parallel", "arbitrary")),
    )(proj, proj, proj, proj, proj)


def _attn_merge(o1, o2, o3, l1, l2, l3, proj):
    tm = 512

    def body(o1r, o2r, o3r, l1r, l2r, l3r, g_ref, mix_ref, pre_ref, lse_ref):
        l1v, l2v, l3v = l1r[...], l2r[...], l3r[...]
        m = jnp.maximum(jnp.maximum(l1v, l2v), l3v)
        e1, e2, e3 = jnp.exp(l1v - m), jnp.exp(l2v - m), jnp.exp(l3v - m)
        tot = e1 + e2 + e3
        out = (e1 * o1r[...] + e2 * o2r[...] + e3 * o3r[...]) / tot
        g = g_ref[...]
        pre_ref[...] = out
        lse_ref[...] = m + jnp.log(tot)
        mix_ref[...] = (out * (g * _sigmoid(g))).astype(BF16)

    t = pl.BlockSpec((tm, D), lambda i: (i, 0))
    return pl.pallas_call(
        body, name="attn_merge", grid=(S // tm,),
        in_specs=[t, t, t, t, t, t, pl.BlockSpec((tm, D), lambda i: (i, OFF_G // D))],
        out_specs=[t, t, t],
        out_shape=[SDS((S, 2 * D), BF16), SDS((S, D), F32), SDS((S, D), F32)],
        compiler_params=_cp(("parallel",)),
    )(o1, o2, o3, l1, l2, l3, proj)


def _attn_gate_bwd(dmix, pre, proj):
    tm = 512

    def body(dm_ref, pre_ref, g_ref, do_ref, delta_ref, dg_ref):
        g, dm, pre_v = g_ref[...], dm_ref[...], pre_ref[...]
        sig = _sigmoid(g)
        do = dm * (g * sig)
        do_ref[...] = do
        dg_ref[...] = dm * pre_v * (sig * (1.0 + g * (1.0 - sig)))
        prod = do * pre_v
        same_head = (_iota((LANE, LANE), 0) // 64 == _iota((LANE, LANE), 1) // 64).astype(F32)
        for cb in range(D // LANE):
            delta_ref[:, cb * LANE:(cb + 1) * LANE] = _dot_hi(prod[:, cb * LANE:(cb + 1) * LANE], same_head)

    t = pl.BlockSpec((tm, D), lambda i: (i, 0))
    return pl.pallas_call(
        body, name="attn_gate_bwd", grid=(S // tm,),
        in_specs=[t, t, pl.BlockSpec((tm, D), lambda i: (i, OFF_G // D))],
        out_specs=[t, t, t],
        out_shape=[SDS((S, D), F32)] * 3,
        compiler_params=_cp(("parallel",)),
    )(dmix, pre, proj)


def _attn_bwd(proj, do, lse, delta, d, acc):
    rb = CH * d
    nb = S // rb
    has_acc = acc is not None

    def body(*refs):
        q_ref, kp_ref, kc_ref, vp_ref, vc_ref, do_ref, lse_ref, dl_ref = refs[:8]
        if has_acc:
            aq_ref, ak_ref, av_ref = refs[8:11]
        dq_ref, dk_ref, dv_ref, ck_ref, cv_ref = refs[-5:]
        i = pl.program_id(1)
        lane_a, mask_p, mask_c = _attn_masks(i)

        @pl.when(i == 0)
        def _():
            ck_ref[...] = jnp.zeros_like(ck_ref)
            cv_ref[...] = jnp.zeros_like(cv_ref)

        def residue(r):
            sl = _rows(r, d)
            q = q_ref[sl, :] * 0.125
            qb = q.astype(BF16)
            kp, kc = kp_ref[sl, :].astype(BF16), kc_ref[sl, :].astype(BF16)
            vp, vc = vp_ref[sl, :].astype(BF16), vc_ref[sl, :].astype(BF16)
            dov, lsev, dlv = do_ref[sl, :], lse_ref[sl, :], dl_ref[sl, :]
            dob = dov.astype(BF16)
            parts = []
            for head_a in (True, False):
                lane = lane_a if head_a else jnp.logical_not(lane_a)
                c0 = 0 if head_a else 64
                qh = jnp.where(lane, q, 0.0).astype(BF16)
                doh = jnp.where(lane, dov, 0.0).astype(BF16)
                lse_h, dl_h = lsev[:, c0:c0 + 1], dlv[:, c0:c0 + 1]
                pp = jnp.exp(jnp.where(mask_p, _dot_nt(qh, kp), NEG) - lse_h)
                pc = jnp.exp(jnp.where(mask_c, _dot_nt(qh, kc), NEG) - lse_h)
                dsp = (pp * (_dot_nt(doh, vp) - dl_h)).astype(BF16)
                dsc = (pc * (_dot_nt(doh, vc) - dl_h)).astype(BF16)
                parts.append((_dot(dsp, kp) + _dot(dsc, kc), _dot_tn(dsp, qb), _dot_tn(dsc, qb),
                              _dot_tn(pp.astype(BF16), dob), _dot_tn(pc.astype(BF16), dob)))
            dq, dkp, dkc, dvp, dvc = [jnp.where(lane_a, a, b) for a, b in zip(*parts)]
            dq = dq * 0.125
            dkp = dkp + ck_ref[sl, :]
            dvp = dvp + cv_ref[sl, :]
            if has_acc:
                dq, dkp, dvp = dq + aq_ref[sl, :], dkp + ak_ref[sl, :], dvp + av_ref[sl, :]
            dq_ref[sl, :] = dq
            dk_ref[sl, :] = dkp
            dv_ref[sl, :] = dvp
            ck_ref[sl, :] = dkc
            cv_ref[sl, :] = dvc

        @pl.when(i < nb)
        def _():
            _for_residues(d, residue)

        @pl.when(i == nb)
        def _():
            if has_acc:
                dk_ref[...] = ck_ref[...] + ak_ref[...]
                dv_ref[...] = cv_ref[...] + av_ref[...]
            else:
                dk_ref[...] = ck_ref[...]
                dv_ref[...] = cv_ref[...]

    last = nb - 1

    def spec(off, prev):
        if prev:
            return pl.BlockSpec((rb, LANE), lambda c, i: (jnp.clip(i - 1, 0, last), off + c))
        return pl.BlockSpec((rb, LANE), lambda c, i: (jnp.minimum(i, last), off + c))

    cur, lag = spec(0, False), spec(0, True)
    in_specs = [cur, spec(8, True), spec(8, False), spec(16, True), spec(16, False), cur, cur, cur]
    args = [proj, proj, proj, proj, proj, do, lse, delta]
    if has_acc:
        in_specs += [cur, lag, lag]
        args += list(acc)
    return pl.pallas_call(
        body, name=f"attn_bwd_d{d}", grid=(NH // 2, nb + 1),
        in_specs=in_specs, out_specs=[cur, lag, lag],
        out_shape=[SDS((S, D), F32)] * 3,
        scratch_shapes=[pltpu.VMEM((rb, LANE), F32), pltpu.VMEM((rb, LANE), F32)],
        compiler_params=_cp(("parallel", "arbitrary")),
    )(*args)


def _conv_taps(cur, prev, first):
    row = _iota(cur.shape, 0)
    prev = jnp.where(first, 0.0, prev)
    taps = [jnp.where(row < s, pltpu.roll(prev, s, 0), pltpu.roll(cur, s, 0)) for s in (3, 2, 1)]
    return taps + [cur]


def _conv(taps, w, b):
    acc = b + w[0:1, :] * taps[0]
    for k in (1, 2, 3):
        acc = acc + w[k:k + 1, :] * taps[k]
    return acc


def _expand():
    return (_iota((LANE, D), 1) // 64 == _iota((LANE, D), 0)).astype(F32)


def _reduce():
    return (_iota((D, LANE), 0) // 64 == _iota((D, LANE), 1)).astype(F32)


def _ssd_common(xs_raw, xs_prev, bc_raw, bc_prev, dt_raw, first, cw, cb, dtb, alog):
    head_lane = _iota((CH, LANE), 1) < NH
    xs_taps = _conv_taps(xs_raw, xs_prev, first)
    bc_taps = _conv_taps(bc_raw, bc_prev, first)
    xs_c = _conv(xs_taps, cw[:, :D], cb[:, :D])
    bc_c = _conv(bc_taps, cw[:, D:], cb[:, D:])
    xs = xs_c * _sigmoid(xs_c)
    bc = bc_c * _sigmoid(bc_c)
    pre = dt_raw + dtb
    dt = jnp.where(head_lane, jnp.maximum(pre, 0.0) + jnp.log(1.0 + jnp.exp(-jnp.abs(pre))), 0.0)
    a_row = jnp.where(head_lane[0:1], -jnp.exp(alog), 0.0)
    tri = (_iota((CH, CH), 1) <= _iota((CH, CH), 0)).astype(F32)
    cs = _dot_hi(tri, dt * a_row)
    cs_last = cs[CH - 1:CH, :]
    expand = _expand()
    dt_b = _dot_hi(dt, expand)
    e_b = _dot_hi(jnp.exp(cs), expand)
    f_b = _dot_hi(jnp.exp(cs_last - cs), expand)
    return dict(xs_taps=xs_taps, bc_taps=bc_taps, xs_c=xs_c, bc_c=bc_c, xs=xs, bc=bc, pre=pre, dt=dt,
                a_row=a_row, cs=cs, cs_t=cs.T, dt_b=dt_b, e_b=e_b, f_b=f_b, t_b=e_b[CH - 1:CH, :])


def _groups(bc):
    bcb = bc.astype(BF16)
    return [bcb[:, 0:128], bcb[:, 128:256]], [bcb[:, 256:384], bcb[:, 384:512]]


def _decay(q, h, tril):
    seg = q["cs"][:, h:h + 1] - q["cs_t"][h:h + 1, :]
    return jnp.exp(jnp.where(tril, seg, NEG))


def _ssm_fwd(proj, mix, cw, cb, dtb, alog, d_b, nw):
    def body(xs_ref, xsp_ref, bc_ref, bcp_ref, dt_ref, z_ref, cw_ref, cb_ref, dtb_ref, alog_ref, db_ref, nw_ref,
             mix_in_ref, mix_ref, y_ref, st_ref, h_ref):
        del mix_in_ref
        i = pl.program_id(0)

        @pl.when(i == 0)
        def _():
            h_ref[...] = jnp.zeros_like(h_ref)

        q = _ssd_common(xs_ref[...], xsp_ref[...], bc_ref[...], bcp_ref[...], dt_ref[...], i == 0,
                        cw_ref[...], cb_ref[...], dtb_ref[...], alog_ref[...])
        bg, cg = _groups(q["bc"])
        xs = q["xs"]
        xdt = xs * q["dt_b"]
        xdt_b = xdt.astype(BF16)
        h_in = h_ref[...]
        st_ref[...] = h_in
        hb = h_in.astype(BF16)
        tril = _iota((CH, CH), 1) <= _iota((CH, CH), 0)
        lane_a = _iota((CH, LANE), 1) < 64
        cbm = [_dot_nt(cg[g], bg[g]) for g in range(2)]
        pairs = []
        for hp in range(NH // 2):
            xp = xdt_b[:, hp * LANE:(hp + 1) * LANE]
            ya = _dot((cbm[hp // 4] * _decay(q, 2 * hp, tril)).astype(BF16), xp)
            yb = _dot((cbm[hp // 4] * _decay(q, 2 * hp + 1, tril)).astype(BF16), xp)
            pairs.append(jnp.where(lane_a, ya, yb))
        y_diag = jnp.concatenate(pairs, axis=1)
        y_off = jnp.concatenate([_dot(cg[g], hb[:, g * 512:(g + 1) * 512]) for g in range(2)], axis=1) * q["e_b"]
        y = y_diag + y_off + db_ref[...] * xs
        y_ref[...] = y
        xf = (xdt * q["f_b"]).astype(BF16)
        h_ref[...] = q["t_b"] * h_in + jnp.concatenate(
            [_dot_tn(bg[g], xf[:, g * 512:(g + 1) * 512]) for g in range(2)], axis=1)
        z = z_ref[...]
        yz = y * (z * _sigmoid(z))
        outs = []
        for g in range(2):
            v = yz[:, g * 512:(g + 1) * 512]
            outs.append(v * lax.rsqrt(jnp.mean(v * v, axis=-1, keepdims=True) + EPS))
        mix_ref[...] = (jnp.concatenate(outs, axis=1) * nw_ref[...]).astype(BF16)

    def col(width, blk, prev=False):
        if prev:
            return pl.BlockSpec((CH, width), lambda i: (jnp.maximum(i - 1, 0), blk))
        return pl.BlockSpec((CH, width), lambda i: (i, blk))

    def full(a):
        return pl.BlockSpec(a.shape, lambda i: (0,) * a.ndim)

    return pl.pallas_call(
        body, name="ssm_fwd", grid=(NC,),
        in_specs=[col(D, 5), col(D, 5, True), col(512, 12), col(512, 12, True), col(LANE, 52), col(D, 4),
                  full(cw), full(cb), full(dtb), full(alog), full(d_b), full(nw), ANY],
        out_specs=[col(D, 1), col(D, 0), pl.BlockSpec((None, CH, D), lambda i: (i, 0, 0))],
        out_shape=[SDS((S, 2 * D), BF16), SDS((S, D), F32), SDS((NC, CH, D), F32)],
        scratch_shapes=[pltpu.VMEM((CH, D), F32)],
        input_output_aliases={12: 0},
        compiler_params=_cp(("arbitrary",)),
    )(proj, proj, proj, proj, proj, proj, cw, cb, dtb, alog, d_b, nw, mix)


def _ssm_bwd(proj, dmix, y_save, states, cw, cb, dtb, alog, d_b, nw):
    def body(xs_ref, xsp_ref, bc_ref, bcp_ref, dt_ref, z_ref, dn_ref, y_ref, st_ref,
             cw_ref, cb_ref, dtb_ref, alog_ref, db_ref, nw_ref,
             dz_ref, dx_ref, dcw_ref, dcb_ref, dsm_ref, dnw_ref, dh_ref, nxs_ref, nbc_ref):
        i = pl.program_id(0)
        ci = NC - 1 - i

        @pl.when(i == 0)
        def _():
            for ref in (dcw_ref, dcb_ref, dsm_ref, dnw_ref, dh_ref, nxs_ref, nbc_ref):
                ref[...] = jnp.zeros_like(ref)

        cw, cb = cw_ref[...], cb_ref[...]
        q = _ssd_common(xs_ref[...], xsp_ref[...], bc_ref[...], bcp_ref[...], dt_ref[...], ci == 0,
                        cw, cb, dtb_ref[...], alog_ref[...])
        bg, cg = _groups(q["bc"])
        xs, dt_b, e_b, f_b, t_b = q["xs"], q["dt_b"], q["e_b"], q["f_b"], q["t_b"]
        xdt = xs * dt_b
        xdt_b = xdt.astype(BF16)
        h_in = st_ref[...]
        hb = h_in.astype(BF16)
        dh_new = dh_ref[...]
        dhb = dh_new.astype(BF16)
        red = _reduce()

        z, y, dn, nw_v = z_ref[...], y_ref[...], dn_ref[...], nw_ref[...]
        sig = _sigmoid(z)
        sz = z * sig
        yz = y * sz
        gdn = dn * nw_v
        dyz, dnw = [], []
        for g in range(2):
            v, gv = yz[:, g * 512:(g + 1) * 512], gdn[:, g * 512:(g + 1) * 512]
            r = lax.rsqrt(jnp.mean(v * v, axis=-1, keepdims=True) + EPS)
            dnw.append(dn[:, g * 512:(g + 1) * 512] * v * r)
            dyz.append(r * (gv - v * (r * r) * jnp.mean(gv * v, axis=-1, keepdims=True)))
        dyz = jnp.concatenate(dyz, axis=1)
        dnw_ref[...] += jnp.sum(jnp.concatenate(dnw, axis=1), axis=0, keepdims=True)
        dy = dyz * sz
        dz_ref[...] = dyz * y * (sig * (1.0 + z * (1.0 - sig)))
        dy_b = dy.astype(BF16)

        tril = _iota((CH, CH), 1) <= _iota((CH, CH), 0)
        lane_a = _iota((CH, LANE), 1) < 64
        cbm = [_dot_nt(cg[g], bg[g]) for g in range(2)]
        dcbm = [jnp.zeros((CH, CH), F32), jnp.zeros((CH, CH), F32)]
        seg_rows = jnp.zeros((CH, LANE), F32)
        seg_cols = jnp.zeros((LANE, CH), F32)
        row_id, col_id = _iota((CH, LANE), 0), _iota((CH, LANE), 1)
        dx_pairs = []
        for hp in range(NH // 2):
            g = hp // 4
            xp = xdt_b[:, hp * LANE:(hp + 1) * LANE]
            dyp_f = dy[:, hp * LANE:(hp + 1) * LANE]
            dyp = dy_b[:, hp * LANE:(hp + 1) * LANE]
            halves = []
            for k in range(2):
                h = 2 * hp + k
                lane = lane_a if k == 0 else jnp.logical_not(lane_a)
                dec = _decay(q, h, tril)
                gm = cbm[g] * dec
                dgm = _dot_nt(jnp.where(lane, dyp_f, 0.0).astype(BF16), xp)
                dcbm[g] = dcbm[g] + dgm * dec
                prod = dgm * gm
                seg_rows = jnp.where(col_id == h, jnp.sum(prod, axis=1, keepdims=True), seg_rows)
                seg_cols = jnp.where(row_id == h, jnp.sum(prod, axis=0, keepdims=True), seg_cols)
                halves.append(_dot_tn(gm.astype(BF16), dyp))
            dx_pairs.append(jnp.where(lane_a, halves[0], halves[1]))
        dxdt_diag = jnp.concatenate(dx_pairs, axis=1)

        qv = jnp.concatenate([_dot(bg[g], dhb[:, g * 512:(g + 1) * 512]) for g in range(2)], axis=1)
        y_off = jnp.concatenate([_dot(cg[g], hb[:, g * 512:(g + 1) * 512]) for g in range(2)], axis=1) * e_b
        xfq = xdt * f_b * qv
        dxdt = dxdt_diag + f_b * qv
        fdf = _dot_hi(xfq, red)
        dcs = seg_rows - seg_cols.T + _dot_hi(dy * y_off, red) - fdf
        tdt = jnp.sum(dh_new * h_in, axis=0, keepdims=True) * t_b
        last = _dot_hi(jnp.broadcast_to(tdt, (8, D)), red)[0:1, :] + jnp.sum(fdf, axis=0, keepdims=True)
        dcs = dcs + jnp.where(_iota((CH, LANE), 0) == CH - 1, last, 0.0)
        tri_t = (_iota((CH, CH), 1) >= _iota((CH, CH), 0)).astype(F32)
        da = _dot_hi(tri_t, dcs)
        ddt = da * q["a_row"] + _dot_hi(dxdt * xs, red)
        dxs = dxdt * dt_b + db_ref[...] * dy
        ddt_raw = ddt * _sigmoid(q["pre"])
        dsm_ref[0:1, :] += jnp.sum(ddt_raw, axis=0, keepdims=True)
        dsm_ref[1:2, :] += jnp.sum(da * q["dt"], axis=0, keepdims=True) * q["a_row"]
        dsm_ref[2:3, :] += jnp.sum(_dot_hi(dy * xs, red), axis=0, keepdims=True)
        edy = (e_b * dy).astype(BF16)
        xf = (xdt * f_b).astype(BF16)
        dbs, dcs_g, dhs = [], [], []
        for g in range(2):
            sl = slice(g * 512, (g + 1) * 512)
            dcb_b = dcbm[g].astype(BF16)
            dcs_g.append(_dot(dcb_b, bg[g]) + _dot_nt(edy[:, sl], hb[:, sl]))
            dbs.append(_dot_tn(dcb_b, cg[g]) + _dot_nt(xf[:, sl], dhb[:, sl]))
            dhs.append(_dot_tn(cg[g], edy[:, sl]))
        dh_ref[...] = t_b * dh_new + jnp.concatenate(dhs, axis=1)
        dbc = jnp.concatenate(dbs + dcs_g, axis=1)

        def conv_bwd(dact, pre, taps, w, nxt_ref, lo):
            s = _sigmoid(pre)
            dconv = dact * (s * (1.0 + pre * (1.0 - s)))
            nxt = nxt_ref[...]
            row = _iota(dconv.shape, 0)
            hi = lo + dconv.shape[1]
            dcb_ref[:, lo:hi] += jnp.sum(dconv, axis=0, keepdims=True)
            dx = w[3:4, :] * dconv
            for k in range(4):
                dcw_ref[k:k + 1, lo:hi] += jnp.sum(dconv * taps[k], axis=0, keepdims=True)
            for s_ in (1, 2, 3):
                up = jnp.where(row >= CH - s_, pltpu.roll(nxt, CH - s_, 0), pltpu.roll(dconv, CH - s_, 0))
                dx = dx + w[3 - s_:4 - s_, :] * up
            nxt_ref[...] = dconv
            return dx

        dx_ref[:, 0:D] = conv_bwd(dxs, q["xs_c"], q["xs_taps"], cw[:, :D], nxs_ref, 0)
        dx_ref[:, D:D + 512] = conv_bwd(dbc, q["bc_c"], q["bc_taps"], cw[:, D:], nbc_ref, D)
        dx_ref[:, D + 512:D + 640] = ddt_raw
        dx_ref[:, D + 640:] = jnp.zeros((CH, 2 * D - D - 640), F32)

    def col(width, blk, prev=False):
        if prev:
            return pl.BlockSpec((CH, width), lambda i: (jnp.maximum(NC - 2 - i, 0), blk))
        return pl.BlockSpec((CH, width), lambda i: (NC - 1 - i, blk))

    def full(a):
        return pl.BlockSpec(a.shape, lambda i: (0,) * len(a.shape))

    acc_shapes = [SDS((4, 1536), F32), SDS((1, 1536), F32), SDS((8, LANE), F32), SDS((1, D), F32)]
    return pl.pallas_call(
        body, name="ssm_bwd", grid=(NC,),
        in_specs=[col(D, 5), col(D, 5, True), col(512, 12), col(512, 12, True), col(LANE, 52), col(D, 4),
                  col(D, 1), col(D, 0), pl.BlockSpec((None, CH, D), lambda i: (NC - 1 - i, 0, 0)),
                  full(cw), full(cb), full(dtb), full(alog), full(d_b), full(nw)],
        out_specs=[col(D, 0), col(2 * D, 0)] + [full(a) for a in acc_shapes],
        out_shape=[SDS((S, D), F32), SDS((S, 2 * D), F32)] + acc_shapes,
        scratch_shapes=[pltpu.VMEM((CH, D), F32), pltpu.VMEM((CH, D), F32), pltpu.VMEM((CH, 512), F32)],
        compiler_params=_cp(("arbitrary",)),
    )(proj, proj, proj, proj, proj, proj, dmix, y_save, states, cw, cb, dtb, alog, d_b, nw)


def _outproj_loss(mix, w_out, x, tgt, nw):
    tm = 256

    def body(mix_ref, w_ref, x_ref, t_ref, nw_ref, dy_ref, dmix_ref, dw_ref, dnw_ref, loss_ref):
        @pl.when(pl.program_id(0) == 0)
        def _():
            dw_ref[...] = jnp.zeros_like(dw_ref)
            dnw_ref[...] = jnp.zeros_like(dnw_ref)
            loss_ref[...] = jnp.zeros_like(loss_ref)

        mixv, w = mix_ref[...], w_ref[...]
        out = _dot(mixv, w)
        r = lax.rsqrt(jnp.mean(out * out, axis=-1, keepdims=True) + EPS)
        nh = out * r
        nw_v = nw_ref[...]
        err = x_ref[...] + nh * nw_v - t_ref[...]
        loss_ref[...] += 0.5 * jnp.sum(jnp.mean(err * err, axis=-1, keepdims=True), axis=0, keepdims=True)
        dy = err * (1.0 / D)
        dy_ref[...] = dy
        dnw_ref[...] += jnp.sum(dy * nh, axis=0, keepdims=True)
        gdn = dy * nw_v
        dout = (r * (gdn - nh * jnp.mean(gdn * nh, axis=-1, keepdims=True))).astype(BF16)
        dmix_ref[...] = _dot_nt(dout, w)
        dw_ref[...] += _dot_tn(mixv, dout)

    row = lambda w: pl.BlockSpec((tm, w), lambda i: (i, 0))
    full = lambda s: pl.BlockSpec(s, lambda i: (0, 0))
    return pl.pallas_call(
        body, name="outproj_loss", grid=(S // tm,),
        in_specs=[row(2 * D), full((2 * D, D)), row(D), row(D), full((1, D))],
        out_specs=[row(D), row(2 * D), full((2 * D, D)), full((1, D)), full((1, LANE))],
        out_shape=[SDS((S, D), F32), SDS((S, 2 * D), F32), SDS((2 * D, D), F32), SDS((1, D), F32),
                   SDS((1, LANE), F32)],
        compiler_params=_cp(("arbitrary",)),
    )(mix, w_out, x, tgt, nw)


def _inproj_bwd_dx(srcs, dxbcdt, w_all, x, dy, nw):
    tm = 256
    nk = DP // D

    def body(*refs):
        src_refs = refs[:nk]
        w_ref, x_ref, dy_ref, nw_ref, gx_ref, dnw_ref, acc_ref = refs[nk:]
        i, kk = pl.program_id(0), pl.program_id(1)

        @pl.when((i == 0) & (kk == 0))
        def _():
            dnw_ref[...] = jnp.zeros_like(dnw_ref)

        @pl.when(kk == 0)
        def _():
            acc_ref[...] = jnp.zeros_like(acc_ref)

        for s, ref in enumerate(src_refs):
            @pl.when(kk == s)
            def _(ref=ref):
                acc_ref[...] += _dot_nt(ref[...].astype(BF16), w_ref[...])

        @pl.when(kk == nk - 1)
        def _():
            xf, du, nw_v = x_ref[...], acc_ref[...], nw_ref[...]
            r = lax.rsqrt(jnp.mean(xf * xf, axis=-1, keepdims=True) + EPS)
            xh = xf * r
            dnw_ref[...] += jnp.sum(du * xh, axis=0, keepdims=True)
            gdu = du * nw_v
            gx_ref[...] = r * (gdu - xh * jnp.mean(gdu * xh, axis=-1, keepdims=True)) + dy_ref[...]

    row = pl.BlockSpec((tm, D), lambda i, k: (i, 0))
    row1 = pl.BlockSpec((tm, D), lambda i, k: (i, 1))
    one = pl.BlockSpec((1, D), lambda i, k: (0, 0))
    return pl.pallas_call(
        body, name="inproj_bwd_dx", grid=(S // tm, nk),
        in_specs=[row] * len(srcs) + [row, row1, pl.BlockSpec((D, D), lambda i, k: (0, k)), row, row, one],
        out_specs=[row, one],
        out_shape=[SDS((S, D), F32), SDS((1, D), F32)],
        scratch_shapes=[pltpu.VMEM((tm, D), F32)],
        compiler_params=_cp(("arbitrary", "arbitrary")),
    )(*srcs, dxbcdt, dxbcdt, w_all, x, dy, nw)


def _dw(u, dsec, name):
    ts = 512
    ncol = dsec.shape[1] // D

    def body(u_ref, d_ref, o_ref):
        @pl.when(pl.program_id(1) == 0)
        def _():
            o_ref[...] = jnp.zeros_like(o_ref)

        o_ref[...] += _dot_tn(u_ref[...], d_ref[...].astype(BF16))

    return pl.pallas_call(
        body, name=name, grid=(ncol, S // ts),
        in_specs=[pl.BlockSpec((ts, D), lambda j, i: (i, 0)), pl.BlockSpec((ts, D), lambda j, i: (i, j))],
        out_specs=pl.BlockSpec((D, D), lambda j, i: (0, j)),
        out_shape=SDS((D, ncol * D), F32),
        compiler_params=_cp(("parallel", "arbitrary")),
    )(u, dsec)


def _place():
    x, y, c = lax.axis_index("x"), lax.axis_index("y"), lax.axis_index("c")
    return x, y, c, 2 * x + y


def _chip_of(x, y, k):
    px = 1 - x if k & 2 else x
    py = 1 - y if k & 1 else y
    return px, py, 2 * px + py


def _remote(src, dst, send_sem, recv_sem, dev):
    return pltpu.make_async_remote_copy(src_ref=src, dst_ref=dst, send_sem=send_sem, recv_sem=recv_sem,
                                        device_id=dev, device_id_type=MESH)


def _gather_weights(w_in_b, w_out_b, conv_w):
    hin, hout = D // 2, w_out_b.shape[0] // 2

    def body(win_ref, wout_ref, cw_ref, gin_ref, gout_ref, gcw_ref, send, recv, fsend, frecv, lsem):
        x, y, c, j = _place()
        sib = (x, y, 1 - c)
        rin, rout = pl.ds(c * hin, hin), pl.ds(c * hout, hout)
        sin, sout = pl.ds((1 - c) * hin, hin), pl.ds((1 - c) * hout, hout)
        local = [pltpu.make_async_copy(win_ref, gin_ref.at[j], lsem.at[0]),
                 pltpu.make_async_copy(wout_ref, gout_ref.at[j], lsem.at[1]),
                 pltpu.make_async_copy(cw_ref, gcw_ref.at[j], lsem.at[2])]
        for cp in local:
            cp.start()
        sends = []
        for k in (1, 2, 3):
            px, py, _ = _chip_of(x, y, k)
            dev = (px, py, c)
            sends += [_remote(win_ref.at[rin], gin_ref.at[j, rin], send.at[k - 1], recv.at[k - 1], dev),
                      _remote(wout_ref.at[rout], gout_ref.at[j, rout], send.at[k + 2], recv.at[k + 2], dev),
                      _remote(cw_ref, gcw_ref.at[j], send.at[k + 5], recv.at[k + 5], dev)]
        for cp in sends:
            cp.start()
        fwd = []
        for k in (1, 2, 3):
            _, _, pj = _chip_of(x, y, k)
            _remote(win_ref.at[rin], gin_ref.at[pj, rin], send.at[k - 1], recv.at[k - 1], sib).wait_recv()
            f_in = _remote(gin_ref.at[pj, rin], gin_ref.at[pj, rin], fsend.at[k - 1], frecv.at[k - 1], sib)
            f_in.start()
            _remote(wout_ref.at[rout], gout_ref.at[pj, rout], send.at[k + 2], recv.at[k + 2], sib).wait_recv()
            f_out = _remote(gout_ref.at[pj, rout], gout_ref.at[pj, rout], fsend.at[k + 2], frecv.at[k + 2], sib)
            f_out.start()
            fwd += [f_in, f_out]
        for k in (1, 2, 3):
            _, _, pj = _chip_of(x, y, k)
            _remote(win_ref.at[sin], gin_ref.at[pj, sin], fsend.at[k - 1], frecv.at[k - 1], sib).wait_recv()
            _remote(wout_ref.at[sout], gout_ref.at[pj, sout], fsend.at[k + 2], frecv.at[k + 2], sib).wait_recv()
            _remote(cw_ref, gcw_ref.at[pj], send.at[k + 5], recv.at[k + 5], sib).wait_recv()
        for cp in sends + fwd:
            cp.wait_send()
        for cp in local:
            cp.wait()

    return pl.pallas_call(
        body, name="gather_weights",
        in_specs=[ANY, ANY, ANY], out_specs=[ANY, ANY, ANY],
        out_shape=[SDS((4,) + w_in_b.shape, BF16), SDS((4,) + w_out_b.shape, BF16), SDS((4,) + conv_w.shape, F32)],
        scratch_shapes=[pltpu.SemaphoreType.DMA((9,)), pltpu.SemaphoreType.DMA((9,)),
                        pltpu.SemaphoreType.DMA((6,)), pltpu.SemaphoreType.DMA((6,)), pltpu.SemaphoreType.DMA((3,))],
        compiler_params=pltpu.CompilerParams(has_side_effects=True),
    )(w_in_b, w_out_b, conv_w)


def _pair_exchange(gw, go):
    hw, ho = gw.shape[1] // 2, go.shape[1] // 2

    def body(gw_ref, go_ref, rw_ref, ro_ref, send, recv):
        x, y, c, _ = _place()
        sib = (x, y, 1 - c)
        theirs_w, theirs_o = pl.ds((1 - c) * hw, hw), pl.ds((1 - c) * ho, ho)
        cps = [_remote(gw_ref.at[:, theirs_w], rw_ref, send.at[0], recv.at[0], sib),
               _remote(go_ref.at[:, theirs_o], ro_ref, send.at[1], recv.at[1], sib)]
        for cp in cps:
            cp.start()
        for cp in cps:
            cp.wait()

    return pl.pallas_call(
        body, name="pair_exchange", in_specs=[ANY, ANY], out_specs=[ANY, ANY],
        out_shape=[SDS((4, hw, gw.shape[2]), F32), SDS((4, ho, go.shape[2]), F32)],
        scratch_shapes=[pltpu.SemaphoreType.DMA((2,)), pltpu.SemaphoreType.DMA((2,))],
        compiler_params=pltpu.CompilerParams(has_side_effects=True),
    )(gw, go)


def _pair_sum(cidx, g, r, name):
    half, width = r.shape[1], r.shape[2]
    tr = min(half, 256)
    nt = half // tr

    def body(c_ref, g_ref, r_ref, o_ref):
        del c_ref
        o_ref[...] = (g_ref[...] + r_ref[...]).astype(BF16)

    return pl.pallas_call(
        body, name=name,
        grid_spec=pltpu.PrefetchScalarGridSpec(
            num_scalar_prefetch=1, grid=(4, nt),
            in_specs=[pl.BlockSpec((None, tr, width), lambda s, t, c: (s, c[0] * nt + t, 0)),
                      pl.BlockSpec((None, tr, width), lambda s, t, c: (s, t, 0))],
            out_specs=pl.BlockSpec((None, tr, width), lambda s, t, c: (s, t, 0))),
        out_shape=SDS(r.shape, BF16),
        compiler_params=_cp(("parallel", "parallel")),
    )(cidx, g, r)


def _chip_exchange(pw, po, small):
    def body(pw_ref, po_ref, sm_ref, rw_ref, ro_ref, rs_ref, send, recv, ssend, srecv, lsem):
        x, y, c, j = _place()
        me = 2 * j + c
        local = [pltpu.make_async_copy(pw_ref.at[j], rw_ref.at[j], lsem.at[0]),
                 pltpu.make_async_copy(po_ref.at[j], ro_ref.at[j], lsem.at[1]),
                 pltpu.make_async_copy(sm_ref, rs_ref.at[me], lsem.at[2])]
        for cp in local:
            cp.start()
        cps = []
        for k in (1, 2, 3):
            px, py, pj = _chip_of(x, y, k)
            dev = (px, py, c)
            cps += [_remote(pw_ref.at[pj], rw_ref.at[j], send.at[k - 1], recv.at[k - 1], dev),
                    _remote(po_ref.at[pj], ro_ref.at[j], send.at[k + 2], recv.at[k + 2], dev)]
        for k in range(1, 8):
            px, py, _ = _chip_of(x, y, k >> 1)
            pc = 1 - c if k & 1 else c
            cps.append(_remote(sm_ref, rs_ref.at[me], ssend.at[k - 1], srecv.at[k - 1], (px, py, pc)))
        for cp in cps:
            cp.start()
        for k in (1, 2, 3):
            _, _, pj = _chip_of(x, y, k)
            _remote(pw_ref.at[pj], rw_ref.at[pj], send.at[k - 1], recv.at[k - 1], (x, y, c)).wait_recv()
            _remote(po_ref.at[pj], ro_ref.at[pj], send.at[k + 2], recv.at[k + 2], (x, y, c)).wait_recv()
        for k in range(1, 8):
            _, _, pj = _chip_of(x, y, k >> 1)
            pc = 1 - c if k & 1 else c
            _remote(sm_ref, rs_ref.at[2 * pj + pc], ssend.at[k - 1], srecv.at[k - 1], (x, y, c)).wait_recv()
        for cp in cps:
            cp.wait_send()
        for cp in local:
            cp.wait()

    return pl.pallas_call(
        body, name="chip_exchange", in_specs=[ANY, ANY, ANY], out_specs=[ANY, ANY, ANY],
        out_shape=[SDS(pw.shape, BF16), SDS(po.shape, BF16), SDS((8,) + small.shape, F32)],
        scratch_shapes=[pltpu.SemaphoreType.DMA((6,)), pltpu.SemaphoreType.DMA((6,)),
                        pltpu.SemaphoreType.DMA((7,)), pltpu.SemaphoreType.DMA((7,)), pltpu.SemaphoreType.DMA((3,))],
        compiler_params=pltpu.CompilerParams(has_side_effects=True),
    )(pw, po, small)


def _slot_sum(r, name):
    n, rows, width = r.shape
    tr = min(rows, 256)

    def body(r_ref, o_ref):
        acc = r_ref[0].astype(F32)
        for s in range(1, n):
            acc = acc + r_ref[s].astype(F32)
        o_ref[...] = acc

    return pl.pallas_call(
        body, name=name, grid=(rows // tr,),
        in_specs=[pl.BlockSpec((n, tr, width), lambda t: (0, t, 0))],
        out_specs=pl.BlockSpec((tr, width), lambda t: (t, 0)),
        out_shape=SDS((rows, width), F32),
        compiler_params=_cp(("parallel",)),
    )(r)


def _half_exchange(hw, ho):
    def body(hw_ref, ho_ref, gw_ref, go_ref, send, recv, lsem):
        x, y, c, _ = _place()
        sib = (x, y, 1 - c)
        mine_w, mine_o = pl.ds(c * hw.shape[0], hw.shape[0]), pl.ds(c * ho.shape[0], ho.shape[0])
        local = [pltpu.make_async_copy(hw_ref, gw_ref.at[mine_w], lsem.at[0]),
                 pltpu.make_async_copy(ho_ref, go_ref.at[mine_o], lsem.at[1])]
        cps = [_remote(hw_ref, gw_ref.at[mine_w], send.at[0], recv.at[0], sib),
               _remote(ho_ref, go_ref.at[mine_o], send.at[1], recv.at[1], sib)]
        for cp in local + cps:
            cp.start()
        theirs_w = pl.ds((1 - c) * hw.shape[0], hw.shape[0])
        theirs_o = pl.ds((1 - c) * ho.shape[0], ho.shape[0])
        _remote(hw_ref, gw_ref.at[theirs_w], send.at[0], recv.at[0], sib).wait_recv()
        _remote(ho_ref, go_ref.at[theirs_o], send.at[1], recv.at[1], sib).wait_recv()
        for cp in cps:
            cp.wait_send()
        for cp in local:
            cp.wait()

    return pl.pallas_call(
        body, name="half_exchange", in_specs=[ANY, ANY], out_specs=[ANY, ANY],
        out_shape=[SDS((2 * hw.shape[0], hw.shape[1]), F32), SDS((2 * ho.shape[0], ho.shape[1]), F32)],
        scratch_shapes=[pltpu.SemaphoreType.DMA((2,)), pltpu.SemaphoreType.DMA((2,)), pltpu.SemaphoreType.DMA((2,))],
        compiler_params=pltpu.CompilerParams(has_side_effects=True),
    )(hw, ho)


def _adamw(w, g, m, v, name):
    rows, width = w.shape
    tr = min(rows, 256)

    def body(w_ref, g_ref, m_ref, v_ref, d_ref, nm_ref, nv_ref):
        gv = g_ref[...]
        nm = ADAM_B1 * m_ref[...] + (1.0 - ADAM_B1) * gv
        nv = ADAM_B2 * v_ref[...] + (1.0 - ADAM_B2) * (gv * gv)
        m_hat = nm / (1.0 - ADAM_B1 ** ADAM_STEP)
        v_hat = nv / (1.0 - ADAM_B2 ** ADAM_STEP)
        d_ref[...] = -ADAM_LR * (m_hat / (jnp.sqrt(v_hat) + ADAM_EPS) + ADAM_WD * w_ref[...])
        nm_ref[...] = nm
        nv_ref[...] = nv

    t = pl.BlockSpec((tr, width), lambda i: (i, 0))
    return pl.pallas_call(
        body, name=name, grid=(rows // tr,), in_specs=[t] * 4, out_specs=[t] * 3,
        out_shape=[SDS(w.shape, F32)] * 3, compiler_params=_cp(("parallel",)),
    )(w, g, m, v)


def _rows128(a, rows):
    flat = a.reshape(-1)
    return jnp.pad(flat, (0, rows * LANE - flat.shape[0])).reshape(rows, LANE)


def _pack_small(conv_w, norm_pre, conv_b, ssm_norm, norm_post, dtb, alog, dsk):
    cw_rows = 48 if conv_w.shape[-1] == 1536 else 16
    vec = jnp.concatenate([_rows128(dtb, 1), _rows128(alog, 1), _rows128(dsk, 1), jnp.zeros((5, LANE), F32)], axis=0)
    return jnp.concatenate([_rows128(conv_w, cw_rows), _rows128(norm_pre, 8), _rows128(conv_b, 16),
                            _rows128(ssm_norm, 8), _rows128(norm_post, 8), vec], axis=0)


def _unpack_small(p, cw_cols):
    cw_rows = 48 if cw_cols == 1536 else 16
    o = cw_rows
    conv_w = p[:cw_rows].reshape(-1)[:4 * cw_cols].reshape(1, 4, cw_cols)
    norm_pre = p[o:o + 8].reshape(1, D)
    conv_b = p[o + 8:o + 24].reshape(-1)[:1536].reshape(1, 1536)
    ssm_norm = p[o + 24:o + 32].reshape(1, D)
    norm_post = p[o + 32:o + 40].reshape(1, D)
    vec = p[o + 40:o + 48]
    return conv_w, norm_pre, conv_b, ssm_norm, norm_post, vec[0:1, :NH], vec[1:2, :NH], vec[2:3, :NH]


def _pad_lanes(a):
    return jnp.pad(a, ((0, 0), (0, LANE - a.shape[1])))


def kernel(x, norm_pre_w, w_in, conv_w, conv_b, dt_bias, a_log, d_skip, ssm_norm_w, w_out, norm_post_w, loss_target, m_norm_pre_w, m_w_in, m_conv_w, m_conv_b, m_dt_bias, m_a_log, m_d_skip, m_ssm_norm_w, m_w_out, m_norm_post_w, v_norm_pre_w, v_w_in, v_conv_w, v_conv_b, v_dt_bias, v_a_log, v_d_skip, v_ssm_norm_w, v_w_out, v_norm_post_w):
    xi, yi, ci = lax.axis_index("x"), lax.axis_index("y"), lax.axis_index("c")
    chip = 2 * xi + yi
    x2, tgt = x[0], loss_target[0]

    gin, gout, gcw = _gather_weights(w_in[0].astype(BF16), w_out[0].astype(BF16), conv_w[0])
    w_all = jnp.concatenate([gin[0], gin[1], gin[2], gin[3], jnp.zeros((D, DP - 4 * SHARD), BF16)], axis=1)
    w_out_all = gout.reshape(2 * D, D)
    cw_all = jnp.concatenate([gcw[0], gcw[1], gcw[2], gcw[3]], axis=1)
    loss_part, grad_x, dw_all, dw_out, small = _local_step(
        x2, tgt, w_all, w_out_all, cw_all, norm_pre_w, conv_b, dt_bias, a_log, d_skip, ssm_norm_w, norm_post_w)
    gw = jnp.stack([dw_all[:, k * SHARD:(k + 1) * SHARD] for k in range(4)])
    go = dw_out.reshape(4, D // 2, D)

    cidx = jnp.reshape(ci, (1,)).astype(jnp.int32)
    rw, ro = _pair_exchange(gw, go)
    pw, po = _pair_sum(cidx, gw, rw, "pair_sum_in"), _pair_sum(cidx, go, ro, "pair_sum_out")
    cw_r, co_r, sm_r = _chip_exchange(pw, po, small)
    g_in, g_out = _half_exchange(_slot_sum(cw_r, "chip_sum_in"), _slot_sum(co_r, "chip_sum_out"))
    g_small = _slot_sum(sm_r, "small_sum")
    g_cw, g_npre, g_cb, g_nssm, g_npost, g_dtb, g_alog, g_dsk = _unpack_small(g_small, 1536)
    g_cw = lax.dynamic_slice_in_dim(g_cw, chip * 384, 384, axis=2)

    d_in, nm_in, nv_in = _adamw(w_in[0], g_in, m_w_in[0], v_w_in[0], "adamw_in")
    d_out, nm_out, nv_out = _adamw(w_out[0], g_out, m_w_out[0], v_w_out[0], "adamw_out")
    packed = [_pack_small(*t) for t in (
        (conv_w, norm_pre_w, conv_b, ssm_norm_w, norm_post_w, dt_bias, a_log, d_skip),
        (g_cw, g_npre, g_cb, g_nssm, g_npost, g_dtb, g_alog, g_dsk),
        (m_conv_w, m_norm_pre_w, m_conv_b, m_ssm_norm_w, m_norm_post_w, m_dt_bias, m_a_log, m_d_skip),
        (v_conv_w, v_norm_pre_w, v_conv_b, v_ssm_norm_w, v_norm_post_w, v_dt_bias, v_a_log, v_d_skip))]
    small_out = [_unpack_small(p, 384) for p in _adamw(*packed, "adamw_small")]

    loss = lax.psum(loss_part[0, 0], ("x", "y", "c"))

    def ordered(cw_, npre, cb_, nssm, npost, dtb_, alog_, dsk_, big_in, big_out):
        return [npre, big_in[None], cw_, cb_, dtb_, alog_, dsk_, nssm, big_out[None], npost]

    grads = ordered(g_cw, g_npre, g_cb, g_nssm, g_npost, g_dtb, g_alog, g_dsk, g_in, g_out)
    deltas = ordered(*small_out[0], d_in, d_out)
    new_m = ordered(*small_out[1], nm_in, nm_out)
    new_v = ordered(*small_out[2], nv_in, nv_out)
    return (loss, grad_x[None], *grads, *deltas, *new_m, *new_v)


def _local_step(x2, tgt, w_all, w_out_all, cw_all, norm_pre_w, conv_b, dt_bias, a_log, d_skip, ssm_norm_w,
                norm_post_w):
    dtb, alog = _pad_lanes(dt_bias), _pad_lanes(a_log)
    d_b = jnp.repeat(d_skip, 64, axis=1)

    proj, u = _inproj_fwd(x2, norm_pre_w, w_all)
    o1, l1 = _attn_fwd(proj, 1)
    o2, l2 = _attn_fwd(proj, 4)
    o3, l3 = _attn_fwd(proj, 16)
    mix, attn_pre, lse = _attn_merge(o1, o2, o3, l1, l2, l3, proj)
    mix, y_save, states = _ssm_fwd(proj, mix, cw_all, conv_b, dtb, alog, d_b, ssm_norm_w)

    dy, dmix, dw_out, dnw_post, loss_part = _outproj_loss(mix, w_out_all, x2, tgt, norm_post_w)
    do, delta, dg = _attn_gate_bwd(dmix, attn_pre, proj)
    acc = _attn_bwd(proj, do, lse, delta, 1, None)
    acc = _attn_bwd(proj, do, lse, delta, 4, acc)
    dq, dk, dv = _attn_bwd(proj, do, lse, delta, 16, acc)
    dz, dxbcdt, dcw, dcb, dvec, dnw_ssm = _ssm_bwd(proj, dmix, y_save, states, cw_all, conv_b, dtb, alog, d_b,
                                                   ssm_norm_w)
    srcs = [dq, dk, dv, dg, dz]
    grad_x, dnw_pre = _inproj_bwd_dx(srcs, dxbcdt, w_all, x2, dy, norm_pre_w)
    dws = [_dw(u, s, f"dw_in_{n}") for s, n in zip(srcs + [dxbcdt], ("q", "k", "v", "g", "z", "xbcdt"))]
    dw_all = jnp.concatenate(dws, axis=1)
    small = _pack_small(dcw, dnw_pre, dcb, dnw_ssm, dnw_post, dvec[0:1, :NH], dvec[1:2, :NH], dvec[2:3, :NH])
    return loss_part, grad_x, dw_all, dw_out, small
```

```python
import functools

import jax
import jax.numpy as jnp
from jax import lax
from jax.experimental import pallas as pl
from jax.experimental.pallas import tpu as pltpu

F32 = jnp.float32
BF16 = jnp.bfloat16
HI = lax.Precision.HIGHEST
MESH = pl.DeviceIdType.MESH
SDS = jax.ShapeDtypeStruct
ANY = pl.BlockSpec(memory_space=pl.ANY)

S = 4096
D = 1024
DP = 7168
SHARD = 1668
OFF_G, OFF_Z = 3072, 4096
NH = 16
CH = 128
NC = S // CH
EPS = 1e-6
NEG = -1e30
LANE = 128
VMEM_LIMIT = 48 * 1024 * 1024

ADAM_LR, ADAM_B1, ADAM_B2, ADAM_EPS, ADAM_WD, ADAM_STEP = 0.001, 0.9, 0.999, 1e-08, 0.01, 10


def _cp(sem, **kw):
    return pltpu.CompilerParams(dimension_semantics=sem, vmem_limit_bytes=VMEM_LIMIT, **kw)


def _dot(a, b):
    return jnp.dot(a, b, preferred_element_type=F32)


def _dot_nt(a, b):
    return lax.dot_general(a, b, (((1,), (1,)), ((), ())), preferred_element_type=F32)


def _dot_tn(a, b):
    return lax.dot_general(a, b, (((0,), (0,)), ((), ())), preferred_element_type=F32)


def _dot_hi(a, b):
    return jnp.dot(a, b, precision=HI, preferred_element_type=F32)


def _sigmoid(v):
    return 1.0 / (1.0 + jnp.exp(-v))


def _iota(shape, dim):
    return lax.broadcasted_iota(jnp.int32, shape, dim)


def _inproj_fwd(x, nw, w_all):
    tm, tn = 512, 1024

    def body(x_ref, nw_ref, w_ref, proj_ref, u_ref):
        @pl.when(pl.program_id(1) == 0)
        def _():
            xf = x_ref[...]
            r = lax.rsqrt(jnp.mean(xf * xf, axis=-1, keepdims=True) + EPS)
            u_ref[...] = (xf * r * nw_ref[...]).astype(BF16)

        proj_ref[...] = _dot(u_ref[...], w_ref[...])

    return pl.pallas_call(
        body, name="inproj_fwd", grid=(S // tm, DP // tn),
        in_specs=[pl.BlockSpec((tm, D), lambda i, j: (i, 0)), pl.BlockSpec((1, D), lambda i, j: (0, 0)),
                  pl.BlockSpec((D, tn), lambda i, j: (0, j))],
        out_specs=[pl.BlockSpec((tm, tn), lambda i, j: (i, j)), pl.BlockSpec((tm, D), lambda i, j: (i, 0))],
        out_shape=[SDS((S, DP), F32), SDS((S, D), BF16)],
        compiler_params=_cp(("parallel", "arbitrary")),
    )(x, nw, w_all)


ATTN_QB = {1: 4, 4: 1, 16: 1}


def _unit_rows(r, u, d):
    return pl.ds(r + d * CH * u, CH, stride=d) if d > 1 else pl.ds(CH * u, CH)


def _for_units(d, qb, fn):
    for r in range(d):
        for u in range(qb):
            fn(r, u)


def _attn_mask(has_prev):
    qi, kj = _iota((2 * CH, 2 * CH), 0) & (CH - 1), _iota((2 * CH, 2 * CH), 1)
    cur_ok = (kj >= CH) & (kj - CH <= qi)
    prev_ok = (kj < CH) & (kj >= qi)
    return cur_ok | (prev_ok & has_prev)


def _stack_heads(v, lane_a):
    return jnp.concatenate([jnp.where(lane_a, v, 0.0), jnp.where(lane_a, 0.0, v)], axis=0).astype(BF16)


def _unit_kv(p_ref, c_ref, r, u, d, qb):
    prev = p_ref[_unit_rows(r, 0, d), :] if u == 0 else c_ref[_unit_rows(r, u - 1, d), :]
    return jnp.concatenate([prev, c_ref[_unit_rows(r, u, d), :]], axis=0).astype(BF16)


def _attn_specs(d, qb, lag_out):
    rows, prows = CH * d * qb, CH * d
    nb = S // rows
    last = nb - 1

    def cur(off):
        return pl.BlockSpec((rows, LANE), lambda c, i: (jnp.minimum(i, last), off + c))

    def prev(off):
        return pl.BlockSpec((prows, LANE), lambda c, i: (jnp.clip(i * qb - 1, 0, S // prows - 1), off + c))

    lag = pl.BlockSpec((rows, LANE), lambda c, i: (jnp.clip(i - 1, 0, last), c)) if lag_out else None
    return nb, cur, prev, lag


def _attn_fwd(proj, d):
    qb = ATTN_QB[d]
    nb, cur, prev, _ = _attn_specs(d, qb, False)

    def body(q_ref, kp_ref, kc_ref, vp_ref, vc_ref, o_ref, l_ref):
        i = pl.program_id(1)
        lane_a = _iota((CH, LANE), 1) < 64
        mask_first, mask_rest = _attn_mask(i > 0), _attn_mask(True)

        def unit(r, u):
            sl = _unit_rows(r, u, d)
            q2 = _stack_heads(q_ref[sl, :] * 0.125, lane_a)
            k2, v2 = _unit_kv(kp_ref, kc_ref, r, u, d, qb), _unit_kv(vp_ref, vc_ref, r, u, d, qb)
            s = jnp.where(mask_first if u == 0 else mask_rest, _dot_nt(q2, k2), NEG)
            m = jnp.max(s, axis=1, keepdims=True)
            p = jnp.exp(s - m)
            l = jnp.sum(p, axis=1, keepdims=True)
            o2 = _dot(p.astype(BF16), v2) / l
            lse2 = m + jnp.log(l)
            o_ref[sl, :] = jnp.where(lane_a, o2[:CH], o2[CH:])
            l_ref[sl, :] = jnp.where(lane_a, lse2[:CH], lse2[CH:])

        _for_units(d, qb, unit)

    return pl.pallas_call(
        body, name=f"attn_fwd_d{d}", grid=(NH // 2, nb),
        in_specs=[cur(0), prev(8), cur(8), prev(16), cur(16)],
        out_specs=[cur(0), cur(0)],
        out_shape=[SDS((S, D), F32), SDS((S, D), F32)],
        compiler_params=_cp(("parallel", "arbitrary")),
    )(proj, proj, proj, proj, proj)


def _attn_merge(o1, o2, o3, l1, l2, l3, proj):
    tm = 512

    def body(o1r, o2r, o3r, l1r, l2r, l3r, g_ref, mix_ref, pre_ref, lse_ref):
        l1v, l2v, l3v = l1r[...], l2r[...], l3r[...]
        m = jnp.maximum(jnp.maximum(l1v, l2v), l3v)
        e1, e2, e3 = jnp.exp(l1v - m), jnp.exp(l2v - m), jnp.exp(l3v - m)
        tot = e1 + e2 + e3
        out = (e1 * o1r[...] + e2 * o2r[...] + e3 * o3r[...]) / tot
        g = g_ref[...]
        pre_ref[...] = out
        lse_ref[...] = m + jnp.log(tot)
        mix_ref[...] = (out * (g * _sigmoid(g))).astype(BF16)

    t = pl.BlockSpec((tm, D), lambda i: (i, 0))
    return pl.pallas_call(
        body, name="attn_merge", grid=(S // tm,),
        in_specs=[t, t, t, t, t, t, pl.BlockSpec((tm, D), lambda i: (i, OFF_G // D))],
        out_specs=[t, t, t],
        out_shape=[SDS((S, 2 * D), BF16), SDS((S, D), F32), SDS((S, D), F32)],
        compiler_params=_cp(("parallel",)),
    )(o1, o2, o3, l1, l2, l3, proj)


def _attn_gate_bwd(dmix, pre, proj):
    tm = 512

    def body(dm_ref, pre_ref, g_ref, do_ref, delta_ref, dg_ref):
        g, dm, pre_v = g_ref[...], dm_ref[...], pre_ref[...]
        sig = _sigmoid(g)
        do = dm * (g * sig)
        do_ref[...] = do
        dg_ref[...] = dm * pre_v * (sig * (1.0 + g * (1.0 - sig)))
        prod = do * pre_v
        same_head = (_iota((LANE, LANE), 0) // 64 == _iota((LANE, LANE), 1) // 64).astype(F32)
        for cb in range(D // LANE):
            delta_ref[:, cb * LANE:(cb + 1) * LANE] = _dot_hi(prod[:, cb * LANE:(cb + 1) * LANE], same_head)

    t = pl.BlockSpec((tm, D), lambda i: (i, 0))
    return pl.pallas_call(
        body, name="attn_gate_bwd", grid=(S // tm,),
        in_specs=[t, t, pl.BlockSpec((tm, D), lambda i: (i, OFF_G // D))],
        out_specs=[t, t, t],
        out_shape=[SDS((S, D), F32)] * 3,
        compiler_params=_cp(("parallel",)),
    )(dmix, pre, proj)


def _attn_bwd(proj, do, lse, delta, d, acc):
    qb = ATTN_QB[d]
    nb, cur, prev, lag = _attn_specs(d, qb, True)
    has_acc = acc is not None

    def body(*refs):
        q_ref, kp_ref, kc_ref, vp_ref, vc_ref, do_ref, lse_ref, dl_ref = refs[:8]
        if has_acc:
            aq_ref, ak_ref, av_ref = refs[8:11]
        dq_ref, dk_ref, dv_ref, ck_ref, cv_ref = refs[-5:]
        i = pl.program_id(1)
        lane_a = _iota((CH, LANE), 1) < 64
        mask_first, mask_rest = _attn_mask(i > 0), _attn_mask(True)

        @pl.when(i == 0)
        def _():
            ck_ref[...] = jnp.zeros_like(ck_ref)
            cv_ref[...] = jnp.zeros_like(cv_ref)

        if has_acc:
            dk_ref[...] = ck_ref[...] + ak_ref[...]
            dv_ref[...] = cv_ref[...] + av_ref[...]
        else:
            dk_ref[...] = ck_ref[...]
            dv_ref[...] = cv_ref[...]

        def unit(r, u):
            sl = _unit_rows(r, u, d)
            q2 = _stack_heads(q_ref[sl, :] * 0.125, lane_a)
            do2 = _stack_heads(do_ref[sl, :], lane_a)
            k2, v2 = _unit_kv(kp_ref, kc_ref, r, u, d, qb), _unit_kv(vp_ref, vc_ref, r, u, d, qb)
            lsev, dlv = lse_ref[sl, :], dl_ref[sl, :]
            lse2 = jnp.concatenate([lsev[:, 0:1], lsev[:, 64:65]], axis=0)
            dl2 = jnp.concatenate([dlv[:, 0:1], dlv[:, 64:65]], axis=0)
            p = jnp.exp(jnp.where(mask_first if u == 0 else mask_rest, _dot_nt(q2, k2), NEG) - lse2)
            ds = (p * (_dot_nt(do2, v2) - dl2)).astype(BF16)
            dq2 = _dot(ds, k2)
            dk2 = _dot_tn(ds, q2)
            dv2 = _dot_tn(p.astype(BF16), do2)
            dq = jnp.where(lane_a, dq2[:CH], dq2[CH:]) * 0.125
            dq_ref[sl, :] = dq + aq_ref[sl, :] if has_acc else dq
            if u == 0:
                before = _unit_rows(r, qb - 1, d)
                dk_ref[before, :] += dk2[:CH]
                dv_ref[before, :] += dv2[:CH]
            else:
                before = _unit_rows(r, u - 1, d)
                ck_ref[before, :] += dk2[:CH]
                cv_ref[before, :] += dv2[:CH]
            ck_ref[sl, :] = dk2[CH:]
            cv_ref[sl, :] = dv2[CH:]

        @pl.when(i < nb)
        def _():
            _for_units(d, qb, unit)

    in_specs = [cur(0), prev(8), cur(8), prev(16), cur(16), cur(0), cur(0), cur(0)]
    args = [proj, proj, proj, proj, proj, do, lse, delta]
    if has_acc:
        in_specs += [cur(0), lag, lag]
        args += list(acc)
    rows = CH * d * qb
    return pl.pallas_call(
        body, name=f"attn_bwd_d{d}", grid=(NH // 2, nb + 1),
        in_specs=in_specs, out_specs=[cur(0), lag, lag],
        out_shape=[SDS((S, D), F32)] * 3,
        scratch_shapes=[pltpu.VMEM((rows, LANE), F32), pltpu.VMEM((rows, LANE), F32)],
        compiler_params=_cp(("parallel", "arbitrary")),
    )(*args)


def _conv_taps(cur, prev, first):
    row = _iota(cur.shape, 0)
    prev = jnp.where(first, 0.0, prev)
    taps = [jnp.where(row < s, pltpu.roll(prev, s, 0), pltpu.roll(cur, s, 0)) for s in (3, 2, 1)]
    return taps + [cur]


def _conv(taps, w, b):
    acc = b + w[0:1, :] * taps[0]
    for k in (1, 2, 3):
        acc = acc + w[k:k + 1, :] * taps[k]
    return acc


def _expand():
    return (_iota((LANE, D), 1) // 64 == _iota((LANE, D), 0)).astype(F32)


def _reduce():
    return (_iota((D, LANE), 0) // 64 == _iota((D, LANE), 1)).astype(F32)


def _ssd_common(xs_raw, xs_prev, bc_raw, bc_prev, dt_raw, first, cw, cb, dtb, alog):
    head_lane = _iota((CH, LANE), 1) < NH
    xs_taps = _conv_taps(xs_raw, xs_prev, first)
    bc_taps = _conv_taps(bc_raw, bc_prev, first)
    xs_c = _conv(xs_taps, cw[:, :D], cb[:, :D])
    bc_c = _conv(bc_taps, cw[:, D:], cb[:, D:])
    xs = xs_c * _sigmoid(xs_c)
    bc = bc_c * _sigmoid(bc_c)
    pre = dt_raw + dtb
    dt = jnp.where(head_lane, jnp.maximum(pre, 0.0) + jnp.log(1.0 + jnp.exp(-jnp.abs(pre))), 0.0)
    a_row = jnp.where(head_lane[0:1], -jnp.exp(alog), 0.0)
    tri = (_iota((CH, CH), 1) <= _iota((CH, CH), 0)).astype(F32)
    cs = _dot_hi(tri, dt * a_row)
    cs_last = cs[CH - 1:CH, :]
    expand = _expand()
    dt_b = _dot_hi(dt, expand)
    e_b = _dot_hi(jnp.exp(cs), expand)
    f_b = _dot_hi(jnp.exp(cs_last - cs), expand)
    return dict(xs_taps=xs_taps, bc_taps=bc_taps, xs_c=xs_c, bc_c=bc_c, xs=xs, bc=bc, pre=pre, dt=dt,
                a_row=a_row, cs=cs, cs_t=cs.T, dt_b=dt_b, e_b=e_b, f_b=f_b, t_b=e_b[CH - 1:CH, :])


def _groups(bc):
    bcb = bc.astype(BF16)
    return [bcb[:, 0:128], bcb[:, 128:256]], [bcb[:, 256:384], bcb[:, 384:512]]


def _decay(q, h, tril):
    seg = q["cs"][:, h:h + 1] - q["cs_t"][h:h + 1, :]
    return jnp.exp(jnp.where(tril, seg, NEG))


def _ssm_fwd(proj, mix, cw, cb, dtb, alog, d_b, nw):
    def body(xs_ref, xsp_ref, bc_ref, bcp_ref, dt_ref, z_ref, cw_ref, cb_ref, dtb_ref, alog_ref, db_ref, nw_ref,
             mix_in_ref, mix_ref, y_ref, st_ref, h_ref):
        del mix_in_ref
        i = pl.program_id(0)

        @pl.when(i == 0)
        def _():
            h_ref[...] = jnp.zeros_like(h_ref)

        q = _ssd_common(xs_ref[...], xsp_ref[...], bc_ref[...], bcp_ref[...], dt_ref[...], i == 0,
                        cw_ref[...], cb_ref[...], dtb_ref[...], alog_ref[...])
        bg, cg = _groups(q["bc"])
        xs = q["xs"]
        xdt = xs * q["dt_b"]
        xdt_b = xdt.astype(BF16)
        h_in = h_ref[...]
        st_ref[...] = h_in
        hb = h_in.astype(BF16)
        tril = _iota((CH, CH), 1) <= _iota((CH, CH), 0)
        lane_a = _iota((CH, LANE), 1) < 64
        cbm = [_dot_nt(cg[g], bg[g]) for g in range(2)]
        pairs = []
        for hp in range(NH // 2):
            xp = xdt_b[:, hp * LANE:(hp + 1) * LANE]
            ya = _dot((cbm[hp // 4] * _decay(q, 2 * hp, tril)).astype(BF16), xp)
            yb = _dot((cbm[hp // 4] * _decay(q, 2 * hp + 1, tril)).astype(BF16), xp)
            pairs.append(jnp.where(lane_a, ya, yb))
        y_diag = jnp.concatenate(pairs, axis=1)
        y_off = jnp.concatenate([_dot(cg[g], hb[:, g * 512:(g + 1) * 512]) for g in range(2)], axis=1) * q["e_b"]
        y = y_diag + y_off + db_ref[...] * xs
        y_ref[...] = y
        xf = (xdt * q["f_b"]).astype(BF16)
        h_ref[...] = q["t_b"] * h_in + jnp.concatenate(
            [_dot_tn(bg[g], xf[:, g * 512:(g + 1) * 512]) for g in range(2)], axis=1)
        z = z_ref[...]
        yz = y * (z * _sigmoid(z))
        outs = []
        for g in range(2):
            v = yz[:, g * 512:(g + 1) * 512]
            outs.append(v * lax.rsqrt(jnp.mean(v * v, axis=-1, keepdims=True) + EPS))
        mix_ref[...] = (jnp.concatenate(outs, axis=1) * nw_ref[...]).astype(BF16)

    def col(width, blk, prev=False):
        if prev:
            return pl.BlockSpec((CH, width), lambda i: (jnp.maximum(i - 1, 0), blk))
        return pl.BlockSpec((CH, width), lambda i: (i, blk))

    def full(a):
        return pl.BlockSpec(a.shape, lambda i: (0,) * a.ndim)

    return pl.pallas_call(
        body, name="ssm_fwd", grid=(NC,),
        in_specs=[col(D, 5), col(D, 5, True), col(512, 12), col(512, 12, True), col(LANE, 52), col(D, 4),
                  full(cw), full(cb), full(dtb), full(alog), full(d_b), full(nw), ANY],
        out_specs=[col(D, 1), col(D, 0), pl.BlockSpec((None, CH, D), lambda i: (i, 0, 0))],
        out_shape=[SDS((S, 2 * D), BF16), SDS((S, D), F32), SDS((NC, CH, D), F32)],
        scratch_shapes=[pltpu.VMEM((CH, D), F32)],
        input_output_aliases={12: 0},
        compiler_params=_cp(("arbitrary",)),
    )(proj, proj, proj, proj, proj, proj, cw, cb, dtb, alog, d_b, nw, mix)


def _ssm_bwd(proj, dmix, y_save, states, cw, cb, dtb, alog, d_b, nw):
    def body(xs_ref, xsp_ref, bc_ref, bcp_ref, dt_ref, z_ref, dn_ref, y_ref, st_ref,
             cw_ref, cb_ref, dtb_ref, alog_ref, db_ref, nw_ref,
             dz_ref, dx_ref, dcw_ref, dcb_ref, dsm_ref, dnw_ref, dh_ref, nxs_ref, nbc_ref):
        i = pl.program_id(0)
        ci = NC - 1 - i

        @pl.when(i == 0)
        def _():
            for ref in (dcw_ref, dcb_ref, dsm_ref, dnw_ref, dh_ref, nxs_ref, nbc_ref):
                ref[...] = jnp.zeros_like(ref)

        cw, cb = cw_ref[...], cb_ref[...]
        q = _ssd_common(xs_ref[...], xsp_ref[...], bc_ref[...], bcp_ref[...], dt_ref[...], ci == 0,
                        cw, cb, dtb_ref[...], alog_ref[...])
        bg, cg = _groups(q["bc"])
        xs, dt_b, e_b, f_b, t_b = q["xs"], q["dt_b"], q["e_b"], q["f_b"], q["t_b"]
        xdt = xs * dt_b
        xdt_b = xdt.astype(BF16)
        h_in = st_ref[...]
        hb = h_in.astype(BF16)
        dh_new = dh_ref[...]
        dhb = dh_new.astype(BF16)
        red = _reduce()

        z, y, dn, nw_v = z_ref[...], y_ref[...], dn_ref[...], nw_ref[...]
        sig = _sigmoid(z)
        sz = z * sig
        yz = y * sz
        gdn = dn * nw_v
        dyz, dnw = [], []
        for g in range(2):
            v, gv = yz[:, g * 512:(g + 1) * 512], gdn[:, g * 512:(g + 1) * 512]
            r = lax.rsqrt(jnp.mean(v * v, axis=-1, keepdims=True) + EPS)
            dnw.append(dn[:, g * 512:(g + 1) * 512] * v * r)
            dyz.append(r * (gv - v * (r * r) * jnp.mean(gv * v, axis=-1, keepdims=True)))
        dyz = jnp.concatenate(dyz, axis=1)
        dnw_ref[...] += jnp.sum(jnp.concatenate(dnw, axis=1), axis=0, keepdims=True)
        dy = dyz * sz
        dz_ref[...] = dyz * y * (sig * (1.0 + z * (1.0 - sig)))
        dy_b = dy.astype(BF16)

        tril = _iota((CH, CH), 1) <= _iota((CH, CH), 0)
        lane_a = _iota((CH, LANE), 1) < 64
        cbm = [_dot_nt(cg[g], bg[g]) for g in range(2)]
        dcbm = [jnp.zeros((CH, CH), F32), jnp.zeros((CH, CH), F32)]
        seg_rows = jnp.zeros((CH, LANE), F32)
        seg_cols = jnp.zeros((LANE, CH), F32)
        row_id, col_id = _iota((CH, LANE), 0), _iota((CH, LANE), 1)
        dx_pairs = []
        for hp in range(NH // 2):
            g = hp // 4
            xp = xdt_b[:, hp * LANE:(hp + 1) * LANE]
            dyp_f = dy[:, hp * LANE:(hp + 1) * LANE]
            dyp = dy_b[:, hp * LANE:(hp + 1) * LANE]
            halves = []
            for k in range(2):
                h = 2 * hp + k
                lane = lane_a if k == 0 else jnp.logical_not(lane_a)
                dec = _decay(q, h, tril)
                gm = cbm[g] * dec
                dgm = _dot_nt(jnp.where(lane, dyp_f, 0.0).astype(BF16), xp)
                dcbm[g] = dcbm[g] + dgm * dec
                prod = dgm * gm
                seg_rows = jnp.where(col_id == h, jnp.sum(prod, axis=1, keepdims=True), seg_rows)
                seg_cols = jnp.where(row_id == h, jnp.sum(prod, axis=0, keepdims=True), seg_cols)
                halves.append(_dot_tn(gm.astype(BF16), dyp))
            dx_pairs.append(jnp.where(lane_a, halves[0], halves[1]))
        dxdt_diag = jnp.concatenate(dx_pairs, axis=1)

        qv = jnp.concatenate([_dot(bg[g], dhb[:, g * 512:(g + 1) * 512]) for g in range(2)], axis=1)
        y_off = jnp.concatenate([_dot(cg[g], hb[:, g * 512:(g + 1) * 512]) for g in range(2)], axis=1) * e_b
        xfq = xdt * f_b * qv
        dxdt = dxdt_diag + f_b * qv
        fdf = _dot_hi(xfq, red)
        dcs = seg_rows - seg_cols.T + _dot_hi(dy * y_off, red) - fdf
        tdt = jnp.sum(dh_new * h_in, axis=0, keepdims=True) * t_b
        last = _dot_hi(jnp.broadcast_to(tdt, (8, D)), red)[0:1, :] + jnp.sum(fdf, axis=0, keepdims=True)
        dcs = dcs + jnp.where(_iota((CH, LANE), 0) == CH - 1, last, 0.0)
        tri_t = (_iota((CH, CH), 1) >= _iota((CH, CH), 0)).astype(F32)
        da = _dot_hi(tri_t, dcs)
        ddt = da * q["a_row"] + _dot_hi(dxdt * xs, red)
        dxs = dxdt * dt_b + db_ref[...] * dy
        ddt_raw = ddt * _sigmoid(q["pre"])
        dsm_ref[0:1, :] += jnp.sum(ddt_raw, axis=0, keepdims=True)
        dsm_ref[1:2, :] += jnp.sum(da * q["dt"], axis=0, keepdims=True) * q["a_row"]
        dsm_ref[2:3, :] += jnp.sum(_dot_hi(dy * xs, red), axis=0, keepdims=True)
        edy = (e_b * dy).astype(BF16)
        xf = (xdt * f_b).astype(BF16)
        dbs, dcs_g, dhs = [], [], []
        for g in range(2):
            sl = slice(g * 512, (g + 1) * 512)
            dcb_b = dcbm[g].astype(BF16)
            dcs_g.append(_dot(dcb_b, bg[g]) + _dot_nt(edy[:, sl], hb[:, sl]))
            dbs.append(_dot_tn(dcb_b, cg[g]) + _dot_nt(xf[:, sl], dhb[:, sl]))
            dhs.append(_dot_tn(cg[g], edy[:, sl]))
        dh_ref[...] = t_b * dh_new + jnp.concatenate(dhs, axis=1)
        dbc = jnp.concatenate(dbs + dcs_g, axis=1)

        def conv_bwd(dact, pre, taps, w, nxt_ref, lo):
            s = _sigmoid(pre)
            dconv = dact * (s * (1.0 + pre * (1.0 - s)))
            nxt = nxt_ref[...]
            row = _iota(dconv.shape, 0)
            hi = lo + dconv.shape[1]
            dcb_ref[:, lo:hi] += jnp.sum(dconv, axis=0, keepdims=True)
            dx = w[3:4, :] * dconv
            for k in range(4):
                dcw_ref[k:k + 1, lo:hi] += jnp.sum(dconv * taps[k], axis=0, keepdims=True)
            for s_ in (1, 2, 3):
                up = jnp.where(row >= CH - s_, pltpu.roll(nxt, CH - s_, 0), pltpu.roll(dconv, CH - s_, 0))
                dx = dx + w[3 - s_:4 - s_, :] * up
            nxt_ref[...] = dconv
            return dx

        dx_ref[:, 0:D] = conv_bwd(dxs, q["xs_c"], q["xs_taps"], cw[:, :D], nxs_ref, 0)
        dx_ref[:, D:D + 512] = conv_bwd(dbc, q["bc_c"], q["bc_taps"], cw[:, D:], nbc_ref, D)
        dx_ref[:, D + 512:D + 640] = ddt_raw
        dx_ref[:, D + 640:] = jnp.zeros((CH, 2 * D - D - 640), F32)

    def col(width, blk, prev=False):
        if prev:
            return pl.BlockSpec((CH, width), lambda i: (jnp.maximum(NC - 2 - i, 0), blk))
        return pl.BlockSpec((CH, width), lambda i: (NC - 1 - i, blk))

    def full(a):
        return pl.BlockSpec(a.shape, lambda i: (0,) * len(a.shape))

    acc_shapes = [SDS((4, 1536), F32), SDS((1, 1536), F32), SDS((8, LANE), F32), SDS((1, D), F32)]
    return pl.pallas_call(
        body, name="ssm_bwd", grid=(NC,),
        in_specs=[col(D, 5), col(D, 5, True), col(512, 12), col(512, 12, True), col(LANE, 52), col(D, 4),
                  col(D, 1), col(D, 0), pl.BlockSpec((None, CH, D), lambda i: (NC - 1 - i, 0, 0)),
                  full(cw), full(cb), full(dtb), full(alog), full(d_b), full(nw)],
        out_specs=[col(D, 0), col(2 * D, 0)] + [full(a) for a in acc_shapes],
        out_shape=[SDS((S, D), F32), SDS((S, 2 * D), F32)] + acc_shapes,
        scratch_shapes=[pltpu.VMEM((CH, D), F32), pltpu.VMEM((CH, D), F32), pltpu.VMEM((CH, 512), F32)],
        compiler_params=_cp(("arbitrary",)),
    )(proj, proj, proj, proj, proj, proj, dmix, y_save, states, cw, cb, dtb, alog, d_b, nw)


def _outproj_loss(mix, w_out, x, tgt, nw):
    tm = 256

    def body(mix_ref, w_ref, x_ref, t_ref, nw_ref, dy_ref, dmix_ref, dw_ref, dnw_ref, loss_ref):
        @pl.when(pl.program_id(0) == 0)
        def _():
            dw_ref[...] = jnp.zeros_like(dw_ref)
            dnw_ref[...] = jnp.zeros_like(dnw_ref)
            loss_ref[...] = jnp.zeros_like(loss_ref)

        mixv, w = mix_ref[...], w_ref[...]
        out = _dot(mixv, w)
        r = lax.rsqrt(jnp.mean(out * out, axis=-1, keepdims=True) + EPS)
        nh = out * r
        nw_v = nw_ref[...]
        err = x_ref[...] + nh * nw_v - t_ref[...]
        loss_ref[...] += 0.5 * jnp.sum(jnp.mean(err * err, axis=-1, keepdims=True), axis=0, keepdims=True)
        dy = err * (1.0 / D)
        dy_ref[...] = dy
        dnw_ref[...] += jnp.sum(dy * nh, axis=0, keepdims=True)
        gdn = dy * nw_v
        dout = (r * (gdn - nh * jnp.mean(gdn * nh, axis=-1, keepdims=True))).astype(BF16)
        dmix_ref[...] = _dot_nt(dout, w)
        dw_ref[...] += _dot_tn(mixv, dout)

    row = lambda w: pl.BlockSpec((tm, w), lambda i: (i, 0))
    full = lambda s: pl.BlockSpec(s, lambda i: (0, 0))
    return pl.pallas_call(
        body, name="outproj_loss", grid=(S // tm,),
        in_specs=[row(2 * D), full((2 * D, D)), row(D), row(D), full((1, D))],
        out_specs=[row(D), row(2 * D), full((2 * D, D)), full((1, D)), full((1, LANE))],
        out_shape=[SDS((S, D), F32), SDS((S, 2 * D), F32), SDS((2 * D, D), F32), SDS((1, D), F32),
                   SDS((1, LANE), F32)],
        compiler_params=_cp(("arbitrary",)),
    )(mix, w_out, x, tgt, nw)


def _inproj_bwd_dx(srcs, dxbcdt, w_all, x, dy, nw):
    tm = 256
    nk = DP // D

    def body(*refs):
        src_refs = refs[:nk]
        w_ref, x_ref, dy_ref, nw_ref, gx_ref, dnw_ref, acc_ref = refs[nk:]
        i, kk = pl.program_id(0), pl.program_id(1)

        @pl.when((i == 0) & (kk == 0))
        def _():
            dnw_ref[...] = jnp.zeros_like(dnw_ref)

        @pl.when(kk == 0)
        def _():
            acc_ref[...] = jnp.zeros_like(acc_ref)

        for s, ref in enumerate(src_refs):
            @pl.when(kk == s)
            def _(ref=ref):
                acc_ref[...] += _dot_nt(ref[...].astype(BF16), w_ref[...])

        @pl.when(kk == nk - 1)
        def _():
            xf, du, nw_v = x_ref[...], acc_ref[...], nw_ref[...]
            r = lax.rsqrt(jnp.mean(xf * xf, axis=-1, keepdims=True) + EPS)
            xh = xf * r
            dnw_ref[...] += jnp.sum(du * xh, axis=0, keepdims=True)
            gdu = du * nw_v
            gx_ref[...] = r * (gdu - xh * jnp.mean(gdu * xh, axis=-1, keepdims=True)) + dy_ref[...]

    row = pl.BlockSpec((tm, D), lambda i, k: (i, 0))
    row1 = pl.BlockSpec((tm, D), lambda i, k: (i, 1))
    one = pl.BlockSpec((1, D), lambda i, k: (0, 0))
    return pl.pallas_call(
        body, name="inproj_bwd_dx", grid=(S // tm, nk),
        in_specs=[row] * len(srcs) + [row, row1, pl.BlockSpec((D, D), lambda i, k: (0, k)), row, row, one],
        out_specs=[row, one],
        out_shape=[SDS((S, D), F32), SDS((1, D), F32)],
        scratch_shapes=[pltpu.VMEM((tm, D), F32)],
        compiler_params=_cp(("arbitrary", "arbitrary")),
    )(*srcs, dxbcdt, dxbcdt, w_all, x, dy, nw)


def _dw(u, dsec, name):
    ts = 512
    ncol = dsec.shape[1] // D

    def body(u_ref, d_ref, o_ref):
        @pl.when(pl.program_id(1) == 0)
        def _():
            o_ref[...] = jnp.zeros_like(o_ref)

        o_ref[...] += _dot_tn(u_ref[...], d_ref[...].astype(BF16))

    return pl.pallas_call(
        body, name=name, grid=(ncol, S // ts),
        in_specs=[pl.BlockSpec((ts, D), lambda j, i: (i, 0)), pl.BlockSpec((ts, D), lambda j, i: (i, j))],
        out_specs=pl.BlockSpec((D, D), lambda j, i: (0, j)),
        out_shape=SDS((D, ncol * D), F32),
        compiler_params=_cp(("parallel", "arbitrary")),
    )(u, dsec)


def _place():
    x, y, c = lax.axis_index("x"), lax.axis_index("y"), lax.axis_index("c")
    return x, y, c, 2 * x + y


def _chip_of(x, y, k):
    px = 1 - x if k & 2 else x
    py = 1 - y if k & 1 else y
    return px, py, 2 * px + py


def _remote(src, dst, send_sem, recv_sem, dev):
    return pltpu.make_async_remote_copy(src_ref=src, dst_ref=dst, send_sem=send_sem, recv_sem=recv_sem,
                                        device_id=dev, device_id_type=MESH)


def _gather_weights(w_in_b, w_out_b, conv_w):
    hin, hout = D // 2, w_out_b.shape[0] // 2

    def body(win_ref, wout_ref, cw_ref, gin_ref, gout_ref, gcw_ref, send, recv, fsend, frecv, lsem):
        x, y, c, j = _place()
        sib = (x, y, 1 - c)
        rin, rout = pl.ds(c * hin, hin), pl.ds(c * hout, hout)
        sin, sout = pl.ds((1 - c) * hin, hin), pl.ds((1 - c) * hout, hout)
        local = [pltpu.make_async_copy(win_ref, gin_ref.at[j], lsem.at[0]),
                 pltpu.make_async_copy(wout_ref, gout_ref.at[j], lsem.at[1]),
                 pltpu.make_async_copy(cw_ref, gcw_ref.at[j], lsem.at[2])]
        for cp in local:
            cp.start()
        sends = []
        for k in (1, 2, 3):
            px, py, _ = _chip_of(x, y, k)
            dev = (px, py, c)
            sends += [_remote(win_ref.at[rin], gin_ref.at[j, rin], send.at[k - 1], recv.at[k - 1], dev),
                      _remote(wout_ref.at[rout], gout_ref.at[j, rout], send.at[k + 2], recv.at[k + 2], dev),
                      _remote(cw_ref, gcw_ref.at[j], send.at[k + 5], recv.at[k + 5], dev)]
        for cp in sends:
            cp.start()
        fwd = []
        for k in (1, 2, 3):
            _, _, pj = _chip_of(x, y, k)
            _remote(win_ref.at[rin], gin_ref.at[pj, rin], send.at[k - 1], recv.at[k - 1], sib).wait_recv()
            f_in = _remote(gin_ref.at[pj, rin], gin_ref.at[pj, rin], fsend.at[k - 1], frecv.at[k - 1], sib)
            f_in.start()
            _remote(wout_ref.at[rout], gout_ref.at[pj, rout], send.at[k + 2], recv.at[k + 2], sib).wait_recv()
            f_out = _remote(gout_ref.at[pj, rout], gout_ref.at[pj, rout], fsend.at[k + 2], frecv.at[k + 2], sib)
            f_out.start()
            fwd += [f_in, f_out]
        for k in (1, 2, 3):
            _, _, pj = _chip_of(x, y, k)
            _remote(win_ref.at[sin], gin_ref.at[pj, sin], fsend.at[k - 1], frecv.at[k - 1], sib).wait_recv()
            _remote(wout_ref.at[sout], gout_ref.at[pj, sout], fsend.at[k + 2], frecv.at[k + 2], sib).wait_recv()
            _remote(cw_ref, gcw_ref.at[pj], send.at[k + 5], recv.at[k + 5], sib).wait_recv()
        for cp in sends + fwd:
            cp.wait_send()
        for cp in local:
            cp.wait()

    return pl.pallas_call(
        body, name="gather_weights",
        in_specs=[ANY, ANY, ANY], out_specs=[ANY, ANY, ANY],
        out_shape=[SDS((4,) + w_in_b.shape, BF16), SDS((4,) + w_out_b.shape, BF16), SDS((4,) + conv_w.shape, F32)],
        scratch_shapes=[pltpu.SemaphoreType.DMA((9,)), pltpu.SemaphoreType.DMA((9,)),
                        pltpu.SemaphoreType.DMA((6,)), pltpu.SemaphoreType.DMA((6,)), pltpu.SemaphoreType.DMA((3,))],
        compiler_params=pltpu.CompilerParams(has_side_effects=True),
    )(w_in_b, w_out_b, conv_w)


def _pair_exchange(gw, go):
    hw, ho = gw.shape[1] // 2, go.shape[1] // 2

    def body(gw_ref, go_ref, rw_ref, ro_ref, send, recv):
        x, y, c, _ = _place()
        sib = (x, y, 1 - c)
        theirs_w, theirs_o = pl.ds((1 - c) * hw, hw), pl.ds((1 - c) * ho, ho)
        cps = [_remote(gw_ref.at[:, theirs_w], rw_ref, send.at[0], recv.at[0], sib),
               _remote(go_ref.at[:, theirs_o], ro_ref, send.at[1], recv.at[1], sib)]
        for cp in cps:
            cp.start()
        for cp in cps:
            cp.wait()

    return pl.pallas_call(
        body, name="pair_exchange", in_specs=[ANY, ANY], out_specs=[ANY, ANY],
        out_shape=[SDS((4, hw, gw.shape[2]), F32), SDS((4, ho, go.shape[2]), F32)],
        scratch_shapes=[pltpu.SemaphoreType.DMA((2,)), pltpu.SemaphoreType.DMA((2,))],
        compiler_params=pltpu.CompilerParams(has_side_effects=True),
    )(gw, go)


def _pair_sum(cidx, g, r, name):
    half, width = r.shape[1], r.shape[2]
    tr = min(half, 256)
    nt = half // tr

    def body(c_ref, g_ref, r_ref, o_ref):
        del c_ref
        o_ref[...] = (g_ref[...] + r_ref[...]).astype(BF16)

    return pl.pallas_call(
        body, name=name,
        grid_spec=pltpu.PrefetchScalarGridSpec(
            num_scalar_prefetch=1, grid=(4, nt),
            in_specs=[pl.BlockSpec((None, tr, width), lambda s, t, c: (s, c[0] * nt + t, 0)),
                      pl.BlockSpec((None, tr, width), lambda s, t, c: (s, t, 0))],
            out_specs=pl.BlockSpec((None, tr, width), lambda s, t, c: (s, t, 0))),
        out_shape=SDS(r.shape, BF16),
        compiler_params=_cp(("parallel", "parallel")),
    )(cidx, g, r)


def _chip_exchange(pw, po, small):
    def body(pw_ref, po_ref, sm_ref, rw_ref, ro_ref, rs_ref, send, recv, ssend, srecv, lsem):
        x, y, c, j = _place()
        me = 2 * j + c
        local = [pltpu.make_async_copy(pw_ref.at[j], rw_ref.at[j], lsem.at[0]),
                 pltpu.make_async_copy(po_ref.at[j], ro_ref.at[j], lsem.at[1]),
                 pltpu.make_async_copy(sm_ref, rs_ref.at[me], lsem.at[2])]
        for cp in local:
            cp.start()
        cps = []
        for k in (1, 2, 3):
            px, py, pj = _chip_of(x, y, k)
            dev = (px, py, c)
            cps += [_remote(pw_ref.at[pj], rw_ref.at[j], send.at[k - 1], recv.at[k - 1], dev),
                    _remote(po_ref.at[pj], ro_ref.at[j], send.at[k + 2], recv.at[k + 2], dev)]
        for k in range(1, 8):
            px, py, _ = _chip_of(x, y, k >> 1)
            pc = 1 - c if k & 1 else c
            cps.append(_remote(sm_ref, rs_ref.at[me], ssend.at[k - 1], srecv.at[k - 1], (px, py, pc)))
        for cp in cps:
            cp.start()
        for k in (1, 2, 3):
            _, _, pj = _chip_of(x, y, k)
            _remote(pw_ref.at[pj], rw_ref.at[pj], send.at[k - 1], recv.at[k - 1], (x, y, c)).wait_recv()
            _remote(po_ref.at[pj], ro_ref.at[pj], send.at[k + 2], recv.at[k + 2], (x, y, c)).wait_recv()
        for k in range(1, 8):
            _, _, pj = _chip_of(x, y, k >> 1)
            pc = 1 - c if k & 1 else c
            _remote(sm_ref, rs_ref.at[2 * pj + pc], ssend.at[k - 1], srecv.at[k - 1], (x, y, c)).wait_recv()
        for cp in cps:
            cp.wait_send()
        for cp in local:
            cp.wait()

    return pl.pallas_call(
        body, name="chip_exchange", in_specs=[ANY, ANY, ANY], out_specs=[ANY, ANY, ANY],
        out_shape=[SDS(pw.shape, BF16), SDS(po.shape, BF16), SDS((8,) + small.shape, F32)],
        scratch_shapes=[pltpu.SemaphoreType.DMA((6,)), pltpu.SemaphoreType.DMA((6,)),
                        pltpu.SemaphoreType.DMA((7,)), pltpu.SemaphoreType.DMA((7,)), pltpu.SemaphoreType.DMA((3,))],
        compiler_params=pltpu.CompilerParams(has_side_effects=True),
    )(pw, po, small)


def _slot_sum(r, name):
    n, rows, width = r.shape
    tr = min(rows, 256)

    def body(r_ref, o_ref):
        acc = r_ref[0].astype(F32)
        for s in range(1, n):
            acc = acc + r_ref[s].astype(F32)
        o_ref[...] = acc

    return pl.pallas_call(
        body, name=name, grid=(rows // tr,),
        in_specs=[pl.BlockSpec((n, tr, width), lambda t: (0, t, 0))],
        out_specs=pl.BlockSpec((tr, width), lambda t: (t, 0)),
        out_shape=SDS((rows, width), F32),
        compiler_params=_cp(("parallel",)),
    )(r)


def _half_exchange(hw, ho):
    def body(hw_ref, ho_ref, gw_ref, go_ref, send, recv, lsem):
        x, y, c, _ = _place()
        sib = (x, y, 1 - c)
        mine_w, mine_o = pl.ds(c * hw.shape[0], hw.shape[0]), pl.ds(c * ho.shape[0], ho.shape[0])
        local = [pltpu.make_async_copy(hw_ref, gw_ref.at[mine_w], lsem.at[0]),
                 pltpu.make_async_copy(ho_ref, go_ref.at[mine_o], lsem.at[1])]
        cps = [_remote(hw_ref, gw_ref.at[mine_w], send.at[0], recv.at[0], sib),
               _remote(ho_ref, go_ref.at[mine_o], send.at[1], recv.at[1], sib)]
        for cp in local + cps:
            cp.start()
        theirs_w = pl.ds((1 - c) * hw.shape[0], hw.shape[0])
        theirs_o = pl.ds((1 - c) * ho.shape[0], ho.shape[0])
        _remote(hw_ref, gw_ref.at[theirs_w], send.at[0], recv.at[0], sib).wait_recv()
        _remote(ho_ref, go_ref.at[theirs_o], send.at[1], recv.at[1], sib).wait_recv()
        for cp in cps:
            cp.wait_send()
        for cp in local:
            cp.wait()

    return pl.pallas_call(
        body, name="half_exchange", in_specs=[ANY, ANY], out_specs=[ANY, ANY],
        out_shape=[SDS((2 * hw.shape[0], hw.shape[1]), F32), SDS((2 * ho.shape[0], ho.shape[1]), F32)],
        scratch_shapes=[pltpu.SemaphoreType.DMA((2,)), pltpu.SemaphoreType.DMA((2,)), pltpu.SemaphoreType.DMA((2,))],
        compiler_params=pltpu.CompilerParams(has_side_effects=True),
    )(hw, ho)


def _adamw(w, g, m, v, name):
    rows, width = w.shape
    tr = min(rows, 256)

    def body(w_ref, g_ref, m_ref, v_ref, d_ref, nm_ref, nv_ref):
        gv = g_ref[...]
        nm = ADAM_B1 * m_ref[...] + (1.0 - ADAM_B1) * gv
        nv = ADAM_B2 * v_ref[...] + (1.0 - ADAM_B2) * (gv * gv)
        m_hat = nm / (1.0 - ADAM_B1 ** ADAM_STEP)
        v_hat = nv / (1.0 - ADAM_B2 ** ADAM_STEP)
        d_ref[...] = -ADAM_LR * (m_hat / (jnp.sqrt(v_hat) + ADAM_EPS) + ADAM_WD * w_ref[...])
        nm_ref[...] = nm
        nv_ref[...] = nv

    t = pl.BlockSpec((tr, width), lambda i: (i, 0))
    return pl.pallas_call(
        body, name=name, grid=(rows // tr,), in_specs=[t] * 4, out_specs=[t] * 3,
        out_shape=[SDS(w.shape, F32)] * 3, compiler_params=_cp(("parallel",)),
    )(w, g, m, v)


def _rows128(a, rows):
    flat = a.reshape(-1)
    return jnp.pad(flat, (0, rows * LANE - flat.shape[0])).reshape(rows, LANE)


def _pack_small(conv_w, norm_pre, conv_b, ssm_norm, norm_post, dtb, alog, dsk):
    cw_rows = 48 if conv_w.shape[-1] == 1536 else 16
    vec = jnp.concatenate([_rows128(dtb, 1), _rows128(alog, 1), _rows128(dsk, 1), jnp.zeros((5, LANE), F32)], axis=0)
    return jnp.concatenate([_rows128(conv_w, cw_rows), _rows128(norm_pre, 8), _rows128(conv_b, 16),
                            _rows128(ssm_norm, 8), _rows128(norm_post, 8), vec], axis=0)


def _unpack_small(p, cw_cols):
    cw_rows = 48 if cw_cols == 1536 else 16
    o = cw_rows
    conv_w = p[:cw_rows].reshape(-1)[:4 * cw_cols].reshape(1, 4, cw_cols)
    norm_pre = p[o:o + 8].reshape(1, D)
    conv_b = p[o + 8:o + 24].reshape(-1)[:1536].reshape(1, 1536)
    ssm_norm = p[o + 24:o + 32].reshape(1, D)
    norm_post = p[o + 32:o + 40].reshape(1, D)
    vec = p[o + 40:o + 48]
    return conv_w, norm_pre, conv_b, ssm_norm, norm_post, vec[0:1, :NH], vec[1:2, :NH], vec[2:3, :NH]


def _pad_lanes(a):
    return jnp.pad(a, ((0, 0), (0, LANE - a.shape[1])))


def kernel(x, norm_pre_w, w_in, conv_w, conv_b, dt_bias, a_log, d_skip, ssm_norm_w, w_out, norm_post_w, loss_target, m_norm_pre_w, m_w_in, m_conv_w, m_conv_b, m_dt_bias, m_a_log, m_d_skip, m_ssm_norm_w, m_w_out, m_norm_post_w, v_norm_pre_w, v_w_in, v_conv_w, v_conv_b, v_dt_bias, v_a_log, v_d_skip, v_ssm_norm_w, v_w_out, v_norm_post_w):
    xi, yi, ci = lax.axis_index("x"), lax.axis_index("y"), lax.axis_index("c")
    chip = 2 * xi + yi
    x2, tgt = x[0], loss_target[0]

    gin, gout, gcw = _gather_weights(w_in[0].astype(BF16), w_out[0].astype(BF16), conv_w[0])
    w_all = jnp.concatenate([gin[0], gin[1], gin[2], gin[3], jnp.zeros((D, DP - 4 * SHARD), BF16)], axis=1)
    w_out_all = gout.reshape(2 * D, D)
    cw_all = jnp.concatenate([gcw[0], gcw[1], gcw[2], gcw[3]], axis=1)
    loss_part, grad_x, dw_all, dw_out, small = _local_step(
        x2, tgt, w_all, w_out_all, cw_all, norm_pre_w, conv_b, dt_bias, a_log, d_skip, ssm_norm_w, norm_post_w)
    gw = jnp.stack([dw_all[:, k * SHARD:(k + 1) * SHARD] for k in range(4)])
    go = dw_out.reshape(4, D // 2, D)

    cidx = jnp.reshape(ci, (1,)).astype(jnp.int32)
    rw, ro = _pair_exchange(gw, go)
    pw, po = _pair_sum(cidx, gw, rw, "pair_sum_in"), _pair_sum(cidx, go, ro, "pair_sum_out")
    cw_r, co_r, sm_r = _chip_exchange(pw, po, small)
    g_in, g_out = _half_exchange(_slot_sum(cw_r, "chip_sum_in"), _slot_sum(co_r, "chip_sum_out"))
    g_small = _slot_sum(sm_r, "small_sum")
    g_cw, g_npre, g_cb, g_nssm, g_npost, g_dtb, g_alog, g_dsk = _unpack_small(g_small, 1536)
    g_cw = lax.dynamic_slice_in_dim(g_cw, chip * 384, 384, axis=2)

    d_in, nm_in, nv_in = _adamw(w_in[0], g_in, m_w_in[0], v_w_in[0], "adamw_in")
    d_out, nm_out, nv_out = _adamw(w_out[0], g_out, m_w_out[0], v_w_out[0], "adamw_out")
    packed = [_pack_small(*t) for t in (
        (conv_w, norm_pre_w, conv_b, ssm_norm_w, norm_post_w, dt_bias, a_log, d_skip),
        (g_cw, g_npre, g_cb, g_nssm, g_npost, g_dtb, g_alog, g_dsk),
        (m_conv_w, m_norm_pre_w, m_conv_b, m_ssm_norm_w, m_norm_post_w, m_dt_bias, m_a_log, m_d_skip),
        (v_conv_w, v_norm_pre_w, v_conv_b, v_ssm_norm_w, v_norm_post_w, v_dt_bias, v_a_log, v_d_skip))]
    small_out = [_unpack_small(p, 384) for p in _adamw(*packed, "adamw_small")]

    loss = lax.psum(loss_part[0, 0], ("x", "y", "c"))

    def ordered(cw_, npre, cb_, nssm, npost, dtb_, alog_, dsk_, big_in, big_out):
        return [npre, big_in[None], cw_, cb_, dtb_, alog_, dsk_, nssm, big_out[None], npost]

    grads = ordered(g_cw, g_npre, g_cb, g_nssm, g_npost, g_dtb, g_alog, g_dsk, g_in, g_out)
    deltas = ordered(*small_out[0], d_in, d_out)
    new_m = ordered(*small_out[1], nm_in, nm_out)
    new_v = ordered(*small_out[2], nv_in, nv_out)
    return (loss, grad_x[None], *grads, *deltas, *new_m, *new_v)


def _local_step(x2, tgt, w_all, w_out_all, cw_all, norm_pre_w, conv_b, dt_bias, a_log, d_skip, ssm_norm_w,
                norm_post_w):
    dtb, alog = _pad_lanes(dt_bias), _pad_lanes(a_log)
    d_b = jnp.repeat(d_skip, 64, axis=1)

    proj, u = _inproj_fwd(x2, norm_pre_w, w_all)
    o1, l1 = _attn_fwd(proj, 1)
    o2, l2 = _attn_fwd(proj, 4)
    o3, l3 = _attn_fwd(proj, 16)
    mix, attn_pre, lse = _attn_merge(o1, o2, o3, l1, l2, l3, proj)
    mix, y_save, states = _ssm_fwd(proj, mix, cw_all, conv_b, dtb, alog, d_b, ssm_norm_w)

    dy, dmix, dw_out, dnw_post, loss_part = _outproj_loss(mix, w_out_all, x2, tgt, norm_post_w)
    do, delta, dg = _attn_gate_bwd(dmix, attn_pre, proj)
    acc = _attn_bwd(proj, do, lse, delta, 1, None)
    acc = _attn_bwd(proj, do, lse, delta, 4, acc)
    dq, dk, dv = _attn_bwd(proj, do, lse, delta, 16, acc)
    dz, dxbcdt, dcw, dcb, dvec, dnw_ssm = _ssm_bwd(proj, dmix, y_save, states, cw_all, conv_b, dtb, alog, d_b,
                                                   ssm_norm_w)
    srcs = [dq, dk, dv, dg, dz]
    grad_x, dnw_pre = _inproj_bwd_dx(srcs, dxbcdt, w_all, x2, dy, norm_pre_w)
    dws = [_dw(u, s, f"dw_in_{n}") for s, n in zip(srcs + [dxbcdt], ("q", "k", "v", "g", "z", "xbcdt"))]
    dw_all = jnp.concatenate(dws, axis=1)
    small = _pack_small(dcw, dnw_pre, dcb, dnw_ssm, dnw_post, dvec[0:1, :NH], dvec[1:2, :NH], dvec[2:3, :NH])
    return loss_part, grad_x, dw_all, dw_out, small
```

```python
import functools

import jax
import jax.numpy as jnp
from jax import lax
from jax.experimental import pallas as pl
from jax.experimental.pallas import tpu as pltpu

F32 = jnp.float32
BF16 = jnp.bfloat16
MESH = pl.DeviceIdType.MESH
SDS = jax.ShapeDtypeStruct
ANY = pl.BlockSpec(memory_space=pl.ANY)

S = 4096
D = 1024
DP = 7168
SHARD = 1668
OFF_G, OFF_Z = 3072, 4096
NH = 16
CH = 128
NC = S // CH
EPS = 1e-6
NEG = -1e30
LANE = 128
VMEM_LIMIT = 48 * 1024 * 1024

ADAM_LR, ADAM_B1, ADAM_B2, ADAM_EPS, ADAM_WD, ADAM_STEP = 0.001, 0.9, 0.999, 1e-08, 0.01, 10


def _cp(sem, **kw):
    return pltpu.CompilerParams(dimension_semantics=sem, vmem_limit_bytes=VMEM_LIMIT, **kw)


def _dot(a, b):
    return jnp.dot(a, b, preferred_element_type=F32)


def _dot_nt(a, b):
    return lax.dot_general(a, b, (((1,), (1,)), ((), ())), preferred_element_type=F32)


def _dot_tn(a, b):
    return lax.dot_general(a, b, (((0,), (0,)), ((), ())), preferred_element_type=F32)


def _pieces(x, n):
    out = []
    for _ in range(n):
        p = x.astype(BF16)
        out.append(p)
        x = x - p.astype(F32)
    return out


def _pick(x, sel, n=2):
    parts = [_dot(p, sel) for p in _pieces(x, n)]
    return functools.reduce(jnp.add, parts)


def _pick_left(sel, x, n=3):
    parts = [_dot(sel, p) for p in _pieces(x, n)]
    return functools.reduce(jnp.add, parts)


def _sigmoid(v):
    return 0.5 * jnp.tanh(0.5 * v) + 0.5


def _iota(shape, dim):
    return lax.broadcasted_iota(jnp.int32, shape, dim)


def _inproj_fwd(x, nw, w_all):
    tm, tn = 512, 1024

    def body(x_ref, nw_ref, w_ref, proj_ref, u_ref):
        @pl.when(pl.program_id(1) == 0)
        def _():
            xf = x_ref[...]
            r = lax.rsqrt(jnp.mean(xf * xf, axis=-1, keepdims=True) + EPS)
            u_ref[...] = (xf * r * nw_ref[...]).astype(BF16)

        proj_ref[...] = _dot(u_ref[...], w_ref[...])

    return pl.pallas_call(
        body, name="inproj_fwd", grid=(S // tm, DP // tn),
        in_specs=[pl.BlockSpec((tm, D), lambda i, j: (i, 0)), pl.BlockSpec((1, D), lambda i, j: (0, 0)),
                  pl.BlockSpec((D, tn), lambda i, j: (0, j))],
        out_specs=[pl.BlockSpec((tm, tn), lambda i, j: (i, j)), pl.BlockSpec((tm, D), lambda i, j: (i, 0))],
        out_shape=[SDS((S, DP), F32), SDS((S, D), BF16)],
        compiler_params=_cp(("parallel", "arbitrary")),
    )(x, nw, w_all)


ATTN_QB = {1: 4, 4: 1, 16: 1}


def _unit_rows(r, u, d):
    return pl.ds(r + d * CH * u, CH, stride=d) if d > 1 else pl.ds(CH * u, CH)


def _for_units(d, qb, fn):
    for r in range(d):
        for u in range(qb):
            fn(r, u)


def _attn_mask(has_prev):
    qi, kj = _iota((2 * CH, 2 * CH), 0) & (CH - 1), _iota((2 * CH, 2 * CH), 1)
    cur_ok = (kj >= CH) & (kj - CH <= qi)
    prev_ok = (kj < CH) & (kj >= qi)
    return cur_ok | (prev_ok & has_prev)


def _stack_heads(v, lane_a):
    return jnp.concatenate([jnp.where(lane_a, v, 0.0), jnp.where(lane_a, 0.0, v)], axis=0).astype(BF16)


def _unit_kv(p_ref, c_ref, r, u, d, qb):
    prev = p_ref[_unit_rows(r, 0, d), :] if u == 0 else c_ref[_unit_rows(r, u - 1, d), :]
    return jnp.concatenate([prev, c_ref[_unit_rows(r, u, d), :]], axis=0).astype(BF16)


def _attn_specs(d, qb, lag_out):
    rows, prows = CH * d * qb, CH * d
    nb = S // rows
    last = nb - 1

    def cur(off):
        return pl.BlockSpec((rows, LANE), lambda c, i: (jnp.minimum(i, last), off + c))

    def prev(off):
        return pl.BlockSpec((prows, LANE), lambda c, i: (jnp.clip(i * qb - 1, 0, S // prows - 1), off + c))

    lag = pl.BlockSpec((rows, LANE), lambda c, i: (jnp.clip(i - 1, 0, last), c)) if lag_out else None
    return nb, cur, prev, lag


def _attn_fwd(proj, d):
    qb = ATTN_QB[d]
    nb, cur, prev, _ = _attn_specs(d, qb, False)

    def body(q_ref, kp_ref, kc_ref, vp_ref, vc_ref, o_ref, l_ref):
        i = pl.program_id(1)
        lane_a = _iota((CH, LANE), 1) < 64
        mask_first, mask_rest = _attn_mask(i > 0), _attn_mask(True)

        def unit(r, u):
            sl = _unit_rows(r, u, d)
            q2 = _stack_heads(q_ref[sl, :] * 0.125, lane_a)
            k2, v2 = _unit_kv(kp_ref, kc_ref, r, u, d, qb), _unit_kv(vp_ref, vc_ref, r, u, d, qb)
            s = jnp.where(mask_first if u == 0 else mask_rest, _dot_nt(q2, k2), NEG)
            m = jnp.max(s, axis=1, keepdims=True)
            p = jnp.exp(s - m)
            l = jnp.sum(p, axis=1, keepdims=True)
            o2 = _dot(p.astype(BF16), v2) / l
            lse2 = m + jnp.log(l)
            o_ref[sl, :] = jnp.where(lane_a, o2[:CH], o2[CH:])
            l_ref[sl, :] = jnp.where(lane_a, lse2[:CH], lse2[CH:])

        _for_units(d, qb, unit)

    return pl.pallas_call(
        body, name=f"attn_fwd_d{d}", grid=(NH // 2, nb),
        in_specs=[cur(0), prev(8), cur(8), prev(16), cur(16)],
        out_specs=[cur(0), cur(0)],
        out_shape=[SDS((S, D), F32), SDS((S, D), F32)],
        compiler_params=_cp(("parallel", "arbitrary")),
    )(proj, proj, proj, proj, proj)


def _attn_merge(o1, o2, o3, l1, l2, l3, proj):
    tm = 512

    def body(o1r, o2r, o3r, l1r, l2r, l3r, g_ref, mix_ref, pre_ref, lse_ref):
        l1v, l2v, l3v = l1r[...], l2r[...], l3r[...]
        m = jnp.maximum(jnp.maximum(l1v, l2v), l3v)
        e1, e2, e3 = jnp.exp(l1v - m), jnp.exp(l2v - m), jnp.exp(l3v - m)
        tot = e1 + e2 + e3
        out = (e1 * o1r[...] + e2 * o2r[...] + e3 * o3r[...]) / tot
        g = g_ref[...]
        pre_ref[...] = out
        lse_ref[...] = m + jnp.log(tot)
        mix_ref[...] = (out * (g * _sigmoid(g))).astype(BF16)

    t = pl.BlockSpec((tm, D), lambda i: (i, 0))
    return pl.pallas_call(
        body, name="attn_merge", grid=(S // tm,),
        in_specs=[t, t, t, t, t, t, pl.BlockSpec((tm, D), lambda i: (i, OFF_G // D))],
        out_specs=[t, t, t],
        out_shape=[SDS((S, 2 * D), BF16), SDS((S, D), F32), SDS((S, D), F32)],
        compiler_params=_cp(("parallel",)),
    )(o1, o2, o3, l1, l2, l3, proj)


def _attn_gate_bwd(dmix, pre, proj):
    tm = 512

    def body(dm_ref, pre_ref, g_ref, do_ref, delta_ref, dg_ref):
        g, dm, pre_v = g_ref[...], dm_ref[...], pre_ref[...]
        sig = _sigmoid(g)
        do = dm * (g * sig)
        do_ref[...] = do
        dg_ref[...] = (dm * pre_v * (sig * (1.0 + g * (1.0 - sig)))).astype(BF16)
        prod = do * pre_v
        same_head = (_iota((LANE, LANE), 0) // 64 == _iota((LANE, LANE), 1) // 64).astype(BF16)
        for cb in range(D // LANE):
            delta_ref[:, cb * LANE:(cb + 1) * LANE] = _pick(prod[:, cb * LANE:(cb + 1) * LANE], same_head)

    t = pl.BlockSpec((tm, D), lambda i: (i, 0))
    return pl.pallas_call(
        body, name="attn_gate_bwd", grid=(S // tm,),
        in_specs=[t, t, pl.BlockSpec((tm, D), lambda i: (i, OFF_G // D))],
        out_specs=[t, t, t],
        out_shape=[SDS((S, D), F32), SDS((S, D), F32), SDS((S, D), BF16)],
        compiler_params=_cp(("parallel",)),
    )(dmix, pre, proj)


def _attn_bwd(proj, do, lse, delta, d, acc, out_dtype):
    qb = ATTN_QB[d]
    nb, cur, prev, lag = _attn_specs(d, qb, True)
    has_acc = acc is not None

    def body(*refs):
        q_ref, kp_ref, kc_ref, vp_ref, vc_ref, do_ref, lse_ref, dl_ref = refs[:8]
        if has_acc:
            aq_ref, ak_ref, av_ref = refs[8:11]
        tail = refs[11:] if has_acc else refs[8:]
        dq_ref, dk_ref, dv_ref, ck_ref, cv_ref = tail[:5]
        dq_f32 = dq_ref if out_dtype == F32 else tail[5]
        i = pl.program_id(1)
        slot = i & 1
        now_k, now_v, old_k, old_v = ck_ref.at[slot], cv_ref.at[slot], ck_ref.at[1 - slot], cv_ref.at[1 - slot]
        lane_a = _iota((CH, LANE), 1) < 64
        mask_first, mask_rest = _attn_mask(i > 0), _attn_mask(True)

        @pl.when(i == 0)
        def _():
            ck_ref[1] = jnp.zeros((ck_ref.shape[1], LANE), F32)
            cv_ref[1] = jnp.zeros((cv_ref.shape[1], LANE), F32)

        def unit(r, u):
            sl = _unit_rows(r, u, d)
            q2 = _stack_heads(q_ref[sl, :] * 0.125, lane_a)
            do2 = _stack_heads(do_ref[sl, :], lane_a)
            k2, v2 = _unit_kv(kp_ref, kc_ref, r, u, d, qb), _unit_kv(vp_ref, vc_ref, r, u, d, qb)
            lsev, dlv = lse_ref[sl, :], dl_ref[sl, :]
            lse2 = jnp.concatenate([lsev[:, 0:1], lsev[:, 64:65]], axis=0)
            dl2 = jnp.concatenate([dlv[:, 0:1], dlv[:, 64:65]], axis=0)
            p = jnp.exp(jnp.where(mask_first if u == 0 else mask_rest, _dot_nt(q2, k2), NEG) - lse2)
            ds = (p * (_dot_nt(do2, v2) - dl2)).astype(BF16)
            dq2 = _dot(ds, k2)
            dk2 = _dot_tn(ds, q2)
            dv2 = _dot_tn(p.astype(BF16), do2)
            dq = jnp.where(lane_a, dq2[:CH], dq2[CH:]) * 0.125
            if has_acc:
                dq = dq + aq_ref[sl, :]
            dq_f32[sl, :] = dq
            if u == 0:
                before = _unit_rows(r, qb - 1, d)
                old_k[before, :] += dk2[:CH]
                old_v[before, :] += dv2[:CH]
            else:
                before = _unit_rows(r, u - 1, d)
                now_k[before, :] += dk2[:CH]
                now_v[before, :] += dv2[:CH]
            now_k[sl, :] = dk2[CH:]
            now_v[sl, :] = dv2[CH:]

        @pl.when(i < nb)
        def _():
            _for_units(d, qb, unit)
            if out_dtype != F32:
                dq_ref[...] = dq_f32[...].astype(out_dtype)

        dk, dv = old_k[...], old_v[...]
        if has_acc:
            dk, dv = dk + ak_ref[...], dv + av_ref[...]
        dk_ref[...] = dk.astype(out_dtype)
        dv_ref[...] = dv.astype(out_dtype)

    in_specs = [cur(0), prev(8), cur(8), prev(16), cur(16), cur(0), cur(0), cur(0)]
    args = [proj, proj, proj, proj, proj, do, lse, delta]
    if has_acc:
        in_specs += [cur(0), lag, lag]
        args += list(acc)
    rows = CH * d * qb
    return pl.pallas_call(
        body, name=f"attn_bwd_d{d}", grid=(NH // 2, nb + 1),
        in_specs=in_specs, out_specs=[cur(0), lag, lag],
        out_shape=[SDS((S, D), out_dtype)] * 3,
        scratch_shapes=[pltpu.VMEM((2, rows, LANE), F32), pltpu.VMEM((2, rows, LANE), F32)]
        + ([] if out_dtype == F32 else [pltpu.VMEM((rows, LANE), F32)]),
        compiler_params=_cp(("parallel", "arbitrary")),
    )(*args)


def _conv_taps(cur, prev8, first):
    row8 = _iota(prev8.shape, 0)
    prev8 = jnp.where(first, 0.0, prev8)
    taps = []
    for s in (3, 2, 1):
        rolled = pltpu.roll(cur, s, 0)
        head = jnp.where(row8 < s, pltpu.roll(prev8, s, 0), rolled[:8])
        taps.append(jnp.concatenate([head, rolled[8:]], axis=0))
    return taps + [cur]


def _conv(taps, w, b):
    acc = b + w[0:1, :] * taps[0]
    for k in (1, 2, 3):
        acc = acc + w[k:k + 1, :] * taps[k]
    return acc


def _expand():
    return (_iota((LANE, D), 1) // 64 == _iota((LANE, D), 0)).astype(BF16)


def _reduce():
    return (_iota((D, LANE), 0) // 64 == _iota((D, LANE), 1)).astype(BF16)


def _ssd_common(xs_raw, xs_prev, bc_raw, bc_prev, dt_raw, first, cw, cb, dtb, alog):
    head_lane = _iota((CH, LANE), 1) < NH
    xs_taps = _conv_taps(xs_raw, xs_prev, first)
    bc_taps = _conv_taps(bc_raw, bc_prev, first)
    xs_c = _conv(xs_taps, cw[:, :D], cb[:, :D])
    bc_c = _conv(bc_taps, cw[:, D:], cb[:, D:])
    xs = xs_c * _sigmoid(xs_c)
    bc = bc_c * _sigmoid(bc_c)
    pre = dt_raw + dtb
    dt = jnp.where(head_lane, jnp.maximum(pre, 0.0) + jnp.log(1.0 + jnp.exp(-jnp.abs(pre))), 0.0)
    a_row = jnp.where(head_lane[0:1], -jnp.exp(alog), 0.0)
    tri = (_iota((CH, CH), 1) <= _iota((CH, CH), 0)).astype(BF16)
    cs = _pick_left(tri, dt * a_row)
    cs_last = cs[CH - 1:CH, :]
    wide = _pick(jnp.concatenate([dt, jnp.exp(cs), jnp.exp(cs_last - cs)], axis=0), _expand())
    dt_b, e_b, f_b = wide[:CH], wide[CH:2 * CH], wide[2 * CH:]
    return dict(xs_taps=xs_taps, bc_taps=bc_taps, xs_c=xs_c, bc_c=bc_c, xs=xs, bc=bc, pre=pre, dt=dt,
                a_row=a_row, cs=cs, cs_t=cs.T, dt_b=dt_b, e_b=e_b, f_b=f_b, t_b=e_b[CH - 1:CH, :])


def _groups(bc):
    bcb = bc.astype(BF16)
    return [bcb[:, 0:128], bcb[:, 128:256]], [bcb[:, 256:384], bcb[:, 384:512]]


def _decay(q, h, tril):
    seg = q["cs"][:, h:h + 1] - q["cs_t"][h:h + 1, :]
    return jnp.exp(jnp.where(tril, seg, NEG))


def _ssm_fwd(proj, mix, cw, cb, dtb, alog, d_b, nw):
    def body(xs_ref, xsp_ref, bc_ref, bcp_ref, dt_ref, z_ref, cw_ref, cb_ref, dtb_ref, alog_ref, db_ref, nw_ref,
             mix_in_ref, mix_ref, y_ref, st_ref, h_ref):
        del mix_in_ref
        i = pl.program_id(0)

        @pl.when(i == 0)
        def _():
            h_ref[...] = jnp.zeros_like(h_ref)

        q = _ssd_common(xs_ref[...], xsp_ref[...], bc_ref[...], bcp_ref[...], dt_ref[...], i == 0,
                        cw_ref[...], cb_ref[...], dtb_ref[...], alog_ref[...])
        bg, cg = _groups(q["bc"])
        xs = q["xs"]
        xdt = xs * q["dt_b"]
        xdt_b = xdt.astype(BF16)
        h_in = h_ref[...]
        st_ref[...] = h_in
        hb = h_in.astype(BF16)
        tril = _iota((CH, CH), 1) <= _iota((CH, CH), 0)
        lane_a = _iota((CH, LANE), 1) < 64
        cbm = [_dot_nt(cg[g], bg[g]) for g in range(2)]
        pairs = []
        for hp in range(NH // 2):
            xp = xdt_b[:, hp * LANE:(hp + 1) * LANE]
            ya = _dot((cbm[hp // 4] * _decay(q, 2 * hp, tril)).astype(BF16), xp)
            yb = _dot((cbm[hp // 4] * _decay(q, 2 * hp + 1, tril)).astype(BF16), xp)
            pairs.append(jnp.where(lane_a, ya, yb))
        y_diag = jnp.concatenate(pairs, axis=1)
        y_off = jnp.concatenate([_dot(cg[g], hb[:, g * 512:(g + 1) * 512]) for g in range(2)], axis=1) * q["e_b"]
        y = y_diag + y_off + db_ref[...] * xs
        y_ref[...] = y
        xf = (xdt * q["f_b"]).astype(BF16)
        h_ref[...] = q["t_b"] * h_in + jnp.concatenate(
            [_dot_tn(bg[g], xf[:, g * 512:(g + 1) * 512]) for g in range(2)], axis=1)
        z = z_ref[...]
        yz = y * (z * _sigmoid(z))
        outs = []
        for g in range(2):
            v = yz[:, g * 512:(g + 1) * 512]
            outs.append(v * lax.rsqrt(jnp.mean(v * v, axis=-1, keepdims=True) + EPS))
        mix_ref[...] = (jnp.concatenate(outs, axis=1) * nw_ref[...]).astype(BF16)

    def col(width, blk, prev=False):
        if prev:
            return pl.BlockSpec((8, width), lambda i: (jnp.maximum(i * (CH // 8) - 1, 0), blk))
        return pl.BlockSpec((CH, width), lambda i: (i, blk))

    def full(a):
        return pl.BlockSpec(a.shape, lambda i: (0,) * a.ndim)

    return pl.pallas_call(
        body, name="ssm_fwd", grid=(NC,),
        in_specs=[col(D, 5), col(D, 5, True), col(512, 12), col(512, 12, True), col(LANE, 52), col(D, 4),
                  full(cw), full(cb), full(dtb), full(alog), full(d_b), full(nw), ANY],
        out_specs=[col(D, 1), col(D, 0), pl.BlockSpec((None, CH, D), lambda i: (i, 0, 0))],
        out_shape=[SDS((S, 2 * D), BF16), SDS((S, D), F32), SDS((NC, CH, D), F32)],
        scratch_shapes=[pltpu.VMEM((CH, D), F32)],
        input_output_aliases={12: 0},
        compiler_params=_cp(("arbitrary",)),
    )(proj, proj, proj, proj, proj, proj, cw, cb, dtb, alog, d_b, nw, mix)


def _ssm_bwd(proj, dmix, y_save, states, cw, cb, dtb, alog, d_b, nw):
    def body(xs_ref, xsp_ref, bc_ref, bcp_ref, dt_ref, z_ref, dn_ref, y_ref, st_ref,
             cw_ref, cb_ref, dtb_ref, alog_ref, db_ref, nw_ref,
             dz_ref, dx_ref, dcw_ref, dcb_ref, dsm_ref, dnw_ref, dh_ref, nxs_ref, nbc_ref):
        i = pl.program_id(0)
        ci = NC - 1 - i

        @pl.when(i == 0)
        def _():
            for ref in (dcw_ref, dcb_ref, dsm_ref, dnw_ref, dh_ref, nxs_ref, nbc_ref):
                ref[...] = jnp.zeros_like(ref)

        cw, cb = cw_ref[...], cb_ref[...]
        q = _ssd_common(xs_ref[...], xsp_ref[...], bc_ref[...], bcp_ref[...], dt_ref[...], ci == 0,
                        cw, cb, dtb_ref[...], alog_ref[...])
        bg, cg = _groups(q["bc"])
        xs, dt_b, e_b, f_b, t_b = q["xs"], q["dt_b"], q["e_b"], q["f_b"], q["t_b"]
        xdt = xs * dt_b
        xdt_b = xdt.astype(BF16)
        h_in = st_ref[...]
        hb = h_in.astype(BF16)
        dh_new = dh_ref[...]
        dhb = dh_new.astype(BF16)
        red = _reduce()

        z, y, dn, nw_v = z_ref[...], y_ref[...], dn_ref[...], nw_ref[...]
        sig = _sigmoid(z)
        sz = z * sig
        yz = y * sz
        gdn = dn * nw_v
        dyz, dnw = [], []
        for g in range(2):
            v, gv = yz[:, g * 512:(g + 1) * 512], gdn[:, g * 512:(g + 1) * 512]
            r = lax.rsqrt(jnp.mean(v * v, axis=-1, keepdims=True) + EPS)
            dnw.append(dn[:, g * 512:(g + 1) * 512] * v * r)
            dyz.append(r * (gv - v * (r * r) * jnp.mean(gv * v, axis=-1, keepdims=True)))
        dyz = jnp.concatenate(dyz, axis=1)
        dnw_ref[...] += jnp.sum(jnp.concatenate(dnw, axis=1), axis=0, keepdims=True)
        dy = dyz * sz
        dz_ref[...] = (dyz * y * (sig * (1.0 + z * (1.0 - sig)))).astype(BF16)
        dy_b = dy.astype(BF16)

        tril = _iota((CH, CH), 1) <= _iota((CH, CH), 0)
        lane_a = _iota((CH, LANE), 1) < 64
        cbm = [_dot_nt(cg[g], bg[g]) for g in range(2)]
        dcbm = [jnp.zeros((CH, CH), F32), jnp.zeros((CH, CH), F32)]
        seg_rows = jnp.zeros((CH, LANE), F32)
        seg_cols = jnp.zeros((LANE, CH), F32)
        row_id, col_id = _iota((CH, LANE), 0), _iota((CH, LANE), 1)
        dx_pairs = []
        for hp in range(NH // 2):
            g = hp // 4
            xp = xdt_b[:, hp * LANE:(hp + 1) * LANE]
            dyp_f = dy[:, hp * LANE:(hp + 1) * LANE]
            dyp = dy_b[:, hp * LANE:(hp + 1) * LANE]
            halves = []
            for k in range(2):
                h = 2 * hp + k
                lane = lane_a if k == 0 else jnp.logical_not(lane_a)
                dec = _decay(q, h, tril)
                gm = cbm[g] * dec
                dgm = _dot_nt(jnp.where(lane, dyp_f, 0.0).astype(BF16), xp)
                dcbm[g] = dcbm[g] + dgm * dec
                prod = dgm * gm
                seg_rows = jnp.where(col_id == h, jnp.sum(prod, axis=1, keepdims=True), seg_rows)
                seg_cols = jnp.where(row_id == h, jnp.sum(prod, axis=0, keepdims=True), seg_cols)
                halves.append(_dot_tn(gm.astype(BF16), dyp))
            dx_pairs.append(jnp.where(lane_a, halves[0], halves[1]))
        dxdt_diag = jnp.concatenate(dx_pairs, axis=1)

        qv = jnp.concatenate([_dot(bg[g], dhb[:, g * 512:(g + 1) * 512]) for g in range(2)], axis=1)
        y_off = jnp.concatenate([_dot(cg[g], hb[:, g * 512:(g + 1) * 512]) for g in range(2)], axis=1) * e_b
        xfq = xdt * f_b * qv
        dxdt = dxdt_diag + f_b * qv
        tdt = jnp.sum(dh_new * h_in, axis=0, keepdims=True) * t_b
        per_head = _pick(jnp.concatenate([xfq, dy * y_off, dxdt * xs, dy * xs, jnp.broadcast_to(tdt, (8, D))],
                                         axis=0), red)
        fdf, dyoff_h, dxdtxs_h, dyxs_h = [per_head[k * CH:(k + 1) * CH] for k in range(4)]
        dcs = seg_rows - seg_cols.T + dyoff_h - fdf
        last = per_head[4 * CH:4 * CH + 1] + jnp.sum(fdf, axis=0, keepdims=True)
        dcs = dcs + jnp.where(_iota((CH, LANE), 0) == CH - 1, last, 0.0)
        tri_t = (_iota((CH, CH), 1) >= _iota((CH, CH), 0)).astype(BF16)
        da = _pick_left(tri_t, dcs)
        ddt = da * q["a_row"] + dxdtxs_h
        dxs = dxdt * dt_b + db_ref[...] * dy
        ddt_raw = ddt * _sigmoid(q["pre"])
        dsm_ref[0:1, :] += jnp.sum(ddt_raw, axis=0, keepdims=True)
        dsm_ref[1:2, :] += jnp.sum(da * q["dt"], axis=0, keepdims=True) * q["a_row"]
        dsm_ref[2:3, :] += jnp.sum(dyxs_h, axis=0, keepdims=True)
        edy = (e_b * dy).astype(BF16)
        xf = (xdt * f_b).astype(BF16)
        dbs, dcs_g, dhs = [], [], []
        for g in range(2):
            sl = slice(g * 512, (g + 1) * 512)
            dcb_b = dcbm[g].astype(BF16)
            dcs_g.append(_dot(dcb_b, bg[g]) + _dot_nt(edy[:, sl], hb[:, sl]))
            dbs.append(_dot_tn(dcb_b, cg[g]) + _dot_nt(xf[:, sl], dhb[:, sl]))
            dhs.append(_dot_tn(cg[g], edy[:, sl]))
        dh_ref[...] = t_b * dh_new + jnp.concatenate(dhs, axis=1)
        dbc = jnp.concatenate(dbs + dcs_g, axis=1)

        def conv_bwd(dact, pre, taps, w, nxt_ref, lo):
            s = _sigmoid(pre)
            dconv = dact * (s * (1.0 + pre * (1.0 - s)))
            nxt8 = nxt_ref[...]
            row8 = _iota(nxt8.shape, 0)
            hi = lo + dconv.shape[1]
            dcb_ref[:, lo:hi] += jnp.sum(dconv, axis=0, keepdims=True)
            dx = w[3:4, :] * dconv
            for k in range(4):
                dcw_ref[k:k + 1, lo:hi] += jnp.sum(dconv * taps[k], axis=0, keepdims=True)
            for s_ in (1, 2, 3):
                rolled = pltpu.roll(dconv, CH - s_, 0)
                tail = jnp.where(row8 >= 8 - s_, pltpu.roll(nxt8, 8 - s_, 0), rolled[CH - 8:])
                dx = dx + w[3 - s_:4 - s_, :] * jnp.concatenate([rolled[:CH - 8], tail], axis=0)
            nxt_ref[...] = dconv[:8]
            return dx

        dx_ref[:, 0:D] = conv_bwd(dxs, q["xs_c"], q["xs_taps"], cw[:, :D], nxs_ref, 0).astype(BF16)
        dx_ref[:, D:D + 512] = conv_bwd(dbc, q["bc_c"], q["bc_taps"], cw[:, D:], nbc_ref, D).astype(BF16)
        dx_ref[:, D + 512:D + 640] = ddt_raw.astype(BF16)
        dx_ref[:, D + 640:] = jnp.zeros((CH, D - 640), BF16)

    def col(width, blk, prev=False):
        if prev:
            return pl.BlockSpec((8, width), lambda i: (jnp.maximum((NC - 1 - i) * (CH // 8) - 1, 0), blk))
        return pl.BlockSpec((CH, width), lambda i: (NC - 1 - i, blk))

    def full(a):
        return pl.BlockSpec(a.shape, lambda i: (0,) * len(a.shape))

    acc_shapes = [SDS((4, 1536), F32), SDS((1, 1536), F32), SDS((8, LANE), F32), SDS((1, D), F32)]
    return pl.pallas_call(
        body, name="ssm_bwd", grid=(NC,),
        in_specs=[col(D, 5), col(D, 5, True), col(512, 12), col(512, 12, True), col(LANE, 52), col(D, 4),
                  col(D, 1), col(D, 0), pl.BlockSpec((None, CH, D), lambda i: (NC - 1 - i, 0, 0)),
                  full(cw), full(cb), full(dtb), full(alog), full(d_b), full(nw)],
        out_specs=[col(D, 0), col(2 * D, 0)] + [full(a) for a in acc_shapes],
        out_shape=[SDS((S, D), BF16), SDS((S, 2 * D), BF16)] + acc_shapes,
        scratch_shapes=[pltpu.VMEM((CH, D), F32), pltpu.VMEM((8, D), F32), pltpu.VMEM((8, 512), F32)],
        compiler_params=_cp(("arbitrary",)),
    )(proj, proj, proj, proj, proj, proj, dmix, y_save, states, cw, cb, dtb, alog, d_b, nw)


def _outproj_loss(mix, w_out, x, tgt, nw):
    tm = 256

    def body(mix_ref, w_ref, x_ref, t_ref, nw_ref, dy_ref, dmix_ref, dw_ref, dnw_ref, loss_ref):
        @pl.when(pl.program_id(0) == 0)
        def _():
            dw_ref[...] = jnp.zeros_like(dw_ref)
            dnw_ref[...] = jnp.zeros_like(dnw_ref)
            loss_ref[...] = jnp.zeros_like(loss_ref)

        mixv, w = mix_ref[...], w_ref[...]
        out = _dot(mixv, w)
        r = lax.rsqrt(jnp.mean(out * out, axis=-1, keepdims=True) + EPS)
        nh = out * r
        nw_v = nw_ref[...]
        err = x_ref[...] + nh * nw_v - t_ref[...]
        loss_ref[...] += 0.5 * jnp.sum(jnp.mean(err * err, axis=-1, keepdims=True), axis=0, keepdims=True)
        dy = err * (1.0 / D)
        dy_ref[...] = dy
        dnw_ref[...] += jnp.sum(dy * nh, axis=0, keepdims=True)
        gdn = dy * nw_v
        dout = (r * (gdn - nh * jnp.mean(gdn * nh, axis=-1, keepdims=True))).astype(BF16)
        dmix_ref[...] = _dot_nt(dout, w)
        dw_ref[...] += _dot_tn(mixv, dout)

    row = lambda w: pl.BlockSpec((tm, w), lambda i: (i, 0))
    full = lambda s: pl.BlockSpec(s, lambda i: (0, 0))
    return pl.pallas_call(
        body, name="outproj_loss", grid=(S // tm,),
        in_specs=[row(2 * D), full((2 * D, D)), row(D), row(D), full((1, D))],
        out_specs=[row(D), row(2 * D), full((2 * D, D)), full((1, D)), full((1, LANE))],
        out_shape=[SDS((S, D), F32), SDS((S, 2 * D), F32), SDS((2 * D, D), F32), SDS((1, D), F32),
                   SDS((1, LANE), F32)],
        compiler_params=_cp(("arbitrary",)),
    )(mix, w_out, x, tgt, nw)


def _inproj_bwd_dx(srcs, dxbcdt, w_all, x, dy, nw):
    tm = 512
    nk = DP // D

    def body(*refs):
        src_refs = refs[:nk]
        w_ref, x_ref, dy_ref, nw_ref, gx_ref, dnw_ref, acc_ref = refs[nk:]
        i, kk = pl.program_id(0), pl.program_id(1)

        @pl.when((i == 0) & (kk == 0))
        def _():
            dnw_ref[...] = jnp.zeros_like(dnw_ref)

        @pl.when(kk == 0)
        def _():
            acc_ref[...] = jnp.zeros_like(acc_ref)

        for s, ref in enumerate(src_refs):
            @pl.when(kk == s)
            def _(ref=ref):
                acc_ref[...] += _dot_nt(ref[...], w_ref[...])

        @pl.when(kk == nk - 1)
        def _():
            xf, du, nw_v = x_ref[...], acc_ref[...], nw_ref[...]
            r = lax.rsqrt(jnp.mean(xf * xf, axis=-1, keepdims=True) + EPS)
            xh = xf * r
            dnw_ref[...] += jnp.sum(du * xh, axis=0, keepdims=True)
            gdu = du * nw_v
            gx_ref[...] = r * (gdu - xh * jnp.mean(gdu * xh, axis=-1, keepdims=True)) + dy_ref[...]

    row = pl.BlockSpec((tm, D), lambda i, k: (i, 0))
    row1 = pl.BlockSpec((tm, D), lambda i, k: (i, 1))
    one = pl.BlockSpec((1, D), lambda i, k: (0, 0))
    return pl.pallas_call(
        body, name="inproj_bwd_dx", grid=(S // tm, nk),
        in_specs=[row] * len(srcs) + [row, row1, pl.BlockSpec((D, D), lambda i, k: (0, k)), row, row, one],
        out_specs=[row, one],
        out_shape=[SDS((S, D), F32), SDS((1, D), F32)],
        scratch_shapes=[pltpu.VMEM((tm, D), F32)],
        compiler_params=_cp(("arbitrary", "arbitrary")),
    )(*srcs, dxbcdt, dxbcdt, w_all, x, dy, nw)


def _dw(u, dsec, name):
    ts = 512
    ncol = dsec.shape[1] // D

    def body(u_ref, d_ref, o_ref):
        @pl.when(pl.program_id(1) == 0)
        def _():
            o_ref[...] = jnp.zeros_like(o_ref)

        o_ref[...] += _dot_tn(u_ref[...], d_ref[...])

    return pl.pallas_call(
        body, name=name, grid=(ncol, S // ts),
        in_specs=[pl.BlockSpec((ts, D), lambda j, i: (i, 0)), pl.BlockSpec((ts, D), lambda j, i: (i, j))],
        out_specs=pl.BlockSpec((D, D), lambda j, i: (0, j)),
        out_shape=SDS((D, ncol * D), F32),
        compiler_params=_cp(("parallel", "arbitrary")),
    )(u, dsec)


def _place():
    x, y, c = lax.axis_index("x"), lax.axis_index("y"), lax.axis_index("c")
    return x, y, c, 2 * x + y


def _chip_of(x, y, k):
    px = 1 - x if k & 2 else x
    py = 1 - y if k & 1 else y
    return px, py, 2 * px + py


def _remote(src, dst, send_sem, recv_sem, dev):
    return pltpu.make_async_remote_copy(src_ref=src, dst_ref=dst, send_sem=send_sem, recv_sem=recv_sem,
                                        device_id=dev, device_id_type=MESH)


def _gather_weights(w_in_b, w_out_b, conv_w):
    hin, hout = D // 2, w_out_b.shape[0] // 2

    def body(win_ref, wout_ref, cw_ref, gin_ref, gout_ref, gcw_ref, send, recv, fsend, frecv):
        x, y, c, j = _place()
        sib = (x, y, 1 - c)
        rin, rout = pl.ds(c * hin, hin), pl.ds(c * hout, hout)
        sin, sout = pl.ds((1 - c) * hin, hin), pl.ds((1 - c) * hout, hout)
        sends = []
        for k in (1, 2, 3):
            px, py, _ = _chip_of(x, y, k)
            dev = (px, py, c)
            sends += [_remote(win_ref.at[rin], gin_ref.at[j, rin], send.at[k - 1], recv.at[k - 1], dev),
                      _remote(wout_ref.at[rout], gout_ref.at[j, rout], send.at[k + 2], recv.at[k + 2], dev),
                      _remote(cw_ref, gcw_ref.at[j], send.at[k + 5], recv.at[k + 5], dev)]
        for cp in sends:
            cp.start()
        fwd = []
        for k in (1, 2, 3):
            _, _, pj = _chip_of(x, y, k)
            _remote(win_ref.at[rin], gin_ref.at[pj, rin], send.at[k - 1], recv.at[k - 1], sib).wait_recv()
            f_in = _remote(gin_ref.at[pj, rin], gin_ref.at[pj, rin], fsend.at[k - 1], frecv.at[k - 1], sib)
            f_in.start()
            _remote(wout_ref.at[rout], gout_ref.at[pj, rout], send.at[k + 2], recv.at[k + 2], sib).wait_recv()
            f_out = _remote(gout_ref.at[pj, rout], gout_ref.at[pj, rout], fsend.at[k + 2], frecv.at[k + 2], sib)
            f_out.start()
            fwd += [f_in, f_out]
        for k in (1, 2, 3):
            _, _, pj = _chip_of(x, y, k)
            _remote(win_ref.at[sin], gin_ref.at[pj, sin], fsend.at[k - 1], frecv.at[k - 1], sib).wait_recv()
            _remote(wout_ref.at[sout], gout_ref.at[pj, sout], fsend.at[k + 2], frecv.at[k + 2], sib).wait_recv()
            _remote(cw_ref, gcw_ref.at[pj], send.at[k + 5], recv.at[k + 5], sib).wait_recv()
        for cp in sends + fwd:
            cp.wait_send()

    return pl.pallas_call(
        body, name="gather_weights",
        in_specs=[ANY, ANY, ANY], out_specs=[ANY, ANY, ANY],
        out_shape=[SDS((4,) + w_in_b.shape, BF16), SDS((4,) + w_out_b.shape, BF16), SDS((4,) + conv_w.shape, F32)],
        scratch_shapes=[pltpu.SemaphoreType.DMA((9,)), pltpu.SemaphoreType.DMA((9,)),
                        pltpu.SemaphoreType.DMA((6,)), pltpu.SemaphoreType.DMA((6,))],
        compiler_params=pltpu.CompilerParams(has_side_effects=True),
    )(w_in_b, w_out_b, conv_w)


def _pair_exchange(gw, go):
    hw, ho = gw.shape[1] // 2, go.shape[1] // 2

    def body(gw_ref, go_ref, rw_ref, ro_ref, send, recv):
        x, y, c, _ = _place()
        sib = (x, y, 1 - c)
        theirs_w, theirs_o = pl.ds((1 - c) * hw, hw), pl.ds((1 - c) * ho, ho)
        cps = [_remote(gw_ref.at[:, theirs_w], rw_ref, send.at[0], recv.at[0], sib),
               _remote(go_ref.at[:, theirs_o], ro_ref, send.at[1], recv.at[1], sib)]
        for cp in cps:
            cp.start()
        for cp in cps:
            cp.wait()

    return pl.pallas_call(
        body, name="pair_exchange", in_specs=[ANY, ANY], out_specs=[ANY, ANY],
        out_shape=[SDS((4, hw, gw.shape[2]), F32), SDS((4, ho, go.shape[2]), F32)],
        scratch_shapes=[pltpu.SemaphoreType.DMA((2,)), pltpu.SemaphoreType.DMA((2,))],
        compiler_params=pltpu.CompilerParams(has_side_effects=True),
    )(gw, go)


def _pair_sum(cidx, g, r, name):
    half, width = r.shape[1], r.shape[2]
    tr = min(half, 256)
    nt = half // tr

    def body(c_ref, g_ref, r_ref, o_ref):
        del c_ref
        o_ref[...] = (g_ref[...] + r_ref[...]).astype(BF16)

    return pl.pallas_call(
        body, name=name,
        grid_spec=pltpu.PrefetchScalarGridSpec(
            num_scalar_prefetch=1, grid=(4, nt),
            in_specs=[pl.BlockSpec((None, tr, width), lambda s, t, c: (s, c[0] * nt + t, 0)),
                      pl.BlockSpec((None, tr, width), lambda s, t, c: (s, t, 0))],
            out_specs=pl.BlockSpec((None, tr, width), lambda s, t, c: (s, t, 0))),
        out_shape=SDS(r.shape, BF16),
        compiler_params=_cp(("parallel", "parallel")),
    )(cidx, g, r)


def _chip_exchange(pw, po, small):
    def body(pw_ref, po_ref, sm_ref, rw_ref, ro_ref, rs_ref, send, recv, ssend, srecv, lsem):
        x, y, c, j = _place()
        me = 2 * j + c
        local = pltpu.make_async_copy(sm_ref, rs_ref.at[me], lsem)
        local.start()
        cps = []
        for k in (1, 2, 3):
            px, py, pj = _chip_of(x, y, k)
            dev = (px, py, c)
            cps += [_remote(pw_ref.at[pj], rw_ref.at[j], send.at[k - 1], recv.at[k - 1], dev),
                    _remote(po_ref.at[pj], ro_ref.at[j], send.at[k + 2], recv.at[k + 2], dev)]
        for k in range(1, 8):
            px, py, _ = _chip_of(x, y, k >> 1)
            pc = 1 - c if k & 1 else c
            cps.append(_remote(sm_ref, rs_ref.at[me], ssend.at[k - 1], srecv.at[k - 1], (px, py, pc)))
        for cp in cps:
            cp.start()
        for k in (1, 2, 3):
            _, _, pj = _chip_of(x, y, k)
            _remote(pw_ref.at[pj], rw_ref.at[pj], send.at[k - 1], recv.at[k - 1], (x, y, c)).wait_recv()
            _remote(po_ref.at[pj], ro_ref.at[pj], send.at[k + 2], recv.at[k + 2], (x, y, c)).wait_recv()
        for k in range(1, 8):
            _, _, pj = _chip_of(x, y, k >> 1)
            pc = 1 - c if k & 1 else c
            _remote(sm_ref, rs_ref.at[2 * pj + pc], ssend.at[k - 1], srecv.at[k - 1], (x, y, c)).wait_recv()
        for cp in cps:
            cp.wait_send()
        local.wait()

    return pl.pallas_call(
        body, name="chip_exchange", in_specs=[ANY, ANY, ANY], out_specs=[ANY, ANY, ANY],
        out_shape=[SDS(pw.shape, BF16), SDS(po.shape, BF16), SDS((8,) + small.shape, F32)],
        scratch_shapes=[pltpu.SemaphoreType.DMA((6,)), pltpu.SemaphoreType.DMA((6,)),
                        pltpu.SemaphoreType.DMA((7,)), pltpu.SemaphoreType.DMA((7,)), pltpu.SemaphoreType.DMA],
        compiler_params=pltpu.CompilerParams(has_side_effects=True),
    )(pw, po, small)


def _slot_sum(r, name):
    n, rows, width = r.shape
    tr = min(rows, 256)

    def body(r_ref, o_ref):
        acc = r_ref[0].astype(F32)
        for s in range(1, n):
            acc = acc + r_ref[s].astype(F32)
        o_ref[...] = acc

    return pl.pallas_call(
        body, name=name, grid=(rows // tr,),
        in_specs=[pl.BlockSpec((n, tr, width), lambda t: (0, t, 0))],
        out_specs=pl.BlockSpec((tr, width), lambda t: (t, 0)),
        out_shape=SDS((rows, width), F32),
        compiler_params=_cp(("parallel",)),
    )(r)


def _chip_sum(chip_idx, recv, own, name):
    n, rows, width = recv.shape
    tr = min(rows, 256)

    def body(j_ref, r_ref, own_ref, o_ref):
        acc = None
        for s in range(n):
            term = jnp.where(j_ref[0] == s, own_ref[...], r_ref[s]).astype(F32)
            acc = term if acc is None else acc + term
        o_ref[...] = acc

    return pl.pallas_call(
        body, name=name,
        grid_spec=pltpu.PrefetchScalarGridSpec(
            num_scalar_prefetch=1, grid=(rows // tr,),
            in_specs=[pl.BlockSpec((n, tr, width), lambda t, j: (0, t, 0)),
                      pl.BlockSpec((None, tr, width), lambda t, j: (j[0], t, 0))],
            out_specs=pl.BlockSpec((tr, width), lambda t, j: (t, 0))),
        out_shape=SDS((rows, width), F32),
        compiler_params=_cp(("parallel",)),
    )(chip_idx, recv, own)


def _half_exchange(hw, ho):
    def body(hw_ref, ho_ref, tw_ref, to_ref, send, recv):
        x, y, c, _ = _place()
        sib = (x, y, 1 - c)
        cps = [_remote(hw_ref, tw_ref, send.at[0], recv.at[0], sib),
               _remote(ho_ref, to_ref, send.at[1], recv.at[1], sib)]
        for cp in cps:
            cp.start()
        for cp in cps:
            cp.wait()

    return pl.pallas_call(
        body, name="half_exchange", in_specs=[ANY, ANY], out_specs=[ANY, ANY],
        out_shape=[SDS(hw.shape, F32), SDS(ho.shape, F32)],
        scratch_shapes=[pltpu.SemaphoreType.DMA((2,)), pltpu.SemaphoreType.DMA((2,))],
        compiler_params=pltpu.CompilerParams(has_side_effects=True),
    )(hw, ho)


def _by_core(c, mine, theirs):
    return jnp.where(c == 0, jnp.concatenate([mine, theirs], axis=0), jnp.concatenate([theirs, mine], axis=0))


def _adamw(w, g, m, v, name):
    rows, width = w.shape
    tr = min(rows, 256)

    def body(w_ref, g_ref, m_ref, v_ref, d_ref, nm_ref, nv_ref):
        gv = g_ref[...]
        nm = ADAM_B1 * m_ref[...] + (1.0 - ADAM_B1) * gv
        nv = ADAM_B2 * v_ref[...] + (1.0 - ADAM_B2) * (gv * gv)
        m_hat = nm / (1.0 - ADAM_B1 ** ADAM_STEP)
        v_hat = nv / (1.0 - ADAM_B2 ** ADAM_STEP)
        d_ref[...] = -ADAM_LR * (m_hat / (jnp.sqrt(v_hat) + ADAM_EPS) + ADAM_WD * w_ref[...])
        nm_ref[...] = nm
        nv_ref[...] = nv

    t = pl.BlockSpec((tr, width), lambda i: (i, 0))
    return pl.pallas_call(
        body, name=name, grid=(rows // tr,), in_specs=[t] * 4, out_specs=[t] * 3,
        out_shape=[SDS(w.shape, F32)] * 3, compiler_params=_cp(("parallel",)),
    )(w, g, m, v)


def _rows128(a, rows):
    flat = a.reshape(-1)
    return jnp.pad(flat, (0, rows * LANE - flat.shape[0])).reshape(rows, LANE)


def _pack_small(conv_w, norm_pre, conv_b, ssm_norm, norm_post, dtb, alog, dsk):
    cw_rows = 48 if conv_w.shape[-1] == 1536 else 16
    vec = jnp.concatenate([_rows128(dtb, 1), _rows128(alog, 1), _rows128(dsk, 1), jnp.zeros((5, LANE), F32)], axis=0)
    return jnp.concatenate([_rows128(conv_w, cw_rows), _rows128(norm_pre, 8), _rows128(conv_b, 16),
                            _rows128(ssm_norm, 8), _rows128(norm_post, 8), vec], axis=0)


def _unpack_small(p, cw_cols):
    cw_rows = 48 if cw_cols == 1536 else 16
    o = cw_rows
    conv_w = p[:cw_rows].reshape(-1)[:4 * cw_cols].reshape(1, 4, cw_cols)
    norm_pre = p[o:o + 8].reshape(1, D)
    conv_b = p[o + 8:o + 24].reshape(-1)[:1536].reshape(1, 1536)
    ssm_norm = p[o + 24:o + 32].reshape(1, D)
    norm_post = p[o + 32:o + 40].reshape(1, D)
    vec = p[o + 40:o + 48]
    return conv_w, norm_pre, conv_b, ssm_norm, norm_post, vec[0:1, :NH], vec[1:2, :NH], vec[2:3, :NH]


def _pad_lanes(a):
    return jnp.pad(a, ((0, 0), (0, LANE - a.shape[1])))


def kernel(x, norm_pre_w, w_in, conv_w, conv_b, dt_bias, a_log, d_skip, ssm_norm_w, w_out, norm_post_w, loss_target, m_norm_pre_w, m_w_in, m_conv_w, m_conv_b, m_dt_bias, m_a_log, m_d_skip, m_ssm_norm_w, m_w_out, m_norm_post_w, v_norm_pre_w, v_w_in, v_conv_w, v_conv_b, v_dt_bias, v_a_log, v_d_skip, v_ssm_norm_w, v_w_out, v_norm_post_w):
    xi, yi, ci = lax.axis_index("x"), lax.axis_index("y"), lax.axis_index("c")
    chip = 2 * xi + yi
    x2, tgt = x[0], loss_target[0]

    w_in_b, w_out_b = w_in[0].astype(BF16), w_out[0].astype(BF16)
    gin, gout, gcw = _gather_weights(w_in_b, w_out_b, conv_w[0])

    def whole(own, gathered, axis):
        return jnp.concatenate([jnp.where(chip == k, own, gathered[k]) for k in range(4)], axis=axis)

    w_all = jnp.concatenate([whole(w_in_b, gin, 1), jnp.zeros((D, DP - 4 * SHARD), BF16)], axis=1)
    w_out_all = whole(w_out_b, gout, 0)
    cw_all = whole(conv_w[0], gcw, 1)
    loss_part, grad_x, dw_all, dw_out, small = _local_step(
        x2, tgt, w_all, w_out_all, cw_all, norm_pre_w, conv_b, dt_bias, a_log, d_skip, ssm_norm_w, norm_post_w)
    gw = jnp.stack([dw_all[:, k * SHARD:(k + 1) * SHARD] for k in range(4)])
    go = dw_out.reshape(4, D // 2, D)

    cidx = jnp.reshape(ci, (1,)).astype(jnp.int32)
    rw, ro = _pair_exchange(gw, go)
    pw, po = _pair_sum(cidx, gw, rw, "pair_sum_in"), _pair_sum(cidx, go, ro, "pair_sum_out")
    cw_r, co_r, sm_r = _chip_exchange(pw, po, small)
    chip_idx = jnp.reshape(chip, (1,)).astype(jnp.int32)
    half_in, half_out = _chip_sum(chip_idx, cw_r, pw, "chip_sum_in"), _chip_sum(chip_idx, co_r, po, "chip_sum_out")
    their_in, their_out = _half_exchange(half_in, half_out)
    g_in, g_out = _by_core(ci, half_in, their_in), _by_core(ci, half_out, their_out)
    g_small = _slot_sum(sm_r, "small_sum")
    g_cw, g_npre, g_cb, g_nssm, g_npost, g_dtb, g_alog, g_dsk = _unpack_small(g_small, 1536)
    g_cw = lax.dynamic_slice_in_dim(g_cw, chip * 384, 384, axis=2)

    d_in, nm_in, nv_in = _adamw(w_in[0], g_in, m_w_in[0], v_w_in[0], "adamw_in")
    d_out, nm_out, nv_out = _adamw(w_out[0], g_out, m_w_out[0], v_w_out[0], "adamw_out")
    packed = [_pack_small(*t) for t in (
        (conv_w, norm_pre_w, conv_b, ssm_norm_w, norm_post_w, dt_bias, a_log, d_skip),
        (g_cw, g_npre, g_cb, g_nssm, g_npost, g_dtb, g_alog, g_dsk),
        (m_conv_w, m_norm_pre_w, m_conv_b, m_ssm_norm_w, m_norm_post_w, m_dt_bias, m_a_log, m_d_skip),
        (v_conv_w, v_norm_pre_w, v_conv_b, v_ssm_norm_w, v_norm_post_w, v_dt_bias, v_a_log, v_d_skip))]
    small_out = [_unpack_small(p, 384) for p in _adamw(*packed, "adamw_small")]

    loss = lax.psum(loss_part[0, 0], ("x", "y", "c"))

    def ordered(cw_, npre, cb_, nssm, npost, dtb_, alog_, dsk_, big_in, big_out):
        return [npre, big_in[None], cw_, cb_, dtb_, alog_, dsk_, nssm, big_out[None], npost]

    grads = ordered(g_cw, g_npre, g_cb, g_nssm, g_npost, g_dtb, g_alog, g_dsk, g_in, g_out)
    deltas = ordered(*small_out[0], d_in, d_out)
    new_m = ordered(*small_out[1], nm_in, nm_out)
    new_v = ordered(*small_out[2], nv_in, nv_out)
    return (loss, grad_x[None], *grads, *deltas, *new_m, *new_v)


def _local_step(x2, tgt, w_all, w_out_all, cw_all, norm_pre_w, conv_b, dt_bias, a_log, d_skip, ssm_norm_w,
                norm_post_w):
    dtb, alog = _pad_lanes(dt_bias), _pad_lanes(a_log)
    d_b = jnp.repeat(d_skip, 64, axis=1)

    proj, u = _inproj_fwd(x2, norm_pre_w, w_all)
    o1, l1 = _attn_fwd(proj, 1)
    o2, l2 = _attn_fwd(proj, 4)
    o3, l3 = _attn_fwd(proj, 16)
    mix, attn_pre, lse = _attn_merge(o1, o2, o3, l1, l2, l3, proj)
    mix, y_save, states = _ssm_fwd(proj, mix, cw_all, conv_b, dtb, alog, d_b, ssm_norm_w)

    dy, dmix, dw_out, dnw_post, loss_part = _outproj_loss(mix, w_out_all, x2, tgt, norm_post_w)
    do, delta, dg = _attn_gate_bwd(dmix, attn_pre, proj)
    acc = _attn_bwd(proj, do, lse, delta, 1, None, F32)
    acc = _attn_bwd(proj, do, lse, delta, 4, acc, F32)
    dq, dk, dv = _attn_bwd(proj, do, lse, delta, 16, acc, BF16)
    dz, dxbcdt, dcw, dcb, dvec, dnw_ssm = _ssm_bwd(proj, dmix, y_save, states, cw_all, conv_b, dtb, alog, d_b,
                                                   ssm_norm_w)
    srcs = [dq, dk, dv, dg, dz]
    grad_x, dnw_pre = _inproj_bwd_dx(srcs, dxbcdt, w_all, x2, dy, norm_pre_w)
    dws = [_dw(u, s, f"dw_in_{n}") for s, n in zip(srcs + [dxbcdt], ("q", "k", "v", "g", "z", "xbcdt"))]
    dw_all = jnp.concatenate(dws, axis=1)
    small = _pack_small(dcw, dnw_pre, dcb, dnw_ssm, dnw_post, dvec[0:1, :NH], dvec[1:2, :NH], dvec[2:3, :NH])
    return loss_part, grad_x, dw_all, dw_out, small
```

```python
import functools

import jax
import jax.numpy as jnp
from jax import lax
from jax.experimental import pallas as pl
from jax.experimental.pallas import tpu as pltpu

F32 = jnp.float32
BF16 = jnp.bfloat16
MESH = pl.DeviceIdType.MESH
SDS = jax.ShapeDtypeStruct
ANY = pl.BlockSpec(memory_space=pl.ANY)

S = 4096
D = 1024
DP = 7168
SHARD = 1668
OFF_G, OFF_Z = 3072, 4096
NH = 16
CH = 128
NC = S // CH
EPS = 1e-6
NEG = -1e30
LANE = 128
VMEM_LIMIT = 48 * 1024 * 1024

ADAM_LR, ADAM_B1, ADAM_B2, ADAM_EPS, ADAM_WD, ADAM_STEP = 0.001, 0.9, 0.999, 1e-08, 0.01, 10


def _cp(sem, **kw):
    return pltpu.CompilerParams(dimension_semantics=sem, vmem_limit_bytes=VMEM_LIMIT, **kw)


def _dot(a, b):
    return jnp.dot(a, b, preferred_element_type=F32)


def _dot_nt(a, b):
    return lax.dot_general(a, b, (((1,), (1,)), ((), ())), preferred_element_type=F32)


def _dot_tn(a, b):
    return lax.dot_general(a, b, (((0,), (0,)), ((), ())), preferred_element_type=F32)


def _pieces(x, n):
    out = []
    for _ in range(n):
        p = x.astype(BF16)
        out.append(p)
        x = x - p.astype(F32)
    return out


def _pick(x, sel, n=2):
    parts = [_dot(p, sel) for p in _pieces(x, n)]
    return functools.reduce(jnp.add, parts)


def _pick_left(sel, x, n=3):
    parts = [_dot(sel, p) for p in _pieces(x, n)]
    return functools.reduce(jnp.add, parts)


def _sigmoid(v):
    return 0.5 * jnp.tanh(0.5 * v) + 0.5


def _iota(shape, dim):
    return lax.broadcasted_iota(jnp.int32, shape, dim)


def _inproj_fwd(x, nw, w_all):
    tm, tn = 1024, 1024

    def body(x_ref, nw_ref, w_ref, proj_ref, u_ref):
        @pl.when(pl.program_id(1) == 0)
        def _():
            xf = x_ref[...]
            r = lax.rsqrt(jnp.mean(xf * xf, axis=-1, keepdims=True) + EPS)
            u_ref[...] = (xf * r * nw_ref[...]).astype(BF16)

        proj_ref[...] = _dot(u_ref[...], w_ref[...])

    return pl.pallas_call(
        body, name="inproj_fwd", grid=(S // tm, DP // tn),
        in_specs=[pl.BlockSpec((tm, D), lambda i, j: (i, 0)), pl.BlockSpec((1, D), lambda i, j: (0, 0)),
                  pl.BlockSpec((D, tn), lambda i, j: (0, j))],
        out_specs=[pl.BlockSpec((tm, tn), lambda i, j: (i, j)), pl.BlockSpec((tm, D), lambda i, j: (i, 0))],
        out_shape=[SDS((S, DP), F32), SDS((S, D), BF16)],
        compiler_params=_cp(("parallel", "arbitrary")),
    )(x, nw, w_all)


ATTN_QB = {1: 4, 4: 1, 16: 1}


def _unit_rows(r, u, d):
    return pl.ds(r + d * CH * u, CH, stride=d) if d > 1 else pl.ds(CH * u, CH)


def _for_units(d, qb, fn):
    for r in range(d):
        for u in range(qb):
            fn(r, u)


def _attn_mask(has_prev):
    qi, kj = _iota((2 * CH, 2 * CH), 0) & (CH - 1), _iota((2 * CH, 2 * CH), 1)
    cur_ok = (kj >= CH) & (kj - CH <= qi)
    prev_ok = (kj < CH) & (kj >= qi)
    return cur_ok | (prev_ok & has_prev)


def _stack_heads(v, lane_a):
    return jnp.concatenate([jnp.where(lane_a, v, 0.0), jnp.where(lane_a, 0.0, v)], axis=0).astype(BF16)


def _unit_kv(p_ref, c_ref, r, u, d, qb):
    prev = p_ref[_unit_rows(r, 0, d), :] if u == 0 else c_ref[_unit_rows(r, u - 1, d), :]
    return jnp.concatenate([prev, c_ref[_unit_rows(r, u, d), :]], axis=0).astype(BF16)


def _attn_specs(d, qb, lag_out):
    rows, prows = CH * d * qb, CH * d
    nb = S // rows
    last = nb - 1

    def cur(off):
        return pl.BlockSpec((rows, LANE), lambda c, i: (jnp.minimum(i, last), off + c))

    def prev(off):
        return pl.BlockSpec((prows, LANE), lambda c, i: (jnp.clip(i * qb - 1, 0, S // prows - 1), off + c))

    lag = pl.BlockSpec((rows, LANE), lambda c, i: (jnp.clip(i - 1, 0, last), c)) if lag_out else None
    return nb, cur, prev, lag


def _attn_fwd(proj, d):
    qb = ATTN_QB[d]
    nb, cur, prev, _ = _attn_specs(d, qb, False)

    def body(q_ref, kp_ref, kc_ref, vp_ref, vc_ref, o_ref, l_ref):
        i = pl.program_id(1)
        lane_a = _iota((CH, LANE), 1) < 64
        mask_first, mask_rest = _attn_mask(i > 0), _attn_mask(True)

        def unit(r, u):
            sl = _unit_rows(r, u, d)
            q2 = _stack_heads(q_ref[sl, :] * 0.125, lane_a)
            k2, v2 = _unit_kv(kp_ref, kc_ref, r, u, d, qb), _unit_kv(vp_ref, vc_ref, r, u, d, qb)
            s = jnp.where(mask_first if u == 0 else mask_rest, _dot_nt(q2, k2), NEG)
            m = jnp.max(s, axis=1, keepdims=True)
            p = jnp.exp(s - m)
            l = jnp.sum(p, axis=1, keepdims=True)
            o2 = _dot(p.astype(BF16), v2) / l
            lse2 = m + jnp.log(l)
            o_ref[sl, :] = jnp.where(lane_a, o2[:CH], o2[CH:])
            l_ref[sl, :] = jnp.where(lane_a, lse2[:CH], lse2[CH:])

        _for_units(d, qb, unit)

    return pl.pallas_call(
        body, name=f"attn_fwd_d{d}", grid=(NH // 2, nb),
        in_specs=[cur(0), prev(8), cur(8), prev(16), cur(16)],
        out_specs=[cur(0), cur(0)],
        out_shape=[SDS((S, D), F32), SDS((S, D), F32)],
        compiler_params=_cp(("parallel", "arbitrary")),
    )(proj, proj, proj, proj, proj)


def _attn_merge(o1, o2, o3, l1, l2, l3, proj):
    tm = 512

    def body(o1r, o2r, o3r, l1r, l2r, l3r, g_ref, mix_ref, pre_ref, lse_ref):
        l1v, l2v, l3v = l1r[...], l2r[...], l3r[...]
        m = jnp.maximum(jnp.maximum(l1v, l2v), l3v)
        e1, e2, e3 = jnp.exp(l1v - m), jnp.exp(l2v - m), jnp.exp(l3v - m)
        tot = e1 + e2 + e3
        out = (e1 * o1r[...] + e2 * o2r[...] + e3 * o3r[...]) / tot
        g = g_ref[...]
        pre_ref[...] = out
        lse_ref[...] = m + jnp.log(tot)
        mix_ref[...] = (out * (g * _sigmoid(g))).astype(BF16)

    t = pl.BlockSpec((tm, D), lambda i: (i, 0))
    return pl.pallas_call(
        body, name="attn_merge", grid=(S // tm,),
        in_specs=[t, t, t, t, t, t, pl.BlockSpec((tm, D), lambda i: (i, OFF_G // D))],
        out_specs=[t, t, t],
        out_shape=[SDS((S, 2 * D), BF16), SDS((S, D), F32), SDS((S, D), F32)],
        compiler_params=_cp(("parallel",)),
    )(o1, o2, o3, l1, l2, l3, proj)


def _attn_gate_bwd(dmix, pre, proj):
    tm = 512

    def body(dm_ref, pre_ref, g_ref, do_ref, delta_ref, dg_ref):
        g, dm, pre_v = g_ref[...], dm_ref[...], pre_ref[...]
        sig = _sigmoid(g)
        do = dm * (g * sig)
        do_ref[...] = do
        dg_ref[...] = (dm * pre_v * (sig * (1.0 + g * (1.0 - sig)))).astype(BF16)
        prod = do * pre_v
        same_head = (_iota((LANE, LANE), 0) // 64 == _iota((LANE, LANE), 1) // 64).astype(BF16)
        for cb in range(D // LANE):
            delta_ref[:, cb * LANE:(cb + 1) * LANE] = _pick(prod[:, cb * LANE:(cb + 1) * LANE], same_head)

    t = pl.BlockSpec((tm, D), lambda i: (i, 0))
    return pl.pallas_call(
        body, name="attn_gate_bwd", grid=(S // tm,),
        in_specs=[t, t, pl.BlockSpec((tm, D), lambda i: (i, OFF_G // D))],
        out_specs=[t, t, t],
        out_shape=[SDS((S, D), F32), SDS((S, D), F32), SDS((S, D), BF16)],
        compiler_params=_cp(("parallel",)),
    )(dmix, pre, proj)


def _attn_bwd(proj, do, lse, delta, d, acc, out_dtype, hosted=None):
    qb = ATTN_QB[d]
    nb, cur, prev, lag = _attn_specs(d, qb, True)
    has_acc = acc is not None
    n_in = 11 if has_acc else 8
    n_host = len(hosted.arrays) if hosted else 0
    n_scr = 2 if out_dtype == F32 else 3

    def body(*refs):
        q_ref, kp_ref, kc_ref, vp_ref, vc_ref, do_ref, lse_ref, dl_ref = refs[:8]
        if has_acc:
            aq_ref, ak_ref, av_ref = refs[8:11]
        host_in, refs = refs[n_in:n_in + n_host], refs[n_in + n_host:]
        dq_ref, dk_ref, dv_ref = refs[:3]
        host_out, refs = refs[3:3 + n_host], refs[3 + n_host:]
        ck_ref, cv_ref = refs[:2]
        dq_f32 = dq_ref if out_dtype == F32 else refs[2]
        host_sems = refs[n_scr:]
        i = pl.program_id(1)
        if hosted:
            pl.when((pl.program_id(0) == 0) & (i == 0))(lambda: hosted.start(host_in, host_out, host_sems))
        slot = i & 1
        now_k, now_v, old_k, old_v = ck_ref.at[slot], cv_ref.at[slot], ck_ref.at[1 - slot], cv_ref.at[1 - slot]
        lane_a = _iota((CH, LANE), 1) < 64
        mask_first, mask_rest = _attn_mask(i > 0), _attn_mask(True)

        @pl.when(i == 0)
        def _():
            ck_ref[1] = jnp.zeros((ck_ref.shape[1], LANE), F32)
            cv_ref[1] = jnp.zeros((cv_ref.shape[1], LANE), F32)

        def unit(r, u):
            sl = _unit_rows(r, u, d)
            q2 = _stack_heads(q_ref[sl, :] * 0.125, lane_a)
            do2 = _stack_heads(do_ref[sl, :], lane_a)
            k2, v2 = _unit_kv(kp_ref, kc_ref, r, u, d, qb), _unit_kv(vp_ref, vc_ref, r, u, d, qb)
            lsev, dlv = lse_ref[sl, :], dl_ref[sl, :]
            lse2 = jnp.concatenate([lsev[:, 0:1], lsev[:, 64:65]], axis=0)
            dl2 = jnp.concatenate([dlv[:, 0:1], dlv[:, 64:65]], axis=0)
            p = jnp.exp(jnp.where(mask_first if u == 0 else mask_rest, _dot_nt(q2, k2), NEG) - lse2)
            ds = (p * (_dot_nt(do2, v2) - dl2)).astype(BF16)
            dq2 = _dot(ds, k2)
            dk2 = _dot_tn(ds, q2)
            dv2 = _dot_tn(p.astype(BF16), do2)
            dq = jnp.where(lane_a, dq2[:CH], dq2[CH:]) * 0.125
            if has_acc:
                dq = dq + aq_ref[sl, :]
            dq_f32[sl, :] = dq
            if u == 0:
                before = _unit_rows(r, qb - 1, d)
                old_k[before, :] += dk2[:CH]
                old_v[before, :] += dv2[:CH]
            else:
                before = _unit_rows(r, u - 1, d)
                now_k[before, :] += dk2[:CH]
                now_v[before, :] += dv2[:CH]
            now_k[sl, :] = dk2[CH:]
            now_v[sl, :] = dv2[CH:]

        @pl.when(i < nb)
        def _():
            _for_units(d, qb, unit)
            if out_dtype != F32:
                dq_ref[...] = dq_f32[...].astype(out_dtype)

        dk, dv = old_k[...], old_v[...]
        if has_acc:
            dk, dv = dk + ak_ref[...], dv + av_ref[...]
        dk_ref[...] = dk.astype(out_dtype)
        dv_ref[...] = dv.astype(out_dtype)
        if hosted:
            pl.when((pl.program_id(0) == NH // 2 - 1) & (i == nb))(
                lambda: hosted.finish(host_in, host_out, host_sems))

    in_specs = [cur(0), prev(8), cur(8), prev(16), cur(16), cur(0), cur(0), cur(0)]
    args = [proj, proj, proj, proj, proj, do, lse, delta]
    if has_acc:
        in_specs += [cur(0), lag, lag]
        args += list(acc)
    rows = CH * d * qb
    scratch = [pltpu.VMEM((2, rows, LANE), F32), pltpu.VMEM((2, rows, LANE), F32)]
    if out_dtype != F32:
        scratch.append(pltpu.VMEM((rows, LANE), F32))
    out_specs, out_shape = [cur(0), lag, lag], [SDS((S, D), out_dtype)] * 3
    if hosted:
        args += hosted.arrays
        in_specs += [ANY] * n_host
        out_specs += [ANY] * n_host
        out_shape += hosted.out_shape
        scratch += hosted.scratch
    outs = pl.pallas_call(
        body, name=f"attn_bwd_d{d}", grid=(NH // 2, nb + 1),
        in_specs=in_specs, out_specs=out_specs, out_shape=out_shape, scratch_shapes=scratch,
        compiler_params=_cp(("arbitrary", "arbitrary") if hosted else ("parallel", "arbitrary")),
    )(*args)
    return (outs[:3], outs[3:]) if hosted else outs


def _conv_taps(cur, prev8, first):
    row8 = _iota(prev8.shape, 0)
    prev8 = jnp.where(first, 0.0, prev8)
    taps = []
    for s in (3, 2, 1):
        rolled = pltpu.roll(cur, s, 0)
        head = jnp.where(row8 < s, pltpu.roll(prev8, s, 0), rolled[:8])
        taps.append(jnp.concatenate([head, rolled[8:]], axis=0))
    return taps + [cur]


def _conv(taps, w, b):
    acc = b + w[0:1, :] * taps[0]
    for k in (1, 2, 3):
        acc = acc + w[k:k + 1, :] * taps[k]
    return acc


def _expand():
    return (_iota((LANE, D), 1) // 64 == _iota((LANE, D), 0)).astype(BF16)


def _reduce():
    return (_iota((D, LANE), 0) // 64 == _iota((D, LANE), 1)).astype(BF16)


def _ssd_common(xs_raw, xs_prev, bc_raw, bc_prev, dt_raw, first, cw, cb, dtb, alog):
    head_lane = _iota((CH, LANE), 1) < NH
    xs_taps = _conv_taps(xs_raw, xs_prev, first)
    bc_taps = _conv_taps(bc_raw, bc_prev, first)
    xs_c = _conv(xs_taps, cw[:, :D], cb[:, :D])
    bc_c = _conv(bc_taps, cw[:, D:], cb[:, D:])
    xs = xs_c * _sigmoid(xs_c)
    bc = bc_c * _sigmoid(bc_c)
    pre = dt_raw + dtb
    dt = jnp.where(head_lane, jnp.maximum(pre, 0.0) + jnp.log(1.0 + jnp.exp(-jnp.abs(pre))), 0.0)
    a_row = jnp.where(head_lane[0:1], -jnp.exp(alog), 0.0)
    tri = (_iota((CH, CH), 1) <= _iota((CH, CH), 0)).astype(BF16)
    cs = _pick_left(tri, dt * a_row)
    cs_last = cs[CH - 1:CH, :]
    wide = _pick(jnp.concatenate([dt, jnp.exp(cs), jnp.exp(cs_last - cs)], axis=0), _expand())
    dt_b, e_b, f_b = wide[:CH], wide[CH:2 * CH], wide[2 * CH:]
    return dict(xs_taps=xs_taps, bc_taps=bc_taps, xs_c=xs_c, bc_c=bc_c, xs=xs, bc=bc, pre=pre, dt=dt,
                a_row=a_row, cs=cs, cs_t=cs.T, dt_b=dt_b, e_b=e_b, f_b=f_b, t_b=e_b[CH - 1:CH, :])


def _groups(bc):
    bcb = bc.astype(BF16)
    return [bcb[:, 0:128], bcb[:, 128:256]], [bcb[:, 256:384], bcb[:, 384:512]]


def _decay(q, h, tril):
    seg = q["cs"][:, h:h + 1] - q["cs_t"][h:h + 1, :]
    return jnp.exp(jnp.where(tril, seg, NEG))


def _ssm_fwd(proj, mix, cw, cb, dtb, alog, d_b, nw):
    def body(xs_ref, xsp_ref, bc_ref, bcp_ref, dt_ref, z_ref, cw_ref, cb_ref, dtb_ref, alog_ref, db_ref, nw_ref,
             mix_in_ref, mix_ref, y_ref, st_ref, h_ref):
        del mix_in_ref
        i = pl.program_id(0)

        @pl.when(i == 0)
        def _():
            h_ref[...] = jnp.zeros_like(h_ref)

        q = _ssd_common(xs_ref[...], xsp_ref[...], bc_ref[...], bcp_ref[...], dt_ref[...], i == 0,
                        cw_ref[...], cb_ref[...], dtb_ref[...], alog_ref[...])
        bg, cg = _groups(q["bc"])
        xs = q["xs"]
        xdt = xs * q["dt_b"]
        xdt_b = xdt.astype(BF16)
        h_in = h_ref[...]
        st_ref[...] = h_in
        hb = h_in.astype(BF16)
        tril = _iota((CH, CH), 1) <= _iota((CH, CH), 0)
        lane_a = _iota((CH, LANE), 1) < 64
        cbm = [_dot_nt(cg[g], bg[g]) for g in range(2)]
        pairs = []
        for hp in range(NH // 2):
            xp = xdt_b[:, hp * LANE:(hp + 1) * LANE]
            ya = _dot((cbm[hp // 4] * _decay(q, 2 * hp, tril)).astype(BF16), xp)
            yb = _dot((cbm[hp // 4] * _decay(q, 2 * hp + 1, tril)).astype(BF16), xp)
            pairs.append(jnp.where(lane_a, ya, yb))
        y_diag = jnp.concatenate(pairs, axis=1)
        y_off = jnp.concatenate([_dot(cg[g], hb[:, g * 512:(g + 1) * 512]) for g in range(2)], axis=1) * q["e_b"]
        y = y_diag + y_off + db_ref[...] * xs
        y_ref[...] = y
        xf = (xdt * q["f_b"]).astype(BF16)
        h_ref[...] = q["t_b"] * h_in + jnp.concatenate(
            [_dot_tn(bg[g], xf[:, g * 512:(g + 1) * 512]) for g in range(2)], axis=1)
        z = z_ref[...]
        yz = y * (z * _sigmoid(z))
        outs = []
        for g in range(2):
            v = yz[:, g * 512:(g + 1) * 512]
            outs.append(v * lax.rsqrt(jnp.mean(v * v, axis=-1, keepdims=True) + EPS))
        mix_ref[...] = (jnp.concatenate(outs, axis=1) * nw_ref[...]).astype(BF16)

    def col(width, blk, prev=False):
        if prev:
            return pl.BlockSpec((8, width), lambda i: (jnp.maximum(i * (CH // 8) - 1, 0), blk))
        return pl.BlockSpec((CH, width), lambda i: (i, blk))

    def full(a):
        return pl.BlockSpec(a.shape, lambda i: (0,) * a.ndim)

    return pl.pallas_call(
        body, name="ssm_fwd", grid=(NC,),
        in_specs=[col(D, 5), col(D, 5, True), col(512, 12), col(512, 12, True), col(LANE, 52), col(D, 4),
                  full(cw), full(cb), full(dtb), full(alog), full(d_b), full(nw), ANY],
        out_specs=[col(D, 1), col(D, 0), pl.BlockSpec((None, CH, D), lambda i: (i, 0, 0))],
        out_shape=[SDS((S, 2 * D), BF16), SDS((S, D), F32), SDS((NC, CH, D), F32)],
        scratch_shapes=[pltpu.VMEM((CH, D), F32)],
        input_output_aliases={12: 0},
        compiler_params=_cp(("arbitrary",)),
    )(proj, proj, proj, proj, proj, proj, cw, cb, dtb, alog, d_b, nw, mix)


def _ssm_bwd(proj, dmix, y_save, states, cw, cb, dtb, alog, d_b, nw):
    def body(xs_ref, xsp_ref, bc_ref, bcp_ref, dt_ref, z_ref, dn_ref, y_ref, st_ref,
             cw_ref, cb_ref, dtb_ref, alog_ref, db_ref, nw_ref,
             dz_ref, dx_ref, dcw_ref, dcb_ref, dsm_ref, dnw_ref, dh_ref, nxs_ref, nbc_ref):
        i = pl.program_id(0)
        ci = NC - 1 - i

        @pl.when(i == 0)
        def _():
            for ref in (dcw_ref, dcb_ref, dsm_ref, dnw_ref, dh_ref, nxs_ref, nbc_ref):
                ref[...] = jnp.zeros_like(ref)

        cw, cb = cw_ref[...], cb_ref[...]
        q = _ssd_common(xs_ref[...], xsp_ref[...], bc_ref[...], bcp_ref[...], dt_ref[...], ci == 0,
                        cw, cb, dtb_ref[...], alog_ref[...])
        bg, cg = _groups(q["bc"])
        xs, dt_b, e_b, f_b, t_b = q["xs"], q["dt_b"], q["e_b"], q["f_b"], q["t_b"]
        xdt = xs * dt_b
        xdt_b = xdt.astype(BF16)
        h_in = st_ref[...]
        hb = h_in.astype(BF16)
        dh_new = dh_ref[...]
        dhb = dh_new.astype(BF16)
        red = _reduce()

        z, y, dn, nw_v = z_ref[...], y_ref[...], dn_ref[...], nw_ref[...]
        sig = _sigmoid(z)
        sz = z * sig
        yz = y * sz
        gdn = dn * nw_v
        dyz, dnw = [], []
        for g in range(2):
            v, gv = yz[:, g * 512:(g + 1) * 512], gdn[:, g * 512:(g + 1) * 512]
            r = lax.rsqrt(jnp.mean(v * v, axis=-1, keepdims=True) + EPS)
            dnw.append(dn[:, g * 512:(g + 1) * 512] * v * r)
            dyz.append(r * (gv - v * (r * r) * jnp.mean(gv * v, axis=-1, keepdims=True)))
        dyz = jnp.concatenate(dyz, axis=1)
        dnw_ref[...] += jnp.sum(jnp.concatenate(dnw, axis=1), axis=0, keepdims=True)
        dy = dyz * sz
        dz_ref[...] = (dyz * y * (sig * (1.0 + z * (1.0 - sig)))).astype(BF16)
        dy_b = dy.astype(BF16)

        tril = _iota((CH, CH), 1) <= _iota((CH, CH), 0)
        lane_a = _iota((CH, LANE), 1) < 64
        cbm = [_dot_nt(cg[g], bg[g]) for g in range(2)]
        dcbm = [jnp.zeros((CH, CH), F32), jnp.zeros((CH, CH), F32)]
        seg_rows = jnp.zeros((CH, LANE), F32)
        seg_cols = jnp.zeros((LANE, CH), F32)
        row_id, col_id = _iota((CH, LANE), 0), _iota((CH, LANE), 1)
        dx_pairs = []
        for hp in range(NH // 2):
            g = hp // 4
            xp = xdt_b[:, hp * LANE:(hp + 1) * LANE]
            dyp_f = dy[:, hp * LANE:(hp + 1) * LANE]
            dyp = dy_b[:, hp * LANE:(hp + 1) * LANE]
            halves = []
            for k in range(2):
                h = 2 * hp + k
                lane = lane_a if k == 0 else jnp.logical_not(lane_a)
                dec = _decay(q, h, tril)
                gm = cbm[g] * dec
                dgm = _dot_nt(jnp.where(lane, dyp_f, 0.0).astype(BF16), xp)
                dcbm[g] = dcbm[g] + dgm * dec
                prod = dgm * gm
                seg_rows = jnp.where(col_id == h, jnp.sum(prod, axis=1, keepdims=True), seg_rows)
                seg_cols = jnp.where(row_id == h, jnp.sum(prod, axis=0, keepdims=True), seg_cols)
                halves.append(_dot_tn(gm.astype(BF16), dyp))
            dx_pairs.append(jnp.where(lane_a, halves[0], halves[1]))
        dxdt_diag = jnp.concatenate(dx_pairs, axis=1)

        qv = jnp.concatenate([_dot(bg[g], dhb[:, g * 512:(g + 1) * 512]) for g in range(2)], axis=1)
        y_off = jnp.concatenate([_dot(cg[g], hb[:, g * 512:(g + 1) * 512]) for g in range(2)], axis=1) * e_b
        xfq = xdt * f_b * qv
        dxdt = dxdt_diag + f_b * qv
        tdt = jnp.sum(dh_new * h_in, axis=0, keepdims=True) * t_b
        per_head = _pick(jnp.concatenate([xfq, dy * y_off, dxdt * xs, dy * xs, jnp.broadcast_to(tdt, (8, D))],
                                         axis=0), red)
        fdf, dyoff_h, dxdtxs_h, dyxs_h = [per_head[k * CH:(k + 1) * CH] for k in range(4)]
        dcs = seg_rows - seg_cols.T + dyoff_h - fdf
        last = per_head[4 * CH:4 * CH + 1] + jnp.sum(fdf, axis=0, keepdims=True)
        dcs = dcs + jnp.where(_iota((CH, LANE), 0) == CH - 1, last, 0.0)
        tri_t = (_iota((CH, CH), 1) >= _iota((CH, CH), 0)).astype(BF16)
        da = _pick_left(tri_t, dcs)
        ddt = da * q["a_row"] + dxdtxs_h
        dxs = dxdt * dt_b + db_ref[...] * dy
        ddt_raw = ddt * _sigmoid(q["pre"])
        dsm_ref[0:1, :] += jnp.sum(ddt_raw, axis=0, keepdims=True)
        dsm_ref[1:2, :] += jnp.sum(da * q["dt"], axis=0, keepdims=True) * q["a_row"]
        dsm_ref[2:3, :] += jnp.sum(dyxs_h, axis=0, keepdims=True)
        edy = (e_b * dy).astype(BF16)
        xf = (xdt * f_b).astype(BF16)
        dbs, dcs_g, dhs = [], [], []
        for g in range(2):
            sl = slice(g * 512, (g + 1) * 512)
            dcb_b = dcbm[g].astype(BF16)
            dcs_g.append(_dot(dcb_b, bg[g]) + _dot_nt(edy[:, sl], hb[:, sl]))
            dbs.append(_dot_tn(dcb_b, cg[g]) + _dot_nt(xf[:, sl], dhb[:, sl]))
            dhs.append(_dot_tn(cg[g], edy[:, sl]))
        dh_ref[...] = t_b * dh_new + jnp.concatenate(dhs, axis=1)
        dbc = jnp.concatenate(dbs + dcs_g, axis=1)

        def conv_bwd(dact, pre, taps, w, nxt_ref, lo):
            s = _sigmoid(pre)
            dconv = dact * (s * (1.0 + pre * (1.0 - s)))
            nxt8 = nxt_ref[...]
            row8 = _iota(nxt8.shape, 0)
            hi = lo + dconv.shape[1]
            dcb_ref[:, lo:hi] += jnp.sum(dconv, axis=0, keepdims=True)
            dx = w[3:4, :] * dconv
            for k in range(4):
                dcw_ref[k:k + 1, lo:hi] += jnp.sum(dconv * taps[k], axis=0, keepdims=True)
            for s_ in (1, 2, 3):
                rolled = pltpu.roll(dconv, CH - s_, 0)
                tail = jnp.where(row8 >= 8 - s_, pltpu.roll(nxt8, 8 - s_, 0), rolled[CH - 8:])
                dx = dx + w[3 - s_:4 - s_, :] * jnp.concatenate([rolled[:CH - 8], tail], axis=0)
            nxt_ref[...] = dconv[:8]
            return dx

        dx_ref[:, 0:D] = conv_bwd(dxs, q["xs_c"], q["xs_taps"], cw[:, :D], nxs_ref, 0).astype(BF16)
        dx_ref[:, D:D + 512] = conv_bwd(dbc, q["bc_c"], q["bc_taps"], cw[:, D:], nbc_ref, D).astype(BF16)
        dx_ref[:, D + 512:D + 640] = ddt_raw.astype(BF16)
        dx_ref[:, D + 640:] = jnp.zeros((CH, D - 640), BF16)

    def col(width, blk, prev=False):
        if prev:
            return pl.BlockSpec((8, width), lambda i: (jnp.maximum((NC - 1 - i) * (CH // 8) - 1, 0), blk))
        return pl.BlockSpec((CH, width), lambda i: (NC - 1 - i, blk))

    def full(a):
        return pl.BlockSpec(a.shape, lambda i: (0,) * len(a.shape))

    acc_shapes = [SDS((4, 1536), F32), SDS((1, 1536), F32), SDS((8, LANE), F32), SDS((1, D), F32)]
    return pl.pallas_call(
        body, name="ssm_bwd", grid=(NC,),
        in_specs=[col(D, 5), col(D, 5, True), col(512, 12), col(512, 12, True), col(LANE, 52), col(D, 4),
                  col(D, 1), col(D, 0), pl.BlockSpec((None, CH, D), lambda i: (NC - 1 - i, 0, 0)),
                  full(cw), full(cb), full(dtb), full(alog), full(d_b), full(nw)],
        out_specs=[col(D, 0), col(2 * D, 0)] + [full(a) for a in acc_shapes],
        out_shape=[SDS((S, D), BF16), SDS((S, 2 * D), BF16)] + acc_shapes,
        scratch_shapes=[pltpu.VMEM((CH, D), F32), pltpu.VMEM((8, D), F32), pltpu.VMEM((8, 512), F32)],
        compiler_params=_cp(("arbitrary",)),
    )(proj, proj, proj, proj, proj, proj, dmix, y_save, states, cw, cb, dtb, alog, d_b, nw)


def _outproj_loss(mix, w_out, x, tgt, nw):
    tm = 256

    def body(mix_ref, w_ref, x_ref, t_ref, nw_ref, dy_ref, dmix_ref, dw_ref, dnw_ref, loss_ref):
        @pl.when(pl.program_id(0) == 0)
        def _():
            dw_ref[...] = jnp.zeros_like(dw_ref)
            dnw_ref[...] = jnp.zeros_like(dnw_ref)
            loss_ref[...] = jnp.zeros_like(loss_ref)

        mixv, w = mix_ref[...], w_ref[...]
        out = _dot(mixv, w)
        r = lax.rsqrt(jnp.mean(out * out, axis=-1, keepdims=True) + EPS)
        nh = out * r
        nw_v = nw_ref[...]
        err = x_ref[...] + nh * nw_v - t_ref[...]
        loss_ref[...] += 0.5 * jnp.sum(jnp.mean(err * err, axis=-1, keepdims=True), axis=0, keepdims=True)
        dy = err * (1.0 / D)
        dy_ref[...] = dy
        dnw_ref[...] += jnp.sum(dy * nh, axis=0, keepdims=True)
        gdn = dy * nw_v
        dout = (r * (gdn - nh * jnp.mean(gdn * nh, axis=-1, keepdims=True))).astype(BF16)
        dmix_ref[...] = _dot_nt(dout, w)
        dw_ref[...] += _dot_tn(mixv, dout)

    row = lambda w: pl.BlockSpec((tm, w), lambda i: (i, 0))
    full = lambda s: pl.BlockSpec(s, lambda i: (0, 0))
    return pl.pallas_call(
        body, name="outproj_loss", grid=(S // tm,),
        in_specs=[row(2 * D), full((2 * D, D)), row(D), row(D), full((1, D))],
        out_specs=[row(D), row(2 * D), full((2 * D, D)), full((1, D)), full((1, LANE))],
        out_shape=[SDS((S, D), F32), SDS((S, 2 * D), F32), SDS((2 * D, D), F32), SDS((1, D), F32),
                   SDS((1, LANE), F32)],
        compiler_params=_cp(("arbitrary",)),
    )(mix, w_out, x, tgt, nw)


def _inproj_bwd_dx(srcs, dxbcdt, w_all, x, dy, nw, hosted=None):
    tm = 512
    nk = DP // D
    n_host = len(hosted.arrays) if hosted else 0

    def body(*refs):
        src_refs = refs[:nk]
        w_ref, x_ref, dy_ref, nw_ref = refs[nk:nk + 4]
        host_in, refs = refs[nk + 4:nk + 4 + n_host], refs[nk + 4 + n_host:]
        gx_ref, dnw_ref = refs[:2]
        host_out, acc_ref, host_sems = refs[2:2 + n_host], refs[2 + n_host], refs[3 + n_host:]
        i, kk = pl.program_id(0), pl.program_id(1)
        if hosted:
            pl.when((i == 0) & (kk == 0))(lambda: hosted.start(host_in, host_out, host_sems))

        @pl.when((i == 0) & (kk == 0))
        def _():
            dnw_ref[...] = jnp.zeros_like(dnw_ref)

        @pl.when(kk == 0)
        def _():
            acc_ref[...] = jnp.zeros_like(acc_ref)

        for s, ref in enumerate(src_refs):
            @pl.when(kk == s)
            def _(ref=ref):
                acc_ref[...] += _dot_nt(ref[...], w_ref[...])

        @pl.when(kk == nk - 1)
        def _():
            xf, du, nw_v = x_ref[...], acc_ref[...], nw_ref[...]
            r = lax.rsqrt(jnp.mean(xf * xf, axis=-1, keepdims=True) + EPS)
            xh = xf * r
            dnw_ref[...] += jnp.sum(du * xh, axis=0, keepdims=True)
            gdu = du * nw_v
            gx_ref[...] = r * (gdu - xh * jnp.mean(gdu * xh, axis=-1, keepdims=True)) + dy_ref[...]

        if hosted:
            pl.when((i == S // tm - 1) & (kk == nk - 1))(lambda: hosted.finish(host_in, host_out, host_sems))

    row = pl.BlockSpec((tm, D), lambda i, k: (i, 0))
    row1 = pl.BlockSpec((tm, D), lambda i, k: (i, 1))
    one = pl.BlockSpec((1, D), lambda i, k: (0, 0))
    args = [*srcs, dxbcdt, dxbcdt, w_all, x, dy, nw]
    in_specs = [row] * len(srcs) + [row, row1, pl.BlockSpec((D, D), lambda i, k: (0, k)), row, row, one]
    out_specs, out_shape = [row, one], [SDS((S, D), F32), SDS((1, D), F32)]
    scratch = [pltpu.VMEM((tm, D), F32)]
    if hosted:
        args += hosted.arrays
        in_specs += [ANY] * n_host
        out_specs += [ANY] * n_host
        out_shape += hosted.out_shape
        scratch += hosted.scratch
    outs = pl.pallas_call(
        body, name="inproj_bwd_dx", grid=(S // tm, nk),
        in_specs=in_specs, out_specs=out_specs, out_shape=out_shape, scratch_shapes=scratch,
        compiler_params=_cp(("arbitrary", "arbitrary")),
    )(*args)
    return (outs[:2], outs[2:]) if hosted else outs


def _dw(u, dsec, name):
    ts = 512
    ncol = dsec.shape[1] // D

    def body(u_ref, d_ref, o_ref):
        @pl.when(pl.program_id(1) == 0)
        def _():
            o_ref[...] = jnp.zeros_like(o_ref)

        o_ref[...] += _dot_tn(u_ref[...], d_ref[...])

    return pl.pallas_call(
        body, name=name, grid=(ncol, S // ts),
        in_specs=[pl.BlockSpec((ts, D), lambda j, i: (i, 0)), pl.BlockSpec((ts, D), lambda j, i: (i, j))],
        out_specs=pl.BlockSpec((D, D), lambda j, i: (0, j)),
        out_shape=SDS((D, ncol * D), F32),
        compiler_params=_cp(("parallel", "arbitrary")),
    )(u, dsec)


def _place():
    x, y, c = lax.axis_index("x"), lax.axis_index("y"), lax.axis_index("c")
    return x, y, c, 2 * x + y


def _chip_of(x, y, k):
    px = 1 - x if k & 2 else x
    py = 1 - y if k & 1 else y
    return px, py, 2 * px + py


def _remote(src, dst, send_sem, recv_sem, dev):
    return pltpu.make_async_remote_copy(src_ref=src, dst_ref=dst, send_sem=send_sem, recv_sem=recv_sem,
                                        device_id=dev, device_id_type=MESH)


def _gather_weights(w_in_b, w_out_b, conv_w):
    hin, hout = D // 2, w_out_b.shape[0] // 2

    def body(win_ref, wout_ref, cw_ref, gin_ref, gout_ref, gcw_ref, send, recv, fsend, frecv):
        x, y, c, j = _place()
        sib = (x, y, 1 - c)
        rin, rout = pl.ds(c * hin, hin), pl.ds(c * hout, hout)
        sin, sout = pl.ds((1 - c) * hin, hin), pl.ds((1 - c) * hout, hout)
        sends = []
        for k in (1, 2, 3):
            px, py, _ = _chip_of(x, y, k)
            dev = (px, py, c)
            sends += [_remote(win_ref.at[rin], gin_ref.at[j, rin], send.at[k - 1], recv.at[k - 1], dev),
                      _remote(wout_ref.at[rout], gout_ref.at[j, rout], send.at[k + 2], recv.at[k + 2], dev),
                      _remote(cw_ref, gcw_ref.at[j], send.at[k + 5], recv.at[k + 5], dev)]
        for cp in sends:
            cp.start()
        fwd = []
        for k in (1, 2, 3):
            _, _, pj = _chip_of(x, y, k)
            _remote(win_ref.at[rin], gin_ref.at[pj, rin], send.at[k - 1], recv.at[k - 1], sib).wait_recv()
            f_in = _remote(gin_ref.at[pj, rin], gin_ref.at[pj, rin], fsend.at[k - 1], frecv.at[k - 1], sib)
            f_in.start()
            _remote(wout_ref.at[rout], gout_ref.at[pj, rout], send.at[k + 2], recv.at[k + 2], sib).wait_recv()
            f_out = _remote(gout_ref.at[pj, rout], gout_ref.at[pj, rout], fsend.at[k + 2], frecv.at[k + 2], sib)
            f_out.start()
            fwd += [f_in, f_out]
        for k in (1, 2, 3):
            _, _, pj = _chip_of(x, y, k)
            _remote(win_ref.at[sin], gin_ref.at[pj, sin], fsend.at[k - 1], frecv.at[k - 1], sib).wait_recv()
            _remote(wout_ref.at[sout], gout_ref.at[pj, sout], fsend.at[k + 2], frecv.at[k + 2], sib).wait_recv()
            _remote(cw_ref, gcw_ref.at[pj], send.at[k + 5], recv.at[k + 5], sib).wait_recv()
        for cp in sends + fwd:
            cp.wait_send()

    return pl.pallas_call(
        body, name="gather_weights",
        in_specs=[ANY, ANY, ANY], out_specs=[ANY, ANY, ANY],
        out_shape=[SDS((4,) + w_in_b.shape, BF16), SDS((4,) + w_out_b.shape, BF16), SDS((4,) + conv_w.shape, F32)],
        scratch_shapes=[pltpu.SemaphoreType.DMA((9,)), pltpu.SemaphoreType.DMA((9,)),
                        pltpu.SemaphoreType.DMA((6,)), pltpu.SemaphoreType.DMA((6,))],
        compiler_params=pltpu.CompilerParams(has_side_effects=True),
    )(w_in_b, w_out_b, conv_w)


def _pair_exchange(arrays, name):
    halves = [a.shape[1] // 2 for a in arrays]
    n = len(arrays)

    def body(*refs):
        x, y, c, _ = _place()
        send, recv = refs[2 * n:]
        cps = [_remote(refs[k].at[:, pl.ds((1 - c) * halves[k], halves[k])], refs[n + k], send.at[k], recv.at[k],
                       (x, y, 1 - c)) for k in range(n)]
        for cp in cps:
            cp.start()
        for cp in cps:
            cp.wait()

    return pl.pallas_call(
        body, name=name, in_specs=[ANY] * n, out_specs=[ANY] * n,
        out_shape=[SDS((a.shape[0], h, a.shape[2]), F32) for a, h in zip(arrays, halves)],
        scratch_shapes=[pltpu.SemaphoreType.DMA((n,)), pltpu.SemaphoreType.DMA((n,))],
        compiler_params=pltpu.CompilerParams(has_side_effects=True),
    )(*arrays)


def _pair_sum(cidx, g, r, name):
    n, half, width = r.shape
    tr = min(half, 256)
    nt = half // tr

    def body(c_ref, g_ref, r_ref, o_ref):
        del c_ref
        o_ref[...] = (g_ref[...] + r_ref[...]).astype(BF16)

    return pl.pallas_call(
        body, name=name,
        grid_spec=pltpu.PrefetchScalarGridSpec(
            num_scalar_prefetch=1, grid=(n, nt),
            in_specs=[pl.BlockSpec((None, tr, width), lambda s, t, c: (s, c[0] * nt + t, 0)),
                      pl.BlockSpec((None, tr, width), lambda s, t, c: (s, t, 0))],
            out_specs=pl.BlockSpec((None, tr, width), lambda s, t, c: (s, t, 0))),
        out_shape=SDS(r.shape, BF16),
        compiler_params=_cp(("parallel", "parallel")),
    )(cidx, g, r)


class _ChipExchange:
    def __init__(self, xrow, pw, po=None):
        self.xrow = xrow
        self.arrays = [pw] if po is None else [pw, po]
        self.out_shape = [SDS((4,) + a.shape[1:], BF16) for a in self.arrays]
        self.scratch = [pltpu.SemaphoreType.DMA((3,)) for _ in range(2 * len(self.arrays))]

    def _each(self, ins, outs, sems):
        x, y, c, j = _place()
        for k in (1, 2, 3):
            px, py, pj = _chip_of(x, y, k)
            yield (px == self.xrow, x == self.xrow,
                   _remote(ins[0].at[py], outs[0].at[j], sems[0].at[k - 1], sems[1].at[k - 1], (px, py, c)),
                   _remote(ins[0].at[0], outs[0].at[pj], sems[0].at[k - 1], sems[1].at[k - 1], (x, y, c)))
            if len(ins) == 2:
                yield (None, None,
                       _remote(ins[1].at[pj], outs[1].at[j], sems[2].at[k - 1], sems[3].at[k - 1], (px, py, c)),
                       _remote(ins[1].at[pj], outs[1].at[pj], sems[2].at[k - 1], sems[3].at[k - 1], (x, y, c)))

    def start(self, ins, outs, sems):
        for sends, _, send, _ in self._each(ins, outs, sems):
            if sends is None:
                send.start()
            else:
                pl.when(sends)(send.start)

    def finish(self, ins, outs, sems):
        for sends, owner, send, arrival in self._each(ins, outs, sems):
            if sends is None:
                arrival.wait_recv()
                send.wait_send()
            else:
                pl.when(owner)(arrival.wait_recv)
                pl.when(sends)(send.wait_send)


def _small_exchange(small):
    def body(sm_ref, rs_ref, send, recv, lsem):
        x, y, c, j = _place()
        me = 2 * j + c
        local = pltpu.make_async_copy(sm_ref, rs_ref.at[me], lsem)
        local.start()
        cps = []
        for k in range(1, 8):
            px, py, _ = _chip_of(x, y, k >> 1)
            pc = 1 - c if k & 1 else c
            cps.append(_remote(sm_ref, rs_ref.at[me], send.at[k - 1], recv.at[k - 1], (px, py, pc)))
        for cp in cps:
            cp.start()
        for k in range(1, 8):
            _, _, pj = _chip_of(x, y, k >> 1)
            pc = 1 - c if k & 1 else c
            _remote(sm_ref, rs_ref.at[2 * pj + pc], send.at[k - 1], recv.at[k - 1], (x, y, c)).wait_recv()
        for cp in cps:
            cp.wait_send()
        local.wait()

    return pl.pallas_call(
        body, name="small_exchange", in_specs=[ANY], out_specs=ANY,
        out_shape=SDS((8,) + small.shape, F32),
        scratch_shapes=[pltpu.SemaphoreType.DMA((7,)), pltpu.SemaphoreType.DMA((7,)), pltpu.SemaphoreType.DMA],
        compiler_params=pltpu.CompilerParams(has_side_effects=True),
    )(small)


def _slot_sum(r, name):
    n, rows, width = r.shape
    tr = min(rows, 256)

    def body(r_ref, o_ref):
        acc = r_ref[0].astype(F32)
        for s in range(1, n):
            acc = acc + r_ref[s].astype(F32)
        o_ref[...] = acc

    return pl.pallas_call(
        body, name=name, grid=(rows // tr,),
        in_specs=[pl.BlockSpec((n, tr, width), lambda t: (0, t, 0))],
        out_specs=pl.BlockSpec((tr, width), lambda t: (t, 0)),
        out_shape=SDS((rows, width), F32),
        compiler_params=_cp(("parallel",)),
    )(r)


def _chip_sum(chip_idx, recv, own, name):
    n, rows, width = recv.shape
    tr = min(rows, 256)

    def body(j_ref, r_ref, own_ref, o_ref):
        acc = None
        for s in range(n):
            term = jnp.where(j_ref[0] == s, own_ref[...], r_ref[s]).astype(F32)
            acc = term if acc is None else acc + term
        o_ref[...] = acc

    return pl.pallas_call(
        body, name=name,
        grid_spec=pltpu.PrefetchScalarGridSpec(
            num_scalar_prefetch=1, grid=(rows // tr,),
            in_specs=[pl.BlockSpec((n, tr, width), lambda t, j: (0, t, 0)),
                      pl.BlockSpec((None, tr, width), lambda t, j: (j[0], t, 0))],
            out_specs=pl.BlockSpec((tr, width), lambda t, j: (t, 0))),
        out_shape=SDS((rows, width), F32),
        compiler_params=_cp(("parallel",)),
    )(chip_idx, recv, own)


def _chip_sum_rows(place, recv0, own0, recv1, own1, name):
    n, rows, width = recv0.shape
    tr = min(rows, 256)

    def body(p_ref, r0_ref, o0_ref, r1_ref, o1_ref, o_ref):
        first_row = p_ref[1] == 0
        own = jnp.where(first_row, o0_ref[...], o1_ref[...])
        acc = None
        for s in range(n):
            term = jnp.where(p_ref[0] == s, own, jnp.where(first_row, r0_ref[s], r1_ref[s])).astype(F32)
            acc = term if acc is None else acc + term
        o_ref[...] = acc

    recv = pl.BlockSpec((n, tr, width), lambda t, p: (0, t, 0))
    own = pl.BlockSpec((None, tr, width), lambda t, p: (p[2], t, 0))
    return pl.pallas_call(
        body, name=name,
        grid_spec=pltpu.PrefetchScalarGridSpec(
            num_scalar_prefetch=1, grid=(rows // tr,), in_specs=[recv, own, recv, own],
            out_specs=pl.BlockSpec((tr, width), lambda t, p: (t, 0))),
        out_shape=SDS((rows, width), F32),
        compiler_params=_cp(("parallel",)),
    )(place, recv0, own0, recv1, own1)


def _half_exchange(hw, ho):
    def body(hw_ref, ho_ref, tw_ref, to_ref, send, recv):
        x, y, c, _ = _place()
        sib = (x, y, 1 - c)
        cps = [_remote(hw_ref, tw_ref, send.at[0], recv.at[0], sib),
               _remote(ho_ref, to_ref, send.at[1], recv.at[1], sib)]
        for cp in cps:
            cp.start()
        for cp in cps:
            cp.wait()

    return pl.pallas_call(
        body, name="half_exchange", in_specs=[ANY, ANY], out_specs=[ANY, ANY],
        out_shape=[SDS(hw.shape, F32), SDS(ho.shape, F32)],
        scratch_shapes=[pltpu.SemaphoreType.DMA((2,)), pltpu.SemaphoreType.DMA((2,))],
        compiler_params=pltpu.CompilerParams(has_side_effects=True),
    )(hw, ho)


def _by_core(c, mine, theirs):
    return jnp.where(c == 0, jnp.concatenate([mine, theirs], axis=0), jnp.concatenate([theirs, mine], axis=0))


def _adamw(w, g, m, v, name):
    rows, width = w.shape
    tr = min(rows, 256)

    def body(w_ref, g_ref, m_ref, v_ref, d_ref, nm_ref, nv_ref):
        gv = g_ref[...]
        nm = ADAM_B1 * m_ref[...] + (1.0 - ADAM_B1) * gv
        nv = ADAM_B2 * v_ref[...] + (1.0 - ADAM_B2) * (gv * gv)
        m_hat = nm / (1.0 - ADAM_B1 ** ADAM_STEP)
        v_hat = nv / (1.0 - ADAM_B2 ** ADAM_STEP)
        d_ref[...] = -ADAM_LR * (m_hat / (jnp.sqrt(v_hat) + ADAM_EPS) + ADAM_WD * w_ref[...])
        nm_ref[...] = nm
        nv_ref[...] = nv

    t = pl.BlockSpec((tr, width), lambda i: (i, 0))
    return pl.pallas_call(
        body, name=name, grid=(rows // tr,), in_specs=[t] * 4, out_specs=[t] * 3,
        out_shape=[SDS(w.shape, F32)] * 3, compiler_params=_cp(("parallel",)),
    )(w, g, m, v)


def _rows128(a, rows):
    flat = a.reshape(-1)
    return jnp.pad(flat, (0, rows * LANE - flat.shape[0])).reshape(rows, LANE)


def _pack_small(conv_w, norm_pre, conv_b, ssm_norm, norm_post, dtb, alog, dsk, extra=None):
    cw_rows = 48 if conv_w.shape[-1] == 1536 else 16
    extra = jnp.zeros((1, LANE), F32) if extra is None else _rows128(extra, 1)
    vec = jnp.concatenate([_rows128(dtb, 1), _rows128(alog, 1), _rows128(dsk, 1), extra, jnp.zeros((4, LANE), F32)],
                          axis=0)
    return jnp.concatenate([_rows128(conv_w, cw_rows), _rows128(norm_pre, 8), _rows128(conv_b, 16),
                            _rows128(ssm_norm, 8), _rows128(norm_post, 8), vec], axis=0)


def _unpack_small(p, cw_cols):
    cw_rows = 48 if cw_cols == 1536 else 16
    o = cw_rows
    conv_w = p[:cw_rows].reshape(-1)[:4 * cw_cols].reshape(1, 4, cw_cols)
    norm_pre = p[o:o + 8].reshape(1, D)
    conv_b = p[o + 8:o + 24].reshape(-1)[:1536].reshape(1, 1536)
    ssm_norm = p[o + 24:o + 32].reshape(1, D)
    norm_post = p[o + 32:o + 40].reshape(1, D)
    vec = p[o + 40:o + 48]
    return conv_w, norm_pre, conv_b, ssm_norm, norm_post, vec[0:1, :NH], vec[1:2, :NH], vec[2:3, :NH], vec[3, 0]


def _pad_lanes(a):
    return jnp.pad(a, ((0, 0), (0, LANE - a.shape[1])))


class _GradReduce:
    SPLIT = 2 * SHARD - OFF_G

    def __init__(self, xi, yi, ci):
        self.ci = ci
        self.cidx = jnp.reshape(ci, (1,)).astype(jnp.int32)
        self.place = jnp.stack([2 * xi + yi, xi, yi]).astype(jnp.int32)

    def first(self, dw_g, dw_z, dw_x, dw_out):
        cols = jnp.concatenate([dw_g[:, self.SPLIT:], dw_z, dw_x], axis=1)
        gw = jnp.stack([cols[:, :SHARD], cols[:, SHARD:2 * SHARD]])
        go = dw_out.reshape(4, D // 2, D)
        rw, ro = _pair_exchange([gw, go], "pair_exchange_hi")
        self.pw_hi = _pair_sum(self.cidx, gw, rw, "pair_sum_hi")
        self.po = _pair_sum(self.cidx, go, ro, "pair_sum_out")
        return _ChipExchange(1, self.pw_hi, self.po)

    def first_done(self, got):
        self.rw_hi, self.ro = got

    def second(self, dw_q, dw_k, dw_v, dw_g):
        cols = jnp.concatenate([dw_q, dw_k, dw_v, dw_g[:, :self.SPLIT]], axis=1)
        gw = jnp.stack([cols[:, :SHARD], cols[:, SHARD:]])
        (rw,) = _pair_exchange([gw], "pair_exchange_lo")
        self.pw_lo = _pair_sum(self.cidx, gw, rw, "pair_sum_lo")
        return _ChipExchange(0, self.pw_lo)

    def second_done(self, got):
        (self.rw_lo,) = got

    def result(self):
        half_in = _chip_sum_rows(self.place, self.rw_lo, self.pw_lo, self.rw_hi, self.pw_hi, "chip_sum_in")
        half_out = _chip_sum(self.place[0:1], self.ro, self.po, "chip_sum_out")
        their_in, their_out = _half_exchange(half_in, half_out)
        return _by_core(self.ci, half_in, their_in), _by_core(self.ci, half_out, their_out)


def kernel(x, norm_pre_w, w_in, conv_w, conv_b, dt_bias, a_log, d_skip, ssm_norm_w, w_out, norm_post_w, loss_target, m_norm_pre_w, m_w_in, m_conv_w, m_conv_b, m_dt_bias, m_a_log, m_d_skip, m_ssm_norm_w, m_w_out, m_norm_post_w, v_norm_pre_w, v_w_in, v_conv_w, v_conv_b, v_dt_bias, v_a_log, v_d_skip, v_ssm_norm_w, v_w_out, v_norm_post_w):
    xi, yi, ci = lax.axis_index("x"), lax.axis_index("y"), lax.axis_index("c")
    chip = 2 * xi + yi
    x2, tgt = x[0], loss_target[0]

    w_in_b, w_out_b = w_in[0].astype(BF16), w_out[0].astype(BF16)
    gin, gout, gcw = _gather_weights(w_in_b, w_out_b, conv_w[0])

    def whole(own, gathered, axis):
        return jnp.concatenate([jnp.where(chip == k, own, gathered[k]) for k in range(4)], axis=axis)

    w_all = jnp.concatenate([whole(w_in_b, gin, 1), jnp.zeros((D, DP - 4 * SHARD), BF16)], axis=1)
    w_out_all = whole(w_out_b, gout, 0)
    cw_all = whole(conv_w[0], gcw, 1)
    reduce = _GradReduce(xi, yi, ci)
    grad_x, small = _local_step(x2, tgt, w_all, w_out_all, cw_all, norm_pre_w, conv_b, dt_bias, a_log, d_skip,
                                ssm_norm_w, norm_post_w, reduce)[:2]
    g_in, g_out = reduce.result()
    g_small = _slot_sum(_small_exchange(small), "small_sum")
    g_cw, g_npre, g_cb, g_nssm, g_npost, g_dtb, g_alog, g_dsk, loss = _unpack_small(g_small, 1536)
    g_cw = lax.dynamic_slice_in_dim(g_cw, chip * 384, 384, axis=2)

    d_in, nm_in, nv_in = _adamw(w_in[0], g_in, m_w_in[0], v_w_in[0], "adamw_in")
    d_out, nm_out, nv_out = _adamw(w_out[0], g_out, m_w_out[0], v_w_out[0], "adamw_out")
    packed = [_pack_small(*t) for t in (
        (conv_w, norm_pre_w, conv_b, ssm_norm_w, norm_post_w, dt_bias, a_log, d_skip),
        (g_cw, g_npre, g_cb, g_nssm, g_npost, g_dtb, g_alog, g_dsk),
        (m_conv_w, m_norm_pre_w, m_conv_b, m_ssm_norm_w, m_norm_post_w, m_dt_bias, m_a_log, m_d_skip),
        (v_conv_w, v_norm_pre_w, v_conv_b, v_ssm_norm_w, v_norm_post_w, v_dt_bias, v_a_log, v_d_skip))]
    small_out = [_unpack_small(p, 384)[:8] for p in _adamw(*packed, "adamw_small")]

    def ordered(cw_, npre, cb_, nssm, npost, dtb_, alog_, dsk_, big_in, big_out):
        return [npre, big_in[None], cw_, cb_, dtb_, alog_, dsk_, nssm, big_out[None], npost]

    grads = ordered(g_cw, g_npre, g_cb, g_nssm, g_npost, g_dtb, g_alog, g_dsk, g_in, g_out)
    deltas = ordered(*small_out[0], d_in, d_out)
    new_m = ordered(*small_out[1], nm_in, nm_out)
    new_v = ordered(*small_out[2], nv_in, nv_out)
    return (loss, grad_x[None], *grads, *deltas, *new_m, *new_v)


def _local_step(x2, tgt, w_all, w_out_all, cw_all, norm_pre_w, conv_b, dt_bias, a_log, d_skip, ssm_norm_w,
                norm_post_w, reduce=None):
    dtb, alog = _pad_lanes(dt_bias), _pad_lanes(a_log)
    d_b = jnp.repeat(d_skip, 64, axis=1)

    proj, u = _inproj_fwd(x2, norm_pre_w, w_all)
    o1, l1 = _attn_fwd(proj, 1)
    o2, l2 = _attn_fwd(proj, 4)
    o3, l3 = _attn_fwd(proj, 16)
    mix, attn_pre, lse = _attn_merge(o1, o2, o3, l1, l2, l3, proj)
    mix, y_save, states = _ssm_fwd(proj, mix, cw_all, conv_b, dtb, alog, d_b, ssm_norm_w)

    dy, dmix, dw_out, dnw_post, loss_part = _outproj_loss(mix, w_out_all, x2, tgt, norm_post_w)
    do, delta, dg = _attn_gate_bwd(dmix, attn_pre, proj)
    dz, dxbcdt, dcw, dcb, dvec, dnw_ssm = _ssm_bwd(proj, dmix, y_save, states, cw_all, conv_b, dtb, alog, d_b,
                                                   ssm_norm_w)
    dw_g, dw_z, dw_x = _dw(u, dg, "dw_in_g"), _dw(u, dz, "dw_in_z"), _dw(u, dxbcdt, "dw_in_xbcdt")
    acc = _attn_bwd(proj, do, lse, delta, 1, None, F32, reduce.first(dw_g, dw_z, dw_x, dw_out) if reduce else None)
    if reduce:
        acc, got = acc
        reduce.first_done(got)
    acc = _attn_bwd(proj, do, lse, delta, 4, acc, F32)
    dq, dk, dv = _attn_bwd(proj, do, lse, delta, 16, acc, BF16)
    dw_q, dw_k, dw_v = _dw(u, dq, "dw_in_q"), _dw(u, dk, "dw_in_k"), _dw(u, dv, "dw_in_v")
    res = _inproj_bwd_dx([dq, dk, dv, dg, dz], dxbcdt, w_all, x2, dy, norm_pre_w,
                         reduce.second(dw_q, dw_k, dw_v, dw_g) if reduce else None)
    if reduce:
        res, got = res
        reduce.second_done(got)
    grad_x, dnw_pre = res
    small = _pack_small(dcw, dnw_pre, dcb, dnw_ssm, dnw_post, dvec[0:1, :NH], dvec[1:2, :NH], dvec[2:3, :NH],
                        loss_part[:, :1])
    dw_all = jnp.concatenate([dw_q, dw_k, dw_v, dw_g, dw_z, dw_x], axis=1)
    return grad_x, small, dw_all, dw_out
```

```python
import functools

import jax
import jax.numpy as jnp
from jax import lax
from jax.experimental import pallas as pl
from jax.experimental.pallas import tpu as pltpu

F32 = jnp.float32
BF16 = jnp.bfloat16
MESH = pl.DeviceIdType.MESH
SDS = jax.ShapeDtypeStruct
ANY = pl.BlockSpec(memory_space=pl.ANY)

S = 4096
D = 1024
DP = 7168
SHARD = 1668
OFF_G, OFF_Z = 3072, 4096
NH = 16
CH = 128
NC = S // CH
EPS = 1e-6
NEG = -1e30
LANE = 128
VMEM_LIMIT = 48 * 1024 * 1024

ADAM_LR, ADAM_B1, ADAM_B2, ADAM_EPS, ADAM_WD, ADAM_STEP = 0.001, 0.9, 0.999, 1e-08, 0.01, 10


def _cp(sem, **kw):
    return pltpu.CompilerParams(dimension_semantics=sem, vmem_limit_bytes=VMEM_LIMIT, **kw)


def _dot(a, b):
    return jnp.dot(a, b, preferred_element_type=F32)


def _dot_nt(a, b):
    return lax.dot_general(a, b, (((1,), (1,)), ((), ())), preferred_element_type=F32)


def _dot_tn(a, b):
    return lax.dot_general(a, b, (((0,), (0,)), ((), ())), preferred_element_type=F32)


def _pieces(x, n):
    out = []
    for _ in range(n):
        p = x.astype(BF16)
        out.append(p)
        x = x - p.astype(F32)
    return out


def _pick(x, sel, n=2):
    parts = [_dot(p, sel) for p in _pieces(x, n)]
    return functools.reduce(jnp.add, parts)


def _pick_left(sel, x, n=3):
    parts = [_dot(sel, p) for p in _pieces(x, n)]
    return functools.reduce(jnp.add, parts)


def _sigmoid(v):
    return 0.5 * jnp.tanh(0.5 * v) + 0.5


def _iota(shape, dim):
    return lax.broadcasted_iota(jnp.int32, shape, dim)


def _inproj_fwd(x, nw, w_all):
    tm, tn = 1024, 1024

    def body(x_ref, nw_ref, w_ref, proj_ref, u_ref):
        @pl.when(pl.program_id(1) == 0)
        def _():
            xf = x_ref[...]
            r = lax.rsqrt(jnp.mean(xf * xf, axis=-1, keepdims=True) + EPS)
            u_ref[...] = (xf * r * nw_ref[...]).astype(BF16)

        proj_ref[...] = _dot(u_ref[...], w_ref[...])

    return pl.pallas_call(
        body, name="inproj_fwd", grid=(S // tm, DP // tn),
        in_specs=[pl.BlockSpec((tm, D), lambda i, j: (i, 0)), pl.BlockSpec((1, D), lambda i, j: (0, 0)),
                  pl.BlockSpec((D, tn), lambda i, j: (0, j))],
        out_specs=[pl.BlockSpec((tm, tn), lambda i, j: (i, j)), pl.BlockSpec((tm, D), lambda i, j: (i, 0))],
        out_shape=[SDS((S, DP), F32), SDS((S, D), BF16)],
        compiler_params=_cp(("parallel", "arbitrary")),
    )(x, nw, w_all)


ATTN_QB = {1: 4, 4: 1, 16: 1}


def _unit_rows(r, u, d):
    return pl.ds(r + d * CH * u, CH, stride=d) if d > 1 else pl.ds(CH * u, CH)


def _for_units(d, qb, fn):
    for r in range(d):
        for u in range(qb):
            fn(r, u)


def _attn_mask(has_prev):
    qi, kj = _iota((2 * CH, 2 * CH), 0) & (CH - 1), _iota((2 * CH, 2 * CH), 1)
    cur_ok = (kj >= CH) & (kj - CH <= qi)
    prev_ok = (kj < CH) & (kj >= qi)
    return cur_ok | (prev_ok & has_prev)


def _stack_heads(v, lane_a):
    return jnp.concatenate([jnp.where(lane_a, v, 0.0), jnp.where(lane_a, 0.0, v)], axis=0).astype(BF16)


def _unit_kv(p_ref, c_ref, r, u, d, qb):
    prev = p_ref[_unit_rows(r, 0, d), :] if u == 0 else c_ref[_unit_rows(r, u - 1, d), :]
    return jnp.concatenate([prev, c_ref[_unit_rows(r, u, d), :]], axis=0).astype(BF16)


def _attn_specs(d, qb, lag_out):
    rows, prows = CH * d * qb, CH * d
    nb = S // rows
    last = nb - 1

    def cur(off):
        return pl.BlockSpec((rows, LANE), lambda c, i: (jnp.minimum(i, last), off + c))

    def prev(off):
        return pl.BlockSpec((prows, LANE), lambda c, i: (jnp.clip(i * qb - 1, 0, S // prows - 1), off + c))

    lag = pl.BlockSpec((rows, LANE), lambda c, i: (jnp.clip(i - 1, 0, last), c)) if lag_out else None
    return nb, cur, prev, lag


def _attn_fwd(proj, d, prior=None, final=False):
    qb = ATTN_QB[d]
    nb, cur, prev, _ = _attn_specs(d, qb, False)
    n_prior = 2 if prior is not None else 0

    def body(*refs):
        q_ref, kp_ref, kc_ref, vp_ref, vc_ref = refs[:5]
        prior_refs, refs = refs[5:5 + n_prior], refs[5 + n_prior:]
        if final:
            g_ref, mix_ref, o_ref, l_ref = refs
        else:
            o_ref, l_ref = refs
        i = pl.program_id(1)
        lane_a = _iota((CH, LANE), 1) < 64
        mask_first, mask_rest = _attn_mask(i > 0), _attn_mask(True)

        def unit(r, u):
            sl = _unit_rows(r, u, d)
            q2 = _stack_heads(q_ref[sl, :] * 0.125, lane_a)
            k2, v2 = _unit_kv(kp_ref, kc_ref, r, u, d, qb), _unit_kv(vp_ref, vc_ref, r, u, d, qb)
            s = jnp.where(mask_first if u == 0 else mask_rest, _dot_nt(q2, k2), NEG)
            m = jnp.max(s, axis=1, keepdims=True)
            p = jnp.exp(s - m)
            l = jnp.sum(p, axis=1, keepdims=True)
            o2 = _dot(p.astype(BF16), v2) / l
            lse2 = m + jnp.log(l)
            o = jnp.where(lane_a, o2[:CH], o2[CH:])
            lse = jnp.where(lane_a, lse2[:CH], lse2[CH:])
            if n_prior:
                o_a, l_a = prior_refs[0][sl, :], prior_refs[1][sl, :]
                top = jnp.maximum(l_a, lse)
                e_a, e_b = jnp.exp(l_a - top), jnp.exp(lse - top)
                tot = e_a + e_b
                o = (e_a * o_a + e_b * o) / tot
                lse = top + jnp.log(tot)
            o_ref[sl, :] = o
            l_ref[sl, :] = lse
            if final:
                g = g_ref[sl, :]
                mix_ref[sl, :] = (o * (g * _sigmoid(g))).astype(BF16)

        _for_units(d, qb, unit)

    in_specs = [cur(0), prev(8), cur(8), prev(16), cur(16)] + [cur(0)] * n_prior
    args = [proj] * 5 + (list(prior) if n_prior else [])
    out_specs, out_shape = [cur(0), cur(0)], [SDS((S, D), F32), SDS((S, D), F32)]
    if final:
        assert d == 1
        in_specs.append(cur(OFF_G // LANE))
        args.append(proj)
        out_specs, out_shape = [cur(0)] + out_specs, [SDS((S, 2 * D), BF16)] + out_shape
    return pl.pallas_call(
        body, name=f"attn_fwd_d{d}", grid=(NH // 2, nb),
        in_specs=in_specs, out_specs=out_specs, out_shape=out_shape,
        compiler_params=_cp(("parallel", "arbitrary")),
    )(*args)


def _attn_gate_bwd(dmix, pre, proj):
    tm = 512

    def body(dm_ref, pre_ref, g_ref, do_ref, delta_ref, dg_ref):
        g, dm, pre_v = g_ref[...], dm_ref[...], pre_ref[...]
        sig = _sigmoid(g)
        do = dm * (g * sig)
        do_ref[...] = do
        dg_ref[...] = (dm * pre_v * (sig * (1.0 + g * (1.0 - sig)))).astype(BF16)
        prod = do * pre_v
        same_head = (_iota((LANE, LANE), 0) // 64 == _iota((LANE, LANE), 1) // 64).astype(BF16)
        for cb in range(D // LANE):
            delta_ref[:, cb * LANE:(cb + 1) * LANE] = _pick(prod[:, cb * LANE:(cb + 1) * LANE], same_head)

    t = pl.BlockSpec((tm, D), lambda i: (i, 0))
    return pl.pallas_call(
        body, name="attn_gate_bwd", grid=(S // tm,),
        in_specs=[t, t, pl.BlockSpec((tm, D), lambda i: (i, OFF_G // D))],
        out_specs=[t, t, t],
        out_shape=[SDS((S, D), F32), SDS((S, D), F32), SDS((S, D), BF16)],
        compiler_params=_cp(("parallel",)),
    )(dmix, pre, proj)


def _attn_bwd(proj, do, lse, delta, d, acc, out_dtype):
    qb = ATTN_QB[d]
    nb, cur, prev, lag = _attn_specs(d, qb, True)
    has_acc = acc is not None

    def body(*refs):
        q_ref, kp_ref, kc_ref, vp_ref, vc_ref, do_ref, lse_ref, dl_ref = refs[:8]
        if has_acc:
            aq_ref, ak_ref, av_ref = refs[8:11]
        tail = refs[11:] if has_acc else refs[8:]
        dq_ref, dk_ref, dv_ref, ck_ref, cv_ref = tail[:5]
        dq_f32 = dq_ref if out_dtype == F32 else tail[5]
        i = pl.program_id(1)
        slot = i & 1
        now_k, now_v, old_k, old_v = ck_ref.at[slot], cv_ref.at[slot], ck_ref.at[1 - slot], cv_ref.at[1 - slot]
        lane_a = _iota((CH, LANE), 1) < 64
        mask_first, mask_rest = _attn_mask(i > 0), _attn_mask(True)

        @pl.when(i == 0)
        def _():
            ck_ref[1] = jnp.zeros((ck_ref.shape[1], LANE), F32)
            cv_ref[1] = jnp.zeros((cv_ref.shape[1], LANE), F32)

        def unit(r, u):
            sl = _unit_rows(r, u, d)
            q2 = _stack_heads(q_ref[sl, :] * 0.125, lane_a)
            do2 = _stack_heads(do_ref[sl, :], lane_a)
            k2, v2 = _unit_kv(kp_ref, kc_ref, r, u, d, qb), _unit_kv(vp_ref, vc_ref, r, u, d, qb)
            lsev, dlv = lse_ref[sl, :], dl_ref[sl, :]
            lse2 = jnp.concatenate([lsev[:, 0:1], lsev[:, 64:65]], axis=0)
            dl2 = jnp.concatenate([dlv[:, 0:1], dlv[:, 64:65]], axis=0)
            p = jnp.exp(jnp.where(mask_first if u == 0 else mask_rest, _dot_nt(q2, k2), NEG) - lse2)
            ds = (p * (_dot_nt(do2, v2) - dl2)).astype(BF16)
            dq2 = _dot(ds, k2)
            dk2 = _dot_tn(ds, q2)
            dv2 = _dot_tn(p.astype(BF16), do2)
            dq = jnp.where(lane_a, dq2[:CH], dq2[CH:]) * 0.125
            if has_acc:
                dq = dq + aq_ref[sl, :]
            dq_f32[sl, :] = dq
            if u == 0:
                before = _unit_rows(r, qb - 1, d)
                old_k[before, :] += dk2[:CH]
                old_v[before, :] += dv2[:CH]
            else:
                before = _unit_rows(r, u - 1, d)
                now_k[before, :] += dk2[:CH]
                now_v[before, :] += dv2[:CH]
            now_k[sl, :] = dk2[CH:]
            now_v[sl, :] = dv2[CH:]

        @pl.when(i < nb)
        def _():
            _for_units(d, qb, unit)
            if out_dtype != F32:
                dq_ref[...] = dq_f32[...].astype(out_dtype)

        dk, dv = old_k[...], old_v[...]
        if has_acc:
            dk, dv = dk + ak_ref[...], dv + av_ref[...]
        dk_ref[...] = dk.astype(out_dtype)
        dv_ref[...] = dv.astype(out_dtype)

    in_specs = [cur(0), prev(8), cur(8), prev(16), cur(16), cur(0), cur(0), cur(0)]
    args = [proj, proj, proj, proj, proj, do, lse, delta]
    if has_acc:
        in_specs += [cur(0), lag, lag]
        args += list(acc)
    rows = CH * d * qb
    scratch = [pltpu.VMEM((2, rows, LANE), F32), pltpu.VMEM((2, rows, LANE), F32)]
    if out_dtype != F32:
        scratch.append(pltpu.VMEM((rows, LANE), F32))
    return pl.pallas_call(
        body, name=f"attn_bwd_d{d}", grid=(NH // 2, nb + 1),
        in_specs=in_specs, out_specs=[cur(0), lag, lag], out_shape=[SDS((S, D), out_dtype)] * 3,
        scratch_shapes=scratch, compiler_params=_cp(("parallel", "arbitrary")),
    )(*args)


def _conv_taps(cur, prev8, first):
    row8 = _iota(prev8.shape, 0)
    prev8 = jnp.where(first, 0.0, prev8)
    taps = []
    for s in (3, 2, 1):
        rolled = pltpu.roll(cur, s, 0)
        head = jnp.where(row8 < s, pltpu.roll(prev8, s, 0), rolled[:8])
        taps.append(jnp.concatenate([head, rolled[8:]], axis=0))
    return taps + [cur]


def _conv(taps, w, b):
    acc = b + w[0:1, :] * taps[0]
    for k in (1, 2, 3):
        acc = acc + w[k:k + 1, :] * taps[k]
    return acc


def _expand():
    return (_iota((LANE, D), 1) // 64 == _iota((LANE, D), 0)).astype(BF16)


def _reduce():
    return (_iota((D, LANE), 0) // 64 == _iota((D, LANE), 1)).astype(BF16)


def _ssd_common(xs_raw, xs_prev, bc_raw, bc_prev, dt_raw, first, cw, cb, dtb, alog):
    head_lane = _iota((CH, LANE), 1) < NH
    xs_taps = _conv_taps(xs_raw, xs_prev, first)
    bc_taps = _conv_taps(bc_raw, bc_prev, first)
    xs_c = _conv(xs_taps, cw[:, :D], cb[:, :D])
    bc_c = _conv(bc_taps, cw[:, D:], cb[:, D:])
    xs = xs_c * _sigmoid(xs_c)
    bc = bc_c * _sigmoid(bc_c)
    pre = dt_raw + dtb
    dt = jnp.where(head_lane, jnp.maximum(pre, 0.0) + jnp.log(1.0 + jnp.exp(-jnp.abs(pre))), 0.0)
    a_row = jnp.where(head_lane[0:1], -jnp.exp(alog), 0.0)
    tri = (_iota((CH, CH), 1) <= _iota((CH, CH), 0)).astype(BF16)
    cs = _pick_left(tri, dt * a_row)
    cs_last = cs[CH - 1:CH, :]
    wide = _pick(jnp.concatenate([dt, jnp.exp(cs), jnp.exp(cs_last - cs)], axis=0), _expand())
    dt_b, e_b, f_b = wide[:CH], wide[CH:2 * CH], wide[2 * CH:]
    return dict(xs_taps=xs_taps, bc_taps=bc_taps, xs_c=xs_c, bc_c=bc_c, xs=xs, bc=bc, pre=pre, dt=dt,
                a_row=a_row, cs=cs, cs_t=cs.T, dt_b=dt_b, e_b=e_b, f_b=f_b, t_b=e_b[CH - 1:CH, :])


def _groups(bc):
    bcb = bc.astype(BF16)
    return [bcb[:, 0:128], bcb[:, 128:256]], [bcb[:, 256:384], bcb[:, 384:512]]


def _decay(q, h, tril):
    seg = q["cs"][:, h:h + 1] - q["cs_t"][h:h + 1, :]
    return jnp.exp(jnp.where(tril, seg, NEG))


def _ssm_fwd(proj, mix, cw, cb, dtb, alog, d_b, nw):
    def body(xs_ref, xsp_ref, bc_ref, bcp_ref, dt_ref, z_ref, cw_ref, cb_ref, dtb_ref, alog_ref, db_ref, nw_ref,
             mix_in_ref, mix_ref, y_ref, st_ref, h_ref):
        del mix_in_ref
        i = pl.program_id(0)

        @pl.when(i == 0)
        def _():
            h_ref[...] = jnp.zeros_like(h_ref)

        q = _ssd_common(xs_ref[...], xsp_ref[...], bc_ref[...], bcp_ref[...], dt_ref[...], i == 0,
                        cw_ref[...], cb_ref[...], dtb_ref[...], alog_ref[...])
        bg, cg = _groups(q["bc"])
        xs = q["xs"]
        xdt = xs * q["dt_b"]
        xdt_b = xdt.astype(BF16)
        h_in = h_ref[...]
        st_ref[...] = h_in
        hb = h_in.astype(BF16)
        tril = _iota((CH, CH), 1) <= _iota((CH, CH), 0)
        lane_a = _iota((CH, LANE), 1) < 64
        cbm = [_dot_nt(cg[g], bg[g]) for g in range(2)]
        pairs = []
        for hp in range(NH // 2):
            xp = xdt_b[:, hp * LANE:(hp + 1) * LANE]
            ya = _dot((cbm[hp // 4] * _decay(q, 2 * hp, tril)).astype(BF16), xp)
            yb = _dot((cbm[hp // 4] * _decay(q, 2 * hp + 1, tril)).astype(BF16), xp)
            pairs.append(jnp.where(lane_a, ya, yb))
        y_diag = jnp.concatenate(pairs, axis=1)
        y_off = jnp.concatenate([_dot(cg[g], hb[:, g * 512:(g + 1) * 512]) for g in range(2)], axis=1) * q["e_b"]
        y = y_diag + y_off + db_ref[...] * xs
        y_ref[...] = y
        xf = (xdt * q["f_b"]).astype(BF16)
        h_ref[...] = q["t_b"] * h_in + jnp.concatenate(
            [_dot_tn(bg[g], xf[:, g * 512:(g + 1) * 512]) for g in range(2)], axis=1)
        z = z_ref[...]
        yz = y * (z * _sigmoid(z))
        outs = []
        for g in range(2):
            v = yz[:, g * 512:(g + 1) * 512]
            outs.append(v * lax.rsqrt(jnp.mean(v * v, axis=-1, keepdims=True) + EPS))
        mix_ref[...] = (jnp.concatenate(outs, axis=1) * nw_ref[...]).astype(BF16)

    def col(width, blk, prev=False):
        if prev:
            return pl.BlockSpec((8, width), lambda i: (jnp.maximum(i * (CH // 8) - 1, 0), blk))
        return pl.BlockSpec((CH, width), lambda i: (i, blk))

    def full(a):
        return pl.BlockSpec(a.shape, lambda i: (0,) * a.ndim)

    return pl.pallas_call(
        body, name="ssm_fwd", grid=(NC,),
        in_specs=[col(D, 5), col(D, 5, True), col(512, 12), col(512, 12, True), col(LANE, 52), col(D, 4),
                  full(cw), full(cb), full(dtb), full(alog), full(d_b), full(nw), ANY],
        out_specs=[col(D, 1), col(D, 0), pl.BlockSpec((None, CH, D), lambda i: (i, 0, 0))],
        out_shape=[SDS((S, 2 * D), BF16), SDS((S, D), F32), SDS((NC, CH, D), F32)],
        scratch_shapes=[pltpu.VMEM((CH, D), F32)],
        input_output_aliases={12: 0},
        compiler_params=_cp(("arbitrary",)),
    )(proj, proj, proj, proj, proj, proj, cw, cb, dtb, alog, d_b, nw, mix)


def _ssm_bwd(proj, dmix, y_save, states, cw, cb, dtb, alog, d_b, nw):
    def body(xs_ref, xsp_ref, bc_ref, bcp_ref, dt_ref, z_ref, dn_ref, y_ref, st_ref,
             cw_ref, cb_ref, dtb_ref, alog_ref, db_ref, nw_ref,
             dz_ref, dx_ref, dcw_ref, dcb_ref, dsm_ref, dnw_ref, dh_ref, nxs_ref, nbc_ref):
        i = pl.program_id(0)
        ci = NC - 1 - i

        @pl.when(i == 0)
        def _():
            for ref in (dcw_ref, dcb_ref, dsm_ref, dnw_ref, dh_ref, nxs_ref, nbc_ref):
                ref[...] = jnp.zeros_like(ref)

        cw, cb = cw_ref[...], cb_ref[...]
        q = _ssd_common(xs_ref[...], xsp_ref[...], bc_ref[...], bcp_ref[...], dt_ref[...], ci == 0,
                        cw, cb, dtb_ref[...], alog_ref[...])
        bg, cg = _groups(q["bc"])
        xs, dt_b, e_b, f_b, t_b = q["xs"], q["dt_b"], q["e_b"], q["f_b"], q["t_b"]
        xdt = xs * dt_b
        xdt_b = xdt.astype(BF16)
        h_in = st_ref[...]
        hb = h_in.astype(BF16)
        dh_new = dh_ref[...]
        dhb = dh_new.astype(BF16)
        red = _reduce()

        z, y, dn, nw_v = z_ref[...], y_ref[...], dn_ref[...], nw_ref[...]
        sig = _sigmoid(z)
        sz = z * sig
        yz = y * sz
        gdn = dn * nw_v
        dyz, dnw = [], []
        for g in range(2):
            v, gv = yz[:, g * 512:(g + 1) * 512], gdn[:, g * 512:(g + 1) * 512]
            r = lax.rsqrt(jnp.mean(v * v, axis=-1, keepdims=True) + EPS)
            dnw.append(dn[:, g * 512:(g + 1) * 512] * v * r)
            dyz.append(r * (gv - v * (r * r) * jnp.mean(gv * v, axis=-1, keepdims=True)))
        dyz = jnp.concatenate(dyz, axis=1)
        dnw_ref[...] += jnp.sum(jnp.concatenate(dnw, axis=1), axis=0, keepdims=True)
        dy = dyz * sz
        dz_ref[...] = (dyz * y * (sig * (1.0 + z * (1.0 - sig)))).astype(BF16)
        dy_b = dy.astype(BF16)

        tril = _iota((CH, CH), 1) <= _iota((CH, CH), 0)
        lane_a = _iota((CH, LANE), 1) < 64
        cbm = [_dot_nt(cg[g], bg[g]) for g in range(2)]
        dcbm = [jnp.zeros((CH, CH), F32), jnp.zeros((CH, CH), F32)]
        seg_rows = jnp.zeros((CH, LANE), F32)
        seg_cols = jnp.zeros((LANE, CH), F32)
        row_id, col_id = _iota((CH, LANE), 0), _iota((CH, LANE), 1)
        dx_pairs = []
        for hp in range(NH // 2):
            g = hp // 4
            xp = xdt_b[:, hp * LANE:(hp + 1) * LANE]
            dyp_f = dy[:, hp * LANE:(hp + 1) * LANE]
            dyp = dy_b[:, hp * LANE:(hp + 1) * LANE]
            halves = []
            for k in range(2):
                h = 2 * hp + k
                lane = lane_a if k == 0 else jnp.logical_not(lane_a)
                dec = _decay(q, h, tril)
                gm = cbm[g] * dec
                dgm = _dot_nt(jnp.where(lane, dyp_f, 0.0).astype(BF16), xp)
                dcbm[g] = dcbm[g] + dgm * dec
                prod = dgm * gm
                seg_rows = jnp.where(col_id == h, jnp.sum(prod, axis=1, keepdims=True), seg_rows)
                seg_cols = jnp.where(row_id == h, jnp.sum(prod, axis=0, keepdims=True), seg_cols)
                halves.append(_dot_tn(gm.astype(BF16), dyp))
            dx_pairs.append(jnp.where(lane_a, halves[0], halves[1]))
        dxdt_diag = jnp.concatenate(dx_pairs, axis=1)

        qv = jnp.concatenate([_dot(bg[g], dhb[:, g * 512:(g + 1) * 512]) for g in range(2)], axis=1)
        y_off = jnp.concatenate([_dot(cg[g], hb[:, g * 512:(g + 1) * 512]) for g in range(2)], axis=1) * e_b
        xfq = xdt * f_b * qv
        dxdt = dxdt_diag + f_b * qv
        tdt = jnp.sum(dh_new * h_in, axis=0, keepdims=True) * t_b
        per_head = _pick(jnp.concatenate([xfq, dy * y_off, dxdt * xs, dy * xs, jnp.broadcast_to(tdt, (8, D))],
                                         axis=0), red)
        fdf, dyoff_h, dxdtxs_h, dyxs_h = [per_head[k * CH:(k + 1) * CH] for k in range(4)]
        dcs = seg_rows - seg_cols.T + dyoff_h - fdf
        last = per_head[4 * CH:4 * CH + 1] + jnp.sum(fdf, axis=0, keepdims=True)
        dcs = dcs + jnp.where(_iota((CH, LANE), 0) == CH - 1, last, 0.0)
        tri_t = (_iota((CH, CH), 1) >= _iota((CH, CH), 0)).astype(BF16)
        da = _pick_left(tri_t, dcs)
        ddt = da * q["a_row"] + dxdtxs_h
        dxs = dxdt * dt_b + db_ref[...] * dy
        ddt_raw = ddt * _sigmoid(q["pre"])
        dsm_ref[0:1, :] += jnp.sum(ddt_raw, axis=0, keepdims=True)
        dsm_ref[1:2, :] += jnp.sum(da * q["dt"], axis=0, keepdims=True) * q["a_row"]
        dsm_ref[2:3, :] += jnp.sum(dyxs_h, axis=0, keepdims=True)
        edy = (e_b * dy).astype(BF16)
        xf = (xdt * f_b).astype(BF16)
        dbs, dcs_g, dhs = [], [], []
        for g in range(2):
            sl = slice(g * 512, (g + 1) * 512)
            dcb_b = dcbm[g].astype(BF16)
            dcs_g.append(_dot(dcb_b, bg[g]) + _dot_nt(edy[:, sl], hb[:, sl]))
            dbs.append(_dot_tn(dcb_b, cg[g]) + _dot_nt(xf[:, sl], dhb[:, sl]))
            dhs.append(_dot_tn(cg[g], edy[:, sl]))
        dh_ref[...] = t_b * dh_new + jnp.concatenate(dhs, axis=1)
        dbc = jnp.concatenate(dbs + dcs_g, axis=1)

        def conv_bwd(dact, pre, taps, w, nxt_ref, lo):
            s = _sigmoid(pre)
            dconv = dact * (s * (1.0 + pre * (1.0 - s)))
            nxt8 = nxt_ref[...]
            row8 = _iota(nxt8.shape, 0)
            hi = lo + dconv.shape[1]
            dcb_ref[:, lo:hi] += jnp.sum(dconv, axis=0, keepdims=True)
            dx = w[3:4, :] * dconv
            for k in range(4):
                dcw_ref[k:k + 1, lo:hi] += jnp.sum(dconv * taps[k], axis=0, keepdims=True)
            for s_ in (1, 2, 3):
                rolled = pltpu.roll(dconv, CH - s_, 0)
                tail = jnp.where(row8 >= 8 - s_, pltpu.roll(nxt8, 8 - s_, 0), rolled[CH - 8:])
                dx = dx + w[3 - s_:4 - s_, :] * jnp.concatenate([rolled[:CH - 8], tail], axis=0)
            nxt_ref[...] = dconv[:8]
            return dx

        dx_ref[:, 0:D] = conv_bwd(dxs, q["xs_c"], q["xs_taps"], cw[:, :D], nxs_ref, 0).astype(BF16)
        dx_ref[:, D:D + 512] = conv_bwd(dbc, q["bc_c"], q["bc_taps"], cw[:, D:], nbc_ref, D).astype(BF16)
        dx_ref[:, D + 512:D + 640] = ddt_raw.astype(BF16)
        dx_ref[:, D + 640:] = jnp.zeros((CH, D - 640), BF16)

    def col(width, blk, prev=False):
        if prev:
            return pl.BlockSpec((8, width), lambda i: (jnp.maximum((NC - 1 - i) * (CH // 8) - 1, 0), blk))
        return pl.BlockSpec((CH, width), lambda i: (NC - 1 - i, blk))

    def full(a):
        return pl.BlockSpec(a.shape, lambda i: (0,) * len(a.shape))

    acc_shapes = [SDS((4, 1536), F32), SDS((1, 1536), F32), SDS((8, LANE), F32), SDS((1, D), F32)]
    return pl.pallas_call(
        body, name="ssm_bwd", grid=(NC,),
        in_specs=[col(D, 5), col(D, 5, True), col(512, 12), col(512, 12, True), col(LANE, 52), col(D, 4),
                  col(D, 1), col(D, 0), pl.BlockSpec((None, CH, D), lambda i: (NC - 1 - i, 0, 0)),
                  full(cw), full(cb), full(dtb), full(alog), full(d_b), full(nw)],
        out_specs=[col(D, 0), col(2 * D, 0)] + [full(a) for a in acc_shapes],
        out_shape=[SDS((S, D), BF16), SDS((S, 2 * D), BF16)] + acc_shapes,
        scratch_shapes=[pltpu.VMEM((CH, D), F32), pltpu.VMEM((8, D), F32), pltpu.VMEM((8, 512), F32)],
        compiler_params=_cp(("arbitrary",)),
    )(proj, proj, proj, proj, proj, proj, dmix, y_save, states, cw, cb, dtb, alog, d_b, nw)


def _outproj_loss(mix, w_out, x, tgt, nw):
    tm = 256

    def body(mix_ref, w_ref, x_ref, t_ref, nw_ref, dy_ref, dmix_ref, dw_ref, dnw_ref, loss_ref):
        @pl.when(pl.program_id(0) == 0)
        def _():
            dw_ref[...] = jnp.zeros_like(dw_ref)
            dnw_ref[...] = jnp.zeros_like(dnw_ref)
            loss_ref[...] = jnp.zeros_like(loss_ref)

        mixv, w = mix_ref[...], w_ref[...]
        out = _dot(mixv, w)
        r = lax.rsqrt(jnp.mean(out * out, axis=-1, keepdims=True) + EPS)
        nh = out * r
        nw_v = nw_ref[...]
        err = x_ref[...] + nh * nw_v - t_ref[...]
        loss_ref[...] += 0.5 * jnp.sum(jnp.mean(err * err, axis=-1, keepdims=True), axis=0, keepdims=True)
        dy = err * (1.0 / D)
        dy_ref[...] = dy
        dnw_ref[...] += jnp.sum(dy * nh, axis=0, keepdims=True)
        gdn = dy * nw_v
        dout = (r * (gdn - nh * jnp.mean(gdn * nh, axis=-1, keepdims=True))).astype(BF16)
        dmix_ref[...] = _dot_nt(dout, w)
        dw_ref[...] += _dot_tn(mixv, dout)

    row = lambda w: pl.BlockSpec((tm, w), lambda i: (i, 0))
    full = lambda s: pl.BlockSpec(s, lambda i: (0, 0))
    return pl.pallas_call(
        body, name="outproj_loss", grid=(S // tm,),
        in_specs=[row(2 * D), full((2 * D, D)), row(D), row(D), full((1, D))],
        out_specs=[row(D), row(2 * D), full((2 * D, D)), full((1, D)), full((1, LANE))],
        out_shape=[SDS((S, D), F32), SDS((S, 2 * D), F32), SDS((2 * D, D), F32), SDS((1, D), F32),
                   SDS((1, LANE), F32)],
        compiler_params=_cp(("arbitrary",)),
    )(mix, w_out, x, tgt, nw)


def _inproj_bwd_dx(srcs, dxbcdt, w_all_t, x, dy, nw, hosted=None):
    tm = 512
    nk = DP // D
    n_host = len(hosted.arrays) if hosted else 0

    def body(*refs):
        src_refs = refs[:nk]
        w_ref, x_ref, dy_ref, nw_ref = refs[nk:nk + 4]
        host_in, refs = refs[nk + 4:nk + 4 + n_host], refs[nk + 4 + n_host:]
        gx_ref, dnw_ref = refs[:2]
        host_out, acc_ref, host_sems = refs[2:2 + n_host], refs[2 + n_host], refs[3 + n_host:]
        i, kk = pl.program_id(0), pl.program_id(1)
        if hosted:
            pl.when((i == 0) & (kk == 0))(lambda: hosted.start(host_in, host_out, host_sems))

        @pl.when((i == 0) & (kk == 0))
        def _():
            dnw_ref[...] = jnp.zeros_like(dnw_ref)

        @pl.when(kk == 0)
        def _():
            acc_ref[...] = jnp.zeros_like(acc_ref)

        for s, ref in enumerate(src_refs):
            @pl.when(kk == s)
            def _(ref=ref):
                acc_ref[...] += _dot(ref[...], w_ref[...])

        @pl.when(kk == nk - 1)
        def _():
            xf, du, nw_v = x_ref[...], acc_ref[...], nw_ref[...]
            r = lax.rsqrt(jnp.mean(xf * xf, axis=-1, keepdims=True) + EPS)
            xh = xf * r
            dnw_ref[...] += jnp.sum(du * xh, axis=0, keepdims=True)
            gdu = du * nw_v
            gx_ref[...] = r * (gdu - xh * jnp.mean(gdu * xh, axis=-1, keepdims=True)) + dy_ref[...]

        if hosted:
            pl.when((i == S // tm - 1) & (kk == nk - 1))(lambda: hosted.finish(host_in, host_out, host_sems))

    row = pl.BlockSpec((tm, D), lambda i, k: (i, 0))
    row1 = pl.BlockSpec((tm, D), lambda i, k: (i, 1))
    one = pl.BlockSpec((1, D), lambda i, k: (0, 0))
    args = [*srcs, dxbcdt, dxbcdt, w_all_t, x, dy, nw]
    in_specs = [row] * len(srcs) + [row, row1, pl.BlockSpec((D, D), lambda i, k: (k, 0)), row, row, one]
    out_specs, out_shape = [row, one], [SDS((S, D), F32), SDS((1, D), F32)]
    scratch = [pltpu.VMEM((tm, D), F32)]
    if hosted:
        args += hosted.arrays
        in_specs += [ANY] * n_host
        out_specs += [ANY] * n_host
        out_shape += hosted.out_shape
        scratch += hosted.scratch
    outs = pl.pallas_call(
        body, name="inproj_bwd_dx", grid=(S // tm, nk),
        in_specs=in_specs, out_specs=out_specs, out_shape=out_shape, scratch_shapes=scratch,
        compiler_params=_cp(("arbitrary", "arbitrary")),
    )(*args)
    return (outs[:2], outs[2:]) if hosted else outs


def _dw(u, dsec, name):
    ts = 512
    ncol = dsec.shape[1] // D

    def body(u_ref, d_ref, o_ref):
        @pl.when(pl.program_id(1) == 0)
        def _():
            o_ref[...] = jnp.zeros_like(o_ref)

        o_ref[...] += _dot_tn(u_ref[...], d_ref[...])

    return pl.pallas_call(
        body, name=name, grid=(ncol, S // ts),
        in_specs=[pl.BlockSpec((ts, D), lambda j, i: (i, 0)), pl.BlockSpec((ts, D), lambda j, i: (i, j))],
        out_specs=pl.BlockSpec((D, D), lambda j, i: (0, j)),
        out_shape=SDS((D, ncol * D), F32),
        compiler_params=_cp(("parallel", "arbitrary")),
    )(u, dsec)


def _place():
    x, y, c = lax.axis_index("x"), lax.axis_index("y"), lax.axis_index("c")
    return x, y, c, 2 * x + y


def _chip_of(x, y, k):
    px = 1 - x if k & 2 else x
    py = 1 - y if k & 1 else y
    return px, py, 2 * px + py


def _remote(src, dst, send_sem, recv_sem, dev):
    return pltpu.make_async_remote_copy(src_ref=src, dst_ref=dst, send_sem=send_sem, recv_sem=recv_sem,
                                        device_id=dev, device_id_type=MESH)


def _gather_weights(w_in_b, w_out_b, conv_w):
    def body(win_ref, wout_ref, cw_ref, gin_ref, gout_ref, gcw_ref, send, recv, csend, crecv):
        x, y, c, j = _place()
        me, sib = (x, y, c), (x, y, 1 - c)
        nbr = {"x": _chip_of(x, y, 2), "y": _chip_of(x, y, 1)}
        diag = _chip_of(x, y, 3)[2]
        conv = [_remote(cw_ref, gcw_ref.at[j], csend.at[k - 1], crecv.at[k - 1], (*_chip_of(x, y, k)[:2], c))
                for k in (1, 2, 3)]
        for cp in conv:
            cp.start()
        started, arrivals = [], []
        pairs = ((win_ref, gin_ref), (wout_ref, gout_ref))

        def rows(src, n_quarter=None, sibling=False):
            half = src.shape[0] // 2
            base = (1 - c if sibling else c) * half
            return pl.ds(base, half) if n_quarter is None else pl.ds(base + n_quarter * (half // 2), half // 2)

        def sem(a, n):
            return send.at[8 * a + n], recv.at[8 * a + n]

        def go(cp):
            cp.start()
            started.append(cp)

        for a, (src, dst) in enumerate(pairs):
            for n, axis in enumerate("xy"):
                px, py, _ = nbr[axis]
                go(_remote(src.at[rows(src)], dst.at[j, rows(src)], *sem(a, n), (px, py, c)))
        for n, axis in enumerate("xy"):
            ox, oy, _ = nbr["y" if axis == "x" else "x"]
            pj = nbr[axis][2]
            for a, (src, dst) in enumerate(pairs):
                _remote(src.at[rows(src)], dst.at[pj, rows(src)], *sem(a, n), me).wait_recv()
                go(_remote(dst.at[pj, rows(src, n)], dst.at[pj, rows(src, n)], *sem(a, 2 + n), (ox, oy, c)))
                go(_remote(dst.at[pj, rows(src)], dst.at[pj, rows(src)], *sem(a, 4 + n), sib))
                arrivals.append(_remote(src.at[rows(src)], dst.at[pj, rows(src, None, True)], *sem(a, 4 + n), me))
        for n in range(2):
            for a, (src, dst) in enumerate(pairs):
                part = rows(src, n)
                _remote(dst.at[diag, part], dst.at[diag, part], *sem(a, 2 + n), me).wait_recv()
                go(_remote(dst.at[diag, part], dst.at[diag, part], *sem(a, 6 + n), sib))
                sib_part = rows(src, n, True)
                arrivals.append(_remote(dst.at[diag, sib_part], dst.at[diag, sib_part], *sem(a, 6 + n), me))
        for cp in arrivals:
            cp.wait_recv()
        for k in (1, 2, 3):
            pj = _chip_of(x, y, k)[2]
            _remote(cw_ref, gcw_ref.at[pj], csend.at[k - 1], crecv.at[k - 1], me).wait_recv()
        for cp in started + conv:
            cp.wait_send()

    return pl.pallas_call(
        body, name="gather_weights",
        in_specs=[ANY, ANY, ANY], out_specs=[ANY, ANY, ANY],
        out_shape=[SDS((4,) + w_in_b.shape, BF16), SDS((4,) + w_out_b.shape, BF16), SDS((4,) + conv_w.shape, F32)],
        scratch_shapes=[pltpu.SemaphoreType.DMA((16,)), pltpu.SemaphoreType.DMA((16,)),
                        pltpu.SemaphoreType.DMA((3,)), pltpu.SemaphoreType.DMA((3,))],
        compiler_params=pltpu.CompilerParams(has_side_effects=True),
    )(w_in_b, w_out_b, conv_w)


def _pair_exchange(arrays, name):
    halves = [a.shape[1] // 2 for a in arrays]
    n = len(arrays)

    def body(*refs):
        x, y, c, _ = _place()
        send, recv = refs[2 * n:]
        cps = [_remote(refs[k].at[:, pl.ds((1 - c) * halves[k], halves[k])], refs[n + k], send.at[k], recv.at[k],
                       (x, y, 1 - c)) for k in range(n)]
        for cp in cps:
            cp.start()
        for cp in cps:
            cp.wait()

    return pl.pallas_call(
        body, name=name, in_specs=[ANY] * n, out_specs=[ANY] * n,
        out_shape=[SDS((a.shape[0], h, a.shape[2]), F32) for a, h in zip(arrays, halves)],
        scratch_shapes=[pltpu.SemaphoreType.DMA((n,)), pltpu.SemaphoreType.DMA((n,))],
        compiler_params=pltpu.CompilerParams(has_side_effects=True),
    )(*arrays)


def _pair_sum(cidx, g, r, name):
    n, half, width = r.shape
    tr = min(half, 256)
    nt = half // tr

    def body(c_ref, g_ref, r_ref, o_ref):
        del c_ref
        o_ref[...] = (g_ref[...] + r_ref[...]).astype(BF16)

    return pl.pallas_call(
        body, name=name,
        grid_spec=pltpu.PrefetchScalarGridSpec(
            num_scalar_prefetch=1, grid=(n, nt),
            in_specs=[pl.BlockSpec((None, tr, width), lambda s, t, c: (s, c[0] * nt + t, 0)),
                      pl.BlockSpec((None, tr, width), lambda s, t, c: (s, t, 0))],
            out_specs=pl.BlockSpec((None, tr, width), lambda s, t, c: (s, t, 0))),
        out_shape=SDS(r.shape, BF16),
        compiler_params=_cp(("parallel", "parallel")),
    )(cidx, g, r)


class _ChipExchange:
    def __init__(self, arrays):
        self.arrays = list(arrays)
        self.out_shape = [SDS(a.shape, BF16) for a in self.arrays]
        self.scratch = [pltpu.SemaphoreType.DMA((3 * len(self.arrays),)) for _ in range(2)]

    def _copies(self, ins, outs, sems):
        x, y, c, j = _place()
        send, recv = sems
        for a, (src, dst) in enumerate(zip(ins, outs)):
            for k in (1, 2, 3):
                px, py, pj = _chip_of(x, y, k)
                n = 3 * a + k - 1
                yield (_remote(src.at[pj], dst.at[j], send.at[n], recv.at[n], (px, py, c)),
                       _remote(src.at[pj], dst.at[pj], send.at[n], recv.at[n], (x, y, c)))

    def start(self, ins, outs, sems):
        for send, _ in self._copies(ins, outs, sems):
            send.start()

    def finish(self, ins, outs, sems):
        for send, arrival in self._copies(ins, outs, sems):
            arrival.wait_recv()
            send.wait_send()


def _small_exchange(small):
    def body(sm_ref, rs_ref, send, recv, lsem):
        x, y, c, j = _place()
        me = 2 * j + c
        local = pltpu.make_async_copy(sm_ref, rs_ref.at[me], lsem)
        local.start()
        cps = []
        for k in range(1, 8):
            px, py, _ = _chip_of(x, y, k >> 1)
            pc = 1 - c if k & 1 else c
            cps.append(_remote(sm_ref, rs_ref.at[me], send.at[k - 1], recv.at[k - 1], (px, py, pc)))
        for cp in cps:
            cp.start()
        for k in range(1, 8):
            _, _, pj = _chip_of(x, y, k >> 1)
            pc = 1 - c if k & 1 else c
            _remote(sm_ref, rs_ref.at[2 * pj + pc], send.at[k - 1], recv.at[k - 1], (x, y, c)).wait_recv()
        for cp in cps:
            cp.wait_send()
        local.wait()

    return pl.pallas_call(
        body, name="small_exchange", in_specs=[ANY], out_specs=ANY,
        out_shape=SDS((8,) + small.shape, F32),
        scratch_shapes=[pltpu.SemaphoreType.DMA((7,)), pltpu.SemaphoreType.DMA((7,)), pltpu.SemaphoreType.DMA],
        compiler_params=pltpu.CompilerParams(has_side_effects=True),
    )(small)


def _slot_sum(r, name):
    n, rows, width = r.shape
    tr = min(rows, 256)

    def body(r_ref, o_ref):
        acc = r_ref[0].astype(F32)
        for s in range(1, n):
            acc = acc + r_ref[s].astype(F32)
        o_ref[...] = acc

    return pl.pallas_call(
        body, name=name, grid=(rows // tr,),
        in_specs=[pl.BlockSpec((n, tr, width), lambda t: (0, t, 0))],
        out_specs=pl.BlockSpec((tr, width), lambda t: (t, 0)),
        out_shape=SDS((rows, width), F32),
        compiler_params=_cp(("parallel",)),
    )(r)


def _chip_sum(chip_idx, recv, own, name):
    n, rows, width = recv.shape
    tr = min(rows, 256)

    def body(j_ref, r_ref, own_ref, o_ref):
        acc = None
        for s in range(n):
            term = jnp.where(j_ref[0] == s, own_ref[...], r_ref[s]).astype(F32)
            acc = term if acc is None else acc + term
        o_ref[...] = acc

    return pl.pallas_call(
        body, name=name,
        grid_spec=pltpu.PrefetchScalarGridSpec(
            num_scalar_prefetch=1, grid=(rows // tr,),
            in_specs=[pl.BlockSpec((n, tr, width), lambda t, j: (0, t, 0)),
                      pl.BlockSpec((None, tr, width), lambda t, j: (j[0], t, 0))],
            out_specs=pl.BlockSpec((tr, width), lambda t, j: (t, 0))),
        out_shape=SDS((rows, width), F32),
        compiler_params=_cp(("parallel",)),
    )(chip_idx, recv, own)


def _half_exchange(hw, ho):
    def body(hw_ref, ho_ref, tw_ref, to_ref, send, recv):
        x, y, c, _ = _place()
        sib = (x, y, 1 - c)
        cps = [_remote(hw_ref, tw_ref, send.at[0], recv.at[0], sib),
               _remote(ho_ref, to_ref, send.at[1], recv.at[1], sib)]
        for cp in cps:
            cp.start()
        for cp in cps:
            cp.wait()

    return pl.pallas_call(
        body, name="half_exchange", in_specs=[ANY, ANY], out_specs=[ANY, ANY],
        out_shape=[SDS(hw.shape, F32), SDS(ho.shape, F32)],
        scratch_shapes=[pltpu.SemaphoreType.DMA((2,)), pltpu.SemaphoreType.DMA((2,))],
        compiler_params=pltpu.CompilerParams(has_side_effects=True),
    )(hw, ho)


def _by_core(c, mine, theirs):
    return jnp.where(c == 0, jnp.concatenate([mine, theirs], axis=0), jnp.concatenate([theirs, mine], axis=0))


def _adamw(w, g, m, v, name):
    rows, width = w.shape
    tr = min(rows, 256)

    def body(w_ref, g_ref, m_ref, v_ref, d_ref, nm_ref, nv_ref):
        gv = g_ref[...]
        nm = ADAM_B1 * m_ref[...] + (1.0 - ADAM_B1) * gv
        nv = ADAM_B2 * v_ref[...] + (1.0 - ADAM_B2) * (gv * gv)
        m_hat = nm / (1.0 - ADAM_B1 ** ADAM_STEP)
        v_hat = nv / (1.0 - ADAM_B2 ** ADAM_STEP)
        d_ref[...] = -ADAM_LR * (m_hat / (jnp.sqrt(v_hat) + ADAM_EPS) + ADAM_WD * w_ref[...])
        nm_ref[...] = nm
        nv_ref[...] = nv

    t = pl.BlockSpec((tr, width), lambda i: (i, 0))
    return pl.pallas_call(
        body, name=name, grid=(rows // tr,), in_specs=[t] * 4, out_specs=[t] * 3,
        out_shape=[SDS(w.shape, F32)] * 3, compiler_params=_cp(("parallel",)),
    )(w, g, m, v)


def _rows128(a, rows):
    flat = a.reshape(-1)
    return jnp.pad(flat, (0, rows * LANE - flat.shape[0])).reshape(rows, LANE)


def _pack_small(conv_w, norm_pre, conv_b, ssm_norm, norm_post, dtb, alog, dsk, extra=None):
    cw_rows = 48 if conv_w.shape[-1] == 1536 else 16
    extra = jnp.zeros((1, LANE), F32) if extra is None else _rows128(extra, 1)
    vec = jnp.concatenate([_rows128(dtb, 1), _rows128(alog, 1), _rows128(dsk, 1), extra, jnp.zeros((4, LANE), F32)],
                          axis=0)
    return jnp.concatenate([_rows128(conv_w, cw_rows), _rows128(norm_pre, 8), _rows128(conv_b, 16),
                            _rows128(ssm_norm, 8), _rows128(norm_post, 8), vec], axis=0)


def _unpack_small(p, cw_cols):
    cw_rows = 48 if cw_cols == 1536 else 16
    o = cw_rows
    conv_w = p[:cw_rows].reshape(-1)[:4 * cw_cols].reshape(1, 4, cw_cols)
    norm_pre = p[o:o + 8].reshape(1, D)
    conv_b = p[o + 8:o + 24].reshape(-1)[:1536].reshape(1, 1536)
    ssm_norm = p[o + 24:o + 32].reshape(1, D)
    norm_post = p[o + 32:o + 40].reshape(1, D)
    vec = p[o + 40:o + 48]
    return conv_w, norm_pre, conv_b, ssm_norm, norm_post, vec[0:1, :NH], vec[1:2, :NH], vec[2:3, :NH], vec[3, 0]


def _pad_lanes(a):
    return jnp.pad(a, ((0, 0), (0, LANE - a.shape[1])))


class _GradReduce:
    def __init__(self, chip, ci):
        self.ci = ci
        self.cidx = jnp.reshape(ci, (1,)).astype(jnp.int32)
        self.chip_idx = jnp.reshape(chip, (1,)).astype(jnp.int32)

    def start(self, dw_all, dw_out):
        gw = jnp.stack([dw_all[:, k * SHARD:(k + 1) * SHARD] for k in range(4)])
        go = dw_out.reshape(4, D // 2, D)
        rw, ro = _pair_exchange([gw, go], "pair_exchange")
        self.own = [_pair_sum(self.cidx, gw, rw, "pair_sum_in"), _pair_sum(self.cidx, go, ro, "pair_sum_out")]
        return _ChipExchange(self.own)

    def done(self, got):
        self.got = got

    def result(self):
        half_in = _chip_sum(self.chip_idx, self.got[0], self.own[0], "chip_sum_in")
        half_out = _chip_sum(self.chip_idx, self.got[1], self.own[1], "chip_sum_out")
        their_in, their_out = _half_exchange(half_in, half_out)
        return _by_core(self.ci, half_in, their_in), _by_core(self.ci, half_out, their_out)


def kernel(x, norm_pre_w, w_in, conv_w, conv_b, dt_bias, a_log, d_skip, ssm_norm_w, w_out, norm_post_w, loss_target, m_norm_pre_w, m_w_in, m_conv_w, m_conv_b, m_dt_bias, m_a_log, m_d_skip, m_ssm_norm_w, m_w_out, m_norm_post_w, v_norm_pre_w, v_w_in, v_conv_w, v_conv_b, v_dt_bias, v_a_log, v_d_skip, v_ssm_norm_w, v_w_out, v_norm_post_w):
    xi, yi, ci = lax.axis_index("x"), lax.axis_index("y"), lax.axis_index("c")
    chip = 2 * xi + yi
    x2, tgt = x[0], loss_target[0]

    w_in_b, w_out_b = w_in[0].astype(BF16), w_out[0].astype(BF16)
    gin, gout, gcw = _gather_weights(w_in_b, w_out_b, conv_w[0])

    def whole(own, gathered, axis):
        return jnp.concatenate([jnp.where(chip == k, own, gathered[k]) for k in range(4)], axis=axis)

    w_all = jnp.concatenate([whole(w_in_b, gin, 1), jnp.zeros((D, DP - 4 * SHARD), BF16)], axis=1)
    w_out_all = whole(w_out_b, gout, 0)
    cw_all = whole(conv_w[0], gcw, 1)
    reduce = _GradReduce(chip, ci)
    grad_x, small = _local_step(x2, tgt, w_all, w_out_all, cw_all, norm_pre_w, conv_b, dt_bias, a_log, d_skip,
                                ssm_norm_w, norm_post_w, reduce)[:2]
    g_in, g_out = reduce.result()
    g_small = _slot_sum(_small_exchange(small), "small_sum")
    g_cw, g_npre, g_cb, g_nssm, g_npost, g_dtb, g_alog, g_dsk, loss = _unpack_small(g_small, 1536)
    g_cw = lax.dynamic_slice_in_dim(g_cw, chip * 384, 384, axis=2)

    d_in, nm_in, nv_in = _adamw(w_in[0], g_in, m_w_in[0], v_w_in[0], "adamw_in")
    d_out, nm_out, nv_out = _adamw(w_out[0], g_out, m_w_out[0], v_w_out[0], "adamw_out")
    packed = [_pack_small(*t) for t in (
        (conv_w, norm_pre_w, conv_b, ssm_norm_w, norm_post_w, dt_bias, a_log, d_skip),
        (g_cw, g_npre, g_cb, g_nssm, g_npost, g_dtb, g_alog, g_dsk),
        (m_conv_w, m_norm_pre_w, m_conv_b, m_ssm_norm_w, m_norm_post_w, m_dt_bias, m_a_log, m_d_skip),
        (v_conv_w, v_norm_pre_w, v_conv_b, v_ssm_norm_w, v_norm_post_w, v_dt_bias, v_a_log, v_d_skip))]
    small_out = [_unpack_small(p, 384)[:8] for p in _adamw(*packed, "adamw_small")]

    def ordered(cw_, npre, cb_, nssm, npost, dtb_, alog_, dsk_, big_in, big_out):
        return [npre, big_in[None], cw_, cb_, dtb_, alog_, dsk_, nssm, big_out[None], npost]

    grads = ordered(g_cw, g_npre, g_cb, g_nssm, g_npost, g_dtb, g_alog, g_dsk, g_in, g_out)
    deltas = ordered(*small_out[0], d_in, d_out)
    new_m = ordered(*small_out[1], nm_in, nm_out)
    new_v = ordered(*small_out[2], nv_in, nv_out)
    return (loss, grad_x[None], *grads, *deltas, *new_m, *new_v)


def _local_step(x2, tgt, w_all, w_out_all, cw_all, norm_pre_w, conv_b, dt_bias, a_log, d_skip, ssm_norm_w,
                norm_post_w, reduce=None):
    dtb, alog = _pad_lanes(dt_bias), _pad_lanes(a_log)
    d_b = jnp.repeat(d_skip, 64, axis=1)

    proj, u = _inproj_fwd(x2, norm_pre_w, w_all)
    mix, attn_pre, lse = _attn_fwd(proj, 1, _attn_fwd(proj, 4, _attn_fwd(proj, 16)), final=True)
    mix, y_save, states = _ssm_fwd(proj, mix, cw_all, conv_b, dtb, alog, d_b, ssm_norm_w)

    dy, dmix, dw_out, dnw_post, loss_part = _outproj_loss(mix, w_out_all, x2, tgt, norm_post_w)
    do, delta, dg = _attn_gate_bwd(dmix, attn_pre, proj)
    dz, dxbcdt, dcw, dcb, dvec, dnw_ssm = _ssm_bwd(proj, dmix, y_save, states, cw_all, conv_b, dtb, alog, d_b,
                                                   ssm_norm_w)
    acc = _attn_bwd(proj, do, lse, delta, 16, None, F32)
    acc = _attn_bwd(proj, do, lse, delta, 4, acc, F32)
    dq, dk, dv = _attn_bwd(proj, do, lse, delta, 1, acc, BF16)
    srcs = [dq, dk, dv, dg, dz]
    dws = [_dw(u, s, f"dw_in_{n}") for s, n in zip(srcs + [dxbcdt], ("q", "k", "v", "g", "z", "xbcdt"))]
    dw_all = jnp.concatenate(dws, axis=1)
    res = _inproj_bwd_dx(srcs, dxbcdt, w_all.T, x2, dy, norm_pre_w, reduce.start(dw_all, dw_out) if reduce else None)
    if reduce:
        res, got = res
        reduce.done(got)
    grad_x, dnw_pre = res
    small = _pack_small(dcw, dnw_pre, dcb, dnw_ssm, dnw_post, dvec[0:1, :NH], dvec[1:2, :NH], dvec[2:3, :NH],
                        loss_part[:, :1])
    return grad_x, small, dw_all, dw_out
```

```python
import functools

import jax
import jax.numpy as jnp
from jax import lax
from jax.experimental import pallas as pl
from jax.experimental.pallas import tpu as pltpu

F32 = jnp.float32
BF16 = jnp.bfloat16
MESH = pl.DeviceIdType.MESH
SDS = jax.ShapeDtypeStruct
ANY = pl.BlockSpec(memory_space=pl.ANY)

S = 4096
D = 1024
DP = 7168
SHARD = 1668
OFF_G, OFF_Z = 3072, 4096
NH = 16
CH = 128
NC = S // CH
EPS = 1e-6
NEG = -1e30
LANE = 128
VMEM_LIMIT = 48 * 1024 * 1024

ADAM_LR, ADAM_B1, ADAM_B2, ADAM_EPS, ADAM_WD, ADAM_STEP = 0.001, 0.9, 0.999, 1e-08, 0.01, 10


def _cp(sem, **kw):
    return pltpu.CompilerParams(dimension_semantics=sem, vmem_limit_bytes=VMEM_LIMIT, **kw)


def _dot(a, b):
    return jnp.dot(a, b, preferred_element_type=F32)


def _dot_nt(a, b):
    return lax.dot_general(a, b, (((1,), (1,)), ((), ())), preferred_element_type=F32)


def _dot_tn(a, b):
    return lax.dot_general(a, b, (((0,), (0,)), ((), ())), preferred_element_type=F32)


def _pieces(x, n):
    out = []
    for _ in range(n):
        p = x.astype(BF16)
        out.append(p)
        x = x - p.astype(F32)
    return out


def _pick(x, sel, n=2):
    parts = [_dot(p, sel) for p in _pieces(x, n)]
    return functools.reduce(jnp.add, parts)


def _pick_left(sel, x, n=3):
    parts = [_dot(sel, p) for p in _pieces(x, n)]
    return functools.reduce(jnp.add, parts)


def _sigmoid(v):
    return 0.5 * jnp.tanh(0.5 * v) + 0.5


def _iota(shape, dim):
    return lax.broadcasted_iota(jnp.int32, shape, dim)


def _inproj_fwd(x, nw, w_all):
    tm, tn = 1024, 1024

    def body(x_ref, nw_ref, w_ref, proj_ref, u_ref):
        @pl.when(pl.program_id(1) == 0)
        def _():
            xf = x_ref[...]
            r = lax.rsqrt(jnp.mean(xf * xf, axis=-1, keepdims=True) + EPS)
            u_ref[...] = (xf * r * nw_ref[...]).astype(BF16)

        proj_ref[...] = _dot(u_ref[...], w_ref[...])

    return pl.pallas_call(
        body, name="inproj_fwd", grid=(S // tm, DP // tn),
        in_specs=[pl.BlockSpec((tm, D), lambda i, j: (i, 0)), pl.BlockSpec((1, D), lambda i, j: (0, 0)),
                  pl.BlockSpec((D, tn), lambda i, j: (0, j))],
        out_specs=[pl.BlockSpec((tm, tn), lambda i, j: (i, j)), pl.BlockSpec((tm, D), lambda i, j: (i, 0))],
        out_shape=[SDS((S, DP), F32), SDS((S, D), BF16)],
        compiler_params=_cp(("parallel", "arbitrary")),
    )(x, nw, w_all)


ATTN_QB = {1: 4, 4: 1, 16: 1}


def _unit_rows(r, u, d):
    return pl.ds(r + d * CH * u, CH, stride=d) if d > 1 else pl.ds(CH * u, CH)


def _for_units(d, qb, fn):
    for r in range(d):
        for u in range(qb):
            fn(r, u)


def _attn_mask(has_prev):
    qi, kj = _iota((2 * CH, 2 * CH), 0) & (CH - 1), _iota((2 * CH, 2 * CH), 1)
    cur_ok = (kj >= CH) & (kj - CH <= qi)
    prev_ok = (kj < CH) & (kj >= qi)
    return cur_ok | (prev_ok & has_prev)


def _stack_heads(v, lane_a):
    return jnp.concatenate([jnp.where(lane_a, v, 0.0), jnp.where(lane_a, 0.0, v)], axis=0).astype(BF16)


def _attn_specs(d, qb, lag_out):
    rows, prows = CH * d * qb, CH * d
    nb = S // rows
    last = nb - 1

    def cur(off):
        return pl.BlockSpec((rows, LANE), lambda c, i: (jnp.minimum(i, last), off + c))

    def prev(off):
        return pl.BlockSpec((prows, LANE), lambda c, i: (jnp.clip(i * qb - 1, 0, S // prows - 1), off + c))

    lag = pl.BlockSpec((rows, LANE), lambda c, i: (jnp.clip(i - 1, 0, last), c)) if lag_out else None
    return nb, cur, prev, lag


def _gather16(src_ref, dense_ref, tmp_ref):
    for a in range(4):
        tmp_ref[...] = src_ref[pl.ds(a, 4 * CH, stride=4), :]
        for b in range(4):
            dense_ref[a + 4 * b] = tmp_ref[pl.ds(b, CH, stride=4), :]


def _scatter16(dense_ref, dst_ref, tmp_ref):
    for a in range(4):
        for b in range(4):
            tmp_ref[pl.ds(b, CH, stride=4), :] = dense_ref[a + 4 * b]
        dst_ref[pl.ds(a, 4 * CH, stride=4), :] = tmp_ref[...]


def _unit_index(r, u, d):
    return (r,) if d == 16 else (_unit_rows(r, u, d), slice(None))


def _unit_kv(p_ref, c_ref, r, u, d):
    prev = p_ref[_unit_index(r, 0, d)] if u == 0 else c_ref[_unit_index(r, u - 1, d)]
    return jnp.concatenate([prev, c_ref[_unit_index(r, u, d)]], axis=0).astype(BF16)


def _dense_scratch(d, n):
    return [pltpu.VMEM((16, CH, LANE), F32)] * n + [pltpu.VMEM((4 * CH, LANE), F32)] if d == 16 else []


def _attn_fwd(proj, d, prior=None, final=False):
    qb = ATTN_QB[d]
    nb, cur, prev, _ = _attn_specs(d, qb, False)
    n_prior = 2 if prior is not None else 0
    n_in, n_out = 5 + n_prior + final, 2 + final
    assert not (d == 16 and (n_prior or final))

    def body(*refs):
        ins, outs, scratch = refs[:n_in], refs[n_in:n_in + n_out], refs[n_in + n_out:]
        if d == 16:
            tmp_ref = scratch[-1]
            for src, dense in zip(ins, scratch):
                _gather16(src, dense, tmp_ref)
            block_outs, ins, outs = outs, scratch[:n_in], scratch[n_in:n_in + n_out]
        q_ref, kp_ref, kc_ref, vp_ref, vc_ref = ins[:5]
        prior_refs = ins[5:5 + n_prior]
        if final:
            g_ref, (mix_ref, o_ref, l_ref) = ins[-1], outs
        else:
            o_ref, l_ref = outs
        i = pl.program_id(1)
        lane_a = _iota((CH, LANE), 1) < 64
        mask_first, mask_rest = _attn_mask(i > 0), _attn_mask(True)

        def unit(r, u):
            at = _unit_index(r, u, d)
            q2 = _stack_heads(q_ref[at] * 0.125, lane_a)
            k2, v2 = _unit_kv(kp_ref, kc_ref, r, u, d), _unit_kv(vp_ref, vc_ref, r, u, d)
            s = jnp.where(mask_first if u == 0 else mask_rest, _dot_nt(q2, k2), NEG)
            m = jnp.max(s, axis=1, keepdims=True)
            p = jnp.exp(s - m)
            l = jnp.sum(p, axis=1, keepdims=True)
            o2 = _dot(p.astype(BF16), v2) / l
            lse2 = m + jnp.log(l)
            o = jnp.where(lane_a, o2[:CH], o2[CH:])
            lse = jnp.where(lane_a, lse2[:CH], lse2[CH:])
            if n_prior:
                o_a, l_a = prior_refs[0][at], prior_refs[1][at]
                top = jnp.maximum(l_a, lse)
                e_a, e_b = jnp.exp(l_a - top), jnp.exp(lse - top)
                tot = e_a + e_b
                o = (e_a * o_a + e_b * o) / tot
                lse = top + jnp.log(tot)
            o_ref[at] = o
            l_ref[at] = lse
            if final:
                g = g_ref[at]
                mix_ref[at] = (o * (g * _sigmoid(g))).astype(BF16)

        _for_units(d, qb, unit)
        if d == 16:
            for dense, dst in zip(outs, block_outs):
                _scatter16(dense, dst, tmp_ref)

    in_specs = [cur(0), prev(8), cur(8), prev(16), cur(16)] + [cur(0)] * n_prior
    args = [proj] * 5 + (list(prior) if n_prior else [])
    out_specs, out_shape = [cur(0), cur(0)], [SDS((S, D), F32), SDS((S, D), F32)]
    if final:
        assert d == 1
        in_specs.append(cur(OFF_G // LANE))
        args.append(proj)
        out_specs, out_shape = [cur(0)] + out_specs, [SDS((S, 2 * D), BF16)] + out_shape
    return pl.pallas_call(
        body, name=f"attn_fwd_d{d}", grid=(NH // 2, nb),
        in_specs=in_specs, out_specs=out_specs, out_shape=out_shape,
        scratch_shapes=_dense_scratch(d, n_in + n_out),
        compiler_params=_cp(("parallel", "arbitrary")),
    )(*args)


def _attn_gate_bwd(dmix, pre, proj):
    tm = 512

    def body(dm_ref, pre_ref, g_ref, do_ref, delta_ref, dg_ref):
        g, dm, pre_v = g_ref[...], dm_ref[...], pre_ref[...]
        sig = _sigmoid(g)
        do = dm * (g * sig)
        do_ref[...] = do
        dg_ref[...] = (dm * pre_v * (sig * (1.0 + g * (1.0 - sig)))).astype(BF16)
        prod = do * pre_v
        same_head = (_iota((LANE, LANE), 0) // 64 == _iota((LANE, LANE), 1) // 64).astype(BF16)
        for cb in range(D // LANE):
            delta_ref[:, cb * LANE:(cb + 1) * LANE] = _pick(prod[:, cb * LANE:(cb + 1) * LANE], same_head)

    t = pl.BlockSpec((tm, D), lambda i: (i, 0))
    return pl.pallas_call(
        body, name="attn_gate_bwd", grid=(S // tm,),
        in_specs=[t, t, pl.BlockSpec((tm, D), lambda i: (i, OFF_G // D))],
        out_specs=[t, t, t],
        out_shape=[SDS((S, D), F32), SDS((S, D), F32), SDS((S, D), BF16)],
        compiler_params=_cp(("parallel",)),
    )(dmix, pre, proj)


def _attn_bwd(proj, do, lse, delta, d, acc, out_dtype):
    qb = ATTN_QB[d]
    nb, cur, prev, lag = _attn_specs(d, qb, True)
    has_acc = acc is not None
    n_in = 11 if has_acc else 8
    assert not (d == 16 and (has_acc or out_dtype != F32))
    rows = CH * d * qb
    carry = (2, 16, CH, LANE) if d == 16 else (2, rows, LANE)

    def body(*refs):
        ins, (dq_ref, dk_ref, dv_ref), scratch = refs[:n_in], refs[n_in:n_in + 3], refs[n_in + 3:]
        ck_ref, cv_ref = scratch[:2]
        dq_f32 = dq_ref if out_dtype == F32 else scratch[2]
        i = pl.program_id(1)
        if d == 16:
            dense, dq_f32, tmp_ref = scratch[2:2 + n_in], scratch[2 + n_in], scratch[-1]

            @pl.when(i < nb)
            def _():
                for src, dst in zip(ins, dense):
                    _gather16(src, dst, tmp_ref)

            ins = dense
        q_ref, kp_ref, kc_ref, vp_ref, vc_ref, do_ref, lse_ref, dl_ref = ins[:8]
        if has_acc:
            aq_ref, ak_ref, av_ref = ins[8:11]
        slot = i & 1
        now_k, now_v, old_k, old_v = ck_ref.at[slot], cv_ref.at[slot], ck_ref.at[1 - slot], cv_ref.at[1 - slot]
        lane_a = _iota((CH, LANE), 1) < 64
        mask_first, mask_rest = _attn_mask(i > 0), _attn_mask(True)

        @pl.when(i == 0)
        def _():
            ck_ref[1] = jnp.zeros(carry[1:], F32)
            cv_ref[1] = jnp.zeros(carry[1:], F32)

        def unit(r, u):
            at = _unit_index(r, u, d)
            q2 = _stack_heads(q_ref[at] * 0.125, lane_a)
            do2 = _stack_heads(do_ref[at], lane_a)
            k2, v2 = _unit_kv(kp_ref, kc_ref, r, u, d), _unit_kv(vp_ref, vc_ref, r, u, d)
            lsev, dlv = lse_ref[at], dl_ref[at]
            lse2 = jnp.concatenate([lsev[:, 0:1], lsev[:, 64:65]], axis=0)
            dl2 = jnp.concatenate([dlv[:, 0:1], dlv[:, 64:65]], axis=0)
            p = jnp.exp(jnp.where(mask_first if u == 0 else mask_rest, _dot_nt(q2, k2), NEG) - lse2)
            ds = (p * (_dot_nt(do2, v2) - dl2)).astype(BF16)
            dq2 = _dot(ds, k2)
            dk2 = _dot_tn(ds, q2)
            dv2 = _dot_tn(p.astype(BF16), do2)
            dq = jnp.where(lane_a, dq2[:CH], dq2[CH:]) * 0.125
            if has_acc:
                dq = dq + aq_ref[at]
            dq_f32[at] = dq
            if u == 0:
                before = _unit_index(r, qb - 1, d)
                old_k[before] += dk2[:CH]
                old_v[before] += dv2[:CH]
            else:
                before = _unit_index(r, u - 1, d)
                now_k[before] += dk2[:CH]
                now_v[before] += dv2[:CH]
            now_k[at] = dk2[CH:]
            now_v[at] = dv2[CH:]

        @pl.when(i < nb)
        def _():
            _for_units(d, qb, unit)
            if d == 16:
                _scatter16(dq_f32, dq_ref, tmp_ref)
            elif out_dtype != F32:
                dq_ref[...] = dq_f32[...].astype(out_dtype)

        if d == 16:
            _scatter16(old_k, dk_ref, tmp_ref)
            _scatter16(old_v, dv_ref, tmp_ref)
        else:
            dk, dv = old_k[...], old_v[...]
            if has_acc:
                dk, dv = dk + ak_ref[...], dv + av_ref[...]
            dk_ref[...] = dk.astype(out_dtype)
            dv_ref[...] = dv.astype(out_dtype)

    in_specs = [cur(0), prev(8), cur(8), prev(16), cur(16), cur(0), cur(0), cur(0)]
    args = [proj, proj, proj, proj, proj, do, lse, delta]
    if has_acc:
        in_specs += [cur(0), lag, lag]
        args += list(acc)
    scratch = [pltpu.VMEM(carry, F32), pltpu.VMEM(carry, F32)]
    if d == 16:
        scratch += _dense_scratch(d, n_in + 1)
    elif out_dtype != F32:
        scratch.append(pltpu.VMEM((rows, LANE), F32))
    return pl.pallas_call(
        body, name=f"attn_bwd_d{d}", grid=(NH // 2, nb + 1),
        in_specs=in_specs, out_specs=[cur(0), lag, lag], out_shape=[SDS((S, D), out_dtype)] * 3,
        scratch_shapes=scratch, compiler_params=_cp(("parallel", "arbitrary")),
    )(*args)


def _conv_taps(cur, prev8, first):
    row8 = _iota(prev8.shape, 0)
    prev8 = jnp.where(first, 0.0, prev8)
    taps = []
    for s in (3, 2, 1):
        rolled = pltpu.roll(cur, s, 0)
        head = jnp.where(row8 < s, pltpu.roll(prev8, s, 0), rolled[:8])
        taps.append(jnp.concatenate([head, rolled[8:]], axis=0))
    return taps + [cur]


def _conv(taps, w, b):
    acc = b + w[0:1, :] * taps[0]
    for k in (1, 2, 3):
        acc = acc + w[k:k + 1, :] * taps[k]
    return acc


def _expand():
    return (_iota((LANE, D), 1) // 64 == _iota((LANE, D), 0)).astype(BF16)


def _reduce():
    return (_iota((D, LANE), 0) // 64 == _iota((D, LANE), 1)).astype(BF16)


def _ssd_common(xs_raw, xs_prev, bc_raw, bc_prev, dt_raw, first, cw, cb, dtb, alog):
    head_lane = _iota((CH, LANE), 1) < NH
    xs_taps = _conv_taps(xs_raw, xs_prev, first)
    bc_taps = _conv_taps(bc_raw, bc_prev, first)
    xs_c = _conv(xs_taps, cw[:, :D], cb[:, :D])
    bc_c = _conv(bc_taps, cw[:, D:], cb[:, D:])
    xs = xs_c * _sigmoid(xs_c)
    bc = bc_c * _sigmoid(bc_c)
    pre = dt_raw + dtb
    dt = jnp.where(head_lane, jnp.maximum(pre, 0.0) + jnp.log(1.0 + jnp.exp(-jnp.abs(pre))), 0.0)
    a_row = jnp.where(head_lane[0:1], -jnp.exp(alog), 0.0)
    tri = (_iota((CH, CH), 1) <= _iota((CH, CH), 0)).astype(BF16)
    cs = _pick_left(tri, dt * a_row)
    cs_last = cs[CH - 1:CH, :]
    wide = _pick(jnp.concatenate([dt, jnp.exp(cs), jnp.exp(cs_last - cs)], axis=0), _expand())
    dt_b, e_b, f_b = wide[:CH], wide[CH:2 * CH], wide[2 * CH:]
    return dict(xs_taps=xs_taps, bc_taps=bc_taps, xs_c=xs_c, bc_c=bc_c, xs=xs, bc=bc, pre=pre, dt=dt,
                a_row=a_row, cs=cs, cs_t=cs.T, dt_b=dt_b, e_b=e_b, f_b=f_b, t_b=e_b[CH - 1:CH, :])


def _groups(bc):
    bcb = bc.astype(BF16)
    return [bcb[:, 0:128], bcb[:, 128:256]], [bcb[:, 256:384], bcb[:, 384:512]]


def _decay(q, h, tril):
    seg = q["cs"][:, h:h + 1] - q["cs_t"][h:h + 1, :]
    return jnp.exp(jnp.where(tril, seg, NEG))


def _ssm_fwd(proj, mix, cw, cb, dtb, alog, d_b, nw):
    def body(xs_ref, xsp_ref, bc_ref, bcp_ref, dt_ref, z_ref, cw_ref, cb_ref, dtb_ref, alog_ref, db_ref, nw_ref,
             mix_in_ref, mix_ref, y_ref, st_ref, h_ref):
        del mix_in_ref
        i = pl.program_id(0)

        @pl.when(i == 0)
        def _():
            h_ref[...] = jnp.zeros_like(h_ref)

        q = _ssd_common(xs_ref[...], xsp_ref[...], bc_ref[...], bcp_ref[...], dt_ref[...], i == 0,
                        cw_ref[...], cb_ref[...], dtb_ref[...], alog_ref[...])
        bg, cg = _groups(q["bc"])
        xs = q["xs"]
        xdt = xs * q["dt_b"]
        xdt_b = xdt.astype(BF16)
        h_in = h_ref[...]
        st_ref[...] = h_in
        hb = h_in.astype(BF16)
        tril = _iota((CH, CH), 1) <= _iota((CH, CH), 0)
        lane_a = _iota((CH, LANE), 1) < 64
        cbm = [_dot_nt(cg[g], bg[g]) for g in range(2)]
        pairs = []
        for hp in range(NH // 2):
            xp = xdt_b[:, hp * LANE:(hp + 1) * LANE]
            ya = _dot((cbm[hp // 4] * _decay(q, 2 * hp, tril)).astype(BF16), xp)
            yb = _dot((cbm[hp // 4] * _decay(q, 2 * hp + 1, tril)).astype(BF16), xp)
            pairs.append(jnp.where(lane_a, ya, yb))
        y_diag = jnp.concatenate(pairs, axis=1)
        y_off = jnp.concatenate([_dot(cg[g], hb[:, g * 512:(g + 1) * 512]) for g in range(2)], axis=1) * q["e_b"]
        y = y_diag + y_off + db_ref[...] * xs
        y_ref[...] = y
        xf = (xdt * q["f_b"]).astype(BF16)
        h_ref[...] = q["t_b"] * h_in + jnp.concatenate(
            [_dot_tn(bg[g], xf[:, g * 512:(g + 1) * 512]) for g in range(2)], axis=1)
        z = z_ref[...]
        yz = y * (z * _sigmoid(z))
        outs = []
        for g in range(2):
            v = yz[:, g * 512:(g + 1) * 512]
            outs.append(v * lax.rsqrt(jnp.mean(v * v, axis=-1, keepdims=True) + EPS))
        mix_ref[...] = (jnp.concatenate(outs, axis=1) * nw_ref[...]).astype(BF16)

    def col(width, blk, prev=False):
        if prev:
            return pl.BlockSpec((8, width), lambda i: (jnp.maximum(i * (CH // 8) - 1, 0), blk))
        return pl.BlockSpec((CH, width), lambda i: (i, blk))

    def full(a):
        return pl.BlockSpec(a.shape, lambda i: (0,) * a.ndim)

    return pl.pallas_call(
        body, name="ssm_fwd", grid=(NC,),
        in_specs=[col(D, 5), col(D, 5, True), col(512, 12), col(512, 12, True), col(LANE, 52), col(D, 4),
                  full(cw), full(cb), full(dtb), full(alog), full(d_b), full(nw), ANY],
        out_specs=[col(D, 1), col(D, 0), pl.BlockSpec((None, CH, D), lambda i: (i, 0, 0))],
        out_shape=[SDS((S, 2 * D), BF16), SDS((S, D), F32), SDS((NC, CH, D), F32)],
        scratch_shapes=[pltpu.VMEM((CH, D), F32)],
        input_output_aliases={12: 0},
        compiler_params=_cp(("arbitrary",)),
    )(proj, proj, proj, proj, proj, proj, cw, cb, dtb, alog, d_b, nw, mix)


def _ssm_bwd(proj, dmix, y_save, states, cw, cb, dtb, alog, d_b, nw):
    def body(xs_ref, xsp_ref, bc_ref, bcp_ref, dt_ref, z_ref, dn_ref, y_ref, st_ref,
             cw_ref, cb_ref, dtb_ref, alog_ref, db_ref, nw_ref,
             dz_ref, dx_ref, dcw_ref, dcb_ref, dsm_ref, dnw_ref, dh_ref, nxs_ref, nbc_ref):
        i = pl.program_id(0)
        ci = NC - 1 - i

        @pl.when(i == 0)
        def _():
            for ref in (dcw_ref, dcb_ref, dsm_ref, dnw_ref, dh_ref, nxs_ref, nbc_ref):
                ref[...] = jnp.zeros_like(ref)

        cw, cb = cw_ref[...], cb_ref[...]
        q = _ssd_common(xs_ref[...], xsp_ref[...], bc_ref[...], bcp_ref[...], dt_ref[...], ci == 0,
                        cw, cb, dtb_ref[...], alog_ref[...])
        bg, cg = _groups(q["bc"])
        xs, dt_b, e_b, f_b, t_b = q["xs"], q["dt_b"], q["e_b"], q["f_b"], q["t_b"]
        xdt = xs * dt_b
        xdt_b = xdt.astype(BF16)
        h_in = st_ref[...]
        hb = h_in.astype(BF16)
        dh_new = dh_ref[...]
        dhb = dh_new.astype(BF16)
        red = _reduce()

        z, y, dn, nw_v = z_ref[...], y_ref[...], dn_ref[...], nw_ref[...]
        sig = _sigmoid(z)
        sz = z * sig
        yz = y * sz
        gdn = dn * nw_v
        dyz, dnw = [], []
        for g in range(2):
            v, gv = yz[:, g * 512:(g + 1) * 512], gdn[:, g * 512:(g + 1) * 512]
            r = lax.rsqrt(jnp.mean(v * v, axis=-1, keepdims=True) + EPS)
            dnw.append(dn[:, g * 512:(g + 1) * 512] * v * r)
            dyz.append(r * (gv - v * (r * r) * jnp.mean(gv * v, axis=-1, keepdims=True)))
        dyz = jnp.concatenate(dyz, axis=1)
        dnw_ref[...] += jnp.sum(jnp.concatenate(dnw, axis=1), axis=0, keepdims=True)
        dy = dyz * sz
        dz_ref[...] = (dyz * y * (sig * (1.0 + z * (1.0 - sig)))).astype(BF16)
        dy_b = dy.astype(BF16)

        tril = _iota((CH, CH), 1) <= _iota((CH, CH), 0)
        lane_a = _iota((CH, LANE), 1) < 64
        cbm = [_dot_nt(cg[g], bg[g]) for g in range(2)]
        dcbm = [jnp.zeros((CH, CH), F32), jnp.zeros((CH, CH), F32)]
        seg_rows = jnp.zeros((CH, LANE), F32)
        seg_cols = jnp.zeros((LANE, CH), F32)
        row_id, col_id = _iota((CH, LANE), 0), _iota((CH, LANE), 1)
        dx_pairs = []
        for hp in range(NH // 2):
            g = hp // 4
            xp = xdt_b[:, hp * LANE:(hp + 1) * LANE]
            dyp_f = dy[:, hp * LANE:(hp + 1) * LANE]
            dyp = dy_b[:, hp * LANE:(hp + 1) * LANE]
            halves = []
            for k in range(2):
                h = 2 * hp + k
                lane = lane_a if k == 0 else jnp.logical_not(lane_a)
                dec = _decay(q, h, tril)
                gm = cbm[g] * dec
                dgm = _dot_nt(jnp.where(lane, dyp_f, 0.0).astype(BF16), xp)
                dcbm[g] = dcbm[g] + dgm * dec
                prod = dgm * gm
                seg_rows = jnp.where(col_id == h, jnp.sum(prod, axis=1, keepdims=True), seg_rows)
                seg_cols = jnp.where(row_id == h, jnp.sum(prod, axis=0, keepdims=True), seg_cols)
                halves.append(_dot_tn(gm.astype(BF16), dyp))
            dx_pairs.append(jnp.where(lane_a, halves[0], halves[1]))
        dxdt_diag = jnp.concatenate(dx_pairs, axis=1)

        qv = jnp.concatenate([_dot(bg[g], dhb[:, g * 512:(g + 1) * 512]) for g in range(2)], axis=1)
        y_off = jnp.concatenate([_dot(cg[g], hb[:, g * 512:(g + 1) * 512]) for g in range(2)], axis=1) * e_b
        xfq = xdt * f_b * qv
        dxdt = dxdt_diag + f_b * qv
        tdt = jnp.sum(dh_new * h_in, axis=0, keepdims=True) * t_b
        per_head = _pick(jnp.concatenate([xfq, dy * y_off, dxdt * xs, dy * xs, jnp.broadcast_to(tdt, (8, D))],
                                         axis=0), red)
        fdf, dyoff_h, dxdtxs_h, dyxs_h = [per_head[k * CH:(k + 1) * CH] for k in range(4)]
        dcs = seg_rows - seg_cols.T + dyoff_h - fdf
        last = per_head[4 * CH:4 * CH + 1] + jnp.sum(fdf, axis=0, keepdims=True)
        dcs = dcs + jnp.where(_iota((CH, LANE), 0) == CH - 1, last, 0.0)
        tri_t = (_iota((CH, CH), 1) >= _iota((CH, CH), 0)).astype(BF16)
        da = _pick_left(tri_t, dcs)
        ddt = da * q["a_row"] + dxdtxs_h
        dxs = dxdt * dt_b + db_ref[...] * dy
        ddt_raw = ddt * _sigmoid(q["pre"])
        dsm_ref[0:1, :] += jnp.sum(ddt_raw, axis=0, keepdims=True)
        dsm_ref[1:2, :] += jnp.sum(da * q["dt"], axis=0, keepdims=True) * q["a_row"]
        dsm_ref[2:3, :] += jnp.sum(dyxs_h, axis=0, keepdims=True)
        edy = (e_b * dy).astype(BF16)
        xf = (xdt * f_b).astype(BF16)
        dbs, dcs_g, dhs = [], [], []
        for g in range(2):
            sl = slice(g * 512, (g + 1) * 512)
            dcb_b = dcbm[g].astype(BF16)
            dcs_g.append(_dot(dcb_b, bg[g]) + _dot_nt(edy[:, sl], hb[:, sl]))
            dbs.append(_dot_tn(dcb_b, cg[g]) + _dot_nt(xf[:, sl], dhb[:, sl]))
            dhs.append(_dot_tn(cg[g], edy[:, sl]))
        dh_ref[...] = t_b * dh_new + jnp.concatenate(dhs, axis=1)
        dbc = jnp.concatenate(dbs + dcs_g, axis=1)

        def conv_bwd(dact, pre, taps, w, nxt_ref, lo):
            s = _sigmoid(pre)
            dconv = dact * (s * (1.0 + pre * (1.0 - s)))
            nxt8 = nxt_ref[...]
            row8 = _iota(nxt8.shape, 0)
            hi = lo + dconv.shape[1]
            dcb_ref[:, lo:hi] += jnp.sum(dconv, axis=0, keepdims=True)
            dx = w[3:4, :] * dconv
            for k in range(4):
                dcw_ref[k:k + 1, lo:hi] += jnp.sum(dconv * taps[k], axis=0, keepdims=True)
            for s_ in (1, 2, 3):
                rolled = pltpu.roll(dconv, CH - s_, 0)
                tail = jnp.where(row8 >= 8 - s_, pltpu.roll(nxt8, 8 - s_, 0), rolled[CH - 8:])
                dx = dx + w[3 - s_:4 - s_, :] * jnp.concatenate([rolled[:CH - 8], tail], axis=0)
            nxt_ref[...] = dconv[:8]
            return dx

        dx_ref[:, 0:D] = conv_bwd(dxs, q["xs_c"], q["xs_taps"], cw[:, :D], nxs_ref, 0).astype(BF16)
        dx_ref[:, D:D + 512] = conv_bwd(dbc, q["bc_c"], q["bc_taps"], cw[:, D:], nbc_ref, D).astype(BF16)
        dx_ref[:, D + 512:D + 640] = ddt_raw.astype(BF16)
        dx_ref[:, D + 640:] = jnp.zeros((CH, D - 640), BF16)

    def col(width, blk, prev=False):
        if prev:
            return pl.BlockSpec((8, width), lambda i: (jnp.maximum((NC - 1 - i) * (CH // 8) - 1, 0), blk))
        return pl.BlockSpec((CH, width), lambda i: (NC - 1 - i, blk))

    def full(a):
        return pl.BlockSpec(a.shape, lambda i: (0,) * len(a.shape))

    acc_shapes = [SDS((4, 1536), F32), SDS((1, 1536), F32), SDS((8, LANE), F32), SDS((1, D), F32)]
    return pl.pallas_call(
        body, name="ssm_bwd", grid=(NC,),
        in_specs=[col(D, 5), col(D, 5, True), col(512, 12), col(512, 12, True), col(LANE, 52), col(D, 4),
                  col(D, 1), col(D, 0), pl.BlockSpec((None, CH, D), lambda i: (NC - 1 - i, 0, 0)),
                  full(cw), full(cb), full(dtb), full(alog), full(d_b), full(nw)],
        out_specs=[col(D, 0), col(2 * D, 0)] + [full(a) for a in acc_shapes],
        out_shape=[SDS((S, D), BF16), SDS((S, 2 * D), BF16)] + acc_shapes,
        scratch_shapes=[pltpu.VMEM((CH, D), F32), pltpu.VMEM((8, D), F32), pltpu.VMEM((8, 512), F32)],
        compiler_params=_cp(("arbitrary",)),
    )(proj, proj, proj, proj, proj, proj, dmix, y_save, states, cw, cb, dtb, alog, d_b, nw)


def _outproj_loss(mix, w_out, x, tgt, nw):
    tm = 256

    def body(mix_ref, w_ref, x_ref, t_ref, nw_ref, dy_ref, dmix_ref, dw_ref, dnw_ref, loss_ref):
        @pl.when(pl.program_id(0) == 0)
        def _():
            dw_ref[...] = jnp.zeros_like(dw_ref)
            dnw_ref[...] = jnp.zeros_like(dnw_ref)
            loss_ref[...] = jnp.zeros_like(loss_ref)

        mixv, w = mix_ref[...], w_ref[...]
        out = _dot(mixv, w)
        r = lax.rsqrt(jnp.mean(out * out, axis=-1, keepdims=True) + EPS)
        nh = out * r
        nw_v = nw_ref[...]
        err = x_ref[...] + nh * nw_v - t_ref[...]
        loss_ref[...] += 0.5 * jnp.sum(jnp.mean(err * err, axis=-1, keepdims=True), axis=0, keepdims=True)
        dy = err * (1.0 / D)
        dy_ref[...] = dy
        dnw_ref[...] += jnp.sum(dy * nh, axis=0, keepdims=True)
        gdn = dy * nw_v
        dout = (r * (gdn - nh * jnp.mean(gdn * nh, axis=-1, keepdims=True))).astype(BF16)
        dmix_ref[...] = _dot_nt(dout, w)
        dw_ref[...] += _dot_tn(mixv, dout)

    row = lambda w: pl.BlockSpec((tm, w), lambda i: (i, 0))
    full = lambda s: pl.BlockSpec(s, lambda i: (0, 0))
    return pl.pallas_call(
        body, name="outproj_loss", grid=(S // tm,),
        in_specs=[row(2 * D), full((2 * D, D)), row(D), row(D), full((1, D))],
        out_specs=[row(D), row(2 * D), full((2 * D, D)), full((1, D)), full((1, LANE))],
        out_shape=[SDS((S, D), F32), SDS((S, 2 * D), F32), SDS((2 * D, D), F32), SDS((1, D), F32),
                   SDS((1, LANE), F32)],
        compiler_params=_cp(("arbitrary",)),
    )(mix, w_out, x, tgt, nw)


def _inproj_bwd_dx(srcs, dxbcdt, w_all, x, dy, nw, hosted=None):
    tm = 512
    nk = DP // D
    n_host = len(hosted.arrays) if hosted else 0

    def body(*refs):
        src_refs = refs[:nk]
        w_ref, x_ref, dy_ref, nw_ref = refs[nk:nk + 4]
        host_in, refs = refs[nk + 4:nk + 4 + n_host], refs[nk + 4 + n_host:]
        gx_ref, dnw_ref = refs[:2]
        host_out, acc_ref, host_sems = refs[2:2 + n_host], refs[2 + n_host], refs[3 + n_host:]
        i, kk = pl.program_id(0), pl.program_id(1)
        if hosted:
            pl.when((i == 0) & (kk == 0))(lambda: hosted.start(host_in, host_out, host_sems))

        @pl.when((i == 0) & (kk == 0))
        def _():
            dnw_ref[...] = jnp.zeros_like(dnw_ref)

        @pl.when(kk == 0)
        def _():
            acc_ref[...] = jnp.zeros_like(acc_ref)

        for s, ref in enumerate(src_refs):
            @pl.when(kk == s)
            def _(ref=ref):
                acc_ref[...] += _dot_nt(ref[...], w_ref[...])

        @pl.when(kk == nk - 1)
        def _():
            xf, du, nw_v = x_ref[...], acc_ref[...], nw_ref[...]
            r = lax.rsqrt(jnp.mean(xf * xf, axis=-1, keepdims=True) + EPS)
            xh = xf * r
            dnw_ref[...] += jnp.sum(du * xh, axis=0, keepdims=True)
            gdu = du * nw_v
            gx_ref[...] = r * (gdu - xh * jnp.mean(gdu * xh, axis=-1, keepdims=True)) + dy_ref[...]

        if hosted:
            pl.when((i == S // tm - 1) & (kk == nk - 1))(lambda: hosted.finish(host_in, host_out, host_sems))

    row = pl.BlockSpec((tm, D), lambda i, k: (i, 0))
    row1 = pl.BlockSpec((tm, D), lambda i, k: (i, 1))
    one = pl.BlockSpec((1, D), lambda i, k: (0, 0))
    args = [*srcs, dxbcdt, dxbcdt, w_all, x, dy, nw]
    in_specs = [row] * len(srcs) + [row, row1, pl.BlockSpec((D, D), lambda i, k: (0, k)), row, row, one]
    out_specs, out_shape = [row, one], [SDS((S, D), F32), SDS((1, D), F32)]
    scratch = [pltpu.VMEM((tm, D), F32)]
    if hosted:
        args += hosted.arrays
        in_specs += [ANY] * n_host
        out_specs += [ANY] * n_host
        out_shape += hosted.out_shape
        scratch += hosted.scratch
    outs = pl.pallas_call(
        body, name="inproj_bwd_dx", grid=(S // tm, nk),
        in_specs=in_specs, out_specs=out_specs, out_shape=out_shape, scratch_shapes=scratch,
        compiler_params=_cp(("arbitrary", "arbitrary")),
    )(*args)
    return (outs[:2], outs[2:]) if hosted else outs


def _dw(u, dsec, name):
    ts = 512
    ncol = dsec.shape[1] // D

    def body(u_ref, d_ref, o_ref):
        @pl.when(pl.program_id(1) == 0)
        def _():
            o_ref[...] = jnp.zeros_like(o_ref)

        o_ref[...] += _dot_tn(u_ref[...], d_ref[...])

    return pl.pallas_call(
        body, name=name, grid=(ncol, S // ts),
        in_specs=[pl.BlockSpec((ts, D), lambda j, i: (i, 0)), pl.BlockSpec((ts, D), lambda j, i: (i, j))],
        out_specs=pl.BlockSpec((D, D), lambda j, i: (0, j)),
        out_shape=SDS((D, ncol * D), F32),
        compiler_params=_cp(("parallel", "arbitrary")),
    )(u, dsec)


def _place():
    x, y, c = lax.axis_index("x"), lax.axis_index("y"), lax.axis_index("c")
    return x, y, c, 2 * x + y


def _chip_of(x, y, k):
    px = 1 - x if k & 2 else x
    py = 1 - y if k & 1 else y
    return px, py, 2 * px + py


def _remote(src, dst, send_sem, recv_sem, dev):
    return pltpu.make_async_remote_copy(src_ref=src, dst_ref=dst, send_sem=send_sem, recv_sem=recv_sem,
                                        device_id=dev, device_id_type=MESH)


def _gather_weights(w_in_b, w_out_b, conv_w):
    def body(win_ref, wout_ref, cw_ref, gin_ref, gout_ref, gcw_ref, send, recv, csend, crecv):
        x, y, c, j = _place()
        me, sib = (x, y, c), (x, y, 1 - c)
        nbr = {"x": _chip_of(x, y, 2), "y": _chip_of(x, y, 1)}
        diag = _chip_of(x, y, 3)[2]
        conv = [_remote(cw_ref, gcw_ref.at[j], csend.at[k - 1], crecv.at[k - 1], (*_chip_of(x, y, k)[:2], c))
                for k in (1, 2, 3)]
        for cp in conv:
            cp.start()
        started, arrivals = [], []
        pairs = ((win_ref, gin_ref), (wout_ref, gout_ref))

        def rows(src, n_quarter=None, sibling=False):
            half = src.shape[0] // 2
            base = (1 - c if sibling else c) * half
            return pl.ds(base, half) if n_quarter is None else pl.ds(base + n_quarter * (half // 2), half // 2)

        def sem(a, n):
            return send.at[8 * a + n], recv.at[8 * a + n]

        def go(cp):
            cp.start()
            started.append(cp)

        for a, (src, dst) in enumerate(pairs):
            for n, axis in enumerate("xy"):
                px, py, _ = nbr[axis]
                go(_remote(src.at[rows(src)], dst.at[j, rows(src)], *sem(a, n), (px, py, c)))
        for n, axis in enumerate("xy"):
            ox, oy, _ = nbr["y" if axis == "x" else "x"]
            pj = nbr[axis][2]
            for a, (src, dst) in enumerate(pairs):
                _remote(src.at[rows(src)], dst.at[pj, rows(src)], *sem(a, n), me).wait_recv()
                go(_remote(dst.at[pj, rows(src, n)], dst.at[pj, rows(src, n)], *sem(a, 2 + n), (ox, oy, c)))
                go(_remote(dst.at[pj, rows(src)], dst.at[pj, rows(src)], *sem(a, 4 + n), sib))
                arrivals.append(_remote(src.at[rows(src)], dst.at[pj, rows(src, None, True)], *sem(a, 4 + n), me))
        for n in range(2):
            for a, (src, dst) in enumerate(pairs):
                part = rows(src, n)
                _remote(dst.at[diag, part], dst.at[diag, part], *sem(a, 2 + n), me).wait_recv()
                go(_remote(dst.at[diag, part], dst.at[diag, part], *sem(a, 6 + n), sib))
                sib_part = rows(src, n, True)
                arrivals.append(_remote(dst.at[diag, sib_part], dst.at[diag, sib_part], *sem(a, 6 + n), me))
        for cp in arrivals:
            cp.wait_recv()
        for k in (1, 2, 3):
            pj = _chip_of(x, y, k)[2]
            _remote(cw_ref, gcw_ref.at[pj], csend.at[k - 1], crecv.at[k - 1], me).wait_recv()
        for cp in started + conv:
            cp.wait_send()

    return pl.pallas_call(
        body, name="gather_weights",
        in_specs=[ANY, ANY, ANY], out_specs=[ANY, ANY, ANY],
        out_shape=[SDS((4,) + w_in_b.shape, BF16), SDS((4,) + w_out_b.shape, BF16), SDS((4,) + conv_w.shape, F32)],
        scratch_shapes=[pltpu.SemaphoreType.DMA((16,)), pltpu.SemaphoreType.DMA((16,)),
                        pltpu.SemaphoreType.DMA((3,)), pltpu.SemaphoreType.DMA((3,))],
        compiler_params=pltpu.CompilerParams(has_side_effects=True),
    )(w_in_b, w_out_b, conv_w)


def _pair_exchange(arrays, name):
    halves = [a.shape[1] // 2 for a in arrays]
    n = len(arrays)

    def body(*refs):
        x, y, c, _ = _place()
        send, recv = refs[2 * n:]
        cps = [_remote(refs[k].at[:, pl.ds((1 - c) * halves[k], halves[k])], refs[n + k], send.at[k], recv.at[k],
                       (x, y, 1 - c)) for k in range(n)]
        for cp in cps:
            cp.start()
        for cp in cps:
            cp.wait()

    return pl.pallas_call(
        body, name=name, in_specs=[ANY] * n, out_specs=[ANY] * n,
        out_shape=[SDS((a.shape[0], h, a.shape[2]), F32) for a, h in zip(arrays, halves)],
        scratch_shapes=[pltpu.SemaphoreType.DMA((n,)), pltpu.SemaphoreType.DMA((n,))],
        compiler_params=pltpu.CompilerParams(has_side_effects=True),
    )(*arrays)


def _pair_sum(cidx, g, r, name):
    n, half, width = r.shape
    tr = min(half, 256)
    nt = half // tr

    def body(c_ref, g_ref, r_ref, o_ref):
        del c_ref
        o_ref[...] = (g_ref[...] + r_ref[...]).astype(BF16)

    return pl.pallas_call(
        body, name=name,
        grid_spec=pltpu.PrefetchScalarGridSpec(
            num_scalar_prefetch=1, grid=(n, nt),
            in_specs=[pl.BlockSpec((None, tr, width), lambda s, t, c: (s, c[0] * nt + t, 0)),
                      pl.BlockSpec((None, tr, width), lambda s, t, c: (s, t, 0))],
            out_specs=pl.BlockSpec((None, tr, width), lambda s, t, c: (s, t, 0))),
        out_shape=SDS(r.shape, BF16),
        compiler_params=_cp(("parallel", "parallel")),
    )(cidx, g, r)


class _ChipExchange:
    def __init__(self, arrays):
        self.arrays = list(arrays)
        self.out_shape = [SDS(a.shape, BF16) for a in self.arrays]
        self.scratch = [pltpu.SemaphoreType.DMA((3 * len(self.arrays),)) for _ in range(2)]

    def _copies(self, ins, outs, sems):
        x, y, c, j = _place()
        send, recv = sems
        for a, (src, dst) in enumerate(zip(ins, outs)):
            for k in (1, 2, 3):
                px, py, pj = _chip_of(x, y, k)
                n = 3 * a + k - 1
                yield (_remote(src.at[pj], dst.at[j], send.at[n], recv.at[n], (px, py, c)),
                       _remote(src.at[pj], dst.at[pj], send.at[n], recv.at[n], (x, y, c)))

    def start(self, ins, outs, sems):
        for send, _ in self._copies(ins, outs, sems):
            send.start()

    def finish(self, ins, outs, sems):
        for send, arrival in self._copies(ins, outs, sems):
            arrival.wait_recv()
            send.wait_send()


def _small_exchange(small):
    def body(sm_ref, rs_ref, send, recv, lsem):
        x, y, c, j = _place()
        me = 2 * j + c
        local = pltpu.make_async_copy(sm_ref, rs_ref.at[me], lsem)
        local.start()
        cps = []
        for k in range(1, 8):
            px, py, _ = _chip_of(x, y, k >> 1)
            pc = 1 - c if k & 1 else c
            cps.append(_remote(sm_ref, rs_ref.at[me], send.at[k - 1], recv.at[k - 1], (px, py, pc)))
        for cp in cps:
            cp.start()
        for k in range(1, 8):
            _, _, pj = _chip_of(x, y, k >> 1)
            pc = 1 - c if k & 1 else c
            _remote(sm_ref, rs_ref.at[2 * pj + pc], send.at[k - 1], recv.at[k - 1], (x, y, c)).wait_recv()
        for cp in cps:
            cp.wait_send()
        local.wait()

    return pl.pallas_call(
        body, name="small_exchange", in_specs=[ANY], out_specs=ANY,
        out_shape=SDS((8,) + small.shape, F32),
        scratch_shapes=[pltpu.SemaphoreType.DMA((7,)), pltpu.SemaphoreType.DMA((7,)), pltpu.SemaphoreType.DMA],
        compiler_params=pltpu.CompilerParams(has_side_effects=True),
    )(small)


def _slot_sum(r, name):
    n, rows, width = r.shape
    tr = min(rows, 256)

    def body(r_ref, o_ref):
        acc = r_ref[0].astype(F32)
        for s in range(1, n):
            acc = acc + r_ref[s].astype(F32)
        o_ref[...] = acc

    return pl.pallas_call(
        body, name=name, grid=(rows // tr,),
        in_specs=[pl.BlockSpec((n, tr, width), lambda t: (0, t, 0))],
        out_specs=pl.BlockSpec((tr, width), lambda t: (t, 0)),
        out_shape=SDS((rows, width), F32),
        compiler_params=_cp(("parallel",)),
    )(r)


def _chip_sum(chip_idx, recv, own, name):
    n, rows, width = recv.shape
    tr = min(rows, 256)

    def body(j_ref, r_ref, own_ref, o_ref):
        acc = None
        for s in range(n):
            term = jnp.where(j_ref[0] == s, own_ref[...], r_ref[s]).astype(F32)
            acc = term if acc is None else acc + term
        o_ref[...] = acc

    return pl.pallas_call(
        body, name=name,
        grid_spec=pltpu.PrefetchScalarGridSpec(
            num_scalar_prefetch=1, grid=(rows // tr,),
            in_specs=[pl.BlockSpec((n, tr, width), lambda t, j: (0, t, 0)),
                      pl.BlockSpec((None, tr, width), lambda t, j: (j[0], t, 0))],
            out_specs=pl.BlockSpec((tr, width), lambda t, j: (t, 0))),
        out_shape=SDS((rows, width), F32),
        compiler_params=_cp(("parallel",)),
    )(chip_idx, recv, own)


def _half_exchange(hw, ho):
    def body(hw_ref, ho_ref, tw_ref, to_ref, send, recv):
        x, y, c, _ = _place()
        sib = (x, y, 1 - c)
        cps = [_remote(hw_ref, tw_ref, send.at[0], recv.at[0], sib),
               _remote(ho_ref, to_ref, send.at[1], recv.at[1], sib)]
        for cp in cps:
            cp.start()
        for cp in cps:
            cp.wait()

    return pl.pallas_call(
        body, name="half_exchange", in_specs=[ANY, ANY], out_specs=[ANY, ANY],
        out_shape=[SDS(hw.shape, F32), SDS(ho.shape, F32)],
        scratch_shapes=[pltpu.SemaphoreType.DMA((2,)), pltpu.SemaphoreType.DMA((2,))],
        compiler_params=pltpu.CompilerParams(has_side_effects=True),
    )(hw, ho)


def _by_core(c, mine, theirs):
    return jnp.where(c == 0, jnp.concatenate([mine, theirs], axis=0), jnp.concatenate([theirs, mine], axis=0))


def _adamw(w, g, m, v, name):
    rows, width = w.shape
    tr = min(rows, 256)

    def body(w_ref, g_ref, m_ref, v_ref, d_ref, nm_ref, nv_ref):
        gv = g_ref[...]
        nm = ADAM_B1 * m_ref[...] + (1.0 - ADAM_B1) * gv
        nv = ADAM_B2 * v_ref[...] + (1.0 - ADAM_B2) * (gv * gv)
        m_hat = nm / (1.0 - ADAM_B1 ** ADAM_STEP)
        v_hat = nv / (1.0 - ADAM_B2 ** ADAM_STEP)
        d_ref[...] = -ADAM_LR * (m_hat / (jnp.sqrt(v_hat) + ADAM_EPS) + ADAM_WD * w_ref[...])
        nm_ref[...] = nm
        nv_ref[...] = nv

    t = pl.BlockSpec((tr, width), lambda i: (i, 0))
    return pl.pallas_call(
        body, name=name, grid=(rows // tr,), in_specs=[t] * 4, out_specs=[t] * 3,
        out_shape=[SDS(w.shape, F32)] * 3, compiler_params=_cp(("parallel",)),
    )(w, g, m, v)


def _rows128(a, rows):
    flat = a.reshape(-1)
    return jnp.pad(flat, (0, rows * LANE - flat.shape[0])).reshape(rows, LANE)


def _pack_small(conv_w, norm_pre, conv_b, ssm_norm, norm_post, dtb, alog, dsk, extra=None):
    cw_rows = 48 if conv_w.shape[-1] == 1536 else 16
    extra = jnp.zeros((1, LANE), F32) if extra is None else _rows128(extra, 1)
    vec = jnp.concatenate([_rows128(dtb, 1), _rows128(alog, 1), _rows128(dsk, 1), extra, jnp.zeros((4, LANE), F32)],
                          axis=0)
    return jnp.concatenate([_rows128(conv_w, cw_rows), _rows128(norm_pre, 8), _rows128(conv_b, 16),
                            _rows128(ssm_norm, 8), _rows128(norm_post, 8), vec], axis=0)


def _unpack_small(p, cw_cols):
    cw_rows = 48 if cw_cols == 1536 else 16
    o = cw_rows
    conv_w = p[:cw_rows].reshape(-1)[:4 * cw_cols].reshape(1, 4, cw_cols)
    norm_pre = p[o:o + 8].reshape(1, D)
    conv_b = p[o + 8:o + 24].reshape(-1)[:1536].reshape(1, 1536)
    ssm_norm = p[o + 24:o + 32].reshape(1, D)
    norm_post = p[o + 32:o + 40].reshape(1, D)
    vec = p[o + 40:o + 48]
    return conv_w, norm_pre, conv_b, ssm_norm, norm_post, vec[0:1, :NH], vec[1:2, :NH], vec[2:3, :NH], vec[3, 0]


def _pad_lanes(a):
    return jnp.pad(a, ((0, 0), (0, LANE - a.shape[1])))


class _GradReduce:
    def __init__(self, chip, ci):
        self.ci = ci
        self.cidx = jnp.reshape(ci, (1,)).astype(jnp.int32)
        self.chip_idx = jnp.reshape(chip, (1,)).astype(jnp.int32)

    def start(self, dw_all, dw_out):
        gw = jnp.stack([dw_all[:, k * SHARD:(k + 1) * SHARD] for k in range(4)])
        go = dw_out.reshape(4, D // 2, D)
        rw, ro = _pair_exchange([gw, go], "pair_exchange")
        self.own = [_pair_sum(self.cidx, gw, rw, "pair_sum_in"), _pair_sum(self.cidx, go, ro, "pair_sum_out")]
        return _ChipExchange(self.own)

    def done(self, got):
        self.got = got

    def result(self):
        half_in = _chip_sum(self.chip_idx, self.got[0], self.own[0], "chip_sum_in")
        half_out = _chip_sum(self.chip_idx, self.got[1], self.own[1], "chip_sum_out")
        their_in, their_out = _half_exchange(half_in, half_out)
        return _by_core(self.ci, half_in, their_in), _by_core(self.ci, half_out, their_out)


def kernel(x, norm_pre_w, w_in, conv_w, conv_b, dt_bias, a_log, d_skip, ssm_norm_w, w_out, norm_post_w, loss_target, m_norm_pre_w, m_w_in, m_conv_w, m_conv_b, m_dt_bias, m_a_log, m_d_skip, m_ssm_norm_w, m_w_out, m_norm_post_w, v_norm_pre_w, v_w_in, v_conv_w, v_conv_b, v_dt_bias, v_a_log, v_d_skip, v_ssm_norm_w, v_w_out, v_norm_post_w):
    xi, yi, ci = lax.axis_index("x"), lax.axis_index("y"), lax.axis_index("c")
    chip = 2 * xi + yi
    x2, tgt = x[0], loss_target[0]

    w_in_b, w_out_b = w_in[0].astype(BF16), w_out[0].astype(BF16)
    gin, gout, gcw = _gather_weights(w_in_b, w_out_b, conv_w[0])

    def whole(own, gathered, axis):
        return jnp.concatenate([jnp.where(chip == k, own, gathered[k]) for k in range(4)], axis=axis)

    w_all = jnp.concatenate([whole(w_in_b, gin, 1), jnp.zeros((D, DP - 4 * SHARD), BF16)], axis=1)
    w_out_all = whole(w_out_b, gout, 0)
    cw_all = whole(conv_w[0], gcw, 1)
    reduce = _GradReduce(chip, ci)
    grad_x, small = _local_step(x2, tgt, w_all, w_out_all, cw_all, norm_pre_w, conv_b, dt_bias, a_log, d_skip,
                                ssm_norm_w, norm_post_w, reduce)[:2]
    g_in, g_out = reduce.result()
    g_small = _slot_sum(_small_exchange(small), "small_sum")
    g_cw, g_npre, g_cb, g_nssm, g_npost, g_dtb, g_alog, g_dsk, loss = _unpack_small(g_small, 1536)
    g_cw = lax.dynamic_slice_in_dim(g_cw, chip * 384, 384, axis=2)

    d_in, nm_in, nv_in = _adamw(w_in[0], g_in, m_w_in[0], v_w_in[0], "adamw_in")
    d_out, nm_out, nv_out = _adamw(w_out[0], g_out, m_w_out[0], v_w_out[0], "adamw_out")
    packed = [_pack_small(*t) for t in (
        (conv_w, norm_pre_w, conv_b, ssm_norm_w, norm_post_w, dt_bias, a_log, d_skip),
        (g_cw, g_npre, g_cb, g_nssm, g_npost, g_dtb, g_alog, g_dsk),
        (m_conv_w, m_norm_pre_w, m_conv_b, m_ssm_norm_w, m_norm_post_w, m_dt_bias, m_a_log, m_d_skip),
        (v_conv_w, v_norm_pre_w, v_conv_b, v_ssm_norm_w, v_norm_post_w, v_dt_bias, v_a_log, v_d_skip))]
    small_out = [_unpack_small(p, 384)[:8] for p in _adamw(*packed, "adamw_small")]

    def ordered(cw_, npre, cb_, nssm, npost, dtb_, alog_, dsk_, big_in, big_out):
        return [npre, big_in[None], cw_, cb_, dtb_, alog_, dsk_, nssm, big_out[None], npost]

    grads = ordered(g_cw, g_npre, g_cb, g_nssm, g_npost, g_dtb, g_alog, g_dsk, g_in, g_out)
    deltas = ordered(*small_out[0], d_in, d_out)
    new_m = ordered(*small_out[1], nm_in, nm_out)
    new_v = ordered(*small_out[2], nv_in, nv_out)
    return (loss, grad_x[None], *grads, *deltas, *new_m, *new_v)


def _local_step(x2, tgt, w_all, w_out_all, cw_all, norm_pre_w, conv_b, dt_bias, a_log, d_skip, ssm_norm_w,
                norm_post_w, reduce=None):
    dtb, alog = _pad_lanes(dt_bias), _pad_lanes(a_log)
    d_b = jnp.repeat(d_skip, 64, axis=1)

    proj, u = _inproj_fwd(x2, norm_pre_w, w_all)
    mix, attn_pre, lse = _attn_fwd(proj, 1, _attn_fwd(proj, 4, _attn_fwd(proj, 16)), final=True)
    mix, y_save, states = _ssm_fwd(proj, mix, cw_all, conv_b, dtb, alog, d_b, ssm_norm_w)

    dy, dmix, dw_out, dnw_post, loss_part = _outproj_loss(mix, w_out_all, x2, tgt, norm_post_w)
    do, delta, dg = _attn_gate_bwd(dmix, attn_pre, proj)
    dz, dxbcdt, dcw, dcb, dvec, dnw_ssm = _ssm_bwd(proj, dmix, y_save, states, cw_all, conv_b, dtb, alog, d_b,
                                                   ssm_norm_w)
    acc = _attn_bwd(proj, do, lse, delta, 16, None, F32)
    acc = _attn_bwd(proj, do, lse, delta, 4, acc, F32)
    dq, dk, dv = _attn_bwd(proj, do, lse, delta, 1, acc, BF16)
    srcs = [dq, dk, dv, dg, dz]
    dws = [_dw(u, s, f"dw_in_{n}") for s, n in zip(srcs + [dxbcdt], ("q", "k", "v", "g", "z", "xbcdt"))]
    dw_all = jnp.concatenate(dws, axis=1)
    res = _inproj_bwd_dx(srcs, dxbcdt, w_all, x2, dy, norm_pre_w, reduce.start(dw_all, dw_out) if reduce else None)
    if reduce:
        res, got = res
        reduce.done(got)
    grad_x, dnw_pre = res
    small = _pack_small(dcw, dnw_pre, dcb, dnw_ssm, dnw_post, dvec[0:1, :NH], dvec[1:2, :NH], dvec[2:3, :NH],
                        loss_part[:, :1])
    return grad_x, small, dw_all, dw_out
```

```python
import functools

import jax
import jax.numpy as jnp
from jax import lax
from jax.experimental import pallas as pl
from jax.experimental.pallas import tpu as pltpu

F32 = jnp.float32
BF16 = jnp.bfloat16
MESH = pl.DeviceIdType.MESH
SDS = jax.ShapeDtypeStruct
ANY = pl.BlockSpec(memory_space=pl.ANY)

S = 4096
D = 1024
DP = 7168
SHARD = 1668
OFF_G, OFF_Z = 3072, 4096
NH = 16
CH = 128
NC = S // CH
EPS = 1e-6
NEG = -1e30
LANE = 128
VMEM_LIMIT = 48 * 1024 * 1024

ADAM_LR, ADAM_B1, ADAM_B2, ADAM_EPS, ADAM_WD, ADAM_STEP = 0.001, 0.9, 0.999, 1e-08, 0.01, 10


def _cp(sem, **kw):
    return pltpu.CompilerParams(dimension_semantics=sem, vmem_limit_bytes=VMEM_LIMIT, **kw)


def _dot(a, b):
    return jnp.dot(a, b, preferred_element_type=F32)


def _dot_nt(a, b):
    return lax.dot_general(a, b, (((1,), (1,)), ((), ())), preferred_element_type=F32)


def _dot_tn(a, b):
    return lax.dot_general(a, b, (((0,), (0,)), ((), ())), preferred_element_type=F32)


def _pieces(x, n):
    out = []
    for _ in range(n):
        p = x.astype(BF16)
        out.append(p)
        x = x - p.astype(F32)
    return out


def _pick(x, sel, n=2):
    parts = [_dot(p, sel) for p in _pieces(x, n)]
    return functools.reduce(jnp.add, parts)


def _pick_left(sel, x, n=3):
    parts = [_dot(sel, p) for p in _pieces(x, n)]
    return functools.reduce(jnp.add, parts)


def _sigmoid(v):
    return 0.5 * jnp.tanh(0.5 * v) + 0.5


def _iota(shape, dim):
    return lax.broadcasted_iota(jnp.int32, shape, dim)


def _inproj_fwd(x, nw, w_all):
    tm, tn = 1024, 1024

    def body(x_ref, nw_ref, w_ref, proj_ref, u_ref):
        @pl.when(pl.program_id(1) == 0)
        def _():
            xf = x_ref[...]
            r = lax.rsqrt(jnp.mean(xf * xf, axis=-1, keepdims=True) + EPS)
            u_ref[...] = (xf * r * nw_ref[...]).astype(BF16)

        proj_ref[...] = _dot(u_ref[...], w_ref[...])

    return pl.pallas_call(
        body, name="inproj_fwd", grid=(S // tm, DP // tn),
        in_specs=[pl.BlockSpec((tm, D), lambda i, j: (i, 0)), pl.BlockSpec((1, D), lambda i, j: (0, 0)),
                  pl.BlockSpec((D, tn), lambda i, j: (0, j))],
        out_specs=[pl.BlockSpec((tm, tn), lambda i, j: (i, j)), pl.BlockSpec((tm, D), lambda i, j: (i, 0))],
        out_shape=[SDS((S, DP), F32), SDS((S, D), BF16)],
        compiler_params=_cp(("parallel", "arbitrary")),
    )(x, nw, w_all)


ATTN_QB = {1: 16, 4: 4, 16: 1}


def _unit_rows(r, u, d):
    return pl.ds(r + d * CH * u, CH, stride=d) if d > 1 else pl.ds(CH * u, CH)


def _for_units(d, qb, fn):
    for r in range(d):
        for u in range(qb):
            fn(r, u)


def _attn_mask(has_prev):
    qi, kj = _iota((2 * CH, 2 * CH), 0) & (CH - 1), _iota((2 * CH, 2 * CH), 1)
    cur_ok = (kj >= CH) & (kj - CH <= qi)
    prev_ok = (kj < CH) & (kj >= qi)
    return cur_ok | (prev_ok & has_prev)


def _stack_heads(v, lane_a):
    return jnp.concatenate([jnp.where(lane_a, v, 0.0), jnp.where(lane_a, 0.0, v)], axis=0).astype(BF16)


def _attn_specs(d, qb, lag_out):
    rows, prows = CH * d * qb, CH * d
    nb = S // rows
    last = nb - 1

    def cur(off):
        return pl.BlockSpec((rows, LANE), lambda c, i: (jnp.minimum(i, last), off + c))

    def prev(off):
        return pl.BlockSpec((prows, LANE), lambda c, i: (jnp.clip(i * qb - 1, 0, S // prows - 1), off + c))

    lag = pl.BlockSpec((rows, LANE), lambda c, i: (jnp.clip(i - 1, 0, last), c)) if lag_out else None
    return nb, cur, prev, lag


def _gather16(src_ref, dense_ref, tmp_ref):
    for a in range(4):
        tmp_ref[...] = src_ref[pl.ds(a, 4 * CH, stride=4), :]
        for b in range(4):
            dense_ref[a + 4 * b] = tmp_ref[pl.ds(b, CH, stride=4), :]


def _scatter16(dense_ref, dst_ref, tmp_ref):
    for a in range(4):
        for b in range(4):
            tmp_ref[pl.ds(b, CH, stride=4), :] = dense_ref[a + 4 * b]
        dst_ref[pl.ds(a, 4 * CH, stride=4), :] = tmp_ref[...]


def _unit_index(r, u, d):
    return (r,) if d == 16 else (_unit_rows(r, u, d), slice(None))


def _unit_kv(p_ref, c_ref, r, u, d):
    prev = p_ref[_unit_index(r, 0, d)] if u == 0 else c_ref[_unit_index(r, u - 1, d)]
    return jnp.concatenate([prev, c_ref[_unit_index(r, u, d)]], axis=0).astype(BF16)


def _dense_scratch(d, n):
    return [pltpu.VMEM((16, CH, LANE), F32)] * n + [pltpu.VMEM((4 * CH, LANE), F32)] if d == 16 else []


def _attn_fwd(proj, d, prior=None, final=False):
    qb = ATTN_QB[d]
    nb, cur, prev, _ = _attn_specs(d, qb, False)
    n_prior = 2 if prior is not None else 0
    n_in, n_out = 5 + n_prior + final, 2 + final
    assert not (d == 16 and (n_prior or final))

    def body(*refs):
        ins, outs, scratch = refs[:n_in], refs[n_in:n_in + n_out], refs[n_in + n_out:]
        if d == 16:
            tmp_ref = scratch[-1]
            for src, dense in zip(ins, scratch):
                _gather16(src, dense, tmp_ref)
            block_outs, ins, outs = outs, scratch[:n_in], scratch[n_in:n_in + n_out]
        q_ref, kp_ref, kc_ref, vp_ref, vc_ref = ins[:5]
        prior_refs = ins[5:5 + n_prior]
        if final:
            g_ref, (mix_ref, o_ref, l_ref) = ins[-1], outs
        else:
            o_ref, l_ref = outs
        i = pl.program_id(1)
        lane_a = _iota((CH, LANE), 1) < 64
        mask_first, mask_rest = _attn_mask(i > 0), _attn_mask(True)

        def unit(r, u):
            at = _unit_index(r, u, d)
            q2 = _stack_heads(q_ref[at] * 0.125, lane_a)
            k2, v2 = _unit_kv(kp_ref, kc_ref, r, u, d), _unit_kv(vp_ref, vc_ref, r, u, d)
            s = jnp.where(mask_first if u == 0 else mask_rest, _dot_nt(q2, k2), NEG)
            m = jnp.max(s, axis=1, keepdims=True)
            p = jnp.exp(s - m)
            l = jnp.sum(p, axis=1, keepdims=True)
            o2 = _dot(p.astype(BF16), v2) / l
            lse2 = m + jnp.log(l)
            o = jnp.where(lane_a, o2[:CH], o2[CH:])
            lse = jnp.where(lane_a, lse2[:CH], lse2[CH:])
            if n_prior:
                o_a, l_a = prior_refs[0][at], prior_refs[1][at]
                top = jnp.maximum(l_a, lse)
                e_a, e_b = jnp.exp(l_a - top), jnp.exp(lse - top)
                tot = e_a + e_b
                o = (e_a * o_a + e_b * o) / tot
                lse = top + jnp.log(tot)
            o_ref[at] = o
            l_ref[at] = lse
            if final:
                g = g_ref[at]
                mix_ref[at] = (o * (g * _sigmoid(g))).astype(BF16)

        _for_units(d, qb, unit)
        if d == 16:
            for dense, dst in zip(outs, block_outs):
                _scatter16(dense, dst, tmp_ref)

    in_specs = [cur(0), prev(8), cur(8), prev(16), cur(16)] + [cur(0)] * n_prior
    args = [proj] * 5 + (list(prior) if n_prior else [])
    out_specs, out_shape = [cur(0), cur(0)], [SDS((S, D), F32), SDS((S, D), F32)]
    if final:
        assert d == 1
        in_specs.append(cur(OFF_G // LANE))
        args.append(proj)
        out_specs, out_shape = [cur(0)] + out_specs, [SDS((S, 2 * D), BF16)] + out_shape
    return pl.pallas_call(
        body, name=f"attn_fwd_d{d}", grid=(NH // 2, nb),
        in_specs=in_specs, out_specs=out_specs, out_shape=out_shape,
        scratch_shapes=_dense_scratch(d, n_in + n_out),
        compiler_params=_cp(("parallel", "arbitrary")),
    )(*args)


def _attn_gate_bwd(dmix, pre, proj):
    tm = 512

    def body(dm_ref, pre_ref, g_ref, do_ref, delta_ref, dg_ref):
        g, dm, pre_v = g_ref[...], dm_ref[...], pre_ref[...]
        sig = _sigmoid(g)
        do = dm * (g * sig)
        do_ref[...] = do
        dg_ref[...] = (dm * pre_v * (sig * (1.0 + g * (1.0 - sig)))).astype(BF16)
        prod = do * pre_v
        same_head = (_iota((LANE, LANE), 0) // 64 == _iota((LANE, LANE), 1) // 64).astype(BF16)
        for cb in range(D // LANE):
            delta_ref[:, cb * LANE:(cb + 1) * LANE] = _pick(prod[:, cb * LANE:(cb + 1) * LANE], same_head)

    t = pl.BlockSpec((tm, D), lambda i: (i, 0))
    return pl.pallas_call(
        body, name="attn_gate_bwd", grid=(S // tm,),
        in_specs=[t, t, pl.BlockSpec((tm, D), lambda i: (i, OFF_G // D))],
        out_specs=[t, t, t],
        out_shape=[SDS((S, D), F32), SDS((S, D), F32), SDS((S, D), BF16)],
        compiler_params=_cp(("parallel",)),
    )(dmix, pre, proj)


def _attn_bwd(proj, do, lse, delta, d, acc, out_dtype):
    qb = ATTN_QB[d]
    nb, cur, prev, lag = _attn_specs(d, qb, True)
    has_acc = acc is not None
    n_in = 11 if has_acc else 8
    assert not (d == 16 and (has_acc or out_dtype != F32))
    rows = CH * d * qb
    carry = (2, 16, CH, LANE) if d == 16 else (2, rows, LANE)

    def body(*refs):
        ins, (dq_ref, dk_ref, dv_ref), scratch = refs[:n_in], refs[n_in:n_in + 3], refs[n_in + 3:]
        ck_ref, cv_ref = scratch[:2]
        dq_f32 = dq_ref if out_dtype == F32 else scratch[2]
        i = pl.program_id(1)
        if d == 16:
            dense, dq_f32, tmp_ref = scratch[2:2 + n_in], scratch[2 + n_in], scratch[-1]

            @pl.when(i < nb)
            def _():
                for src, dst in zip(ins, dense):
                    _gather16(src, dst, tmp_ref)

            ins = dense
        q_ref, kp_ref, kc_ref, vp_ref, vc_ref, do_ref, lse_ref, dl_ref = ins[:8]
        if has_acc:
            aq_ref, ak_ref, av_ref = ins[8:11]
        slot = i & 1
        now_k, now_v, old_k, old_v = ck_ref.at[slot], cv_ref.at[slot], ck_ref.at[1 - slot], cv_ref.at[1 - slot]
        lane_a = _iota((CH, LANE), 1) < 64
        mask_first, mask_rest = _attn_mask(i > 0), _attn_mask(True)

        @pl.when(i == 0)
        def _():
            ck_ref[1] = jnp.zeros(carry[1:], F32)
            cv_ref[1] = jnp.zeros(carry[1:], F32)

        def unit(r, u):
            at = _unit_index(r, u, d)
            q2 = _stack_heads(q_ref[at] * 0.125, lane_a)
            do2 = _stack_heads(do_ref[at], lane_a)
            k2, v2 = _unit_kv(kp_ref, kc_ref, r, u, d), _unit_kv(vp_ref, vc_ref, r, u, d)
            lsev, dlv = lse_ref[at], dl_ref[at]
            lse2 = jnp.concatenate([lsev[:, 0:1], lsev[:, 64:65]], axis=0)
            dl2 = jnp.concatenate([dlv[:, 0:1], dlv[:, 64:65]], axis=0)
            p = jnp.exp(jnp.where(mask_first if u == 0 else mask_rest, _dot_nt(q2, k2), NEG) - lse2)
            ds = (p * (_dot_nt(do2, v2) - dl2)).astype(BF16)
            dq2 = _dot(ds, k2)
            dk2 = _dot_tn(ds, q2)
            dv2 = _dot_tn(p.astype(BF16), do2)
            dq = jnp.where(lane_a, dq2[:CH], dq2[CH:]) * 0.125
            if has_acc:
                dq = dq + aq_ref[at]
            dq_f32[at] = dq
            if u == 0:
                before = _unit_index(r, qb - 1, d)
                old_k[before] += dk2[:CH]
                old_v[before] += dv2[:CH]
            else:
                before = _unit_index(r, u - 1, d)
                now_k[before] += dk2[:CH]
                now_v[before] += dv2[:CH]
            now_k[at] = dk2[CH:]
            now_v[at] = dv2[CH:]

        @pl.when(i < nb)
        def _():
            _for_units(d, qb, unit)
            if d == 16:
                _scatter16(dq_f32, dq_ref, tmp_ref)
            elif out_dtype != F32:
                dq_ref[...] = dq_f32[...].astype(out_dtype)

        if d == 16:
            _scatter16(old_k, dk_ref, tmp_ref)
            _scatter16(old_v, dv_ref, tmp_ref)
        else:
            dk, dv = old_k[...], old_v[...]
            if has_acc:
                dk, dv = dk + ak_ref[...], dv + av_ref[...]
            dk_ref[...] = dk.astype(out_dtype)
            dv_ref[...] = dv.astype(out_dtype)

    in_specs = [cur(0), prev(8), cur(8), prev(16), cur(16), cur(0), cur(0), cur(0)]
    args = [proj, proj, proj, proj, proj, do, lse, delta]
    if has_acc:
        in_specs += [cur(0), lag, lag]
        args += list(acc)
    scratch = [pltpu.VMEM(carry, F32), pltpu.VMEM(carry, F32)]
    if d == 16:
        scratch += _dense_scratch(d, n_in + 1)
    elif out_dtype != F32:
        scratch.append(pltpu.VMEM((rows, LANE), F32))
    return pl.pallas_call(
        body, name=f"attn_bwd_d{d}", grid=(NH // 2, nb + 1),
        in_specs=in_specs, out_specs=[cur(0), lag, lag], out_shape=[SDS((S, D), out_dtype)] * 3,
        scratch_shapes=scratch, compiler_params=_cp(("parallel", "arbitrary")),
    )(*args)


def _conv_taps(cur, prev8, first):
    row8 = _iota(prev8.shape, 0)
    prev8 = jnp.where(first, 0.0, prev8)
    taps = []
    for s in (3, 2, 1):
        rolled = pltpu.roll(cur, s, 0)
        head = jnp.where(row8 < s, pltpu.roll(prev8, s, 0), rolled[:8])
        taps.append(jnp.concatenate([head, rolled[8:]], axis=0))
    return taps + [cur]


def _conv(taps, w, b):
    acc = b + w[0:1, :] * taps[0]
    for k in (1, 2, 3):
        acc = acc + w[k:k + 1, :] * taps[k]
    return acc


def _expand():
    return (_iota((LANE, D), 1) // 64 == _iota((LANE, D), 0)).astype(BF16)


def _reduce():
    return (_iota((D, LANE), 0) // 64 == _iota((D, LANE), 1)).astype(BF16)


def _ssd_common(xs_raw, xs_prev, bc_raw, bc_prev, dt_raw, first, cw, cb, dtb, alog):
    head_lane = _iota((CH, LANE), 1) < NH
    xs_taps = _conv_taps(xs_raw, xs_prev, first)
    bc_taps = _conv_taps(bc_raw, bc_prev, first)
    xs_c = _conv(xs_taps, cw[:, :D], cb[:, :D])
    bc_c = _conv(bc_taps, cw[:, D:], cb[:, D:])
    xs = xs_c * _sigmoid(xs_c)
    bc = bc_c * _sigmoid(bc_c)
    pre = dt_raw + dtb
    dt = jnp.where(head_lane, jnp.maximum(pre, 0.0) + jnp.log(1.0 + jnp.exp(-jnp.abs(pre))), 0.0)
    a_row = jnp.where(head_lane[0:1], -jnp.exp(alog), 0.0)
    tri = (_iota((CH, CH), 1) <= _iota((CH, CH), 0)).astype(BF16)
    cs = _pick_left(tri, dt * a_row)
    cs_last = cs[CH - 1:CH, :]
    wide = _pick(jnp.concatenate([dt, jnp.exp(cs), jnp.exp(cs_last - cs)], axis=0), _expand())
    dt_b, e_b, f_b = wide[:CH], wide[CH:2 * CH], wide[2 * CH:]
    return dict(xs_taps=xs_taps, bc_taps=bc_taps, xs_c=xs_c, bc_c=bc_c, xs=xs, bc=bc, pre=pre, dt=dt,
                a_row=a_row, cs=cs, cs_t=cs.T, dt_b=dt_b, e_b=e_b, f_b=f_b, t_b=e_b[CH - 1:CH, :])


def _groups(bc):
    bcb = bc.astype(BF16)
    return [bcb[:, 0:128], bcb[:, 128:256]], [bcb[:, 256:384], bcb[:, 384:512]]


def _decay(q, h, tril):
    seg = q["cs"][:, h:h + 1] - q["cs_t"][h:h + 1, :]
    return jnp.exp(jnp.where(tril, seg, NEG))


def _ssm_fwd(proj, mix, cw, cb, dtb, alog, d_b, nw):
    def body(xs_ref, xsp_ref, bc_ref, bcp_ref, dt_ref, z_ref, cw_ref, cb_ref, dtb_ref, alog_ref, db_ref, nw_ref,
             mix_in_ref, mix_ref, y_ref, st_ref, h_ref):
        del mix_in_ref
        i = pl.program_id(0)

        @pl.when(i == 0)
        def _():
            h_ref[...] = jnp.zeros_like(h_ref)

        q = _ssd_common(xs_ref[...], xsp_ref[...], bc_ref[...], bcp_ref[...], dt_ref[...], i == 0,
                        cw_ref[...], cb_ref[...], dtb_ref[...], alog_ref[...])
        bg, cg = _groups(q["bc"])
        xs = q["xs"]
        xdt = xs * q["dt_b"]
        xdt_b = xdt.astype(BF16)
        h_in = h_ref[...]
        st_ref[...] = h_in
        hb = h_in.astype(BF16)
        tril = _iota((CH, CH), 1) <= _iota((CH, CH), 0)
        lane_a = _iota((CH, LANE), 1) < 64
        cbm = [_dot_nt(cg[g], bg[g]) for g in range(2)]
        pairs = []
        for hp in range(NH // 2):
            xp = xdt_b[:, hp * LANE:(hp + 1) * LANE]
            ya = _dot((cbm[hp // 4] * _decay(q, 2 * hp, tril)).astype(BF16), xp)
            yb = _dot((cbm[hp // 4] * _decay(q, 2 * hp + 1, tril)).astype(BF16), xp)
            pairs.append(jnp.where(lane_a, ya, yb))
        y_diag = jnp.concatenate(pairs, axis=1)
        y_off = jnp.concatenate([_dot(cg[g], hb[:, g * 512:(g + 1) * 512]) for g in range(2)], axis=1) * q["e_b"]
        y = y_diag + y_off + db_ref[...] * xs
        y_ref[...] = y
        xf = (xdt * q["f_b"]).astype(BF16)
        h_ref[...] = q["t_b"] * h_in + jnp.concatenate(
            [_dot_tn(bg[g], xf[:, g * 512:(g + 1) * 512]) for g in range(2)], axis=1)
        z = z_ref[...]
        yz = y * (z * _sigmoid(z))
        outs = []
        for g in range(2):
            v = yz[:, g * 512:(g + 1) * 512]
            outs.append(v * lax.rsqrt(jnp.mean(v * v, axis=-1, keepdims=True) + EPS))
        mix_ref[...] = (jnp.concatenate(outs, axis=1) * nw_ref[...]).astype(BF16)

    def col(width, blk, prev=False):
        if prev:
            return pl.BlockSpec((8, width), lambda i: (jnp.maximum(i * (CH // 8) - 1, 0), blk))
        return pl.BlockSpec((CH, width), lambda i: (i, blk))

    def full(a):
        return pl.BlockSpec(a.shape, lambda i: (0,) * a.ndim)

    return pl.pallas_call(
        body, name="ssm_fwd", grid=(NC,),
        in_specs=[col(D, 5), col(D, 5, True), col(512, 12), col(512, 12, True), col(LANE, 52), col(D, 4),
                  full(cw), full(cb), full(dtb), full(alog), full(d_b), full(nw), ANY],
        out_specs=[col(D, 1), col(D, 0), pl.BlockSpec((None, CH, D), lambda i: (i, 0, 0))],
        out_shape=[SDS((S, 2 * D), BF16), SDS((S, D), F32), SDS((NC, CH, D), F32)],
        scratch_shapes=[pltpu.VMEM((CH, D), F32)],
        input_output_aliases={12: 0},
        compiler_params=_cp(("arbitrary",)),
    )(proj, proj, proj, proj, proj, proj, cw, cb, dtb, alog, d_b, nw, mix)


def _ssm_bwd(proj, dmix, y_save, states, cw, cb, dtb, alog, d_b, nw):
    def body(xs_ref, xsp_ref, bc_ref, bcp_ref, dt_ref, z_ref, dn_ref, y_ref, st_ref,
             cw_ref, cb_ref, dtb_ref, alog_ref, db_ref, nw_ref,
             dz_ref, dx_ref, dcw_ref, dcb_ref, dsm_ref, dnw_ref, dh_ref, nxs_ref, nbc_ref):
        i = pl.program_id(0)
        ci = NC - 1 - i

        @pl.when(i == 0)
        def _():
            for ref in (dcw_ref, dcb_ref, dsm_ref, dnw_ref, dh_ref, nxs_ref, nbc_ref):
                ref[...] = jnp.zeros_like(ref)

        cw, cb = cw_ref[...], cb_ref[...]
        q = _ssd_common(xs_ref[...], xsp_ref[...], bc_ref[...], bcp_ref[...], dt_ref[...], ci == 0,
                        cw, cb, dtb_ref[...], alog_ref[...])
        bg, cg = _groups(q["bc"])
        xs, dt_b, e_b, f_b, t_b = q["xs"], q["dt_b"], q["e_b"], q["f_b"], q["t_b"]
        xdt = xs * dt_b
        xdt_b = xdt.astype(BF16)
        h_in = st_ref[...]
        hb = h_in.astype(BF16)
        dh_new = dh_ref[...]
        dhb = dh_new.astype(BF16)
        red = _reduce()

        z, y, dn, nw_v = z_ref[...], y_ref[...], dn_ref[...], nw_ref[...]
        sig = _sigmoid(z)
        sz = z * sig
        yz = y * sz
        gdn = dn * nw_v
        dyz, dnw = [], []
        for g in range(2):
            v, gv = yz[:, g * 512:(g + 1) * 512], gdn[:, g * 512:(g + 1) * 512]
            r = lax.rsqrt(jnp.mean(v * v, axis=-1, keepdims=True) + EPS)
            dnw.append(dn[:, g * 512:(g + 1) * 512] * v * r)
            dyz.append(r * (gv - v * (r * r) * jnp.mean(gv * v, axis=-1, keepdims=True)))
        dyz = jnp.concatenate(dyz, axis=1)
        dnw_ref[...] += jnp.sum(jnp.concatenate(dnw, axis=1), axis=0, keepdims=True)
        dy = dyz * sz
        dz_ref[...] = (dyz * y * (sig * (1.0 + z * (1.0 - sig)))).astype(BF16)
        dy_b = dy.astype(BF16)

        tril = _iota((CH, CH), 1) <= _iota((CH, CH), 0)
        lane_a = _iota((CH, LANE), 1) < 64
        cbm = [_dot_nt(cg[g], bg[g]) for g in range(2)]
        dcbm = [jnp.zeros((CH, CH), F32), jnp.zeros((CH, CH), F32)]
        seg_rows = jnp.zeros((CH, LANE), F32)
        seg_cols = jnp.zeros((LANE, CH), F32)
        row_id, col_id = _iota((CH, LANE), 0), _iota((CH, LANE), 1)
        dx_pairs = []
        for hp in range(NH // 2):
            g = hp // 4
            xp = xdt_b[:, hp * LANE:(hp + 1) * LANE]
            dyp_f = dy[:, hp * LANE:(hp + 1) * LANE]
            dyp = dy_b[:, hp * LANE:(hp + 1) * LANE]
            halves = []
            for k in range(2):
                h = 2 * hp + k
                lane = lane_a if k == 0 else jnp.logical_not(lane_a)
                dec = _decay(q, h, tril)
                gm = cbm[g] * dec
                dgm = _dot_nt(jnp.where(lane, dyp_f, 0.0).astype(BF16), xp)
                dcbm[g] = dcbm[g] + dgm * dec
                prod = dgm * gm
                seg_rows = jnp.where(col_id == h, jnp.sum(prod, axis=1, keepdims=True), seg_rows)
                seg_cols = jnp.where(row_id == h, jnp.sum(prod, axis=0, keepdims=True), seg_cols)
                halves.append(_dot_tn(gm.astype(BF16), dyp))
            dx_pairs.append(jnp.where(lane_a, halves[0], halves[1]))
        dxdt_diag = jnp.concatenate(dx_pairs, axis=1)

        qv = jnp.concatenate([_dot(bg[g], dhb[:, g * 512:(g + 1) * 512]) for g in range(2)], axis=1)
        y_off = jnp.concatenate([_dot(cg[g], hb[:, g * 512:(g + 1) * 512]) for g in range(2)], axis=1) * e_b
        xfq = xdt * f_b * qv
        dxdt = dxdt_diag + f_b * qv
        tdt = jnp.sum(dh_new * h_in, axis=0, keepdims=True) * t_b
        per_head = _pick(jnp.concatenate([xfq, dy * y_off, dxdt * xs, dy * xs, jnp.broadcast_to(tdt, (8, D))],
                                         axis=0), red)
        fdf, dyoff_h, dxdtxs_h, dyxs_h = [per_head[k * CH:(k + 1) * CH] for k in range(4)]
        dcs = seg_rows - seg_cols.T + dyoff_h - fdf
        last = per_head[4 * CH:4 * CH + 1] + jnp.sum(fdf, axis=0, keepdims=True)
        dcs = dcs + jnp.where(_iota((CH, LANE), 0) == CH - 1, last, 0.0)
        tri_t = (_iota((CH, CH), 1) >= _iota((CH, CH), 0)).astype(BF16)
        da = _pick_left(tri_t, dcs)
        ddt = da * q["a_row"] + dxdtxs_h
        dxs = dxdt * dt_b + db_ref[...] * dy
        ddt_raw = ddt * _sigmoid(q["pre"])
        dsm_ref[0:1, :] += jnp.sum(ddt_raw, axis=0, keepdims=True)
        dsm_ref[1:2, :] += jnp.sum(da * q["dt"], axis=0, keepdims=True) * q["a_row"]
        dsm_ref[2:3, :] += jnp.sum(dyxs_h, axis=0, keepdims=True)
        edy = (e_b * dy).astype(BF16)
        xf = (xdt * f_b).astype(BF16)
        dbs, dcs_g, dhs = [], [], []
        for g in range(2):
            sl = slice(g * 512, (g + 1) * 512)
            dcb_b = dcbm[g].astype(BF16)
            dcs_g.append(_dot(dcb_b, bg[g]) + _dot_nt(edy[:, sl], hb[:, sl]))
            dbs.append(_dot_tn(dcb_b, cg[g]) + _dot_nt(xf[:, sl], dhb[:, sl]))
            dhs.append(_dot_tn(cg[g], edy[:, sl]))
        dh_ref[...] = t_b * dh_new + jnp.concatenate(dhs, axis=1)
        dbc = jnp.concatenate(dbs + dcs_g, axis=1)

        def conv_bwd(dact, pre, taps, w, nxt_ref, lo):
            s = _sigmoid(pre)
            dconv = dact * (s * (1.0 + pre * (1.0 - s)))
            nxt8 = nxt_ref[...]
            row8 = _iota(nxt8.shape, 0)
            hi = lo + dconv.shape[1]
            dcb_ref[:, lo:hi] += jnp.sum(dconv, axis=0, keepdims=True)
            dx = w[3:4, :] * dconv
            for k in range(4):
                dcw_ref[k:k + 1, lo:hi] += jnp.sum(dconv * taps[k], axis=0, keepdims=True)
            for s_ in (1, 2, 3):
                rolled = pltpu.roll(dconv, CH - s_, 0)
                tail = jnp.where(row8 >= 8 - s_, pltpu.roll(nxt8, 8 - s_, 0), rolled[CH - 8:])
                dx = dx + w[3 - s_:4 - s_, :] * jnp.concatenate([rolled[:CH - 8], tail], axis=0)
            nxt_ref[...] = dconv[:8]
            return dx

        dx_ref[:, 0:D] = conv_bwd(dxs, q["xs_c"], q["xs_taps"], cw[:, :D], nxs_ref, 0).astype(BF16)
        dx_ref[:, D:D + 512] = conv_bwd(dbc, q["bc_c"], q["bc_taps"], cw[:, D:], nbc_ref, D).astype(BF16)
        dx_ref[:, D + 512:D + 640] = ddt_raw.astype(BF16)
        dx_ref[:, D + 640:] = jnp.zeros((CH, D - 640), BF16)

    def col(width, blk, prev=False):
        if prev:
            return pl.BlockSpec((8, width), lambda i: (jnp.maximum((NC - 1 - i) * (CH // 8) - 1, 0), blk))
        return pl.BlockSpec((CH, width), lambda i: (NC - 1 - i, blk))

    def full(a):
        return pl.BlockSpec(a.shape, lambda i: (0,) * len(a.shape))

    acc_shapes = [SDS((4, 1536), F32), SDS((1, 1536), F32), SDS((8, LANE), F32), SDS((1, D), F32)]
    return pl.pallas_call(
        body, name="ssm_bwd", grid=(NC,),
        in_specs=[col(D, 5), col(D, 5, True), col(512, 12), col(512, 12, True), col(LANE, 52), col(D, 4),
                  col(D, 1), col(D, 0), pl.BlockSpec((None, CH, D), lambda i: (NC - 1 - i, 0, 0)),
                  full(cw), full(cb), full(dtb), full(alog), full(d_b), full(nw)],
        out_specs=[col(D, 0), col(2 * D, 0)] + [full(a) for a in acc_shapes],
        out_shape=[SDS((S, D), BF16), SDS((S, 2 * D), BF16)] + acc_shapes,
        scratch_shapes=[pltpu.VMEM((CH, D), F32), pltpu.VMEM((8, D), F32), pltpu.VMEM((8, 512), F32)],
        compiler_params=_cp(("arbitrary",)),
    )(proj, proj, proj, proj, proj, proj, dmix, y_save, states, cw, cb, dtb, alog, d_b, nw)


def _outproj_loss(mix, w_out, x, tgt, nw):
    tm = 256

    def body(mix_ref, w_ref, x_ref, t_ref, nw_ref, dy_ref, dmix_ref, dw_ref, dnw_ref, loss_ref):
        @pl.when(pl.program_id(0) == 0)
        def _():
            dw_ref[...] = jnp.zeros_like(dw_ref)
            dnw_ref[...] = jnp.zeros_like(dnw_ref)
            loss_ref[...] = jnp.zeros_like(loss_ref)

        mixv, w = mix_ref[...], w_ref[...]
        out = _dot(mixv, w)
        r = lax.rsqrt(jnp.mean(out * out, axis=-1, keepdims=True) + EPS)
        nh = out * r
        nw_v = nw_ref[...]
        err = x_ref[...] + nh * nw_v - t_ref[...]
        loss_ref[...] += 0.5 * jnp.sum(jnp.mean(err * err, axis=-1, keepdims=True), axis=0, keepdims=True)
        dy = err * (1.0 / D)
        dy_ref[...] = dy
        dnw_ref[...] += jnp.sum(dy * nh, axis=0, keepdims=True)
        gdn = dy * nw_v
        dout = (r * (gdn - nh * jnp.mean(gdn * nh, axis=-1, keepdims=True))).astype(BF16)
        dmix_ref[...] = _dot_nt(dout, w)
        dw_ref[...] += _dot_tn(mixv, dout)

    row = lambda w: pl.BlockSpec((tm, w), lambda i: (i, 0))
    full = lambda s: pl.BlockSpec(s, lambda i: (0, 0))
    return pl.pallas_call(
        body, name="outproj_loss", grid=(S // tm,),
        in_specs=[row(2 * D), full((2 * D, D)), row(D), row(D), full((1, D))],
        out_specs=[row(D), row(2 * D), full((2 * D, D)), full((1, D)), full((1, LANE))],
        out_shape=[SDS((S, D), F32), SDS((S, 2 * D), F32), SDS((2 * D, D), F32), SDS((1, D), F32),
                   SDS((1, LANE), F32)],
        compiler_params=_cp(("arbitrary",)),
    )(mix, w_out, x, tgt, nw)


def _inproj_bwd_dx(srcs, dxbcdt, w_all, x, dy, nw, hosted=None):
    tm = 512
    nk = DP // D
    n_host = len(hosted.arrays) if hosted else 0

    def body(*refs):
        src_refs = refs[:nk]
        w_ref, x_ref, dy_ref, nw_ref = refs[nk:nk + 4]
        host_in, refs = refs[nk + 4:nk + 4 + n_host], refs[nk + 4 + n_host:]
        gx_ref, dnw_ref = refs[:2]
        host_out, acc_ref, host_sems = refs[2:2 + n_host], refs[2 + n_host], refs[3 + n_host:]
        i, kk = pl.program_id(0), pl.program_id(1)
        if hosted:
            pl.when((i == 0) & (kk == 0))(lambda: hosted.start(host_in, host_out, host_sems))

        @pl.when((i == 0) & (kk == 0))
        def _():
            dnw_ref[...] = jnp.zeros_like(dnw_ref)

        @pl.when(kk == 0)
        def _():
            acc_ref[...] = jnp.zeros_like(acc_ref)

        for s, ref in enumerate(src_refs):
            @pl.when(kk == s)
            def _(ref=ref):
                acc_ref[...] += _dot_nt(ref[...], w_ref[...])

        @pl.when(kk == nk - 1)
        def _():
            xf, du, nw_v = x_ref[...], acc_ref[...], nw_ref[...]
            r = lax.rsqrt(jnp.mean(xf * xf, axis=-1, keepdims=True) + EPS)
            xh = xf * r
            dnw_ref[...] += jnp.sum(du * xh, axis=0, keepdims=True)
            gdu = du * nw_v
            gx_ref[...] = r * (gdu - xh * jnp.mean(gdu * xh, axis=-1, keepdims=True)) + dy_ref[...]

        if hosted:
            pl.when((i == S // tm - 1) & (kk == nk - 1))(lambda: hosted.finish(host_in, host_out, host_sems))

    row = pl.BlockSpec((tm, D), lambda i, k: (i, 0))
    row1 = pl.BlockSpec((tm, D), lambda i, k: (i, 1))
    one = pl.BlockSpec((1, D), lambda i, k: (0, 0))
    args = [*srcs, dxbcdt, dxbcdt, w_all, x, dy, nw]
    in_specs = [row] * len(srcs) + [row, row1, pl.BlockSpec((D, D), lambda i, k: (0, k)), row, row, one]
    out_specs, out_shape = [row, one], [SDS((S, D), F32), SDS((1, D), F32)]
    scratch = [pltpu.VMEM((tm, D), F32)]
    if hosted:
        args += hosted.arrays
        in_specs += [ANY] * n_host
        out_specs += [ANY] * n_host
        out_shape += hosted.out_shape
        scratch += hosted.scratch
    outs = pl.pallas_call(
        body, name="inproj_bwd_dx", grid=(S // tm, nk),
        in_specs=in_specs, out_specs=out_specs, out_shape=out_shape, scratch_shapes=scratch,
        compiler_params=_cp(("arbitrary", "arbitrary")),
    )(*args)
    return (outs[:2], outs[2:]) if hosted else outs


def _dw(u, dsec, name):
    ts = 512
    ncol = dsec.shape[1] // D

    def body(u_ref, d_ref, o_ref):
        @pl.when(pl.program_id(1) == 0)
        def _():
            o_ref[...] = jnp.zeros_like(o_ref)

        o_ref[...] += _dot_tn(u_ref[...], d_ref[...])

    return pl.pallas_call(
        body, name=name, grid=(ncol, S // ts),
        in_specs=[pl.BlockSpec((ts, D), lambda j, i: (i, 0)), pl.BlockSpec((ts, D), lambda j, i: (i, j))],
        out_specs=pl.BlockSpec((D, D), lambda j, i: (0, j)),
        out_shape=SDS((D, ncol * D), F32),
        compiler_params=_cp(("parallel", "arbitrary")),
    )(u, dsec)


def _place():
    x, y, c = lax.axis_index("x"), lax.axis_index("y"), lax.axis_index("c")
    return x, y, c, 2 * x + y


def _chip_of(x, y, k):
    px = 1 - x if k & 2 else x
    py = 1 - y if k & 1 else y
    return px, py, 2 * px + py


def _remote(src, dst, send_sem, recv_sem, dev):
    return pltpu.make_async_remote_copy(src_ref=src, dst_ref=dst, send_sem=send_sem, recv_sem=recv_sem,
                                        device_id=dev, device_id_type=MESH)


def _gather_weights(w_in_b, w_out_b, conv_w):
    def body(win_ref, wout_ref, cw_ref, gin_ref, gout_ref, gcw_ref, send, recv, csend, crecv):
        x, y, c, j = _place()
        me, sib = (x, y, c), (x, y, 1 - c)
        nbr = {"x": _chip_of(x, y, 2), "y": _chip_of(x, y, 1)}
        diag = _chip_of(x, y, 3)[2]
        conv = [_remote(cw_ref, gcw_ref.at[j], csend.at[k - 1], crecv.at[k - 1], (*_chip_of(x, y, k)[:2], c))
                for k in (1, 2, 3)]
        for cp in conv:
            cp.start()
        started, arrivals = [], []
        pairs = ((win_ref, gin_ref), (wout_ref, gout_ref))

        def rows(src, n_quarter=None, sibling=False):
            half = src.shape[0] // 2
            base = (1 - c if sibling else c) * half
            return pl.ds(base, half) if n_quarter is None else pl.ds(base + n_quarter * (half // 2), half // 2)

        def sem(a, n):
            return send.at[8 * a + n], recv.at[8 * a + n]

        def go(cp):
            cp.start()
            started.append(cp)

        for a, (src, dst) in enumerate(pairs):
            for n, axis in enumerate("xy"):
                px, py, _ = nbr[axis]
                go(_remote(src.at[rows(src)], dst.at[j, rows(src)], *sem(a, n), (px, py, c)))
        for n, axis in enumerate("xy"):
            ox, oy, _ = nbr["y" if axis == "x" else "x"]
            pj = nbr[axis][2]
            for a, (src, dst) in enumerate(pairs):
                _remote(src.at[rows(src)], dst.at[pj, rows(src)], *sem(a, n), me).wait_recv()
                go(_remote(dst.at[pj, rows(src, n)], dst.at[pj, rows(src, n)], *sem(a, 2 + n), (ox, oy, c)))
                go(_remote(dst.at[pj, rows(src)], dst.at[pj, rows(src)], *sem(a, 4 + n), sib))
                arrivals.append(_remote(src.at[rows(src)], dst.at[pj, rows(src, None, True)], *sem(a, 4 + n), me))
        for n in range(2):
            for a, (src, dst) in enumerate(pairs):
                part = rows(src, n)
                _remote(dst.at[diag, part], dst.at[diag, part], *sem(a, 2 + n), me).wait_recv()
                go(_remote(dst.at[diag, part], dst.at[diag, part], *sem(a, 6 + n), sib))
                sib_part = rows(src, n, True)
                arrivals.append(_remote(dst.at[diag, sib_part], dst.at[diag, sib_part], *sem(a, 6 + n), me))
        for cp in arrivals:
            cp.wait_recv()
        for k in (1, 2, 3):
            pj = _chip_of(x, y, k)[2]
            _remote(cw_ref, gcw_ref.at[pj], csend.at[k - 1], crecv.at[k - 1], me).wait_recv()
        for cp in started + conv:
            cp.wait_send()

    return pl.pallas_call(
        body, name="gather_weights",
        in_specs=[ANY, ANY, ANY], out_specs=[ANY, ANY, ANY],
        out_shape=[SDS((4,) + w_in_b.shape, BF16), SDS((4,) + w_out_b.shape, BF16), SDS((4,) + conv_w.shape, F32)],
        scratch_shapes=[pltpu.SemaphoreType.DMA((16,)), pltpu.SemaphoreType.DMA((16,)),
                        pltpu.SemaphoreType.DMA((3,)), pltpu.SemaphoreType.DMA((3,))],
        compiler_params=pltpu.CompilerParams(has_side_effects=True),
    )(w_in_b, w_out_b, conv_w)


def _pair_exchange(arrays, name):
    halves = [a.shape[1] // 2 for a in arrays]
    n = len(arrays)

    def body(*refs):
        x, y, c, _ = _place()
        send, recv = refs[2 * n:]
        cps = [_remote(refs[k].at[:, pl.ds((1 - c) * halves[k], halves[k])], refs[n + k], send.at[k], recv.at[k],
                       (x, y, 1 - c)) for k in range(n)]
        for cp in cps:
            cp.start()
        for cp in cps:
            cp.wait()

    return pl.pallas_call(
        body, name=name, in_specs=[ANY] * n, out_specs=[ANY] * n,
        out_shape=[SDS((a.shape[0], h, a.shape[2]), F32) for a, h in zip(arrays, halves)],
        scratch_shapes=[pltpu.SemaphoreType.DMA((n,)), pltpu.SemaphoreType.DMA((n,))],
        compiler_params=pltpu.CompilerParams(has_side_effects=True),
    )(*arrays)


def _pair_sum(cidx, g, r, name):
    n, half, width = r.shape
    tr = min(half, 256)
    nt = half // tr

    def body(c_ref, g_ref, r_ref, o_ref):
        del c_ref
        o_ref[...] = (g_ref[...] + r_ref[...]).astype(BF16)

    return pl.pallas_call(
        body, name=name,
        grid_spec=pltpu.PrefetchScalarGridSpec(
            num_scalar_prefetch=1, grid=(n, nt),
            in_specs=[pl.BlockSpec((None, tr, width), lambda s, t, c: (s, c[0] * nt + t, 0)),
                      pl.BlockSpec((None, tr, width), lambda s, t, c: (s, t, 0))],
            out_specs=pl.BlockSpec((None, tr, width), lambda s, t, c: (s, t, 0))),
        out_shape=SDS(r.shape, BF16),
        compiler_params=_cp(("parallel", "parallel")),
    )(cidx, g, r)


class _ChipExchange:
    def __init__(self, arrays):
        self.arrays = list(arrays)
        self.out_shape = [SDS(a.shape, BF16) for a in self.arrays]
        self.scratch = [pltpu.SemaphoreType.DMA((3 * len(self.arrays),)) for _ in range(2)]

    def _copies(self, ins, outs, sems):
        x, y, c, j = _place()
        send, recv = sems
        for a, (src, dst) in enumerate(zip(ins, outs)):
            for k in (1, 2, 3):
                px, py, pj = _chip_of(x, y, k)
                n = 3 * a + k - 1
                yield (_remote(src.at[pj], dst.at[j], send.at[n], recv.at[n], (px, py, c)),
                       _remote(src.at[pj], dst.at[pj], send.at[n], recv.at[n], (x, y, c)))

    def start(self, ins, outs, sems):
        for send, _ in self._copies(ins, outs, sems):
            send.start()

    def finish(self, ins, outs, sems):
        for send, arrival in self._copies(ins, outs, sems):
            arrival.wait_recv()
            send.wait_send()


def _small_exchange(small):
    def body(sm_ref, rs_ref, send, recv, lsem):
        x, y, c, j = _place()
        me = 2 * j + c
        local = pltpu.make_async_copy(sm_ref, rs_ref.at[me], lsem)
        local.start()
        cps = []
        for k in range(1, 8):
            px, py, _ = _chip_of(x, y, k >> 1)
            pc = 1 - c if k & 1 else c
            cps.append(_remote(sm_ref, rs_ref.at[me], send.at[k - 1], recv.at[k - 1], (px, py, pc)))
        for cp in cps:
            cp.start()
        for k in range(1, 8):
            _, _, pj = _chip_of(x, y, k >> 1)
            pc = 1 - c if k & 1 else c
            _remote(sm_ref, rs_ref.at[2 * pj + pc], send.at[k - 1], recv.at[k - 1], (x, y, c)).wait_recv()
        for cp in cps:
            cp.wait_send()
        local.wait()

    return pl.pallas_call(
        body, name="small_exchange", in_specs=[ANY], out_specs=ANY,
        out_shape=SDS((8,) + small.shape, F32),
        scratch_shapes=[pltpu.SemaphoreType.DMA((7,)), pltpu.SemaphoreType.DMA((7,)), pltpu.SemaphoreType.DMA],
        compiler_params=pltpu.CompilerParams(has_side_effects=True),
    )(small)


def _slot_sum(r, name):
    n, rows, width = r.shape
    tr = min(rows, 256)

    def body(r_ref, o_ref):
        acc = r_ref[0].astype(F32)
        for s in range(1, n):
            acc = acc + r_ref[s].astype(F32)
        o_ref[...] = acc

    return pl.pallas_call(
        body, name=name, grid=(rows // tr,),
        in_specs=[pl.BlockSpec((n, tr, width), lambda t: (0, t, 0))],
        out_specs=pl.BlockSpec((tr, width), lambda t: (t, 0)),
        out_shape=SDS((rows, width), F32),
        compiler_params=_cp(("parallel",)),
    )(r)


def _chip_sum(chip_idx, recv, own, name):
    n, rows, width = recv.shape
    tr = min(rows, 256)

    def body(j_ref, r_ref, own_ref, o_ref):
        acc = None
        for s in range(n):
            term = jnp.where(j_ref[0] == s, own_ref[...], r_ref[s]).astype(F32)
            acc = term if acc is None else acc + term
        o_ref[...] = acc

    return pl.pallas_call(
        body, name=name,
        grid_spec=pltpu.PrefetchScalarGridSpec(
            num_scalar_prefetch=1, grid=(rows // tr,),
            in_specs=[pl.BlockSpec((n, tr, width), lambda t, j: (0, t, 0)),
                      pl.BlockSpec((None, tr, width), lambda t, j: (j[0], t, 0))],
            out_specs=pl.BlockSpec((tr, width), lambda t, j: (t, 0))),
        out_shape=SDS((rows, width), F32),
        compiler_params=_cp(("parallel",)),
    )(chip_idx, recv, own)


def _half_exchange(hw, ho):
    def body(hw_ref, ho_ref, tw_ref, to_ref, send, recv):
        x, y, c, _ = _place()
        sib = (x, y, 1 - c)
        cps = [_remote(hw_ref, tw_ref, send.at[0], recv.at[0], sib),
               _remote(ho_ref, to_ref, send.at[1], recv.at[1], sib)]
        for cp in cps:
            cp.start()
        for cp in cps:
            cp.wait()

    return pl.pallas_call(
        body, name="half_exchange", in_specs=[ANY, ANY], out_specs=[ANY, ANY],
        out_shape=[SDS(hw.shape, F32), SDS(ho.shape, F32)],
        scratch_shapes=[pltpu.SemaphoreType.DMA((2,)), pltpu.SemaphoreType.DMA((2,))],
        compiler_params=pltpu.CompilerParams(has_side_effects=True),
    )(hw, ho)


def _by_core(c, mine, theirs):
    return jnp.where(c == 0, jnp.concatenate([mine, theirs], axis=0), jnp.concatenate([theirs, mine], axis=0))


def _adamw(w, g, m, v, name):
    rows, width = w.shape
    tr = min(rows, 256)

    def body(w_ref, g_ref, m_ref, v_ref, d_ref, nm_ref, nv_ref):
        gv = g_ref[...]
        nm = ADAM_B1 * m_ref[...] + (1.0 - ADAM_B1) * gv
        nv = ADAM_B2 * v_ref[...] + (1.0 - ADAM_B2) * (gv * gv)
        m_hat = nm / (1.0 - ADAM_B1 ** ADAM_STEP)
        v_hat = nv / (1.0 - ADAM_B2 ** ADAM_STEP)
        d_ref[...] = -ADAM_LR * (m_hat / (jnp.sqrt(v_hat) + ADAM_EPS) + ADAM_WD * w_ref[...])
        nm_ref[...] = nm
        nv_ref[...] = nv

    t = pl.BlockSpec((tr, width), lambda i: (i, 0))
    return pl.pallas_call(
        body, name=name, grid=(rows // tr,), in_specs=[t] * 4, out_specs=[t] * 3,
        out_shape=[SDS(w.shape, F32)] * 3, compiler_params=_cp(("parallel",)),
    )(w, g, m, v)


def _rows128(a, rows):
    flat = a.reshape(-1)
    return jnp.pad(flat, (0, rows * LANE - flat.shape[0])).reshape(rows, LANE)


def _pack_small(conv_w, norm_pre, conv_b, ssm_norm, norm_post, dtb, alog, dsk, extra=None):
    cw_rows = 48 if conv_w.shape[-1] == 1536 else 16
    extra = jnp.zeros((1, LANE), F32) if extra is None else _rows128(extra, 1)
    vec = jnp.concatenate([_rows128(dtb, 1), _rows128(alog, 1), _rows128(dsk, 1), extra, jnp.zeros((4, LANE), F32)],
                          axis=0)
    return jnp.concatenate([_rows128(conv_w, cw_rows), _rows128(norm_pre, 8), _rows128(conv_b, 16),
                            _rows128(ssm_norm, 8), _rows128(norm_post, 8), vec], axis=0)


def _unpack_small(p, cw_cols):
    cw_rows = 48 if cw_cols == 1536 else 16
    o = cw_rows
    conv_w = p[:cw_rows].reshape(-1)[:4 * cw_cols].reshape(1, 4, cw_cols)
    norm_pre = p[o:o + 8].reshape(1, D)
    conv_b = p[o + 8:o + 24].reshape(-1)[:1536].reshape(1, 1536)
    ssm_norm = p[o + 24:o + 32].reshape(1, D)
    norm_post = p[o + 32:o + 40].reshape(1, D)
    vec = p[o + 40:o + 48]
    return conv_w, norm_pre, conv_b, ssm_norm, norm_post, vec[0:1, :NH], vec[1:2, :NH], vec[2:3, :NH], vec[3, 0]


def _pad_lanes(a):
    return jnp.pad(a, ((0, 0), (0, LANE - a.shape[1])))


class _GradReduce:
    def __init__(self, chip, ci):
        self.ci = ci
        self.cidx = jnp.reshape(ci, (1,)).astype(jnp.int32)
        self.chip_idx = jnp.reshape(chip, (1,)).astype(jnp.int32)

    def start(self, dw_all, dw_out):
        gw = jnp.stack([dw_all[:, k * SHARD:(k + 1) * SHARD] for k in range(4)])
        go = dw_out.reshape(4, D // 2, D)
        rw, ro = _pair_exchange([gw, go], "pair_exchange")
        self.own = [_pair_sum(self.cidx, gw, rw, "pair_sum_in"), _pair_sum(self.cidx, go, ro, "pair_sum_out")]
        return _ChipExchange(self.own)

    def done(self, got):
        self.got = got

    def result(self):
        half_in = _chip_sum(self.chip_idx, self.got[0], self.own[0], "chip_sum_in")
        half_out = _chip_sum(self.chip_idx, self.got[1], self.own[1], "chip_sum_out")
        their_in, their_out = _half_exchange(half_in, half_out)
        return _by_core(self.ci, half_in, their_in), _by_core(self.ci, half_out, their_out)


def kernel(x, norm_pre_w, w_in, conv_w, conv_b, dt_bias, a_log, d_skip, ssm_norm_w, w_out, norm_post_w, loss_target, m_norm_pre_w, m_w_in, m_conv_w, m_conv_b, m_dt_bias, m_a_log, m_d_skip, m_ssm_norm_w, m_w_out, m_norm_post_w, v_norm_pre_w, v_w_in, v_conv_w, v_conv_b, v_dt_bias, v_a_log, v_d_skip, v_ssm_norm_w, v_w_out, v_norm_post_w):
    xi, yi, ci = lax.axis_index("x"), lax.axis_index("y"), lax.axis_index("c")
    chip = 2 * xi + yi
    x2, tgt = x[0], loss_target[0]

    w_in_b, w_out_b = w_in[0].astype(BF16), w_out[0].astype(BF16)
    gin, gout, gcw = _gather_weights(w_in_b, w_out_b, conv_w[0])

    def whole(own, gathered, axis):
        return jnp.concatenate([jnp.where(chip == k, own, gathered[k]) for k in range(4)], axis=axis)

    w_all = jnp.concatenate([whole(w_in_b, gin, 1), jnp.zeros((D, DP - 4 * SHARD), BF16)], axis=1)
    w_out_all = whole(w_out_b, gout, 0)
    cw_all = whole(conv_w[0], gcw, 1)
    reduce = _GradReduce(chip, ci)
    grad_x, small = _local_step(x2, tgt, w_all, w_out_all, cw_all, norm_pre_w, conv_b, dt_bias, a_log, d_skip,
                                ssm_norm_w, norm_post_w, reduce)[:2]
    g_in, g_out = reduce.result()
    g_small = _slot_sum(_small_exchange(small), "small_sum")
    g_cw, g_npre, g_cb, g_nssm, g_npost, g_dtb, g_alog, g_dsk, loss = _unpack_small(g_small, 1536)
    g_cw = lax.dynamic_slice_in_dim(g_cw, chip * 384, 384, axis=2)

    d_in, nm_in, nv_in = _adamw(w_in[0], g_in, m_w_in[0], v_w_in[0], "adamw_in")
    d_out, nm_out, nv_out = _adamw(w_out[0], g_out, m_w_out[0], v_w_out[0], "adamw_out")
    packed = [_pack_small(*t) for t in (
        (conv_w, norm_pre_w, conv_b, ssm_norm_w, norm_post_w, dt_bias, a_log, d_skip),
        (g_cw, g_npre, g_cb, g_nssm, g_npost, g_dtb, g_alog, g_dsk),
        (m_conv_w, m_norm_pre_w, m_conv_b, m_ssm_norm_w, m_norm_post_w, m_dt_bias, m_a_log, m_d_skip),
        (v_conv_w, v_norm_pre_w, v_conv_b, v_ssm_norm_w, v_norm_post_w, v_dt_bias, v_a_log, v_d_skip))]
    small_out = [_unpack_small(p, 384)[:8] for p in _adamw(*packed, "adamw_small")]

    def ordered(cw_, npre, cb_, nssm, npost, dtb_, alog_, dsk_, big_in, big_out):
        return [npre, big_in[None], cw_, cb_, dtb_, alog_, dsk_, nssm, big_out[None], npost]

    grads = ordered(g_cw, g_npre, g_cb, g_nssm, g_npost, g_dtb, g_alog, g_dsk, g_in, g_out)
    deltas = ordered(*small_out[0], d_in, d_out)
    new_m = ordered(*small_out[1], nm_in, nm_out)
    new_v = ordered(*small_out[2], nv_in, nv_out)
    return (loss, grad_x[None], *grads, *deltas, *new_m, *new_v)


def _local_step(x2, tgt, w_all, w_out_all, cw_all, norm_pre_w, conv_b, dt_bias, a_log, d_skip, ssm_norm_w,
                norm_post_w, reduce=None):
    dtb, alog = _pad_lanes(dt_bias), _pad_lanes(a_log)
    d_b = jnp.repeat(d_skip, 64, axis=1)

    proj, u = _inproj_fwd(x2, norm_pre_w, w_all)
    mix, attn_pre, lse = _attn_fwd(proj, 1, _attn_fwd(proj, 4, _attn_fwd(proj, 16)), final=True)
    mix, y_save, states = _ssm_fwd(proj, mix, cw_all, conv_b, dtb, alog, d_b, ssm_norm_w)

    dy, dmix, dw_out, dnw_post, loss_part = _outproj_loss(mix, w_out_all, x2, tgt, norm_post_w)
    do, delta, dg = _attn_gate_bwd(dmix, attn_pre, proj)
    dz, dxbcdt, dcw, dcb, dvec, dnw_ssm = _ssm_bwd(proj, dmix, y_save, states, cw_all, conv_b, dtb, alog, d_b,
                                                   ssm_norm_w)
    acc = _attn_bwd(proj, do, lse, delta, 16, None, F32)
    acc = _attn_bwd(proj, do, lse, delta, 4, acc, F32)
    dq, dk, dv = _attn_bwd(proj, do, lse, delta, 1, acc, BF16)
    srcs = [dq, dk, dv, dg, dz]
    dws = [_dw(u, s, f"dw_in_{n}") for s, n in zip(srcs + [dxbcdt], ("q", "k", "v", "g", "z", "xbcdt"))]
    dw_all = jnp.concatenate(dws, axis=1)
    res = _inproj_bwd_dx(srcs, dxbcdt, w_all, x2, dy, norm_pre_w, reduce.start(dw_all, dw_out) if reduce else None)
    if reduce:
        res, got = res
        reduce.done(got)
    grad_x, dnw_pre = res
    small = _pack_small(dcw, dnw_pre, dcb, dnw_ssm, dnw_post, dvec[0:1, :NH], dvec[1:2, :NH], dvec[2:3, :NH],
                        loss_part[:, :1])
    return grad_x, small, dw_all, dw_out
```

```python
import functools

import jax
import jax.numpy as jnp
from jax import lax
from jax.experimental import pallas as pl
from jax.experimental.pallas import tpu as pltpu

F32 = jnp.float32
BF16 = jnp.bfloat16
MESH = pl.DeviceIdType.MESH
SDS = jax.ShapeDtypeStruct
ANY = pl.BlockSpec(memory_space=pl.ANY)

S = 4096
D = 1024
DP = 7168
SHARD = 1668
OFF_G, OFF_Z = 3072, 4096
NH = 16
CH = 128
NC = S // CH
EPS = 1e-6
NEG = -1e30
LANE = 128
VMEM_LIMIT = 48 * 1024 * 1024

ADAM_LR, ADAM_B1, ADAM_B2, ADAM_EPS, ADAM_WD, ADAM_STEP = 0.001, 0.9, 0.999, 1e-08, 0.01, 10


def _cp(sem, **kw):
    return pltpu.CompilerParams(dimension_semantics=sem, vmem_limit_bytes=VMEM_LIMIT, **kw)


def _dot(a, b):
    return jnp.dot(a, b, preferred_element_type=F32)


def _dot_nt(a, b):
    return lax.dot_general(a, b, (((1,), (1,)), ((), ())), preferred_element_type=F32)


def _dot_tn(a, b):
    return lax.dot_general(a, b, (((0,), (0,)), ((), ())), preferred_element_type=F32)


def _pieces(x, n):
    out = []
    for _ in range(n):
        p = x.astype(BF16)
        out.append(p)
        x = x - p.astype(F32)
    return out


def _pick(x, sel, n=2):
    parts = [_dot(p, sel) for p in _pieces(x, n)]
    return functools.reduce(jnp.add, parts)


def _pick_left(sel, x, n=3):
    parts = [_dot(sel, p) for p in _pieces(x, n)]
    return functools.reduce(jnp.add, parts)


def _sigmoid(v):
    return 0.5 * jnp.tanh(0.5 * v) + 0.5


def _iota(shape, dim):
    return lax.broadcasted_iota(jnp.int32, shape, dim)


def _inproj_fwd(x, nw, w_all):
    tm, tn = 1024, 1024

    def body(x_ref, nw_ref, w_ref, proj_ref, u_ref):
        @pl.when(pl.program_id(1) == 0)
        def _():
            xf = x_ref[...]
            r = lax.rsqrt(jnp.mean(xf * xf, axis=-1, keepdims=True) + EPS)
            u_ref[...] = (xf * r * nw_ref[...]).astype(BF16)

        proj_ref[...] = _dot(u_ref[...], w_ref[...])

    return pl.pallas_call(
        body, name="inproj_fwd", grid=(S // tm, DP // tn),
        in_specs=[pl.BlockSpec((tm, D), lambda i, j: (i, 0)), pl.BlockSpec((1, D), lambda i, j: (0, 0)),
                  pl.BlockSpec((D, tn), lambda i, j: (0, j))],
        out_specs=[pl.BlockSpec((tm, tn), lambda i, j: (i, j)), pl.BlockSpec((tm, D), lambda i, j: (i, 0))],
        out_shape=[SDS((S, DP), F32), SDS((S, D), BF16)],
        compiler_params=_cp(("parallel", "arbitrary")),
    )(x, nw, w_all)


ATTN_QB = {1: 16, 4: 4, 16: 1}


def _unit_rows(r, u, d):
    return pl.ds(r + d * CH * u, CH, stride=d) if d > 1 else pl.ds(CH * u, CH)


def _for_units(d, qb, fn):
    for r in range(d):
        for u in range(qb):
            fn(r, u)


def _attn_mask(has_prev):
    qi, kj = _iota((2 * CH, 2 * CH), 0) & (CH - 1), _iota((2 * CH, 2 * CH), 1)
    cur_ok = (kj >= CH) & (kj - CH <= qi)
    prev_ok = (kj < CH) & (kj >= qi)
    return cur_ok | (prev_ok & has_prev)


def _stack_heads(v, lane_a):
    return jnp.concatenate([jnp.where(lane_a, v, 0.0), jnp.where(lane_a, 0.0, v)], axis=0).astype(BF16)


def _attn_specs(d, qb):
    rows, prows = CH * d * qb, CH * d
    nb = S // rows
    steps = (NH // 2) * nb

    def at(t):
        t = jnp.minimum(t, steps - 1)
        return t % nb, t // nb

    def cur(off):
        return pl.BlockSpec((rows, LANE), lambda t: (at(t)[0], off + at(t)[1]))

    def prev(off):
        return pl.BlockSpec((prows, LANE), lambda t: (jnp.maximum(at(t)[0] * qb - 1, 0), off + at(t)[1]))

    lag = pl.BlockSpec((rows, LANE), lambda t: at(jnp.maximum(t - 1, 0)))
    return nb, steps, cur, prev, lag


def _gather16(src_ref, dense_ref, tmp_ref):
    for a in range(4):
        tmp_ref[...] = src_ref[pl.ds(a, 4 * CH, stride=4), :]
        for b in range(4):
            dense_ref[a + 4 * b] = tmp_ref[pl.ds(b, CH, stride=4), :]


def _scatter16(dense_ref, dst_ref, tmp_ref):
    for a in range(4):
        for b in range(4):
            tmp_ref[pl.ds(b, CH, stride=4), :] = dense_ref[a + 4 * b]
        dst_ref[pl.ds(a, 4 * CH, stride=4), :] = tmp_ref[...]


def _unit_index(r, u, d):
    return (r,) if d == 16 else (_unit_rows(r, u, d), slice(None))


def _unit_kv(p_ref, c_ref, r, u, d):
    prev = p_ref[_unit_index(r, 0, d)] if u == 0 else c_ref[_unit_index(r, u - 1, d)]
    return jnp.concatenate([prev, c_ref[_unit_index(r, u, d)]], axis=0).astype(BF16)


def _dense_scratch(d, n):
    return [pltpu.VMEM((16, CH, LANE), F32)] * n + [pltpu.VMEM((4 * CH, LANE), F32)] if d == 16 else []


def _attn_fwd(proj, d, prior=None, final=False):
    qb = ATTN_QB[d]
    nb, steps, cur, prev, _ = _attn_specs(d, qb)
    n_prior = 2 if prior is not None else 0
    n_in, n_out = 5 + n_prior + final, 2 + final
    assert not (d == 16 and (n_prior or final))

    def body(*refs):
        ins, outs, scratch = refs[:n_in], refs[n_in:n_in + n_out], refs[n_in + n_out:]
        if d == 16:
            tmp_ref = scratch[-1]
            for src, dense in zip(ins, scratch):
                _gather16(src, dense, tmp_ref)
            block_outs, ins, outs = outs, scratch[:n_in], scratch[n_in:n_in + n_out]
        q_ref, kp_ref, kc_ref, vp_ref, vc_ref = ins[:5]
        prior_refs = ins[5:5 + n_prior]
        if final:
            g_ref, (mix_ref, o_ref, l_ref) = ins[-1], outs
        else:
            o_ref, l_ref = outs
        i = pl.program_id(0) % nb
        lane_a = _iota((CH, LANE), 1) < 64
        mask_first, mask_rest = _attn_mask(i > 0), _attn_mask(True)

        def unit(r, u):
            at = _unit_index(r, u, d)
            q2 = _stack_heads(q_ref[at] * 0.125, lane_a)
            k2, v2 = _unit_kv(kp_ref, kc_ref, r, u, d), _unit_kv(vp_ref, vc_ref, r, u, d)
            s = jnp.where(mask_first if u == 0 else mask_rest, _dot_nt(q2, k2), NEG)
            m = jnp.max(s, axis=1, keepdims=True)
            p = jnp.exp(s - m)
            l = jnp.sum(p, axis=1, keepdims=True)
            o2 = _dot(p.astype(BF16), v2) / l
            lse2 = m + jnp.log(l)
            o = jnp.where(lane_a, o2[:CH], o2[CH:])
            lse = jnp.where(lane_a, lse2[:CH], lse2[CH:])
            if n_prior:
                o_a, l_a = prior_refs[0][at], prior_refs[1][at]
                top = jnp.maximum(l_a, lse)
                e_a, e_b = jnp.exp(l_a - top), jnp.exp(lse - top)
                tot = e_a + e_b
                o = (e_a * o_a + e_b * o) / tot
                lse = top + jnp.log(tot)
            o_ref[at] = o
            l_ref[at] = lse
            if final:
                g = g_ref[at]
                mix_ref[at] = (o * (g * _sigmoid(g))).astype(BF16)

        _for_units(d, qb, unit)
        if d == 16:
            for dense, dst in zip(outs, block_outs):
                _scatter16(dense, dst, tmp_ref)

    in_specs = [cur(0), prev(8), cur(8), prev(16), cur(16)] + [cur(0)] * n_prior
    args = [proj] * 5 + (list(prior) if n_prior else [])
    out_specs, out_shape = [cur(0), cur(0)], [SDS((S, D), F32), SDS((S, D), F32)]
    if final:
        assert d == 1
        in_specs.append(cur(OFF_G // LANE))
        args.append(proj)
        out_specs, out_shape = [cur(0)] + out_specs, [SDS((S, 2 * D), BF16)] + out_shape
    return pl.pallas_call(
        body, name=f"attn_fwd_d{d}", grid=(steps,),
        in_specs=in_specs, out_specs=out_specs, out_shape=out_shape,
        scratch_shapes=_dense_scratch(d, n_in + n_out),
        compiler_params=_cp(("parallel",)),
    )(*args)


def _attn_gate_bwd(dmix, pre, proj):
    tm = 512

    def body(dm_ref, pre_ref, g_ref, do_ref, delta_ref, dg_ref):
        g, dm, pre_v = g_ref[...], dm_ref[...], pre_ref[...]
        sig = _sigmoid(g)
        do = dm * (g * sig)
        do_ref[...] = do
        dg_ref[...] = (dm * pre_v * (sig * (1.0 + g * (1.0 - sig)))).astype(BF16)
        prod = do * pre_v
        same_head = (_iota((LANE, LANE), 0) // 64 == _iota((LANE, LANE), 1) // 64).astype(BF16)
        for cb in range(D // LANE):
            delta_ref[:, cb * LANE:(cb + 1) * LANE] = _pick(prod[:, cb * LANE:(cb + 1) * LANE], same_head)

    t = pl.BlockSpec((tm, D), lambda i: (i, 0))
    return pl.pallas_call(
        body, name="attn_gate_bwd", grid=(S // tm,),
        in_specs=[t, t, pl.BlockSpec((tm, D), lambda i: (i, OFF_G // D))],
        out_specs=[t, t, t],
        out_shape=[SDS((S, D), F32), SDS((S, D), F32), SDS((S, D), BF16)],
        compiler_params=_cp(("parallel",)),
    )(dmix, pre, proj)


def _attn_bwd(proj, do, lse, delta, d, acc, out_dtype):
    qb = ATTN_QB[d]
    nb, steps, cur, prev, lag = _attn_specs(d, qb)
    has_acc = acc is not None
    n_in = 11 if has_acc else 8
    assert not (d == 16 and (has_acc or out_dtype != F32))
    rows = CH * d * qb
    carry = (2, 16, CH, LANE) if d == 16 else (2, rows, LANE)

    def body(*refs):
        ins, (dq_ref, dk_ref, dv_ref), scratch = refs[:n_in], refs[n_in:n_in + 3], refs[n_in + 3:]
        ck_ref, cv_ref = scratch[:2]
        dq_f32 = dq_ref if out_dtype == F32 else scratch[2]
        t = pl.program_id(0)
        i = t % nb
        if d == 16:
            dense, dq_f32, tmp_ref = scratch[2:2 + n_in], scratch[2 + n_in], scratch[-1]

            @pl.when(t < steps)
            def _():
                for src, dst in zip(ins, dense):
                    _gather16(src, dst, tmp_ref)

            ins = dense
        q_ref, kp_ref, kc_ref, vp_ref, vc_ref, do_ref, lse_ref, dl_ref = ins[:8]
        if has_acc:
            aq_ref, ak_ref, av_ref = ins[8:11]
        slot = t & 1
        now_k, now_v, old_k, old_v = ck_ref.at[slot], cv_ref.at[slot], ck_ref.at[1 - slot], cv_ref.at[1 - slot]
        lane_a = _iota((CH, LANE), 1) < 64
        mask_first, mask_rest = _attn_mask(i > 0), _attn_mask(True)

        @pl.when(t == 0)
        def _():
            ck_ref[1] = jnp.zeros(carry[1:], F32)
            cv_ref[1] = jnp.zeros(carry[1:], F32)

        def unit(r, u):
            at = _unit_index(r, u, d)
            q2 = _stack_heads(q_ref[at] * 0.125, lane_a)
            do2 = _stack_heads(do_ref[at], lane_a)
            k2, v2 = _unit_kv(kp_ref, kc_ref, r, u, d), _unit_kv(vp_ref, vc_ref, r, u, d)
            lsev, dlv = lse_ref[at], dl_ref[at]
            lse2 = jnp.concatenate([lsev[:, 0:1], lsev[:, 64:65]], axis=0)
            dl2 = jnp.concatenate([dlv[:, 0:1], dlv[:, 64:65]], axis=0)
            p = jnp.exp(jnp.where(mask_first if u == 0 else mask_rest, _dot_nt(q2, k2), NEG) - lse2)
            ds = (p * (_dot_nt(do2, v2) - dl2)).astype(BF16)
            dq2 = _dot(ds, k2)
            dk2 = _dot_tn(ds, q2)
            dv2 = _dot_tn(p.astype(BF16), do2)
            dq = jnp.where(lane_a, dq2[:CH], dq2[CH:]) * 0.125
            if has_acc:
                dq = dq + aq_ref[at]
            dq_f32[at] = dq
            if u == 0:
                before = _unit_index(r, qb - 1, d)
                old_k[before] += dk2[:CH]
                old_v[before] += dv2[:CH]
            else:
                before = _unit_index(r, u - 1, d)
                now_k[before] += dk2[:CH]
                now_v[before] += dv2[:CH]
            now_k[at] = dk2[CH:]
            now_v[at] = dv2[CH:]

        @pl.when(t < steps)
        def _():
            _for_units(d, qb, unit)
            if d == 16:
                _scatter16(dq_f32, dq_ref, tmp_ref)
            elif out_dtype != F32:
                dq_ref[...] = dq_f32[...].astype(out_dtype)

        if d == 16:
            _scatter16(old_k, dk_ref, tmp_ref)
            _scatter16(old_v, dv_ref, tmp_ref)
        else:
            dk, dv = old_k[...], old_v[...]
            if has_acc:
                dk, dv = dk + ak_ref[...], dv + av_ref[...]
            dk_ref[...] = dk.astype(out_dtype)
            dv_ref[...] = dv.astype(out_dtype)

    in_specs = [cur(0), prev(8), cur(8), prev(16), cur(16), cur(0), cur(0), cur(0)]
    args = [proj, proj, proj, proj, proj, do, lse, delta]
    if has_acc:
        in_specs += [cur(0), lag, lag]
        args += list(acc)
    scratch = [pltpu.VMEM(carry, F32), pltpu.VMEM(carry, F32)]
    if d == 16:
        scratch += _dense_scratch(d, n_in + 1)
    elif out_dtype != F32:
        scratch.append(pltpu.VMEM((rows, LANE), F32))
    return pl.pallas_call(
        body, name=f"attn_bwd_d{d}", grid=(steps + 1,),
        in_specs=in_specs, out_specs=[cur(0), lag, lag], out_shape=[SDS((S, D), out_dtype)] * 3,
        scratch_shapes=scratch, compiler_params=_cp(("arbitrary",)),
    )(*args)


def _conv_taps(cur, prev8, first):
    row8 = _iota(prev8.shape, 0)
    prev8 = jnp.where(first, 0.0, prev8)
    taps = []
    for s in (3, 2, 1):
        rolled = pltpu.roll(cur, s, 0)
        head = jnp.where(row8 < s, pltpu.roll(prev8, s, 0), rolled[:8])
        taps.append(jnp.concatenate([head, rolled[8:]], axis=0))
    return taps + [cur]


def _conv(taps, w, b):
    acc = b + w[0:1, :] * taps[0]
    for k in (1, 2, 3):
        acc = acc + w[k:k + 1, :] * taps[k]
    return acc


def _expand():
    return (_iota((LANE, D), 1) // 64 == _iota((LANE, D), 0)).astype(BF16)


def _reduce():
    return (_iota((D, LANE), 0) // 64 == _iota((D, LANE), 1)).astype(BF16)


def _ssd_common(xs_raw, xs_prev, bc_raw, bc_prev, dt_raw, first, cw, cb, dtb, alog):
    head_lane = _iota((CH, LANE), 1) < NH
    xs_taps = _conv_taps(xs_raw, xs_prev, first)
    bc_taps = _conv_taps(bc_raw, bc_prev, first)
    xs_c = _conv(xs_taps, cw[:, :D], cb[:, :D])
    bc_c = _conv(bc_taps, cw[:, D:], cb[:, D:])
    xs = xs_c * _sigmoid(xs_c)
    bc = bc_c * _sigmoid(bc_c)
    pre = dt_raw + dtb
    dt = jnp.where(head_lane, jnp.maximum(pre, 0.0) + jnp.log(1.0 + jnp.exp(-jnp.abs(pre))), 0.0)
    a_row = jnp.where(head_lane[0:1], -jnp.exp(alog), 0.0)
    tri = (_iota((CH, CH), 1) <= _iota((CH, CH), 0)).astype(BF16)
    cs = _pick_left(tri, dt * a_row)
    cs_last = cs[CH - 1:CH, :]
    wide = _pick(jnp.concatenate([dt, jnp.exp(cs), jnp.exp(cs_last - cs)], axis=0), _expand())
    dt_b, e_b, f_b = wide[:CH], wide[CH:2 * CH], wide[2 * CH:]
    return dict(xs_taps=xs_taps, bc_taps=bc_taps, xs_c=xs_c, bc_c=bc_c, xs=xs, bc=bc, pre=pre, dt=dt,
                a_row=a_row, cs=cs, cs_t=cs.T, dt_b=dt_b, e_b=e_b, f_b=f_b, t_b=e_b[CH - 1:CH, :])


def _groups(bc):
    bcb = bc.astype(BF16)
    return [bcb[:, 0:128], bcb[:, 128:256]], [bcb[:, 256:384], bcb[:, 384:512]]


def _decay(q, h, tril):
    seg = q["cs"][:, h:h + 1] - q["cs_t"][h:h + 1, :]
    return jnp.exp(jnp.where(tril, seg, NEG))


def _ssm_fwd(proj, mix, cw, cb, dtb, alog, d_b, nw):
    def body(xs_ref, xsp_ref, bc_ref, bcp_ref, dt_ref, z_ref, cw_ref, cb_ref, dtb_ref, alog_ref, db_ref, nw_ref,
             mix_in_ref, mix_ref, y_ref, st_ref, h_ref):
        del mix_in_ref
        i = pl.program_id(0)

        @pl.when(i == 0)
        def _():
            h_ref[...] = jnp.zeros_like(h_ref)

        q = _ssd_common(xs_ref[...], xsp_ref[...], bc_ref[...], bcp_ref[...], dt_ref[...], i == 0,
                        cw_ref[...], cb_ref[...], dtb_ref[...], alog_ref[...])
        bg, cg = _groups(q["bc"])
        xs = q["xs"]
        xdt = xs * q["dt_b"]
        xdt_b = xdt.astype(BF16)
        h_in = h_ref[...]
        st_ref[...] = h_in
        hb = h_in.astype(BF16)
        tril = _iota((CH, CH), 1) <= _iota((CH, CH), 0)
        lane_a = _iota((CH, LANE), 1) < 64
        cbm = [_dot_nt(cg[g], bg[g]) for g in range(2)]
        pairs = []
        for hp in range(NH // 2):
            xp = xdt_b[:, hp * LANE:(hp + 1) * LANE]
            ya = _dot((cbm[hp // 4] * _decay(q, 2 * hp, tril)).astype(BF16), xp)
            yb = _dot((cbm[hp // 4] * _decay(q, 2 * hp + 1, tril)).astype(BF16), xp)
            pairs.append(jnp.where(lane_a, ya, yb))
        y_diag = jnp.concatenate(pairs, axis=1)
        y_off = jnp.concatenate([_dot(cg[g], hb[:, g * 512:(g + 1) * 512]) for g in range(2)], axis=1) * q["e_b"]
        y = y_diag + y_off + db_ref[...] * xs
        y_ref[...] = y
        xf = (xdt * q["f_b"]).astype(BF16)
        h_ref[...] = q["t_b"] * h_in + jnp.concatenate(
            [_dot_tn(bg[g], xf[:, g * 512:(g + 1) * 512]) for g in range(2)], axis=1)
        z = z_ref[...]
        yz = y * (z * _sigmoid(z))
        outs = []
        for g in range(2):
            v = yz[:, g * 512:(g + 1) * 512]
            outs.append(v * lax.rsqrt(jnp.mean(v * v, axis=-1, keepdims=True) + EPS))
        mix_ref[...] = (jnp.concatenate(outs, axis=1) * nw_ref[...]).astype(BF16)

    def col(width, blk, prev=False):
        if prev:
            return pl.BlockSpec((8, width), lambda i: (jnp.maximum(i * (CH // 8) - 1, 0), blk))
        return pl.BlockSpec((CH, width), lambda i: (i, blk))

    def full(a):
        return pl.BlockSpec(a.shape, lambda i: (0,) * a.ndim)

    return pl.pallas_call(
        body, name="ssm_fwd", grid=(NC,),
        in_specs=[col(D, 5), col(D, 5, True), col(512, 12), col(512, 12, True), col(LANE, 52), col(D, 4),
                  full(cw), full(cb), full(dtb), full(alog), full(d_b), full(nw), ANY],
        out_specs=[col(D, 1), col(D, 0), pl.BlockSpec((None, CH, D), lambda i: (i, 0, 0))],
        out_shape=[SDS((S, 2 * D), BF16), SDS((S, D), F32), SDS((NC, CH, D), F32)],
        scratch_shapes=[pltpu.VMEM((CH, D), F32)],
        input_output_aliases={12: 0},
        compiler_params=_cp(("arbitrary",)),
    )(proj, proj, proj, proj, proj, proj, cw, cb, dtb, alog, d_b, nw, mix)


def _ssm_bwd(proj, dmix, y_save, states, cw, cb, dtb, alog, d_b, nw):
    def body(xs_ref, xsp_ref, bc_ref, bcp_ref, dt_ref, z_ref, dn_ref, y_ref, st_ref,
             cw_ref, cb_ref, dtb_ref, alog_ref, db_ref, nw_ref,
             dz_ref, dx_ref, dcw_ref, dcb_ref, dsm_ref, dnw_ref, dh_ref, nxs_ref, nbc_ref):
        i = pl.program_id(0)
        ci = NC - 1 - i

        @pl.when(i == 0)
        def _():
            for ref in (dcw_ref, dcb_ref, dsm_ref, dnw_ref, dh_ref, nxs_ref, nbc_ref):
                ref[...] = jnp.zeros_like(ref)

        cw, cb = cw_ref[...], cb_ref[...]
        q = _ssd_common(xs_ref[...], xsp_ref[...], bc_ref[...], bcp_ref[...], dt_ref[...], ci == 0,
                        cw, cb, dtb_ref[...], alog_ref[...])
        bg, cg = _groups(q["bc"])
        xs, dt_b, e_b, f_b, t_b = q["xs"], q["dt_b"], q["e_b"], q["f_b"], q["t_b"]
        xdt = xs * dt_b
        xdt_b = xdt.astype(BF16)
        h_in = st_ref[...]
        hb = h_in.astype(BF16)
        dh_new = dh_ref[...]
        dhb = dh_new.astype(BF16)
        red = _reduce()

        z, y, dn, nw_v = z_ref[...], y_ref[...], dn_ref[...], nw_ref[...]
        sig = _sigmoid(z)
        sz = z * sig
        yz = y * sz
        gdn = dn * nw_v
        dyz, dnw = [], []
        for g in range(2):
            v, gv = yz[:, g * 512:(g + 1) * 512], gdn[:, g * 512:(g + 1) * 512]
            r = lax.rsqrt(jnp.mean(v * v, axis=-1, keepdims=True) + EPS)
            dnw.append(dn[:, g * 512:(g + 1) * 512] * v * r)
            dyz.append(r * (gv - v * (r * r) * jnp.mean(gv * v, axis=-1, keepdims=True)))
        dyz = jnp.concatenate(dyz, axis=1)
        dnw_ref[...] += jnp.sum(jnp.concatenate(dnw, axis=1), axis=0, keepdims=True)
        dy = dyz * sz
        dz_ref[...] = (dyz * y * (sig * (1.0 + z * (1.0 - sig)))).astype(BF16)
        dy_b = dy.astype(BF16)

        tril = _iota((CH, CH), 1) <= _iota((CH, CH), 0)
        lane_a = _iota((CH, LANE), 1) < 64
        cbm = [_dot_nt(cg[g], bg[g]) for g in range(2)]
        dcbm = [jnp.zeros((CH, CH), F32), jnp.zeros((CH, CH), F32)]
        seg_rows = jnp.zeros((CH, LANE), F32)
        seg_cols = jnp.zeros((LANE, CH), F32)
        row_id, col_id = _iota((CH, LANE), 0), _iota((CH, LANE), 1)
        dx_pairs = []
        for hp in range(NH // 2):
            g = hp // 4
            xp = xdt_b[:, hp * LANE:(hp + 1) * LANE]
            dyp_f = dy[:, hp * LANE:(hp + 1) * LANE]
            dyp = dy_b[:, hp * LANE:(hp + 1) * LANE]
            halves = []
            for k in range(2):
                h = 2 * hp + k
                lane = lane_a if k == 0 else jnp.logical_not(lane_a)
                dec = _decay(q, h, tril)
                gm = cbm[g] * dec
                dgm = _dot_nt(jnp.where(lane, dyp_f, 0.0).astype(BF16), xp)
                dcbm[g] = dcbm[g] + dgm * dec
                prod = dgm * gm
                seg_rows = jnp.where(col_id == h, jnp.sum(prod, axis=1, keepdims=True), seg_rows)
                seg_cols = jnp.where(row_id == h, jnp.sum(prod, axis=0, keepdims=True), seg_cols)
                halves.append(_dot_tn(gm.astype(BF16), dyp))
            dx_pairs.append(jnp.where(lane_a, halves[0], halves[1]))
        dxdt_diag = jnp.concatenate(dx_pairs, axis=1)

        qv = jnp.concatenate([_dot(bg[g], dhb[:, g * 512:(g + 1) * 512]) for g in range(2)], axis=1)
        y_off = jnp.concatenate([_dot(cg[g], hb[:, g * 512:(g + 1) * 512]) for g in range(2)], axis=1) * e_b
        xfq = xdt * f_b * qv
        dxdt = dxdt_diag + f_b * qv
        tdt = jnp.sum(dh_new * h_in, axis=0, keepdims=True) * t_b
        per_head = _pick(jnp.concatenate([xfq, dy * y_off, dxdt * xs, dy * xs, jnp.broadcast_to(tdt, (8, D))],
                                         axis=0), red)
        fdf, dyoff_h, dxdtxs_h, dyxs_h = [per_head[k * CH:(k + 1) * CH] for k in range(4)]
        dcs = seg_rows - seg_cols.T + dyoff_h - fdf
        last = per_head[4 * CH:4 * CH + 1] + jnp.sum(fdf, axis=0, keepdims=True)
        dcs = dcs + jnp.where(_iota((CH, LANE), 0) == CH - 1, last, 0.0)
        tri_t = (_iota((CH, CH), 1) >= _iota((CH, CH), 0)).astype(BF16)
        da = _pick_left(tri_t, dcs)
        ddt = da * q["a_row"] + dxdtxs_h
        dxs = dxdt * dt_b + db_ref[...] * dy
        ddt_raw = ddt * _sigmoid(q["pre"])
        dsm_ref[0:1, :] += jnp.sum(ddt_raw, axis=0, keepdims=True)
        dsm_ref[1:2, :] += jnp.sum(da * q["dt"], axis=0, keepdims=True) * q["a_row"]
        dsm_ref[2:3, :] += jnp.sum(dyxs_h, axis=0, keepdims=True)
        edy = (e_b * dy).astype(BF16)
        xf = (xdt * f_b).astype(BF16)
        dbs, dcs_g, dhs = [], [], []
        for g in range(2):
            sl = slice(g * 512, (g + 1) * 512)
            dcb_b = dcbm[g].astype(BF16)
            dcs_g.append(_dot(dcb_b, bg[g]) + _dot_nt(edy[:, sl], hb[:, sl]))
            dbs.append(_dot_tn(dcb_b, cg[g]) + _dot_nt(xf[:, sl], dhb[:, sl]))
            dhs.append(_dot_tn(cg[g], edy[:, sl]))
        dh_ref[...] = t_b * dh_new + jnp.concatenate(dhs, axis=1)
        dbc = jnp.concatenate(dbs + dcs_g, axis=1)

        def conv_bwd(dact, pre, taps, w, nxt_ref, lo):
            s = _sigmoid(pre)
            dconv = dact * (s * (1.0 + pre * (1.0 - s)))
            nxt8 = nxt_ref[...]
            row8 = _iota(nxt8.shape, 0)
            hi = lo + dconv.shape[1]
            dcb_ref[:, lo:hi] += jnp.sum(dconv, axis=0, keepdims=True)
            dx = w[3:4, :] * dconv
            for k in range(4):
                dcw_ref[k:k + 1, lo:hi] += jnp.sum(dconv * taps[k], axis=0, keepdims=True)
            for s_ in (1, 2, 3):
                rolled = pltpu.roll(dconv, CH - s_, 0)
                tail = jnp.where(row8 >= 8 - s_, pltpu.roll(nxt8, 8 - s_, 0), rolled[CH - 8:])
                dx = dx + w[3 - s_:4 - s_, :] * jnp.concatenate([rolled[:CH - 8], tail], axis=0)
            nxt_ref[...] = dconv[:8]
            return dx

        dx_ref[:, 0:D] = conv_bwd(dxs, q["xs_c"], q["xs_taps"], cw[:, :D], nxs_ref, 0).astype(BF16)
        dx_ref[:, D:D + 512] = conv_bwd(dbc, q["bc_c"], q["bc_taps"], cw[:, D:], nbc_ref, D).astype(BF16)
        dx_ref[:, D + 512:D + 640] = ddt_raw.astype(BF16)
        dx_ref[:, D + 640:] = jnp.zeros((CH, D - 640), BF16)

    def col(width, blk, prev=False):
        if prev:
            return pl.BlockSpec((8, width), lambda i: (jnp.maximum((NC - 1 - i) * (CH // 8) - 1, 0), blk))
        return pl.BlockSpec((CH, width), lambda i: (NC - 1 - i, blk))

    def full(a):
        return pl.BlockSpec(a.shape, lambda i: (0,) * len(a.shape))

    acc_shapes = [SDS((4, 1536), F32), SDS((1, 1536), F32), SDS((8, LANE), F32), SDS((1, D), F32)]
    return pl.pallas_call(
        body, name="ssm_bwd", grid=(NC,),
        in_specs=[col(D, 5), col(D, 5, True), col(512, 12), col(512, 12, True), col(LANE, 52), col(D, 4),
                  col(D, 1), col(D, 0), pl.BlockSpec((None, CH, D), lambda i: (NC - 1 - i, 0, 0)),
                  full(cw), full(cb), full(dtb), full(alog), full(d_b), full(nw)],
        out_specs=[col(D, 0), col(2 * D, 0)] + [full(a) for a in acc_shapes],
        out_shape=[SDS((S, D), BF16), SDS((S, 2 * D), BF16)] + acc_shapes,
        scratch_shapes=[pltpu.VMEM((CH, D), F32), pltpu.VMEM((8, D), F32), pltpu.VMEM((8, 512), F32)],
        compiler_params=_cp(("arbitrary",)),
    )(proj, proj, proj, proj, proj, proj, dmix, y_save, states, cw, cb, dtb, alog, d_b, nw)


def _outproj_loss(mix, w_out, x, tgt, nw):
    tm = 256

    def body(mix_ref, w_ref, x_ref, t_ref, nw_ref, dy_ref, dmix_ref, dw_ref, dnw_ref, loss_ref):
        @pl.when(pl.program_id(0) == 0)
        def _():
            dw_ref[...] = jnp.zeros_like(dw_ref)
            dnw_ref[...] = jnp.zeros_like(dnw_ref)
            loss_ref[...] = jnp.zeros_like(loss_ref)

        mixv, w = mix_ref[...], w_ref[...]
        out = _dot(mixv, w)
        r = lax.rsqrt(jnp.mean(out * out, axis=-1, keepdims=True) + EPS)
        nh = out * r
        nw_v = nw_ref[...]
        err = x_ref[...] + nh * nw_v - t_ref[...]
        loss_ref[...] += 0.5 * jnp.sum(jnp.mean(err * err, axis=-1, keepdims=True), axis=0, keepdims=True)
        dy = err * (1.0 / D)
        dy_ref[...] = dy
        dnw_ref[...] += jnp.sum(dy * nh, axis=0, keepdims=True)
        gdn = dy * nw_v
        dout = (r * (gdn - nh * jnp.mean(gdn * nh, axis=-1, keepdims=True))).astype(BF16)
        dmix_ref[...] = _dot_nt(dout, w)
        dw_ref[...] += _dot_tn(mixv, dout)

    row = lambda w: pl.BlockSpec((tm, w), lambda i: (i, 0))
    full = lambda s: pl.BlockSpec(s, lambda i: (0, 0))
    return pl.pallas_call(
        body, name="outproj_loss", grid=(S // tm,),
        in_specs=[row(2 * D), full((2 * D, D)), row(D), row(D), full((1, D))],
        out_specs=[row(D), row(2 * D), full((2 * D, D)), full((1, D)), full((1, LANE))],
        out_shape=[SDS((S, D), F32), SDS((S, 2 * D), F32), SDS((2 * D, D), F32), SDS((1, D), F32),
                   SDS((1, LANE), F32)],
        compiler_params=_cp(("arbitrary",)),
    )(mix, w_out, x, tgt, nw)


def _inproj_bwd_dx(srcs, dxbcdt, w_all, x, dy, nw, hosted=None):
    tm = 512
    nk = DP // D
    n_host = len(hosted.arrays) if hosted else 0

    def body(*refs):
        src_refs = refs[:nk]
        w_ref, x_ref, dy_ref, nw_ref = refs[nk:nk + 4]
        host_in, refs = refs[nk + 4:nk + 4 + n_host], refs[nk + 4 + n_host:]
        gx_ref, dnw_ref = refs[:2]
        host_out, acc_ref, host_sems = refs[2:2 + n_host], refs[2 + n_host], refs[3 + n_host:]
        i, kk = pl.program_id(0), pl.program_id(1)
        if hosted:
            pl.when((i == 0) & (kk == 0))(lambda: hosted.start(host_in, host_out, host_sems))

        @pl.when((i == 0) & (kk == 0))
        def _():
            dnw_ref[...] = jnp.zeros_like(dnw_ref)

        @pl.when(kk == 0)
        def _():
            acc_ref[...] = jnp.zeros_like(acc_ref)

        for s, ref in enumerate(src_refs):
            @pl.when(kk == s)
            def _(ref=ref):
                acc_ref[...] += _dot_nt(ref[...], w_ref[...])

        @pl.when(kk == nk - 1)
        def _():
            xf, du, nw_v = x_ref[...], acc_ref[...], nw_ref[...]
            r = lax.rsqrt(jnp.mean(xf * xf, axis=-1, keepdims=True) + EPS)
            xh = xf * r
            dnw_ref[...] += jnp.sum(du * xh, axis=0, keepdims=True)
            gdu = du * nw_v
            gx_ref[...] = r * (gdu - xh * jnp.mean(gdu * xh, axis=-1, keepdims=True)) + dy_ref[...]

        if hosted:
            pl.when((i == S // tm - 1) & (kk == nk - 1))(lambda: hosted.finish(host_in, host_out, host_sems))

    row = pl.BlockSpec((tm, D), lambda i, k: (i, 0))
    row1 = pl.BlockSpec((tm, D), lambda i, k: (i, 1))
    one = pl.BlockSpec((1, D), lambda i, k: (0, 0))
    args = [*srcs, dxbcdt, dxbcdt, w_all, x, dy, nw]
    in_specs = [row] * len(srcs) + [row, row1, pl.BlockSpec((D, D), lambda i, k: (0, k)), row, row, one]
    out_specs, out_shape = [row, one], [SDS((S, D), F32), SDS((1, D), F32)]
    scratch = [pltpu.VMEM((tm, D), F32)]
    if hosted:
        args += hosted.arrays
        in_specs += [ANY] * n_host
        out_specs += [ANY] * n_host
        out_shape += hosted.out_shape
        scratch += hosted.scratch
    outs = pl.pallas_call(
        body, name="inproj_bwd_dx", grid=(S // tm, nk),
        in_specs=in_specs, out_specs=out_specs, out_shape=out_shape, scratch_shapes=scratch,
        compiler_params=_cp(("arbitrary", "arbitrary")),
    )(*args)
    return (outs[:2], outs[2:]) if hosted else outs


def _dw(u, dsec, name):
    ts = 512
    ncol = dsec.shape[1] // D

    def body(u_ref, d_ref, o_ref):
        @pl.when(pl.program_id(1) == 0)
        def _():
            o_ref[...] = jnp.zeros_like(o_ref)

        o_ref[...] += _dot_tn(u_ref[...], d_ref[...])

    return pl.pallas_call(
        body, name=name, grid=(ncol, S // ts),
        in_specs=[pl.BlockSpec((ts, D), lambda j, i: (i, 0)), pl.BlockSpec((ts, D), lambda j, i: (i, j))],
        out_specs=pl.BlockSpec((D, D), lambda j, i: (0, j)),
        out_shape=SDS((D, ncol * D), F32),
        compiler_params=_cp(("parallel", "arbitrary")),
    )(u, dsec)


def _place():
    x, y, c = lax.axis_index("x"), lax.axis_index("y"), lax.axis_index("c")
    return x, y, c, 2 * x + y


def _chip_of(x, y, k):
    px = 1 - x if k & 2 else x
    py = 1 - y if k & 1 else y
    return px, py, 2 * px + py


def _remote(src, dst, send_sem, recv_sem, dev):
    return pltpu.make_async_remote_copy(src_ref=src, dst_ref=dst, send_sem=send_sem, recv_sem=recv_sem,
                                        device_id=dev, device_id_type=MESH)


def _gather_weights(w_in_b, w_out_b, conv_w):
    def body(win_ref, wout_ref, cw_ref, gin_ref, gout_ref, gcw_ref, send, recv, csend, crecv):
        x, y, c, j = _place()
        me, sib = (x, y, c), (x, y, 1 - c)
        nbr = {"x": _chip_of(x, y, 2), "y": _chip_of(x, y, 1)}
        diag = _chip_of(x, y, 3)[2]
        conv = [_remote(cw_ref, gcw_ref.at[j], csend.at[k - 1], crecv.at[k - 1], (*_chip_of(x, y, k)[:2], c))
                for k in (1, 2, 3)]
        for cp in conv:
            cp.start()
        started, arrivals = [], []
        pairs = ((win_ref, gin_ref), (wout_ref, gout_ref))

        def rows(src, n_quarter=None, sibling=False):
            half = src.shape[0] // 2
            base = (1 - c if sibling else c) * half
            return pl.ds(base, half) if n_quarter is None else pl.ds(base + n_quarter * (half // 2), half // 2)

        def sem(a, n):
            return send.at[8 * a + n], recv.at[8 * a + n]

        def go(cp):
            cp.start()
            started.append(cp)

        for a, (src, dst) in enumerate(pairs):
            for n, axis in enumerate("xy"):
                px, py, _ = nbr[axis]
                go(_remote(src.at[rows(src)], dst.at[j, rows(src)], *sem(a, n), (px, py, c)))
        for n, axis in enumerate("xy"):
            ox, oy, _ = nbr["y" if axis == "x" else "x"]
            pj = nbr[axis][2]
            for a, (src, dst) in enumerate(pairs):
                _remote(src.at[rows(src)], dst.at[pj, rows(src)], *sem(a, n), me).wait_recv()
                go(_remote(dst.at[pj, rows(src, n)], dst.at[pj, rows(src, n)], *sem(a, 2 + n), (ox, oy, c)))
                go(_remote(dst.at[pj, rows(src)], dst.at[pj, rows(src)], *sem(a, 4 + n), sib))
                arrivals.append(_remote(src.at[rows(src)], dst.at[pj, rows(src, None, True)], *sem(a, 4 + n), me))
        for n in range(2):
            for a, (src, dst) in enumerate(pairs):
                part = rows(src, n)
                _remote(dst.at[diag, part], dst.at[diag, part], *sem(a, 2 + n), me).wait_recv()
                go(_remote(dst.at[diag, part], dst.at[diag, part], *sem(a, 6 + n), sib))
                sib_part = rows(src, n, True)
                arrivals.append(_remote(dst.at[diag, sib_part], dst.at[diag, sib_part], *sem(a, 6 + n), me))
        for cp in arrivals:
            cp.wait_recv()
        for k in (1, 2, 3):
            pj = _chip_of(x, y, k)[2]
            _remote(cw_ref, gcw_ref.at[pj], csend.at[k - 1], crecv.at[k - 1], me).wait_recv()
        for cp in started + conv:
            cp.wait_send()

    return pl.pallas_call(
        body, name="gather_weights",
        in_specs=[ANY, ANY, ANY], out_specs=[ANY, ANY, ANY],
        out_shape=[SDS((4,) + w_in_b.shape, BF16), SDS((4,) + w_out_b.shape, BF16), SDS((4,) + conv_w.shape, F32)],
        scratch_shapes=[pltpu.SemaphoreType.DMA((16,)), pltpu.SemaphoreType.DMA((16,)),
                        pltpu.SemaphoreType.DMA((3,)), pltpu.SemaphoreType.DMA((3,))],
        compiler_params=pltpu.CompilerParams(has_side_effects=True),
    )(w_in_b, w_out_b, conv_w)


def _pair_exchange(arrays, name):
    halves = [a.shape[1] // 2 for a in arrays]
    n = len(arrays)

    def body(*refs):
        x, y, c, _ = _place()
        send, recv = refs[2 * n:]
        cps = [_remote(refs[k].at[:, pl.ds((1 - c) * halves[k], halves[k])], refs[n + k], send.at[k], recv.at[k],
                       (x, y, 1 - c)) for k in range(n)]
        for cp in cps:
            cp.start()
        for cp in cps:
            cp.wait()

    return pl.pallas_call(
        body, name=name, in_specs=[ANY] * n, out_specs=[ANY] * n,
        out_shape=[SDS((a.shape[0], h, a.shape[2]), F32) for a, h in zip(arrays, halves)],
        scratch_shapes=[pltpu.SemaphoreType.DMA((n,)), pltpu.SemaphoreType.DMA((n,))],
        compiler_params=pltpu.CompilerParams(has_side_effects=True),
    )(*arrays)


def _pair_sum(cidx, g, r, name):
    n, half, width = r.shape
    tr = min(half, 256)
    nt = half // tr

    def body(c_ref, g_ref, r_ref, o_ref):
        del c_ref
        o_ref[...] = (g_ref[...] + r_ref[...]).astype(BF16)

    return pl.pallas_call(
        body, name=name,
        grid_spec=pltpu.PrefetchScalarGridSpec(
            num_scalar_prefetch=1, grid=(n, nt),
            in_specs=[pl.BlockSpec((None, tr, width), lambda s, t, c: (s, c[0] * nt + t, 0)),
                      pl.BlockSpec((None, tr, width), lambda s, t, c: (s, t, 0))],
            out_specs=pl.BlockSpec((None, tr, width), lambda s, t, c: (s, t, 0))),
        out_shape=SDS(r.shape, BF16),
        compiler_params=_cp(("parallel", "parallel")),
    )(cidx, g, r)


class _ChipExchange:
    def __init__(self, arrays):
        self.arrays = list(arrays)
        self.out_shape = [SDS(a.shape, BF16) for a in self.arrays]
        self.scratch = [pltpu.SemaphoreType.DMA((3 * len(self.arrays),)) for _ in range(2)]

    def _copies(self, ins, outs, sems):
        x, y, c, j = _place()
        send, recv = sems
        for a, (src, dst) in enumerate(zip(ins, outs)):
            for k in (1, 2, 3):
                px, py, pj = _chip_of(x, y, k)
                n = 3 * a + k - 1
                yield (_remote(src.at[pj], dst.at[j], send.at[n], recv.at[n], (px, py, c)),
                       _remote(src.at[pj], dst.at[pj], send.at[n], recv.at[n], (x, y, c)))

    def start(self, ins, outs, sems):
        for send, _ in self._copies(ins, outs, sems):
            send.start()

    def finish(self, ins, outs, sems):
        for send, arrival in self._copies(ins, outs, sems):
            arrival.wait_recv()
            send.wait_send()


def _small_exchange(small):
    def body(sm_ref, rs_ref, send, recv, lsem):
        x, y, c, j = _place()
        me = 2 * j + c
        local = pltpu.make_async_copy(sm_ref, rs_ref.at[me], lsem)
        local.start()
        cps = []
        for k in range(1, 8):
            px, py, _ = _chip_of(x, y, k >> 1)
            pc = 1 - c if k & 1 else c
            cps.append(_remote(sm_ref, rs_ref.at[me], send.at[k - 1], recv.at[k - 1], (px, py, pc)))
        for cp in cps:
            cp.start()
        for k in range(1, 8):
            _, _, pj = _chip_of(x, y, k >> 1)
            pc = 1 - c if k & 1 else c
            _remote(sm_ref, rs_ref.at[2 * pj + pc], send.at[k - 1], recv.at[k - 1], (x, y, c)).wait_recv()
        for cp in cps:
            cp.wait_send()
        local.wait()

    return pl.pallas_call(
        body, name="small_exchange", in_specs=[ANY], out_specs=ANY,
        out_shape=SDS((8,) + small.shape, F32),
        scratch_shapes=[pltpu.SemaphoreType.DMA((7,)), pltpu.SemaphoreType.DMA((7,)), pltpu.SemaphoreType.DMA],
        compiler_params=pltpu.CompilerParams(has_side_effects=True),
    )(small)


def _slot_sum(r, name):
    n, rows, width = r.shape
    tr = min(rows, 256)

    def body(r_ref, o_ref):
        acc = r_ref[0].astype(F32)
        for s in range(1, n):
            acc = acc + r_ref[s].astype(F32)
        o_ref[...] = acc

    return pl.pallas_call(
        body, name=name, grid=(rows // tr,),
        in_specs=[pl.BlockSpec((n, tr, width), lambda t: (0, t, 0))],
        out_specs=pl.BlockSpec((tr, width), lambda t: (t, 0)),
        out_shape=SDS((rows, width), F32),
        compiler_params=_cp(("parallel",)),
    )(r)


def _chip_sum(chip_idx, recv, own, name):
    n, rows, width = recv.shape
    tr = min(rows, 256)

    def body(j_ref, r_ref, own_ref, o_ref):
        acc = None
        for s in range(n):
            term = jnp.where(j_ref[0] == s, own_ref[...], r_ref[s]).astype(F32)
            acc = term if acc is None else acc + term
        o_ref[...] = acc

    return pl.pallas_call(
        body, name=name,
        grid_spec=pltpu.PrefetchScalarGridSpec(
            num_scalar_prefetch=1, grid=(rows // tr,),
            in_specs=[pl.BlockSpec((n, tr, width), lambda t, j: (0, t, 0)),
                      pl.BlockSpec((None, tr, width), lambda t, j: (j[0], t, 0))],
            out_specs=pl.BlockSpec((tr, width), lambda t, j: (t, 0))),
        out_shape=SDS((rows, width), F32),
        compiler_params=_cp(("parallel",)),
    )(chip_idx, recv, own)


def _half_exchange(hw, ho):
    def body(hw_ref, ho_ref, tw_ref, to_ref, send, recv):
        x, y, c, _ = _place()
        sib = (x, y, 1 - c)
        cps = [_remote(hw_ref, tw_ref, send.at[0], recv.at[0], sib),
               _remote(ho_ref, to_ref, send.at[1], recv.at[1], sib)]
        for cp in cps:
            cp.start()
        for cp in cps:
            cp.wait()

    return pl.pallas_call(
        body, name="half_exchange", in_specs=[ANY, ANY], out_specs=[ANY, ANY],
        out_shape=[SDS(hw.shape, F32), SDS(ho.shape, F32)],
        scratch_shapes=[pltpu.SemaphoreType.DMA((2,)), pltpu.SemaphoreType.DMA((2,))],
        compiler_params=pltpu.CompilerParams(has_side_effects=True),
    )(hw, ho)


def _by_core(c, mine, theirs):
    return jnp.where(c == 0, jnp.concatenate([mine, theirs], axis=0), jnp.concatenate([theirs, mine], axis=0))


def _adamw(w, g, m, v, name):
    rows, width = w.shape
    tr = min(rows, 256)

    def body(w_ref, g_ref, m_ref, v_ref, d_ref, nm_ref, nv_ref):
        gv = g_ref[...]
        nm = ADAM_B1 * m_ref[...] + (1.0 - ADAM_B1) * gv
        nv = ADAM_B2 * v_ref[...] + (1.0 - ADAM_B2) * (gv * gv)
        m_hat = nm / (1.0 - ADAM_B1 ** ADAM_STEP)
        v_hat = nv / (1.0 - ADAM_B2 ** ADAM_STEP)
        d_ref[...] = -ADAM_LR * (m_hat / (jnp.sqrt(v_hat) + ADAM_EPS) + ADAM_WD * w_ref[...])
        nm_ref[...] = nm
        nv_ref[...] = nv

    t = pl.BlockSpec((tr, width), lambda i: (i, 0))
    return pl.pallas_call(
        body, name=name, grid=(rows // tr,), in_specs=[t] * 4, out_specs=[t] * 3,
        out_shape=[SDS(w.shape, F32)] * 3, compiler_params=_cp(("parallel",)),
    )(w, g, m, v)


def _rows128(a, rows):
    flat = a.reshape(-1)
    return jnp.pad(flat, (0, rows * LANE - flat.shape[0])).reshape(rows, LANE)


def _pack_small(conv_w, norm_pre, conv_b, ssm_norm, norm_post, dtb, alog, dsk, extra=None):
    cw_rows = 48 if conv_w.shape[-1] == 1536 else 16
    extra = jnp.zeros((1, LANE), F32) if extra is None else _rows128(extra, 1)
    vec = jnp.concatenate([_rows128(dtb, 1), _rows128(alog, 1), _rows128(dsk, 1), extra, jnp.zeros((4, LANE), F32)],
                          axis=0)
    return jnp.concatenate([_rows128(conv_w, cw_rows), _rows128(norm_pre, 8), _rows128(conv_b, 16),
                            _rows128(ssm_norm, 8), _rows128(norm_post, 8), vec], axis=0)


def _unpack_small(p, cw_cols):
    cw_rows = 48 if cw_cols == 1536 else 16
    o = cw_rows
    conv_w = p[:cw_rows].reshape(-1)[:4 * cw_cols].reshape(1, 4, cw_cols)
    norm_pre = p[o:o + 8].reshape(1, D)
    conv_b = p[o + 8:o + 24].reshape(-1)[:1536].reshape(1, 1536)
    ssm_norm = p[o + 24:o + 32].reshape(1, D)
    norm_post = p[o + 32:o + 40].reshape(1, D)
    vec = p[o + 40:o + 48]
    return conv_w, norm_pre, conv_b, ssm_norm, norm_post, vec[0:1, :NH], vec[1:2, :NH], vec[2:3, :NH], vec[3, 0]


def _pad_lanes(a):
    return jnp.pad(a, ((0, 0), (0, LANE - a.shape[1])))


class _GradReduce:
    def __init__(self, chip, ci):
        self.ci = ci
        self.cidx = jnp.reshape(ci, (1,)).astype(jnp.int32)
        self.chip_idx = jnp.reshape(chip, (1,)).astype(jnp.int32)

    def start(self, dw_all, dw_out):
        gw = jnp.stack([dw_all[:, k * SHARD:(k + 1) * SHARD] for k in range(4)])
        go = dw_out.reshape(4, D // 2, D)
        rw, ro = _pair_exchange([gw, go], "pair_exchange")
        self.own = [_pair_sum(self.cidx, gw, rw, "pair_sum_in"), _pair_sum(self.cidx, go, ro, "pair_sum_out")]
        return _ChipExchange(self.own)

    def done(self, got):
        self.got = got

    def result(self):
        half_in = _chip_sum(self.chip_idx, self.got[0], self.own[0], "chip_sum_in")
        half_out = _chip_sum(self.chip_idx, self.got[1], self.own[1], "chip_sum_out")
        their_in, their_out = _half_exchange(half_in, half_out)
        return _by_core(self.ci, half_in, their_in), _by_core(self.ci, half_out, their_out)


def kernel(x, norm_pre_w, w_in, conv_w, conv_b, dt_bias, a_log, d_skip, ssm_norm_w, w_out, norm_post_w, loss_target, m_norm_pre_w, m_w_in, m_conv_w, m_conv_b, m_dt_bias, m_a_log, m_d_skip, m_ssm_norm_w, m_w_out, m_norm_post_w, v_norm_pre_w, v_w_in, v_conv_w, v_conv_b, v_dt_bias, v_a_log, v_d_skip, v_ssm_norm_w, v_w_out, v_norm_post_w):
    xi, yi, ci = lax.axis_index("x"), lax.axis_index("y"), lax.axis_index("c")
    chip = 2 * xi + yi
    x2, tgt = x[0], loss_target[0]

    w_in_b, w_out_b = w_in[0].astype(BF16), w_out[0].astype(BF16)
    gin, gout, gcw = _gather_weights(w_in_b, w_out_b, conv_w[0])

    def whole(own, gathered, axis):
        return jnp.concatenate([jnp.where(chip == k, own, gathered[k]) for k in range(4)], axis=axis)

    w_all = jnp.concatenate([whole(w_in_b, gin, 1), jnp.zeros((D, DP - 4 * SHARD), BF16)], axis=1)
    w_out_all = whole(w_out_b, gout, 0)
    cw_all = whole(conv_w[0], gcw, 1)
    reduce = _GradReduce(chip, ci)
    grad_x, small = _local_step(x2, tgt, w_all, w_out_all, cw_all, norm_pre_w, conv_b, dt_bias, a_log, d_skip,
                                ssm_norm_w, norm_post_w, reduce)[:2]
    g_in, g_out = reduce.result()
    g_small = _slot_sum(_small_exchange(small), "small_sum")
    g_cw, g_npre, g_cb, g_nssm, g_npost, g_dtb, g_alog, g_dsk, loss = _unpack_small(g_small, 1536)
    g_cw = lax.dynamic_slice_in_dim(g_cw, chip * 384, 384, axis=2)

    d_in, nm_in, nv_in = _adamw(w_in[0], g_in, m_w_in[0], v_w_in[0], "adamw_in")
    d_out, nm_out, nv_out = _adamw(w_out[0], g_out, m_w_out[0], v_w_out[0], "adamw_out")
    packed = [_pack_small(*t) for t in (
        (conv_w, norm_pre_w, conv_b, ssm_norm_w, norm_post_w, dt_bias, a_log, d_skip),
        (g_cw, g_npre, g_cb, g_nssm, g_npost, g_dtb, g_alog, g_dsk),
        (m_conv_w, m_norm_pre_w, m_conv_b, m_ssm_norm_w, m_norm_post_w, m_dt_bias, m_a_log, m_d_skip),
        (v_conv_w, v_norm_pre_w, v_conv_b, v_ssm_norm_w, v_norm_post_w, v_dt_bias, v_a_log, v_d_skip))]
    small_out = [_unpack_small(p, 384)[:8] for p in _adamw(*packed, "adamw_small")]

    def ordered(cw_, npre, cb_, nssm, npost, dtb_, alog_, dsk_, big_in, big_out):
        return [npre, big_in[None], cw_, cb_, dtb_, alog_, dsk_, nssm, big_out[None], npost]

    grads = ordered(g_cw, g_npre, g_cb, g_nssm, g_npost, g_dtb, g_alog, g_dsk, g_in, g_out)
    deltas = ordered(*small_out[0], d_in, d_out)
    new_m = ordered(*small_out[1], nm_in, nm_out)
    new_v = ordered(*small_out[2], nv_in, nv_out)
    return (loss, grad_x[None], *grads, *deltas, *new_m, *new_v)


def _local_step(x2, tgt, w_all, w_out_all, cw_all, norm_pre_w, conv_b, dt_bias, a_log, d_skip, ssm_norm_w,
                norm_post_w, reduce=None):
    dtb, alog = _pad_lanes(dt_bias), _pad_lanes(a_log)
    d_b = jnp.repeat(d_skip, 64, axis=1)

    proj, u = _inproj_fwd(x2, norm_pre_w, w_all)
    mix, attn_pre, lse = _attn_fwd(proj, 1, _attn_fwd(proj, 4, _attn_fwd(proj, 16)), final=True)
    mix, y_save, states = _ssm_fwd(proj, mix, cw_all, conv_b, dtb, alog, d_b, ssm_norm_w)

    dy, dmix, dw_out, dnw_post, loss_part = _outproj_loss(mix, w_out_all, x2, tgt, norm_post_w)
    do, delta, dg = _attn_gate_bwd(dmix, attn_pre, proj)
    dz, dxbcdt, dcw, dcb, dvec, dnw_ssm = _ssm_bwd(proj, dmix, y_save, states, cw_all, conv_b, dtb, alog, d_b,
                                                   ssm_norm_w)
    acc = _attn_bwd(proj, do, lse, delta, 16, None, F32)
    acc = _attn_bwd(proj, do, lse, delta, 4, acc, F32)
    dq, dk, dv = _attn_bwd(proj, do, lse, delta, 1, acc, BF16)
    srcs = [dq, dk, dv, dg, dz]
    dws = [_dw(u, s, f"dw_in_{n}") for s, n in zip(srcs + [dxbcdt], ("q", "k", "v", "g", "z", "xbcdt"))]
    dw_all = jnp.concatenate(dws, axis=1)
    res = _inproj_bwd_dx(srcs, dxbcdt, w_all, x2, dy, norm_pre_w, reduce.start(dw_all, dw_out) if reduce else None)
    if reduce:
        res, got = res
        reduce.done(got)
    grad_x, dnw_pre = res
    small = _pack_small(dcw, dnw_pre, dcb, dnw_ssm, dnw_post, dvec[0:1, :NH], dvec[1:2, :NH], dvec[2:3, :NH],
                        loss_part[:, :1])
    return grad_x, small, dw_all, dw_out
```

```python
import functools

import jax
import jax.numpy as jnp
from jax import lax
from jax.experimental import pallas as pl
from jax.experimental.pallas import tpu as pltpu

F32 = jnp.float32
BF16 = jnp.bfloat16
MESH = pl.DeviceIdType.MESH
SDS = jax.ShapeDtypeStruct
ANY = pl.BlockSpec(memory_space=pl.ANY)

S = 4096
D = 1024
DP = 7168
SHARD = 1668
OFF_G, OFF_Z = 3072, 4096
NH = 16
CH = 128
NC = S // CH
EPS = 1e-6
NEG = -1e30
LANE = 128
VMEM_LIMIT = 48 * 1024 * 1024

ADAM_LR, ADAM_B1, ADAM_B2, ADAM_EPS, ADAM_WD, ADAM_STEP = 0.001, 0.9, 0.999, 1e-08, 0.01, 10


def _cp(sem, **kw):
    return pltpu.CompilerParams(dimension_semantics=sem, vmem_limit_bytes=VMEM_LIMIT, **kw)


def _dot(a, b):
    return jnp.dot(a, b, preferred_element_type=F32)


def _dot_nt(a, b):
    return lax.dot_general(a, b, (((1,), (1,)), ((), ())), preferred_element_type=F32)


def _dot_tn(a, b):
    return lax.dot_general(a, b, (((0,), (0,)), ((), ())), preferred_element_type=F32)


def _pieces(x, n):
    out = []
    for _ in range(n):
        p = x.astype(BF16)
        out.append(p)
        x = x - p.astype(F32)
    return out


def _pick(x, sel, n=2):
    parts = [_dot(p, sel) for p in _pieces(x, n)]
    return functools.reduce(jnp.add, parts)


def _pick_left(sel, x, n=3):
    parts = [_dot(sel, p) for p in _pieces(x, n)]
    return functools.reduce(jnp.add, parts)


def _sigmoid(v):
    return 0.5 * jnp.tanh(0.5 * v) + 0.5


def _iota(shape, dim):
    return lax.broadcasted_iota(jnp.int32, shape, dim)


def _inproj_fwd(x, nw, w_all):
    tm, tn = 1024, 1024

    def body(x_ref, nw_ref, w_ref, proj_ref, u_ref):
        @pl.when(pl.program_id(1) == 0)
        def _():
            xf = x_ref[...]
            r = lax.rsqrt(jnp.mean(xf * xf, axis=-1, keepdims=True) + EPS)
            u_ref[...] = (xf * r * nw_ref[...]).astype(BF16)

        proj_ref[...] = _dot(u_ref[...], w_ref[...])

    return pl.pallas_call(
        body, name="inproj_fwd", grid=(S // tm, DP // tn),
        in_specs=[pl.BlockSpec((tm, D), lambda i, j: (i, 0)), pl.BlockSpec((1, D), lambda i, j: (0, 0)),
                  pl.BlockSpec((D, tn), lambda i, j: (0, j))],
        out_specs=[pl.BlockSpec((tm, tn), lambda i, j: (i, j)), pl.BlockSpec((tm, D), lambda i, j: (i, 0))],
        out_shape=[SDS((S, DP), F32), SDS((S, D), BF16)],
        compiler_params=_cp(("parallel", "arbitrary")),
    )(x, nw, w_all)


ATTN_QB = {1: 16, 4: 4, 16: 1}


def _unit_rows(r, u, d):
    return pl.ds(r + d * CH * u, CH, stride=d) if d > 1 else pl.ds(CH * u, CH)


def _for_units(d, qb, fn):
    for r in range(d):
        for u in range(qb):
            fn(r, u)


def _attn_mask(has_prev):
    qi, kj = _iota((2 * CH, 2 * CH), 0) & (CH - 1), _iota((2 * CH, 2 * CH), 1)
    cur_ok = (kj >= CH) & (kj - CH <= qi)
    prev_ok = (kj < CH) & (kj >= qi)
    return cur_ok | (prev_ok & has_prev)


def _stack_heads(v, lane_a):
    return jnp.concatenate([jnp.where(lane_a, v, 0.0), jnp.where(lane_a, 0.0, v)], axis=0).astype(BF16)


def _attn_specs(d, qb):
    rows, prows = CH * d * qb, CH * d
    nb = S // rows
    steps = (NH // 2) * nb

    def at(t):
        t = jnp.minimum(t, steps - 1)
        return t % nb, t // nb

    def cur(off):
        return pl.BlockSpec((rows, LANE), lambda t: (at(t)[0], off + at(t)[1]))

    def prev(off):
        return pl.BlockSpec((prows, LANE), lambda t: (jnp.maximum(at(t)[0] * qb - 1, 0), off + at(t)[1]))

    lag = pl.BlockSpec((rows, LANE), lambda t: at(jnp.maximum(t - 1, 0)))
    return nb, steps, cur, prev, lag


def _gather16(src_ref, dense_ref, tmp_ref):
    for a in range(4):
        tmp_ref[...] = src_ref[pl.ds(a, 4 * CH, stride=4), :]
        for b in range(4):
            dense_ref[a + 4 * b] = tmp_ref[pl.ds(b, CH, stride=4), :]


def _scatter16(dense_ref, dst_ref, tmp_ref):
    for a in range(4):
        for b in range(4):
            tmp_ref[pl.ds(b, CH, stride=4), :] = dense_ref[a + 4 * b]
        dst_ref[pl.ds(a, 4 * CH, stride=4), :] = tmp_ref[...]


def _unit_index(r, u, d):
    return (r,) if d == 16 else (_unit_rows(r, u, d), slice(None))


def _unit_kv(p_ref, c_ref, r, u, d):
    prev = p_ref[_unit_index(r, 0, d)] if u == 0 else c_ref[_unit_index(r, u - 1, d)]
    return jnp.concatenate([prev, c_ref[_unit_index(r, u, d)]], axis=0).astype(BF16)


def _dense_scratch(d, n):
    return [pltpu.VMEM((16, CH, LANE), F32)] * n + [pltpu.VMEM((4 * CH, LANE), F32)] if d == 16 else []


def _attn_fwd(proj, d, prior=None, final=False):
    qb = ATTN_QB[d]
    nb, steps, cur, prev, _ = _attn_specs(d, qb)
    n_prior = 2 if prior is not None else 0
    n_in, n_out = 5 + n_prior + final, 2 + final
    assert not (d == 16 and (n_prior or final))

    def body(*refs):
        ins, outs, scratch = refs[:n_in], refs[n_in:n_in + n_out], refs[n_in + n_out:]
        if d == 16:
            tmp_ref = scratch[-1]
            for src, dense in zip(ins, scratch):
                _gather16(src, dense, tmp_ref)
            block_outs, ins, outs = outs, scratch[:n_in], scratch[n_in:n_in + n_out]
        q_ref, kp_ref, kc_ref, vp_ref, vc_ref = ins[:5]
        prior_refs = ins[5:5 + n_prior]
        if final:
            g_ref, (mix_ref, o_ref, l_ref) = ins[-1], outs
        else:
            o_ref, l_ref = outs
        i = pl.program_id(0) % nb
        lane_a = _iota((CH, LANE), 1) < 64
        mask_first, mask_rest = _attn_mask(i > 0), _attn_mask(True)

        def unit(r, u):
            at = _unit_index(r, u, d)
            q2 = _stack_heads(q_ref[at] * 0.125, lane_a)
            k2, v2 = _unit_kv(kp_ref, kc_ref, r, u, d), _unit_kv(vp_ref, vc_ref, r, u, d)
            s = jnp.where(mask_first if u == 0 else mask_rest, _dot_nt(q2, k2), NEG)
            m = jnp.max(s, axis=1, keepdims=True)
            p = jnp.exp(s - m)
            l = jnp.sum(p, axis=1, keepdims=True)
            o2 = _dot(p.astype(BF16), v2) / l
            lse2 = m + jnp.log(l)
            o = jnp.where(lane_a, o2[:CH], o2[CH:])
            lse = jnp.where(lane_a, lse2[:CH], lse2[CH:])
            if n_prior:
                o_a, l_a = prior_refs[0][at], prior_refs[1][at]
                top = jnp.maximum(l_a, lse)
                e_a, e_b = jnp.exp(l_a - top), jnp.exp(lse - top)
                tot = e_a + e_b
                o = (e_a * o_a + e_b * o) / tot
                lse = top + jnp.log(tot)
            o_ref[at] = o
            l_ref[at] = lse
            if final:
                g = g_ref[at]
                mix_ref[at] = (o * (g * _sigmoid(g))).astype(BF16)

        _for_units(d, qb, unit)
        if d == 16:
            for dense, dst in zip(outs, block_outs):
                _scatter16(dense, dst, tmp_ref)

    in_specs = [cur(0), prev(8), cur(8), prev(16), cur(16)] + [cur(0)] * n_prior
    args = [proj] * 5 + (list(prior) if n_prior else [])
    out_specs, out_shape = [cur(0), cur(0)], [SDS((S, D), F32), SDS((S, D), F32)]
    if final:
        assert d == 1
        in_specs.append(cur(OFF_G // LANE))
        args.append(proj)
        out_specs, out_shape = [cur(0)] + out_specs, [SDS((S, 2 * D), BF16)] + out_shape
    return pl.pallas_call(
        body, name=f"attn_fwd_d{d}", grid=(steps,),
        in_specs=in_specs, out_specs=out_specs, out_shape=out_shape,
        scratch_shapes=_dense_scratch(d, n_in + n_out),
        compiler_params=_cp(("parallel",)),
    )(*args)


def _attn_gate_bwd(dmix, pre, proj):
    tm = 512

    def body(dm_ref, pre_ref, g_ref, do_ref, delta_ref, dg_ref):
        g, dm, pre_v = g_ref[...], dm_ref[...], pre_ref[...]
        sig = _sigmoid(g)
        do = dm * (g * sig)
        do_ref[...] = do
        dg_ref[...] = (dm * pre_v * (sig * (1.0 + g * (1.0 - sig)))).astype(BF16)
        prod = do * pre_v
        same_head = (_iota((LANE, LANE), 0) // 64 == _iota((LANE, LANE), 1) // 64).astype(BF16)
        for cb in range(D // LANE):
            delta_ref[:, cb * LANE:(cb + 1) * LANE] = _pick(prod[:, cb * LANE:(cb + 1) * LANE], same_head)

    t = pl.BlockSpec((tm, D), lambda i: (i, 0))
    return pl.pallas_call(
        body, name="attn_gate_bwd", grid=(S // tm,),
        in_specs=[t, t, pl.BlockSpec((tm, D), lambda i: (i, OFF_G // D))],
        out_specs=[t, t, t],
        out_shape=[SDS((S, D), F32), SDS((S, D), F32), SDS((S, D), BF16)],
        compiler_params=_cp(("parallel",)),
    )(dmix, pre, proj)


def _attn_bwd(proj, do, lse, delta, d, acc, out_dtype):
    qb = ATTN_QB[d]
    nb, steps, cur, prev, lag = _attn_specs(d, qb)
    has_acc = acc is not None
    n_in = 11 if has_acc else 8
    assert not (d == 16 and (has_acc or out_dtype != F32))
    rows = CH * d * qb
    carry = (2, 16, CH, LANE) if d == 16 else (2, rows, LANE)

    def body(*refs):
        ins, (dq_ref, dk_ref, dv_ref), scratch = refs[:n_in], refs[n_in:n_in + 3], refs[n_in + 3:]
        ck_ref, cv_ref = scratch[:2]
        dq_f32 = dq_ref if out_dtype == F32 else scratch[2]
        t = pl.program_id(0)
        i = t % nb
        if d == 16:
            dense, dq_f32, tmp_ref = scratch[2:2 + n_in], scratch[2 + n_in], scratch[-1]

            @pl.when(t < steps)
            def _():
                for src, dst in zip(ins, dense):
                    _gather16(src, dst, tmp_ref)

            ins = dense
        q_ref, kp_ref, kc_ref, vp_ref, vc_ref, do_ref, lse_ref, dl_ref = ins[:8]
        if has_acc:
            aq_ref, ak_ref, av_ref = ins[8:11]
        slot = t & 1
        now_k, now_v, old_k, old_v = ck_ref.at[slot], cv_ref.at[slot], ck_ref.at[1 - slot], cv_ref.at[1 - slot]
        lane_a = _iota((CH, LANE), 1) < 64
        mask_first, mask_rest = _attn_mask(i > 0), _attn_mask(True)

        @pl.when(t == 0)
        def _():
            ck_ref[1] = jnp.zeros(carry[1:], F32)
            cv_ref[1] = jnp.zeros(carry[1:], F32)

        def unit(r, u):
            at = _unit_index(r, u, d)
            q2 = _stack_heads(q_ref[at] * 0.125, lane_a)
            do2 = _stack_heads(do_ref[at], lane_a)
            k2, v2 = _unit_kv(kp_ref, kc_ref, r, u, d), _unit_kv(vp_ref, vc_ref, r, u, d)
            lsev, dlv = lse_ref[at], dl_ref[at]
            lse2 = jnp.concatenate([lsev[:, 0:1], lsev[:, 64:65]], axis=0)
            dl2 = jnp.concatenate([dlv[:, 0:1], dlv[:, 64:65]], axis=0)
            p = jnp.exp(jnp.where(mask_first if u == 0 else mask_rest, _dot_nt(q2, k2), NEG) - lse2)
            ds = (p * (_dot_nt(do2, v2) - dl2)).astype(BF16)
            dq2 = _dot(ds, k2)
            dk2 = _dot_tn(ds, q2)
            dv2 = _dot_tn(p.astype(BF16), do2)
            dq = jnp.where(lane_a, dq2[:CH], dq2[CH:]) * 0.125
            if has_acc:
                dq = dq + aq_ref[at]
            dq_f32[at] = dq
            if u == 0:
                before = _unit_index(r, qb - 1, d)
                old_k[before] += dk2[:CH]
                old_v[before] += dv2[:CH]
            else:
                before = _unit_index(r, u - 1, d)
                now_k[before] += dk2[:CH]
                now_v[before] += dv2[:CH]
            now_k[at] = dk2[CH:]
            now_v[at] = dv2[CH:]

        @pl.when(t < steps)
        def _():
            _for_units(d, qb, unit)
            if d == 16:
                _scatter16(dq_f32, dq_ref, tmp_ref)
            elif out_dtype != F32:
                dq_ref[...] = dq_f32[...].astype(out_dtype)

        if d == 16:
            _scatter16(old_k, dk_ref, tmp_ref)
            _scatter16(old_v, dv_ref, tmp_ref)
        else:
            dk, dv = old_k[...], old_v[...]
            if has_acc:
                dk, dv = dk + ak_ref[...], dv + av_ref[...]
            dk_ref[...] = dk.astype(out_dtype)
            dv_ref[...] = dv.astype(out_dtype)

    in_specs = [cur(0), prev(8), cur(8), prev(16), cur(16), cur(0), cur(0), cur(0)]
    args = [proj, proj, proj, proj, proj, do, lse, delta]
    if has_acc:
        in_specs += [cur(0), lag, lag]
        args += list(acc)
    scratch = [pltpu.VMEM(carry, F32), pltpu.VMEM(carry, F32)]
    if d == 16:
        scratch += _dense_scratch(d, n_in + 1)
    elif out_dtype != F32:
        scratch.append(pltpu.VMEM((rows, LANE), F32))
    return pl.pallas_call(
        body, name=f"attn_bwd_d{d}", grid=(steps + 1,),
        in_specs=in_specs, out_specs=[cur(0), lag, lag], out_shape=[SDS((S, D), out_dtype)] * 3,
        scratch_shapes=scratch, compiler_params=_cp(("arbitrary",)),
    )(*args)


def _conv_taps(cur, prev8, first):
    row8 = _iota(prev8.shape, 0)
    prev8 = jnp.where(first, 0.0, prev8)
    taps = []
    for s in (3, 2, 1):
        rolled = pltpu.roll(cur, s, 0)
        head = jnp.where(row8 < s, pltpu.roll(prev8, s, 0), rolled[:8])
        taps.append(jnp.concatenate([head, rolled[8:]], axis=0))
    return taps + [cur]


def _conv(taps, w, b):
    acc = b + w[0:1, :] * taps[0]
    for k in (1, 2, 3):
        acc = acc + w[k:k + 1, :] * taps[k]
    return acc


def _expand():
    return (_iota((LANE, D), 1) // 64 == _iota((LANE, D), 0)).astype(BF16)


def _reduce():
    return (_iota((D, LANE), 0) // 64 == _iota((D, LANE), 1)).astype(BF16)


def _ssd_common(xs_raw, xs_prev, bc_raw, bc_prev, dt_raw, first, cw, cb, dtb, alog):
    head_lane = _iota((CH, LANE), 1) < NH
    xs_taps = _conv_taps(xs_raw, xs_prev, first)
    bc_taps = _conv_taps(bc_raw, bc_prev, first)
    xs_c = _conv(xs_taps, cw[:, :D], cb[:, :D])
    bc_c = _conv(bc_taps, cw[:, D:], cb[:, D:])
    xs = xs_c * _sigmoid(xs_c)
    bc = bc_c * _sigmoid(bc_c)
    pre = dt_raw + dtb
    dt = jnp.where(head_lane, jnp.maximum(pre, 0.0) + jnp.log(1.0 + jnp.exp(-jnp.abs(pre))), 0.0)
    a_row = jnp.where(head_lane[0:1], -jnp.exp(alog), 0.0)
    tri = (_iota((CH, CH), 1) <= _iota((CH, CH), 0)).astype(BF16)
    cs = _pick_left(tri, dt * a_row)
    cs_last = cs[CH - 1:CH, :]
    wide = _pick(jnp.concatenate([dt, jnp.exp(cs), jnp.exp(cs_last - cs)], axis=0), _expand())
    dt_b, e_b, f_b = wide[:CH], wide[CH:2 * CH], wide[2 * CH:]
    return dict(xs_taps=xs_taps, bc_taps=bc_taps, xs_c=xs_c, bc_c=bc_c, xs=xs, bc=bc, pre=pre, dt=dt,
                a_row=a_row, cs=cs, cs_t=cs.T, dt_b=dt_b, e_b=e_b, f_b=f_b, t_b=e_b[CH - 1:CH, :])


def _groups(bc):
    bcb = bc.astype(BF16)
    return [bcb[:, 0:128], bcb[:, 128:256]], [bcb[:, 256:384], bcb[:, 384:512]]


def _decay(q, h, tril):
    seg = q["cs"][:, h:h + 1] - q["cs_t"][h:h + 1, :]
    return jnp.exp(jnp.where(tril, seg, NEG))


def _ssm_fwd(proj, mix, cw, cb, dtb, alog, d_b, nw):
    def body(xs_ref, xsp_ref, bc_ref, bcp_ref, dt_ref, z_ref, cw_ref, cb_ref, dtb_ref, alog_ref, db_ref, nw_ref,
             mix_in_ref, mix_ref, y_ref, st_ref, h_ref):
        del mix_in_ref
        i = pl.program_id(0)

        @pl.when(i == 0)
        def _():
            h_ref[...] = jnp.zeros_like(h_ref)

        q = _ssd_common(xs_ref[...], xsp_ref[...], bc_ref[...], bcp_ref[...], dt_ref[...], i == 0,
                        cw_ref[...], cb_ref[...], dtb_ref[...], alog_ref[...])
        bg, cg = _groups(q["bc"])
        xs = q["xs"]
        xdt = xs * q["dt_b"]
        xdt_b = xdt.astype(BF16)
        h_in = h_ref[...]
        st_ref[...] = h_in
        hb = h_in.astype(BF16)
        tril = _iota((CH, CH), 1) <= _iota((CH, CH), 0)
        lane_a = _iota((CH, LANE), 1) < 64
        cbm = [_dot_nt(cg[g], bg[g]) for g in range(2)]
        pairs = []
        for hp in range(NH // 2):
            xp = xdt_b[:, hp * LANE:(hp + 1) * LANE]
            ya = _dot((cbm[hp // 4] * _decay(q, 2 * hp, tril)).astype(BF16), xp)
            yb = _dot((cbm[hp // 4] * _decay(q, 2 * hp + 1, tril)).astype(BF16), xp)
            pairs.append(jnp.where(lane_a, ya, yb))
        y_diag = jnp.concatenate(pairs, axis=1)
        y_off = jnp.concatenate([_dot(cg[g], hb[:, g * 512:(g + 1) * 512]) for g in range(2)], axis=1) * q["e_b"]
        y = y_diag + y_off + db_ref[...] * xs
        y_ref[...] = y
        xf = (xdt * q["f_b"]).astype(BF16)
        h_ref[...] = q["t_b"] * h_in + jnp.concatenate(
            [_dot_tn(bg[g], xf[:, g * 512:(g + 1) * 512]) for g in range(2)], axis=1)
        z = z_ref[...]
        yz = y * (z * _sigmoid(z))
        outs = []
        for g in range(2):
            v = yz[:, g * 512:(g + 1) * 512]
            outs.append(v * lax.rsqrt(jnp.mean(v * v, axis=-1, keepdims=True) + EPS))
        mix_ref[...] = (jnp.concatenate(outs, axis=1) * nw_ref[...]).astype(BF16)

    def col(width, blk, prev=False):
        if prev:
            return pl.BlockSpec((8, width), lambda i: (jnp.maximum(i * (CH // 8) - 1, 0), blk))
        return pl.BlockSpec((CH, width), lambda i: (i, blk))

    def full(a):
        return pl.BlockSpec(a.shape, lambda i: (0,) * a.ndim)

    return pl.pallas_call(
        body, name="ssm_fwd", grid=(NC,),
        in_specs=[col(D, 5), col(D, 5, True), col(512, 12), col(512, 12, True), col(LANE, 52), col(D, 4),
                  full(cw), full(cb), full(dtb), full(alog), full(d_b), full(nw), ANY],
        out_specs=[col(D, 1), col(D, 0), pl.BlockSpec((None, CH, D), lambda i: (i, 0, 0))],
        out_shape=[SDS((S, 2 * D), BF16), SDS((S, D), F32), SDS((NC, CH, D), F32)],
        scratch_shapes=[pltpu.VMEM((CH, D), F32)],
        input_output_aliases={12: 0},
        compiler_params=_cp(("arbitrary",)),
    )(proj, proj, proj, proj, proj, proj, cw, cb, dtb, alog, d_b, nw, mix)


def _ssm_bwd(proj, dmix, y_save, states, cw, cb, dtb, alog, d_b, nw):
    def body(xs_ref, xsp_ref, bc_ref, bcp_ref, dt_ref, z_ref, dn_ref, y_ref, st_ref,
             cw_ref, cb_ref, dtb_ref, alog_ref, db_ref, nw_ref,
             dz_ref, dx_ref, dcw_ref, dcb_ref, dsm_ref, dnw_ref, dh_ref, nxs_ref, nbc_ref):
        i = pl.program_id(0)
        ci = NC - 1 - i

        @pl.when(i == 0)
        def _():
            for ref in (dcw_ref, dcb_ref, dsm_ref, dnw_ref, dh_ref, nxs_ref, nbc_ref):
                ref[...] = jnp.zeros_like(ref)

        cw, cb = cw_ref[...], cb_ref[...]
        q = _ssd_common(xs_ref[...], xsp_ref[...], bc_ref[...], bcp_ref[...], dt_ref[...], ci == 0,
                        cw, cb, dtb_ref[...], alog_ref[...])
        bg, cg = _groups(q["bc"])
        xs, dt_b, e_b, f_b, t_b = q["xs"], q["dt_b"], q["e_b"], q["f_b"], q["t_b"]
        xdt = xs * dt_b
        xdt_b = xdt.astype(BF16)
        h_in = st_ref[...]
        hb = h_in.astype(BF16)
        dh_new = dh_ref[...]
        dhb = dh_new.astype(BF16)
        red = _reduce()

        z, y, dn, nw_v = z_ref[...], y_ref[...], dn_ref[...], nw_ref[...]
        sig = _sigmoid(z)
        sz = z * sig
        yz = y * sz
        gdn = dn * nw_v
        dyz, dnw = [], []
        for g in range(2):
            v, gv = yz[:, g * 512:(g + 1) * 512], gdn[:, g * 512:(g + 1) * 512]
            r = lax.rsqrt(jnp.mean(v * v, axis=-1, keepdims=True) + EPS)
            dnw.append(dn[:, g * 512:(g + 1) * 512] * v * r)
            dyz.append(r * (gv - v * (r * r) * jnp.mean(gv * v, axis=-1, keepdims=True)))
        dyz = jnp.concatenate(dyz, axis=1)
        dnw_ref[...] += jnp.sum(jnp.concatenate(dnw, axis=1), axis=0, keepdims=True)
        dy = dyz * sz
        dz_ref[...] = (dyz * y * (sig * (1.0 + z * (1.0 - sig)))).astype(BF16)
        dy_b = dy.astype(BF16)

        tril = _iota((CH, CH), 1) <= _iota((CH, CH), 0)
        lane_a = _iota((CH, LANE), 1) < 64
        cbm = [_dot_nt(cg[g], bg[g]) for g in range(2)]
        dcbm = [jnp.zeros((CH, CH), F32), jnp.zeros((CH, CH), F32)]
        seg_rows = jnp.zeros((CH, LANE), F32)
        seg_cols = jnp.zeros((LANE, CH), F32)
        row_id, col_id = _iota((CH, LANE), 0), _iota((CH, LANE), 1)
        dx_pairs = []
        for hp in range(NH // 2):
            g = hp // 4
            xp = xdt_b[:, hp * LANE:(hp + 1) * LANE]
            dyp_f = dy[:, hp * LANE:(hp + 1) * LANE]
            dyp = dy_b[:, hp * LANE:(hp + 1) * LANE]
            halves = []
            for k in range(2):
                h = 2 * hp + k
                lane = lane_a if k == 0 else jnp.logical_not(lane_a)
                dec = _decay(q, h, tril)
                gm = cbm[g] * dec
                dgm = _dot_nt(jnp.where(lane, dyp_f, 0.0).astype(BF16), xp)
                dcbm[g] = dcbm[g] + dgm * dec
                prod = dgm * gm
                seg_rows = jnp.where(col_id == h, jnp.sum(prod, axis=1, keepdims=True), seg_rows)
                seg_cols = jnp.where(row_id == h, jnp.sum(prod, axis=0, keepdims=True), seg_cols)
                halves.append(_dot_tn(gm.astype(BF16), dyp))
            dx_pairs.append(jnp.where(lane_a, halves[0], halves[1]))
        dxdt_diag = jnp.concatenate(dx_pairs, axis=1)

        qv = jnp.concatenate([_dot(bg[g], dhb[:, g * 512:(g + 1) * 512]) for g in range(2)], axis=1)
        y_off = jnp.concatenate([_dot(cg[g], hb[:, g * 512:(g + 1) * 512]) for g in range(2)], axis=1) * e_b
        xfq = xdt * f_b * qv
        dxdt = dxdt_diag + f_b * qv
        tdt = jnp.sum(dh_new * h_in, axis=0, keepdims=True) * t_b
        per_head = _pick(jnp.concatenate([xfq, dy * y_off, dxdt * xs, dy * xs, jnp.broadcast_to(tdt, (8, D))],
                                         axis=0), red)
        fdf, dyoff_h, dxdtxs_h, dyxs_h = [per_head[k * CH:(k + 1) * CH] for k in range(4)]
        dcs = seg_rows - seg_cols.T + dyoff_h - fdf
        last = per_head[4 * CH:4 * CH + 1] + jnp.sum(fdf, axis=0, keepdims=True)
        dcs = dcs + jnp.where(_iota((CH, LANE), 0) == CH - 1, last, 0.0)
        tri_t = (_iota((CH, CH), 1) >= _iota((CH, CH), 0)).astype(BF16)
        da = _pick_left(tri_t, dcs)
        ddt = da * q["a_row"] + dxdtxs_h
        dxs = dxdt * dt_b + db_ref[...] * dy
        ddt_raw = ddt * _sigmoid(q["pre"])
        dsm_ref[0:1, :] += jnp.sum(ddt_raw, axis=0, keepdims=True)
        dsm_ref[1:2, :] += jnp.sum(da * q["dt"], axis=0, keepdims=True) * q["a_row"]
        dsm_ref[2:3, :] += jnp.sum(dyxs_h, axis=0, keepdims=True)
        edy = (e_b * dy).astype(BF16)
        xf = (xdt * f_b).astype(BF16)
        dbs, dcs_g, dhs = [], [], []
        for g in range(2):
            sl = slice(g * 512, (g + 1) * 512)
            dcb_b = dcbm[g].astype(BF16)
            dcs_g.append(_dot(dcb_b, bg[g]) + _dot_nt(edy[:, sl], hb[:, sl]))
            dbs.append(_dot_tn(dcb_b, cg[g]) + _dot_nt(xf[:, sl], dhb[:, sl]))
            dhs.append(_dot_tn(cg[g], edy[:, sl]))
        dh_ref[...] = t_b * dh_new + jnp.concatenate(dhs, axis=1)
        dbc = jnp.concatenate(dbs + dcs_g, axis=1)

        def conv_bwd(dact, pre, taps, w, nxt_ref, lo):
            s = _sigmoid(pre)
            dconv = dact * (s * (1.0 + pre * (1.0 - s)))
            nxt8 = nxt_ref[...]
            row8 = _iota(nxt8.shape, 0)
            hi = lo + dconv.shape[1]
            dcb_ref[:, lo:hi] += jnp.sum(dconv, axis=0, keepdims=True)
            dx = w[3:4, :] * dconv
            for k in range(4):
                dcw_ref[k:k + 1, lo:hi] += jnp.sum(dconv * taps[k], axis=0, keepdims=True)
            for s_ in (1, 2, 3):
                rolled = pltpu.roll(dconv, CH - s_, 0)
                tail = jnp.where(row8 >= 8 - s_, pltpu.roll(nxt8, 8 - s_, 0), rolled[CH - 8:])
                dx = dx + w[3 - s_:4 - s_, :] * jnp.concatenate([rolled[:CH - 8], tail], axis=0)
            nxt_ref[...] = dconv[:8]
            return dx

        dx_ref[:, 0:D] = conv_bwd(dxs, q["xs_c"], q["xs_taps"], cw[:, :D], nxs_ref, 0).astype(BF16)
        dx_ref[:, D:D + 512] = conv_bwd(dbc, q["bc_c"], q["bc_taps"], cw[:, D:], nbc_ref, D).astype(BF16)
        dx_ref[:, D + 512:D + 640] = ddt_raw.astype(BF16)
        dx_ref[:, D + 640:] = jnp.zeros((CH, D - 640), BF16)

    def col(width, blk, prev=False):
        if prev:
            return pl.BlockSpec((8, width), lambda i: (jnp.maximum((NC - 1 - i) * (CH // 8) - 1, 0), blk))
        return pl.BlockSpec((CH, width), lambda i: (NC - 1 - i, blk))

    def full(a):
        return pl.BlockSpec(a.shape, lambda i: (0,) * len(a.shape))

    acc_shapes = [SDS((4, 1536), F32), SDS((1, 1536), F32), SDS((8, LANE), F32), SDS((1, D), F32)]
    return pl.pallas_call(
        body, name="ssm_bwd", grid=(NC,),
        in_specs=[col(D, 5), col(D, 5, True), col(512, 12), col(512, 12, True), col(LANE, 52), col(D, 4),
                  col(D, 1), col(D, 0), pl.BlockSpec((None, CH, D), lambda i: (NC - 1 - i, 0, 0)),
                  full(cw), full(cb), full(dtb), full(alog), full(d_b), full(nw)],
        out_specs=[col(D, 0), col(2 * D, 0)] + [full(a) for a in acc_shapes],
        out_shape=[SDS((S, D), BF16), SDS((S, 2 * D), BF16)] + acc_shapes,
        scratch_shapes=[pltpu.VMEM((CH, D), F32), pltpu.VMEM((8, D), F32), pltpu.VMEM((8, 512), F32)],
        compiler_params=_cp(("arbitrary",)),
    )(proj, proj, proj, proj, proj, proj, dmix, y_save, states, cw, cb, dtb, alog, d_b, nw)


def _outproj_loss(mix, w_out, x, tgt, nw):
    tm = 256

    def body(mix_ref, w_ref, x_ref, t_ref, nw_ref, dy_ref, dmix_ref, dw_ref, dnw_ref, loss_ref):
        @pl.when(pl.program_id(0) == 0)
        def _():
            dw_ref[...] = jnp.zeros_like(dw_ref)
            dnw_ref[...] = jnp.zeros_like(dnw_ref)
            loss_ref[...] = jnp.zeros_like(loss_ref)

        mixv, w = mix_ref[...], w_ref[...]
        out = _dot(mixv, w)
        r = lax.rsqrt(jnp.mean(out * out, axis=-1, keepdims=True) + EPS)
        nh = out * r
        nw_v = nw_ref[...]
        err = x_ref[...] + nh * nw_v - t_ref[...]
        loss_ref[...] += 0.5 * jnp.sum(jnp.mean(err * err, axis=-1, keepdims=True), axis=0, keepdims=True)
        dy = err * (1.0 / D)
        dy_ref[...] = dy
        dnw_ref[...] += jnp.sum(dy * nh, axis=0, keepdims=True)
        gdn = dy * nw_v
        dout = (r * (gdn - nh * jnp.mean(gdn * nh, axis=-1, keepdims=True))).astype(BF16)
        dmix_ref[...] = _dot_nt(dout, w)
        dw_ref[...] += _dot_tn(mixv, dout)

    row = lambda w: pl.BlockSpec((tm, w), lambda i: (i, 0))
    full = lambda s: pl.BlockSpec(s, lambda i: (0, 0))
    return pl.pallas_call(
        body, name="outproj_loss", grid=(S // tm,),
        in_specs=[row(2 * D), full((2 * D, D)), row(D), row(D), full((1, D))],
        out_specs=[row(D), row(2 * D), full((2 * D, D)), full((1, D)), full((1, LANE))],
        out_shape=[SDS((S, D), F32), SDS((S, 2 * D), F32), SDS((2 * D, D), F32), SDS((1, D), F32),
                   SDS((1, LANE), F32)],
        compiler_params=_cp(("arbitrary",)),
    )(mix, w_out, x, tgt, nw)


def _inproj_bwd_dx(srcs, dxbcdt, w_all, x, dy, nw, hosted=None):
    tm = 512
    nk = DP // D
    n_host = len(hosted.arrays) if hosted else 0

    def body(*refs):
        src_refs = refs[:nk]
        w_ref, x_ref, dy_ref, nw_ref = refs[nk:nk + 4]
        host_in, refs = refs[nk + 4:nk + 4 + n_host], refs[nk + 4 + n_host:]
        gx_ref, dnw_ref = refs[:2]
        host_out, host_sems = refs[2:2 + n_host], refs[2 + n_host:]
        i = pl.program_id(0)

        @pl.when(i == 0)
        def _():
            if hosted:
                hosted.start(host_in, host_out, host_sems)
            dnw_ref[...] = jnp.zeros_like(dnw_ref)

        du = None
        for k, ref in enumerate(src_refs):
            part = _dot_nt(ref[...], w_ref[:, k * D:(k + 1) * D])
            du = part if du is None else du + part
        xf, nw_v = x_ref[...], nw_ref[...]
        r = lax.rsqrt(jnp.mean(xf * xf, axis=-1, keepdims=True) + EPS)
        xh = xf * r
        dnw_ref[...] += jnp.sum(du * xh, axis=0, keepdims=True)
        gdu = du * nw_v
        gx_ref[...] = r * (gdu - xh * jnp.mean(gdu * xh, axis=-1, keepdims=True)) + dy_ref[...]

        if hosted:
            pl.when(i == S // tm - 1)(lambda: hosted.finish(host_in, host_out, host_sems))

    row = pl.BlockSpec((tm, D), lambda i: (i, 0))
    row1 = pl.BlockSpec((tm, D), lambda i: (i, 1))
    one = pl.BlockSpec((1, D), lambda i: (0, 0))
    whole_w = pl.BlockSpec((D, DP), lambda i: (0, 0), pipeline_mode=pl.Buffered(1))
    args = [*srcs, dxbcdt, dxbcdt, w_all, x, dy, nw]
    in_specs = [row] * len(srcs) + [row, row1, whole_w, row, row, one]
    out_specs, out_shape, scratch = [row, one], [SDS((S, D), F32), SDS((1, D), F32)], []
    if hosted:
        args += hosted.arrays
        in_specs += [ANY] * n_host
        out_specs += [ANY] * n_host
        out_shape += hosted.out_shape
        scratch += hosted.scratch
    outs = pl.pallas_call(
        body, name="inproj_bwd_dx", grid=(S // tm,),
        in_specs=in_specs, out_specs=out_specs, out_shape=out_shape, scratch_shapes=scratch,
        compiler_params=_cp(("arbitrary",)),
    )(*args)
    return (outs[:2], outs[2:]) if hosted else outs


def _dw(u, dsec, name):
    ts = 512
    ncol = dsec.shape[1] // D

    def body(u_ref, d_ref, o_ref):
        @pl.when(pl.program_id(1) == 0)
        def _():
            o_ref[...] = jnp.zeros_like(o_ref)

        o_ref[...] += _dot_tn(u_ref[...], d_ref[...])

    return pl.pallas_call(
        body, name=name, grid=(ncol, S // ts),
        in_specs=[pl.BlockSpec((ts, D), lambda j, i: (i, 0)), pl.BlockSpec((ts, D), lambda j, i: (i, j))],
        out_specs=pl.BlockSpec((D, D), lambda j, i: (0, j)),
        out_shape=SDS((D, ncol * D), F32),
        compiler_params=_cp(("parallel", "arbitrary")),
    )(u, dsec)


def _place():
    x, y, c = lax.axis_index("x"), lax.axis_index("y"), lax.axis_index("c")
    return x, y, c, 2 * x + y


def _chip_of(x, y, k):
    px = 1 - x if k & 2 else x
    py = 1 - y if k & 1 else y
    return px, py, 2 * px + py


def _remote(src, dst, send_sem, recv_sem, dev):
    return pltpu.make_async_remote_copy(src_ref=src, dst_ref=dst, send_sem=send_sem, recv_sem=recv_sem,
                                        device_id=dev, device_id_type=MESH)


def _gather_weights(w_in_b, w_out_b, conv_w):
    def body(win_ref, wout_ref, cw_ref, gin_ref, gout_ref, gcw_ref, send, recv, csend, crecv, osend, orecv):
        x, y, c, j = _place()
        me, sib = (x, y, c), (x, y, 1 - c)
        nbr = {"x": _chip_of(x, y, 2), "y": _chip_of(x, y, 1)}
        diag = _chip_of(x, y, 3)[2]
        conv = [_remote(cw_ref, gcw_ref.at[j], csend.at[k - 1], crecv.at[k - 1], (*_chip_of(x, y, k)[:2], c))
                for k in (1, 2, 3)]
        own = [_remote(src, dst.at[j], osend.at[n], orecv.at[n], sib)
               for n, (src, dst) in enumerate(((win_ref, gin_ref), (wout_ref, gout_ref), (cw_ref, gcw_ref)))]
        for cp in conv + own:
            cp.start()
        started, arrivals = [], []
        pairs = ((win_ref, gin_ref), (wout_ref, gout_ref))

        def rows(src, n_quarter=None, sibling=False):
            half = src.shape[0] // 2
            base = (1 - c if sibling else c) * half
            return pl.ds(base, half) if n_quarter is None else pl.ds(base + n_quarter * (half // 2), half // 2)

        def sem(a, n):
            return send.at[8 * a + n], recv.at[8 * a + n]

        def go(cp):
            cp.start()
            started.append(cp)

        for a, (src, dst) in enumerate(pairs):
            for n, axis in enumerate("xy"):
                px, py, _ = nbr[axis]
                go(_remote(src.at[rows(src)], dst.at[j, rows(src)], *sem(a, n), (px, py, c)))
        for n, axis in enumerate("xy"):
            ox, oy, _ = nbr["y" if axis == "x" else "x"]
            pj = nbr[axis][2]
            for a, (src, dst) in enumerate(pairs):
                _remote(src.at[rows(src)], dst.at[pj, rows(src)], *sem(a, n), me).wait_recv()
                go(_remote(dst.at[pj, rows(src, n)], dst.at[pj, rows(src, n)], *sem(a, 2 + n), (ox, oy, c)))
                go(_remote(dst.at[pj, rows(src)], dst.at[pj, rows(src)], *sem(a, 4 + n), sib))
                arrivals.append(_remote(src.at[rows(src)], dst.at[pj, rows(src, None, True)], *sem(a, 4 + n), me))
        for n in range(2):
            for a, (src, dst) in enumerate(pairs):
                part = rows(src, n)
                _remote(dst.at[diag, part], dst.at[diag, part], *sem(a, 2 + n), me).wait_recv()
                go(_remote(dst.at[diag, part], dst.at[diag, part], *sem(a, 6 + n), sib))
                sib_part = rows(src, n, True)
                arrivals.append(_remote(dst.at[diag, sib_part], dst.at[diag, sib_part], *sem(a, 6 + n), me))
        for cp in arrivals:
            cp.wait_recv()
        for k in (1, 2, 3):
            pj = _chip_of(x, y, k)[2]
            _remote(cw_ref, gcw_ref.at[pj], csend.at[k - 1], crecv.at[k - 1], me).wait_recv()
        for cp in own:
            cp.wait_recv()
        for cp in started + conv + own:
            cp.wait_send()

    return pl.pallas_call(
        body, name="gather_weights",
        in_specs=[ANY, ANY, ANY], out_specs=[ANY, ANY, ANY],
        out_shape=[SDS((4,) + w_in_b.shape, BF16), SDS((4,) + w_out_b.shape, BF16), SDS((4,) + conv_w.shape, F32)],
        scratch_shapes=[pltpu.SemaphoreType.DMA((16,)), pltpu.SemaphoreType.DMA((16,)),
                        pltpu.SemaphoreType.DMA((3,)), pltpu.SemaphoreType.DMA((3,)),
                        pltpu.SemaphoreType.DMA((3,)), pltpu.SemaphoreType.DMA((3,))],
        compiler_params=pltpu.CompilerParams(has_side_effects=True),
    )(w_in_b, w_out_b, conv_w)


def _pair_exchange(arrays, name):
    halves = [a.shape[1] // 2 for a in arrays]
    n = len(arrays)

    def body(*refs):
        x, y, c, _ = _place()
        send, recv = refs[2 * n:]
        cps = [_remote(refs[k].at[:, pl.ds((1 - c) * halves[k], halves[k])], refs[n + k], send.at[k], recv.at[k],
                       (x, y, 1 - c)) for k in range(n)]
        for cp in cps:
            cp.start()
        for cp in cps:
            cp.wait()

    return pl.pallas_call(
        body, name=name, in_specs=[ANY] * n, out_specs=[ANY] * n,
        out_shape=[SDS((a.shape[0], h, a.shape[2]), F32) for a, h in zip(arrays, halves)],
        scratch_shapes=[pltpu.SemaphoreType.DMA((n,)), pltpu.SemaphoreType.DMA((n,))],
        compiler_params=pltpu.CompilerParams(has_side_effects=True),
    )(*arrays)


def _pair_sum(cidx, g, r, name):
    n, half, width = r.shape
    tr = min(half, 256)
    nt = half // tr

    def body(c_ref, g_ref, r_ref, o_ref):
        del c_ref
        o_ref[...] = (g_ref[...] + r_ref[...]).astype(BF16)

    return pl.pallas_call(
        body, name=name,
        grid_spec=pltpu.PrefetchScalarGridSpec(
            num_scalar_prefetch=1, grid=(n, nt),
            in_specs=[pl.BlockSpec((None, tr, width), lambda s, t, c: (s, c[0] * nt + t, 0)),
                      pl.BlockSpec((None, tr, width), lambda s, t, c: (s, t, 0))],
            out_specs=pl.BlockSpec((None, tr, width), lambda s, t, c: (s, t, 0))),
        out_shape=SDS(r.shape, BF16),
        compiler_params=_cp(("parallel", "parallel")),
    )(cidx, g, r)


class _ChipExchange:
    def __init__(self, arrays):
        self.arrays = list(arrays)
        self.out_shape = [SDS(a.shape, BF16) for a in self.arrays]
        self.scratch = [pltpu.SemaphoreType.DMA((3 * len(self.arrays),)) for _ in range(2)]

    def _copies(self, ins, outs, sems):
        x, y, c, j = _place()
        send, recv = sems
        for a, (src, dst) in enumerate(zip(ins, outs)):
            for k in (1, 2, 3):
                px, py, pj = _chip_of(x, y, k)
                n = 3 * a + k - 1
                yield (_remote(src.at[pj], dst.at[j], send.at[n], recv.at[n], (px, py, c)),
                       _remote(src.at[pj], dst.at[pj], send.at[n], recv.at[n], (x, y, c)))

    def start(self, ins, outs, sems):
        for send, _ in self._copies(ins, outs, sems):
            send.start()

    def finish(self, ins, outs, sems):
        for send, arrival in self._copies(ins, outs, sems):
            arrival.wait_recv()
            send.wait_send()


def _small_exchange(small):
    def body(sm_ref, rs_ref, send, recv, lsem):
        x, y, c, j = _place()
        me = 2 * j + c
        local = pltpu.make_async_copy(sm_ref, rs_ref.at[me], lsem)
        local.start()
        cps = []
        for k in range(1, 8):
            px, py, _ = _chip_of(x, y, k >> 1)
            pc = 1 - c if k & 1 else c
            cps.append(_remote(sm_ref, rs_ref.at[me], send.at[k - 1], recv.at[k - 1], (px, py, pc)))
        for cp in cps:
            cp.start()
        for k in range(1, 8):
            _, _, pj = _chip_of(x, y, k >> 1)
            pc = 1 - c if k & 1 else c
            _remote(sm_ref, rs_ref.at[2 * pj + pc], send.at[k - 1], recv.at[k - 1], (x, y, c)).wait_recv()
        for cp in cps:
            cp.wait_send()
        local.wait()

    return pl.pallas_call(
        body, name="small_exchange", in_specs=[ANY], out_specs=ANY,
        out_shape=SDS((8,) + small.shape, F32),
        scratch_shapes=[pltpu.SemaphoreType.DMA((7,)), pltpu.SemaphoreType.DMA((7,)), pltpu.SemaphoreType.DMA],
        compiler_params=pltpu.CompilerParams(has_side_effects=True),
    )(small)


def _slot_sum(r, name):
    n, rows, width = r.shape
    tr = min(rows, 256)

    def body(r_ref, o_ref):
        acc = r_ref[0].astype(F32)
        for s in range(1, n):
            acc = acc + r_ref[s].astype(F32)
        o_ref[...] = acc

    return pl.pallas_call(
        body, name=name, grid=(rows // tr,),
        in_specs=[pl.BlockSpec((n, tr, width), lambda t: (0, t, 0))],
        out_specs=pl.BlockSpec((tr, width), lambda t: (t, 0)),
        out_shape=SDS((rows, width), F32),
        compiler_params=_cp(("parallel",)),
    )(r)


def _chip_sum(chip_idx, recv, own, name):
    n, rows, width = recv.shape
    tr = min(rows, 256)

    def body(j_ref, r_ref, own_ref, o_ref):
        acc = None
        for s in range(n):
            term = jnp.where(j_ref[0] == s, own_ref[...], r_ref[s]).astype(F32)
            acc = term if acc is None else acc + term
        o_ref[...] = acc

    return pl.pallas_call(
        body, name=name,
        grid_spec=pltpu.PrefetchScalarGridSpec(
            num_scalar_prefetch=1, grid=(rows // tr,),
            in_specs=[pl.BlockSpec((n, tr, width), lambda t, j: (0, t, 0)),
                      pl.BlockSpec((None, tr, width), lambda t, j: (j[0], t, 0))],
            out_specs=pl.BlockSpec((tr, width), lambda t, j: (t, 0))),
        out_shape=SDS((rows, width), F32),
        compiler_params=_cp(("parallel",)),
    )(chip_idx, recv, own)


def _half_exchange(hw, ho):
    def body(hw_ref, ho_ref, tw_ref, to_ref, send, recv):
        x, y, c, _ = _place()
        sib = (x, y, 1 - c)
        cps = [_remote(hw_ref, tw_ref, send.at[0], recv.at[0], sib),
               _remote(ho_ref, to_ref, send.at[1], recv.at[1], sib)]
        for cp in cps:
            cp.start()
        for cp in cps:
            cp.wait()

    return pl.pallas_call(
        body, name="half_exchange", in_specs=[ANY, ANY], out_specs=[ANY, ANY],
        out_shape=[SDS(hw.shape, F32), SDS(ho.shape, F32)],
        scratch_shapes=[pltpu.SemaphoreType.DMA((2,)), pltpu.SemaphoreType.DMA((2,))],
        compiler_params=pltpu.CompilerParams(has_side_effects=True),
    )(hw, ho)


def _by_core(c, mine, theirs):
    return jnp.where(c == 0, jnp.concatenate([mine, theirs], axis=0), jnp.concatenate([theirs, mine], axis=0))


def _adamw(w, g, m, v, name):
    rows, width = w.shape
    tr = min(rows, 256)

    def body(w_ref, g_ref, m_ref, v_ref, d_ref, nm_ref, nv_ref):
        gv = g_ref[...]
        nm = ADAM_B1 * m_ref[...] + (1.0 - ADAM_B1) * gv
        nv = ADAM_B2 * v_ref[...] + (1.0 - ADAM_B2) * (gv * gv)
        m_hat = nm / (1.0 - ADAM_B1 ** ADAM_STEP)
        v_hat = nv / (1.0 - ADAM_B2 ** ADAM_STEP)
        d_ref[...] = -ADAM_LR * (m_hat / (jnp.sqrt(v_hat) + ADAM_EPS) + ADAM_WD * w_ref[...])
        nm_ref[...] = nm
        nv_ref[...] = nv

    t = pl.BlockSpec((tr, width), lambda i: (i, 0))
    return pl.pallas_call(
        body, name=name, grid=(rows // tr,), in_specs=[t] * 4, out_specs=[t] * 3,
        out_shape=[SDS(w.shape, F32)] * 3, compiler_params=_cp(("parallel",)),
    )(w, g, m, v)


def _rows128(a, rows):
    flat = a.reshape(-1)
    return jnp.pad(flat, (0, rows * LANE - flat.shape[0])).reshape(rows, LANE)


def _pack_small(conv_w, norm_pre, conv_b, ssm_norm, norm_post, dtb, alog, dsk, extra=None):
    cw_rows = 48 if conv_w.shape[-1] == 1536 else 16
    extra = jnp.zeros((1, LANE), F32) if extra is None else _rows128(extra, 1)
    vec = jnp.concatenate([_rows128(dtb, 1), _rows128(alog, 1), _rows128(dsk, 1), extra, jnp.zeros((4, LANE), F32)],
                          axis=0)
    return jnp.concatenate([_rows128(conv_w, cw_rows), _rows128(norm_pre, 8), _rows128(conv_b, 16),
                            _rows128(ssm_norm, 8), _rows128(norm_post, 8), vec], axis=0)


def _unpack_small(p, cw_cols):
    cw_rows = 48 if cw_cols == 1536 else 16
    o = cw_rows
    conv_w = p[:cw_rows].reshape(-1)[:4 * cw_cols].reshape(1, 4, cw_cols)
    norm_pre = p[o:o + 8].reshape(1, D)
    conv_b = p[o + 8:o + 24].reshape(-1)[:1536].reshape(1, 1536)
    ssm_norm = p[o + 24:o + 32].reshape(1, D)
    norm_post = p[o + 32:o + 40].reshape(1, D)
    vec = p[o + 40:o + 48]
    return conv_w, norm_pre, conv_b, ssm_norm, norm_post, vec[0:1, :NH], vec[1:2, :NH], vec[2:3, :NH], vec[3, 0]


def _pad_lanes(a):
    return jnp.pad(a, ((0, 0), (0, LANE - a.shape[1])))


class _GradReduce:
    def __init__(self, chip, ci):
        self.ci = ci
        self.cidx = jnp.reshape(ci, (1,)).astype(jnp.int32)
        self.chip_idx = jnp.reshape(chip, (1,)).astype(jnp.int32)

    def start(self, dw_all, dw_out):
        gw = jnp.stack([dw_all[:, k * SHARD:(k + 1) * SHARD] for k in range(4)])
        go = dw_out.reshape(4, D // 2, D)
        rw, ro = _pair_exchange([gw, go], "pair_exchange")
        self.own = [_pair_sum(self.cidx, gw, rw, "pair_sum_in"), _pair_sum(self.cidx, go, ro, "pair_sum_out")]
        return _ChipExchange(self.own)

    def done(self, got):
        self.got = got

    def result(self):
        half_in = _chip_sum(self.chip_idx, self.got[0], self.own[0], "chip_sum_in")
        half_out = _chip_sum(self.chip_idx, self.got[1], self.own[1], "chip_sum_out")
        their_in, their_out = _half_exchange(half_in, half_out)
        return _by_core(self.ci, half_in, their_in), _by_core(self.ci, half_out, their_out)


def kernel(x, norm_pre_w, w_in, conv_w, conv_b, dt_bias, a_log, d_skip, ssm_norm_w, w_out, norm_post_w, loss_target, m_norm_pre_w, m_w_in, m_conv_w, m_conv_b, m_dt_bias, m_a_log, m_d_skip, m_ssm_norm_w, m_w_out, m_norm_post_w, v_norm_pre_w, v_w_in, v_conv_w, v_conv_b, v_dt_bias, v_a_log, v_d_skip, v_ssm_norm_w, v_w_out, v_norm_post_w):
    xi, yi, ci = lax.axis_index("x"), lax.axis_index("y"), lax.axis_index("c")
    chip = 2 * xi + yi
    x2, tgt = x[0], loss_target[0]

    w_in_b, w_out_b = w_in[0].astype(BF16), w_out[0].astype(BF16)
    gin, gout, gcw = _gather_weights(w_in_b, w_out_b, conv_w[0])

    w_all = jnp.concatenate([gin[0], gin[1], gin[2], gin[3], jnp.zeros((D, DP - 4 * SHARD), BF16)], axis=1)
    w_out_all = gout.reshape(2 * D, D)
    cw_all = jnp.concatenate([gcw[0], gcw[1], gcw[2], gcw[3]], axis=1)
    reduce = _GradReduce(chip, ci)
    grad_x, small = _local_step(x2, tgt, w_all, w_out_all, cw_all, norm_pre_w, conv_b, dt_bias, a_log, d_skip,
                                ssm_norm_w, norm_post_w, reduce)[:2]
    g_in, g_out = reduce.result()
    g_small = _slot_sum(_small_exchange(small), "small_sum")
    g_cw, g_npre, g_cb, g_nssm, g_npost, g_dtb, g_alog, g_dsk, loss = _unpack_small(g_small, 1536)
    g_cw = lax.dynamic_slice_in_dim(g_cw, chip * 384, 384, axis=2)

    d_in, nm_in, nv_in = _adamw(w_in[0], g_in, m_w_in[0], v_w_in[0], "adamw_in")
    d_out, nm_out, nv_out = _adamw(w_out[0], g_out, m_w_out[0], v_w_out[0], "adamw_out")
    packed = [_pack_small(*t) for t in (
        (conv_w, norm_pre_w, conv_b, ssm_norm_w, norm_post_w, dt_bias, a_log, d_skip),
        (g_cw, g_npre, g_cb, g_nssm, g_npost, g_dtb, g_alog, g_dsk),
        (m_conv_w, m_norm_pre_w, m_conv_b, m_ssm_norm_w, m_norm_post_w, m_dt_bias, m_a_log, m_d_skip),
        (v_conv_w, v_norm_pre_w, v_conv_b, v_ssm_norm_w, v_norm_post_w, v_dt_bias, v_a_log, v_d_skip))]
    small_out = [_unpack_small(p, 384)[:8] for p in _adamw(*packed, "adamw_small")]

    def ordered(cw_, npre, cb_, nssm, npost, dtb_, alog_, dsk_, big_in, big_out):
        return [npre, big_in[None], cw_, cb_, dtb_, alog_, dsk_, nssm, big_out[None], npost]

    grads = ordered(g_cw, g_npre, g_cb, g_nssm, g_npost, g_dtb, g_alog, g_dsk, g_in, g_out)
    deltas = ordered(*small_out[0], d_in, d_out)
    new_m = ordered(*small_out[1], nm_in, nm_out)
    new_v = ordered(*small_out[2], nv_in, nv_out)
    return (loss, grad_x[None], *grads, *deltas, *new_m, *new_v)


def _local_step(x2, tgt, w_all, w_out_all, cw_all, norm_pre_w, conv_b, dt_bias, a_log, d_skip, ssm_norm_w,
                norm_post_w, reduce=None):
    dtb, alog = _pad_lanes(dt_bias), _pad_lanes(a_log)
    d_b = jnp.repeat(d_skip, 64, axis=1)

    proj, u = _inproj_fwd(x2, norm_pre_w, w_all)
    mix, attn_pre, lse = _attn_fwd(proj, 1, _attn_fwd(proj, 4, _attn_fwd(proj, 16)), final=True)
    mix, y_save, states = _ssm_fwd(proj, mix, cw_all, conv_b, dtb, alog, d_b, ssm_norm_w)

    dy, dmix, dw_out, dnw_post, loss_part = _outproj_loss(mix, w_out_all, x2, tgt, norm_post_w)
    do, delta, dg = _attn_gate_bwd(dmix, attn_pre, proj)
    dz, dxbcdt, dcw, dcb, dvec, dnw_ssm = _ssm_bwd(proj, dmix, y_save, states, cw_all, conv_b, dtb, alog, d_b,
                                                   ssm_norm_w)
    acc = _attn_bwd(proj, do, lse, delta, 16, None, F32)
    acc = _attn_bwd(proj, do, lse, delta, 4, acc, F32)
    dq, dk, dv = _attn_bwd(proj, do, lse, delta, 1, acc, BF16)
    srcs = [dq, dk, dv, dg, dz]
    dws = [_dw(u, s, f"dw_in_{n}") for s, n in zip(srcs + [dxbcdt], ("q", "k", "v", "g", "z", "xbcdt"))]
    dw_all = jnp.concatenate(dws, axis=1)
    res = _inproj_bwd_dx(srcs, dxbcdt, w_all, x2, dy, norm_pre_w, reduce.start(dw_all, dw_out) if reduce else None)
    if reduce:
        res, got = res
        reduce.done(got)
    grad_x, dnw_pre = res
    small = _pack_small(dcw, dnw_pre, dcb, dnw_ssm, dnw_post, dvec[0:1, :NH], dvec[1:2, :NH], dvec[2:3, :NH],
                        loss_part[:, :1])
    return grad_x, small, dw_all, dw_out
```

```python
import functools

import jax
import jax.numpy as jnp
from jax import lax
from jax.experimental import pallas as pl
from jax.experimental.pallas import tpu as pltpu

F32 = jnp.float32
BF16 = jnp.bfloat16
MESH = pl.DeviceIdType.MESH
SDS = jax.ShapeDtypeStruct
ANY = pl.BlockSpec(memory_space=pl.ANY)

S = 4096
D = 1024
DP = 7168
SHARD = 1668
OFF_G, OFF_Z = 3072, 4096
NH = 16
CH = 128
NC = S // CH
EPS = 1e-6
NEG = -1e30
LANE = 128
VMEM_LIMIT = 48 * 1024 * 1024

ADAM_LR, ADAM_B1, ADAM_B2, ADAM_EPS, ADAM_WD, ADAM_STEP = 0.001, 0.9, 0.999, 1e-08, 0.01, 10


def _cp(sem, **kw):
    return pltpu.CompilerParams(dimension_semantics=sem, vmem_limit_bytes=VMEM_LIMIT, **kw)


def _dot(a, b):
    return jnp.dot(a, b, preferred_element_type=F32)


def _dot_nt(a, b):
    return lax.dot_general(a, b, (((1,), (1,)), ((), ())), preferred_element_type=F32)


def _dot_tn(a, b):
    return lax.dot_general(a, b, (((0,), (0,)), ((), ())), preferred_element_type=F32)


def _pieces(x, n):
    out = []
    for _ in range(n):
        p = x.astype(BF16)
        out.append(p)
        x = x - p.astype(F32)
    return out


def _pick(x, sel, n=2):
    parts = [_dot(p, sel) for p in _pieces(x, n)]
    return functools.reduce(jnp.add, parts)


def _pick_left(sel, x, n=3):
    parts = [_dot(sel, p) for p in _pieces(x, n)]
    return functools.reduce(jnp.add, parts)


def _sigmoid(v):
    return 0.5 * jnp.tanh(0.5 * v) + 0.5


def _iota(shape, dim):
    return lax.broadcasted_iota(jnp.int32, shape, dim)


def _inproj_fwd(x, nw, w_all):
    tm, tn = 1024, 1024

    def body(x_ref, nw_ref, w_ref, proj_ref, u_ref):
        @pl.when(pl.program_id(1) == 0)
        def _():
            xf = x_ref[...]
            r = lax.rsqrt(jnp.mean(xf * xf, axis=-1, keepdims=True) + EPS)
            u_ref[...] = (xf * r * nw_ref[...]).astype(BF16)

        proj_ref[...] = _dot(u_ref[...], w_ref[...])

    return pl.pallas_call(
        body, name="inproj_fwd", grid=(S // tm, DP // tn),
        in_specs=[pl.BlockSpec((tm, D), lambda i, j: (i, 0)), pl.BlockSpec((1, D), lambda i, j: (0, 0)),
                  pl.BlockSpec((D, tn), lambda i, j: (0, j))],
        out_specs=[pl.BlockSpec((tm, tn), lambda i, j: (i, j)), pl.BlockSpec((tm, D), lambda i, j: (i, 0))],
        out_shape=[SDS((S, DP), F32), SDS((S, D), BF16)],
        compiler_params=_cp(("parallel", "arbitrary")),
    )(x, nw, w_all)


ATTN_QB = {1: 16, 4: 4, 16: 1}


def _unit_rows(r, u, d):
    return pl.ds(r + d * CH * u, CH, stride=d) if d > 1 else pl.ds(CH * u, CH)


def _for_units(d, qb, fn):
    for r in range(d):
        for u in range(qb):
            fn(r, u)


def _attn_mask(has_prev):
    qi, kj = _iota((2 * CH, 2 * CH), 0) & (CH - 1), _iota((2 * CH, 2 * CH), 1)
    cur_ok = (kj >= CH) & (kj - CH <= qi)
    prev_ok = (kj < CH) & (kj >= qi)
    return cur_ok | (prev_ok & has_prev)


def _stack_heads(v, lane_a):
    return jnp.concatenate([jnp.where(lane_a, v, 0.0), jnp.where(lane_a, 0.0, v)], axis=0).astype(BF16)


def _attn_specs(d, qb):
    rows, prows = CH * d * qb, CH * d
    nb = S // rows
    steps = (NH // 2) * nb

    def at(t):
        t = jnp.minimum(t, steps - 1)
        return t % nb, t // nb

    def cur(off):
        return pl.BlockSpec((rows, LANE), lambda t: (at(t)[0], off + at(t)[1]))

    def prev(off):
        return pl.BlockSpec((prows, LANE), lambda t: (jnp.maximum(at(t)[0] * qb - 1, 0), off + at(t)[1]))

    lag = pl.BlockSpec((rows, LANE), lambda t: at(jnp.maximum(t - 1, 0)))
    return nb, steps, cur, prev, lag


def _gather16(src_ref, dense_ref, tmp_ref):
    for a in range(4):
        tmp_ref[...] = src_ref[pl.ds(a, 4 * CH, stride=4), :]
        for b in range(4):
            dense_ref[a + 4 * b] = tmp_ref[pl.ds(b, CH, stride=4), :]


def _scatter16(dense_ref, dst_ref, tmp_ref):
    for a in range(4):
        for b in range(4):
            tmp_ref[pl.ds(b, CH, stride=4), :] = dense_ref[a + 4 * b]
        dst_ref[pl.ds(a, 4 * CH, stride=4), :] = tmp_ref[...]


def _unit_index(r, u, d):
    return (r,) if d == 16 else (_unit_rows(r, u, d), slice(None))


def _unit_kv(p_ref, c_ref, r, u, d):
    prev = p_ref[_unit_index(r, 0, d)] if u == 0 else c_ref[_unit_index(r, u - 1, d)]
    return jnp.concatenate([prev, c_ref[_unit_index(r, u, d)]], axis=0).astype(BF16)


def _dense_scratch(d, n):
    return [pltpu.VMEM((16, CH, LANE), F32)] * n + [pltpu.VMEM((4 * CH, LANE), F32)] if d == 16 else []


def _attn_fwd(proj, d, prior=None, final=False):
    qb = ATTN_QB[d]
    nb, steps, cur, prev, _ = _attn_specs(d, qb)
    n_prior = 2 if prior is not None else 0
    n_in, n_out = 5 + n_prior + final, 2 + final
    assert not (d == 16 and (n_prior or final))

    def body(*refs):
        ins, outs, scratch = refs[:n_in], refs[n_in:n_in + n_out], refs[n_in + n_out:]
        if d == 16:
            tmp_ref = scratch[-1]
            for src, dense in zip(ins, scratch):
                _gather16(src, dense, tmp_ref)
            block_outs, ins, outs = outs, scratch[:n_in], scratch[n_in:n_in + n_out]
        q_ref, kp_ref, kc_ref, vp_ref, vc_ref = ins[:5]
        prior_refs = ins[5:5 + n_prior]
        if final:
            g_ref, (mix_ref, o_ref, l_ref) = ins[-1], outs
        else:
            o_ref, l_ref = outs
        i = pl.program_id(0) % nb
        lane_a = _iota((CH, LANE), 1) < 64
        mask_first, mask_rest = _attn_mask(i > 0), _attn_mask(True)

        def unit(r, u):
            at = _unit_index(r, u, d)
            q2 = _stack_heads(q_ref[at] * 0.125, lane_a)
            k2, v2 = _unit_kv(kp_ref, kc_ref, r, u, d), _unit_kv(vp_ref, vc_ref, r, u, d)
            s = jnp.where(mask_first if u == 0 else mask_rest, _dot_nt(q2, k2), NEG)
            m = jnp.max(s, axis=1, keepdims=True)
            p = jnp.exp(s - m)
            l = jnp.sum(p, axis=1, keepdims=True)
            o2 = _dot(p.astype(BF16), v2) / l
            lse2 = m + jnp.log(l)
            o = jnp.where(lane_a, o2[:CH], o2[CH:])
            lse = jnp.where(lane_a, lse2[:CH], lse2[CH:])
            if n_prior:
                o_a, l_a = prior_refs[0][at], prior_refs[1][at]
                top = jnp.maximum(l_a, lse)
                e_a, e_b = jnp.exp(l_a - top), jnp.exp(lse - top)
                tot = e_a + e_b
                o = (e_a * o_a + e_b * o) / tot
                lse = top + jnp.log(tot)
            o_ref[at] = o
            l_ref[at] = lse
            if final:
                g = g_ref[at]
                mix_ref[at] = (o * (g * _sigmoid(g))).astype(BF16)

        _for_units(d, qb, unit)
        if d == 16:
            for dense, dst in zip(outs, block_outs):
                _scatter16(dense, dst, tmp_ref)

    in_specs = [cur(0), prev(8), cur(8), prev(16), cur(16)] + [cur(0)] * n_prior
    args = [proj] * 5 + (list(prior) if n_prior else [])
    out_specs, out_shape = [cur(0), cur(0)], [SDS((S, D), F32), SDS((S, D), F32)]
    if final:
        assert d == 1
        in_specs.append(cur(OFF_G // LANE))
        args.append(proj)
        out_specs, out_shape = [cur(0)] + out_specs, [SDS((S, 2 * D), BF16)] + out_shape
    return pl.pallas_call(
        body, name=f"attn_fwd_d{d}", grid=(steps,),
        in_specs=in_specs, out_specs=out_specs, out_shape=out_shape,
        scratch_shapes=_dense_scratch(d, n_in + n_out),
        compiler_params=_cp(("parallel",)),
    )(*args)


def _attn_gate_bwd(dmix, pre, proj):
    tm = 512

    def body(dm_ref, pre_ref, g_ref, do_ref, delta_ref, dg_ref):
        g, dm, pre_v = g_ref[...], dm_ref[...], pre_ref[...]
        sig = _sigmoid(g)
        do = dm * (g * sig)
        do_ref[...] = do
        dg_ref[...] = (dm * pre_v * (sig * (1.0 + g * (1.0 - sig)))).astype(BF16)
        prod = do * pre_v
        same_head = (_iota((LANE, LANE), 0) // 64 == _iota((LANE, LANE), 1) // 64).astype(BF16)
        for cb in range(D // LANE):
            delta_ref[:, cb * LANE:(cb + 1) * LANE] = _pick(prod[:, cb * LANE:(cb + 1) * LANE], same_head)

    t = pl.BlockSpec((tm, D), lambda i: (i, 0))
    return pl.pallas_call(
        body, name="attn_gate_bwd", grid=(S // tm,),
        in_specs=[t, t, pl.BlockSpec((tm, D), lambda i: (i, OFF_G // D))],
        out_specs=[t, t, t],
        out_shape=[SDS((S, D), F32), SDS((S, D), F32), SDS((S, D), BF16)],
        compiler_params=_cp(("parallel",)),
    )(dmix, pre, proj)


def _attn_bwd(proj, do, lse, delta, d, acc, out_dtype, hosted=None):
    qb = ATTN_QB[d]
    nb, steps, cur, prev, lag = _attn_specs(d, qb)
    has_acc = acc is not None
    n_in = 11 if has_acc else 8
    n_host = len(hosted.arrays) if hosted else 0
    assert not (d == 16 and (has_acc or out_dtype != F32))
    rows = CH * d * qb
    carry = (2, 16, CH, LANE) if d == 16 else (2, rows, LANE)

    def body(*refs):
        ins, host_in, refs = refs[:n_in], refs[n_in:n_in + n_host], refs[n_in + n_host:]
        (dq_ref, dk_ref, dv_ref), host_out, scratch = refs[:3], refs[3:3 + n_host], refs[3 + n_host:]
        if hosted:
            scratch, host_sems = scratch[:-len(hosted.scratch)], scratch[-len(hosted.scratch):]
        ck_ref, cv_ref = scratch[:2]
        dq_f32 = dq_ref if out_dtype == F32 else scratch[2]
        t = pl.program_id(0)
        i = t % nb
        if hosted:
            pl.when(t == 0)(lambda: hosted.start(host_in, host_out, host_sems))
        if d == 16:
            dense, dq_f32, tmp_ref = scratch[2:2 + n_in], scratch[2 + n_in], scratch[-1]

            @pl.when(t < steps)
            def _():
                for src, dst in zip(ins, dense):
                    _gather16(src, dst, tmp_ref)

            ins = dense
        q_ref, kp_ref, kc_ref, vp_ref, vc_ref, do_ref, lse_ref, dl_ref = ins[:8]
        if has_acc:
            aq_ref, ak_ref, av_ref = ins[8:11]
        slot = t & 1
        now_k, now_v, old_k, old_v = ck_ref.at[slot], cv_ref.at[slot], ck_ref.at[1 - slot], cv_ref.at[1 - slot]
        lane_a = _iota((CH, LANE), 1) < 64
        mask_first, mask_rest = _attn_mask(i > 0), _attn_mask(True)

        @pl.when(t == 0)
        def _():
            ck_ref[1] = jnp.zeros(carry[1:], F32)
            cv_ref[1] = jnp.zeros(carry[1:], F32)

        def unit(r, u):
            at = _unit_index(r, u, d)
            q2 = _stack_heads(q_ref[at] * 0.125, lane_a)
            do2 = _stack_heads(do_ref[at], lane_a)
            k2, v2 = _unit_kv(kp_ref, kc_ref, r, u, d), _unit_kv(vp_ref, vc_ref, r, u, d)
            lsev, dlv = lse_ref[at], dl_ref[at]
            lse2 = jnp.concatenate([lsev[:, 0:1], lsev[:, 64:65]], axis=0)
            dl2 = jnp.concatenate([dlv[:, 0:1], dlv[:, 64:65]], axis=0)
            p = jnp.exp(jnp.where(mask_first if u == 0 else mask_rest, _dot_nt(q2, k2), NEG) - lse2)
            ds = (p * (_dot_nt(do2, v2) - dl2)).astype(BF16)
            dq2 = _dot(ds, k2)
            dk2 = _dot_tn(ds, q2)
            dv2 = _dot_tn(p.astype(BF16), do2)
            dq = jnp.where(lane_a, dq2[:CH], dq2[CH:]) * 0.125
            if has_acc:
                dq = dq + aq_ref[at]
            dq_f32[at] = dq
            if u == 0:
                before = _unit_index(r, qb - 1, d)
                old_k[before] += dk2[:CH]
                old_v[before] += dv2[:CH]
            else:
                before = _unit_index(r, u - 1, d)
                now_k[before] += dk2[:CH]
                now_v[before] += dv2[:CH]
            now_k[at] = dk2[CH:]
            now_v[at] = dv2[CH:]

        @pl.when(t < steps)
        def _():
            _for_units(d, qb, unit)
            if d == 16:
                _scatter16(dq_f32, dq_ref, tmp_ref)
            elif out_dtype != F32:
                dq_ref[...] = dq_f32[...].astype(out_dtype)

        if d == 16:
            _scatter16(old_k, dk_ref, tmp_ref)
            _scatter16(old_v, dv_ref, tmp_ref)
        else:
            dk, dv = old_k[...], old_v[...]
            if has_acc:
                dk, dv = dk + ak_ref[...], dv + av_ref[...]
            dk_ref[...] = dk.astype(out_dtype)
            dv_ref[...] = dv.astype(out_dtype)
        if hosted:
            pl.when(t == steps)(lambda: hosted.finish(host_in, host_out, host_sems))

    in_specs = [cur(0), prev(8), cur(8), prev(16), cur(16), cur(0), cur(0), cur(0)]
    args = [proj, proj, proj, proj, proj, do, lse, delta]
    if has_acc:
        in_specs += [cur(0), lag, lag]
        args += list(acc)
    scratch = [pltpu.VMEM(carry, F32), pltpu.VMEM(carry, F32)]
    if d == 16:
        scratch += _dense_scratch(d, n_in + 1)
    elif out_dtype != F32:
        scratch.append(pltpu.VMEM((rows, LANE), F32))
    out_specs, out_shape = [cur(0), lag, lag], [SDS((S, D), out_dtype)] * 3
    if hosted:
        args += hosted.arrays
        in_specs += [ANY] * n_host
        out_specs += [ANY] * n_host
        out_shape += hosted.out_shape
        scratch += hosted.scratch
    outs = pl.pallas_call(
        body, name=f"attn_bwd_d{d}", grid=(steps + 1,),
        in_specs=in_specs, out_specs=out_specs, out_shape=out_shape,
        scratch_shapes=scratch, compiler_params=_cp(("arbitrary",)),
    )(*args)
    return (outs[:3], outs[3:]) if hosted else outs


def _conv_taps(cur, prev8, first):
    row8 = _iota(prev8.shape, 0)
    prev8 = jnp.where(first, 0.0, prev8)
    taps = []
    for s in (3, 2, 1):
        rolled = pltpu.roll(cur, s, 0)
        head = jnp.where(row8 < s, pltpu.roll(prev8, s, 0), rolled[:8])
        taps.append(jnp.concatenate([head, rolled[8:]], axis=0))
    return taps + [cur]


def _conv(taps, w, b):
    acc = b + w[0:1, :] * taps[0]
    for k in (1, 2, 3):
        acc = acc + w[k:k + 1, :] * taps[k]
    return acc


def _expand():
    return (_iota((LANE, D), 1) // 64 == _iota((LANE, D), 0)).astype(BF16)


def _reduce():
    return (_iota((D, LANE), 0) // 64 == _iota((D, LANE), 1)).astype(BF16)


def _ssd_common(xs_raw, xs_prev, bc_raw, bc_prev, dt_raw, first, cw, cb, dtb, alog):
    head_lane = _iota((CH, LANE), 1) < NH
    xs_taps = _conv_taps(xs_raw, xs_prev, first)
    bc_taps = _conv_taps(bc_raw, bc_prev, first)
    xs_c = _conv(xs_taps, cw[:, :D], cb[:, :D])
    bc_c = _conv(bc_taps, cw[:, D:], cb[:, D:])
    xs = xs_c * _sigmoid(xs_c)
    bc = bc_c * _sigmoid(bc_c)
    pre = dt_raw + dtb
    dt = jnp.where(head_lane, jnp.maximum(pre, 0.0) + jnp.log(1.0 + jnp.exp(-jnp.abs(pre))), 0.0)
    a_row = jnp.where(head_lane[0:1], -jnp.exp(alog), 0.0)
    tri = (_iota((CH, CH), 1) <= _iota((CH, CH), 0)).astype(BF16)
    cs = _pick_left(tri, dt * a_row)
    cs_last = cs[CH - 1:CH, :]
    wide = _pick(jnp.concatenate([dt, jnp.exp(cs), jnp.exp(cs_last - cs)], axis=0), _expand())
    dt_b, e_b, f_b = wide[:CH], wide[CH:2 * CH], wide[2 * CH:]
    return dict(xs_taps=xs_taps, bc_taps=bc_taps, xs_c=xs_c, bc_c=bc_c, xs=xs, bc=bc, pre=pre, dt=dt,
                a_row=a_row, cs=cs, cs_t=cs.T, dt_b=dt_b, e_b=e_b, f_b=f_b, t_b=e_b[CH - 1:CH, :])


def _groups(bc):
    bcb = bc.astype(BF16)
    return [bcb[:, 0:128], bcb[:, 128:256]], [bcb[:, 256:384], bcb[:, 384:512]]


def _decay(q, h, tril):
    seg = q["cs"][:, h:h + 1] - q["cs_t"][h:h + 1, :]
    return jnp.exp(jnp.where(tril, seg, NEG))


def _ssm_fwd(proj, mix, cw, cb, dtb, alog, d_b, nw):
    def body(xs_ref, xsp_ref, bc_ref, bcp_ref, dt_ref, z_ref, cw_ref, cb_ref, dtb_ref, alog_ref, db_ref, nw_ref,
             mix_in_ref, mix_ref, y_ref, st_ref, h_ref):
        del mix_in_ref
        i = pl.program_id(0)

        @pl.when(i == 0)
        def _():
            h_ref[...] = jnp.zeros_like(h_ref)

        q = _ssd_common(xs_ref[...], xsp_ref[...], bc_ref[...], bcp_ref[...], dt_ref[...], i == 0,
                        cw_ref[...], cb_ref[...], dtb_ref[...], alog_ref[...])
        bg, cg = _groups(q["bc"])
        xs = q["xs"]
        xdt = xs * q["dt_b"]
        xdt_b = xdt.astype(BF16)
        h_in = h_ref[...]
        st_ref[...] = h_in
        hb = h_in.astype(BF16)
        tril = _iota((CH, CH), 1) <= _iota((CH, CH), 0)
        lane_a = _iota((CH, LANE), 1) < 64
        cbm = [_dot_nt(cg[g], bg[g]) for g in range(2)]
        pairs = []
        for hp in range(NH // 2):
            xp = xdt_b[:, hp * LANE:(hp + 1) * LANE]
            ya = _dot((cbm[hp // 4] * _decay(q, 2 * hp, tril)).astype(BF16), xp)
            yb = _dot((cbm[hp // 4] * _decay(q, 2 * hp + 1, tril)).astype(BF16), xp)
            pairs.append(jnp.where(lane_a, ya, yb))
        y_diag = jnp.concatenate(pairs, axis=1)
        y_off = jnp.concatenate([_dot(cg[g], hb[:, g * 512:(g + 1) * 512]) for g in range(2)], axis=1) * q["e_b"]
        y = y_diag + y_off + db_ref[...] * xs
        y_ref[...] = y
        xf = (xdt * q["f_b"]).astype(BF16)
        h_ref[...] = q["t_b"] * h_in + jnp.concatenate(
            [_dot_tn(bg[g], xf[:, g * 512:(g + 1) * 512]) for g in range(2)], axis=1)
        z = z_ref[...]
        yz = y * (z * _sigmoid(z))
        outs = []
        for g in range(2):
            v = yz[:, g * 512:(g + 1) * 512]
            outs.append(v * lax.rsqrt(jnp.mean(v * v, axis=-1, keepdims=True) + EPS))
        mix_ref[...] = (jnp.concatenate(outs, axis=1) * nw_ref[...]).astype(BF16)

    def col(width, blk, prev=False):
        if prev:
            return pl.BlockSpec((8, width), lambda i: (jnp.maximum(i * (CH // 8) - 1, 0), blk))
        return pl.BlockSpec((CH, width), lambda i: (i, blk))

    def full(a):
        return pl.BlockSpec(a.shape, lambda i: (0,) * a.ndim)

    return pl.pallas_call(
        body, name="ssm_fwd", grid=(NC,),
        in_specs=[col(D, 5), col(D, 5, True), col(512, 12), col(512, 12, True), col(LANE, 52), col(D, 4),
                  full(cw), full(cb), full(dtb), full(alog), full(d_b), full(nw), ANY],
        out_specs=[col(D, 1), col(D, 0), pl.BlockSpec((None, CH, D), lambda i: (i, 0, 0))],
        out_shape=[SDS((S, 2 * D), BF16), SDS((S, D), F32), SDS((NC, CH, D), F32)],
        scratch_shapes=[pltpu.VMEM((CH, D), F32)],
        input_output_aliases={12: 0},
        compiler_params=_cp(("arbitrary",)),
    )(proj, proj, proj, proj, proj, proj, cw, cb, dtb, alog, d_b, nw, mix)


def _ssm_bwd(proj, dmix, y_save, states, cw, cb, dtb, alog, d_b, nw):
    def body(xs_ref, xsp_ref, bc_ref, bcp_ref, dt_ref, z_ref, dn_ref, y_ref, st_ref,
             cw_ref, cb_ref, dtb_ref, alog_ref, db_ref, nw_ref,
             dz_ref, dx_ref, dcw_ref, dcb_ref, dsm_ref, dnw_ref, dh_ref, nxs_ref, nbc_ref):
        i = pl.program_id(0)
        ci = NC - 1 - i

        @pl.when(i == 0)
        def _():
            for ref in (dcw_ref, dcb_ref, dsm_ref, dnw_ref, dh_ref, nxs_ref, nbc_ref):
                ref[...] = jnp.zeros_like(ref)

        cw, cb = cw_ref[...], cb_ref[...]
        q = _ssd_common(xs_ref[...], xsp_ref[...], bc_ref[...], bcp_ref[...], dt_ref[...], ci == 0,
                        cw, cb, dtb_ref[...], alog_ref[...])
        bg, cg = _groups(q["bc"])
        xs, dt_b, e_b, f_b, t_b = q["xs"], q["dt_b"], q["e_b"], q["f_b"], q["t_b"]
        xdt = xs * dt_b
        xdt_b = xdt.astype(BF16)
        h_in = st_ref[...]
        hb = h_in.astype(BF16)
        dh_new = dh_ref[...]
        dhb = dh_new.astype(BF16)
        red = _reduce()

        z, y, dn, nw_v = z_ref[...], y_ref[...], dn_ref[...], nw_ref[...]
        sig = _sigmoid(z)
        sz = z * sig
        yz = y * sz
        gdn = dn * nw_v
        dyz, dnw = [], []
        for g in range(2):
            v, gv = yz[:, g * 512:(g + 1) * 512], gdn[:, g * 512:(g + 1) * 512]
            r = lax.rsqrt(jnp.mean(v * v, axis=-1, keepdims=True) + EPS)
            dnw.append(dn[:, g * 512:(g + 1) * 512] * v * r)
            dyz.append(r * (gv - v * (r * r) * jnp.mean(gv * v, axis=-1, keepdims=True)))
        dyz = jnp.concatenate(dyz, axis=1)
        dnw_ref[...] += jnp.sum(jnp.concatenate(dnw, axis=1), axis=0, keepdims=True)
        dy = dyz * sz
        dz_ref[...] = (dyz * y * (sig * (1.0 + z * (1.0 - sig)))).astype(BF16)
        dy_b = dy.astype(BF16)

        tril = _iota((CH, CH), 1) <= _iota((CH, CH), 0)
        lane_a = _iota((CH, LANE), 1) < 64
        cbm = [_dot_nt(cg[g], bg[g]) for g in range(2)]
        dcbm = [jnp.zeros((CH, CH), F32), jnp.zeros((CH, CH), F32)]
        seg_rows = jnp.zeros((CH, LANE), F32)
        seg_cols = jnp.zeros((LANE, CH), F32)
        row_id, col_id = _iota((CH, LANE), 0), _iota((CH, LANE), 1)
        dx_pairs = []
        for hp in range(NH // 2):
            g = hp // 4
            xp = xdt_b[:, hp * LANE:(hp + 1) * LANE]
            dyp_f = dy[:, hp * LANE:(hp + 1) * LANE]
            dyp = dy_b[:, hp * LANE:(hp + 1) * LANE]
            halves = []
            for k in range(2):
                h = 2 * hp + k
                lane = lane_a if k == 0 else jnp.logical_not(lane_a)
                dec = _decay(q, h, tril)
                gm = cbm[g] * dec
                dgm = _dot_nt(jnp.where(lane, dyp_f, 0.0).astype(BF16), xp)
                dcbm[g] = dcbm[g] + dgm * dec
                prod = dgm * gm
                seg_rows = jnp.where(col_id == h, jnp.sum(prod, axis=1, keepdims=True), seg_rows)
                seg_cols = jnp.where(row_id == h, jnp.sum(prod, axis=0, keepdims=True), seg_cols)
                halves.append(_dot_tn(gm.astype(BF16), dyp))
            dx_pairs.append(jnp.where(lane_a, halves[0], halves[1]))
        dxdt_diag = jnp.concatenate(dx_pairs, axis=1)

        qv = jnp.concatenate([_dot(bg[g], dhb[:, g * 512:(g + 1) * 512]) for g in range(2)], axis=1)
        y_off = jnp.concatenate([_dot(cg[g], hb[:, g * 512:(g + 1) * 512]) for g in range(2)], axis=1) * e_b
        xfq = xdt * f_b * qv
        dxdt = dxdt_diag + f_b * qv
        tdt = jnp.sum(dh_new * h_in, axis=0, keepdims=True) * t_b
        per_head = _pick(jnp.concatenate([xfq, dy * y_off, dxdt * xs, dy * xs, jnp.broadcast_to(tdt, (8, D))],
                                         axis=0), red)
        fdf, dyoff_h, dxdtxs_h, dyxs_h = [per_head[k * CH:(k + 1) * CH] for k in range(4)]
        dcs = seg_rows - seg_cols.T + dyoff_h - fdf
        last = per_head[4 * CH:4 * CH + 1] + jnp.sum(fdf, axis=0, keepdims=True)
        dcs = dcs + jnp.where(_iota((CH, LANE), 0) == CH - 1, last, 0.0)
        tri_t = (_iota((CH, CH), 1) >= _iota((CH, CH), 0)).astype(BF16)
        da = _pick_left(tri_t, dcs)
        ddt = da * q["a_row"] + dxdtxs_h
        dxs = dxdt * dt_b + db_ref[...] * dy
        ddt_raw = ddt * _sigmoid(q["pre"])
        dsm_ref[0:1, :] += jnp.sum(ddt_raw, axis=0, keepdims=True)
        dsm_ref[1:2, :] += jnp.sum(da * q["dt"], axis=0, keepdims=True) * q["a_row"]
        dsm_ref[2:3, :] += jnp.sum(dyxs_h, axis=0, keepdims=True)
        edy = (e_b * dy).astype(BF16)
        xf = (xdt * f_b).astype(BF16)
        dbs, dcs_g, dhs = [], [], []
        for g in range(2):
            sl = slice(g * 512, (g + 1) * 512)
            dcb_b = dcbm[g].astype(BF16)
            dcs_g.append(_dot(dcb_b, bg[g]) + _dot_nt(edy[:, sl], hb[:, sl]))
            dbs.append(_dot_tn(dcb_b, cg[g]) + _dot_nt(xf[:, sl], dhb[:, sl]))
            dhs.append(_dot_tn(cg[g], edy[:, sl]))
        dh_ref[...] = t_b * dh_new + jnp.concatenate(dhs, axis=1)
        dbc = jnp.concatenate(dbs + dcs_g, axis=1)

        def conv_bwd(dact, pre, taps, w, nxt_ref, lo):
            s = _sigmoid(pre)
            dconv = dact * (s * (1.0 + pre * (1.0 - s)))
            nxt8 = nxt_ref[...]
            row8 = _iota(nxt8.shape, 0)
            hi = lo + dconv.shape[1]
            dcb_ref[:, lo:hi] += jnp.sum(dconv, axis=0, keepdims=True)
            dx = w[3:4, :] * dconv
            for k in range(4):
                dcw_ref[k:k + 1, lo:hi] += jnp.sum(dconv * taps[k], axis=0, keepdims=True)
            for s_ in (1, 2, 3):
                rolled = pltpu.roll(dconv, CH - s_, 0)
                tail = jnp.where(row8 >= 8 - s_, pltpu.roll(nxt8, 8 - s_, 0), rolled[CH - 8:])
                dx = dx + w[3 - s_:4 - s_, :] * jnp.concatenate([rolled[:CH - 8], tail], axis=0)
            nxt_ref[...] = dconv[:8]
            return dx

        dx_ref[:, 0:D] = conv_bwd(dxs, q["xs_c"], q["xs_taps"], cw[:, :D], nxs_ref, 0).astype(BF16)
        dx_ref[:, D:D + 512] = conv_bwd(dbc, q["bc_c"], q["bc_taps"], cw[:, D:], nbc_ref, D).astype(BF16)
        dx_ref[:, D + 512:D + 640] = ddt_raw.astype(BF16)
        dx_ref[:, D + 640:] = jnp.zeros((CH, D - 640), BF16)

    def col(width, blk, prev=False):
        if prev:
            return pl.BlockSpec((8, width), lambda i: (jnp.maximum((NC - 1 - i) * (CH // 8) - 1, 0), blk))
        return pl.BlockSpec((CH, width), lambda i: (NC - 1 - i, blk))

    def full(a):
        return pl.BlockSpec(a.shape, lambda i: (0,) * len(a.shape))

    acc_shapes = [SDS((4, 1536), F32), SDS((1, 1536), F32), SDS((8, LANE), F32), SDS((1, D), F32)]
    return pl.pallas_call(
        body, name="ssm_bwd", grid=(NC,),
        in_specs=[col(D, 5), col(D, 5, True), col(512, 12), col(512, 12, True), col(LANE, 52), col(D, 4),
                  col(D, 1), col(D, 0), pl.BlockSpec((None, CH, D), lambda i: (NC - 1 - i, 0, 0)),
                  full(cw), full(cb), full(dtb), full(alog), full(d_b), full(nw)],
        out_specs=[col(D, 0), col(2 * D, 0)] + [full(a) for a in acc_shapes],
        out_shape=[SDS((S, D), BF16), SDS((S, 2 * D), BF16)] + acc_shapes,
        scratch_shapes=[pltpu.VMEM((CH, D), F32), pltpu.VMEM((8, D), F32), pltpu.VMEM((8, 512), F32)],
        compiler_params=_cp(("arbitrary",)),
    )(proj, proj, proj, proj, proj, proj, dmix, y_save, states, cw, cb, dtb, alog, d_b, nw)


def _outproj_loss(mix, w_out, x, tgt, nw):
    tm = 256

    def body(mix_ref, w_ref, x_ref, t_ref, nw_ref, dy_ref, dmix_ref, dw_ref, dnw_ref, loss_ref):
        @pl.when(pl.program_id(0) == 0)
        def _():
            dw_ref[...] = jnp.zeros_like(dw_ref)
            dnw_ref[...] = jnp.zeros_like(dnw_ref)
            loss_ref[...] = jnp.zeros_like(loss_ref)

        mixv, w = mix_ref[...], w_ref[...]
        out = _dot(mixv, w)
        r = lax.rsqrt(jnp.mean(out * out, axis=-1, keepdims=True) + EPS)
        nh = out * r
        nw_v = nw_ref[...]
        err = x_ref[...] + nh * nw_v - t_ref[...]
        loss_ref[...] += 0.5 * jnp.sum(jnp.mean(err * err, axis=-1, keepdims=True), axis=0, keepdims=True)
        dy = err * (1.0 / D)
        dy_ref[...] = dy
        dnw_ref[...] += jnp.sum(dy * nh, axis=0, keepdims=True)
        gdn = dy * nw_v
        dout = (r * (gdn - nh * jnp.mean(gdn * nh, axis=-1, keepdims=True))).astype(BF16)
        dmix_ref[...] = _dot_nt(dout, w)
        dw_ref[...] += _dot_tn(mixv, dout)

    row = lambda w: pl.BlockSpec((tm, w), lambda i: (i, 0))
    full = lambda s: pl.BlockSpec(s, lambda i: (0, 0))
    return pl.pallas_call(
        body, name="outproj_loss", grid=(S // tm,),
        in_specs=[row(2 * D), full((2 * D, D)), row(D), row(D), full((1, D))],
        out_specs=[row(D), row(2 * D), full((2 * D, D)), full((1, D)), full((1, LANE))],
        out_shape=[SDS((S, D), F32), SDS((S, 2 * D), F32), SDS((2 * D, D), F32), SDS((1, D), F32),
                   SDS((1, LANE), F32)],
        compiler_params=_cp(("arbitrary",)),
    )(mix, w_out, x, tgt, nw)


def _inproj_bwd_dx(srcs, dxbcdt, w_all, x, dy, nw, hosted=None):
    tm = 512
    nk = DP // D
    n_host = len(hosted.arrays) if hosted else 0

    def body(*refs):
        src_refs = refs[:nk]
        w_ref, x_ref, dy_ref, nw_ref = refs[nk:nk + 4]
        host_in, refs = refs[nk + 4:nk + 4 + n_host], refs[nk + 4 + n_host:]
        gx_ref, dnw_ref = refs[:2]
        host_out, host_sems = refs[2:2 + n_host], refs[2 + n_host:]
        i = pl.program_id(0)

        @pl.when(i == 0)
        def _():
            if hosted:
                hosted.start(host_in, host_out, host_sems)
            dnw_ref[...] = jnp.zeros_like(dnw_ref)

        du = None
        for k, ref in enumerate(src_refs):
            part = _dot_nt(ref[...], w_ref[:, k * D:(k + 1) * D])
            du = part if du is None else du + part
        xf, nw_v = x_ref[...], nw_ref[...]
        r = lax.rsqrt(jnp.mean(xf * xf, axis=-1, keepdims=True) + EPS)
        xh = xf * r
        dnw_ref[...] += jnp.sum(du * xh, axis=0, keepdims=True)
        gdu = du * nw_v
        gx_ref[...] = r * (gdu - xh * jnp.mean(gdu * xh, axis=-1, keepdims=True)) + dy_ref[...]

        if hosted:
            pl.when(i == S // tm - 1)(lambda: hosted.finish(host_in, host_out, host_sems))

    row = pl.BlockSpec((tm, D), lambda i: (i, 0))
    row1 = pl.BlockSpec((tm, D), lambda i: (i, 1))
    one = pl.BlockSpec((1, D), lambda i: (0, 0))
    whole_w = pl.BlockSpec((D, DP), lambda i: (0, 0), pipeline_mode=pl.Buffered(1))
    args = [*srcs, dxbcdt, dxbcdt, w_all, x, dy, nw]
    in_specs = [row] * len(srcs) + [row, row1, whole_w, row, row, one]
    out_specs, out_shape, scratch = [row, one], [SDS((S, D), F32), SDS((1, D), F32)], []
    if hosted:
        args += hosted.arrays
        in_specs += [ANY] * n_host
        out_specs += [ANY] * n_host
        out_shape += hosted.out_shape
        scratch += hosted.scratch
    outs = pl.pallas_call(
        body, name="inproj_bwd_dx", grid=(S // tm,),
        in_specs=in_specs, out_specs=out_specs, out_shape=out_shape, scratch_shapes=scratch,
        compiler_params=_cp(("arbitrary",)),
    )(*args)
    return (outs[:2], outs[2:]) if hosted else outs


def _dw(u, dsec, name):
    ts = 512
    ncol = dsec.shape[1] // D

    def body(u_ref, d_ref, o_ref):
        @pl.when(pl.program_id(1) == 0)
        def _():
            o_ref[...] = jnp.zeros_like(o_ref)

        o_ref[...] += _dot_tn(u_ref[...], d_ref[...])

    return pl.pallas_call(
        body, name=name, grid=(ncol, S // ts),
        in_specs=[pl.BlockSpec((ts, D), lambda j, i: (i, 0)), pl.BlockSpec((ts, D), lambda j, i: (i, j))],
        out_specs=pl.BlockSpec((D, D), lambda j, i: (0, j)),
        out_shape=SDS((D, ncol * D), F32),
        compiler_params=_cp(("parallel", "arbitrary")),
    )(u, dsec)


def _place():
    x, y, c = lax.axis_index("x"), lax.axis_index("y"), lax.axis_index("c")
    return x, y, c, 2 * x + y


def _chip_of(x, y, k):
    px = 1 - x if k & 2 else x
    py = 1 - y if k & 1 else y
    return px, py, 2 * px + py


def _remote(src, dst, send_sem, recv_sem, dev):
    return pltpu.make_async_remote_copy(src_ref=src, dst_ref=dst, send_sem=send_sem, recv_sem=recv_sem,
                                        device_id=dev, device_id_type=MESH)


def _gather_weights(w_in_b, w_out_b, conv_w):
    def body(win_ref, wout_ref, cw_ref, gin_ref, gout_ref, gcw_ref, send, recv, csend, crecv, osend, orecv):
        x, y, c, j = _place()
        me, sib = (x, y, c), (x, y, 1 - c)
        nbr = {"x": _chip_of(x, y, 2), "y": _chip_of(x, y, 1)}
        diag = _chip_of(x, y, 3)[2]
        conv = [_remote(cw_ref, gcw_ref.at[j], csend.at[k - 1], crecv.at[k - 1], (*_chip_of(x, y, k)[:2], c))
                for k in (1, 2, 3)]
        own = [_remote(src, dst.at[j], osend.at[n], orecv.at[n], sib)
               for n, (src, dst) in enumerate(((win_ref, gin_ref), (wout_ref, gout_ref), (cw_ref, gcw_ref)))]
        for cp in conv + own:
            cp.start()
        started, arrivals = [], []
        pairs = ((win_ref, gin_ref), (wout_ref, gout_ref))

        def rows(src, n_quarter=None, sibling=False):
            half = src.shape[0] // 2
            base = (1 - c if sibling else c) * half
            return pl.ds(base, half) if n_quarter is None else pl.ds(base + n_quarter * (half // 2), half // 2)

        def sem(a, n):
            return send.at[8 * a + n], recv.at[8 * a + n]

        def go(cp):
            cp.start()
            started.append(cp)

        for a, (src, dst) in enumerate(pairs):
            for n, axis in enumerate("xy"):
                px, py, _ = nbr[axis]
                go(_remote(src.at[rows(src)], dst.at[j, rows(src)], *sem(a, n), (px, py, c)))
        for n, axis in enumerate("xy"):
            ox, oy, _ = nbr["y" if axis == "x" else "x"]
            pj = nbr[axis][2]
            for a, (src, dst) in enumerate(pairs):
                _remote(src.at[rows(src)], dst.at[pj, rows(src)], *sem(a, n), me).wait_recv()
                go(_remote(dst.at[pj, rows(src, n)], dst.at[pj, rows(src, n)], *sem(a, 2 + n), (ox, oy, c)))
                go(_remote(dst.at[pj, rows(src)], dst.at[pj, rows(src)], *sem(a, 4 + n), sib))
                arrivals.append(_remote(src.at[rows(src)], dst.at[pj, rows(src, None, True)], *sem(a, 4 + n), me))
        for n in range(2):
            for a, (src, dst) in enumerate(pairs):
                part = rows(src, n)
                _remote(dst.at[diag, part], dst.at[diag, part], *sem(a, 2 + n), me).wait_recv()
                go(_remote(dst.at[diag, part], dst.at[diag, part], *sem(a, 6 + n), sib))
                sib_part = rows(src, n, True)
                arrivals.append(_remote(dst.at[diag, sib_part], dst.at[diag, sib_part], *sem(a, 6 + n), me))
        for cp in arrivals:
            cp.wait_recv()
        for k in (1, 2, 3):
            pj = _chip_of(x, y, k)[2]
            _remote(cw_ref, gcw_ref.at[pj], csend.at[k - 1], crecv.at[k - 1], me).wait_recv()
        for cp in own:
            cp.wait_recv()
        for cp in started + conv + own:
            cp.wait_send()

    return pl.pallas_call(
        body, name="gather_weights",
        in_specs=[ANY, ANY, ANY], out_specs=[ANY, ANY, ANY],
        out_shape=[SDS((4,) + w_in_b.shape, BF16), SDS((4,) + w_out_b.shape, BF16), SDS((4,) + conv_w.shape, F32)],
        scratch_shapes=[pltpu.SemaphoreType.DMA((16,)), pltpu.SemaphoreType.DMA((16,)),
                        pltpu.SemaphoreType.DMA((3,)), pltpu.SemaphoreType.DMA((3,)),
                        pltpu.SemaphoreType.DMA((3,)), pltpu.SemaphoreType.DMA((3,))],
        compiler_params=pltpu.CompilerParams(has_side_effects=True),
    )(w_in_b, w_out_b, conv_w)


def _pair_exchange(arrays, name):
    halves = [a.shape[1] // 2 for a in arrays]
    n = len(arrays)

    def body(*refs):
        x, y, c, _ = _place()
        send, recv = refs[2 * n:]
        cps = [_remote(refs[k].at[:, pl.ds((1 - c) * halves[k], halves[k])], refs[n + k], send.at[k], recv.at[k],
                       (x, y, 1 - c)) for k in range(n)]
        for cp in cps:
            cp.start()
        for cp in cps:
            cp.wait()

    return pl.pallas_call(
        body, name=name, in_specs=[ANY] * n, out_specs=[ANY] * n,
        out_shape=[SDS((a.shape[0], h, a.shape[2]), F32) for a, h in zip(arrays, halves)],
        scratch_shapes=[pltpu.SemaphoreType.DMA((n,)), pltpu.SemaphoreType.DMA((n,))],
        compiler_params=pltpu.CompilerParams(has_side_effects=True),
    )(*arrays)


def _pair_sum(cidx, g, r, name):
    n, half, width = r.shape
    tr = min(half, 256)
    nt = half // tr

    def body(c_ref, g_ref, r_ref, o_ref):
        del c_ref
        o_ref[...] = (g_ref[...] + r_ref[...]).astype(BF16)

    return pl.pallas_call(
        body, name=name,
        grid_spec=pltpu.PrefetchScalarGridSpec(
            num_scalar_prefetch=1, grid=(n, nt),
            in_specs=[pl.BlockSpec((None, tr, width), lambda s, t, c: (s, c[0] * nt + t, 0)),
                      pl.BlockSpec((None, tr, width), lambda s, t, c: (s, t, 0))],
            out_specs=pl.BlockSpec((None, tr, width), lambda s, t, c: (s, t, 0))),
        out_shape=SDS(r.shape, BF16),
        compiler_params=_cp(("parallel", "parallel")),
    )(cidx, g, r)


class _ChipExchange:
    def __init__(self, arrays, rows):
        self.arrays, self.rows = list(arrays), list(rows)
        self.out_shape = [SDS((4,) + a.shape[1:], BF16) for a in self.arrays]
        self.scratch = [pltpu.SemaphoreType.DMA((3 * len(self.arrays),)) for _ in range(2)]

    def _copies(self, ins, outs, sems):
        x, y, c, j = _place()
        send, recv = sems
        for a, (src, dst, row) in enumerate(zip(ins, outs, self.rows)):
            for k in (1, 2, 3):
                px, py, pj = _chip_of(x, y, k)
                n = 3 * a + k - 1
                slot = pj if row is None else py
                yield (None if row is None else px == row, None if row is None else x == row,
                       _remote(src.at[slot], dst.at[j], send.at[n], recv.at[n], (px, py, c)),
                       _remote(src.at[0], dst.at[pj], send.at[n], recv.at[n], (x, y, c)))

    def start(self, ins, outs, sems):
        for sends, _, send, _ in self._copies(ins, outs, sems):
            if sends is None:
                send.start()
            else:
                pl.when(sends)(send.start)

    def finish(self, ins, outs, sems):
        for sends, owns, send, arrival in self._copies(ins, outs, sems):
            if sends is None:
                arrival.wait_recv()
                send.wait_send()
            else:
                pl.when(owns)(arrival.wait_recv)
                pl.when(sends)(send.wait_send)


def _small_exchange(small):
    def body(sm_ref, rs_ref, send, recv, lsem):
        x, y, c, j = _place()
        me = 2 * j + c
        local = pltpu.make_async_copy(sm_ref, rs_ref.at[me], lsem)
        local.start()
        cps = []
        for k in range(1, 8):
            px, py, _ = _chip_of(x, y, k >> 1)
            pc = 1 - c if k & 1 else c
            cps.append(_remote(sm_ref, rs_ref.at[me], send.at[k - 1], recv.at[k - 1], (px, py, pc)))
        for cp in cps:
            cp.start()
        for k in range(1, 8):
            _, _, pj = _chip_of(x, y, k >> 1)
            pc = 1 - c if k & 1 else c
            _remote(sm_ref, rs_ref.at[2 * pj + pc], send.at[k - 1], recv.at[k - 1], (x, y, c)).wait_recv()
        for cp in cps:
            cp.wait_send()
        local.wait()

    return pl.pallas_call(
        body, name="small_exchange", in_specs=[ANY], out_specs=ANY,
        out_shape=SDS((8,) + small.shape, F32),
        scratch_shapes=[pltpu.SemaphoreType.DMA((7,)), pltpu.SemaphoreType.DMA((7,)), pltpu.SemaphoreType.DMA],
        compiler_params=pltpu.CompilerParams(has_side_effects=True),
    )(small)


def _slot_sum(r, name):
    n, rows, width = r.shape
    tr = min(rows, 256)

    def body(r_ref, o_ref):
        acc = r_ref[0].astype(F32)
        for s in range(1, n):
            acc = acc + r_ref[s].astype(F32)
        o_ref[...] = acc

    return pl.pallas_call(
        body, name=name, grid=(rows // tr,),
        in_specs=[pl.BlockSpec((n, tr, width), lambda t: (0, t, 0))],
        out_specs=pl.BlockSpec((tr, width), lambda t: (t, 0)),
        out_shape=SDS((rows, width), F32),
        compiler_params=_cp(("parallel",)),
    )(r)


def _chip_sum(chip_idx, recv, own, name):
    n, rows, width = recv.shape
    tr = min(rows, 256)

    def body(j_ref, r_ref, own_ref, o_ref):
        acc = None
        for s in range(n):
            term = jnp.where(j_ref[0] == s, own_ref[...], r_ref[s]).astype(F32)
            acc = term if acc is None else acc + term
        o_ref[...] = acc

    return pl.pallas_call(
        body, name=name,
        grid_spec=pltpu.PrefetchScalarGridSpec(
            num_scalar_prefetch=1, grid=(rows // tr,),
            in_specs=[pl.BlockSpec((n, tr, width), lambda t, j: (0, t, 0)),
                      pl.BlockSpec((None, tr, width), lambda t, j: (j[0], t, 0))],
            out_specs=pl.BlockSpec((tr, width), lambda t, j: (t, 0))),
        out_shape=SDS((rows, width), F32),
        compiler_params=_cp(("parallel",)),
    )(chip_idx, recv, own)


def _chip_sum_rows(place, recv0, own0, recv1, own1, name):
    n, rows, width = recv0.shape
    tr = min(rows, 256)

    def body(p_ref, r0_ref, o0_ref, r1_ref, o1_ref, o_ref):
        first_row = p_ref[1] == 0
        own = jnp.where(first_row, o0_ref[...], o1_ref[...])
        acc = None
        for s in range(n):
            term = jnp.where(p_ref[0] == s, own, jnp.where(first_row, r0_ref[s], r1_ref[s])).astype(F32)
            acc = term if acc is None else acc + term
        o_ref[...] = acc

    recv = pl.BlockSpec((n, tr, width), lambda t, p: (0, t, 0))
    own = pl.BlockSpec((None, tr, width), lambda t, p: (p[2], t, 0))
    return pl.pallas_call(
        body, name=name,
        grid_spec=pltpu.PrefetchScalarGridSpec(
            num_scalar_prefetch=1, grid=(rows // tr,), in_specs=[recv, own, recv, own],
            out_specs=pl.BlockSpec((tr, width), lambda t, p: (t, 0))),
        out_shape=SDS((rows, width), F32),
        compiler_params=_cp(("parallel",)),
    )(place, recv0, own0, recv1, own1)


def _half_exchange(hw, ho):
    def body(hw_ref, ho_ref, tw_ref, to_ref, send, recv):
        x, y, c, _ = _place()
        sib = (x, y, 1 - c)
        cps = [_remote(hw_ref, tw_ref, send.at[0], recv.at[0], sib),
               _remote(ho_ref, to_ref, send.at[1], recv.at[1], sib)]
        for cp in cps:
            cp.start()
        for cp in cps:
            cp.wait()

    return pl.pallas_call(
        body, name="half_exchange", in_specs=[ANY, ANY], out_specs=[ANY, ANY],
        out_shape=[SDS(hw.shape, F32), SDS(ho.shape, F32)],
        scratch_shapes=[pltpu.SemaphoreType.DMA((2,)), pltpu.SemaphoreType.DMA((2,))],
        compiler_params=pltpu.CompilerParams(has_side_effects=True),
    )(hw, ho)


def _by_core(c, mine, theirs):
    return jnp.where(c == 0, jnp.concatenate([mine, theirs], axis=0), jnp.concatenate([theirs, mine], axis=0))


def _adamw(w, g, m, v, name):
    rows, width = w.shape
    tr = min(rows, 256)

    def body(w_ref, g_ref, m_ref, v_ref, d_ref, nm_ref, nv_ref):
        gv = g_ref[...]
        nm = ADAM_B1 * m_ref[...] + (1.0 - ADAM_B1) * gv
        nv = ADAM_B2 * v_ref[...] + (1.0 - ADAM_B2) * (gv * gv)
        m_hat = nm / (1.0 - ADAM_B1 ** ADAM_STEP)
        v_hat = nv / (1.0 - ADAM_B2 ** ADAM_STEP)
        d_ref[...] = -ADAM_LR * (m_hat / (jnp.sqrt(v_hat) + ADAM_EPS) + ADAM_WD * w_ref[...])
        nm_ref[...] = nm
        nv_ref[...] = nv

    t = pl.BlockSpec((tr, width), lambda i: (i, 0))
    return pl.pallas_call(
        body, name=name, grid=(rows // tr,), in_specs=[t] * 4, out_specs=[t] * 3,
        out_shape=[SDS(w.shape, F32)] * 3, compiler_params=_cp(("parallel",)),
    )(w, g, m, v)


def _rows128(a, rows):
    flat = a.reshape(-1)
    return jnp.pad(flat, (0, rows * LANE - flat.shape[0])).reshape(rows, LANE)


def _pack_small(conv_w, norm_pre, conv_b, ssm_norm, norm_post, dtb, alog, dsk, extra=None):
    cw_rows = 48 if conv_w.shape[-1] == 1536 else 16
    extra = jnp.zeros((1, LANE), F32) if extra is None else _rows128(extra, 1)
    vec = jnp.concatenate([_rows128(dtb, 1), _rows128(alog, 1), _rows128(dsk, 1), extra, jnp.zeros((4, LANE), F32)],
                          axis=0)
    return jnp.concatenate([_rows128(conv_w, cw_rows), _rows128(norm_pre, 8), _rows128(conv_b, 16),
                            _rows128(ssm_norm, 8), _rows128(norm_post, 8), vec], axis=0)


def _unpack_small(p, cw_cols):
    cw_rows = 48 if cw_cols == 1536 else 16
    o = cw_rows
    conv_w = p[:cw_rows].reshape(-1)[:4 * cw_cols].reshape(1, 4, cw_cols)
    norm_pre = p[o:o + 8].reshape(1, D)
    conv_b = p[o + 8:o + 24].reshape(-1)[:1536].reshape(1, 1536)
    ssm_norm = p[o + 24:o + 32].reshape(1, D)
    norm_post = p[o + 32:o + 40].reshape(1, D)
    vec = p[o + 40:o + 48]
    return conv_w, norm_pre, conv_b, ssm_norm, norm_post, vec[0:1, :NH], vec[1:2, :NH], vec[2:3, :NH], vec[3, 0]


def _pad_lanes(a):
    return jnp.pad(a, ((0, 0), (0, LANE - a.shape[1])))


class _GradReduce:
    SPLIT = 2 * SHARD - OFF_G

    def __init__(self, xi, yi, ci):
        self.ci = ci
        self.cidx = jnp.reshape(ci, (1,)).astype(jnp.int32)
        self.place = jnp.stack([2 * xi + yi, xi, yi]).astype(jnp.int32)

    def first(self, dw_g, dw_z, dw_x, dw_out):
        cols = jnp.concatenate([dw_g[:, self.SPLIT:], dw_z, dw_x], axis=1)
        gw = jnp.stack([cols[:, :SHARD], cols[:, SHARD:2 * SHARD]])
        go = dw_out.reshape(4, D // 2, D)
        rw, ro = _pair_exchange([gw, go], "pair_exchange_hi")
        self.pw_hi = _pair_sum(self.cidx, gw, rw, "pair_sum_hi")
        self.po = _pair_sum(self.cidx, go, ro, "pair_sum_out")
        return _ChipExchange([self.pw_hi, self.po], [1, None])

    def first_done(self, got):
        self.rw_hi, self.ro = got

    def second(self, dw_q, dw_k, dw_v, dw_g):
        cols = jnp.concatenate([dw_q, dw_k, dw_v, dw_g[:, :self.SPLIT]], axis=1)
        gw = jnp.stack([cols[:, :SHARD], cols[:, SHARD:]])
        (rw,) = _pair_exchange([gw], "pair_exchange_lo")
        self.pw_lo = _pair_sum(self.cidx, gw, rw, "pair_sum_lo")
        return _ChipExchange([self.pw_lo], [0])

    def second_done(self, got):
        (self.rw_lo,) = got

    def result(self):
        half_in = _chip_sum_rows(self.place, self.rw_lo, self.pw_lo, self.rw_hi, self.pw_hi, "chip_sum_in")
        half_out = _chip_sum(self.place[0:1], self.ro, self.po, "chip_sum_out")
        their_in, their_out = _half_exchange(half_in, half_out)
        return _by_core(self.ci, half_in, their_in), _by_core(self.ci, half_out, their_out)


def kernel(x, norm_pre_w, w_in, conv_w, conv_b, dt_bias, a_log, d_skip, ssm_norm_w, w_out, norm_post_w, loss_target, m_norm_pre_w, m_w_in, m_conv_w, m_conv_b, m_dt_bias, m_a_log, m_d_skip, m_ssm_norm_w, m_w_out, m_norm_post_w, v_norm_pre_w, v_w_in, v_conv_w, v_conv_b, v_dt_bias, v_a_log, v_d_skip, v_ssm_norm_w, v_w_out, v_norm_post_w):
    xi, yi, ci = lax.axis_index("x"), lax.axis_index("y"), lax.axis_index("c")
    chip = 2 * xi + yi
    x2, tgt = x[0], loss_target[0]

    w_in_b, w_out_b = w_in[0].astype(BF16), w_out[0].astype(BF16)
    gin, gout, gcw = _gather_weights(w_in_b, w_out_b, conv_w[0])

    w_all = jnp.concatenate([gin[0], gin[1], gin[2], gin[3], jnp.zeros((D, DP - 4 * SHARD), BF16)], axis=1)
    w_out_all = gout.reshape(2 * D, D)
    cw_all = jnp.concatenate([gcw[0], gcw[1], gcw[2], gcw[3]], axis=1)
    reduce = _GradReduce(xi, yi, ci)
    grad_x, small = _local_step(x2, tgt, w_all, w_out_all, cw_all, norm_pre_w, conv_b, dt_bias, a_log, d_skip,
                                ssm_norm_w, norm_post_w, reduce)[:2]
    g_in, g_out = reduce.result()
    g_small = _slot_sum(_small_exchange(small), "small_sum")
    g_cw, g_npre, g_cb, g_nssm, g_npost, g_dtb, g_alog, g_dsk, loss = _unpack_small(g_small, 1536)
    g_cw = lax.dynamic_slice_in_dim(g_cw, chip * 384, 384, axis=2)

    d_in, nm_in, nv_in = _adamw(w_in[0], g_in, m_w_in[0], v_w_in[0], "adamw_in")
    d_out, nm_out, nv_out = _adamw(w_out[0], g_out, m_w_out[0], v_w_out[0], "adamw_out")
    packed = [_pack_small(*t) for t in (
        (conv_w, norm_pre_w, conv_b, ssm_norm_w, norm_post_w, dt_bias, a_log, d_skip),
        (g_cw, g_npre, g_cb, g_nssm, g_npost, g_dtb, g_alog, g_dsk),
        (m_conv_w, m_norm_pre_w, m_conv_b, m_ssm_norm_w, m_norm_post_w, m_dt_bias, m_a_log, m_d_skip),
        (v_conv_w, v_norm_pre_w, v_conv_b, v_ssm_norm_w, v_norm_post_w, v_dt_bias, v_a_log, v_d_skip))]
    small_out = [_unpack_small(p, 384)[:8] for p in _adamw(*packed, "adamw_small")]

    def ordered(cw_, npre, cb_, nssm, npost, dtb_, alog_, dsk_, big_in, big_out):
        return [npre, big_in[None], cw_, cb_, dtb_, alog_, dsk_, nssm, big_out[None], npost]

    grads = ordered(g_cw, g_npre, g_cb, g_nssm, g_npost, g_dtb, g_alog, g_dsk, g_in, g_out)
    deltas = ordered(*small_out[0], d_in, d_out)
    new_m = ordered(*small_out[1], nm_in, nm_out)
    new_v = ordered(*small_out[2], nv_in, nv_out)
    return (loss, grad_x[None], *grads, *deltas, *new_m, *new_v)


def _local_step(x2, tgt, w_all, w_out_all, cw_all, norm_pre_w, conv_b, dt_bias, a_log, d_skip, ssm_norm_w,
                norm_post_w, reduce=None):
    dtb, alog = _pad_lanes(dt_bias), _pad_lanes(a_log)
    d_b = jnp.repeat(d_skip, 64, axis=1)

    proj, u = _inproj_fwd(x2, norm_pre_w, w_all)
    mix, attn_pre, lse = _attn_fwd(proj, 1, _attn_fwd(proj, 4, _attn_fwd(proj, 16)), final=True)
    mix, y_save, states = _ssm_fwd(proj, mix, cw_all, conv_b, dtb, alog, d_b, ssm_norm_w)

    dy, dmix, dw_out, dnw_post, loss_part = _outproj_loss(mix, w_out_all, x2, tgt, norm_post_w)
    do, delta, dg = _attn_gate_bwd(dmix, attn_pre, proj)
    dz, dxbcdt, dcw, dcb, dvec, dnw_ssm = _ssm_bwd(proj, dmix, y_save, states, cw_all, conv_b, dtb, alog, d_b,
                                                   ssm_norm_w)
    dw_g, dw_z, dw_x = _dw(u, dg, "dw_in_g"), _dw(u, dz, "dw_in_z"), _dw(u, dxbcdt, "dw_in_xbcdt")
    acc = _attn_bwd(proj, do, lse, delta, 16, None, F32, reduce.first(dw_g, dw_z, dw_x, dw_out) if reduce else None)
    if reduce:
        acc, got = acc
        reduce.first_done(got)
    acc = _attn_bwd(proj, do, lse, delta, 4, acc, F32)
    dq, dk, dv = _attn_bwd(proj, do, lse, delta, 1, acc, BF16)
    dw_q, dw_k, dw_v = _dw(u, dq, "dw_in_q"), _dw(u, dk, "dw_in_k"), _dw(u, dv, "dw_in_v")
    res = _inproj_bwd_dx([dq, dk, dv, dg, dz], dxbcdt, w_all, x2, dy, norm_pre_w,
                         reduce.second(dw_q, dw_k, dw_v, dw_g) if reduce else None)
    if reduce:
        res, got = res
        reduce.second_done(got)
    grad_x, dnw_pre = res
    dw_all = jnp.concatenate([dw_q, dw_k, dw_v, dw_g, dw_z, dw_x], axis=1)
    small = _pack_small(dcw, dnw_pre, dcb, dnw_ssm, dnw_post, dvec[0:1, :NH], dvec[1:2, :NH], dvec[2:3, :NH],
                        loss_part[:, :1])
    return grad_x, small, dw_all, dw_out
```

```python
import functools

import jax
import jax.numpy as jnp
from jax import lax
from jax.experimental import pallas as pl
from jax.experimental.pallas import tpu as pltpu

F32 = jnp.float32
BF16 = jnp.bfloat16
MESH = pl.DeviceIdType.MESH
SDS = jax.ShapeDtypeStruct
ANY = pl.BlockSpec(memory_space=pl.ANY)

S = 4096
D = 1024
DP = 7168
SHARD = 1668
OFF_G, OFF_Z = 3072, 4096
NH = 16
CH = 128
NC = S // CH
EPS = 1e-6
NEG = -1e30
LANE = 128
VMEM_LIMIT = 48 * 1024 * 1024

ADAM_LR, ADAM_B1, ADAM_B2, ADAM_EPS, ADAM_WD, ADAM_STEP = 0.001, 0.9, 0.999, 1e-08, 0.01, 10


def _cp(sem, **kw):
    return pltpu.CompilerParams(dimension_semantics=sem, vmem_limit_bytes=VMEM_LIMIT, **kw)


def _dot(a, b):
    return jnp.dot(a, b, preferred_element_type=F32)


def _dot_nt(a, b):
    return lax.dot_general(a, b, (((1,), (1,)), ((), ())), preferred_element_type=F32)


def _dot_tn(a, b):
    return lax.dot_general(a, b, (((0,), (0,)), ((), ())), preferred_element_type=F32)


def _pieces(x, n):
    out = []
    for _ in range(n):
        p = x.astype(BF16)
        out.append(p)
        x = x - p.astype(F32)
    return out


def _pick(x, sel, n=2):
    parts = [_dot(p, sel) for p in _pieces(x, n)]
    return functools.reduce(jnp.add, parts)


def _pick_left(sel, x, n=3):
    parts = [_dot(sel, p) for p in _pieces(x, n)]
    return functools.reduce(jnp.add, parts)


def _sigmoid(v):
    return 0.5 * jnp.tanh(0.5 * v) + 0.5


def _iota(shape, dim):
    return lax.broadcasted_iota(jnp.int32, shape, dim)


def _inproj_fwd(x, nw, w_all):
    tm, tn = 1024, 1024

    def body(x_ref, nw_ref, w_ref, proj_ref, u_ref):
        @pl.when(pl.program_id(1) == 0)
        def _():
            xf = x_ref[...]
            r = lax.rsqrt(jnp.mean(xf * xf, axis=-1, keepdims=True) + EPS)
            u_ref[...] = (xf * r * nw_ref[...]).astype(BF16)

        proj_ref[...] = _dot(u_ref[...], w_ref[...])

    return pl.pallas_call(
        body, name="inproj_fwd", grid=(S // tm, DP // tn),
        in_specs=[pl.BlockSpec((tm, D), lambda i, j: (i, 0)), pl.BlockSpec((1, D), lambda i, j: (0, 0)),
                  pl.BlockSpec((D, tn), lambda i, j: (0, j))],
        out_specs=[pl.BlockSpec((tm, tn), lambda i, j: (i, j)), pl.BlockSpec((tm, D), lambda i, j: (i, 0))],
        out_shape=[SDS((S, DP), F32), SDS((S, D), BF16)],
        compiler_params=_cp(("parallel", "arbitrary")),
    )(x, nw, w_all)


ATTN_QB = {1: 16, 4: 4, 16: 1}


def _unit_rows(r, u, d):
    return pl.ds(r + d * CH * u, CH, stride=d) if d > 1 else pl.ds(CH * u, CH)


def _for_units(d, qb, fn):
    for r in range(d):
        for u in range(qb):
            fn(r, u)


def _attn_mask(has_prev):
    qi, kj = _iota((2 * CH, 2 * CH), 0) & (CH - 1), _iota((2 * CH, 2 * CH), 1)
    cur_ok = (kj >= CH) & (kj - CH <= qi)
    prev_ok = (kj < CH) & (kj >= qi)
    return cur_ok | (prev_ok & has_prev)


def _stack_heads(v, lane_a):
    return jnp.concatenate([jnp.where(lane_a, v, 0.0), jnp.where(lane_a, 0.0, v)], axis=0).astype(BF16)


def _attn_specs(d, qb):
    rows, prows = CH * d * qb, CH * d
    nb = S // rows
    steps = (NH // 2) * nb

    def at(t):
        t = jnp.minimum(t, steps - 1)
        return t % nb, t // nb

    def cur(off):
        return pl.BlockSpec((rows, LANE), lambda t: (at(t)[0], off + at(t)[1]))

    def prev(off):
        return pl.BlockSpec((prows, LANE), lambda t: (jnp.maximum(at(t)[0] * qb - 1, 0), off + at(t)[1]))

    lag = pl.BlockSpec((rows, LANE), lambda t: at(jnp.maximum(t - 1, 0)))
    return nb, steps, cur, prev, lag


def _gather16(src_ref, dense_ref, tmp_ref):
    for a in range(4):
        tmp_ref[...] = src_ref[pl.ds(a, 4 * CH, stride=4), :]
        for b in range(4):
            dense_ref[a + 4 * b] = tmp_ref[pl.ds(b, CH, stride=4), :]


def _scatter16(dense_ref, dst_ref, tmp_ref):
    for a in range(4):
        for b in range(4):
            tmp_ref[pl.ds(b, CH, stride=4), :] = dense_ref[a + 4 * b]
        dst_ref[pl.ds(a, 4 * CH, stride=4), :] = tmp_ref[...]


def _unit_index(r, u, d):
    return (r,) if d == 16 else (_unit_rows(r, u, d), slice(None))


def _unit_kv(p_ref, c_ref, r, u, d):
    prev = p_ref[_unit_index(r, 0, d)] if u == 0 else c_ref[_unit_index(r, u - 1, d)]
    return jnp.concatenate([prev, c_ref[_unit_index(r, u, d)]], axis=0).astype(BF16)


def _dense_scratch(d, n):
    return [pltpu.VMEM((16, CH, LANE), F32)] * n + [pltpu.VMEM((4 * CH, LANE), F32)] if d == 16 else []


def _attn_fwd(proj, d, prior=None, final=False):
    qb = ATTN_QB[d]
    nb, steps, cur, prev, _ = _attn_specs(d, qb)
    n_prior = 2 if prior is not None else 0
    n_in, n_out = 5 + n_prior + final, 2 + final
    assert not (d == 16 and (n_prior or final))

    def body(*refs):
        ins, outs, scratch = refs[:n_in], refs[n_in:n_in + n_out], refs[n_in + n_out:]
        if d == 16:
            tmp_ref = scratch[-1]
            for src, dense in zip(ins, scratch):
                _gather16(src, dense, tmp_ref)
            block_outs, ins, outs = outs, scratch[:n_in], scratch[n_in:n_in + n_out]
        q_ref, kp_ref, kc_ref, vp_ref, vc_ref = ins[:5]
        prior_refs = ins[5:5 + n_prior]
        if final:
            g_ref, (mix_ref, o_ref, l_ref) = ins[-1], outs
        else:
            o_ref, l_ref = outs
        i = pl.program_id(0) % nb
        lane_a = _iota((CH, LANE), 1) < 64
        mask_first, mask_rest = _attn_mask(i > 0), _attn_mask(True)

        def unit(r, u):
            at = _unit_index(r, u, d)
            q2 = _stack_heads(q_ref[at] * 0.125, lane_a)
            k2, v2 = _unit_kv(kp_ref, kc_ref, r, u, d), _unit_kv(vp_ref, vc_ref, r, u, d)
            s = jnp.where(mask_first if u == 0 else mask_rest, _dot_nt(q2, k2), NEG)
            m = jnp.max(s, axis=1, keepdims=True)
            p = jnp.exp(s - m)
            l = jnp.sum(p, axis=1, keepdims=True)
            o2 = _dot(p.astype(BF16), v2) / l
            lse2 = m + jnp.log(l)
            o = jnp.where(lane_a, o2[:CH], o2[CH:])
            lse = jnp.where(lane_a, lse2[:CH], lse2[CH:])
            if n_prior:
                o_a, l_a = prior_refs[0][at], prior_refs[1][at]
                top = jnp.maximum(l_a, lse)
                e_a, e_b = jnp.exp(l_a - top), jnp.exp(lse - top)
                tot = e_a + e_b
                o = (e_a * o_a + e_b * o) / tot
                lse = top + jnp.log(tot)
            o_ref[at] = o
            l_ref[at] = lse
            if final:
                g = g_ref[at]
                mix_ref[at] = (o * (g * _sigmoid(g))).astype(BF16)

        _for_units(d, qb, unit)
        if d == 16:
            for dense, dst in zip(outs, block_outs):
                _scatter16(dense, dst, tmp_ref)

    in_specs = [cur(0), prev(8), cur(8), prev(16), cur(16)] + [cur(0)] * n_prior
    args = [proj] * 5 + (list(prior) if n_prior else [])
    out_specs, out_shape = [cur(0), cur(0)], [SDS((S, D), F32), SDS((S, D), F32)]
    if final:
        assert d == 1
        in_specs.append(cur(OFF_G // LANE))
        args.append(proj)
        out_specs, out_shape = [cur(0)] + out_specs, [SDS((S, 2 * D), BF16)] + out_shape
    return pl.pallas_call(
        body, name=f"attn_fwd_d{d}", grid=(steps,),
        in_specs=in_specs, out_specs=out_specs, out_shape=out_shape,
        scratch_shapes=_dense_scratch(d, n_in + n_out),
        compiler_params=_cp(("parallel",)),
    )(*args)


def _attn_bwd(proj, do, lse, delta, d, acc, out_dtype, hosted=None):
    qb = ATTN_QB[d]
    nb, steps, cur, prev, lag = _attn_specs(d, qb)
    has_acc = acc is not None
    n_in = 11 if has_acc else 8
    n_host = len(hosted.arrays) if hosted else 0
    assert not (d == 16 and (has_acc or out_dtype != F32))
    rows = CH * d * qb
    carry = (2, 16, CH, LANE) if d == 16 else (2, rows, LANE)

    def body(*refs):
        ins, host_in, refs = refs[:n_in], refs[n_in:n_in + n_host], refs[n_in + n_host:]
        (dq_ref, dk_ref, dv_ref), host_out, scratch = refs[:3], refs[3:3 + n_host], refs[3 + n_host:]
        if hosted:
            scratch, host_sems = scratch[:-len(hosted.scratch)], scratch[-len(hosted.scratch):]
        ck_ref, cv_ref = scratch[:2]
        dq_f32 = dq_ref if out_dtype == F32 else scratch[2]
        t = pl.program_id(0)
        i = t % nb
        if hosted:
            pl.when(t == 0)(lambda: hosted.start(host_in, host_out, host_sems))
        if d == 16:
            dense, dq_f32, tmp_ref = scratch[2:2 + n_in], scratch[2 + n_in], scratch[-1]

            @pl.when(t < steps)
            def _():
                for src, dst in zip(ins, dense):
                    _gather16(src, dst, tmp_ref)

            ins = dense
        q_ref, kp_ref, kc_ref, vp_ref, vc_ref, do_ref, lse_ref, dl_ref = ins[:8]
        if has_acc:
            aq_ref, ak_ref, av_ref = ins[8:11]
        slot = t & 1
        now_k, now_v, old_k, old_v = ck_ref.at[slot], cv_ref.at[slot], ck_ref.at[1 - slot], cv_ref.at[1 - slot]
        lane_a = _iota((CH, LANE), 1) < 64
        mask_first, mask_rest = _attn_mask(i > 0), _attn_mask(True)

        @pl.when(t == 0)
        def _():
            ck_ref[1] = jnp.zeros(carry[1:], F32)
            cv_ref[1] = jnp.zeros(carry[1:], F32)

        def unit(r, u):
            at = _unit_index(r, u, d)
            q2 = _stack_heads(q_ref[at] * 0.125, lane_a)
            do2 = _stack_heads(do_ref[at], lane_a)
            k2, v2 = _unit_kv(kp_ref, kc_ref, r, u, d), _unit_kv(vp_ref, vc_ref, r, u, d)
            lsev, dlv = lse_ref[at], dl_ref[at]
            lse2 = jnp.concatenate([lsev[:, 0:1], lsev[:, 64:65]], axis=0)
            dl2 = jnp.concatenate([dlv[:, 0:1], dlv[:, 64:65]], axis=0)
            p = jnp.exp(jnp.where(mask_first if u == 0 else mask_rest, _dot_nt(q2, k2), NEG) - lse2)
            ds = (p * (_dot_nt(do2, v2) - dl2)).astype(BF16)
            dq2 = _dot(ds, k2)
            dk2 = _dot_tn(ds, q2)
            dv2 = _dot_tn(p.astype(BF16), do2)
            dq = jnp.where(lane_a, dq2[:CH], dq2[CH:]) * 0.125
            if has_acc:
                dq = dq + aq_ref[at]
            dq_f32[at] = dq
            if u == 0:
                before = _unit_index(r, qb - 1, d)
                old_k[before] += dk2[:CH]
                old_v[before] += dv2[:CH]
            else:
                before = _unit_index(r, u - 1, d)
                now_k[before] += dk2[:CH]
                now_v[before] += dv2[:CH]
            now_k[at] = dk2[CH:]
            now_v[at] = dv2[CH:]

        @pl.when(t < steps)
        def _():
            _for_units(d, qb, unit)
            if d == 16:
                _scatter16(dq_f32, dq_ref, tmp_ref)
            elif out_dtype != F32:
                dq_ref[...] = dq_f32[...].astype(out_dtype)

        if d == 16:
            _scatter16(old_k, dk_ref, tmp_ref)
            _scatter16(old_v, dv_ref, tmp_ref)
        else:
            dk, dv = old_k[...], old_v[...]
            if has_acc:
                dk, dv = dk + ak_ref[...], dv + av_ref[...]
            dk_ref[...] = dk.astype(out_dtype)
            dv_ref[...] = dv.astype(out_dtype)
        if hosted:
            pl.when(t == steps)(lambda: hosted.finish(host_in, host_out, host_sems))

    in_specs = [cur(0), prev(8), cur(8), prev(16), cur(16), cur(0), cur(0), cur(0)]
    args = [proj, proj, proj, proj, proj, do, lse, delta]
    if has_acc:
        in_specs += [cur(0), lag, lag]
        args += list(acc)
    scratch = [pltpu.VMEM(carry, F32), pltpu.VMEM(carry, F32)]
    if d == 16:
        scratch += _dense_scratch(d, n_in + 1)
    elif out_dtype != F32:
        scratch.append(pltpu.VMEM((rows, LANE), F32))
    out_specs, out_shape = [cur(0), lag, lag], [SDS((S, D), out_dtype)] * 3
    if hosted:
        args += hosted.arrays
        in_specs += [ANY] * n_host
        out_specs += [ANY] * n_host
        out_shape += hosted.out_shape
        scratch += hosted.scratch
    outs = pl.pallas_call(
        body, name=f"attn_bwd_d{d}", grid=(steps + 1,),
        in_specs=in_specs, out_specs=out_specs, out_shape=out_shape,
        scratch_shapes=scratch, compiler_params=_cp(("arbitrary",)),
    )(*args)
    return (outs[:3], outs[3:]) if hosted else outs


def _conv_taps(cur, prev8, first):
    row8 = _iota(prev8.shape, 0)
    prev8 = jnp.where(first, 0.0, prev8)
    taps = []
    for s in (3, 2, 1):
        rolled = pltpu.roll(cur, s, 0)
        head = jnp.where(row8 < s, pltpu.roll(prev8, s, 0), rolled[:8])
        taps.append(jnp.concatenate([head, rolled[8:]], axis=0))
    return taps + [cur]


def _conv(taps, w, b):
    acc = b + w[0:1, :] * taps[0]
    for k in (1, 2, 3):
        acc = acc + w[k:k + 1, :] * taps[k]
    return acc


def _expand():
    return (_iota((LANE, D), 1) // 64 == _iota((LANE, D), 0)).astype(BF16)


def _reduce():
    return (_iota((D, LANE), 0) // 64 == _iota((D, LANE), 1)).astype(BF16)


def _ssd_common(xs_c, bc_c, dt_raw, dtb, alog):
    head_lane = _iota((CH, LANE), 1) < NH
    xs = xs_c * _sigmoid(xs_c)
    bc = bc_c * _sigmoid(bc_c)
    pre = dt_raw + dtb
    dt = jnp.where(head_lane, jnp.maximum(pre, 0.0) + jnp.log(1.0 + jnp.exp(-jnp.abs(pre))), 0.0)
    a_row = jnp.where(head_lane[0:1], -jnp.exp(alog), 0.0)
    tri = (_iota((CH, CH), 1) <= _iota((CH, CH), 0)).astype(BF16)
    cs = _pick_left(tri, dt * a_row)
    cs_last = cs[CH - 1:CH, :]
    wide = _pick(jnp.concatenate([dt, jnp.exp(cs), jnp.exp(cs_last - cs)], axis=0), _expand())
    dt_b, e_b, f_b = wide[:CH], wide[CH:2 * CH], wide[2 * CH:]
    return dict(xs=xs, bc=bc, pre=pre, dt=dt, a_row=a_row, cs=cs, cs_t=cs.T, dt_b=dt_b, e_b=e_b, f_b=f_b,
                t_b=e_b[CH - 1:CH, :])


def _groups(bc):
    bcb = bc.astype(BF16)
    return [bcb[:, 0:128], bcb[:, 128:256]], [bcb[:, 256:384], bcb[:, 384:512]]


def _decay(q, h, tril):
    seg = q["cs"][:, h:h + 1] - q["cs_t"][h:h + 1, :]
    return jnp.exp(jnp.where(tril, seg, NEG))


def _ssm_fwd(proj, mix, cw, cb, dtb, alog, d_b, nw):
    def body(xs_ref, xsp_ref, bc_ref, bcp_ref, dt_ref, z_ref, cw_ref, cb_ref, dtb_ref, alog_ref, db_ref, nw_ref,
             mix_in_ref, mix_ref, y_ref, st_ref, conv_ref, h_ref):
        del mix_in_ref
        i = pl.program_id(0)

        @pl.when(i == 0)
        def _():
            h_ref[...] = jnp.zeros_like(h_ref)

        cw, cb = cw_ref[...], cb_ref[...]
        xs_c = _conv(_conv_taps(xs_ref[...], xsp_ref[...], i == 0), cw[:, :D], cb[:, :D])
        bc_c = _conv(_conv_taps(bc_ref[...], bcp_ref[...], i == 0), cw[:, D:], cb[:, D:])
        conv_ref[:, :D] = xs_c
        conv_ref[:, D:] = bc_c
        q = _ssd_common(xs_c, bc_c, dt_ref[...], dtb_ref[...], alog_ref[...])
        bg, cg = _groups(q["bc"])
        xs = q["xs"]
        xdt = xs * q["dt_b"]
        xdt_b = xdt.astype(BF16)
        h_in = h_ref[...]
        st_ref[...] = h_in
        hb = h_in.astype(BF16)
        tril = _iota((CH, CH), 1) <= _iota((CH, CH), 0)
        lane_a = _iota((CH, LANE), 1) < 64
        cbm = [_dot_nt(cg[g], bg[g]) for g in range(2)]
        pairs = []
        for hp in range(NH // 2):
            xp = xdt_b[:, hp * LANE:(hp + 1) * LANE]
            ya = _dot((cbm[hp // 4] * _decay(q, 2 * hp, tril)).astype(BF16), xp)
            yb = _dot((cbm[hp // 4] * _decay(q, 2 * hp + 1, tril)).astype(BF16), xp)
            pairs.append(jnp.where(lane_a, ya, yb))
        y_diag = jnp.concatenate(pairs, axis=1)
        y_off = jnp.concatenate([_dot(cg[g], hb[:, g * 512:(g + 1) * 512]) for g in range(2)], axis=1) * q["e_b"]
        y = y_diag + y_off + db_ref[...] * xs
        y_ref[...] = y
        xf = (xdt * q["f_b"]).astype(BF16)
        h_ref[...] = q["t_b"] * h_in + jnp.concatenate(
            [_dot_tn(bg[g], xf[:, g * 512:(g + 1) * 512]) for g in range(2)], axis=1)
        z = z_ref[...]
        yz = y * (z * _sigmoid(z))
        outs = []
        for g in range(2):
            v = yz[:, g * 512:(g + 1) * 512]
            outs.append(v * lax.rsqrt(jnp.mean(v * v, axis=-1, keepdims=True) + EPS))
        mix_ref[...] = (jnp.concatenate(outs, axis=1) * nw_ref[...]).astype(BF16)

    def col(width, blk, prev=False):
        if prev:
            return pl.BlockSpec((8, width), lambda i: (jnp.maximum(i * (CH // 8) - 1, 0), blk))
        return pl.BlockSpec((CH, width), lambda i: (i, blk))

    def full(a):
        return pl.BlockSpec(a.shape, lambda i: (0,) * a.ndim)

    return pl.pallas_call(
        body, name="ssm_fwd", grid=(NC,),
        in_specs=[col(D, 5), col(D, 5, True), col(512, 12), col(512, 12, True), col(LANE, 52), col(D, 4),
                  full(cw), full(cb), full(dtb), full(alog), full(d_b), full(nw), ANY],
        out_specs=[col(D, 1), col(D, 0), pl.BlockSpec((None, CH, D), lambda i: (i, 0, 0)), col(D + 512, 0)],
        out_shape=[SDS((S, 2 * D), BF16), SDS((S, D), F32), SDS((NC, CH, D), F32), SDS((S, D + 512), F32)],
        scratch_shapes=[pltpu.VMEM((CH, D), F32)],
        input_output_aliases={12: 0},
        compiler_params=_cp(("arbitrary",)),
    )(proj, proj, proj, proj, proj, proj, cw, cb, dtb, alog, d_b, nw, mix)


def _ssm_bwd(proj, dn, y_save, states, conv_out, cw, dtb, alog, d_b, nw):
    def body(xs_ref, bc_ref, dt_ref, z_ref, dn_ref, y_ref, st_ref, conv_ref,
             cw_ref, dtb_ref, alog_ref, db_ref, nw_ref,
             dz_ref, dx_ref, dcw_ref, dcb_ref, dsm_ref, dnw_ref, dh_ref, nxs_ref, nbc_ref):
        i = pl.program_id(0)
        ci = NC - 1 - i

        @pl.when(i == 0)
        def _():
            for ref in (dcw_ref, dcb_ref, dsm_ref, dnw_ref, dh_ref, nxs_ref, nbc_ref):
                ref[...] = jnp.zeros_like(ref)

        cw = cw_ref[...]
        xs_c, bc_c = conv_ref[:, :D], conv_ref[:, D:]
        q = _ssd_common(xs_c, bc_c, dt_ref[...], dtb_ref[...], alog_ref[...])
        bg, cg = _groups(q["bc"])
        xs, dt_b, e_b, f_b, t_b = q["xs"], q["dt_b"], q["e_b"], q["f_b"], q["t_b"]
        xdt = xs * dt_b
        xdt_b = xdt.astype(BF16)
        h_in = st_ref[...]
        hb = h_in.astype(BF16)
        dh_new = dh_ref[...]
        dhb = dh_new.astype(BF16)
        red = _reduce()

        z, y, dn, nw_v = z_ref[...], y_ref[...], dn_ref[...], nw_ref[...]
        sig = _sigmoid(z)
        sz = z * sig
        yz = y * sz
        gdn = dn * nw_v
        dyz, dnw = [], []
        for g in range(2):
            v, gv = yz[:, g * 512:(g + 1) * 512], gdn[:, g * 512:(g + 1) * 512]
            r = lax.rsqrt(jnp.mean(v * v, axis=-1, keepdims=True) + EPS)
            dnw.append(dn[:, g * 512:(g + 1) * 512] * v * r)
            dyz.append(r * (gv - v * (r * r) * jnp.mean(gv * v, axis=-1, keepdims=True)))
        dyz = jnp.concatenate(dyz, axis=1)
        dnw_ref[...] += jnp.sum(jnp.concatenate(dnw, axis=1), axis=0, keepdims=True)
        dy = dyz * sz
        dz_ref[...] = (dyz * y * (sig * (1.0 + z * (1.0 - sig)))).astype(BF16)
        dy_b = dy.astype(BF16)

        tril = _iota((CH, CH), 1) <= _iota((CH, CH), 0)
        lane_a = _iota((CH, LANE), 1) < 64
        cbm = [_dot_nt(cg[g], bg[g]) for g in range(2)]
        dcbm = [jnp.zeros((CH, CH), F32), jnp.zeros((CH, CH), F32)]
        seg_rows = jnp.zeros((CH, LANE), F32)
        seg_cols = jnp.zeros((LANE, CH), F32)
        row_id, col_id = _iota((CH, LANE), 0), _iota((CH, LANE), 1)
        dx_pairs = []
        for hp in range(NH // 2):
            g = hp // 4
            xp = xdt_b[:, hp * LANE:(hp + 1) * LANE]
            dyp_f = dy[:, hp * LANE:(hp + 1) * LANE]
            dyp = dy_b[:, hp * LANE:(hp + 1) * LANE]
            halves = []
            for k in range(2):
                h = 2 * hp + k
                lane = lane_a if k == 0 else jnp.logical_not(lane_a)
                dec = _decay(q, h, tril)
                gm = cbm[g] * dec
                dgm = _dot_nt(jnp.where(lane, dyp_f, 0.0).astype(BF16), xp)
                dcbm[g] = dcbm[g] + dgm * dec
                prod = dgm * gm
                seg_rows = jnp.where(col_id == h, jnp.sum(prod, axis=1, keepdims=True), seg_rows)
                seg_cols = jnp.where(row_id == h, jnp.sum(prod, axis=0, keepdims=True), seg_cols)
                halves.append(_dot_tn(gm.astype(BF16), dyp))
            dx_pairs.append(jnp.where(lane_a, halves[0], halves[1]))
        dxdt_diag = jnp.concatenate(dx_pairs, axis=1)

        qv = jnp.concatenate([_dot(bg[g], dhb[:, g * 512:(g + 1) * 512]) for g in range(2)], axis=1)
        y_off = jnp.concatenate([_dot(cg[g], hb[:, g * 512:(g + 1) * 512]) for g in range(2)], axis=1) * e_b
        xfq = xdt * f_b * qv
        dxdt = dxdt_diag + f_b * qv
        tdt = jnp.sum(dh_new * h_in, axis=0, keepdims=True) * t_b
        per_head = _pick(jnp.concatenate([xfq, dy * y_off, dxdt * xs, dy * xs, jnp.broadcast_to(tdt, (8, D))],
                                         axis=0), red)
        fdf, dyoff_h, dxdtxs_h, dyxs_h = [per_head[k * CH:(k + 1) * CH] for k in range(4)]
        dcs = seg_rows - seg_cols.T + dyoff_h - fdf
        last = per_head[4 * CH:4 * CH + 1] + jnp.sum(fdf, axis=0, keepdims=True)
        dcs = dcs + jnp.where(_iota((CH, LANE), 0) == CH - 1, last, 0.0)
        tri_t = (_iota((CH, CH), 1) >= _iota((CH, CH), 0)).astype(BF16)
        da = _pick_left(tri_t, dcs)
        ddt = da * q["a_row"] + dxdtxs_h
        dxs = dxdt * dt_b + db_ref[...] * dy
        ddt_raw = ddt * _sigmoid(q["pre"])
        dsm_ref[0:1, :] += jnp.sum(ddt_raw, axis=0, keepdims=True)
        dsm_ref[1:2, :] += jnp.sum(da * q["dt"], axis=0, keepdims=True) * q["a_row"]
        dsm_ref[2:3, :] += jnp.sum(dyxs_h, axis=0, keepdims=True)
        edy = (e_b * dy).astype(BF16)
        xf = (xdt * f_b).astype(BF16)
        dbs, dcs_g, dhs = [], [], []
        for g in range(2):
            sl = slice(g * 512, (g + 1) * 512)
            dcb_b = dcbm[g].astype(BF16)
            dcs_g.append(_dot(dcb_b, bg[g]) + _dot_nt(edy[:, sl], hb[:, sl]))
            dbs.append(_dot_tn(dcb_b, cg[g]) + _dot_nt(xf[:, sl], dhb[:, sl]))
            dhs.append(_dot_tn(cg[g], edy[:, sl]))
        dh_ref[...] = t_b * dh_new + jnp.concatenate(dhs, axis=1)
        dbc = jnp.concatenate(dbs + dcs_g, axis=1)

        def conv_bwd(dact, pre, x_raw, w, nxt_ref, lo):
            s = _sigmoid(pre)
            dconv = dact * (s * (1.0 + pre * (1.0 - s)))
            nxt8 = nxt_ref[...]
            row8 = _iota(nxt8.shape, 0)
            hi = lo + dconv.shape[1]
            dcb_ref[:, lo:hi] += jnp.sum(dconv, axis=0, keepdims=True)
            later = [dconv]
            for s_ in (1, 2, 3):
                rolled = pltpu.roll(dconv, CH - s_, 0)
                tail = jnp.where(row8 >= 8 - s_, pltpu.roll(nxt8, 8 - s_, 0), rolled[CH - 8:])
                later.append(jnp.concatenate([rolled[:CH - 8], tail], axis=0))
            dx = None
            for s_, up in enumerate(later):
                k = 3 - s_
                dcw_ref[k:k + 1, lo:hi] += jnp.sum(up * x_raw, axis=0, keepdims=True)
                dx = w[k:k + 1, :] * up if dx is None else dx + w[k:k + 1, :] * up
            nxt_ref[...] = dconv[:8]
            return dx

        dx_ref[:, 0:D] = conv_bwd(dxs, xs_c, xs_ref[...], cw[:, :D], nxs_ref, 0).astype(BF16)
        dx_ref[:, D:D + 512] = conv_bwd(dbc, bc_c, bc_ref[...], cw[:, D:], nbc_ref, D).astype(BF16)
        dx_ref[:, D + 512:D + 640] = ddt_raw.astype(BF16)
        dx_ref[:, D + 640:] = jnp.zeros((CH, D - 640), BF16)

    def col(width, blk):
        return pl.BlockSpec((CH, width), lambda i: (NC - 1 - i, blk))

    def full(a):
        return pl.BlockSpec(a.shape, lambda i: (0,) * len(a.shape))

    acc_shapes = [SDS((4, 1536), F32), SDS((1, 1536), F32), SDS((8, LANE), F32), SDS((1, D), F32)]
    return pl.pallas_call(
        body, name="ssm_bwd", grid=(NC,),
        in_specs=[col(D, 5), col(512, 12), col(LANE, 52), col(D, 4),
                  col(D, 0), col(D, 0), pl.BlockSpec((None, CH, D), lambda i: (NC - 1 - i, 0, 0)), col(D + 512, 0),
                  full(cw), full(dtb), full(alog), full(d_b), full(nw)],
        out_specs=[col(D, 0), col(2 * D, 0)] + [full(a) for a in acc_shapes],
        out_shape=[SDS((S, D), BF16), SDS((S, 2 * D), BF16)] + acc_shapes,
        scratch_shapes=[pltpu.VMEM((CH, D), F32), pltpu.VMEM((8, D), F32), pltpu.VMEM((8, 512), F32)],
        compiler_params=_cp(("arbitrary",)),
    )(proj, proj, proj, proj, dn, y_save, states, conv_out, cw, dtb, alog, d_b, nw)


def _outproj_loss(mix, w_out, x, tgt, nw, attn_pre, proj):
    tm = 256

    def body(mix_ref, w_ref, x_ref, t_ref, nw_ref, pre_ref, g_ref,
             dy_ref, dn_ref, do_ref, delta_ref, dg_ref, dw_ref, dnw_ref, loss_ref):
        @pl.when(pl.program_id(0) == 0)
        def _():
            dw_ref[...] = jnp.zeros_like(dw_ref)
            dnw_ref[...] = jnp.zeros_like(dnw_ref)
            loss_ref[...] = jnp.zeros_like(loss_ref)

        mixv, w = mix_ref[...], w_ref[...]
        out = _dot(mixv, w)
        r = lax.rsqrt(jnp.mean(out * out, axis=-1, keepdims=True) + EPS)
        nh = out * r
        nw_v = nw_ref[...]
        err = x_ref[...] + nh * nw_v - t_ref[...]
        loss_ref[...] += 0.5 * jnp.sum(jnp.mean(err * err, axis=-1, keepdims=True), axis=0, keepdims=True)
        dy = err * (1.0 / D)
        dy_ref[...] = dy
        dnw_ref[...] += jnp.sum(dy * nh, axis=0, keepdims=True)
        gdn = dy * nw_v
        dout = (r * (gdn - nh * jnp.mean(gdn * nh, axis=-1, keepdims=True))).astype(BF16)
        dmix = _dot_nt(dout, w)
        dw_ref[...] += _dot_tn(mixv, dout)
        dn_ref[...] = dmix[:, D:]
        dm, g, pre_v = dmix[:, :D], g_ref[...], pre_ref[...]
        sig = _sigmoid(g)
        do = dm * (g * sig)
        do_ref[...] = do
        dg_ref[...] = (dm * pre_v * (sig * (1.0 + g * (1.0 - sig)))).astype(BF16)
        prod = do * pre_v
        same_head = (_iota((LANE, LANE), 0) // 64 == _iota((LANE, LANE), 1) // 64).astype(BF16)
        for cb in range(D // LANE):
            delta_ref[:, cb * LANE:(cb + 1) * LANE] = _pick(prod[:, cb * LANE:(cb + 1) * LANE], same_head)

    row = lambda w: pl.BlockSpec((tm, w), lambda i: (i, 0))
    full = lambda s: pl.BlockSpec(s, lambda i: (0, 0))
    return pl.pallas_call(
        body, name="outproj_loss", grid=(S // tm,),
        in_specs=[row(2 * D), full((2 * D, D)), row(D), row(D), full((1, D)), row(D),
                  pl.BlockSpec((tm, D), lambda i: (i, OFF_G // D))],
        out_specs=[row(D), row(D), row(D), row(D), row(D), full((2 * D, D)), full((1, D)), full((1, LANE))],
        out_shape=[SDS((S, D), F32)] * 4 + [SDS((S, D), BF16), SDS((2 * D, D), F32), SDS((1, D), F32),
                                            SDS((1, LANE), F32)],
        compiler_params=_cp(("arbitrary",)),
    )(mix, w_out, x, tgt, nw, attn_pre, proj)


def _inproj_bwd_dx(srcs, dxbcdt, w_all, x, dy, nw, hosted=None):
    tm = 512
    nk = DP // D
    n_host = len(hosted.arrays) if hosted else 0

    def body(*refs):
        src_refs = refs[:nk]
        w_ref, x_ref, dy_ref, nw_ref = refs[nk:nk + 4]
        host_in, refs = refs[nk + 4:nk + 4 + n_host], refs[nk + 4 + n_host:]
        gx_ref, dnw_ref = refs[:2]
        host_out, host_sems = refs[2:2 + n_host], refs[2 + n_host:]
        i = pl.program_id(0)

        @pl.when(i == 0)
        def _():
            if hosted:
                hosted.start(host_in, host_out, host_sems)
            dnw_ref[...] = jnp.zeros_like(dnw_ref)

        du = None
        for k, ref in enumerate(src_refs):
            part = _dot_nt(ref[...], w_ref[:, k * D:(k + 1) * D])
            du = part if du is None else du + part
        xf, nw_v = x_ref[...], nw_ref[...]
        r = lax.rsqrt(jnp.mean(xf * xf, axis=-1, keepdims=True) + EPS)
        xh = xf * r
        dnw_ref[...] += jnp.sum(du * xh, axis=0, keepdims=True)
        gdu = du * nw_v
        gx_ref[...] = r * (gdu - xh * jnp.mean(gdu * xh, axis=-1, keepdims=True)) + dy_ref[...]

        if hosted:
            pl.when(i == S // tm - 1)(lambda: hosted.finish(host_in, host_out, host_sems))

    row = pl.BlockSpec((tm, D), lambda i: (i, 0))
    row1 = pl.BlockSpec((tm, D), lambda i: (i, 1))
    one = pl.BlockSpec((1, D), lambda i: (0, 0))
    whole_w = pl.BlockSpec((D, DP), lambda i: (0, 0), pipeline_mode=pl.Buffered(1))
    args = [*srcs, dxbcdt, dxbcdt, w_all, x, dy, nw]
    in_specs = [row] * len(srcs) + [row, row1, whole_w, row, row, one]
    out_specs, out_shape, scratch = [row, one], [SDS((S, D), F32), SDS((1, D), F32)], []
    if hosted:
        args += hosted.arrays
        in_specs += [ANY] * n_host
        out_specs += [ANY] * n_host
        out_shape += hosted.out_shape
        scratch += hosted.scratch
    outs = pl.pallas_call(
        body, name="inproj_bwd_dx", grid=(S // tm,),
        in_specs=in_specs, out_specs=out_specs, out_shape=out_shape, scratch_shapes=scratch,
        compiler_params=_cp(("arbitrary",)),
    )(*args)
    return (outs[:2], outs[2:]) if hosted else outs


def _dw(u, dsec, name):
    ts = 512
    ncol = dsec.shape[1] // D

    def body(u_ref, d_ref, o_ref):
        @pl.when(pl.program_id(1) == 0)
        def _():
            o_ref[...] = jnp.zeros_like(o_ref)

        o_ref[...] += _dot_tn(u_ref[...], d_ref[...])

    return pl.pallas_call(
        body, name=name, grid=(ncol, S // ts),
        in_specs=[pl.BlockSpec((ts, D), lambda j, i: (i, 0)), pl.BlockSpec((ts, D), lambda j, i: (i, j))],
        out_specs=pl.BlockSpec((D, D), lambda j, i: (0, j)),
        out_shape=SDS((D, ncol * D), F32),
        compiler_params=_cp(("parallel", "arbitrary")),
    )(u, dsec)


def _place():
    x, y, c = lax.axis_index("x"), lax.axis_index("y"), lax.axis_index("c")
    return x, y, c, 2 * x + y


def _chip_of(x, y, k):
    px = 1 - x if k & 2 else x
    py = 1 - y if k & 1 else y
    return px, py, 2 * px + py


def _remote(src, dst, send_sem, recv_sem, dev):
    return pltpu.make_async_remote_copy(src_ref=src, dst_ref=dst, send_sem=send_sem, recv_sem=recv_sem,
                                        device_id=dev, device_id_type=MESH)


def _gather_weights(w_in_b, w_out_b, conv_w):
    def body(win_ref, wout_ref, cw_ref, gin_ref, gout_ref, gcw_ref, send, recv, csend, crecv, osend, orecv):
        x, y, c, j = _place()
        me, sib = (x, y, c), (x, y, 1 - c)
        nbr = {"x": _chip_of(x, y, 2), "y": _chip_of(x, y, 1)}
        diag = _chip_of(x, y, 3)[2]
        conv = [_remote(cw_ref, gcw_ref.at[j], csend.at[k - 1], crecv.at[k - 1], (*_chip_of(x, y, k)[:2], c))
                for k in (1, 2, 3)]
        own = [_remote(src, dst.at[j], osend.at[n], orecv.at[n], sib)
               for n, (src, dst) in enumerate(((win_ref, gin_ref), (wout_ref, gout_ref), (cw_ref, gcw_ref)))]
        for cp in conv + own:
            cp.start()
        started, arrivals = [], []
        pairs = ((win_ref, gin_ref), (wout_ref, gout_ref))

        def rows(src, n_quarter=None, sibling=False):
            half = src.shape[0] // 2
            base = (1 - c if sibling else c) * half
            return pl.ds(base, half) if n_quarter is None else pl.ds(base + n_quarter * (half // 2), half // 2)

        def sem(a, n):
            return send.at[8 * a + n], recv.at[8 * a + n]

        def go(cp):
            cp.start()
            started.append(cp)

        for a, (src, dst) in enumerate(pairs):
            for n, axis in enumerate("xy"):
                px, py, _ = nbr[axis]
                go(_remote(src.at[rows(src)], dst.at[j, rows(src)], *sem(a, n), (px, py, c)))
        for n, axis in enumerate("xy"):
            ox, oy, _ = nbr["y" if axis == "x" else "x"]
            pj = nbr[axis][2]
            for a, (src, dst) in enumerate(pairs):
                _remote(src.at[rows(src)], dst.at[pj, rows(src)], *sem(a, n), me).wait_recv()
                go(_remote(dst.at[pj, rows(src, n)], dst.at[pj, rows(src, n)], *sem(a, 2 + n), (ox, oy, c)))
                go(_remote(dst.at[pj, rows(src)], dst.at[pj, rows(src)], *sem(a, 4 + n), sib))
                arrivals.append(_remote(src.at[rows(src)], dst.at[pj, rows(src, None, True)], *sem(a, 4 + n), me))
        for n in range(2):
            for a, (src, dst) in enumerate(pairs):
                part = rows(src, n)
                _remote(dst.at[diag, part], dst.at[diag, part], *sem(a, 2 + n), me).wait_recv()
                go(_remote(dst.at[diag, part], dst.at[diag, part], *sem(a, 6 + n), sib))
                sib_part = rows(src, n, True)
                arrivals.append(_remote(dst.at[diag, sib_part], dst.at[diag, sib_part], *sem(a, 6 + n), me))
        for cp in arrivals:
            cp.wait_recv()
        for k in (1, 2, 3):
            pj = _chip_of(x, y, k)[2]
            _remote(cw_ref, gcw_ref.at[pj], csend.at[k - 1], crecv.at[k - 1], me).wait_recv()
        for cp in own:
            cp.wait_recv()
        for cp in started + conv + own:
            cp.wait_send()

    return pl.pallas_call(
        body, name="gather_weights",
        in_specs=[ANY, ANY, ANY], out_specs=[ANY, ANY, ANY],
        out_shape=[SDS((4,) + w_in_b.shape, BF16), SDS((4,) + w_out_b.shape, BF16), SDS((4,) + conv_w.shape, F32)],
        scratch_shapes=[pltpu.SemaphoreType.DMA((16,)), pltpu.SemaphoreType.DMA((16,)),
                        pltpu.SemaphoreType.DMA((3,)), pltpu.SemaphoreType.DMA((3,)),
                        pltpu.SemaphoreType.DMA((3,)), pltpu.SemaphoreType.DMA((3,))],
        compiler_params=pltpu.CompilerParams(has_side_effects=True),
    )(w_in_b, w_out_b, conv_w)


def _pair_exchange(arrays, name):
    halves = [a.shape[1] // 2 for a in arrays]
    n = len(arrays)

    def body(*refs):
        x, y, c, _ = _place()
        send, recv = refs[2 * n:]
        cps = [_remote(refs[k].at[:, pl.ds((1 - c) * halves[k], halves[k])], refs[n + k], send.at[k], recv.at[k],
                       (x, y, 1 - c)) for k in range(n)]
        for cp in cps:
            cp.start()
        for cp in cps:
            cp.wait()

    return pl.pallas_call(
        body, name=name, in_specs=[ANY] * n, out_specs=[ANY] * n,
        out_shape=[SDS((a.shape[0], h, a.shape[2]), F32) for a, h in zip(arrays, halves)],
        scratch_shapes=[pltpu.SemaphoreType.DMA((n,)), pltpu.SemaphoreType.DMA((n,))],
        compiler_params=pltpu.CompilerParams(has_side_effects=True),
    )(*arrays)


def _pair_sum(cidx, g, r, name):
    n, half, width = r.shape
    tr = min(half, 256)
    nt = half // tr

    def body(c_ref, g_ref, r_ref, o_ref):
        del c_ref
        o_ref[...] = (g_ref[...] + r_ref[...]).astype(BF16)

    return pl.pallas_call(
        body, name=name,
        grid_spec=pltpu.PrefetchScalarGridSpec(
            num_scalar_prefetch=1, grid=(n, nt),
            in_specs=[pl.BlockSpec((None, tr, width), lambda s, t, c: (s, c[0] * nt + t, 0)),
                      pl.BlockSpec((None, tr, width), lambda s, t, c: (s, t, 0))],
            out_specs=pl.BlockSpec((None, tr, width), lambda s, t, c: (s, t, 0))),
        out_shape=SDS(r.shape, BF16),
        compiler_params=_cp(("parallel", "parallel")),
    )(cidx, g, r)


class _ChipExchange:
    def __init__(self, arrays, rows):
        self.arrays, self.rows = list(arrays), list(rows)
        self.out_shape = [SDS((4,) + a.shape[1:], BF16) for a in self.arrays]
        self.scratch = [pltpu.SemaphoreType.DMA((3 * len(self.arrays),)) for _ in range(2)]

    def _copies(self, ins, outs, sems):
        x, y, c, j = _place()
        send, recv = sems
        for a, (src, dst, row) in enumerate(zip(ins, outs, self.rows)):
            for k in (1, 2, 3):
                px, py, pj = _chip_of(x, y, k)
                n = 3 * a + k - 1
                slot = pj if row is None else py
                yield (None if row is None else px == row, None if row is None else x == row,
                       _remote(src.at[slot], dst.at[j], send.at[n], recv.at[n], (px, py, c)),
                       _remote(src.at[0], dst.at[pj], send.at[n], recv.at[n], (x, y, c)))

    def start(self, ins, outs, sems):
        for sends, _, send, _ in self._copies(ins, outs, sems):
            if sends is None:
                send.start()
            else:
                pl.when(sends)(send.start)

    def finish(self, ins, outs, sems):
        for sends, owns, send, arrival in self._copies(ins, outs, sems):
            if sends is None:
                arrival.wait_recv()
                send.wait_send()
            else:
                pl.when(owns)(arrival.wait_recv)
                pl.when(sends)(send.wait_send)


def _small_exchange(small):
    def body(sm_ref, rs_ref, send, recv, lsem):
        x, y, c, j = _place()
        me = 2 * j + c
        local = pltpu.make_async_copy(sm_ref, rs_ref.at[me], lsem)
        local.start()
        cps = []
        for k in range(1, 8):
            px, py, _ = _chip_of(x, y, k >> 1)
            pc = 1 - c if k & 1 else c
            cps.append(_remote(sm_ref, rs_ref.at[me], send.at[k - 1], recv.at[k - 1], (px, py, pc)))
        for cp in cps:
            cp.start()
        for k in range(1, 8):
            _, _, pj = _chip_of(x, y, k >> 1)
            pc = 1 - c if k & 1 else c
            _remote(sm_ref, rs_ref.at[2 * pj + pc], send.at[k - 1], recv.at[k - 1], (x, y, c)).wait_recv()
        for cp in cps:
            cp.wait_send()
        local.wait()

    return pl.pallas_call(
        body, name="small_exchange", in_specs=[ANY], out_specs=ANY,
        out_shape=SDS((8,) + small.shape, F32),
        scratch_shapes=[pltpu.SemaphoreType.DMA((7,)), pltpu.SemaphoreType.DMA((7,)), pltpu.SemaphoreType.DMA],
        compiler_params=pltpu.CompilerParams(has_side_effects=True),
    )(small)


def _slot_sum(r, name):
    n, rows, width = r.shape
    tr = min(rows, 256)

    def body(r_ref, o_ref):
        acc = r_ref[0].astype(F32)
        for s in range(1, n):
            acc = acc + r_ref[s].astype(F32)
        o_ref[...] = acc

    return pl.pallas_call(
        body, name=name, grid=(rows // tr,),
        in_specs=[pl.BlockSpec((n, tr, width), lambda t: (0, t, 0))],
        out_specs=pl.BlockSpec((tr, width), lambda t: (t, 0)),
        out_shape=SDS((rows, width), F32),
        compiler_params=_cp(("parallel",)),
    )(r)


def _chip_sum(chip_idx, recv, own, name):
    n, rows, width = recv.shape
    tr = min(rows, 256)

    def body(j_ref, r_ref, own_ref, o_ref):
        acc = None
        for s in range(n):
            term = jnp.where(j_ref[0] == s, own_ref[...], r_ref[s]).astype(F32)
            acc = term if acc is None else acc + term
        o_ref[...] = acc

    return pl.pallas_call(
        body, name=name,
        grid_spec=pltpu.PrefetchScalarGridSpec(
            num_scalar_prefetch=1, grid=(rows // tr,),
            in_specs=[pl.BlockSpec((n, tr, width), lambda t, j: (0, t, 0)),
                      pl.BlockSpec((None, tr, width), lambda t, j: (j[0], t, 0))],
            out_specs=pl.BlockSpec((tr, width), lambda t, j: (t, 0))),
        out_shape=SDS((rows, width), F32),
        compiler_params=_cp(("parallel",)),
    )(chip_idx, recv, own)


def _chip_sum_rows(place, recv0, own0, recv1, own1, name):
    n, rows, width = recv0.shape
    tr = min(rows, 256)

    def body(p_ref, r0_ref, o0_ref, r1_ref, o1_ref, o_ref):
        first_row = p_ref[1] == 0
        own = jnp.where(first_row, o0_ref[...], o1_ref[...])
        acc = None
        for s in range(n):
            term = jnp.where(p_ref[0] == s, own, jnp.where(first_row, r0_ref[s], r1_ref[s])).astype(F32)
            acc = term if acc is None else acc + term
        o_ref[...] = acc

    recv = pl.BlockSpec((n, tr, width), lambda t, p: (0, t, 0))
    own = pl.BlockSpec((None, tr, width), lambda t, p: (p[2], t, 0))
    return pl.pallas_call(
        body, name=name,
        grid_spec=pltpu.PrefetchScalarGridSpec(
            num_scalar_prefetch=1, grid=(rows // tr,), in_specs=[recv, own, recv, own],
            out_specs=pl.BlockSpec((tr, width), lambda t, p: (t, 0))),
        out_shape=SDS((rows, width), F32),
        compiler_params=_cp(("parallel",)),
    )(place, recv0, own0, recv1, own1)


def _half_exchange(hw, ho):
    def body(hw_ref, ho_ref, tw_ref, to_ref, send, recv):
        x, y, c, _ = _place()
        sib = (x, y, 1 - c)
        cps = [_remote(hw_ref, tw_ref, send.at[0], recv.at[0], sib),
               _remote(ho_ref, to_ref, send.at[1], recv.at[1], sib)]
        for cp in cps:
            cp.start()
        for cp in cps:
            cp.wait()

    return pl.pallas_call(
        body, name="half_exchange", in_specs=[ANY, ANY], out_specs=[ANY, ANY],
        out_shape=[SDS(hw.shape, F32), SDS(ho.shape, F32)],
        scratch_shapes=[pltpu.SemaphoreType.DMA((2,)), pltpu.SemaphoreType.DMA((2,))],
        compiler_params=pltpu.CompilerParams(has_side_effects=True),
    )(hw, ho)


def _by_core(c, mine, theirs):
    return jnp.where(c == 0, jnp.concatenate([mine, theirs], axis=0), jnp.concatenate([theirs, mine], axis=0))


def _adamw(w, g, m, v, name):
    rows, width = w.shape
    tr = min(rows, 256)

    def body(w_ref, g_ref, m_ref, v_ref, d_ref, nm_ref, nv_ref):
        gv = g_ref[...]
        nm = ADAM_B1 * m_ref[...] + (1.0 - ADAM_B1) * gv
        nv = ADAM_B2 * v_ref[...] + (1.0 - ADAM_B2) * (gv * gv)
        m_hat = nm / (1.0 - ADAM_B1 ** ADAM_STEP)
        v_hat = nv / (1.0 - ADAM_B2 ** ADAM_STEP)
        d_ref[...] = -ADAM_LR * (m_hat / (jnp.sqrt(v_hat) + ADAM_EPS) + ADAM_WD * w_ref[...])
        nm_ref[...] = nm
        nv_ref[...] = nv

    t = pl.BlockSpec((tr, width), lambda i: (i, 0))
    return pl.pallas_call(
        body, name=name, grid=(rows // tr,), in_specs=[t] * 4, out_specs=[t] * 3,
        out_shape=[SDS(w.shape, F32)] * 3, compiler_params=_cp(("parallel",)),
    )(w, g, m, v)


def _rows128(a, rows):
    flat = a.reshape(-1)
    return jnp.pad(flat, (0, rows * LANE - flat.shape[0])).reshape(rows, LANE)


def _pack_small(conv_w, norm_pre, conv_b, ssm_norm, norm_post, dtb, alog, dsk, extra=None):
    cw_rows = 48 if conv_w.shape[-1] == 1536 else 16
    extra = jnp.zeros((1, LANE), F32) if extra is None else _rows128(extra, 1)
    vec = jnp.concatenate([_rows128(dtb, 1), _rows128(alog, 1), _rows128(dsk, 1), extra, jnp.zeros((4, LANE), F32)],
                          axis=0)
    return jnp.concatenate([_rows128(conv_w, cw_rows), _rows128(norm_pre, 8), _rows128(conv_b, 16),
                            _rows128(ssm_norm, 8), _rows128(norm_post, 8), vec], axis=0)


def _unpack_small(p, cw_cols):
    cw_rows = 48 if cw_cols == 1536 else 16
    o = cw_rows
    conv_w = p[:cw_rows].reshape(-1)[:4 * cw_cols].reshape(1, 4, cw_cols)
    norm_pre = p[o:o + 8].reshape(1, D)
    conv_b = p[o + 8:o + 24].reshape(-1)[:1536].reshape(1, 1536)
    ssm_norm = p[o + 24:o + 32].reshape(1, D)
    norm_post = p[o + 32:o + 40].reshape(1, D)
    vec = p[o + 40:o + 48]
    return conv_w, norm_pre, conv_b, ssm_norm, norm_post, vec[0:1, :NH], vec[1:2, :NH], vec[2:3, :NH], vec[3, 0]


def _pad_lanes(a):
    return jnp.pad(a, ((0, 0), (0, LANE - a.shape[1])))


class _GradReduce:
    SPLIT = 2 * SHARD - OFF_G

    def __init__(self, xi, yi, ci):
        self.ci = ci
        self.cidx = jnp.reshape(ci, (1,)).astype(jnp.int32)
        self.place = jnp.stack([2 * xi + yi, xi, yi]).astype(jnp.int32)

    def first(self, dw_g, dw_z, dw_x, dw_out):
        cols = jnp.concatenate([dw_g[:, self.SPLIT:], dw_z, dw_x], axis=1)
        gw = jnp.stack([cols[:, :SHARD], cols[:, SHARD:2 * SHARD]])
        go = dw_out.reshape(4, D // 2, D)
        rw, ro = _pair_exchange([gw, go], "pair_exchange_hi")
        self.pw_hi = _pair_sum(self.cidx, gw, rw, "pair_sum_hi")
        self.po = _pair_sum(self.cidx, go, ro, "pair_sum_out")
        return _ChipExchange([self.pw_hi, self.po], [1, None])

    def first_done(self, got):
        self.rw_hi, self.ro = got

    def second(self, dw_q, dw_k, dw_v, dw_g):
        cols = jnp.concatenate([dw_q, dw_k, dw_v, dw_g[:, :self.SPLIT]], axis=1)
        gw = jnp.stack([cols[:, :SHARD], cols[:, SHARD:]])
        (rw,) = _pair_exchange([gw], "pair_exchange_lo")
        self.pw_lo = _pair_sum(self.cidx, gw, rw, "pair_sum_lo")
        return _ChipExchange([self.pw_lo], [0])

    def second_done(self, got):
        (self.rw_lo,) = got

    def result(self):
        half_in = _chip_sum_rows(self.place, self.rw_lo, self.pw_lo, self.rw_hi, self.pw_hi, "chip_sum_in")
        half_out = _chip_sum(self.place[0:1], self.ro, self.po, "chip_sum_out")
        their_in, their_out = _half_exchange(half_in, half_out)
        return _by_core(self.ci, half_in, their_in), _by_core(self.ci, half_out, their_out)


def kernel(x, norm_pre_w, w_in, conv_w, conv_b, dt_bias, a_log, d_skip, ssm_norm_w, w_out, norm_post_w, loss_target, m_norm_pre_w, m_w_in, m_conv_w, m_conv_b, m_dt_bias, m_a_log, m_d_skip, m_ssm_norm_w, m_w_out, m_norm_post_w, v_norm_pre_w, v_w_in, v_conv_w, v_conv_b, v_dt_bias, v_a_log, v_d_skip, v_ssm_norm_w, v_w_out, v_norm_post_w):
    xi, yi, ci = lax.axis_index("x"), lax.axis_index("y"), lax.axis_index("c")
    chip = 2 * xi + yi
    x2, tgt = x[0], loss_target[0]

    w_in_b, w_out_b = w_in[0].astype(BF16), w_out[0].astype(BF16)
    gin, gout, gcw = _gather_weights(w_in_b, w_out_b, conv_w[0])

    w_all = jnp.concatenate([gin[0], gin[1], gin[2], gin[3], jnp.zeros((D, DP - 4 * SHARD), BF16)], axis=1)
    w_out_all = gout.reshape(2 * D, D)
    cw_all = jnp.concatenate([gcw[0], gcw[1], gcw[2], gcw[3]], axis=1)
    reduce = _GradReduce(xi, yi, ci)
    grad_x, small = _local_step(x2, tgt, w_all, w_out_all, cw_all, norm_pre_w, conv_b, dt_bias, a_log, d_skip,
                                ssm_norm_w, norm_post_w, reduce)[:2]
    g_in, g_out = reduce.result()
    g_small = _slot_sum(_small_exchange(small), "small_sum")
    g_cw, g_npre, g_cb, g_nssm, g_npost, g_dtb, g_alog, g_dsk, loss = _unpack_small(g_small, 1536)
    g_cw = lax.dynamic_slice_in_dim(g_cw, chip * 384, 384, axis=2)

    d_in, nm_in, nv_in = _adamw(w_in[0], g_in, m_w_in[0], v_w_in[0], "adamw_in")
    d_out, nm_out, nv_out = _adamw(w_out[0], g_out, m_w_out[0], v_w_out[0], "adamw_out")
    packed = [_pack_small(*t) for t in (
        (conv_w, norm_pre_w, conv_b, ssm_norm_w, norm_post_w, dt_bias, a_log, d_skip),
        (g_cw, g_npre, g_cb, g_nssm, g_npost, g_dtb, g_alog, g_dsk),
        (m_conv_w, m_norm_pre_w, m_conv_b, m_ssm_norm_w, m_norm_post_w, m_dt_bias, m_a_log, m_d_skip),
        (v_conv_w, v_norm_pre_w, v_conv_b, v_ssm_norm_w, v_norm_post_w, v_dt_bias, v_a_log, v_d_skip))]
    small_out = [_unpack_small(p, 384)[:8] for p in _adamw(*packed, "adamw_small")]

    def ordered(cw_, npre, cb_, nssm, npost, dtb_, alog_, dsk_, big_in, big_out):
        return [npre, big_in[None], cw_, cb_, dtb_, alog_, dsk_, nssm, big_out[None], npost]

    grads = ordered(g_cw, g_npre, g_cb, g_nssm, g_npost, g_dtb, g_alog, g_dsk, g_in, g_out)
    deltas = ordered(*small_out[0], d_in, d_out)
    new_m = ordered(*small_out[1], nm_in, nm_out)
    new_v = ordered(*small_out[2], nv_in, nv_out)
    return (loss, grad_x[None], *grads, *deltas, *new_m, *new_v)


def _local_step(x2, tgt, w_all, w_out_all, cw_all, norm_pre_w, conv_b, dt_bias, a_log, d_skip, ssm_norm_w,
                norm_post_w, reduce=None):
    dtb, alog = _pad_lanes(dt_bias), _pad_lanes(a_log)
    d_b = jnp.repeat(d_skip, 64, axis=1)

    proj, u = _inproj_fwd(x2, norm_pre_w, w_all)
    mix, attn_pre, lse = _attn_fwd(proj, 1, _attn_fwd(proj, 4, _attn_fwd(proj, 16)), final=True)
    mix, y_save, states, conv_out = _ssm_fwd(proj, mix, cw_all, conv_b, dtb, alog, d_b, ssm_norm_w)

    dy, dn_ssm, do, delta, dg, dw_out, dnw_post, loss_part = _outproj_loss(mix, w_out_all, x2, tgt, norm_post_w,
                                                                          attn_pre, proj)
    dz, dxbcdt, dcw, dcb, dvec, dnw_ssm = _ssm_bwd(proj, dn_ssm, y_save, states, conv_out, cw_all, dtb, alog, d_b,
                                                   ssm_norm_w)
    dw_g, dw_z, dw_x = _dw(u, dg, "dw_in_g"), _dw(u, dz, "dw_in_z"), _dw(u, dxbcdt, "dw_in_xbcdt")
    acc = _attn_bwd(proj, do, lse, delta, 16, None, F32, reduce.first(dw_g, dw_z, dw_x, dw_out) if reduce else None)
    if reduce:
        acc, got = acc
        reduce.first_done(got)
    acc = _attn_bwd(proj, do, lse, delta, 4, acc, F32)
    dq, dk, dv = _attn_bwd(proj, do, lse, delta, 1, acc, BF16)
    dw_q, dw_k, dw_v = _dw(u, dq, "dw_in_q"), _dw(u, dk, "dw_in_k"), _dw(u, dv, "dw_in_v")
    res = _inproj_bwd_dx([dq, dk, dv, dg, dz], dxbcdt, w_all, x2, dy, norm_pre_w,
                         reduce.second(dw_q, dw_k, dw_v, dw_g) if reduce else None)
    if reduce:
        res, got = res
        reduce.second_done(got)
    grad_x, dnw_pre = res
    dw_all = jnp.concatenate([dw_q, dw_k, dw_v, dw_g, dw_z, dw_x], axis=1)
    small = _pack_small(dcw, dnw_pre, dcb, dnw_ssm, dnw_post, dvec[0:1, :NH], dvec[1:2, :NH], dvec[2:3, :NH],
                        loss_part[:, :1])
    return grad_x, small, dw_all, dw_out
```

```python
import functools

import jax
import jax.numpy as jnp
from jax import lax
from jax.experimental import pallas as pl
from jax.experimental.pallas import tpu as pltpu

F32 = jnp.float32
BF16 = jnp.bfloat16
MESH = pl.DeviceIdType.MESH
SDS = jax.ShapeDtypeStruct
ANY = pl.BlockSpec(memory_space=pl.ANY)

S = 4096
D = 1024
DP = 7168
SHARD = 1668
OFF_G, OFF_Z = 3072, 4096
NH = 16
CH = 128
NC = S // CH
EPS = 1e-6
NEG = -1e30
LANE = 128
VMEM_LIMIT = 48 * 1024 * 1024

ADAM_LR, ADAM_B1, ADAM_B2, ADAM_EPS, ADAM_WD, ADAM_STEP = 0.001, 0.9, 0.999, 1e-08, 0.01, 10


def _cp(sem, **kw):
    return pltpu.CompilerParams(dimension_semantics=sem, vmem_limit_bytes=VMEM_LIMIT, **kw)


def _dot(a, b):
    return jnp.dot(a, b, preferred_element_type=F32)


def _dot_nt(a, b):
    return lax.dot_general(a, b, (((1,), (1,)), ((), ())), preferred_element_type=F32)


def _dot_tn(a, b):
    return lax.dot_general(a, b, (((0,), (0,)), ((), ())), preferred_element_type=F32)


def _pieces(x, n):
    out = []
    for _ in range(n):
        p = x.astype(BF16)
        out.append(p)
        x = x - p.astype(F32)
    return out


def _pick(x, sel, n=2):
    parts = [_dot(p, sel) for p in _pieces(x, n)]
    return functools.reduce(jnp.add, parts)


def _pick_left(sel, x, n=3):
    parts = [_dot(sel, p) for p in _pieces(x, n)]
    return functools.reduce(jnp.add, parts)


def _sigmoid(v):
    return 0.5 * jnp.tanh(0.5 * v) + 0.5


def _iota(shape, dim):
    return lax.broadcasted_iota(jnp.int32, shape, dim)


def _inproj_fwd(x, nw, w_all, hosted=None):
    tm, tn = 1024, 1024
    n_host = len(hosted.arrays) if hosted else 0

    def body(x_ref, nw_ref, w_ref, *refs):
        host_in, (proj_ref, u_ref), refs = refs[:n_host], refs[n_host:n_host + 2], refs[n_host + 2:]
        host_out, host_sems = refs[:n_host], refs[n_host:]
        i, j = pl.program_id(0), pl.program_id(1)
        if hosted:
            pl.when((i == 0) & (j == 0))(lambda: hosted.start(host_in, host_out, host_sems))

        @pl.when(j == 0)
        def _():
            xf = x_ref[...]
            r = lax.rsqrt(jnp.mean(xf * xf, axis=-1, keepdims=True) + EPS)
            u_ref[...] = (xf * r * nw_ref[...]).astype(BF16)

        proj_ref[...] = _dot(u_ref[...], w_ref[...])
        if hosted:
            pl.when((i == S // tm - 1) & (j == DP // tn - 1))(lambda: hosted.finish(host_in, host_out, host_sems))

    outs = pl.pallas_call(
        body, name="inproj_fwd", grid=(S // tm, DP // tn),
        in_specs=[pl.BlockSpec((tm, D), lambda i, j: (i, 0)), pl.BlockSpec((1, D), lambda i, j: (0, 0)),
                  pl.BlockSpec((D, tn), lambda i, j: (0, j))] + [ANY] * n_host,
        out_specs=[pl.BlockSpec((tm, tn), lambda i, j: (i, j)), pl.BlockSpec((tm, D), lambda i, j: (i, 0))]
        + [ANY] * n_host,
        out_shape=[SDS((S, DP), F32), SDS((S, D), BF16)] + (hosted.out_shape if hosted else []),
        scratch_shapes=hosted.scratch if hosted else [],
        compiler_params=_cp(("arbitrary", "arbitrary") if hosted else ("parallel", "arbitrary")),
    )(x, nw, w_all, *(hosted.arrays if hosted else []))
    return (outs[:2], outs[2:]) if hosted else outs


ATTN_QB = {1: 16, 4: 4, 16: 1}


def _unit_rows(r, u, d):
    return pl.ds(r + d * CH * u, CH, stride=d) if d > 1 else pl.ds(CH * u, CH)


def _for_units(d, qb, fn):
    for r in range(d):
        for u in range(qb):
            fn(r, u)


def _attn_mask(has_prev):
    qi, kj = _iota((2 * CH, 2 * CH), 0) & (CH - 1), _iota((2 * CH, 2 * CH), 1)
    cur_ok = (kj >= CH) & (kj - CH <= qi)
    prev_ok = (kj < CH) & (kj >= qi)
    return cur_ok | (prev_ok & has_prev)


def _stack_heads(v, lane_a):
    return jnp.concatenate([jnp.where(lane_a, v, 0.0), jnp.where(lane_a, 0.0, v)], axis=0).astype(BF16)


def _attn_specs(d, qb):
    rows, prows = CH * d * qb, CH * d
    nb = S // rows
    steps = (NH // 2) * nb

    def at(t):
        t = jnp.minimum(t, steps - 1)
        return t % nb, t // nb

    def cur(off):
        return pl.BlockSpec((rows, LANE), lambda t: (at(t)[0], off + at(t)[1]))

    def prev(off):
        return pl.BlockSpec((prows, LANE), lambda t: (jnp.maximum(at(t)[0] * qb - 1, 0), off + at(t)[1]))

    lag = pl.BlockSpec((rows, LANE), lambda t: at(jnp.maximum(t - 1, 0)))
    return nb, steps, cur, prev, lag


def _gather16(src_ref, dense_ref, tmp_ref):
    for a in range(4):
        tmp_ref[...] = src_ref[pl.ds(a, 4 * CH, stride=4), :]
        for b in range(4):
            dense_ref[a + 4 * b] = tmp_ref[pl.ds(b, CH, stride=4), :]


def _scatter16(dense_ref, dst_ref, tmp_ref):
    for a in range(4):
        for b in range(4):
            tmp_ref[pl.ds(b, CH, stride=4), :] = dense_ref[a + 4 * b]
        dst_ref[pl.ds(a, 4 * CH, stride=4), :] = tmp_ref[...]


def _unit_index(r, u, d):
    return (r,) if d == 16 else (_unit_rows(r, u, d), slice(None))


def _unit_kv(p_ref, c_ref, r, u, d):
    prev = p_ref[_unit_index(r, 0, d)] if u == 0 else c_ref[_unit_index(r, u - 1, d)]
    return jnp.concatenate([prev, c_ref[_unit_index(r, u, d)]], axis=0).astype(BF16)


def _dense_scratch(d, n):
    return [pltpu.VMEM((16, CH, LANE), F32)] * n + [pltpu.VMEM((4 * CH, LANE), F32)] if d == 16 else []


def _attn_fwd(proj, d, prior=None, final=False):
    qb = ATTN_QB[d]
    nb, steps, cur, prev, _ = _attn_specs(d, qb)
    n_prior = 2 if prior is not None else 0
    n_in, n_out = 5 + n_prior + final, 2 + final
    assert not (d == 16 and (n_prior or final))

    def body(*refs):
        ins, outs, scratch = refs[:n_in], refs[n_in:n_in + n_out], refs[n_in + n_out:]
        if d == 16:
            tmp_ref = scratch[-1]
            for src, dense in zip(ins, scratch):
                _gather16(src, dense, tmp_ref)
            block_outs, ins, outs = outs, scratch[:n_in], scratch[n_in:n_in + n_out]
        q_ref, kp_ref, kc_ref, vp_ref, vc_ref = ins[:5]
        prior_refs = ins[5:5 + n_prior]
        if final:
            g_ref, (mix_ref, o_ref, l_ref) = ins[-1], outs
        else:
            o_ref, l_ref = outs
        i = pl.program_id(0) % nb
        lane_a = _iota((CH, LANE), 1) < 64
        mask_first, mask_rest = _attn_mask(i > 0), _attn_mask(True)

        def unit(r, u):
            at = _unit_index(r, u, d)
            q2 = _stack_heads(q_ref[at] * 0.125, lane_a)
            k2, v2 = _unit_kv(kp_ref, kc_ref, r, u, d), _unit_kv(vp_ref, vc_ref, r, u, d)
            s = jnp.where(mask_first if u == 0 else mask_rest, _dot_nt(q2, k2), NEG)
            m = jnp.max(s, axis=1, keepdims=True)
            p = jnp.exp(s - m)
            l = jnp.sum(p, axis=1, keepdims=True)
            o2 = _dot(p.astype(BF16), v2) / l
            lse2 = m + jnp.log(l)
            o = jnp.where(lane_a, o2[:CH], o2[CH:])
            lse = jnp.where(lane_a, lse2[:CH], lse2[CH:])
            if n_prior:
                o_a, l_a = prior_refs[0][at], prior_refs[1][at]
                top = jnp.maximum(l_a, lse)
                e_a, e_b = jnp.exp(l_a - top), jnp.exp(lse - top)
                tot = e_a + e_b
                o = (e_a * o_a + e_b * o) / tot
                lse = top + jnp.log(tot)
            o_ref[at] = o
            l_ref[at] = lse
            if final:
                g = g_ref[at]
                mix_ref[at] = (o * (g * _sigmoid(g))).astype(BF16)

        _for_units(d, qb, unit)
        if d == 16:
            for dense, dst in zip(outs, block_outs):
                _scatter16(dense, dst, tmp_ref)

    in_specs = [cur(0), prev(8), cur(8), prev(16), cur(16)] + [cur(0)] * n_prior
    args = [proj] * 5 + (list(prior) if n_prior else [])
    out_specs, out_shape = [cur(0), cur(0)], [SDS((S, D), F32), SDS((S, D), F32)]
    if final:
        assert d == 1
        in_specs.append(cur(OFF_G // LANE))
        args.append(proj)
        out_specs, out_shape = [cur(0)] + out_specs, [SDS((S, 2 * D), BF16)] + out_shape
    return pl.pallas_call(
        body, name=f"attn_fwd_d{d}", grid=(steps,),
        in_specs=in_specs, out_specs=out_specs, out_shape=out_shape,
        scratch_shapes=_dense_scratch(d, n_in + n_out),
        compiler_params=_cp(("parallel",)),
    )(*args)


def _attn_bwd(proj, do, lse, delta, d, acc, out_dtype, hosted=None):
    qb = ATTN_QB[d]
    nb, steps, cur, prev, lag = _attn_specs(d, qb)
    has_acc = acc is not None
    n_in = 11 if has_acc else 8
    n_host = len(hosted.arrays) if hosted else 0
    assert not (d == 16 and (has_acc or out_dtype != F32))
    rows = CH * d * qb
    carry = (2, 16, CH, LANE) if d == 16 else (2, rows, LANE)

    def body(*refs):
        ins, host_in, refs = refs[:n_in], refs[n_in:n_in + n_host], refs[n_in + n_host:]
        (dq_ref, dk_ref, dv_ref), host_out, scratch = refs[:3], refs[3:3 + n_host], refs[3 + n_host:]
        if hosted:
            scratch, host_sems = scratch[:-len(hosted.scratch)], scratch[-len(hosted.scratch):]
        ck_ref, cv_ref = scratch[:2]
        dq_f32 = dq_ref if out_dtype == F32 else scratch[2]
        t = pl.program_id(0)
        i = t % nb
        if hosted:
            pl.when(t == 0)(lambda: hosted.start(host_in, host_out, host_sems))
        if d == 16:
            dense, dq_f32, tmp_ref = scratch[2:2 + n_in], scratch[2 + n_in], scratch[-1]

            @pl.when(t < steps)
            def _():
                for src, dst in zip(ins, dense):
                    _gather16(src, dst, tmp_ref)

            ins = dense
        q_ref, kp_ref, kc_ref, vp_ref, vc_ref, do_ref, lse_ref, dl_ref = ins[:8]
        if has_acc:
            aq_ref, ak_ref, av_ref = ins[8:11]
        slot = t & 1
        now_k, now_v, old_k, old_v = ck_ref.at[slot], cv_ref.at[slot], ck_ref.at[1 - slot], cv_ref.at[1 - slot]
        lane_a = _iota((CH, LANE), 1) < 64
        mask_first, mask_rest = _attn_mask(i > 0), _attn_mask(True)

        @pl.when(t == 0)
        def _():
            ck_ref[1] = jnp.zeros(carry[1:], F32)
            cv_ref[1] = jnp.zeros(carry[1:], F32)

        def unit(r, u):
            at = _unit_index(r, u, d)
            q2 = _stack_heads(q_ref[at] * 0.125, lane_a)
            do2 = _stack_heads(do_ref[at], lane_a)
            k2, v2 = _unit_kv(kp_ref, kc_ref, r, u, d), _unit_kv(vp_ref, vc_ref, r, u, d)
            lsev, dlv = lse_ref[at], dl_ref[at]
            lse2 = jnp.concatenate([lsev[:, 0:1], lsev[:, 64:65]], axis=0)
            dl2 = jnp.concatenate([dlv[:, 0:1], dlv[:, 64:65]], axis=0)
            p = jnp.exp(jnp.where(mask_first if u == 0 else mask_rest, _dot_nt(q2, k2), NEG) - lse2)
            ds = (p * (_dot_nt(do2, v2) - dl2)).astype(BF16)
            dq2 = _dot(ds, k2)
            dk2 = _dot_tn(ds, q2)
            dv2 = _dot_tn(p.astype(BF16), do2)
            dq = jnp.where(lane_a, dq2[:CH], dq2[CH:]) * 0.125
            if has_acc:
                dq = dq + aq_ref[at]
            dq_f32[at] = dq
            if u == 0:
                before = _unit_index(r, qb - 1, d)
                old_k[before] += dk2[:CH]
                old_v[before] += dv2[:CH]
            else:
                before = _unit_index(r, u - 1, d)
                now_k[before] += dk2[:CH]
                now_v[before] += dv2[:CH]
            now_k[at] = dk2[CH:]
            now_v[at] = dv2[CH:]

        @pl.when(t < steps)
        def _():
            _for_units(d, qb, unit)
            if d == 16:
                _scatter16(dq_f32, dq_ref, tmp_ref)
            elif out_dtype != F32:
                dq_ref[...] = dq_f32[...].astype(out_dtype)

        if d == 16:
            _scatter16(old_k, dk_ref, tmp_ref)
            _scatter16(old_v, dv_ref, tmp_ref)
        else:
            dk, dv = old_k[...], old_v[...]
            if has_acc:
                dk, dv = dk + ak_ref[...], dv + av_ref[...]
            dk_ref[...] = dk.astype(out_dtype)
            dv_ref[...] = dv.astype(out_dtype)
        if hosted:
            pl.when(t == steps)(lambda: hosted.finish(host_in, host_out, host_sems))

    in_specs = [cur(0), prev(8), cur(8), prev(16), cur(16), cur(0), cur(0), cur(0)]
    args = [proj, proj, proj, proj, proj, do, lse, delta]
    if has_acc:
        in_specs += [cur(0), lag, lag]
        args += list(acc)
    scratch = [pltpu.VMEM(carry, F32), pltpu.VMEM(carry, F32)]
    if d == 16:
        scratch += _dense_scratch(d, n_in + 1)
    elif out_dtype != F32:
        scratch.append(pltpu.VMEM((rows, LANE), F32))
    out_specs, out_shape = [cur(0), lag, lag], [SDS((S, D), out_dtype)] * 3
    if hosted:
        args += hosted.arrays
        in_specs += [ANY] * n_host
        out_specs += [ANY] * n_host
        out_shape += hosted.out_shape
        scratch += hosted.scratch
    outs = pl.pallas_call(
        body, name=f"attn_bwd_d{d}", grid=(steps + 1,),
        in_specs=in_specs, out_specs=out_specs, out_shape=out_shape,
        scratch_shapes=scratch, compiler_params=_cp(("arbitrary",)),
    )(*args)
    return (outs[:3], outs[3:]) if hosted else outs


def _conv_taps(cur, prev8, first):
    row8 = _iota(prev8.shape, 0)
    prev8 = jnp.where(first, 0.0, prev8)
    taps = []
    for s in (3, 2, 1):
        rolled = pltpu.roll(cur, s, 0)
        head = jnp.where(row8 < s, pltpu.roll(prev8, s, 0), rolled[:8])
        taps.append(jnp.concatenate([head, rolled[8:]], axis=0))
    return taps + [cur]


def _conv(taps, w, b):
    acc = b + w[0:1, :] * taps[0]
    for k in (1, 2, 3):
        acc = acc + w[k:k + 1, :] * taps[k]
    return acc


def _expand():
    return (_iota((LANE, D), 1) // 64 == _iota((LANE, D), 0)).astype(BF16)


def _reduce():
    return (_iota((D, LANE), 0) // 64 == _iota((D, LANE), 1)).astype(BF16)


def _ssd_common(xs_c, bc_c, dt_raw, dtb, alog):
    head_lane = _iota((CH, LANE), 1) < NH
    xs = xs_c * _sigmoid(xs_c)
    bc = bc_c * _sigmoid(bc_c)
    pre = dt_raw + dtb
    dt = jnp.where(head_lane, jnp.maximum(pre, 0.0) + jnp.log(1.0 + jnp.exp(-jnp.abs(pre))), 0.0)
    a_row = jnp.where(head_lane[0:1], -jnp.exp(alog), 0.0)
    tri = (_iota((CH, CH), 1) <= _iota((CH, CH), 0)).astype(BF16)
    cs = _pick_left(tri, dt * a_row)
    cs_last = cs[CH - 1:CH, :]
    wide = _pick(jnp.concatenate([dt, jnp.exp(cs), jnp.exp(cs_last - cs)], axis=0), _expand())
    dt_b, e_b, f_b = wide[:CH], wide[CH:2 * CH], wide[2 * CH:]
    return dict(xs=xs, bc=bc, pre=pre, dt=dt, a_row=a_row, cs=cs, cs_t=cs.T, dt_b=dt_b, e_b=e_b, f_b=f_b,
                t_b=e_b[CH - 1:CH, :])


def _groups(bc):
    bcb = bc.astype(BF16)
    return [bcb[:, 0:128], bcb[:, 128:256]], [bcb[:, 256:384], bcb[:, 384:512]]


def _decay(q, h, tril):
    seg = q["cs"][:, h:h + 1] - q["cs_t"][h:h + 1, :]
    return jnp.exp(jnp.where(tril, seg, NEG))


def _ssm_fwd(proj, mix, cw, cb, dtb, alog, d_b, nw):
    def body(xs_ref, xsp_ref, bc_ref, bcp_ref, dt_ref, z_ref, cw_ref, cb_ref, dtb_ref, alog_ref, db_ref, nw_ref,
             mix_in_ref, mix_ref, y_ref, st_ref, conv_ref, h_ref):
        del mix_in_ref
        i = pl.program_id(0)

        @pl.when(i == 0)
        def _():
            h_ref[...] = jnp.zeros_like(h_ref)

        cw, cb = cw_ref[...], cb_ref[...]
        xs_c = _conv(_conv_taps(xs_ref[...], xsp_ref[...], i == 0), cw[:, :D], cb[:, :D])
        bc_c = _conv(_conv_taps(bc_ref[...], bcp_ref[...], i == 0), cw[:, D:], cb[:, D:])
        conv_ref[:, :D] = xs_c
        conv_ref[:, D:] = bc_c
        q = _ssd_common(xs_c, bc_c, dt_ref[...], dtb_ref[...], alog_ref[...])
        bg, cg = _groups(q["bc"])
        xs = q["xs"]
        xdt = xs * q["dt_b"]
        xdt_b = xdt.astype(BF16)
        h_in = h_ref[...]
        st_ref[...] = h_in
        hb = h_in.astype(BF16)
        tril = _iota((CH, CH), 1) <= _iota((CH, CH), 0)
        lane_a = _iota((CH, LANE), 1) < 64
        cbm = [_dot_nt(cg[g], bg[g]) for g in range(2)]
        pairs = []
        for hp in range(NH // 2):
            xp = xdt_b[:, hp * LANE:(hp + 1) * LANE]
            ya = _dot((cbm[hp // 4] * _decay(q, 2 * hp, tril)).astype(BF16), xp)
            yb = _dot((cbm[hp // 4] * _decay(q, 2 * hp + 1, tril)).astype(BF16), xp)
            pairs.append(jnp.where(lane_a, ya, yb))
        y_diag = jnp.concatenate(pairs, axis=1)
        y_off = jnp.concatenate([_dot(cg[g], hb[:, g * 512:(g + 1) * 512]) for g in range(2)], axis=1) * q["e_b"]
        y = y_diag + y_off + db_ref[...] * xs
        y_ref[...] = y
        xf = (xdt * q["f_b"]).astype(BF16)
        h_ref[...] = q["t_b"] * h_in + jnp.concatenate(
            [_dot_tn(bg[g], xf[:, g * 512:(g + 1) * 512]) for g in range(2)], axis=1)
        z = z_ref[...]
        yz = y * (z * _sigmoid(z))
        outs = []
        for g in range(2):
            v = yz[:, g * 512:(g + 1) * 512]
            outs.append(v * lax.rsqrt(jnp.mean(v * v, axis=-1, keepdims=True) + EPS))
        mix_ref[...] = (jnp.concatenate(outs, axis=1) * nw_ref[...]).astype(BF16)

    def col(width, blk, prev=False):
        if prev:
            return pl.BlockSpec((8, width), lambda i: (jnp.maximum(i * (CH // 8) - 1, 0), blk))
        return pl.BlockSpec((CH, width), lambda i: (i, blk))

    def full(a):
        return pl.BlockSpec(a.shape, lambda i: (0,) * a.ndim)

    return pl.pallas_call(
        body, name="ssm_fwd", grid=(NC,),
        in_specs=[col(D, 5), col(D, 5, True), col(512, 12), col(512, 12, True), col(LANE, 52), col(D, 4),
                  full(cw), full(cb), full(dtb), full(alog), full(d_b), full(nw), ANY],
        out_specs=[col(D, 1), col(D, 0), pl.BlockSpec((None, CH, D), lambda i: (i, 0, 0)), col(D + 512, 0)],
        out_shape=[SDS((S, 2 * D), BF16), SDS((S, D), F32), SDS((NC, CH, D), F32), SDS((S, D + 512), F32)],
        scratch_shapes=[pltpu.VMEM((CH, D), F32)],
        input_output_aliases={12: 0},
        compiler_params=_cp(("arbitrary",)),
    )(proj, proj, proj, proj, proj, proj, cw, cb, dtb, alog, d_b, nw, mix)


def _ssm_bwd(proj, dn, y_save, states, conv_out, cw, dtb, alog, d_b, nw):
    def body(xs_ref, bc_ref, dt_ref, z_ref, dn_ref, y_ref, st_ref, conv_ref,
             cw_ref, dtb_ref, alog_ref, db_ref, nw_ref,
             dz_ref, dx_ref, dcw_ref, dcb_ref, dsm_ref, dnw_ref, dh_ref, nxs_ref, nbc_ref):
        i = pl.program_id(0)
        ci = NC - 1 - i

        @pl.when(i == 0)
        def _():
            for ref in (dcw_ref, dcb_ref, dsm_ref, dnw_ref, dh_ref, nxs_ref, nbc_ref):
                ref[...] = jnp.zeros_like(ref)

        cw = cw_ref[...]
        xs_c, bc_c = conv_ref[:, :D], conv_ref[:, D:]
        q = _ssd_common(xs_c, bc_c, dt_ref[...], dtb_ref[...], alog_ref[...])
        bg, cg = _groups(q["bc"])
        xs, dt_b, e_b, f_b, t_b = q["xs"], q["dt_b"], q["e_b"], q["f_b"], q["t_b"]
        xdt = xs * dt_b
        xdt_b = xdt.astype(BF16)
        h_in = st_ref[...]
        hb = h_in.astype(BF16)
        dh_new = dh_ref[...]
        dhb = dh_new.astype(BF16)
        red = _reduce()

        z, y, dn, nw_v = z_ref[...], y_ref[...], dn_ref[...], nw_ref[...]
        sig = _sigmoid(z)
        sz = z * sig
        yz = y * sz
        gdn = dn * nw_v
        dyz, dnw = [], []
        for g in range(2):
            v, gv = yz[:, g * 512:(g + 1) * 512], gdn[:, g * 512:(g + 1) * 512]
            r = lax.rsqrt(jnp.mean(v * v, axis=-1, keepdims=True) + EPS)
            dnw.append(dn[:, g * 512:(g + 1) * 512] * v * r)
            dyz.append(r * (gv - v * (r * r) * jnp.mean(gv * v, axis=-1, keepdims=True)))
        dyz = jnp.concatenate(dyz, axis=1)
        dnw_ref[...] += jnp.sum(jnp.concatenate(dnw, axis=1), axis=0, keepdims=True)
        dy = dyz * sz
        dz_ref[...] = (dyz * y * (sig * (1.0 + z * (1.0 - sig)))).astype(BF16)
        dy_b = dy.astype(BF16)

        tril = _iota((CH, CH), 1) <= _iota((CH, CH), 0)
        lane_a = _iota((CH, LANE), 1) < 64
        cbm = [_dot_nt(cg[g], bg[g]) for g in range(2)]
        dcbm = [jnp.zeros((CH, CH), F32), jnp.zeros((CH, CH), F32)]
        seg_rows = jnp.zeros((CH, LANE), F32)
        seg_cols = jnp.zeros((LANE, CH), F32)
        row_id, col_id = _iota((CH, LANE), 0), _iota((CH, LANE), 1)
        dx_pairs = []
        for hp in range(NH // 2):
            g = hp // 4
            xp = xdt_b[:, hp * LANE:(hp + 1) * LANE]
            dyp_f = dy[:, hp * LANE:(hp + 1) * LANE]
            dyp = dy_b[:, hp * LANE:(hp + 1) * LANE]
            halves = []
            for k in range(2):
                h = 2 * hp + k
                lane = lane_a if k == 0 else jnp.logical_not(lane_a)
                dec = _decay(q, h, tril)
                gm = cbm[g] * dec
                dgm = _dot_nt(jnp.where(lane, dyp_f, 0.0).astype(BF16), xp)
                dcbm[g] = dcbm[g] + dgm * dec
                prod = dgm * gm
                seg_rows = jnp.where(col_id == h, jnp.sum(prod, axis=1, keepdims=True), seg_rows)
                seg_cols = jnp.where(row_id == h, jnp.sum(prod, axis=0, keepdims=True), seg_cols)
                halves.append(_dot_tn(gm.astype(BF16), dyp))
            dx_pairs.append(jnp.where(lane_a, halves[0], halves[1]))
        dxdt_diag = jnp.concatenate(dx_pairs, axis=1)

        qv = jnp.concatenate([_dot(bg[g], dhb[:, g * 512:(g + 1) * 512]) for g in range(2)], axis=1)
        y_off = jnp.concatenate([_dot(cg[g], hb[:, g * 512:(g + 1) * 512]) for g in range(2)], axis=1) * e_b
        xfq = xdt * f_b * qv
        dxdt = dxdt_diag + f_b * qv
        tdt = jnp.sum(dh_new * h_in, axis=0, keepdims=True) * t_b
        per_head = _pick(jnp.concatenate([xfq, dy * y_off, dxdt * xs, dy * xs, jnp.broadcast_to(tdt, (8, D))],
                                         axis=0), red)
        fdf, dyoff_h, dxdtxs_h, dyxs_h = [per_head[k * CH:(k + 1) * CH] for k in range(4)]
        dcs = seg_rows - seg_cols.T + dyoff_h - fdf
        last = per_head[4 * CH:4 * CH + 1] + jnp.sum(fdf, axis=0, keepdims=True)
        dcs = dcs + jnp.where(_iota((CH, LANE), 0) == CH - 1, last, 0.0)
        tri_t = (_iota((CH, CH), 1) >= _iota((CH, CH), 0)).astype(BF16)
        da = _pick_left(tri_t, dcs)
        ddt = da * q["a_row"] + dxdtxs_h
        dxs = dxdt * dt_b + db_ref[...] * dy
        ddt_raw = ddt * _sigmoid(q["pre"])
        dsm_ref[0:1, :] += jnp.sum(ddt_raw, axis=0, keepdims=True)
        dsm_ref[1:2, :] += jnp.sum(da * q["dt"], axis=0, keepdims=True) * q["a_row"]
        dsm_ref[2:3, :] += jnp.sum(dyxs_h, axis=0, keepdims=True)
        edy = (e_b * dy).astype(BF16)
        xf = (xdt * f_b).astype(BF16)
        dbs, dcs_g, dhs = [], [], []
        for g in range(2):
            sl = slice(g * 512, (g + 1) * 512)
            dcb_b = dcbm[g].astype(BF16)
            dcs_g.append(_dot(dcb_b, bg[g]) + _dot_nt(edy[:, sl], hb[:, sl]))
            dbs.append(_dot_tn(dcb_b, cg[g]) + _dot_nt(xf[:, sl], dhb[:, sl]))
            dhs.append(_dot_tn(cg[g], edy[:, sl]))
        dh_ref[...] = t_b * dh_new + jnp.concatenate(dhs, axis=1)
        dbc = jnp.concatenate(dbs + dcs_g, axis=1)

        def conv_bwd(dact, pre, x_raw, w, nxt_ref, lo):
            s = _sigmoid(pre)
            dconv = dact * (s * (1.0 + pre * (1.0 - s)))
            nxt8 = nxt_ref[...]
            row8 = _iota(nxt8.shape, 0)
            hi = lo + dconv.shape[1]
            dcb_ref[:, lo:hi] += jnp.sum(dconv, axis=0, keepdims=True)
            later = [dconv]
            for s_ in (1, 2, 3):
                rolled = pltpu.roll(dconv, CH - s_, 0)
                tail = jnp.where(row8 >= 8 - s_, pltpu.roll(nxt8, 8 - s_, 0), rolled[CH - 8:])
                later.append(jnp.concatenate([rolled[:CH - 8], tail], axis=0))
            dx = None
            for s_, up in enumerate(later):
                k = 3 - s_
                dcw_ref[k:k + 1, lo:hi] += jnp.sum(up * x_raw, axis=0, keepdims=True)
                dx = w[k:k + 1, :] * up if dx is None else dx + w[k:k + 1, :] * up
            nxt_ref[...] = dconv[:8]
            return dx

        dx_ref[:, 0:D] = conv_bwd(dxs, xs_c, xs_ref[...], cw[:, :D], nxs_ref, 0).astype(BF16)
        dx_ref[:, D:D + 512] = conv_bwd(dbc, bc_c, bc_ref[...], cw[:, D:], nbc_ref, D).astype(BF16)
        dx_ref[:, D + 512:D + 640] = ddt_raw.astype(BF16)
        dx_ref[:, D + 640:] = jnp.zeros((CH, D - 640), BF16)

    def col(width, blk):
        return pl.BlockSpec((CH, width), lambda i: (NC - 1 - i, blk))

    def full(a):
        return pl.BlockSpec(a.shape, lambda i: (0,) * len(a.shape))

    acc_shapes = [SDS((4, 1536), F32), SDS((1, 1536), F32), SDS((8, LANE), F32), SDS((1, D), F32)]
    return pl.pallas_call(
        body, name="ssm_bwd", grid=(NC,),
        in_specs=[col(D, 5), col(512, 12), col(LANE, 52), col(D, 4),
                  col(D, 0), col(D, 0), pl.BlockSpec((None, CH, D), lambda i: (NC - 1 - i, 0, 0)), col(D + 512, 0),
                  full(cw), full(dtb), full(alog), full(d_b), full(nw)],
        out_specs=[col(D, 0), col(2 * D, 0)] + [full(a) for a in acc_shapes],
        out_shape=[SDS((S, D), BF16), SDS((S, 2 * D), BF16)] + acc_shapes,
        scratch_shapes=[pltpu.VMEM((CH, D), F32), pltpu.VMEM((8, D), F32), pltpu.VMEM((8, 512), F32)],
        compiler_params=_cp(("arbitrary",)),
    )(proj, proj, proj, proj, dn, y_save, states, conv_out, cw, dtb, alog, d_b, nw)


def _outproj_loss(mix, w_out, x, tgt, nw, attn_pre, proj):
    tm = 256

    def body(mix_ref, w_ref, x_ref, t_ref, nw_ref, pre_ref, g_ref,
             dy_ref, dn_ref, do_ref, delta_ref, dg_ref, dw_ref, dnw_ref, loss_ref):
        @pl.when(pl.program_id(0) == 0)
        def _():
            dw_ref[...] = jnp.zeros_like(dw_ref)
            dnw_ref[...] = jnp.zeros_like(dnw_ref)
            loss_ref[...] = jnp.zeros_like(loss_ref)

        mixv, w = mix_ref[...], w_ref[...]
        out = _dot(mixv, w)
        r = lax.rsqrt(jnp.mean(out * out, axis=-1, keepdims=True) + EPS)
        nh = out * r
        nw_v = nw_ref[...]
        err = x_ref[...] + nh * nw_v - t_ref[...]
        loss_ref[...] += 0.5 * jnp.sum(jnp.mean(err * err, axis=-1, keepdims=True), axis=0, keepdims=True)
        dy = err * (1.0 / D)
        dy_ref[...] = dy
        dnw_ref[...] += jnp.sum(dy * nh, axis=0, keepdims=True)
        gdn = dy * nw_v
        dout = (r * (gdn - nh * jnp.mean(gdn * nh, axis=-1, keepdims=True))).astype(BF16)
        dmix = _dot_nt(dout, w)
        dw_ref[...] += _dot_tn(mixv, dout)
        dn_ref[...] = dmix[:, D:]
        dm, g, pre_v = dmix[:, :D], g_ref[...], pre_ref[...]
        sig = _sigmoid(g)
        do = dm * (g * sig)
        do_ref[...] = do
        dg_ref[...] = (dm * pre_v * (sig * (1.0 + g * (1.0 - sig)))).astype(BF16)
        prod = do * pre_v
        same_head = (_iota((LANE, LANE), 0) // 64 == _iota((LANE, LANE), 1) // 64).astype(BF16)
        for cb in range(D // LANE):
            delta_ref[:, cb * LANE:(cb + 1) * LANE] = _pick(prod[:, cb * LANE:(cb + 1) * LANE], same_head)

    row = lambda w: pl.BlockSpec((tm, w), lambda i: (i, 0))
    full = lambda s: pl.BlockSpec(s, lambda i: (0, 0))
    return pl.pallas_call(
        body, name="outproj_loss", grid=(S // tm,),
        in_specs=[row(2 * D), full((2 * D, D)), row(D), row(D), full((1, D)), row(D),
                  pl.BlockSpec((tm, D), lambda i: (i, OFF_G // D))],
        out_specs=[row(D), row(D), row(D), row(D), row(D), full((2 * D, D)), full((1, D)), full((1, LANE))],
        out_shape=[SDS((S, D), F32)] * 4 + [SDS((S, D), BF16), SDS((2 * D, D), F32), SDS((1, D), F32),
                                            SDS((1, LANE), F32)],
        compiler_params=_cp(("arbitrary",)),
    )(mix, w_out, x, tgt, nw, attn_pre, proj)


def _inproj_bwd_dx(srcs, dxbcdt, w_all, x, dy, nw, hosted=None):
    tm = 512
    nk = DP // D
    n_host = len(hosted.arrays) if hosted else 0

    def body(*refs):
        src_refs = refs[:nk]
        w_ref, x_ref, dy_ref, nw_ref = refs[nk:nk + 4]
        host_in, refs = refs[nk + 4:nk + 4 + n_host], refs[nk + 4 + n_host:]
        gx_ref, dnw_ref = refs[:2]
        host_out, host_sems = refs[2:2 + n_host], refs[2 + n_host:]
        i = pl.program_id(0)

        @pl.when(i == 0)
        def _():
            if hosted:
                hosted.start(host_in, host_out, host_sems)
            dnw_ref[...] = jnp.zeros_like(dnw_ref)

        du = None
        for k, ref in enumerate(src_refs):
            part = _dot_nt(ref[...], w_ref[:, k * D:(k + 1) * D])
            du = part if du is None else du + part
        xf, nw_v = x_ref[...], nw_ref[...]
        r = lax.rsqrt(jnp.mean(xf * xf, axis=-1, keepdims=True) + EPS)
        xh = xf * r
        dnw_ref[...] += jnp.sum(du * xh, axis=0, keepdims=True)
        gdu = du * nw_v
        gx_ref[...] = r * (gdu - xh * jnp.mean(gdu * xh, axis=-1, keepdims=True)) + dy_ref[...]

        if hosted:
            pl.when(i == S // tm - 1)(lambda: hosted.finish(host_in, host_out, host_sems))

    row = pl.BlockSpec((tm, D), lambda i: (i, 0))
    row1 = pl.BlockSpec((tm, D), lambda i: (i, 1))
    one = pl.BlockSpec((1, D), lambda i: (0, 0))
    whole_w = pl.BlockSpec((D, DP), lambda i: (0, 0), pipeline_mode=pl.Buffered(1))
    args = [*srcs, dxbcdt, dxbcdt, w_all, x, dy, nw]
    in_specs = [row] * len(srcs) + [row, row1, whole_w, row, row, one]
    out_specs, out_shape, scratch = [row, one], [SDS((S, D), F32), SDS((1, D), F32)], []
    if hosted:
        args += hosted.arrays
        in_specs += [ANY] * n_host
        out_specs += [ANY] * n_host
        out_shape += hosted.out_shape
        scratch += hosted.scratch
    outs = pl.pallas_call(
        body, name="inproj_bwd_dx", grid=(S // tm,),
        in_specs=in_specs, out_specs=out_specs, out_shape=out_shape, scratch_shapes=scratch,
        compiler_params=_cp(("arbitrary",)),
    )(*args)
    return (outs[:2], outs[2:]) if hosted else outs


def _dw(u, dsec, name):
    ts = 512
    ncol = dsec.shape[1] // D

    def body(u_ref, d_ref, o_ref):
        @pl.when(pl.program_id(1) == 0)
        def _():
            o_ref[...] = jnp.zeros_like(o_ref)

        o_ref[...] += _dot_tn(u_ref[...], d_ref[...])

    return pl.pallas_call(
        body, name=name, grid=(ncol, S // ts),
        in_specs=[pl.BlockSpec((ts, D), lambda j, i: (i, 0)), pl.BlockSpec((ts, D), lambda j, i: (i, j))],
        out_specs=pl.BlockSpec((D, D), lambda j, i: (0, j)),
        out_shape=SDS((D, ncol * D), F32),
        compiler_params=_cp(("parallel", "arbitrary")),
    )(u, dsec)


def _place():
    x, y, c = lax.axis_index("x"), lax.axis_index("y"), lax.axis_index("c")
    return x, y, c, 2 * x + y


def _chip_of(x, y, k):
    px = 1 - x if k & 2 else x
    py = 1 - y if k & 1 else y
    return px, py, 2 * px + py


def _remote(src, dst, send_sem, recv_sem, dev):
    return pltpu.make_async_remote_copy(src_ref=src, dst_ref=dst, send_sem=send_sem, recv_sem=recv_sem,
                                        device_id=dev, device_id_type=MESH)


def _gather_weights(w_in_b):
    half = w_in_b.shape[0] // 2
    quarter = half // 2

    def body(src, dst, send, recv):
        x, y, c, j = _place()
        me, sib = (x, y, c), (x, y, 1 - c)
        nbr = {"x": _chip_of(x, y, 2), "y": _chip_of(x, y, 1)}
        diag = _chip_of(x, y, 3)[2]
        started, arrivals = [], []

        def rows(n_quarter=None, sibling=False):
            base = (1 - c if sibling else c) * half
            return pl.ds(base, half) if n_quarter is None else pl.ds(base + n_quarter * quarter, quarter)

        def sem(n):
            return send.at[n], recv.at[n]

        def go(cp):
            cp.start()
            started.append(cp)

        own = _remote(src, dst.at[j], *sem(8), sib)
        go(own)
        for n, axis in enumerate("xy"):
            px, py, _ = nbr[axis]
            go(_remote(src.at[rows()], dst.at[j, rows()], *sem(n), (px, py, c)))
        for n, axis in enumerate("xy"):
            ox, oy, _ = nbr["y" if axis == "x" else "x"]
            pj = nbr[axis][2]
            _remote(src.at[rows()], dst.at[pj, rows()], *sem(n), me).wait_recv()
            go(_remote(dst.at[pj, rows(n)], dst.at[pj, rows(n)], *sem(2 + n), (ox, oy, c)))
            go(_remote(dst.at[pj, rows()], dst.at[pj, rows()], *sem(4 + n), sib))
            arrivals.append(_remote(src.at[rows()], dst.at[pj, rows(None, True)], *sem(4 + n), me))
        for n in range(2):
            _remote(dst.at[diag, rows(n)], dst.at[diag, rows(n)], *sem(2 + n), me).wait_recv()
            go(_remote(dst.at[diag, rows(n)], dst.at[diag, rows(n)], *sem(6 + n), sib))
            arrivals.append(_remote(dst.at[diag, rows(n, True)], dst.at[diag, rows(n, True)], *sem(6 + n), me))
        for cp in arrivals + [own]:
            cp.wait_recv()
        for cp in started:
            cp.wait_send()

    return pl.pallas_call(
        body, name="gather_weights", in_specs=[ANY], out_specs=ANY,
        out_shape=SDS((4,) + w_in_b.shape, BF16),
        scratch_shapes=[pltpu.SemaphoreType.DMA((9,)), pltpu.SemaphoreType.DMA((9,))],
        compiler_params=pltpu.CompilerParams(has_side_effects=True),
    )(w_in_b)


class _LateGather:
    def __init__(self, w_out_b, conv_w):
        self.arrays = [w_out_b, conv_w]
        self.out_shape = [SDS((4,) + w_out_b.shape, BF16), SDS((4,) + conv_w.shape, F32)]
        self.scratch = [pltpu.SemaphoreType.DMA((11,)), pltpu.SemaphoreType.DMA((11,))]

    def _plan(self, ins, outs, sems):
        x, y, c, j = _place()
        send, recv = sems
        (wo, cw), (gwo, gcw) = ins, outs
        half = wo.shape[0] // 2
        mine, theirs = pl.ds(c * half, half), pl.ds((1 - c) * half, half)
        me, sib = (x, y, c), (x, y, 1 - c)
        first, arrive, forward, last = [], [], [], []
        for k in (1, 2, 3):
            px, py, pj = _chip_of(x, y, k)
            first += [_remote(wo.at[mine], gwo.at[j, mine], send.at[k - 1], recv.at[k - 1], (px, py, c)),
                      _remote(cw, gcw.at[j], send.at[k + 2], recv.at[k + 2], (px, py, c))]
            arrive.append(_remote(wo.at[mine], gwo.at[pj, mine], send.at[k - 1], recv.at[k - 1], me))
            forward.append(_remote(gwo.at[pj, mine], gwo.at[pj, mine], send.at[k + 5], recv.at[k + 5], sib))
            last += [_remote(cw, gcw.at[pj], send.at[k + 2], recv.at[k + 2], me),
                     _remote(wo.at[theirs], gwo.at[pj, theirs], send.at[k + 5], recv.at[k + 5], me)]
        first += [_remote(wo, gwo.at[j], send.at[9], recv.at[9], sib),
                  _remote(cw, gcw.at[j], send.at[10], recv.at[10], sib)]
        last += first[-2:]
        return first, arrive, forward, last

    def start(self, ins, outs, sems):
        for cp in self._plan(ins, outs, sems)[0]:
            cp.start()

    def finish(self, ins, outs, sems):
        first, arrive, forward, last = self._plan(ins, outs, sems)
        for got, fwd in zip(arrive, forward):
            got.wait_recv()
            fwd.start()
        for cp in last:
            cp.wait_recv()
        for cp in first + forward:
            cp.wait_send()


class _PairExchange:
    def __init__(self, arrays):
        self.arrays = list(arrays)
        self.out_shape = [SDS((a.shape[0], a.shape[1] // 2, a.shape[2]), F32) for a in self.arrays]
        self.scratch = [pltpu.SemaphoreType.DMA((len(self.arrays),)) for _ in range(2)]

    def _copies(self, ins, outs, sems):
        x, y, c, _ = _place()
        for k, (src, dst) in enumerate(zip(ins, outs)):
            half = src.shape[1] // 2
            yield _remote(src.at[:, pl.ds((1 - c) * half, half)], dst, sems[0].at[k], sems[1].at[k], (x, y, 1 - c))

    def start(self, ins, outs, sems):
        for cp in self._copies(ins, outs, sems):
            cp.start()

    def finish(self, ins, outs, sems):
        for cp in self._copies(ins, outs, sems):
            cp.wait()


def _pair_exchange(arrays, name):
    halves = [a.shape[1] // 2 for a in arrays]
    n = len(arrays)

    def body(*refs):
        x, y, c, _ = _place()
        send, recv = refs[2 * n:]
        cps = [_remote(refs[k].at[:, pl.ds((1 - c) * halves[k], halves[k])], refs[n + k], send.at[k], recv.at[k],
                       (x, y, 1 - c)) for k in range(n)]
        for cp in cps:
            cp.start()
        for cp in cps:
            cp.wait()

    return pl.pallas_call(
        body, name=name, in_specs=[ANY] * n, out_specs=[ANY] * n,
        out_shape=[SDS((a.shape[0], h, a.shape[2]), F32) for a, h in zip(arrays, halves)],
        scratch_shapes=[pltpu.SemaphoreType.DMA((n,)), pltpu.SemaphoreType.DMA((n,))],
        compiler_params=pltpu.CompilerParams(has_side_effects=True),
    )(*arrays)


def _pair_sum(cidx, g, r, name):
    n, half, width = r.shape
    tr = min(half, 256)
    nt = half // tr

    def body(c_ref, g_ref, r_ref, o_ref):
        del c_ref
        o_ref[...] = (g_ref[...] + r_ref[...]).astype(BF16)

    return pl.pallas_call(
        body, name=name,
        grid_spec=pltpu.PrefetchScalarGridSpec(
            num_scalar_prefetch=1, grid=(n, nt),
            in_specs=[pl.BlockSpec((None, tr, width), lambda s, t, c: (s, c[0] * nt + t, 0)),
                      pl.BlockSpec((None, tr, width), lambda s, t, c: (s, t, 0))],
            out_specs=pl.BlockSpec((None, tr, width), lambda s, t, c: (s, t, 0))),
        out_shape=SDS(r.shape, BF16),
        compiler_params=_cp(("parallel", "parallel")),
    )(cidx, g, r)


class _ChipExchange:
    def __init__(self, arrays, rows):
        self.arrays, self.rows = list(arrays), list(rows)
        self.out_shape = [SDS((4,) + a.shape[1:], BF16) for a in self.arrays]
        self.scratch = [pltpu.SemaphoreType.DMA((3 * len(self.arrays),)) for _ in range(2)]

    def _copies(self, ins, outs, sems):
        x, y, c, j = _place()
        send, recv = sems
        for a, (src, dst, row) in enumerate(zip(ins, outs, self.rows)):
            for k in (1, 2, 3):
                px, py, pj = _chip_of(x, y, k)
                n = 3 * a + k - 1
                slot = pj if row is None else py
                yield (None if row is None else px == row, None if row is None else x == row,
                       _remote(src.at[slot], dst.at[j], send.at[n], recv.at[n], (px, py, c)),
                       _remote(src.at[0], dst.at[pj], send.at[n], recv.at[n], (x, y, c)))

    def start(self, ins, outs, sems):
        for sends, _, send, _ in self._copies(ins, outs, sems):
            if sends is None:
                send.start()
            else:
                pl.when(sends)(send.start)

    def finish(self, ins, outs, sems):
        for sends, owns, send, arrival in self._copies(ins, outs, sems):
            if sends is None:
                arrival.wait_recv()
                send.wait_send()
            else:
                pl.when(owns)(arrival.wait_recv)
                pl.when(sends)(send.wait_send)


def _small_exchange(small):
    def body(sm_ref, rs_ref, send, recv, lsem):
        x, y, c, j = _place()
        me = 2 * j + c
        local = pltpu.make_async_copy(sm_ref, rs_ref.at[me], lsem)
        local.start()
        cps = []
        for k in range(1, 8):
            px, py, _ = _chip_of(x, y, k >> 1)
            pc = 1 - c if k & 1 else c
            cps.append(_remote(sm_ref, rs_ref.at[me], send.at[k - 1], recv.at[k - 1], (px, py, pc)))
        for cp in cps:
            cp.start()
        for k in range(1, 8):
            _, _, pj = _chip_of(x, y, k >> 1)
            pc = 1 - c if k & 1 else c
            _remote(sm_ref, rs_ref.at[2 * pj + pc], send.at[k - 1], recv.at[k - 1], (x, y, c)).wait_recv()
        for cp in cps:
            cp.wait_send()
        local.wait()

    return pl.pallas_call(
        body, name="small_exchange", in_specs=[ANY], out_specs=ANY,
        out_shape=SDS((8,) + small.shape, F32),
        scratch_shapes=[pltpu.SemaphoreType.DMA((7,)), pltpu.SemaphoreType.DMA((7,)), pltpu.SemaphoreType.DMA],
        compiler_params=pltpu.CompilerParams(has_side_effects=True),
    )(small)


def _slot_sum(r, name):
    n, rows, width = r.shape
    tr = min(rows, 256)

    def body(r_ref, o_ref):
        acc = r_ref[0].astype(F32)
        for s in range(1, n):
            acc = acc + r_ref[s].astype(F32)
        o_ref[...] = acc

    return pl.pallas_call(
        body, name=name, grid=(rows // tr,),
        in_specs=[pl.BlockSpec((n, tr, width), lambda t: (0, t, 0))],
        out_specs=pl.BlockSpec((tr, width), lambda t: (t, 0)),
        out_shape=SDS((rows, width), F32),
        compiler_params=_cp(("parallel",)),
    )(r)


def _chip_sum(chip_idx, recv, own, name):
    n, rows, width = recv.shape
    tr = min(rows, 256)

    def body(j_ref, r_ref, own_ref, o_ref):
        acc = None
        for s in range(n):
            term = jnp.where(j_ref[0] == s, own_ref[...], r_ref[s]).astype(F32)
            acc = term if acc is None else acc + term
        o_ref[...] = acc

    return pl.pallas_call(
        body, name=name,
        grid_spec=pltpu.PrefetchScalarGridSpec(
            num_scalar_prefetch=1, grid=(rows // tr,),
            in_specs=[pl.BlockSpec((n, tr, width), lambda t, j: (0, t, 0)),
                      pl.BlockSpec((None, tr, width), lambda t, j: (j[0], t, 0))],
            out_specs=pl.BlockSpec((tr, width), lambda t, j: (t, 0))),
        out_shape=SDS((rows, width), F32),
        compiler_params=_cp(("parallel",)),
    )(chip_idx, recv, own)


def _chip_sum_rows(place, recv0, own0, recv1, own1, name):
    n, rows, width = recv0.shape
    tr = min(rows, 256)

    def body(p_ref, r0_ref, o0_ref, r1_ref, o1_ref, o_ref):
        first_row = p_ref[1] == 0
        own = jnp.where(first_row, o0_ref[...], o1_ref[...])
        acc = None
        for s in range(n):
            term = jnp.where(p_ref[0] == s, own, jnp.where(first_row, r0_ref[s], r1_ref[s])).astype(F32)
            acc = term if acc is None else acc + term
        o_ref[...] = acc

    recv = pl.BlockSpec((n, tr, width), lambda t, p: (0, t, 0))
    own = pl.BlockSpec((None, tr, width), lambda t, p: (p[2], t, 0))
    return pl.pallas_call(
        body, name=name,
        grid_spec=pltpu.PrefetchScalarGridSpec(
            num_scalar_prefetch=1, grid=(rows // tr,), in_specs=[recv, own, recv, own],
            out_specs=pl.BlockSpec((tr, width), lambda t, p: (t, 0))),
        out_shape=SDS((rows, width), F32),
        compiler_params=_cp(("parallel",)),
    )(place, recv0, own0, recv1, own1)


def _half_exchange(hw, ho):
    def body(hw_ref, ho_ref, tw_ref, to_ref, send, recv):
        x, y, c, _ = _place()
        sib = (x, y, 1 - c)
        cps = [_remote(hw_ref, tw_ref, send.at[0], recv.at[0], sib),
               _remote(ho_ref, to_ref, send.at[1], recv.at[1], sib)]
        for cp in cps:
            cp.start()
        for cp in cps:
            cp.wait()

    return pl.pallas_call(
        body, name="half_exchange", in_specs=[ANY, ANY], out_specs=[ANY, ANY],
        out_shape=[SDS(hw.shape, F32), SDS(ho.shape, F32)],
        scratch_shapes=[pltpu.SemaphoreType.DMA((2,)), pltpu.SemaphoreType.DMA((2,))],
        compiler_params=pltpu.CompilerParams(has_side_effects=True),
    )(hw, ho)


def _by_core(c, mine, theirs):
    return jnp.where(c == 0, jnp.concatenate([mine, theirs], axis=0), jnp.concatenate([theirs, mine], axis=0))


def _adamw(w, g, m, v, name):
    rows, width = w.shape
    tr = min(rows, 256)

    def body(w_ref, g_ref, m_ref, v_ref, d_ref, nm_ref, nv_ref):
        gv = g_ref[...]
        nm = ADAM_B1 * m_ref[...] + (1.0 - ADAM_B1) * gv
        nv = ADAM_B2 * v_ref[...] + (1.0 - ADAM_B2) * (gv * gv)
        m_hat = nm / (1.0 - ADAM_B1 ** ADAM_STEP)
        v_hat = nv / (1.0 - ADAM_B2 ** ADAM_STEP)
        d_ref[...] = -ADAM_LR * (m_hat / (jnp.sqrt(v_hat) + ADAM_EPS) + ADAM_WD * w_ref[...])
        nm_ref[...] = nm
        nv_ref[...] = nv

    t = pl.BlockSpec((tr, width), lambda i: (i, 0))
    return pl.pallas_call(
        body, name=name, grid=(rows // tr,), in_specs=[t] * 4, out_specs=[t] * 3,
        out_shape=[SDS(w.shape, F32)] * 3, compiler_params=_cp(("parallel",)),
    )(w, g, m, v)


def _rows128(a, rows):
    flat = a.reshape(-1)
    return jnp.pad(flat, (0, rows * LANE - flat.shape[0])).reshape(rows, LANE)


def _pack_small(conv_w, norm_pre, conv_b, ssm_norm, norm_post, dtb, alog, dsk, extra=None):
    cw_rows = 48 if conv_w.shape[-1] == 1536 else 16
    extra = jnp.zeros((1, LANE), F32) if extra is None else _rows128(extra, 1)
    vec = jnp.concatenate([_rows128(dtb, 1), _rows128(alog, 1), _rows128(dsk, 1), extra, jnp.zeros((4, LANE), F32)],
                          axis=0)
    return jnp.concatenate([_rows128(conv_w, cw_rows), _rows128(norm_pre, 8), _rows128(conv_b, 16),
                            _rows128(ssm_norm, 8), _rows128(norm_post, 8), vec], axis=0)


def _unpack_small(p, cw_cols):
    cw_rows = 48 if cw_cols == 1536 else 16
    o = cw_rows
    conv_w = p[:cw_rows].reshape(-1)[:4 * cw_cols].reshape(1, 4, cw_cols)
    norm_pre = p[o:o + 8].reshape(1, D)
    conv_b = p[o + 8:o + 24].reshape(-1)[:1536].reshape(1, 1536)
    ssm_norm = p[o + 24:o + 32].reshape(1, D)
    norm_post = p[o + 32:o + 40].reshape(1, D)
    vec = p[o + 40:o + 48]
    return conv_w, norm_pre, conv_b, ssm_norm, norm_post, vec[0:1, :NH], vec[1:2, :NH], vec[2:3, :NH], vec[3, 0]


def _pad_lanes(a):
    return jnp.pad(a, ((0, 0), (0, LANE - a.shape[1])))


class _GradReduce:
    SPLIT = 2 * SHARD - OFF_G

    def __init__(self, xi, yi, ci):
        self.ci = ci
        self.cidx = jnp.reshape(ci, (1,)).astype(jnp.int32)
        self.place = jnp.stack([2 * xi + yi, xi, yi]).astype(jnp.int32)

    def pairs(self, dw_g, dw_z, dw_x, dw_out):
        cols = jnp.concatenate([dw_g[:, self.SPLIT:], dw_z, dw_x], axis=1)
        self.gw_hi = jnp.stack([cols[:, :SHARD], cols[:, SHARD:2 * SHARD]])
        self.go = dw_out.reshape(4, D // 2, D)
        return _PairExchange([self.gw_hi, self.go])

    def first(self, got):
        rw, ro = got
        self.pw_hi = _pair_sum(self.cidx, self.gw_hi, rw, "pair_sum_hi")
        self.po = _pair_sum(self.cidx, self.go, ro, "pair_sum_out")
        return _ChipExchange([self.pw_hi, self.po], [1, None])

    def first_done(self, got):
        self.rw_hi, self.ro = got

    def second(self, dw_q, dw_k, dw_v, dw_g):
        cols = jnp.concatenate([dw_q, dw_k, dw_v, dw_g[:, :self.SPLIT]], axis=1)
        gw = jnp.stack([cols[:, :SHARD], cols[:, SHARD:]])
        (rw,) = _pair_exchange([gw], "pair_exchange_lo")
        self.pw_lo = _pair_sum(self.cidx, gw, rw, "pair_sum_lo")
        return _ChipExchange([self.pw_lo], [0])

    def second_done(self, got):
        (self.rw_lo,) = got

    def result(self):
        half_in = _chip_sum_rows(self.place, self.rw_lo, self.pw_lo, self.rw_hi, self.pw_hi, "chip_sum_in")
        half_out = _chip_sum(self.place[0:1], self.ro, self.po, "chip_sum_out")
        their_in, their_out = _half_exchange(half_in, half_out)
        return _by_core(self.ci, half_in, their_in), _by_core(self.ci, half_out, their_out)


def kernel(x, norm_pre_w, w_in, conv_w, conv_b, dt_bias, a_log, d_skip, ssm_norm_w, w_out, norm_post_w, loss_target, m_norm_pre_w, m_w_in, m_conv_w, m_conv_b, m_dt_bias, m_a_log, m_d_skip, m_ssm_norm_w, m_w_out, m_norm_post_w, v_norm_pre_w, v_w_in, v_conv_w, v_conv_b, v_dt_bias, v_a_log, v_d_skip, v_ssm_norm_w, v_w_out, v_norm_post_w):
    xi, yi, ci = lax.axis_index("x"), lax.axis_index("y"), lax.axis_index("c")
    chip = 2 * xi + yi
    x2, tgt = x[0], loss_target[0]

    gin = _gather_weights(w_in[0].astype(BF16))
    w_all = jnp.concatenate([gin[0], gin[1], gin[2], gin[3], jnp.zeros((D, DP - 4 * SHARD), BF16)], axis=1)
    reduce = _GradReduce(xi, yi, ci)
    grad_x, small = _local_step(x2, tgt, w_all, _LateGather(w_out[0].astype(BF16), conv_w[0]), norm_pre_w, conv_b,
                                dt_bias, a_log, d_skip, ssm_norm_w, norm_post_w, reduce)[:2]
    g_in, g_out = reduce.result()
    g_small = _slot_sum(_small_exchange(small), "small_sum")
    g_cw, g_npre, g_cb, g_nssm, g_npost, g_dtb, g_alog, g_dsk, loss = _unpack_small(g_small, 1536)
    g_cw = lax.dynamic_slice_in_dim(g_cw, chip * 384, 384, axis=2)

    d_in, nm_in, nv_in = _adamw(w_in[0], g_in, m_w_in[0], v_w_in[0], "adamw_in")
    d_out, nm_out, nv_out = _adamw(w_out[0], g_out, m_w_out[0], v_w_out[0], "adamw_out")
    packed = [_pack_small(*t) for t in (
        (conv_w, norm_pre_w, conv_b, ssm_norm_w, norm_post_w, dt_bias, a_log, d_skip),
        (g_cw, g_npre, g_cb, g_nssm, g_npost, g_dtb, g_alog, g_dsk),
        (m_conv_w, m_norm_pre_w, m_conv_b, m_ssm_norm_w, m_norm_post_w, m_dt_bias, m_a_log, m_d_skip),
        (v_conv_w, v_norm_pre_w, v_conv_b, v_ssm_norm_w, v_norm_post_w, v_dt_bias, v_a_log, v_d_skip))]
    small_out = [_unpack_small(p, 384)[:8] for p in _adamw(*packed, "adamw_small")]

    def ordered(cw_, npre, cb_, nssm, npost, dtb_, alog_, dsk_, big_in, big_out):
        return [npre, big_in[None], cw_, cb_, dtb_, alog_, dsk_, nssm, big_out[None], npost]

    grads = ordered(g_cw, g_npre, g_cb, g_nssm, g_npost, g_dtb, g_alog, g_dsk, g_in, g_out)
    deltas = ordered(*small_out[0], d_in, d_out)
    new_m = ordered(*small_out[1], nm_in, nm_out)
    new_v = ordered(*small_out[2], nv_in, nv_out)
    return (loss, grad_x[None], *grads, *deltas, *new_m, *new_v)


def _local_step(x2, tgt, w_all, late, norm_pre_w, conv_b, dt_bias, a_log, d_skip, ssm_norm_w,
                norm_post_w, reduce=None):
    dtb, alog = _pad_lanes(dt_bias), _pad_lanes(a_log)
    d_b = jnp.repeat(d_skip, 64, axis=1)

    if isinstance(late, _LateGather):
        (proj, u), (gout, gcw) = _inproj_fwd(x2, norm_pre_w, w_all, late)
        w_out_all = gout.reshape(2 * D, D)
        cw_all = jnp.concatenate([gcw[0], gcw[1], gcw[2], gcw[3]], axis=1)
    else:
        proj, u = _inproj_fwd(x2, norm_pre_w, w_all)
        w_out_all, cw_all = late
    mix, attn_pre, lse = _attn_fwd(proj, 1, _attn_fwd(proj, 4, _attn_fwd(proj, 16)), final=True)
    mix, y_save, states, conv_out = _ssm_fwd(proj, mix, cw_all, conv_b, dtb, alog, d_b, ssm_norm_w)

    dy, dn_ssm, do, delta, dg, dw_out, dnw_post, loss_part = _outproj_loss(mix, w_out_all, x2, tgt, norm_post_w,
                                                                          attn_pre, proj)
    dz, dxbcdt, dcw, dcb, dvec, dnw_ssm = _ssm_bwd(proj, dn_ssm, y_save, states, conv_out, cw_all, dtb, alog, d_b,
                                                   ssm_norm_w)
    dw_g, dw_z, dw_x = _dw(u, dg, "dw_in_g"), _dw(u, dz, "dw_in_z"), _dw(u, dxbcdt, "dw_in_xbcdt")
    acc = _attn_bwd(proj, do, lse, delta, 16, None, F32, reduce.pairs(dw_g, dw_z, dw_x, dw_out) if reduce else None)
    if reduce:
        acc, got = acc
    acc = _attn_bwd(proj, do, lse, delta, 4, acc, F32, reduce.first(got) if reduce else None)
    if reduce:
        acc, got = acc
        reduce.first_done(got)
    dq, dk, dv = _attn_bwd(proj, do, lse, delta, 1, acc, BF16)
    dw_q, dw_k, dw_v = _dw(u, dq, "dw_in_q"), _dw(u, dk, "dw_in_k"), _dw(u, dv, "dw_in_v")
    res = _inproj_bwd_dx([dq, dk, dv, dg, dz], dxbcdt, w_all, x2, dy, norm_pre_w,
                         reduce.second(dw_q, dw_k, dw_v, dw_g) if reduce else None)
    if reduce:
        res, got = res
        reduce.second_done(got)
    grad_x, dnw_pre = res
    dw_all = jnp.concatenate([dw_q, dw_k, dw_v, dw_g, dw_z, dw_x], axis=1)
    small = _pack_small(dcw, dnw_pre, dcb, dnw_ssm, dnw_post, dvec[0:1, :NH], dvec[1:2, :NH], dvec[2:3, :NH],
                        loss_part[:, :1])
    return grad_x, small, dw_all, dw_out
```

```python
import functools

import jax
import jax.numpy as jnp
from jax import lax
from jax.experimental import pallas as pl
from jax.experimental.pallas import tpu as pltpu

F32 = jnp.float32
BF16 = jnp.bfloat16
MESH = pl.DeviceIdType.MESH
SDS = jax.ShapeDtypeStruct
ANY = pl.BlockSpec(memory_space=pl.ANY)

S = 4096
D = 1024
DP = 7168
SHARD = 1668
OFF_G, OFF_Z = 3072, 4096
NH = 16
CH = 128
NC = S // CH
EPS = 1e-6
NEG = -1e30
LANE = 128
VMEM_LIMIT = 48 * 1024 * 1024

ADAM_LR, ADAM_B1, ADAM_B2, ADAM_EPS, ADAM_WD, ADAM_STEP = 0.001, 0.9, 0.999, 1e-08, 0.01, 10


def _cp(sem, **kw):
    return pltpu.CompilerParams(dimension_semantics=sem, vmem_limit_bytes=VMEM_LIMIT, **kw)


def _dot(a, b):
    return jnp.dot(a, b, preferred_element_type=F32)


def _dot_nt(a, b):
    return lax.dot_general(a, b, (((1,), (1,)), ((), ())), preferred_element_type=F32)


def _dot_tn(a, b):
    return lax.dot_general(a, b, (((0,), (0,)), ((), ())), preferred_element_type=F32)


def _pieces(x, n):
    out = []
    for _ in range(n):
        p = x.astype(BF16)
        out.append(p)
        x = x - p.astype(F32)
    return out


def _pick(x, sel, n=2):
    parts = [_dot(p, sel) for p in _pieces(x, n)]
    return functools.reduce(jnp.add, parts)


def _pick_left(sel, x, n=3):
    parts = [_dot(sel, p) for p in _pieces(x, n)]
    return functools.reduce(jnp.add, parts)


def _sigmoid(v):
    return 0.5 * jnp.tanh(0.5 * v) + 0.5


def _iota(shape, dim):
    return lax.broadcasted_iota(jnp.int32, shape, dim)


def _inproj_fwd(x, nw, w_all, hosted=None):
    tm, tn = 1024, 1024
    n_host = len(hosted.arrays) if hosted else 0

    def body(x_ref, nw_ref, w_ref, *refs):
        host_in, (proj_ref, u_ref), refs = refs[:n_host], refs[n_host:n_host + 2], refs[n_host + 2:]
        host_out, host_sems = refs[:n_host], refs[n_host:]
        i, j = pl.program_id(0), pl.program_id(1)
        if hosted:
            pl.when((i == 0) & (j == 0))(lambda: hosted.start(host_in, host_out, host_sems))

        @pl.when(j == 0)
        def _():
            xf = x_ref[...]
            r = lax.rsqrt(jnp.mean(xf * xf, axis=-1, keepdims=True) + EPS)
            u_ref[...] = (xf * r * nw_ref[...]).astype(BF16)

        proj_ref[...] = _dot(u_ref[...], w_ref[...])
        if hosted:
            pl.when((i == S // tm - 1) & (j == DP // tn - 1))(lambda: hosted.finish(host_in, host_out, host_sems))

    outs = pl.pallas_call(
        body, name="inproj_fwd", grid=(S // tm, DP // tn),
        in_specs=[pl.BlockSpec((tm, D), lambda i, j: (i, 0)), pl.BlockSpec((1, D), lambda i, j: (0, 0)),
                  pl.BlockSpec((D, tn), lambda i, j: (0, j))] + [ANY] * n_host,
        out_specs=[pl.BlockSpec((tm, tn), lambda i, j: (i, j)), pl.BlockSpec((tm, D), lambda i, j: (i, 0))]
        + [ANY] * n_host,
        out_shape=[SDS((S, DP), F32), SDS((S, D), BF16)] + (hosted.out_shape if hosted else []),
        scratch_shapes=hosted.scratch if hosted else [],
        compiler_params=_cp(("arbitrary", "arbitrary") if hosted else ("parallel", "arbitrary")),
    )(x, nw, w_all, *(hosted.arrays if hosted else []))
    return (outs[:2], outs[2:]) if hosted else outs


ATTN_QB = {1: 16, 4: 4, 16: 1}


def _unit_rows(r, u, d):
    return pl.ds(r + d * CH * u, CH, stride=d) if d > 1 else pl.ds(CH * u, CH)


def _for_units(d, qb, fn):
    for r in range(d):
        for u in range(qb):
            fn(r, u)


def _attn_mask(has_prev):
    qi, kj = _iota((2 * CH, 2 * CH), 0) & (CH - 1), _iota((2 * CH, 2 * CH), 1)
    cur_ok = (kj >= CH) & (kj - CH <= qi)
    prev_ok = (kj < CH) & (kj >= qi)
    return cur_ok | (prev_ok & has_prev)


def _stack_heads(v, lane_a):
    return jnp.concatenate([jnp.where(lane_a, v, 0.0), jnp.where(lane_a, 0.0, v)], axis=0).astype(BF16)


def _attn_specs(d, qb):
    rows, prows = CH * d * qb, CH * d
    nb = S // rows
    steps = (NH // 2) * nb

    def at(t):
        t = jnp.minimum(t, steps - 1)
        return t % nb, t // nb

    def cur(off):
        return pl.BlockSpec((rows, LANE), lambda t: (at(t)[0], off + at(t)[1]))

    def prev(off):
        return pl.BlockSpec((prows, LANE), lambda t: (jnp.maximum(at(t)[0] * qb - 1, 0), off + at(t)[1]))

    lag = pl.BlockSpec((rows, LANE), lambda t: at(jnp.maximum(t - 1, 0)))
    return nb, steps, cur, prev, lag


def _gather16(src_ref, dense_ref, tmp_ref):
    for a in range(4):
        tmp_ref[...] = src_ref[pl.ds(a, 4 * CH, stride=4), :]
        for b in range(4):
            dense_ref[a + 4 * b] = tmp_ref[pl.ds(b, CH, stride=4), :]


def _scatter16(dense_ref, dst_ref, tmp_ref):
    for a in range(4):
        for b in range(4):
            tmp_ref[pl.ds(b, CH, stride=4), :] = dense_ref[a + 4 * b]
        dst_ref[pl.ds(a, 4 * CH, stride=4), :] = tmp_ref[...]


def _unit_index(r, u, d):
    return (r,) if d == 16 else (_unit_rows(r, u, d), slice(None))


def _unit_kv(p_ref, c_ref, r, u, d):
    prev = p_ref[_unit_index(r, 0, d)] if u == 0 else c_ref[_unit_index(r, u - 1, d)]
    return jnp.concatenate([prev, c_ref[_unit_index(r, u, d)]], axis=0).astype(BF16)


def _dense_scratch(d, n):
    return [pltpu.VMEM((16, CH, LANE), F32)] * n + [pltpu.VMEM((4 * CH, LANE), F32)] if d == 16 else []


def _attn_fwd(proj, d, prior=None, final=False):
    qb = ATTN_QB[d]
    nb, steps, cur, prev, _ = _attn_specs(d, qb)
    n_prior = 2 if prior is not None else 0
    n_in, n_out = 5 + n_prior + final, 2 + final
    assert not (d == 16 and (n_prior or final))

    def body(*refs):
        ins, outs, scratch = refs[:n_in], refs[n_in:n_in + n_out], refs[n_in + n_out:]
        if d == 16:
            tmp_ref = scratch[-1]
            for src, dense in zip(ins, scratch):
                _gather16(src, dense, tmp_ref)
            block_outs, ins, outs = outs, scratch[:n_in], scratch[n_in:n_in + n_out]
        q_ref, kp_ref, kc_ref, vp_ref, vc_ref = ins[:5]
        prior_refs = ins[5:5 + n_prior]
        if final:
            g_ref, (mix_ref, o_ref, l_ref) = ins[-1], outs
        else:
            o_ref, l_ref = outs
        i = pl.program_id(0) % nb
        lane_a = _iota((CH, LANE), 1) < 64
        mask_first, mask_rest = _attn_mask(i > 0), _attn_mask(True)

        def unit(r, u):
            at = _unit_index(r, u, d)
            q2 = _stack_heads(q_ref[at] * 0.125, lane_a)
            k2, v2 = _unit_kv(kp_ref, kc_ref, r, u, d), _unit_kv(vp_ref, vc_ref, r, u, d)
            s = jnp.where(mask_first if u == 0 else mask_rest, _dot_nt(q2, k2), NEG)
            m = jnp.max(s, axis=1, keepdims=True)
            p = jnp.exp(s - m)
            l = jnp.sum(p, axis=1, keepdims=True)
            o2 = _dot(p.astype(BF16), v2) / l
            lse2 = m + jnp.log(l)
            o = jnp.where(lane_a, o2[:CH], o2[CH:])
            lse = jnp.where(lane_a, lse2[:CH], lse2[CH:])
            if n_prior:
                o_a, l_a = prior_refs[0][at], prior_refs[1][at]
                top = jnp.maximum(l_a, lse)
                e_a, e_b = jnp.exp(l_a - top), jnp.exp(lse - top)
                tot = e_a + e_b
                o = (e_a * o_a + e_b * o) / tot
                lse = top + jnp.log(tot)
            o_ref[at] = o
            l_ref[at] = lse
            if final:
                g = g_ref[at]
                mix_ref[at] = (o * (g * _sigmoid(g))).astype(BF16)

        _for_units(d, qb, unit)
        if d == 16:
            for dense, dst in zip(outs, block_outs):
                _scatter16(dense, dst, tmp_ref)

    in_specs = [cur(0), prev(8), cur(8), prev(16), cur(16)] + [cur(0)] * n_prior
    args = [proj] * 5 + (list(prior) if n_prior else [])
    out_specs, out_shape = [cur(0), cur(0)], [SDS((S, D), F32), SDS((S, D), F32)]
    if final:
        assert d == 1
        in_specs.append(cur(OFF_G // LANE))
        args.append(proj)
        out_specs, out_shape = [cur(0)] + out_specs, [SDS((S, 2 * D), BF16)] + out_shape
    return pl.pallas_call(
        body, name=f"attn_fwd_d{d}", grid=(steps,),
        in_specs=in_specs, out_specs=out_specs, out_shape=out_shape,
        scratch_shapes=_dense_scratch(d, n_in + n_out),
        compiler_params=_cp(("parallel",)),
    )(*args)


def _attn_bwd(proj, do, lse, delta, d, acc, out_dtype, hosted=None):
    qb = ATTN_QB[d]
    nb, steps, cur, prev, lag = _attn_specs(d, qb)
    has_acc = acc is not None
    n_in = 11 if has_acc else 8
    n_host = len(hosted.arrays) if hosted else 0
    assert not (d == 16 and (has_acc or out_dtype != F32))
    rows = CH * d * qb
    carry = (2, 16, CH, LANE) if d == 16 else (2, rows, LANE)

    def body(*refs):
        ins, host_in, refs = refs[:n_in], refs[n_in:n_in + n_host], refs[n_in + n_host:]
        (dq_ref, dk_ref, dv_ref), host_out, scratch = refs[:3], refs[3:3 + n_host], refs[3 + n_host:]
        if hosted:
            scratch, host_sems = scratch[:-len(hosted.scratch)], scratch[-len(hosted.scratch):]
        ck_ref, cv_ref = scratch[:2]
        dq_f32 = dq_ref if out_dtype == F32 else scratch[2]
        t = pl.program_id(0)
        i = t % nb
        if hosted:
            pl.when(t == 0)(lambda: hosted.start(host_in, host_out, host_sems))
        if d == 16:
            dense, dq_f32, tmp_ref = scratch[2:2 + n_in], scratch[2 + n_in], scratch[-1]

            @pl.when(t < steps)
            def _():
                for src, dst in zip(ins, dense):
                    _gather16(src, dst, tmp_ref)

            ins = dense
        q_ref, kp_ref, kc_ref, vp_ref, vc_ref, do_ref, lse_ref, dl_ref = ins[:8]
        if has_acc:
            aq_ref, ak_ref, av_ref = ins[8:11]
        slot = t & 1
        now_k, now_v, old_k, old_v = ck_ref.at[slot], cv_ref.at[slot], ck_ref.at[1 - slot], cv_ref.at[1 - slot]
        lane_a = _iota((CH, LANE), 1) < 64
        mask_first, mask_rest = _attn_mask(i > 0), _attn_mask(True)

        @pl.when(t == 0)
        def _():
            ck_ref[1] = jnp.zeros(carry[1:], F32)
            cv_ref[1] = jnp.zeros(carry[1:], F32)

        def unit(r, u):
            at = _unit_index(r, u, d)
            q2 = _stack_heads(q_ref[at] * 0.125, lane_a)
            do2 = _stack_heads(do_ref[at], lane_a)
            k2, v2 = _unit_kv(kp_ref, kc_ref, r, u, d), _unit_kv(vp_ref, vc_ref, r, u, d)
            lsev, dlv = lse_ref[at], dl_ref[at]
            lse2 = jnp.concatenate([lsev[:, 0:1], lsev[:, 64:65]], axis=0)
            dl2 = jnp.concatenate([dlv[:, 0:1], dlv[:, 64:65]], axis=0)
            p = jnp.exp(jnp.where(mask_first if u == 0 else mask_rest, _dot_nt(q2, k2), NEG) - lse2)
            ds = (p * (_dot_nt(do2, v2) - dl2)).astype(BF16)
            dq2 = _dot(ds, k2)
            dk2 = _dot_tn(ds, q2)
            dv2 = _dot_tn(p.astype(BF16), do2)
            dq = jnp.where(lane_a, dq2[:CH], dq2[CH:]) * 0.125
            if has_acc:
                dq = dq + aq_ref[at]
            dq_f32[at] = dq
            if u == 0:
                before = _unit_index(r, qb - 1, d)
                old_k[before] += dk2[:CH]
                old_v[before] += dv2[:CH]
            else:
                before = _unit_index(r, u - 1, d)
                now_k[before] += dk2[:CH]
                now_v[before] += dv2[:CH]
            now_k[at] = dk2[CH:]
            now_v[at] = dv2[CH:]

        @pl.when(t < steps)
        def _():
            _for_units(d, qb, unit)
            if d == 16:
                _scatter16(dq_f32, dq_ref, tmp_ref)
            elif out_dtype != F32:
                dq_ref[...] = dq_f32[...].astype(out_dtype)

        if d == 16:
            _scatter16(old_k, dk_ref, tmp_ref)
            _scatter16(old_v, dv_ref, tmp_ref)
        else:
            dk, dv = old_k[...], old_v[...]
            if has_acc:
                dk, dv = dk + ak_ref[...], dv + av_ref[...]
            dk_ref[...] = dk.astype(out_dtype)
            dv_ref[...] = dv.astype(out_dtype)
        if hosted:
            pl.when(t == steps)(lambda: hosted.finish(host_in, host_out, host_sems))

    in_specs = [cur(0), prev(8), cur(8), prev(16), cur(16), cur(0), cur(0), cur(0)]
    args = [proj, proj, proj, proj, proj, do, lse, delta]
    if has_acc:
        in_specs += [cur(0), lag, lag]
        args += list(acc)
    scratch = [pltpu.VMEM(carry, F32), pltpu.VMEM(carry, F32)]
    if d == 16:
        scratch += _dense_scratch(d, n_in + 1)
    elif out_dtype != F32:
        scratch.append(pltpu.VMEM((rows, LANE), F32))
    out_specs, out_shape = [cur(0), lag, lag], [SDS((S, D), out_dtype)] * 3
    if hosted:
        args += hosted.arrays
        in_specs += [ANY] * n_host
        out_specs += [ANY] * n_host
        out_shape += hosted.out_shape
        scratch += hosted.scratch
    outs = pl.pallas_call(
        body, name=f"attn_bwd_d{d}", grid=(steps + 1,),
        in_specs=in_specs, out_specs=out_specs, out_shape=out_shape,
        scratch_shapes=scratch, compiler_params=_cp(("arbitrary",)),
    )(*args)
    return (outs[:3], outs[3:]) if hosted else outs


def _conv_taps(cur, prev8, first):
    row8 = _iota(prev8.shape, 0)
    prev8 = jnp.where(first, 0.0, prev8)
    taps = []
    for s in (3, 2, 1):
        rolled = pltpu.roll(cur, s, 0)
        head = jnp.where(row8 < s, pltpu.roll(prev8, s, 0), rolled[:8])
        taps.append(jnp.concatenate([head, rolled[8:]], axis=0))
    return taps + [cur]


def _conv(taps, w, b):
    acc = b + w[0:1, :] * taps[0]
    for k in (1, 2, 3):
        acc = acc + w[k:k + 1, :] * taps[k]
    return acc


def _expand():
    return (_iota((LANE, D), 1) // 64 == _iota((LANE, D), 0)).astype(BF16)


def _reduce():
    return (_iota((D, LANE), 0) // 64 == _iota((D, LANE), 1)).astype(BF16)


def _ssd_common(xs_c, bc_c, dt_raw, dtb, alog):
    head_lane = _iota((CH, LANE), 1) < NH
    xs = xs_c * _sigmoid(xs_c)
    bc = bc_c * _sigmoid(bc_c)
    pre = dt_raw + dtb
    dt = jnp.where(head_lane, jnp.maximum(pre, 0.0) + jnp.log(1.0 + jnp.exp(-jnp.abs(pre))), 0.0)
    a_row = jnp.where(head_lane[0:1], -jnp.exp(alog), 0.0)
    tri = (_iota((CH, CH), 1) <= _iota((CH, CH), 0)).astype(BF16)
    cs = _pick_left(tri, dt * a_row)
    cs_last = cs[CH - 1:CH, :]
    wide = _pick(jnp.concatenate([dt, jnp.exp(cs), jnp.exp(cs_last - cs)], axis=0), _expand())
    dt_b, e_b, f_b = wide[:CH], wide[CH:2 * CH], wide[2 * CH:]
    return dict(xs=xs, bc=bc, pre=pre, dt=dt, a_row=a_row, cs=cs, cs_t=cs.T, dt_b=dt_b, e_b=e_b, f_b=f_b,
                t_b=e_b[CH - 1:CH, :])


def _groups(bc):
    bcb = bc.astype(BF16)
    return [bcb[:, 0:128], bcb[:, 128:256]], [bcb[:, 256:384], bcb[:, 384:512]]


def _decay(q, h, tril):
    seg = q["cs"][:, h:h + 1] - q["cs_t"][h:h + 1, :]
    return jnp.exp(jnp.where(tril, seg, NEG))


def _ssm_fwd(proj, mix, cw, cb, dtb, alog, d_b, nw):
    def body(xs_ref, xsp_ref, bc_ref, bcp_ref, dt_ref, z_ref, cw_ref, cb_ref, dtb_ref, alog_ref, db_ref, nw_ref,
             mix_in_ref, mix_ref, y_ref, st_ref, conv_ref, h_ref):
        del mix_in_ref
        i = pl.program_id(0)

        @pl.when(i == 0)
        def _():
            h_ref[...] = jnp.zeros_like(h_ref)

        cw, cb = cw_ref[...], cb_ref[...]
        xs_c = _conv(_conv_taps(xs_ref[...], xsp_ref[...], i == 0), cw[:, :D], cb[:, :D])
        bc_c = _conv(_conv_taps(bc_ref[...], bcp_ref[...], i == 0), cw[:, D:], cb[:, D:])
        conv_ref[:, :D] = xs_c
        conv_ref[:, D:] = bc_c
        q = _ssd_common(xs_c, bc_c, dt_ref[...], dtb_ref[...], alog_ref[...])
        bg, cg = _groups(q["bc"])
        xs = q["xs"]
        xdt = xs * q["dt_b"]
        xdt_b = xdt.astype(BF16)
        h_in = h_ref[...]
        st_ref[...] = h_in
        hb = h_in.astype(BF16)
        tril = _iota((CH, CH), 1) <= _iota((CH, CH), 0)
        lane_a = _iota((CH, LANE), 1) < 64
        cbm = [_dot_nt(cg[g], bg[g]) for g in range(2)]
        pairs = []
        for hp in range(NH // 2):
            xp = xdt_b[:, hp * LANE:(hp + 1) * LANE]
            ya = _dot((cbm[hp // 4] * _decay(q, 2 * hp, tril)).astype(BF16), xp)
            yb = _dot((cbm[hp // 4] * _decay(q, 2 * hp + 1, tril)).astype(BF16), xp)
            pairs.append(jnp.where(lane_a, ya, yb))
        y_diag = jnp.concatenate(pairs, axis=1)
        y_off = jnp.concatenate([_dot(cg[g], hb[:, g * 512:(g + 1) * 512]) for g in range(2)], axis=1) * q["e_b"]
        y = y_diag + y_off + db_ref[...] * xs
        y_ref[...] = y
        xf = (xdt * q["f_b"]).astype(BF16)
        h_ref[...] = q["t_b"] * h_in + jnp.concatenate(
            [_dot_tn(bg[g], xf[:, g * 512:(g + 1) * 512]) for g in range(2)], axis=1)
        z = z_ref[...]
        yz = y * (z * _sigmoid(z))
        outs = []
        for g in range(2):
            v = yz[:, g * 512:(g + 1) * 512]
            outs.append(v * lax.rsqrt(jnp.mean(v * v, axis=-1, keepdims=True) + EPS))
        mix_ref[...] = (jnp.concatenate(outs, axis=1) * nw_ref[...]).astype(BF16)

    def col(width, blk, prev=False):
        if prev:
            return pl.BlockSpec((8, width), lambda i: (jnp.maximum(i * (CH // 8) - 1, 0), blk))
        return pl.BlockSpec((CH, width), lambda i: (i, blk))

    def full(a):
        return pl.BlockSpec(a.shape, lambda i: (0,) * a.ndim)

    return pl.pallas_call(
        body, name="ssm_fwd", grid=(NC,),
        in_specs=[col(D, 5), col(D, 5, True), col(512, 12), col(512, 12, True), col(LANE, 52), col(D, 4),
                  full(cw), full(cb), full(dtb), full(alog), full(d_b), full(nw), ANY],
        out_specs=[col(D, 1), col(D, 0), pl.BlockSpec((None, CH, D), lambda i: (i, 0, 0)), col(D + 512, 0)],
        out_shape=[SDS((S, 2 * D), BF16), SDS((S, D), F32), SDS((NC, CH, D), F32), SDS((S, D + 512), F32)],
        scratch_shapes=[pltpu.VMEM((CH, D), F32)],
        input_output_aliases={12: 0},
        compiler_params=_cp(("arbitrary",)),
    )(proj, proj, proj, proj, proj, proj, cw, cb, dtb, alog, d_b, nw, mix)


def _ssm_bwd(proj, dn, y_save, states, conv_out, cw, dtb, alog, d_b, nw):
    def body(xs_ref, bc_ref, dt_ref, z_ref, dn_ref, y_ref, st_ref, conv_ref,
             cw_ref, dtb_ref, alog_ref, db_ref, nw_ref,
             dz_ref, dx_ref, dcw_ref, dcb_ref, dsm_ref, dnw_ref, dh_ref, nxs_ref, nbc_ref):
        i = pl.program_id(0)
        ci = NC - 1 - i

        @pl.when(i == 0)
        def _():
            for ref in (dcw_ref, dcb_ref, dsm_ref, dnw_ref, dh_ref, nxs_ref, nbc_ref):
                ref[...] = jnp.zeros_like(ref)

        cw = cw_ref[...]
        xs_c, bc_c = conv_ref[:, :D], conv_ref[:, D:]
        q = _ssd_common(xs_c, bc_c, dt_ref[...], dtb_ref[...], alog_ref[...])
        bg, cg = _groups(q["bc"])
        xs, dt_b, e_b, f_b, t_b = q["xs"], q["dt_b"], q["e_b"], q["f_b"], q["t_b"]
        xdt = xs * dt_b
        xdt_b = xdt.astype(BF16)
        h_in = st_ref[...]
        hb = h_in.astype(BF16)
        dh_new = dh_ref[...]
        dhb = dh_new.astype(BF16)
        red = _reduce()

        z, y, dn, nw_v = z_ref[...], y_ref[...], dn_ref[...], nw_ref[...]
        sig = _sigmoid(z)
        sz = z * sig
        yz = y * sz
        gdn = dn * nw_v
        dyz, dnw = [], []
        for g in range(2):
            v, gv = yz[:, g * 512:(g + 1) * 512], gdn[:, g * 512:(g + 1) * 512]
            r = lax.rsqrt(jnp.mean(v * v, axis=-1, keepdims=True) + EPS)
            dnw.append(dn[:, g * 512:(g + 1) * 512] * v * r)
            dyz.append(r * (gv - v * (r * r) * jnp.mean(gv * v, axis=-1, keepdims=True)))
        dyz = jnp.concatenate(dyz, axis=1)
        dnw_ref[...] += jnp.sum(jnp.concatenate(dnw, axis=1), axis=0, keepdims=True)
        dy = dyz * sz
        dz_ref[...] = (dyz * y * (sig * (1.0 + z * (1.0 - sig)))).astype(BF16)
        dy_b = dy.astype(BF16)

        tril = _iota((CH, CH), 1) <= _iota((CH, CH), 0)
        lane_a = _iota((CH, LANE), 1) < 64
        cbm = [_dot_nt(cg[g], bg[g]) for g in range(2)]
        dcbm = [jnp.zeros((CH, CH), F32), jnp.zeros((CH, CH), F32)]
        seg_rows = jnp.zeros((CH, LANE), F32)
        seg_cols = jnp.zeros((LANE, CH), F32)
        row_id, col_id = _iota((CH, LANE), 0), _iota((CH, LANE), 1)
        dx_pairs = []
        for hp in range(NH // 2):
            g = hp // 4
            xp = xdt_b[:, hp * LANE:(hp + 1) * LANE]
            dyp_f = dy[:, hp * LANE:(hp + 1) * LANE]
            dyp = dy_b[:, hp * LANE:(hp + 1) * LANE]
            halves = []
            for k in range(2):
                h = 2 * hp + k
                lane = lane_a if k == 0 else jnp.logical_not(lane_a)
                dec = _decay(q, h, tril)
                gm = cbm[g] * dec
                dgm = _dot_nt(jnp.where(lane, dyp_f, 0.0).astype(BF16), xp)
                dcbm[g] = dcbm[g] + dgm * dec
                prod = dgm * gm
                seg_rows = jnp.where(col_id == h, jnp.sum(prod, axis=1, keepdims=True), seg_rows)
                seg_cols = jnp.where(row_id == h, jnp.sum(prod, axis=0, keepdims=True), seg_cols)
                halves.append(_dot_tn(gm.astype(BF16), dyp))
            dx_pairs.append(jnp.where(lane_a, halves[0], halves[1]))
        dxdt_diag = jnp.concatenate(dx_pairs, axis=1)

        qv = jnp.concatenate([_dot(bg[g], dhb[:, g * 512:(g + 1) * 512]) for g in range(2)], axis=1)
        y_off = jnp.concatenate([_dot(cg[g], hb[:, g * 512:(g + 1) * 512]) for g in range(2)], axis=1) * e_b
        xfq = xdt * f_b * qv
        dxdt = dxdt_diag + f_b * qv
        tdt = jnp.sum(dh_new * h_in, axis=0, keepdims=True) * t_b
        per_head = _pick(jnp.concatenate([xfq, dy * y_off, dxdt * xs, dy * xs, jnp.broadcast_to(tdt, (8, D))],
                                         axis=0), red)
        fdf, dyoff_h, dxdtxs_h, dyxs_h = [per_head[k * CH:(k + 1) * CH] for k in range(4)]
        dcs = seg_rows - seg_cols.T + dyoff_h - fdf
        last = per_head[4 * CH:4 * CH + 1] + jnp.sum(fdf, axis=0, keepdims=True)
        dcs = dcs + jnp.where(_iota((CH, LANE), 0) == CH - 1, last, 0.0)
        tri_t = (_iota((CH, CH), 1) >= _iota((CH, CH), 0)).astype(BF16)
        da = _pick_left(tri_t, dcs)
        ddt = da * q["a_row"] + dxdtxs_h
        dxs = dxdt * dt_b + db_ref[...] * dy
        ddt_raw = ddt * _sigmoid(q["pre"])
        dsm_ref[0:1, :] += jnp.sum(ddt_raw, axis=0, keepdims=True)
        dsm_ref[1:2, :] += jnp.sum(da * q["dt"], axis=0, keepdims=True) * q["a_row"]
        dsm_ref[2:3, :] += jnp.sum(dyxs_h, axis=0, keepdims=True)
        edy = (e_b * dy).astype(BF16)
        xf = (xdt * f_b).astype(BF16)
        dbs, dcs_g, dhs = [], [], []
        for g in range(2):
            sl = slice(g * 512, (g + 1) * 512)
            dcb_b = dcbm[g].astype(BF16)
            dcs_g.append(_dot(dcb_b, bg[g]) + _dot_nt(edy[:, sl], hb[:, sl]))
            dbs.append(_dot_tn(dcb_b, cg[g]) + _dot_nt(xf[:, sl], dhb[:, sl]))
            dhs.append(_dot_tn(cg[g], edy[:, sl]))
        dh_ref[...] = t_b * dh_new + jnp.concatenate(dhs, axis=1)
        dbc = jnp.concatenate(dbs + dcs_g, axis=1)

        def conv_bwd(dact, pre, x_raw, w, nxt_ref, lo):
            s = _sigmoid(pre)
            dconv = dact * (s * (1.0 + pre * (1.0 - s)))
            nxt8 = nxt_ref[...]
            row8 = _iota(nxt8.shape, 0)
            hi = lo + dconv.shape[1]
            dcb_ref[:, lo:hi] += jnp.sum(dconv, axis=0, keepdims=True)
            later = [dconv]
            for s_ in (1, 2, 3):
                rolled = pltpu.roll(dconv, CH - s_, 0)
                tail = jnp.where(row8 >= 8 - s_, pltpu.roll(nxt8, 8 - s_, 0), rolled[CH - 8:])
                later.append(jnp.concatenate([rolled[:CH - 8], tail], axis=0))
            dx = None
            for s_, up in enumerate(later):
                k = 3 - s_
                dcw_ref[k:k + 1, lo:hi] += jnp.sum(up * x_raw, axis=0, keepdims=True)
                dx = w[k:k + 1, :] * up if dx is None else dx + w[k:k + 1, :] * up
            nxt_ref[...] = dconv[:8]
            return dx

        dx_ref[:, 0:D] = conv_bwd(dxs, xs_c, xs_ref[...], cw[:, :D], nxs_ref, 0).astype(BF16)
        dx_ref[:, D:D + 512] = conv_bwd(dbc, bc_c, bc_ref[...], cw[:, D:], nbc_ref, D).astype(BF16)
        dx_ref[:, D + 512:D + 640] = ddt_raw.astype(BF16)
        dx_ref[:, D + 640:] = jnp.zeros((CH, D - 640), BF16)

    def col(width, blk):
        return pl.BlockSpec((CH, width), lambda i: (NC - 1 - i, blk))

    def full(a):
        return pl.BlockSpec(a.shape, lambda i: (0,) * len(a.shape))

    acc_shapes = [SDS((4, 1536), F32), SDS((1, 1536), F32), SDS((8, LANE), F32), SDS((1, D), F32)]
    return pl.pallas_call(
        body, name="ssm_bwd", grid=(NC,),
        in_specs=[col(D, 5), col(512, 12), col(LANE, 52), col(D, 4),
                  col(D, 0), col(D, 0), pl.BlockSpec((None, CH, D), lambda i: (NC - 1 - i, 0, 0)), col(D + 512, 0),
                  full(cw), full(dtb), full(alog), full(d_b), full(nw)],
        out_specs=[col(D, 0), col(2 * D, 0)] + [full(a) for a in acc_shapes],
        out_shape=[SDS((S, D), BF16), SDS((S, 2 * D), BF16)] + acc_shapes,
        scratch_shapes=[pltpu.VMEM((CH, D), F32), pltpu.VMEM((8, D), F32), pltpu.VMEM((8, 512), F32)],
        compiler_params=_cp(("arbitrary",)),
    )(proj, proj, proj, proj, dn, y_save, states, conv_out, cw, dtb, alog, d_b, nw)


def _outproj_loss(mix, w_out, x, tgt, nw, attn_pre, proj):
    tm = 256

    def body(mix_ref, w_ref, x_ref, t_ref, nw_ref, pre_ref, g_ref,
             dy_ref, dn_ref, do_ref, delta_ref, dg_ref, dw_ref, dnw_ref, loss_ref):
        @pl.when(pl.program_id(0) == 0)
        def _():
            dw_ref[...] = jnp.zeros_like(dw_ref)
            dnw_ref[...] = jnp.zeros_like(dnw_ref)
            loss_ref[...] = jnp.zeros_like(loss_ref)

        mixv, w = mix_ref[...], w_ref[...]
        out = _dot(mixv, w)
        r = lax.rsqrt(jnp.mean(out * out, axis=-1, keepdims=True) + EPS)
        nh = out * r
        nw_v = nw_ref[...]
        err = x_ref[...] + nh * nw_v - t_ref[...]
        loss_ref[...] += 0.5 * jnp.sum(jnp.mean(err * err, axis=-1, keepdims=True), axis=0, keepdims=True)
        dy = err * (1.0 / D)
        dy_ref[...] = dy
        dnw_ref[...] += jnp.sum(dy * nh, axis=0, keepdims=True)
        gdn = dy * nw_v
        dout = (r * (gdn - nh * jnp.mean(gdn * nh, axis=-1, keepdims=True))).astype(BF16)
        dmix = _dot_nt(dout, w)
        dw_ref[...] += _dot_tn(mixv, dout)
        dn_ref[...] = dmix[:, D:]
        dm, g, pre_v = dmix[:, :D], g_ref[...], pre_ref[...]
        sig = _sigmoid(g)
        do = dm * (g * sig)
        do_ref[...] = do
        dg_ref[...] = (dm * pre_v * (sig * (1.0 + g * (1.0 - sig)))).astype(BF16)
        prod = do * pre_v
        same_head = (_iota((LANE, LANE), 0) // 64 == _iota((LANE, LANE), 1) // 64).astype(BF16)
        for cb in range(D // LANE):
            delta_ref[:, cb * LANE:(cb + 1) * LANE] = _pick(prod[:, cb * LANE:(cb + 1) * LANE], same_head)

    row = lambda w: pl.BlockSpec((tm, w), lambda i: (i, 0))
    full = lambda s: pl.BlockSpec(s, lambda i: (0, 0))
    return pl.pallas_call(
        body, name="outproj_loss", grid=(S // tm,),
        in_specs=[row(2 * D), full((2 * D, D)), row(D), row(D), full((1, D)), row(D),
                  pl.BlockSpec((tm, D), lambda i: (i, OFF_G // D))],
        out_specs=[row(D), row(D), row(D), row(D), row(D), full((2 * D, D)), full((1, D)), full((1, LANE))],
        out_shape=[SDS((S, D), F32)] * 4 + [SDS((S, D), BF16), SDS((2 * D, D), F32), SDS((1, D), F32),
                                            SDS((1, LANE), F32)],
        compiler_params=_cp(("arbitrary",)),
    )(mix, w_out, x, tgt, nw, attn_pre, proj)


def _inproj_bwd_dx(srcs, dxbcdt, w_all, x, dy, nw, hosted=None):
    tm = 512
    nk = DP // D
    n_host = len(hosted.arrays) if hosted else 0

    def body(*refs):
        src_refs = refs[:nk]
        w_ref, x_ref, dy_ref, nw_ref = refs[nk:nk + 4]
        host_in, refs = refs[nk + 4:nk + 4 + n_host], refs[nk + 4 + n_host:]
        gx_ref, dnw_ref = refs[:2]
        host_out, host_sems = refs[2:2 + n_host], refs[2 + n_host:]
        i = pl.program_id(0)

        @pl.when(i == 0)
        def _():
            if hosted:
                hosted.start(host_in, host_out, host_sems)
            dnw_ref[...] = jnp.zeros_like(dnw_ref)

        du = None
        for k, ref in enumerate(src_refs):
            part = _dot_nt(ref[...], w_ref[:, k * D:(k + 1) * D])
            du = part if du is None else du + part
        xf, nw_v = x_ref[...], nw_ref[...]
        r = lax.rsqrt(jnp.mean(xf * xf, axis=-1, keepdims=True) + EPS)
        xh = xf * r
        dnw_ref[...] += jnp.sum(du * xh, axis=0, keepdims=True)
        gdu = du * nw_v
        gx_ref[...] = r * (gdu - xh * jnp.mean(gdu * xh, axis=-1, keepdims=True)) + dy_ref[...]

        if hosted:
            pl.when(i == S // tm - 1)(lambda: hosted.finish(host_in, host_out, host_sems))

    row = pl.BlockSpec((tm, D), lambda i: (i, 0))
    row1 = pl.BlockSpec((tm, D), lambda i: (i, 1))
    one = pl.BlockSpec((1, D), lambda i: (0, 0))
    whole_w = pl.BlockSpec((D, DP), lambda i: (0, 0), pipeline_mode=pl.Buffered(1))
    args = [*srcs, dxbcdt, dxbcdt, w_all, x, dy, nw]
    in_specs = [row] * len(srcs) + [row, row1, whole_w, row, row, one]
    out_specs, out_shape, scratch = [row, one], [SDS((S, D), F32), SDS((1, D), F32)], []
    if hosted:
        args += hosted.arrays
        in_specs += [ANY] * n_host
        out_specs += [ANY] * n_host
        out_shape += hosted.out_shape
        scratch += hosted.scratch
    outs = pl.pallas_call(
        body, name="inproj_bwd_dx", grid=(S // tm,),
        in_specs=in_specs, out_specs=out_specs, out_shape=out_shape, scratch_shapes=scratch,
        compiler_params=_cp(("arbitrary",)),
    )(*args)
    return (outs[:2], outs[2:]) if hosted else outs


def _dw(u, dsec, name):
    ts = 512
    ncol = dsec.shape[1] // D

    def body(u_ref, d_ref, o_ref):
        @pl.when(pl.program_id(1) == 0)
        def _():
            o_ref[...] = jnp.zeros_like(o_ref)

        o_ref[...] += _dot_tn(u_ref[...], d_ref[...])

    return pl.pallas_call(
        body, name=name, grid=(ncol, S // ts),
        in_specs=[pl.BlockSpec((ts, D), lambda j, i: (i, 0)), pl.BlockSpec((ts, D), lambda j, i: (i, j))],
        out_specs=pl.BlockSpec((D, D), lambda j, i: (0, j)),
        out_shape=SDS((D, ncol * D), F32),
        compiler_params=_cp(("parallel", "arbitrary")),
    )(u, dsec)


def _place():
    x, y, c = lax.axis_index("x"), lax.axis_index("y"), lax.axis_index("c")
    return x, y, c, 2 * x + y


def _chip_of(x, y, k):
    px = 1 - x if k & 2 else x
    py = 1 - y if k & 1 else y
    return px, py, 2 * px + py


def _remote(src, dst, send_sem, recv_sem, dev):
    return pltpu.make_async_remote_copy(src_ref=src, dst_ref=dst, send_sem=send_sem, recv_sem=recv_sem,
                                        device_id=dev, device_id_type=MESH)


def _gather_weights(w_in_b):
    half = w_in_b.shape[0] // 2
    quarter = half // 2

    def body(src, dst, send, recv):
        x, y, c, j = _place()
        me, sib = (x, y, c), (x, y, 1 - c)
        nbr = {"x": _chip_of(x, y, 2), "y": _chip_of(x, y, 1)}
        diag = _chip_of(x, y, 3)[2]
        started, arrivals = [], []

        def rows(n_quarter=None, sibling=False):
            base = (1 - c if sibling else c) * half
            return pl.ds(base, half) if n_quarter is None else pl.ds(base + n_quarter * quarter, quarter)

        def sem(n):
            return send.at[n], recv.at[n]

        def go(cp):
            cp.start()
            started.append(cp)

        own = _remote(src, dst.at[j], *sem(8), sib)
        go(own)
        for n, axis in enumerate("xy"):
            px, py, _ = nbr[axis]
            go(_remote(src.at[rows()], dst.at[j, rows()], *sem(n), (px, py, c)))
        for n, axis in enumerate("xy"):
            ox, oy, _ = nbr["y" if axis == "x" else "x"]
            pj = nbr[axis][2]
            _remote(src.at[rows()], dst.at[pj, rows()], *sem(n), me).wait_recv()
            go(_remote(dst.at[pj, rows(n)], dst.at[pj, rows(n)], *sem(2 + n), (ox, oy, c)))
            go(_remote(dst.at[pj, rows()], dst.at[pj, rows()], *sem(4 + n), sib))
            arrivals.append(_remote(src.at[rows()], dst.at[pj, rows(None, True)], *sem(4 + n), me))
        for n in range(2):
            _remote(dst.at[diag, rows(n)], dst.at[diag, rows(n)], *sem(2 + n), me).wait_recv()
            go(_remote(dst.at[diag, rows(n)], dst.at[diag, rows(n)], *sem(6 + n), sib))
            arrivals.append(_remote(dst.at[diag, rows(n, True)], dst.at[diag, rows(n, True)], *sem(6 + n), me))
        for cp in arrivals + [own]:
            cp.wait_recv()
        for cp in started:
            cp.wait_send()

    return pl.pallas_call(
        body, name="gather_weights", in_specs=[ANY], out_specs=ANY,
        out_shape=SDS((4,) + w_in_b.shape, BF16),
        scratch_shapes=[pltpu.SemaphoreType.DMA((9,)), pltpu.SemaphoreType.DMA((9,))],
        compiler_params=pltpu.CompilerParams(has_side_effects=True),
    )(w_in_b)


class _LateGather:
    def __init__(self, w_out_b, conv_w):
        self.arrays = [w_out_b, conv_w]
        self.out_shape = [SDS((4,) + w_out_b.shape, BF16), SDS((4,) + conv_w.shape, F32)]
        self.scratch = [pltpu.SemaphoreType.DMA((11,)), pltpu.SemaphoreType.DMA((11,))]

    def _plan(self, ins, outs, sems):
        x, y, c, j = _place()
        send, recv = sems
        (wo, cw), (gwo, gcw) = ins, outs
        half = wo.shape[0] // 2
        mine, theirs = pl.ds(c * half, half), pl.ds((1 - c) * half, half)
        me, sib = (x, y, c), (x, y, 1 - c)
        first, arrive, forward, last = [], [], [], []
        for k in (1, 2, 3):
            px, py, pj = _chip_of(x, y, k)
            first += [_remote(wo.at[mine], gwo.at[j, mine], send.at[k - 1], recv.at[k - 1], (px, py, c)),
                      _remote(cw, gcw.at[j], send.at[k + 2], recv.at[k + 2], (px, py, c))]
            arrive.append(_remote(wo.at[mine], gwo.at[pj, mine], send.at[k - 1], recv.at[k - 1], me))
            forward.append(_remote(gwo.at[pj, mine], gwo.at[pj, mine], send.at[k + 5], recv.at[k + 5], sib))
            last += [_remote(cw, gcw.at[pj], send.at[k + 2], recv.at[k + 2], me),
                     _remote(wo.at[theirs], gwo.at[pj, theirs], send.at[k + 5], recv.at[k + 5], me)]
        first += [_remote(wo, gwo.at[j], send.at[9], recv.at[9], sib),
                  _remote(cw, gcw.at[j], send.at[10], recv.at[10], sib)]
        last += first[-2:]
        return first, arrive, forward, last

    def start(self, ins, outs, sems):
        for cp in self._plan(ins, outs, sems)[0]:
            cp.start()

    def finish(self, ins, outs, sems):
        first, arrive, forward, last = self._plan(ins, outs, sems)
        for got, fwd in zip(arrive, forward):
            got.wait_recv()
            fwd.start()
        for cp in last:
            cp.wait_recv()
        for cp in first + forward:
            cp.wait_send()


class _PairExchange:
    def __init__(self, arrays):
        self.arrays = list(arrays)
        self.out_shape = [SDS((a.shape[0], a.shape[1] // 2, a.shape[2]), F32) for a in self.arrays]
        self.scratch = [pltpu.SemaphoreType.DMA((len(self.arrays),)) for _ in range(2)]

    def _copies(self, ins, outs, sems):
        x, y, c, _ = _place()
        for k, (src, dst) in enumerate(zip(ins, outs)):
            half = src.shape[1] // 2
            yield _remote(src.at[:, pl.ds((1 - c) * half, half)], dst, sems[0].at[k], sems[1].at[k], (x, y, 1 - c))

    def start(self, ins, outs, sems):
        for cp in self._copies(ins, outs, sems):
            cp.start()

    def finish(self, ins, outs, sems):
        for cp in self._copies(ins, outs, sems):
            cp.wait()


def _pair_exchange(arrays, name):
    halves = [a.shape[1] // 2 for a in arrays]
    n = len(arrays)

    def body(*refs):
        x, y, c, _ = _place()
        send, recv = refs[2 * n:]
        cps = [_remote(refs[k].at[:, pl.ds((1 - c) * halves[k], halves[k])], refs[n + k], send.at[k], recv.at[k],
                       (x, y, 1 - c)) for k in range(n)]
        for cp in cps:
            cp.start()
        for cp in cps:
            cp.wait()

    return pl.pallas_call(
        body, name=name, in_specs=[ANY] * n, out_specs=[ANY] * n,
        out_shape=[SDS((a.shape[0], h, a.shape[2]), F32) for a, h in zip(arrays, halves)],
        scratch_shapes=[pltpu.SemaphoreType.DMA((n,)), pltpu.SemaphoreType.DMA((n,))],
        compiler_params=pltpu.CompilerParams(has_side_effects=True),
    )(*arrays)


def _pair_sum(cidx, g, r, name):
    n, half, width = r.shape
    tr = min(half, 256)
    nt = half // tr

    def body(c_ref, g_ref, r_ref, o_ref):
        del c_ref
        o_ref[...] = (g_ref[...] + r_ref[...]).astype(BF16)

    return pl.pallas_call(
        body, name=name,
        grid_spec=pltpu.PrefetchScalarGridSpec(
            num_scalar_prefetch=1, grid=(n, nt),
            in_specs=[pl.BlockSpec((None, tr, width), lambda s, t, c: (s, c[0] * nt + t, 0)),
                      pl.BlockSpec((None, tr, width), lambda s, t, c: (s, t, 0))],
            out_specs=pl.BlockSpec((None, tr, width), lambda s, t, c: (s, t, 0))),
        out_shape=SDS(r.shape, BF16),
        compiler_params=_cp(("parallel", "parallel")),
    )(cidx, g, r)


class _ChipExchange:
    def __init__(self, arrays, rows):
        self.arrays, self.rows = list(arrays), list(rows)
        self.out_shape = [SDS((4,) + a.shape[1:], BF16) for a in self.arrays]
        self.scratch = [pltpu.SemaphoreType.DMA((3 * len(self.arrays),)) for _ in range(2)]

    def _copies(self, ins, outs, sems):
        x, y, c, j = _place()
        send, recv = sems
        for a, (src, dst, row) in enumerate(zip(ins, outs, self.rows)):
            for k in (1, 2, 3):
                px, py, pj = _chip_of(x, y, k)
                n = 3 * a + k - 1
                slot = pj if row is None else py
                yield (None if row is None else px == row, None if row is None else x == row,
                       _remote(src.at[slot], dst.at[j], send.at[n], recv.at[n], (px, py, c)),
                       _remote(src.at[0], dst.at[pj], send.at[n], recv.at[n], (x, y, c)))

    def start(self, ins, outs, sems):
        for sends, _, send, _ in self._copies(ins, outs, sems):
            if sends is None:
                send.start()
            else:
                pl.when(sends)(send.start)

    def finish(self, ins, outs, sems):
        for sends, owns, send, arrival in self._copies(ins, outs, sems):
            if sends is None:
                arrival.wait_recv()
                send.wait_send()
            else:
                pl.when(owns)(arrival.wait_recv)
                pl.when(sends)(send.wait_send)


def _small_exchange(small):
    def body(sm_ref, rs_ref, send, recv, lsem):
        x, y, c, j = _place()
        me = 2 * j + c
        local = pltpu.make_async_copy(sm_ref, rs_ref.at[me], lsem)
        local.start()
        cps = []
        for k in range(1, 8):
            px, py, _ = _chip_of(x, y, k >> 1)
            pc = 1 - c if k & 1 else c
            cps.append(_remote(sm_ref, rs_ref.at[me], send.at[k - 1], recv.at[k - 1], (px, py, pc)))
        for cp in cps:
            cp.start()
        for k in range(1, 8):
            _, _, pj = _chip_of(x, y, k >> 1)
            pc = 1 - c if k & 1 else c
            _remote(sm_ref, rs_ref.at[2 * pj + pc], send.at[k - 1], recv.at[k - 1], (x, y, c)).wait_recv()
        for cp in cps:
            cp.wait_send()
        local.wait()

    return pl.pallas_call(
        body, name="small_exchange", in_specs=[ANY], out_specs=ANY,
        out_shape=SDS((8,) + small.shape, F32),
        scratch_shapes=[pltpu.SemaphoreType.DMA((7,)), pltpu.SemaphoreType.DMA((7,)), pltpu.SemaphoreType.DMA],
        compiler_params=pltpu.CompilerParams(has_side_effects=True),
    )(small)


def _slot_sum(r, name):
    n, rows, width = r.shape
    tr = min(rows, 256)

    def body(r_ref, o_ref):
        acc = r_ref[0].astype(F32)
        for s in range(1, n):
            acc = acc + r_ref[s].astype(F32)
        o_ref[...] = acc

    return pl.pallas_call(
        body, name=name, grid=(rows // tr,),
        in_specs=[pl.BlockSpec((n, tr, width), lambda t: (0, t, 0))],
        out_specs=pl.BlockSpec((tr, width), lambda t: (t, 0)),
        out_shape=SDS((rows, width), F32),
        compiler_params=_cp(("parallel",)),
    )(r)


def _chip_sum(chip_idx, recv, own, name):
    n, rows, width = recv.shape
    tr = min(rows, 256)

    def body(j_ref, r_ref, own_ref, o_ref):
        acc = None
        for s in range(n):
            term = jnp.where(j_ref[0] == s, own_ref[...], r_ref[s]).astype(F32)
            acc = term if acc is None else acc + term
        o_ref[...] = acc

    return pl.pallas_call(
        body, name=name,
        grid_spec=pltpu.PrefetchScalarGridSpec(
            num_scalar_prefetch=1, grid=(rows // tr,),
            in_specs=[pl.BlockSpec((n, tr, width), lambda t, j: (0, t, 0)),
                      pl.BlockSpec((None, tr, width), lambda t, j: (j[0], t, 0))],
            out_specs=pl.BlockSpec((tr, width), lambda t, j: (t, 0))),
        out_shape=SDS((rows, width), F32),
        compiler_params=_cp(("parallel",)),
    )(chip_idx, recv, own)


def _chip_sum_rows(place, recv0, own0, recv1, own1, name):
    n, rows, width = recv0.shape
    tr = min(rows, 256)

    def body(p_ref, r0_ref, o0_ref, r1_ref, o1_ref, o_ref):
        first_row = p_ref[1] == 0
        own = jnp.where(first_row, o0_ref[...], o1_ref[...])
        acc = None
        for s in range(n):
            term = jnp.where(p_ref[0] == s, own, jnp.where(first_row, r0_ref[s], r1_ref[s])).astype(F32)
            acc = term if acc is None else acc + term
        o_ref[...] = acc

    recv = pl.BlockSpec((n, tr, width), lambda t, p: (0, t, 0))
    own = pl.BlockSpec((None, tr, width), lambda t, p: (p[2], t, 0))
    return pl.pallas_call(
        body, name=name,
        grid_spec=pltpu.PrefetchScalarGridSpec(
            num_scalar_prefetch=1, grid=(rows // tr,), in_specs=[recv, own, recv, own],
            out_specs=pl.BlockSpec((tr, width), lambda t, p: (t, 0))),
        out_shape=SDS((rows, width), F32),
        compiler_params=_cp(("parallel",)),
    )(place, recv0, own0, recv1, own1)


def _half_exchange(hw, ho):
    def body(hw_ref, ho_ref, tw_ref, to_ref, send, recv):
        x, y, c, _ = _place()
        sib = (x, y, 1 - c)
        cps = [_remote(hw_ref, tw_ref, send.at[0], recv.at[0], sib),
               _remote(ho_ref, to_ref, send.at[1], recv.at[1], sib)]
        for cp in cps:
            cp.start()
        for cp in cps:
            cp.wait()

    return pl.pallas_call(
        body, name="half_exchange", in_specs=[ANY, ANY], out_specs=[ANY, ANY],
        out_shape=[SDS(hw.shape, F32), SDS(ho.shape, F32)],
        scratch_shapes=[pltpu.SemaphoreType.DMA((2,)), pltpu.SemaphoreType.DMA((2,))],
        compiler_params=pltpu.CompilerParams(has_side_effects=True),
    )(hw, ho)


def _by_core(c, mine, theirs):
    return jnp.where(c == 0, jnp.concatenate([mine, theirs], axis=0), jnp.concatenate([theirs, mine], axis=0))


def _adamw(w, g, m, v, name):
    rows, width = w.shape
    tr = next(t for t in (256, rows // 3, rows) if rows % t == 0 and t % 8 == 0)

    def body(w_ref, g_ref, m_ref, v_ref, d_ref, nm_ref, nv_ref):
        gv = g_ref[...]
        nm = ADAM_B1 * m_ref[...] + (1.0 - ADAM_B1) * gv
        nv = ADAM_B2 * v_ref[...] + (1.0 - ADAM_B2) * (gv * gv)
        m_hat = nm / (1.0 - ADAM_B1 ** ADAM_STEP)
        v_hat = nv / (1.0 - ADAM_B2 ** ADAM_STEP)
        d_ref[...] = -ADAM_LR * (m_hat / (jnp.sqrt(v_hat) + ADAM_EPS) + ADAM_WD * w_ref[...])
        nm_ref[...] = nm
        nv_ref[...] = nv

    t = pl.BlockSpec((tr, width), lambda i: (i, 0))
    return pl.pallas_call(
        body, name=name, grid=(rows // tr,), in_specs=[t] * 4, out_specs=[t] * 3,
        out_shape=[SDS(w.shape, F32)] * 3, compiler_params=_cp(("parallel",)),
    )(w, g, m, v)


def _rowwise(a):
    return jnp.transpose(a, (2, 0, 1)).reshape(SHARD * D // LANE, LANE)


def _from_rowwise(a):
    return jnp.transpose(a.reshape(SHARD, 1, D), (1, 2, 0))


def _rows128(a, rows):
    flat = a.reshape(-1)
    return jnp.pad(flat, (0, rows * LANE - flat.shape[0])).reshape(rows, LANE)


def _pack_small(conv_w, norm_pre, conv_b, ssm_norm, norm_post, dtb, alog, dsk, extra=None):
    cw_rows = 48 if conv_w.shape[-1] == 1536 else 16
    extra = jnp.zeros((1, LANE), F32) if extra is None else _rows128(extra, 1)
    vec = jnp.concatenate([_rows128(dtb, 1), _rows128(alog, 1), _rows128(dsk, 1), extra, jnp.zeros((4, LANE), F32)],
                          axis=0)
    return jnp.concatenate([_rows128(conv_w, cw_rows), _rows128(norm_pre, 8), _rows128(conv_b, 16),
                            _rows128(ssm_norm, 8), _rows128(norm_post, 8), vec], axis=0)


def _unpack_small(p, cw_cols):
    cw_rows = 48 if cw_cols == 1536 else 16
    o = cw_rows
    conv_w = p[:cw_rows].reshape(-1)[:4 * cw_cols].reshape(1, 4, cw_cols)
    norm_pre = p[o:o + 8].reshape(1, D)
    conv_b = p[o + 8:o + 24].reshape(-1)[:1536].reshape(1, 1536)
    ssm_norm = p[o + 24:o + 32].reshape(1, D)
    norm_post = p[o + 32:o + 40].reshape(1, D)
    vec = p[o + 40:o + 48]
    return conv_w, norm_pre, conv_b, ssm_norm, norm_post, vec[0:1, :NH], vec[1:2, :NH], vec[2:3, :NH], vec[3, 0]


def _pad_lanes(a):
    return jnp.pad(a, ((0, 0), (0, LANE - a.shape[1])))


class _GradReduce:
    SPLIT = 2 * SHARD - OFF_G

    def __init__(self, xi, yi, ci):
        self.ci = ci
        self.cidx = jnp.reshape(ci, (1,)).astype(jnp.int32)
        self.place = jnp.stack([2 * xi + yi, xi, yi]).astype(jnp.int32)

    def pairs(self, dw_g, dw_z, dw_x, dw_out):
        cols = jnp.concatenate([dw_g[:, self.SPLIT:], dw_z, dw_x], axis=1)
        self.gw_hi = jnp.stack([cols[:, :SHARD], cols[:, SHARD:2 * SHARD]])
        self.go = dw_out.reshape(4, D // 2, D)
        return _PairExchange([self.gw_hi, self.go])

    def first(self, got):
        rw, ro = got
        self.pw_hi = _pair_sum(self.cidx, self.gw_hi, rw, "pair_sum_hi")
        self.po = _pair_sum(self.cidx, self.go, ro, "pair_sum_out")
        return _ChipExchange([self.pw_hi, self.po], [1, None])

    def first_done(self, got):
        self.rw_hi, self.ro = got

    def second(self, dw_q, dw_k, dw_v, dw_g):
        cols = jnp.concatenate([dw_q, dw_k, dw_v, dw_g[:, :self.SPLIT]], axis=1)
        gw = jnp.stack([cols[:, :SHARD], cols[:, SHARD:]])
        (rw,) = _pair_exchange([gw], "pair_exchange_lo")
        self.pw_lo = _pair_sum(self.cidx, gw, rw, "pair_sum_lo")
        return _ChipExchange([self.pw_lo], [0])

    def second_done(self, got):
        (self.rw_lo,) = got

    def result(self):
        half_in = _chip_sum_rows(self.place, self.rw_lo, self.pw_lo, self.rw_hi, self.pw_hi, "chip_sum_in")
        half_out = _chip_sum(self.place[0:1], self.ro, self.po, "chip_sum_out")
        their_in, their_out = _half_exchange(half_in, half_out)
        return _by_core(self.ci, half_in, their_in), _by_core(self.ci, half_out, their_out)


def kernel(x, norm_pre_w, w_in, conv_w, conv_b, dt_bias, a_log, d_skip, ssm_norm_w, w_out, norm_post_w, loss_target, m_norm_pre_w, m_w_in, m_conv_w, m_conv_b, m_dt_bias, m_a_log, m_d_skip, m_ssm_norm_w, m_w_out, m_norm_post_w, v_norm_pre_w, v_w_in, v_conv_w, v_conv_b, v_dt_bias, v_a_log, v_d_skip, v_ssm_norm_w, v_w_out, v_norm_post_w):
    xi, yi, ci = lax.axis_index("x"), lax.axis_index("y"), lax.axis_index("c")
    chip = 2 * xi + yi
    x2, tgt = x[0], loss_target[0]

    gin = _gather_weights(w_in[0].astype(BF16))
    w_all = jnp.concatenate([gin[0], gin[1], gin[2], gin[3], jnp.zeros((D, DP - 4 * SHARD), BF16)], axis=1)
    reduce = _GradReduce(xi, yi, ci)
    grad_x, small = _local_step(x2, tgt, w_all, _LateGather(w_out[0].astype(BF16), conv_w[0]), norm_pre_w, conv_b,
                                dt_bias, a_log, d_skip, ssm_norm_w, norm_post_w, reduce)[:2]
    g_in, g_out = reduce.result()
    g_small = _slot_sum(_small_exchange(small), "small_sum")
    g_cw, g_npre, g_cb, g_nssm, g_npost, g_dtb, g_alog, g_dsk, loss = _unpack_small(g_small, 1536)
    g_cw = lax.dynamic_slice_in_dim(g_cw, chip * 384, 384, axis=2)

    d_in, nm_in, nv_in = [_from_rowwise(a)[0] for a in _adamw(
        _rowwise(w_in), _rowwise(g_in[None]), _rowwise(m_w_in), _rowwise(v_w_in), "adamw_in")]
    d_out, nm_out, nv_out = _adamw(w_out[0], g_out, m_w_out[0], v_w_out[0], "adamw_out")
    packed = [_pack_small(*t) for t in (
        (conv_w, norm_pre_w, conv_b, ssm_norm_w, norm_post_w, dt_bias, a_log, d_skip),
        (g_cw, g_npre, g_cb, g_nssm, g_npost, g_dtb, g_alog, g_dsk),
        (m_conv_w, m_norm_pre_w, m_conv_b, m_ssm_norm_w, m_norm_post_w, m_dt_bias, m_a_log, m_d_skip),
        (v_conv_w, v_norm_pre_w, v_conv_b, v_ssm_norm_w, v_norm_post_w, v_dt_bias, v_a_log, v_d_skip))]
    small_out = [_unpack_small(p, 384)[:8] for p in _adamw(*packed, "adamw_small")]

    def ordered(cw_, npre, cb_, nssm, npost, dtb_, alog_, dsk_, big_in, big_out):
        return [npre, big_in[None], cw_, cb_, dtb_, alog_, dsk_, nssm, big_out[None], npost]

    grads = ordered(g_cw, g_npre, g_cb, g_nssm, g_npost, g_dtb, g_alog, g_dsk, g_in, g_out)
    deltas = ordered(*small_out[0], d_in, d_out)
    new_m = ordered(*small_out[1], nm_in, nm_out)
    new_v = ordered(*small_out[2], nv_in, nv_out)
    return (loss, grad_x[None], *grads, *deltas, *new_m, *new_v)


def _local_step(x2, tgt, w_all, late, norm_pre_w, conv_b, dt_bias, a_log, d_skip, ssm_norm_w,
                norm_post_w, reduce=None):
    dtb, alog = _pad_lanes(dt_bias), _pad_lanes(a_log)
    d_b = jnp.repeat(d_skip, 64, axis=1)

    if isinstance(late, _LateGather):
        (proj, u), (gout, gcw) = _inproj_fwd(x2, norm_pre_w, w_all, late)
        w_out_all = gout.reshape(2 * D, D)
        cw_all = jnp.concatenate([gcw[0], gcw[1], gcw[2], gcw[3]], axis=1)
    else:
        proj, u = _inproj_fwd(x2, norm_pre_w, w_all)
        w_out_all, cw_all = late
    mix, attn_pre, lse = _attn_fwd(proj, 1, _attn_fwd(proj, 4, _attn_fwd(proj, 16)), final=True)
    mix, y_save, states, conv_out = _ssm_fwd(proj, mix, cw_all, conv_b, dtb, alog, d_b, ssm_norm_w)

    dy, dn_ssm, do, delta, dg, dw_out, dnw_post, loss_part = _outproj_loss(mix, w_out_all, x2, tgt, norm_post_w,
                                                                          attn_pre, proj)
    dz, dxbcdt, dcw, dcb, dvec, dnw_ssm = _ssm_bwd(proj, dn_ssm, y_save, states, conv_out, cw_all, dtb, alog, d_b,
                                                   ssm_norm_w)
    dw_g, dw_z, dw_x = _dw(u, dg, "dw_in_g"), _dw(u, dz, "dw_in_z"), _dw(u, dxbcdt, "dw_in_xbcdt")
    acc = _attn_bwd(proj, do, lse, delta, 16, None, F32, reduce.pairs(dw_g, dw_z, dw_x, dw_out) if reduce else None)
    if reduce:
        acc, got = acc
    acc = _attn_bwd(proj, do, lse, delta, 4, acc, F32, reduce.first(got) if reduce else None)
    if reduce:
        acc, got = acc
        reduce.first_done(got)
    dq, dk, dv = _attn_bwd(proj, do, lse, delta, 1, acc, BF16)
    dw_q, dw_k, dw_v = _dw(u, dq, "dw_in_q"), _dw(u, dk, "dw_in_k"), _dw(u, dv, "dw_in_v")
    res = _inproj_bwd_dx([dq, dk, dv, dg, dz], dxbcdt, w_all, x2, dy, norm_pre_w,
                         reduce.second(dw_q, dw_k, dw_v, dw_g) if reduce else None)
    if reduce:
        res, got = res
        reduce.second_done(got)
    grad_x, dnw_pre = res
    dw_all = jnp.concatenate([dw_q, dw_k, dw_v, dw_g, dw_z, dw_x], axis=1)
    small = _pack_small(dcw, dnw_pre, dcb, dnw_ssm, dnw_post, dvec[0:1, :NH], dvec[1:2, :NH], dvec[2:3, :NH],
                        loss_part[:, :1])
    return grad_x, small, dw_all, dw_out
```

```python
import functools

import jax
import jax.numpy as jnp
from jax import lax
from jax.experimental import pallas as pl
from jax.experimental.pallas import tpu as pltpu

F32 = jnp.float32
BF16 = jnp.bfloat16
MESH = pl.DeviceIdType.MESH
SDS = jax.ShapeDtypeStruct
ANY = pl.BlockSpec(memory_space=pl.ANY)

S = 4096
D = 1024
DP = 7168
SHARD = 1668
OFF_G, OFF_Z = 3072, 4096
NH = 16
CH = 128
NC = S // CH
EPS = 1e-6
NEG = -1e30
LANE = 128
VMEM_LIMIT = 48 * 1024 * 1024

ADAM_LR, ADAM_B1, ADAM_B2, ADAM_EPS, ADAM_WD, ADAM_STEP = 0.001, 0.9, 0.999, 1e-08, 0.01, 10


def _cp(sem, **kw):
    return pltpu.CompilerParams(dimension_semantics=sem, vmem_limit_bytes=VMEM_LIMIT, **kw)


def _dot(a, b):
    return jnp.dot(a, b, preferred_element_type=F32)


def _dot_nt(a, b):
    return lax.dot_general(a, b, (((1,), (1,)), ((), ())), preferred_element_type=F32)


def _dot_tn(a, b):
    return lax.dot_general(a, b, (((0,), (0,)), ((), ())), preferred_element_type=F32)


def _pieces(x, n):
    out = []
    for _ in range(n):
        p = x.astype(BF16)
        out.append(p)
        x = x - p.astype(F32)
    return out


def _pick(x, sel, n=2):
    parts = [_dot(p, sel) for p in _pieces(x, n)]
    return functools.reduce(jnp.add, parts)


def _pick_left(sel, x, n=3):
    parts = [_dot(sel, p) for p in _pieces(x, n)]
    return functools.reduce(jnp.add, parts)


def _sigmoid(v):
    return 0.5 * jnp.tanh(0.5 * v) + 0.5


def _iota(shape, dim):
    return lax.broadcasted_iota(jnp.int32, shape, dim)


def _inproj_fwd(x, nw, w_all, hosted=None):
    tm, tn = 1024, 1024
    n_host = len(hosted.arrays) if hosted else 0

    def body(x_ref, nw_ref, w_ref, *refs):
        host_in, (proj_ref, u_ref), refs = refs[:n_host], refs[n_host:n_host + 2], refs[n_host + 2:]
        host_out, host_sems = refs[:n_host], refs[n_host:]
        i, j = pl.program_id(0), pl.program_id(1)
        if hosted:
            pl.when((i == 0) & (j == 0))(lambda: hosted.start(host_in, host_out, host_sems))

        @pl.when(j == 0)
        def _():
            xf = x_ref[...]
            r = lax.rsqrt(jnp.mean(xf * xf, axis=-1, keepdims=True) + EPS)
            u_ref[...] = (xf * r * nw_ref[...]).astype(BF16)

        proj_ref[...] = _dot(u_ref[...], w_ref[...])
        if hosted:
            pl.when((i == S // tm - 1) & (j == DP // tn - 1))(lambda: hosted.finish(host_in, host_out, host_sems))

    outs = pl.pallas_call(
        body, name="inproj_fwd", grid=(S // tm, DP // tn),
        in_specs=[pl.BlockSpec((tm, D), lambda i, j: (i, 0)), pl.BlockSpec((1, D), lambda i, j: (0, 0)),
                  pl.BlockSpec((D, tn), lambda i, j: (0, j))] + [ANY] * n_host,
        out_specs=[pl.BlockSpec((tm, tn), lambda i, j: (i, j)), pl.BlockSpec((tm, D), lambda i, j: (i, 0))]
        + [ANY] * n_host,
        out_shape=[SDS((S, DP), F32), SDS((S, D), BF16)] + (hosted.out_shape if hosted else []),
        scratch_shapes=hosted.scratch if hosted else [],
        compiler_params=_cp(("arbitrary", "arbitrary") if hosted else ("parallel", "arbitrary")),
    )(x, nw, w_all, *(hosted.arrays if hosted else []))
    return (outs[:2], outs[2:]) if hosted else outs


ATTN_QB = {1: 16, 4: 4, 16: 1}


def _unit_rows(r, u, d):
    return pl.ds(r + d * CH * u, CH, stride=d) if d > 1 else pl.ds(CH * u, CH)


def _for_units(d, qb, fn):
    for r in range(d):
        for u in range(qb):
            fn(r, u)


def _attn_mask(has_prev):
    qi, kj = _iota((2 * CH, 2 * CH), 0) & (CH - 1), _iota((2 * CH, 2 * CH), 1)
    cur_ok = (kj >= CH) & (kj - CH <= qi)
    prev_ok = (kj < CH) & (kj >= qi)
    return cur_ok | (prev_ok & has_prev)


def _stack_heads(v, lane_a):
    return jnp.concatenate([jnp.where(lane_a, v, 0.0), jnp.where(lane_a, 0.0, v)], axis=0).astype(BF16)


def _attn_specs(d, qb):
    rows, prows = CH * d * qb, CH * d
    nb = S // rows
    steps = (NH // 2) * nb

    def at(t):
        t = jnp.minimum(t, steps - 1)
        return t % nb, t // nb

    def cur(off):
        return pl.BlockSpec((rows, LANE), lambda t: (at(t)[0], off + at(t)[1]))

    def prev(off):
        return pl.BlockSpec((prows, LANE), lambda t: (jnp.maximum(at(t)[0] * qb - 1, 0), off + at(t)[1]))

    lag = pl.BlockSpec((rows, LANE), lambda t: at(jnp.maximum(t - 1, 0)))
    return nb, steps, cur, prev, lag


def _gather16(src_ref, dense_ref, tmp_ref):
    for a in range(4):
        tmp_ref[...] = src_ref[pl.ds(a, 4 * CH, stride=4), :]
        for b in range(4):
            dense_ref[a + 4 * b] = tmp_ref[pl.ds(b, CH, stride=4), :]


def _scatter16(dense_ref, dst_ref, tmp_ref):
    for a in range(4):
        for b in range(4):
            tmp_ref[pl.ds(b, CH, stride=4), :] = dense_ref[a + 4 * b]
        dst_ref[pl.ds(a, 4 * CH, stride=4), :] = tmp_ref[...]


def _unit_index(r, u, d):
    return (r,) if d == 16 else (_unit_rows(r, u, d), slice(None))


def _unit_kv(p_ref, c_ref, r, u, d):
    prev = p_ref[_unit_index(r, 0, d)] if u == 0 else c_ref[_unit_index(r, u - 1, d)]
    return jnp.concatenate([prev, c_ref[_unit_index(r, u, d)]], axis=0).astype(BF16)


def _dense_scratch(d, n):
    return [pltpu.VMEM((16, CH, LANE), F32)] * n + [pltpu.VMEM((4 * CH, LANE), F32)] if d == 16 else []


def _attn_fwd(proj, d, prior=None, final=False):
    qb = ATTN_QB[d]
    nb, steps, cur, prev, _ = _attn_specs(d, qb)
    n_prior = 2 if prior is not None else 0
    n_in, n_out = 5 + n_prior + final, 2 + final
    assert not (d == 16 and (n_prior or final))

    def body(*refs):
        ins, outs, scratch = refs[:n_in], refs[n_in:n_in + n_out], refs[n_in + n_out:]
        if d == 16:
            tmp_ref = scratch[-1]
            for src, dense in zip(ins, scratch):
                _gather16(src, dense, tmp_ref)
            block_outs, ins, outs = outs, scratch[:n_in], scratch[n_in:n_in + n_out]
        q_ref, kp_ref, kc_ref, vp_ref, vc_ref = ins[:5]
        prior_refs = ins[5:5 + n_prior]
        if final:
            g_ref, (mix_ref, o_ref, l_ref) = ins[-1], outs
        else:
            o_ref, l_ref = outs
        i = pl.program_id(0) % nb
        lane_a = _iota((CH, LANE), 1) < 64
        mask_first, mask_rest = _attn_mask(i > 0), _attn_mask(True)

        def unit(r, u):
            at = _unit_index(r, u, d)
            q2 = _stack_heads(q_ref[at] * 0.125, lane_a)
            k2, v2 = _unit_kv(kp_ref, kc_ref, r, u, d), _unit_kv(vp_ref, vc_ref, r, u, d)
            s = jnp.where(mask_first if u == 0 else mask_rest, _dot_nt(q2, k2), NEG)
            m = jnp.max(s, axis=1, keepdims=True)
            p = jnp.exp(s - m)
            l = jnp.sum(p, axis=1, keepdims=True)
            o2 = _dot(p.astype(BF16), v2) / l
            lse2 = m + jnp.log(l)
            o = jnp.where(lane_a, o2[:CH], o2[CH:])
            lse = jnp.where(lane_a, lse2[:CH], lse2[CH:])
            if n_prior:
                o_a, l_a = prior_refs[0][at], prior_refs[1][at]
                top = jnp.maximum(l_a, lse)
                e_a, e_b = jnp.exp(l_a - top), jnp.exp(lse - top)
                tot = e_a + e_b
                o = (e_a * o_a + e_b * o) / tot
                lse = top + jnp.log(tot)
            o_ref[at] = o
            l_ref[at] = lse
            if final:
                g = g_ref[at]
                mix_ref[at] = (o * (g * _sigmoid(g))).astype(BF16)

        _for_units(d, qb, unit)
        if d == 16:
            for dense, dst in zip(outs, block_outs):
                _scatter16(dense, dst, tmp_ref)

    in_specs = [cur(0), prev(8), cur(8), prev(16), cur(16)] + [cur(0)] * n_prior
    args = [proj] * 5 + (list(prior) if n_prior else [])
    out_specs, out_shape = [cur(0), cur(0)], [SDS((S, D), F32), SDS((S, D), F32)]
    if final:
        assert d == 1
        in_specs.append(cur(OFF_G // LANE))
        args.append(proj)
        out_specs, out_shape = [cur(0)] + out_specs, [SDS((S, 2 * D), BF16)] + out_shape
    return pl.pallas_call(
        body, name=f"attn_fwd_d{d}", grid=(steps,),
        in_specs=in_specs, out_specs=out_specs, out_shape=out_shape,
        scratch_shapes=_dense_scratch(d, n_in + n_out),
        compiler_params=_cp(("parallel",)),
    )(*args)


def _attn_bwd(proj, do, lse, delta, d, acc, out_dtype, hosted=None):
    qb = ATTN_QB[d]
    nb, steps, cur, prev, lag = _attn_specs(d, qb)
    has_acc = acc is not None
    n_in = 11 if has_acc else 8
    n_host = len(hosted.arrays) if hosted else 0
    assert not (d == 16 and (has_acc or out_dtype != F32))
    rows = CH * d * qb
    carry = (2, 16, CH, LANE) if d == 16 else (2, rows, LANE)

    def body(*refs):
        ins, host_in, refs = refs[:n_in], refs[n_in:n_in + n_host], refs[n_in + n_host:]
        (dq_ref, dk_ref, dv_ref), host_out, scratch = refs[:3], refs[3:3 + n_host], refs[3 + n_host:]
        if hosted:
            scratch, host_sems = scratch[:-len(hosted.scratch)], scratch[-len(hosted.scratch):]
        ck_ref, cv_ref = scratch[:2]
        dq_f32 = dq_ref if out_dtype == F32 else scratch[2]
        t = pl.program_id(0)
        i = t % nb
        if hosted:
            pl.when(t == 0)(lambda: hosted.start(host_in, host_out, host_sems))
        if d == 16:
            dense, dq_f32, tmp_ref = scratch[2:2 + n_in], scratch[2 + n_in], scratch[-1]

            @pl.when(t < steps)
            def _():
                for src, dst in zip(ins, dense):
                    _gather16(src, dst, tmp_ref)

            ins = dense
        q_ref, kp_ref, kc_ref, vp_ref, vc_ref, do_ref, lse_ref, dl_ref = ins[:8]
        if has_acc:
            aq_ref, ak_ref, av_ref = ins[8:11]
        slot = t & 1
        now_k, now_v, old_k, old_v = ck_ref.at[slot], cv_ref.at[slot], ck_ref.at[1 - slot], cv_ref.at[1 - slot]
        lane_a = _iota((CH, LANE), 1) < 64
        mask_first, mask_rest = _attn_mask(i > 0), _attn_mask(True)

        @pl.when(t == 0)
        def _():
            ck_ref[1] = jnp.zeros(carry[1:], F32)
            cv_ref[1] = jnp.zeros(carry[1:], F32)

        def unit(r, u):
            at = _unit_index(r, u, d)
            q2 = _stack_heads(q_ref[at] * 0.125, lane_a)
            do2 = _stack_heads(do_ref[at], lane_a)
            k2, v2 = _unit_kv(kp_ref, kc_ref, r, u, d), _unit_kv(vp_ref, vc_ref, r, u, d)
            lsev, dlv = lse_ref[at], dl_ref[at]
            lse2 = jnp.concatenate([lsev[:, 0:1], lsev[:, 64:65]], axis=0)
            dl2 = jnp.concatenate([dlv[:, 0:1], dlv[:, 64:65]], axis=0)
            p = jnp.exp(jnp.where(mask_first if u == 0 else mask_rest, _dot_nt(q2, k2), NEG) - lse2)
            ds = (p * (_dot_nt(do2, v2) - dl2)).astype(BF16)
            dq2 = _dot(ds, k2)
            dk2 = _dot_tn(ds, q2)
            dv2 = _dot_tn(p.astype(BF16), do2)
            dq = jnp.where(lane_a, dq2[:CH], dq2[CH:]) * 0.125
            if has_acc:
                dq = dq + aq_ref[at]
            dq_f32[at] = dq
            if u == 0:
                before = _unit_index(r, qb - 1, d)
                old_k[before] += dk2[:CH]
                old_v[before] += dv2[:CH]
            else:
                before = _unit_index(r, u - 1, d)
                now_k[before] += dk2[:CH]
                now_v[before] += dv2[:CH]
            now_k[at] = dk2[CH:]
            now_v[at] = dv2[CH:]

        @pl.when(t < steps)
        def _():
            _for_units(d, qb, unit)
            if d == 16:
                _scatter16(dq_f32, dq_ref, tmp_ref)
            elif out_dtype != F32:
                dq_ref[...] = dq_f32[...].astype(out_dtype)

        if d == 16:
            _scatter16(old_k, dk_ref, tmp_ref)
            _scatter16(old_v, dv_ref, tmp_ref)
        else:
            dk, dv = old_k[...], old_v[...]
            if has_acc:
                dk, dv = dk + ak_ref[...], dv + av_ref[...]
            dk_ref[...] = dk.astype(out_dtype)
            dv_ref[...] = dv.astype(out_dtype)
        if hosted:
            pl.when(t == steps)(lambda: hosted.finish(host_in, host_out, host_sems))

    in_specs = [cur(0), prev(8), cur(8), prev(16), cur(16), cur(0), cur(0), cur(0)]
    args = [proj, proj, proj, proj, proj, do, lse, delta]
    if has_acc:
        in_specs += [cur(0), lag, lag]
        args += list(acc)
    scratch = [pltpu.VMEM(carry, F32), pltpu.VMEM(carry, F32)]
    if d == 16:
        scratch += _dense_scratch(d, n_in + 1)
    elif out_dtype != F32:
        scratch.append(pltpu.VMEM((rows, LANE), F32))
    out_specs, out_shape = [cur(0), lag, lag], [SDS((S, D), out_dtype)] * 3
    if hosted:
        args += hosted.arrays
        in_specs += [ANY] * n_host
        out_specs += [ANY] * n_host
        out_shape += hosted.out_shape
        scratch += hosted.scratch
    outs = pl.pallas_call(
        body, name=f"attn_bwd_d{d}", grid=(steps + 1,),
        in_specs=in_specs, out_specs=out_specs, out_shape=out_shape,
        scratch_shapes=scratch, compiler_params=_cp(("arbitrary",)),
    )(*args)
    return (outs[:3], outs[3:]) if hosted else outs


def _conv_taps(cur, prev8, first):
    row8 = _iota(prev8.shape, 0)
    prev8 = jnp.where(first, 0.0, prev8)
    taps = []
    for s in (3, 2, 1):
        rolled = pltpu.roll(cur, s, 0)
        head = jnp.where(row8 < s, pltpu.roll(prev8, s, 0), rolled[:8])
        taps.append(jnp.concatenate([head, rolled[8:]], axis=0))
    return taps + [cur]


def _conv(taps, w, b):
    acc = b + w[0:1, :] * taps[0]
    for k in (1, 2, 3):
        acc = acc + w[k:k + 1, :] * taps[k]
    return acc


def _expand():
    return (_iota((LANE, D), 1) // 64 == _iota((LANE, D), 0)).astype(BF16)


def _reduce():
    return (_iota((D, LANE), 0) // 64 == _iota((D, LANE), 1)).astype(BF16)


def _ssd_common(xs_c, bc_c, dt_raw, dtb, alog):
    head_lane = _iota((CH, LANE), 1) < NH
    xs = xs_c * _sigmoid(xs_c)
    bc = bc_c * _sigmoid(bc_c)
    pre = dt_raw + dtb
    dt = jnp.where(head_lane, jnp.maximum(pre, 0.0) + jnp.log(1.0 + jnp.exp(-jnp.abs(pre))), 0.0)
    a_row = jnp.where(head_lane[0:1], -jnp.exp(alog), 0.0)
    tri = (_iota((CH, CH), 1) <= _iota((CH, CH), 0)).astype(BF16)
    cs = _pick_left(tri, dt * a_row)
    cs_last = cs[CH - 1:CH, :]
    wide = _pick(jnp.concatenate([dt, jnp.exp(cs), jnp.exp(cs_last - cs)], axis=0), _expand())
    dt_b, e_b, f_b = wide[:CH], wide[CH:2 * CH], wide[2 * CH:]
    return dict(xs=xs, bc=bc, pre=pre, dt=dt, a_row=a_row, cs=cs, cs_t=cs.T, dt_b=dt_b, e_b=e_b, f_b=f_b,
                t_b=e_b[CH - 1:CH, :])


def _groups(bc):
    bcb = bc.astype(BF16)
    return [bcb[:, 0:128], bcb[:, 128:256]], [bcb[:, 256:384], bcb[:, 384:512]]


def _decay(q, h, tril):
    seg = q["cs"][:, h:h + 1] - q["cs_t"][h:h + 1, :]
    return jnp.exp(jnp.where(tril, seg, NEG))


def _ssm_fwd(proj, mix, cw, cb, dtb, alog, d_b, nw):
    def body(xs_ref, xsp_ref, bc_ref, bcp_ref, dt_ref, z_ref, cw_ref, cb_ref, dtb_ref, alog_ref, db_ref, nw_ref,
             mix_in_ref, mix_ref, y_ref, st_ref, conv_ref, h_ref):
        del mix_in_ref
        i = pl.program_id(0)

        @pl.when(i == 0)
        def _():
            h_ref[...] = jnp.zeros_like(h_ref)

        cw, cb = cw_ref[...], cb_ref[...]
        xs_c = _conv(_conv_taps(xs_ref[...], xsp_ref[...], i == 0), cw[:, :D], cb[:, :D])
        bc_c = _conv(_conv_taps(bc_ref[...], bcp_ref[...], i == 0), cw[:, D:], cb[:, D:])
        conv_ref[:, :D] = xs_c
        conv_ref[:, D:] = bc_c
        q = _ssd_common(xs_c, bc_c, dt_ref[...], dtb_ref[...], alog_ref[...])
        bg, cg = _groups(q["bc"])
        xs = q["xs"]
        xdt = xs * q["dt_b"]
        xdt_b = xdt.astype(BF16)
        h_in = h_ref[...]
        st_ref[...] = h_in
        hb = h_in.astype(BF16)
        tril = _iota((CH, CH), 1) <= _iota((CH, CH), 0)
        lane_a = _iota((CH, LANE), 1) < 64
        cbm = [_dot_nt(cg[g], bg[g]) for g in range(2)]
        pairs = []
        for hp in range(NH // 2):
            xp = xdt_b[:, hp * LANE:(hp + 1) * LANE]
            ya = _dot((cbm[hp // 4] * _decay(q, 2 * hp, tril)).astype(BF16), xp)
            yb = _dot((cbm[hp // 4] * _decay(q, 2 * hp + 1, tril)).astype(BF16), xp)
            pairs.append(jnp.where(lane_a, ya, yb))
        y_diag = jnp.concatenate(pairs, axis=1)
        y_off = jnp.concatenate([_dot(cg[g], hb[:, g * 512:(g + 1) * 512]) for g in range(2)], axis=1) * q["e_b"]
        y = y_diag + y_off + db_ref[...] * xs
        y_ref[...] = y
        xf = (xdt * q["f_b"]).astype(BF16)
        h_ref[...] = q["t_b"] * h_in + jnp.concatenate(
            [_dot_tn(bg[g], xf[:, g * 512:(g + 1) * 512]) for g in range(2)], axis=1)
        z = z_ref[...]
        yz = y * (z * _sigmoid(z))
        outs = []
        for g in range(2):
            v = yz[:, g * 512:(g + 1) * 512]
            outs.append(v * lax.rsqrt(jnp.mean(v * v, axis=-1, keepdims=True) + EPS))
        mix_ref[...] = (jnp.concatenate(outs, axis=1) * nw_ref[...]).astype(BF16)

    def col(width, blk, prev=False):
        if prev:
            return pl.BlockSpec((8, width), lambda i: (jnp.maximum(i * (CH // 8) - 1, 0), blk))
        return pl.BlockSpec((CH, width), lambda i: (i, blk))

    def full(a):
        return pl.BlockSpec(a.shape, lambda i: (0,) * a.ndim)

    return pl.pallas_call(
        body, name="ssm_fwd", grid=(NC,),
        in_specs=[col(D, 5), col(D, 5, True), col(512, 12), col(512, 12, True), col(LANE, 52), col(D, 4),
                  full(cw), full(cb), full(dtb), full(alog), full(d_b), full(nw), ANY],
        out_specs=[col(D, 1), col(D, 0), pl.BlockSpec((None, CH, D), lambda i: (i, 0, 0)), col(D + 512, 0)],
        out_shape=[SDS((S, 2 * D), BF16), SDS((S, D), F32), SDS((NC, CH, D), F32), SDS((S, D + 512), F32)],
        scratch_shapes=[pltpu.VMEM((CH, D), F32)],
        input_output_aliases={12: 0},
        compiler_params=_cp(("arbitrary",)),
    )(proj, proj, proj, proj, proj, proj, cw, cb, dtb, alog, d_b, nw, mix)


def _ssm_bwd(proj, dn, y_save, states, conv_out, cw, dtb, alog, d_b, nw):
    def body(xs_ref, bc_ref, dt_ref, z_ref, dn_ref, y_ref, st_ref, conv_ref,
             cw_ref, dtb_ref, alog_ref, db_ref, nw_ref,
             dz_ref, dx_ref, dcw_ref, dcb_ref, dsm_ref, dnw_ref, dh_ref, nxs_ref, nbc_ref):
        i = pl.program_id(0)
        ci = NC - 1 - i

        @pl.when(i == 0)
        def _():
            for ref in (dcw_ref, dcb_ref, dsm_ref, dnw_ref, dh_ref, nxs_ref, nbc_ref):
                ref[...] = jnp.zeros_like(ref)

        cw = cw_ref[...]
        xs_c, bc_c = conv_ref[:, :D], conv_ref[:, D:]
        q = _ssd_common(xs_c, bc_c, dt_ref[...], dtb_ref[...], alog_ref[...])
        bg, cg = _groups(q["bc"])
        xs, dt_b, e_b, f_b, t_b = q["xs"], q["dt_b"], q["e_b"], q["f_b"], q["t_b"]
        xdt = xs * dt_b
        xdt_b = xdt.astype(BF16)
        h_in = st_ref[...]
        hb = h_in.astype(BF16)
        dh_new = dh_ref[...]
        dhb = dh_new.astype(BF16)
        red = _reduce()

        z, y, dn, nw_v = z_ref[...], y_ref[...], dn_ref[...], nw_ref[...]
        sig = _sigmoid(z)
        sz = z * sig
        yz = y * sz
        gdn = dn * nw_v
        dyz, dnw = [], []
        for g in range(2):
            v, gv = yz[:, g * 512:(g + 1) * 512], gdn[:, g * 512:(g + 1) * 512]
            r = lax.rsqrt(jnp.mean(v * v, axis=-1, keepdims=True) + EPS)
            dnw.append(dn[:, g * 512:(g + 1) * 512] * v * r)
            dyz.append(r * (gv - v * (r * r) * jnp.mean(gv * v, axis=-1, keepdims=True)))
        dyz = jnp.concatenate(dyz, axis=1)
        dnw_ref[...] += jnp.sum(jnp.concatenate(dnw, axis=1), axis=0, keepdims=True)
        dy = dyz * sz
        dz_ref[...] = (dyz * y * (sig * (1.0 + z * (1.0 - sig)))).astype(BF16)
        dy_b = dy.astype(BF16)

        tril = _iota((CH, CH), 1) <= _iota((CH, CH), 0)
        lane_a = _iota((CH, LANE), 1) < 64
        cbm = [_dot_nt(cg[g], bg[g]) for g in range(2)]
        dcbm = [jnp.zeros((CH, CH), F32), jnp.zeros((CH, CH), F32)]
        seg_rows = jnp.zeros((CH, LANE), F32)
        seg_cols = jnp.zeros((LANE, CH), F32)
        row_id, col_id = _iota((CH, LANE), 0), _iota((CH, LANE), 1)
        dx_pairs = []
        for hp in range(NH // 2):
            g = hp // 4
            xp = xdt_b[:, hp * LANE:(hp + 1) * LANE]
            dyp_f = dy[:, hp * LANE:(hp + 1) * LANE]
            dyp = dy_b[:, hp * LANE:(hp + 1) * LANE]
            halves = []
            for k in range(2):
                h = 2 * hp + k
                lane = lane_a if k == 0 else jnp.logical_not(lane_a)
                dec = _decay(q, h, tril)
                gm = cbm[g] * dec
                dgm = _dot_nt(jnp.where(lane, dyp_f, 0.0).astype(BF16), xp)
                dcbm[g] = dcbm[g] + dgm * dec
                prod = dgm * gm
                seg_rows = jnp.where(col_id == h, jnp.sum(prod, axis=1, keepdims=True), seg_rows)
                seg_cols = jnp.where(row_id == h, jnp.sum(prod, axis=0, keepdims=True), seg_cols)
                halves.append(_dot_tn(gm.astype(BF16), dyp))
            dx_pairs.append(jnp.where(lane_a, halves[0], halves[1]))
        dxdt_diag = jnp.concatenate(dx_pairs, axis=1)

        qv = jnp.concatenate([_dot(bg[g], dhb[:, g * 512:(g + 1) * 512]) for g in range(2)], axis=1)
        y_off = jnp.concatenate([_dot(cg[g], hb[:, g * 512:(g + 1) * 512]) for g in range(2)], axis=1) * e_b
        xfq = xdt * f_b * qv
        dxdt = dxdt_diag + f_b * qv
        tdt = jnp.sum(dh_new * h_in, axis=0, keepdims=True) * t_b
        per_head = _pick(jnp.concatenate([xfq, dy * y_off, dxdt * xs, dy * xs, jnp.broadcast_to(tdt, (8, D))],
                                         axis=0), red)
        fdf, dyoff_h, dxdtxs_h, dyxs_h = [per_head[k * CH:(k + 1) * CH] for k in range(4)]
        dcs = seg_rows - seg_cols.T + dyoff_h - fdf
        last = per_head[4 * CH:4 * CH + 1] + jnp.sum(fdf, axis=0, keepdims=True)
        dcs = dcs + jnp.where(_iota((CH, LANE), 0) == CH - 1, last, 0.0)
        tri_t = (_iota((CH, CH), 1) >= _iota((CH, CH), 0)).astype(BF16)
        da = _pick_left(tri_t, dcs)
        ddt = da * q["a_row"] + dxdtxs_h
        dxs = dxdt * dt_b + db_ref[...] * dy
        ddt_raw = ddt * _sigmoid(q["pre"])
        dsm_ref[0:1, :] += jnp.sum(ddt_raw, axis=0, keepdims=True)
        dsm_ref[1:2, :] += jnp.sum(da * q["dt"], axis=0, keepdims=True) * q["a_row"]
        dsm_ref[2:3, :] += jnp.sum(dyxs_h, axis=0, keepdims=True)
        edy = (e_b * dy).astype(BF16)
        xf = (xdt * f_b).astype(BF16)
        dbs, dcs_g, dhs = [], [], []
        for g in range(2):
            sl = slice(g * 512, (g + 1) * 512)
            dcb_b = dcbm[g].astype(BF16)
            dcs_g.append(_dot(dcb_b, bg[g]) + _dot_nt(edy[:, sl], hb[:, sl]))
            dbs.append(_dot_tn(dcb_b, cg[g]) + _dot_nt(xf[:, sl], dhb[:, sl]))
            dhs.append(_dot_tn(cg[g], edy[:, sl]))
        dh_ref[...] = t_b * dh_new + jnp.concatenate(dhs, axis=1)
        dbc = jnp.concatenate(dbs + dcs_g, axis=1)

        def conv_bwd(dact, pre, x_raw, w, nxt_ref, lo):
            s = _sigmoid(pre)
            dconv = dact * (s * (1.0 + pre * (1.0 - s)))
            nxt8 = nxt_ref[...]
            row8 = _iota(nxt8.shape, 0)
            hi = lo + dconv.shape[1]
            dcb_ref[:, lo:hi] += jnp.sum(dconv, axis=0, keepdims=True)
            later = [dconv]
            for s_ in (1, 2, 3):
                rolled = pltpu.roll(dconv, CH - s_, 0)
                tail = jnp.where(row8 >= 8 - s_, pltpu.roll(nxt8, 8 - s_, 0), rolled[CH - 8:])
                later.append(jnp.concatenate([rolled[:CH - 8], tail], axis=0))
            dx = None
            for s_, up in enumerate(later):
                k = 3 - s_
                dcw_ref[k:k + 1, lo:hi] += jnp.sum(up * x_raw, axis=0, keepdims=True)
                dx = w[k:k + 1, :] * up if dx is None else dx + w[k:k + 1, :] * up
            nxt_ref[...] = dconv[:8]
            return dx

        dx_ref[:, 0:D] = conv_bwd(dxs, xs_c, xs_ref[...], cw[:, :D], nxs_ref, 0).astype(BF16)
        dx_ref[:, D:D + 512] = conv_bwd(dbc, bc_c, bc_ref[...], cw[:, D:], nbc_ref, D).astype(BF16)
        dx_ref[:, D + 512:D + 640] = ddt_raw.astype(BF16)
        dx_ref[:, D + 640:] = jnp.zeros((CH, D - 640), BF16)

    def col(width, blk):
        return pl.BlockSpec((CH, width), lambda i: (NC - 1 - i, blk))

    def full(a):
        return pl.BlockSpec(a.shape, lambda i: (0,) * len(a.shape))

    acc_shapes = [SDS((4, 1536), F32), SDS((1, 1536), F32), SDS((8, LANE), F32), SDS((1, D), F32)]
    return pl.pallas_call(
        body, name="ssm_bwd", grid=(NC,),
        in_specs=[col(D, 5), col(512, 12), col(LANE, 52), col(D, 4),
                  col(D, 0), col(D, 0), pl.BlockSpec((None, CH, D), lambda i: (NC - 1 - i, 0, 0)), col(D + 512, 0),
                  full(cw), full(dtb), full(alog), full(d_b), full(nw)],
        out_specs=[col(D, 0), col(2 * D, 0)] + [full(a) for a in acc_shapes],
        out_shape=[SDS((S, D), BF16), SDS((S, 2 * D), BF16)] + acc_shapes,
        scratch_shapes=[pltpu.VMEM((CH, D), F32), pltpu.VMEM((8, D), F32), pltpu.VMEM((8, 512), F32)],
        compiler_params=_cp(("arbitrary",)),
    )(proj, proj, proj, proj, dn, y_save, states, conv_out, cw, dtb, alog, d_b, nw)


def _outproj_loss(mix, w_out, x, tgt, nw, attn_pre, proj):
    tm = 256

    def body(mix_ref, w_ref, x_ref, t_ref, nw_ref, pre_ref, g_ref,
             dy_ref, dn_ref, do_ref, delta_ref, dg_ref, dw_ref, dnw_ref, loss_ref):
        @pl.when(pl.program_id(0) == 0)
        def _():
            dw_ref[...] = jnp.zeros_like(dw_ref)
            dnw_ref[...] = jnp.zeros_like(dnw_ref)
            loss_ref[...] = jnp.zeros_like(loss_ref)

        mixv, w = mix_ref[...], w_ref[...]
        out = _dot(mixv, w)
        r = lax.rsqrt(jnp.mean(out * out, axis=-1, keepdims=True) + EPS)
        nh = out * r
        nw_v = nw_ref[...]
        err = x_ref[...] + nh * nw_v - t_ref[...]
        loss_ref[...] += 0.5 * jnp.sum(jnp.mean(err * err, axis=-1, keepdims=True), axis=0, keepdims=True)
        dy = err * (1.0 / D)
        dy_ref[...] = dy
        dnw_ref[...] += jnp.sum(dy * nh, axis=0, keepdims=True)
        gdn = dy * nw_v
        dout = (r * (gdn - nh * jnp.mean(gdn * nh, axis=-1, keepdims=True))).astype(BF16)
        dmix = _dot_nt(dout, w)
        dw_ref[...] += _dot_tn(mixv, dout)
        dn_ref[...] = dmix[:, D:]
        dm, g, pre_v = dmix[:, :D], g_ref[...], pre_ref[...]
        sig = _sigmoid(g)
        do = dm * (g * sig)
        do_ref[...] = do
        dg_ref[...] = (dm * pre_v * (sig * (1.0 + g * (1.0 - sig)))).astype(BF16)
        prod = do * pre_v
        same_head = (_iota((LANE, LANE), 0) // 64 == _iota((LANE, LANE), 1) // 64).astype(BF16)
        for cb in range(D // LANE):
            delta_ref[:, cb * LANE:(cb + 1) * LANE] = _pick(prod[:, cb * LANE:(cb + 1) * LANE], same_head)

    row = lambda w: pl.BlockSpec((tm, w), lambda i: (i, 0))
    full = lambda s: pl.BlockSpec(s, lambda i: (0, 0))
    return pl.pallas_call(
        body, name="outproj_loss", grid=(S // tm,),
        in_specs=[row(2 * D), full((2 * D, D)), row(D), row(D), full((1, D)), row(D),
                  pl.BlockSpec((tm, D), lambda i: (i, OFF_G // D))],
        out_specs=[row(D), row(D), row(D), row(D), row(D), full((2 * D, D)), full((1, D)), full((1, LANE))],
        out_shape=[SDS((S, D), F32)] * 4 + [SDS((S, D), BF16), SDS((2 * D, D), F32), SDS((1, D), F32),
                                            SDS((1, LANE), F32)],
        compiler_params=_cp(("arbitrary",)),
    )(mix, w_out, x, tgt, nw, attn_pre, proj)


def _inproj_bwd_dx(srcs, dxbcdt, w_all, x, dy, nw, hosted=None):
    tm = 512
    nk = DP // D
    n_host = len(hosted.arrays) if hosted else 0

    def body(*refs):
        src_refs = refs[:nk]
        w_ref, x_ref, dy_ref, nw_ref = refs[nk:nk + 4]
        host_in, refs = refs[nk + 4:nk + 4 + n_host], refs[nk + 4 + n_host:]
        gx_ref, dnw_ref = refs[:2]
        host_out, host_sems = refs[2:2 + n_host], refs[2 + n_host:]
        i = pl.program_id(0)

        @pl.when(i == 0)
        def _():
            if hosted:
                hosted.start(host_in, host_out, host_sems)
            dnw_ref[...] = jnp.zeros_like(dnw_ref)

        du = None
        for k, ref in enumerate(src_refs):
            part = _dot_nt(ref[...], w_ref[:, k * D:(k + 1) * D])
            du = part if du is None else du + part
        xf, nw_v = x_ref[...], nw_ref[...]
        r = lax.rsqrt(jnp.mean(xf * xf, axis=-1, keepdims=True) + EPS)
        xh = xf * r
        dnw_ref[...] += jnp.sum(du * xh, axis=0, keepdims=True)
        gdu = du * nw_v
        gx_ref[...] = r * (gdu - xh * jnp.mean(gdu * xh, axis=-1, keepdims=True)) + dy_ref[...]

        if hosted:
            pl.when(i == S // tm - 1)(lambda: hosted.finish(host_in, host_out, host_sems))

    row = pl.BlockSpec((tm, D), lambda i: (i, 0))
    row1 = pl.BlockSpec((tm, D), lambda i: (i, 1))
    one = pl.BlockSpec((1, D), lambda i: (0, 0))
    whole_w = pl.BlockSpec((D, DP), lambda i: (0, 0), pipeline_mode=pl.Buffered(1))
    args = [*srcs, dxbcdt, dxbcdt, w_all, x, dy, nw]
    in_specs = [row] * len(srcs) + [row, row1, whole_w, row, row, one]
    out_specs, out_shape, scratch = [row, one], [SDS((S, D), F32), SDS((1, D), F32)], []
    if hosted:
        args += hosted.arrays
        in_specs += [ANY] * n_host
        out_specs += [ANY] * n_host
        out_shape += hosted.out_shape
        scratch += hosted.scratch
    outs = pl.pallas_call(
        body, name="inproj_bwd_dx", grid=(S // tm,),
        in_specs=in_specs, out_specs=out_specs, out_shape=out_shape, scratch_shapes=scratch,
        compiler_params=_cp(("arbitrary",)),
    )(*args)
    return (outs[:2], outs[2:]) if hosted else outs


def _dw(u, dsec, name):
    ts = 1024
    ncol = dsec.shape[1] // D

    def body(u_ref, d_ref, o_ref):
        @pl.when(pl.program_id(1) == 0)
        def _():
            o_ref[...] = jnp.zeros_like(o_ref)

        o_ref[...] += _dot_tn(u_ref[...], d_ref[...])

    return pl.pallas_call(
        body, name=name, grid=(ncol, S // ts),
        in_specs=[pl.BlockSpec((ts, D), lambda j, i: (i, 0)), pl.BlockSpec((ts, D), lambda j, i: (i, j))],
        out_specs=pl.BlockSpec((D, D), lambda j, i: (0, j)),
        out_shape=SDS((D, ncol * D), F32),
        compiler_params=_cp(("parallel", "arbitrary")),
    )(u, dsec)


def _place():
    x, y, c = lax.axis_index("x"), lax.axis_index("y"), lax.axis_index("c")
    return x, y, c, 2 * x + y


def _chip_of(x, y, k):
    px = 1 - x if k & 2 else x
    py = 1 - y if k & 1 else y
    return px, py, 2 * px + py


def _remote(src, dst, send_sem, recv_sem, dev):
    return pltpu.make_async_remote_copy(src_ref=src, dst_ref=dst, send_sem=send_sem, recv_sem=recv_sem,
                                        device_id=dev, device_id_type=MESH)


def _gather_weights(w_in_b):
    half = w_in_b.shape[0] // 2
    quarter = half // 2

    def body(src, dst, send, recv):
        x, y, c, j = _place()
        me, sib = (x, y, c), (x, y, 1 - c)
        nbr = {"x": _chip_of(x, y, 2), "y": _chip_of(x, y, 1)}
        diag = _chip_of(x, y, 3)[2]
        started, arrivals = [], []

        def rows(n_quarter=None, sibling=False):
            base = (1 - c if sibling else c) * half
            return pl.ds(base, half) if n_quarter is None else pl.ds(base + n_quarter * quarter, quarter)

        def sem(n):
            return send.at[n], recv.at[n]

        def go(cp):
            cp.start()
            started.append(cp)

        own = _remote(src, dst.at[j], *sem(8), sib)
        go(own)
        for n, axis in enumerate("xy"):
            px, py, _ = nbr[axis]
            go(_remote(src.at[rows()], dst.at[j, rows()], *sem(n), (px, py, c)))
        for n, axis in enumerate("xy"):
            ox, oy, _ = nbr["y" if axis == "x" else "x"]
            pj = nbr[axis][2]
            _remote(src.at[rows()], dst.at[pj, rows()], *sem(n), me).wait_recv()
            go(_remote(dst.at[pj, rows(n)], dst.at[pj, rows(n)], *sem(2 + n), (ox, oy, c)))
            go(_remote(dst.at[pj, rows()], dst.at[pj, rows()], *sem(4 + n), sib))
            arrivals.append(_remote(src.at[rows()], dst.at[pj, rows(None, True)], *sem(4 + n), me))
        for n in range(2):
            _remote(dst.at[diag, rows(n)], dst.at[diag, rows(n)], *sem(2 + n), me).wait_recv()
            go(_remote(dst.at[diag, rows(n)], dst.at[diag, rows(n)], *sem(6 + n), sib))
            arrivals.append(_remote(dst.at[diag, rows(n, True)], dst.at[diag, rows(n, True)], *sem(6 + n), me))
        for cp in arrivals + [own]:
            cp.wait_recv()
        for cp in started:
            cp.wait_send()

    return pl.pallas_call(
        body, name="gather_weights", in_specs=[ANY], out_specs=ANY,
        out_shape=SDS((4,) + w_in_b.shape, BF16),
        scratch_shapes=[pltpu.SemaphoreType.DMA((9,)), pltpu.SemaphoreType.DMA((9,))],
        compiler_params=pltpu.CompilerParams(has_side_effects=True),
    )(w_in_b)


class _LateGather:
    def __init__(self, w_out_b, conv_w):
        self.arrays = [w_out_b, conv_w]
        self.out_shape = [SDS((4,) + w_out_b.shape, BF16), SDS((4,) + conv_w.shape, F32)]
        self.scratch = [pltpu.SemaphoreType.DMA((11,)), pltpu.SemaphoreType.DMA((11,))]

    def _plan(self, ins, outs, sems):
        x, y, c, j = _place()
        send, recv = sems
        (wo, cw), (gwo, gcw) = ins, outs
        half = wo.shape[0] // 2
        mine, theirs = pl.ds(c * half, half), pl.ds((1 - c) * half, half)
        me, sib = (x, y, c), (x, y, 1 - c)
        first, arrive, forward, last = [], [], [], []
        for k in (1, 2, 3):
            px, py, pj = _chip_of(x, y, k)
            first += [_remote(wo.at[mine], gwo.at[j, mine], send.at[k - 1], recv.at[k - 1], (px, py, c)),
                      _remote(cw, gcw.at[j], send.at[k + 2], recv.at[k + 2], (px, py, c))]
            arrive.append(_remote(wo.at[mine], gwo.at[pj, mine], send.at[k - 1], recv.at[k - 1], me))
            forward.append(_remote(gwo.at[pj, mine], gwo.at[pj, mine], send.at[k + 5], recv.at[k + 5], sib))
            last += [_remote(cw, gcw.at[pj], send.at[k + 2], recv.at[k + 2], me),
                     _remote(wo.at[theirs], gwo.at[pj, theirs], send.at[k + 5], recv.at[k + 5], me)]
        first += [_remote(wo, gwo.at[j], send.at[9], recv.at[9], sib),
                  _remote(cw, gcw.at[j], send.at[10], recv.at[10], sib)]
        last += first[-2:]
        return first, arrive, forward, last

    def start(self, ins, outs, sems):
        for cp in self._plan(ins, outs, sems)[0]:
            cp.start()

    def finish(self, ins, outs, sems):
        first, arrive, forward, last = self._plan(ins, outs, sems)
        for got, fwd in zip(arrive, forward):
            got.wait_recv()
            fwd.start()
        for cp in last:
            cp.wait_recv()
        for cp in first + forward:
            cp.wait_send()


class _PairExchange:
    def __init__(self, arrays):
        self.arrays = list(arrays)
        self.out_shape = [SDS((a.shape[0], a.shape[1] // 2, a.shape[2]), F32) for a in self.arrays]
        self.scratch = [pltpu.SemaphoreType.DMA((len(self.arrays),)) for _ in range(2)]

    def _copies(self, ins, outs, sems):
        x, y, c, _ = _place()
        for k, (src, dst) in enumerate(zip(ins, outs)):
            half = src.shape[1] // 2
            yield _remote(src.at[:, pl.ds((1 - c) * half, half)], dst, sems[0].at[k], sems[1].at[k], (x, y, 1 - c))

    def start(self, ins, outs, sems):
        for cp in self._copies(ins, outs, sems):
            cp.start()

    def finish(self, ins, outs, sems):
        for cp in self._copies(ins, outs, sems):
            cp.wait()


def _pair_exchange(arrays, name):
    halves = [a.shape[1] // 2 for a in arrays]
    n = len(arrays)

    def body(*refs):
        x, y, c, _ = _place()
        send, recv = refs[2 * n:]
        cps = [_remote(refs[k].at[:, pl.ds((1 - c) * halves[k], halves[k])], refs[n + k], send.at[k], recv.at[k],
                       (x, y, 1 - c)) for k in range(n)]
        for cp in cps:
            cp.start()
        for cp in cps:
            cp.wait()

    return pl.pallas_call(
        body, name=name, in_specs=[ANY] * n, out_specs=[ANY] * n,
        out_shape=[SDS((a.shape[0], h, a.shape[2]), F32) for a, h in zip(arrays, halves)],
        scratch_shapes=[pltpu.SemaphoreType.DMA((n,)), pltpu.SemaphoreType.DMA((n,))],
        compiler_params=pltpu.CompilerParams(has_side_effects=True),
    )(*arrays)


def _pair_sum(cidx, g, r, name):
    n, half, width = r.shape
    tr = min(half, 256)
    nt = half // tr

    def body(c_ref, g_ref, r_ref, o_ref):
        del c_ref
        o_ref[...] = (g_ref[...] + r_ref[...]).astype(BF16)

    return pl.pallas_call(
        body, name=name,
        grid_spec=pltpu.PrefetchScalarGridSpec(
            num_scalar_prefetch=1, grid=(n, nt),
            in_specs=[pl.BlockSpec((None, tr, width), lambda s, t, c: (s, c[0] * nt + t, 0)),
                      pl.BlockSpec((None, tr, width), lambda s, t, c: (s, t, 0))],
            out_specs=pl.BlockSpec((None, tr, width), lambda s, t, c: (s, t, 0))),
        out_shape=SDS(r.shape, BF16),
        compiler_params=_cp(("parallel", "parallel")),
    )(cidx, g, r)


class _ChipExchange:
    def __init__(self, arrays, rows):
        self.arrays, self.rows = list(arrays), list(rows)
        self.out_shape = [SDS((4,) + a.shape[1:], BF16) for a in self.arrays]
        self.scratch = [pltpu.SemaphoreType.DMA((3 * len(self.arrays),)) for _ in range(2)]

    def _copies(self, ins, outs, sems):
        x, y, c, j = _place()
        send, recv = sems
        for a, (src, dst, row) in enumerate(zip(ins, outs, self.rows)):
            for k in (1, 2, 3):
                px, py, pj = _chip_of(x, y, k)
                n = 3 * a + k - 1
                slot = pj if row is None else py
                yield (None if row is None else px == row, None if row is None else x == row,
                       _remote(src.at[slot], dst.at[j], send.at[n], recv.at[n], (px, py, c)),
                       _remote(src.at[0], dst.at[pj], send.at[n], recv.at[n], (x, y, c)))

    def start(self, ins, outs, sems):
        for sends, _, send, _ in self._copies(ins, outs, sems):
            if sends is None:
                send.start()
            else:
                pl.when(sends)(send.start)

    def finish(self, ins, outs, sems):
        for sends, owns, send, arrival in self._copies(ins, outs, sems):
            if sends is None:
                arrival.wait_recv()
                send.wait_send()
            else:
                pl.when(owns)(arrival.wait_recv)
                pl.when(sends)(send.wait_send)


def _small_exchange(small):
    def body(sm_ref, rs_ref, send, recv, lsem):
        x, y, c, j = _place()
        me = 2 * j + c
        local = pltpu.make_async_copy(sm_ref, rs_ref.at[me], lsem)
        local.start()
        cps = []
        for k in range(1, 8):
            px, py, _ = _chip_of(x, y, k >> 1)
            pc = 1 - c if k & 1 else c
            cps.append(_remote(sm_ref, rs_ref.at[me], send.at[k - 1], recv.at[k - 1], (px, py, pc)))
        for cp in cps:
            cp.start()
        for k in range(1, 8):
            _, _, pj = _chip_of(x, y, k >> 1)
            pc = 1 - c if k & 1 else c
            _remote(sm_ref, rs_ref.at[2 * pj + pc], send.at[k - 1], recv.at[k - 1], (x, y, c)).wait_recv()
        for cp in cps:
            cp.wait_send()
        local.wait()

    return pl.pallas_call(
        body, name="small_exchange", in_specs=[ANY], out_specs=ANY,
        out_shape=SDS((8,) + small.shape, F32),
        scratch_shapes=[pltpu.SemaphoreType.DMA((7,)), pltpu.SemaphoreType.DMA((7,)), pltpu.SemaphoreType.DMA],
        compiler_params=pltpu.CompilerParams(has_side_effects=True),
    )(small)


def _slot_sum(r, name):
    n, rows, width = r.shape
    tr = min(rows, 256)

    def body(r_ref, o_ref):
        acc = r_ref[0].astype(F32)
        for s in range(1, n):
            acc = acc + r_ref[s].astype(F32)
        o_ref[...] = acc

    return pl.pallas_call(
        body, name=name, grid=(rows // tr,),
        in_specs=[pl.BlockSpec((n, tr, width), lambda t: (0, t, 0))],
        out_specs=pl.BlockSpec((tr, width), lambda t: (t, 0)),
        out_shape=SDS((rows, width), F32),
        compiler_params=_cp(("parallel",)),
    )(r)


def _chip_sum(chip_idx, recv, own, name):
    n, rows, width = recv.shape
    tr = min(rows, 256)

    def body(j_ref, r_ref, own_ref, o_ref):
        acc = None
        for s in range(n):
            term = jnp.where(j_ref[0] == s, own_ref[...], r_ref[s]).astype(F32)
            acc = term if acc is None else acc + term
        o_ref[...] = acc

    return pl.pallas_call(
        body, name=name,
        grid_spec=pltpu.PrefetchScalarGridSpec(
            num_scalar_prefetch=1, grid=(rows // tr,),
            in_specs=[pl.BlockSpec((n, tr, width), lambda t, j: (0, t, 0)),
                      pl.BlockSpec((None, tr, width), lambda t, j: (j[0], t, 0))],
            out_specs=pl.BlockSpec((tr, width), lambda t, j: (t, 0))),
        out_shape=SDS((rows, width), F32),
        compiler_params=_cp(("parallel",)),
    )(chip_idx, recv, own)


def _chip_sum_rows(place, recv0, own0, recv1, own1, name):
    n, rows, width = recv0.shape
    tr = min(rows, 256)

    def body(p_ref, r0_ref, o0_ref, r1_ref, o1_ref, o_ref):
        first_row = p_ref[1] == 0
        own = jnp.where(first_row, o0_ref[...], o1_ref[...])
        acc = None
        for s in range(n):
            term = jnp.where(p_ref[0] == s, own, jnp.where(first_row, r0_ref[s], r1_ref[s])).astype(F32)
            acc = term if acc is None else acc + term
        o_ref[...] = acc

    recv = pl.BlockSpec((n, tr, width), lambda t, p: (0, t, 0))
    own = pl.BlockSpec((None, tr, width), lambda t, p: (p[2], t, 0))
    return pl.pallas_call(
        body, name=name,
        grid_spec=pltpu.PrefetchScalarGridSpec(
            num_scalar_prefetch=1, grid=(rows // tr,), in_specs=[recv, own, recv, own],
            out_specs=pl.BlockSpec((tr, width), lambda t, p: (t, 0))),
        out_shape=SDS((rows, width), F32),
        compiler_params=_cp(("parallel",)),
    )(place, recv0, own0, recv1, own1)


def _half_exchange(hw, ho):
    def body(hw_ref, ho_ref, tw_ref, to_ref, send, recv):
        x, y, c, _ = _place()
        sib = (x, y, 1 - c)
        cps = [_remote(hw_ref, tw_ref, send.at[0], recv.at[0], sib),
               _remote(ho_ref, to_ref, send.at[1], recv.at[1], sib)]
        for cp in cps:
            cp.start()
        for cp in cps:
            cp.wait()

    return pl.pallas_call(
        body, name="half_exchange", in_specs=[ANY, ANY], out_specs=[ANY, ANY],
        out_shape=[SDS(hw.shape, F32), SDS(ho.shape, F32)],
        scratch_shapes=[pltpu.SemaphoreType.DMA((2,)), pltpu.SemaphoreType.DMA((2,))],
        compiler_params=pltpu.CompilerParams(has_side_effects=True),
    )(hw, ho)


def _by_core(c, mine, theirs):
    return jnp.where(c == 0, jnp.concatenate([mine, theirs], axis=0), jnp.concatenate([theirs, mine], axis=0))


def _adamw(w, g, m, v, name):
    rows, width = w.shape
    tr = next(t for t in (256, rows // 3, rows) if rows % t == 0 and t % 8 == 0)

    def body(w_ref, g_ref, m_ref, v_ref, d_ref, nm_ref, nv_ref):
        gv = g_ref[...]
        nm = ADAM_B1 * m_ref[...] + (1.0 - ADAM_B1) * gv
        nv = ADAM_B2 * v_ref[...] + (1.0 - ADAM_B2) * (gv * gv)
        m_hat = nm / (1.0 - ADAM_B1 ** ADAM_STEP)
        v_hat = nv / (1.0 - ADAM_B2 ** ADAM_STEP)
        d_ref[...] = -ADAM_LR * (m_hat / (jnp.sqrt(v_hat) + ADAM_EPS) + ADAM_WD * w_ref[...])
        nm_ref[...] = nm
        nv_ref[...] = nv

    t = pl.BlockSpec((tr, width), lambda i: (i, 0))
    return pl.pallas_call(
        body, name=name, grid=(rows // tr,), in_specs=[t] * 4, out_specs=[t] * 3,
        out_shape=[SDS(w.shape, F32)] * 3, compiler_params=_cp(("parallel",)),
    )(w, g, m, v)


def _rowwise(a):
    return jnp.transpose(a, (2, 0, 1)).reshape(SHARD * D // LANE, LANE)


def _from_rowwise(a):
    return jnp.transpose(a.reshape(SHARD, 1, D), (1, 2, 0))


def _rows128(a, rows):
    flat = a.reshape(-1)
    return jnp.pad(flat, (0, rows * LANE - flat.shape[0])).reshape(rows, LANE)


def _pack_small(conv_w, norm_pre, conv_b, ssm_norm, norm_post, dtb, alog, dsk, extra=None):
    cw_rows = 48 if conv_w.shape[-1] == 1536 else 16
    extra = jnp.zeros((1, LANE), F32) if extra is None else _rows128(extra, 1)
    vec = jnp.concatenate([_rows128(dtb, 1), _rows128(alog, 1), _rows128(dsk, 1), extra, jnp.zeros((4, LANE), F32)],
                          axis=0)
    return jnp.concatenate([_rows128(conv_w, cw_rows), _rows128(norm_pre, 8), _rows128(conv_b, 16),
                            _rows128(ssm_norm, 8), _rows128(norm_post, 8), vec], axis=0)


def _unpack_small(p, cw_cols):
    cw_rows = 48 if cw_cols == 1536 else 16
    o = cw_rows
    conv_w = p[:cw_rows].reshape(-1)[:4 * cw_cols].reshape(1, 4, cw_cols)
    norm_pre = p[o:o + 8].reshape(1, D)
    conv_b = p[o + 8:o + 24].reshape(-1)[:1536].reshape(1, 1536)
    ssm_norm = p[o + 24:o + 32].reshape(1, D)
    norm_post = p[o + 32:o + 40].reshape(1, D)
    vec = p[o + 40:o + 48]
    return conv_w, norm_pre, conv_b, ssm_norm, norm_post, vec[0:1, :NH], vec[1:2, :NH], vec[2:3, :NH], vec[3, 0]


def _pad_lanes(a):
    return jnp.pad(a, ((0, 0), (0, LANE - a.shape[1])))


class _GradReduce:
    SPLIT = 2 * SHARD - OFF_G

    def __init__(self, xi, yi, ci):
        self.ci = ci
        self.cidx = jnp.reshape(ci, (1,)).astype(jnp.int32)
        self.place = jnp.stack([2 * xi + yi, xi, yi]).astype(jnp.int32)

    def pairs(self, dw_g, dw_z, dw_x, dw_out):
        cols = jnp.concatenate([dw_g[:, self.SPLIT:], dw_z, dw_x], axis=1)
        self.gw_hi = jnp.stack([cols[:, :SHARD], cols[:, SHARD:2 * SHARD]])
        self.go = dw_out.reshape(4, D // 2, D)
        return _PairExchange([self.gw_hi, self.go])

    def first(self, got):
        rw, ro = got
        self.pw_hi = _pair_sum(self.cidx, self.gw_hi, rw, "pair_sum_hi")
        self.po = _pair_sum(self.cidx, self.go, ro, "pair_sum_out")
        return _ChipExchange([self.pw_hi, self.po], [1, None])

    def first_done(self, got):
        self.rw_hi, self.ro = got

    def second(self, dw_q, dw_k, dw_v, dw_g):
        cols = jnp.concatenate([dw_q, dw_k, dw_v, dw_g[:, :self.SPLIT]], axis=1)
        gw = jnp.stack([cols[:, :SHARD], cols[:, SHARD:]])
        (rw,) = _pair_exchange([gw], "pair_exchange_lo")
        self.pw_lo = _pair_sum(self.cidx, gw, rw, "pair_sum_lo")
        return _ChipExchange([self.pw_lo], [0])

    def second_done(self, got):
        (self.rw_lo,) = got

    def result(self):
        half_in = _chip_sum_rows(self.place, self.rw_lo, self.pw_lo, self.rw_hi, self.pw_hi, "chip_sum_in")
        half_out = _chip_sum(self.place[0:1], self.ro, self.po, "chip_sum_out")
        their_in, their_out = _half_exchange(half_in, half_out)
        return _by_core(self.ci, half_in, their_in), _by_core(self.ci, half_out, their_out)


def kernel(x, norm_pre_w, w_in, conv_w, conv_b, dt_bias, a_log, d_skip, ssm_norm_w, w_out, norm_post_w, loss_target, m_norm_pre_w, m_w_in, m_conv_w, m_conv_b, m_dt_bias, m_a_log, m_d_skip, m_ssm_norm_w, m_w_out, m_norm_post_w, v_norm_pre_w, v_w_in, v_conv_w, v_conv_b, v_dt_bias, v_a_log, v_d_skip, v_ssm_norm_w, v_w_out, v_norm_post_w):
    xi, yi, ci = lax.axis_index("x"), lax.axis_index("y"), lax.axis_index("c")
    chip = 2 * xi + yi
    x2, tgt = x[0], loss_target[0]

    gin = _gather_weights(w_in[0].astype(BF16))
    w_all = jnp.concatenate([gin[0], gin[1], gin[2], gin[3], jnp.zeros((D, DP - 4 * SHARD), BF16)], axis=1)
    reduce = _GradReduce(xi, yi, ci)
    grad_x, small = _local_step(x2, tgt, w_all, _LateGather(w_out[0].astype(BF16), conv_w[0]), norm_pre_w, conv_b,
                                dt_bias, a_log, d_skip, ssm_norm_w, norm_post_w, reduce)[:2]
    g_in, g_out = reduce.result()
    g_small = _slot_sum(_small_exchange(small), "small_sum")
    g_cw, g_npre, g_cb, g_nssm, g_npost, g_dtb, g_alog, g_dsk, loss = _unpack_small(g_small, 1536)
    g_cw = lax.dynamic_slice_in_dim(g_cw, chip * 384, 384, axis=2)

    d_in, nm_in, nv_in = [_from_rowwise(a)[0] for a in _adamw(
        _rowwise(w_in), _rowwise(g_in[None]), _rowwise(m_w_in), _rowwise(v_w_in), "adamw_in")]
    d_out, nm_out, nv_out = _adamw(w_out[0], g_out, m_w_out[0], v_w_out[0], "adamw_out")
    packed = [_pack_small(*t) for t in (
        (conv_w, norm_pre_w, conv_b, ssm_norm_w, norm_post_w, dt_bias, a_log, d_skip),
        (g_cw, g_npre, g_cb, g_nssm, g_npost, g_dtb, g_alog, g_dsk),
        (m_conv_w, m_norm_pre_w, m_conv_b, m_ssm_norm_w, m_norm_post_w, m_dt_bias, m_a_log, m_d_skip),
        (v_conv_w, v_norm_pre_w, v_conv_b, v_ssm_norm_w, v_norm_post_w, v_dt_bias, v_a_log, v_d_skip))]
    small_out = [_unpack_small(p, 384)[:8] for p in _adamw(*packed, "adamw_small")]

    def ordered(cw_, npre, cb_, nssm, npost, dtb_, alog_, dsk_, big_in, big_out):
        return [npre, big_in[None], cw_, cb_, dtb_, alog_, dsk_, nssm, big_out[None], npost]

    grads = ordered(g_cw, g_npre, g_cb, g_nssm, g_npost, g_dtb, g_alog, g_dsk, g_in, g_out)
    deltas = ordered(*small_out[0], d_in, d_out)
    new_m = ordered(*small_out[1], nm_in, nm_out)
    new_v = ordered(*small_out[2], nv_in, nv_out)
    return (loss, grad_x[None], *grads, *deltas, *new_m, *new_v)


def _local_step(x2, tgt, w_all, late, norm_pre_w, conv_b, dt_bias, a_log, d_skip, ssm_norm_w,
                norm_post_w, reduce=None):
    dtb, alog = _pad_lanes(dt_bias), _pad_lanes(a_log)
    d_b = jnp.repeat(d_skip, 64, axis=1)

    if isinstance(late, _LateGather):
        (proj, u), (gout, gcw) = _inproj_fwd(x2, norm_pre_w, w_all, late)
        w_out_all = gout.reshape(2 * D, D)
        cw_all = jnp.concatenate([gcw[0], gcw[1], gcw[2], gcw[3]], axis=1)
    else:
        proj, u = _inproj_fwd(x2, norm_pre_w, w_all)
        w_out_all, cw_all = late
    mix, attn_pre, lse = _attn_fwd(proj, 1, _attn_fwd(proj, 4, _attn_fwd(proj, 16)), final=True)
    mix, y_save, states, conv_out = _ssm_fwd(proj, mix, cw_all, conv_b, dtb, alog, d_b, ssm_norm_w)

    dy, dn_ssm, do, delta, dg, dw_out, dnw_post, loss_part = _outproj_loss(mix, w_out_all, x2, tgt, norm_post_w,
                                                                          attn_pre, proj)
    dz, dxbcdt, dcw, dcb, dvec, dnw_ssm = _ssm_bwd(proj, dn_ssm, y_save, states, conv_out, cw_all, dtb, alog, d_b,
                                                   ssm_norm_w)
    dw_g, dw_z, dw_x = _dw(u, dg, "dw_in_g"), _dw(u, dz, "dw_in_z"), _dw(u, dxbcdt, "dw_in_xbcdt")
    acc = _attn_bwd(proj, do, lse, delta, 16, None, F32, reduce.pairs(dw_g, dw_z, dw_x, dw_out) if reduce else None)
    if reduce:
        acc, got = acc
    acc = _attn_bwd(proj, do, lse, delta, 4, acc, F32, reduce.first(got) if reduce else None)
    if reduce:
        acc, got = acc
        reduce.first_done(got)
    dq, dk, dv = _attn_bwd(proj, do, lse, delta, 1, acc, BF16)
    dw_q, dw_k, dw_v = _dw(u, dq, "dw_in_q"), _dw(u, dk, "dw_in_k"), _dw(u, dv, "dw_in_v")
    res = _inproj_bwd_dx([dq, dk, dv, dg, dz], dxbcdt, w_all, x2, dy, norm_pre_w,
                         reduce.second(dw_q, dw_k, dw_v, dw_g) if reduce else None)
    if reduce:
        res, got = res
        reduce.second_done(got)
    grad_x, dnw_pre = res
    dw_all = jnp.concatenate([dw_q, dw_k, dw_v, dw_g, dw_z, dw_x], axis=1)
    small = _pack_small(dcw, dnw_pre, dcb, dnw_ssm, dnw_post, dvec[0:1, :NH], dvec[1:2, :NH], dvec[2:3, :NH],
                        loss_part[:, :1])
    return grad_x, small, dw_all, dw_out
```

```python
import functools

import jax
import jax.numpy as jnp
from jax import lax
from jax.experimental import pallas as pl
from jax.experimental.pallas import tpu as pltpu

F32 = jnp.float32
BF16 = jnp.bfloat16
MESH = pl.DeviceIdType.MESH
SDS = jax.ShapeDtypeStruct
ANY = pl.BlockSpec(memory_space=pl.ANY)

S = 4096
D = 1024
DP = 7168
SHARD = 1668
OFF_G, OFF_Z = 3072, 4096
NH = 16
CH = 128
NC = S // CH
EPS = 1e-6
NEG = -1e30
LANE = 128
VMEM_LIMIT = 48 * 1024 * 1024

ADAM_LR, ADAM_B1, ADAM_B2, ADAM_EPS, ADAM_WD, ADAM_STEP = 0.001, 0.9, 0.999, 1e-08, 0.01, 10


def _cp(sem, **kw):
    return pltpu.CompilerParams(dimension_semantics=sem, vmem_limit_bytes=VMEM_LIMIT, **kw)


def _dot(a, b):
    return jnp.dot(a, b, preferred_element_type=F32)


def _dot_nt(a, b):
    return lax.dot_general(a, b, (((1,), (1,)), ((), ())), preferred_element_type=F32)


def _dot_tn(a, b):
    return lax.dot_general(a, b, (((0,), (0,)), ((), ())), preferred_element_type=F32)


def _pieces(x, n):
    out = []
    for _ in range(n):
        p = x.astype(BF16)
        out.append(p)
        x = x - p.astype(F32)
    return out


def _pick(x, sel, n=2):
    parts = [_dot(p, sel) for p in _pieces(x, n)]
    return functools.reduce(jnp.add, parts)


def _pick_left(sel, x, n=3):
    parts = [_dot(sel, p) for p in _pieces(x, n)]
    return functools.reduce(jnp.add, parts)


def _sigmoid(v):
    return 0.5 * jnp.tanh(0.5 * v) + 0.5


def _iota(shape, dim):
    return lax.broadcasted_iota(jnp.int32, shape, dim)


def _inproj_fwd(x, nw, w_all, hosted=None):
    tm, tn = 1024, 1024
    n_host = len(hosted.arrays) if hosted else 0

    def body(x_ref, nw_ref, w_ref, *refs):
        host_in, (proj_ref, u_ref), refs = refs[:n_host], refs[n_host:n_host + 2], refs[n_host + 2:]
        host_out, host_sems = refs[:n_host], refs[n_host:]
        i, j = pl.program_id(0), pl.program_id(1)
        if hosted:
            pl.when((i == 0) & (j == 0))(lambda: hosted.start(host_in, host_out, host_sems))

        @pl.when(j == 0)
        def _():
            xf = x_ref[...]
            r = lax.rsqrt(jnp.mean(xf * xf, axis=-1, keepdims=True) + EPS)
            u_ref[...] = (xf * r * nw_ref[...]).astype(BF16)

        proj_ref[...] = _dot(u_ref[...], w_ref[...])
        if hosted:
            pl.when((i == S // tm - 1) & (j == DP // tn - 1))(lambda: hosted.finish(host_in, host_out, host_sems))

    outs = pl.pallas_call(
        body, name="inproj_fwd", grid=(S // tm, DP // tn),
        in_specs=[pl.BlockSpec((tm, D), lambda i, j: (i, 0)), pl.BlockSpec((1, D), lambda i, j: (0, 0)),
                  pl.BlockSpec((D, tn), lambda i, j: (0, j))] + [ANY] * n_host,
        out_specs=[pl.BlockSpec((tm, tn), lambda i, j: (i, j)), pl.BlockSpec((tm, D), lambda i, j: (i, 0))]
        + [ANY] * n_host,
        out_shape=[SDS((S, DP), F32), SDS((S, D), BF16)] + (hosted.out_shape if hosted else []),
        scratch_shapes=hosted.scratch if hosted else [],
        compiler_params=_cp(("arbitrary", "arbitrary") if hosted else ("parallel", "arbitrary")),
    )(x, nw, w_all, *(hosted.arrays if hosted else []))
    return (outs[:2], outs[2:]) if hosted else outs


ATTN_QB = {1: 16, 4: 4, 16: 1}


def _unit_rows(r, u, d):
    return pl.ds(r + d * CH * u, CH, stride=d) if d > 1 else pl.ds(CH * u, CH)


def _for_units(d, qb, fn):
    for r in range(d):
        for u in range(qb):
            fn(r, u)


def _attn_mask(has_prev):
    qi, kj = _iota((2 * CH, 2 * CH), 0) & (CH - 1), _iota((2 * CH, 2 * CH), 1)
    cur_ok = (kj >= CH) & (kj - CH <= qi)
    prev_ok = (kj < CH) & (kj >= qi)
    return cur_ok | (prev_ok & has_prev)


def _stack_heads(v, lane_a):
    return jnp.concatenate([jnp.where(lane_a, v, 0.0), jnp.where(lane_a, 0.0, v)], axis=0).astype(BF16)


def _attn_specs(d, qb):
    rows, prows = CH * d * qb, CH * d
    nb = S // rows
    steps = (NH // 2) * nb

    def at(t):
        t = jnp.minimum(t, steps - 1)
        return t % nb, t // nb

    def cur(off):
        return pl.BlockSpec((rows, LANE), lambda t: (at(t)[0], off + at(t)[1]))

    def prev(off):
        return pl.BlockSpec((prows, LANE), lambda t: (jnp.maximum(at(t)[0] * qb - 1, 0), off + at(t)[1]))

    lag = pl.BlockSpec((rows, LANE), lambda t: at(jnp.maximum(t - 1, 0)))
    return nb, steps, cur, prev, lag


def _gather16(src_ref, dense_ref, tmp_ref):
    for a in range(4):
        tmp_ref[...] = src_ref[pl.ds(a, 4 * CH, stride=4), :]
        for b in range(4):
            dense_ref[a + 4 * b] = tmp_ref[pl.ds(b, CH, stride=4), :]


def _scatter16(dense_ref, dst_ref, tmp_ref):
    for a in range(4):
        for b in range(4):
            tmp_ref[pl.ds(b, CH, stride=4), :] = dense_ref[a + 4 * b]
        dst_ref[pl.ds(a, 4 * CH, stride=4), :] = tmp_ref[...]


def _unit_index(r, u, d):
    return (r,) if d == 16 else (_unit_rows(r, u, d), slice(None))


def _unit_kv(p_ref, c_ref, r, u, d):
    prev = p_ref[_unit_index(r, 0, d)] if u == 0 else c_ref[_unit_index(r, u - 1, d)]
    return jnp.concatenate([prev, c_ref[_unit_index(r, u, d)]], axis=0).astype(BF16)


def _dense_scratch(d, n):
    return [pltpu.VMEM((16, CH, LANE), F32)] * n + [pltpu.VMEM((4 * CH, LANE), F32)] if d == 16 else []


def _attn_fwd(proj, d, prior=None, final=False):
    qb = ATTN_QB[d]
    nb, steps, cur, prev, _ = _attn_specs(d, qb)
    n_prior = 2 if prior is not None else 0
    n_in, n_out = 5 + n_prior + final, 2 + final
    assert not (d == 16 and (n_prior or final))

    def body(*refs):
        ins, outs, scratch = refs[:n_in], refs[n_in:n_in + n_out], refs[n_in + n_out:]
        if d == 16:
            tmp_ref = scratch[-1]
            for src, dense in zip(ins, scratch):
                _gather16(src, dense, tmp_ref)
            block_outs, ins, outs = outs, scratch[:n_in], scratch[n_in:n_in + n_out]
        q_ref, kp_ref, kc_ref, vp_ref, vc_ref = ins[:5]
        prior_refs = ins[5:5 + n_prior]
        if final:
            g_ref, (mix_ref, o_ref, l_ref) = ins[-1], outs
        else:
            o_ref, l_ref = outs
        i = pl.program_id(0) % nb
        lane_a = _iota((CH, LANE), 1) < 64
        mask_first, mask_rest = _attn_mask(i > 0), _attn_mask(True)

        def unit(r, u):
            at = _unit_index(r, u, d)
            q2 = _stack_heads(q_ref[at] * 0.125, lane_a)
            k2, v2 = _unit_kv(kp_ref, kc_ref, r, u, d), _unit_kv(vp_ref, vc_ref, r, u, d)
            s = jnp.where(mask_first if u == 0 else mask_rest, _dot_nt(q2, k2), NEG)
            m = jnp.max(s, axis=1, keepdims=True)
            p = jnp.exp(s - m)
            l = jnp.sum(p, axis=1, keepdims=True)
            o2 = _dot(p.astype(BF16), v2) / l
            lse2 = m + jnp.log(l)
            o = jnp.where(lane_a, o2[:CH], o2[CH:])
            lse = jnp.where(lane_a, lse2[:CH], lse2[CH:])
            if n_prior:
                o_a, l_a = prior_refs[0][at], prior_refs[1][at]
                top = jnp.maximum(l_a, lse)
                e_a, e_b = jnp.exp(l_a - top), jnp.exp(lse - top)
                tot = e_a + e_b
                o = (e_a * o_a + e_b * o) / tot
                lse = top + jnp.log(tot)
            o_ref[at] = o
            l_ref[at] = lse
            if final:
                g = g_ref[at]
                mix_ref[at] = (o * (g * _sigmoid(g))).astype(BF16)

        _for_units(d, qb, unit)
        if d == 16:
            for dense, dst in zip(outs, block_outs):
                _scatter16(dense, dst, tmp_ref)

    in_specs = [cur(0), prev(8), cur(8), prev(16), cur(16)] + [cur(0)] * n_prior
    args = [proj] * 5 + (list(prior) if n_prior else [])
    out_specs, out_shape = [cur(0), cur(0)], [SDS((S, D), F32), SDS((S, D), F32)]
    if final:
        assert d == 1
        in_specs.append(cur(OFF_G // LANE))
        args.append(proj)
        out_specs, out_shape = [cur(0)] + out_specs, [SDS((S, 2 * D), BF16)] + out_shape
    return pl.pallas_call(
        body, name=f"attn_fwd_d{d}", grid=(steps,),
        in_specs=in_specs, out_specs=out_specs, out_shape=out_shape,
        scratch_shapes=_dense_scratch(d, n_in + n_out),
        compiler_params=_cp(("parallel",)),
    )(*args)


def _attn_bwd(proj, do, lse, delta, d, acc, out_dtype, hosted=None):
    qb = ATTN_QB[d]
    nb, steps, cur, prev, lag = _attn_specs(d, qb)
    has_acc = acc is not None
    n_in = 11 if has_acc else 8
    n_host = len(hosted.arrays) if hosted else 0
    assert not (d == 16 and (has_acc or out_dtype != F32))
    rows = CH * d * qb
    carry = (2, 16, CH, LANE) if d == 16 else (2, rows, LANE)

    def body(*refs):
        ins, host_in, refs = refs[:n_in], refs[n_in:n_in + n_host], refs[n_in + n_host:]
        (dq_ref, dk_ref, dv_ref), host_out, scratch = refs[:3], refs[3:3 + n_host], refs[3 + n_host:]
        if hosted:
            scratch, host_sems = scratch[:-len(hosted.scratch)], scratch[-len(hosted.scratch):]
        ck_ref, cv_ref = scratch[:2]
        dq_f32 = dq_ref if out_dtype == F32 else scratch[2]
        t = pl.program_id(0)
        i = t % nb
        if hosted:
            pl.when(t == 0)(lambda: hosted.start(host_in, host_out, host_sems))
        if d == 16:
            dense, dq_f32, tmp_ref = scratch[2:2 + n_in], scratch[2 + n_in], scratch[-1]

            @pl.when(t < steps)
            def _():
                for src, dst in zip(ins, dense):
                    _gather16(src, dst, tmp_ref)

            ins = dense
        q_ref, kp_ref, kc_ref, vp_ref, vc_ref, do_ref, lse_ref, dl_ref = ins[:8]
        if has_acc:
            aq_ref, ak_ref, av_ref = ins[8:11]
        slot = t & 1
        now_k, now_v, old_k, old_v = ck_ref.at[slot], cv_ref.at[slot], ck_ref.at[1 - slot], cv_ref.at[1 - slot]
        lane_a = _iota((CH, LANE), 1) < 64
        mask_first, mask_rest = _attn_mask(i > 0), _attn_mask(True)

        @pl.when(t == 0)
        def _():
            ck_ref[1] = jnp.zeros(carry[1:], F32)
            cv_ref[1] = jnp.zeros(carry[1:], F32)

        def unit(r, u):
            at = _unit_index(r, u, d)
            q2 = _stack_heads(q_ref[at] * 0.125, lane_a)
            do2 = _stack_heads(do_ref[at], lane_a)
            k2, v2 = _unit_kv(kp_ref, kc_ref, r, u, d), _unit_kv(vp_ref, vc_ref, r, u, d)
            lsev, dlv = lse_ref[at], dl_ref[at]
            lse2 = jnp.concatenate([lsev[:, 0:1], lsev[:, 64:65]], axis=0)
            dl2 = jnp.concatenate([dlv[:, 0:1], dlv[:, 64:65]], axis=0)
            p = jnp.exp(jnp.where(mask_first if u == 0 else mask_rest, _dot_nt(q2, k2), NEG) - lse2)
            ds = (p * (_dot_nt(do2, v2) - dl2)).astype(BF16)
            dq2 = _dot(ds, k2)
            dk2 = _dot_tn(ds, q2)
            dv2 = _dot_tn(p.astype(BF16), do2)
            dq = jnp.where(lane_a, dq2[:CH], dq2[CH:]) * 0.125
            if has_acc:
                dq = dq + aq_ref[at]
            dq_f32[at] = dq
            if u == 0:
                before = _unit_index(r, qb - 1, d)
                old_k[before] += dk2[:CH]
                old_v[before] += dv2[:CH]
            else:
                before = _unit_index(r, u - 1, d)
                now_k[before] += dk2[:CH]
                now_v[before] += dv2[:CH]
            now_k[at] = dk2[CH:]
            now_v[at] = dv2[CH:]

        @pl.when(t < steps)
        def _():
            _for_units(d, qb, unit)
            if d == 16:
                _scatter16(dq_f32, dq_ref, tmp_ref)
            elif out_dtype != F32:
                dq_ref[...] = dq_f32[...].astype(out_dtype)

        if d == 16:
            _scatter16(old_k, dk_ref, tmp_ref)
            _scatter16(old_v, dv_ref, tmp_ref)
        else:
            dk, dv = old_k[...], old_v[...]
            if has_acc:
                dk, dv = dk + ak_ref[...], dv + av_ref[...]
            dk_ref[...] = dk.astype(out_dtype)
            dv_ref[...] = dv.astype(out_dtype)
        if hosted:
            pl.when(t == steps)(lambda: hosted.finish(host_in, host_out, host_sems))

    in_specs = [cur(0), prev(8), cur(8), prev(16), cur(16), cur(0), cur(0), cur(0)]
    args = [proj, proj, proj, proj, proj, do, lse, delta]
    if has_acc:
        in_specs += [cur(0), lag, lag]
        args += list(acc)
    scratch = [pltpu.VMEM(carry, F32), pltpu.VMEM(carry, F32)]
    if d == 16:
        scratch += _dense_scratch(d, n_in + 1)
    elif out_dtype != F32:
        scratch.append(pltpu.VMEM((rows, LANE), F32))
    out_specs, out_shape = [cur(0), lag, lag], [SDS((S, D), out_dtype)] * 3
    if hosted:
        args += hosted.arrays
        in_specs += [ANY] * n_host
        out_specs += [ANY] * n_host
        out_shape += hosted.out_shape
        scratch += hosted.scratch
    outs = pl.pallas_call(
        body, name=f"attn_bwd_d{d}", grid=(steps + 1,),
        in_specs=in_specs, out_specs=out_specs, out_shape=out_shape,
        scratch_shapes=scratch, compiler_params=_cp(("arbitrary",)),
    )(*args)
    return (outs[:3], outs[3:]) if hosted else outs


def _conv_taps(cur, prev8, first):
    row8 = _iota(prev8.shape, 0)
    prev8 = jnp.where(first, 0.0, prev8)
    taps = []
    for s in (3, 2, 1):
        rolled = pltpu.roll(cur, s, 0)
        head = jnp.where(row8 < s, pltpu.roll(prev8, s, 0), rolled[:8])
        taps.append(jnp.concatenate([head, rolled[8:]], axis=0))
    return taps + [cur]


def _conv(taps, w, b):
    acc = b + w[0:1, :] * taps[0]
    for k in (1, 2, 3):
        acc = acc + w[k:k + 1, :] * taps[k]
    return acc


def _expand():
    return (_iota((LANE, D), 1) // 64 == _iota((LANE, D), 0)).astype(BF16)


def _reduce():
    return (_iota((D, LANE), 0) // 64 == _iota((D, LANE), 1)).astype(BF16)


def _ssd_common(xs_c, bc_c, dt_raw, dtb, alog):
    head_lane = _iota((CH, LANE), 1) < NH
    xs = xs_c * _sigmoid(xs_c)
    bc = bc_c * _sigmoid(bc_c)
    pre = dt_raw + dtb
    dt = jnp.where(head_lane, jnp.maximum(pre, 0.0) + jnp.log(1.0 + jnp.exp(-jnp.abs(pre))), 0.0)
    a_row = jnp.where(head_lane[0:1], -jnp.exp(alog), 0.0)
    tri = (_iota((CH, CH), 1) <= _iota((CH, CH), 0)).astype(BF16)
    cs = _pick_left(tri, dt * a_row)
    cs_last = cs[CH - 1:CH, :]
    wide = _pick(jnp.concatenate([dt, jnp.exp(cs), jnp.exp(cs_last - cs)], axis=0), _expand())
    dt_b, e_b, f_b = wide[:CH], wide[CH:2 * CH], wide[2 * CH:]
    return dict(xs=xs, bc=bc, pre=pre, dt=dt, a_row=a_row, cs=cs, cs_t=cs.T, dt_b=dt_b, e_b=e_b, f_b=f_b,
                t_b=e_b[CH - 1:CH, :])


def _groups(bc):
    bcb = bc.astype(BF16)
    return [bcb[:, 0:128], bcb[:, 128:256]], [bcb[:, 256:384], bcb[:, 384:512]]


def _decay(q, h, tril):
    seg = q["cs"][:, h:h + 1] - q["cs_t"][h:h + 1, :]
    return jnp.exp(jnp.where(tril, seg, NEG))


def _ssm_fwd(proj, mix, cw, cb, dtb, alog, d_b, nw):
    def body(xs_ref, xsp_ref, bc_ref, bcp_ref, dt_ref, z_ref, cw_ref, cb_ref, dtb_ref, alog_ref, db_ref, nw_ref,
             mix_in_ref, mix_ref, y_ref, st_ref, conv_ref, h_ref):
        del mix_in_ref
        i = pl.program_id(0)

        @pl.when(i == 0)
        def _():
            h_ref[...] = jnp.zeros_like(h_ref)

        cw, cb = cw_ref[...], cb_ref[...]
        xs_c = _conv(_conv_taps(xs_ref[...], xsp_ref[...], i == 0), cw[:, :D], cb[:, :D])
        bc_c = _conv(_conv_taps(bc_ref[...], bcp_ref[...], i == 0), cw[:, D:], cb[:, D:])
        conv_ref[:, :D] = xs_c
        conv_ref[:, D:] = bc_c
        q = _ssd_common(xs_c, bc_c, dt_ref[...], dtb_ref[...], alog_ref[...])
        bg, cg = _groups(q["bc"])
        xs = q["xs"]
        xdt = xs * q["dt_b"]
        xdt_b = xdt.astype(BF16)
        h_in = h_ref[...]
        st_ref[...] = h_in
        hb = h_in.astype(BF16)
        tril = _iota((CH, CH), 1) <= _iota((CH, CH), 0)
        lane_a = _iota((CH, LANE), 1) < 64
        cbm = [_dot_nt(cg[g], bg[g]) for g in range(2)]
        pairs = []
        for hp in range(NH // 2):
            xp = xdt_b[:, hp * LANE:(hp + 1) * LANE]
            ya = _dot((cbm[hp // 4] * _decay(q, 2 * hp, tril)).astype(BF16), xp)
            yb = _dot((cbm[hp // 4] * _decay(q, 2 * hp + 1, tril)).astype(BF16), xp)
            pairs.append(jnp.where(lane_a, ya, yb))
        y_diag = jnp.concatenate(pairs, axis=1)
        y_off = jnp.concatenate([_dot(cg[g], hb[:, g * 512:(g + 1) * 512]) for g in range(2)], axis=1) * q["e_b"]
        y = y_diag + y_off + db_ref[...] * xs
        y_ref[...] = y
        xf = (xdt * q["f_b"]).astype(BF16)
        h_ref[...] = q["t_b"] * h_in + jnp.concatenate(
            [_dot_tn(bg[g], xf[:, g * 512:(g + 1) * 512]) for g in range(2)], axis=1)
        z = z_ref[...]
        yz = y * (z * _sigmoid(z))
        outs = []
        for g in range(2):
            v = yz[:, g * 512:(g + 1) * 512]
            outs.append(v * lax.rsqrt(jnp.mean(v * v, axis=-1, keepdims=True) + EPS))
        mix_ref[...] = (jnp.concatenate(outs, axis=1) * nw_ref[...]).astype(BF16)

    def col(width, blk, prev=False):
        if prev:
            return pl.BlockSpec((8, width), lambda i: (jnp.maximum(i * (CH // 8) - 1, 0), blk))
        return pl.BlockSpec((CH, width), lambda i: (i, blk))

    def full(a):
        return pl.BlockSpec(a.shape, lambda i: (0,) * a.ndim)

    return pl.pallas_call(
        body, name="ssm_fwd", grid=(NC,),
        in_specs=[col(D, 5), col(D, 5, True), col(512, 12), col(512, 12, True), col(LANE, 52), col(D, 4),
                  full(cw), full(cb), full(dtb), full(alog), full(d_b), full(nw), ANY],
        out_specs=[col(D, 1), col(D, 0), pl.BlockSpec((None, CH, D), lambda i: (i, 0, 0)), col(D + 512, 0)],
        out_shape=[SDS((S, 2 * D), BF16), SDS((S, D), F32), SDS((NC, CH, D), F32), SDS((S, D + 512), F32)],
        scratch_shapes=[pltpu.VMEM((CH, D), F32)],
        input_output_aliases={12: 0},
        compiler_params=_cp(("arbitrary",)),
    )(proj, proj, proj, proj, proj, proj, cw, cb, dtb, alog, d_b, nw, mix)


def _ssm_bwd(proj, dn, y_save, states, conv_out, cw, dtb, alog, d_b, nw):
    def body(xs_ref, bc_ref, dt_ref, z_ref, dn_ref, y_ref, st_ref, conv_ref,
             cw_ref, dtb_ref, alog_ref, db_ref, nw_ref,
             dz_ref, dx_ref, dcw_ref, dcb_ref, dsm_ref, dnw_ref, dh_ref, nxs_ref, nbc_ref):
        i = pl.program_id(0)
        ci = NC - 1 - i

        @pl.when(i == 0)
        def _():
            for ref in (dcw_ref, dcb_ref, dsm_ref, dnw_ref, dh_ref, nxs_ref, nbc_ref):
                ref[...] = jnp.zeros_like(ref)

        cw = cw_ref[...]
        xs_c, bc_c = conv_ref[:, :D], conv_ref[:, D:]
        q = _ssd_common(xs_c, bc_c, dt_ref[...], dtb_ref[...], alog_ref[...])
        bg, cg = _groups(q["bc"])
        xs, dt_b, e_b, f_b, t_b = q["xs"], q["dt_b"], q["e_b"], q["f_b"], q["t_b"]
        xdt = xs * dt_b
        xdt_b = xdt.astype(BF16)
        h_in = st_ref[...]
        hb = h_in.astype(BF16)
        dh_new = dh_ref[...]
        dhb = dh_new.astype(BF16)
        red = _reduce()

        z, y, dn, nw_v = z_ref[...], y_ref[...], dn_ref[...], nw_ref[...]
        sig = _sigmoid(z)
        sz = z * sig
        yz = y * sz
        gdn = dn * nw_v
        dyz, dnw = [], []
        for g in range(2):
            v, gv = yz[:, g * 512:(g + 1) * 512], gdn[:, g * 512:(g + 1) * 512]
            r = lax.rsqrt(jnp.mean(v * v, axis=-1, keepdims=True) + EPS)
            dnw.append(dn[:, g * 512:(g + 1) * 512] * v * r)
            dyz.append(r * (gv - v * (r * r) * jnp.mean(gv * v, axis=-1, keepdims=True)))
        dyz = jnp.concatenate(dyz, axis=1)
        dnw_ref[...] += jnp.sum(jnp.concatenate(dnw, axis=1), axis=0, keepdims=True)
        dy = dyz * sz
        dz_ref[...] = (dyz * y * (sig * (1.0 + z * (1.0 - sig)))).astype(BF16)
        dy_b = dy.astype(BF16)

        tril = _iota((CH, CH), 1) <= _iota((CH, CH), 0)
        lane_a = _iota((CH, LANE), 1) < 64
        cbm = [_dot_nt(cg[g], bg[g]) for g in range(2)]
        dcbm = [jnp.zeros((CH, CH), F32), jnp.zeros((CH, CH), F32)]
        seg_rows = jnp.zeros((CH, LANE), F32)
        seg_cols = jnp.zeros((LANE, CH), F32)
        row_id, col_id = _iota((CH, LANE), 0), _iota((CH, LANE), 1)
        dx_pairs = []
        for hp in range(NH // 2):
            g = hp // 4
            xp = xdt_b[:, hp * LANE:(hp + 1) * LANE]
            dyp_f = dy[:, hp * LANE:(hp + 1) * LANE]
            dyp = dy_b[:, hp * LANE:(hp + 1) * LANE]
            halves = []
            for k in range(2):
                h = 2 * hp + k
                lane = lane_a if k == 0 else jnp.logical_not(lane_a)
                dec = _decay(q, h, tril)
                gm = cbm[g] * dec
                dgm = _dot_nt(jnp.where(lane, dyp_f, 0.0).astype(BF16), xp)
                dcbm[g] = dcbm[g] + dgm * dec
                prod = dgm * gm
                seg_rows = jnp.where(col_id == h, jnp.sum(prod, axis=1, keepdims=True), seg_rows)
                seg_cols = jnp.where(row_id == h, jnp.sum(prod, axis=0, keepdims=True), seg_cols)
                halves.append(_dot_tn(gm.astype(BF16), dyp))
            dx_pairs.append(jnp.where(lane_a, halves[0], halves[1]))
        dxdt_diag = jnp.concatenate(dx_pairs, axis=1)

        qv = jnp.concatenate([_dot(bg[g], dhb[:, g * 512:(g + 1) * 512]) for g in range(2)], axis=1)
        y_off = jnp.concatenate([_dot(cg[g], hb[:, g * 512:(g + 1) * 512]) for g in range(2)], axis=1) * e_b
        xfq = xdt * f_b * qv
        dxdt = dxdt_diag + f_b * qv
        tdt = jnp.sum(dh_new * h_in, axis=0, keepdims=True) * t_b
        per_head = _pick(jnp.concatenate([xfq, dy * y_off, dxdt * xs, dy * xs, jnp.broadcast_to(tdt, (8, D))],
                                         axis=0), red)
        fdf, dyoff_h, dxdtxs_h, dyxs_h = [per_head[k * CH:(k + 1) * CH] for k in range(4)]
        dcs = seg_rows - seg_cols.T + dyoff_h - fdf
        last = per_head[4 * CH:4 * CH + 1] + jnp.sum(fdf, axis=0, keepdims=True)
        dcs = dcs + jnp.where(_iota((CH, LANE), 0) == CH - 1, last, 0.0)
        tri_t = (_iota((CH, CH), 1) >= _iota((CH, CH), 0)).astype(BF16)
        da = _pick_left(tri_t, dcs)
        ddt = da * q["a_row"] + dxdtxs_h
        dxs = dxdt * dt_b + db_ref[...] * dy
        ddt_raw = ddt * _sigmoid(q["pre"])
        dsm_ref[0:1, :] += jnp.sum(ddt_raw, axis=0, keepdims=True)
        dsm_ref[1:2, :] += jnp.sum(da * q["dt"], axis=0, keepdims=True) * q["a_row"]
        dsm_ref[2:3, :] += jnp.sum(dyxs_h, axis=0, keepdims=True)
        edy = (e_b * dy).astype(BF16)
        xf = (xdt * f_b).astype(BF16)
        dbs, dcs_g, dhs = [], [], []
        for g in range(2):
            sl = slice(g * 512, (g + 1) * 512)
            dcb_b = dcbm[g].astype(BF16)
            dcs_g.append(_dot(dcb_b, bg[g]) + _dot_nt(edy[:, sl], hb[:, sl]))
            dbs.append(_dot_tn(dcb_b, cg[g]) + _dot_nt(xf[:, sl], dhb[:, sl]))
            dhs.append(_dot_tn(cg[g], edy[:, sl]))
        dh_ref[...] = t_b * dh_new + jnp.concatenate(dhs, axis=1)
        dbc = jnp.concatenate(dbs + dcs_g, axis=1)

        def conv_bwd(dact, pre, x_raw, w, nxt_ref, lo):
            s = _sigmoid(pre)
            dconv = dact * (s * (1.0 + pre * (1.0 - s)))
            nxt8 = nxt_ref[...]
            row8 = _iota(nxt8.shape, 0)
            hi = lo + dconv.shape[1]
            dcb_ref[:, lo:hi] += jnp.sum(dconv, axis=0, keepdims=True)
            later = [dconv]
            for s_ in (1, 2, 3):
                rolled = pltpu.roll(dconv, CH - s_, 0)
                tail = jnp.where(row8 >= 8 - s_, pltpu.roll(nxt8, 8 - s_, 0), rolled[CH - 8:])
                later.append(jnp.concatenate([rolled[:CH - 8], tail], axis=0))
            dx = None
            for s_, up in enumerate(later):
                k = 3 - s_
                dcw_ref[k:k + 1, lo:hi] += jnp.sum(up * x_raw, axis=0, keepdims=True)
                dx = w[k:k + 1, :] * up if dx is None else dx + w[k:k + 1, :] * up
            nxt_ref[...] = dconv[:8]
            return dx

        dx_ref[:, 0:D] = conv_bwd(dxs, xs_c, xs_ref[...], cw[:, :D], nxs_ref, 0).astype(BF16)
        dx_ref[:, D:D + 512] = conv_bwd(dbc, bc_c, bc_ref[...], cw[:, D:], nbc_ref, D).astype(BF16)
        dx_ref[:, D + 512:D + 640] = ddt_raw.astype(BF16)
        dx_ref[:, D + 640:] = jnp.zeros((CH, D - 640), BF16)

    def col(width, blk):
        return pl.BlockSpec((CH, width), lambda i: (NC - 1 - i, blk))

    def full(a):
        return pl.BlockSpec(a.shape, lambda i: (0,) * len(a.shape))

    acc_shapes = [SDS((4, 1536), F32), SDS((1, 1536), F32), SDS((8, LANE), F32), SDS((1, D), F32)]
    return pl.pallas_call(
        body, name="ssm_bwd", grid=(NC,),
        in_specs=[col(D, 5), col(512, 12), col(LANE, 52), col(D, 4),
                  col(D, 0), col(D, 0), pl.BlockSpec((None, CH, D), lambda i: (NC - 1 - i, 0, 0)), col(D + 512, 0),
                  full(cw), full(dtb), full(alog), full(d_b), full(nw)],
        out_specs=[col(D, 0), col(2 * D, 0)] + [full(a) for a in acc_shapes],
        out_shape=[SDS((S, D), BF16), SDS((S, 2 * D), BF16)] + acc_shapes,
        scratch_shapes=[pltpu.VMEM((CH, D), F32), pltpu.VMEM((8, D), F32), pltpu.VMEM((8, 512), F32)],
        compiler_params=_cp(("arbitrary",)),
    )(proj, proj, proj, proj, dn, y_save, states, conv_out, cw, dtb, alog, d_b, nw)


def _outproj_loss(mix, w_out, x, tgt, nw, attn_pre, proj):
    tm = 256

    def body(mix_ref, w_ref, x_ref, t_ref, nw_ref, pre_ref, g_ref,
             dy_ref, dn_ref, do_ref, delta_ref, dg_ref, dw_ref, dnw_ref, loss_ref):
        @pl.when(pl.program_id(0) == 0)
        def _():
            dw_ref[...] = jnp.zeros_like(dw_ref)
            dnw_ref[...] = jnp.zeros_like(dnw_ref)
            loss_ref[...] = jnp.zeros_like(loss_ref)

        mixv, w = mix_ref[...], w_ref[...]
        out = _dot(mixv, w)
        r = lax.rsqrt(jnp.mean(out * out, axis=-1, keepdims=True) + EPS)
        nh = out * r
        nw_v = nw_ref[...]
        err = x_ref[...] + nh * nw_v - t_ref[...]
        loss_ref[...] += 0.5 * jnp.sum(jnp.mean(err * err, axis=-1, keepdims=True), axis=0, keepdims=True)
        dy = err * (1.0 / D)
        dy_ref[...] = dy
        dnw_ref[...] += jnp.sum(dy * nh, axis=0, keepdims=True)
        gdn = dy * nw_v
        dout = (r * (gdn - nh * jnp.mean(gdn * nh, axis=-1, keepdims=True))).astype(BF16)
        dmix = _dot_nt(dout, w)
        dw_ref[...] += _dot_tn(mixv, dout)
        dn_ref[...] = dmix[:, D:]
        dm, g, pre_v = dmix[:, :D], g_ref[...], pre_ref[...]
        sig = _sigmoid(g)
        do = dm * (g * sig)
        do_ref[...] = do
        dg_ref[...] = (dm * pre_v * (sig * (1.0 + g * (1.0 - sig)))).astype(BF16)
        prod = do * pre_v
        same_head = (_iota((LANE, LANE), 0) // 64 == _iota((LANE, LANE), 1) // 64).astype(BF16)
        for cb in range(D // LANE):
            delta_ref[:, cb * LANE:(cb + 1) * LANE] = _pick(prod[:, cb * LANE:(cb + 1) * LANE], same_head)

    row = lambda w: pl.BlockSpec((tm, w), lambda i: (i, 0))
    full = lambda s: pl.BlockSpec(s, lambda i: (0, 0))
    return pl.pallas_call(
        body, name="outproj_loss", grid=(S // tm,),
        in_specs=[row(2 * D), full((2 * D, D)), row(D), row(D), full((1, D)), row(D),
                  pl.BlockSpec((tm, D), lambda i: (i, OFF_G // D))],
        out_specs=[row(D), row(D), row(D), row(D), row(D), full((2 * D, D)), full((1, D)), full((1, LANE))],
        out_shape=[SDS((S, D), F32)] * 4 + [SDS((S, D), BF16), SDS((2 * D, D), F32), SDS((1, D), F32),
                                            SDS((1, LANE), F32)],
        compiler_params=_cp(("arbitrary",)),
    )(mix, w_out, x, tgt, nw, attn_pre, proj)


def _inproj_bwd_dx(srcs, dxbcdt, w_all, x, dy, nw, hosted=None):
    tm = 512
    nk = DP // D
    n_host = len(hosted.arrays) if hosted else 0

    def body(*refs):
        src_refs = refs[:nk]
        w_ref, x_ref, dy_ref, nw_ref = refs[nk:nk + 4]
        host_in, refs = refs[nk + 4:nk + 4 + n_host], refs[nk + 4 + n_host:]
        gx_ref, dnw_ref = refs[:2]
        host_out, host_sems = refs[2:2 + n_host], refs[2 + n_host:]
        i = pl.program_id(0)

        @pl.when(i == 0)
        def _():
            if hosted:
                hosted.start(host_in, host_out, host_sems)
            dnw_ref[...] = jnp.zeros_like(dnw_ref)

        du = None
        for k, ref in enumerate(src_refs):
            part = _dot_nt(ref[...], w_ref[:, k * D:(k + 1) * D])
            du = part if du is None else du + part
        xf, nw_v = x_ref[...], nw_ref[...]
        r = lax.rsqrt(jnp.mean(xf * xf, axis=-1, keepdims=True) + EPS)
        xh = xf * r
        dnw_ref[...] += jnp.sum(du * xh, axis=0, keepdims=True)
        gdu = du * nw_v
        gx_ref[...] = r * (gdu - xh * jnp.mean(gdu * xh, axis=-1, keepdims=True)) + dy_ref[...]

        if hosted:
            pl.when(i == S // tm - 1)(lambda: hosted.finish(host_in, host_out, host_sems))

    row = pl.BlockSpec((tm, D), lambda i: (i, 0))
    row1 = pl.BlockSpec((tm, D), lambda i: (i, 1))
    one = pl.BlockSpec((1, D), lambda i: (0, 0))
    whole_w = pl.BlockSpec((D, DP), lambda i: (0, 0), pipeline_mode=pl.Buffered(1))
    args = [*srcs, dxbcdt, dxbcdt, w_all, x, dy, nw]
    in_specs = [row] * len(srcs) + [row, row1, whole_w, row, row, one]
    out_specs, out_shape, scratch = [row, one], [SDS((S, D), F32), SDS((1, D), F32)], []
    if hosted:
        args += hosted.arrays
        in_specs += [ANY] * n_host
        out_specs += [ANY] * n_host
        out_shape += hosted.out_shape
        scratch += hosted.scratch
    outs = pl.pallas_call(
        body, name="inproj_bwd_dx", grid=(S // tm,),
        in_specs=in_specs, out_specs=out_specs, out_shape=out_shape, scratch_shapes=scratch,
        compiler_params=_cp(("arbitrary",)),
    )(*args)
    return (outs[:2], outs[2:]) if hosted else outs


def _dw(u, dsec, name):
    ts = 1024
    ncol = dsec.shape[1] // D

    def body(u_ref, d_ref, o_ref):
        @pl.when(pl.program_id(1) == 0)
        def _():
            o_ref[...] = jnp.zeros_like(o_ref)

        o_ref[...] += _dot_tn(u_ref[...], d_ref[...])

    return pl.pallas_call(
        body, name=name, grid=(ncol, S // ts),
        in_specs=[pl.BlockSpec((ts, D), lambda j, i: (i, 0)), pl.BlockSpec((ts, D), lambda j, i: (i, j))],
        out_specs=pl.BlockSpec((D, D), lambda j, i: (0, j)),
        out_shape=SDS((D, ncol * D), F32),
        compiler_params=_cp(("parallel", "arbitrary")),
    )(u, dsec)


def _place():
    x, y, c = lax.axis_index("x"), lax.axis_index("y"), lax.axis_index("c")
    return x, y, c, 2 * x + y


def _chip_of(x, y, k):
    px = 1 - x if k & 2 else x
    py = 1 - y if k & 1 else y
    return px, py, 2 * px + py


def _remote(src, dst, send_sem, recv_sem, dev):
    return pltpu.make_async_remote_copy(src_ref=src, dst_ref=dst, send_sem=send_sem, recv_sem=recv_sem,
                                        device_id=dev, device_id_type=MESH)


def _gather_weights(w_in_b):
    half = w_in_b.shape[0] // 2
    quarter = half // 2

    def body(src, dst, send, recv):
        x, y, c, j = _place()
        me, sib = (x, y, c), (x, y, 1 - c)
        nbr = {"x": _chip_of(x, y, 2), "y": _chip_of(x, y, 1)}
        diag = _chip_of(x, y, 3)[2]
        started, arrivals = [], []

        def rows(n_quarter=None, sibling=False):
            base = (1 - c if sibling else c) * half
            return pl.ds(base, half) if n_quarter is None else pl.ds(base + n_quarter * quarter, quarter)

        def sem(n):
            return send.at[n], recv.at[n]

        def go(cp):
            cp.start()
            started.append(cp)

        own = _remote(src, dst.at[j], *sem(8), sib)
        go(own)
        for n, axis in enumerate("xy"):
            px, py, _ = nbr[axis]
            go(_remote(src.at[rows()], dst.at[j, rows()], *sem(n), (px, py, c)))
        for n, axis in enumerate("xy"):
            ox, oy, _ = nbr["y" if axis == "x" else "x"]
            pj = nbr[axis][2]
            _remote(src.at[rows()], dst.at[pj, rows()], *sem(n), me).wait_recv()
            go(_remote(dst.at[pj, rows(n)], dst.at[pj, rows(n)], *sem(2 + n), (ox, oy, c)))
            go(_remote(dst.at[pj, rows()], dst.at[pj, rows()], *sem(4 + n), sib))
            arrivals.append(_remote(src.at[rows()], dst.at[pj, rows(None, True)], *sem(4 + n), me))
        for n in range(2):
            _remote(dst.at[diag, rows(n)], dst.at[diag, rows(n)], *sem(2 + n), me).wait_recv()
            go(_remote(dst.at[diag, rows(n)], dst.at[diag, rows(n)], *sem(6 + n), sib))
            arrivals.append(_remote(dst.at[diag, rows(n, True)], dst.at[diag, rows(n, True)], *sem(6 + n), me))
        for cp in arrivals + [own]:
            cp.wait_recv()
        for cp in started:
            cp.wait_send()

    return pl.pallas_call(
        body, name="gather_weights", in_specs=[ANY], out_specs=ANY,
        out_shape=SDS((4,) + w_in_b.shape, BF16),
        scratch_shapes=[pltpu.SemaphoreType.DMA((9,)), pltpu.SemaphoreType.DMA((9,))],
        compiler_params=pltpu.CompilerParams(has_side_effects=True),
    )(w_in_b)


class _LateGather:
    def __init__(self, w_out_b, conv_w):
        self.arrays = [w_out_b, conv_w]
        self.out_shape = [SDS((4,) + w_out_b.shape, BF16), SDS((4,) + conv_w.shape, F32)]
        self.scratch = [pltpu.SemaphoreType.DMA((11,)), pltpu.SemaphoreType.DMA((11,))]

    def _plan(self, ins, outs, sems):
        x, y, c, j = _place()
        send, recv = sems
        (wo, cw), (gwo, gcw) = ins, outs
        half = wo.shape[0] // 2
        mine, theirs = pl.ds(c * half, half), pl.ds((1 - c) * half, half)
        me, sib = (x, y, c), (x, y, 1 - c)
        first, arrive, forward, last = [], [], [], []
        for k in (1, 2, 3):
            px, py, pj = _chip_of(x, y, k)
            first += [_remote(wo.at[mine], gwo.at[j, mine], send.at[k - 1], recv.at[k - 1], (px, py, c)),
                      _remote(cw, gcw.at[j], send.at[k + 2], recv.at[k + 2], (px, py, c))]
            arrive.append(_remote(wo.at[mine], gwo.at[pj, mine], send.at[k - 1], recv.at[k - 1], me))
            forward.append(_remote(gwo.at[pj, mine], gwo.at[pj, mine], send.at[k + 5], recv.at[k + 5], sib))
            last += [_remote(cw, gcw.at[pj], send.at[k + 2], recv.at[k + 2], me),
                     _remote(wo.at[theirs], gwo.at[pj, theirs], send.at[k + 5], recv.at[k + 5], me)]
        first += [_remote(wo, gwo.at[j], send.at[9], recv.at[9], sib),
                  _remote(cw, gcw.at[j], send.at[10], recv.at[10], sib)]
        last += first[-2:]
        return first, arrive, forward, last

    def start(self, ins, outs, sems):
        for cp in self._plan(ins, outs, sems)[0]:
            cp.start()

    def finish(self, ins, outs, sems):
        first, arrive, forward, last = self._plan(ins, outs, sems)
        for got, fwd in zip(arrive, forward):
            got.wait_recv()
            fwd.start()
        for cp in last:
            cp.wait_recv()
        for cp in first + forward:
            cp.wait_send()


class _PairExchange:
    def __init__(self, arrays):
        self.arrays = list(arrays)
        self.out_shape = [SDS((a.shape[0], a.shape[1] // 2, a.shape[2]), F32) for a in self.arrays]
        self.scratch = [pltpu.SemaphoreType.DMA((len(self.arrays),)) for _ in range(2)]

    def _copies(self, ins, outs, sems):
        x, y, c, _ = _place()
        for k, (src, dst) in enumerate(zip(ins, outs)):
            half = src.shape[1] // 2
            yield _remote(src.at[:, pl.ds((1 - c) * half, half)], dst, sems[0].at[k], sems[1].at[k], (x, y, 1 - c))

    def start(self, ins, outs, sems):
        for cp in self._copies(ins, outs, sems):
            cp.start()

    def finish(self, ins, outs, sems):
        for cp in self._copies(ins, outs, sems):
            cp.wait()


def _pair_exchange(arrays, name):
    halves = [a.shape[1] // 2 for a in arrays]
    n = len(arrays)

    def body(*refs):
        x, y, c, _ = _place()
        send, recv = refs[2 * n:]
        cps = [_remote(refs[k].at[:, pl.ds((1 - c) * halves[k], halves[k])], refs[n + k], send.at[k], recv.at[k],
                       (x, y, 1 - c)) for k in range(n)]
        for cp in cps:
            cp.start()
        for cp in cps:
            cp.wait()

    return pl.pallas_call(
        body, name=name, in_specs=[ANY] * n, out_specs=[ANY] * n,
        out_shape=[SDS((a.shape[0], h, a.shape[2]), F32) for a, h in zip(arrays, halves)],
        scratch_shapes=[pltpu.SemaphoreType.DMA((n,)), pltpu.SemaphoreType.DMA((n,))],
        compiler_params=pltpu.CompilerParams(has_side_effects=True),
    )(*arrays)


def _pair_sum(cidx, g, r, name):
    n, half, width = r.shape
    tr = min(half, 256)
    nt = half // tr

    def body(c_ref, g_ref, r_ref, o_ref):
        del c_ref
        o_ref[...] = (g_ref[...] + r_ref[...]).astype(BF16)

    return pl.pallas_call(
        body, name=name,
        grid_spec=pltpu.PrefetchScalarGridSpec(
            num_scalar_prefetch=1, grid=(n, nt),
            in_specs=[pl.BlockSpec((None, tr, width), lambda s, t, c: (s, c[0] * nt + t, 0)),
                      pl.BlockSpec((None, tr, width), lambda s, t, c: (s, t, 0))],
            out_specs=pl.BlockSpec((None, tr, width), lambda s, t, c: (s, t, 0))),
        out_shape=SDS(r.shape, BF16),
        compiler_params=_cp(("parallel", "parallel")),
    )(cidx, g, r)


class _ChipExchange:
    def __init__(self, arrays, rows):
        self.arrays, self.rows = list(arrays), list(rows)
        self.out_shape = [SDS((4,) + a.shape[1:], BF16) for a in self.arrays]
        self.scratch = [pltpu.SemaphoreType.DMA((3 * len(self.arrays),)) for _ in range(2)]

    def _copies(self, ins, outs, sems):
        x, y, c, j = _place()
        send, recv = sems
        for a, (src, dst, row) in enumerate(zip(ins, outs, self.rows)):
            for k in (1, 2, 3):
                px, py, pj = _chip_of(x, y, k)
                n = 3 * a + k - 1
                slot = pj if row is None else py
                yield (None if row is None else px == row, None if row is None else x == row,
                       _remote(src.at[slot], dst.at[j], send.at[n], recv.at[n], (px, py, c)),
                       _remote(src.at[0], dst.at[pj], send.at[n], recv.at[n], (x, y, c)))

    def start(self, ins, outs, sems):
        for sends, _, send, _ in self._copies(ins, outs, sems):
            if sends is None:
                send.start()
            else:
                pl.when(sends)(send.start)

    def finish(self, ins, outs, sems):
        for sends, owns, send, arrival in self._copies(ins, outs, sems):
            if sends is None:
                arrival.wait_recv()
                send.wait_send()
            else:
                pl.when(owns)(arrival.wait_recv)
                pl.when(sends)(send.wait_send)


def _small_exchange(small):
    def body(sm_ref, rs_ref, send, recv, lsem):
        x, y, c, j = _place()
        me = 2 * j + c
        local = pltpu.make_async_copy(sm_ref, rs_ref.at[me], lsem)
        local.start()
        cps = []
        for k in range(1, 8):
            px, py, _ = _chip_of(x, y, k >> 1)
            pc = 1 - c if k & 1 else c
            cps.append(_remote(sm_ref, rs_ref.at[me], send.at[k - 1], recv.at[k - 1], (px, py, pc)))
        for cp in cps:
            cp.start()
        for k in range(1, 8):
            _, _, pj = _chip_of(x, y, k >> 1)
            pc = 1 - c if k & 1 else c
            _remote(sm_ref, rs_ref.at[2 * pj + pc], send.at[k - 1], recv.at[k - 1], (x, y, c)).wait_recv()
        for cp in cps:
            cp.wait_send()
        local.wait()

    return pl.pallas_call(
        body, name="small_exchange", in_specs=[ANY], out_specs=ANY,
        out_shape=SDS((8,) + small.shape, F32),
        scratch_shapes=[pltpu.SemaphoreType.DMA((7,)), pltpu.SemaphoreType.DMA((7,)), pltpu.SemaphoreType.DMA],
        compiler_params=pltpu.CompilerParams(has_side_effects=True),
    )(small)


def _slot_sum(r, name):
    n, rows, width = r.shape
    tr = min(rows, 256)

    def body(r_ref, o_ref):
        acc = r_ref[0].astype(F32)
        for s in range(1, n):
            acc = acc + r_ref[s].astype(F32)
        o_ref[...] = acc

    return pl.pallas_call(
        body, name=name, grid=(rows // tr,),
        in_specs=[pl.BlockSpec((n, tr, width), lambda t: (0, t, 0))],
        out_specs=pl.BlockSpec((tr, width), lambda t: (t, 0)),
        out_shape=SDS((rows, width), F32),
        compiler_params=_cp(("parallel",)),
    )(r)


def _chip_sum(chip_idx, recv, own, name):
    n, rows, width = recv.shape
    tr = min(rows, 256)

    def body(j_ref, r_ref, own_ref, o_ref):
        acc = None
        for s in range(n):
            term = jnp.where(j_ref[0] == s, own_ref[...], r_ref[s]).astype(F32)
            acc = term if acc is None else acc + term
        o_ref[...] = acc

    return pl.pallas_call(
        body, name=name,
        grid_spec=pltpu.PrefetchScalarGridSpec(
            num_scalar_prefetch=1, grid=(rows // tr,),
            in_specs=[pl.BlockSpec((n, tr, width), lambda t, j: (0, t, 0)),
                      pl.BlockSpec((None, tr, width), lambda t, j: (j[0], t, 0))],
            out_specs=pl.BlockSpec((tr, width), lambda t, j: (t, 0))),
        out_shape=SDS((rows, width), F32),
        compiler_params=_cp(("parallel",)),
    )(chip_idx, recv, own)


def _chip_sum_rows(place, recv0, own0, recv1, own1, name):
    n, rows, width = recv0.shape
    tr = min(rows, 256)

    def body(p_ref, r0_ref, o0_ref, r1_ref, o1_ref, o_ref):
        first_row = p_ref[1] == 0
        own = jnp.where(first_row, o0_ref[...], o1_ref[...])
        acc = None
        for s in range(n):
            term = jnp.where(p_ref[0] == s, own, jnp.where(first_row, r0_ref[s], r1_ref[s])).astype(F32)
            acc = term if acc is None else acc + term
        o_ref[...] = acc

    recv = pl.BlockSpec((n, tr, width), lambda t, p: (0, t, 0))
    own = pl.BlockSpec((None, tr, width), lambda t, p: (p[2], t, 0))
    return pl.pallas_call(
        body, name=name,
        grid_spec=pltpu.PrefetchScalarGridSpec(
            num_scalar_prefetch=1, grid=(rows // tr,), in_specs=[recv, own, recv, own],
            out_specs=pl.BlockSpec((tr, width), lambda t, p: (t, 0))),
        out_shape=SDS((rows, width), F32),
        compiler_params=_cp(("parallel",)),
    )(place, recv0, own0, recv1, own1)


def _half_exchange(hw, ho):
    def body(hw_ref, ho_ref, tw_ref, to_ref, send, recv):
        x, y, c, _ = _place()
        sib = (x, y, 1 - c)
        cps = [_remote(hw_ref, tw_ref, send.at[0], recv.at[0], sib),
               _remote(ho_ref, to_ref, send.at[1], recv.at[1], sib)]
        for cp in cps:
            cp.start()
        for cp in cps:
            cp.wait()

    return pl.pallas_call(
        body, name="half_exchange", in_specs=[ANY, ANY], out_specs=[ANY, ANY],
        out_shape=[SDS(hw.shape, F32), SDS(ho.shape, F32)],
        scratch_shapes=[pltpu.SemaphoreType.DMA((2,)), pltpu.SemaphoreType.DMA((2,))],
        compiler_params=pltpu.CompilerParams(has_side_effects=True),
    )(hw, ho)


def _by_core(c, mine, theirs):
    return jnp.where(c == 0, jnp.concatenate([mine, theirs], axis=0), jnp.concatenate([theirs, mine], axis=0))


def _adamw(w, g, m, v, name):
    rows, width = w.shape
    tr = min(rows, 256)

    def body(w_ref, g_ref, m_ref, v_ref, d_ref, nm_ref, nv_ref):
        gv = g_ref[...]
        nm = ADAM_B1 * m_ref[...] + (1.0 - ADAM_B1) * gv
        nv = ADAM_B2 * v_ref[...] + (1.0 - ADAM_B2) * (gv * gv)
        m_hat = nm / (1.0 - ADAM_B1 ** ADAM_STEP)
        v_hat = nv / (1.0 - ADAM_B2 ** ADAM_STEP)
        d_ref[...] = -ADAM_LR * (m_hat / (jnp.sqrt(v_hat) + ADAM_EPS) + ADAM_WD * w_ref[...])
        nm_ref[...] = nm
        nv_ref[...] = nv

    t = pl.BlockSpec((tr, width), lambda i: (i, 0))
    return pl.pallas_call(
        body, name=name, grid=(rows // tr,), in_specs=[t] * 4, out_specs=[t] * 3,
        out_shape=[SDS(w.shape, F32)] * 3, compiler_params=_cp(("parallel",)),
    )(w, g, m, v)


def _rowwise(a):
    return jnp.transpose(a, (2, 0, 1)).reshape(SHARD * D // LANE, LANE)


def _adamw_in(w, g, m_rows, v_rows):
    def body(w_ref, g_ref, m_ref, v_ref, d_ref, nm_ref, nv_ref):
        def columns(ref):
            return jnp.concatenate([ref[pl.ds(c, LANE, stride=8), :].T for c in range(D // LANE)], axis=0)

        gv = g_ref[...]
        nm = ADAM_B1 * columns(m_ref) + (1.0 - ADAM_B1) * gv
        nv = ADAM_B2 * columns(v_ref) + (1.0 - ADAM_B2) * (gv * gv)
        m_hat = nm / (1.0 - ADAM_B1 ** ADAM_STEP)
        v_hat = nv / (1.0 - ADAM_B2 ** ADAM_STEP)
        d_ref[...] = -ADAM_LR * (m_hat / (jnp.sqrt(v_hat) + ADAM_EPS) + ADAM_WD * w_ref[...])
        nm_ref[...] = nm
        nv_ref[...] = nv

    tile = pl.BlockSpec((D, LANE), lambda t: (0, t))
    rows = pl.BlockSpec((D, LANE), lambda t: (t, 0))
    return pl.pallas_call(
        body, name="adamw_in", grid=(pl.cdiv(SHARD, LANE),), in_specs=[tile, tile, rows, rows],
        out_specs=[tile] * 3, out_shape=[SDS(w.shape, F32)] * 3, compiler_params=_cp(("parallel",)),
    )(w, g, m_rows, v_rows)


def _rows128(a, rows):
    flat = a.reshape(-1)
    return jnp.pad(flat, (0, rows * LANE - flat.shape[0])).reshape(rows, LANE)


def _pack_small(conv_w, norm_pre, conv_b, ssm_norm, norm_post, dtb, alog, dsk, extra=None):
    cw_rows = 48 if conv_w.shape[-1] == 1536 else 16
    extra = jnp.zeros((1, LANE), F32) if extra is None else _rows128(extra, 1)
    vec = jnp.concatenate([_rows128(dtb, 1), _rows128(alog, 1), _rows128(dsk, 1), extra, jnp.zeros((4, LANE), F32)],
                          axis=0)
    return jnp.concatenate([_rows128(conv_w, cw_rows), _rows128(norm_pre, 8), _rows128(conv_b, 16),
                            _rows128(ssm_norm, 8), _rows128(norm_post, 8), vec], axis=0)


def _unpack_small(p, cw_cols):
    cw_rows = 48 if cw_cols == 1536 else 16
    o = cw_rows
    conv_w = p[:cw_rows].reshape(-1)[:4 * cw_cols].reshape(1, 4, cw_cols)
    norm_pre = p[o:o + 8].reshape(1, D)
    conv_b = p[o + 8:o + 24].reshape(-1)[:1536].reshape(1, 1536)
    ssm_norm = p[o + 24:o + 32].reshape(1, D)
    norm_post = p[o + 32:o + 40].reshape(1, D)
    vec = p[o + 40:o + 48]
    return conv_w, norm_pre, conv_b, ssm_norm, norm_post, vec[0:1, :NH], vec[1:2, :NH], vec[2:3, :NH], vec[3, 0]


def _pad_lanes(a):
    return jnp.pad(a, ((0, 0), (0, LANE - a.shape[1])))


class _GradReduce:
    SPLIT = 2 * SHARD - OFF_G

    def __init__(self, xi, yi, ci):
        self.ci = ci
        self.cidx = jnp.reshape(ci, (1,)).astype(jnp.int32)
        self.place = jnp.stack([2 * xi + yi, xi, yi]).astype(jnp.int32)

    def pairs(self, dw_g, dw_z, dw_x, dw_out):
        cols = jnp.concatenate([dw_g[:, self.SPLIT:], dw_z, dw_x], axis=1)
        self.gw_hi = jnp.stack([cols[:, :SHARD], cols[:, SHARD:2 * SHARD]])
        self.go = dw_out.reshape(4, D // 2, D)
        return _PairExchange([self.gw_hi, self.go])

    def first(self, got):
        rw, ro = got
        self.pw_hi = _pair_sum(self.cidx, self.gw_hi, rw, "pair_sum_hi")
        self.po = _pair_sum(self.cidx, self.go, ro, "pair_sum_out")
        return _ChipExchange([self.pw_hi, self.po], [1, None])

    def first_done(self, got):
        self.rw_hi, self.ro = got

    def second(self, dw_q, dw_k, dw_v, dw_g):
        cols = jnp.concatenate([dw_q, dw_k, dw_v, dw_g[:, :self.SPLIT]], axis=1)
        gw = jnp.stack([cols[:, :SHARD], cols[:, SHARD:]])
        (rw,) = _pair_exchange([gw], "pair_exchange_lo")
        self.pw_lo = _pair_sum(self.cidx, gw, rw, "pair_sum_lo")
        return _ChipExchange([self.pw_lo], [0])

    def second_done(self, got):
        (self.rw_lo,) = got

    def result(self):
        half_in = _chip_sum_rows(self.place, self.rw_lo, self.pw_lo, self.rw_hi, self.pw_hi, "chip_sum_in")
        half_out = _chip_sum(self.place[0:1], self.ro, self.po, "chip_sum_out")
        their_in, their_out = _half_exchange(half_in, half_out)
        return _by_core(self.ci, half_in, their_in), _by_core(self.ci, half_out, their_out)


def kernel(x, norm_pre_w, w_in, conv_w, conv_b, dt_bias, a_log, d_skip, ssm_norm_w, w_out, norm_post_w, loss_target, m_norm_pre_w, m_w_in, m_conv_w, m_conv_b, m_dt_bias, m_a_log, m_d_skip, m_ssm_norm_w, m_w_out, m_norm_post_w, v_norm_pre_w, v_w_in, v_conv_w, v_conv_b, v_dt_bias, v_a_log, v_d_skip, v_ssm_norm_w, v_w_out, v_norm_post_w):
    xi, yi, ci = lax.axis_index("x"), lax.axis_index("y"), lax.axis_index("c")
    chip = 2 * xi + yi
    x2, tgt = x[0], loss_target[0]

    gin = _gather_weights(w_in[0].astype(BF16))
    w_all = jnp.concatenate([gin[0], gin[1], gin[2], gin[3], jnp.zeros((D, DP - 4 * SHARD), BF16)], axis=1)
    reduce = _GradReduce(xi, yi, ci)
    grad_x, small = _local_step(x2, tgt, w_all, _LateGather(w_out[0].astype(BF16), conv_w[0]), norm_pre_w, conv_b,
                                dt_bias, a_log, d_skip, ssm_norm_w, norm_post_w, reduce)[:2]
    g_in, g_out = reduce.result()
    g_small = _slot_sum(_small_exchange(small), "small_sum")
    g_cw, g_npre, g_cb, g_nssm, g_npost, g_dtb, g_alog, g_dsk, loss = _unpack_small(g_small, 1536)
    g_cw = lax.dynamic_slice_in_dim(g_cw, chip * 384, 384, axis=2)

    d_in, nm_in, nv_in = _adamw_in(w_in[0], g_in, _rowwise(m_w_in), _rowwise(v_w_in))
    d_out, nm_out, nv_out = _adamw(w_out[0], g_out, m_w_out[0], v_w_out[0], "adamw_out")
    packed = [_pack_small(*t) for t in (
        (conv_w, norm_pre_w, conv_b, ssm_norm_w, norm_post_w, dt_bias, a_log, d_skip),
        (g_cw, g_npre, g_cb, g_nssm, g_npost, g_dtb, g_alog, g_dsk),
        (m_conv_w, m_norm_pre_w, m_conv_b, m_ssm_norm_w, m_norm_post_w, m_dt_bias, m_a_log, m_d_skip),
        (v_conv_w, v_norm_pre_w, v_conv_b, v_ssm_norm_w, v_norm_post_w, v_dt_bias, v_a_log, v_d_skip))]
    small_out = [_unpack_small(p, 384)[:8] for p in _adamw(*packed, "adamw_small")]

    def ordered(cw_, npre, cb_, nssm, npost, dtb_, alog_, dsk_, big_in, big_out):
        return [npre, big_in[None], cw_, cb_, dtb_, alog_, dsk_, nssm, big_out[None], npost]

    grads = ordered(g_cw, g_npre, g_cb, g_nssm, g_npost, g_dtb, g_alog, g_dsk, g_in, g_out)
    deltas = ordered(*small_out[0], d_in, d_out)
    new_m = ordered(*small_out[1], nm_in, nm_out)
    new_v = ordered(*small_out[2], nv_in, nv_out)
    return (loss, grad_x[None], *grads, *deltas, *new_m, *new_v)


def _local_step(x2, tgt, w_all, late, norm_pre_w, conv_b, dt_bias, a_log, d_skip, ssm_norm_w,
                norm_post_w, reduce=None):
    dtb, alog = _pad_lanes(dt_bias), _pad_lanes(a_log)
    d_b = jnp.repeat(d_skip, 64, axis=1)

    if isinstance(late, _LateGather):
        (proj, u), (gout, gcw) = _inproj_fwd(x2, norm_pre_w, w_all, late)
        w_out_all = gout.reshape(2 * D, D)
        cw_all = jnp.concatenate([gcw[0], gcw[1], gcw[2], gcw[3]], axis=1)
    else:
        proj, u = _inproj_fwd(x2, norm_pre_w, w_all)
        w_out_all, cw_all = late
    mix, attn_pre, lse = _attn_fwd(proj, 1, _attn_fwd(proj, 4, _attn_fwd(proj, 16)), final=True)
    mix, y_save, states, conv_out = _ssm_fwd(proj, mix, cw_all, conv_b, dtb, alog, d_b, ssm_norm_w)

    dy, dn_ssm, do, delta, dg, dw_out, dnw_post, loss_part = _outproj_loss(mix, w_out_all, x2, tgt, norm_post_w,
                                                                          attn_pre, proj)
    dz, dxbcdt, dcw, dcb, dvec, dnw_ssm = _ssm_bwd(proj, dn_ssm, y_save, states, conv_out, cw_all, dtb, alog, d_b,
                                                   ssm_norm_w)
    dw_g, dw_z, dw_x = _dw(u, dg, "dw_in_g"), _dw(u, dz, "dw_in_z"), _dw(u, dxbcdt, "dw_in_xbcdt")
    acc = _attn_bwd(proj, do, lse, delta, 16, None, F32, reduce.pairs(dw_g, dw_z, dw_x, dw_out) if reduce else None)
    if reduce:
        acc, got = acc
    acc = _attn_bwd(proj, do, lse, delta, 4, acc, F32, reduce.first(got) if reduce else None)
    if reduce:
        acc, got = acc
        reduce.first_done(got)
    dq, dk, dv = _attn_bwd(proj, do, lse, delta, 1, acc, BF16)
    dw_q, dw_k, dw_v = _dw(u, dq, "dw_in_q"), _dw(u, dk, "dw_in_k"), _dw(u, dv, "dw_in_v")
    res = _inproj_bwd_dx([dq, dk, dv, dg, dz], dxbcdt, w_all, x2, dy, norm_pre_w,
                         reduce.second(dw_q, dw_k, dw_v, dw_g) if reduce else None)
    if reduce:
        res, got = res
        reduce.second_done(got)
    grad_x, dnw_pre = res
    dw_all = jnp.concatenate([dw_q, dw_k, dw_v, dw_g, dw_z, dw_x], axis=1)
    small = _pack_small(dcw, dnw_pre, dcb, dnw_ssm, dnw_post, dvec[0:1, :NH], dvec[1:2, :NH], dvec[2:3, :NH],
                        loss_part[:, :1])
    return grad_x, small, dw_all, dw_out
```

```python
import functools

import jax
import jax.numpy as jnp
from jax import lax
from jax.experimental import pallas as pl
from jax.experimental.pallas import tpu as pltpu

F32 = jnp.float32
BF16 = jnp.bfloat16
MESH = pl.DeviceIdType.MESH
SDS = jax.ShapeDtypeStruct
ANY = pl.BlockSpec(memory_space=pl.ANY)

S = 4096
D = 1024
DP = 7168
SHARD = 1668
OFF_G, OFF_Z = 3072, 4096
NH = 16
CH = 128
NC = S // CH
EPS = 1e-6
NEG = -1e30
LANE = 128
VMEM_LIMIT = 48 * 1024 * 1024

ADAM_LR, ADAM_B1, ADAM_B2, ADAM_EPS, ADAM_WD, ADAM_STEP = 0.001, 0.9, 0.999, 1e-08, 0.01, 10


def _cp(sem, **kw):
    return pltpu.CompilerParams(dimension_semantics=sem, vmem_limit_bytes=VMEM_LIMIT, **kw)


def _dot(a, b):
    return jnp.dot(a, b, preferred_element_type=F32)


def _dot_nt(a, b):
    return lax.dot_general(a, b, (((1,), (1,)), ((), ())), preferred_element_type=F32)


def _dot_tn(a, b):
    return lax.dot_general(a, b, (((0,), (0,)), ((), ())), preferred_element_type=F32)


def _pieces(x, n):
    out = []
    for _ in range(n):
        p = x.astype(BF16)
        out.append(p)
        x = x - p.astype(F32)
    return out


def _pick(x, sel, n=2):
    parts = [_dot(p, sel) for p in _pieces(x, n)]
    return functools.reduce(jnp.add, parts)


def _pick_left(sel, x, n=3):
    parts = [_dot(sel, p) for p in _pieces(x, n)]
    return functools.reduce(jnp.add, parts)


def _sigmoid(v):
    return 0.5 * jnp.tanh(0.5 * v) + 0.5


def _iota(shape, dim):
    return lax.broadcasted_iota(jnp.int32, shape, dim)


def _inproj_fwd(x, nw, w_all, hosted=None):
    tm, tn = 1024, 1024
    n_host = len(hosted.arrays) if hosted else 0

    def body(x_ref, nw_ref, w_ref, *refs):
        host_in, (proj_ref, u_ref), refs = refs[:n_host], refs[n_host:n_host + 2], refs[n_host + 2:]
        host_out, host_sems = refs[:n_host], refs[n_host:]
        i, j = pl.program_id(0), pl.program_id(1)
        if hosted:
            pl.when((i == 0) & (j == 0))(lambda: hosted.start(host_in, host_out, host_sems))

        @pl.when(j == 0)
        def _():
            xf = x_ref[...]
            r = lax.rsqrt(jnp.mean(xf * xf, axis=-1, keepdims=True) + EPS)
            u_ref[...] = (xf * r * nw_ref[...]).astype(BF16)

        proj_ref[...] = _dot(u_ref[...], w_ref[...])
        if hosted:
            pl.when((i == S // tm - 1) & (j == DP // tn - 1))(lambda: hosted.finish(host_in, host_out, host_sems))

    outs = pl.pallas_call(
        body, name="inproj_fwd", grid=(S // tm, DP // tn),
        in_specs=[pl.BlockSpec((tm, D), lambda i, j: (i, 0)), pl.BlockSpec((1, D), lambda i, j: (0, 0)),
                  pl.BlockSpec((D, tn), lambda i, j: (0, j))] + [ANY] * n_host,
        out_specs=[pl.BlockSpec((tm, tn), lambda i, j: (i, j)), pl.BlockSpec((tm, D), lambda i, j: (i, 0))]
        + [ANY] * n_host,
        out_shape=[SDS((S, DP), F32), SDS((S, D), BF16)] + (hosted.out_shape if hosted else []),
        scratch_shapes=hosted.scratch if hosted else [],
        compiler_params=_cp(("arbitrary", "arbitrary") if hosted else ("parallel", "arbitrary")),
    )(x, nw, w_all, *(hosted.arrays if hosted else []))
    return (outs[:2], outs[2:]) if hosted else outs


ATTN_QB = {1: 16, 4: 4, 16: 1}


def _unit_rows(r, u, d):
    return pl.ds(r + d * CH * u, CH, stride=d) if d > 1 else pl.ds(CH * u, CH)


def _for_units(d, qb, fn):
    for r in range(d):
        for u in range(qb):
            fn(r, u)


def _attn_mask(has_prev):
    qi, kj = _iota((2 * CH, 2 * CH), 0) & (CH - 1), _iota((2 * CH, 2 * CH), 1)
    cur_ok = (kj >= CH) & (kj - CH <= qi)
    prev_ok = (kj < CH) & (kj >= qi)
    return cur_ok | (prev_ok & has_prev)


def _stack_heads(v, lane_a):
    return jnp.concatenate([jnp.where(lane_a, v, 0.0), jnp.where(lane_a, 0.0, v)], axis=0).astype(BF16)


def _attn_specs(d, qb):
    rows, prows = CH * d * qb, CH * d
    nb = S // rows
    steps = (NH // 2) * nb

    def at(t):
        t = jnp.minimum(t, steps - 1)
        return t % nb, t // nb

    def cur(off):
        return pl.BlockSpec((rows, LANE), lambda t: (at(t)[0], off + at(t)[1]))

    def prev(off):
        return pl.BlockSpec((prows, LANE), lambda t: (jnp.maximum(at(t)[0] * qb - 1, 0), off + at(t)[1]))

    lag = pl.BlockSpec((rows, LANE), lambda t: at(jnp.maximum(t - 1, 0)))
    return nb, steps, cur, prev, lag


def _gather16(src_ref, dense_ref, tmp_ref):
    for a in range(4):
        tmp_ref[...] = src_ref[pl.ds(a, 4 * CH, stride=4), :]
        for b in range(4):
            dense_ref[a + 4 * b] = tmp_ref[pl.ds(b, CH, stride=4), :]


def _scatter16(dense_ref, dst_ref, tmp_ref):
    for a in range(4):
        for b in range(4):
            tmp_ref[pl.ds(b, CH, stride=4), :] = dense_ref[a + 4 * b]
        dst_ref[pl.ds(a, 4 * CH, stride=4), :] = tmp_ref[...]


def _unit_index(r, u, d):
    return (r,) if d == 16 else (_unit_rows(r, u, d), slice(None))


def _unit_kv(p_ref, c_ref, r, u, d):
    prev = p_ref[_unit_index(r, 0, d)] if u == 0 else c_ref[_unit_index(r, u - 1, d)]
    return jnp.concatenate([prev, c_ref[_unit_index(r, u, d)]], axis=0).astype(BF16)


def _dense_scratch(d, n):
    return [pltpu.VMEM((16, CH, LANE), F32)] * n + [pltpu.VMEM((4 * CH, LANE), F32)] if d == 16 else []


def _attn_fwd(proj, d, prior=None, final=False):
    qb = ATTN_QB[d]
    nb, steps, cur, prev, _ = _attn_specs(d, qb)
    n_prior = 2 if prior is not None else 0
    n_in, n_out = 5 + n_prior + final, 2 + final
    assert not (d == 16 and (n_prior or final))

    def body(*refs):
        ins, outs, scratch = refs[:n_in], refs[n_in:n_in + n_out], refs[n_in + n_out:]
        if d == 16:
            tmp_ref = scratch[-1]
            for src, dense in zip(ins, scratch):
                _gather16(src, dense, tmp_ref)
            block_outs, ins, outs = outs, scratch[:n_in], scratch[n_in:n_in + n_out]
        q_ref, kp_ref, kc_ref, vp_ref, vc_ref = ins[:5]
        prior_refs = ins[5:5 + n_prior]
        if final:
            g_ref, (mix_ref, o_ref, l_ref) = ins[-1], outs
        else:
            o_ref, l_ref = outs
        i = pl.program_id(0) % nb
        lane_a = _iota((CH, LANE), 1) < 64
        mask_first, mask_rest = _attn_mask(i > 0), _attn_mask(True)

        def unit(r, u):
            at = _unit_index(r, u, d)
            q2 = _stack_heads(q_ref[at] * 0.125, lane_a)
            k2, v2 = _unit_kv(kp_ref, kc_ref, r, u, d), _unit_kv(vp_ref, vc_ref, r, u, d)
            s = jnp.where(mask_first if u == 0 else mask_rest, _dot_nt(q2, k2), NEG)
            m = jnp.max(s, axis=1, keepdims=True)
            p = jnp.exp(s - m)
            l = jnp.sum(p, axis=1, keepdims=True)
            o2 = _dot(p.astype(BF16), v2) / l
            lse2 = m + jnp.log(l)
            o = jnp.where(lane_a, o2[:CH], o2[CH:])
            lse = jnp.where(lane_a, lse2[:CH], lse2[CH:])
            if n_prior:
                o_a, l_a = prior_refs[0][at], prior_refs[1][at]
                top = jnp.maximum(l_a, lse)
                e_a, e_b = jnp.exp(l_a - top), jnp.exp(lse - top)
                tot = e_a + e_b
                o = (e_a * o_a + e_b * o) / tot
                lse = top + jnp.log(tot)
            o_ref[at] = o
            l_ref[at] = lse
            if final:
                g = g_ref[at]
                mix_ref[at] = (o * (g * _sigmoid(g))).astype(BF16)

        _for_units(d, qb, unit)
        if d == 16:
            for dense, dst in zip(outs, block_outs):
                _scatter16(dense, dst, tmp_ref)

    in_specs = [cur(0), prev(8), cur(8), prev(16), cur(16)] + [cur(0)] * n_prior
    args = [proj] * 5 + (list(prior) if n_prior else [])
    out_specs, out_shape = [cur(0), cur(0)], [SDS((S, D), F32), SDS((S, D), F32)]
    if final:
        assert d == 1
        in_specs.append(cur(OFF_G // LANE))
        args.append(proj)
        out_specs, out_shape = [cur(0)] + out_specs, [SDS((S, 2 * D), BF16)] + out_shape
    return pl.pallas_call(
        body, name=f"attn_fwd_d{d}", grid=(steps,),
        in_specs=in_specs, out_specs=out_specs, out_shape=out_shape,
        scratch_shapes=_dense_scratch(d, n_in + n_out),
        compiler_params=_cp(("parallel",)),
    )(*args)


def _attn_bwd(proj, do, lse, delta, d, acc, out_dtype, hosted=None):
    qb = ATTN_QB[d]
    nb, steps, cur, prev, lag = _attn_specs(d, qb)
    has_acc = acc is not None
    n_in = 11 if has_acc else 8
    n_host = len(hosted.arrays) if hosted else 0
    assert not (d == 16 and (has_acc or out_dtype != F32))
    rows = CH * d * qb
    carry = (2, 16, CH, LANE) if d == 16 else (2, rows, LANE)

    def body(*refs):
        ins, host_in, refs = refs[:n_in], refs[n_in:n_in + n_host], refs[n_in + n_host:]
        (dq_ref, dk_ref, dv_ref), host_out, scratch = refs[:3], refs[3:3 + n_host], refs[3 + n_host:]
        if hosted:
            scratch, host_sems = scratch[:-len(hosted.scratch)], scratch[-len(hosted.scratch):]
        ck_ref, cv_ref = scratch[:2]
        dq_f32 = dq_ref if out_dtype == F32 else scratch[2]
        t = pl.program_id(0)
        i = t % nb
        if hosted:
            pl.when(t == 0)(lambda: hosted.start(host_in, host_out, host_sems))
        if d == 16:
            dense, dq_f32, tmp_ref = scratch[2:2 + n_in], scratch[2 + n_in], scratch[-1]

            @pl.when(t < steps)
            def _():
                for src, dst in zip(ins, dense):
                    _gather16(src, dst, tmp_ref)

            ins = dense
        q_ref, kp_ref, kc_ref, vp_ref, vc_ref, do_ref, lse_ref, dl_ref = ins[:8]
        if has_acc:
            aq_ref, ak_ref, av_ref = ins[8:11]
        slot = t & 1
        now_k, now_v, old_k, old_v = ck_ref.at[slot], cv_ref.at[slot], ck_ref.at[1 - slot], cv_ref.at[1 - slot]
        lane_a = _iota((CH, LANE), 1) < 64
        mask_first, mask_rest = _attn_mask(i > 0), _attn_mask(True)

        @pl.when(t == 0)
        def _():
            ck_ref[1] = jnp.zeros(carry[1:], F32)
            cv_ref[1] = jnp.zeros(carry[1:], F32)

        def unit(r, u):
            at = _unit_index(r, u, d)
            q2 = _stack_heads(q_ref[at] * 0.125, lane_a)
            do2 = _stack_heads(do_ref[at], lane_a)
            k2, v2 = _unit_kv(kp_ref, kc_ref, r, u, d), _unit_kv(vp_ref, vc_ref, r, u, d)
            lsev, dlv = lse_ref[at], dl_ref[at]
            lse2 = jnp.concatenate([lsev[:, 0:1], lsev[:, 64:65]], axis=0)
            dl2 = jnp.concatenate([dlv[:, 0:1], dlv[:, 64:65]], axis=0)
            p = jnp.exp(jnp.where(mask_first if u == 0 else mask_rest, _dot_nt(q2, k2), NEG) - lse2)
            ds = (p * (_dot_nt(do2, v2) - dl2)).astype(BF16)
            dq2 = _dot(ds, k2)
            dk2 = _dot_tn(ds, q2)
            dv2 = _dot_tn(p.astype(BF16), do2)
            dq = jnp.where(lane_a, dq2[:CH], dq2[CH:]) * 0.125
            if has_acc:
                dq = dq + aq_ref[at]
            dq_f32[at] = dq
            if u == 0:
                before = _unit_index(r, qb - 1, d)
                old_k[before] += dk2[:CH]
                old_v[before] += dv2[:CH]
            else:
                before = _unit_index(r, u - 1, d)
                now_k[before] += dk2[:CH]
                now_v[before] += dv2[:CH]
            now_k[at] = dk2[CH:]
            now_v[at] = dv2[CH:]

        @pl.when(t < steps)
        def _():
            _for_units(d, qb, unit)
            if d == 16:
                _scatter16(dq_f32, dq_ref, tmp_ref)
            elif out_dtype != F32:
                dq_ref[...] = dq_f32[...].astype(out_dtype)

        if d == 16:
            _scatter16(old_k, dk_ref, tmp_ref)
            _scatter16(old_v, dv_ref, tmp_ref)
        else:
            dk, dv = old_k[...], old_v[...]
            if has_acc:
                dk, dv = dk + ak_ref[...], dv + av_ref[...]
            dk_ref[...] = dk.astype(out_dtype)
            dv_ref[...] = dv.astype(out_dtype)
        if hosted:
            pl.when(t == steps)(lambda: hosted.finish(host_in, host_out, host_sems))

    in_specs = [cur(0), prev(8), cur(8), prev(16), cur(16), cur(0), cur(0), cur(0)]
    args = [proj, proj, proj, proj, proj, do, lse, delta]
    if has_acc:
        in_specs += [cur(0), lag, lag]
        args += list(acc)
    scratch = [pltpu.VMEM(carry, F32), pltpu.VMEM(carry, F32)]
    if d == 16:
        scratch += _dense_scratch(d, n_in + 1)
    elif out_dtype != F32:
        scratch.append(pltpu.VMEM((rows, LANE), F32))
    out_specs, out_shape = [cur(0), lag, lag], [SDS((S, D), out_dtype)] * 3
    if hosted:
        args += hosted.arrays
        in_specs += [ANY] * n_host
        out_specs += [ANY] * n_host
        out_shape += hosted.out_shape
        scratch += hosted.scratch
    outs = pl.pallas_call(
        body, name=f"attn_bwd_d{d}", grid=(steps + 1,),
        in_specs=in_specs, out_specs=out_specs, out_shape=out_shape,
        scratch_shapes=scratch, compiler_params=_cp(("arbitrary",)),
    )(*args)
    return (outs[:3], outs[3:]) if hosted else outs


def _conv_taps(cur, prev8, first):
    row8 = _iota(prev8.shape, 0)
    prev8 = jnp.where(first, 0.0, prev8)
    taps = []
    for s in (3, 2, 1):
        rolled = pltpu.roll(cur, s, 0)
        head = jnp.where(row8 < s, pltpu.roll(prev8, s, 0), rolled[:8])
        taps.append(jnp.concatenate([head, rolled[8:]], axis=0))
    return taps + [cur]


def _conv(taps, w, b):
    acc = b + w[0:1, :] * taps[0]
    for k in (1, 2, 3):
        acc = acc + w[k:k + 1, :] * taps[k]
    return acc


def _expand():
    return (_iota((LANE, D), 1) // 64 == _iota((LANE, D), 0)).astype(BF16)


def _reduce():
    return (_iota((D, LANE), 0) // 64 == _iota((D, LANE), 1)).astype(BF16)


def _ssd_common(xs_c, bc_c, dt_raw, dtb, alog):
    head_lane = _iota((CH, LANE), 1) < NH
    xs = xs_c * _sigmoid(xs_c)
    bc = bc_c * _sigmoid(bc_c)
    pre = dt_raw + dtb
    dt = jnp.where(head_lane, jnp.maximum(pre, 0.0) + jnp.log(1.0 + jnp.exp(-jnp.abs(pre))), 0.0)
    a_row = jnp.where(head_lane[0:1], -jnp.exp(alog), 0.0)
    tri = (_iota((CH, CH), 1) <= _iota((CH, CH), 0)).astype(BF16)
    cs = _pick_left(tri, dt * a_row)
    cs_last = cs[CH - 1:CH, :]
    wide = _pick(jnp.concatenate([dt, jnp.exp(cs), jnp.exp(cs_last - cs)], axis=0), _expand())
    dt_b, e_b, f_b = wide[:CH], wide[CH:2 * CH], wide[2 * CH:]
    return dict(xs=xs, bc=bc, pre=pre, dt=dt, a_row=a_row, cs=cs, cs_t=cs.T, dt_b=dt_b, e_b=e_b, f_b=f_b,
                t_b=e_b[CH - 1:CH, :])


def _groups(bc):
    bcb = bc.astype(BF16)
    return [bcb[:, 0:128], bcb[:, 128:256]], [bcb[:, 256:384], bcb[:, 384:512]]


def _decay(q, h, tril):
    seg = q["cs"][:, h:h + 1] - q["cs_t"][h:h + 1, :]
    return jnp.exp(jnp.where(tril, seg, NEG))


def _ssm_fwd(proj, mix, cw, cb, dtb, alog, d_b, nw):
    def body(xs_ref, xsp_ref, bc_ref, bcp_ref, dt_ref, z_ref, cw_ref, cb_ref, dtb_ref, alog_ref, db_ref, nw_ref,
             mix_in_ref, mix_ref, y_ref, st_ref, conv_ref, h_ref):
        del mix_in_ref
        i = pl.program_id(0)

        @pl.when(i == 0)
        def _():
            h_ref[...] = jnp.zeros_like(h_ref)

        cw, cb = cw_ref[...], cb_ref[...]
        xs_c = _conv(_conv_taps(xs_ref[...], xsp_ref[...], i == 0), cw[:, :D], cb[:, :D])
        bc_c = _conv(_conv_taps(bc_ref[...], bcp_ref[...], i == 0), cw[:, D:], cb[:, D:])
        conv_ref[:, :D] = xs_c
        conv_ref[:, D:] = bc_c
        q = _ssd_common(xs_c, bc_c, dt_ref[...], dtb_ref[...], alog_ref[...])
        bg, cg = _groups(q["bc"])
        xs = q["xs"]
        xdt = xs * q["dt_b"]
        xdt_b = xdt.astype(BF16)
        h_in = h_ref[...]
        st_ref[...] = h_in
        hb = h_in.astype(BF16)
        tril = _iota((CH, CH), 1) <= _iota((CH, CH), 0)
        lane_a = _iota((CH, LANE), 1) < 64
        cbm = [_dot_nt(cg[g], bg[g]) for g in range(2)]
        pairs = []
        for hp in range(NH // 2):
            xp = xdt_b[:, hp * LANE:(hp + 1) * LANE]
            ya = _dot((cbm[hp // 4] * _decay(q, 2 * hp, tril)).astype(BF16), xp)
            yb = _dot((cbm[hp // 4] * _decay(q, 2 * hp + 1, tril)).astype(BF16), xp)
            pairs.append(jnp.where(lane_a, ya, yb))
        y_diag = jnp.concatenate(pairs, axis=1)
        y_off = jnp.concatenate([_dot(cg[g], hb[:, g * 512:(g + 1) * 512]) for g in range(2)], axis=1) * q["e_b"]
        y = y_diag + y_off + db_ref[...] * xs
        y_ref[...] = y
        xf = (xdt * q["f_b"]).astype(BF16)
        h_ref[...] = q["t_b"] * h_in + jnp.concatenate(
            [_dot_tn(bg[g], xf[:, g * 512:(g + 1) * 512]) for g in range(2)], axis=1)
        z = z_ref[...]
        yz = y * (z * _sigmoid(z))
        outs = []
        for g in range(2):
            v = yz[:, g * 512:(g + 1) * 512]
            outs.append(v * lax.rsqrt(jnp.mean(v * v, axis=-1, keepdims=True) + EPS))
        mix_ref[...] = (jnp.concatenate(outs, axis=1) * nw_ref[...]).astype(BF16)

    def col(width, blk, prev=False):
        if prev:
            return pl.BlockSpec((8, width), lambda i: (jnp.maximum(i * (CH // 8) - 1, 0), blk))
        return pl.BlockSpec((CH, width), lambda i: (i, blk))

    def full(a):
        return pl.BlockSpec(a.shape, lambda i: (0,) * a.ndim)

    return pl.pallas_call(
        body, name="ssm_fwd", grid=(NC,),
        in_specs=[col(D, 5), col(D, 5, True), col(512, 12), col(512, 12, True), col(LANE, 52), col(D, 4),
                  full(cw), full(cb), full(dtb), full(alog), full(d_b), full(nw), ANY],
        out_specs=[col(D, 1), col(D, 0), pl.BlockSpec((None, CH, D), lambda i: (i, 0, 0)), col(D + 512, 0)],
        out_shape=[SDS((S, 2 * D), BF16), SDS((S, D), F32), SDS((NC, CH, D), F32), SDS((S, D + 512), F32)],
        scratch_shapes=[pltpu.VMEM((CH, D), F32)],
        input_output_aliases={12: 0},
        compiler_params=_cp(("arbitrary",)),
    )(proj, proj, proj, proj, proj, proj, cw, cb, dtb, alog, d_b, nw, mix)


def _ssm_bwd(proj, dn, y_save, states, conv_out, cw, dtb, alog, d_b, nw):
    def body(xs_ref, bc_ref, dt_ref, z_ref, dn_ref, y_ref, st_ref, conv_ref,
             cw_ref, dtb_ref, alog_ref, db_ref, nw_ref,
             dz_ref, dx_ref, dcw_ref, dcb_ref, dsm_ref, dnw_ref, dh_ref, nxs_ref, nbc_ref):
        i = pl.program_id(0)
        ci = NC - 1 - i

        @pl.when(i == 0)
        def _():
            for ref in (dcw_ref, dcb_ref, dsm_ref, dnw_ref, dh_ref, nxs_ref, nbc_ref):
                ref[...] = jnp.zeros_like(ref)

        cw = cw_ref[...]
        xs_c, bc_c = conv_ref[:, :D], conv_ref[:, D:]
        q = _ssd_common(xs_c, bc_c, dt_ref[...], dtb_ref[...], alog_ref[...])
        bg, cg = _groups(q["bc"])
        xs, dt_b, e_b, f_b, t_b = q["xs"], q["dt_b"], q["e_b"], q["f_b"], q["t_b"]
        xdt = xs * dt_b
        xdt_b = xdt.astype(BF16)
        h_in = st_ref[...]
        hb = h_in.astype(BF16)
        dh_new = dh_ref[...]
        dhb = dh_new.astype(BF16)
        red = _reduce()

        z, y, dn, nw_v = z_ref[...], y_ref[...], dn_ref[...], nw_ref[...]
        sig = _sigmoid(z)
        sz = z * sig
        yz = y * sz
        gdn = dn * nw_v
        dyz, dnw = [], []
        for g in range(2):
            v, gv = yz[:, g * 512:(g + 1) * 512], gdn[:, g * 512:(g + 1) * 512]
            r = lax.rsqrt(jnp.mean(v * v, axis=-1, keepdims=True) + EPS)
            dnw.append(dn[:, g * 512:(g + 1) * 512] * v * r)
            dyz.append(r * (gv - v * (r * r) * jnp.mean(gv * v, axis=-1, keepdims=True)))
        dyz = jnp.concatenate(dyz, axis=1)
        dnw_ref[...] += jnp.sum(jnp.concatenate(dnw, axis=1), axis=0, keepdims=True)
        dy = dyz * sz
        dz_ref[...] = (dyz * y * (sig * (1.0 + z * (1.0 - sig)))).astype(BF16)
        dy_b = dy.astype(BF16)

        tril = _iota((CH, CH), 1) <= _iota((CH, CH), 0)
        lane_a = _iota((CH, LANE), 1) < 64
        cbm = [_dot_nt(cg[g], bg[g]) for g in range(2)]
        dcbm = [jnp.zeros((CH, CH), F32), jnp.zeros((CH, CH), F32)]
        seg_rows = jnp.zeros((CH, LANE), F32)
        seg_cols = jnp.zeros((LANE, CH), F32)
        row_id, col_id = _iota((CH, LANE), 0), _iota((CH, LANE), 1)
        dx_pairs = []
        for hp in range(NH // 2):
            g = hp // 4
            xp = xdt_b[:, hp * LANE:(hp + 1) * LANE]
            dyp_f = dy[:, hp * LANE:(hp + 1) * LANE]
            dyp = dy_b[:, hp * LANE:(hp + 1) * LANE]
            halves = []
            for k in range(2):
                h = 2 * hp + k
                lane = lane_a if k == 0 else jnp.logical_not(lane_a)
                dec = _decay(q, h, tril)
                gm = cbm[g] * dec
                dgm = _dot_nt(jnp.where(lane, dyp_f, 0.0).astype(BF16), xp)
                dcbm[g] = dcbm[g] + dgm * dec
                prod = dgm * gm
                seg_rows = jnp.where(col_id == h, jnp.sum(prod, axis=1, keepdims=True), seg_rows)
                seg_cols = jnp.where(row_id == h, jnp.sum(prod, axis=0, keepdims=True), seg_cols)
                halves.append(_dot_tn(gm.astype(BF16), dyp))
            dx_pairs.append(jnp.where(lane_a, halves[0], halves[1]))
        dxdt_diag = jnp.concatenate(dx_pairs, axis=1)

        qv = jnp.concatenate([_dot(bg[g], dhb[:, g * 512:(g + 1) * 512]) for g in range(2)], axis=1)
        y_off = jnp.concatenate([_dot(cg[g], hb[:, g * 512:(g + 1) * 512]) for g in range(2)], axis=1) * e_b
        xfq = xdt * f_b * qv
        dxdt = dxdt_diag + f_b * qv
        tdt = jnp.sum(dh_new * h_in, axis=0, keepdims=True) * t_b
        per_head = _pick(jnp.concatenate([xfq, dy * y_off, dxdt * xs, dy * xs, jnp.broadcast_to(tdt, (8, D))],
                                         axis=0), red)
        fdf, dyoff_h, dxdtxs_h, dyxs_h = [per_head[k * CH:(k + 1) * CH] for k in range(4)]
        dcs = seg_rows - seg_cols.T + dyoff_h - fdf
        last = per_head[4 * CH:4 * CH + 1] + jnp.sum(fdf, axis=0, keepdims=True)
        dcs = dcs + jnp.where(_iota((CH, LANE), 0) == CH - 1, last, 0.0)
        tri_t = (_iota((CH, CH), 1) >= _iota((CH, CH), 0)).astype(BF16)
        da = _pick_left(tri_t, dcs)
        ddt = da * q["a_row"] + dxdtxs_h
        dxs = dxdt * dt_b + db_ref[...] * dy
        ddt_raw = ddt * _sigmoid(q["pre"])
        dsm_ref[0:1, :] += jnp.sum(ddt_raw, axis=0, keepdims=True)
        dsm_ref[1:2, :] += jnp.sum(da * q["dt"], axis=0, keepdims=True) * q["a_row"]
        dsm_ref[2:3, :] += jnp.sum(dyxs_h, axis=0, keepdims=True)
        edy = (e_b * dy).astype(BF16)
        xf = (xdt * f_b).astype(BF16)
        dbs, dcs_g, dhs = [], [], []
        for g in range(2):
            sl = slice(g * 512, (g + 1) * 512)
            dcb_b = dcbm[g].astype(BF16)
            dcs_g.append(_dot(dcb_b, bg[g]) + _dot_nt(edy[:, sl], hb[:, sl]))
            dbs.append(_dot_tn(dcb_b, cg[g]) + _dot_nt(xf[:, sl], dhb[:, sl]))
            dhs.append(_dot_tn(cg[g], edy[:, sl]))
        dh_ref[...] = t_b * dh_new + jnp.concatenate(dhs, axis=1)
        dbc = jnp.concatenate(dbs + dcs_g, axis=1)

        def conv_bwd(dact, pre, x_raw, w, nxt_ref, lo):
            s = _sigmoid(pre)
            dconv = dact * (s * (1.0 + pre * (1.0 - s)))
            nxt8 = nxt_ref[...]
            row8 = _iota(nxt8.shape, 0)
            hi = lo + dconv.shape[1]
            dcb_ref[:, lo:hi] += jnp.sum(dconv, axis=0, keepdims=True)
            later = [dconv]
            for s_ in (1, 2, 3):
                rolled = pltpu.roll(dconv, CH - s_, 0)
                tail = jnp.where(row8 >= 8 - s_, pltpu.roll(nxt8, 8 - s_, 0), rolled[CH - 8:])
                later.append(jnp.concatenate([rolled[:CH - 8], tail], axis=0))
            dx = None
            for s_, up in enumerate(later):
                k = 3 - s_
                dcw_ref[k:k + 1, lo:hi] += jnp.sum(up * x_raw, axis=0, keepdims=True)
                dx = w[k:k + 1, :] * up if dx is None else dx + w[k:k + 1, :] * up
            nxt_ref[...] = dconv[:8]
            return dx

        dx_ref[:, 0:D] = conv_bwd(dxs, xs_c, xs_ref[...], cw[:, :D], nxs_ref, 0).astype(BF16)
        dx_ref[:, D:D + 512] = conv_bwd(dbc, bc_c, bc_ref[...], cw[:, D:], nbc_ref, D).astype(BF16)
        dx_ref[:, D + 512:D + 640] = ddt_raw.astype(BF16)
        dx_ref[:, D + 640:] = jnp.zeros((CH, D - 640), BF16)

    def col(width, blk):
        return pl.BlockSpec((CH, width), lambda i: (NC - 1 - i, blk))

    def full(a):
        return pl.BlockSpec(a.shape, lambda i: (0,) * len(a.shape))

    acc_shapes = [SDS((4, 1536), F32), SDS((1, 1536), F32), SDS((8, LANE), F32), SDS((1, D), F32)]
    return pl.pallas_call(
        body, name="ssm_bwd", grid=(NC,),
        in_specs=[col(D, 5), col(512, 12), col(LANE, 52), col(D, 4),
                  col(D, 0), col(D, 0), pl.BlockSpec((None, CH, D), lambda i: (NC - 1 - i, 0, 0)), col(D + 512, 0),
                  full(cw), full(dtb), full(alog), full(d_b), full(nw)],
        out_specs=[col(D, 0), col(2 * D, 0)] + [full(a) for a in acc_shapes],
        out_shape=[SDS((S, D), BF16), SDS((S, 2 * D), BF16)] + acc_shapes,
        scratch_shapes=[pltpu.VMEM((CH, D), F32), pltpu.VMEM((8, D), F32), pltpu.VMEM((8, 512), F32)],
        compiler_params=_cp(("arbitrary",)),
    )(proj, proj, proj, proj, dn, y_save, states, conv_out, cw, dtb, alog, d_b, nw)


def _outproj_loss(mix, w_out, x, tgt, nw, attn_pre, proj):
    tm = 256

    def body(mix_ref, w_ref, x_ref, t_ref, nw_ref, pre_ref, g_ref,
             dy_ref, dn_ref, do_ref, delta_ref, dg_ref, dw_ref, dnw_ref, loss_ref):
        @pl.when(pl.program_id(0) == 0)
        def _():
            dw_ref[...] = jnp.zeros_like(dw_ref)
            dnw_ref[...] = jnp.zeros_like(dnw_ref)
            loss_ref[...] = jnp.zeros_like(loss_ref)

        mixv, w = mix_ref[...], w_ref[...]
        out = _dot(mixv, w)
        r = lax.rsqrt(jnp.mean(out * out, axis=-1, keepdims=True) + EPS)
        nh = out * r
        nw_v = nw_ref[...]
        err = x_ref[...] + nh * nw_v - t_ref[...]
        loss_ref[...] += 0.5 * jnp.sum(jnp.mean(err * err, axis=-1, keepdims=True), axis=0, keepdims=True)
        dy = err * (1.0 / D)
        dy_ref[...] = dy
        dnw_ref[...] += jnp.sum(dy * nh, axis=0, keepdims=True)
        gdn = dy * nw_v
        dout = (r * (gdn - nh * jnp.mean(gdn * nh, axis=-1, keepdims=True))).astype(BF16)
        dmix = _dot_nt(dout, w)
        dw_ref[...] += _dot_tn(mixv, dout)
        dn_ref[...] = dmix[:, D:]
        dm, g, pre_v = dmix[:, :D], g_ref[...], pre_ref[...]
        sig = _sigmoid(g)
        do = dm * (g * sig)
        do_ref[...] = do
        dg_ref[...] = (dm * pre_v * (sig * (1.0 + g * (1.0 - sig)))).astype(BF16)
        prod = do * pre_v
        same_head = (_iota((LANE, LANE), 0) // 64 == _iota((LANE, LANE), 1) // 64).astype(BF16)
        for cb in range(D // LANE):
            delta_ref[:, cb * LANE:(cb + 1) * LANE] = _pick(prod[:, cb * LANE:(cb + 1) * LANE], same_head)

    row = lambda w: pl.BlockSpec((tm, w), lambda i: (i, 0))
    full = lambda s: pl.BlockSpec(s, lambda i: (0, 0))
    return pl.pallas_call(
        body, name="outproj_loss", grid=(S // tm,),
        in_specs=[row(2 * D), full((2 * D, D)), row(D), row(D), full((1, D)), row(D),
                  pl.BlockSpec((tm, D), lambda i: (i, OFF_G // D))],
        out_specs=[row(D), row(D), row(D), row(D), row(D), full((2 * D, D)), full((1, D)), full((1, LANE))],
        out_shape=[SDS((S, D), F32)] * 4 + [SDS((S, D), BF16), SDS((2 * D, D), F32), SDS((1, D), F32),
                                            SDS((1, LANE), F32)],
        compiler_params=_cp(("arbitrary",)),
    )(mix, w_out, x, tgt, nw, attn_pre, proj)


def _inproj_bwd_dx(srcs, dxbcdt, w_all, x, dy, nw, hosted=None):
    tm = 512
    nk = DP // D
    n_host = len(hosted.arrays) if hosted else 0

    def body(*refs):
        src_refs = refs[:nk]
        w_ref, x_ref, dy_ref, nw_ref = refs[nk:nk + 4]
        host_in, refs = refs[nk + 4:nk + 4 + n_host], refs[nk + 4 + n_host:]
        gx_ref, dnw_ref = refs[:2]
        host_out, host_sems = refs[2:2 + n_host], refs[2 + n_host:]
        i = pl.program_id(0)

        @pl.when(i == 0)
        def _():
            if hosted:
                hosted.start(host_in, host_out, host_sems)
            dnw_ref[...] = jnp.zeros_like(dnw_ref)

        du = None
        for k, ref in enumerate(src_refs):
            part = _dot_nt(ref[...], w_ref[:, k * D:(k + 1) * D])
            du = part if du is None else du + part
        xf, nw_v = x_ref[...], nw_ref[...]
        r = lax.rsqrt(jnp.mean(xf * xf, axis=-1, keepdims=True) + EPS)
        xh = xf * r
        dnw_ref[...] += jnp.sum(du * xh, axis=0, keepdims=True)
        gdu = du * nw_v
        gx_ref[...] = r * (gdu - xh * jnp.mean(gdu * xh, axis=-1, keepdims=True)) + dy_ref[...]

        if hosted:
            pl.when(i == S // tm - 1)(lambda: hosted.finish(host_in, host_out, host_sems))

    row = pl.BlockSpec((tm, D), lambda i: (i, 0))
    row1 = pl.BlockSpec((tm, D), lambda i: (i, 1))
    one = pl.BlockSpec((1, D), lambda i: (0, 0))
    whole_w = pl.BlockSpec((D, DP), lambda i: (0, 0), pipeline_mode=pl.Buffered(1))
    args = [*srcs, dxbcdt, dxbcdt, w_all, x, dy, nw]
    in_specs = [row] * len(srcs) + [row, row1, whole_w, row, row, one]
    out_specs, out_shape, scratch = [row, one], [SDS((S, D), F32), SDS((1, D), F32)], []
    if hosted:
        args += hosted.arrays
        in_specs += [ANY] * n_host
        out_specs += [ANY] * n_host
        out_shape += hosted.out_shape
        scratch += hosted.scratch
    outs = pl.pallas_call(
        body, name="inproj_bwd_dx", grid=(S // tm,),
        in_specs=in_specs, out_specs=out_specs, out_shape=out_shape, scratch_shapes=scratch,
        compiler_params=_cp(("arbitrary",)),
    )(*args)
    return (outs[:2], outs[2:]) if hosted else outs


def _dw(u, dsec, name):
    ts = 1024
    ncol = dsec.shape[1] // D

    def body(u_ref, d_ref, o_ref):
        @pl.when(pl.program_id(1) == 0)
        def _():
            o_ref[...] = jnp.zeros_like(o_ref)

        o_ref[...] += _dot_tn(u_ref[...], d_ref[...])

    return pl.pallas_call(
        body, name=name, grid=(ncol, S // ts),
        in_specs=[pl.BlockSpec((ts, D), lambda j, i: (i, 0)), pl.BlockSpec((ts, D), lambda j, i: (i, j))],
        out_specs=pl.BlockSpec((D, D), lambda j, i: (0, j)),
        out_shape=SDS((D, ncol * D), F32),
        compiler_params=_cp(("parallel", "arbitrary")),
    )(u, dsec)


def _place():
    x, y, c = lax.axis_index("x"), lax.axis_index("y"), lax.axis_index("c")
    return x, y, c, 2 * x + y


def _chip_of(x, y, k):
    px = 1 - x if k & 2 else x
    py = 1 - y if k & 1 else y
    return px, py, 2 * px + py


def _remote(src, dst, send_sem, recv_sem, dev):
    return pltpu.make_async_remote_copy(src_ref=src, dst_ref=dst, send_sem=send_sem, recv_sem=recv_sem,
                                        device_id=dev, device_id_type=MESH)


def _gather_weights(w_in_b):
    half = w_in_b.shape[0] // 2
    quarter = half // 2

    def body(src, dst, send, recv):
        x, y, c, j = _place()
        me, sib = (x, y, c), (x, y, 1 - c)
        nbr = {"x": _chip_of(x, y, 2), "y": _chip_of(x, y, 1)}
        diag = _chip_of(x, y, 3)[2]
        started, arrivals = [], []

        def rows(n_quarter=None, sibling=False):
            base = (1 - c if sibling else c) * half
            return pl.ds(base, half) if n_quarter is None else pl.ds(base + n_quarter * quarter, quarter)

        def sem(n):
            return send.at[n], recv.at[n]

        def go(cp):
            cp.start()
            started.append(cp)

        own = _remote(src, dst.at[j], *sem(8), sib)
        go(own)
        for n, axis in enumerate("xy"):
            px, py, _ = nbr[axis]
            go(_remote(src.at[rows()], dst.at[j, rows()], *sem(n), (px, py, c)))
        for n, axis in enumerate("xy"):
            ox, oy, _ = nbr["y" if axis == "x" else "x"]
            pj = nbr[axis][2]
            _remote(src.at[rows()], dst.at[pj, rows()], *sem(n), me).wait_recv()
            go(_remote(dst.at[pj, rows(n)], dst.at[pj, rows(n)], *sem(2 + n), (ox, oy, c)))
            go(_remote(dst.at[pj, rows()], dst.at[pj, rows()], *sem(4 + n), sib))
            arrivals.append(_remote(src.at[rows()], dst.at[pj, rows(None, True)], *sem(4 + n), me))
        for n in range(2):
            _remote(dst.at[diag, rows(n)], dst.at[diag, rows(n)], *sem(2 + n), me).wait_recv()
            go(_remote(dst.at[diag, rows(n)], dst.at[diag, rows(n)], *sem(6 + n), sib))
            arrivals.append(_remote(dst.at[diag, rows(n, True)], dst.at[diag, rows(n, True)], *sem(6 + n), me))
        for cp in arrivals + [own]:
            cp.wait_recv()
        for cp in started:
            cp.wait_send()

    return pl.pallas_call(
        body, name="gather_weights", in_specs=[ANY], out_specs=ANY,
        out_shape=SDS((4,) + w_in_b.shape, BF16),
        scratch_shapes=[pltpu.SemaphoreType.DMA((9,)), pltpu.SemaphoreType.DMA((9,))],
        compiler_params=pltpu.CompilerParams(has_side_effects=True),
    )(w_in_b)


class _LateGather:
    def __init__(self, w_out_b, conv_w):
        self.arrays = [w_out_b, conv_w]
        self.out_shape = [SDS((4,) + w_out_b.shape, BF16), SDS((4,) + conv_w.shape, F32)]
        self.scratch = [pltpu.SemaphoreType.DMA((11,)), pltpu.SemaphoreType.DMA((11,))]

    def _plan(self, ins, outs, sems):
        x, y, c, j = _place()
        send, recv = sems
        (wo, cw), (gwo, gcw) = ins, outs
        half = wo.shape[0] // 2
        mine, theirs = pl.ds(c * half, half), pl.ds((1 - c) * half, half)
        me, sib = (x, y, c), (x, y, 1 - c)
        first, arrive, forward, last = [], [], [], []
        for k in (1, 2, 3):
            px, py, pj = _chip_of(x, y, k)
            first += [_remote(wo.at[mine], gwo.at[j, mine], send.at[k - 1], recv.at[k - 1], (px, py, c)),
                      _remote(cw, gcw.at[j], send.at[k + 2], recv.at[k + 2], (px, py, c))]
            arrive.append(_remote(wo.at[mine], gwo.at[pj, mine], send.at[k - 1], recv.at[k - 1], me))
            forward.append(_remote(gwo.at[pj, mine], gwo.at[pj, mine], send.at[k + 5], recv.at[k + 5], sib))
            last += [_remote(cw, gcw.at[pj], send.at[k + 2], recv.at[k + 2], me),
                     _remote(wo.at[theirs], gwo.at[pj, theirs], send.at[k + 5], recv.at[k + 5], me)]
        first += [_remote(wo, gwo.at[j], send.at[9], recv.at[9], sib),
                  _remote(cw, gcw.at[j], send.at[10], recv.at[10], sib)]
        last += first[-2:]
        return first, arrive, forward, last

    def start(self, ins, outs, sems):
        for cp in self._plan(ins, outs, sems)[0]:
            cp.start()

    def finish(self, ins, outs, sems):
        first, arrive, forward, last = self._plan(ins, outs, sems)
        for got, fwd in zip(arrive, forward):
            got.wait_recv()
            fwd.start()
        for cp in last:
            cp.wait_recv()
        for cp in first + forward:
            cp.wait_send()


class _PairExchange:
    def __init__(self, arrays):
        self.arrays = list(arrays)
        self.out_shape = [SDS((a.shape[0], a.shape[1] // 2, a.shape[2]), F32) for a in self.arrays]
        self.scratch = [pltpu.SemaphoreType.DMA((len(self.arrays),)) for _ in range(2)]

    def _copies(self, ins, outs, sems):
        x, y, c, _ = _place()
        for k, (src, dst) in enumerate(zip(ins, outs)):
            half = src.shape[1] // 2
            yield _remote(src.at[:, pl.ds((1 - c) * half, half)], dst, sems[0].at[k], sems[1].at[k], (x, y, 1 - c))

    def start(self, ins, outs, sems):
        for cp in self._copies(ins, outs, sems):
            cp.start()

    def finish(self, ins, outs, sems):
        for cp in self._copies(ins, outs, sems):
            cp.wait()


def _pair_exchange(arrays, name):
    halves = [a.shape[1] // 2 for a in arrays]
    n = len(arrays)

    def body(*refs):
        x, y, c, _ = _place()
        send, recv = refs[2 * n:]
        cps = [_remote(refs[k].at[:, pl.ds((1 - c) * halves[k], halves[k])], refs[n + k], send.at[k], recv.at[k],
                       (x, y, 1 - c)) for k in range(n)]
        for cp in cps:
            cp.start()
        for cp in cps:
            cp.wait()

    return pl.pallas_call(
        body, name=name, in_specs=[ANY] * n, out_specs=[ANY] * n,
        out_shape=[SDS((a.shape[0], h, a.shape[2]), F32) for a, h in zip(arrays, halves)],
        scratch_shapes=[pltpu.SemaphoreType.DMA((n,)), pltpu.SemaphoreType.DMA((n,))],
        compiler_params=pltpu.CompilerParams(has_side_effects=True),
    )(*arrays)


def _pair_sum(cidx, g, r, name):
    n, half, width = r.shape
    tr = min(half, 256)
    nt = half // tr

    def body(c_ref, g_ref, r_ref, o_ref):
        del c_ref
        o_ref[...] = (g_ref[...] + r_ref[...]).astype(BF16)

    return pl.pallas_call(
        body, name=name,
        grid_spec=pltpu.PrefetchScalarGridSpec(
            num_scalar_prefetch=1, grid=(n, nt),
            in_specs=[pl.BlockSpec((None, tr, width), lambda s, t, c: (s, c[0] * nt + t, 0)),
                      pl.BlockSpec((None, tr, width), lambda s, t, c: (s, t, 0))],
            out_specs=pl.BlockSpec((None, tr, width), lambda s, t, c: (s, t, 0))),
        out_shape=SDS(r.shape, BF16),
        compiler_params=_cp(("parallel", "parallel")),
    )(cidx, g, r)


class _ChipExchange:
    def __init__(self, arrays, rows):
        self.arrays, self.rows = list(arrays), list(rows)
        self.out_shape = [SDS((4,) + a.shape[1:], BF16) for a in self.arrays]
        self.scratch = [pltpu.SemaphoreType.DMA((3 * len(self.arrays),)) for _ in range(2)]

    def _copies(self, ins, outs, sems):
        x, y, c, j = _place()
        send, recv = sems
        for a, (src, dst, row) in enumerate(zip(ins, outs, self.rows)):
            for k in (1, 2, 3):
                px, py, pj = _chip_of(x, y, k)
                n = 3 * a + k - 1
                slot = pj if row is None else py
                yield (None if row is None else px == row, None if row is None else x == row,
                       _remote(src.at[slot], dst.at[j], send.at[n], recv.at[n], (px, py, c)),
                       _remote(src.at[0], dst.at[pj], send.at[n], recv.at[n], (x, y, c)))

    def start(self, ins, outs, sems):
        for sends, _, send, _ in self._copies(ins, outs, sems):
            if sends is None:
                send.start()
            else:
                pl.when(sends)(send.start)

    def finish(self, ins, outs, sems):
        for sends, owns, send, arrival in self._copies(ins, outs, sems):
            if sends is None:
                arrival.wait_recv()
                send.wait_send()
            else:
                pl.when(owns)(arrival.wait_recv)
                pl.when(sends)(send.wait_send)


def _small_exchange(small):
    def body(sm_ref, rs_ref, send, recv, lsem):
        x, y, c, j = _place()
        me = 2 * j + c
        local = pltpu.make_async_copy(sm_ref, rs_ref.at[me], lsem)
        local.start()
        cps = []
        for k in range(1, 8):
            px, py, _ = _chip_of(x, y, k >> 1)
            pc = 1 - c if k & 1 else c
            cps.append(_remote(sm_ref, rs_ref.at[me], send.at[k - 1], recv.at[k - 1], (px, py, pc)))
        for cp in cps:
            cp.start()
        for k in range(1, 8):
            _, _, pj = _chip_of(x, y, k >> 1)
            pc = 1 - c if k & 1 else c
            _remote(sm_ref, rs_ref.at[2 * pj + pc], send.at[k - 1], recv.at[k - 1], (x, y, c)).wait_recv()
        for cp in cps:
            cp.wait_send()
        local.wait()

    return pl.pallas_call(
        body, name="small_exchange", in_specs=[ANY], out_specs=ANY,
        out_shape=SDS((8,) + small.shape, F32),
        scratch_shapes=[pltpu.SemaphoreType.DMA((7,)), pltpu.SemaphoreType.DMA((7,)), pltpu.SemaphoreType.DMA],
        compiler_params=pltpu.CompilerParams(has_side_effects=True),
    )(small)


def _slot_sum(r, name):
    n, rows, width = r.shape
    tr = min(rows, 256)

    def body(r_ref, o_ref):
        acc = r_ref[0].astype(F32)
        for s in range(1, n):
            acc = acc + r_ref[s].astype(F32)
        o_ref[...] = acc

    return pl.pallas_call(
        body, name=name, grid=(rows // tr,),
        in_specs=[pl.BlockSpec((n, tr, width), lambda t: (0, t, 0))],
        out_specs=pl.BlockSpec((tr, width), lambda t: (t, 0)),
        out_shape=SDS((rows, width), F32),
        compiler_params=_cp(("parallel",)),
    )(r)


def _chip_sum(chip_idx, recv, own, name):
    n, rows, width = recv.shape
    tr = min(rows, 256)

    def body(j_ref, r_ref, own_ref, o_ref):
        acc = None
        for s in range(n):
            term = jnp.where(j_ref[0] == s, own_ref[...], r_ref[s]).astype(F32)
            acc = term if acc is None else acc + term
        o_ref[...] = acc

    return pl.pallas_call(
        body, name=name,
        grid_spec=pltpu.PrefetchScalarGridSpec(
            num_scalar_prefetch=1, grid=(rows // tr,),
            in_specs=[pl.BlockSpec((n, tr, width), lambda t, j: (0, t, 0)),
                      pl.BlockSpec((None, tr, width), lambda t, j: (j[0], t, 0))],
            out_specs=pl.BlockSpec((tr, width), lambda t, j: (t, 0))),
        out_shape=SDS((rows, width), F32),
        compiler_params=_cp(("parallel",)),
    )(chip_idx, recv, own)


def _chip_sum_rows(place, recv0, own0, recv1, own1, name):
    n, rows, width = recv0.shape
    tr = min(rows, 256)

    def body(p_ref, r0_ref, o0_ref, r1_ref, o1_ref, o_ref):
        first_row = p_ref[1] == 0
        own = jnp.where(first_row, o0_ref[...], o1_ref[...])
        acc = None
        for s in range(n):
            term = jnp.where(p_ref[0] == s, own, jnp.where(first_row, r0_ref[s], r1_ref[s])).astype(F32)
            acc = term if acc is None else acc + term
        o_ref[...] = acc

    recv = pl.BlockSpec((n, tr, width), lambda t, p: (0, t, 0))
    own = pl.BlockSpec((None, tr, width), lambda t, p: (p[2], t, 0))
    return pl.pallas_call(
        body, name=name,
        grid_spec=pltpu.PrefetchScalarGridSpec(
            num_scalar_prefetch=1, grid=(rows // tr,), in_specs=[recv, own, recv, own],
            out_specs=pl.BlockSpec((tr, width), lambda t, p: (t, 0))),
        out_shape=SDS((rows, width), F32),
        compiler_params=_cp(("parallel",)),
    )(place, recv0, own0, recv1, own1)


def _half_exchange(hw, ho):
    def body(hw_ref, ho_ref, tw_ref, to_ref, send, recv):
        x, y, c, _ = _place()
        sib = (x, y, 1 - c)
        cps = [_remote(hw_ref, tw_ref, send.at[0], recv.at[0], sib),
               _remote(ho_ref, to_ref, send.at[1], recv.at[1], sib)]
        for cp in cps:
            cp.start()
        for cp in cps:
            cp.wait()

    return pl.pallas_call(
        body, name="half_exchange", in_specs=[ANY, ANY], out_specs=[ANY, ANY],
        out_shape=[SDS(hw.shape, F32), SDS(ho.shape, F32)],
        scratch_shapes=[pltpu.SemaphoreType.DMA((2,)), pltpu.SemaphoreType.DMA((2,))],
        compiler_params=pltpu.CompilerParams(has_side_effects=True),
    )(hw, ho)


def _by_core(c, mine, theirs):
    return jnp.where(c == 0, jnp.concatenate([mine, theirs], axis=0), jnp.concatenate([theirs, mine], axis=0))


def _adamw(w, g, m, v, name):
    rows, width = w.shape
    tr = min(rows, 256)

    def body(w_ref, g_ref, m_ref, v_ref, d_ref, nm_ref, nv_ref):
        gv = g_ref[...]
        nm = ADAM_B1 * m_ref[...] + (1.0 - ADAM_B1) * gv
        nv = ADAM_B2 * v_ref[...] + (1.0 - ADAM_B2) * (gv * gv)
        m_hat = nm / (1.0 - ADAM_B1 ** ADAM_STEP)
        v_hat = nv / (1.0 - ADAM_B2 ** ADAM_STEP)
        d_ref[...] = -ADAM_LR * (m_hat / (jnp.sqrt(v_hat) + ADAM_EPS) + ADAM_WD * w_ref[...])
        nm_ref[...] = nm
        nv_ref[...] = nv

    t = pl.BlockSpec((tr, width), lambda i: (i, 0))
    return pl.pallas_call(
        body, name=name, grid=(rows // tr,), in_specs=[t] * 4, out_specs=[t] * 3,
        out_shape=[SDS(w.shape, F32)] * 3, compiler_params=_cp(("parallel",)),
    )(w, g, m, v)


def _rowwise(a):
    return jnp.transpose(a, (2, 0, 1)).reshape(SHARD * D // LANE, LANE)


def _columns(ref):
    return jnp.concatenate([ref[pl.ds(c, LANE, stride=8), :].T for c in range(D // LANE)], axis=0)


def _shard_bf16(w_rows):
    def body(w_ref, o_ref):
        o_ref[...] = _columns(w_ref).astype(BF16)

    return pl.pallas_call(
        body, name="shard_bf16", grid=(pl.cdiv(SHARD, LANE),),
        in_specs=[pl.BlockSpec((D, LANE), lambda t: (t, 0))], out_specs=pl.BlockSpec((D, LANE), lambda t: (0, t)),
        out_shape=SDS((D, SHARD), BF16), compiler_params=_cp(("parallel",)),
    )(w_rows)


def _adamw_in(w_rows, g, m_rows, v_rows):
    def body(w_ref, g_ref, m_ref, v_ref, d_ref, nm_ref, nv_ref):
        columns = _columns
        gv = g_ref[...]
        nm = ADAM_B1 * columns(m_ref) + (1.0 - ADAM_B1) * gv
        nv = ADAM_B2 * columns(v_ref) + (1.0 - ADAM_B2) * (gv * gv)
        m_hat = nm / (1.0 - ADAM_B1 ** ADAM_STEP)
        v_hat = nv / (1.0 - ADAM_B2 ** ADAM_STEP)
        d_ref[...] = -ADAM_LR * (m_hat / (jnp.sqrt(v_hat) + ADAM_EPS) + ADAM_WD * columns(w_ref))
        nm_ref[...] = nm
        nv_ref[...] = nv

    tile = pl.BlockSpec((D, LANE), lambda t: (0, t))
    rows = pl.BlockSpec((D, LANE), lambda t: (t, 0))
    return pl.pallas_call(
        body, name="adamw_in", grid=(pl.cdiv(SHARD, LANE),), in_specs=[rows, tile, rows, rows],
        out_specs=[tile] * 3, out_shape=[SDS(g.shape, F32)] * 3, compiler_params=_cp(("parallel",)),
    )(w_rows, g, m_rows, v_rows)


def _rows128(a, rows):
    flat = a.reshape(-1)
    return jnp.pad(flat, (0, rows * LANE - flat.shape[0])).reshape(rows, LANE)


def _pack_small(conv_w, norm_pre, conv_b, ssm_norm, norm_post, dtb, alog, dsk, extra=None):
    cw_rows = 48 if conv_w.shape[-1] == 1536 else 16
    extra = jnp.zeros((1, LANE), F32) if extra is None else _rows128(extra, 1)
    vec = jnp.concatenate([_rows128(dtb, 1), _rows128(alog, 1), _rows128(dsk, 1), extra, jnp.zeros((4, LANE), F32)],
                          axis=0)
    return jnp.concatenate([_rows128(conv_w, cw_rows), _rows128(norm_pre, 8), _rows128(conv_b, 16),
                            _rows128(ssm_norm, 8), _rows128(norm_post, 8), vec], axis=0)


def _unpack_small(p, cw_cols):
    cw_rows = 48 if cw_cols == 1536 else 16
    o = cw_rows
    conv_w = p[:cw_rows].reshape(-1)[:4 * cw_cols].reshape(1, 4, cw_cols)
    norm_pre = p[o:o + 8].reshape(1, D)
    conv_b = p[o + 8:o + 24].reshape(-1)[:1536].reshape(1, 1536)
    ssm_norm = p[o + 24:o + 32].reshape(1, D)
    norm_post = p[o + 32:o + 40].reshape(1, D)
    vec = p[o + 40:o + 48]
    return conv_w, norm_pre, conv_b, ssm_norm, norm_post, vec[0:1, :NH], vec[1:2, :NH], vec[2:3, :NH], vec[3, 0]


def _pad_lanes(a):
    return jnp.pad(a, ((0, 0), (0, LANE - a.shape[1])))


class _GradReduce:
    SPLIT = 2 * SHARD - OFF_G

    def __init__(self, xi, yi, ci):
        self.ci = ci
        self.cidx = jnp.reshape(ci, (1,)).astype(jnp.int32)
        self.place = jnp.stack([2 * xi + yi, xi, yi]).astype(jnp.int32)

    def pairs(self, dw_g, dw_z, dw_x, dw_out):
        cols = jnp.concatenate([dw_g[:, self.SPLIT:], dw_z, dw_x], axis=1)
        self.gw_hi = jnp.stack([cols[:, :SHARD], cols[:, SHARD:2 * SHARD]])
        self.go = dw_out.reshape(4, D // 2, D)
        return _PairExchange([self.gw_hi, self.go])

    def first(self, got):
        rw, ro = got
        self.pw_hi = _pair_sum(self.cidx, self.gw_hi, rw, "pair_sum_hi")
        self.po = _pair_sum(self.cidx, self.go, ro, "pair_sum_out")
        return _ChipExchange([self.pw_hi, self.po], [1, None])

    def first_done(self, got):
        self.rw_hi, self.ro = got

    def second(self, dw_q, dw_k, dw_v, dw_g):
        cols = jnp.concatenate([dw_q, dw_k, dw_v, dw_g[:, :self.SPLIT]], axis=1)
        gw = jnp.stack([cols[:, :SHARD], cols[:, SHARD:]])
        (rw,) = _pair_exchange([gw], "pair_exchange_lo")
        self.pw_lo = _pair_sum(self.cidx, gw, rw, "pair_sum_lo")
        return _ChipExchange([self.pw_lo], [0])

    def second_done(self, got):
        (self.rw_lo,) = got

    def result(self):
        half_in = _chip_sum_rows(self.place, self.rw_lo, self.pw_lo, self.rw_hi, self.pw_hi, "chip_sum_in")
        half_out = _chip_sum(self.place[0:1], self.ro, self.po, "chip_sum_out")
        their_in, their_out = _half_exchange(half_in, half_out)
        return _by_core(self.ci, half_in, their_in), _by_core(self.ci, half_out, their_out)


def kernel(x, norm_pre_w, w_in, conv_w, conv_b, dt_bias, a_log, d_skip, ssm_norm_w, w_out, norm_post_w, loss_target, m_norm_pre_w, m_w_in, m_conv_w, m_conv_b, m_dt_bias, m_a_log, m_d_skip, m_ssm_norm_w, m_w_out, m_norm_post_w, v_norm_pre_w, v_w_in, v_conv_w, v_conv_b, v_dt_bias, v_a_log, v_d_skip, v_ssm_norm_w, v_w_out, v_norm_post_w):
    xi, yi, ci = lax.axis_index("x"), lax.axis_index("y"), lax.axis_index("c")
    chip = 2 * xi + yi
    x2, tgt = x[0], loss_target[0]

    w_rows = _rowwise(w_in)
    gin = _gather_weights(_shard_bf16(w_rows))
    w_all = jnp.concatenate([gin[0], gin[1], gin[2], gin[3], jnp.zeros((D, DP - 4 * SHARD), BF16)], axis=1)
    reduce = _GradReduce(xi, yi, ci)
    grad_x, small = _local_step(x2, tgt, w_all, _LateGather(w_out[0].astype(BF16), conv_w[0]), norm_pre_w, conv_b,
                                dt_bias, a_log, d_skip, ssm_norm_w, norm_post_w, reduce)[:2]
    g_in, g_out = reduce.result()
    g_small = _slot_sum(_small_exchange(small), "small_sum")
    g_cw, g_npre, g_cb, g_nssm, g_npost, g_dtb, g_alog, g_dsk, loss = _unpack_small(g_small, 1536)
    g_cw = lax.dynamic_slice_in_dim(g_cw, chip * 384, 384, axis=2)

    d_in, nm_in, nv_in = _adamw_in(w_rows, g_in, _rowwise(m_w_in), _rowwise(v_w_in))
    d_out, nm_out, nv_out = _adamw(w_out[0], g_out, m_w_out[0], v_w_out[0], "adamw_out")
    packed = [_pack_small(*t) for t in (
        (conv_w, norm_pre_w, conv_b, ssm_norm_w, norm_post_w, dt_bias, a_log, d_skip),
        (g_cw, g_npre, g_cb, g_nssm, g_npost, g_dtb, g_alog, g_dsk),
        (m_conv_w, m_norm_pre_w, m_conv_b, m_ssm_norm_w, m_norm_post_w, m_dt_bias, m_a_log, m_d_skip),
        (v_conv_w, v_norm_pre_w, v_conv_b, v_ssm_norm_w, v_norm_post_w, v_dt_bias, v_a_log, v_d_skip))]
    small_out = [_unpack_small(p, 384)[:8] for p in _adamw(*packed, "adamw_small")]

    def ordered(cw_, npre, cb_, nssm, npost, dtb_, alog_, dsk_, big_in, big_out):
        return [npre, big_in[None], cw_, cb_, dtb_, alog_, dsk_, nssm, big_out[None], npost]

    grads = ordered(g_cw, g_npre, g_cb, g_nssm, g_npost, g_dtb, g_alog, g_dsk, g_in, g_out)
    deltas = ordered(*small_out[0], d_in, d_out)
    new_m = ordered(*small_out[1], nm_in, nm_out)
    new_v = ordered(*small_out[2], nv_in, nv_out)
    return (loss, grad_x[None], *grads, *deltas, *new_m, *new_v)


def _local_step(x2, tgt, w_all, late, norm_pre_w, conv_b, dt_bias, a_log, d_skip, ssm_norm_w,
                norm_post_w, reduce=None):
    dtb, alog = _pad_lanes(dt_bias), _pad_lanes(a_log)
    d_b = jnp.repeat(d_skip, 64, axis=1)

    if isinstance(late, _LateGather):
        (proj, u), (gout, gcw) = _inproj_fwd(x2, norm_pre_w, w_all, late)
        w_out_all = gout.reshape(2 * D, D)
        cw_all = jnp.concatenate([gcw[0], gcw[1], gcw[2], gcw[3]], axis=1)
    else:
        proj, u = _inproj_fwd(x2, norm_pre_w, w_all)
        w_out_all, cw_all = late
    mix, attn_pre, lse = _attn_fwd(proj, 1, _attn_fwd(proj, 4, _attn_fwd(proj, 16)), final=True)
    mix, y_save, states, conv_out = _ssm_fwd(proj, mix, cw_all, conv_b, dtb, alog, d_b, ssm_norm_w)

    dy, dn_ssm, do, delta, dg, dw_out, dnw_post, loss_part = _outproj_loss(mix, w_out_all, x2, tgt, norm_post_w,
                                                                          attn_pre, proj)
    dz, dxbcdt, dcw, dcb, dvec, dnw_ssm = _ssm_bwd(proj, dn_ssm, y_save, states, conv_out, cw_all, dtb, alog, d_b,
                                                   ssm_norm_w)
    dw_g, dw_z, dw_x = _dw(u, dg, "dw_in_g"), _dw(u, dz, "dw_in_z"), _dw(u, dxbcdt, "dw_in_xbcdt")
    acc = _attn_bwd(proj, do, lse, delta, 16, None, F32, reduce.pairs(dw_g, dw_z, dw_x, dw_out) if reduce else None)
    if reduce:
        acc, got = acc
    acc = _attn_bwd(proj, do, lse, delta, 4, acc, F32, reduce.first(got) if reduce else None)
    if reduce:
        acc, got = acc
        reduce.first_done(got)
    dq, dk, dv = _attn_bwd(proj, do, lse, delta, 1, acc, BF16)
    dw_q, dw_k, dw_v = _dw(u, dq, "dw_in_q"), _dw(u, dk, "dw_in_k"), _dw(u, dv, "dw_in_v")
    res = _inproj_bwd_dx([dq, dk, dv, dg, dz], dxbcdt, w_all, x2, dy, norm_pre_w,
                         reduce.second(dw_q, dw_k, dw_v, dw_g) if reduce else None)
    if reduce:
        res, got = res
        reduce.second_done(got)
    grad_x, dnw_pre = res
    dw_all = jnp.concatenate([dw_q, dw_k, dw_v, dw_g, dw_z, dw_x], axis=1)
    small = _pack_small(dcw, dnw_pre, dcb, dnw_ssm, dnw_post, dvec[0:1, :NH], dvec[1:2, :NH], dvec[2:3, :NH],
                        loss_part[:, :1])
    return grad_x, small, dw_all, dw_out
```

```python
import functools

import jax
import jax.numpy as jnp
from jax import lax
from jax.experimental import pallas as pl
from jax.experimental.pallas import tpu as pltpu

F32 = jnp.float32
BF16 = jnp.bfloat16
MESH = pl.DeviceIdType.MESH
SDS = jax.ShapeDtypeStruct
ANY = pl.BlockSpec(memory_space=pl.ANY)

S = 4096
D = 1024
DP = 7168
SHARD = 1668
OFF_G, OFF_Z = 3072, 4096
NH = 16
CH = 128
NC = S // CH
EPS = 1e-6
NEG = -1e30
LANE = 128
VMEM_LIMIT = 48 * 1024 * 1024

ADAM_LR, ADAM_B1, ADAM_B2, ADAM_EPS, ADAM_WD, ADAM_STEP = 0.001, 0.9, 0.999, 1e-08, 0.01, 10


def _cp(sem, **kw):
    return pltpu.CompilerParams(dimension_semantics=sem, vmem_limit_bytes=VMEM_LIMIT, **kw)


def _dot(a, b):
    return jnp.dot(a, b, preferred_element_type=F32)


def _dot_nt(a, b):
    return lax.dot_general(a, b, (((1,), (1,)), ((), ())), preferred_element_type=F32)


def _dot_tn(a, b):
    return lax.dot_general(a, b, (((0,), (0,)), ((), ())), preferred_element_type=F32)


def _pieces(x, n):
    out = []
    for _ in range(n):
        p = x.astype(BF16)
        out.append(p)
        x = x - p.astype(F32)
    return out


def _pick(x, sel, n=2):
    parts = [_dot(p, sel) for p in _pieces(x, n)]
    return functools.reduce(jnp.add, parts)


def _pick_left(sel, x, n=3):
    parts = [_dot(sel, p) for p in _pieces(x, n)]
    return functools.reduce(jnp.add, parts)


def _sigmoid(v):
    return 0.5 * jnp.tanh(0.5 * v) + 0.5


def _iota(shape, dim):
    return lax.broadcasted_iota(jnp.int32, shape, dim)


def _inproj_fwd(x, nw, w_all, hosted=None):
    tm, tn = 1024, 1024
    n_host = len(hosted.arrays) if hosted else 0

    def body(x_ref, nw_ref, w_ref, *refs):
        host_in, (proj_ref, u_ref), refs = refs[:n_host], refs[n_host:n_host + 2], refs[n_host + 2:]
        host_out, host_sems = refs[:n_host], refs[n_host:]
        i, j = pl.program_id(0), pl.program_id(1)
        if hosted:
            pl.when((i == 0) & (j == 0))(lambda: hosted.start(host_in, host_out, host_sems))

        @pl.when(j == 0)
        def _():
            xf = x_ref[...]
            r = lax.rsqrt(jnp.mean(xf * xf, axis=-1, keepdims=True) + EPS)
            u_ref[...] = (xf * r * nw_ref[...]).astype(BF16)

        proj_ref[...] = _dot(u_ref[...], w_ref[...])
        if hosted:
            pl.when((i == S // tm - 1) & (j == DP // tn - 1))(lambda: hosted.finish(host_in, host_out, host_sems))

    outs = pl.pallas_call(
        body, name="inproj_fwd", grid=(S // tm, DP // tn),
        in_specs=[pl.BlockSpec((tm, D), lambda i, j: (i, 0)), pl.BlockSpec((1, D), lambda i, j: (0, 0)),
                  pl.BlockSpec((D, tn), lambda i, j: (0, j))] + [ANY] * n_host,
        out_specs=[pl.BlockSpec((tm, tn), lambda i, j: (i, j)), pl.BlockSpec((tm, D), lambda i, j: (i, 0))]
        + [ANY] * n_host,
        out_shape=[SDS((S, DP), F32), SDS((S, D), BF16)] + (hosted.out_shape if hosted else []),
        scratch_shapes=hosted.scratch if hosted else [],
        compiler_params=_cp(("arbitrary", "arbitrary") if hosted else ("parallel", "arbitrary")),
    )(x, nw, w_all, *(hosted.arrays if hosted else []))
    return (outs[:2], outs[2:]) if hosted else outs


ATTN_QB = {1: 16, 4: 4, 16: 1}


def _unit_rows(r, u, d):
    return pl.ds(r + d * CH * u, CH, stride=d) if d > 1 else pl.ds(CH * u, CH)


def _for_units(d, qb, fn):
    for r in range(d):
        for u in range(qb):
            fn(r, u)


def _attn_mask(has_prev):
    qi, kj = _iota((2 * CH, 2 * CH), 0) & (CH - 1), _iota((2 * CH, 2 * CH), 1)
    cur_ok = (kj >= CH) & (kj - CH <= qi)
    prev_ok = (kj < CH) & (kj >= qi)
    return cur_ok | (prev_ok & has_prev)


def _stack_heads(v, lane_a):
    return jnp.concatenate([jnp.where(lane_a, v, 0.0), jnp.where(lane_a, 0.0, v)], axis=0).astype(BF16)


def _attn_specs(d, qb):
    rows, prows = CH * d * qb, CH * d
    nb = S // rows
    steps = (NH // 2) * nb

    def at(t):
        t = jnp.minimum(t, steps - 1)
        return t % nb, t // nb

    def cur(off):
        return pl.BlockSpec((rows, LANE), lambda t: (at(t)[0], off + at(t)[1]))

    def prev(off):
        return pl.BlockSpec((prows, LANE), lambda t: (jnp.maximum(at(t)[0] * qb - 1, 0), off + at(t)[1]))

    lag = pl.BlockSpec((rows, LANE), lambda t: at(jnp.maximum(t - 1, 0)))
    return nb, steps, cur, prev, lag


def _gather16(src_ref, dense_ref, tmp_ref):
    for a in range(4):
        tmp_ref[...] = src_ref[pl.ds(a, 4 * CH, stride=4), :]
        for b in range(4):
            dense_ref[a + 4 * b] = tmp_ref[pl.ds(b, CH, stride=4), :]


def _scatter16(dense_ref, dst_ref, tmp_ref):
    for a in range(4):
        for b in range(4):
            tmp_ref[pl.ds(b, CH, stride=4), :] = dense_ref[a + 4 * b]
        dst_ref[pl.ds(a, 4 * CH, stride=4), :] = tmp_ref[...]


def _unit_index(r, u, d):
    return (r,) if d == 16 else (_unit_rows(r, u, d), slice(None))


def _unit_kv(p_ref, c_ref, r, u, d):
    prev = p_ref[_unit_index(r, 0, d)] if u == 0 else c_ref[_unit_index(r, u - 1, d)]
    return jnp.concatenate([prev, c_ref[_unit_index(r, u, d)]], axis=0).astype(BF16)


def _dense_scratch(d, n):
    return [pltpu.VMEM((16, CH, LANE), F32)] * n + [pltpu.VMEM((4 * CH, LANE), F32)] if d == 16 else []


def _attn_fwd(proj, d, prior=None, final=False):
    qb = ATTN_QB[d]
    nb, steps, cur, prev, _ = _attn_specs(d, qb)
    n_prior = 2 if prior is not None else 0
    n_in, n_out = 5 + n_prior + final, 2 + final
    assert not (d == 16 and (n_prior or final))

    def body(*refs):
        ins, outs, scratch = refs[:n_in], refs[n_in:n_in + n_out], refs[n_in + n_out:]
        if d == 16:
            tmp_ref = scratch[-1]
            for src, dense in zip(ins, scratch):
                _gather16(src, dense, tmp_ref)
            block_outs, ins, outs = outs, scratch[:n_in], scratch[n_in:n_in + n_out]
        q_ref, kp_ref, kc_ref, vp_ref, vc_ref = ins[:5]
        prior_refs = ins[5:5 + n_prior]
        if final:
            g_ref, (mix_ref, o_ref, l_ref) = ins[-1], outs
        else:
            o_ref, l_ref = outs
        i = pl.program_id(0) % nb
        lane_a = _iota((CH, LANE), 1) < 64
        mask_first, mask_rest = _attn_mask(i > 0), _attn_mask(True)

        def unit(r, u):
            at = _unit_index(r, u, d)
            q2 = _stack_heads(q_ref[at] * 0.125, lane_a)
            k2, v2 = _unit_kv(kp_ref, kc_ref, r, u, d), _unit_kv(vp_ref, vc_ref, r, u, d)
            s = jnp.where(mask_first if u == 0 else mask_rest, _dot_nt(q2, k2), NEG)
            m = jnp.max(s, axis=1, keepdims=True)
            p = jnp.exp(s - m)
            l = jnp.sum(p, axis=1, keepdims=True)
            o2 = _dot(p.astype(BF16), v2) / l
            lse2 = m + jnp.log(l)
            o = jnp.where(lane_a, o2[:CH], o2[CH:])
            lse = jnp.where(lane_a, lse2[:CH], lse2[CH:])
            if n_prior:
                o_a, l_a = prior_refs[0][at], prior_refs[1][at]
                top = jnp.maximum(l_a, lse)
                e_a, e_b = jnp.exp(l_a - top), jnp.exp(lse - top)
                tot = e_a + e_b
                o = (e_a * o_a + e_b * o) / tot
                lse = top + jnp.log(tot)
            o_ref[at] = o
            l_ref[at] = lse
            if final:
                g = g_ref[at]
                mix_ref[at] = (o * (g * _sigmoid(g))).astype(BF16)

        _for_units(d, qb, unit)
        if d == 16:
            for dense, dst in zip(outs, block_outs):
                _scatter16(dense, dst, tmp_ref)

    in_specs = [cur(0), prev(8), cur(8), prev(16), cur(16)] + [cur(0)] * n_prior
    args = [proj] * 5 + (list(prior) if n_prior else [])
    out_specs, out_shape = [cur(0), cur(0)], [SDS((S, D), F32), SDS((S, D), F32)]
    if final:
        assert d == 1
        in_specs.append(cur(OFF_G // LANE))
        args.append(proj)
        out_specs, out_shape = [cur(0)] + out_specs, [SDS((S, 2 * D), BF16)] + out_shape
    return pl.pallas_call(
        body, name=f"attn_fwd_d{d}", grid=(steps,),
        in_specs=in_specs, out_specs=out_specs, out_shape=out_shape,
        scratch_shapes=_dense_scratch(d, n_in + n_out),
        compiler_params=_cp(("parallel",)),
    )(*args)


def _attn_bwd(proj, do, lse, delta, d, acc, out_dtype, hosted=None):
    qb = ATTN_QB[d]
    nb, steps, cur, prev, lag = _attn_specs(d, qb)
    has_acc = acc is not None
    n_in = 11 if has_acc else 8
    n_host = len(hosted.arrays) if hosted else 0
    assert not (d == 16 and (has_acc or out_dtype != F32))
    rows = CH * d * qb
    carry = (2, 16, CH, LANE) if d == 16 else (2, rows, LANE)

    def body(*refs):
        ins, host_in, refs = refs[:n_in], refs[n_in:n_in + n_host], refs[n_in + n_host:]
        (dq_ref, dk_ref, dv_ref), host_out, scratch = refs[:3], refs[3:3 + n_host], refs[3 + n_host:]
        if hosted:
            scratch, host_sems = scratch[:-len(hosted.scratch)], scratch[-len(hosted.scratch):]
        ck_ref, cv_ref = scratch[:2]
        dq_f32 = dq_ref if out_dtype == F32 else scratch[2]
        t = pl.program_id(0)
        i = t % nb
        if hosted:
            pl.when(t == 0)(lambda: hosted.start(host_in, host_out, host_sems))
        if d == 16:
            dense, dq_f32, tmp_ref = scratch[2:2 + n_in], scratch[2 + n_in], scratch[-1]

            @pl.when(t < steps)
            def _():
                for src, dst in zip(ins, dense):
                    _gather16(src, dst, tmp_ref)

            ins = dense
        q_ref, kp_ref, kc_ref, vp_ref, vc_ref, do_ref, lse_ref, dl_ref = ins[:8]
        if has_acc:
            aq_ref, ak_ref, av_ref = ins[8:11]
        slot = t & 1
        now_k, now_v, old_k, old_v = ck_ref.at[slot], cv_ref.at[slot], ck_ref.at[1 - slot], cv_ref.at[1 - slot]
        lane_a = _iota((CH, LANE), 1) < 64
        mask_first, mask_rest = _attn_mask(i > 0), _attn_mask(True)

        @pl.when(t == 0)
        def _():
            ck_ref[1] = jnp.zeros(carry[1:], F32)
            cv_ref[1] = jnp.zeros(carry[1:], F32)

        def unit(r, u):
            at = _unit_index(r, u, d)
            q2 = _stack_heads(q_ref[at] * 0.125, lane_a)
            do2 = _stack_heads(do_ref[at], lane_a)
            k2, v2 = _unit_kv(kp_ref, kc_ref, r, u, d), _unit_kv(vp_ref, vc_ref, r, u, d)
            lsev, dlv = lse_ref[at], dl_ref[at]
            lse2 = jnp.concatenate([lsev[:, 0:1], lsev[:, 64:65]], axis=0)
            dl2 = jnp.concatenate([dlv[:, 0:1], dlv[:, 64:65]], axis=0)
            p = jnp.exp(jnp.where(mask_first if u == 0 else mask_rest, _dot_nt(q2, k2), NEG) - lse2)
            ds = (p * (_dot_nt(do2, v2) - dl2)).astype(BF16)
            dq2 = _dot(ds, k2)
            dk2 = _dot_tn(ds, q2)
            dv2 = _dot_tn(p.astype(BF16), do2)
            dq = jnp.where(lane_a, dq2[:CH], dq2[CH:]) * 0.125
            if has_acc:
                dq = dq + aq_ref[at]
            dq_f32[at] = dq
            if u == 0:
                before = _unit_index(r, qb - 1, d)
                old_k[before] += dk2[:CH]
                old_v[before] += dv2[:CH]
            else:
                before = _unit_index(r, u - 1, d)
                now_k[before] += dk2[:CH]
                now_v[before] += dv2[:CH]
            now_k[at] = dk2[CH:]
            now_v[at] = dv2[CH:]

        @pl.when(t < steps)
        def _():
            _for_units(d, qb, unit)
            if d == 16:
                _scatter16(dq_f32, dq_ref, tmp_ref)
            elif out_dtype != F32:
                dq_ref[...] = dq_f32[...].astype(out_dtype)

        if d == 16:
            _scatter16(old_k, dk_ref, tmp_ref)
            _scatter16(old_v, dv_ref, tmp_ref)
        else:
            dk, dv = old_k[...], old_v[...]
            if has_acc:
                dk, dv = dk + ak_ref[...], dv + av_ref[...]
            dk_ref[...] = dk.astype(out_dtype)
            dv_ref[...] = dv.astype(out_dtype)
        if hosted:
            pl.when(t == steps)(lambda: hosted.finish(host_in, host_out, host_sems))

    in_specs = [cur(0), prev(8), cur(8), prev(16), cur(16), cur(0), cur(0), cur(0)]
    args = [proj, proj, proj, proj, proj, do, lse, delta]
    if has_acc:
        in_specs += [cur(0), lag, lag]
        args += list(acc)
    scratch = [pltpu.VMEM(carry, F32), pltpu.VMEM(carry, F32)]
    if d == 16:
        scratch += _dense_scratch(d, n_in + 1)
    elif out_dtype != F32:
        scratch.append(pltpu.VMEM((rows, LANE), F32))
    out_specs, out_shape = [cur(0), lag, lag], [SDS((S, D), out_dtype)] * 3
    if hosted:
        args += hosted.arrays
        in_specs += [ANY] * n_host
        out_specs += [ANY] * n_host
        out_shape += hosted.out_shape
        scratch += hosted.scratch
    outs = pl.pallas_call(
        body, name=f"attn_bwd_d{d}", grid=(steps + 1,),
        in_specs=in_specs, out_specs=out_specs, out_shape=out_shape,
        scratch_shapes=scratch, compiler_params=_cp(("arbitrary",)),
    )(*args)
    return (outs[:3], outs[3:]) if hosted else outs


def _conv_taps(cur, prev8, first):
    row8 = _iota(prev8.shape, 0)
    prev8 = jnp.where(first, 0.0, prev8)
    taps = []
    for s in (3, 2, 1):
        rolled = pltpu.roll(cur, s, 0)
        head = jnp.where(row8 < s, pltpu.roll(prev8, s, 0), rolled[:8])
        taps.append(jnp.concatenate([head, rolled[8:]], axis=0))
    return taps + [cur]


def _conv(taps, w, b):
    acc = b + w[0:1, :] * taps[0]
    for k in (1, 2, 3):
        acc = acc + w[k:k + 1, :] * taps[k]
    return acc


def _expand():
    return (_iota((LANE, D), 1) // 64 == _iota((LANE, D), 0)).astype(BF16)


def _reduce():
    return (_iota((D, LANE), 0) // 64 == _iota((D, LANE), 1)).astype(BF16)


def _ssd_common(xs_c, bc_c, dt_raw, dtb, alog):
    head_lane = _iota((CH, LANE), 1) < NH
    xs = xs_c * _sigmoid(xs_c)
    bc = bc_c * _sigmoid(bc_c)
    pre = dt_raw + dtb
    dt = jnp.where(head_lane, jnp.maximum(pre, 0.0) + jnp.log(1.0 + jnp.exp(-jnp.abs(pre))), 0.0)
    a_row = jnp.where(head_lane[0:1], -jnp.exp(alog), 0.0)
    tri = (_iota((CH, CH), 1) <= _iota((CH, CH), 0)).astype(BF16)
    cs = _pick_left(tri, dt * a_row)
    cs_last = cs[CH - 1:CH, :]
    wide = _pick(jnp.concatenate([dt, jnp.exp(cs), jnp.exp(cs_last - cs)], axis=0), _expand())
    dt_b, e_b, f_b = wide[:CH], wide[CH:2 * CH], wide[2 * CH:]
    return dict(xs=xs, bc=bc, pre=pre, dt=dt, a_row=a_row, cs=cs, cs_t=cs.T, dt_b=dt_b, e_b=e_b, f_b=f_b,
                t_b=e_b[CH - 1:CH, :])


def _groups(bc):
    bcb = bc.astype(BF16)
    return [bcb[:, 0:128], bcb[:, 128:256]], [bcb[:, 256:384], bcb[:, 384:512]]


def _decay(q, h, tril):
    seg = q["cs"][:, h:h + 1] - q["cs_t"][h:h + 1, :]
    return jnp.exp(jnp.where(tril, seg, NEG))


def _ssm_fwd(proj, mix, cw, cb, dtb, alog, d_b, nw):
    def body(xs_ref, xsp_ref, bc_ref, bcp_ref, dt_ref, z_ref, cw_ref, cb_ref, dtb_ref, alog_ref, db_ref, nw_ref,
             mix_in_ref, mix_ref, y_ref, st_ref, conv_ref, h_ref):
        del mix_in_ref
        i = pl.program_id(0)

        @pl.when(i == 0)
        def _():
            h_ref[...] = jnp.zeros_like(h_ref)

        cw, cb = cw_ref[...], cb_ref[...]
        xs_c = _conv(_conv_taps(xs_ref[...], xsp_ref[...], i == 0), cw[:, :D], cb[:, :D])
        bc_c = _conv(_conv_taps(bc_ref[...], bcp_ref[...], i == 0), cw[:, D:], cb[:, D:])
        conv_ref[:, :D] = xs_c
        conv_ref[:, D:] = bc_c
        q = _ssd_common(xs_c, bc_c, dt_ref[...], dtb_ref[...], alog_ref[...])
        bg, cg = _groups(q["bc"])
        xs = q["xs"]
        xdt = xs * q["dt_b"]
        xdt_b = xdt.astype(BF16)
        h_in = h_ref[...]
        st_ref[...] = h_in
        hb = h_in.astype(BF16)
        tril = _iota((CH, CH), 1) <= _iota((CH, CH), 0)
        lane_a = _iota((CH, LANE), 1) < 64
        cbm = [_dot_nt(cg[g], bg[g]) for g in range(2)]
        pairs = []
        for hp in range(NH // 2):
            xp = xdt_b[:, hp * LANE:(hp + 1) * LANE]
            ya = _dot((cbm[hp // 4] * _decay(q, 2 * hp, tril)).astype(BF16), xp)
            yb = _dot((cbm[hp // 4] * _decay(q, 2 * hp + 1, tril)).astype(BF16), xp)
            pairs.append(jnp.where(lane_a, ya, yb))
        y_diag = jnp.concatenate(pairs, axis=1)
        y_off = jnp.concatenate([_dot(cg[g], hb[:, g * 512:(g + 1) * 512]) for g in range(2)], axis=1) * q["e_b"]
        y = y_diag + y_off + db_ref[...] * xs
        y_ref[...] = y
        xf = (xdt * q["f_b"]).astype(BF16)
        h_ref[...] = q["t_b"] * h_in + jnp.concatenate(
            [_dot_tn(bg[g], xf[:, g * 512:(g + 1) * 512]) for g in range(2)], axis=1)
        z = z_ref[...]
        yz = y * (z * _sigmoid(z))
        outs = []
        for g in range(2):
            v = yz[:, g * 512:(g + 1) * 512]
            outs.append(v * lax.rsqrt(jnp.mean(v * v, axis=-1, keepdims=True) + EPS))
        mix_ref[...] = (jnp.concatenate(outs, axis=1) * nw_ref[...]).astype(BF16)

    def col(width, blk, prev=False):
        if prev:
            return pl.BlockSpec((8, width), lambda i: (jnp.maximum(i * (CH // 8) - 1, 0), blk))
        return pl.BlockSpec((CH, width), lambda i: (i, blk))

    def full(a):
        return pl.BlockSpec(a.shape, lambda i: (0,) * a.ndim)

    return pl.pallas_call(
        body, name="ssm_fwd", grid=(NC,),
        in_specs=[col(D, 5), col(D, 5, True), col(512, 12), col(512, 12, True), col(LANE, 52), col(D, 4),
                  full(cw), full(cb), full(dtb), full(alog), full(d_b), full(nw), ANY],
        out_specs=[col(D, 1), col(D, 0), pl.BlockSpec((None, CH, D), lambda i: (i, 0, 0)), col(D + 512, 0)],
        out_shape=[SDS((S, 2 * D), BF16), SDS((S, D), F32), SDS((NC, CH, D), F32), SDS((S, D + 512), F32)],
        scratch_shapes=[pltpu.VMEM((CH, D), F32)],
        input_output_aliases={12: 0},
        compiler_params=_cp(("arbitrary",)),
    )(proj, proj, proj, proj, proj, proj, cw, cb, dtb, alog, d_b, nw, mix)


def _ssm_bwd(proj, dn, y_save, states, conv_out, cw, dtb, alog, d_b, nw):
    def body(xs_ref, bc_ref, dt_ref, z_ref, dn_ref, y_ref, st_ref, conv_ref,
             cw_ref, dtb_ref, alog_ref, db_ref, nw_ref,
             dz_ref, dx_ref, dcw_ref, dcb_ref, dsm_ref, dnw_ref, dh_ref, nxs_ref, nbc_ref):
        i = pl.program_id(0)
        ci = NC - 1 - i

        @pl.when(i == 0)
        def _():
            for ref in (dcw_ref, dcb_ref, dsm_ref, dnw_ref, dh_ref, nxs_ref, nbc_ref):
                ref[...] = jnp.zeros_like(ref)

        cw = cw_ref[...]
        xs_c, bc_c = conv_ref[:, :D], conv_ref[:, D:]
        q = _ssd_common(xs_c, bc_c, dt_ref[...], dtb_ref[...], alog_ref[...])
        bg, cg = _groups(q["bc"])
        xs, dt_b, e_b, f_b, t_b = q["xs"], q["dt_b"], q["e_b"], q["f_b"], q["t_b"]
        xdt = xs * dt_b
        xdt_b = xdt.astype(BF16)
        h_in = st_ref[...]
        hb = h_in.astype(BF16)
        dh_new = dh_ref[...]
        dhb = dh_new.astype(BF16)
        red = _reduce()

        z, y, dn, nw_v = z_ref[...], y_ref[...], dn_ref[...], nw_ref[...]
        sig = _sigmoid(z)
        sz = z * sig
        yz = y * sz
        gdn = dn * nw_v
        dyz, dnw = [], []
        for g in range(2):
            v, gv = yz[:, g * 512:(g + 1) * 512], gdn[:, g * 512:(g + 1) * 512]
            r = lax.rsqrt(jnp.mean(v * v, axis=-1, keepdims=True) + EPS)
            dnw.append(dn[:, g * 512:(g + 1) * 512] * v * r)
            dyz.append(r * (gv - v * (r * r) * jnp.mean(gv * v, axis=-1, keepdims=True)))
        dyz = jnp.concatenate(dyz, axis=1)
        dnw_ref[...] += jnp.sum(jnp.concatenate(dnw, axis=1), axis=0, keepdims=True)
        dy = dyz * sz
        dz_ref[...] = (dyz * y * (sig * (1.0 + z * (1.0 - sig)))).astype(BF16)
        dy_b = dy.astype(BF16)

        tril = _iota((CH, CH), 1) <= _iota((CH, CH), 0)
        lane_a = _iota((CH, LANE), 1) < 64
        cbm = [_dot_nt(cg[g], bg[g]) for g in range(2)]
        dcbm = [jnp.zeros((CH, CH), F32), jnp.zeros((CH, CH), F32)]
        seg_rows = jnp.zeros((CH, LANE), F32)
        seg_cols = jnp.zeros((LANE, CH), F32)
        row_id, col_id = _iota((CH, LANE), 0), _iota((CH, LANE), 1)
        dx_pairs = []
        for hp in range(NH // 2):
            g = hp // 4
            xp = xdt_b[:, hp * LANE:(hp + 1) * LANE]
            dyp_f = dy[:, hp * LANE:(hp + 1) * LANE]
            dyp = dy_b[:, hp * LANE:(hp + 1) * LANE]
            halves = []
            for k in range(2):
                h = 2 * hp + k
                lane = lane_a if k == 0 else jnp.logical_not(lane_a)
                dec = _decay(q, h, tril)
                gm = cbm[g] * dec
                dgm = _dot_nt(jnp.where(lane, dyp_f, 0.0).astype(BF16), xp)
                dcbm[g] = dcbm[g] + dgm * dec
                prod = dgm * gm
                seg_rows = jnp.where(col_id == h, jnp.sum(prod, axis=1, keepdims=True), seg_rows)
                seg_cols = jnp.where(row_id == h, jnp.sum(prod, axis=0, keepdims=True), seg_cols)
                halves.append(_dot_tn(gm.astype(BF16), dyp))
            dx_pairs.append(jnp.where(lane_a, halves[0], halves[1]))
        dxdt_diag = jnp.concatenate(dx_pairs, axis=1)

        qv = jnp.concatenate([_dot(bg[g], dhb[:, g * 512:(g + 1) * 512]) for g in range(2)], axis=1)
        y_off = jnp.concatenate([_dot(cg[g], hb[:, g * 512:(g + 1) * 512]) for g in range(2)], axis=1) * e_b
        xfq = xdt * f_b * qv
        dxdt = dxdt_diag + f_b * qv
        tdt = jnp.sum(dh_new * h_in, axis=0, keepdims=True) * t_b
        per_head = _pick(jnp.concatenate([xfq, dy * y_off, dxdt * xs, dy * xs, jnp.broadcast_to(tdt, (8, D))],
                                         axis=0), red)
        fdf, dyoff_h, dxdtxs_h, dyxs_h = [per_head[k * CH:(k + 1) * CH] for k in range(4)]
        dcs = seg_rows - seg_cols.T + dyoff_h - fdf
        last = per_head[4 * CH:4 * CH + 1] + jnp.sum(fdf, axis=0, keepdims=True)
        dcs = dcs + jnp.where(_iota((CH, LANE), 0) == CH - 1, last, 0.0)
        tri_t = (_iota((CH, CH), 1) >= _iota((CH, CH), 0)).astype(BF16)
        da = _pick_left(tri_t, dcs)
        ddt = da * q["a_row"] + dxdtxs_h
        dxs = dxdt * dt_b + db_ref[...] * dy
        ddt_raw = ddt * _sigmoid(q["pre"])
        dsm_ref[0:1, :] += jnp.sum(ddt_raw, axis=0, keepdims=True)
        dsm_ref[1:2, :] += jnp.sum(da * q["dt"], axis=0, keepdims=True) * q["a_row"]
        dsm_ref[2:3, :] += jnp.sum(dyxs_h, axis=0, keepdims=True)
        edy = (e_b * dy).astype(BF16)
        xf = (xdt * f_b).astype(BF16)
        dbs, dcs_g, dhs = [], [], []
        for g in range(2):
            sl = slice(g * 512, (g + 1) * 512)
            dcb_b = dcbm[g].astype(BF16)
            dcs_g.append(_dot(dcb_b, bg[g]) + _dot_nt(edy[:, sl], hb[:, sl]))
            dbs.append(_dot_tn(dcb_b, cg[g]) + _dot_nt(xf[:, sl], dhb[:, sl]))
            dhs.append(_dot_tn(cg[g], edy[:, sl]))
        dh_ref[...] = t_b * dh_new + jnp.concatenate(dhs, axis=1)
        dbc = jnp.concatenate(dbs + dcs_g, axis=1)

        def conv_bwd(dact, pre, x_raw, w, nxt_ref, lo):
            s = _sigmoid(pre)
            dconv = dact * (s * (1.0 + pre * (1.0 - s)))
            nxt8 = nxt_ref[...]
            row8 = _iota(nxt8.shape, 0)
            hi = lo + dconv.shape[1]
            dcb_ref[:, lo:hi] += jnp.sum(dconv, axis=0, keepdims=True)
            later = [dconv]
            for s_ in (1, 2, 3):
                rolled = pltpu.roll(dconv, CH - s_, 0)
                tail = jnp.where(row8 >= 8 - s_, pltpu.roll(nxt8, 8 - s_, 0), rolled[CH - 8:])
                later.append(jnp.concatenate([rolled[:CH - 8], tail], axis=0))
            dx = None
            for s_, up in enumerate(later):
                k = 3 - s_
                dcw_ref[k:k + 1, lo:hi] += jnp.sum(up * x_raw, axis=0, keepdims=True)
                dx = w[k:k + 1, :] * up if dx is None else dx + w[k:k + 1, :] * up
            nxt_ref[...] = dconv[:8]
            return dx

        dx_ref[:, 0:D] = conv_bwd(dxs, xs_c, xs_ref[...], cw[:, :D], nxs_ref, 0).astype(BF16)
        dx_ref[:, D:D + 512] = conv_bwd(dbc, bc_c, bc_ref[...], cw[:, D:], nbc_ref, D).astype(BF16)
        dx_ref[:, D + 512:D + 640] = ddt_raw.astype(BF16)
        dx_ref[:, D + 640:] = jnp.zeros((CH, D - 640), BF16)

    def col(width, blk):
        return pl.BlockSpec((CH, width), lambda i: (NC - 1 - i, blk))

    def full(a):
        return pl.BlockSpec(a.shape, lambda i: (0,) * len(a.shape))

    acc_shapes = [SDS((4, 1536), F32), SDS((1, 1536), F32), SDS((8, LANE), F32), SDS((1, D), F32)]
    return pl.pallas_call(
        body, name="ssm_bwd", grid=(NC,),
        in_specs=[col(D, 5), col(512, 12), col(LANE, 52), col(D, 4),
                  col(D, 0), col(D, 0), pl.BlockSpec((None, CH, D), lambda i: (NC - 1 - i, 0, 0)), col(D + 512, 0),
                  full(cw), full(dtb), full(alog), full(d_b), full(nw)],
        out_specs=[col(D, 0), col(2 * D, 0)] + [full(a) for a in acc_shapes],
        out_shape=[SDS((S, D), BF16), SDS((S, 2 * D), BF16)] + acc_shapes,
        scratch_shapes=[pltpu.VMEM((CH, D), F32), pltpu.VMEM((8, D), F32), pltpu.VMEM((8, 512), F32)],
        compiler_params=_cp(("arbitrary",)),
    )(proj, proj, proj, proj, dn, y_save, states, conv_out, cw, dtb, alog, d_b, nw)


def _outproj_loss(mix, w_out, x, tgt, nw, attn_pre, proj):
    tm = 256

    def body(mix_ref, w_ref, x_ref, t_ref, nw_ref, pre_ref, g_ref,
             dy_ref, dn_ref, do_ref, delta_ref, dg_ref, dw_ref, dnw_ref, loss_ref):
        @pl.when(pl.program_id(0) == 0)
        def _():
            dw_ref[...] = jnp.zeros_like(dw_ref)
            dnw_ref[...] = jnp.zeros_like(dnw_ref)
            loss_ref[...] = jnp.zeros_like(loss_ref)

        mixv, w = mix_ref[...], w_ref[...]
        out = _dot(mixv, w)
        r = lax.rsqrt(jnp.mean(out * out, axis=-1, keepdims=True) + EPS)
        nh = out * r
        nw_v = nw_ref[...]
        err = x_ref[...] + nh * nw_v - t_ref[...]
        loss_ref[...] += 0.5 * jnp.sum(jnp.mean(err * err, axis=-1, keepdims=True), axis=0, keepdims=True)
        dy = err * (1.0 / D)
        dy_ref[...] = dy
        dnw_ref[...] += jnp.sum(dy * nh, axis=0, keepdims=True)
        gdn = dy * nw_v
        dout = (r * (gdn - nh * jnp.mean(gdn * nh, axis=-1, keepdims=True))).astype(BF16)
        dmix = _dot_nt(dout, w)
        dw_ref[...] += _dot_tn(mixv, dout)
        dn_ref[...] = dmix[:, D:]
        dm, g, pre_v = dmix[:, :D], g_ref[...], pre_ref[...]
        sig = _sigmoid(g)
        do = dm * (g * sig)
        do_ref[...] = do
        dg_ref[...] = (dm * pre_v * (sig * (1.0 + g * (1.0 - sig)))).astype(BF16)
        prod = do * pre_v
        same_head = (_iota((LANE, LANE), 0) // 64 == _iota((LANE, LANE), 1) // 64).astype(BF16)
        for cb in range(D // LANE):
            delta_ref[:, cb * LANE:(cb + 1) * LANE] = _pick(prod[:, cb * LANE:(cb + 1) * LANE], same_head)

    row = lambda w: pl.BlockSpec((tm, w), lambda i: (i, 0))
    full = lambda s: pl.BlockSpec(s, lambda i: (0, 0))
    return pl.pallas_call(
        body, name="outproj_loss", grid=(S // tm,),
        in_specs=[row(2 * D), full((2 * D, D)), row(D), row(D), full((1, D)), row(D),
                  pl.BlockSpec((tm, D), lambda i: (i, OFF_G // D))],
        out_specs=[row(D), row(D), row(D), row(D), row(D), full((2 * D, D)), full((1, D)), full((1, LANE))],
        out_shape=[SDS((S, D), F32)] * 4 + [SDS((S, D), BF16), SDS((2 * D, D), F32), SDS((1, D), F32),
                                            SDS((1, LANE), F32)],
        compiler_params=_cp(("arbitrary",)),
    )(mix, w_out, x, tgt, nw, attn_pre, proj)


def _inproj_bwd_dx(srcs, dxbcdt, w_all, x, dy, nw, hosted=None):
    tm = 512
    nk = DP // D
    n_host = len(hosted.arrays) if hosted else 0

    def body(*refs):
        src_refs = refs[:nk]
        w_ref, x_ref, dy_ref, nw_ref = refs[nk:nk + 4]
        host_in, refs = refs[nk + 4:nk + 4 + n_host], refs[nk + 4 + n_host:]
        gx_ref, dnw_ref = refs[:2]
        host_out, host_sems = refs[2:2 + n_host], refs[2 + n_host:]
        i = pl.program_id(0)

        @pl.when(i == 0)
        def _():
            if hosted:
                hosted.start(host_in, host_out, host_sems)
            dnw_ref[...] = jnp.zeros_like(dnw_ref)

        du = None
        for k, ref in enumerate(src_refs):
            part = _dot_nt(ref[...], w_ref[:, k * D:(k + 1) * D])
            du = part if du is None else du + part
        xf, nw_v = x_ref[...], nw_ref[...]
        r = lax.rsqrt(jnp.mean(xf * xf, axis=-1, keepdims=True) + EPS)
        xh = xf * r
        dnw_ref[...] += jnp.sum(du * xh, axis=0, keepdims=True)
        gdu = du * nw_v
        gx_ref[...] = r * (gdu - xh * jnp.mean(gdu * xh, axis=-1, keepdims=True)) + dy_ref[...]

        if hosted:
            pl.when(i == S // tm - 1)(lambda: hosted.finish(host_in, host_out, host_sems))

    row = pl.BlockSpec((tm, D), lambda i: (i, 0))
    row1 = pl.BlockSpec((tm, D), lambda i: (i, 1))
    one = pl.BlockSpec((1, D), lambda i: (0, 0))
    whole_w = pl.BlockSpec((D, DP), lambda i: (0, 0), pipeline_mode=pl.Buffered(1))
    args = [*srcs, dxbcdt, dxbcdt, w_all, x, dy, nw]
    in_specs = [row] * len(srcs) + [row, row1, whole_w, row, row, one]
    out_specs, out_shape, scratch = [row, one], [SDS((S, D), F32), SDS((1, D), F32)], []
    if hosted:
        args += hosted.arrays
        in_specs += [ANY] * n_host
        out_specs += [ANY] * n_host
        out_shape += hosted.out_shape
        scratch += hosted.scratch
    outs = pl.pallas_call(
        body, name="inproj_bwd_dx", grid=(S // tm,),
        in_specs=in_specs, out_specs=out_specs, out_shape=out_shape, scratch_shapes=scratch,
        compiler_params=_cp(("arbitrary",)),
    )(*args)
    return (outs[:2], outs[2:]) if hosted else outs


def _dw(u, dsec, name):
    ts = 1024
    ncol = dsec.shape[1] // D

    def body(u_ref, d_ref, o_ref):
        @pl.when(pl.program_id(1) == 0)
        def _():
            o_ref[...] = jnp.zeros_like(o_ref)

        o_ref[...] += _dot_tn(u_ref[...], d_ref[...])

    return pl.pallas_call(
        body, name=name, grid=(ncol, S // ts),
        in_specs=[pl.BlockSpec((ts, D), lambda j, i: (i, 0)), pl.BlockSpec((ts, D), lambda j, i: (i, j))],
        out_specs=pl.BlockSpec((D, D), lambda j, i: (0, j)),
        out_shape=SDS((D, ncol * D), F32),
        compiler_params=_cp(("parallel", "arbitrary")),
    )(u, dsec)


def _place():
    x, y, c = lax.axis_index("x"), lax.axis_index("y"), lax.axis_index("c")
    return x, y, c, 2 * x + y


def _chip_of(x, y, k):
    px = 1 - x if k & 2 else x
    py = 1 - y if k & 1 else y
    return px, py, 2 * px + py


def _remote(src, dst, send_sem, recv_sem, dev):
    return pltpu.make_async_remote_copy(src_ref=src, dst_ref=dst, send_sem=send_sem, recv_sem=recv_sem,
                                        device_id=dev, device_id_type=MESH)


def _gather_weights(w_in_b):
    half = w_in_b.shape[0] // 2
    quarter = half // 2

    def body(src, dst, send, recv):
        x, y, c, j = _place()
        me, sib = (x, y, c), (x, y, 1 - c)
        nbr = {"x": _chip_of(x, y, 2), "y": _chip_of(x, y, 1)}
        diag = _chip_of(x, y, 3)[2]
        started, arrivals = [], []

        def rows(n_quarter=None, sibling=False):
            base = (1 - c if sibling else c) * half
            return pl.ds(base, half) if n_quarter is None else pl.ds(base + n_quarter * quarter, quarter)

        def sem(n):
            return send.at[n], recv.at[n]

        def go(cp):
            cp.start()
            started.append(cp)

        own = _remote(src, dst.at[j], *sem(8), sib)
        go(own)
        for n, axis in enumerate("xy"):
            px, py, _ = nbr[axis]
            go(_remote(src.at[rows()], dst.at[j, rows()], *sem(n), (px, py, c)))
        for n, axis in enumerate("xy"):
            ox, oy, _ = nbr["y" if axis == "x" else "x"]
            pj = nbr[axis][2]
            _remote(src.at[rows()], dst.at[pj, rows()], *sem(n), me).wait_recv()
            go(_remote(dst.at[pj, rows(n)], dst.at[pj, rows(n)], *sem(2 + n), (ox, oy, c)))
            go(_remote(dst.at[pj, rows()], dst.at[pj, rows()], *sem(4 + n), sib))
            arrivals.append(_remote(src.at[rows()], dst.at[pj, rows(None, True)], *sem(4 + n), me))
        for n in range(2):
            _remote(dst.at[diag, rows(n)], dst.at[diag, rows(n)], *sem(2 + n), me).wait_recv()
            go(_remote(dst.at[diag, rows(n)], dst.at[diag, rows(n)], *sem(6 + n), sib))
            arrivals.append(_remote(dst.at[diag, rows(n, True)], dst.at[diag, rows(n, True)], *sem(6 + n), me))
        for cp in arrivals + [own]:
            cp.wait_recv()
        for cp in started:
            cp.wait_send()

    return pl.pallas_call(
        body, name="gather_weights", in_specs=[ANY], out_specs=ANY,
        out_shape=SDS((4,) + w_in_b.shape, BF16),
        scratch_shapes=[pltpu.SemaphoreType.DMA((9,)), pltpu.SemaphoreType.DMA((9,))],
        compiler_params=pltpu.CompilerParams(has_side_effects=True),
    )(w_in_b)


class _LateGather:
    def __init__(self, w_out_b, conv_w):
        self.arrays = [w_out_b, conv_w]
        self.out_shape = [SDS((4,) + w_out_b.shape, BF16), SDS((4,) + conv_w.shape, F32)]
        self.scratch = [pltpu.SemaphoreType.DMA((11,)), pltpu.SemaphoreType.DMA((11,))]

    def _plan(self, ins, outs, sems):
        x, y, c, j = _place()
        send, recv = sems
        (wo, cw), (gwo, gcw) = ins, outs
        half = wo.shape[0] // 2
        mine, theirs = pl.ds(c * half, half), pl.ds((1 - c) * half, half)
        me, sib = (x, y, c), (x, y, 1 - c)
        first, arrive, forward, last = [], [], [], []
        for k in (1, 2, 3):
            px, py, pj = _chip_of(x, y, k)
            first += [_remote(wo.at[mine], gwo.at[j, mine], send.at[k - 1], recv.at[k - 1], (px, py, c)),
                      _remote(cw, gcw.at[j], send.at[k + 2], recv.at[k + 2], (px, py, c))]
            arrive.append(_remote(wo.at[mine], gwo.at[pj, mine], send.at[k - 1], recv.at[k - 1], me))
            forward.append(_remote(gwo.at[pj, mine], gwo.at[pj, mine], send.at[k + 5], recv.at[k + 5], sib))
            last += [_remote(cw, gcw.at[pj], send.at[k + 2], recv.at[k + 2], me),
                     _remote(wo.at[theirs], gwo.at[pj, theirs], send.at[k + 5], recv.at[k + 5], me)]
        first += [_remote(wo, gwo.at[j], send.at[9], recv.at[9], sib),
                  _remote(cw, gcw.at[j], send.at[10], recv.at[10], sib)]
        last += first[-2:]
        return first, arrive, forward, last

    def start(self, ins, outs, sems):
        for cp in self._plan(ins, outs, sems)[0]:
            cp.start()

    def finish(self, ins, outs, sems):
        first, arrive, forward, last = self._plan(ins, outs, sems)
        for got, fwd in zip(arrive, forward):
            got.wait_recv()
            fwd.start()
        for cp in last:
            cp.wait_recv()
        for cp in first + forward:
            cp.wait_send()


class _PairExchange:
    def __init__(self, arrays):
        self.arrays = list(arrays)
        self.out_shape = [SDS((a.shape[0], a.shape[1] // 2, a.shape[2]), F32) for a in self.arrays]
        self.scratch = [pltpu.SemaphoreType.DMA((len(self.arrays),)) for _ in range(2)]

    def _copies(self, ins, outs, sems):
        x, y, c, _ = _place()
        for k, (src, dst) in enumerate(zip(ins, outs)):
            half = src.shape[1] // 2
            yield _remote(src.at[:, pl.ds((1 - c) * half, half)], dst, sems[0].at[k], sems[1].at[k], (x, y, 1 - c))

    def start(self, ins, outs, sems):
        for cp in self._copies(ins, outs, sems):
            cp.start()

    def finish(self, ins, outs, sems):
        for cp in self._copies(ins, outs, sems):
            cp.wait()


def _pair_exchange(arrays, name):
    halves = [a.shape[1] // 2 for a in arrays]
    n = len(arrays)

    def body(*refs):
        x, y, c, _ = _place()
        send, recv = refs[2 * n:]
        cps = [_remote(refs[k].at[:, pl.ds((1 - c) * halves[k], halves[k])], refs[n + k], send.at[k], recv.at[k],
                       (x, y, 1 - c)) for k in range(n)]
        for cp in cps:
            cp.start()
        for cp in cps:
            cp.wait()

    return pl.pallas_call(
        body, name=name, in_specs=[ANY] * n, out_specs=[ANY] * n,
        out_shape=[SDS((a.shape[0], h, a.shape[2]), F32) for a, h in zip(arrays, halves)],
        scratch_shapes=[pltpu.SemaphoreType.DMA((n,)), pltpu.SemaphoreType.DMA((n,))],
        compiler_params=pltpu.CompilerParams(has_side_effects=True),
    )(*arrays)


def _pair_sum(cidx, g, r, name):
    n, half, width = r.shape
    tr = min(half, 256)
    nt = half // tr

    def body(c_ref, g_ref, r_ref, o_ref):
        del c_ref
        o_ref[...] = (g_ref[...] + r_ref[...]).astype(BF16)

    return pl.pallas_call(
        body, name=name,
        grid_spec=pltpu.PrefetchScalarGridSpec(
            num_scalar_prefetch=1, grid=(n, nt),
            in_specs=[pl.BlockSpec((None, tr, width), lambda s, t, c: (s, c[0] * nt + t, 0)),
                      pl.BlockSpec((None, tr, width), lambda s, t, c: (s, t, 0))],
            out_specs=pl.BlockSpec((None, tr, width), lambda s, t, c: (s, t, 0))),
        out_shape=SDS(r.shape, BF16),
        compiler_params=_cp(("parallel", "parallel")),
    )(cidx, g, r)


class _ChipExchange:
    def __init__(self, arrays, rows):
        self.arrays, self.rows = list(arrays), list(rows)
        self.out_shape = [SDS((4,) + a.shape[1:], BF16) for a in self.arrays]
        self.scratch = [pltpu.SemaphoreType.DMA((3 * len(self.arrays),)) for _ in range(2)]

    def _copies(self, ins, outs, sems):
        x, y, c, j = _place()
        send, recv = sems
        for a, (src, dst, row) in enumerate(zip(ins, outs, self.rows)):
            for k in (1, 2, 3):
                px, py, pj = _chip_of(x, y, k)
                n = 3 * a + k - 1
                slot = pj if row is None else py
                yield (None if row is None else px == row, None if row is None else x == row,
                       _remote(src.at[slot], dst.at[j], send.at[n], recv.at[n], (px, py, c)),
                       _remote(src.at[0], dst.at[pj], send.at[n], recv.at[n], (x, y, c)))

    def start(self, ins, outs, sems):
        for sends, _, send, _ in self._copies(ins, outs, sems):
            if sends is None:
                send.start()
            else:
                pl.when(sends)(send.start)

    def finish(self, ins, outs, sems):
        for sends, owns, send, arrival in self._copies(ins, outs, sems):
            if sends is None:
                arrival.wait_recv()
                send.wait_send()
            else:
                pl.when(owns)(arrival.wait_recv)
                pl.when(sends)(send.wait_send)


def _small_exchange(small):
    def body(sm_ref, rs_ref, send, recv, lsem):
        x, y, c, j = _place()
        me = 2 * j + c
        local = pltpu.make_async_copy(sm_ref, rs_ref.at[me], lsem)
        local.start()
        cps = []
        for k in range(1, 8):
            px, py, _ = _chip_of(x, y, k >> 1)
            pc = 1 - c if k & 1 else c
            cps.append(_remote(sm_ref, rs_ref.at[me], send.at[k - 1], recv.at[k - 1], (px, py, pc)))
        for cp in cps:
            cp.start()
        for k in range(1, 8):
            _, _, pj = _chip_of(x, y, k >> 1)
            pc = 1 - c if k & 1 else c
            _remote(sm_ref, rs_ref.at[2 * pj + pc], send.at[k - 1], recv.at[k - 1], (x, y, c)).wait_recv()
        for cp in cps:
            cp.wait_send()
        local.wait()

    return pl.pallas_call(
        body, name="small_exchange", in_specs=[ANY], out_specs=ANY,
        out_shape=SDS((8,) + small.shape, F32),
        scratch_shapes=[pltpu.SemaphoreType.DMA((7,)), pltpu.SemaphoreType.DMA((7,)), pltpu.SemaphoreType.DMA],
        compiler_params=pltpu.CompilerParams(has_side_effects=True),
    )(small)


def _slot_sum(r, name):
    n, rows, width = r.shape
    tr = min(rows, 256)

    def body(r_ref, o_ref):
        acc = r_ref[0].astype(F32)
        for s in range(1, n):
            acc = acc + r_ref[s].astype(F32)
        o_ref[...] = acc

    return pl.pallas_call(
        body, name=name, grid=(rows // tr,),
        in_specs=[pl.BlockSpec((n, tr, width), lambda t: (0, t, 0))],
        out_specs=pl.BlockSpec((tr, width), lambda t: (t, 0)),
        out_shape=SDS((rows, width), F32),
        compiler_params=_cp(("parallel",)),
    )(r)


def _chip_sum(where, recv, own, name):
    n, rows, width = recv.shape
    tr = min(rows, 256)
    nt = rows // tr

    def body(j_ref, r_ref, own_ref, o_ref):
        acc = None
        for s in range(n):
            term = jnp.where(j_ref[0] == s, own_ref[...], r_ref[s]).astype(F32)
            acc = term if acc is None else acc + term
        o_ref[...] = acc

    return pl.pallas_call(
        body, name=name,
        grid_spec=pltpu.PrefetchScalarGridSpec(
            num_scalar_prefetch=1, grid=(nt,),
            in_specs=[pl.BlockSpec((n, tr, width), lambda t, j: (0, t, 0)),
                      pl.BlockSpec((None, tr, width), lambda t, j: (j[0], t, 0))],
            out_specs=pl.BlockSpec((tr, width), lambda t, j: (j[1] * nt + t, 0))),
        out_shape=SDS((2 * rows, width), F32),
        compiler_params=_cp(("parallel",)),
    )(where, recv, own)


def _chip_sum_rows(place, recv0, own0, recv1, own1, name):
    n, rows, width = recv0.shape
    tr = min(rows, 256)
    nt = rows // tr

    def body(p_ref, r0_ref, o0_ref, r1_ref, o1_ref, o_ref):
        first_row = p_ref[2] == 0
        own = jnp.where(first_row, o0_ref[...], o1_ref[...])
        acc = None
        for s in range(n):
            term = jnp.where(p_ref[0] == s, own, jnp.where(first_row, r0_ref[s], r1_ref[s])).astype(F32)
            acc = term if acc is None else acc + term
        o_ref[...] = acc

    recv = pl.BlockSpec((n, tr, width), lambda t, p: (0, t, 0))
    own = pl.BlockSpec((None, tr, width), lambda t, p: (p[3], t, 0))
    return pl.pallas_call(
        body, name=name,
        grid_spec=pltpu.PrefetchScalarGridSpec(
            num_scalar_prefetch=1, grid=(nt,), in_specs=[recv, own, recv, own],
            out_specs=pl.BlockSpec((tr, width), lambda t, p: (p[1] * nt + t, 0))),
        out_shape=SDS((2 * rows, width), F32),
        compiler_params=_cp(("parallel",)),
    )(place, recv0, own0, recv1, own1)


def _half_exchange(gw, go):
    def body(gw_in, go_in, gw_ref, go_ref, send, recv):
        del gw_in, go_in
        x, y, c, _ = _place()
        mine = [pl.ds(c * (r.shape[0] // 2), r.shape[0] // 2) for r in (gw_ref, go_ref)]
        cps = [_remote(r.at[rows], r.at[rows], send.at[k], recv.at[k], (x, y, 1 - c))
               for k, (r, rows) in enumerate(zip((gw_ref, go_ref), mine))]
        for cp in cps:
            cp.start()
        for k, r in enumerate((gw_ref, go_ref)):
            theirs = pl.ds((1 - c) * (r.shape[0] // 2), r.shape[0] // 2)
            _remote(r.at[theirs], r.at[theirs], send.at[k], recv.at[k], (x, y, c)).wait_recv()
        for cp in cps:
            cp.wait_send()

    return pl.pallas_call(
        body, name="half_exchange", in_specs=[ANY, ANY], out_specs=[ANY, ANY],
        out_shape=[SDS(gw.shape, F32), SDS(go.shape, F32)], input_output_aliases={0: 0, 1: 1},
        scratch_shapes=[pltpu.SemaphoreType.DMA((2,)), pltpu.SemaphoreType.DMA((2,))],
        compiler_params=pltpu.CompilerParams(has_side_effects=True),
    )(gw, go)


def _adamw(w, g, m, v, name):
    rows, width = w.shape
    tr = min(rows, 256)

    def body(w_ref, g_ref, m_ref, v_ref, d_ref, nm_ref, nv_ref):
        gv = g_ref[...]
        nm = ADAM_B1 * m_ref[...] + (1.0 - ADAM_B1) * gv
        nv = ADAM_B2 * v_ref[...] + (1.0 - ADAM_B2) * (gv * gv)
        m_hat = nm / (1.0 - ADAM_B1 ** ADAM_STEP)
        v_hat = nv / (1.0 - ADAM_B2 ** ADAM_STEP)
        d_ref[...] = -ADAM_LR * (m_hat / (jnp.sqrt(v_hat) + ADAM_EPS) + ADAM_WD * w_ref[...])
        nm_ref[...] = nm
        nv_ref[...] = nv

    t = pl.BlockSpec((tr, width), lambda i: (i, 0))
    return pl.pallas_call(
        body, name=name, grid=(rows // tr,), in_specs=[t] * 4, out_specs=[t] * 3,
        out_shape=[SDS(w.shape, F32)] * 3, compiler_params=_cp(("parallel",)),
    )(w, g, m, v)


def _rowwise(a):
    return jnp.transpose(a, (2, 0, 1)).reshape(SHARD * D // LANE, LANE)


def _columns(ref):
    return jnp.concatenate([ref[pl.ds(c, LANE, stride=8), :].T for c in range(D // LANE)], axis=0)


def _shard_bf16(w_rows):
    def body(w_ref, o_ref):
        o_ref[...] = _columns(w_ref).astype(BF16)

    return pl.pallas_call(
        body, name="shard_bf16", grid=(pl.cdiv(SHARD, LANE),),
        in_specs=[pl.BlockSpec((D, LANE), lambda t: (t, 0))], out_specs=pl.BlockSpec((D, LANE), lambda t: (0, t)),
        out_shape=SDS((D, SHARD), BF16), compiler_params=_cp(("parallel",)),
    )(w_rows)


def _adamw_in(w_rows, g, m_rows, v_rows):
    def body(w_ref, g_ref, m_ref, v_ref, d_ref, nm_ref, nv_ref):
        columns = _columns
        gv = g_ref[...]
        nm = ADAM_B1 * columns(m_ref) + (1.0 - ADAM_B1) * gv
        nv = ADAM_B2 * columns(v_ref) + (1.0 - ADAM_B2) * (gv * gv)
        m_hat = nm / (1.0 - ADAM_B1 ** ADAM_STEP)
        v_hat = nv / (1.0 - ADAM_B2 ** ADAM_STEP)
        d_ref[...] = -ADAM_LR * (m_hat / (jnp.sqrt(v_hat) + ADAM_EPS) + ADAM_WD * columns(w_ref))
        nm_ref[...] = nm
        nv_ref[...] = nv

    tile = pl.BlockSpec((D, LANE), lambda t: (0, t))
    rows = pl.BlockSpec((D, LANE), lambda t: (t, 0))
    return pl.pallas_call(
        body, name="adamw_in", grid=(pl.cdiv(SHARD, LANE),), in_specs=[rows, tile, rows, rows],
        out_specs=[tile] * 3, out_shape=[SDS(g.shape, F32)] * 3, compiler_params=_cp(("parallel",)),
    )(w_rows, g, m_rows, v_rows)


def _rows128(a, rows):
    flat = a.reshape(-1)
    return jnp.pad(flat, (0, rows * LANE - flat.shape[0])).reshape(rows, LANE)


def _pack_small(conv_w, norm_pre, conv_b, ssm_norm, norm_post, dtb, alog, dsk, extra=None):
    cw_rows = 48 if conv_w.shape[-1] == 1536 else 16
    extra = jnp.zeros((1, LANE), F32) if extra is None else _rows128(extra, 1)
    vec = jnp.concatenate([_rows128(dtb, 1), _rows128(alog, 1), _rows128(dsk, 1), extra, jnp.zeros((4, LANE), F32)],
                          axis=0)
    return jnp.concatenate([_rows128(conv_w, cw_rows), _rows128(norm_pre, 8), _rows128(conv_b, 16),
                            _rows128(ssm_norm, 8), _rows128(norm_post, 8), vec], axis=0)


def _unpack_small(p, cw_cols):
    cw_rows = 48 if cw_cols == 1536 else 16
    o = cw_rows
    conv_w = p[:cw_rows].reshape(-1)[:4 * cw_cols].reshape(1, 4, cw_cols)
    norm_pre = p[o:o + 8].reshape(1, D)
    conv_b = p[o + 8:o + 24].reshape(-1)[:1536].reshape(1, 1536)
    ssm_norm = p[o + 24:o + 32].reshape(1, D)
    norm_post = p[o + 32:o + 40].reshape(1, D)
    vec = p[o + 40:o + 48]
    return conv_w, norm_pre, conv_b, ssm_norm, norm_post, vec[0:1, :NH], vec[1:2, :NH], vec[2:3, :NH], vec[3, 0]


def _pad_lanes(a):
    return jnp.pad(a, ((0, 0), (0, LANE - a.shape[1])))


class _GradReduce:
    SPLIT = 2 * SHARD - OFF_G

    def __init__(self, xi, yi, ci):
        self.cidx = jnp.reshape(ci, (1,)).astype(jnp.int32)
        self.place = jnp.stack([2 * xi + yi, ci, xi, yi]).astype(jnp.int32)

    def pairs(self, dw_g, dw_z, dw_x, dw_out):
        cols = jnp.concatenate([dw_g[:, self.SPLIT:], dw_z, dw_x], axis=1)
        self.gw_hi = jnp.stack([cols[:, :SHARD], cols[:, SHARD:2 * SHARD]])
        self.go = dw_out.reshape(4, D // 2, D)
        return _PairExchange([self.gw_hi, self.go])

    def first(self, got):
        rw, ro = got
        self.pw_hi = _pair_sum(self.cidx, self.gw_hi, rw, "pair_sum_hi")
        self.po = _pair_sum(self.cidx, self.go, ro, "pair_sum_out")
        return _ChipExchange([self.pw_hi, self.po], [1, None])

    def first_done(self, got):
        self.rw_hi, self.ro = got

    def second(self, dw_q, dw_k, dw_v, dw_g):
        cols = jnp.concatenate([dw_q, dw_k, dw_v, dw_g[:, :self.SPLIT]], axis=1)
        gw = jnp.stack([cols[:, :SHARD], cols[:, SHARD:]])
        (rw,) = _pair_exchange([gw], "pair_exchange_lo")
        self.pw_lo = _pair_sum(self.cidx, gw, rw, "pair_sum_lo")
        return _ChipExchange([self.pw_lo], [0])

    def second_done(self, got):
        (self.rw_lo,) = got

    def result(self):
        half_in = _chip_sum_rows(self.place, self.rw_lo, self.pw_lo, self.rw_hi, self.pw_hi, "chip_sum_in")
        half_out = _chip_sum(self.place[0:2], self.ro, self.po, "chip_sum_out")
        return _half_exchange(half_in, half_out)


def kernel(x, norm_pre_w, w_in, conv_w, conv_b, dt_bias, a_log, d_skip, ssm_norm_w, w_out, norm_post_w, loss_target, m_norm_pre_w, m_w_in, m_conv_w, m_conv_b, m_dt_bias, m_a_log, m_d_skip, m_ssm_norm_w, m_w_out, m_norm_post_w, v_norm_pre_w, v_w_in, v_conv_w, v_conv_b, v_dt_bias, v_a_log, v_d_skip, v_ssm_norm_w, v_w_out, v_norm_post_w):
    xi, yi, ci = lax.axis_index("x"), lax.axis_index("y"), lax.axis_index("c")
    chip = 2 * xi + yi
    x2, tgt = x[0], loss_target[0]

    w_rows = _rowwise(w_in)
    gin = _gather_weights(_shard_bf16(w_rows))
    w_all = jnp.concatenate([gin[0], gin[1], gin[2], gin[3], jnp.zeros((D, DP - 4 * SHARD), BF16)], axis=1)
    reduce = _GradReduce(xi, yi, ci)
    grad_x, small = _local_step(x2, tgt, w_all, _LateGather(w_out[0].astype(BF16), conv_w[0]), norm_pre_w, conv_b,
                                dt_bias, a_log, d_skip, ssm_norm_w, norm_post_w, reduce)[:2]
    g_in, g_out = reduce.result()
    g_small = _slot_sum(_small_exchange(small), "small_sum")
    g_cw, g_npre, g_cb, g_nssm, g_npost, g_dtb, g_alog, g_dsk, loss = _unpack_small(g_small, 1536)
    g_cw = lax.dynamic_slice_in_dim(g_cw, chip * 384, 384, axis=2)

    d_in, nm_in, nv_in = _adamw_in(w_rows, g_in, _rowwise(m_w_in), _rowwise(v_w_in))
    d_out, nm_out, nv_out = _adamw(w_out[0], g_out, m_w_out[0], v_w_out[0], "adamw_out")
    packed = [_pack_small(*t) for t in (
        (conv_w, norm_pre_w, conv_b, ssm_norm_w, norm_post_w, dt_bias, a_log, d_skip),
        (g_cw, g_npre, g_cb, g_nssm, g_npost, g_dtb, g_alog, g_dsk),
        (m_conv_w, m_norm_pre_w, m_conv_b, m_ssm_norm_w, m_norm_post_w, m_dt_bias, m_a_log, m_d_skip),
        (v_conv_w, v_norm_pre_w, v_conv_b, v_ssm_norm_w, v_norm_post_w, v_dt_bias, v_a_log, v_d_skip))]
    small_out = [_unpack_small(p, 384)[:8] for p in _adamw(*packed, "adamw_small")]

    def ordered(cw_, npre, cb_, nssm, npost, dtb_, alog_, dsk_, big_in, big_out):
        return [npre, big_in[None], cw_, cb_, dtb_, alog_, dsk_, nssm, big_out[None], npost]

    grads = ordered(g_cw, g_npre, g_cb, g_nssm, g_npost, g_dtb, g_alog, g_dsk, g_in, g_out)
    deltas = ordered(*small_out[0], d_in, d_out)
    new_m = ordered(*small_out[1], nm_in, nm_out)
    new_v = ordered(*small_out[2], nv_in, nv_out)
    return (loss, grad_x[None], *grads, *deltas, *new_m, *new_v)


def _local_step(x2, tgt, w_all, late, norm_pre_w, conv_b, dt_bias, a_log, d_skip, ssm_norm_w,
                norm_post_w, reduce=None):
    dtb, alog = _pad_lanes(dt_bias), _pad_lanes(a_log)
    d_b = jnp.repeat(d_skip, 64, axis=1)

    if isinstance(late, _LateGather):
        (proj, u), (gout, gcw) = _inproj_fwd(x2, norm_pre_w, w_all, late)
        w_out_all = gout.reshape(2 * D, D)
        cw_all = jnp.concatenate([gcw[0], gcw[1], gcw[2], gcw[3]], axis=1)
    else:
        proj, u = _inproj_fwd(x2, norm_pre_w, w_all)
        w_out_all, cw_all = late
    mix, attn_pre, lse = _attn_fwd(proj, 1, _attn_fwd(proj, 4, _attn_fwd(proj, 16)), final=True)
    mix, y_save, states, conv_out = _ssm_fwd(proj, mix, cw_all, conv_b, dtb, alog, d_b, ssm_norm_w)

    dy, dn_ssm, do, delta, dg, dw_out, dnw_post, loss_part = _outproj_loss(mix, w_out_all, x2, tgt, norm_post_w,
                                                                          attn_pre, proj)
    dz, dxbcdt, dcw, dcb, dvec, dnw_ssm = _ssm_bwd(proj, dn_ssm, y_save, states, conv_out, cw_all, dtb, alog, d_b,
                                                   ssm_norm_w)
    dw_g, dw_z, dw_x = _dw(u, dg, "dw_in_g"), _dw(u, dz, "dw_in_z"), _dw(u, dxbcdt, "dw_in_xbcdt")
    acc = _attn_bwd(proj, do, lse, delta, 16, None, F32, reduce.pairs(dw_g, dw_z, dw_x, dw_out) if reduce else None)
    if reduce:
        acc, got = acc
    acc = _attn_bwd(proj, do, lse, delta, 4, acc, F32, reduce.first(got) if reduce else None)
    if reduce:
        acc, got = acc
        reduce.first_done(got)
    dq, dk, dv = _attn_bwd(proj, do, lse, delta, 1, acc, BF16)
    dw_q, dw_k, dw_v = _dw(u, dq, "dw_in_q"), _dw(u, dk, "dw_in_k"), _dw(u, dv, "dw_in_v")
    res = _inproj_bwd_dx([dq, dk, dv, dg, dz], dxbcdt, w_all, x2, dy, norm_pre_w,
                         reduce.second(dw_q, dw_k, dw_v, dw_g) if reduce else None)
    if reduce:
        res, got = res
        reduce.second_done(got)
    grad_x, dnw_pre = res
    dw_all = jnp.concatenate([dw_q, dw_k, dw_v, dw_g, dw_z, dw_x], axis=1)
    small = _pack_small(dcw, dnw_pre, dcb, dnw_ssm, dnw_post, dvec[0:1, :NH], dvec[1:2, :NH], dvec[2:3, :NH],
                        loss_part[:, :1])
    return grad_x, small, dw_all, dw_out
```

```python
import functools

import jax
import jax.numpy as jnp
from jax import lax
from jax.experimental import pallas as pl
from jax.experimental.pallas import tpu as pltpu

F32 = jnp.float32
BF16 = jnp.bfloat16
MESH = pl.DeviceIdType.MESH
SDS = jax.ShapeDtypeStruct
ANY = pl.BlockSpec(memory_space=pl.ANY)

S = 4096
D = 1024
DP = 7168
SHARD = 1668
OFF_G, OFF_Z = 3072, 4096
NH = 16
CH = 128
NC = S // CH
EPS = 1e-6
NEG = -1e30
LANE = 128
VMEM_LIMIT = 48 * 1024 * 1024

TILES = SHARD // LANE
WIN = (TILES + 1) * LANE
SHIFT = SHARD - TILES * LANE
SECTION_TILES = {"q": (0, 8), "k": (8, 8), "v": (16, 8), "g": (24, 8), "z": (32, 8), "x": (40, 16)}

ADAM_LR, ADAM_B1, ADAM_B2, ADAM_EPS, ADAM_WD, ADAM_STEP = 0.001, 0.9, 0.999, 1e-08, 0.01, 10


def _cp(sem, **kw):
    return pltpu.CompilerParams(dimension_semantics=sem, vmem_limit_bytes=VMEM_LIMIT, **kw)


def _dot(a, b):
    return jnp.dot(a, b, preferred_element_type=F32)


def _dot_nt(a, b):
    return lax.dot_general(a, b, (((1,), (1,)), ((), ())), preferred_element_type=F32)


def _dot_tn(a, b):
    return lax.dot_general(a, b, (((0,), (0,)), ((), ())), preferred_element_type=F32)


def _pieces(x, n):
    out = []
    for _ in range(n):
        p = x.astype(BF16)
        out.append(p)
        x = x - p.astype(F32)
    return out


def _pick(x, sel, n=2):
    parts = [_dot(p, sel) for p in _pieces(x, n)]
    return functools.reduce(jnp.add, parts)


def _pick_left(sel, x, n=3):
    parts = [_dot(sel, p) for p in _pieces(x, n)]
    return functools.reduce(jnp.add, parts)


def _sigmoid(v):
    return 0.5 * jnp.tanh(0.5 * v) + 0.5


def _iota(shape, dim):
    return lax.broadcasted_iota(jnp.int32, shape, dim)


def _inproj_fwd(x, nw, w_all, hosted=None):
    tm, tn = 1024, 1024
    n_host = len(hosted.arrays) if hosted else 0

    def body(x_ref, nw_ref, w_ref, *refs):
        host_in, (proj_ref, u_ref), refs = refs[:n_host], refs[n_host:n_host + 2], refs[n_host + 2:]
        host_out, host_sems = refs[:n_host], refs[n_host:]
        i, j = pl.program_id(0), pl.program_id(1)
        if hosted:
            pl.when((i == 0) & (j == 0))(lambda: hosted.start(host_in, host_out, host_sems))

        @pl.when(j == 0)
        def _():
            xf = x_ref[...]
            r = lax.rsqrt(jnp.mean(xf * xf, axis=-1, keepdims=True) + EPS)
            u_ref[...] = (xf * r * nw_ref[...]).astype(BF16)

        proj_ref[...] = _dot(u_ref[...], w_ref[...])
        if hosted:
            pl.when((i == S // tm - 1) & (j == DP // tn - 1))(lambda: hosted.finish(host_in, host_out, host_sems))

    outs = pl.pallas_call(
        body, name="inproj_fwd", grid=(S // tm, DP // tn),
        in_specs=[pl.BlockSpec((tm, D), lambda i, j: (i, 0)), pl.BlockSpec((1, D), lambda i, j: (0, 0)),
                  pl.BlockSpec((D, tn), lambda i, j: (0, j))] + [ANY] * n_host,
        out_specs=[pl.BlockSpec((tm, tn), lambda i, j: (i, j)), pl.BlockSpec((tm, D), lambda i, j: (i, 0))]
        + [ANY] * n_host,
        out_shape=[SDS((S, DP), F32), SDS((S, D), BF16)] + (hosted.out_shape if hosted else []),
        scratch_shapes=hosted.scratch if hosted else [],
        compiler_params=_cp(("arbitrary", "arbitrary") if hosted else ("parallel", "arbitrary")),
    )(x, nw, w_all, *(hosted.arrays if hosted else []))
    return (outs[:2], outs[2:]) if hosted else outs


ATTN_QB = {1: 16, 4: 4, 16: 1}


def _unit_rows(r, u, d):
    return pl.ds(r + d * CH * u, CH, stride=d) if d > 1 else pl.ds(CH * u, CH)


def _for_units(d, qb, fn):
    for r in range(d):
        for u in range(qb):
            fn(r, u)


def _attn_mask(has_prev):
    qi, kj = _iota((2 * CH, 2 * CH), 0) & (CH - 1), _iota((2 * CH, 2 * CH), 1)
    cur_ok = (kj >= CH) & (kj - CH <= qi)
    prev_ok = (kj < CH) & (kj >= qi)
    return cur_ok | (prev_ok & has_prev)


def _stack_heads(v, lane_a):
    return jnp.concatenate([jnp.where(lane_a, v, 0.0), jnp.where(lane_a, 0.0, v)], axis=0).astype(BF16)


def _attn_specs(d, qb):
    rows, prows = CH * d * qb, CH * d
    nb = S // rows
    steps = (NH // 2) * nb

    def at(t):
        t = jnp.minimum(t, steps - 1)
        return t % nb, t // nb

    def cur(off):
        return pl.BlockSpec((rows, LANE), lambda t: (at(t)[0], off + at(t)[1]))

    def prev(off):
        return pl.BlockSpec((prows, LANE), lambda t: (jnp.maximum(at(t)[0] * qb - 1, 0), off + at(t)[1]))

    lag = pl.BlockSpec((rows, LANE), lambda t: at(jnp.maximum(t - 1, 0)))
    return nb, steps, cur, prev, lag


def _gather16(src_ref, dense_ref, tmp_ref):
    for a in range(4):
        tmp_ref[...] = src_ref[pl.ds(a, 4 * CH, stride=4), :]
        for b in range(4):
            dense_ref[a + 4 * b] = tmp_ref[pl.ds(b, CH, stride=4), :]


def _scatter16(dense_ref, dst_ref, tmp_ref):
    for a in range(4):
        for b in range(4):
            tmp_ref[pl.ds(b, CH, stride=4), :] = dense_ref[a + 4 * b]
        dst_ref[pl.ds(a, 4 * CH, stride=4), :] = tmp_ref[...]


def _unit_index(r, u, d):
    return (r,) if d == 16 else (_unit_rows(r, u, d), slice(None))


def _unit_kv(p_ref, c_ref, r, u, d):
    prev = p_ref[_unit_index(r, 0, d)] if u == 0 else c_ref[_unit_index(r, u - 1, d)]
    return jnp.concatenate([prev, c_ref[_unit_index(r, u, d)]], axis=0).astype(BF16)


def _dense_scratch(d, n):
    return [pltpu.VMEM((16, CH, LANE), F32)] * n + [pltpu.VMEM((4 * CH, LANE), F32)] if d == 16 else []


def _attn_fwd(proj, d, prior=None, final=False):
    qb = ATTN_QB[d]
    nb, steps, cur, prev, _ = _attn_specs(d, qb)
    n_prior = 2 if prior is not None else 0
    n_in, n_out = 5 + n_prior + final, 2 + final
    assert not (d == 16 and (n_prior or final))

    def body(*refs):
        ins, outs, scratch = refs[:n_in], refs[n_in:n_in + n_out], refs[n_in + n_out:]
        if d == 16:
            tmp_ref = scratch[-1]
            for src, dense in zip(ins, scratch):
                _gather16(src, dense, tmp_ref)
            block_outs, ins, outs = outs, scratch[:n_in], scratch[n_in:n_in + n_out]
        q_ref, kp_ref, kc_ref, vp_ref, vc_ref = ins[:5]
        prior_refs = ins[5:5 + n_prior]
        if final:
            g_ref, (mix_ref, o_ref, l_ref) = ins[-1], outs
        else:
            o_ref, l_ref = outs
        i = pl.program_id(0) % nb
        lane_a = _iota((CH, LANE), 1) < 64
        mask_first, mask_rest = _attn_mask(i > 0), _attn_mask(True)

        def unit(r, u):
            at = _unit_index(r, u, d)
            q2 = _stack_heads(q_ref[at] * 0.125, lane_a)
            k2, v2 = _unit_kv(kp_ref, kc_ref, r, u, d), _unit_kv(vp_ref, vc_ref, r, u, d)
            s = jnp.where(mask_first if u == 0 else mask_rest, _dot_nt(q2, k2), NEG)
            m = jnp.max(s, axis=1, keepdims=True)
            p = jnp.exp(s - m)
            l = jnp.sum(p, axis=1, keepdims=True)
            o2 = _dot(p.astype(BF16), v2) / l
            lse2 = m + jnp.log(l)
            o = jnp.where(lane_a, o2[:CH], o2[CH:])
            lse = jnp.where(lane_a, lse2[:CH], lse2[CH:])
            if n_prior:
                o_a, l_a = prior_refs[0][at], prior_refs[1][at]
                top = jnp.maximum(l_a, lse)
                e_a, e_b = jnp.exp(l_a - top), jnp.exp(lse - top)
                tot = e_a + e_b
                o = (e_a * o_a + e_b * o) / tot
                lse = top + jnp.log(tot)
            o_ref[at] = o
            l_ref[at] = lse
            if final:
                g = g_ref[at]
                mix_ref[at] = (o * (g * _sigmoid(g))).astype(BF16)

        _for_units(d, qb, unit)
        if d == 16:
            for dense, dst in zip(outs, block_outs):
                _scatter16(dense, dst, tmp_ref)

    in_specs = [cur(0), prev(8), cur(8), prev(16), cur(16)] + [cur(0)] * n_prior
    args = [proj] * 5 + (list(prior) if n_prior else [])
    out_specs, out_shape = [cur(0), cur(0)], [SDS((S, D), F32), SDS((S, D), F32)]
    if final:
        assert d == 1
        in_specs.append(cur(OFF_G // LANE))
        args.append(proj)
        out_specs, out_shape = [cur(0)] + out_specs, [SDS((S, 2 * D), BF16)] + out_shape
    return pl.pallas_call(
        body, name=f"attn_fwd_d{d}", grid=(steps,),
        in_specs=in_specs, out_specs=out_specs, out_shape=out_shape,
        scratch_shapes=_dense_scratch(d, n_in + n_out),
        compiler_params=_cp(("parallel",)),
    )(*args)


def _attn_bwd(proj, do, lse, delta, d, acc, out_dtype, hosted=None):
    qb = ATTN_QB[d]
    nb, steps, cur, prev, lag = _attn_specs(d, qb)
    has_acc = acc is not None
    n_in = 11 if has_acc else 8
    n_host, n_host_out = (len(hosted.arrays), len(hosted.out_shape)) if hosted else (0, 0)
    assert not (d == 16 and (has_acc or out_dtype != F32))
    rows = CH * d * qb
    carry = (2, 16, CH, LANE) if d == 16 else (2, rows, LANE)

    def body(*refs):
        ins, host_in, refs = refs[:n_in], refs[n_in:n_in + n_host], refs[n_in + n_host:]
        (dq_ref, dk_ref, dv_ref), host_out, scratch = refs[:3], refs[3:3 + n_host_out], refs[3 + n_host_out:]
        if hosted:
            scratch, host_sems = scratch[:-len(hosted.scratch)], scratch[-len(hosted.scratch):]
        ck_ref, cv_ref = scratch[:2]
        dq_f32 = dq_ref if out_dtype == F32 else scratch[2]
        t = pl.program_id(0)
        i = t % nb
        if hosted:
            pl.when(t == 0)(lambda: hosted.start(host_in, host_out, host_sems))
        if d == 16:
            dense, dq_f32, tmp_ref = scratch[2:2 + n_in], scratch[2 + n_in], scratch[-1]

            @pl.when(t < steps)
            def _():
                for src, dst in zip(ins, dense):
                    _gather16(src, dst, tmp_ref)

            ins = dense
        q_ref, kp_ref, kc_ref, vp_ref, vc_ref, do_ref, lse_ref, dl_ref = ins[:8]
        if has_acc:
            aq_ref, ak_ref, av_ref = ins[8:11]
        slot = t & 1
        now_k, now_v, old_k, old_v = ck_ref.at[slot], cv_ref.at[slot], ck_ref.at[1 - slot], cv_ref.at[1 - slot]
        lane_a = _iota((CH, LANE), 1) < 64
        mask_first, mask_rest = _attn_mask(i > 0), _attn_mask(True)

        @pl.when(t == 0)
        def _():
            ck_ref[1] = jnp.zeros(carry[1:], F32)
            cv_ref[1] = jnp.zeros(carry[1:], F32)

        def unit(r, u):
            at = _unit_index(r, u, d)
            q2 = _stack_heads(q_ref[at] * 0.125, lane_a)
            do2 = _stack_heads(do_ref[at], lane_a)
            k2, v2 = _unit_kv(kp_ref, kc_ref, r, u, d), _unit_kv(vp_ref, vc_ref, r, u, d)
            lsev, dlv = lse_ref[at], dl_ref[at]
            lse2 = jnp.concatenate([lsev[:, 0:1], lsev[:, 64:65]], axis=0)
            dl2 = jnp.concatenate([dlv[:, 0:1], dlv[:, 64:65]], axis=0)
            p = jnp.exp(jnp.where(mask_first if u == 0 else mask_rest, _dot_nt(q2, k2), NEG) - lse2)
            ds = (p * (_dot_nt(do2, v2) - dl2)).astype(BF16)
            dq2 = _dot(ds, k2)
            dk2 = _dot_tn(ds, q2)
            dv2 = _dot_tn(p.astype(BF16), do2)
            dq = jnp.where(lane_a, dq2[:CH], dq2[CH:]) * 0.125
            if has_acc:
                dq = dq + aq_ref[at]
            dq_f32[at] = dq
            if u == 0:
                before = _unit_index(r, qb - 1, d)
                old_k[before] += dk2[:CH]
                old_v[before] += dv2[:CH]
            else:
                before = _unit_index(r, u - 1, d)
                now_k[before] += dk2[:CH]
                now_v[before] += dv2[:CH]
            now_k[at] = dk2[CH:]
            now_v[at] = dv2[CH:]

        @pl.when(t < steps)
        def _():
            _for_units(d, qb, unit)
            if d == 16:
                _scatter16(dq_f32, dq_ref, tmp_ref)
            elif out_dtype != F32:
                dq_ref[...] = dq_f32[...].astype(out_dtype)

        if d == 16:
            _scatter16(old_k, dk_ref, tmp_ref)
            _scatter16(old_v, dv_ref, tmp_ref)
        else:
            dk, dv = old_k[...], old_v[...]
            if has_acc:
                dk, dv = dk + ak_ref[...], dv + av_ref[...]
            dk_ref[...] = dk.astype(out_dtype)
            dv_ref[...] = dv.astype(out_dtype)
        if hosted:
            pl.when(t == steps)(lambda: hosted.finish(host_in, host_out, host_sems))

    in_specs = [cur(0), prev(8), cur(8), prev(16), cur(16), cur(0), cur(0), cur(0)]
    args = [proj, proj, proj, proj, proj, do, lse, delta]
    if has_acc:
        in_specs += [cur(0), lag, lag]
        args += list(acc)
    scratch = [pltpu.VMEM(carry, F32), pltpu.VMEM(carry, F32)]
    if d == 16:
        scratch += _dense_scratch(d, n_in + 1)
    elif out_dtype != F32:
        scratch.append(pltpu.VMEM((rows, LANE), F32))
    out_specs, out_shape = [cur(0), lag, lag], [SDS((S, D), out_dtype)] * 3
    if hosted:
        args += hosted.arrays
        in_specs += [ANY] * n_host
        out_specs += [ANY] * n_host_out
        out_shape += hosted.out_shape
        scratch += hosted.scratch
    outs = pl.pallas_call(
        body, name=f"attn_bwd_d{d}", grid=(steps + 1,),
        in_specs=in_specs, out_specs=out_specs, out_shape=out_shape,
        scratch_shapes=scratch, compiler_params=_cp(("arbitrary",)),
    )(*args)
    return (outs[:3], outs[3:]) if hosted else outs


def _conv_taps(cur, prev8, first):
    row8 = _iota(prev8.shape, 0)
    prev8 = jnp.where(first, 0.0, prev8)
    taps = []
    for s in (3, 2, 1):
        rolled = pltpu.roll(cur, s, 0)
        head = jnp.where(row8 < s, pltpu.roll(prev8, s, 0), rolled[:8])
        taps.append(jnp.concatenate([head, rolled[8:]], axis=0))
    return taps + [cur]


def _conv(taps, w, b):
    acc = b + w[0:1, :] * taps[0]
    for k in (1, 2, 3):
        acc = acc + w[k:k + 1, :] * taps[k]
    return acc


def _expand():
    return (_iota((LANE, D), 1) // 64 == _iota((LANE, D), 0)).astype(BF16)


def _reduce():
    return (_iota((D, LANE), 0) // 64 == _iota((D, LANE), 1)).astype(BF16)


def _ssd_common(xs_c, bc_c, dt_raw, dtb, alog):
    head_lane = _iota((CH, LANE), 1) < NH
    xs = xs_c * _sigmoid(xs_c)
    bc = bc_c * _sigmoid(bc_c)
    pre = dt_raw + dtb
    dt = jnp.where(head_lane, jnp.maximum(pre, 0.0) + jnp.log(1.0 + jnp.exp(-jnp.abs(pre))), 0.0)
    a_row = jnp.where(head_lane[0:1], -jnp.exp(alog), 0.0)
    tri = (_iota((CH, CH), 1) <= _iota((CH, CH), 0)).astype(BF16)
    cs = _pick_left(tri, dt * a_row)
    cs_last = cs[CH - 1:CH, :]
    wide = _pick(jnp.concatenate([dt, jnp.exp(cs), jnp.exp(cs_last - cs)], axis=0), _expand())
    dt_b, e_b, f_b = wide[:CH], wide[CH:2 * CH], wide[2 * CH:]
    return dict(xs=xs, bc=bc, pre=pre, dt=dt, a_row=a_row, cs=cs, cs_t=cs.T, dt_b=dt_b, e_b=e_b, f_b=f_b,
                t_b=e_b[CH - 1:CH, :])


def _groups(bc):
    bcb = bc.astype(BF16)
    return [bcb[:, 0:128], bcb[:, 128:256]], [bcb[:, 256:384], bcb[:, 384:512]]


def _decay(q, h, tril):
    seg = q["cs"][:, h:h + 1] - q["cs_t"][h:h + 1, :]
    return jnp.exp(jnp.where(tril, seg, NEG))


def _ssm_fwd(proj, mix, cw, cb, dtb, alog, d_b, nw):
    def body(xs_ref, xsp_ref, bc_ref, bcp_ref, dt_ref, z_ref, cw_ref, cb_ref, dtb_ref, alog_ref, db_ref, nw_ref,
             mix_in_ref, mix_ref, y_ref, st_ref, conv_ref, h_ref):
        del mix_in_ref
        i = pl.program_id(0)

        @pl.when(i == 0)
        def _():
            h_ref[...] = jnp.zeros_like(h_ref)

        cw, cb = cw_ref[...], cb_ref[...]
        xs_c = _conv(_conv_taps(xs_ref[...], xsp_ref[...], i == 0), cw[:, :D], cb[:, :D])
        bc_c = _conv(_conv_taps(bc_ref[...], bcp_ref[...], i == 0), cw[:, D:], cb[:, D:])
        conv_ref[:, :D] = xs_c
        conv_ref[:, D:] = bc_c
        q = _ssd_common(xs_c, bc_c, dt_ref[...], dtb_ref[...], alog_ref[...])
        bg, cg = _groups(q["bc"])
        xs = q["xs"]
        xdt = xs * q["dt_b"]
        xdt_b = xdt.astype(BF16)
        h_in = h_ref[...]
        st_ref[...] = h_in
        hb = h_in.astype(BF16)
        tril = _iota((CH, CH), 1) <= _iota((CH, CH), 0)
        lane_a = _iota((CH, LANE), 1) < 64
        cbm = [_dot_nt(cg[g], bg[g]) for g in range(2)]
        pairs = []
        for hp in range(NH // 2):
            xp = xdt_b[:, hp * LANE:(hp + 1) * LANE]
            ya = _dot((cbm[hp // 4] * _decay(q, 2 * hp, tril)).astype(BF16), xp)
            yb = _dot((cbm[hp // 4] * _decay(q, 2 * hp + 1, tril)).astype(BF16), xp)
            pairs.append(jnp.where(lane_a, ya, yb))
        y_diag = jnp.concatenate(pairs, axis=1)
        y_off = jnp.concatenate([_dot(cg[g], hb[:, g * 512:(g + 1) * 512]) for g in range(2)], axis=1) * q["e_b"]
        y = y_diag + y_off + db_ref[...] * xs
        y_ref[...] = y
        xf = (xdt * q["f_b"]).astype(BF16)
        h_ref[...] = q["t_b"] * h_in + jnp.concatenate(
            [_dot_tn(bg[g], xf[:, g * 512:(g + 1) * 512]) for g in range(2)], axis=1)
        z = z_ref[...]
        yz = y * (z * _sigmoid(z))
        outs = []
        for g in range(2):
            v = yz[:, g * 512:(g + 1) * 512]
            outs.append(v * lax.rsqrt(jnp.mean(v * v, axis=-1, keepdims=True) + EPS))
        mix_ref[...] = (jnp.concatenate(outs, axis=1) * nw_ref[...]).astype(BF16)

    def col(width, blk, prev=False):
        if prev:
            return pl.BlockSpec((8, width), lambda i: (jnp.maximum(i * (CH // 8) - 1, 0), blk))
        return pl.BlockSpec((CH, width), lambda i: (i, blk))

    def full(a):
        return pl.BlockSpec(a.shape, lambda i: (0,) * a.ndim)

    return pl.pallas_call(
        body, name="ssm_fwd", grid=(NC,),
        in_specs=[col(D, 5), col(D, 5, True), col(512, 12), col(512, 12, True), col(LANE, 52), col(D, 4),
                  full(cw), full(cb), full(dtb), full(alog), full(d_b), full(nw), ANY],
        out_specs=[col(D, 1), col(D, 0), pl.BlockSpec((None, CH, D), lambda i: (i, 0, 0)), col(D + 512, 0)],
        out_shape=[SDS((S, 2 * D), BF16), SDS((S, D), F32), SDS((NC, CH, D), F32), SDS((S, D + 512), F32)],
        scratch_shapes=[pltpu.VMEM((CH, D), F32)],
        input_output_aliases={12: 0},
        compiler_params=_cp(("arbitrary",)),
    )(proj, proj, proj, proj, proj, proj, cw, cb, dtb, alog, d_b, nw, mix)


def _ssm_bwd(proj, dn, y_save, states, conv_out, cw, dtb, alog, d_b, nw):
    def body(xs_ref, bc_ref, dt_ref, z_ref, dn_ref, y_ref, st_ref, conv_ref,
             cw_ref, dtb_ref, alog_ref, db_ref, nw_ref,
             dz_ref, dx_ref, dcw_ref, dcb_ref, dsm_ref, dnw_ref, dh_ref, nxs_ref, nbc_ref):
        i = pl.program_id(0)
        ci = NC - 1 - i

        @pl.when(i == 0)
        def _():
            for ref in (dcw_ref, dcb_ref, dsm_ref, dnw_ref, dh_ref, nxs_ref, nbc_ref):
                ref[...] = jnp.zeros_like(ref)

        cw = cw_ref[...]
        xs_c, bc_c = conv_ref[:, :D], conv_ref[:, D:]
        q = _ssd_common(xs_c, bc_c, dt_ref[...], dtb_ref[...], alog_ref[...])
        bg, cg = _groups(q["bc"])
        xs, dt_b, e_b, f_b, t_b = q["xs"], q["dt_b"], q["e_b"], q["f_b"], q["t_b"]
        xdt = xs * dt_b
        xdt_b = xdt.astype(BF16)
        h_in = st_ref[...]
        hb = h_in.astype(BF16)
        dh_new = dh_ref[...]
        dhb = dh_new.astype(BF16)
        red = _reduce()

        z, y, dn, nw_v = z_ref[...], y_ref[...], dn_ref[...], nw_ref[...]
        sig = _sigmoid(z)
        sz = z * sig
        yz = y * sz
        gdn = dn * nw_v
        dyz, dnw = [], []
        for g in range(2):
            v, gv = yz[:, g * 512:(g + 1) * 512], gdn[:, g * 512:(g + 1) * 512]
            r = lax.rsqrt(jnp.mean(v * v, axis=-1, keepdims=True) + EPS)
            dnw.append(dn[:, g * 512:(g + 1) * 512] * v * r)
            dyz.append(r * (gv - v * (r * r) * jnp.mean(gv * v, axis=-1, keepdims=True)))
        dyz = jnp.concatenate(dyz, axis=1)
        dnw_ref[...] += jnp.sum(jnp.concatenate(dnw, axis=1), axis=0, keepdims=True)
        dy = dyz * sz
        dz_ref[...] = (dyz * y * (sig * (1.0 + z * (1.0 - sig)))).astype(BF16)
        dy_b = dy.astype(BF16)

        tril = _iota((CH, CH), 1) <= _iota((CH, CH), 0)
        lane_a = _iota((CH, LANE), 1) < 64
        cbm = [_dot_nt(cg[g], bg[g]) for g in range(2)]
        dcbm = [jnp.zeros((CH, CH), F32), jnp.zeros((CH, CH), F32)]
        seg_rows = jnp.zeros((CH, LANE), F32)
        seg_cols = jnp.zeros((LANE, CH), F32)
        row_id, col_id = _iota((CH, LANE), 0), _iota((CH, LANE), 1)
        dx_pairs = []
        for hp in range(NH // 2):
            g = hp // 4
            xp = xdt_b[:, hp * LANE:(hp + 1) * LANE]
            dyp_f = dy[:, hp * LANE:(hp + 1) * LANE]
            dyp = dy_b[:, hp * LANE:(hp + 1) * LANE]
            halves = []
            for k in range(2):
                h = 2 * hp + k
                lane = lane_a if k == 0 else jnp.logical_not(lane_a)
                dec = _decay(q, h, tril)
                gm = cbm[g] * dec
                dgm = _dot_nt(jnp.where(lane, dyp_f, 0.0).astype(BF16), xp)
                dcbm[g] = dcbm[g] + dgm * dec
                prod = dgm * gm
                seg_rows = jnp.where(col_id == h, jnp.sum(prod, axis=1, keepdims=True), seg_rows)
                seg_cols = jnp.where(row_id == h, jnp.sum(prod, axis=0, keepdims=True), seg_cols)
                halves.append(_dot_tn(gm.astype(BF16), dyp))
            dx_pairs.append(jnp.where(lane_a, halves[0], halves[1]))
        dxdt_diag = jnp.concatenate(dx_pairs, axis=1)

        qv = jnp.concatenate([_dot(bg[g], dhb[:, g * 512:(g + 1) * 512]) for g in range(2)], axis=1)
        y_off = jnp.concatenate([_dot(cg[g], hb[:, g * 512:(g + 1) * 512]) for g in range(2)], axis=1) * e_b
        xfq = xdt * f_b * qv
        dxdt = dxdt_diag + f_b * qv
        tdt = jnp.sum(dh_new * h_in, axis=0, keepdims=True) * t_b
        per_head = _pick(jnp.concatenate([xfq, dy * y_off, dxdt * xs, dy * xs, jnp.broadcast_to(tdt, (8, D))],
                                         axis=0), red)
        fdf, dyoff_h, dxdtxs_h, dyxs_h = [per_head[k * CH:(k + 1) * CH] for k in range(4)]
        dcs = seg_rows - seg_cols.T + dyoff_h - fdf
        last = per_head[4 * CH:4 * CH + 1] + jnp.sum(fdf, axis=0, keepdims=True)
        dcs = dcs + jnp.where(_iota((CH, LANE), 0) == CH - 1, last, 0.0)
        tri_t = (_iota((CH, CH), 1) >= _iota((CH, CH), 0)).astype(BF16)
        da = _pick_left(tri_t, dcs)
        ddt = da * q["a_row"] + dxdtxs_h
        dxs = dxdt * dt_b + db_ref[...] * dy
        ddt_raw = ddt * _sigmoid(q["pre"])
        dsm_ref[0:1, :] += jnp.sum(ddt_raw, axis=0, keepdims=True)
        dsm_ref[1:2, :] += jnp.sum(da * q["dt"], axis=0, keepdims=True) * q["a_row"]
        dsm_ref[2:3, :] += jnp.sum(dyxs_h, axis=0, keepdims=True)
        edy = (e_b * dy).astype(BF16)
        xf = (xdt * f_b).astype(BF16)
        dbs, dcs_g, dhs = [], [], []
        for g in range(2):
            sl = slice(g * 512, (g + 1) * 512)
            dcb_b = dcbm[g].astype(BF16)
            dcs_g.append(_dot(dcb_b, bg[g]) + _dot_nt(edy[:, sl], hb[:, sl]))
            dbs.append(_dot_tn(dcb_b, cg[g]) + _dot_nt(xf[:, sl], dhb[:, sl]))
            dhs.append(_dot_tn(cg[g], edy[:, sl]))
        dh_ref[...] = t_b * dh_new + jnp.concatenate(dhs, axis=1)
        dbc = jnp.concatenate(dbs + dcs_g, axis=1)

        def conv_bwd(dact, pre, x_raw, w, nxt_ref, lo):
            s = _sigmoid(pre)
            dconv = dact * (s * (1.0 + pre * (1.0 - s)))
            nxt8 = nxt_ref[...]
            row8 = _iota(nxt8.shape, 0)
            hi = lo + dconv.shape[1]
            dcb_ref[:, lo:hi] += jnp.sum(dconv, axis=0, keepdims=True)
            later = [dconv]
            for s_ in (1, 2, 3):
                rolled = pltpu.roll(dconv, CH - s_, 0)
                tail = jnp.where(row8 >= 8 - s_, pltpu.roll(nxt8, 8 - s_, 0), rolled[CH - 8:])
                later.append(jnp.concatenate([rolled[:CH - 8], tail], axis=0))
            dx = None
            for s_, up in enumerate(later):
                k = 3 - s_
                dcw_ref[k:k + 1, lo:hi] += jnp.sum(up * x_raw, axis=0, keepdims=True)
                dx = w[k:k + 1, :] * up if dx is None else dx + w[k:k + 1, :] * up
            nxt_ref[...] = dconv[:8]
            return dx

        dx_ref[:, 0:D] = conv_bwd(dxs, xs_c, xs_ref[...], cw[:, :D], nxs_ref, 0).astype(BF16)
        dx_ref[:, D:D + 512] = conv_bwd(dbc, bc_c, bc_ref[...], cw[:, D:], nbc_ref, D).astype(BF16)
        dx_ref[:, D + 512:D + 640] = ddt_raw.astype(BF16)
        dx_ref[:, D + 640:] = jnp.zeros((CH, D - 640), BF16)

    def col(width, blk):
        return pl.BlockSpec((CH, width), lambda i: (NC - 1 - i, blk))

    def full(a):
        return pl.BlockSpec(a.shape, lambda i: (0,) * len(a.shape))

    acc_shapes = [SDS((4, 1536), F32), SDS((1, 1536), F32), SDS((8, LANE), F32), SDS((1, D), F32)]
    return pl.pallas_call(
        body, name="ssm_bwd", grid=(NC,),
        in_specs=[col(D, 5), col(512, 12), col(LANE, 52), col(D, 4),
                  col(D, 0), col(D, 0), pl.BlockSpec((None, CH, D), lambda i: (NC - 1 - i, 0, 0)), col(D + 512, 0),
                  full(cw), full(dtb), full(alog), full(d_b), full(nw)],
        out_specs=[col(D, 0), col(2 * D, 0)] + [full(a) for a in acc_shapes],
        out_shape=[SDS((S, D), BF16), SDS((S, 2 * D), BF16)] + acc_shapes,
        scratch_shapes=[pltpu.VMEM((CH, D), F32), pltpu.VMEM((8, D), F32), pltpu.VMEM((8, 512), F32)],
        compiler_params=_cp(("arbitrary",)),
    )(proj, proj, proj, proj, dn, y_save, states, conv_out, cw, dtb, alog, d_b, nw)


def _outproj_loss(mix, w_out, x, tgt, nw, attn_pre, proj):
    tm = 256

    def body(mix_ref, w_ref, x_ref, t_ref, nw_ref, pre_ref, g_ref,
             dy_ref, dn_ref, do_ref, delta_ref, dg_ref, dw_ref, dnw_ref, loss_ref):
        @pl.when(pl.program_id(0) == 0)
        def _():
            dw_ref[...] = jnp.zeros_like(dw_ref)
            dnw_ref[...] = jnp.zeros_like(dnw_ref)
            loss_ref[...] = jnp.zeros_like(loss_ref)

        mixv, w = mix_ref[...], w_ref[...]
        out = _dot(mixv, w)
        r = lax.rsqrt(jnp.mean(out * out, axis=-1, keepdims=True) + EPS)
        nh = out * r
        nw_v = nw_ref[...]
        err = x_ref[...] + nh * nw_v - t_ref[...]
        loss_ref[...] += 0.5 * jnp.sum(jnp.mean(err * err, axis=-1, keepdims=True), axis=0, keepdims=True)
        dy = err * (1.0 / D)
        dy_ref[...] = dy
        dnw_ref[...] += jnp.sum(dy * nh, axis=0, keepdims=True)
        gdn = dy * nw_v
        dout = (r * (gdn - nh * jnp.mean(gdn * nh, axis=-1, keepdims=True))).astype(BF16)
        dmix = _dot_nt(dout, w)
        dw_ref[...] += _dot_tn(mixv, dout)
        dn_ref[...] = dmix[:, D:]
        dm, g, pre_v = dmix[:, :D], g_ref[...], pre_ref[...]
        sig = _sigmoid(g)
        do = dm * (g * sig)
        do_ref[...] = do
        dg_ref[...] = (dm * pre_v * (sig * (1.0 + g * (1.0 - sig)))).astype(BF16)
        prod = do * pre_v
        same_head = (_iota((LANE, LANE), 0) // 64 == _iota((LANE, LANE), 1) // 64).astype(BF16)
        for cb in range(D // LANE):
            delta_ref[:, cb * LANE:(cb + 1) * LANE] = _pick(prod[:, cb * LANE:(cb + 1) * LANE], same_head)

    row = lambda w: pl.BlockSpec((tm, w), lambda i: (i, 0))
    full = lambda s: pl.BlockSpec(s, lambda i: (0, 0))
    return pl.pallas_call(
        body, name="outproj_loss", grid=(S // tm,),
        in_specs=[row(2 * D), full((2 * D, D)), row(D), row(D), full((1, D)), row(D),
                  pl.BlockSpec((tm, D), lambda i: (i, OFF_G // D))],
        out_specs=[row(D), row(D), row(D), row(D), row(D), full((2 * D, D)), full((1, D)), full((1, LANE))],
        out_shape=[SDS((S, D), F32)] * 4 + [SDS((S, D), BF16), SDS((2 * D, D), F32), SDS((1, D), F32),
                                            SDS((1, LANE), F32)],
        compiler_params=_cp(("arbitrary",)),
    )(mix, w_out, x, tgt, nw, attn_pre, proj)


def _inproj_bwd_dx(srcs, dxbcdt, w_all, x, dy, nw, hosted=None):
    tm = 512
    nk = DP // D
    n_host = len(hosted.arrays) if hosted else 0

    def body(*refs):
        src_refs = refs[:nk]
        w_ref, x_ref, dy_ref, nw_ref = refs[nk:nk + 4]
        host_in, refs = refs[nk + 4:nk + 4 + n_host], refs[nk + 4 + n_host:]
        gx_ref, dnw_ref = refs[:2]
        host_out, host_sems = refs[2:2 + n_host], refs[2 + n_host:]
        i = pl.program_id(0)

        @pl.when(i == 0)
        def _():
            if hosted:
                hosted.start(host_in, host_out, host_sems)
            dnw_ref[...] = jnp.zeros_like(dnw_ref)

        du = None
        for k, ref in enumerate(src_refs):
            part = _dot_nt(ref[...], w_ref[:, k * D:(k + 1) * D])
            du = part if du is None else du + part
        xf, nw_v = x_ref[...], nw_ref[...]
        r = lax.rsqrt(jnp.mean(xf * xf, axis=-1, keepdims=True) + EPS)
        xh = xf * r
        dnw_ref[...] += jnp.sum(du * xh, axis=0, keepdims=True)
        gdu = du * nw_v
        gx_ref[...] = r * (gdu - xh * jnp.mean(gdu * xh, axis=-1, keepdims=True)) + dy_ref[...]

        if hosted:
            pl.when(i == S // tm - 1)(lambda: hosted.finish(host_in, host_out, host_sems))

    row = pl.BlockSpec((tm, D), lambda i: (i, 0))
    row1 = pl.BlockSpec((tm, D), lambda i: (i, 1))
    one = pl.BlockSpec((1, D), lambda i: (0, 0))
    whole_w = pl.BlockSpec((D, DP), lambda i: (0, 0), pipeline_mode=pl.Buffered(1))
    args = [*srcs, dxbcdt, dxbcdt, w_all, x, dy, nw]
    in_specs = [row] * len(srcs) + [row, row1, whole_w, row, row, one]
    out_specs, out_shape, scratch = [row, one], [SDS((S, D), F32), SDS((1, D), F32)], []
    if hosted:
        args += hosted.arrays
        in_specs += [ANY] * n_host
        out_specs += [ANY] * n_host
        out_shape += hosted.out_shape
        scratch += hosted.scratch
    outs = pl.pallas_call(
        body, name="inproj_bwd_dx", grid=(S // tm,),
        in_specs=in_specs, out_specs=out_specs, out_shape=out_shape, scratch_shapes=scratch,
        compiler_params=_cp(("arbitrary",)),
    )(*args)
    return (outs[:2], outs[2:]) if hosted else outs


def _dw(u, dsec, name):
    ts = 1024
    ncol = dsec.shape[1] // D

    def body(u_ref, d_ref, o_ref):
        @pl.when(pl.program_id(1) == 0)
        def _():
            o_ref[...] = jnp.zeros_like(o_ref)

        o_ref[...] += _dot_tn(u_ref[...], d_ref[...])

    return pl.pallas_call(
        body, name=name, grid=(ncol, S // ts),
        in_specs=[pl.BlockSpec((ts, D), lambda j, i: (i, 0)), pl.BlockSpec((ts, D), lambda j, i: (i, j))],
        out_specs=pl.BlockSpec((D, D), lambda j, i: (0, j)),
        out_shape=SDS((D, ncol * D), F32),
        compiler_params=_cp(("parallel", "arbitrary")),
    )(u, dsec)


def _place():
    x, y, c = lax.axis_index("x"), lax.axis_index("y"), lax.axis_index("c")
    return x, y, c, 2 * x + y


def _chip_of(x, y, k):
    px = 1 - x if k & 2 else x
    py = 1 - y if k & 1 else y
    return px, py, 2 * px + py


def _remote(src, dst, send_sem, recv_sem, dev):
    return pltpu.make_async_remote_copy(src_ref=src, dst_ref=dst, send_sem=send_sem, recv_sem=recv_sem,
                                        device_id=dev, device_id_type=MESH)


def _gather_weights(w_in_b):
    half = w_in_b.shape[0] // 2
    quarter = half // 2

    def body(src, dst, send, recv):
        x, y, c, j = _place()
        me, sib = (x, y, c), (x, y, 1 - c)
        nbr = {"x": _chip_of(x, y, 2), "y": _chip_of(x, y, 1)}
        diag = _chip_of(x, y, 3)[2]
        started, arrivals = [], []

        def rows(n_quarter=None, sibling=False):
            base = (1 - c if sibling else c) * half
            return pl.ds(base, half) if n_quarter is None else pl.ds(base + n_quarter * quarter, quarter)

        def sem(n):
            return send.at[n], recv.at[n]

        def go(cp):
            cp.start()
            started.append(cp)

        own = _remote(src, dst.at[j], *sem(8), sib)
        go(own)
        for n, axis in enumerate("xy"):
            px, py, _ = nbr[axis]
            go(_remote(src.at[rows()], dst.at[j, rows()], *sem(n), (px, py, c)))
        for n, axis in enumerate("xy"):
            ox, oy, _ = nbr["y" if axis == "x" else "x"]
            pj = nbr[axis][2]
            _remote(src.at[rows()], dst.at[pj, rows()], *sem(n), me).wait_recv()
            go(_remote(dst.at[pj, rows(n)], dst.at[pj, rows(n)], *sem(2 + n), (ox, oy, c)))
            go(_remote(dst.at[pj, rows()], dst.at[pj, rows()], *sem(4 + n), sib))
            arrivals.append(_remote(src.at[rows()], dst.at[pj, rows(None, True)], *sem(4 + n), me))
        for n in range(2):
            _remote(dst.at[diag, rows(n)], dst.at[diag, rows(n)], *sem(2 + n), me).wait_recv()
            go(_remote(dst.at[diag, rows(n)], dst.at[diag, rows(n)], *sem(6 + n), sib))
            arrivals.append(_remote(dst.at[diag, rows(n, True)], dst.at[diag, rows(n, True)], *sem(6 + n), me))
        for cp in arrivals + [own]:
            cp.wait_recv()
        for cp in started:
            cp.wait_send()

    return pl.pallas_call(
        body, name="gather_weights", in_specs=[ANY], out_specs=ANY,
        out_shape=SDS((4,) + w_in_b.shape, BF16),
        scratch_shapes=[pltpu.SemaphoreType.DMA((9,)), pltpu.SemaphoreType.DMA((9,))],
        compiler_params=pltpu.CompilerParams(has_side_effects=True),
    )(w_in_b)


class _LateGather:
    def __init__(self, w_out_b, conv_w):
        self.arrays = [w_out_b, conv_w]
        self.out_shape = [SDS((4,) + w_out_b.shape, BF16), SDS((4,) + conv_w.shape, F32)]
        self.scratch = [pltpu.SemaphoreType.DMA((11,)), pltpu.SemaphoreType.DMA((11,))]

    def _plan(self, ins, outs, sems):
        x, y, c, j = _place()
        send, recv = sems
        (wo, cw), (gwo, gcw) = ins, outs
        half = wo.shape[0] // 2
        mine, theirs = pl.ds(c * half, half), pl.ds((1 - c) * half, half)
        me, sib = (x, y, c), (x, y, 1 - c)
        first, arrive, forward, last = [], [], [], []
        for k in (1, 2, 3):
            px, py, pj = _chip_of(x, y, k)
            first += [_remote(wo.at[mine], gwo.at[j, mine], send.at[k - 1], recv.at[k - 1], (px, py, c)),
                      _remote(cw, gcw.at[j], send.at[k + 2], recv.at[k + 2], (px, py, c))]
            arrive.append(_remote(wo.at[mine], gwo.at[pj, mine], send.at[k - 1], recv.at[k - 1], me))
            forward.append(_remote(gwo.at[pj, mine], gwo.at[pj, mine], send.at[k + 5], recv.at[k + 5], sib))
            last += [_remote(cw, gcw.at[pj], send.at[k + 2], recv.at[k + 2], me),
                     _remote(wo.at[theirs], gwo.at[pj, theirs], send.at[k + 5], recv.at[k + 5], me)]
        first += [_remote(wo, gwo.at[j], send.at[9], recv.at[9], sib),
                  _remote(cw, gcw.at[j], send.at[10], recv.at[10], sib)]
        last += first[-2:]
        return first, arrive, forward, last

    def start(self, ins, outs, sems):
        for cp in self._plan(ins, outs, sems)[0]:
            cp.start()

    def finish(self, ins, outs, sems):
        first, arrive, forward, last = self._plan(ins, outs, sems)
        for got, fwd in zip(arrive, forward):
            got.wait_recv()
            fwd.start()
        for cp in last:
            cp.wait_recv()
        for cp in first + forward:
            cp.wait_send()


def _window(s, names):
    lo, hi = TILES * s, TILES * s + TILES + 1
    pieces = []
    for n, name in enumerate(names):
        a, count = SECTION_TILES[name]
        first, last = max(lo, a), min(hi, a + count)
        if first < last:
            pieces.append((n, first - a, last - first, first - lo))
    assert sum(p[2] for p in pieces) == TILES + 1
    return pieces


class _PairExchange:
    def __init__(self, names, sections, shards, more=()):
        self.names, self.shards = names, shards
        self.arrays = list(sections) + list(more)
        self.out_shape = [SDS((len(shards), sections[0].shape[0] // 2, WIN), F32)]
        self.out_shape += [SDS((a.shape[0], a.shape[1] // 2, a.shape[2]), F32) for a in more]
        n = sum(len(_window(s, names)) for s in shards) + len(more)
        self.scratch = [pltpu.SemaphoreType.DMA((n,)) for _ in range(2)]

    def _copies(self, ins, outs, sems):
        x, y, c, _ = _place()
        sib = (x, y, 1 - c)
        half = ins[0].shape[0] // 2
        rows = pl.ds((1 - c) * half, half)
        k = 0
        for i, s in enumerate(self.shards):
            for n, tile, tiles, at in _window(s, self.names):
                yield _remote(ins[n].at[rows, pl.ds(tile * LANE, tiles * LANE)],
                              outs[0].at[i, :, pl.ds(at * LANE, tiles * LANE)], sems[0].at[k], sems[1].at[k], sib)
                k += 1
        for src, dst in zip(ins[len(self.names):], outs[1:]):
            half = src.shape[1] // 2
            yield _remote(src.at[:, pl.ds((1 - c) * half, half)], dst, sems[0].at[k], sems[1].at[k], sib)
            k += 1

    def start(self, ins, outs, sems):
        for cp in self._copies(ins, outs, sems):
            cp.start()

    def finish(self, ins, outs, sems):
        for cp in self._copies(ins, outs, sems):
            cp.wait()


def _exchange_call(exchange, name):
    n, n_out = len(exchange.arrays), len(exchange.out_shape)

    def body(*refs):
        ins, outs, sems = refs[:n], refs[n:n + n_out], refs[n + n_out:]
        exchange.start(ins, outs, sems)
        exchange.finish(ins, outs, sems)

    return pl.pallas_call(
        body, name=name, in_specs=[ANY] * n, out_specs=[ANY] * n_out, out_shape=exchange.out_shape,
        scratch_shapes=exchange.scratch, compiler_params=pltpu.CompilerParams(has_side_effects=True),
    )(*exchange.arrays)


def _pair_sum_windows(cidx, names, sections, shards, r, name):
    n, half, _ = r.shape
    tr = min(half, 256)
    nt = half // tr

    def body(c_ref, *refs):
        del c_ref
        secs, r_ref, o_ref = refs[:-2], refs[-2], refs[-1]
        for i, s in enumerate(shards):
            for k, tile, tiles, at in _window(s, names):
                own = secs[k][:, tile * LANE:(tile + tiles) * LANE]
                there = slice(at * LANE, (at + tiles) * LANE)
                o_ref[i, :, there] = (own + r_ref[i, :, there]).astype(BF16)

    window = pl.BlockSpec((n, tr, WIN), lambda t, c: (0, t, 0))
    return pl.pallas_call(
        body, name=name,
        grid_spec=pltpu.PrefetchScalarGridSpec(
            num_scalar_prefetch=1, grid=(nt,),
            in_specs=[pl.BlockSpec((tr, a.shape[1]), lambda t, c: (c[0] * nt + t, 0)) for a in sections] + [window],
            out_specs=window),
        out_shape=SDS(r.shape, BF16),
        compiler_params=_cp(("parallel",)),
    )(cidx, *sections, r)


def _pair_sum(cidx, g, r, name):
    n, half, width = r.shape
    tr = min(half, 256)
    nt = half // tr

    def body(c_ref, g_ref, r_ref, o_ref):
        del c_ref
        o_ref[...] = (g_ref[...] + r_ref[...]).astype(BF16)

    return pl.pallas_call(
        body, name=name,
        grid_spec=pltpu.PrefetchScalarGridSpec(
            num_scalar_prefetch=1, grid=(n, nt),
            in_specs=[pl.BlockSpec((None, tr, width), lambda s, t, c: (s, c[0] * nt + t, 0)),
                      pl.BlockSpec((None, tr, width), lambda s, t, c: (s, t, 0))],
            out_specs=pl.BlockSpec((None, tr, width), lambda s, t, c: (s, t, 0))),
        out_shape=SDS(r.shape, BF16),
        compiler_params=_cp(("parallel", "parallel")),
    )(cidx, g, r)


class _ChipExchange:
    def __init__(self, arrays, rows):
        self.arrays, self.rows = list(arrays), list(rows)
        self.out_shape = [SDS((4,) + a.shape[1:], BF16) for a in self.arrays]
        self.scratch = [pltpu.SemaphoreType.DMA((3 * len(self.arrays),)) for _ in range(2)]

    def _copies(self, ins, outs, sems):
        x, y, c, j = _place()
        send, recv = sems
        for a, (src, dst, row) in enumerate(zip(ins, outs, self.rows)):
            for k in (1, 2, 3):
                px, py, pj = _chip_of(x, y, k)
                n = 3 * a + k - 1
                slot = pj if row is None else py
                yield (None if row is None else px == row, None if row is None else x == row,
                       _remote(src.at[slot], dst.at[j], send.at[n], recv.at[n], (px, py, c)),
                       _remote(src.at[0], dst.at[pj], send.at[n], recv.at[n], (x, y, c)))

    def start(self, ins, outs, sems):
        for sends, _, send, _ in self._copies(ins, outs, sems):
            if sends is None:
                send.start()
            else:
                pl.when(sends)(send.start)

    def finish(self, ins, outs, sems):
        for sends, owns, send, arrival in self._copies(ins, outs, sems):
            if sends is None:
                arrival.wait_recv()
                send.wait_send()
            else:
                pl.when(owns)(arrival.wait_recv)
                pl.when(sends)(send.wait_send)


def _small_exchange(small):
    def body(sm_ref, rs_ref, send, recv, lsem):
        x, y, c, j = _place()
        me = 2 * j + c
        local = pltpu.make_async_copy(sm_ref, rs_ref.at[me], lsem)
        local.start()
        cps = []
        for k in range(1, 8):
            px, py, _ = _chip_of(x, y, k >> 1)
            pc = 1 - c if k & 1 else c
            cps.append(_remote(sm_ref, rs_ref.at[me], send.at[k - 1], recv.at[k - 1], (px, py, pc)))
        for cp in cps:
            cp.start()
        for k in range(1, 8):
            _, _, pj = _chip_of(x, y, k >> 1)
            pc = 1 - c if k & 1 else c
            _remote(sm_ref, rs_ref.at[2 * pj + pc], send.at[k - 1], recv.at[k - 1], (x, y, c)).wait_recv()
        for cp in cps:
            cp.wait_send()
        local.wait()

    return pl.pallas_call(
        body, name="small_exchange", in_specs=[ANY], out_specs=ANY,
        out_shape=SDS((8,) + small.shape, F32),
        scratch_shapes=[pltpu.SemaphoreType.DMA((7,)), pltpu.SemaphoreType.DMA((7,)), pltpu.SemaphoreType.DMA],
        compiler_params=pltpu.CompilerParams(has_side_effects=True),
    )(small)


def _slot_sum(r, name):
    n, rows, width = r.shape
    tr = min(rows, 256)

    def body(r_ref, o_ref):
        acc = r_ref[0].astype(F32)
        for s in range(1, n):
            acc = acc + r_ref[s].astype(F32)
        o_ref[...] = acc

    return pl.pallas_call(
        body, name=name, grid=(rows // tr,),
        in_specs=[pl.BlockSpec((n, tr, width), lambda t: (0, t, 0))],
        out_specs=pl.BlockSpec((tr, width), lambda t: (t, 0)),
        out_shape=SDS((rows, width), F32),
        compiler_params=_cp(("parallel",)),
    )(r)


def _chip_sum(where, recv, own, name):
    n, rows, width = recv.shape
    tr = min(rows, 256)
    nt = rows // tr

    def body(j_ref, r_ref, own_ref, o_ref):
        acc = None
        for s in range(n):
            term = jnp.where(j_ref[0] == s, own_ref[...], r_ref[s]).astype(F32)
            acc = term if acc is None else acc + term
        o_ref[...] = acc

    return pl.pallas_call(
        body, name=name,
        grid_spec=pltpu.PrefetchScalarGridSpec(
            num_scalar_prefetch=1, grid=(nt,),
            in_specs=[pl.BlockSpec((n, tr, width), lambda t, j: (0, t, 0)),
                      pl.BlockSpec((None, tr, width), lambda t, j: (j[0], t, 0))],
            out_specs=pl.BlockSpec((tr, width), lambda t, j: (j[1] * nt + t, 0))),
        out_shape=SDS((2 * rows, width), F32),
        compiler_params=_cp(("parallel",)),
    )(where, recv, own)


def _chip_sum_rows(place, recv0, own0, recv1, own1, name):
    n, rows, width = recv0.shape
    tr = min(rows, 256)
    nt = rows // tr

    def body(p_ref, r0_ref, o0_ref, r1_ref, o1_ref, o_ref):
        first_row = p_ref[2] == 0
        own = jnp.where(first_row, o0_ref[...], o1_ref[...])
        acc = None
        for s in range(n):
            term = jnp.where(p_ref[0] == s, own, jnp.where(first_row, r0_ref[s], r1_ref[s])).astype(F32)
            acc = term if acc is None else acc + term
        o_ref[...] = acc

    recv = pl.BlockSpec((n, tr, width), lambda t, p: (0, t, 0))
    own = pl.BlockSpec((None, tr, width), lambda t, p: (p[3], t, 0))
    return pl.pallas_call(
        body, name=name,
        grid_spec=pltpu.PrefetchScalarGridSpec(
            num_scalar_prefetch=1, grid=(nt,), in_specs=[recv, own, recv, own],
            out_specs=pl.BlockSpec((tr, width), lambda t, p: (p[1] * nt + t, 0))),
        out_shape=SDS((2 * rows, width), F32),
        compiler_params=_cp(("parallel",)),
    )(place, recv0, own0, recv1, own1)


def _half_exchange(gw, go):
    def body(gw_in, go_in, gw_ref, go_ref, send, recv):
        del gw_in, go_in
        x, y, c, _ = _place()
        mine = [pl.ds(c * (r.shape[0] // 2), r.shape[0] // 2) for r in (gw_ref, go_ref)]
        cps = [_remote(r.at[rows], r.at[rows], send.at[k], recv.at[k], (x, y, 1 - c))
               for k, (r, rows) in enumerate(zip((gw_ref, go_ref), mine))]
        for cp in cps:
            cp.start()
        for k, r in enumerate((gw_ref, go_ref)):
            theirs = pl.ds((1 - c) * (r.shape[0] // 2), r.shape[0] // 2)
            _remote(r.at[theirs], r.at[theirs], send.at[k], recv.at[k], (x, y, c)).wait_recv()
        for cp in cps:
            cp.wait_send()

    return pl.pallas_call(
        body, name="half_exchange", in_specs=[ANY, ANY], out_specs=[ANY, ANY],
        out_shape=[SDS(gw.shape, F32), SDS(go.shape, F32)], input_output_aliases={0: 0, 1: 1},
        scratch_shapes=[pltpu.SemaphoreType.DMA((2,)), pltpu.SemaphoreType.DMA((2,))],
        compiler_params=pltpu.CompilerParams(has_side_effects=True),
    )(gw, go)


def _adamw(w, g, m, v, name):
    rows, width = w.shape
    tr = min(rows, 256)

    def body(w_ref, g_ref, m_ref, v_ref, d_ref, nm_ref, nv_ref):
        gv = g_ref[...]
        nm = ADAM_B1 * m_ref[...] + (1.0 - ADAM_B1) * gv
        nv = ADAM_B2 * v_ref[...] + (1.0 - ADAM_B2) * (gv * gv)
        m_hat = nm / (1.0 - ADAM_B1 ** ADAM_STEP)
        v_hat = nv / (1.0 - ADAM_B2 ** ADAM_STEP)
        d_ref[...] = -ADAM_LR * (m_hat / (jnp.sqrt(v_hat) + ADAM_EPS) + ADAM_WD * w_ref[...])
        nm_ref[...] = nm
        nv_ref[...] = nv

    t = pl.BlockSpec((tr, width), lambda i: (i, 0))
    return pl.pallas_call(
        body, name=name, grid=(rows // tr,), in_specs=[t] * 4, out_specs=[t] * 3,
        out_shape=[SDS(w.shape, F32)] * 3, compiler_params=_cp(("parallel",)),
    )(w, g, m, v)


def _rowwise(a):
    return jnp.transpose(a, (2, 0, 1)).reshape(SHARD * D // LANE, LANE)


def _columns(ref):
    return jnp.concatenate([ref[pl.ds(c, LANE, stride=8), :].T for c in range(D // LANE)], axis=0)


def _shard_bf16(w_rows):
    def body(w_ref, o_ref):
        o_ref[...] = _columns(w_ref).astype(BF16)

    return pl.pallas_call(
        body, name="shard_bf16", grid=(pl.cdiv(SHARD, LANE),),
        in_specs=[pl.BlockSpec((D, LANE), lambda t: (t, 0))], out_specs=pl.BlockSpec((D, LANE), lambda t: (0, t)),
        out_shape=SDS((D, SHARD), BF16), compiler_params=_cp(("parallel",)),
    )(w_rows)


def _adamw_in(chip, w_rows, g_win, m_rows, v_rows):
    def body(j_ref, w_ref, g_ref, next_ref, m_ref, v_ref, grad_ref, d_ref, nm_ref, nv_ref):
        columns = _columns
        for s in range(4):
            @pl.when(j_ref[0] == s)
            def _():
                if s == 0:
                    grad_ref[...] = g_ref[...]
                else:
                    back = LANE - SHIFT * s
                    grad_ref[...] = jnp.where(_iota((D, LANE), 1) < back, pltpu.roll(g_ref[...], back, 1),
                                              pltpu.roll(next_ref[...], back, 1))
        gv = grad_ref[...]
        nm = ADAM_B1 * columns(m_ref) + (1.0 - ADAM_B1) * gv
        nv = ADAM_B2 * columns(v_ref) + (1.0 - ADAM_B2) * (gv * gv)
        m_hat = nm / (1.0 - ADAM_B1 ** ADAM_STEP)
        v_hat = nv / (1.0 - ADAM_B2 ** ADAM_STEP)
        d_ref[...] = -ADAM_LR * (m_hat / (jnp.sqrt(v_hat) + ADAM_EPS) + ADAM_WD * columns(w_ref))
        nm_ref[...] = nm
        nv_ref[...] = nv

    tile = pl.BlockSpec((D, LANE), lambda t, j: (0, t))
    next_tile = pl.BlockSpec((D, LANE), lambda t, j: (0, jnp.minimum(t + 1, TILES)))
    rows = pl.BlockSpec((D, LANE), lambda t, j: (t, 0))
    return pl.pallas_call(
        body, name="adamw_in",
        grid_spec=pltpu.PrefetchScalarGridSpec(
            num_scalar_prefetch=1, grid=(TILES + 1,), in_specs=[rows, tile, next_tile, rows, rows],
            out_specs=[tile] * 4),
        out_shape=[SDS((D, SHARD), F32)] * 4, compiler_params=_cp(("parallel",)),
    )(chip, w_rows, g_win, g_win, m_rows, v_rows)


def _rows128(a, rows):
    flat = a.reshape(-1)
    return jnp.pad(flat, (0, rows * LANE - flat.shape[0])).reshape(rows, LANE)


def _pack_small(conv_w, norm_pre, conv_b, ssm_norm, norm_post, dtb, alog, dsk, extra=None):
    cw_rows = 48 if conv_w.shape[-1] == 1536 else 16
    extra = jnp.zeros((1, LANE), F32) if extra is None else _rows128(extra, 1)
    vec = jnp.concatenate([_rows128(dtb, 1), _rows128(alog, 1), _rows128(dsk, 1), extra, jnp.zeros((4, LANE), F32)],
                          axis=0)
    return jnp.concatenate([_rows128(conv_w, cw_rows), _rows128(norm_pre, 8), _rows128(conv_b, 16),
                            _rows128(ssm_norm, 8), _rows128(norm_post, 8), vec], axis=0)


def _unpack_small(p, cw_cols):
    cw_rows = 48 if cw_cols == 1536 else 16
    o = cw_rows
    conv_w = p[:cw_rows].reshape(-1)[:4 * cw_cols].reshape(1, 4, cw_cols)
    norm_pre = p[o:o + 8].reshape(1, D)
    conv_b = p[o + 8:o + 24].reshape(-1)[:1536].reshape(1, 1536)
    ssm_norm = p[o + 24:o + 32].reshape(1, D)
    norm_post = p[o + 32:o + 40].reshape(1, D)
    vec = p[o + 40:o + 48]
    return conv_w, norm_pre, conv_b, ssm_norm, norm_post, vec[0:1, :NH], vec[1:2, :NH], vec[2:3, :NH], vec[3, 0]


def _pad_lanes(a):
    return jnp.pad(a, ((0, 0), (0, LANE - a.shape[1])))


class _GradReduce:
    LO, HI = ("q", "k", "v", "g"), ("g", "z", "x")

    def __init__(self, xi, yi, ci):
        self.cidx = jnp.reshape(ci, (1,)).astype(jnp.int32)
        self.place = jnp.stack([2 * xi + yi, ci, xi, yi]).astype(jnp.int32)

    def pairs(self, dw_g, dw_z, dw_x, dw_out):
        self.hi = [dw_g, dw_z, dw_x]
        self.go = dw_out.reshape(4, D // 2, D)
        return _PairExchange(self.HI, self.hi, (2, 3), [self.go])

    def first(self, got):
        rw, ro = got
        self.pw_hi = _pair_sum_windows(self.cidx, self.HI, self.hi, (2, 3), rw, "pair_sum_hi")
        self.po = _pair_sum(self.cidx, self.go, ro, "pair_sum_out")
        return _ChipExchange([self.pw_hi, self.po], [1, None])

    def first_done(self, got):
        self.rw_hi, self.ro = got

    def second(self, dw_q, dw_k, dw_v, dw_g):
        lo = [dw_q, dw_k, dw_v, dw_g]
        (rw,) = _exchange_call(_PairExchange(self.LO, lo, (0, 1)), "pair_exchange_lo")
        self.pw_lo = _pair_sum_windows(self.cidx, self.LO, lo, (0, 1), rw, "pair_sum_lo")
        return _ChipExchange([self.pw_lo], [0])

    def second_done(self, got):
        (self.rw_lo,) = got

    def result(self):
        half_in = _chip_sum_rows(self.place, self.rw_lo, self.pw_lo, self.rw_hi, self.pw_hi, "chip_sum_in")
        half_out = _chip_sum(self.place[0:2], self.ro, self.po, "chip_sum_out")
        return _half_exchange(half_in, half_out)


def kernel(x, norm_pre_w, w_in, conv_w, conv_b, dt_bias, a_log, d_skip, ssm_norm_w, w_out, norm_post_w, loss_target, m_norm_pre_w, m_w_in, m_conv_w, m_conv_b, m_dt_bias, m_a_log, m_d_skip, m_ssm_norm_w, m_w_out, m_norm_post_w, v_norm_pre_w, v_w_in, v_conv_w, v_conv_b, v_dt_bias, v_a_log, v_d_skip, v_ssm_norm_w, v_w_out, v_norm_post_w):
    xi, yi, ci = lax.axis_index("x"), lax.axis_index("y"), lax.axis_index("c")
    chip = 2 * xi + yi
    x2, tgt = x[0], loss_target[0]

    w_rows = _rowwise(w_in)
    gin = _gather_weights(_shard_bf16(w_rows))
    w_all = jnp.concatenate([gin[0], gin[1], gin[2], gin[3], jnp.zeros((D, DP - 4 * SHARD), BF16)], axis=1)
    reduce = _GradReduce(xi, yi, ci)
    grad_x, small = _local_step(x2, tgt, w_all, _LateGather(w_out[0].astype(BF16), conv_w[0]), norm_pre_w, conv_b,
                                dt_bias, a_log, d_skip, ssm_norm_w, norm_post_w, reduce)[:2]
    g_win, g_out = reduce.result()
    g_small = _slot_sum(_small_exchange(small), "small_sum")
    g_cw, g_npre, g_cb, g_nssm, g_npost, g_dtb, g_alog, g_dsk, loss = _unpack_small(g_small, 1536)
    g_cw = lax.dynamic_slice_in_dim(g_cw, chip * 384, 384, axis=2)

    g_in, d_in, nm_in, nv_in = _adamw_in(jnp.reshape(chip, (1,)).astype(jnp.int32), w_rows, g_win,
                                         _rowwise(m_w_in), _rowwise(v_w_in))
    d_out, nm_out, nv_out = _adamw(w_out[0], g_out, m_w_out[0], v_w_out[0], "adamw_out")
    packed = [_pack_small(*t) for t in (
        (conv_w, norm_pre_w, conv_b, ssm_norm_w, norm_post_w, dt_bias, a_log, d_skip),
        (g_cw, g_npre, g_cb, g_nssm, g_npost, g_dtb, g_alog, g_dsk),
        (m_conv_w, m_norm_pre_w, m_conv_b, m_ssm_norm_w, m_norm_post_w, m_dt_bias, m_a_log, m_d_skip),
        (v_conv_w, v_norm_pre_w, v_conv_b, v_ssm_norm_w, v_norm_post_w, v_dt_bias, v_a_log, v_d_skip))]
    small_out = [_unpack_small(p, 384)[:8] for p in _adamw(*packed, "adamw_small")]

    def ordered(cw_, npre, cb_, nssm, npost, dtb_, alog_, dsk_, big_in, big_out):
        return [npre, big_in[None], cw_, cb_, dtb_, alog_, dsk_, nssm, big_out[None], npost]

    grads = ordered(g_cw, g_npre, g_cb, g_nssm, g_npost, g_dtb, g_alog, g_dsk, g_in, g_out)
    deltas = ordered(*small_out[0], d_in, d_out)
    new_m = ordered(*small_out[1], nm_in, nm_out)
    new_v = ordered(*small_out[2], nv_in, nv_out)
    return (loss, grad_x[None], *grads, *deltas, *new_m, *new_v)


def _local_step(x2, tgt, w_all, late, norm_pre_w, conv_b, dt_bias, a_log, d_skip, ssm_norm_w,
                norm_post_w, reduce=None):
    dtb, alog = _pad_lanes(dt_bias), _pad_lanes(a_log)
    d_b = jnp.repeat(d_skip, 64, axis=1)

    if isinstance(late, _LateGather):
        (proj, u), (gout, gcw) = _inproj_fwd(x2, norm_pre_w, w_all, late)
        w_out_all = gout.reshape(2 * D, D)
        cw_all = jnp.concatenate([gcw[0], gcw[1], gcw[2], gcw[3]], axis=1)
    else:
        proj, u = _inproj_fwd(x2, norm_pre_w, w_all)
        w_out_all, cw_all = late
    mix, attn_pre, lse = _attn_fwd(proj, 1, _attn_fwd(proj, 4, _attn_fwd(proj, 16)), final=True)
    mix, y_save, states, conv_out = _ssm_fwd(proj, mix, cw_all, conv_b, dtb, alog, d_b, ssm_norm_w)

    dy, dn_ssm, do, delta, dg, dw_out, dnw_post, loss_part = _outproj_loss(mix, w_out_all, x2, tgt, norm_post_w,
                                                                          attn_pre, proj)
    dz, dxbcdt, dcw, dcb, dvec, dnw_ssm = _ssm_bwd(proj, dn_ssm, y_save, states, conv_out, cw_all, dtb, alog, d_b,
                                                   ssm_norm_w)
    dw_g, dw_z, dw_x = _dw(u, dg, "dw_in_g"), _dw(u, dz, "dw_in_z"), _dw(u, dxbcdt, "dw_in_xbcdt")
    acc = _attn_bwd(proj, do, lse, delta, 16, None, F32, reduce.pairs(dw_g, dw_z, dw_x, dw_out) if reduce else None)
    if reduce:
        acc, got = acc
    acc = _attn_bwd(proj, do, lse, delta, 4, acc, F32, reduce.first(got) if reduce else None)
    if reduce:
        acc, got = acc
        reduce.first_done(got)
    dq, dk, dv = _attn_bwd(proj, do, lse, delta, 1, acc, BF16)
    dw_q, dw_k, dw_v = _dw(u, dq, "dw_in_q"), _dw(u, dk, "dw_in_k"), _dw(u, dv, "dw_in_v")
    res = _inproj_bwd_dx([dq, dk, dv, dg, dz], dxbcdt, w_all, x2, dy, norm_pre_w,
                         reduce.second(dw_q, dw_k, dw_v, dw_g) if reduce else None)
    if reduce:
        res, got = res
        reduce.second_done(got)
    grad_x, dnw_pre = res
    dw_all = jnp.concatenate([dw_q, dw_k, dw_v, dw_g, dw_z, dw_x], axis=1)
    small = _pack_small(dcw, dnw_pre, dcb, dnw_ssm, dnw_post, dvec[0:1, :NH], dvec[1:2, :NH], dvec[2:3, :NH],
                        loss_part[:, :1])
    return grad_x, small, dw_all, dw_out
```

```python
import functools

import jax
import jax.numpy as jnp
from jax import lax
from jax.experimental import pallas as pl
from jax.experimental.pallas import tpu as pltpu

F32 = jnp.float32
BF16 = jnp.bfloat16
MESH = pl.DeviceIdType.MESH
SDS = jax.ShapeDtypeStruct
ANY = pl.BlockSpec(memory_space=pl.ANY)

S = 4096
D = 1024
DP = 7168
SHARD = 1668
OFF_G, OFF_Z = 3072, 4096
NH = 16
CH = 128
NC = S // CH
EPS = 1e-6
NEG = -1e30
LANE = 128
VMEM_LIMIT = 48 * 1024 * 1024

TILES = SHARD // LANE
WIN = (TILES + 1) * LANE
SHIFT = SHARD - TILES * LANE
SECTION_TILES = {"q": (0, 8), "k": (8, 8), "v": (16, 8), "g": (24, 8), "z": (32, 8), "x": (40, 16)}

ADAM_LR, ADAM_B1, ADAM_B2, ADAM_EPS, ADAM_WD, ADAM_STEP = 0.001, 0.9, 0.999, 1e-08, 0.01, 10


def _cp(sem, **kw):
    return pltpu.CompilerParams(dimension_semantics=sem, vmem_limit_bytes=VMEM_LIMIT, **kw)


def _dot(a, b):
    return jnp.dot(a, b, preferred_element_type=F32)


def _dot_nt(a, b):
    return lax.dot_general(a, b, (((1,), (1,)), ((), ())), preferred_element_type=F32)


def _dot_tn(a, b):
    return lax.dot_general(a, b, (((0,), (0,)), ((), ())), preferred_element_type=F32)


def _pieces(x, n):
    out = []
    for _ in range(n):
        p = x.astype(BF16)
        out.append(p)
        x = x - p.astype(F32)
    return out


def _pick(x, sel, n=2):
    parts = [_dot(p, sel) for p in _pieces(x, n)]
    return functools.reduce(jnp.add, parts)


def _pick_left(sel, x, n=3):
    parts = [_dot(sel, p) for p in _pieces(x, n)]
    return functools.reduce(jnp.add, parts)


def _sigmoid(v):
    return 0.5 * jnp.tanh(0.5 * v) + 0.5


def _iota(shape, dim):
    return lax.broadcasted_iota(jnp.int32, shape, dim)


def _inproj_fwd(x, nw, w_all, hosted=None):
    tm, tn = 1024, 1024
    n_host = len(hosted.arrays) if hosted else 0

    def body(x_ref, nw_ref, w_ref, *refs):
        host_in, (proj_ref, u_ref), refs = refs[:n_host], refs[n_host:n_host + 2], refs[n_host + 2:]
        host_out, host_sems = refs[:n_host], refs[n_host:]
        i, j = pl.program_id(0), pl.program_id(1)
        if hosted:
            pl.when((i == 0) & (j == 0))(lambda: hosted.start(host_in, host_out, host_sems))

        @pl.when(j == 0)
        def _():
            xf = x_ref[...]
            r = lax.rsqrt(jnp.mean(xf * xf, axis=-1, keepdims=True) + EPS)
            u_ref[...] = (xf * r * nw_ref[...]).astype(BF16)

        proj_ref[...] = _dot(u_ref[...], w_ref[...])
        if hosted:
            pl.when((i == S // tm - 1) & (j == DP // tn - 1))(lambda: hosted.finish(host_in, host_out, host_sems))

    outs = pl.pallas_call(
        body, name="inproj_fwd", grid=(S // tm, DP // tn),
        in_specs=[pl.BlockSpec((tm, D), lambda i, j: (i, 0)), pl.BlockSpec((1, D), lambda i, j: (0, 0)),
                  pl.BlockSpec((D, tn), lambda i, j: (0, j))] + [ANY] * n_host,
        out_specs=[pl.BlockSpec((tm, tn), lambda i, j: (i, j)), pl.BlockSpec((tm, D), lambda i, j: (i, 0))]
        + [ANY] * n_host,
        out_shape=[SDS((S, DP), F32), SDS((S, D), BF16)] + (hosted.out_shape if hosted else []),
        scratch_shapes=hosted.scratch if hosted else [],
        compiler_params=_cp(("arbitrary", "arbitrary") if hosted else ("parallel", "arbitrary")),
    )(x, nw, w_all, *(hosted.arrays if hosted else []))
    return (outs[:2], outs[2:]) if hosted else outs


ATTN_QB = {1: 16, 4: 4, 16: 1}


def _unit_rows(r, u, d):
    return pl.ds(r + d * CH * u, CH, stride=d) if d > 1 else pl.ds(CH * u, CH)


def _for_units(d, qb, fn):
    for r in range(d):
        for u in range(qb):
            fn(r, u)


def _attn_mask(has_prev):
    qi, kj = _iota((2 * CH, 2 * CH), 0) & (CH - 1), _iota((2 * CH, 2 * CH), 1)
    cur_ok = (kj >= CH) & (kj - CH <= qi)
    prev_ok = (kj < CH) & (kj >= qi)
    return cur_ok | (prev_ok & has_prev)


def _stack_heads(v, lane_a):
    return jnp.concatenate([jnp.where(lane_a, v, 0.0), jnp.where(lane_a, 0.0, v)], axis=0).astype(BF16)


def _attn_specs(d, qb):
    rows, prows = CH * d * qb, CH * d
    nb = S // rows
    steps = (NH // 2) * nb

    def at(t):
        t = jnp.minimum(t, steps - 1)
        return t % nb, t // nb

    def cur(off):
        return pl.BlockSpec((rows, LANE), lambda t: (at(t)[0], off + at(t)[1]))

    def prev(off):
        return pl.BlockSpec((prows, LANE), lambda t: (jnp.maximum(at(t)[0] * qb - 1, 0), off + at(t)[1]))

    lag = pl.BlockSpec((rows, LANE), lambda t: at(jnp.maximum(t - 1, 0)))
    return nb, steps, cur, prev, lag


def _gather16(src_ref, dense_ref, tmp_ref):
    for a in range(4):
        tmp_ref[...] = src_ref[pl.ds(a, 4 * CH, stride=4), :]
        for b in range(4):
            dense_ref[a + 4 * b] = tmp_ref[pl.ds(b, CH, stride=4), :]


def _scatter16(dense_ref, dst_ref, tmp_ref):
    for a in range(4):
        for b in range(4):
            tmp_ref[pl.ds(b, CH, stride=4), :] = dense_ref[a + 4 * b]
        dst_ref[pl.ds(a, 4 * CH, stride=4), :] = tmp_ref[...]


def _unit_index(r, u, d):
    return (r,) if d == 16 else (_unit_rows(r, u, d), slice(None))


def _unit_kv(p_ref, c_ref, r, u, d):
    prev = p_ref[_unit_index(r, 0, d)] if u == 0 else c_ref[_unit_index(r, u - 1, d)]
    return jnp.concatenate([prev, c_ref[_unit_index(r, u, d)]], axis=0).astype(BF16)


def _dense_scratch(d, n):
    return [pltpu.VMEM((16, CH, LANE), F32)] * n + [pltpu.VMEM((4 * CH, LANE), F32)] if d == 16 else []


def _attn_fwd(proj, d, prior=None, final=False):
    qb = ATTN_QB[d]
    nb, steps, cur, prev, _ = _attn_specs(d, qb)
    n_prior = 2 if prior is not None else 0
    n_in, n_out = 5 + n_prior + final, 2 + final
    assert not (d == 16 and (n_prior or final))

    def body(*refs):
        ins, outs, scratch = refs[:n_in], refs[n_in:n_in + n_out], refs[n_in + n_out:]
        if d == 16:
            tmp_ref = scratch[-1]
            for src, dense in zip(ins, scratch):
                _gather16(src, dense, tmp_ref)
            block_outs, ins, outs = outs, scratch[:n_in], scratch[n_in:n_in + n_out]
        q_ref, kp_ref, kc_ref, vp_ref, vc_ref = ins[:5]
        prior_refs = ins[5:5 + n_prior]
        if final:
            g_ref, (mix_ref, o_ref, l_ref) = ins[-1], outs
        else:
            o_ref, l_ref = outs
        i = pl.program_id(0) % nb
        lane_a = _iota((CH, LANE), 1) < 64
        mask_first, mask_rest = _attn_mask(i > 0), _attn_mask(True)

        def unit(r, u):
            at = _unit_index(r, u, d)
            q2 = _stack_heads(q_ref[at] * 0.125, lane_a)
            k2, v2 = _unit_kv(kp_ref, kc_ref, r, u, d), _unit_kv(vp_ref, vc_ref, r, u, d)
            s = jnp.where(mask_first if u == 0 else mask_rest, _dot_nt(q2, k2), NEG)
            m = jnp.max(s, axis=1, keepdims=True)
            p = jnp.exp(s - m)
            l = jnp.sum(p, axis=1, keepdims=True)
            o2 = _dot(p.astype(BF16), v2) / l
            lse2 = m + jnp.log(l)
            o = jnp.where(lane_a, o2[:CH], o2[CH:])
            lse = jnp.where(lane_a, lse2[:CH], lse2[CH:])
            if n_prior:
                o_a, l_a = prior_refs[0][at], prior_refs[1][at]
                top = jnp.maximum(l_a, lse)
                e_a, e_b = jnp.exp(l_a - top), jnp.exp(lse - top)
                tot = e_a + e_b
                o = (e_a * o_a + e_b * o) / tot
                lse = top + jnp.log(tot)
            o_ref[at] = o
            l_ref[at] = lse
            if final:
                g = g_ref[at]
                mix_ref[at] = (o * (g * _sigmoid(g))).astype(BF16)

        _for_units(d, qb, unit)
        if d == 16:
            for dense, dst in zip(outs, block_outs):
                _scatter16(dense, dst, tmp_ref)

    in_specs = [cur(0), prev(8), cur(8), prev(16), cur(16)] + [cur(0)] * n_prior
    args = [proj] * 5 + (list(prior) if n_prior else [])
    out_specs, out_shape = [cur(0), cur(0)], [SDS((S, D), F32), SDS((S, D), F32)]
    if final:
        assert d == 1
        in_specs.append(cur(OFF_G // LANE))
        args.append(proj)
        out_specs, out_shape = [cur(0)] + out_specs, [SDS((S, 2 * D), BF16)] + out_shape
    return pl.pallas_call(
        body, name=f"attn_fwd_d{d}", grid=(steps,),
        in_specs=in_specs, out_specs=out_specs, out_shape=out_shape,
        scratch_shapes=_dense_scratch(d, n_in + n_out),
        compiler_params=_cp(("parallel",)),
    )(*args)


def _attn_bwd(proj, do, lse, delta, d, acc, out_dtype, hosted=None):
    qb = ATTN_QB[d]
    nb, steps, cur, prev, lag = _attn_specs(d, qb)
    has_acc = acc is not None
    n_in = 11 if has_acc else 8
    n_host, n_host_out = (len(hosted.arrays), len(hosted.out_shape)) if hosted else (0, 0)
    assert not (d == 16 and (has_acc or out_dtype != F32))
    rows = CH * d * qb
    carry = (2, 16, CH, LANE) if d == 16 else (2, rows, LANE)

    def body(*refs):
        ins, host_in, refs = refs[:n_in], refs[n_in:n_in + n_host], refs[n_in + n_host:]
        (dq_ref, dk_ref, dv_ref), host_out, scratch = refs[:3], refs[3:3 + n_host_out], refs[3 + n_host_out:]
        if hosted:
            scratch, host_sems = scratch[:-len(hosted.scratch)], scratch[-len(hosted.scratch):]
        ck_ref, cv_ref = scratch[:2]
        dq_f32 = dq_ref if out_dtype == F32 else scratch[2]
        t = pl.program_id(0)
        i = t % nb
        if hosted:
            pl.when(t == 0)(lambda: hosted.start(host_in, host_out, host_sems))
        if d == 16:
            dense, dq_f32, tmp_ref = scratch[2:2 + n_in], scratch[2 + n_in], scratch[-1]

            @pl.when(t < steps)
            def _():
                for src, dst in zip(ins, dense):
                    _gather16(src, dst, tmp_ref)

            ins = dense
        q_ref, kp_ref, kc_ref, vp_ref, vc_ref, do_ref, lse_ref, dl_ref = ins[:8]
        if has_acc:
            aq_ref, ak_ref, av_ref = ins[8:11]
        slot = t & 1
        now_k, now_v, old_k, old_v = ck_ref.at[slot], cv_ref.at[slot], ck_ref.at[1 - slot], cv_ref.at[1 - slot]
        lane_a = _iota((CH, LANE), 1) < 64
        mask_first, mask_rest = _attn_mask(i > 0), _attn_mask(True)

        @pl.when(t == 0)
        def _():
            ck_ref[1] = jnp.zeros(carry[1:], F32)
            cv_ref[1] = jnp.zeros(carry[1:], F32)

        def unit(r, u):
            at = _unit_index(r, u, d)
            q2 = _stack_heads(q_ref[at] * 0.125, lane_a)
            do2 = _stack_heads(do_ref[at], lane_a)
            k2, v2 = _unit_kv(kp_ref, kc_ref, r, u, d), _unit_kv(vp_ref, vc_ref, r, u, d)
            lsev, dlv = lse_ref[at], dl_ref[at]
            lse2 = jnp.concatenate([lsev[:, 0:1], lsev[:, 64:65]], axis=0)
            dl2 = jnp.concatenate([dlv[:, 0:1], dlv[:, 64:65]], axis=0)
            p = jnp.exp(jnp.where(mask_first if u == 0 else mask_rest, _dot_nt(q2, k2), NEG) - lse2)
            ds = (p * (_dot_nt(do2, v2) - dl2)).astype(BF16)
            dq2 = _dot(ds, k2)
            dk2 = _dot_tn(ds, q2)
            dv2 = _dot_tn(p.astype(BF16), do2)
            dq = jnp.where(lane_a, dq2[:CH], dq2[CH:]) * 0.125
            if has_acc:
                dq = dq + aq_ref[at]
            dq_f32[at] = dq
            if u == 0:
                before = _unit_index(r, qb - 1, d)
                old_k[before] += dk2[:CH]
                old_v[before] += dv2[:CH]
            else:
                before = _unit_index(r, u - 1, d)
                now_k[before] += dk2[:CH]
                now_v[before] += dv2[:CH]
            now_k[at] = dk2[CH:]
            now_v[at] = dv2[CH:]

        @pl.when(t < steps)
        def _():
            _for_units(d, qb, unit)
            if d == 16:
                _scatter16(dq_f32, dq_ref, tmp_ref)
            elif out_dtype != F32:
                dq_ref[...] = dq_f32[...].astype(out_dtype)

        if d == 16:
            _scatter16(old_k, dk_ref, tmp_ref)
            _scatter16(old_v, dv_ref, tmp_ref)
        else:
            dk, dv = old_k[...], old_v[...]
            if has_acc:
                dk, dv = dk + ak_ref[...], dv + av_ref[...]
            dk_ref[...] = dk.astype(out_dtype)
            dv_ref[...] = dv.astype(out_dtype)
        if hosted:
            pl.when(t == steps)(lambda: hosted.finish(host_in, host_out, host_sems))

    in_specs = [cur(0), prev(8), cur(8), prev(16), cur(16), cur(0), cur(0), cur(0)]
    args = [proj, proj, proj, proj, proj, do, lse, delta]
    if has_acc:
        in_specs += [cur(0), lag, lag]
        args += list(acc)
    scratch = [pltpu.VMEM(carry, F32), pltpu.VMEM(carry, F32)]
    if d == 16:
        scratch += _dense_scratch(d, n_in + 1)
    elif out_dtype != F32:
        scratch.append(pltpu.VMEM((rows, LANE), F32))
    out_specs, out_shape = [cur(0), lag, lag], [SDS((S, D), out_dtype)] * 3
    if hosted:
        args += hosted.arrays
        in_specs += [ANY] * n_host
        out_specs += [ANY] * n_host_out
        out_shape += hosted.out_shape
        scratch += hosted.scratch
    outs = pl.pallas_call(
        body, name=f"attn_bwd_d{d}", grid=(steps + 1,),
        in_specs=in_specs, out_specs=out_specs, out_shape=out_shape,
        scratch_shapes=scratch, compiler_params=_cp(("arbitrary",)),
    )(*args)
    return (outs[:3], outs[3:]) if hosted else outs


def _conv_taps(cur, prev8, first):
    row8 = _iota(prev8.shape, 0)
    prev8 = jnp.where(first, 0.0, prev8)
    taps = []
    for s in (3, 2, 1):
        rolled = pltpu.roll(cur, s, 0)
        head = jnp.where(row8 < s, pltpu.roll(prev8, s, 0), rolled[:8])
        taps.append(jnp.concatenate([head, rolled[8:]], axis=0))
    return taps + [cur]


def _conv(taps, w, b):
    acc = b + w[0:1, :] * taps[0]
    for k in (1, 2, 3):
        acc = acc + w[k:k + 1, :] * taps[k]
    return acc


def _expand():
    return (_iota((LANE, D), 1) // 64 == _iota((LANE, D), 0)).astype(BF16)


def _reduce():
    return (_iota((D, LANE), 0) // 64 == _iota((D, LANE), 1)).astype(BF16)


def _ssd_common(xs_c, bc_c, dt_raw, dtb, alog):
    head_lane = _iota((CH, LANE), 1) < NH
    xs = xs_c * _sigmoid(xs_c)
    bc = bc_c * _sigmoid(bc_c)
    pre = dt_raw + dtb
    dt = jnp.where(head_lane, jnp.maximum(pre, 0.0) + jnp.log(1.0 + jnp.exp(-jnp.abs(pre))), 0.0)
    a_row = jnp.where(head_lane[0:1], -jnp.exp(alog), 0.0)
    tri = (_iota((CH, CH), 1) <= _iota((CH, CH), 0)).astype(BF16)
    cs = _pick_left(tri, dt * a_row)
    cs_last = cs[CH - 1:CH, :]
    wide = _pick(jnp.concatenate([dt, jnp.exp(cs), jnp.exp(cs_last - cs)], axis=0), _expand())
    dt_b, e_b, f_b = wide[:CH], wide[CH:2 * CH], wide[2 * CH:]
    return dict(xs=xs, bc=bc, pre=pre, dt=dt, a_row=a_row, cs=cs, cs_t=cs.T, dt_b=dt_b, e_b=e_b, f_b=f_b,
                t_b=e_b[CH - 1:CH, :])


def _groups(bc):
    bcb = bc.astype(BF16)
    return [bcb[:, 0:128], bcb[:, 128:256]], [bcb[:, 256:384], bcb[:, 384:512]]


def _decay(q, h, tril):
    seg = q["cs"][:, h:h + 1] - q["cs_t"][h:h + 1, :]
    return jnp.exp(jnp.where(tril, seg, NEG))


def _ssm_fwd(proj, mix, cw, cb, dtb, alog, d_b, nw):
    def body(xs_ref, xsp_ref, bc_ref, bcp_ref, dt_ref, z_ref, cw_ref, cb_ref, dtb_ref, alog_ref, db_ref, nw_ref,
             mix_in_ref, mix_ref, y_ref, st_ref, conv_ref, h_ref):
        del mix_in_ref
        i = pl.program_id(0)

        @pl.when(i == 0)
        def _():
            h_ref[...] = jnp.zeros_like(h_ref)

        cw, cb = cw_ref[...], cb_ref[...]
        xs_c = _conv(_conv_taps(xs_ref[...], xsp_ref[...], i == 0), cw[:, :D], cb[:, :D])
        bc_c = _conv(_conv_taps(bc_ref[...], bcp_ref[...], i == 0), cw[:, D:], cb[:, D:])
        conv_ref[:, :D] = xs_c
        conv_ref[:, D:] = bc_c
        q = _ssd_common(xs_c, bc_c, dt_ref[...], dtb_ref[...], alog_ref[...])
        bg, cg = _groups(q["bc"])
        xs = q["xs"]
        xdt = xs * q["dt_b"]
        xdt_b = xdt.astype(BF16)
        h_in = h_ref[...]
        st_ref[...] = h_in
        hb = h_in.astype(BF16)
        tril = _iota((CH, CH), 1) <= _iota((CH, CH), 0)
        lane_a = _iota((CH, LANE), 1) < 64
        cbm = [_dot_nt(cg[g], bg[g]) for g in range(2)]
        pairs = []
        for hp in range(NH // 2):
            xp = xdt_b[:, hp * LANE:(hp + 1) * LANE]
            ya = _dot((cbm[hp // 4] * _decay(q, 2 * hp, tril)).astype(BF16), xp)
            yb = _dot((cbm[hp // 4] * _decay(q, 2 * hp + 1, tril)).astype(BF16), xp)
            pairs.append(jnp.where(lane_a, ya, yb))
        y_diag = jnp.concatenate(pairs, axis=1)
        y_off = jnp.concatenate([_dot(cg[g], hb[:, g * 512:(g + 1) * 512]) for g in range(2)], axis=1) * q["e_b"]
        y = y_diag + y_off + db_ref[...] * xs
        y_ref[...] = y
        xf = (xdt * q["f_b"]).astype(BF16)
        h_ref[...] = q["t_b"] * h_in + jnp.concatenate(
            [_dot_tn(bg[g], xf[:, g * 512:(g + 1) * 512]) for g in range(2)], axis=1)
        z = z_ref[...]
        yz = y * (z * _sigmoid(z))
        outs = []
        for g in range(2):
            v = yz[:, g * 512:(g + 1) * 512]
            outs.append(v * lax.rsqrt(jnp.mean(v * v, axis=-1, keepdims=True) + EPS))
        mix_ref[...] = (jnp.concatenate(outs, axis=1) * nw_ref[...]).astype(BF16)

    def col(width, blk, prev=False):
        if prev:
            return pl.BlockSpec((8, width), lambda i: (jnp.maximum(i * (CH // 8) - 1, 0), blk))
        return pl.BlockSpec((CH, width), lambda i: (i, blk))

    def full(a):
        return pl.BlockSpec(a.shape, lambda i: (0,) * a.ndim)

    return pl.pallas_call(
        body, name="ssm_fwd", grid=(NC,),
        in_specs=[col(D, 5), col(D, 5, True), col(512, 12), col(512, 12, True), col(LANE, 52), col(D, 4),
                  full(cw), full(cb), full(dtb), full(alog), full(d_b), full(nw), ANY],
        out_specs=[col(D, 1), col(D, 0), pl.BlockSpec((None, CH, D), lambda i: (i, 0, 0)), col(D + 512, 0)],
        out_shape=[SDS((S, 2 * D), BF16), SDS((S, D), F32), SDS((NC, CH, D), F32), SDS((S, D + 512), F32)],
        scratch_shapes=[pltpu.VMEM((CH, D), F32)],
        input_output_aliases={12: 0},
        compiler_params=_cp(("arbitrary",)),
    )(proj, proj, proj, proj, proj, proj, cw, cb, dtb, alog, d_b, nw, mix)


def _ssm_bwd(proj, dn, y_save, states, conv_out, cw, dtb, alog, d_b, nw):
    def body(xs_ref, bc_ref, dt_ref, z_ref, dn_ref, y_ref, st_ref, conv_ref,
             cw_ref, dtb_ref, alog_ref, db_ref, nw_ref,
             dz_ref, dx_ref, dcw_ref, dcb_ref, dsm_ref, dnw_ref, dh_ref, nxs_ref, nbc_ref):
        i = pl.program_id(0)
        ci = NC - 1 - i

        @pl.when(i == 0)
        def _():
            for ref in (dcw_ref, dcb_ref, dsm_ref, dnw_ref, dh_ref, nxs_ref, nbc_ref):
                ref[...] = jnp.zeros_like(ref)

        cw = cw_ref[...]
        xs_c, bc_c = conv_ref[:, :D], conv_ref[:, D:]
        q = _ssd_common(xs_c, bc_c, dt_ref[...], dtb_ref[...], alog_ref[...])
        bg, cg = _groups(q["bc"])
        xs, dt_b, e_b, f_b, t_b = q["xs"], q["dt_b"], q["e_b"], q["f_b"], q["t_b"]
        xdt = xs * dt_b
        xdt_b = xdt.astype(BF16)
        h_in = st_ref[...]
        hb = h_in.astype(BF16)
        dh_new = dh_ref[...]
        dhb = dh_new.astype(BF16)
        red = _reduce()

        z, y, dn, nw_v = z_ref[...], y_ref[...], dn_ref[...], nw_ref[...]
        sig = _sigmoid(z)
        sz = z * sig
        yz = y * sz
        gdn = dn * nw_v
        dyz, dnw = [], []
        for g in range(2):
            v, gv = yz[:, g * 512:(g + 1) * 512], gdn[:, g * 512:(g + 1) * 512]
            r = lax.rsqrt(jnp.mean(v * v, axis=-1, keepdims=True) + EPS)
            dnw.append(dn[:, g * 512:(g + 1) * 512] * v * r)
            dyz.append(r * (gv - v * (r * r) * jnp.mean(gv * v, axis=-1, keepdims=True)))
        dyz = jnp.concatenate(dyz, axis=1)
        dnw_ref[...] += jnp.sum(jnp.concatenate(dnw, axis=1), axis=0, keepdims=True)
        dy = dyz * sz
        dz_ref[...] = (dyz * y * (sig * (1.0 + z * (1.0 - sig)))).astype(BF16)
        dy_b = dy.astype(BF16)

        tril = _iota((CH, CH), 1) <= _iota((CH, CH), 0)
        lane_a = _iota((CH, LANE), 1) < 64
        cbm = [_dot_nt(cg[g], bg[g]) for g in range(2)]
        dcbm = [jnp.zeros((CH, CH), F32), jnp.zeros((CH, CH), F32)]
        seg_rows = jnp.zeros((CH, LANE), F32)
        seg_cols = jnp.zeros((LANE, CH), F32)
        row_id, col_id = _iota((CH, LANE), 0), _iota((CH, LANE), 1)
        dx_pairs = []
        for hp in range(NH // 2):
            g = hp // 4
            xp = xdt_b[:, hp * LANE:(hp + 1) * LANE]
            dyp_f = dy[:, hp * LANE:(hp + 1) * LANE]
            dyp = dy_b[:, hp * LANE:(hp + 1) * LANE]
            halves = []
            for k in range(2):
                h = 2 * hp + k
                lane = lane_a if k == 0 else jnp.logical_not(lane_a)
                dec = _decay(q, h, tril)
                gm = cbm[g] * dec
                dgm = _dot_nt(jnp.where(lane, dyp_f, 0.0).astype(BF16), xp)
                dcbm[g] = dcbm[g] + dgm * dec
                prod = dgm * gm
                seg_rows = jnp.where(col_id == h, jnp.sum(prod, axis=1, keepdims=True), seg_rows)
                seg_cols = jnp.where(row_id == h, jnp.sum(prod, axis=0, keepdims=True), seg_cols)
                halves.append(_dot_tn(gm.astype(BF16), dyp))
            dx_pairs.append(jnp.where(lane_a, halves[0], halves[1]))
        dxdt_diag = jnp.concatenate(dx_pairs, axis=1)

        qv = jnp.concatenate([_dot(bg[g], dhb[:, g * 512:(g + 1) * 512]) for g in range(2)], axis=1)
        y_off = jnp.concatenate([_dot(cg[g], hb[:, g * 512:(g + 1) * 512]) for g in range(2)], axis=1) * e_b
        xfq = xdt * f_b * qv
        dxdt = dxdt_diag + f_b * qv
        tdt = jnp.sum(dh_new * h_in, axis=0, keepdims=True) * t_b
        per_head = _pick(jnp.concatenate([xfq, dy * y_off, dxdt * xs, dy * xs, jnp.broadcast_to(tdt, (8, D))],
                                         axis=0), red)
        fdf, dyoff_h, dxdtxs_h, dyxs_h = [per_head[k * CH:(k + 1) * CH] for k in range(4)]
        dcs = seg_rows - seg_cols.T + dyoff_h - fdf
        last = per_head[4 * CH:4 * CH + 1] + jnp.sum(fdf, axis=0, keepdims=True)
        dcs = dcs + jnp.where(_iota((CH, LANE), 0) == CH - 1, last, 0.0)
        tri_t = (_iota((CH, CH), 1) >= _iota((CH, CH), 0)).astype(BF16)
        da = _pick_left(tri_t, dcs)
        ddt = da * q["a_row"] + dxdtxs_h
        dxs = dxdt * dt_b + db_ref[...] * dy
        ddt_raw = ddt * _sigmoid(q["pre"])
        dsm_ref[0:1, :] += jnp.sum(ddt_raw, axis=0, keepdims=True)
        dsm_ref[1:2, :] += jnp.sum(da * q["dt"], axis=0, keepdims=True) * q["a_row"]
        dsm_ref[2:3, :] += jnp.sum(dyxs_h, axis=0, keepdims=True)
        edy = (e_b * dy).astype(BF16)
        xf = (xdt * f_b).astype(BF16)
        dbs, dcs_g, dhs = [], [], []
        for g in range(2):
            sl = slice(g * 512, (g + 1) * 512)
            dcb_b = dcbm[g].astype(BF16)
            dcs_g.append(_dot(dcb_b, bg[g]) + _dot_nt(edy[:, sl], hb[:, sl]))
            dbs.append(_dot_tn(dcb_b, cg[g]) + _dot_nt(xf[:, sl], dhb[:, sl]))
            dhs.append(_dot_tn(cg[g], edy[:, sl]))
        dh_ref[...] = t_b * dh_new + jnp.concatenate(dhs, axis=1)
        dbc = jnp.concatenate(dbs + dcs_g, axis=1)

        def conv_bwd(dact, pre, x_raw, w, nxt_ref, lo):
            s = _sigmoid(pre)
            dconv = dact * (s * (1.0 + pre * (1.0 - s)))
            nxt8 = nxt_ref[...]
            row8 = _iota(nxt8.shape, 0)
            hi = lo + dconv.shape[1]
            dcb_ref[:, lo:hi] += jnp.sum(dconv, axis=0, keepdims=True)
            later = [dconv]
            for s_ in (1, 2, 3):
                rolled = pltpu.roll(dconv, CH - s_, 0)
                tail = jnp.where(row8 >= 8 - s_, pltpu.roll(nxt8, 8 - s_, 0), rolled[CH - 8:])
                later.append(jnp.concatenate([rolled[:CH - 8], tail], axis=0))
            dx = None
            for s_, up in enumerate(later):
                k = 3 - s_
                dcw_ref[k:k + 1, lo:hi] += jnp.sum(up * x_raw, axis=0, keepdims=True)
                dx = w[k:k + 1, :] * up if dx is None else dx + w[k:k + 1, :] * up
            nxt_ref[...] = dconv[:8]
            return dx

        dx_ref[:, 0:D] = conv_bwd(dxs, xs_c, xs_ref[...], cw[:, :D], nxs_ref, 0).astype(BF16)
        dx_ref[:, D:D + 512] = conv_bwd(dbc, bc_c, bc_ref[...], cw[:, D:], nbc_ref, D).astype(BF16)
        dx_ref[:, D + 512:D + 640] = ddt_raw.astype(BF16)
        dx_ref[:, D + 640:] = jnp.zeros((CH, D - 640), BF16)

    def col(width, blk):
        return pl.BlockSpec((CH, width), lambda i: (NC - 1 - i, blk))

    def full(a):
        return pl.BlockSpec(a.shape, lambda i: (0,) * len(a.shape))

    acc_shapes = [SDS((4, 1536), F32), SDS((1, 1536), F32), SDS((8, LANE), F32), SDS((1, D), F32)]
    return pl.pallas_call(
        body, name="ssm_bwd", grid=(NC,),
        in_specs=[col(D, 5), col(512, 12), col(LANE, 52), col(D, 4),
                  col(D, 0), col(D, 0), pl.BlockSpec((None, CH, D), lambda i: (NC - 1 - i, 0, 0)), col(D + 512, 0),
                  full(cw), full(dtb), full(alog), full(d_b), full(nw)],
        out_specs=[col(D, 0), col(2 * D, 0)] + [full(a) for a in acc_shapes],
        out_shape=[SDS((S, D), BF16), SDS((S, 2 * D), BF16)] + acc_shapes,
        scratch_shapes=[pltpu.VMEM((CH, D), F32), pltpu.VMEM((8, D), F32), pltpu.VMEM((8, 512), F32)],
        compiler_params=_cp(("arbitrary",)),
    )(proj, proj, proj, proj, dn, y_save, states, conv_out, cw, dtb, alog, d_b, nw)


def _outproj_loss(mix, w_out, x, tgt, nw, attn_pre, proj):
    tm = 256

    def body(mix_ref, w_ref, x_ref, t_ref, nw_ref, pre_ref, g_ref,
             dy_ref, dn_ref, do_ref, delta_ref, dg_ref, dw_ref, dnw_ref, loss_ref):
        @pl.when(pl.program_id(0) == 0)
        def _():
            dw_ref[...] = jnp.zeros_like(dw_ref)
            dnw_ref[...] = jnp.zeros_like(dnw_ref)
            loss_ref[...] = jnp.zeros_like(loss_ref)

        mixv, w = mix_ref[...], w_ref[...]
        out = _dot(mixv, w)
        r = lax.rsqrt(jnp.mean(out * out, axis=-1, keepdims=True) + EPS)
        nh = out * r
        nw_v = nw_ref[...]
        err = x_ref[...] + nh * nw_v - t_ref[...]
        loss_ref[...] += 0.5 * jnp.sum(jnp.mean(err * err, axis=-1, keepdims=True), axis=0, keepdims=True)
        dy = err * (1.0 / D)
        dy_ref[...] = dy
        dnw_ref[...] += jnp.sum(dy * nh, axis=0, keepdims=True)
        gdn = dy * nw_v
        dout = (r * (gdn - nh * jnp.mean(gdn * nh, axis=-1, keepdims=True))).astype(BF16)
        dmix = _dot_nt(dout, w)
        dw_ref[...] += _dot_tn(mixv, dout)
        dn_ref[...] = dmix[:, D:]
        dm, g, pre_v = dmix[:, :D], g_ref[...], pre_ref[...]
        sig = _sigmoid(g)
        do = dm * (g * sig)
        do_ref[...] = do
        dg_ref[...] = (dm * pre_v * (sig * (1.0 + g * (1.0 - sig)))).astype(BF16)
        prod = do * pre_v
        same_head = (_iota((LANE, LANE), 0) // 64 == _iota((LANE, LANE), 1) // 64).astype(BF16)
        for cb in range(D // LANE):
            delta_ref[:, cb * LANE:(cb + 1) * LANE] = _pick(prod[:, cb * LANE:(cb + 1) * LANE], same_head)

    row = lambda w: pl.BlockSpec((tm, w), lambda i: (i, 0))
    full = lambda s: pl.BlockSpec(s, lambda i: (0, 0))
    return pl.pallas_call(
        body, name="outproj_loss", grid=(S // tm,),
        in_specs=[row(2 * D), full((2 * D, D)), row(D), row(D), full((1, D)), row(D),
                  pl.BlockSpec((tm, D), lambda i: (i, OFF_G // D))],
        out_specs=[row(D), row(D), row(D), row(D), row(D), full((2 * D, D)), full((1, D)), full((1, LANE))],
        out_shape=[SDS((S, D), F32)] * 4 + [SDS((S, D), BF16), SDS((2 * D, D), F32), SDS((1, D), F32),
                                            SDS((1, LANE), F32)],
        compiler_params=_cp(("arbitrary",)),
    )(mix, w_out, x, tgt, nw, attn_pre, proj)


def _inproj_bwd_dx(srcs, dxbcdt, w_all, x, dy, nw, hosted=None):
    tm = 512
    nk = DP // D
    n_host = len(hosted.arrays) if hosted else 0

    def body(*refs):
        src_refs = refs[:nk]
        w_ref, x_ref, dy_ref, nw_ref = refs[nk:nk + 4]
        host_in, refs = refs[nk + 4:nk + 4 + n_host], refs[nk + 4 + n_host:]
        gx_ref, dnw_ref = refs[:2]
        host_out, host_sems = refs[2:2 + n_host], refs[2 + n_host:]
        i = pl.program_id(0)

        @pl.when(i == 0)
        def _():
            if hosted:
                hosted.start(host_in, host_out, host_sems)
            dnw_ref[...] = jnp.zeros_like(dnw_ref)

        du = None
        for k, ref in enumerate(src_refs):
            part = _dot_nt(ref[...], w_ref[:, k * D:(k + 1) * D])
            du = part if du is None else du + part
        xf, nw_v = x_ref[...], nw_ref[...]
        r = lax.rsqrt(jnp.mean(xf * xf, axis=-1, keepdims=True) + EPS)
        xh = xf * r
        dnw_ref[...] += jnp.sum(du * xh, axis=0, keepdims=True)
        gdu = du * nw_v
        gx_ref[...] = r * (gdu - xh * jnp.mean(gdu * xh, axis=-1, keepdims=True)) + dy_ref[...]

        if hosted:
            pl.when(i == S // tm - 1)(lambda: hosted.finish(host_in, host_out, host_sems))

    row = pl.BlockSpec((tm, D), lambda i: (i, 0))
    row1 = pl.BlockSpec((tm, D), lambda i: (i, 1))
    one = pl.BlockSpec((1, D), lambda i: (0, 0))
    whole_w = pl.BlockSpec((D, DP), lambda i: (0, 0), pipeline_mode=pl.Buffered(1))
    args = [*srcs, dxbcdt, dxbcdt, w_all, x, dy, nw]
    in_specs = [row] * len(srcs) + [row, row1, whole_w, row, row, one]
    out_specs, out_shape, scratch = [row, one], [SDS((S, D), F32), SDS((1, D), F32)], []
    if hosted:
        args += hosted.arrays
        in_specs += [ANY] * n_host
        out_specs += [ANY] * n_host
        out_shape += hosted.out_shape
        scratch += hosted.scratch
    outs = pl.pallas_call(
        body, name="inproj_bwd_dx", grid=(S // tm,),
        in_specs=in_specs, out_specs=out_specs, out_shape=out_shape, scratch_shapes=scratch,
        compiler_params=_cp(("arbitrary",)),
    )(*args)
    return (outs[:2], outs[2:]) if hosted else outs


def _dw(u, dsec, name):
    ts = 1024
    ncol = dsec.shape[1] // D

    def body(u_ref, d_ref, o_ref):
        @pl.when(pl.program_id(1) == 0)
        def _():
            o_ref[...] = jnp.zeros_like(o_ref)

        o_ref[...] += _dot_tn(u_ref[...], d_ref[...])

    return pl.pallas_call(
        body, name=name, grid=(ncol, S // ts),
        in_specs=[pl.BlockSpec((ts, D), lambda j, i: (i, 0)), pl.BlockSpec((ts, D), lambda j, i: (i, j))],
        out_specs=pl.BlockSpec((D, D), lambda j, i: (0, j)),
        out_shape=SDS((D, ncol * D), F32),
        compiler_params=_cp(("parallel", "arbitrary")),
    )(u, dsec)


def _place():
    x, y, c = lax.axis_index("x"), lax.axis_index("y"), lax.axis_index("c")
    return x, y, c, 2 * x + y


def _chip_of(x, y, k):
    px = 1 - x if k & 2 else x
    py = 1 - y if k & 1 else y
    return px, py, 2 * px + py


def _remote(src, dst, send_sem, recv_sem, dev):
    return pltpu.make_async_remote_copy(src_ref=src, dst_ref=dst, send_sem=send_sem, recv_sem=recv_sem,
                                        device_id=dev, device_id_type=MESH)


def _gather_weights(w_in_b):
    half = w_in_b.shape[0] // 2
    quarter = half // 2

    def body(src, dst, send, recv):
        x, y, c, j = _place()
        me, sib = (x, y, c), (x, y, 1 - c)
        nbr = {"x": _chip_of(x, y, 2), "y": _chip_of(x, y, 1)}
        diag = _chip_of(x, y, 3)[2]
        started, arrivals = [], []

        def rows(n_quarter=None, sibling=False):
            base = (1 - c if sibling else c) * half
            return pl.ds(base, half) if n_quarter is None else pl.ds(base + n_quarter * quarter, quarter)

        def sem(n):
            return send.at[n], recv.at[n]

        def go(cp):
            cp.start()
            started.append(cp)

        own = _remote(src, dst.at[j], *sem(8), sib)
        go(own)
        for n, axis in enumerate("xy"):
            px, py, _ = nbr[axis]
            go(_remote(src.at[rows()], dst.at[j, rows()], *sem(n), (px, py, c)))
        for n, axis in enumerate("xy"):
            ox, oy, _ = nbr["y" if axis == "x" else "x"]
            pj = nbr[axis][2]
            _remote(src.at[rows()], dst.at[pj, rows()], *sem(n), me).wait_recv()
            go(_remote(dst.at[pj, rows(n)], dst.at[pj, rows(n)], *sem(2 + n), (ox, oy, c)))
            go(_remote(dst.at[pj, rows()], dst.at[pj, rows()], *sem(4 + n), sib))
            arrivals.append(_remote(src.at[rows()], dst.at[pj, rows(None, True)], *sem(4 + n), me))
        for n in range(2):
            _remote(dst.at[diag, rows(n)], dst.at[diag, rows(n)], *sem(2 + n), me).wait_recv()
            go(_remote(dst.at[diag, rows(n)], dst.at[diag, rows(n)], *sem(6 + n), sib))
            arrivals.append(_remote(dst.at[diag, rows(n, True)], dst.at[diag, rows(n, True)], *sem(6 + n), me))
        for cp in arrivals + [own]:
            cp.wait_recv()
        for cp in started:
            cp.wait_send()

    return pl.pallas_call(
        body, name="gather_weights", in_specs=[ANY], out_specs=ANY,
        out_shape=SDS((4,) + w_in_b.shape, BF16),
        scratch_shapes=[pltpu.SemaphoreType.DMA((9,)), pltpu.SemaphoreType.DMA((9,))],
        compiler_params=pltpu.CompilerParams(has_side_effects=True),
    )(w_in_b)


class _LateGather:
    def __init__(self, w_out_b, conv_w):
        self.arrays = [w_out_b, conv_w]
        self.out_shape = [SDS((4,) + w_out_b.shape, BF16), SDS((4,) + conv_w.shape, F32)]
        self.scratch = [pltpu.SemaphoreType.DMA((11,)), pltpu.SemaphoreType.DMA((11,))]

    def _plan(self, ins, outs, sems):
        x, y, c, j = _place()
        send, recv = sems
        (wo, cw), (gwo, gcw) = ins, outs
        half = wo.shape[0] // 2
        mine, theirs = pl.ds(c * half, half), pl.ds((1 - c) * half, half)
        me, sib = (x, y, c), (x, y, 1 - c)
        first, arrive, forward, last = [], [], [], []
        for k in (1, 2, 3):
            px, py, pj = _chip_of(x, y, k)
            first += [_remote(wo.at[mine], gwo.at[j, mine], send.at[k - 1], recv.at[k - 1], (px, py, c)),
                      _remote(cw, gcw.at[j], send.at[k + 2], recv.at[k + 2], (px, py, c))]
            arrive.append(_remote(wo.at[mine], gwo.at[pj, mine], send.at[k - 1], recv.at[k - 1], me))
            forward.append(_remote(gwo.at[pj, mine], gwo.at[pj, mine], send.at[k + 5], recv.at[k + 5], sib))
            last += [_remote(cw, gcw.at[pj], send.at[k + 2], recv.at[k + 2], me),
                     _remote(wo.at[theirs], gwo.at[pj, theirs], send.at[k + 5], recv.at[k + 5], me)]
        first += [_remote(wo, gwo.at[j], send.at[9], recv.at[9], sib),
                  _remote(cw, gcw.at[j], send.at[10], recv.at[10], sib)]
        last += first[-2:]
        return first, arrive, forward, last

    def start(self, ins, outs, sems):
        for cp in self._plan(ins, outs, sems)[0]:
            cp.start()

    def finish(self, ins, outs, sems):
        first, arrive, forward, last = self._plan(ins, outs, sems)
        for got, fwd in zip(arrive, forward):
            got.wait_recv()
            fwd.start()
        for cp in last:
            cp.wait_recv()
        for cp in first + forward:
            cp.wait_send()


def _window(s, names):
    lo, hi = TILES * s, TILES * s + TILES + 1
    pieces = []
    for n, name in enumerate(names):
        a, count = SECTION_TILES[name]
        first, last = max(lo, a), min(hi, a + count)
        if first < last:
            pieces.append((n, first - a, last - first, first - lo))
    assert sum(p[2] for p in pieces) == TILES + 1
    return pieces


class _PairExchange:
    def __init__(self, names, sections, shards, more=()):
        self.names, self.shards = names, shards
        self.arrays = list(sections) + list(more)
        self.out_shape = [SDS((len(shards), sections[0].shape[0] // 2, WIN), F32)]
        self.out_shape += [SDS((a.shape[0], a.shape[1] // 2, a.shape[2]), F32) for a in more]
        n = sum(len(_window(s, names)) for s in shards) + len(more)
        self.scratch = [pltpu.SemaphoreType.DMA((n,)) for _ in range(2)]

    def _copies(self, ins, outs, sems):
        x, y, c, _ = _place()
        sib = (x, y, 1 - c)
        half = ins[0].shape[0] // 2
        rows = pl.ds((1 - c) * half, half)
        k = 0
        for i, s in enumerate(self.shards):
            for n, tile, tiles, at in _window(s, self.names):
                yield _remote(ins[n].at[rows, pl.ds(tile * LANE, tiles * LANE)],
                              outs[0].at[i, :, pl.ds(at * LANE, tiles * LANE)], sems[0].at[k], sems[1].at[k], sib)
                k += 1
        for src, dst in zip(ins[len(self.names):], outs[1:]):
            half = src.shape[1] // 2
            yield _remote(src.at[:, pl.ds((1 - c) * half, half)], dst, sems[0].at[k], sems[1].at[k], sib)
            k += 1

    def start(self, ins, outs, sems):
        for cp in self._copies(ins, outs, sems):
            cp.start()

    def finish(self, ins, outs, sems):
        for cp in self._copies(ins, outs, sems):
            cp.wait()


def _exchange_call(exchange, name):
    n, n_out = len(exchange.arrays), len(exchange.out_shape)

    def body(*refs):
        ins, outs, sems = refs[:n], refs[n:n + n_out], refs[n + n_out:]
        exchange.start(ins, outs, sems)
        exchange.finish(ins, outs, sems)

    return pl.pallas_call(
        body, name=name, in_specs=[ANY] * n, out_specs=[ANY] * n_out, out_shape=exchange.out_shape,
        scratch_shapes=exchange.scratch, compiler_params=pltpu.CompilerParams(has_side_effects=True),
    )(*exchange.arrays)


def _pair_sum_windows(cidx, names, sections, shards, r, name):
    n, half, _ = r.shape
    tr = min(half, 256)
    nt = half // tr

    def body(c_ref, *refs):
        del c_ref
        secs, r_ref, o_ref = refs[:-2], refs[-2], refs[-1]
        for i, s in enumerate(shards):
            for k, tile, tiles, at in _window(s, names):
                own = secs[k][:, tile * LANE:(tile + tiles) * LANE]
                there = slice(at * LANE, (at + tiles) * LANE)
                o_ref[i, :, there] = (own + r_ref[i, :, there]).astype(BF16)

    window = pl.BlockSpec((n, tr, WIN), lambda t, c: (0, t, 0))
    return pl.pallas_call(
        body, name=name,
        grid_spec=pltpu.PrefetchScalarGridSpec(
            num_scalar_prefetch=1, grid=(nt,),
            in_specs=[pl.BlockSpec((tr, a.shape[1]), lambda t, c: (c[0] * nt + t, 0)) for a in sections] + [window],
            out_specs=window),
        out_shape=SDS(r.shape, BF16),
        compiler_params=_cp(("parallel",)),
    )(cidx, *sections, r)


def _pair_sum(cidx, g, r, name):
    n, half, width = r.shape
    tr = min(half, 256)
    nt = half // tr

    def body(c_ref, g_ref, r_ref, o_ref):
        del c_ref
        o_ref[...] = (g_ref[...] + r_ref[...]).astype(BF16)

    return pl.pallas_call(
        body, name=name,
        grid_spec=pltpu.PrefetchScalarGridSpec(
            num_scalar_prefetch=1, grid=(n, nt),
            in_specs=[pl.BlockSpec((None, tr, width), lambda s, t, c: (s, c[0] * nt + t, 0)),
                      pl.BlockSpec((None, tr, width), lambda s, t, c: (s, t, 0))],
            out_specs=pl.BlockSpec((None, tr, width), lambda s, t, c: (s, t, 0))),
        out_shape=SDS(r.shape, BF16),
        compiler_params=_cp(("parallel", "parallel")),
    )(cidx, g, r)


class _ChipExchange:
    def __init__(self, arrays, rows):
        self.arrays, self.rows = list(arrays), list(rows)
        self.out_shape = [SDS((4,) + a.shape[1:], BF16) for a in self.arrays]
        self.scratch = [pltpu.SemaphoreType.DMA((3 * len(self.arrays),)) for _ in range(2)]

    def _copies(self, ins, outs, sems):
        x, y, c, j = _place()
        send, recv = sems
        for a, (src, dst, row) in enumerate(zip(ins, outs, self.rows)):
            for k in (1, 2, 3):
                px, py, pj = _chip_of(x, y, k)
                n = 3 * a + k - 1
                slot = pj if row is None else py
                yield (None if row is None else px == row, None if row is None else x == row,
                       _remote(src.at[slot], dst.at[j], send.at[n], recv.at[n], (px, py, c)),
                       _remote(src.at[0], dst.at[pj], send.at[n], recv.at[n], (x, y, c)))

    def start(self, ins, outs, sems):
        for sends, _, send, _ in self._copies(ins, outs, sems):
            if sends is None:
                send.start()
            else:
                pl.when(sends)(send.start)

    def finish(self, ins, outs, sems):
        for sends, owns, send, arrival in self._copies(ins, outs, sems):
            if sends is None:
                arrival.wait_recv()
                send.wait_send()
            else:
                pl.when(owns)(arrival.wait_recv)
                pl.when(sends)(send.wait_send)


def _small_exchange(small):
    def body(sm_ref, rs_ref, send, recv, lsem):
        x, y, c, j = _place()
        me = 2 * j + c
        local = pltpu.make_async_copy(sm_ref, rs_ref.at[me], lsem)
        local.start()
        cps = []
        for k in range(1, 8):
            px, py, _ = _chip_of(x, y, k >> 1)
            pc = 1 - c if k & 1 else c
            cps.append(_remote(sm_ref, rs_ref.at[me], send.at[k - 1], recv.at[k - 1], (px, py, pc)))
        for cp in cps:
            cp.start()
        for k in range(1, 8):
            _, _, pj = _chip_of(x, y, k >> 1)
            pc = 1 - c if k & 1 else c
            _remote(sm_ref, rs_ref.at[2 * pj + pc], send.at[k - 1], recv.at[k - 1], (x, y, c)).wait_recv()
        for cp in cps:
            cp.wait_send()
        local.wait()

    return pl.pallas_call(
        body, name="small_exchange", in_specs=[ANY], out_specs=ANY,
        out_shape=SDS((8,) + small.shape, F32),
        scratch_shapes=[pltpu.SemaphoreType.DMA((7,)), pltpu.SemaphoreType.DMA((7,)), pltpu.SemaphoreType.DMA],
        compiler_params=pltpu.CompilerParams(has_side_effects=True),
    )(small)


def _slot_sum(r, name):
    n, rows, width = r.shape
    tr = min(rows, 256)

    def body(r_ref, o_ref):
        acc = r_ref[0].astype(F32)
        for s in range(1, n):
            acc = acc + r_ref[s].astype(F32)
        o_ref[...] = acc

    return pl.pallas_call(
        body, name=name, grid=(rows // tr,),
        in_specs=[pl.BlockSpec((n, tr, width), lambda t: (0, t, 0))],
        out_specs=pl.BlockSpec((tr, width), lambda t: (t, 0)),
        out_shape=SDS((rows, width), F32),
        compiler_params=_cp(("parallel",)),
    )(r)


def _chip_sum(where, recv, own, name):
    n, rows, width = recv.shape
    tr = min(rows, 256)
    nt = rows // tr

    def body(j_ref, r_ref, own_ref, o_ref):
        acc = None
        for s in range(n):
            term = jnp.where(j_ref[0] == s, own_ref[...], r_ref[s]).astype(F32)
            acc = term if acc is None else acc + term
        o_ref[...] = acc

    return pl.pallas_call(
        body, name=name,
        grid_spec=pltpu.PrefetchScalarGridSpec(
            num_scalar_prefetch=1, grid=(nt,),
            in_specs=[pl.BlockSpec((n, tr, width), lambda t, j: (0, t, 0)),
                      pl.BlockSpec((None, tr, width), lambda t, j: (j[0], t, 0))],
            out_specs=pl.BlockSpec((tr, width), lambda t, j: (j[1] * nt + t, 0))),
        out_shape=SDS((2 * rows, width), F32),
        compiler_params=_cp(("parallel",)),
    )(where, recv, own)


def _chip_sum_rows(place, recv0, own0, recv1, own1, name):
    n, rows, width = recv0.shape
    tr = min(rows, 256)
    nt = rows // tr

    def body(p_ref, r0_ref, o0_ref, r1_ref, o1_ref, o_ref):
        first_row = p_ref[2] == 0
        own = jnp.where(first_row, o0_ref[...], o1_ref[...])
        acc = None
        for s in range(n):
            term = jnp.where(p_ref[0] == s, own, jnp.where(first_row, r0_ref[s], r1_ref[s])).astype(F32)
            acc = term if acc is None else acc + term
        o_ref[...] = acc

    recv = pl.BlockSpec((n, tr, width), lambda t, p: (0, t, 0))
    own = pl.BlockSpec((None, tr, width), lambda t, p: (p[3], t, 0))
    return pl.pallas_call(
        body, name=name,
        grid_spec=pltpu.PrefetchScalarGridSpec(
            num_scalar_prefetch=1, grid=(nt,), in_specs=[recv, own, recv, own],
            out_specs=pl.BlockSpec((tr, width), lambda t, p: (p[1] * nt + t, 0))),
        out_shape=SDS((2 * rows, width), F32),
        compiler_params=_cp(("parallel",)),
    )(place, recv0, own0, recv1, own1)


def _half_exchange(gw, go):
    def body(gw_in, go_in, gw_ref, go_ref, send, recv):
        del gw_in, go_in
        x, y, c, _ = _place()
        mine = [pl.ds(c * (r.shape[0] // 2), r.shape[0] // 2) for r in (gw_ref, go_ref)]
        cps = [_remote(r.at[rows], r.at[rows], send.at[k], recv.at[k], (x, y, 1 - c))
               for k, (r, rows) in enumerate(zip((gw_ref, go_ref), mine))]
        for cp in cps:
            cp.start()
        for k, r in enumerate((gw_ref, go_ref)):
            theirs = pl.ds((1 - c) * (r.shape[0] // 2), r.shape[0] // 2)
            _remote(r.at[theirs], r.at[theirs], send.at[k], recv.at[k], (x, y, c)).wait_recv()
        for cp in cps:
            cp.wait_send()

    return pl.pallas_call(
        body, name="half_exchange", in_specs=[ANY, ANY], out_specs=[ANY, ANY],
        out_shape=[SDS(gw.shape, F32), SDS(go.shape, F32)], input_output_aliases={0: 0, 1: 1},
        scratch_shapes=[pltpu.SemaphoreType.DMA((2,)), pltpu.SemaphoreType.DMA((2,))],
        compiler_params=pltpu.CompilerParams(has_side_effects=True),
    )(gw, go)


def _adamw(w, g, m, v, name):
    rows, width = w.shape
    tr = min(rows, 256)

    def body(w_ref, g_ref, m_ref, v_ref, d_ref, nm_ref, nv_ref):
        gv = g_ref[...]
        nm = ADAM_B1 * m_ref[...] + (1.0 - ADAM_B1) * gv
        nv = ADAM_B2 * v_ref[...] + (1.0 - ADAM_B2) * (gv * gv)
        m_hat = nm / (1.0 - ADAM_B1 ** ADAM_STEP)
        v_hat = nv / (1.0 - ADAM_B2 ** ADAM_STEP)
        d_ref[...] = -ADAM_LR * (m_hat / (jnp.sqrt(v_hat) + ADAM_EPS) + ADAM_WD * w_ref[...])
        nm_ref[...] = nm
        nv_ref[...] = nv

    t = pl.BlockSpec((tr, width), lambda i: (i, 0))
    return pl.pallas_call(
        body, name=name, grid=(rows // tr,), in_specs=[t] * 4, out_specs=[t] * 3,
        out_shape=[SDS(w.shape, F32)] * 3, compiler_params=_cp(("parallel",)),
    )(w, g, m, v)


def _rowwise(a):
    return jnp.transpose(a, (2, 0, 1)).reshape(SHARD * D // LANE, LANE)


def _columns(ref):
    return jnp.concatenate([ref[pl.ds(c, LANE, stride=8), :].T for c in range(D // LANE)], axis=0)


def _shard_bf16(chip, w_rows):
    def body(j_ref, w_ref, o_ref, prev_ref):
        t = pl.program_id(0)
        cur = _columns(w_ref)

        @pl.when(t == 0)
        def _():
            prev_ref[...] = jnp.zeros_like(prev_ref)

        lane = _iota((D, LANE), 1)
        for s in range(4):
            @pl.when(j_ref[0] == s)
            def _():
                off = SHIFT * s
                moved = cur if s == 0 else jnp.where(lane < off, pltpu.roll(prev_ref[...], off, 1),
                                                     pltpu.roll(cur, off, 1))
                col = t * LANE + lane - off
                o_ref[...] = jnp.where((col >= 0) & (col < SHARD), moved, 0.0).astype(BF16)
        prev_ref[...] = cur

    return pl.pallas_call(
        body, name="shard_bf16",
        grid_spec=pltpu.PrefetchScalarGridSpec(
            num_scalar_prefetch=1, grid=(TILES + 1,),
            in_specs=[pl.BlockSpec((D, LANE), lambda t, j: (t, 0))],
            out_specs=pl.BlockSpec((D, LANE), lambda t, j: (0, t)),
            scratch_shapes=[pltpu.VMEM((D, LANE), F32)]),
        out_shape=SDS((D, WIN), BF16), compiler_params=_cp(("arbitrary",)),
    )(chip, w_rows)


def _whole_w_in(windows):
    tr = 256
    n = windows.shape[0]

    def body(g_ref, o_ref):
        lane = _iota((tr, LANE), 1)
        for s in range(n):
            first = TILES * s
            head = g_ref[s, :, :LANE]
            if s:
                tail = g_ref[s - 1, :, TILES * LANE:]
                head = jnp.where(lane < SHIFT * s, tail.astype(F32), head.astype(F32)).astype(BF16)
            o_ref[:, first * LANE:(first + 1) * LANE] = head
            o_ref[:, (first + 1) * LANE:(first + TILES) * LANE] = g_ref[s, :, LANE:TILES * LANE]
        o_ref[:, n * TILES * LANE:(n * TILES + 1) * LANE] = g_ref[n - 1, :, TILES * LANE:]
        o_ref[:, (n * TILES + 1) * LANE:] = jnp.zeros((tr, DP - (n * TILES + 1) * LANE), BF16)

    return pl.pallas_call(
        body, name="whole_w_in", grid=(D // tr,),
        in_specs=[pl.BlockSpec((n, tr, WIN), lambda t: (0, t, 0))], out_specs=pl.BlockSpec((tr, DP), lambda t: (t, 0)),
        out_shape=SDS((D, DP), BF16), compiler_params=_cp(("parallel",)),
    )(windows)


def _adamw_in(chip, w_rows, g_win, m_rows, v_rows):
    def body(j_ref, w_ref, g_ref, next_ref, m_ref, v_ref, grad_ref, d_ref, nm_ref, nv_ref):
        columns = _columns
        for s in range(4):
            @pl.when(j_ref[0] == s)
            def _():
                if s == 0:
                    grad_ref[...] = g_ref[...]
                else:
                    back = LANE - SHIFT * s
                    grad_ref[...] = jnp.where(_iota((D, LANE), 1) < back, pltpu.roll(g_ref[...], back, 1),
                                              pltpu.roll(next_ref[...], back, 1))
        gv = grad_ref[...]
        nm = ADAM_B1 * columns(m_ref) + (1.0 - ADAM_B1) * gv
        nv = ADAM_B2 * columns(v_ref) + (1.0 - ADAM_B2) * (gv * gv)
        m_hat = nm / (1.0 - ADAM_B1 ** ADAM_STEP)
        v_hat = nv / (1.0 - ADAM_B2 ** ADAM_STEP)
        d_ref[...] = -ADAM_LR * (m_hat / (jnp.sqrt(v_hat) + ADAM_EPS) + ADAM_WD * columns(w_ref))
        nm_ref[...] = nm
        nv_ref[...] = nv

    tile = pl.BlockSpec((D, LANE), lambda t, j: (0, t))
    next_tile = pl.BlockSpec((D, LANE), lambda t, j: (0, jnp.minimum(t + 1, TILES)))
    rows = pl.BlockSpec((D, LANE), lambda t, j: (t, 0))
    return pl.pallas_call(
        body, name="adamw_in",
        grid_spec=pltpu.PrefetchScalarGridSpec(
            num_scalar_prefetch=1, grid=(TILES + 1,), in_specs=[rows, tile, next_tile, rows, rows],
            out_specs=[tile] * 4),
        out_shape=[SDS((D, SHARD), F32)] * 4, compiler_params=_cp(("parallel",)),
    )(chip, w_rows, g_win, g_win, m_rows, v_rows)


def _rows128(a, rows):
    flat = a.reshape(-1)
    return jnp.pad(flat, (0, rows * LANE - flat.shape[0])).reshape(rows, LANE)


def _pack_small(conv_w, norm_pre, conv_b, ssm_norm, norm_post, dtb, alog, dsk, extra=None):
    cw_rows = 48 if conv_w.shape[-1] == 1536 else 16
    extra = jnp.zeros((1, LANE), F32) if extra is None else _rows128(extra, 1)
    vec = jnp.concatenate([_rows128(dtb, 1), _rows128(alog, 1), _rows128(dsk, 1), extra, jnp.zeros((4, LANE), F32)],
                          axis=0)
    return jnp.concatenate([_rows128(conv_w, cw_rows), _rows128(norm_pre, 8), _rows128(conv_b, 16),
                            _rows128(ssm_norm, 8), _rows128(norm_post, 8), vec], axis=0)


def _unpack_small(p, cw_cols):
    cw_rows = 48 if cw_cols == 1536 else 16
    o = cw_rows
    conv_w = p[:cw_rows].reshape(-1)[:4 * cw_cols].reshape(1, 4, cw_cols)
    norm_pre = p[o:o + 8].reshape(1, D)
    conv_b = p[o + 8:o + 24].reshape(-1)[:1536].reshape(1, 1536)
    ssm_norm = p[o + 24:o + 32].reshape(1, D)
    norm_post = p[o + 32:o + 40].reshape(1, D)
    vec = p[o + 40:o + 48]
    return conv_w, norm_pre, conv_b, ssm_norm, norm_post, vec[0:1, :NH], vec[1:2, :NH], vec[2:3, :NH], vec[3, 0]


def _pad_lanes(a):
    return jnp.pad(a, ((0, 0), (0, LANE - a.shape[1])))


class _GradReduce:
    LO, HI = ("q", "k", "v", "g"), ("g", "z", "x")

    def __init__(self, xi, yi, ci):
        self.cidx = jnp.reshape(ci, (1,)).astype(jnp.int32)
        self.place = jnp.stack([2 * xi + yi, ci, xi, yi]).astype(jnp.int32)

    def pairs(self, dw_g, dw_z, dw_x, dw_out):
        self.hi = [dw_g, dw_z, dw_x]
        self.go = dw_out.reshape(4, D // 2, D)
        return _PairExchange(self.HI, self.hi, (2, 3), [self.go])

    def first(self, got):
        rw, ro = got
        self.pw_hi = _pair_sum_windows(self.cidx, self.HI, self.hi, (2, 3), rw, "pair_sum_hi")
        self.po = _pair_sum(self.cidx, self.go, ro, "pair_sum_out")
        return _ChipExchange([self.pw_hi, self.po], [1, None])

    def first_done(self, got):
        self.rw_hi, self.ro = got

    def second(self, dw_q, dw_k, dw_v, dw_g):
        lo = [dw_q, dw_k, dw_v, dw_g]
        (rw,) = _exchange_call(_PairExchange(self.LO, lo, (0, 1)), "pair_exchange_lo")
        self.pw_lo = _pair_sum_windows(self.cidx, self.LO, lo, (0, 1), rw, "pair_sum_lo")
        return _ChipExchange([self.pw_lo], [0])

    def second_done(self, got):
        (self.rw_lo,) = got

    def result(self):
        half_in = _chip_sum_rows(self.place, self.rw_lo, self.pw_lo, self.rw_hi, self.pw_hi, "chip_sum_in")
        half_out = _chip_sum(self.place[0:2], self.ro, self.po, "chip_sum_out")
        return _half_exchange(half_in, half_out)


def kernel(x, norm_pre_w, w_in, conv_w, conv_b, dt_bias, a_log, d_skip, ssm_norm_w, w_out, norm_post_w, loss_target, m_norm_pre_w, m_w_in, m_conv_w, m_conv_b, m_dt_bias, m_a_log, m_d_skip, m_ssm_norm_w, m_w_out, m_norm_post_w, v_norm_pre_w, v_w_in, v_conv_w, v_conv_b, v_dt_bias, v_a_log, v_d_skip, v_ssm_norm_w, v_w_out, v_norm_post_w):
    xi, yi, ci = lax.axis_index("x"), lax.axis_index("y"), lax.axis_index("c")
    chip = 2 * xi + yi
    x2, tgt = x[0], loss_target[0]

    chip_idx = jnp.reshape(chip, (1,)).astype(jnp.int32)
    w_rows = _rowwise(w_in)
    w_all = _whole_w_in(_gather_weights(_shard_bf16(chip_idx, w_rows)))
    reduce = _GradReduce(xi, yi, ci)
    grad_x, small = _local_step(x2, tgt, w_all, _LateGather(w_out[0].astype(BF16), conv_w[0]), norm_pre_w, conv_b,
                                dt_bias, a_log, d_skip, ssm_norm_w, norm_post_w, reduce)[:2]
    g_win, g_out = reduce.result()
    g_small = _slot_sum(_small_exchange(small), "small_sum")
    g_cw, g_npre, g_cb, g_nssm, g_npost, g_dtb, g_alog, g_dsk, loss = _unpack_small(g_small, 1536)
    g_cw = lax.dynamic_slice_in_dim(g_cw, chip * 384, 384, axis=2)

    g_in, d_in, nm_in, nv_in = _adamw_in(chip_idx, w_rows, g_win, _rowwise(m_w_in), _rowwise(v_w_in))
    d_out, nm_out, nv_out = _adamw(w_out[0], g_out, m_w_out[0], v_w_out[0], "adamw_out")
    packed = [_pack_small(*t) for t in (
        (conv_w, norm_pre_w, conv_b, ssm_norm_w, norm_post_w, dt_bias, a_log, d_skip),
        (g_cw, g_npre, g_cb, g_nssm, g_npost, g_dtb, g_alog, g_dsk),
        (m_conv_w, m_norm_pre_w, m_conv_b, m_ssm_norm_w, m_norm_post_w, m_dt_bias, m_a_log, m_d_skip),
        (v_conv_w, v_norm_pre_w, v_conv_b, v_ssm_norm_w, v_norm_post_w, v_dt_bias, v_a_log, v_d_skip))]
    small_out = [_unpack_small(p, 384)[:8] for p in _adamw(*packed, "adamw_small")]

    def ordered(cw_, npre, cb_, nssm, npost, dtb_, alog_, dsk_, big_in, big_out):
        return [npre, big_in[None], cw_, cb_, dtb_, alog_, dsk_, nssm, big_out[None], npost]

    grads = ordered(g_cw, g_npre, g_cb, g_nssm, g_npost, g_dtb, g_alog, g_dsk, g_in, g_out)
    deltas = ordered(*small_out[0], d_in, d_out)
    new_m = ordered(*small_out[1], nm_in, nm_out)
    new_v = ordered(*small_out[2], nv_in, nv_out)
    return (loss, grad_x[None], *grads, *deltas, *new_m, *new_v)


def _local_step(x2, tgt, w_all, late, norm_pre_w, conv_b, dt_bias, a_log, d_skip, ssm_norm_w,
                norm_post_w, reduce=None):
    dtb, alog = _pad_lanes(dt_bias), _pad_lanes(a_log)
    d_b = jnp.repeat(d_skip, 64, axis=1)

    if isinstance(late, _LateGather):
        (proj, u), (gout, gcw) = _inproj_fwd(x2, norm_pre_w, w_all, late)
        w_out_all = gout.reshape(2 * D, D)
        cw_all = jnp.concatenate([gcw[0], gcw[1], gcw[2], gcw[3]], axis=1)
    else:
        proj, u = _inproj_fwd(x2, norm_pre_w, w_all)
        w_out_all, cw_all = late
    mix, attn_pre, lse = _attn_fwd(proj, 1, _attn_fwd(proj, 4, _attn_fwd(proj, 16)), final=True)
    mix, y_save, states, conv_out = _ssm_fwd(proj, mix, cw_all, conv_b, dtb, alog, d_b, ssm_norm_w)

    dy, dn_ssm, do, delta, dg, dw_out, dnw_post, loss_part = _outproj_loss(mix, w_out_all, x2, tgt, norm_post_w,
                                                                          attn_pre, proj)
    dz, dxbcdt, dcw, dcb, dvec, dnw_ssm = _ssm_bwd(proj, dn_ssm, y_save, states, conv_out, cw_all, dtb, alog, d_b,
                                                   ssm_norm_w)
    dw_g, dw_z, dw_x = _dw(u, dg, "dw_in_g"), _dw(u, dz, "dw_in_z"), _dw(u, dxbcdt, "dw_in_xbcdt")
    acc = _attn_bwd(proj, do, lse, delta, 16, None, F32, reduce.pairs(dw_g, dw_z, dw_x, dw_out) if reduce else None)
    if reduce:
        acc, got = acc
    acc = _attn_bwd(proj, do, lse, delta, 4, acc, F32, reduce.first(got) if reduce else None)
    if reduce:
        acc, got = acc
        reduce.first_done(got)
    dq, dk, dv = _attn_bwd(proj, do, lse, delta, 1, acc, BF16)
    dw_q, dw_k, dw_v = _dw(u, dq, "dw_in_q"), _dw(u, dk, "dw_in_k"), _dw(u, dv, "dw_in_v")
    res = _inproj_bwd_dx([dq, dk, dv, dg, dz], dxbcdt, w_all, x2, dy, norm_pre_w,
                         reduce.second(dw_q, dw_k, dw_v, dw_g) if reduce else None)
    if reduce:
        res, got = res
        reduce.second_done(got)
    grad_x, dnw_pre = res
    dw_all = jnp.concatenate([dw_q, dw_k, dw_v, dw_g, dw_z, dw_x], axis=1)
    small = _pack_small(dcw, dnw_pre, dcb, dnw_ssm, dnw_post, dvec[0:1, :NH], dvec[1:2, :NH], dvec[2:3, :NH],
                        loss_part[:, :1])
    return grad_x, small, dw_all, dw_out
```

```python
import functools

import jax
import jax.numpy as jnp
from jax import lax
from jax.experimental import pallas as pl
from jax.experimental.pallas import tpu as pltpu

F32 = jnp.float32
BF16 = jnp.bfloat16
MESH = pl.DeviceIdType.MESH
SDS = jax.ShapeDtypeStruct
ANY = pl.BlockSpec(memory_space=pl.ANY)

S = 4096
D = 1024
DP = 7168
SHARD = 1668
OFF_G, OFF_Z = 3072, 4096
NH = 16
CH = 128
NC = S // CH
EPS = 1e-6
NEG = -1e30
LANE = 128
VMEM_LIMIT = 48 * 1024 * 1024

TILES = SHARD // LANE
WIN = (TILES + 1) * LANE
SHIFT = SHARD - TILES * LANE
SECTION_TILES = {"q": (0, 8), "k": (8, 8), "v": (16, 8), "g": (24, 8), "z": (32, 8), "x": (40, 16)}

ADAM_LR, ADAM_B1, ADAM_B2, ADAM_EPS, ADAM_WD, ADAM_STEP = 0.001, 0.9, 0.999, 1e-08, 0.01, 10


def _cp(sem, **kw):
    return pltpu.CompilerParams(dimension_semantics=sem, vmem_limit_bytes=VMEM_LIMIT, **kw)


def _dot(a, b):
    return jnp.dot(a, b, preferred_element_type=F32)


def _dot_nt(a, b):
    return lax.dot_general(a, b, (((1,), (1,)), ((), ())), preferred_element_type=F32)


def _dot_tn(a, b):
    return lax.dot_general(a, b, (((0,), (0,)), ((), ())), preferred_element_type=F32)


def _pieces(x, n):
    out = []
    for _ in range(n):
        p = x.astype(BF16)
        out.append(p)
        x = x - p.astype(F32)
    return out


def _pick(x, sel, n=2):
    parts = [_dot(p, sel) for p in _pieces(x, n)]
    return functools.reduce(jnp.add, parts)


def _pick_left(sel, x, n=3):
    parts = [_dot(sel, p) for p in _pieces(x, n)]
    return functools.reduce(jnp.add, parts)


def _sigmoid(v):
    return 0.5 * jnp.tanh(0.5 * v) + 0.5


def _iota(shape, dim):
    return lax.broadcasted_iota(jnp.int32, shape, dim)


def _inproj_fwd(x, nw, w_all, hosted=None):
    tm, tn = 1024, 1024
    n_host = len(hosted.arrays) if hosted else 0

    def body(x_ref, nw_ref, w_ref, *refs):
        host_in, (proj_ref, u_ref), refs = refs[:n_host], refs[n_host:n_host + 2], refs[n_host + 2:]
        host_out, host_sems = refs[:n_host], refs[n_host:]
        i, j = pl.program_id(0), pl.program_id(1)
        if hosted:
            pl.when((i == 0) & (j == 0))(lambda: hosted.start(host_in, host_out, host_sems))

        @pl.when(j == 0)
        def _():
            xf = x_ref[...]
            r = lax.rsqrt(jnp.mean(xf * xf, axis=-1, keepdims=True) + EPS)
            u_ref[...] = (xf * r * nw_ref[...]).astype(BF16)

        proj_ref[...] = _dot(u_ref[...], w_ref[...])
        if hosted:
            pl.when((i == S // tm - 1) & (j == DP // tn - 1))(lambda: hosted.finish(host_in, host_out, host_sems))

    outs = pl.pallas_call(
        body, name="inproj_fwd", grid=(S // tm, DP // tn),
        in_specs=[pl.BlockSpec((tm, D), lambda i, j: (i, 0)), pl.BlockSpec((1, D), lambda i, j: (0, 0)),
                  pl.BlockSpec((D, tn), lambda i, j: (0, j))] + [ANY] * n_host,
        out_specs=[pl.BlockSpec((tm, tn), lambda i, j: (i, j)), pl.BlockSpec((tm, D), lambda i, j: (i, 0))]
        + [ANY] * n_host,
        out_shape=[SDS((S, DP), F32), SDS((S, D), BF16)] + (hosted.out_shape if hosted else []),
        scratch_shapes=hosted.scratch if hosted else [],
        compiler_params=_cp(("arbitrary", "arbitrary") if hosted else ("parallel", "arbitrary")),
    )(x, nw, w_all, *(hosted.arrays if hosted else []))
    return (outs[:2], outs[2:]) if hosted else outs


ATTN_QB = {1: 16, 4: 4, 16: 1}


def _unit_rows(r, u, d):
    return pl.ds(r + d * CH * u, CH, stride=d) if d > 1 else pl.ds(CH * u, CH)


def _for_units(d, qb, fn):
    for r in range(d):
        for u in range(qb):
            fn(r, u)


def _attn_mask(has_prev):
    qi, kj = _iota((2 * CH, 2 * CH), 0) & (CH - 1), _iota((2 * CH, 2 * CH), 1)
    cur_ok = (kj >= CH) & (kj - CH <= qi)
    prev_ok = (kj < CH) & (kj >= qi)
    return cur_ok | (prev_ok & has_prev)


def _stack_heads(v, lane_a):
    return jnp.concatenate([jnp.where(lane_a, v, 0.0), jnp.where(lane_a, 0.0, v)], axis=0).astype(BF16)


def _attn_specs(d, qb):
    rows, prows = CH * d * qb, CH * d
    nb = S // rows
    steps = (NH // 2) * nb

    def at(t):
        t = jnp.minimum(t, steps - 1)
        return t % nb, t // nb

    def cur(off):
        return pl.BlockSpec((rows, LANE), lambda t: (at(t)[0], off + at(t)[1]))

    def prev(off):
        return pl.BlockSpec((prows, LANE), lambda t: (jnp.maximum(at(t)[0] * qb - 1, 0), off + at(t)[1]))

    lag = pl.BlockSpec((rows, LANE), lambda t: at(jnp.maximum(t - 1, 0)))
    return nb, steps, cur, prev, lag


def _gather16(src_ref, dense_ref, tmp_ref):
    for a in range(4):
        tmp_ref[...] = src_ref[pl.ds(a, 4 * CH, stride=4), :]
        for b in range(4):
            dense_ref[a + 4 * b] = tmp_ref[pl.ds(b, CH, stride=4), :]


def _scatter16(dense_ref, dst_ref, tmp_ref):
    for a in range(4):
        for b in range(4):
            tmp_ref[pl.ds(b, CH, stride=4), :] = dense_ref[a + 4 * b]
        dst_ref[pl.ds(a, 4 * CH, stride=4), :] = tmp_ref[...]


def _unit_index(r, u, d):
    return (r,) if d == 16 else (_unit_rows(r, u, d), slice(None))


def _unit_kv(p_ref, c_ref, r, u, d):
    prev = p_ref[_unit_index(r, 0, d)] if u == 0 else c_ref[_unit_index(r, u - 1, d)]
    return jnp.concatenate([prev, c_ref[_unit_index(r, u, d)]], axis=0).astype(BF16)


def _dense_scratch(d, n):
    return [pltpu.VMEM((16, CH, LANE), F32)] * n + [pltpu.VMEM((4 * CH, LANE), F32)] if d == 16 else []


def _attn_fwd(proj, d, prior=None, final=False):
    qb = ATTN_QB[d]
    nb, steps, cur, prev, _ = _attn_specs(d, qb)
    n_prior = 2 if prior is not None else 0
    n_in, n_out = 5 + n_prior + final, 2 + final
    assert not (d == 16 and (n_prior or final))

    def body(*refs):
        ins, outs, scratch = refs[:n_in], refs[n_in:n_in + n_out], refs[n_in + n_out:]
        if d == 16:
            tmp_ref = scratch[-1]
            for src, dense in zip(ins, scratch):
                _gather16(src, dense, tmp_ref)
            block_outs, ins, outs = outs, scratch[:n_in], scratch[n_in:n_in + n_out]
        q_ref, kp_ref, kc_ref, vp_ref, vc_ref = ins[:5]
        prior_refs = ins[5:5 + n_prior]
        if final:
            g_ref, (mix_ref, o_ref, l_ref) = ins[-1], outs
        else:
            o_ref, l_ref = outs
        i = pl.program_id(0) % nb
        lane_a = _iota((CH, LANE), 1) < 64
        mask_first, mask_rest = _attn_mask(i > 0), _attn_mask(True)

        def unit(r, u):
            at = _unit_index(r, u, d)
            q2 = _stack_heads(q_ref[at] * 0.125, lane_a)
            k2, v2 = _unit_kv(kp_ref, kc_ref, r, u, d), _unit_kv(vp_ref, vc_ref, r, u, d)
            s = jnp.where(mask_first if u == 0 else mask_rest, _dot_nt(q2, k2), NEG)
            m = jnp.max(s, axis=1, keepdims=True)
            p = jnp.exp(s - m)
            l = jnp.sum(p, axis=1, keepdims=True)
            o2 = _dot(p.astype(BF16), v2) / l
            lse2 = m + jnp.log(l)
            o = jnp.where(lane_a, o2[:CH], o2[CH:])
            lse = jnp.where(lane_a, lse2[:CH], lse2[CH:])
            if n_prior:
                o_a, l_a = prior_refs[0][at], prior_refs[1][at]
                top = jnp.maximum(l_a, lse)
                e_a, e_b = jnp.exp(l_a - top), jnp.exp(lse - top)
                tot = e_a + e_b
                o = (e_a * o_a + e_b * o) / tot
                lse = top + jnp.log(tot)
            o_ref[at] = o
            l_ref[at] = lse
            if final:
                g = g_ref[at]
                mix_ref[at] = (o * (g * _sigmoid(g))).astype(BF16)

        _for_units(d, qb, unit)
        if d == 16:
            for dense, dst in zip(outs, block_outs):
                _scatter16(dense, dst, tmp_ref)

    in_specs = [cur(0), prev(8), cur(8), prev(16), cur(16)] + [cur(0)] * n_prior
    args = [proj] * 5 + (list(prior) if n_prior else [])
    out_specs, out_shape = [cur(0), cur(0)], [SDS((S, D), F32), SDS((S, D), F32)]
    if final:
        assert d == 1
        in_specs.append(cur(OFF_G // LANE))
        args.append(proj)
        out_specs, out_shape = [cur(0)] + out_specs, [SDS((S, 2 * D), BF16)] + out_shape
    return pl.pallas_call(
        body, name=f"attn_fwd_d{d}", grid=(steps,),
        in_specs=in_specs, out_specs=out_specs, out_shape=out_shape,
        scratch_shapes=_dense_scratch(d, n_in + n_out),
        compiler_params=_cp(("parallel",)),
    )(*args)


def _attn_bwd(proj, do, lse, delta, d, acc, out_dtype, hosted=None):
    qb = ATTN_QB[d]
    nb, steps, cur, prev, lag = _attn_specs(d, qb)
    has_acc = acc is not None
    n_in = 11 if has_acc else 8
    n_host, n_host_out = (len(hosted.arrays), len(hosted.out_shape)) if hosted else (0, 0)
    assert not (d == 16 and (has_acc or out_dtype != F32))
    rows = CH * d * qb
    carry = (2, 16, CH, LANE) if d == 16 else (2, rows, LANE)

    def body(*refs):
        ins, host_in, refs = refs[:n_in], refs[n_in:n_in + n_host], refs[n_in + n_host:]
        (dq_ref, dk_ref, dv_ref), host_out, scratch = refs[:3], refs[3:3 + n_host_out], refs[3 + n_host_out:]
        if hosted:
            scratch, host_sems = scratch[:-len(hosted.scratch)], scratch[-len(hosted.scratch):]
        ck_ref, cv_ref = scratch[:2]
        dq_f32 = dq_ref if out_dtype == F32 else scratch[2]
        t = pl.program_id(0)
        i = t % nb
        if hosted:
            pl.when(t == 0)(lambda: hosted.start(host_in, host_out, host_sems))
        if d == 16:
            dense, dq_f32, tmp_ref = scratch[2:2 + n_in], scratch[2 + n_in], scratch[-1]

            @pl.when(t < steps)
            def _():
                for src, dst in zip(ins, dense):
                    _gather16(src, dst, tmp_ref)

            ins = dense
        q_ref, kp_ref, kc_ref, vp_ref, vc_ref, do_ref, lse_ref, dl_ref = ins[:8]
        if has_acc:
            aq_ref, ak_ref, av_ref = ins[8:11]
        slot = t & 1
        now_k, now_v, old_k, old_v = ck_ref.at[slot], cv_ref.at[slot], ck_ref.at[1 - slot], cv_ref.at[1 - slot]
        lane_a = _iota((CH, LANE), 1) < 64
        mask_first, mask_rest = _attn_mask(i > 0), _attn_mask(True)

        @pl.when(t == 0)
        def _():
            ck_ref[1] = jnp.zeros(carry[1:], F32)
            cv_ref[1] = jnp.zeros(carry[1:], F32)

        def unit(r, u):
            at = _unit_index(r, u, d)
            q2 = _stack_heads(q_ref[at] * 0.125, lane_a)
            do2 = _stack_heads(do_ref[at], lane_a)
            k2, v2 = _unit_kv(kp_ref, kc_ref, r, u, d), _unit_kv(vp_ref, vc_ref, r, u, d)
            lsev, dlv = lse_ref[at], dl_ref[at]
            lse2 = jnp.concatenate([lsev[:, 0:1], lsev[:, 64:65]], axis=0)
            dl2 = jnp.concatenate([dlv[:, 0:1], dlv[:, 64:65]], axis=0)
            p = jnp.exp(jnp.where(mask_first if u == 0 else mask_rest, _dot_nt(q2, k2), NEG) - lse2)
            ds = (p * (_dot_nt(do2, v2) - dl2)).astype(BF16)
            dq2 = _dot(ds, k2)
            dk2 = _dot_tn(ds, q2)
            dv2 = _dot_tn(p.astype(BF16), do2)
            dq = jnp.where(lane_a, dq2[:CH], dq2[CH:]) * 0.125
            if has_acc:
                dq = dq + aq_ref[at]
            dq_f32[at] = dq
            if u == 0:
                before = _unit_index(r, qb - 1, d)
                old_k[before] += dk2[:CH]
                old_v[before] += dv2[:CH]
            else:
                before = _unit_index(r, u - 1, d)
                now_k[before] += dk2[:CH]
                now_v[before] += dv2[:CH]
            now_k[at] = dk2[CH:]
            now_v[at] = dv2[CH:]

        @pl.when(t < steps)
        def _():
            _for_units(d, qb, unit)
            if d == 16:
                _scatter16(dq_f32, dq_ref, tmp_ref)
            elif out_dtype != F32:
                dq_ref[...] = dq_f32[...].astype(out_dtype)

        if d == 16:
            _scatter16(old_k, dk_ref, tmp_ref)
            _scatter16(old_v, dv_ref, tmp_ref)
        else:
            dk, dv = old_k[...], old_v[...]
            if has_acc:
                dk, dv = dk + ak_ref[...], dv + av_ref[...]
            dk_ref[...] = dk.astype(out_dtype)
            dv_ref[...] = dv.astype(out_dtype)
        if hosted:
            pl.when(t == steps)(lambda: hosted.finish(host_in, host_out, host_sems))

    in_specs = [cur(0), prev(8), cur(8), prev(16), cur(16), cur(0), cur(0), cur(0)]
    args = [proj, proj, proj, proj, proj, do, lse, delta]
    if has_acc:
        in_specs += [cur(0), lag, lag]
        args += list(acc)
    scratch = [pltpu.VMEM(carry, F32), pltpu.VMEM(carry, F32)]
    if d == 16:
        scratch += _dense_scratch(d, n_in + 1)
    elif out_dtype != F32:
        scratch.append(pltpu.VMEM((rows, LANE), F32))
    out_specs, out_shape = [cur(0), lag, lag], [SDS((S, D), out_dtype)] * 3
    if hosted:
        args += hosted.arrays
        in_specs += [ANY] * n_host
        out_specs += [ANY] * n_host_out
        out_shape += hosted.out_shape
        scratch += hosted.scratch
    outs = pl.pallas_call(
        body, name=f"attn_bwd_d{d}", grid=(steps + 1,),
        in_specs=in_specs, out_specs=out_specs, out_shape=out_shape,
        scratch_shapes=scratch, compiler_params=_cp(("arbitrary",)),
    )(*args)
    return (outs[:3], outs[3:]) if hosted else outs


def _conv_taps(cur, prev8, first):
    row8 = _iota(prev8.shape, 0)
    prev8 = jnp.where(first, 0.0, prev8)
    taps = []
    for s in (3, 2, 1):
        rolled = pltpu.roll(cur, s, 0)
        head = jnp.where(row8 < s, pltpu.roll(prev8, s, 0), rolled[:8])
        taps.append(jnp.concatenate([head, rolled[8:]], axis=0))
    return taps + [cur]


def _conv(taps, w, b):
    acc = b + w[0:1, :] * taps[0]
    for k in (1, 2, 3):
        acc = acc + w[k:k + 1, :] * taps[k]
    return acc


def _expand():
    return (_iota((LANE, D), 1) // 64 == _iota((LANE, D), 0)).astype(BF16)


def _reduce():
    return (_iota((D, LANE), 0) // 64 == _iota((D, LANE), 1)).astype(BF16)


def _ssd_common(xs_c, bc_c, dt_raw, dtb, alog):
    head_lane = _iota((CH, LANE), 1) < NH
    xs = xs_c * _sigmoid(xs_c)
    bc = bc_c * _sigmoid(bc_c)
    pre = dt_raw + dtb
    dt = jnp.where(head_lane, jnp.maximum(pre, 0.0) + jnp.log(1.0 + jnp.exp(-jnp.abs(pre))), 0.0)
    a_row = jnp.where(head_lane[0:1], -jnp.exp(alog), 0.0)
    tri = (_iota((CH, CH), 1) <= _iota((CH, CH), 0)).astype(BF16)
    cs = _pick_left(tri, dt * a_row)
    cs_last = cs[CH - 1:CH, :]
    wide = _pick(jnp.concatenate([dt, jnp.exp(cs), jnp.exp(cs_last - cs)], axis=0), _expand())
    dt_b, e_b, f_b = wide[:CH], wide[CH:2 * CH], wide[2 * CH:]
    return dict(xs=xs, bc=bc, pre=pre, dt=dt, a_row=a_row, cs=cs, cs_t=cs.T, dt_b=dt_b, e_b=e_b, f_b=f_b,
                t_b=e_b[CH - 1:CH, :])


def _groups(bc):
    bcb = bc.astype(BF16)
    return [bcb[:, 0:128], bcb[:, 128:256]], [bcb[:, 256:384], bcb[:, 384:512]]


def _decay(q, h, tril):
    seg = q["cs"][:, h:h + 1] - q["cs_t"][h:h + 1, :]
    return jnp.exp(jnp.where(tril, seg, NEG))


def _ssm_fwd(proj, mix, cw, cb, dtb, alog, d_b, nw):
    def body(xs_ref, xsp_ref, bc_ref, bcp_ref, dt_ref, z_ref, cw_ref, cb_ref, dtb_ref, alog_ref, db_ref, nw_ref,
             mix_in_ref, mix_ref, y_ref, st_ref, conv_ref, h_ref):
        del mix_in_ref
        i = pl.program_id(0)

        @pl.when(i == 0)
        def _():
            h_ref[...] = jnp.zeros_like(h_ref)

        cw, cb = cw_ref[...], cb_ref[...]
        xs_c = _conv(_conv_taps(xs_ref[...], xsp_ref[...], i == 0), cw[:, :D], cb[:, :D])
        bc_c = _conv(_conv_taps(bc_ref[...], bcp_ref[...], i == 0), cw[:, D:], cb[:, D:])
        conv_ref[:, :D] = xs_c
        conv_ref[:, D:] = bc_c
        q = _ssd_common(xs_c, bc_c, dt_ref[...], dtb_ref[...], alog_ref[...])
        bg, cg = _groups(q["bc"])
        xs = q["xs"]
        xdt = xs * q["dt_b"]
        xdt_b = xdt.astype(BF16)
        h_in = h_ref[...]
        st_ref[...] = h_in
        hb = h_in.astype(BF16)
        tril = _iota((CH, CH), 1) <= _iota((CH, CH), 0)
        lane_a = _iota((CH, LANE), 1) < 64
        cbm = [_dot_nt(cg[g], bg[g]) for g in range(2)]
        pairs = []
        for hp in range(NH // 2):
            xp = xdt_b[:, hp * LANE:(hp + 1) * LANE]
            ya = _dot((cbm[hp // 4] * _decay(q, 2 * hp, tril)).astype(BF16), xp)
            yb = _dot((cbm[hp // 4] * _decay(q, 2 * hp + 1, tril)).astype(BF16), xp)
            pairs.append(jnp.where(lane_a, ya, yb))
        y_diag = jnp.concatenate(pairs, axis=1)
        y_off = jnp.concatenate([_dot(cg[g], hb[:, g * 512:(g + 1) * 512]) for g in range(2)], axis=1) * q["e_b"]
        y = y_diag + y_off + db_ref[...] * xs
        y_ref[...] = y
        xf = (xdt * q["f_b"]).astype(BF16)
        h_ref[...] = q["t_b"] * h_in + jnp.concatenate(
            [_dot_tn(bg[g], xf[:, g * 512:(g + 1) * 512]) for g in range(2)], axis=1)
        z = z_ref[...]
        yz = y * (z * _sigmoid(z))
        outs = []
        for g in range(2):
            v = yz[:, g * 512:(g + 1) * 512]
            outs.append(v * lax.rsqrt(jnp.mean(v * v, axis=-1, keepdims=True) + EPS))
        mix_ref[...] = (jnp.concatenate(outs, axis=1) * nw_ref[...]).astype(BF16)

    def col(width, blk, prev=False):
        if prev:
            return pl.BlockSpec((8, width), lambda i: (jnp.maximum(i * (CH // 8) - 1, 0), blk))
        return pl.BlockSpec((CH, width), lambda i: (i, blk))

    def full(a):
        return pl.BlockSpec(a.shape, lambda i: (0,) * a.ndim)

    return pl.pallas_call(
        body, name="ssm_fwd", grid=(NC,),
        in_specs=[col(D, 5), col(D, 5, True), col(512, 12), col(512, 12, True), col(LANE, 52), col(D, 4),
                  full(cw), full(cb), full(dtb), full(alog), full(d_b), full(nw), ANY],
        out_specs=[col(D, 1), col(D, 0), pl.BlockSpec((None, CH, D), lambda i: (i, 0, 0)), col(D + 512, 0)],
        out_shape=[SDS((S, 2 * D), BF16), SDS((S, D), F32), SDS((NC, CH, D), F32), SDS((S, D + 512), F32)],
        scratch_shapes=[pltpu.VMEM((CH, D), F32)],
        input_output_aliases={12: 0},
        compiler_params=_cp(("arbitrary",)),
    )(proj, proj, proj, proj, proj, proj, cw, cb, dtb, alog, d_b, nw, mix)


def _ssm_bwd(proj, dn, y_save, states, conv_out, cw, dtb, alog, d_b, nw):
    def body(xs_ref, bc_ref, dt_ref, z_ref, dn_ref, y_ref, st_ref, conv_ref,
             cw_ref, dtb_ref, alog_ref, db_ref, nw_ref,
             dz_ref, dx_ref, dcw_ref, dcb_ref, dsm_ref, dnw_ref, dh_ref, nxs_ref, nbc_ref):
        i = pl.program_id(0)
        ci = NC - 1 - i

        @pl.when(i == 0)
        def _():
            for ref in (dcw_ref, dcb_ref, dsm_ref, dnw_ref, dh_ref, nxs_ref, nbc_ref):
                ref[...] = jnp.zeros_like(ref)

        cw = cw_ref[...]
        xs_c, bc_c = conv_ref[:, :D], conv_ref[:, D:]
        q = _ssd_common(xs_c, bc_c, dt_ref[...], dtb_ref[...], alog_ref[...])
        bg, cg = _groups(q["bc"])
        xs, dt_b, e_b, f_b, t_b = q["xs"], q["dt_b"], q["e_b"], q["f_b"], q["t_b"]
        xdt = xs * dt_b
        xdt_b = xdt.astype(BF16)
        h_in = st_ref[...]
        hb = h_in.astype(BF16)
        dh_new = dh_ref[...]
        dhb = dh_new.astype(BF16)
        red = _reduce()

        z, y, dn, nw_v = z_ref[...], y_ref[...], dn_ref[...], nw_ref[...]
        sig = _sigmoid(z)
        sz = z * sig
        yz = y * sz
        gdn = dn * nw_v
        dyz, dnw = [], []
        for g in range(2):
            v, gv = yz[:, g * 512:(g + 1) * 512], gdn[:, g * 512:(g + 1) * 512]
            r = lax.rsqrt(jnp.mean(v * v, axis=-1, keepdims=True) + EPS)
            dnw.append(dn[:, g * 512:(g + 1) * 512] * v * r)
            dyz.append(r * (gv - v * (r * r) * jnp.mean(gv * v, axis=-1, keepdims=True)))
        dyz = jnp.concatenate(dyz, axis=1)
        dnw_ref[...] += jnp.sum(jnp.concatenate(dnw, axis=1), axis=0, keepdims=True)
        dy = dyz * sz
        dz_ref[...] = (dyz * y * (sig * (1.0 + z * (1.0 - sig)))).astype(BF16)
        dy_b = dy.astype(BF16)

        tril = _iota((CH, CH), 1) <= _iota((CH, CH), 0)
        lane_a = _iota((CH, LANE), 1) < 64
        cbm = [_dot_nt(cg[g], bg[g]) for g in range(2)]
        dcbm = [jnp.zeros((CH, CH), F32), jnp.zeros((CH, CH), F32)]
        seg_rows = jnp.zeros((CH, LANE), F32)
        seg_cols = jnp.zeros((LANE, CH), F32)
        row_id, col_id = _iota((CH, LANE), 0), _iota((CH, LANE), 1)
        dx_pairs = []
        for hp in range(NH // 2):
            g = hp // 4
            xp = xdt_b[:, hp * LANE:(hp + 1) * LANE]
            dyp_f = dy[:, hp * LANE:(hp + 1) * LANE]
            dyp = dy_b[:, hp * LANE:(hp + 1) * LANE]
            halves = []
            for k in range(2):
                h = 2 * hp + k
                lane = lane_a if k == 0 else jnp.logical_not(lane_a)
                dec = _decay(q, h, tril)
                gm = cbm[g] * dec
                dgm = _dot_nt(jnp.where(lane, dyp_f, 0.0).astype(BF16), xp)
                dcbm[g] = dcbm[g] + dgm * dec
                prod = dgm * gm
                seg_rows = jnp.where(col_id == h, jnp.sum(prod, axis=1, keepdims=True), seg_rows)
                seg_cols = jnp.where(row_id == h, jnp.sum(prod, axis=0, keepdims=True), seg_cols)
                halves.append(_dot_tn(gm.astype(BF16), dyp))
            dx_pairs.append(jnp.where(lane_a, halves[0], halves[1]))
        dxdt_diag = jnp.concatenate(dx_pairs, axis=1)

        qv = jnp.concatenate([_dot(bg[g], dhb[:, g * 512:(g + 1) * 512]) for g in range(2)], axis=1)
        y_off = jnp.concatenate([_dot(cg[g], hb[:, g * 512:(g + 1) * 512]) for g in range(2)], axis=1) * e_b
        xfq = xdt * f_b * qv
        dxdt = dxdt_diag + f_b * qv
        tdt = jnp.sum(dh_new * h_in, axis=0, keepdims=True) * t_b
        per_head = _pick(jnp.concatenate([xfq, dy * y_off, dxdt * xs, dy * xs, jnp.broadcast_to(tdt, (8, D))],
                                         axis=0), red)
        fdf, dyoff_h, dxdtxs_h, dyxs_h = [per_head[k * CH:(k + 1) * CH] for k in range(4)]
        dcs = seg_rows - seg_cols.T + dyoff_h - fdf
        last = per_head[4 * CH:4 * CH + 1] + jnp.sum(fdf, axis=0, keepdims=True)
        dcs = dcs + jnp.where(_iota((CH, LANE), 0) == CH - 1, last, 0.0)
        tri_t = (_iota((CH, CH), 1) >= _iota((CH, CH), 0)).astype(BF16)
        da = _pick_left(tri_t, dcs)
        ddt = da * q["a_row"] + dxdtxs_h
        dxs = dxdt * dt_b + db_ref[...] * dy
        ddt_raw = ddt * _sigmoid(q["pre"])
        dsm_ref[0:1, :] += jnp.sum(ddt_raw, axis=0, keepdims=True)
        dsm_ref[1:2, :] += jnp.sum(da * q["dt"], axis=0, keepdims=True) * q["a_row"]
        dsm_ref[2:3, :] += jnp.sum(dyxs_h, axis=0, keepdims=True)
        edy = (e_b * dy).astype(BF16)
        xf = (xdt * f_b).astype(BF16)
        dbs, dcs_g, dhs = [], [], []
        for g in range(2):
            sl = slice(g * 512, (g + 1) * 512)
            dcb_b = dcbm[g].astype(BF16)
            dcs_g.append(_dot(dcb_b, bg[g]) + _dot_nt(edy[:, sl], hb[:, sl]))
            dbs.append(_dot_tn(dcb_b, cg[g]) + _dot_nt(xf[:, sl], dhb[:, sl]))
            dhs.append(_dot_tn(cg[g], edy[:, sl]))
        dh_ref[...] = t_b * dh_new + jnp.concatenate(dhs, axis=1)
        dbc = jnp.concatenate(dbs + dcs_g, axis=1)

        def conv_bwd(dact, pre, x_raw, w, nxt_ref, lo):
            s = _sigmoid(pre)
            dconv = dact * (s * (1.0 + pre * (1.0 - s)))
            nxt8 = nxt_ref[...]
            row8 = _iota(nxt8.shape, 0)
            hi = lo + dconv.shape[1]
            dcb_ref[:, lo:hi] += jnp.sum(dconv, axis=0, keepdims=True)
            later = [dconv]
            for s_ in (1, 2, 3):
                rolled = pltpu.roll(dconv, CH - s_, 0)
                tail = jnp.where(row8 >= 8 - s_, pltpu.roll(nxt8, 8 - s_, 0), rolled[CH - 8:])
                later.append(jnp.concatenate([rolled[:CH - 8], tail], axis=0))
            dx = None
            for s_, up in enumerate(later):
                k = 3 - s_
                dcw_ref[k:k + 1, lo:hi] += jnp.sum(up * x_raw, axis=0, keepdims=True)
                dx = w[k:k + 1, :] * up if dx is None else dx + w[k:k + 1, :] * up
            nxt_ref[...] = dconv[:8]
            return dx

        dx_ref[:, 0:D] = conv_bwd(dxs, xs_c, xs_ref[...], cw[:, :D], nxs_ref, 0).astype(BF16)
        dx_ref[:, D:D + 512] = conv_bwd(dbc, bc_c, bc_ref[...], cw[:, D:], nbc_ref, D).astype(BF16)
        dx_ref[:, D + 512:D + 640] = ddt_raw.astype(BF16)
        dx_ref[:, D + 640:] = jnp.zeros((CH, D - 640), BF16)

    def col(width, blk):
        return pl.BlockSpec((CH, width), lambda i: (NC - 1 - i, blk))

    def full(a):
        return pl.BlockSpec(a.shape, lambda i: (0,) * len(a.shape))

    acc_shapes = [SDS((4, 1536), F32), SDS((1, 1536), F32), SDS((8, LANE), F32), SDS((1, D), F32)]
    return pl.pallas_call(
        body, name="ssm_bwd", grid=(NC,),
        in_specs=[col(D, 5), col(512, 12), col(LANE, 52), col(D, 4),
                  col(D, 0), col(D, 0), pl.BlockSpec((None, CH, D), lambda i: (NC - 1 - i, 0, 0)), col(D + 512, 0),
                  full(cw), full(dtb), full(alog), full(d_b), full(nw)],
        out_specs=[col(D, 0), col(2 * D, 0)] + [full(a) for a in acc_shapes],
        out_shape=[SDS((S, D), BF16), SDS((S, 2 * D), BF16)] + acc_shapes,
        scratch_shapes=[pltpu.VMEM((CH, D), F32), pltpu.VMEM((8, D), F32), pltpu.VMEM((8, 512), F32)],
        compiler_params=_cp(("arbitrary",)),
    )(proj, proj, proj, proj, dn, y_save, states, conv_out, cw, dtb, alog, d_b, nw)


def _outproj_loss(mix, w_out, x, tgt, nw, attn_pre, proj):
    tm = 256

    def body(mix_ref, w_ref, x_ref, t_ref, nw_ref, pre_ref, g_ref,
             dy_ref, dn_ref, do_ref, delta_ref, dg_ref, dw_ref, dnw_ref, loss_ref):
        @pl.when(pl.program_id(0) == 0)
        def _():
            dw_ref[...] = jnp.zeros_like(dw_ref)
            dnw_ref[...] = jnp.zeros_like(dnw_ref)
            loss_ref[...] = jnp.zeros_like(loss_ref)

        mixv, w = mix_ref[...], w_ref[...]
        out = _dot(mixv, w)
        r = lax.rsqrt(jnp.mean(out * out, axis=-1, keepdims=True) + EPS)
        nh = out * r
        nw_v = nw_ref[...]
        err = x_ref[...] + nh * nw_v - t_ref[...]
        loss_ref[...] += 0.5 * jnp.sum(jnp.mean(err * err, axis=-1, keepdims=True), axis=0, keepdims=True)
        dy = err * (1.0 / D)
        dy_ref[...] = dy
        dnw_ref[...] += jnp.sum(dy * nh, axis=0, keepdims=True)
        gdn = dy * nw_v
        dout = (r * (gdn - nh * jnp.mean(gdn * nh, axis=-1, keepdims=True))).astype(BF16)
        dmix = _dot_nt(dout, w)
        dw_ref[...] += _dot_tn(mixv, dout)
        dn_ref[...] = dmix[:, D:]
        dm, g, pre_v = dmix[:, :D], g_ref[...], pre_ref[...]
        sig = _sigmoid(g)
        do = dm * (g * sig)
        do_ref[...] = do
        dg_ref[...] = (dm * pre_v * (sig * (1.0 + g * (1.0 - sig)))).astype(BF16)
        prod = do * pre_v
        same_head = (_iota((LANE, LANE), 0) // 64 == _iota((LANE, LANE), 1) // 64).astype(BF16)
        for cb in range(D // LANE):
            delta_ref[:, cb * LANE:(cb + 1) * LANE] = _pick(prod[:, cb * LANE:(cb + 1) * LANE], same_head)

    row = lambda w: pl.BlockSpec((tm, w), lambda i: (i, 0))
    full = lambda s: pl.BlockSpec(s, lambda i: (0, 0))
    return pl.pallas_call(
        body, name="outproj_loss", grid=(S // tm,),
        in_specs=[row(2 * D), full((2 * D, D)), row(D), row(D), full((1, D)), row(D),
                  pl.BlockSpec((tm, D), lambda i: (i, OFF_G // D))],
        out_specs=[row(D), row(D), row(D), row(D), row(D), full((2 * D, D)), full((1, D)), full((1, LANE))],
        out_shape=[SDS((S, D), F32)] * 4 + [SDS((S, D), BF16), SDS((2 * D, D), F32), SDS((1, D), F32),
                                            SDS((1, LANE), F32)],
        compiler_params=_cp(("arbitrary",)),
    )(mix, w_out, x, tgt, nw, attn_pre, proj)


def _inproj_bwd_dx(srcs, dxbcdt, w_all, x, dy, nw, hosted=None):
    tm = 512
    nk = DP // D
    n_host, n_host_out = (len(hosted.arrays), len(hosted.out_shape)) if hosted else (0, 0)

    def body(*refs):
        src_refs = refs[:nk]
        w_ref, x_ref, dy_ref, nw_ref = refs[nk:nk + 4]
        host_in, refs = refs[nk + 4:nk + 4 + n_host], refs[nk + 4 + n_host:]
        gx_ref, dnw_ref = refs[:2]
        host_out, host_sems = refs[2:2 + n_host_out], refs[2 + n_host_out:]
        i = pl.program_id(0)

        @pl.when(i == 0)
        def _():
            if hosted:
                hosted.start(host_in, host_out, host_sems)
            dnw_ref[...] = jnp.zeros_like(dnw_ref)

        du = None
        for k, ref in enumerate(src_refs):
            part = _dot_nt(ref[...], w_ref[:, k * D:(k + 1) * D])
            du = part if du is None else du + part
        xf, nw_v = x_ref[...], nw_ref[...]
        r = lax.rsqrt(jnp.mean(xf * xf, axis=-1, keepdims=True) + EPS)
        xh = xf * r
        dnw_ref[...] += jnp.sum(du * xh, axis=0, keepdims=True)
        gdu = du * nw_v
        gx_ref[...] = r * (gdu - xh * jnp.mean(gdu * xh, axis=-1, keepdims=True)) + dy_ref[...]

        if hosted:
            pl.when(i == S // tm - 1)(lambda: hosted.finish(host_in, host_out, host_sems))

    row = pl.BlockSpec((tm, D), lambda i: (i, 0))
    row1 = pl.BlockSpec((tm, D), lambda i: (i, 1))
    one = pl.BlockSpec((1, D), lambda i: (0, 0))
    whole_w = pl.BlockSpec((D, DP), lambda i: (0, 0), pipeline_mode=pl.Buffered(1))
    args = [*srcs, dxbcdt, dxbcdt, w_all, x, dy, nw]
    in_specs = [row] * len(srcs) + [row, row1, whole_w, row, row, one]
    out_specs, out_shape, scratch = [row, one], [SDS((S, D), F32), SDS((1, D), F32)], []
    if hosted:
        args += hosted.arrays
        in_specs += [ANY] * n_host
        out_specs += [ANY] * n_host_out
        out_shape += hosted.out_shape
        scratch += hosted.scratch
    outs = pl.pallas_call(
        body, name="inproj_bwd_dx", grid=(S // tm,),
        in_specs=in_specs, out_specs=out_specs, out_shape=out_shape, scratch_shapes=scratch,
        compiler_params=_cp(("arbitrary",)),
    )(*args)
    return (outs[:2], outs[2:]) if hosted else outs


def _dw(u, dsec, name):
    ts = 1024
    ncol = dsec.shape[1] // D

    def body(u_ref, d_ref, o_ref):
        @pl.when(pl.program_id(1) == 0)
        def _():
            o_ref[...] = jnp.zeros_like(o_ref)

        o_ref[...] += _dot_tn(u_ref[...], d_ref[...])

    return pl.pallas_call(
        body, name=name, grid=(ncol, S // ts),
        in_specs=[pl.BlockSpec((ts, D), lambda j, i: (i, 0)), pl.BlockSpec((ts, D), lambda j, i: (i, j))],
        out_specs=pl.BlockSpec((D, D), lambda j, i: (0, j)),
        out_shape=SDS((D, ncol * D), F32),
        compiler_params=_cp(("parallel", "arbitrary")),
    )(u, dsec)


def _place():
    x, y, c = lax.axis_index("x"), lax.axis_index("y"), lax.axis_index("c")
    return x, y, c, 2 * x + y


def _chip_of(x, y, k):
    px = 1 - x if k & 2 else x
    py = 1 - y if k & 1 else y
    return px, py, 2 * px + py


def _remote(src, dst, send_sem, recv_sem, dev):
    return pltpu.make_async_remote_copy(src_ref=src, dst_ref=dst, send_sem=send_sem, recv_sem=recv_sem,
                                        device_id=dev, device_id_type=MESH)


def _gather_weights(w_in_b):
    half = w_in_b.shape[0] // 2
    quarter = half // 2

    def body(src, dst, send, recv):
        x, y, c, j = _place()
        me, sib = (x, y, c), (x, y, 1 - c)
        nbr = {"x": _chip_of(x, y, 2), "y": _chip_of(x, y, 1)}
        diag = _chip_of(x, y, 3)[2]
        started, arrivals = [], []

        def rows(n_quarter=None, sibling=False):
            base = (1 - c if sibling else c) * half
            return pl.ds(base, half) if n_quarter is None else pl.ds(base + n_quarter * quarter, quarter)

        def sem(n):
            return send.at[n], recv.at[n]

        def go(cp):
            cp.start()
            started.append(cp)

        own = _remote(src, dst.at[j], *sem(8), sib)
        go(own)
        for n, axis in enumerate("xy"):
            px, py, _ = nbr[axis]
            go(_remote(src.at[rows()], dst.at[j, rows()], *sem(n), (px, py, c)))
        for n, axis in enumerate("xy"):
            ox, oy, _ = nbr["y" if axis == "x" else "x"]
            pj = nbr[axis][2]
            _remote(src.at[rows()], dst.at[pj, rows()], *sem(n), me).wait_recv()
            go(_remote(dst.at[pj, rows(n)], dst.at[pj, rows(n)], *sem(2 + n), (ox, oy, c)))
            go(_remote(dst.at[pj, rows()], dst.at[pj, rows()], *sem(4 + n), sib))
            arrivals.append(_remote(src.at[rows()], dst.at[pj, rows(None, True)], *sem(4 + n), me))
        for n in range(2):
            _remote(dst.at[diag, rows(n)], dst.at[diag, rows(n)], *sem(2 + n), me).wait_recv()
            go(_remote(dst.at[diag, rows(n)], dst.at[diag, rows(n)], *sem(6 + n), sib))
            arrivals.append(_remote(dst.at[diag, rows(n, True)], dst.at[diag, rows(n, True)], *sem(6 + n), me))
        for cp in arrivals + [own]:
            cp.wait_recv()
        for cp in started:
            cp.wait_send()

    return pl.pallas_call(
        body, name="gather_weights", in_specs=[ANY], out_specs=ANY,
        out_shape=SDS((4,) + w_in_b.shape, BF16),
        scratch_shapes=[pltpu.SemaphoreType.DMA((9,)), pltpu.SemaphoreType.DMA((9,))],
        compiler_params=pltpu.CompilerParams(has_side_effects=True),
    )(w_in_b)


class _LateGather:
    def __init__(self, w_out_b, conv_w):
        self.arrays = [w_out_b, conv_w]
        self.out_shape = [SDS((4,) + w_out_b.shape, BF16), SDS((4,) + conv_w.shape, F32)]
        self.scratch = [pltpu.SemaphoreType.DMA((11,)), pltpu.SemaphoreType.DMA((11,))]

    def _plan(self, ins, outs, sems):
        x, y, c, j = _place()
        send, recv = sems
        (wo, cw), (gwo, gcw) = ins, outs
        half = wo.shape[0] // 2
        mine, theirs = pl.ds(c * half, half), pl.ds((1 - c) * half, half)
        me, sib = (x, y, c), (x, y, 1 - c)
        first, arrive, forward, last = [], [], [], []
        for k in (1, 2, 3):
            px, py, pj = _chip_of(x, y, k)
            first += [_remote(wo.at[mine], gwo.at[j, mine], send.at[k - 1], recv.at[k - 1], (px, py, c)),
                      _remote(cw, gcw.at[j], send.at[k + 2], recv.at[k + 2], (px, py, c))]
            arrive.append(_remote(wo.at[mine], gwo.at[pj, mine], send.at[k - 1], recv.at[k - 1], me))
            forward.append(_remote(gwo.at[pj, mine], gwo.at[pj, mine], send.at[k + 5], recv.at[k + 5], sib))
            last += [_remote(cw, gcw.at[pj], send.at[k + 2], recv.at[k + 2], me),
                     _remote(wo.at[theirs], gwo.at[pj, theirs], send.at[k + 5], recv.at[k + 5], me)]
        first += [_remote(wo, gwo.at[j], send.at[9], recv.at[9], sib),
                  _remote(cw, gcw.at[j], send.at[10], recv.at[10], sib)]
        last += first[-2:]
        return first, arrive, forward, last

    def start(self, ins, outs, sems):
        for cp in self._plan(ins, outs, sems)[0]:
            cp.start()

    def finish(self, ins, outs, sems):
        first, arrive, forward, last = self._plan(ins, outs, sems)
        for got, fwd in zip(arrive, forward):
            got.wait_recv()
            fwd.start()
        for cp in last:
            cp.wait_recv()
        for cp in first + forward:
            cp.wait_send()


def _window(s, names):
    lo, hi = TILES * s, TILES * s + TILES + 1
    pieces = []
    for n, name in enumerate(names):
        a, count = SECTION_TILES[name]
        first, last = max(lo, a), min(hi, a + count)
        if first < last:
            pieces.append((n, first - a, last - first, first - lo))
    assert sum(p[2] for p in pieces) == TILES + 1
    return pieces


class _PairExchange:
    def __init__(self, names, sections, shards, more=()):
        self.names, self.shards = names, shards
        self.arrays = list(sections) + list(more)
        self.out_shape = [SDS((len(shards), sections[0].shape[0] // 2, WIN), F32)]
        self.out_shape += [SDS((a.shape[0], a.shape[1] // 2, a.shape[2]), F32) for a in more]
        n = sum(len(_window(s, names)) for s in shards) + len(more)
        self.scratch = [pltpu.SemaphoreType.DMA((n,)) for _ in range(2)]

    def _copies(self, ins, outs, sems):
        x, y, c, _ = _place()
        sib = (x, y, 1 - c)
        half = ins[0].shape[0] // 2
        rows = pl.ds((1 - c) * half, half)
        k = 0
        for i, s in enumerate(self.shards):
            for n, tile, tiles, at in _window(s, self.names):
                yield _remote(ins[n].at[rows, pl.ds(tile * LANE, tiles * LANE)],
                              outs[0].at[i, :, pl.ds(at * LANE, tiles * LANE)], sems[0].at[k], sems[1].at[k], sib)
                k += 1
        for src, dst in zip(ins[len(self.names):], outs[1:]):
            half = src.shape[1] // 2
            yield _remote(src.at[:, pl.ds((1 - c) * half, half)], dst, sems[0].at[k], sems[1].at[k], sib)
            k += 1

    def start(self, ins, outs, sems):
        for cp in self._copies(ins, outs, sems):
            cp.start()

    def finish(self, ins, outs, sems):
        for cp in self._copies(ins, outs, sems):
            cp.wait()


def _exchange_call(exchange, name):
    n, n_out = len(exchange.arrays), len(exchange.out_shape)

    def body(*refs):
        ins, outs, sems = refs[:n], refs[n:n + n_out], refs[n + n_out:]
        exchange.start(ins, outs, sems)
        exchange.finish(ins, outs, sems)

    return pl.pallas_call(
        body, name=name, in_specs=[ANY] * n, out_specs=[ANY] * n_out, out_shape=exchange.out_shape,
        scratch_shapes=exchange.scratch, compiler_params=pltpu.CompilerParams(has_side_effects=True),
    )(*exchange.arrays)


def _pair_sum_windows(cidx, names, sections, shards, r, name):
    n, half, _ = r.shape
    tr = min(half, 256)
    nt = half // tr

    def body(c_ref, *refs):
        del c_ref
        secs, r_ref, o_ref = refs[:-2], refs[-2], refs[-1]
        for i, s in enumerate(shards):
            for k, tile, tiles, at in _window(s, names):
                own = secs[k][:, tile * LANE:(tile + tiles) * LANE]
                there = slice(at * LANE, (at + tiles) * LANE)
                o_ref[i, :, there] = (own + r_ref[i, :, there]).astype(BF16)

    window = pl.BlockSpec((n, tr, WIN), lambda t, c: (0, t, 0))
    return pl.pallas_call(
        body, name=name,
        grid_spec=pltpu.PrefetchScalarGridSpec(
            num_scalar_prefetch=1, grid=(nt,),
            in_specs=[pl.BlockSpec((tr, a.shape[1]), lambda t, c: (c[0] * nt + t, 0)) for a in sections] + [window],
            out_specs=window),
        out_shape=SDS(r.shape, BF16),
        compiler_params=_cp(("parallel",)),
    )(cidx, *sections, r)


def _pair_sum(cidx, g, r, name):
    n, half, width = r.shape
    tr = min(half, 256)
    nt = half // tr

    def body(c_ref, g_ref, r_ref, o_ref):
        del c_ref
        o_ref[...] = (g_ref[...] + r_ref[...]).astype(BF16)

    return pl.pallas_call(
        body, name=name,
        grid_spec=pltpu.PrefetchScalarGridSpec(
            num_scalar_prefetch=1, grid=(n, nt),
            in_specs=[pl.BlockSpec((None, tr, width), lambda s, t, c: (s, c[0] * nt + t, 0)),
                      pl.BlockSpec((None, tr, width), lambda s, t, c: (s, t, 0))],
            out_specs=pl.BlockSpec((None, tr, width), lambda s, t, c: (s, t, 0))),
        out_shape=SDS(r.shape, BF16),
        compiler_params=_cp(("parallel", "parallel")),
    )(cidx, g, r)


class _ChipExchange:
    def __init__(self, arrays, rows):
        self.arrays, self.rows = list(arrays), list(rows)
        self.out_shape = [SDS((4,) + a.shape[1:], BF16) for a in self.arrays]
        self.scratch = [pltpu.SemaphoreType.DMA((3 * len(self.arrays),)) for _ in range(2)]

    def _copies(self, ins, outs, sems):
        x, y, c, j = _place()
        send, recv = sems
        for a, (src, dst, row) in enumerate(zip(ins, outs, self.rows)):
            for k in (1, 2, 3):
                px, py, pj = _chip_of(x, y, k)
                n = 3 * a + k - 1
                slot = pj if row is None else py
                yield (None if row is None else px == row, None if row is None else x == row,
                       _remote(src.at[slot], dst.at[j], send.at[n], recv.at[n], (px, py, c)),
                       _remote(src.at[0], dst.at[pj], send.at[n], recv.at[n], (x, y, c)))

    def start(self, ins, outs, sems):
        for sends, _, send, _ in self._copies(ins, outs, sems):
            if sends is None:
                send.start()
            else:
                pl.when(sends)(send.start)

    def finish(self, ins, outs, sems):
        for sends, owns, send, arrival in self._copies(ins, outs, sems):
            if sends is None:
                arrival.wait_recv()
                send.wait_send()
            else:
                pl.when(owns)(arrival.wait_recv)
                pl.when(sends)(send.wait_send)


def _all_gather_rows(src, dst, rows, send, recv, local_sem):
    x, y, c, j = _place()
    me = 2 * j + c
    local = pltpu.make_async_copy(src, dst.at[me, rows], local_sem)
    cps, arrivals = [], []
    for k in range(1, 8):
        px, py, pj = _chip_of(x, y, k >> 1)
        pc = 1 - c if k & 1 else c
        cps.append(_remote(src, dst.at[me, rows], send.at[k - 1], recv.at[k - 1], (px, py, pc)))
        arrivals.append(_remote(src, dst.at[2 * pj + pc, rows], send.at[k - 1], recv.at[k - 1], (x, y, c)))
    starts = [local.start] + [cp.start for cp in cps]
    waits = [cp.wait_recv for cp in arrivals] + [cp.wait_send for cp in cps] + [local.wait]
    return starts, waits


class _SmallExchange:
    def __init__(self, small):
        self.arrays = [small]
        self.out_shape = [SDS((8,) + small.shape, F32)]
        self.scratch = [pltpu.SemaphoreType.DMA((7,)), pltpu.SemaphoreType.DMA((7,)), pltpu.SemaphoreType.DMA]

    def start(self, ins, outs, sems):
        for go in _all_gather_rows(ins[0], outs[0], slice(None), *sems)[0]:
            go()

    def finish(self, ins, outs, sems):
        for wait in _all_gather_rows(ins[0], outs[0], slice(None), *sems)[1]:
            wait()


class _Both:
    def __init__(self, a, b):
        self.parts = (a, b)
        self.arrays, self.out_shape, self.scratch = a.arrays + b.arrays, a.out_shape + b.out_shape, a.scratch + b.scratch

    def _split(self, ins, outs, sems):
        a, b = self.parts
        return ((a, ins[:len(a.arrays)], outs[:len(a.out_shape)], sems[:len(a.scratch)]),
                (b, ins[len(a.arrays):], outs[len(a.out_shape):], sems[len(a.scratch):]))

    def start(self, ins, outs, sems):
        for part, *refs in self._split(ins, outs, sems):
            part.start(*refs)

    def finish(self, ins, outs, sems):
        for part, *refs in self._split(ins, outs, sems):
            part.finish(*refs)


def _slot_sum(r, name):
    n, rows, width = r.shape
    tr = min(rows, 256)

    def body(r_ref, o_ref):
        acc = r_ref[0].astype(F32)
        for s in range(1, n):
            acc = acc + r_ref[s].astype(F32)
        o_ref[...] = acc

    return pl.pallas_call(
        body, name=name, grid=(rows // tr,),
        in_specs=[pl.BlockSpec((n, tr, width), lambda t: (0, t, 0))],
        out_specs=pl.BlockSpec((tr, width), lambda t: (t, 0)),
        out_shape=SDS((rows, width), F32),
        compiler_params=_cp(("parallel",)),
    )(r)


def _chip_sum(where, recv, own, name):
    n, rows, width = recv.shape
    tr = min(rows, 256)
    nt = rows // tr

    def body(j_ref, r_ref, own_ref, o_ref):
        acc = None
        for s in range(n):
            term = jnp.where(j_ref[0] == s, own_ref[...], r_ref[s]).astype(F32)
            acc = term if acc is None else acc + term
        o_ref[...] = acc

    return pl.pallas_call(
        body, name=name,
        grid_spec=pltpu.PrefetchScalarGridSpec(
            num_scalar_prefetch=1, grid=(nt,),
            in_specs=[pl.BlockSpec((n, tr, width), lambda t, j: (0, t, 0)),
                      pl.BlockSpec((None, tr, width), lambda t, j: (j[0], t, 0))],
            out_specs=pl.BlockSpec((tr, width), lambda t, j: (j[1] * nt + t, 0))),
        out_shape=SDS((2 * rows, width), F32),
        compiler_params=_cp(("parallel",)),
    )(where, recv, own)


def _chip_sum_rows(place, recv0, own0, recv1, own1, name):
    n, rows, width = recv0.shape
    tr = min(rows, 256)
    nt = rows // tr

    def body(p_ref, r0_ref, o0_ref, r1_ref, o1_ref, o_ref):
        first_row = p_ref[2] == 0
        own = jnp.where(first_row, o0_ref[...], o1_ref[...])
        acc = None
        for s in range(n):
            term = jnp.where(p_ref[0] == s, own, jnp.where(first_row, r0_ref[s], r1_ref[s])).astype(F32)
            acc = term if acc is None else acc + term
        o_ref[...] = acc

    recv = pl.BlockSpec((n, tr, width), lambda t, p: (0, t, 0))
    own = pl.BlockSpec((None, tr, width), lambda t, p: (p[3], t, 0))
    return pl.pallas_call(
        body, name=name,
        grid_spec=pltpu.PrefetchScalarGridSpec(
            num_scalar_prefetch=1, grid=(nt,), in_specs=[recv, own, recv, own],
            out_specs=pl.BlockSpec((tr, width), lambda t, p: (p[1] * nt + t, 0))),
        out_shape=SDS((2 * rows, width), F32),
        compiler_params=_cp(("parallel",)),
    )(place, recv0, own0, recv1, own1)


def _half_exchange(gw, go, gathered, late, row):
    def body(gw_in, go_in, ga_in, late_ref, gw_ref, go_ref, ga_ref, send, recv, late_send, late_recv, late_local):
        del gw_in, go_in, ga_in
        x, y, c, _ = _place()
        starts, waits = _all_gather_rows(late_ref, ga_ref, pl.ds(row, late.shape[0]), late_send, late_recv,
                                         late_local)
        for go_ in starts:
            go_()
        mine = [pl.ds(c * (r.shape[0] // 2), r.shape[0] // 2) for r in (gw_ref, go_ref)]
        cps = [_remote(r.at[rows], r.at[rows], send.at[k], recv.at[k], (x, y, 1 - c))
               for k, (r, rows) in enumerate(zip((gw_ref, go_ref), mine))]
        for cp in cps:
            cp.start()
        for k, r in enumerate((gw_ref, go_ref)):
            theirs = pl.ds((1 - c) * (r.shape[0] // 2), r.shape[0] // 2)
            _remote(r.at[theirs], r.at[theirs], send.at[k], recv.at[k], (x, y, c)).wait_recv()
        for cp in cps:
            cp.wait_send()
        for wait in waits:
            wait()

    return pl.pallas_call(
        body, name="half_exchange", in_specs=[ANY] * 4, out_specs=[ANY] * 3,
        out_shape=[SDS(gw.shape, F32), SDS(go.shape, F32), SDS(gathered.shape, F32)],
        input_output_aliases={0: 0, 1: 1, 2: 2},
        scratch_shapes=[pltpu.SemaphoreType.DMA((2,)), pltpu.SemaphoreType.DMA((2,)),
                        pltpu.SemaphoreType.DMA((7,)), pltpu.SemaphoreType.DMA((7,)), pltpu.SemaphoreType.DMA],
        compiler_params=pltpu.CompilerParams(has_side_effects=True),
    )(gw, go, gathered, late)


def _adamw(w, g, m, v, name):
    rows, width = w.shape
    tr = min(rows, 256)

    def body(w_ref, g_ref, m_ref, v_ref, d_ref, nm_ref, nv_ref):
        gv = g_ref[...]
        nm = ADAM_B1 * m_ref[...] + (1.0 - ADAM_B1) * gv
        nv = ADAM_B2 * v_ref[...] + (1.0 - ADAM_B2) * (gv * gv)
        m_hat = nm / (1.0 - ADAM_B1 ** ADAM_STEP)
        v_hat = nv / (1.0 - ADAM_B2 ** ADAM_STEP)
        d_ref[...] = -ADAM_LR * (m_hat / (jnp.sqrt(v_hat) + ADAM_EPS) + ADAM_WD * w_ref[...])
        nm_ref[...] = nm
        nv_ref[...] = nv

    t = pl.BlockSpec((tr, width), lambda i: (i, 0))
    return pl.pallas_call(
        body, name=name, grid=(rows // tr,), in_specs=[t] * 4, out_specs=[t] * 3,
        out_shape=[SDS(w.shape, F32)] * 3, compiler_params=_cp(("parallel",)),
    )(w, g, m, v)


def _rowwise(a):
    return jnp.transpose(a, (2, 0, 1)).reshape(SHARD * D // LANE, LANE)


def _columns(ref):
    return jnp.concatenate([ref[pl.ds(c, LANE, stride=8), :].T for c in range(D // LANE)], axis=0)


def _shard_bf16(chip, w_rows):
    def body(j_ref, w_ref, o_ref, prev_ref):
        t = pl.program_id(0)
        cur = _columns(w_ref)

        @pl.when(t == 0)
        def _():
            prev_ref[...] = jnp.zeros_like(prev_ref)

        lane = _iota((D, LANE), 1)
        for s in range(4):
            @pl.when(j_ref[0] == s)
            def _():
                off = SHIFT * s
                moved = cur if s == 0 else jnp.where(lane < off, pltpu.roll(prev_ref[...], off, 1),
                                                     pltpu.roll(cur, off, 1))
                col = t * LANE + lane - off
                o_ref[...] = jnp.where((col >= 0) & (col < SHARD), moved, 0.0).astype(BF16)
        prev_ref[...] = cur

    return pl.pallas_call(
        body, name="shard_bf16",
        grid_spec=pltpu.PrefetchScalarGridSpec(
            num_scalar_prefetch=1, grid=(TILES + 1,),
            in_specs=[pl.BlockSpec((D, LANE), lambda t, j: (t, 0))],
            out_specs=pl.BlockSpec((D, LANE), lambda t, j: (0, t)),
            scratch_shapes=[pltpu.VMEM((D, LANE), F32)]),
        out_shape=SDS((D, WIN), BF16), compiler_params=_cp(("arbitrary",)),
    )(chip, w_rows)


def _whole_w_in(windows):
    tr = 256
    n = windows.shape[0]

    def body(g_ref, o_ref):
        lane = _iota((tr, LANE), 1)
        for s in range(n):
            first = TILES * s
            head = g_ref[s, :, :LANE]
            if s:
                tail = g_ref[s - 1, :, TILES * LANE:]
                head = jnp.where(lane < SHIFT * s, tail.astype(F32), head.astype(F32)).astype(BF16)
            o_ref[:, first * LANE:(first + 1) * LANE] = head
            o_ref[:, (first + 1) * LANE:(first + TILES) * LANE] = g_ref[s, :, LANE:TILES * LANE]
        o_ref[:, n * TILES * LANE:(n * TILES + 1) * LANE] = g_ref[n - 1, :, TILES * LANE:]
        o_ref[:, (n * TILES + 1) * LANE:] = jnp.zeros((tr, DP - (n * TILES + 1) * LANE), BF16)

    return pl.pallas_call(
        body, name="whole_w_in", grid=(D // tr,),
        in_specs=[pl.BlockSpec((n, tr, WIN), lambda t: (0, t, 0))], out_specs=pl.BlockSpec((tr, DP), lambda t: (t, 0)),
        out_shape=SDS((D, DP), BF16), compiler_params=_cp(("parallel",)),
    )(windows)


def _adamw_in(chip, w_rows, g_win, m_rows, v_rows):
    def body(j_ref, w_ref, g_ref, next_ref, m_ref, v_ref, grad_ref, d_ref, nm_ref, nv_ref):
        columns = _columns
        for s in range(4):
            @pl.when(j_ref[0] == s)
            def _():
                if s == 0:
                    grad_ref[...] = g_ref[...]
                else:
                    back = LANE - SHIFT * s
                    grad_ref[...] = jnp.where(_iota((D, LANE), 1) < back, pltpu.roll(g_ref[...], back, 1),
                                              pltpu.roll(next_ref[...], back, 1))
        gv = grad_ref[...]
        nm = ADAM_B1 * columns(m_ref) + (1.0 - ADAM_B1) * gv
        nv = ADAM_B2 * columns(v_ref) + (1.0 - ADAM_B2) * (gv * gv)
        m_hat = nm / (1.0 - ADAM_B1 ** ADAM_STEP)
        v_hat = nv / (1.0 - ADAM_B2 ** ADAM_STEP)
        d_ref[...] = -ADAM_LR * (m_hat / (jnp.sqrt(v_hat) + ADAM_EPS) + ADAM_WD * columns(w_ref))
        nm_ref[...] = nm
        nv_ref[...] = nv

    tile = pl.BlockSpec((D, LANE), lambda t, j: (0, t))
    next_tile = pl.BlockSpec((D, LANE), lambda t, j: (0, jnp.minimum(t + 1, TILES)))
    rows = pl.BlockSpec((D, LANE), lambda t, j: (t, 0))
    return pl.pallas_call(
        body, name="adamw_in",
        grid_spec=pltpu.PrefetchScalarGridSpec(
            num_scalar_prefetch=1, grid=(TILES + 1,), in_specs=[rows, tile, next_tile, rows, rows],
            out_specs=[tile] * 4),
        out_shape=[SDS((D, SHARD), F32)] * 4, compiler_params=_cp(("parallel",)),
    )(chip, w_rows, g_win, g_win, m_rows, v_rows)


def _rows128(a, rows):
    flat = a.reshape(-1)
    return jnp.pad(flat, (0, rows * LANE - flat.shape[0])).reshape(rows, LANE)


CONV_ROWS = 48


def _pack_small(conv_w, norm_pre, conv_b, ssm_norm, norm_post, dtb, alog, dsk, extra=None):
    cw_rows = CONV_ROWS if conv_w.shape[-1] == 1536 else 16
    extra = jnp.zeros((1, LANE), F32) if extra is None else _rows128(extra, 1)
    vec = jnp.concatenate([_rows128(dtb, 1), _rows128(alog, 1), _rows128(dsk, 1), extra, jnp.zeros((4, LANE), F32)],
                          axis=0)
    return jnp.concatenate([_rows128(conv_w, cw_rows), _rows128(norm_pre, 8), _rows128(conv_b, 16),
                            _rows128(ssm_norm, 8), _rows128(norm_post, 8), vec], axis=0)


def _unpack_small(p, cw_cols):
    cw_rows = CONV_ROWS if cw_cols == 1536 else 16
    o = cw_rows
    conv_w = p[:cw_rows].reshape(-1)[:4 * cw_cols].reshape(1, 4, cw_cols)
    norm_pre = p[o:o + 8].reshape(1, D)
    conv_b = p[o + 8:o + 24].reshape(-1)[:1536].reshape(1, 1536)
    ssm_norm = p[o + 24:o + 32].reshape(1, D)
    norm_post = p[o + 32:o + 40].reshape(1, D)
    vec = p[o + 40:o + 48]
    return conv_w, norm_pre, conv_b, ssm_norm, norm_post, vec[0:1, :NH], vec[1:2, :NH], vec[2:3, :NH], vec[3, 0]


def _pad_lanes(a):
    return jnp.pad(a, ((0, 0), (0, LANE - a.shape[1])))


class _GradReduce:
    LO, HI = ("q", "k", "v", "g"), ("g", "z", "x")

    def __init__(self, xi, yi, ci):
        self.cidx = jnp.reshape(ci, (1,)).astype(jnp.int32)
        self.place = jnp.stack([2 * xi + yi, ci, xi, yi]).astype(jnp.int32)

    def pairs(self, dw_g, dw_z, dw_x, dw_out):
        self.hi = [dw_g, dw_z, dw_x]
        self.go = dw_out.reshape(4, D // 2, D)
        return _PairExchange(self.HI, self.hi, (2, 3), [self.go])

    def first(self, got):
        rw, ro = got
        self.pw_hi = _pair_sum_windows(self.cidx, self.HI, self.hi, (2, 3), rw, "pair_sum_hi")
        self.po = _pair_sum(self.cidx, self.go, ro, "pair_sum_out")
        return _ChipExchange([self.pw_hi, self.po], [1, None])

    def first_done(self, got):
        self.rw_hi, self.ro = got

    def second(self, dw_q, dw_k, dw_v, dw_g, small):
        lo = [dw_q, dw_k, dw_v, dw_g]
        (rw,) = _exchange_call(_PairExchange(self.LO, lo, (0, 1)), "pair_exchange_lo")
        self.pw_lo = _pair_sum_windows(self.cidx, self.LO, lo, (0, 1), rw, "pair_sum_lo")
        return _Both(_ChipExchange([self.pw_lo], [0]), _SmallExchange(small))

    def second_done(self, got):
        self.rw_lo, self.small = got

    def result(self, late, row):
        half_in = _chip_sum_rows(self.place, self.rw_lo, self.pw_lo, self.rw_hi, self.pw_hi, "chip_sum_in")
        half_out = _chip_sum(self.place[0:2], self.ro, self.po, "chip_sum_out")
        return _half_exchange(half_in, half_out, self.small, late, row)


def kernel(x, norm_pre_w, w_in, conv_w, conv_b, dt_bias, a_log, d_skip, ssm_norm_w, w_out, norm_post_w, loss_target, m_norm_pre_w, m_w_in, m_conv_w, m_conv_b, m_dt_bias, m_a_log, m_d_skip, m_ssm_norm_w, m_w_out, m_norm_post_w, v_norm_pre_w, v_w_in, v_conv_w, v_conv_b, v_dt_bias, v_a_log, v_d_skip, v_ssm_norm_w, v_w_out, v_norm_post_w):
    xi, yi, ci = lax.axis_index("x"), lax.axis_index("y"), lax.axis_index("c")
    chip = 2 * xi + yi
    x2, tgt = x[0], loss_target[0]

    chip_idx = jnp.reshape(chip, (1,)).astype(jnp.int32)
    w_rows = _rowwise(w_in)
    w_all = _whole_w_in(_gather_weights(_shard_bf16(chip_idx, w_rows)))
    reduce = _GradReduce(xi, yi, ci)
    grad_x, dnw_pre = _local_step(x2, tgt, w_all, _LateGather(w_out[0].astype(BF16), conv_w[0]), norm_pre_w, conv_b,
                                  dt_bias, a_log, d_skip, ssm_norm_w, norm_post_w, reduce)
    g_win, g_out, small = reduce.result(_rows128(dnw_pre, D // LANE), CONV_ROWS)
    g_small = _slot_sum(small, "small_sum")
    g_cw, g_npre, g_cb, g_nssm, g_npost, g_dtb, g_alog, g_dsk, loss = _unpack_small(g_small, 1536)
    g_cw = lax.dynamic_slice_in_dim(g_cw, chip * 384, 384, axis=2)

    g_in, d_in, nm_in, nv_in = _adamw_in(chip_idx, w_rows, g_win, _rowwise(m_w_in), _rowwise(v_w_in))
    d_out, nm_out, nv_out = _adamw(w_out[0], g_out, m_w_out[0], v_w_out[0], "adamw_out")
    packed = [_pack_small(*t) for t in (
        (conv_w, norm_pre_w, conv_b, ssm_norm_w, norm_post_w, dt_bias, a_log, d_skip),
        (g_cw, g_npre, g_cb, g_nssm, g_npost, g_dtb, g_alog, g_dsk),
        (m_conv_w, m_norm_pre_w, m_conv_b, m_ssm_norm_w, m_norm_post_w, m_dt_bias, m_a_log, m_d_skip),
        (v_conv_w, v_norm_pre_w, v_conv_b, v_ssm_norm_w, v_norm_post_w, v_dt_bias, v_a_log, v_d_skip))]
    small_out = [_unpack_small(p, 384)[:8] for p in _adamw(*packed, "adamw_small")]

    def ordered(cw_, npre, cb_, nssm, npost, dtb_, alog_, dsk_, big_in, big_out):
        return [npre, big_in[None], cw_, cb_, dtb_, alog_, dsk_, nssm, big_out[None], npost]

    grads = ordered(g_cw, g_npre, g_cb, g_nssm, g_npost, g_dtb, g_alog, g_dsk, g_in, g_out)
    deltas = ordered(*small_out[0], d_in, d_out)
    new_m = ordered(*small_out[1], nm_in, nm_out)
    new_v = ordered(*small_out[2], nv_in, nv_out)
    return (loss, grad_x[None], *grads, *deltas, *new_m, *new_v)


def _local_step(x2, tgt, w_all, late, norm_pre_w, conv_b, dt_bias, a_log, d_skip, ssm_norm_w,
                norm_post_w, reduce=None):
    dtb, alog = _pad_lanes(dt_bias), _pad_lanes(a_log)
    d_b = jnp.repeat(d_skip, 64, axis=1)

    if isinstance(late, _LateGather):
        (proj, u), (gout, gcw) = _inproj_fwd(x2, norm_pre_w, w_all, late)
        w_out_all = gout.reshape(2 * D, D)
        cw_all = jnp.concatenate([gcw[0], gcw[1], gcw[2], gcw[3]], axis=1)
    else:
        proj, u = _inproj_fwd(x2, norm_pre_w, w_all)
        w_out_all, cw_all = late
    mix, attn_pre, lse = _attn_fwd(proj, 1, _attn_fwd(proj, 4, _attn_fwd(proj, 16)), final=True)
    mix, y_save, states, conv_out = _ssm_fwd(proj, mix, cw_all, conv_b, dtb, alog, d_b, ssm_norm_w)

    dy, dn_ssm, do, delta, dg, dw_out, dnw_post, loss_part = _outproj_loss(mix, w_out_all, x2, tgt, norm_post_w,
                                                                          attn_pre, proj)
    dz, dxbcdt, dcw, dcb, dvec, dnw_ssm = _ssm_bwd(proj, dn_ssm, y_save, states, conv_out, cw_all, dtb, alog, d_b,
                                                   ssm_norm_w)
    dw_g, dw_z, dw_x = _dw(u, dg, "dw_in_g"), _dw(u, dz, "dw_in_z"), _dw(u, dxbcdt, "dw_in_xbcdt")
    acc = _attn_bwd(proj, do, lse, delta, 16, None, F32, reduce.pairs(dw_g, dw_z, dw_x, dw_out) if reduce else None)
    if reduce:
        acc, got = acc
    acc = _attn_bwd(proj, do, lse, delta, 4, acc, F32, reduce.first(got) if reduce else None)
    if reduce:
        acc, got = acc
        reduce.first_done(got)
    dq, dk, dv = _attn_bwd(proj, do, lse, delta, 1, acc, BF16)
    dw_q, dw_k, dw_v = _dw(u, dq, "dw_in_q"), _dw(u, dk, "dw_in_k"), _dw(u, dv, "dw_in_v")

    def small(dnw_pre):
        return _pack_small(dcw, dnw_pre, dcb, dnw_ssm, dnw_post, dvec[0:1, :NH], dvec[1:2, :NH], dvec[2:3, :NH],
                           loss_part[:, :1])

    res = _inproj_bwd_dx([dq, dk, dv, dg, dz], dxbcdt, w_all, x2, dy, norm_pre_w,
                         reduce.second(dw_q, dw_k, dw_v, dw_g, small(jnp.zeros((1, D), F32))) if reduce else None)
    if reduce:
        res, got = res
        reduce.second_done(got)
        return res
    grad_x, dnw_pre = res
    dw_all = jnp.concatenate([dw_q, dw_k, dw_v, dw_g, dw_z, dw_x], axis=1)
    return grad_x, small(dnw_pre), dw_all, dw_out
```

```python
import functools

import jax
import jax.numpy as jnp
from jax import lax
from jax.experimental import pallas as pl
from jax.experimental.pallas import tpu as pltpu

F32 = jnp.float32
BF16 = jnp.bfloat16
MESH = pl.DeviceIdType.MESH
SDS = jax.ShapeDtypeStruct
ANY = pl.BlockSpec(memory_space=pl.ANY)

S = 4096
D = 1024
DP = 7168
SHARD = 1668
OFF_G, OFF_Z = 3072, 4096
NH = 16
CH = 128
NC = S // CH
EPS = 1e-6
NEG = -1e30
LANE = 128
VMEM_LIMIT = 48 * 1024 * 1024

TILES = SHARD // LANE
WIN = (TILES + 1) * LANE
SHIFT = SHARD - TILES * LANE
SECTION_TILES = {"q": (0, 8), "k": (8, 8), "v": (16, 8), "g": (24, 8), "z": (32, 8), "x": (40, 16)}

ADAM_LR, ADAM_B1, ADAM_B2, ADAM_EPS, ADAM_WD, ADAM_STEP = 0.001, 0.9, 0.999, 1e-08, 0.01, 10


def _cp(sem, **kw):
    return pltpu.CompilerParams(dimension_semantics=sem, vmem_limit_bytes=VMEM_LIMIT, **kw)


def _dot(a, b):
    return jnp.dot(a, b, preferred_element_type=F32)


def _dot_nt(a, b):
    return lax.dot_general(a, b, (((1,), (1,)), ((), ())), preferred_element_type=F32)


def _dot_tn(a, b):
    return lax.dot_general(a, b, (((0,), (0,)), ((), ())), preferred_element_type=F32)


def _pieces(x, n):
    out = []
    for _ in range(n):
        p = x.astype(BF16)
        out.append(p)
        x = x - p.astype(F32)
    return out


def _pick(x, sel, n=2):
    parts = [_dot(p, sel) for p in _pieces(x, n)]
    return functools.reduce(jnp.add, parts)


def _pick_left(sel, x, n=3):
    parts = [_dot(sel, p) for p in _pieces(x, n)]
    return functools.reduce(jnp.add, parts)


def _sigmoid(v):
    return 0.5 * jnp.tanh(0.5 * v) + 0.5


def _iota(shape, dim):
    return lax.broadcasted_iota(jnp.int32, shape, dim)


def _inproj_fwd(x, nw, w_all, hosted=None):
    tm, tn = 1024, 1024
    n_host = len(hosted.arrays) if hosted else 0

    def body(x_ref, nw_ref, w_ref, *refs):
        host_in, (proj_ref, u_ref), refs = refs[:n_host], refs[n_host:n_host + 2], refs[n_host + 2:]
        host_out, host_sems = refs[:n_host], refs[n_host:]
        i, j = pl.program_id(0), pl.program_id(1)
        if hosted:
            pl.when((i == 0) & (j == 0))(lambda: hosted.start(host_in, host_out, host_sems))

        @pl.when(j == 0)
        def _():
            xf = x_ref[...]
            r = lax.rsqrt(jnp.mean(xf * xf, axis=-1, keepdims=True) + EPS)
            u_ref[...] = (xf * r * nw_ref[...]).astype(BF16)

        proj_ref[...] = _dot(u_ref[...], w_ref[...])
        if hosted:
            pl.when((i == S // tm - 1) & (j == DP // tn - 1))(lambda: hosted.finish(host_in, host_out, host_sems))

    outs = pl.pallas_call(
        body, name="inproj_fwd", grid=(S // tm, DP // tn),
        in_specs=[pl.BlockSpec((tm, D), lambda i, j: (i, 0)), pl.BlockSpec((1, D), lambda i, j: (0, 0)),
                  pl.BlockSpec((D, tn), lambda i, j: (0, j))] + [ANY] * n_host,
        out_specs=[pl.BlockSpec((tm, tn), lambda i, j: (i, j)), pl.BlockSpec((tm, D), lambda i, j: (i, 0))]
        + [ANY] * n_host,
        out_shape=[SDS((S, DP), F32), SDS((S, D), BF16)] + (hosted.out_shape if hosted else []),
        scratch_shapes=hosted.scratch if hosted else [],
        compiler_params=_cp(("arbitrary", "arbitrary") if hosted else ("parallel", "arbitrary")),
    )(x, nw, w_all, *(hosted.arrays if hosted else []))
    return (outs[:2], outs[2:]) if hosted else outs


ATTN_QB = {1: 16, 4: 4, 16: 1}


def _unit_rows(r, u, d):
    return pl.ds(r + d * CH * u, CH, stride=d) if d > 1 else pl.ds(CH * u, CH)


def _for_units(d, qb, fn):
    for r in range(d):
        for u in range(qb):
            fn(r, u)


def _attn_mask(has_prev):
    qi, kj = _iota((2 * CH, 2 * CH), 0) & (CH - 1), _iota((2 * CH, 2 * CH), 1)
    cur_ok = (kj >= CH) & (kj - CH <= qi)
    prev_ok = (kj < CH) & (kj >= qi)
    return cur_ok | (prev_ok & has_prev)


def _stack_heads(v, lane_a):
    return jnp.concatenate([jnp.where(lane_a, v, 0.0), jnp.where(lane_a, 0.0, v)], axis=0).astype(BF16)


def _attn_specs(d, qb):
    rows, prows = CH * d * qb, CH * d
    nb = S // rows
    steps = (NH // 2) * nb

    def at(t):
        t = jnp.minimum(t, steps - 1)
        return t % nb, t // nb

    def cur(off):
        return pl.BlockSpec((rows, LANE), lambda t: (at(t)[0], off + at(t)[1]))

    def prev(off):
        return pl.BlockSpec((prows, LANE), lambda t: (jnp.maximum(at(t)[0] * qb - 1, 0), off + at(t)[1]))

    lag = pl.BlockSpec((rows, LANE), lambda t: at(jnp.maximum(t - 1, 0)))
    return nb, steps, cur, prev, lag


def _gather16(src_ref, dense_ref, tmp_ref):
    for a in range(4):
        tmp_ref[...] = src_ref[pl.ds(a, 4 * CH, stride=4), :]
        for b in range(4):
            dense_ref[a + 4 * b] = tmp_ref[pl.ds(b, CH, stride=4), :]


def _scatter16(dense_ref, dst_ref, tmp_ref):
    for a in range(4):
        for b in range(4):
            tmp_ref[pl.ds(b, CH, stride=4), :] = dense_ref[a + 4 * b]
        dst_ref[pl.ds(a, 4 * CH, stride=4), :] = tmp_ref[...]


def _unit_index(r, u, d):
    return (r,) if d == 16 else (_unit_rows(r, u, d), slice(None))


def _unit_kv(p_ref, c_ref, r, u, d):
    prev = p_ref[_unit_index(r, 0, d)] if u == 0 else c_ref[_unit_index(r, u - 1, d)]
    return jnp.concatenate([prev, c_ref[_unit_index(r, u, d)]], axis=0).astype(BF16)


def _dense_scratch(d, n):
    return [pltpu.VMEM((16, CH, LANE), F32)] * n + [pltpu.VMEM((4 * CH, LANE), F32)] if d == 16 else []


def _attn_fwd(proj, d, prior=None, final=False):
    qb = ATTN_QB[d]
    nb, steps, cur, prev, _ = _attn_specs(d, qb)
    n_prior = 2 if prior is not None else 0
    n_in, n_out = 5 + n_prior + final, 2 + final
    assert not (d == 16 and (n_prior or final))

    def body(*refs):
        ins, outs, scratch = refs[:n_in], refs[n_in:n_in + n_out], refs[n_in + n_out:]
        if d == 16:
            tmp_ref = scratch[-1]
            for src, dense in zip(ins, scratch):
                _gather16(src, dense, tmp_ref)
            block_outs, ins, outs = outs, scratch[:n_in], scratch[n_in:n_in + n_out]
        q_ref, kp_ref, kc_ref, vp_ref, vc_ref = ins[:5]
        prior_refs = ins[5:5 + n_prior]
        if final:
            g_ref, (mix_ref, o_ref, l_ref) = ins[-1], outs
        else:
            o_ref, l_ref = outs
        i = pl.program_id(0) % nb
        lane_a = _iota((CH, LANE), 1) < 64
        mask_first, mask_rest = _attn_mask(i > 0), _attn_mask(True)

        def unit(r, u):
            at = _unit_index(r, u, d)
            q2 = _stack_heads(q_ref[at] * 0.125, lane_a)
            k2, v2 = _unit_kv(kp_ref, kc_ref, r, u, d), _unit_kv(vp_ref, vc_ref, r, u, d)
            s = jnp.where(mask_first if u == 0 else mask_rest, _dot_nt(q2, k2), NEG)
            m = jnp.max(s, axis=1, keepdims=True)
            p = jnp.exp(s - m)
            l = jnp.sum(p, axis=1, keepdims=True)
            o2 = _dot(p.astype(BF16), v2) / l
            lse2 = m + jnp.log(l)
            o = jnp.where(lane_a, o2[:CH], o2[CH:])
            lse = jnp.where(lane_a, lse2[:CH], lse2[CH:])
            if n_prior:
                o_a, l_a = prior_refs[0][at], prior_refs[1][at]
                top = jnp.maximum(l_a, lse)
                e_a, e_b = jnp.exp(l_a - top), jnp.exp(lse - top)
                tot = e_a + e_b
                o = (e_a * o_a + e_b * o) / tot
                lse = top + jnp.log(tot)
            o_ref[at] = o
            l_ref[at] = lse
            if final:
                g = g_ref[at]
                mix_ref[at] = (o * (g * _sigmoid(g))).astype(BF16)

        _for_units(d, qb, unit)
        if d == 16:
            for dense, dst in zip(outs, block_outs):
                _scatter16(dense, dst, tmp_ref)

    in_specs = [cur(0), prev(8), cur(8), prev(16), cur(16)] + [cur(0)] * n_prior
    args = [proj] * 5 + (list(prior) if n_prior else [])
    out_specs, out_shape = [cur(0), cur(0)], [SDS((S, D), F32), SDS((S, D), F32)]
    if final:
        assert d == 1
        in_specs.append(cur(OFF_G // LANE))
        args.append(proj)
        out_specs, out_shape = [cur(0)] + out_specs, [SDS((S, 2 * D), BF16)] + out_shape
    return pl.pallas_call(
        body, name=f"attn_fwd_d{d}", grid=(steps,),
        in_specs=in_specs, out_specs=out_specs, out_shape=out_shape,
        scratch_shapes=_dense_scratch(d, n_in + n_out),
        compiler_params=_cp(("parallel",)),
    )(*args)


def _attn_bwd(proj, do, lse, delta, d, acc, out_dtype, hosted=None):
    qb = ATTN_QB[d]
    nb, steps, cur, prev, lag = _attn_specs(d, qb)
    has_acc = acc is not None
    n_in = 11 if has_acc else 8
    n_host, n_host_out = (len(hosted.arrays), len(hosted.out_shape)) if hosted else (0, 0)
    assert not (d == 16 and (has_acc or out_dtype != F32))
    rows = CH * d * qb
    carry = (2, 16, CH, LANE) if d == 16 else (2, rows, LANE)

    def body(*refs):
        ins, host_in, refs = refs[:n_in], refs[n_in:n_in + n_host], refs[n_in + n_host:]
        (dq_ref, dk_ref, dv_ref), host_out, scratch = refs[:3], refs[3:3 + n_host_out], refs[3 + n_host_out:]
        if hosted:
            scratch, host_sems = scratch[:-len(hosted.scratch)], scratch[-len(hosted.scratch):]
        ck_ref, cv_ref = scratch[:2]
        dq_f32 = dq_ref if out_dtype == F32 else scratch[2]
        t = pl.program_id(0)
        i = t % nb
        if hosted:
            pl.when(t == 0)(lambda: hosted.start(host_in, host_out, host_sems))
        if d == 16:
            dense, dq_f32, tmp_ref = scratch[2:2 + n_in], scratch[2 + n_in], scratch[-1]

            @pl.when(t < steps)
            def _():
                for src, dst in zip(ins, dense):
                    _gather16(src, dst, tmp_ref)

            ins = dense
        q_ref, kp_ref, kc_ref, vp_ref, vc_ref, do_ref, lse_ref, dl_ref = ins[:8]
        if has_acc:
            aq_ref, ak_ref, av_ref = ins[8:11]
        slot = t & 1
        now_k, now_v, old_k, old_v = ck_ref.at[slot], cv_ref.at[slot], ck_ref.at[1 - slot], cv_ref.at[1 - slot]
        lane_a = _iota((CH, LANE), 1) < 64
        mask_first, mask_rest = _attn_mask(i > 0), _attn_mask(True)

        @pl.when(t == 0)
        def _():
            ck_ref[1] = jnp.zeros(carry[1:], F32)
            cv_ref[1] = jnp.zeros(carry[1:], F32)

        def unit(r, u):
            at = _unit_index(r, u, d)
            q2 = _stack_heads(q_ref[at] * 0.125, lane_a)
            do2 = _stack_heads(do_ref[at], lane_a)
            k2, v2 = _unit_kv(kp_ref, kc_ref, r, u, d), _unit_kv(vp_ref, vc_ref, r, u, d)
            lsev, dlv = lse_ref[at], dl_ref[at]
            lse2 = jnp.concatenate([lsev[:, 0:1], lsev[:, 64:65]], axis=0)
            dl2 = jnp.concatenate([dlv[:, 0:1], dlv[:, 64:65]], axis=0)
            p = jnp.exp(jnp.where(mask_first if u == 0 else mask_rest, _dot_nt(q2, k2), NEG) - lse2)
            ds = (p * (_dot_nt(do2, v2) - dl2)).astype(BF16)
            dq2 = _dot(ds, k2)
            dk2 = _dot_tn(ds, q2)
            dv2 = _dot_tn(p.astype(BF16), do2)
            dq = jnp.where(lane_a, dq2[:CH], dq2[CH:]) * 0.125
            if has_acc:
                dq = dq + aq_ref[at]
            dq_f32[at] = dq
            if u == 0:
                before = _unit_index(r, qb - 1, d)
                old_k[before] += dk2[:CH]
                old_v[before] += dv2[:CH]
            else:
                before = _unit_index(r, u - 1, d)
                now_k[before] += dk2[:CH]
                now_v[before] += dv2[:CH]
            now_k[at] = dk2[CH:]
            now_v[at] = dv2[CH:]

        @pl.when(t < steps)
        def _():
            _for_units(d, qb, unit)
            if d == 16:
                _scatter16(dq_f32, dq_ref, tmp_ref)
            elif out_dtype != F32:
                dq_ref[...] = dq_f32[...].astype(out_dtype)

        if d == 16:
            _scatter16(old_k, dk_ref, tmp_ref)
            _scatter16(old_v, dv_ref, tmp_ref)
        else:
            dk, dv = old_k[...], old_v[...]
            if has_acc:
                dk, dv = dk + ak_ref[...], dv + av_ref[...]
            dk_ref[...] = dk.astype(out_dtype)
            dv_ref[...] = dv.astype(out_dtype)
        if hosted:
            pl.when(t == steps)(lambda: hosted.finish(host_in, host_out, host_sems))

    in_specs = [cur(0), prev(8), cur(8), prev(16), cur(16), cur(0), cur(0), cur(0)]
    args = [proj, proj, proj, proj, proj, do, lse, delta]
    if has_acc:
        in_specs += [cur(0), lag, lag]
        args += list(acc)
    scratch = [pltpu.VMEM(carry, F32), pltpu.VMEM(carry, F32)]
    if d == 16:
        scratch += _dense_scratch(d, n_in + 1)
    elif out_dtype != F32:
        scratch.append(pltpu.VMEM((rows, LANE), F32))
    out_specs, out_shape = [cur(0), lag, lag], [SDS((S, D), out_dtype)] * 3
    if hosted:
        args += hosted.arrays
        in_specs += [ANY] * n_host
        out_specs += [ANY] * n_host_out
        out_shape += hosted.out_shape
        scratch += hosted.scratch
    outs = pl.pallas_call(
        body, name=f"attn_bwd_d{d}", grid=(steps + 1,),
        in_specs=in_specs, out_specs=out_specs, out_shape=out_shape,
        scratch_shapes=scratch, compiler_params=_cp(("arbitrary",)),
    )(*args)
    return (outs[:3], outs[3:]) if hosted else outs


def _conv_taps(cur, prev8, first):
    row8 = _iota(prev8.shape, 0)
    prev8 = jnp.where(first, 0.0, prev8)
    taps = []
    for s in (3, 2, 1):
        rolled = pltpu.roll(cur, s, 0)
        head = jnp.where(row8 < s, pltpu.roll(prev8, s, 0), rolled[:8])
        taps.append(jnp.concatenate([head, rolled[8:]], axis=0))
    return taps + [cur]


def _conv(taps, w, b):
    acc = b + w[0:1, :] * taps[0]
    for k in (1, 2, 3):
        acc = acc + w[k:k + 1, :] * taps[k]
    return acc


def _expand():
    return (_iota((LANE, D), 1) // 64 == _iota((LANE, D), 0)).astype(BF16)


def _reduce():
    return (_iota((D, LANE), 0) // 64 == _iota((D, LANE), 1)).astype(BF16)


def _ssd_common(xs_c, bc_c, dt_raw, dtb, alog):
    head_lane = _iota((CH, LANE), 1) < NH
    xs = xs_c * _sigmoid(xs_c)
    bc = bc_c * _sigmoid(bc_c)
    pre = dt_raw + dtb
    dt = jnp.where(head_lane, jnp.maximum(pre, 0.0) + jnp.log(1.0 + jnp.exp(-jnp.abs(pre))), 0.0)
    a_row = jnp.where(head_lane[0:1], -jnp.exp(alog), 0.0)
    tri = (_iota((CH, CH), 1) <= _iota((CH, CH), 0)).astype(BF16)
    cs = _pick_left(tri, dt * a_row)
    cs_last = cs[CH - 1:CH, :]
    wide = _pick(jnp.concatenate([dt, jnp.exp(cs), jnp.exp(cs_last - cs)], axis=0), _expand())
    dt_b, e_b, f_b = wide[:CH], wide[CH:2 * CH], wide[2 * CH:]
    return dict(xs=xs, bc=bc, pre=pre, dt=dt, a_row=a_row, cs=cs, cs_t=cs.T, dt_b=dt_b, e_b=e_b, f_b=f_b,
                t_b=e_b[CH - 1:CH, :])


def _groups(bc):
    bcb = bc.astype(BF16)
    return [bcb[:, 0:128], bcb[:, 128:256]], [bcb[:, 256:384], bcb[:, 384:512]]


def _decay(q, h, tril):
    seg = q["cs"][:, h:h + 1] - q["cs_t"][h:h + 1, :]
    return jnp.exp(jnp.where(tril, seg, NEG))


def _ssm_fwd(proj, mix, cw, cb, dtb, alog, d_b, nw):
    def body(xs_ref, xsp_ref, bc_ref, bcp_ref, dt_ref, z_ref, cw_ref, cb_ref, dtb_ref, alog_ref, db_ref, nw_ref,
             mix_in_ref, mix_ref, y_ref, st_ref, conv_ref, h_ref):
        del mix_in_ref
        i = pl.program_id(0)

        @pl.when(i == 0)
        def _():
            h_ref[...] = jnp.zeros_like(h_ref)

        cw, cb = cw_ref[...], cb_ref[...]
        xs_c = _conv(_conv_taps(xs_ref[...], xsp_ref[...], i == 0), cw[:, :D], cb[:, :D])
        bc_c = _conv(_conv_taps(bc_ref[...], bcp_ref[...], i == 0), cw[:, D:], cb[:, D:])
        conv_ref[:, :D] = xs_c
        conv_ref[:, D:] = bc_c
        q = _ssd_common(xs_c, bc_c, dt_ref[...], dtb_ref[...], alog_ref[...])
        bg, cg = _groups(q["bc"])
        xs = q["xs"]
        xdt = xs * q["dt_b"]
        xdt_b = xdt.astype(BF16)
        h_in = h_ref[...]
        st_ref[...] = h_in
        hb = h_in.astype(BF16)
        tril = _iota((CH, CH), 1) <= _iota((CH, CH), 0)
        lane_a = _iota((CH, LANE), 1) < 64
        cbm = [_dot_nt(cg[g], bg[g]) for g in range(2)]
        pairs = []
        for hp in range(NH // 2):
            xp = xdt_b[:, hp * LANE:(hp + 1) * LANE]
            ya = _dot((cbm[hp // 4] * _decay(q, 2 * hp, tril)).astype(BF16), xp)
            yb = _dot((cbm[hp // 4] * _decay(q, 2 * hp + 1, tril)).astype(BF16), xp)
            pairs.append(jnp.where(lane_a, ya, yb))
        y_diag = jnp.concatenate(pairs, axis=1)
        y_off = jnp.concatenate([_dot(cg[g], hb[:, g * 512:(g + 1) * 512]) for g in range(2)], axis=1) * q["e_b"]
        y = y_diag + y_off + db_ref[...] * xs
        y_ref[...] = y
        xf = (xdt * q["f_b"]).astype(BF16)
        h_ref[...] = q["t_b"] * h_in + jnp.concatenate(
            [_dot_tn(bg[g], xf[:, g * 512:(g + 1) * 512]) for g in range(2)], axis=1)
        z = z_ref[...]
        yz = y * (z * _sigmoid(z))
        outs = []
        for g in range(2):
            v = yz[:, g * 512:(g + 1) * 512]
            outs.append(v * lax.rsqrt(jnp.mean(v * v, axis=-1, keepdims=True) + EPS))
        mix_ref[...] = (jnp.concatenate(outs, axis=1) * nw_ref[...]).astype(BF16)

    def col(width, blk, prev=False):
        if prev:
            return pl.BlockSpec((8, width), lambda i: (jnp.maximum(i * (CH // 8) - 1, 0), blk))
        return pl.BlockSpec((CH, width), lambda i: (i, blk))

    def full(a):
        return pl.BlockSpec(a.shape, lambda i: (0,) * a.ndim)

    return pl.pallas_call(
        body, name="ssm_fwd", grid=(NC,),
        in_specs=[col(D, 5), col(D, 5, True), col(512, 12), col(512, 12, True), col(LANE, 52), col(D, 4),
                  full(cw), full(cb), full(dtb), full(alog), full(d_b), full(nw), ANY],
        out_specs=[col(D, 1), col(D, 0), pl.BlockSpec((None, CH, D), lambda i: (i, 0, 0)), col(D + 512, 0)],
        out_shape=[SDS((S, 2 * D), BF16), SDS((S, D), F32), SDS((NC, CH, D), F32), SDS((S, D + 512), F32)],
        scratch_shapes=[pltpu.VMEM((CH, D), F32)],
        input_output_aliases={12: 0},
        compiler_params=_cp(("arbitrary",)),
    )(proj, proj, proj, proj, proj, proj, cw, cb, dtb, alog, d_b, nw, mix)


def _ssm_bwd(proj, dn, y_save, states, conv_out, cw, dtb, alog, d_b, nw):
    def body(xs_ref, bc_ref, dt_ref, z_ref, dn_ref, y_ref, st_ref, conv_ref,
             cw_ref, dtb_ref, alog_ref, db_ref, nw_ref,
             dz_ref, dx_ref, dcw_ref, dcb_ref, dsm_ref, dnw_ref, dh_ref, nxs_ref, nbc_ref):
        i = pl.program_id(0)
        ci = NC - 1 - i

        @pl.when(i == 0)
        def _():
            for ref in (dcw_ref, dcb_ref, dsm_ref, dnw_ref, dh_ref, nxs_ref, nbc_ref):
                ref[...] = jnp.zeros_like(ref)

        cw = cw_ref[...]
        xs_c, bc_c = conv_ref[:, :D], conv_ref[:, D:]
        q = _ssd_common(xs_c, bc_c, dt_ref[...], dtb_ref[...], alog_ref[...])
        bg, cg = _groups(q["bc"])
        xs, dt_b, e_b, f_b, t_b = q["xs"], q["dt_b"], q["e_b"], q["f_b"], q["t_b"]
        xdt = xs * dt_b
        xdt_b = xdt.astype(BF16)
        h_in = st_ref[...]
        hb = h_in.astype(BF16)
        dh_new = dh_ref[...]
        dhb = dh_new.astype(BF16)
        red = _reduce()

        z, y, dn, nw_v = z_ref[...], y_ref[...], dn_ref[...], nw_ref[...]
        sig = _sigmoid(z)
        sz = z * sig
        yz = y * sz
        gdn = dn * nw_v
        dyz, dnw = [], []
        for g in range(2):
            v, gv = yz[:, g * 512:(g + 1) * 512], gdn[:, g * 512:(g + 1) * 512]
            r = lax.rsqrt(jnp.mean(v * v, axis=-1, keepdims=True) + EPS)
            dnw.append(dn[:, g * 512:(g + 1) * 512] * v * r)
            dyz.append(r * (gv - v * (r * r) * jnp.mean(gv * v, axis=-1, keepdims=True)))
        dyz = jnp.concatenate(dyz, axis=1)
        dnw_ref[...] += jnp.sum(jnp.concatenate(dnw, axis=1), axis=0, keepdims=True)
        dy = dyz * sz
        dz_ref[...] = (dyz * y * (sig * (1.0 + z * (1.0 - sig)))).astype(BF16)
        dy_b = dy.astype(BF16)

        tril = _iota((CH, CH), 1) <= _iota((CH, CH), 0)
        lane_a = _iota((CH, LANE), 1) < 64
        cbm = [_dot_nt(cg[g], bg[g]) for g in range(2)]
        dcbm = [jnp.zeros((CH, CH), F32), jnp.zeros((CH, CH), F32)]
        seg_rows = jnp.zeros((CH, LANE), F32)
        seg_cols = jnp.zeros((LANE, CH), F32)
        row_id, col_id = _iota((CH, LANE), 0), _iota((CH, LANE), 1)
        dx_pairs = []
        for hp in range(NH // 2):
            g = hp // 4
            xp = xdt_b[:, hp * LANE:(hp + 1) * LANE]
            dyp_f = dy[:, hp * LANE:(hp + 1) * LANE]
            dyp = dy_b[:, hp * LANE:(hp + 1) * LANE]
            halves = []
            for k in range(2):
                h = 2 * hp + k
                lane = lane_a if k == 0 else jnp.logical_not(lane_a)
                dec = _decay(q, h, tril)
                gm = cbm[g] * dec
                dgm = _dot_nt(jnp.where(lane, dyp_f, 0.0).astype(BF16), xp)
                dcbm[g] = dcbm[g] + dgm * dec
                prod = dgm * gm
                seg_rows = jnp.where(col_id == h, jnp.sum(prod, axis=1, keepdims=True), seg_rows)
                seg_cols = jnp.where(row_id == h, jnp.sum(prod, axis=0, keepdims=True), seg_cols)
                halves.append(_dot_tn(gm.astype(BF16), dyp))
            dx_pairs.append(jnp.where(lane_a, halves[0], halves[1]))
        dxdt_diag = jnp.concatenate(dx_pairs, axis=1)

        qv = jnp.concatenate([_dot(bg[g], dhb[:, g * 512:(g + 1) * 512]) for g in range(2)], axis=1)
        y_off = jnp.concatenate([_dot(cg[g], hb[:, g * 512:(g + 1) * 512]) for g in range(2)], axis=1) * e_b
        xfq = xdt * f_b * qv
        dxdt = dxdt_diag + f_b * qv
        tdt = jnp.sum(dh_new * h_in, axis=0, keepdims=True) * t_b
        per_head = _pick(jnp.concatenate([xfq, dy * y_off, dxdt * xs, dy * xs, jnp.broadcast_to(tdt, (8, D))],
                                         axis=0), red)
        fdf, dyoff_h, dxdtxs_h, dyxs_h = [per_head[k * CH:(k + 1) * CH] for k in range(4)]
        dcs = seg_rows - seg_cols.T + dyoff_h - fdf
        last = per_head[4 * CH:4 * CH + 1] + jnp.sum(fdf, axis=0, keepdims=True)
        dcs = dcs + jnp.where(_iota((CH, LANE), 0) == CH - 1, last, 0.0)
        tri_t = (_iota((CH, CH), 1) >= _iota((CH, CH), 0)).astype(BF16)
        da = _pick_left(tri_t, dcs)
        ddt = da * q["a_row"] + dxdtxs_h
        dxs = dxdt * dt_b + db_ref[...] * dy
        ddt_raw = ddt * _sigmoid(q["pre"])
        dsm_ref[0:1, :] += jnp.sum(ddt_raw, axis=0, keepdims=True)
        dsm_ref[1:2, :] += jnp.sum(da * q["dt"], axis=0, keepdims=True) * q["a_row"]
        dsm_ref[2:3, :] += jnp.sum(dyxs_h, axis=0, keepdims=True)
        edy = (e_b * dy).astype(BF16)
        xf = (xdt * f_b).astype(BF16)
        dbs, dcs_g, dhs = [], [], []
        for g in range(2):
            sl = slice(g * 512, (g + 1) * 512)
            dcb_b = dcbm[g].astype(BF16)
            dcs_g.append(_dot(dcb_b, bg[g]) + _dot_nt(edy[:, sl], hb[:, sl]))
            dbs.append(_dot_tn(dcb_b, cg[g]) + _dot_nt(xf[:, sl], dhb[:, sl]))
            dhs.append(_dot_tn(cg[g], edy[:, sl]))
        dh_ref[...] = t_b * dh_new + jnp.concatenate(dhs, axis=1)
        dbc = jnp.concatenate(dbs + dcs_g, axis=1)

        def conv_bwd(dact, pre, x_raw, w, nxt_ref, lo):
            s = _sigmoid(pre)
            dconv = dact * (s * (1.0 + pre * (1.0 - s)))
            nxt8 = nxt_ref[...]
            row8 = _iota(nxt8.shape, 0)
            hi = lo + dconv.shape[1]
            dcb_ref[:, lo:hi] += jnp.sum(dconv, axis=0, keepdims=True)
            later = [dconv]
            for s_ in (1, 2, 3):
                rolled = pltpu.roll(dconv, CH - s_, 0)
                tail = jnp.where(row8 >= 8 - s_, pltpu.roll(nxt8, 8 - s_, 0), rolled[CH - 8:])
                later.append(jnp.concatenate([rolled[:CH - 8], tail], axis=0))
            dx = None
            for s_, up in enumerate(later):
                k = 3 - s_
                dcw_ref[k:k + 1, lo:hi] += jnp.sum(up * x_raw, axis=0, keepdims=True)
                dx = w[k:k + 1, :] * up if dx is None else dx + w[k:k + 1, :] * up
            nxt_ref[...] = dconv[:8]
            return dx

        dx_ref[:, 0:D] = conv_bwd(dxs, xs_c, xs_ref[...], cw[:, :D], nxs_ref, 0).astype(BF16)
        dx_ref[:, D:D + 512] = conv_bwd(dbc, bc_c, bc_ref[...], cw[:, D:], nbc_ref, D).astype(BF16)
        dx_ref[:, D + 512:D + 640] = ddt_raw.astype(BF16)
        dx_ref[:, D + 640:] = jnp.zeros((CH, D - 640), BF16)

    def col(width, blk):
        return pl.BlockSpec((CH, width), lambda i: (NC - 1 - i, blk))

    def full(a):
        return pl.BlockSpec(a.shape, lambda i: (0,) * len(a.shape))

    acc_shapes = [SDS((4, 1536), F32), SDS((1, 1536), F32), SDS((8, LANE), F32), SDS((1, D), F32)]
    return pl.pallas_call(
        body, name="ssm_bwd", grid=(NC,),
        in_specs=[col(D, 5), col(512, 12), col(LANE, 52), col(D, 4),
                  col(D, 0), col(D, 0), pl.BlockSpec((None, CH, D), lambda i: (NC - 1 - i, 0, 0)), col(D + 512, 0),
                  full(cw), full(dtb), full(alog), full(d_b), full(nw)],
        out_specs=[col(D, 0), col(2 * D, 0)] + [full(a) for a in acc_shapes],
        out_shape=[SDS((S, D), BF16), SDS((S, 2 * D), BF16)] + acc_shapes,
        scratch_shapes=[pltpu.VMEM((CH, D), F32), pltpu.VMEM((8, D), F32), pltpu.VMEM((8, 512), F32)],
        compiler_params=_cp(("arbitrary",)),
    )(proj, proj, proj, proj, dn, y_save, states, conv_out, cw, dtb, alog, d_b, nw)


def _outproj_loss(mix, w_out, x, tgt, nw, attn_pre, proj):
    tm = 256

    def body(mix_ref, w_ref, x_ref, t_ref, nw_ref, pre_ref, g_ref,
             dy_ref, dn_ref, do_ref, delta_ref, dg_ref, dw_ref, dnw_ref, loss_ref):
        @pl.when(pl.program_id(0) == 0)
        def _():
            dw_ref[...] = jnp.zeros_like(dw_ref)
            dnw_ref[...] = jnp.zeros_like(dnw_ref)
            loss_ref[...] = jnp.zeros_like(loss_ref)

        mixv, w = mix_ref[...], w_ref[...]
        out = _dot(mixv, w)
        r = lax.rsqrt(jnp.mean(out * out, axis=-1, keepdims=True) + EPS)
        nh = out * r
        nw_v = nw_ref[...]
        err = x_ref[...] + nh * nw_v - t_ref[...]
        loss_ref[...] += 0.5 * jnp.sum(jnp.mean(err * err, axis=-1, keepdims=True), axis=0, keepdims=True)
        dy = err * (1.0 / D)
        dy_ref[...] = dy
        dnw_ref[...] += jnp.sum(dy * nh, axis=0, keepdims=True)
        gdn = dy * nw_v
        dout = (r * (gdn - nh * jnp.mean(gdn * nh, axis=-1, keepdims=True))).astype(BF16)
        dmix = _dot_nt(dout, w)
        dw_ref[...] += _dot_tn(mixv, dout)
        dn_ref[...] = dmix[:, D:]
        dm, g, pre_v = dmix[:, :D], g_ref[...], pre_ref[...]
        sig = _sigmoid(g)
        do = dm * (g * sig)
        do_ref[...] = do
        dg_ref[...] = (dm * pre_v * (sig * (1.0 + g * (1.0 - sig)))).astype(BF16)
        prod = do * pre_v
        same_head = (_iota((LANE, LANE), 0) // 64 == _iota((LANE, LANE), 1) // 64).astype(BF16)
        for cb in range(D // LANE):
            delta_ref[:, cb * LANE:(cb + 1) * LANE] = _pick(prod[:, cb * LANE:(cb + 1) * LANE], same_head)

    row = lambda w: pl.BlockSpec((tm, w), lambda i: (i, 0))
    full = lambda s: pl.BlockSpec(s, lambda i: (0, 0))
    return pl.pallas_call(
        body, name="outproj_loss", grid=(S // tm,),
        in_specs=[row(2 * D), full((2 * D, D)), row(D), row(D), full((1, D)), row(D),
                  pl.BlockSpec((tm, D), lambda i: (i, OFF_G // D))],
        out_specs=[row(D), row(D), row(D), row(D), row(D), full((2 * D, D)), full((1, D)), full((1, LANE))],
        out_shape=[SDS((S, D), F32)] * 4 + [SDS((S, D), BF16), SDS((2 * D, D), F32), SDS((1, D), F32),
                                            SDS((1, LANE), F32)],
        compiler_params=_cp(("arbitrary",)),
    )(mix, w_out, x, tgt, nw, attn_pre, proj)


def _inproj_bwd_dx(srcs, dxbcdt, w_all, x, dy, nw, hosted=None):
    tm = 512
    nk = DP // D
    n_host, n_host_out = (len(hosted.arrays), len(hosted.out_shape)) if hosted else (0, 0)

    def body(*refs):
        src_refs = refs[:nk]
        w_ref, x_ref, dy_ref, nw_ref = refs[nk:nk + 4]
        host_in, refs = refs[nk + 4:nk + 4 + n_host], refs[nk + 4 + n_host:]
        gx_ref, dnw_ref = refs[:2]
        host_out, host_sems = refs[2:2 + n_host_out], refs[2 + n_host_out:]
        i = pl.program_id(0)

        @pl.when(i == 0)
        def _():
            if hosted:
                hosted.start(host_in, host_out, host_sems)
            dnw_ref[...] = jnp.zeros_like(dnw_ref)

        du = None
        for k, ref in enumerate(src_refs):
            part = _dot_nt(ref[...], w_ref[:, k * D:(k + 1) * D])
            du = part if du is None else du + part
        xf, nw_v = x_ref[...], nw_ref[...]
        r = lax.rsqrt(jnp.mean(xf * xf, axis=-1, keepdims=True) + EPS)
        xh = xf * r
        dnw_ref[...] += jnp.sum(du * xh, axis=0, keepdims=True)
        gdu = du * nw_v
        gx_ref[...] = r * (gdu - xh * jnp.mean(gdu * xh, axis=-1, keepdims=True)) + dy_ref[...]

        if hosted:
            pl.when(i == S // tm - 1)(lambda: hosted.finish(host_in, host_out, host_sems))

    row = pl.BlockSpec((tm, D), lambda i: (i, 0))
    row1 = pl.BlockSpec((tm, D), lambda i: (i, 1))
    one = pl.BlockSpec((1, D), lambda i: (0, 0))
    whole_w = pl.BlockSpec((D, DP), lambda i: (0, 0), pipeline_mode=pl.Buffered(1))
    args = [*srcs, dxbcdt, dxbcdt, w_all, x, dy, nw]
    in_specs = [row] * len(srcs) + [row, row1, whole_w, row, row, one]
    out_specs, out_shape, scratch = [row, one], [SDS((S, D), F32), SDS((1, D), F32)], []
    if hosted:
        args += hosted.arrays
        in_specs += [ANY] * n_host
        out_specs += [ANY] * n_host_out
        out_shape += hosted.out_shape
        scratch += hosted.scratch
    outs = pl.pallas_call(
        body, name="inproj_bwd_dx", grid=(S // tm,),
        in_specs=in_specs, out_specs=out_specs, out_shape=out_shape, scratch_shapes=scratch,
        compiler_params=_cp(("arbitrary",)),
    )(*args)
    return (outs[:2], outs[2:]) if hosted else outs


def _dw(u, dsec, name):
    ts = 1024
    ncol = dsec.shape[1] // D

    def body(u_ref, d_ref, o_ref):
        @pl.when(pl.program_id(1) == 0)
        def _():
            o_ref[...] = jnp.zeros_like(o_ref)

        o_ref[...] += _dot_tn(u_ref[...], d_ref[...])

    return pl.pallas_call(
        body, name=name, grid=(ncol, S // ts),
        in_specs=[pl.BlockSpec((ts, D), lambda j, i: (i, 0)), pl.BlockSpec((ts, D), lambda j, i: (i, j))],
        out_specs=pl.BlockSpec((D, D), lambda j, i: (0, j)),
        out_shape=SDS((D, ncol * D), F32),
        compiler_params=_cp(("parallel", "arbitrary")),
    )(u, dsec)


def _place():
    x, y, c = lax.axis_index("x"), lax.axis_index("y"), lax.axis_index("c")
    return x, y, c, 2 * x + y


def _chip_of(x, y, k):
    px = 1 - x if k & 2 else x
    py = 1 - y if k & 1 else y
    return px, py, 2 * px + py


def _remote(src, dst, send_sem, recv_sem, dev):
    return pltpu.make_async_remote_copy(src_ref=src, dst_ref=dst, send_sem=send_sem, recv_sem=recv_sem,
                                        device_id=dev, device_id_type=MESH)


def _gather_weights(w_in_b):
    half = w_in_b.shape[0] // 2
    quarter = half // 2

    def body(src, dst, send, recv):
        x, y, c, j = _place()
        me, sib = (x, y, c), (x, y, 1 - c)
        nbr = {"x": _chip_of(x, y, 2), "y": _chip_of(x, y, 1)}
        diag = _chip_of(x, y, 3)[2]
        started, arrivals = [], []

        def rows(n_quarter=None, sibling=False):
            base = (1 - c if sibling else c) * half
            return pl.ds(base, half) if n_quarter is None else pl.ds(base + n_quarter * quarter, quarter)

        def sem(n):
            return send.at[n], recv.at[n]

        def go(cp):
            cp.start()
            started.append(cp)

        own = _remote(src, dst.at[j], *sem(8), sib)
        go(own)
        for n, axis in enumerate("xy"):
            px, py, _ = nbr[axis]
            go(_remote(src.at[rows()], dst.at[j, rows()], *sem(n), (px, py, c)))
        for n, axis in enumerate("xy"):
            ox, oy, _ = nbr["y" if axis == "x" else "x"]
            pj = nbr[axis][2]
            _remote(src.at[rows()], dst.at[pj, rows()], *sem(n), me).wait_recv()
            go(_remote(dst.at[pj, rows(n)], dst.at[pj, rows(n)], *sem(2 + n), (ox, oy, c)))
            go(_remote(dst.at[pj, rows()], dst.at[pj, rows()], *sem(4 + n), sib))
            arrivals.append(_remote(src.at[rows()], dst.at[pj, rows(None, True)], *sem(4 + n), me))
        for n in range(2):
            _remote(dst.at[diag, rows(n)], dst.at[diag, rows(n)], *sem(2 + n), me).wait_recv()
            go(_remote(dst.at[diag, rows(n)], dst.at[diag, rows(n)], *sem(6 + n), sib))
            arrivals.append(_remote(dst.at[diag, rows(n, True)], dst.at[diag, rows(n, True)], *sem(6 + n), me))
        for cp in arrivals + [own]:
            cp.wait_recv()
        for cp in started:
            cp.wait_send()

    return pl.pallas_call(
        body, name="gather_weights", in_specs=[ANY], out_specs=ANY,
        out_shape=SDS((4,) + w_in_b.shape, BF16),
        scratch_shapes=[pltpu.SemaphoreType.DMA((9,)), pltpu.SemaphoreType.DMA((9,))],
        compiler_params=pltpu.CompilerParams(has_side_effects=True),
    )(w_in_b)


class _LateGather:
    def __init__(self, w_out_b, conv_w):
        self.arrays = [w_out_b, conv_w]
        self.out_shape = [SDS((4,) + w_out_b.shape, BF16), SDS((4,) + conv_w.shape, F32)]
        self.scratch = [pltpu.SemaphoreType.DMA((11,)), pltpu.SemaphoreType.DMA((11,))]

    def _plan(self, ins, outs, sems):
        x, y, c, j = _place()
        send, recv = sems
        (wo, cw), (gwo, gcw) = ins, outs
        half = wo.shape[0] // 2
        mine, theirs = pl.ds(c * half, half), pl.ds((1 - c) * half, half)
        me, sib = (x, y, c), (x, y, 1 - c)
        first, arrive, forward, last = [], [], [], []
        for k in (1, 2, 3):
            px, py, pj = _chip_of(x, y, k)
            first += [_remote(wo.at[mine], gwo.at[j, mine], send.at[k - 1], recv.at[k - 1], (px, py, c)),
                      _remote(cw, gcw.at[j], send.at[k + 2], recv.at[k + 2], (px, py, c))]
            arrive.append(_remote(wo.at[mine], gwo.at[pj, mine], send.at[k - 1], recv.at[k - 1], me))
            forward.append(_remote(gwo.at[pj, mine], gwo.at[pj, mine], send.at[k + 5], recv.at[k + 5], sib))
            last += [_remote(cw, gcw.at[pj], send.at[k + 2], recv.at[k + 2], me),
                     _remote(wo.at[theirs], gwo.at[pj, theirs], send.at[k + 5], recv.at[k + 5], me)]
        first += [_remote(wo, gwo.at[j], send.at[9], recv.at[9], sib),
                  _remote(cw, gcw.at[j], send.at[10], recv.at[10], sib)]
        last += first[-2:]
        return first, arrive, forward, last

    def start(self, ins, outs, sems):
        for cp in self._plan(ins, outs, sems)[0]:
            cp.start()

    def finish(self, ins, outs, sems):
        first, arrive, forward, last = self._plan(ins, outs, sems)
        for got, fwd in zip(arrive, forward):
            got.wait_recv()
            fwd.start()
        for cp in last:
            cp.wait_recv()
        for cp in first + forward:
            cp.wait_send()


def _window(s, names):
    lo, hi = TILES * s, TILES * s + TILES + 1
    pieces = []
    for n, name in enumerate(names):
        a, count = SECTION_TILES[name]
        first, last = max(lo, a), min(hi, a + count)
        if first < last:
            pieces.append((n, first - a, last - first, first - lo))
    assert sum(p[2] for p in pieces) == TILES + 1
    return pieces


class _PairExchange:
    def __init__(self, names, sections, shards, more=()):
        self.names, self.shards = names, shards
        self.arrays = list(sections) + list(more)
        self.out_shape = [SDS((len(shards), sections[0].shape[0] // 2, WIN), F32)]
        self.out_shape += [SDS((a.shape[0], a.shape[1] // 2, a.shape[2]), F32) for a in more]
        n = sum(len(_window(s, names)) for s in shards) + len(more)
        self.scratch = [pltpu.SemaphoreType.DMA((n,)) for _ in range(2)]

    def _copies(self, ins, outs, sems):
        x, y, c, _ = _place()
        sib = (x, y, 1 - c)
        half = ins[0].shape[0] // 2
        rows = pl.ds((1 - c) * half, half)
        k = 0
        for i, s in enumerate(self.shards):
            for n, tile, tiles, at in _window(s, self.names):
                yield _remote(ins[n].at[rows, pl.ds(tile * LANE, tiles * LANE)],
                              outs[0].at[i, :, pl.ds(at * LANE, tiles * LANE)], sems[0].at[k], sems[1].at[k], sib)
                k += 1
        for src, dst in zip(ins[len(self.names):], outs[1:]):
            half = src.shape[1] // 2
            yield _remote(src.at[:, pl.ds((1 - c) * half, half)], dst, sems[0].at[k], sems[1].at[k], sib)
            k += 1

    def start(self, ins, outs, sems):
        for cp in self._copies(ins, outs, sems):
            cp.start()

    def finish(self, ins, outs, sems):
        for cp in self._copies(ins, outs, sems):
            cp.wait()


def _exchange_call(exchange, name):
    n, n_out = len(exchange.arrays), len(exchange.out_shape)

    def body(*refs):
        ins, outs, sems = refs[:n], refs[n:n + n_out], refs[n + n_out:]
        exchange.start(ins, outs, sems)
        exchange.finish(ins, outs, sems)

    return pl.pallas_call(
        body, name=name, in_specs=[ANY] * n, out_specs=[ANY] * n_out, out_shape=exchange.out_shape,
        scratch_shapes=exchange.scratch, compiler_params=pltpu.CompilerParams(has_side_effects=True),
    )(*exchange.arrays)


def _pair_sum_windows(cidx, names, sections, shards, r, name):
    n, half, _ = r.shape
    tr = min(half, 256)
    nt = half // tr

    def body(c_ref, *refs):
        del c_ref
        secs, r_ref, o_ref = refs[:-2], refs[-2], refs[-1]
        for i, s in enumerate(shards):
            for k, tile, tiles, at in _window(s, names):
                own = secs[k][:, tile * LANE:(tile + tiles) * LANE]
                there = slice(at * LANE, (at + tiles) * LANE)
                o_ref[i, :, there] = (own + r_ref[i, :, there]).astype(BF16)

    window = pl.BlockSpec((n, tr, WIN), lambda t, c: (0, t, 0))
    return pl.pallas_call(
        body, name=name,
        grid_spec=pltpu.PrefetchScalarGridSpec(
            num_scalar_prefetch=1, grid=(nt,),
            in_specs=[pl.BlockSpec((tr, a.shape[1]), lambda t, c: (c[0] * nt + t, 0)) for a in sections] + [window],
            out_specs=window),
        out_shape=SDS(r.shape, BF16),
        compiler_params=_cp(("parallel",)),
    )(cidx, *sections, r)


def _pair_sum(cidx, g, r, name):
    n, half, width = r.shape
    tr = min(half, 256)
    nt = half // tr

    def body(c_ref, g_ref, r_ref, o_ref):
        del c_ref
        o_ref[...] = (g_ref[...] + r_ref[...]).astype(BF16)

    return pl.pallas_call(
        body, name=name,
        grid_spec=pltpu.PrefetchScalarGridSpec(
            num_scalar_prefetch=1, grid=(n, nt),
            in_specs=[pl.BlockSpec((None, tr, width), lambda s, t, c: (s, c[0] * nt + t, 0)),
                      pl.BlockSpec((None, tr, width), lambda s, t, c: (s, t, 0))],
            out_specs=pl.BlockSpec((None, tr, width), lambda s, t, c: (s, t, 0))),
        out_shape=SDS(r.shape, BF16),
        compiler_params=_cp(("parallel", "parallel")),
    )(cidx, g, r)


class _ChipExchange:
    def __init__(self, arrays, rows):
        self.arrays, self.rows = list(arrays), list(rows)
        self.out_shape = [SDS((4,) + a.shape[1:], BF16) for a in self.arrays]
        self.scratch = [pltpu.SemaphoreType.DMA((3 * len(self.arrays),)) for _ in range(2)]

    def _copies(self, ins, outs, sems):
        x, y, c, j = _place()
        send, recv = sems
        for a, (src, dst, row) in enumerate(zip(ins, outs, self.rows)):
            for k in (1, 2, 3):
                px, py, pj = _chip_of(x, y, k)
                n = 3 * a + k - 1
                slot = pj if row is None else py
                yield (None if row is None else px == row, None if row is None else x == row,
                       _remote(src.at[slot], dst.at[j], send.at[n], recv.at[n], (px, py, c)),
                       _remote(src.at[0], dst.at[pj], send.at[n], recv.at[n], (x, y, c)))

    def start(self, ins, outs, sems):
        for sends, _, send, _ in self._copies(ins, outs, sems):
            if sends is None:
                send.start()
            else:
                pl.when(sends)(send.start)

    def finish(self, ins, outs, sems):
        for sends, owns, send, arrival in self._copies(ins, outs, sems):
            if sends is None:
                arrival.wait_recv()
                send.wait_send()
            else:
                pl.when(owns)(arrival.wait_recv)
                pl.when(sends)(send.wait_send)


def _all_gather_rows(src, dst, rows, send, recv, local_sem):
    x, y, c, j = _place()
    me = 2 * j + c
    local = pltpu.make_async_copy(src, dst.at[me, rows], local_sem)
    cps, arrivals = [], []
    for k in range(1, 8):
        px, py, pj = _chip_of(x, y, k >> 1)
        pc = 1 - c if k & 1 else c
        cps.append(_remote(src, dst.at[me, rows], send.at[k - 1], recv.at[k - 1], (px, py, pc)))
        arrivals.append(_remote(src, dst.at[2 * pj + pc, rows], send.at[k - 1], recv.at[k - 1], (x, y, c)))
    starts = [local.start] + [cp.start for cp in cps]
    waits = [cp.wait_recv for cp in arrivals] + [cp.wait_send for cp in cps] + [local.wait]
    return starts, waits


class _SmallExchange:
    def __init__(self, small):
        self.arrays = [small]
        self.out_shape = [SDS((8,) + small.shape, F32)]
        self.scratch = [pltpu.SemaphoreType.DMA((7,)), pltpu.SemaphoreType.DMA((7,)), pltpu.SemaphoreType.DMA]

    def start(self, ins, outs, sems):
        for go in _all_gather_rows(ins[0], outs[0], slice(None), *sems)[0]:
            go()

    def finish(self, ins, outs, sems):
        for wait in _all_gather_rows(ins[0], outs[0], slice(None), *sems)[1]:
            wait()


class _Both:
    def __init__(self, a, b):
        self.parts = (a, b)
        self.arrays, self.out_shape, self.scratch = a.arrays + b.arrays, a.out_shape + b.out_shape, a.scratch + b.scratch

    def _split(self, ins, outs, sems):
        a, b = self.parts
        return ((a, ins[:len(a.arrays)], outs[:len(a.out_shape)], sems[:len(a.scratch)]),
                (b, ins[len(a.arrays):], outs[len(a.out_shape):], sems[len(a.scratch):]))

    def start(self, ins, outs, sems):
        for part, *refs in self._split(ins, outs, sems):
            part.start(*refs)

    def finish(self, ins, outs, sems):
        for part, *refs in self._split(ins, outs, sems):
            part.finish(*refs)


def _slot_sum(r, name):
    n, rows, width = r.shape
    tr = min(rows, 256)

    def body(r_ref, o_ref):
        acc = r_ref[0].astype(F32)
        for s in range(1, n):
            acc = acc + r_ref[s].astype(F32)
        o_ref[...] = acc

    return pl.pallas_call(
        body, name=name, grid=(rows // tr,),
        in_specs=[pl.BlockSpec((n, tr, width), lambda t: (0, t, 0))],
        out_specs=pl.BlockSpec((tr, width), lambda t: (t, 0)),
        out_shape=SDS((rows, width), F32),
        compiler_params=_cp(("parallel",)),
    )(r)


def _chip_sum(where, recv, own, name):
    n, rows, width = recv.shape
    tr = min(rows, 256)
    nt = rows // tr

    def body(j_ref, r_ref, own_ref, o_ref):
        acc = None
        for s in range(n):
            term = jnp.where(j_ref[0] == s, own_ref[...], r_ref[s]).astype(F32)
            acc = term if acc is None else acc + term
        o_ref[...] = acc

    return pl.pallas_call(
        body, name=name,
        grid_spec=pltpu.PrefetchScalarGridSpec(
            num_scalar_prefetch=1, grid=(nt,),
            in_specs=[pl.BlockSpec((n, tr, width), lambda t, j: (0, t, 0)),
                      pl.BlockSpec((None, tr, width), lambda t, j: (j[0], t, 0))],
            out_specs=pl.BlockSpec((tr, width), lambda t, j: (j[1] * nt + t, 0))),
        out_shape=SDS((2 * rows, width), F32),
        compiler_params=_cp(("parallel",)),
    )(where, recv, own)


def _chip_sum_rows(place, recv0, own0, recv1, own1, name):
    n, rows, width = recv0.shape
    tr = min(rows, 256)
    nt = rows // tr

    def body(p_ref, r0_ref, o0_ref, r1_ref, o1_ref, o_ref):
        first_row = p_ref[2] == 0
        own = jnp.where(first_row, o0_ref[...], o1_ref[...])
        acc = None
        for s in range(n):
            term = jnp.where(p_ref[0] == s, own, jnp.where(first_row, r0_ref[s], r1_ref[s])).astype(F32)
            acc = term if acc is None else acc + term
        o_ref[...] = acc

    recv = pl.BlockSpec((n, tr, width), lambda t, p: (0, t, 0))
    own = pl.BlockSpec((None, tr, width), lambda t, p: (p[3], t, 0))
    return pl.pallas_call(
        body, name=name,
        grid_spec=pltpu.PrefetchScalarGridSpec(
            num_scalar_prefetch=1, grid=(nt,), in_specs=[recv, own, recv, own],
            out_specs=pl.BlockSpec((tr, width), lambda t, p: (p[1] * nt + t, 0))),
        out_shape=SDS((2 * rows, width), F32),
        compiler_params=_cp(("parallel",)),
    )(place, recv0, own0, recv1, own1)


def _half_exchange(gw, go, gathered, late, row):
    def body(gw_in, go_in, ga_in, late_ref, gw_ref, go_ref, ga_ref, send, recv, late_send, late_recv, late_local):
        del gw_in, go_in, ga_in
        x, y, c, _ = _place()
        starts, waits = _all_gather_rows(late_ref, ga_ref, pl.ds(row, late.shape[0]), late_send, late_recv,
                                         late_local)
        for go_ in starts:
            go_()
        mine = [pl.ds(c * (r.shape[0] // 2), r.shape[0] // 2) for r in (gw_ref, go_ref)]
        cps = [_remote(r.at[rows], r.at[rows], send.at[k], recv.at[k], (x, y, 1 - c))
               for k, (r, rows) in enumerate(zip((gw_ref, go_ref), mine))]
        for cp in cps:
            cp.start()
        for k, r in enumerate((gw_ref, go_ref)):
            theirs = pl.ds((1 - c) * (r.shape[0] // 2), r.shape[0] // 2)
            _remote(r.at[theirs], r.at[theirs], send.at[k], recv.at[k], (x, y, c)).wait_recv()
        for cp in cps:
            cp.wait_send()
        for wait in waits:
            wait()

    return pl.pallas_call(
        body, name="half_exchange", in_specs=[ANY] * 4, out_specs=[ANY] * 3,
        out_shape=[SDS(gw.shape, F32), SDS(go.shape, F32), SDS(gathered.shape, F32)],
        input_output_aliases={0: 0, 1: 1, 2: 2},
        scratch_shapes=[pltpu.SemaphoreType.DMA((2,)), pltpu.SemaphoreType.DMA((2,)),
                        pltpu.SemaphoreType.DMA((7,)), pltpu.SemaphoreType.DMA((7,)), pltpu.SemaphoreType.DMA],
        compiler_params=pltpu.CompilerParams(has_side_effects=True),
    )(gw, go, gathered, late)


def _adamw(w, g, m, v, name):
    rows, width = w.shape
    tr = min(rows, 256)

    def body(w_ref, g_ref, m_ref, v_ref, d_ref, nm_ref, nv_ref):
        gv = g_ref[...]
        nm = ADAM_B1 * m_ref[...] + (1.0 - ADAM_B1) * gv
        nv = ADAM_B2 * v_ref[...] + (1.0 - ADAM_B2) * (gv * gv)
        m_hat = nm / (1.0 - ADAM_B1 ** ADAM_STEP)
        v_hat = nv / (1.0 - ADAM_B2 ** ADAM_STEP)
        d_ref[...] = -ADAM_LR * (m_hat / (jnp.sqrt(v_hat) + ADAM_EPS) + ADAM_WD * w_ref[...])
        nm_ref[...] = nm
        nv_ref[...] = nv

    t = pl.BlockSpec((tr, width), lambda i: (i, 0))
    return pl.pallas_call(
        body, name=name, grid=(rows // tr,), in_specs=[t] * 4, out_specs=[t] * 3,
        out_shape=[SDS(w.shape, F32)] * 3, compiler_params=_cp(("parallel",)),
    )(w, g, m, v)


def _rowwise(a):
    return jnp.transpose(a, (2, 0, 1)).reshape(SHARD * D // LANE, LANE)


def _columns(ref):
    return jnp.concatenate([ref[pl.ds(c, LANE, stride=8), :].T for c in range(D // LANE)], axis=0)


def _shard_bf16(chip, w_rows):
    def body(j_ref, w_ref, o_ref, prev_ref):
        t = pl.program_id(0)
        cur = _columns(w_ref)

        @pl.when(t == 0)
        def _():
            prev_ref[...] = jnp.zeros_like(prev_ref)

        lane = _iota((D, LANE), 1)
        for s in range(4):
            @pl.when(j_ref[0] == s)
            def _():
                off = SHIFT * s
                moved = cur if s == 0 else jnp.where(lane < off, pltpu.roll(prev_ref[...], off, 1),
                                                     pltpu.roll(cur, off, 1))
                col = t * LANE + lane - off
                o_ref[...] = jnp.where((col >= 0) & (col < SHARD), moved, 0.0).astype(BF16)
        prev_ref[...] = cur

    return pl.pallas_call(
        body, name="shard_bf16",
        grid_spec=pltpu.PrefetchScalarGridSpec(
            num_scalar_prefetch=1, grid=(TILES + 1,),
            in_specs=[pl.BlockSpec((D, LANE), lambda t, j: (t, 0))],
            out_specs=pl.BlockSpec((D, LANE), lambda t, j: (0, t)),
            scratch_shapes=[pltpu.VMEM((D, LANE), F32)]),
        out_shape=SDS((D, WIN), BF16), compiler_params=_cp(("arbitrary",)),
    )(chip, w_rows)


def _whole_w_in(windows):
    tr = 256
    n = windows.shape[0]

    def body(g_ref, o_ref):
        lane = _iota((tr, LANE), 1)
        for s in range(n):
            first = TILES * s
            head = g_ref[s, :, :LANE]
            if s:
                tail = g_ref[s - 1, :, TILES * LANE:]
                head = jnp.where(lane < SHIFT * s, tail.astype(F32), head.astype(F32)).astype(BF16)
            o_ref[:, first * LANE:(first + 1) * LANE] = head
            o_ref[:, (first + 1) * LANE:(first + TILES) * LANE] = g_ref[s, :, LANE:TILES * LANE]
        o_ref[:, n * TILES * LANE:(n * TILES + 1) * LANE] = g_ref[n - 1, :, TILES * LANE:]
        o_ref[:, (n * TILES + 1) * LANE:] = jnp.zeros((tr, DP - (n * TILES + 1) * LANE), BF16)

    return pl.pallas_call(
        body, name="whole_w_in", grid=(D // tr,),
        in_specs=[pl.BlockSpec((n, tr, WIN), lambda t: (0, t, 0))], out_specs=pl.BlockSpec((tr, DP), lambda t: (t, 0)),
        out_shape=SDS((D, DP), BF16), compiler_params=_cp(("parallel",)),
    )(windows)


def _own_buffer(a, name):
    tr = 512
    block = pl.BlockSpec((tr, a.shape[1]), lambda t: (t, 0))

    def body(a_ref, o_ref):
        o_ref[...] = a_ref[...]

    return pl.pallas_call(
        body, name=name, grid=(a.shape[0] // tr,), in_specs=[block], out_specs=block,
        out_shape=SDS(a.shape, a.dtype), compiler_params=_cp(("parallel",)),
    )(a)


def _new_m(m_ref, gv):
    return ADAM_B1 * _columns(m_ref) + (1.0 - ADAM_B1) * gv


def _new_v(v_ref, gv):
    return ADAM_B2 * _columns(v_ref) + (1.0 - ADAM_B2) * (gv * gv)


def _adamw_in(chip, w_rows, g_win, m_rows, v_rows):
    def grad_body(j_ref, g_ref, next_ref, m_ref, grad_ref, nm_ref):
        for s in range(4):
            @pl.when(j_ref[0] == s)
            def _():
                if s == 0:
                    grad_ref[...] = g_ref[...]
                else:
                    back = LANE - SHIFT * s
                    grad_ref[...] = jnp.where(_iota((D, LANE), 1) < back, pltpu.roll(g_ref[...], back, 1),
                                              pltpu.roll(next_ref[...], back, 1))
        nm_ref[...] = _new_m(m_ref, grad_ref[...])

    def v_body(g_ref, v_ref, nv_ref):
        nv_ref[...] = _new_v(v_ref, g_ref[...])

    def delta_body(w_ref, g_ref, m_ref, v_ref, d_ref):
        gv = g_ref[...]
        m_hat = _new_m(m_ref, gv) / (1.0 - ADAM_B1 ** ADAM_STEP)
        v_hat = _new_v(v_ref, gv) / (1.0 - ADAM_B2 ** ADAM_STEP)
        d_ref[...] = -ADAM_LR * (m_hat / (jnp.sqrt(v_hat) + ADAM_EPS) + ADAM_WD * _columns(w_ref))

    shard = SDS((D, SHARD), F32)
    tile = pl.BlockSpec((D, LANE), lambda t, j: (0, t))
    next_tile = pl.BlockSpec((D, LANE), lambda t, j: (0, jnp.minimum(t + 1, TILES)))
    rows = pl.BlockSpec((D, LANE), lambda t, j: (t, 0))
    grad, nm = pl.pallas_call(
        grad_body, name="adamw_in_m",
        grid_spec=pltpu.PrefetchScalarGridSpec(
            num_scalar_prefetch=1, grid=(TILES + 1,), in_specs=[tile, next_tile, rows], out_specs=[tile] * 2),
        out_shape=[shard] * 2, compiler_params=_cp(("parallel",)),
    )(chip, g_win, g_win, m_rows)
    tile = pl.BlockSpec((D, LANE), lambda t: (0, t))
    rows = pl.BlockSpec((D, LANE), lambda t: (t, 0))
    nv = pl.pallas_call(
        v_body, name="adamw_in_v", grid=(TILES + 1,), in_specs=[tile, rows], out_specs=tile, out_shape=shard,
        compiler_params=_cp(("parallel",)),
    )(grad, v_rows)
    delta = pl.pallas_call(
        delta_body, name="adamw_in_delta", grid=(TILES + 1,), in_specs=[rows, tile, rows, rows], out_specs=tile,
        out_shape=shard, compiler_params=_cp(("parallel",)),
    )(w_rows, grad, m_rows, v_rows)
    return grad, delta, nm, nv


def _rows128(a, rows):
    flat = a.reshape(-1)
    return jnp.pad(flat, (0, rows * LANE - flat.shape[0])).reshape(rows, LANE)


CONV_ROWS = 48


def _pack_small(conv_w, norm_pre, conv_b, ssm_norm, norm_post, dtb, alog, dsk, extra=None):
    cw_rows = CONV_ROWS if conv_w.shape[-1] == 1536 else 16
    extra = jnp.zeros((1, LANE), F32) if extra is None else _rows128(extra, 1)
    vec = jnp.concatenate([_rows128(dtb, 1), _rows128(alog, 1), _rows128(dsk, 1), extra, jnp.zeros((4, LANE), F32)],
                          axis=0)
    return jnp.concatenate([_rows128(conv_w, cw_rows), _rows128(norm_pre, 8), _rows128(conv_b, 16),
                            _rows128(ssm_norm, 8), _rows128(norm_post, 8), vec], axis=0)


def _unpack_small(p, cw_cols):
    cw_rows = CONV_ROWS if cw_cols == 1536 else 16
    o = cw_rows
    conv_w = p[:cw_rows].reshape(-1)[:4 * cw_cols].reshape(1, 4, cw_cols)
    norm_pre = p[o:o + 8].reshape(1, D)
    conv_b = p[o + 8:o + 24].reshape(-1)[:1536].reshape(1, 1536)
    ssm_norm = p[o + 24:o + 32].reshape(1, D)
    norm_post = p[o + 32:o + 40].reshape(1, D)
    vec = p[o + 40:o + 48]
    return conv_w, norm_pre, conv_b, ssm_norm, norm_post, vec[0:1, :NH], vec[1:2, :NH], vec[2:3, :NH], vec[3, 0]


def _pad_lanes(a):
    return jnp.pad(a, ((0, 0), (0, LANE - a.shape[1])))


class _GradReduce:
    LO, HI = ("q", "k", "v", "g"), ("g", "z", "x")

    def __init__(self, xi, yi, ci):
        self.cidx = jnp.reshape(ci, (1,)).astype(jnp.int32)
        self.place = jnp.stack([2 * xi + yi, ci, xi, yi]).astype(jnp.int32)

    def pairs(self, dw_g, dw_z, dw_x, dw_out):
        self.hi = [dw_g, dw_z, dw_x]
        self.go = dw_out.reshape(4, D // 2, D)
        return _PairExchange(self.HI, self.hi, (2, 3), [self.go])

    def first(self, got):
        rw, ro = got
        self.pw_hi = _pair_sum_windows(self.cidx, self.HI, self.hi, (2, 3), rw, "pair_sum_hi")
        self.po = _pair_sum(self.cidx, self.go, ro, "pair_sum_out")
        return _ChipExchange([self.pw_hi, self.po], [1, None])

    def first_done(self, got):
        self.rw_hi, self.ro = got

    def second(self, dw_q, dw_k, dw_v, dw_g, small):
        lo = [dw_q, dw_k, dw_v, dw_g]
        (rw,) = _exchange_call(_PairExchange(self.LO, lo, (0, 1)), "pair_exchange_lo")
        self.pw_lo = _pair_sum_windows(self.cidx, self.LO, lo, (0, 1), rw, "pair_sum_lo")
        return _Both(_ChipExchange([self.pw_lo], [0]), _SmallExchange(small))

    def second_done(self, got):
        self.rw_lo, self.small = got

    def result(self, late, row):
        half_in = _chip_sum_rows(self.place, self.rw_lo, self.pw_lo, self.rw_hi, self.pw_hi, "chip_sum_in")
        half_out = _chip_sum(self.place[0:2], self.ro, self.po, "chip_sum_out")
        return _half_exchange(half_in, half_out, self.small, late, row)


def kernel(x, norm_pre_w, w_in, conv_w, conv_b, dt_bias, a_log, d_skip, ssm_norm_w, w_out, norm_post_w, loss_target, m_norm_pre_w, m_w_in, m_conv_w, m_conv_b, m_dt_bias, m_a_log, m_d_skip, m_ssm_norm_w, m_w_out, m_norm_post_w, v_norm_pre_w, v_w_in, v_conv_w, v_conv_b, v_dt_bias, v_a_log, v_d_skip, v_ssm_norm_w, v_w_out, v_norm_post_w):
    xi, yi, ci = lax.axis_index("x"), lax.axis_index("y"), lax.axis_index("c")
    chip = 2 * xi + yi
    x2, tgt = x[0], loss_target[0]

    chip_idx = jnp.reshape(chip, (1,)).astype(jnp.int32)
    w_rows = _rowwise(w_in)
    w_all = _whole_w_in(_gather_weights(_shard_bf16(chip_idx, w_rows)))
    reduce = _GradReduce(xi, yi, ci)
    grad_x, dnw_pre = _local_step(x2, tgt, w_all, _LateGather(w_out[0].astype(BF16), conv_w[0]), norm_pre_w, conv_b,
                                  dt_bias, a_log, d_skip, ssm_norm_w, norm_post_w, reduce)
    g_win, g_out, small = reduce.result(_rows128(dnw_pre, D // LANE), CONV_ROWS)
    g_small = _slot_sum(small, "small_sum")
    g_cw, g_npre, g_cb, g_nssm, g_npost, g_dtb, g_alog, g_dsk, loss = _unpack_small(g_small, 1536)
    g_cw = lax.dynamic_slice_in_dim(g_cw, chip * 384, 384, axis=2)

    g_in, d_in, nm_in, nv_in = _adamw_in(chip_idx, w_rows, g_win, _rowwise(m_w_in), _rowwise(v_w_in))
    grad_x = _own_buffer(grad_x, "grad_x_copy")
    d_out, nm_out, nv_out = _adamw(w_out[0], g_out, m_w_out[0], v_w_out[0], "adamw_out")
    packed = [_pack_small(*t) for t in (
        (conv_w, norm_pre_w, conv_b, ssm_norm_w, norm_post_w, dt_bias, a_log, d_skip),
        (g_cw, g_npre, g_cb, g_nssm, g_npost, g_dtb, g_alog, g_dsk),
        (m_conv_w, m_norm_pre_w, m_conv_b, m_ssm_norm_w, m_norm_post_w, m_dt_bias, m_a_log, m_d_skip),
        (v_conv_w, v_norm_pre_w, v_conv_b, v_ssm_norm_w, v_norm_post_w, v_dt_bias, v_a_log, v_d_skip))]
    small_out = [_unpack_small(p, 384)[:8] for p in _adamw(*packed, "adamw_small")]

    def ordered(cw_, npre, cb_, nssm, npost, dtb_, alog_, dsk_, big_in, big_out):
        return [npre, big_in[None], cw_, cb_, dtb_, alog_, dsk_, nssm, big_out[None], npost]

    grads = ordered(g_cw, g_npre, g_cb, g_nssm, g_npost, g_dtb, g_alog, g_dsk, g_in, g_out)
    deltas = ordered(*small_out[0], d_in, d_out)
    new_m = ordered(*small_out[1], nm_in, nm_out)
    new_v = ordered(*small_out[2], nv_in, nv_out)
    return (loss, grad_x[None], *grads, *deltas, *new_m, *new_v)


def _local_step(x2, tgt, w_all, late, norm_pre_w, conv_b, dt_bias, a_log, d_skip, ssm_norm_w,
                norm_post_w, reduce=None):
    dtb, alog = _pad_lanes(dt_bias), _pad_lanes(a_log)
    d_b = jnp.repeat(d_skip, 64, axis=1)

    if isinstance(late, _LateGather):
        (proj, u), (gout, gcw) = _inproj_fwd(x2, norm_pre_w, w_all, late)
        w_out_all = gout.reshape(2 * D, D)
        cw_all = jnp.concatenate([gcw[0], gcw[1], gcw[2], gcw[3]], axis=1)
    else:
        proj, u = _inproj_fwd(x2, norm_pre_w, w_all)
        w_out_all, cw_all = late
    mix, attn_pre, lse = _attn_fwd(proj, 1, _attn_fwd(proj, 4, _attn_fwd(proj, 16)), final=True)
    mix, y_save, states, conv_out = _ssm_fwd(proj, mix, cw_all, conv_b, dtb, alog, d_b, ssm_norm_w)

    dy, dn_ssm, do, delta, dg, dw_out, dnw_post, loss_part = _outproj_loss(mix, w_out_all, x2, tgt, norm_post_w,
                                                                          attn_pre, proj)
    dz, dxbcdt, dcw, dcb, dvec, dnw_ssm = _ssm_bwd(proj, dn_ssm, y_save, states, conv_out, cw_all, dtb, alog, d_b,
                                                   ssm_norm_w)
    dw_g, dw_z, dw_x = _dw(u, dg, "dw_in_g"), _dw(u, dz, "dw_in_z"), _dw(u, dxbcdt, "dw_in_xbcdt")
    acc = _attn_bwd(proj, do, lse, delta, 16, None, F32, reduce.pairs(dw_g, dw_z, dw_x, dw_out) if reduce else None)
    if reduce:
        acc, got = acc
    acc = _attn_bwd(proj, do, lse, delta, 4, acc, F32, reduce.first(got) if reduce else None)
    if reduce:
        acc, got = acc
        reduce.first_done(got)
    dq, dk, dv = _attn_bwd(proj, do, lse, delta, 1, acc, BF16)
    dw_q, dw_k, dw_v = _dw(u, dq, "dw_in_q"), _dw(u, dk, "dw_in_k"), _dw(u, dv, "dw_in_v")

    def small(dnw_pre):
        return _pack_small(dcw, dnw_pre, dcb, dnw_ssm, dnw_post, dvec[0:1, :NH], dvec[1:2, :NH], dvec[2:3, :NH],
                           loss_part[:, :1])

    res = _inproj_bwd_dx([dq, dk, dv, dg, dz], dxbcdt, w_all, x2, dy, norm_pre_w,
                         reduce.second(dw_q, dw_k, dw_v, dw_g, small(jnp.zeros((1, D), F32))) if reduce else None)
    if reduce:
        res, got = res
        reduce.second_done(got)
        return res
    grad_x, dnw_pre = res
    dw_all = jnp.concatenate([dw_q, dw_k, dw_v, dw_g, dw_z, dw_x], axis=1)
    return grad_x, small(dnw_pre), dw_all, dw_out
```

```python
import functools

import jax
import jax.numpy as jnp
from jax import lax
from jax.experimental import pallas as pl
from jax.experimental.pallas import tpu as pltpu

F32 = jnp.float32
BF16 = jnp.bfloat16
MESH = pl.DeviceIdType.MESH
SDS = jax.ShapeDtypeStruct
ANY = pl.BlockSpec(memory_space=pl.ANY)

S = 4096
D = 1024
DP = 7168
SHARD = 1668
OFF_G, OFF_Z = 3072, 4096
NH = 16
CH = 128
NC = S // CH
EPS = 1e-6
NEG = -1e30
LANE = 128
VMEM_LIMIT = 48 * 1024 * 1024

TILES = SHARD // LANE
WIN = (TILES + 1) * LANE
SHIFT = SHARD - TILES * LANE
SECTION_TILES = {"q": (0, 8), "k": (8, 8), "v": (16, 8), "g": (24, 8), "z": (32, 8), "x": (40, 13)}
X_COLS = SECTION_TILES["x"][1] * LANE

ADAM_LR, ADAM_B1, ADAM_B2, ADAM_EPS, ADAM_WD, ADAM_STEP = 0.001, 0.9, 0.999, 1e-08, 0.01, 10


def _cp(sem, **kw):
    return pltpu.CompilerParams(dimension_semantics=sem, vmem_limit_bytes=VMEM_LIMIT, **kw)


def _dot(a, b):
    return jnp.dot(a, b, preferred_element_type=F32)


def _dot_nt(a, b):
    return lax.dot_general(a, b, (((1,), (1,)), ((), ())), preferred_element_type=F32)


def _dot_tn(a, b):
    return lax.dot_general(a, b, (((0,), (0,)), ((), ())), preferred_element_type=F32)


def _pieces(x, n):
    out = []
    for _ in range(n):
        p = x.astype(BF16)
        out.append(p)
        x = x - p.astype(F32)
    return out


def _pick(x, sel, n=2):
    parts = [_dot(p, sel) for p in _pieces(x, n)]
    return functools.reduce(jnp.add, parts)


def _pick_left(sel, x, n=3):
    parts = [_dot(sel, p) for p in _pieces(x, n)]
    return functools.reduce(jnp.add, parts)


def _sigmoid(v):
    return 0.5 * jnp.tanh(0.5 * v) + 0.5


def _iota(shape, dim):
    return lax.broadcasted_iota(jnp.int32, shape, dim)


def _inproj_fwd(x, nw, w_all, hosted=None):
    tm, tn = 1024, 1024
    n_host = len(hosted.arrays) if hosted else 0

    def body(x_ref, nw_ref, w_ref, *refs):
        host_in, (proj_ref, u_ref), refs = refs[:n_host], refs[n_host:n_host + 2], refs[n_host + 2:]
        host_out, host_sems = refs[:n_host], refs[n_host:]
        i, j = pl.program_id(0), pl.program_id(1)
        if hosted:
            pl.when((i == 0) & (j == 0))(lambda: hosted.start(host_in, host_out, host_sems))

        @pl.when(j == 0)
        def _():
            xf = x_ref[...]
            r = lax.rsqrt(jnp.mean(xf * xf, axis=-1, keepdims=True) + EPS)
            u_ref[...] = (xf * r * nw_ref[...]).astype(BF16)

        proj_ref[...] = _dot(u_ref[...], w_ref[...])
        if hosted:
            pl.when((i == S // tm - 1) & (j == DP // tn - 1))(lambda: hosted.finish(host_in, host_out, host_sems))

    outs = pl.pallas_call(
        body, name="inproj_fwd", grid=(S // tm, DP // tn),
        in_specs=[pl.BlockSpec((tm, D), lambda i, j: (i, 0)), pl.BlockSpec((1, D), lambda i, j: (0, 0)),
                  pl.BlockSpec((D, tn), lambda i, j: (0, j))] + [ANY] * n_host,
        out_specs=[pl.BlockSpec((tm, tn), lambda i, j: (i, j)), pl.BlockSpec((tm, D), lambda i, j: (i, 0))]
        + [ANY] * n_host,
        out_shape=[SDS((S, DP), F32), SDS((S, D), BF16)] + (hosted.out_shape if hosted else []),
        scratch_shapes=hosted.scratch if hosted else [],
        compiler_params=_cp(("arbitrary", "arbitrary") if hosted else ("parallel", "arbitrary")),
    )(x, nw, w_all, *(hosted.arrays if hosted else []))
    return (outs[:2], outs[2:]) if hosted else outs


ATTN_QB = {1: 16, 4: 4, 16: 1}


def _unit_rows(r, u, d):
    return pl.ds(r + d * CH * u, CH, stride=d) if d > 1 else pl.ds(CH * u, CH)


def _for_units(d, qb, fn):
    for r in range(d):
        for u in range(qb):
            fn(r, u)


def _attn_mask(has_prev):
    qi, kj = _iota((2 * CH, 2 * CH), 0) & (CH - 1), _iota((2 * CH, 2 * CH), 1)
    cur_ok = (kj >= CH) & (kj - CH <= qi)
    prev_ok = (kj < CH) & (kj >= qi)
    return cur_ok | (prev_ok & has_prev)


def _stack_heads(v, lane_a):
    return jnp.concatenate([jnp.where(lane_a, v, 0.0), jnp.where(lane_a, 0.0, v)], axis=0).astype(BF16)


def _attn_specs(d, qb):
    rows, prows = CH * d * qb, CH * d
    nb = S // rows
    steps = (NH // 2) * nb

    def at(t):
        t = jnp.minimum(t, steps - 1)
        return t % nb, t // nb

    def cur(off):
        return pl.BlockSpec((rows, LANE), lambda t: (at(t)[0], off + at(t)[1]))

    def prev(off):
        return pl.BlockSpec((prows, LANE), lambda t: (jnp.maximum(at(t)[0] * qb - 1, 0), off + at(t)[1]))

    lag = pl.BlockSpec((rows, LANE), lambda t: at(jnp.maximum(t - 1, 0)))
    return nb, steps, cur, prev, lag


def _gather16(src_ref, dense_ref, tmp_ref):
    for a in range(4):
        tmp_ref[...] = src_ref[pl.ds(a, 4 * CH, stride=4), :]
        for b in range(4):
            dense_ref[a + 4 * b] = tmp_ref[pl.ds(b, CH, stride=4), :]


def _scatter16(dense_ref, dst_ref, tmp_ref):
    for a in range(4):
        for b in range(4):
            tmp_ref[pl.ds(b, CH, stride=4), :] = dense_ref[a + 4 * b]
        dst_ref[pl.ds(a, 4 * CH, stride=4), :] = tmp_ref[...]


def _unit_index(r, u, d):
    return (r,) if d == 16 else (_unit_rows(r, u, d), slice(None))


def _unit_kv(p_ref, c_ref, r, u, d):
    prev = p_ref[_unit_index(r, 0, d)] if u == 0 else c_ref[_unit_index(r, u - 1, d)]
    return jnp.concatenate([prev, c_ref[_unit_index(r, u, d)]], axis=0).astype(BF16)


def _dense_scratch(d, n):
    return [pltpu.VMEM((16, CH, LANE), F32)] * n + [pltpu.VMEM((4 * CH, LANE), F32)] if d == 16 else []


def _attn_fwd(proj, d, prior=None, final=False):
    qb = ATTN_QB[d]
    nb, steps, cur, prev, _ = _attn_specs(d, qb)
    n_prior = 2 if prior is not None else 0
    n_in, n_out = 5 + n_prior + final, 2 + final
    assert not (d == 16 and (n_prior or final))

    def body(*refs):
        ins, outs, scratch = refs[:n_in], refs[n_in:n_in + n_out], refs[n_in + n_out:]
        if d == 16:
            tmp_ref = scratch[-1]
            for src, dense in zip(ins, scratch):
                _gather16(src, dense, tmp_ref)
            block_outs, ins, outs = outs, scratch[:n_in], scratch[n_in:n_in + n_out]
        q_ref, kp_ref, kc_ref, vp_ref, vc_ref = ins[:5]
        prior_refs = ins[5:5 + n_prior]
        if final:
            g_ref, (mix_ref, o_ref, l_ref) = ins[-1], outs
        else:
            o_ref, l_ref = outs
        i = pl.program_id(0) % nb
        lane_a = _iota((CH, LANE), 1) < 64
        mask_first, mask_rest = _attn_mask(i > 0), _attn_mask(True)

        def unit(r, u):
            at = _unit_index(r, u, d)
            q2 = _stack_heads(q_ref[at] * 0.125, lane_a)
            k2, v2 = _unit_kv(kp_ref, kc_ref, r, u, d), _unit_kv(vp_ref, vc_ref, r, u, d)
            s = jnp.where(mask_first if u == 0 else mask_rest, _dot_nt(q2, k2), NEG)
            m = jnp.max(s, axis=1, keepdims=True)
            p = jnp.exp(s - m)
            l = jnp.sum(p, axis=1, keepdims=True)
            o2 = _dot(p.astype(BF16), v2) / l
            lse2 = m + jnp.log(l)
            o = jnp.where(lane_a, o2[:CH], o2[CH:])
            lse = jnp.where(lane_a, lse2[:CH], lse2[CH:])
            if n_prior:
                o_a, l_a = prior_refs[0][at], prior_refs[1][at]
                top = jnp.maximum(l_a, lse)
                e_a, e_b = jnp.exp(l_a - top), jnp.exp(lse - top)
                tot = e_a + e_b
                o = (e_a * o_a + e_b * o) / tot
                lse = top + jnp.log(tot)
            o_ref[at] = o
            l_ref[at] = lse
            if final:
                g = g_ref[at]
                mix_ref[at] = (o * (g * _sigmoid(g))).astype(BF16)

        _for_units(d, qb, unit)
        if d == 16:
            for dense, dst in zip(outs, block_outs):
                _scatter16(dense, dst, tmp_ref)

    in_specs = [cur(0), prev(8), cur(8), prev(16), cur(16)] + [cur(0)] * n_prior
    args = [proj] * 5 + (list(prior) if n_prior else [])
    out_specs, out_shape = [cur(0), cur(0)], [SDS((S, D), F32), SDS((S, D), F32)]
    if final:
        assert d == 1
        in_specs.append(cur(OFF_G // LANE))
        args.append(proj)
        out_specs, out_shape = [cur(0)] + out_specs, [SDS((S, 2 * D), BF16)] + out_shape
    return pl.pallas_call(
        body, name=f"attn_fwd_d{d}", grid=(steps,),
        in_specs=in_specs, out_specs=out_specs, out_shape=out_shape,
        scratch_shapes=_dense_scratch(d, n_in + n_out),
        compiler_params=_cp(("parallel",)),
    )(*args)


def _attn_bwd(proj, do, lse, delta, d, acc, out_dtype, hosted=None):
    qb = ATTN_QB[d]
    nb, steps, cur, prev, lag = _attn_specs(d, qb)
    has_acc = acc is not None
    n_in = 11 if has_acc else 8
    n_host, n_host_out = (len(hosted.arrays), len(hosted.out_shape)) if hosted else (0, 0)
    assert not (d == 16 and (has_acc or out_dtype != F32))
    rows = CH * d * qb
    carry = (2, 16, CH, LANE) if d == 16 else (2, rows, LANE)

    def body(*refs):
        ins, host_in, refs = refs[:n_in], refs[n_in:n_in + n_host], refs[n_in + n_host:]
        (dq_ref, dk_ref, dv_ref), host_out, scratch = refs[:3], refs[3:3 + n_host_out], refs[3 + n_host_out:]
        if hosted:
            scratch, host_sems = scratch[:-len(hosted.scratch)], scratch[-len(hosted.scratch):]
        ck_ref, cv_ref = scratch[:2]
        dq_f32 = dq_ref if out_dtype == F32 else scratch[2]
        t = pl.program_id(0)
        i = t % nb
        if hosted:
            pl.when(t == 0)(lambda: hosted.start(host_in, host_out, host_sems))
        if d == 16:
            dense, dq_f32, tmp_ref = scratch[2:2 + n_in], scratch[2 + n_in], scratch[-1]

            @pl.when(t < steps)
            def _():
                for src, dst in zip(ins, dense):
                    _gather16(src, dst, tmp_ref)

            ins = dense
        q_ref, kp_ref, kc_ref, vp_ref, vc_ref, do_ref, lse_ref, dl_ref = ins[:8]
        if has_acc:
            aq_ref, ak_ref, av_ref = ins[8:11]
        slot = t & 1
        now_k, now_v, old_k, old_v = ck_ref.at[slot], cv_ref.at[slot], ck_ref.at[1 - slot], cv_ref.at[1 - slot]
        lane_a = _iota((CH, LANE), 1) < 64
        mask_first, mask_rest = _attn_mask(i > 0), _attn_mask(True)

        @pl.when(t == 0)
        def _():
            ck_ref[1] = jnp.zeros(carry[1:], F32)
            cv_ref[1] = jnp.zeros(carry[1:], F32)

        def unit(r, u):
            at = _unit_index(r, u, d)
            q2 = _stack_heads(q_ref[at] * 0.125, lane_a)
            do2 = _stack_heads(do_ref[at], lane_a)
            k2, v2 = _unit_kv(kp_ref, kc_ref, r, u, d), _unit_kv(vp_ref, vc_ref, r, u, d)
            lsev, dlv = lse_ref[at], dl_ref[at]
            lse2 = jnp.concatenate([lsev[:, 0:1], lsev[:, 64:65]], axis=0)
            dl2 = jnp.concatenate([dlv[:, 0:1], dlv[:, 64:65]], axis=0)
            p = jnp.exp(jnp.where(mask_first if u == 0 else mask_rest, _dot_nt(q2, k2), NEG) - lse2)
            ds = (p * (_dot_nt(do2, v2) - dl2)).astype(BF16)
            dq2 = _dot(ds, k2)
            dk2 = _dot_tn(ds, q2)
            dv2 = _dot_tn(p.astype(BF16), do2)
            dq = jnp.where(lane_a, dq2[:CH], dq2[CH:]) * 0.125
            if has_acc:
                dq = dq + aq_ref[at]
            dq_f32[at] = dq
            if u == 0:
                before = _unit_index(r, qb - 1, d)
                old_k[before] += dk2[:CH]
                old_v[before] += dv2[:CH]
            else:
                before = _unit_index(r, u - 1, d)
                now_k[before] += dk2[:CH]
                now_v[before] += dv2[:CH]
            now_k[at] = dk2[CH:]
            now_v[at] = dv2[CH:]

        @pl.when(t < steps)
        def _():
            _for_units(d, qb, unit)
            if d == 16:
                _scatter16(dq_f32, dq_ref, tmp_ref)
            elif out_dtype != F32:
                dq_ref[...] = dq_f32[...].astype(out_dtype)

        if d == 16:
            _scatter16(old_k, dk_ref, tmp_ref)
            _scatter16(old_v, dv_ref, tmp_ref)
        else:
            dk, dv = old_k[...], old_v[...]
            if has_acc:
                dk, dv = dk + ak_ref[...], dv + av_ref[...]
            dk_ref[...] = dk.astype(out_dtype)
            dv_ref[...] = dv.astype(out_dtype)
        if hosted:
            pl.when(t == steps)(lambda: hosted.finish(host_in, host_out, host_sems))

    in_specs = [cur(0), prev(8), cur(8), prev(16), cur(16), cur(0), cur(0), cur(0)]
    args = [proj, proj, proj, proj, proj, do, lse, delta]
    if has_acc:
        in_specs += [cur(0), lag, lag]
        args += list(acc)
    scratch = [pltpu.VMEM(carry, F32), pltpu.VMEM(carry, F32)]
    if d == 16:
        scratch += _dense_scratch(d, n_in + 1)
    elif out_dtype != F32:
        scratch.append(pltpu.VMEM((rows, LANE), F32))
    out_specs, out_shape = [cur(0), lag, lag], [SDS((S, D), out_dtype)] * 3
    if hosted:
        args += hosted.arrays
        in_specs += [ANY] * n_host
        out_specs += [ANY] * n_host_out
        out_shape += hosted.out_shape
        scratch += hosted.scratch
    outs = pl.pallas_call(
        body, name=f"attn_bwd_d{d}", grid=(steps + 1,),
        in_specs=in_specs, out_specs=out_specs, out_shape=out_shape,
        scratch_shapes=scratch, compiler_params=_cp(("arbitrary",)),
    )(*args)
    return (outs[:3], outs[3:]) if hosted else outs


def _conv_taps(cur, prev8, first):
    row8 = _iota(prev8.shape, 0)
    prev8 = jnp.where(first, 0.0, prev8)
    taps = []
    for s in (3, 2, 1):
        rolled = pltpu.roll(cur, s, 0)
        head = jnp.where(row8 < s, pltpu.roll(prev8, s, 0), rolled[:8])
        taps.append(jnp.concatenate([head, rolled[8:]], axis=0))
    return taps + [cur]


def _conv(taps, w, b):
    acc = b + w[0:1, :] * taps[0]
    for k in (1, 2, 3):
        acc = acc + w[k:k + 1, :] * taps[k]
    return acc


def _expand():
    return (_iota((LANE, D), 1) // 64 == _iota((LANE, D), 0)).astype(BF16)


def _reduce():
    return (_iota((D, LANE), 0) // 64 == _iota((D, LANE), 1)).astype(BF16)


def _ssd_common(xs_c, bc_c, dt_raw, dtb, alog):
    head_lane = _iota((CH, LANE), 1) < NH
    xs = xs_c * _sigmoid(xs_c)
    bc = bc_c * _sigmoid(bc_c)
    pre = dt_raw + dtb
    dt = jnp.where(head_lane, jnp.maximum(pre, 0.0) + jnp.log(1.0 + jnp.exp(-jnp.abs(pre))), 0.0)
    a_row = jnp.where(head_lane[0:1], -jnp.exp(alog), 0.0)
    tri = (_iota((CH, CH), 1) <= _iota((CH, CH), 0)).astype(BF16)
    cs = _pick_left(tri, dt * a_row)
    cs_last = cs[CH - 1:CH, :]
    wide = _pick(jnp.concatenate([dt, jnp.exp(cs), jnp.exp(cs_last - cs)], axis=0), _expand())
    dt_b, e_b, f_b = wide[:CH], wide[CH:2 * CH], wide[2 * CH:]
    return dict(xs=xs, bc=bc, pre=pre, dt=dt, a_row=a_row, cs=cs, cs_t=cs.T, dt_b=dt_b, e_b=e_b, f_b=f_b,
                t_b=e_b[CH - 1:CH, :])


def _groups(bc):
    bcb = bc.astype(BF16)
    return [bcb[:, 0:128], bcb[:, 128:256]], [bcb[:, 256:384], bcb[:, 384:512]]


def _decay(q, h, tril):
    seg = q["cs"][:, h:h + 1] - q["cs_t"][h:h + 1, :]
    return jnp.exp(jnp.where(tril, seg, NEG))


def _ssm_fwd(proj, mix, cw, cb, dtb, alog, d_b, nw):
    def body(xs_ref, xsp_ref, bc_ref, bcp_ref, dt_ref, z_ref, cw_ref, cb_ref, dtb_ref, alog_ref, db_ref, nw_ref,
             mix_in_ref, mix_ref, y_ref, st_ref, conv_ref, h_ref):
        del mix_in_ref
        i = pl.program_id(0)

        @pl.when(i == 0)
        def _():
            h_ref[...] = jnp.zeros_like(h_ref)

        cw, cb = cw_ref[...], cb_ref[...]
        xs_c = _conv(_conv_taps(xs_ref[...], xsp_ref[...], i == 0), cw[:, :D], cb[:, :D])
        bc_c = _conv(_conv_taps(bc_ref[...], bcp_ref[...], i == 0), cw[:, D:], cb[:, D:])
        conv_ref[:, :D] = xs_c
        conv_ref[:, D:] = bc_c
        q = _ssd_common(xs_c, bc_c, dt_ref[...], dtb_ref[...], alog_ref[...])
        bg, cg = _groups(q["bc"])
        xs = q["xs"]
        xdt = xs * q["dt_b"]
        xdt_b = xdt.astype(BF16)
        h_in = h_ref[...]
        st_ref[...] = h_in
        hb = h_in.astype(BF16)
        tril = _iota((CH, CH), 1) <= _iota((CH, CH), 0)
        lane_a = _iota((CH, LANE), 1) < 64
        cbm = [_dot_nt(cg[g], bg[g]) for g in range(2)]
        pairs = []
        for hp in range(NH // 2):
            xp = xdt_b[:, hp * LANE:(hp + 1) * LANE]
            ya = _dot((cbm[hp // 4] * _decay(q, 2 * hp, tril)).astype(BF16), xp)
            yb = _dot((cbm[hp // 4] * _decay(q, 2 * hp + 1, tril)).astype(BF16), xp)
            pairs.append(jnp.where(lane_a, ya, yb))
        y_diag = jnp.concatenate(pairs, axis=1)
        y_off = jnp.concatenate([_dot(cg[g], hb[:, g * 512:(g + 1) * 512]) for g in range(2)], axis=1) * q["e_b"]
        y = y_diag + y_off + db_ref[...] * xs
        y_ref[...] = y
        xf = (xdt * q["f_b"]).astype(BF16)
        h_ref[...] = q["t_b"] * h_in + jnp.concatenate(
            [_dot_tn(bg[g], xf[:, g * 512:(g + 1) * 512]) for g in range(2)], axis=1)
        z = z_ref[...]
        yz = y * (z * _sigmoid(z))
        outs = []
        for g in range(2):
            v = yz[:, g * 512:(g + 1) * 512]
            outs.append(v * lax.rsqrt(jnp.mean(v * v, axis=-1, keepdims=True) + EPS))
        mix_ref[...] = (jnp.concatenate(outs, axis=1) * nw_ref[...]).astype(BF16)

    def col(width, blk, prev=False):
        if prev:
            return pl.BlockSpec((8, width), lambda i: (jnp.maximum(i * (CH // 8) - 1, 0), blk))
        return pl.BlockSpec((CH, width), lambda i: (i, blk))

    def full(a):
        return pl.BlockSpec(a.shape, lambda i: (0,) * a.ndim)

    return pl.pallas_call(
        body, name="ssm_fwd", grid=(NC,),
        in_specs=[col(D, 5), col(D, 5, True), col(512, 12), col(512, 12, True), col(LANE, 52), col(D, 4),
                  full(cw), full(cb), full(dtb), full(alog), full(d_b), full(nw), ANY],
        out_specs=[col(D, 1), col(D, 0), pl.BlockSpec((None, CH, D), lambda i: (i, 0, 0)), col(D + 512, 0)],
        out_shape=[SDS((S, 2 * D), BF16), SDS((S, D), F32), SDS((NC, CH, D), F32), SDS((S, D + 512), F32)],
        scratch_shapes=[pltpu.VMEM((CH, D), F32)],
        input_output_aliases={12: 0},
        compiler_params=_cp(("arbitrary",)),
    )(proj, proj, proj, proj, proj, proj, cw, cb, dtb, alog, d_b, nw, mix)


def _ssm_bwd(proj, dn, y_save, states, conv_out, cw, dtb, alog, d_b, nw):
    def body(xs_ref, bc_ref, dt_ref, z_ref, dn_ref, y_ref, st_ref, conv_ref,
             cw_ref, dtb_ref, alog_ref, db_ref, nw_ref,
             dz_ref, dx_ref, dcw_ref, dcb_ref, dsm_ref, dnw_ref, dh_ref, nxs_ref, nbc_ref):
        i = pl.program_id(0)
        ci = NC - 1 - i

        @pl.when(i == 0)
        def _():
            for ref in (dcw_ref, dcb_ref, dsm_ref, dnw_ref, dh_ref, nxs_ref, nbc_ref):
                ref[...] = jnp.zeros_like(ref)

        cw = cw_ref[...]
        xs_c, bc_c = conv_ref[:, :D], conv_ref[:, D:]
        q = _ssd_common(xs_c, bc_c, dt_ref[...], dtb_ref[...], alog_ref[...])
        bg, cg = _groups(q["bc"])
        xs, dt_b, e_b, f_b, t_b = q["xs"], q["dt_b"], q["e_b"], q["f_b"], q["t_b"]
        xdt = xs * dt_b
        xdt_b = xdt.astype(BF16)
        h_in = st_ref[...]
        hb = h_in.astype(BF16)
        dh_new = dh_ref[...]
        dhb = dh_new.astype(BF16)
        red = _reduce()

        z, y, dn, nw_v = z_ref[...], y_ref[...], dn_ref[...], nw_ref[...]
        sig = _sigmoid(z)
        sz = z * sig
        yz = y * sz
        gdn = dn * nw_v
        dyz, dnw = [], []
        for g in range(2):
            v, gv = yz[:, g * 512:(g + 1) * 512], gdn[:, g * 512:(g + 1) * 512]
            r = lax.rsqrt(jnp.mean(v * v, axis=-1, keepdims=True) + EPS)
            dnw.append(dn[:, g * 512:(g + 1) * 512] * v * r)
            dyz.append(r * (gv - v * (r * r) * jnp.mean(gv * v, axis=-1, keepdims=True)))
        dyz = jnp.concatenate(dyz, axis=1)
        dnw_ref[...] += jnp.sum(jnp.concatenate(dnw, axis=1), axis=0, keepdims=True)
        dy = dyz * sz
        dz_ref[...] = (dyz * y * (sig * (1.0 + z * (1.0 - sig)))).astype(BF16)
        dy_b = dy.astype(BF16)

        tril = _iota((CH, CH), 1) <= _iota((CH, CH), 0)
        lane_a = _iota((CH, LANE), 1) < 64
        cbm = [_dot_nt(cg[g], bg[g]) for g in range(2)]
        dcbm = [jnp.zeros((CH, CH), F32), jnp.zeros((CH, CH), F32)]
        seg_rows = jnp.zeros((CH, LANE), F32)
        seg_cols = jnp.zeros((LANE, CH), F32)
        row_id, col_id = _iota((CH, LANE), 0), _iota((CH, LANE), 1)
        dx_pairs = []
        for hp in range(NH // 2):
            g = hp // 4
            xp = xdt_b[:, hp * LANE:(hp + 1) * LANE]
            dyp_f = dy[:, hp * LANE:(hp + 1) * LANE]
            dyp = dy_b[:, hp * LANE:(hp + 1) * LANE]
            halves = []
            for k in range(2):
                h = 2 * hp + k
                lane = lane_a if k == 0 else jnp.logical_not(lane_a)
                dec = _decay(q, h, tril)
                gm = cbm[g] * dec
                dgm = _dot_nt(jnp.where(lane, dyp_f, 0.0).astype(BF16), xp)
                dcbm[g] = dcbm[g] + dgm * dec
                prod = dgm * gm
                seg_rows = jnp.where(col_id == h, jnp.sum(prod, axis=1, keepdims=True), seg_rows)
                seg_cols = jnp.where(row_id == h, jnp.sum(prod, axis=0, keepdims=True), seg_cols)
                halves.append(_dot_tn(gm.astype(BF16), dyp))
            dx_pairs.append(jnp.where(lane_a, halves[0], halves[1]))
        dxdt_diag = jnp.concatenate(dx_pairs, axis=1)

        qv = jnp.concatenate([_dot(bg[g], dhb[:, g * 512:(g + 1) * 512]) for g in range(2)], axis=1)
        y_off = jnp.concatenate([_dot(cg[g], hb[:, g * 512:(g + 1) * 512]) for g in range(2)], axis=1) * e_b
        xfq = xdt * f_b * qv
        dxdt = dxdt_diag + f_b * qv
        tdt = jnp.sum(dh_new * h_in, axis=0, keepdims=True) * t_b
        per_head = _pick(jnp.concatenate([xfq, dy * y_off, dxdt * xs, dy * xs, jnp.broadcast_to(tdt, (8, D))],
                                         axis=0), red)
        fdf, dyoff_h, dxdtxs_h, dyxs_h = [per_head[k * CH:(k + 1) * CH] for k in range(4)]
        dcs = seg_rows - seg_cols.T + dyoff_h - fdf
        last = per_head[4 * CH:4 * CH + 1] + jnp.sum(fdf, axis=0, keepdims=True)
        dcs = dcs + jnp.where(_iota((CH, LANE), 0) == CH - 1, last, 0.0)
        tri_t = (_iota((CH, CH), 1) >= _iota((CH, CH), 0)).astype(BF16)
        da = _pick_left(tri_t, dcs)
        ddt = da * q["a_row"] + dxdtxs_h
        dxs = dxdt * dt_b + db_ref[...] * dy
        ddt_raw = ddt * _sigmoid(q["pre"])
        dsm_ref[0:1, :] += jnp.sum(ddt_raw, axis=0, keepdims=True)
        dsm_ref[1:2, :] += jnp.sum(da * q["dt"], axis=0, keepdims=True) * q["a_row"]
        dsm_ref[2:3, :] += jnp.sum(dyxs_h, axis=0, keepdims=True)
        edy = (e_b * dy).astype(BF16)
        xf = (xdt * f_b).astype(BF16)
        dbs, dcs_g, dhs = [], [], []
        for g in range(2):
            sl = slice(g * 512, (g + 1) * 512)
            dcb_b = dcbm[g].astype(BF16)
            dcs_g.append(_dot(dcb_b, bg[g]) + _dot_nt(edy[:, sl], hb[:, sl]))
            dbs.append(_dot_tn(dcb_b, cg[g]) + _dot_nt(xf[:, sl], dhb[:, sl]))
            dhs.append(_dot_tn(cg[g], edy[:, sl]))
        dh_ref[...] = t_b * dh_new + jnp.concatenate(dhs, axis=1)
        dbc = jnp.concatenate(dbs + dcs_g, axis=1)

        def conv_bwd(dact, pre, x_raw, w, nxt_ref, lo):
            s = _sigmoid(pre)
            dconv = dact * (s * (1.0 + pre * (1.0 - s)))
            nxt8 = nxt_ref[...]
            row8 = _iota(nxt8.shape, 0)
            hi = lo + dconv.shape[1]
            dcb_ref[:, lo:hi] += jnp.sum(dconv, axis=0, keepdims=True)
            later = [dconv]
            for s_ in (1, 2, 3):
                rolled = pltpu.roll(dconv, CH - s_, 0)
                tail = jnp.where(row8 >= 8 - s_, pltpu.roll(nxt8, 8 - s_, 0), rolled[CH - 8:])
                later.append(jnp.concatenate([rolled[:CH - 8], tail], axis=0))
            dx = None
            for s_, up in enumerate(later):
                k = 3 - s_
                dcw_ref[k:k + 1, lo:hi] += jnp.sum(up * x_raw, axis=0, keepdims=True)
                dx = w[k:k + 1, :] * up if dx is None else dx + w[k:k + 1, :] * up
            nxt_ref[...] = dconv[:8]
            return dx

        dx_ref[:, 0:D] = conv_bwd(dxs, xs_c, xs_ref[...], cw[:, :D], nxs_ref, 0).astype(BF16)
        dx_ref[:, D:D + 512] = conv_bwd(dbc, bc_c, bc_ref[...], cw[:, D:], nbc_ref, D).astype(BF16)
        dx_ref[:, D + 512:D + 640] = ddt_raw.astype(BF16)
        dx_ref[:, D + 640:] = jnp.zeros((CH, D - 640), BF16)

    def col(width, blk):
        return pl.BlockSpec((CH, width), lambda i: (NC - 1 - i, blk))

    def full(a):
        return pl.BlockSpec(a.shape, lambda i: (0,) * len(a.shape))

    acc_shapes = [SDS((4, 1536), F32), SDS((1, 1536), F32), SDS((8, LANE), F32), SDS((1, D), F32)]
    return pl.pallas_call(
        body, name="ssm_bwd", grid=(NC,),
        in_specs=[col(D, 5), col(512, 12), col(LANE, 52), col(D, 4),
                  col(D, 0), col(D, 0), pl.BlockSpec((None, CH, D), lambda i: (NC - 1 - i, 0, 0)), col(D + 512, 0),
                  full(cw), full(dtb), full(alog), full(d_b), full(nw)],
        out_specs=[col(D, 0), col(2 * D, 0)] + [full(a) for a in acc_shapes],
        out_shape=[SDS((S, D), BF16), SDS((S, 2 * D), BF16)] + acc_shapes,
        scratch_shapes=[pltpu.VMEM((CH, D), F32), pltpu.VMEM((8, D), F32), pltpu.VMEM((8, 512), F32)],
        compiler_params=_cp(("arbitrary",)),
    )(proj, proj, proj, proj, dn, y_save, states, conv_out, cw, dtb, alog, d_b, nw)


def _outproj_loss(mix, w_out, x, tgt, nw, attn_pre, proj):
    tm = 256

    def body(mix_ref, w_ref, x_ref, t_ref, nw_ref, pre_ref, g_ref,
             dy_ref, dn_ref, do_ref, delta_ref, dg_ref, dw_ref, dnw_ref, loss_ref):
        @pl.when(pl.program_id(0) == 0)
        def _():
            dw_ref[...] = jnp.zeros_like(dw_ref)
            dnw_ref[...] = jnp.zeros_like(dnw_ref)
            loss_ref[...] = jnp.zeros_like(loss_ref)

        mixv, w = mix_ref[...], w_ref[...]
        out = _dot(mixv, w)
        r = lax.rsqrt(jnp.mean(out * out, axis=-1, keepdims=True) + EPS)
        nh = out * r
        nw_v = nw_ref[...]
        err = x_ref[...] + nh * nw_v - t_ref[...]
        loss_ref[...] += 0.5 * jnp.sum(jnp.mean(err * err, axis=-1, keepdims=True), axis=0, keepdims=True)
        dy = err * (1.0 / D)
        dy_ref[...] = dy
        dnw_ref[...] += jnp.sum(dy * nh, axis=0, keepdims=True)
        gdn = dy * nw_v
        dout = (r * (gdn - nh * jnp.mean(gdn * nh, axis=-1, keepdims=True))).astype(BF16)
        dmix = _dot_nt(dout, w)
        dw_ref[...] += _dot_tn(mixv, dout)
        dn_ref[...] = dmix[:, D:]
        dm, g, pre_v = dmix[:, :D], g_ref[...], pre_ref[...]
        sig = _sigmoid(g)
        do = dm * (g * sig)
        do_ref[...] = do
        dg_ref[...] = (dm * pre_v * (sig * (1.0 + g * (1.0 - sig)))).astype(BF16)
        prod = do * pre_v
        same_head = (_iota((LANE, LANE), 0) // 64 == _iota((LANE, LANE), 1) // 64).astype(BF16)
        for cb in range(D // LANE):
            delta_ref[:, cb * LANE:(cb + 1) * LANE] = _pick(prod[:, cb * LANE:(cb + 1) * LANE], same_head)

    row = lambda w: pl.BlockSpec((tm, w), lambda i: (i, 0))
    full = lambda s: pl.BlockSpec(s, lambda i: (0, 0))
    return pl.pallas_call(
        body, name="outproj_loss", grid=(S // tm,),
        in_specs=[row(2 * D), full((2 * D, D)), row(D), row(D), full((1, D)), row(D),
                  pl.BlockSpec((tm, D), lambda i: (i, OFF_G // D))],
        out_specs=[row(D), row(D), row(D), row(D), row(D), full((2 * D, D)), full((1, D)), full((1, LANE))],
        out_shape=[SDS((S, D), F32)] * 4 + [SDS((S, D), BF16), SDS((2 * D, D), F32), SDS((1, D), F32),
                                            SDS((1, LANE), F32)],
        compiler_params=_cp(("arbitrary",)),
    )(mix, w_out, x, tgt, nw, attn_pre, proj)


def _inproj_bwd_dx(srcs, dxbcdt, w_all, x, dy, nw, hosted=None):
    tm = 512
    nk = DP // D
    n_host, n_host_out = (len(hosted.arrays), len(hosted.out_shape)) if hosted else (0, 0)

    def body(*refs):
        src_refs = refs[:nk]
        w_ref, x_ref, dy_ref, nw_ref = refs[nk:nk + 4]
        host_in, refs = refs[nk + 4:nk + 4 + n_host], refs[nk + 4 + n_host:]
        gx_ref, dnw_ref = refs[:2]
        host_out, host_sems = refs[2:2 + n_host_out], refs[2 + n_host_out:]
        i = pl.program_id(0)

        @pl.when(i == 0)
        def _():
            if hosted:
                hosted.start(host_in, host_out, host_sems)
            dnw_ref[...] = jnp.zeros_like(dnw_ref)

        du = None
        for k, ref in enumerate(src_refs):
            width = min(D, 5 * D + X_COLS - k * D)
            part = _dot_nt(ref[:, :width], w_ref[:, k * D:k * D + width])
            du = part if du is None else du + part
        xf, nw_v = x_ref[...], nw_ref[...]
        r = lax.rsqrt(jnp.mean(xf * xf, axis=-1, keepdims=True) + EPS)
        xh = xf * r
        dnw_ref[...] += jnp.sum(du * xh, axis=0, keepdims=True)
        gdu = du * nw_v
        gx_ref[...] = r * (gdu - xh * jnp.mean(gdu * xh, axis=-1, keepdims=True)) + dy_ref[...]

        if hosted:
            pl.when(i == S // tm - 1)(lambda: hosted.finish(host_in, host_out, host_sems))

    row = pl.BlockSpec((tm, D), lambda i: (i, 0))
    row1 = pl.BlockSpec((tm, D), lambda i: (i, 1))
    one = pl.BlockSpec((1, D), lambda i: (0, 0))
    whole_w = pl.BlockSpec((D, DP), lambda i: (0, 0), pipeline_mode=pl.Buffered(1))
    args = [*srcs, dxbcdt, dxbcdt, w_all, x, dy, nw]
    in_specs = [row] * len(srcs) + [row, row1, whole_w, row, row, one]
    out_specs, out_shape, scratch = [row, one], [SDS((S, D), F32), SDS((1, D), F32)], []
    if hosted:
        args += hosted.arrays
        in_specs += [ANY] * n_host
        out_specs += [ANY] * n_host_out
        out_shape += hosted.out_shape
        scratch += hosted.scratch
    outs = pl.pallas_call(
        body, name="inproj_bwd_dx", grid=(S // tm,),
        in_specs=in_specs, out_specs=out_specs, out_shape=out_shape, scratch_shapes=scratch,
        compiler_params=_cp(("arbitrary",)),
    )(*args)
    return (outs[:2], outs[2:]) if hosted else outs


def _dw(u, dsec, name, width=D):
    ts = 1024

    def body(u_ref, d_ref, o_ref):
        @pl.when(pl.program_id(0) == 0)
        def _():
            o_ref[...] = jnp.zeros_like(o_ref)

        o_ref[...] += _dot_tn(u_ref[...], d_ref[...])

    return pl.pallas_call(
        body, name=name, grid=(S // ts,),
        in_specs=[pl.BlockSpec((ts, D), lambda i: (i, 0)), pl.BlockSpec((ts, width), lambda i: (i, 0))],
        out_specs=pl.BlockSpec((D, width), lambda i: (0, 0)),
        out_shape=SDS((D, width), F32),
        compiler_params=_cp(("arbitrary",)),
    )(u, dsec)


def _place():
    x, y, c = lax.axis_index("x"), lax.axis_index("y"), lax.axis_index("c")
    return x, y, c, 2 * x + y


def _chip_of(x, y, k):
    px = 1 - x if k & 2 else x
    py = 1 - y if k & 1 else y
    return px, py, 2 * px + py


def _remote(src, dst, send_sem, recv_sem, dev):
    return pltpu.make_async_remote_copy(src_ref=src, dst_ref=dst, send_sem=send_sem, recv_sem=recv_sem,
                                        device_id=dev, device_id_type=MESH)


def _gather_weights(w_in_b):
    half = w_in_b.shape[0] // 2
    quarter = half // 2

    def body(src, dst, send, recv):
        x, y, c, j = _place()
        me, sib = (x, y, c), (x, y, 1 - c)
        nbr = {"x": _chip_of(x, y, 2), "y": _chip_of(x, y, 1)}
        diag = _chip_of(x, y, 3)[2]
        started, arrivals = [], []

        def rows(n_quarter=None, sibling=False):
            base = (1 - c if sibling else c) * half
            return pl.ds(base, half) if n_quarter is None else pl.ds(base + n_quarter * quarter, quarter)

        def sem(n):
            return send.at[n], recv.at[n]

        def go(cp):
            cp.start()
            started.append(cp)

        own = _remote(src, dst.at[j], *sem(8), sib)
        go(own)
        for n, axis in enumerate("xy"):
            px, py, _ = nbr[axis]
            go(_remote(src.at[rows()], dst.at[j, rows()], *sem(n), (px, py, c)))
        for n, axis in enumerate("xy"):
            ox, oy, _ = nbr["y" if axis == "x" else "x"]
            pj = nbr[axis][2]
            _remote(src.at[rows()], dst.at[pj, rows()], *sem(n), me).wait_recv()
            go(_remote(dst.at[pj, rows(n)], dst.at[pj, rows(n)], *sem(2 + n), (ox, oy, c)))
            go(_remote(dst.at[pj, rows()], dst.at[pj, rows()], *sem(4 + n), sib))
            arrivals.append(_remote(src.at[rows()], dst.at[pj, rows(None, True)], *sem(4 + n), me))
        for n in range(2):
            _remote(dst.at[diag, rows(n)], dst.at[diag, rows(n)], *sem(2 + n), me).wait_recv()
            go(_remote(dst.at[diag, rows(n)], dst.at[diag, rows(n)], *sem(6 + n), sib))
            arrivals.append(_remote(dst.at[diag, rows(n, True)], dst.at[diag, rows(n, True)], *sem(6 + n), me))
        for cp in arrivals + [own]:
            cp.wait_recv()
        for cp in started:
            cp.wait_send()

    return pl.pallas_call(
        body, name="gather_weights", in_specs=[ANY], out_specs=ANY,
        out_shape=SDS((4,) + w_in_b.shape, BF16),
        scratch_shapes=[pltpu.SemaphoreType.DMA((9,)), pltpu.SemaphoreType.DMA((9,))],
        compiler_params=pltpu.CompilerParams(has_side_effects=True),
    )(w_in_b)


class _LateGather:
    def __init__(self, w_out_b, conv_w):
        self.arrays = [w_out_b, conv_w]
        self.out_shape = [SDS((4,) + w_out_b.shape, BF16), SDS((4,) + conv_w.shape, F32)]
        self.scratch = [pltpu.SemaphoreType.DMA((11,)), pltpu.SemaphoreType.DMA((11,))]

    def _plan(self, ins, outs, sems):
        x, y, c, j = _place()
        send, recv = sems
        (wo, cw), (gwo, gcw) = ins, outs
        half = wo.shape[0] // 2
        mine, theirs = pl.ds(c * half, half), pl.ds((1 - c) * half, half)
        me, sib = (x, y, c), (x, y, 1 - c)
        first, arrive, forward, last = [], [], [], []
        for k in (1, 2, 3):
            px, py, pj = _chip_of(x, y, k)
            first += [_remote(wo.at[mine], gwo.at[j, mine], send.at[k - 1], recv.at[k - 1], (px, py, c)),
                      _remote(cw, gcw.at[j], send.at[k + 2], recv.at[k + 2], (px, py, c))]
            arrive.append(_remote(wo.at[mine], gwo.at[pj, mine], send.at[k - 1], recv.at[k - 1], me))
            forward.append(_remote(gwo.at[pj, mine], gwo.at[pj, mine], send.at[k + 5], recv.at[k + 5], sib))
            last += [_remote(cw, gcw.at[pj], send.at[k + 2], recv.at[k + 2], me),
                     _remote(wo.at[theirs], gwo.at[pj, theirs], send.at[k + 5], recv.at[k + 5], me)]
        first += [_remote(wo, gwo.at[j], send.at[9], recv.at[9], sib),
                  _remote(cw, gcw.at[j], send.at[10], recv.at[10], sib)]
        last += first[-2:]
        return first, arrive, forward, last

    def start(self, ins, outs, sems):
        for cp in self._plan(ins, outs, sems)[0]:
            cp.start()

    def finish(self, ins, outs, sems):
        first, arrive, forward, last = self._plan(ins, outs, sems)
        for got, fwd in zip(arrive, forward):
            got.wait_recv()
            fwd.start()
        for cp in last:
            cp.wait_recv()
        for cp in first + forward:
            cp.wait_send()


def _window(s, names):
    lo, hi = TILES * s, TILES * s + TILES + 1
    pieces = []
    for n, name in enumerate(names):
        a, count = SECTION_TILES[name]
        first, last = max(lo, a), min(hi, a + count)
        if first < last:
            pieces.append((n, first - a, last - first, first - lo))
    assert sum(p[2] for p in pieces) == TILES + 1
    return pieces


class _PairExchange:
    def __init__(self, names, sections, shards, more=()):
        self.names, self.shards = names, shards
        self.arrays = list(sections) + list(more)
        self.out_shape = [SDS((len(shards), sections[0].shape[0] // 2, WIN), F32)]
        self.out_shape += [SDS((a.shape[0], a.shape[1] // 2, a.shape[2]), F32) for a in more]
        n = sum(len(_window(s, names)) for s in shards) + len(more)
        self.scratch = [pltpu.SemaphoreType.DMA((n,)) for _ in range(2)]

    def _copies(self, ins, outs, sems):
        x, y, c, _ = _place()
        sib = (x, y, 1 - c)
        half = ins[0].shape[0] // 2
        rows = pl.ds((1 - c) * half, half)
        k = 0
        for i, s in enumerate(self.shards):
            for n, tile, tiles, at in _window(s, self.names):
                yield _remote(ins[n].at[rows, pl.ds(tile * LANE, tiles * LANE)],
                              outs[0].at[i, :, pl.ds(at * LANE, tiles * LANE)], sems[0].at[k], sems[1].at[k], sib)
                k += 1
        for src, dst in zip(ins[len(self.names):], outs[1:]):
            half = src.shape[1] // 2
            yield _remote(src.at[:, pl.ds((1 - c) * half, half)], dst, sems[0].at[k], sems[1].at[k], sib)
            k += 1

    def start(self, ins, outs, sems):
        for cp in self._copies(ins, outs, sems):
            cp.start()

    def finish(self, ins, outs, sems):
        for cp in self._copies(ins, outs, sems):
            cp.wait()


def _exchange_call(exchange, name):
    n, n_out = len(exchange.arrays), len(exchange.out_shape)

    def body(*refs):
        ins, outs, sems = refs[:n], refs[n:n + n_out], refs[n + n_out:]
        exchange.start(ins, outs, sems)
        exchange.finish(ins, outs, sems)

    return pl.pallas_call(
        body, name=name, in_specs=[ANY] * n, out_specs=[ANY] * n_out, out_shape=exchange.out_shape,
        scratch_shapes=exchange.scratch, compiler_params=pltpu.CompilerParams(has_side_effects=True),
    )(*exchange.arrays)


def _pair_sum_windows(cidx, names, sections, shards, r, name):
    n, half, _ = r.shape
    tr = min(half, 256)
    nt = half // tr

    def body(c_ref, *refs):
        del c_ref
        secs, r_ref, o_ref = refs[:-2], refs[-2], refs[-1]
        for i, s in enumerate(shards):
            for k, tile, tiles, at in _window(s, names):
                own = secs[k][:, tile * LANE:(tile + tiles) * LANE]
                there = slice(at * LANE, (at + tiles) * LANE)
                o_ref[i, :, there] = (own + r_ref[i, :, there]).astype(BF16)

    window = pl.BlockSpec((n, tr, WIN), lambda t, c: (0, t, 0))
    return pl.pallas_call(
        body, name=name,
        grid_spec=pltpu.PrefetchScalarGridSpec(
            num_scalar_prefetch=1, grid=(nt,),
            in_specs=[pl.BlockSpec((tr, a.shape[1]), lambda t, c: (c[0] * nt + t, 0)) for a in sections] + [window],
            out_specs=window),
        out_shape=SDS(r.shape, BF16),
        compiler_params=_cp(("parallel",)),
    )(cidx, *sections, r)


def _pair_sum(cidx, g, r, name):
    n, half, width = r.shape
    tr = min(half, 256)
    nt = half // tr

    def body(c_ref, g_ref, r_ref, o_ref):
        del c_ref
        o_ref[...] = (g_ref[...] + r_ref[...]).astype(BF16)

    return pl.pallas_call(
        body, name=name,
        grid_spec=pltpu.PrefetchScalarGridSpec(
            num_scalar_prefetch=1, grid=(n, nt),
            in_specs=[pl.BlockSpec((None, tr, width), lambda s, t, c: (s, c[0] * nt + t, 0)),
                      pl.BlockSpec((None, tr, width), lambda s, t, c: (s, t, 0))],
            out_specs=pl.BlockSpec((None, tr, width), lambda s, t, c: (s, t, 0))),
        out_shape=SDS(r.shape, BF16),
        compiler_params=_cp(("parallel", "parallel")),
    )(cidx, g, r)


class _ChipExchange:
    def __init__(self, arrays, rows):
        self.arrays, self.rows = list(arrays), list(rows)
        self.out_shape = [SDS((4,) + a.shape[1:], BF16) for a in self.arrays]
        self.scratch = [pltpu.SemaphoreType.DMA((3 * len(self.arrays),)) for _ in range(2)]

    def _copies(self, ins, outs, sems):
        x, y, c, j = _place()
        send, recv = sems
        for a, (src, dst, row) in enumerate(zip(ins, outs, self.rows)):
            for k in (1, 2, 3):
                px, py, pj = _chip_of(x, y, k)
                n = 3 * a + k - 1
                slot = pj if row is None else py
                yield (None if row is None else px == row, None if row is None else x == row,
                       _remote(src.at[slot], dst.at[j], send.at[n], recv.at[n], (px, py, c)),
                       _remote(src.at[0], dst.at[pj], send.at[n], recv.at[n], (x, y, c)))

    def start(self, ins, outs, sems):
        for sends, _, send, _ in self._copies(ins, outs, sems):
            if sends is None:
                send.start()
            else:
                pl.when(sends)(send.start)

    def finish(self, ins, outs, sems):
        for sends, owns, send, arrival in self._copies(ins, outs, sems):
            if sends is None:
                arrival.wait_recv()
                send.wait_send()
            else:
                pl.when(owns)(arrival.wait_recv)
                pl.when(sends)(send.wait_send)


def _all_gather_rows(src, dst, rows, send, recv, local_sem):
    x, y, c, j = _place()
    me = 2 * j + c
    local = pltpu.make_async_copy(src, dst.at[me, rows], local_sem)
    cps, arrivals = [], []
    for k in range(1, 8):
        px, py, pj = _chip_of(x, y, k >> 1)
        pc = 1 - c if k & 1 else c
        cps.append(_remote(src, dst.at[me, rows], send.at[k - 1], recv.at[k - 1], (px, py, pc)))
        arrivals.append(_remote(src, dst.at[2 * pj + pc, rows], send.at[k - 1], recv.at[k - 1], (x, y, c)))
    starts = [local.start] + [cp.start for cp in cps]
    waits = [cp.wait_recv for cp in arrivals] + [cp.wait_send for cp in cps] + [local.wait]
    return starts, waits


class _SmallExchange:
    def __init__(self, small):
        self.arrays = [small]
        self.out_shape = [SDS((8,) + small.shape, F32)]
        self.scratch = [pltpu.SemaphoreType.DMA((7,)), pltpu.SemaphoreType.DMA((7,)), pltpu.SemaphoreType.DMA]

    def start(self, ins, outs, sems):
        for go in _all_gather_rows(ins[0], outs[0], slice(None), *sems)[0]:
            go()

    def finish(self, ins, outs, sems):
        for wait in _all_gather_rows(ins[0], outs[0], slice(None), *sems)[1]:
            wait()


class _Both:
    def __init__(self, a, b):
        self.parts = (a, b)
        self.arrays, self.out_shape, self.scratch = a.arrays + b.arrays, a.out_shape + b.out_shape, a.scratch + b.scratch

    def _split(self, ins, outs, sems):
        a, b = self.parts
        return ((a, ins[:len(a.arrays)], outs[:len(a.out_shape)], sems[:len(a.scratch)]),
                (b, ins[len(a.arrays):], outs[len(a.out_shape):], sems[len(a.scratch):]))

    def start(self, ins, outs, sems):
        for part, *refs in self._split(ins, outs, sems):
            part.start(*refs)

    def finish(self, ins, outs, sems):
        for part, *refs in self._split(ins, outs, sems):
            part.finish(*refs)


def _slot_sum(r, name):
    n, rows, width = r.shape
    tr = min(rows, 256)

    def body(r_ref, o_ref):
        acc = r_ref[0].astype(F32)
        for s in range(1, n):
            acc = acc + r_ref[s].astype(F32)
        o_ref[...] = acc

    return pl.pallas_call(
        body, name=name, grid=(rows // tr,),
        in_specs=[pl.BlockSpec((n, tr, width), lambda t: (0, t, 0))],
        out_specs=pl.BlockSpec((tr, width), lambda t: (t, 0)),
        out_shape=SDS((rows, width), F32),
        compiler_params=_cp(("parallel",)),
    )(r)


def _chip_sum(where, recv, own, name):
    n, rows, width = recv.shape
    tr = min(rows, 256)
    nt = rows // tr

    def body(j_ref, r_ref, own_ref, o_ref):
        acc = None
        for s in range(n):
            term = jnp.where(j_ref[0] == s, own_ref[...], r_ref[s]).astype(F32)
            acc = term if acc is None else acc + term
        o_ref[...] = acc

    return pl.pallas_call(
        body, name=name,
        grid_spec=pltpu.PrefetchScalarGridSpec(
            num_scalar_prefetch=1, grid=(nt,),
            in_specs=[pl.BlockSpec((n, tr, width), lambda t, j: (0, t, 0)),
                      pl.BlockSpec((None, tr, width), lambda t, j: (j[0], t, 0))],
            out_specs=pl.BlockSpec((tr, width), lambda t, j: (j[1] * nt + t, 0))),
        out_shape=SDS((2 * rows, width), F32),
        compiler_params=_cp(("parallel",)),
    )(where, recv, own)


def _chip_sum_rows(place, recv0, own0, recv1, own1, name):
    n, rows, width = recv0.shape
    tr = min(rows, 256)
    nt = rows // tr

    def body(p_ref, r0_ref, o0_ref, r1_ref, o1_ref, o_ref):
        first_row = p_ref[2] == 0
        own = jnp.where(first_row, o0_ref[...], o1_ref[...])
        acc = None
        for s in range(n):
            term = jnp.where(p_ref[0] == s, own, jnp.where(first_row, r0_ref[s], r1_ref[s])).astype(F32)
            acc = term if acc is None else acc + term
        o_ref[...] = acc

    recv = pl.BlockSpec((n, tr, width), lambda t, p: (0, t, 0))
    own = pl.BlockSpec((None, tr, width), lambda t, p: (p[3], t, 0))
    return pl.pallas_call(
        body, name=name,
        grid_spec=pltpu.PrefetchScalarGridSpec(
            num_scalar_prefetch=1, grid=(nt,), in_specs=[recv, own, recv, own],
            out_specs=pl.BlockSpec((tr, width), lambda t, p: (p[1] * nt + t, 0))),
        out_shape=SDS((2 * rows, width), F32),
        compiler_params=_cp(("parallel",)),
    )(place, recv0, own0, recv1, own1)


def _half_exchange(gw, go, gathered, late, row):
    def body(gw_in, go_in, ga_in, late_ref, gw_ref, go_ref, ga_ref, send, recv, late_send, late_recv, late_local):
        del gw_in, go_in, ga_in
        x, y, c, _ = _place()
        starts, waits = _all_gather_rows(late_ref, ga_ref, pl.ds(row, late.shape[0]), late_send, late_recv,
                                         late_local)
        for go_ in starts:
            go_()
        mine = [pl.ds(c * (r.shape[0] // 2), r.shape[0] // 2) for r in (gw_ref, go_ref)]
        cps = [_remote(r.at[rows], r.at[rows], send.at[k], recv.at[k], (x, y, 1 - c))
               for k, (r, rows) in enumerate(zip((gw_ref, go_ref), mine))]
        for cp in cps:
            cp.start()
        for k, r in enumerate((gw_ref, go_ref)):
            theirs = pl.ds((1 - c) * (r.shape[0] // 2), r.shape[0] // 2)
            _remote(r.at[theirs], r.at[theirs], send.at[k], recv.at[k], (x, y, c)).wait_recv()
        for cp in cps:
            cp.wait_send()
        for wait in waits:
            wait()

    return pl.pallas_call(
        body, name="half_exchange", in_specs=[ANY] * 4, out_specs=[ANY] * 3,
        out_shape=[SDS(gw.shape, F32), SDS(go.shape, F32), SDS(gathered.shape, F32)],
        input_output_aliases={0: 0, 1: 1, 2: 2},
        scratch_shapes=[pltpu.SemaphoreType.DMA((2,)), pltpu.SemaphoreType.DMA((2,)),
                        pltpu.SemaphoreType.DMA((7,)), pltpu.SemaphoreType.DMA((7,)), pltpu.SemaphoreType.DMA],
        compiler_params=pltpu.CompilerParams(has_side_effects=True),
    )(gw, go, gathered, late)


def _adamw(w, g, m, v, name):
    rows, width = w.shape
    tr = min(rows, 256)

    def body(w_ref, g_ref, m_ref, v_ref, d_ref, nm_ref, nv_ref):
        gv = g_ref[...]
        nm = ADAM_B1 * m_ref[...] + (1.0 - ADAM_B1) * gv
        nv = ADAM_B2 * v_ref[...] + (1.0 - ADAM_B2) * (gv * gv)
        m_hat = nm / (1.0 - ADAM_B1 ** ADAM_STEP)
        v_hat = nv / (1.0 - ADAM_B2 ** ADAM_STEP)
        d_ref[...] = -ADAM_LR * (m_hat / (jnp.sqrt(v_hat) + ADAM_EPS) + ADAM_WD * w_ref[...])
        nm_ref[...] = nm
        nv_ref[...] = nv

    t = pl.BlockSpec((tr, width), lambda i: (i, 0))
    return pl.pallas_call(
        body, name=name, grid=(rows // tr,), in_specs=[t] * 4, out_specs=[t] * 3,
        out_shape=[SDS(w.shape, F32)] * 3, compiler_params=_cp(("parallel",)),
    )(w, g, m, v)


def _rowwise(a):
    return jnp.transpose(a, (2, 0, 1)).reshape(SHARD * D // LANE, LANE)


def _columns(ref):
    return jnp.concatenate([ref[pl.ds(c, LANE, stride=8), :].T for c in range(D // LANE)], axis=0)


def _shard_bf16(chip, w_rows):
    def body(j_ref, w_ref, o_ref, prev_ref):
        t = pl.program_id(0)
        cur = _columns(w_ref)

        @pl.when(t == 0)
        def _():
            prev_ref[...] = jnp.zeros_like(prev_ref)

        lane = _iota((D, LANE), 1)
        for s in range(4):
            @pl.when(j_ref[0] == s)
            def _():
                off = SHIFT * s
                moved = cur if s == 0 else jnp.where(lane < off, pltpu.roll(prev_ref[...], off, 1),
                                                     pltpu.roll(cur, off, 1))
                col = t * LANE + lane - off
                o_ref[...] = jnp.where((col >= 0) & (col < SHARD), moved, 0.0).astype(BF16)
        prev_ref[...] = cur

    return pl.pallas_call(
        body, name="shard_bf16",
        grid_spec=pltpu.PrefetchScalarGridSpec(
            num_scalar_prefetch=1, grid=(TILES + 1,),
            in_specs=[pl.BlockSpec((D, LANE), lambda t, j: (t, 0))],
            out_specs=pl.BlockSpec((D, LANE), lambda t, j: (0, t)),
            scratch_shapes=[pltpu.VMEM((D, LANE), F32)]),
        out_shape=SDS((D, WIN), BF16), compiler_params=_cp(("arbitrary",)),
    )(chip, w_rows)


def _whole_w_in(windows):
    tr = 256
    n = windows.shape[0]

    def body(g_ref, o_ref):
        lane = _iota((tr, LANE), 1)
        for s in range(n):
            first = TILES * s
            head = g_ref[s, :, :LANE]
            if s:
                tail = g_ref[s - 1, :, TILES * LANE:]
                head = jnp.where(lane < SHIFT * s, tail.astype(F32), head.astype(F32)).astype(BF16)
            o_ref[:, first * LANE:(first + 1) * LANE] = head
            o_ref[:, (first + 1) * LANE:(first + TILES) * LANE] = g_ref[s, :, LANE:TILES * LANE]
        o_ref[:, n * TILES * LANE:(n * TILES + 1) * LANE] = g_ref[n - 1, :, TILES * LANE:]
        o_ref[:, (n * TILES + 1) * LANE:] = jnp.zeros((tr, DP - (n * TILES + 1) * LANE), BF16)

    return pl.pallas_call(
        body, name="whole_w_in", grid=(D // tr,),
        in_specs=[pl.BlockSpec((n, tr, WIN), lambda t: (0, t, 0))], out_specs=pl.BlockSpec((tr, DP), lambda t: (t, 0)),
        out_shape=SDS((D, DP), BF16), compiler_params=_cp(("parallel",)),
    )(windows)


def _own_buffer(a, name):
    tr = 512
    block = pl.BlockSpec((tr, a.shape[1]), lambda t: (t, 0))

    def body(a_ref, o_ref):
        o_ref[...] = a_ref[...]

    return pl.pallas_call(
        body, name=name, grid=(a.shape[0] // tr,), in_specs=[block], out_specs=block,
        out_shape=SDS(a.shape, a.dtype), compiler_params=_cp(("parallel",)),
    )(a)


def _adamw_in(chip, w_rows, g_win, m_rows, v_rows):
    def body(j_ref, w_ref, g_ref, next_ref, m_ref, v_ref, grad_ref, d_ref, nm_ref, nv_ref):
        columns = _columns
        for s in range(4):
            @pl.when(j_ref[0] == s)
            def _():
                if s == 0:
                    grad_ref[...] = g_ref[...]
                else:
                    back = LANE - SHIFT * s
                    grad_ref[...] = jnp.where(_iota((D, LANE), 1) < back, pltpu.roll(g_ref[...], back, 1),
                                              pltpu.roll(next_ref[...], back, 1))
        gv = grad_ref[...]
        nm = ADAM_B1 * columns(m_ref) + (1.0 - ADAM_B1) * gv
        nv = ADAM_B2 * columns(v_ref) + (1.0 - ADAM_B2) * (gv * gv)
        m_hat = nm / (1.0 - ADAM_B1 ** ADAM_STEP)
        v_hat = nv / (1.0 - ADAM_B2 ** ADAM_STEP)
        d_ref[...] = -ADAM_LR * (m_hat / (jnp.sqrt(v_hat) + ADAM_EPS) + ADAM_WD * columns(w_ref))
        nm_ref[...] = nm
        nv_ref[...] = nv

    tile = pl.BlockSpec((D, LANE), lambda t, j: (0, t))
    next_tile = pl.BlockSpec((D, LANE), lambda t, j: (0, jnp.minimum(t + 1, TILES)))
    rows = pl.BlockSpec((D, LANE), lambda t, j: (t, 0))
    return pl.pallas_call(
        body, name="adamw_in",
        grid_spec=pltpu.PrefetchScalarGridSpec(
            num_scalar_prefetch=1, grid=(TILES + 1,), in_specs=[rows, tile, next_tile, rows, rows],
            out_specs=[tile] * 4),
        out_shape=[SDS((D, SHARD), F32)] * 4, compiler_params=_cp(("parallel",)),
    )(chip, w_rows, g_win, g_win, m_rows, v_rows)


def _rows128(a, rows):
    flat = a.reshape(-1)
    return jnp.pad(flat, (0, rows * LANE - flat.shape[0])).reshape(rows, LANE)


CONV_ROWS = 48


def _pack_small(conv_w, norm_pre, conv_b, ssm_norm, norm_post, dtb, alog, dsk, extra=None):
    cw_rows = CONV_ROWS if conv_w.shape[-1] == 1536 else 16
    extra = jnp.zeros((1, LANE), F32) if extra is None else _rows128(extra, 1)
    vec = jnp.concatenate([_rows128(dtb, 1), _rows128(alog, 1), _rows128(dsk, 1), extra, jnp.zeros((4, LANE), F32)],
                          axis=0)
    return jnp.concatenate([_rows128(conv_w, cw_rows), _rows128(norm_pre, 8), _rows128(conv_b, 16),
                            _rows128(ssm_norm, 8), _rows128(norm_post, 8), vec], axis=0)


def _unpack_small(p, cw_cols):
    cw_rows = CONV_ROWS if cw_cols == 1536 else 16
    o = cw_rows
    conv_w = p[:cw_rows].reshape(-1)[:4 * cw_cols].reshape(1, 4, cw_cols)
    norm_pre = p[o:o + 8].reshape(1, D)
    conv_b = p[o + 8:o + 24].reshape(-1)[:1536].reshape(1, 1536)
    ssm_norm = p[o + 24:o + 32].reshape(1, D)
    norm_post = p[o + 32:o + 40].reshape(1, D)
    vec = p[o + 40:o + 48]
    return conv_w, norm_pre, conv_b, ssm_norm, norm_post, vec[0:1, :NH], vec[1:2, :NH], vec[2:3, :NH], vec[3, 0]


def _pad_lanes(a):
    return jnp.pad(a, ((0, 0), (0, LANE - a.shape[1])))


class _GradReduce:
    LO, HI = ("q", "k", "v", "g"), ("g", "z", "x")

    def __init__(self, xi, yi, ci):
        self.cidx = jnp.reshape(ci, (1,)).astype(jnp.int32)
        self.place = jnp.stack([2 * xi + yi, ci, xi, yi]).astype(jnp.int32)

    def pairs(self, dw_g, dw_z, dw_x, dw_out):
        self.hi = [dw_g, dw_z, dw_x]
        self.go = dw_out.reshape(4, D // 2, D)
        return _PairExchange(self.HI, self.hi, (2, 3), [self.go])

    def first(self, got):
        rw, ro = got
        self.pw_hi = _pair_sum_windows(self.cidx, self.HI, self.hi, (2, 3), rw, "pair_sum_hi")
        self.po = _pair_sum(self.cidx, self.go, ro, "pair_sum_out")
        return _ChipExchange([self.pw_hi, self.po], [1, None])

    def first_done(self, got):
        self.rw_hi, self.ro = got

    def second(self, dw_q, dw_k, dw_v, dw_g, small):
        lo = [dw_q, dw_k, dw_v, dw_g]
        (rw,) = _exchange_call(_PairExchange(self.LO, lo, (0, 1)), "pair_exchange_lo")
        self.pw_lo = _pair_sum_windows(self.cidx, self.LO, lo, (0, 1), rw, "pair_sum_lo")
        return _Both(_ChipExchange([self.pw_lo], [0]), _SmallExchange(small))

    def second_done(self, got):
        self.rw_lo, self.small = got

    def result(self, late, row):
        half_in = _chip_sum_rows(self.place, self.rw_lo, self.pw_lo, self.rw_hi, self.pw_hi, "chip_sum_in")
        half_out = _chip_sum(self.place[0:2], self.ro, self.po, "chip_sum_out")
        return _half_exchange(half_in, half_out, self.small, late, row)


def kernel(x, norm_pre_w, w_in, conv_w, conv_b, dt_bias, a_log, d_skip, ssm_norm_w, w_out, norm_post_w, loss_target, m_norm_pre_w, m_w_in, m_conv_w, m_conv_b, m_dt_bias, m_a_log, m_d_skip, m_ssm_norm_w, m_w_out, m_norm_post_w, v_norm_pre_w, v_w_in, v_conv_w, v_conv_b, v_dt_bias, v_a_log, v_d_skip, v_ssm_norm_w, v_w_out, v_norm_post_w):
    xi, yi, ci = lax.axis_index("x"), lax.axis_index("y"), lax.axis_index("c")
    chip = 2 * xi + yi
    x2, tgt = x[0], loss_target[0]

    chip_idx = jnp.reshape(chip, (1,)).astype(jnp.int32)
    w_rows = _rowwise(w_in)
    w_all = _whole_w_in(_gather_weights(_shard_bf16(chip_idx, w_rows)))
    reduce = _GradReduce(xi, yi, ci)
    grad_x, dnw_pre = _local_step(x2, tgt, w_all, _LateGather(w_out[0].astype(BF16), conv_w[0]), norm_pre_w, conv_b,
                                  dt_bias, a_log, d_skip, ssm_norm_w, norm_post_w, reduce)
    g_win, g_out, small = reduce.result(_rows128(dnw_pre, D // LANE), CONV_ROWS)
    g_small = _slot_sum(small, "small_sum")
    g_cw, g_npre, g_cb, g_nssm, g_npost, g_dtb, g_alog, g_dsk, loss = _unpack_small(g_small, 1536)
    g_cw = lax.dynamic_slice_in_dim(g_cw, chip * 384, 384, axis=2)

    g_in, d_in, nm_in, nv_in = _adamw_in(chip_idx, w_rows, g_win, _rowwise(m_w_in), _rowwise(v_w_in))
    grad_x = _own_buffer(grad_x, "grad_x_copy")
    d_out, nm_out, nv_out = _adamw(w_out[0], g_out, m_w_out[0], v_w_out[0], "adamw_out")
    packed = [_pack_small(*t) for t in (
        (conv_w, norm_pre_w, conv_b, ssm_norm_w, norm_post_w, dt_bias, a_log, d_skip),
        (g_cw, g_npre, g_cb, g_nssm, g_npost, g_dtb, g_alog, g_dsk),
        (m_conv_w, m_norm_pre_w, m_conv_b, m_ssm_norm_w, m_norm_post_w, m_dt_bias, m_a_log, m_d_skip),
        (v_conv_w, v_norm_pre_w, v_conv_b, v_ssm_norm_w, v_norm_post_w, v_dt_bias, v_a_log, v_d_skip))]
    small_out = [_unpack_small(p, 384)[:8] for p in _adamw(*packed, "adamw_small")]

    def ordered(cw_, npre, cb_, nssm, npost, dtb_, alog_, dsk_, big_in, big_out):
        return [npre, big_in[None], cw_, cb_, dtb_, alog_, dsk_, nssm, big_out[None], npost]

    grads = ordered(g_cw, g_npre, g_cb, g_nssm, g_npost, g_dtb, g_alog, g_dsk, g_in, g_out)
    deltas = ordered(*small_out[0], d_in, d_out)
    new_m = ordered(*small_out[1], nm_in, nm_out)
    new_v = ordered(*small_out[2], nv_in, nv_out)
    return (loss, grad_x[None], *grads, *deltas, *new_m, *new_v)


def _local_step(x2, tgt, w_all, late, norm_pre_w, conv_b, dt_bias, a_log, d_skip, ssm_norm_w,
                norm_post_w, reduce=None):
    dtb, alog = _pad_lanes(dt_bias), _pad_lanes(a_log)
    d_b = jnp.repeat(d_skip, 64, axis=1)

    if isinstance(late, _LateGather):
        (proj, u), (gout, gcw) = _inproj_fwd(x2, norm_pre_w, w_all, late)
        w_out_all = gout.reshape(2 * D, D)
        cw_all = jnp.concatenate([gcw[0], gcw[1], gcw[2], gcw[3]], axis=1)
    else:
        proj, u = _inproj_fwd(x2, norm_pre_w, w_all)
        w_out_all, cw_all = late
    mix, attn_pre, lse = _attn_fwd(proj, 1, _attn_fwd(proj, 4, _attn_fwd(proj, 16)), final=True)
    mix, y_save, states, conv_out = _ssm_fwd(proj, mix, cw_all, conv_b, dtb, alog, d_b, ssm_norm_w)

    dy, dn_ssm, do, delta, dg, dw_out, dnw_post, loss_part = _outproj_loss(mix, w_out_all, x2, tgt, norm_post_w,
                                                                          attn_pre, proj)
    dz, dxbcdt, dcw, dcb, dvec, dnw_ssm = _ssm_bwd(proj, dn_ssm, y_save, states, conv_out, cw_all, dtb, alog, d_b,
                                                   ssm_norm_w)
    dw_g, dw_z, dw_x = _dw(u, dg, "dw_in_g"), _dw(u, dz, "dw_in_z"), _dw(u, dxbcdt, "dw_in_xbcdt", X_COLS)
    acc = _attn_bwd(proj, do, lse, delta, 16, None, F32, reduce.pairs(dw_g, dw_z, dw_x, dw_out) if reduce else None)
    if reduce:
        acc, got = acc
    acc = _attn_bwd(proj, do, lse, delta, 4, acc, F32, reduce.first(got) if reduce else None)
    if reduce:
        acc, got = acc
        reduce.first_done(got)
    dq, dk, dv = _attn_bwd(proj, do, lse, delta, 1, acc, BF16)
    dw_q, dw_k, dw_v = _dw(u, dq, "dw_in_q"), _dw(u, dk, "dw_in_k"), _dw(u, dv, "dw_in_v")

    def small(dnw_pre):
        return _pack_small(dcw, dnw_pre, dcb, dnw_ssm, dnw_post, dvec[0:1, :NH], dvec[1:2, :NH], dvec[2:3, :NH],
                           loss_part[:, :1])

    res = _inproj_bwd_dx([dq, dk, dv, dg, dz], dxbcdt, w_all, x2, dy, norm_pre_w,
                         reduce.second(dw_q, dw_k, dw_v, dw_g, small(jnp.zeros((1, D), F32))) if reduce else None)
    if reduce:
        res, got = res
        reduce.second_done(got)
        return res
    grad_x, dnw_pre = res
    dw_all = jnp.concatenate([dw_q, dw_k, dw_v, dw_g, dw_z, dw_x], axis=1)
    return grad_x, small(dnw_pre), dw_all, dw_out
```

```python
import functools

import jax
import jax.numpy as jnp
from jax import lax
from jax.experimental import pallas as pl
from jax.experimental.pallas import tpu as pltpu

F32 = jnp.float32
BF16 = jnp.bfloat16
MESH = pl.DeviceIdType.MESH
SDS = jax.ShapeDtypeStruct
ANY = pl.BlockSpec(memory_space=pl.ANY)

S = 4096
D = 1024
DP = 7168
SHARD = 1668
OFF_G, OFF_Z = 3072, 4096
NH = 16
CH = 128
NC = S // CH
EPS = 1e-6
NEG = -1e30
LANE = 128
VMEM_LIMIT = 48 * 1024 * 1024

TILES = SHARD // LANE
WIN = (TILES + 1) * LANE
SHIFT = SHARD - TILES * LANE
SECTION_TILES = {"q": (0, 8), "k": (8, 8), "v": (16, 8), "g": (24, 8), "z": (32, 8), "x": (40, 13)}
X_COLS = SECTION_TILES["x"][1] * LANE

ADAM_LR, ADAM_B1, ADAM_B2, ADAM_EPS, ADAM_WD, ADAM_STEP = 0.001, 0.9, 0.999, 1e-08, 0.01, 10


def _cp(sem, **kw):
    return pltpu.CompilerParams(dimension_semantics=sem, vmem_limit_bytes=VMEM_LIMIT, **kw)


def _dot(a, b):
    return jnp.dot(a, b, preferred_element_type=F32)


def _dot_nt(a, b):
    return lax.dot_general(a, b, (((1,), (1,)), ((), ())), preferred_element_type=F32)


def _dot_tn(a, b):
    return lax.dot_general(a, b, (((0,), (0,)), ((), ())), preferred_element_type=F32)


def _pieces(x, n):
    out = []
    for _ in range(n):
        p = x.astype(BF16)
        out.append(p)
        x = x - p.astype(F32)
    return out


def _pick(x, sel, n=2):
    parts = [_dot(p, sel) for p in _pieces(x, n)]
    return functools.reduce(jnp.add, parts)


def _pick_left(sel, x, n=3):
    parts = [_dot(sel, p) for p in _pieces(x, n)]
    return functools.reduce(jnp.add, parts)


def _sigmoid(v):
    return 0.5 * jnp.tanh(0.5 * v) + 0.5


def _iota(shape, dim):
    return lax.broadcasted_iota(jnp.int32, shape, dim)


def _inproj_fwd(x, nw, w_all, hosted=None):
    tm, tn = 1024, 1024
    n_host = len(hosted.arrays) if hosted else 0

    def body(x_ref, nw_ref, w_ref, *refs):
        host_in, (proj_ref, u_ref), refs = refs[:n_host], refs[n_host:n_host + 2], refs[n_host + 2:]
        host_out, host_sems = refs[:n_host], refs[n_host:]
        i, j = pl.program_id(0), pl.program_id(1)
        if hosted:
            pl.when((i == 0) & (j == 0))(lambda: hosted.start(host_in, host_out, host_sems))

        @pl.when(j == 0)
        def _():
            xf = x_ref[...]
            r = lax.rsqrt(jnp.mean(xf * xf, axis=-1, keepdims=True) + EPS)
            u_ref[...] = (xf * r * nw_ref[...]).astype(BF16)

        proj_ref[...] = _dot(u_ref[...], w_ref[...])
        if hosted:
            pl.when((i == S // tm - 1) & (j == DP // tn - 1))(lambda: hosted.finish(host_in, host_out, host_sems))

    outs = pl.pallas_call(
        body, name="inproj_fwd", grid=(S // tm, DP // tn),
        in_specs=[pl.BlockSpec((tm, D), lambda i, j: (i, 0)), pl.BlockSpec((1, D), lambda i, j: (0, 0)),
                  pl.BlockSpec((D, tn), lambda i, j: (0, j))] + [ANY] * n_host,
        out_specs=[pl.BlockSpec((tm, tn), lambda i, j: (i, j)), pl.BlockSpec((tm, D), lambda i, j: (i, 0))]
        + [ANY] * n_host,
        out_shape=[SDS((S, DP), F32), SDS((S, D), BF16)] + (hosted.out_shape if hosted else []),
        scratch_shapes=hosted.scratch if hosted else [],
        compiler_params=_cp(("arbitrary", "arbitrary") if hosted else ("parallel", "arbitrary")),
    )(x, nw, w_all, *(hosted.arrays if hosted else []))
    return (outs[:2], outs[2:]) if hosted else outs


ATTN_QB = {1: 16, 4: 4, 16: 1}


def _unit_rows(r, u, d):
    return pl.ds(r + d * CH * u, CH, stride=d) if d > 1 else pl.ds(CH * u, CH)


def _for_units(d, qb, fn):
    for r in range(d):
        for u in range(qb):
            fn(r, u)


def _attn_mask(has_prev):
    qi, kj = _iota((2 * CH, 2 * CH), 0) & (CH - 1), _iota((2 * CH, 2 * CH), 1)
    cur_ok = (kj >= CH) & (kj - CH <= qi)
    prev_ok = (kj < CH) & (kj >= qi)
    return cur_ok | (prev_ok & has_prev)


def _stack_heads(v, lane_a):
    return jnp.concatenate([jnp.where(lane_a, v, 0.0), jnp.where(lane_a, 0.0, v)], axis=0).astype(BF16)


def _attn_specs(d, qb):
    rows, prows = CH * d * qb, CH * d
    nb = S // rows
    steps = (NH // 2) * nb

    def at(t):
        t = jnp.minimum(t, steps - 1)
        return t % nb, t // nb

    def cur(off):
        return pl.BlockSpec((rows, LANE), lambda t: (at(t)[0], off + at(t)[1]))

    def prev(off):
        return pl.BlockSpec((prows, LANE), lambda t: (jnp.maximum(at(t)[0] * qb - 1, 0), off + at(t)[1]))

    lag = pl.BlockSpec((rows, LANE), lambda t: at(jnp.maximum(t - 1, 0)))
    return nb, steps, cur, prev, lag


def _gather16(src_ref, dense_ref, tmp_ref):
    for a in range(4):
        tmp_ref[...] = src_ref[pl.ds(a, 4 * CH, stride=4), :]
        for b in range(4):
            dense_ref[a + 4 * b] = tmp_ref[pl.ds(b, CH, stride=4), :]


def _scatter16(dense_ref, dst_ref, tmp_ref):
    for a in range(4):
        for b in range(4):
            tmp_ref[pl.ds(b, CH, stride=4), :] = dense_ref[a + 4 * b]
        dst_ref[pl.ds(a, 4 * CH, stride=4), :] = tmp_ref[...]


def _unit_index(r, u, d):
    return (r,) if d == 16 else (_unit_rows(r, u, d), slice(None))


def _unit_kv(p_ref, c_ref, r, u, d):
    prev = p_ref[_unit_index(r, 0, d)] if u == 0 else c_ref[_unit_index(r, u - 1, d)]
    return jnp.concatenate([prev, c_ref[_unit_index(r, u, d)]], axis=0).astype(BF16)


def _dense_scratch(d, n):
    return [pltpu.VMEM((16, CH, LANE), F32)] * n + [pltpu.VMEM((4 * CH, LANE), F32)] if d == 16 else []


def _attn_fwd(proj, d, prior=None, final=False):
    qb = ATTN_QB[d]
    nb, steps, cur, prev, _ = _attn_specs(d, qb)
    n_prior = 2 if prior is not None else 0
    n_in, n_out = 5 + n_prior + final, 2 + final
    assert not (d == 16 and (n_prior or final))

    def body(*refs):
        ins, outs, scratch = refs[:n_in], refs[n_in:n_in + n_out], refs[n_in + n_out:]
        if d == 16:
            tmp_ref = scratch[-1]
            for src, dense in zip(ins, scratch):
                _gather16(src, dense, tmp_ref)
            block_outs, ins, outs = outs, scratch[:n_in], scratch[n_in:n_in + n_out]
        q_ref, kp_ref, kc_ref, vp_ref, vc_ref = ins[:5]
        prior_refs = ins[5:5 + n_prior]
        if final:
            g_ref, (mix_ref, o_ref, l_ref) = ins[-1], outs
        else:
            o_ref, l_ref = outs
        i = pl.program_id(0) % nb
        lane_a = _iota((CH, LANE), 1) < 64
        mask_first, mask_rest = _attn_mask(i > 0), _attn_mask(True)

        def unit(r, u):
            at = _unit_index(r, u, d)
            q2 = _stack_heads(q_ref[at] * 0.125, lane_a)
            k2, v2 = _unit_kv(kp_ref, kc_ref, r, u, d), _unit_kv(vp_ref, vc_ref, r, u, d)
            s = jnp.where(mask_first if u == 0 else mask_rest, _dot_nt(q2, k2), NEG)
            m = jnp.max(s, axis=1, keepdims=True)
            p = jnp.exp(s - m)
            l = jnp.sum(p, axis=1, keepdims=True)
            o2 = _dot(p.astype(BF16), v2) / l
            lse2 = m + jnp.log(l)
            o = jnp.where(lane_a, o2[:CH], o2[CH:])
            lse = jnp.where(lane_a, lse2[:CH], lse2[CH:])
            if n_prior:
                o_a, l_a = prior_refs[0][at], prior_refs[1][at]
                top = jnp.maximum(l_a, lse)
                e_a, e_b = jnp.exp(l_a - top), jnp.exp(lse - top)
                tot = e_a + e_b
                o = (e_a * o_a + e_b * o) / tot
                lse = top + jnp.log(tot)
            o_ref[at] = o
            l_ref[at] = lse
            if final:
                g = g_ref[at]
                mix_ref[at] = (o * (g * _sigmoid(g))).astype(BF16)

        _for_units(d, qb, unit)
        if d == 16:
            for dense, dst in zip(outs, block_outs):
                _scatter16(dense, dst, tmp_ref)

    in_specs = [cur(0), prev(8), cur(8), prev(16), cur(16)] + [cur(0)] * n_prior
    args = [proj] * 5 + (list(prior) if n_prior else [])
    out_specs, out_shape = [cur(0), cur(0)], [SDS((S, D), F32), SDS((S, D), F32)]
    if final:
        assert d == 1
        in_specs.append(cur(OFF_G // LANE))
        args.append(proj)
        out_specs, out_shape = [cur(0)] + out_specs, [SDS((S, 2 * D), BF16)] + out_shape
    return pl.pallas_call(
        body, name=f"attn_fwd_d{d}", grid=(steps,),
        in_specs=in_specs, out_specs=out_specs, out_shape=out_shape,
        scratch_shapes=_dense_scratch(d, n_in + n_out),
        compiler_params=_cp(("parallel",)),
    )(*args)


def _attn_bwd(proj, do, lse, delta, d, acc, out_dtype, hosted=None):
    qb = ATTN_QB[d]
    nb, steps, cur, prev, lag = _attn_specs(d, qb)
    has_acc = acc is not None
    n_in = 11 if has_acc else 8
    n_host, n_host_out = (len(hosted.arrays), len(hosted.out_shape)) if hosted else (0, 0)
    assert not (d == 16 and (has_acc or out_dtype != F32))
    rows = CH * d * qb
    carry = (2, 16, CH, LANE) if d == 16 else (2, rows, LANE)

    def body(*refs):
        ins, host_in, refs = refs[:n_in], refs[n_in:n_in + n_host], refs[n_in + n_host:]
        (dq_ref, dk_ref, dv_ref), host_out, scratch = refs[:3], refs[3:3 + n_host_out], refs[3 + n_host_out:]
        if hosted:
            scratch, host_sems = scratch[:-len(hosted.scratch)], scratch[-len(hosted.scratch):]
        ck_ref, cv_ref = scratch[:2]
        dq_f32 = dq_ref if out_dtype == F32 else scratch[2]
        t = pl.program_id(0)
        i = t % nb
        if hosted:
            pl.when(t == 0)(lambda: hosted.start(host_in, host_out, host_sems))
        if d == 16:
            dense, dq_f32, tmp_ref = scratch[2:2 + n_in], scratch[2 + n_in], scratch[-1]

            @pl.when(t < steps)
            def _():
                for src, dst in zip(ins, dense):
                    _gather16(src, dst, tmp_ref)

            ins = dense
        q_ref, kp_ref, kc_ref, vp_ref, vc_ref, do_ref, lse_ref, dl_ref = ins[:8]
        if has_acc:
            aq_ref, ak_ref, av_ref = ins[8:11]
        slot = t & 1
        now_k, now_v, old_k, old_v = ck_ref.at[slot], cv_ref.at[slot], ck_ref.at[1 - slot], cv_ref.at[1 - slot]
        lane_a = _iota((CH, LANE), 1) < 64
        mask_first, mask_rest = _attn_mask(i > 0), _attn_mask(True)

        @pl.when(t == 0)
        def _():
            ck_ref[1] = jnp.zeros(carry[1:], F32)
            cv_ref[1] = jnp.zeros(carry[1:], F32)

        def unit(r, u):
            at = _unit_index(r, u, d)
            q2 = _stack_heads(q_ref[at] * 0.125, lane_a)
            do2 = _stack_heads(do_ref[at], lane_a)
            k2, v2 = _unit_kv(kp_ref, kc_ref, r, u, d), _unit_kv(vp_ref, vc_ref, r, u, d)
            lsev, dlv = lse_ref[at], dl_ref[at]
            lse2 = jnp.concatenate([lsev[:, 0:1], lsev[:, 64:65]], axis=0)
            dl2 = jnp.concatenate([dlv[:, 0:1], dlv[:, 64:65]], axis=0)
            p = jnp.exp(jnp.where(mask_first if u == 0 else mask_rest, _dot_nt(q2, k2), NEG) - lse2)
            ds = (p * (_dot_nt(do2, v2) - dl2)).astype(BF16)
            dq2 = _dot(ds, k2)
            dk2 = _dot_tn(ds, q2)
            dv2 = _dot_tn(p.astype(BF16), do2)
            dq = jnp.where(lane_a, dq2[:CH], dq2[CH:]) * 0.125
            if has_acc:
                dq = dq + aq_ref[at]
            dq_f32[at] = dq
            if u == 0:
                before = _unit_index(r, qb - 1, d)
                old_k[before] += dk2[:CH]
                old_v[before] += dv2[:CH]
            else:
                before = _unit_index(r, u - 1, d)
                now_k[before] += dk2[:CH]
                now_v[before] += dv2[:CH]
            now_k[at] = dk2[CH:]
            now_v[at] = dv2[CH:]

        @pl.when(t < steps)
        def _():
            _for_units(d, qb, unit)
            if d == 16:
                _scatter16(dq_f32, dq_ref, tmp_ref)
            elif out_dtype != F32:
                dq_ref[...] = dq_f32[...].astype(out_dtype)

        if d == 16:
            _scatter16(old_k, dk_ref, tmp_ref)
            _scatter16(old_v, dv_ref, tmp_ref)
        else:
            dk, dv = old_k[...], old_v[...]
            if has_acc:
                dk, dv = dk + ak_ref[...], dv + av_ref[...]
            dk_ref[...] = dk.astype(out_dtype)
            dv_ref[...] = dv.astype(out_dtype)
        if hosted:
            pl.when(t == steps)(lambda: hosted.finish(host_in, host_out, host_sems))

    in_specs = [cur(0), prev(8), cur(8), prev(16), cur(16), cur(0), cur(0), cur(0)]
    args = [proj, proj, proj, proj, proj, do, lse, delta]
    if has_acc:
        in_specs += [cur(0), lag, lag]
        args += list(acc)
    scratch = [pltpu.VMEM(carry, F32), pltpu.VMEM(carry, F32)]
    if d == 16:
        scratch += _dense_scratch(d, n_in + 1)
    elif out_dtype != F32:
        scratch.append(pltpu.VMEM((rows, LANE), F32))
    out_specs, out_shape = [cur(0), lag, lag], [SDS((S, D), out_dtype)] * 3
    if hosted:
        args += hosted.arrays
        in_specs += [ANY] * n_host
        out_specs += [ANY] * n_host_out
        out_shape += hosted.out_shape
        scratch += hosted.scratch
    outs = pl.pallas_call(
        body, name=f"attn_bwd_d{d}", grid=(steps + 1,),
        in_specs=in_specs, out_specs=out_specs, out_shape=out_shape,
        scratch_shapes=scratch, compiler_params=_cp(("arbitrary",)),
    )(*args)
    return (outs[:3], outs[3:]) if hosted else outs


def _conv_taps(cur, prev8, first):
    row8 = _iota(prev8.shape, 0)
    prev8 = jnp.where(first, 0.0, prev8)
    taps = []
    for s in (3, 2, 1):
        rolled = pltpu.roll(cur, s, 0)
        head = jnp.where(row8 < s, pltpu.roll(prev8, s, 0), rolled[:8])
        taps.append(jnp.concatenate([head, rolled[8:]], axis=0))
    return taps + [cur]


def _conv(taps, w, b):
    acc = b + w[0:1, :] * taps[0]
    for k in (1, 2, 3):
        acc = acc + w[k:k + 1, :] * taps[k]
    return acc


def _expand():
    return (_iota((LANE, D), 1) // 64 == _iota((LANE, D), 0)).astype(BF16)


def _reduce():
    return (_iota((D, LANE), 0) // 64 == _iota((D, LANE), 1)).astype(BF16)


def _ssd_common(xs_c, bc_c, dt_raw, dtb, alog):
    head_lane = _iota((CH, LANE), 1) < NH
    xs = xs_c * _sigmoid(xs_c)
    bc = bc_c * _sigmoid(bc_c)
    pre = dt_raw + dtb
    dt = jnp.where(head_lane, jnp.maximum(pre, 0.0) + jnp.log(1.0 + jnp.exp(-jnp.abs(pre))), 0.0)
    a_row = jnp.where(head_lane[0:1], -jnp.exp(alog), 0.0)
    tri = (_iota((CH, CH), 1) <= _iota((CH, CH), 0)).astype(BF16)
    cs = _pick_left(tri, dt * a_row)
    cs_last = cs[CH - 1:CH, :]
    wide = _pick(jnp.concatenate([dt, jnp.exp(cs), jnp.exp(cs_last - cs)], axis=0), _expand())
    dt_b, e_b, f_b = wide[:CH], wide[CH:2 * CH], wide[2 * CH:]
    return dict(xs=xs, bc=bc, pre=pre, dt=dt, a_row=a_row, cs=cs, cs_t=cs.T, dt_b=dt_b, e_b=e_b, f_b=f_b,
                t_b=e_b[CH - 1:CH, :])


def _groups(bc):
    bcb = bc.astype(BF16)
    return [bcb[:, 0:128], bcb[:, 128:256]], [bcb[:, 256:384], bcb[:, 384:512]]


def _decay(q, h, tril):
    seg = q["cs"][:, h:h + 1] - q["cs_t"][h:h + 1, :]
    return jnp.exp(jnp.where(tril, seg, NEG))


def _ssm_fwd(proj, mix, cw, cb, dtb, alog, d_b, nw):
    def body(xs_ref, xsp_ref, bc_ref, bcp_ref, dt_ref, z_ref, cw_ref, cb_ref, dtb_ref, alog_ref, db_ref, nw_ref,
             mix_in_ref, mix_ref, y_ref, st_ref, conv_ref, h_ref):
        del mix_in_ref
        i = pl.program_id(0)

        @pl.when(i == 0)
        def _():
            h_ref[...] = jnp.zeros_like(h_ref)

        cw, cb = cw_ref[...], cb_ref[...]
        xs_c = _conv(_conv_taps(xs_ref[...], xsp_ref[...], i == 0), cw[:, :D], cb[:, :D])
        bc_c = _conv(_conv_taps(bc_ref[...], bcp_ref[...], i == 0), cw[:, D:], cb[:, D:])
        conv_ref[:, :D] = xs_c
        conv_ref[:, D:] = bc_c
        q = _ssd_common(xs_c, bc_c, dt_ref[...], dtb_ref[...], alog_ref[...])
        bg, cg = _groups(q["bc"])
        xs = q["xs"]
        xdt = xs * q["dt_b"]
        xdt_b = xdt.astype(BF16)
        h_in = h_ref[...]
        st_ref[...] = h_in
        hb = h_in.astype(BF16)
        tril = _iota((CH, CH), 1) <= _iota((CH, CH), 0)
        lane_a = _iota((CH, LANE), 1) < 64
        cbm = [_dot_nt(cg[g], bg[g]) for g in range(2)]
        pairs = []
        for hp in range(NH // 2):
            xp = xdt_b[:, hp * LANE:(hp + 1) * LANE]
            ya = _dot((cbm[hp // 4] * _decay(q, 2 * hp, tril)).astype(BF16), xp)
            yb = _dot((cbm[hp // 4] * _decay(q, 2 * hp + 1, tril)).astype(BF16), xp)
            pairs.append(jnp.where(lane_a, ya, yb))
        y_diag = jnp.concatenate(pairs, axis=1)
        y_off = jnp.concatenate([_dot(cg[g], hb[:, g * 512:(g + 1) * 512]) for g in range(2)], axis=1) * q["e_b"]
        y = y_diag + y_off + db_ref[...] * xs
        y_ref[...] = y
        xf = (xdt * q["f_b"]).astype(BF16)
        h_ref[...] = q["t_b"] * h_in + jnp.concatenate(
            [_dot_tn(bg[g], xf[:, g * 512:(g + 1) * 512]) for g in range(2)], axis=1)
        z = z_ref[...]
        yz = y * (z * _sigmoid(z))
        outs = []
        for g in range(2):
            v = yz[:, g * 512:(g + 1) * 512]
            outs.append(v * lax.rsqrt(jnp.mean(v * v, axis=-1, keepdims=True) + EPS))
        mix_ref[...] = (jnp.concatenate(outs, axis=1) * nw_ref[...]).astype(BF16)

    def col(width, blk, prev=False):
        if prev:
            return pl.BlockSpec((8, width), lambda i: (jnp.maximum(i * (CH // 8) - 1, 0), blk))
        return pl.BlockSpec((CH, width), lambda i: (i, blk))

    def full(a):
        return pl.BlockSpec(a.shape, lambda i: (0,) * a.ndim)

    return pl.pallas_call(
        body, name="ssm_fwd", grid=(NC,),
        in_specs=[col(D, 5), col(D, 5, True), col(512, 12), col(512, 12, True), col(LANE, 52), col(D, 4),
                  full(cw), full(cb), full(dtb), full(alog), full(d_b), full(nw), ANY],
        out_specs=[col(D, 1), col(D, 0), pl.BlockSpec((None, CH, D), lambda i: (i, 0, 0)), col(D + 512, 0)],
        out_shape=[SDS((S, 2 * D), BF16), SDS((S, D), F32), SDS((NC, CH, D), F32), SDS((S, D + 512), F32)],
        scratch_shapes=[pltpu.VMEM((CH, D), F32)],
        input_output_aliases={12: 0},
        compiler_params=_cp(("arbitrary",)),
    )(proj, proj, proj, proj, proj, proj, cw, cb, dtb, alog, d_b, nw, mix)


def _ssm_bwd(proj, dn, y_save, states, conv_out, cw, dtb, alog, d_b, nw):
    def body(xs_ref, bc_ref, dt_ref, z_ref, dn_ref, y_ref, st_ref, conv_ref,
             cw_ref, dtb_ref, alog_ref, db_ref, nw_ref,
             dz_ref, dx_ref, dcw_ref, dcb_ref, dsm_ref, dnw_ref, dh_ref, nxs_ref, nbc_ref):
        i = pl.program_id(0)
        ci = NC - 1 - i

        @pl.when(i == 0)
        def _():
            for ref in (dcw_ref, dcb_ref, dsm_ref, dnw_ref, dh_ref, nxs_ref, nbc_ref):
                ref[...] = jnp.zeros_like(ref)

        cw = cw_ref[...]
        xs_c, bc_c = conv_ref[:, :D], conv_ref[:, D:]
        q = _ssd_common(xs_c, bc_c, dt_ref[...], dtb_ref[...], alog_ref[...])
        bg, cg = _groups(q["bc"])
        xs, dt_b, e_b, f_b, t_b = q["xs"], q["dt_b"], q["e_b"], q["f_b"], q["t_b"]
        xdt = xs * dt_b
        xdt_b = xdt.astype(BF16)
        h_in = st_ref[...]
        hb = h_in.astype(BF16)
        dh_new = dh_ref[...]
        dhb = dh_new.astype(BF16)
        red = _reduce()

        z, y, dn, nw_v = z_ref[...], y_ref[...], dn_ref[...], nw_ref[...]
        sig = _sigmoid(z)
        sz = z * sig
        yz = y * sz
        gdn = dn * nw_v
        dyz, dnw = [], []
        for g in range(2):
            v, gv = yz[:, g * 512:(g + 1) * 512], gdn[:, g * 512:(g + 1) * 512]
            r = lax.rsqrt(jnp.mean(v * v, axis=-1, keepdims=True) + EPS)
            dnw.append(dn[:, g * 512:(g + 1) * 512] * v * r)
            dyz.append(r * (gv - v * (r * r) * jnp.mean(gv * v, axis=-1, keepdims=True)))
        dyz = jnp.concatenate(dyz, axis=1)
        dnw_ref[...] += jnp.sum(jnp.concatenate(dnw, axis=1), axis=0, keepdims=True)
        dy = dyz * sz
        dz_ref[...] = (dyz * y * (sig * (1.0 + z * (1.0 - sig)))).astype(BF16)
        dy_b = dy.astype(BF16)

        tril = _iota((CH, CH), 1) <= _iota((CH, CH), 0)
        lane_a = _iota((CH, LANE), 1) < 64
        cbm = [_dot_nt(cg[g], bg[g]) for g in range(2)]
        dcbm = [jnp.zeros((CH, CH), F32), jnp.zeros((CH, CH), F32)]
        seg_rows = jnp.zeros((CH, LANE), F32)
        seg_cols = jnp.zeros((LANE, CH), F32)
        row_id, col_id = _iota((CH, LANE), 0), _iota((CH, LANE), 1)
        dx_pairs = []
        for hp in range(NH // 2):
            g = hp // 4
            xp = xdt_b[:, hp * LANE:(hp + 1) * LANE]
            dyp_f = dy[:, hp * LANE:(hp + 1) * LANE]
            dyp = dy_b[:, hp * LANE:(hp + 1) * LANE]
            halves = []
            for k in range(2):
                h = 2 * hp + k
                lane = lane_a if k == 0 else jnp.logical_not(lane_a)
                dec = _decay(q, h, tril)
                gm = cbm[g] * dec
                dgm = _dot_nt(jnp.where(lane, dyp_f, 0.0).astype(BF16), xp)
                dcbm[g] = dcbm[g] + dgm * dec
                prod = dgm * gm
                seg_rows = jnp.where(col_id == h, jnp.sum(prod, axis=1, keepdims=True), seg_rows)
                seg_cols = jnp.where(row_id == h, jnp.sum(prod, axis=0, keepdims=True), seg_cols)
                halves.append(_dot_tn(gm.astype(BF16), dyp))
            dx_pairs.append(jnp.where(lane_a, halves[0], halves[1]))
        dxdt_diag = jnp.concatenate(dx_pairs, axis=1)

        qv = jnp.concatenate([_dot(bg[g], dhb[:, g * 512:(g + 1) * 512]) for g in range(2)], axis=1)
        y_off = jnp.concatenate([_dot(cg[g], hb[:, g * 512:(g + 1) * 512]) for g in range(2)], axis=1) * e_b
        xfq = xdt * f_b * qv
        dxdt = dxdt_diag + f_b * qv
        tdt = jnp.sum(dh_new * h_in, axis=0, keepdims=True) * t_b
        per_head = _pick(jnp.concatenate([xfq, dy * y_off, dxdt * xs, dy * xs, jnp.broadcast_to(tdt, (8, D))],
                                         axis=0), red)
        fdf, dyoff_h, dxdtxs_h, dyxs_h = [per_head[k * CH:(k + 1) * CH] for k in range(4)]
        dcs = seg_rows - seg_cols.T + dyoff_h - fdf
        last = per_head[4 * CH:4 * CH + 1] + jnp.sum(fdf, axis=0, keepdims=True)
        dcs = dcs + jnp.where(_iota((CH, LANE), 0) == CH - 1, last, 0.0)
        tri_t = (_iota((CH, CH), 1) >= _iota((CH, CH), 0)).astype(BF16)
        da = _pick_left(tri_t, dcs)
        ddt = da * q["a_row"] + dxdtxs_h
        dxs = dxdt * dt_b + db_ref[...] * dy
        ddt_raw = ddt * _sigmoid(q["pre"])
        dsm_ref[0:1, :] += jnp.sum(ddt_raw, axis=0, keepdims=True)
        dsm_ref[1:2, :] += jnp.sum(da * q["dt"], axis=0, keepdims=True) * q["a_row"]
        dsm_ref[2:3, :] += jnp.sum(dyxs_h, axis=0, keepdims=True)
        edy = (e_b * dy).astype(BF16)
        xf = (xdt * f_b).astype(BF16)
        dbs, dcs_g, dhs = [], [], []
        for g in range(2):
            sl = slice(g * 512, (g + 1) * 512)
            dcb_b = dcbm[g].astype(BF16)
            dcs_g.append(_dot(dcb_b, bg[g]) + _dot_nt(edy[:, sl], hb[:, sl]))
            dbs.append(_dot_tn(dcb_b, cg[g]) + _dot_nt(xf[:, sl], dhb[:, sl]))
            dhs.append(_dot_tn(cg[g], edy[:, sl]))
        dh_ref[...] = t_b * dh_new + jnp.concatenate(dhs, axis=1)
        dbc = jnp.concatenate(dbs + dcs_g, axis=1)

        def conv_bwd(dact, pre, x_raw, w, nxt_ref, lo):
            s = _sigmoid(pre)
            dconv = dact * (s * (1.0 + pre * (1.0 - s)))
            nxt8 = nxt_ref[...]
            row8 = _iota(nxt8.shape, 0)
            hi = lo + dconv.shape[1]
            dcb_ref[:, lo:hi] += jnp.sum(dconv, axis=0, keepdims=True)
            later = [dconv]
            for s_ in (1, 2, 3):
                rolled = pltpu.roll(dconv, CH - s_, 0)
                tail = jnp.where(row8 >= 8 - s_, pltpu.roll(nxt8, 8 - s_, 0), rolled[CH - 8:])
                later.append(jnp.concatenate([rolled[:CH - 8], tail], axis=0))
            dx = None
            for s_, up in enumerate(later):
                k = 3 - s_
                dcw_ref[k:k + 1, lo:hi] += jnp.sum(up * x_raw, axis=0, keepdims=True)
                dx = w[k:k + 1, :] * up if dx is None else dx + w[k:k + 1, :] * up
            nxt_ref[...] = dconv[:8]
            return dx

        dx_ref[:, 0:D] = conv_bwd(dxs, xs_c, xs_ref[...], cw[:, :D], nxs_ref, 0).astype(BF16)
        dx_ref[:, D:D + 512] = conv_bwd(dbc, bc_c, bc_ref[...], cw[:, D:], nbc_ref, D).astype(BF16)
        dx_ref[:, D + 512:D + 640] = ddt_raw.astype(BF16)
        dx_ref[:, D + 640:] = jnp.zeros((CH, D - 640), BF16)

    def col(width, blk):
        return pl.BlockSpec((CH, width), lambda i: (NC - 1 - i, blk))

    def full(a):
        return pl.BlockSpec(a.shape, lambda i: (0,) * len(a.shape))

    acc_shapes = [SDS((4, 1536), F32), SDS((1, 1536), F32), SDS((8, LANE), F32), SDS((1, D), F32)]
    return pl.pallas_call(
        body, name="ssm_bwd", grid=(NC,),
        in_specs=[col(D, 5), col(512, 12), col(LANE, 52), col(D, 4),
                  col(D, 0), col(D, 0), pl.BlockSpec((None, CH, D), lambda i: (NC - 1 - i, 0, 0)), col(D + 512, 0),
                  full(cw), full(dtb), full(alog), full(d_b), full(nw)],
        out_specs=[col(D, 0), col(2 * D, 0)] + [full(a) for a in acc_shapes],
        out_shape=[SDS((S, D), BF16), SDS((S, 2 * D), BF16)] + acc_shapes,
        scratch_shapes=[pltpu.VMEM((CH, D), F32), pltpu.VMEM((8, D), F32), pltpu.VMEM((8, 512), F32)],
        compiler_params=_cp(("arbitrary",)),
    )(proj, proj, proj, proj, dn, y_save, states, conv_out, cw, dtb, alog, d_b, nw)


def _outproj_loss(mix, w_out, x, tgt, nw, attn_pre, proj):
    tm = 256

    def body(mix_ref, w_ref, x_ref, t_ref, nw_ref, pre_ref, g_ref,
             dy_ref, dn_ref, do_ref, delta_ref, dg_ref, dw_ref, dnw_ref, loss_ref):
        @pl.when(pl.program_id(0) == 0)
        def _():
            dw_ref[...] = jnp.zeros_like(dw_ref)
            dnw_ref[...] = jnp.zeros_like(dnw_ref)
            loss_ref[...] = jnp.zeros_like(loss_ref)

        mixv, w = mix_ref[...], w_ref[...]
        out = _dot(mixv, w)
        r = lax.rsqrt(jnp.mean(out * out, axis=-1, keepdims=True) + EPS)
        nh = out * r
        nw_v = nw_ref[...]
        err = x_ref[...] + nh * nw_v - t_ref[...]
        loss_ref[...] += 0.5 * jnp.sum(jnp.mean(err * err, axis=-1, keepdims=True), axis=0, keepdims=True)
        dy = err * (1.0 / D)
        dy_ref[...] = dy
        dnw_ref[...] += jnp.sum(dy * nh, axis=0, keepdims=True)
        gdn = dy * nw_v
        dout = (r * (gdn - nh * jnp.mean(gdn * nh, axis=-1, keepdims=True))).astype(BF16)
        dmix = _dot_nt(dout, w)
        dw_ref[...] += _dot_tn(mixv, dout)
        dn_ref[...] = dmix[:, D:]
        dm, g, pre_v = dmix[:, :D], g_ref[...], pre_ref[...]
        sig = _sigmoid(g)
        do = dm * (g * sig)
        do_ref[...] = do
        dg_ref[...] = (dm * pre_v * (sig * (1.0 + g * (1.0 - sig)))).astype(BF16)
        prod = do * pre_v
        same_head = (_iota((LANE, LANE), 0) // 64 == _iota((LANE, LANE), 1) // 64).astype(BF16)
        for cb in range(D // LANE):
            delta_ref[:, cb * LANE:(cb + 1) * LANE] = _pick(prod[:, cb * LANE:(cb + 1) * LANE], same_head)

    row = lambda w: pl.BlockSpec((tm, w), lambda i: (i, 0))
    full = lambda s: pl.BlockSpec(s, lambda i: (0, 0))
    return pl.pallas_call(
        body, name="outproj_loss", grid=(S // tm,),
        in_specs=[row(2 * D), full((2 * D, D)), row(D), row(D), full((1, D)), row(D),
                  pl.BlockSpec((tm, D), lambda i: (i, OFF_G // D))],
        out_specs=[row(D), row(D), row(D), row(D), row(D), full((2 * D, D)), full((1, D)), full((1, LANE))],
        out_shape=[SDS((S, D), F32)] * 4 + [SDS((S, D), BF16), SDS((2 * D, D), F32), SDS((1, D), F32),
                                            SDS((1, LANE), F32)],
        compiler_params=_cp(("arbitrary",)),
    )(mix, w_out, x, tgt, nw, attn_pre, proj)


def _inproj_bwd_dx(srcs, dxbcdt, w_all, x, dy, nw, hosted=None):
    tm = 512
    nk = DP // D
    n_host, n_host_out = (len(hosted.arrays), len(hosted.out_shape)) if hosted else (0, 0)

    def body(*refs):
        src_refs = refs[:nk]
        w_ref, x_ref, dy_ref, nw_ref = refs[nk:nk + 4]
        host_in, refs = refs[nk + 4:nk + 4 + n_host], refs[nk + 4 + n_host:]
        gx_ref, dnw_ref = refs[:2]
        host_out, host_sems = refs[2:2 + n_host_out], refs[2 + n_host_out:]
        i = pl.program_id(0)

        @pl.when(i == 0)
        def _():
            if hosted:
                hosted.start(host_in, host_out, host_sems)
            dnw_ref[...] = jnp.zeros_like(dnw_ref)

        du = None
        for k, ref in enumerate(src_refs):
            width = min(D, 5 * D + X_COLS - k * D)
            part = _dot_nt(ref[:, :width], w_ref[:, k * D:k * D + width])
            du = part if du is None else du + part
        xf, nw_v = x_ref[...], nw_ref[...]
        r = lax.rsqrt(jnp.mean(xf * xf, axis=-1, keepdims=True) + EPS)
        xh = xf * r
        dnw_ref[...] += jnp.sum(du * xh, axis=0, keepdims=True)
        gdu = du * nw_v
        gx_ref[...] = r * (gdu - xh * jnp.mean(gdu * xh, axis=-1, keepdims=True)) + dy_ref[...]

        if hosted:
            pl.when(i == S // tm - 1)(lambda: hosted.finish(host_in, host_out, host_sems))

    row = pl.BlockSpec((tm, D), lambda i: (i, 0))
    row1 = pl.BlockSpec((tm, D), lambda i: (i, 1))
    one = pl.BlockSpec((1, D), lambda i: (0, 0))
    whole_w = pl.BlockSpec((D, DP), lambda i: (0, 0), pipeline_mode=pl.Buffered(1))
    args = [*srcs, dxbcdt, dxbcdt, w_all, x, dy, nw]
    in_specs = [row] * len(srcs) + [row, row1, whole_w, row, row, one]
    out_specs, out_shape, scratch = [row, one], [SDS((S, D), F32), SDS((1, D), F32)], []
    if hosted:
        args += hosted.arrays
        in_specs += [ANY] * n_host
        out_specs += [ANY] * n_host_out
        out_shape += hosted.out_shape
        scratch += hosted.scratch
    outs = pl.pallas_call(
        body, name="inproj_bwd_dx", grid=(S // tm,),
        in_specs=in_specs, out_specs=out_specs, out_shape=out_shape, scratch_shapes=scratch,
        compiler_params=_cp(("arbitrary",)),
    )(*args)
    return (outs[:2], outs[2:]) if hosted else outs


def _dw(u, dsec, name, width=D, hosted=None):
    ts = 1024
    n_host, n_host_out = (len(hosted.arrays), len(hosted.out_shape)) if hosted else (0, 0)

    def body(u_ref, d_ref, *refs):
        host_in, o_ref, refs = refs[:n_host], refs[n_host], refs[n_host + 1:]
        host_out, host_sems = refs[:n_host_out], refs[n_host_out:]
        i = pl.program_id(0)

        @pl.when(i == 0)
        def _():
            if hosted:
                hosted.start(host_in, host_out, host_sems)
            o_ref[...] = jnp.zeros_like(o_ref)

        o_ref[...] += _dot_tn(u_ref[...], d_ref[...])
        if hosted:
            pl.when(i == S // ts - 1)(lambda: hosted.finish(host_in, host_out, host_sems))

    outs = pl.pallas_call(
        body, name=name, grid=(S // ts,),
        in_specs=[pl.BlockSpec((ts, D), lambda i: (i, 0)), pl.BlockSpec((ts, width), lambda i: (i, 0))]
        + [ANY] * n_host,
        out_specs=[pl.BlockSpec((D, width), lambda i: (0, 0))] + [ANY] * n_host_out,
        out_shape=[SDS((D, width), F32)] + (hosted.out_shape if hosted else []),
        scratch_shapes=hosted.scratch if hosted else [],
        compiler_params=_cp(("arbitrary",)),
    )(u, dsec, *(hosted.arrays if hosted else []))
    return (outs[0], outs[1:]) if hosted else outs[0]


def _place():
    x, y, c = lax.axis_index("x"), lax.axis_index("y"), lax.axis_index("c")
    return x, y, c, 2 * x + y


def _chip_of(x, y, k):
    px = 1 - x if k & 2 else x
    py = 1 - y if k & 1 else y
    return px, py, 2 * px + py


def _remote(src, dst, send_sem, recv_sem, dev):
    return pltpu.make_async_remote_copy(src_ref=src, dst_ref=dst, send_sem=send_sem, recv_sem=recv_sem,
                                        device_id=dev, device_id_type=MESH)


def _gather_weights(w_in_b):
    half = w_in_b.shape[0] // 2
    quarter = half // 2

    def body(src, dst, send, recv):
        x, y, c, j = _place()
        me, sib = (x, y, c), (x, y, 1 - c)
        nbr = {"x": _chip_of(x, y, 2), "y": _chip_of(x, y, 1)}
        diag = _chip_of(x, y, 3)[2]
        started, arrivals = [], []

        def rows(n_quarter=None, sibling=False):
            base = (1 - c if sibling else c) * half
            return pl.ds(base, half) if n_quarter is None else pl.ds(base + n_quarter * quarter, quarter)

        def sem(n):
            return send.at[n], recv.at[n]

        def go(cp):
            cp.start()
            started.append(cp)

        own = _remote(src, dst.at[j], *sem(8), sib)
        go(own)
        for n, axis in enumerate("xy"):
            px, py, _ = nbr[axis]
            go(_remote(src.at[rows()], dst.at[j, rows()], *sem(n), (px, py, c)))
        for n, axis in enumerate("xy"):
            ox, oy, _ = nbr["y" if axis == "x" else "x"]
            pj = nbr[axis][2]
            _remote(src.at[rows()], dst.at[pj, rows()], *sem(n), me).wait_recv()
            go(_remote(dst.at[pj, rows(n)], dst.at[pj, rows(n)], *sem(2 + n), (ox, oy, c)))
            go(_remote(dst.at[pj, rows()], dst.at[pj, rows()], *sem(4 + n), sib))
            arrivals.append(_remote(src.at[rows()], dst.at[pj, rows(None, True)], *sem(4 + n), me))
        for n in range(2):
            _remote(dst.at[diag, rows(n)], dst.at[diag, rows(n)], *sem(2 + n), me).wait_recv()
            go(_remote(dst.at[diag, rows(n)], dst.at[diag, rows(n)], *sem(6 + n), sib))
            arrivals.append(_remote(dst.at[diag, rows(n, True)], dst.at[diag, rows(n, True)], *sem(6 + n), me))
        for cp in arrivals + [own]:
            cp.wait_recv()
        for cp in started:
            cp.wait_send()

    return pl.pallas_call(
        body, name="gather_weights", in_specs=[ANY], out_specs=ANY,
        out_shape=SDS((4,) + w_in_b.shape, BF16),
        scratch_shapes=[pltpu.SemaphoreType.DMA((9,)), pltpu.SemaphoreType.DMA((9,))],
        compiler_params=pltpu.CompilerParams(has_side_effects=True),
    )(w_in_b)


class _LateGather:
    def __init__(self, w_out_b, conv_w):
        self.arrays = [w_out_b, conv_w]
        self.out_shape = [SDS((4,) + w_out_b.shape, BF16), SDS((4,) + conv_w.shape, F32)]
        self.scratch = [pltpu.SemaphoreType.DMA((11,)), pltpu.SemaphoreType.DMA((11,))]

    def _plan(self, ins, outs, sems):
        x, y, c, j = _place()
        send, recv = sems
        (wo, cw), (gwo, gcw) = ins, outs
        half = wo.shape[0] // 2
        mine, theirs = pl.ds(c * half, half), pl.ds((1 - c) * half, half)
        me, sib = (x, y, c), (x, y, 1 - c)
        first, arrive, forward, last = [], [], [], []
        for k in (1, 2, 3):
            px, py, pj = _chip_of(x, y, k)
            first += [_remote(wo.at[mine], gwo.at[j, mine], send.at[k - 1], recv.at[k - 1], (px, py, c)),
                      _remote(cw, gcw.at[j], send.at[k + 2], recv.at[k + 2], (px, py, c))]
            arrive.append(_remote(wo.at[mine], gwo.at[pj, mine], send.at[k - 1], recv.at[k - 1], me))
            forward.append(_remote(gwo.at[pj, mine], gwo.at[pj, mine], send.at[k + 5], recv.at[k + 5], sib))
            last += [_remote(cw, gcw.at[pj], send.at[k + 2], recv.at[k + 2], me),
                     _remote(wo.at[theirs], gwo.at[pj, theirs], send.at[k + 5], recv.at[k + 5], me)]
        first += [_remote(wo, gwo.at[j], send.at[9], recv.at[9], sib),
                  _remote(cw, gcw.at[j], send.at[10], recv.at[10], sib)]
        last += first[-2:]
        return first, arrive, forward, last

    def start(self, ins, outs, sems):
        for cp in self._plan(ins, outs, sems)[0]:
            cp.start()

    def finish(self, ins, outs, sems):
        first, arrive, forward, last = self._plan(ins, outs, sems)
        for got, fwd in zip(arrive, forward):
            got.wait_recv()
            fwd.start()
        for cp in last:
            cp.wait_recv()
        for cp in first + forward:
            cp.wait_send()


def _window(s, names):
    lo, hi = TILES * s, TILES * s + TILES + 1
    pieces = []
    for n, name in enumerate(names):
        a, count = SECTION_TILES[name]
        first, last = max(lo, a), min(hi, a + count)
        if first < last:
            pieces.append((n, first - a, last - first, first - lo))
    assert sum(p[2] for p in pieces) == TILES + 1
    return pieces


class _PairExchange:
    def __init__(self, names, sections, shards, more=()):
        self.names, self.shards = names, shards
        self.there = [n for n, a in enumerate(sections) if a is not None]
        self.arrays = [sections[n] for n in self.there] + list(more)
        self.out_shape = [SDS((len(shards), D // 2, WIN), F32)]
        self.out_shape += [SDS((a.shape[0], a.shape[1] // 2, a.shape[2]), F32) for a in more]
        n = sum(p[0] in self.there for s in shards for p in _window(s, names)) + len(more)
        self.scratch = [pltpu.SemaphoreType.DMA((n,)) for _ in range(2)]

    def _copies(self, ins, outs, sems):
        x, y, c, _ = _place()
        sib = (x, y, 1 - c)
        rows = pl.ds((1 - c) * (D // 2), D // 2)
        k = 0
        for i, s in enumerate(self.shards):
            for n, tile, tiles, at in _window(s, self.names):
                if n in self.there:
                    yield _remote(ins[self.there.index(n)].at[rows, pl.ds(tile * LANE, tiles * LANE)],
                                  outs[0].at[i, :, pl.ds(at * LANE, tiles * LANE)], sems[0].at[k], sems[1].at[k], sib)
                    k += 1
        for src, dst in zip(ins[len(self.there):], outs[1:]):
            half = src.shape[1] // 2
            yield _remote(src.at[:, pl.ds((1 - c) * half, half)], dst, sems[0].at[k], sems[1].at[k], sib)
            k += 1

    def start(self, ins, outs, sems):
        for cp in self._copies(ins, outs, sems):
            cp.start()

    def finish(self, ins, outs, sems):
        for cp in self._copies(ins, outs, sems):
            cp.wait()


def _exchange_call(exchange, name, into=None):
    n, n_out = len(exchange.arrays), len(exchange.out_shape)
    given = list(into) if into else []

    def body(*refs):
        ins, outs, sems = refs[:n], refs[n + len(given):n + len(given) + n_out], refs[n + len(given) + n_out:]
        exchange.start(ins, outs, sems)
        exchange.finish(ins, outs, sems)

    return pl.pallas_call(
        body, name=name, in_specs=[ANY] * (n + len(given)), out_specs=[ANY] * n_out, out_shape=exchange.out_shape,
        input_output_aliases={n + k: k for k in range(len(given))},
        scratch_shapes=exchange.scratch, compiler_params=pltpu.CompilerParams(has_side_effects=True),
    )(*exchange.arrays, *given)


def _pair_sum_windows(cidx, names, sections, shards, r, name):
    n, half, _ = r.shape
    tr = min(half, 256)
    nt = half // tr

    def body(c_ref, *refs):
        del c_ref
        secs, r_ref, o_ref = refs[:-2], refs[-2], refs[-1]
        for i, s in enumerate(shards):
            for k, tile, tiles, at in _window(s, names):
                own = secs[k][:, tile * LANE:(tile + tiles) * LANE]
                there = slice(at * LANE, (at + tiles) * LANE)
                o_ref[i, :, there] = (own + r_ref[i, :, there]).astype(BF16)

    window = pl.BlockSpec((n, tr, WIN), lambda t, c: (0, t, 0))
    return pl.pallas_call(
        body, name=name,
        grid_spec=pltpu.PrefetchScalarGridSpec(
            num_scalar_prefetch=1, grid=(nt,),
            in_specs=[pl.BlockSpec((tr, a.shape[1]), lambda t, c: (c[0] * nt + t, 0)) for a in sections] + [window],
            out_specs=window),
        out_shape=SDS(r.shape, BF16),
        compiler_params=_cp(("parallel",)),
    )(cidx, *sections, r)


def _pair_sum(cidx, g, r, name):
    n, half, width = r.shape
    tr = min(half, 256)
    nt = half // tr

    def body(c_ref, g_ref, r_ref, o_ref):
        del c_ref
        o_ref[...] = (g_ref[...] + r_ref[...]).astype(BF16)

    return pl.pallas_call(
        body, name=name,
        grid_spec=pltpu.PrefetchScalarGridSpec(
            num_scalar_prefetch=1, grid=(n, nt),
            in_specs=[pl.BlockSpec((None, tr, width), lambda s, t, c: (s, c[0] * nt + t, 0)),
                      pl.BlockSpec((None, tr, width), lambda s, t, c: (s, t, 0))],
            out_specs=pl.BlockSpec((None, tr, width), lambda s, t, c: (s, t, 0))),
        out_shape=SDS(r.shape, BF16),
        compiler_params=_cp(("parallel", "parallel")),
    )(cidx, g, r)


class _ChipExchange:
    def __init__(self, arrays, rows):
        self.arrays, self.rows = list(arrays), list(rows)
        self.out_shape = [SDS((4,) + a.shape[1:], BF16) for a in self.arrays]
        self.scratch = [pltpu.SemaphoreType.DMA((3 * len(self.arrays),)) for _ in range(2)]

    def _copies(self, ins, outs, sems):
        x, y, c, j = _place()
        send, recv = sems
        for a, (src, dst, row) in enumerate(zip(ins, outs, self.rows)):
            for k in (1, 2, 3):
                px, py, pj = _chip_of(x, y, k)
                n = 3 * a + k - 1
                slot = pj if row is None else py
                yield (None if row is None else px == row, None if row is None else x == row,
                       _remote(src.at[slot], dst.at[j], send.at[n], recv.at[n], (px, py, c)),
                       _remote(src.at[0], dst.at[pj], send.at[n], recv.at[n], (x, y, c)))

    def start(self, ins, outs, sems):
        for sends, _, send, _ in self._copies(ins, outs, sems):
            if sends is None:
                send.start()
            else:
                pl.when(sends)(send.start)

    def finish(self, ins, outs, sems):
        for sends, owns, send, arrival in self._copies(ins, outs, sems):
            if sends is None:
                arrival.wait_recv()
                send.wait_send()
            else:
                pl.when(owns)(arrival.wait_recv)
                pl.when(sends)(send.wait_send)


def _all_gather_rows(src, dst, rows, send, recv, local_sem):
    x, y, c, j = _place()
    me = 2 * j + c
    local = pltpu.make_async_copy(src, dst.at[me, rows], local_sem)
    cps, arrivals = [], []
    for k in range(1, 8):
        px, py, pj = _chip_of(x, y, k >> 1)
        pc = 1 - c if k & 1 else c
        cps.append(_remote(src, dst.at[me, rows], send.at[k - 1], recv.at[k - 1], (px, py, pc)))
        arrivals.append(_remote(src, dst.at[2 * pj + pc, rows], send.at[k - 1], recv.at[k - 1], (x, y, c)))
    starts = [local.start] + [cp.start for cp in cps]
    waits = [cp.wait_recv for cp in arrivals] + [cp.wait_send for cp in cps] + [local.wait]
    return starts, waits


class _SmallExchange:
    def __init__(self, small):
        self.arrays = [small]
        self.out_shape = [SDS((8,) + small.shape, F32)]
        self.scratch = [pltpu.SemaphoreType.DMA((7,)), pltpu.SemaphoreType.DMA((7,)), pltpu.SemaphoreType.DMA]

    def start(self, ins, outs, sems):
        for go in _all_gather_rows(ins[0], outs[0], slice(None), *sems)[0]:
            go()

    def finish(self, ins, outs, sems):
        for wait in _all_gather_rows(ins[0], outs[0], slice(None), *sems)[1]:
            wait()


class _Both:
    def __init__(self, a, b):
        self.parts = (a, b)
        self.arrays, self.out_shape, self.scratch = a.arrays + b.arrays, a.out_shape + b.out_shape, a.scratch + b.scratch

    def _split(self, ins, outs, sems):
        a, b = self.parts
        return ((a, ins[:len(a.arrays)], outs[:len(a.out_shape)], sems[:len(a.scratch)]),
                (b, ins[len(a.arrays):], outs[len(a.out_shape):], sems[len(a.scratch):]))

    def start(self, ins, outs, sems):
        for part, *refs in self._split(ins, outs, sems):
            part.start(*refs)

    def finish(self, ins, outs, sems):
        for part, *refs in self._split(ins, outs, sems):
            part.finish(*refs)


def _slot_sum(r, name):
    n, rows, width = r.shape
    tr = min(rows, 256)

    def body(r_ref, o_ref):
        acc = r_ref[0].astype(F32)
        for s in range(1, n):
            acc = acc + r_ref[s].astype(F32)
        o_ref[...] = acc

    return pl.pallas_call(
        body, name=name, grid=(rows // tr,),
        in_specs=[pl.BlockSpec((n, tr, width), lambda t: (0, t, 0))],
        out_specs=pl.BlockSpec((tr, width), lambda t: (t, 0)),
        out_shape=SDS((rows, width), F32),
        compiler_params=_cp(("parallel",)),
    )(r)


def _chip_sum(where, recv, own, name):
    n, rows, width = recv.shape
    tr = min(rows, 256)
    nt = rows // tr

    def body(j_ref, r_ref, own_ref, o_ref):
        acc = None
        for s in range(n):
            term = jnp.where(j_ref[0] == s, own_ref[...], r_ref[s]).astype(F32)
            acc = term if acc is None else acc + term
        o_ref[...] = acc

    return pl.pallas_call(
        body, name=name,
        grid_spec=pltpu.PrefetchScalarGridSpec(
            num_scalar_prefetch=1, grid=(nt,),
            in_specs=[pl.BlockSpec((n, tr, width), lambda t, j: (0, t, 0)),
                      pl.BlockSpec((None, tr, width), lambda t, j: (j[0], t, 0))],
            out_specs=pl.BlockSpec((tr, width), lambda t, j: (j[1] * nt + t, 0))),
        out_shape=SDS((2 * rows, width), F32),
        compiler_params=_cp(("parallel",)),
    )(where, recv, own)


def _chip_sum_rows(place, recv0, own0, recv1, own1, name):
    n, rows, width = recv0.shape
    tr = min(rows, 256)
    nt = rows // tr

    def body(p_ref, r0_ref, o0_ref, r1_ref, o1_ref, o_ref):
        first_row = p_ref[2] == 0
        own = jnp.where(first_row, o0_ref[...], o1_ref[...])
        acc = None
        for s in range(n):
            term = jnp.where(p_ref[0] == s, own, jnp.where(first_row, r0_ref[s], r1_ref[s])).astype(F32)
            acc = term if acc is None else acc + term
        o_ref[...] = acc

    recv = pl.BlockSpec((n, tr, width), lambda t, p: (0, t, 0))
    own = pl.BlockSpec((None, tr, width), lambda t, p: (p[3], t, 0))
    return pl.pallas_call(
        body, name=name,
        grid_spec=pltpu.PrefetchScalarGridSpec(
            num_scalar_prefetch=1, grid=(nt,), in_specs=[recv, own, recv, own],
            out_specs=pl.BlockSpec((tr, width), lambda t, p: (p[1] * nt + t, 0))),
        out_shape=SDS((2 * rows, width), F32),
        compiler_params=_cp(("parallel",)),
    )(place, recv0, own0, recv1, own1)


def _half_exchange(gw, go, gathered, late, row):
    def body(gw_in, go_in, ga_in, late_ref, gw_ref, go_ref, ga_ref, send, recv, late_send, late_recv, late_local):
        del gw_in, go_in, ga_in
        x, y, c, _ = _place()
        starts, waits = _all_gather_rows(late_ref, ga_ref, pl.ds(row, late.shape[0]), late_send, late_recv,
                                         late_local)
        for go_ in starts:
            go_()
        mine = [pl.ds(c * (r.shape[0] // 2), r.shape[0] // 2) for r in (gw_ref, go_ref)]
        cps = [_remote(r.at[rows], r.at[rows], send.at[k], recv.at[k], (x, y, 1 - c))
               for k, (r, rows) in enumerate(zip((gw_ref, go_ref), mine))]
        for cp in cps:
            cp.start()
        for k, r in enumerate((gw_ref, go_ref)):
            theirs = pl.ds((1 - c) * (r.shape[0] // 2), r.shape[0] // 2)
            _remote(r.at[theirs], r.at[theirs], send.at[k], recv.at[k], (x, y, c)).wait_recv()
        for cp in cps:
            cp.wait_send()
        for wait in waits:
            wait()

    return pl.pallas_call(
        body, name="half_exchange", in_specs=[ANY] * 4, out_specs=[ANY] * 3,
        out_shape=[SDS(gw.shape, F32), SDS(go.shape, F32), SDS(gathered.shape, F32)],
        input_output_aliases={0: 0, 1: 1, 2: 2},
        scratch_shapes=[pltpu.SemaphoreType.DMA((2,)), pltpu.SemaphoreType.DMA((2,)),
                        pltpu.SemaphoreType.DMA((7,)), pltpu.SemaphoreType.DMA((7,)), pltpu.SemaphoreType.DMA],
        compiler_params=pltpu.CompilerParams(has_side_effects=True),
    )(gw, go, gathered, late)


def _adamw(w, g, m, v, name):
    rows, width = w.shape
    tr = min(rows, 256)

    def body(w_ref, g_ref, m_ref, v_ref, d_ref, nm_ref, nv_ref):
        gv = g_ref[...]
        nm = ADAM_B1 * m_ref[...] + (1.0 - ADAM_B1) * gv
        nv = ADAM_B2 * v_ref[...] + (1.0 - ADAM_B2) * (gv * gv)
        m_hat = nm / (1.0 - ADAM_B1 ** ADAM_STEP)
        v_hat = nv / (1.0 - ADAM_B2 ** ADAM_STEP)
        d_ref[...] = -ADAM_LR * (m_hat / (jnp.sqrt(v_hat) + ADAM_EPS) + ADAM_WD * w_ref[...])
        nm_ref[...] = nm
        nv_ref[...] = nv

    t = pl.BlockSpec((tr, width), lambda i: (i, 0))
    return pl.pallas_call(
        body, name=name, grid=(rows // tr,), in_specs=[t] * 4, out_specs=[t] * 3,
        out_shape=[SDS(w.shape, F32)] * 3, compiler_params=_cp(("parallel",)),
    )(w, g, m, v)


def _rowwise(a):
    return jnp.transpose(a, (2, 0, 1)).reshape(SHARD * D // LANE, LANE)


def _columns(ref):
    return jnp.concatenate([ref[pl.ds(c, LANE, stride=8), :].T for c in range(D // LANE)], axis=0)


def _shard_bf16(chip, w_rows):
    def body(j_ref, w_ref, o_ref, prev_ref):
        t = pl.program_id(0)
        cur = _columns(w_ref)

        @pl.when(t == 0)
        def _():
            prev_ref[...] = jnp.zeros_like(prev_ref)

        lane = _iota((D, LANE), 1)
        for s in range(4):
            @pl.when(j_ref[0] == s)
            def _():
                off = SHIFT * s
                moved = cur if s == 0 else jnp.where(lane < off, pltpu.roll(prev_ref[...], off, 1),
                                                     pltpu.roll(cur, off, 1))
                col = t * LANE + lane - off
                o_ref[...] = jnp.where((col >= 0) & (col < SHARD), moved, 0.0).astype(BF16)
        prev_ref[...] = cur

    return pl.pallas_call(
        body, name="shard_bf16",
        grid_spec=pltpu.PrefetchScalarGridSpec(
            num_scalar_prefetch=1, grid=(TILES + 1,),
            in_specs=[pl.BlockSpec((D, LANE), lambda t, j: (t, 0))],
            out_specs=pl.BlockSpec((D, LANE), lambda t, j: (0, t)),
            scratch_shapes=[pltpu.VMEM((D, LANE), F32)]),
        out_shape=SDS((D, WIN), BF16), compiler_params=_cp(("arbitrary",)),
    )(chip, w_rows)


def _whole_w_in(windows):
    tr = 256
    n = windows.shape[0]

    def body(g_ref, o_ref):
        lane = _iota((tr, LANE), 1)
        for s in range(n):
            first = TILES * s
            head = g_ref[s, :, :LANE]
            if s:
                tail = g_ref[s - 1, :, TILES * LANE:]
                head = jnp.where(lane < SHIFT * s, tail.astype(F32), head.astype(F32)).astype(BF16)
            o_ref[:, first * LANE:(first + 1) * LANE] = head
            o_ref[:, (first + 1) * LANE:(first + TILES) * LANE] = g_ref[s, :, LANE:TILES * LANE]
        o_ref[:, n * TILES * LANE:(n * TILES + 1) * LANE] = g_ref[n - 1, :, TILES * LANE:]
        o_ref[:, (n * TILES + 1) * LANE:] = jnp.zeros((tr, DP - (n * TILES + 1) * LANE), BF16)

    return pl.pallas_call(
        body, name="whole_w_in", grid=(D // tr,),
        in_specs=[pl.BlockSpec((n, tr, WIN), lambda t: (0, t, 0))], out_specs=pl.BlockSpec((tr, DP), lambda t: (t, 0)),
        out_shape=SDS((D, DP), BF16), compiler_params=_cp(("parallel",)),
    )(windows)


def _own_buffer(a, name):
    tr = 512
    block = pl.BlockSpec((tr, a.shape[1]), lambda t: (t, 0))

    def body(a_ref, o_ref):
        o_ref[...] = a_ref[...]

    return pl.pallas_call(
        body, name=name, grid=(a.shape[0] // tr,), in_specs=[block], out_specs=block,
        out_shape=SDS(a.shape, a.dtype), compiler_params=_cp(("parallel",)),
    )(a)


def _adamw_in(chip, w_rows, g_win, m_rows, v_rows):
    def body(j_ref, w_ref, g_ref, next_ref, m_ref, v_ref, grad_ref, d_ref, nm_ref, nv_ref):
        columns = _columns
        for s in range(4):
            @pl.when(j_ref[0] == s)
            def _():
                if s == 0:
                    grad_ref[...] = g_ref[...]
                else:
                    back = LANE - SHIFT * s
                    grad_ref[...] = jnp.where(_iota((D, LANE), 1) < back, pltpu.roll(g_ref[...], back, 1),
                                              pltpu.roll(next_ref[...], back, 1))
        gv = grad_ref[...]
        nm = ADAM_B1 * columns(m_ref) + (1.0 - ADAM_B1) * gv
        nv = ADAM_B2 * columns(v_ref) + (1.0 - ADAM_B2) * (gv * gv)
        m_hat = nm / (1.0 - ADAM_B1 ** ADAM_STEP)
        v_hat = nv / (1.0 - ADAM_B2 ** ADAM_STEP)
        d_ref[...] = -ADAM_LR * (m_hat / (jnp.sqrt(v_hat) + ADAM_EPS) + ADAM_WD * columns(w_ref))
        nm_ref[...] = nm
        nv_ref[...] = nv

    tile = pl.BlockSpec((D, LANE), lambda t, j: (0, t))
    next_tile = pl.BlockSpec((D, LANE), lambda t, j: (0, jnp.minimum(t + 1, TILES)))
    rows = pl.BlockSpec((D, LANE), lambda t, j: (t, 0))
    return pl.pallas_call(
        body, name="adamw_in",
        grid_spec=pltpu.PrefetchScalarGridSpec(
            num_scalar_prefetch=1, grid=(TILES + 1,), in_specs=[rows, tile, next_tile, rows, rows],
            out_specs=[tile] * 4),
        out_shape=[SDS((D, SHARD), F32)] * 4, compiler_params=_cp(("parallel",)),
    )(chip, w_rows, g_win, g_win, m_rows, v_rows)


def _rows128(a, rows):
    flat = a.reshape(-1)
    return jnp.pad(flat, (0, rows * LANE - flat.shape[0])).reshape(rows, LANE)


CONV_ROWS = 48


def _pack_small(conv_w, norm_pre, conv_b, ssm_norm, norm_post, dtb, alog, dsk, extra=None):
    cw_rows = CONV_ROWS if conv_w.shape[-1] == 1536 else 16
    extra = jnp.zeros((1, LANE), F32) if extra is None else _rows128(extra, 1)
    vec = jnp.concatenate([_rows128(dtb, 1), _rows128(alog, 1), _rows128(dsk, 1), extra, jnp.zeros((4, LANE), F32)],
                          axis=0)
    return jnp.concatenate([_rows128(conv_w, cw_rows), _rows128(norm_pre, 8), _rows128(conv_b, 16),
                            _rows128(ssm_norm, 8), _rows128(norm_post, 8), vec], axis=0)


def _unpack_small(p, cw_cols):
    cw_rows = CONV_ROWS if cw_cols == 1536 else 16
    o = cw_rows
    conv_w = p[:cw_rows].reshape(-1)[:4 * cw_cols].reshape(1, 4, cw_cols)
    norm_pre = p[o:o + 8].reshape(1, D)
    conv_b = p[o + 8:o + 24].reshape(-1)[:1536].reshape(1, 1536)
    ssm_norm = p[o + 24:o + 32].reshape(1, D)
    norm_post = p[o + 32:o + 40].reshape(1, D)
    vec = p[o + 40:o + 48]
    return conv_w, norm_pre, conv_b, ssm_norm, norm_post, vec[0:1, :NH], vec[1:2, :NH], vec[2:3, :NH], vec[3, 0]


def _pad_lanes(a):
    return jnp.pad(a, ((0, 0), (0, LANE - a.shape[1])))


class _GradReduce:
    LO, HI = ("q", "k", "v", "g"), ("g", "z", "x")

    def __init__(self, xi, yi, ci):
        self.cidx = jnp.reshape(ci, (1,)).astype(jnp.int32)
        self.place = jnp.stack([2 * xi + yi, ci, xi, yi]).astype(jnp.int32)

    def pairs(self, dw_g, dw_z, dw_x, dw_out):
        self.hi = [dw_g, dw_z, dw_x]
        self.go = dw_out.reshape(4, D // 2, D)
        return _PairExchange(self.HI, self.hi, (2, 3), [self.go])

    def first(self, got):
        rw, ro = got
        self.pw_hi = _pair_sum_windows(self.cidx, self.HI, self.hi, (2, 3), rw, "pair_sum_hi")
        self.po = _pair_sum(self.cidx, self.go, ro, "pair_sum_out")
        return _ChipExchange([self.pw_hi, self.po], [1, None])

    def first_done(self, got):
        self.rw_hi, self.ro = got

    def second_pairs(self, dw_q, dw_k, dw_g):
        self.lo = [dw_q, dw_k, None, dw_g]
        return _PairExchange(self.LO, self.lo, (0, 1))

    def second(self, dw_v, got, small):
        rest = _PairExchange(self.LO, [None, None, dw_v, None], (0, 1))
        (rw,) = _exchange_call(rest, "pair_exchange_v", into=got)
        lo = [dw_v if a is None else a for a in self.lo]
        self.pw_lo = _pair_sum_windows(self.cidx, self.LO, lo, (0, 1), rw, "pair_sum_lo")
        return _Both(_ChipExchange([self.pw_lo], [0]), _SmallExchange(small))

    def second_done(self, got):
        self.rw_lo, self.small = got

    def result(self, late, row):
        half_in = _chip_sum_rows(self.place, self.rw_lo, self.pw_lo, self.rw_hi, self.pw_hi, "chip_sum_in")
        half_out = _chip_sum(self.place[0:2], self.ro, self.po, "chip_sum_out")
        return _half_exchange(half_in, half_out, self.small, late, row)


def kernel(x, norm_pre_w, w_in, conv_w, conv_b, dt_bias, a_log, d_skip, ssm_norm_w, w_out, norm_post_w, loss_target, m_norm_pre_w, m_w_in, m_conv_w, m_conv_b, m_dt_bias, m_a_log, m_d_skip, m_ssm_norm_w, m_w_out, m_norm_post_w, v_norm_pre_w, v_w_in, v_conv_w, v_conv_b, v_dt_bias, v_a_log, v_d_skip, v_ssm_norm_w, v_w_out, v_norm_post_w):
    xi, yi, ci = lax.axis_index("x"), lax.axis_index("y"), lax.axis_index("c")
    chip = 2 * xi + yi
    x2, tgt = x[0], loss_target[0]

    chip_idx = jnp.reshape(chip, (1,)).astype(jnp.int32)
    w_rows = _rowwise(w_in)
    w_all = _whole_w_in(_gather_weights(_shard_bf16(chip_idx, w_rows)))
    reduce = _GradReduce(xi, yi, ci)
    grad_x, dnw_pre = _local_step(x2, tgt, w_all, _LateGather(w_out[0].astype(BF16), conv_w[0]), norm_pre_w, conv_b,
                                  dt_bias, a_log, d_skip, ssm_norm_w, norm_post_w, reduce)
    g_win, g_out, small = reduce.result(_rows128(dnw_pre, D // LANE), CONV_ROWS)
    g_small = _slot_sum(small, "small_sum")
    g_cw, g_npre, g_cb, g_nssm, g_npost, g_dtb, g_alog, g_dsk, loss = _unpack_small(g_small, 1536)
    g_cw = lax.dynamic_slice_in_dim(g_cw, chip * 384, 384, axis=2)

    g_in, d_in, nm_in, nv_in = _adamw_in(chip_idx, w_rows, g_win, _rowwise(m_w_in), _rowwise(v_w_in))
    grad_x = _own_buffer(grad_x, "grad_x_copy")
    d_out, nm_out, nv_out = _adamw(w_out[0], g_out, m_w_out[0], v_w_out[0], "adamw_out")
    packed = [_pack_small(*t) for t in (
        (conv_w, norm_pre_w, conv_b, ssm_norm_w, norm_post_w, dt_bias, a_log, d_skip),
        (g_cw, g_npre, g_cb, g_nssm, g_npost, g_dtb, g_alog, g_dsk),
        (m_conv_w, m_norm_pre_w, m_conv_b, m_ssm_norm_w, m_norm_post_w, m_dt_bias, m_a_log, m_d_skip),
        (v_conv_w, v_norm_pre_w, v_conv_b, v_ssm_norm_w, v_norm_post_w, v_dt_bias, v_a_log, v_d_skip))]
    small_out = [_unpack_small(p, 384)[:8] for p in _adamw(*packed, "adamw_small")]

    def ordered(cw_, npre, cb_, nssm, npost, dtb_, alog_, dsk_, big_in, big_out):
        return [npre, big_in[None], cw_, cb_, dtb_, alog_, dsk_, nssm, big_out[None], npost]

    grads = ordered(g_cw, g_npre, g_cb, g_nssm, g_npost, g_dtb, g_alog, g_dsk, g_in, g_out)
    deltas = ordered(*small_out[0], d_in, d_out)
    new_m = ordered(*small_out[1], nm_in, nm_out)
    new_v = ordered(*small_out[2], nv_in, nv_out)
    return (loss, grad_x[None], *grads, *deltas, *new_m, *new_v)


def _local_step(x2, tgt, w_all, late, norm_pre_w, conv_b, dt_bias, a_log, d_skip, ssm_norm_w,
                norm_post_w, reduce=None):
    dtb, alog = _pad_lanes(dt_bias), _pad_lanes(a_log)
    d_b = jnp.repeat(d_skip, 64, axis=1)

    if isinstance(late, _LateGather):
        (proj, u), (gout, gcw) = _inproj_fwd(x2, norm_pre_w, w_all, late)
        w_out_all = gout.reshape(2 * D, D)
        cw_all = jnp.concatenate([gcw[0], gcw[1], gcw[2], gcw[3]], axis=1)
    else:
        proj, u = _inproj_fwd(x2, norm_pre_w, w_all)
        w_out_all, cw_all = late
    mix, attn_pre, lse = _attn_fwd(proj, 1, _attn_fwd(proj, 4, _attn_fwd(proj, 16)), final=True)
    mix, y_save, states, conv_out = _ssm_fwd(proj, mix, cw_all, conv_b, dtb, alog, d_b, ssm_norm_w)

    dy, dn_ssm, do, delta, dg, dw_out, dnw_post, loss_part = _outproj_loss(mix, w_out_all, x2, tgt, norm_post_w,
                                                                          attn_pre, proj)
    dz, dxbcdt, dcw, dcb, dvec, dnw_ssm = _ssm_bwd(proj, dn_ssm, y_save, states, conv_out, cw_all, dtb, alog, d_b,
                                                   ssm_norm_w)
    dw_g, dw_z, dw_x = _dw(u, dg, "dw_in_g"), _dw(u, dz, "dw_in_z"), _dw(u, dxbcdt, "dw_in_xbcdt", X_COLS)
    acc = _attn_bwd(proj, do, lse, delta, 16, None, F32, reduce.pairs(dw_g, dw_z, dw_x, dw_out) if reduce else None)
    if reduce:
        acc, got = acc
    acc = _attn_bwd(proj, do, lse, delta, 4, acc, F32, reduce.first(got) if reduce else None)
    if reduce:
        acc, got = acc
        reduce.first_done(got)
    dq, dk, dv = _attn_bwd(proj, do, lse, delta, 1, acc, BF16)
    dw_q, dw_k = _dw(u, dq, "dw_in_q"), _dw(u, dk, "dw_in_k")
    dw_v = _dw(u, dv, "dw_in_v", hosted=reduce.second_pairs(dw_q, dw_k, dw_g) if reduce else None)
    if reduce:
        dw_v, got = dw_v

    def small(dnw_pre):
        return _pack_small(dcw, dnw_pre, dcb, dnw_ssm, dnw_post, dvec[0:1, :NH], dvec[1:2, :NH], dvec[2:3, :NH],
                           loss_part[:, :1])

    res = _inproj_bwd_dx([dq, dk, dv, dg, dz], dxbcdt, w_all, x2, dy, norm_pre_w,
                         reduce.second(dw_v, got, small(jnp.zeros((1, D), F32))) if reduce else None)
    if reduce:
        res, got = res
        reduce.second_done(got)
        return res
    grad_x, dnw_pre = res
    dw_all = jnp.concatenate([dw_q, dw_k, dw_v, dw_g, dw_z, dw_x], axis=1)
    return grad_x, small(dnw_pre), dw_all, dw_out
```

```python
import functools

import jax
import jax.numpy as jnp
from jax import lax
from jax.experimental import pallas as pl
from jax.experimental.pallas import tpu as pltpu

F32 = jnp.float32
BF16 = jnp.bfloat16
MESH = pl.DeviceIdType.MESH
SDS = jax.ShapeDtypeStruct
ANY = pl.BlockSpec(memory_space=pl.ANY)

S = 4096
D = 1024
DP = 7168
SHARD = 1668
OFF_G, OFF_Z = 3072, 4096
NH = 16
CH = 128
NC = S // CH
EPS = 1e-6
NEG = -1e30
LANE = 128
VMEM_LIMIT = 48 * 1024 * 1024

TILES = SHARD // LANE
WIN = (TILES + 1) * LANE
SHIFT = SHARD - TILES * LANE
SECTION_TILES = {"q": (0, 8), "k": (8, 8), "v": (16, 8), "g": (24, 8), "z": (32, 8), "x": (40, 13)}
X_COLS = SECTION_TILES["x"][1] * LANE

ADAM_LR, ADAM_B1, ADAM_B2, ADAM_EPS, ADAM_WD, ADAM_STEP = 0.001, 0.9, 0.999, 1e-08, 0.01, 10


def _cp(sem, **kw):
    return pltpu.CompilerParams(dimension_semantics=sem, vmem_limit_bytes=VMEM_LIMIT, **kw)


def _dot(a, b):
    return jnp.dot(a, b, preferred_element_type=F32)


def _dot_nt(a, b):
    return lax.dot_general(a, b, (((1,), (1,)), ((), ())), preferred_element_type=F32)


def _dot_tn(a, b):
    return lax.dot_general(a, b, (((0,), (0,)), ((), ())), preferred_element_type=F32)


def _pieces(x, n):
    out = []
    for _ in range(n):
        p = x.astype(BF16)
        out.append(p)
        x = x - p.astype(F32)
    return out


def _pick(x, sel, n=2):
    parts = [_dot(p, sel) for p in _pieces(x, n)]
    return functools.reduce(jnp.add, parts)


def _pick_left(sel, x, n=3):
    parts = [_dot(sel, p) for p in _pieces(x, n)]
    return functools.reduce(jnp.add, parts)


def _sigmoid(v):
    return 0.5 * jnp.tanh(0.5 * v) + 0.5


def _iota(shape, dim):
    return lax.broadcasted_iota(jnp.int32, shape, dim)


def _inproj_fwd(x, nw, w_all, hosted=None):
    tm, tn = 1024, 1024
    n_host = len(hosted.arrays) if hosted else 0

    def body(x_ref, nw_ref, w_ref, *refs):
        host_in, (proj_ref, u_ref), refs = refs[:n_host], refs[n_host:n_host + 2], refs[n_host + 2:]
        host_out, host_sems = refs[:n_host], refs[n_host:]
        i, j = pl.program_id(0), pl.program_id(1)
        if hosted:
            pl.when((i == 0) & (j == 0))(lambda: hosted.start(host_in, host_out, host_sems))

        @pl.when(j == 0)
        def _():
            xf = x_ref[...]
            r = lax.rsqrt(jnp.mean(xf * xf, axis=-1, keepdims=True) + EPS)
            u_ref[...] = (xf * r * nw_ref[...]).astype(BF16)

        proj_ref[...] = _dot(u_ref[...], w_ref[...])
        if hosted:
            pl.when((i == S // tm // 2) & (j == 0))(lambda: hosted.pass_on(host_in, host_out, host_sems))
            pl.when((i == S // tm - 1) & (j == DP // tn - 1))(lambda: hosted.finish(host_in, host_out, host_sems))

    outs = pl.pallas_call(
        body, name="inproj_fwd", grid=(S // tm, DP // tn),
        in_specs=[pl.BlockSpec((tm, D), lambda i, j: (i, 0)), pl.BlockSpec((1, D), lambda i, j: (0, 0)),
                  pl.BlockSpec((D, tn), lambda i, j: (0, j))] + [ANY] * n_host,
        out_specs=[pl.BlockSpec((tm, tn), lambda i, j: (i, j)), pl.BlockSpec((tm, D), lambda i, j: (i, 0))]
        + [ANY] * n_host,
        out_shape=[SDS((S, DP), F32), SDS((S, D), BF16)] + (hosted.out_shape if hosted else []),
        scratch_shapes=hosted.scratch if hosted else [],
        compiler_params=_cp(("arbitrary", "arbitrary") if hosted else ("parallel", "arbitrary")),
    )(x, nw, w_all, *(hosted.arrays if hosted else []))
    return (outs[:2], outs[2:]) if hosted else outs


ATTN_QB = {1: 16, 4: 4, 16: 1}


def _unit_rows(r, u, d):
    return pl.ds(r + d * CH * u, CH, stride=d) if d > 1 else pl.ds(CH * u, CH)


def _for_units(d, qb, fn):
    for r in range(d):
        for u in range(qb):
            fn(r, u)


def _attn_mask(has_prev):
    qi, kj = _iota((2 * CH, 2 * CH), 0) & (CH - 1), _iota((2 * CH, 2 * CH), 1)
    cur_ok = (kj >= CH) & (kj - CH <= qi)
    prev_ok = (kj < CH) & (kj >= qi)
    return cur_ok | (prev_ok & has_prev)


def _stack_heads(v, lane_a):
    return jnp.concatenate([jnp.where(lane_a, v, 0.0), jnp.where(lane_a, 0.0, v)], axis=0).astype(BF16)


def _attn_specs(d, qb):
    rows, prows = CH * d * qb, CH * d
    nb = S // rows
    steps = (NH // 2) * nb

    def at(t):
        t = jnp.minimum(t, steps - 1)
        return t % nb, t // nb

    def cur(off):
        return pl.BlockSpec((rows, LANE), lambda t: (at(t)[0], off + at(t)[1]))

    def prev(off):
        return pl.BlockSpec((prows, LANE), lambda t: (jnp.maximum(at(t)[0] * qb - 1, 0), off + at(t)[1]))

    lag = pl.BlockSpec((rows, LANE), lambda t: at(jnp.maximum(t - 1, 0)))
    return nb, steps, cur, prev, lag


def _gather16(src_ref, dense_ref, tmp_ref):
    for a in range(4):
        tmp_ref[...] = src_ref[pl.ds(a, 4 * CH, stride=4), :]
        for b in range(4):
            dense_ref[a + 4 * b] = tmp_ref[pl.ds(b, CH, stride=4), :]


def _scatter16(dense_ref, dst_ref, tmp_ref):
    for a in range(4):
        for b in range(4):
            tmp_ref[pl.ds(b, CH, stride=4), :] = dense_ref[a + 4 * b]
        dst_ref[pl.ds(a, 4 * CH, stride=4), :] = tmp_ref[...]


def _unit_index(r, u, d):
    return (r,) if d == 16 else (_unit_rows(r, u, d), slice(None))


def _unit_kv(p_ref, c_ref, r, u, d):
    prev = p_ref[_unit_index(r, 0, d)] if u == 0 else c_ref[_unit_index(r, u - 1, d)]
    return jnp.concatenate([prev, c_ref[_unit_index(r, u, d)]], axis=0).astype(BF16)


def _dense_scratch(d, n):
    return [pltpu.VMEM((16, CH, LANE), F32)] * n + [pltpu.VMEM((4 * CH, LANE), F32)] if d == 16 else []


def _attn_fwd(proj, d, prior=None, final=False):
    qb = ATTN_QB[d]
    nb, steps, cur, prev, _ = _attn_specs(d, qb)
    n_prior = 2 if prior is not None else 0
    n_in, n_out = 5 + n_prior + final, 2 + final
    assert not (d == 16 and (n_prior or final))

    def body(*refs):
        ins, outs, scratch = refs[:n_in], refs[n_in:n_in + n_out], refs[n_in + n_out:]
        if d == 16:
            tmp_ref = scratch[-1]
            for src, dense in zip(ins, scratch):
                _gather16(src, dense, tmp_ref)
            block_outs, ins, outs = outs, scratch[:n_in], scratch[n_in:n_in + n_out]
        q_ref, kp_ref, kc_ref, vp_ref, vc_ref = ins[:5]
        prior_refs = ins[5:5 + n_prior]
        if final:
            g_ref, (mix_ref, o_ref, l_ref) = ins[-1], outs
        else:
            o_ref, l_ref = outs
        i = pl.program_id(0) % nb
        lane_a = _iota((CH, LANE), 1) < 64
        mask_first, mask_rest = _attn_mask(i > 0), _attn_mask(True)

        def unit(r, u):
            at = _unit_index(r, u, d)
            q2 = _stack_heads(q_ref[at] * 0.125, lane_a)
            k2, v2 = _unit_kv(kp_ref, kc_ref, r, u, d), _unit_kv(vp_ref, vc_ref, r, u, d)
            s = jnp.where(mask_first if u == 0 else mask_rest, _dot_nt(q2, k2), NEG)
            m = jnp.max(s, axis=1, keepdims=True)
            p = jnp.exp(s - m)
            l = jnp.sum(p, axis=1, keepdims=True)
            o2 = _dot(p.astype(BF16), v2) / l
            lse2 = m + jnp.log(l)
            o = jnp.where(lane_a, o2[:CH], o2[CH:])
            lse = jnp.where(lane_a, lse2[:CH], lse2[CH:])
            if n_prior:
                o_a, l_a = prior_refs[0][at], prior_refs[1][at]
                top = jnp.maximum(l_a, lse)
                e_a, e_b = jnp.exp(l_a - top), jnp.exp(lse - top)
                tot = e_a + e_b
                o = (e_a * o_a + e_b * o) / tot
                lse = top + jnp.log(tot)
            o_ref[at] = o
            l_ref[at] = lse
            if final:
                g = g_ref[at]
                mix_ref[at] = (o * (g * _sigmoid(g))).astype(BF16)

        _for_units(d, qb, unit)
        if d == 16:
            for dense, dst in zip(outs, block_outs):
                _scatter16(dense, dst, tmp_ref)

    in_specs = [cur(0), prev(8), cur(8), prev(16), cur(16)] + [cur(0)] * n_prior
    args = [proj] * 5 + (list(prior) if n_prior else [])
    out_specs, out_shape = [cur(0), cur(0)], [SDS((S, D), F32), SDS((S, D), F32)]
    if final:
        assert d == 1
        in_specs.append(cur(OFF_G // LANE))
        args.append(proj)
        out_specs, out_shape = [cur(0)] + out_specs, [SDS((S, 2 * D), BF16)] + out_shape
    return pl.pallas_call(
        body, name=f"attn_fwd_d{d}", grid=(steps,),
        in_specs=in_specs, out_specs=out_specs, out_shape=out_shape,
        scratch_shapes=_dense_scratch(d, n_in + n_out),
        compiler_params=_cp(("parallel",)),
    )(*args)


def _attn_bwd(proj, do, lse, delta, d, acc, out_dtype, hosted=None):
    qb = ATTN_QB[d]
    nb, steps, cur, prev, lag = _attn_specs(d, qb)
    has_acc = acc is not None
    n_in = 11 if has_acc else 8
    n_host, n_host_out = (len(hosted.arrays), len(hosted.out_shape)) if hosted else (0, 0)
    assert not (d == 16 and (has_acc or out_dtype != F32))
    rows = CH * d * qb
    carry = (2, 16, CH, LANE) if d == 16 else (2, rows, LANE)

    def body(*refs):
        ins, host_in, refs = refs[:n_in], refs[n_in:n_in + n_host], refs[n_in + n_host:]
        (dq_ref, dk_ref, dv_ref), host_out, scratch = refs[:3], refs[3:3 + n_host_out], refs[3 + n_host_out:]
        if hosted:
            scratch, host_sems = scratch[:-len(hosted.scratch)], scratch[-len(hosted.scratch):]
        ck_ref, cv_ref = scratch[:2]
        dq_f32 = dq_ref if out_dtype == F32 else scratch[2]
        t = pl.program_id(0)
        i = t % nb
        if hosted:
            pl.when(t == 0)(lambda: hosted.start(host_in, host_out, host_sems))
        if d == 16:
            dense, dq_f32, tmp_ref = scratch[2:2 + n_in], scratch[2 + n_in], scratch[-1]

            @pl.when(t < steps)
            def _():
                for src, dst in zip(ins, dense):
                    _gather16(src, dst, tmp_ref)

            ins = dense
        q_ref, kp_ref, kc_ref, vp_ref, vc_ref, do_ref, lse_ref, dl_ref = ins[:8]
        if has_acc:
            aq_ref, ak_ref, av_ref = ins[8:11]
        slot = t & 1
        now_k, now_v, old_k, old_v = ck_ref.at[slot], cv_ref.at[slot], ck_ref.at[1 - slot], cv_ref.at[1 - slot]
        lane_a = _iota((CH, LANE), 1) < 64
        mask_first, mask_rest = _attn_mask(i > 0), _attn_mask(True)

        @pl.when(t == 0)
        def _():
            ck_ref[1] = jnp.zeros(carry[1:], F32)
            cv_ref[1] = jnp.zeros(carry[1:], F32)

        def unit(r, u):
            at = _unit_index(r, u, d)
            q2 = _stack_heads(q_ref[at] * 0.125, lane_a)
            do2 = _stack_heads(do_ref[at], lane_a)
            k2, v2 = _unit_kv(kp_ref, kc_ref, r, u, d), _unit_kv(vp_ref, vc_ref, r, u, d)
            lsev, dlv = lse_ref[at], dl_ref[at]
            lse2 = jnp.concatenate([lsev[:, 0:1], lsev[:, 64:65]], axis=0)
            dl2 = jnp.concatenate([dlv[:, 0:1], dlv[:, 64:65]], axis=0)
            p = jnp.exp(jnp.where(mask_first if u == 0 else mask_rest, _dot_nt(q2, k2), NEG) - lse2)
            ds = (p * (_dot_nt(do2, v2) - dl2)).astype(BF16)
            dq2 = _dot(ds, k2)
            dk2 = _dot_tn(ds, q2)
            dv2 = _dot_tn(p.astype(BF16), do2)
            dq = jnp.where(lane_a, dq2[:CH], dq2[CH:]) * 0.125
            if has_acc:
                dq = dq + aq_ref[at]
            dq_f32[at] = dq
            if u == 0:
                before = _unit_index(r, qb - 1, d)
                old_k[before] += dk2[:CH]
                old_v[before] += dv2[:CH]
            else:
                before = _unit_index(r, u - 1, d)
                now_k[before] += dk2[:CH]
                now_v[before] += dv2[:CH]
            now_k[at] = dk2[CH:]
            now_v[at] = dv2[CH:]

        @pl.when(t < steps)
        def _():
            _for_units(d, qb, unit)
            if d == 16:
                _scatter16(dq_f32, dq_ref, tmp_ref)
            elif out_dtype != F32:
                dq_ref[...] = dq_f32[...].astype(out_dtype)

        if d == 16:
            _scatter16(old_k, dk_ref, tmp_ref)
            _scatter16(old_v, dv_ref, tmp_ref)
        else:
            dk, dv = old_k[...], old_v[...]
            if has_acc:
                dk, dv = dk + ak_ref[...], dv + av_ref[...]
            dk_ref[...] = dk.astype(out_dtype)
            dv_ref[...] = dv.astype(out_dtype)
        if hosted:
            pl.when(t == steps)(lambda: hosted.finish(host_in, host_out, host_sems))

    in_specs = [cur(0), prev(8), cur(8), prev(16), cur(16), cur(0), cur(0), cur(0)]
    args = [proj, proj, proj, proj, proj, do, lse, delta]
    if has_acc:
        in_specs += [cur(0), lag, lag]
        args += list(acc)
    scratch = [pltpu.VMEM(carry, F32), pltpu.VMEM(carry, F32)]
    if d == 16:
        scratch += _dense_scratch(d, n_in + 1)
    elif out_dtype != F32:
        scratch.append(pltpu.VMEM((rows, LANE), F32))
    out_specs, out_shape = [cur(0), lag, lag], [SDS((S, D), out_dtype)] * 3
    if hosted:
        args += hosted.arrays
        in_specs += [ANY] * n_host
        out_specs += [ANY] * n_host_out
        out_shape += hosted.out_shape
        scratch += hosted.scratch
    outs = pl.pallas_call(
        body, name=f"attn_bwd_d{d}", grid=(steps + 1,),
        in_specs=in_specs, out_specs=out_specs, out_shape=out_shape,
        scratch_shapes=scratch, compiler_params=_cp(("arbitrary",)),
    )(*args)
    return (outs[:3], outs[3:]) if hosted else outs


def _conv_taps(cur, prev8, first):
    row8 = _iota(prev8.shape, 0)
    prev8 = jnp.where(first, 0.0, prev8)
    taps = []
    for s in (3, 2, 1):
        rolled = pltpu.roll(cur, s, 0)
        head = jnp.where(row8 < s, pltpu.roll(prev8, s, 0), rolled[:8])
        taps.append(jnp.concatenate([head, rolled[8:]], axis=0))
    return taps + [cur]


def _conv(taps, w, b):
    acc = b + w[0:1, :] * taps[0]
    for k in (1, 2, 3):
        acc = acc + w[k:k + 1, :] * taps[k]
    return acc


def _expand():
    return (_iota((LANE, D), 1) // 64 == _iota((LANE, D), 0)).astype(BF16)


def _reduce():
    return (_iota((D, LANE), 0) // 64 == _iota((D, LANE), 1)).astype(BF16)


def _ssd_common(xs_c, bc_c, dt_raw, dtb, alog):
    head_lane = _iota((CH, LANE), 1) < NH
    xs = xs_c * _sigmoid(xs_c)
    bc = bc_c * _sigmoid(bc_c)
    pre = dt_raw + dtb
    dt = jnp.where(head_lane, jnp.maximum(pre, 0.0) + jnp.log(1.0 + jnp.exp(-jnp.abs(pre))), 0.0)
    a_row = jnp.where(head_lane[0:1], -jnp.exp(alog), 0.0)
    tri = (_iota((CH, CH), 1) <= _iota((CH, CH), 0)).astype(BF16)
    cs = _pick_left(tri, dt * a_row)
    cs_last = cs[CH - 1:CH, :]
    wide = _pick(jnp.concatenate([dt, jnp.exp(cs), jnp.exp(cs_last - cs)], axis=0), _expand())
    dt_b, e_b, f_b = wide[:CH], wide[CH:2 * CH], wide[2 * CH:]
    return dict(xs=xs, bc=bc, pre=pre, dt=dt, a_row=a_row, cs=cs, cs_t=cs.T, dt_b=dt_b, e_b=e_b, f_b=f_b,
                t_b=e_b[CH - 1:CH, :])


def _groups(bc):
    bcb = bc.astype(BF16)
    return [bcb[:, 0:128], bcb[:, 128:256]], [bcb[:, 256:384], bcb[:, 384:512]]


def _decay(q, h, tril):
    seg = q["cs"][:, h:h + 1] - q["cs_t"][h:h + 1, :]
    return jnp.exp(jnp.where(tril, seg, NEG))


def _ssm_fwd(proj, mix, cw, cb, dtb, alog, d_b, nw):
    def body(xs_ref, xsp_ref, bc_ref, bcp_ref, dt_ref, z_ref, cw_ref, cb_ref, dtb_ref, alog_ref, db_ref, nw_ref,
             mix_in_ref, mix_ref, y_ref, st_ref, conv_ref, h_ref):
        del mix_in_ref
        i = pl.program_id(0)

        @pl.when(i == 0)
        def _():
            h_ref[...] = jnp.zeros_like(h_ref)

        cw, cb = cw_ref[...], cb_ref[...]
        xs_c = _conv(_conv_taps(xs_ref[...], xsp_ref[...], i == 0), cw[:, :D], cb[:, :D])
        bc_c = _conv(_conv_taps(bc_ref[...], bcp_ref[...], i == 0), cw[:, D:], cb[:, D:])
        conv_ref[:, :D] = xs_c
        conv_ref[:, D:] = bc_c
        q = _ssd_common(xs_c, bc_c, dt_ref[...], dtb_ref[...], alog_ref[...])
        bg, cg = _groups(q["bc"])
        xs = q["xs"]
        xdt = xs * q["dt_b"]
        xdt_b = xdt.astype(BF16)
        h_in = h_ref[...]
        st_ref[...] = h_in
        hb = h_in.astype(BF16)
        tril = _iota((CH, CH), 1) <= _iota((CH, CH), 0)
        lane_a = _iota((CH, LANE), 1) < 64
        cbm = [_dot_nt(cg[g], bg[g]) for g in range(2)]
        pairs = []
        for hp in range(NH // 2):
            xp = xdt_b[:, hp * LANE:(hp + 1) * LANE]
            ya = _dot((cbm[hp // 4] * _decay(q, 2 * hp, tril)).astype(BF16), xp)
            yb = _dot((cbm[hp // 4] * _decay(q, 2 * hp + 1, tril)).astype(BF16), xp)
            pairs.append(jnp.where(lane_a, ya, yb))
        y_diag = jnp.concatenate(pairs, axis=1)
        y_off = jnp.concatenate([_dot(cg[g], hb[:, g * 512:(g + 1) * 512]) for g in range(2)], axis=1) * q["e_b"]
        y = y_diag + y_off + db_ref[...] * xs
        y_ref[...] = y
        xf = (xdt * q["f_b"]).astype(BF16)
        h_ref[...] = q["t_b"] * h_in + jnp.concatenate(
            [_dot_tn(bg[g], xf[:, g * 512:(g + 1) * 512]) for g in range(2)], axis=1)
        z = z_ref[...]
        yz = y * (z * _sigmoid(z))
        outs = []
        for g in range(2):
            v = yz[:, g * 512:(g + 1) * 512]
            outs.append(v * lax.rsqrt(jnp.mean(v * v, axis=-1, keepdims=True) + EPS))
        mix_ref[...] = (jnp.concatenate(outs, axis=1) * nw_ref[...]).astype(BF16)

    def col(width, blk, prev=False):
        if prev:
            return pl.BlockSpec((8, width), lambda i: (jnp.maximum(i * (CH // 8) - 1, 0), blk))
        return pl.BlockSpec((CH, width), lambda i: (i, blk))

    def full(a):
        return pl.BlockSpec(a.shape, lambda i: (0,) * a.ndim)

    return pl.pallas_call(
        body, name="ssm_fwd", grid=(NC,),
        in_specs=[col(D, 5), col(D, 5, True), col(512, 12), col(512, 12, True), col(LANE, 52), col(D, 4),
                  full(cw), full(cb), full(dtb), full(alog), full(d_b), full(nw), ANY],
        out_specs=[col(D, 1), col(D, 0), pl.BlockSpec((None, CH, D), lambda i: (i, 0, 0)), col(D + 512, 0)],
        out_shape=[SDS((S, 2 * D), BF16), SDS((S, D), F32), SDS((NC, CH, D), F32), SDS((S, D + 512), F32)],
        scratch_shapes=[pltpu.VMEM((CH, D), F32)],
        input_output_aliases={12: 0},
        compiler_params=_cp(("arbitrary",)),
    )(proj, proj, proj, proj, proj, proj, cw, cb, dtb, alog, d_b, nw, mix)


def _ssm_bwd(proj, dn, y_save, states, conv_out, cw, dtb, alog, d_b, nw):
    def body(xs_ref, bc_ref, dt_ref, z_ref, dn_ref, y_ref, st_ref, conv_ref,
             cw_ref, dtb_ref, alog_ref, db_ref, nw_ref,
             dz_ref, dx_ref, dcw_ref, dcb_ref, dsm_ref, dnw_ref, dh_ref, nxs_ref, nbc_ref):
        i = pl.program_id(0)
        ci = NC - 1 - i

        @pl.when(i == 0)
        def _():
            for ref in (dcw_ref, dcb_ref, dsm_ref, dnw_ref, dh_ref, nxs_ref, nbc_ref):
                ref[...] = jnp.zeros_like(ref)

        cw = cw_ref[...]
        xs_c, bc_c = conv_ref[:, :D], conv_ref[:, D:]
        q = _ssd_common(xs_c, bc_c, dt_ref[...], dtb_ref[...], alog_ref[...])
        bg, cg = _groups(q["bc"])
        xs, dt_b, e_b, f_b, t_b = q["xs"], q["dt_b"], q["e_b"], q["f_b"], q["t_b"]
        xdt = xs * dt_b
        xdt_b = xdt.astype(BF16)
        h_in = st_ref[...]
        hb = h_in.astype(BF16)
        dh_new = dh_ref[...]
        dhb = dh_new.astype(BF16)
        red = _reduce()

        z, y, dn, nw_v = z_ref[...], y_ref[...], dn_ref[...], nw_ref[...]
        sig = _sigmoid(z)
        sz = z * sig
        yz = y * sz
        gdn = dn * nw_v
        dyz, dnw = [], []
        for g in range(2):
            v, gv = yz[:, g * 512:(g + 1) * 512], gdn[:, g * 512:(g + 1) * 512]
            r = lax.rsqrt(jnp.mean(v * v, axis=-1, keepdims=True) + EPS)
            dnw.append(dn[:, g * 512:(g + 1) * 512] * v * r)
            dyz.append(r * (gv - v * (r * r) * jnp.mean(gv * v, axis=-1, keepdims=True)))
        dyz = jnp.concatenate(dyz, axis=1)
        dnw_ref[...] += jnp.sum(jnp.concatenate(dnw, axis=1), axis=0, keepdims=True)
        dy = dyz * sz
        dz_ref[...] = (dyz * y * (sig * (1.0 + z * (1.0 - sig)))).astype(BF16)
        dy_b = dy.astype(BF16)

        tril = _iota((CH, CH), 1) <= _iota((CH, CH), 0)
        lane_a = _iota((CH, LANE), 1) < 64
        cbm = [_dot_nt(cg[g], bg[g]) for g in range(2)]
        dcbm = [jnp.zeros((CH, CH), F32), jnp.zeros((CH, CH), F32)]
        seg_rows = jnp.zeros((CH, LANE), F32)
        seg_cols = jnp.zeros((LANE, CH), F32)
        row_id, col_id = _iota((CH, LANE), 0), _iota((CH, LANE), 1)
        dx_pairs = []
        for hp in range(NH // 2):
            g = hp // 4
            xp = xdt_b[:, hp * LANE:(hp + 1) * LANE]
            dyp_f = dy[:, hp * LANE:(hp + 1) * LANE]
            dyp = dy_b[:, hp * LANE:(hp + 1) * LANE]
            halves = []
            for k in range(2):
                h = 2 * hp + k
                lane = lane_a if k == 0 else jnp.logical_not(lane_a)
                dec = _decay(q, h, tril)
                gm = cbm[g] * dec
                dgm = _dot_nt(jnp.where(lane, dyp_f, 0.0).astype(BF16), xp)
                dcbm[g] = dcbm[g] + dgm * dec
                prod = dgm * gm
                seg_rows = jnp.where(col_id == h, jnp.sum(prod, axis=1, keepdims=True), seg_rows)
                seg_cols = jnp.where(row_id == h, jnp.sum(prod, axis=0, keepdims=True), seg_cols)
                halves.append(_dot_tn(gm.astype(BF16), dyp))
            dx_pairs.append(jnp.where(lane_a, halves[0], halves[1]))
        dxdt_diag = jnp.concatenate(dx_pairs, axis=1)

        qv = jnp.concatenate([_dot(bg[g], dhb[:, g * 512:(g + 1) * 512]) for g in range(2)], axis=1)
        y_off = jnp.concatenate([_dot(cg[g], hb[:, g * 512:(g + 1) * 512]) for g in range(2)], axis=1) * e_b
        xfq = xdt * f_b * qv
        dxdt = dxdt_diag + f_b * qv
        tdt = jnp.sum(dh_new * h_in, axis=0, keepdims=True) * t_b
        per_head = _pick(jnp.concatenate([xfq, dy * y_off, dxdt * xs, dy * xs, jnp.broadcast_to(tdt, (8, D))],
                                         axis=0), red)
        fdf, dyoff_h, dxdtxs_h, dyxs_h = [per_head[k * CH:(k + 1) * CH] for k in range(4)]
        dcs = seg_rows - seg_cols.T + dyoff_h - fdf
        last = per_head[4 * CH:4 * CH + 1] + jnp.sum(fdf, axis=0, keepdims=True)
        dcs = dcs + jnp.where(_iota((CH, LANE), 0) == CH - 1, last, 0.0)
        tri_t = (_iota((CH, CH), 1) >= _iota((CH, CH), 0)).astype(BF16)
        da = _pick_left(tri_t, dcs)
        ddt = da * q["a_row"] + dxdtxs_h
        dxs = dxdt * dt_b + db_ref[...] * dy
        ddt_raw = ddt * _sigmoid(q["pre"])
        dsm_ref[0:1, :] += jnp.sum(ddt_raw, axis=0, keepdims=True)
        dsm_ref[1:2, :] += jnp.sum(da * q["dt"], axis=0, keepdims=True) * q["a_row"]
        dsm_ref[2:3, :] += jnp.sum(dyxs_h, axis=0, keepdims=True)
        edy = (e_b * dy).astype(BF16)
        xf = (xdt * f_b).astype(BF16)
        dbs, dcs_g, dhs = [], [], []
        for g in range(2):
            sl = slice(g * 512, (g + 1) * 512)
            dcb_b = dcbm[g].astype(BF16)
            dcs_g.append(_dot(dcb_b, bg[g]) + _dot_nt(edy[:, sl], hb[:, sl]))
            dbs.append(_dot_tn(dcb_b, cg[g]) + _dot_nt(xf[:, sl], dhb[:, sl]))
            dhs.append(_dot_tn(cg[g], edy[:, sl]))
        dh_ref[...] = t_b * dh_new + jnp.concatenate(dhs, axis=1)
        dbc = jnp.concatenate(dbs + dcs_g, axis=1)

        def conv_bwd(dact, pre, x_raw, w, nxt_ref, lo):
            s = _sigmoid(pre)
            dconv = dact * (s * (1.0 + pre * (1.0 - s)))
            nxt8 = nxt_ref[...]
            row8 = _iota(nxt8.shape, 0)
            hi = lo + dconv.shape[1]
            dcb_ref[:, lo:hi] += jnp.sum(dconv, axis=0, keepdims=True)
            later = [dconv]
            for s_ in (1, 2, 3):
                rolled = pltpu.roll(dconv, CH - s_, 0)
                tail = jnp.where(row8 >= 8 - s_, pltpu.roll(nxt8, 8 - s_, 0), rolled[CH - 8:])
                later.append(jnp.concatenate([rolled[:CH - 8], tail], axis=0))
            dx = None
            for s_, up in enumerate(later):
                k = 3 - s_
                dcw_ref[k:k + 1, lo:hi] += jnp.sum(up * x_raw, axis=0, keepdims=True)
                dx = w[k:k + 1, :] * up if dx is None else dx + w[k:k + 1, :] * up
            nxt_ref[...] = dconv[:8]
            return dx

        dx_ref[:, 0:D] = conv_bwd(dxs, xs_c, xs_ref[...], cw[:, :D], nxs_ref, 0).astype(BF16)
        dx_ref[:, D:D + 512] = conv_bwd(dbc, bc_c, bc_ref[...], cw[:, D:], nbc_ref, D).astype(BF16)
        dx_ref[:, D + 512:D + 640] = ddt_raw.astype(BF16)
        dx_ref[:, D + 640:] = jnp.zeros((CH, D - 640), BF16)

    def col(width, blk):
        return pl.BlockSpec((CH, width), lambda i: (NC - 1 - i, blk))

    def full(a):
        return pl.BlockSpec(a.shape, lambda i: (0,) * len(a.shape))

    acc_shapes = [SDS((4, 1536), F32), SDS((1, 1536), F32), SDS((8, LANE), F32), SDS((1, D), F32)]
    return pl.pallas_call(
        body, name="ssm_bwd", grid=(NC,),
        in_specs=[col(D, 5), col(512, 12), col(LANE, 52), col(D, 4),
                  col(D, 0), col(D, 0), pl.BlockSpec((None, CH, D), lambda i: (NC - 1 - i, 0, 0)), col(D + 512, 0),
                  full(cw), full(dtb), full(alog), full(d_b), full(nw)],
        out_specs=[col(D, 0), col(2 * D, 0)] + [full(a) for a in acc_shapes],
        out_shape=[SDS((S, D), BF16), SDS((S, 2 * D), BF16)] + acc_shapes,
        scratch_shapes=[pltpu.VMEM((CH, D), F32), pltpu.VMEM((8, D), F32), pltpu.VMEM((8, 512), F32)],
        compiler_params=_cp(("arbitrary",)),
    )(proj, proj, proj, proj, dn, y_save, states, conv_out, cw, dtb, alog, d_b, nw)


def _outproj_loss(mix, w_out, x, tgt, nw, attn_pre, proj):
    tm = 256

    def body(mix_ref, w_ref, x_ref, t_ref, nw_ref, pre_ref, g_ref,
             dy_ref, dn_ref, do_ref, delta_ref, dg_ref, dw_ref, dnw_ref, loss_ref):
        @pl.when(pl.program_id(0) == 0)
        def _():
            dw_ref[...] = jnp.zeros_like(dw_ref)
            dnw_ref[...] = jnp.zeros_like(dnw_ref)
            loss_ref[...] = jnp.zeros_like(loss_ref)

        mixv, w = mix_ref[...], w_ref[...]
        out = _dot(mixv, w)
        r = lax.rsqrt(jnp.mean(out * out, axis=-1, keepdims=True) + EPS)
        nh = out * r
        nw_v = nw_ref[...]
        err = x_ref[...] + nh * nw_v - t_ref[...]
        loss_ref[...] += 0.5 * jnp.sum(jnp.mean(err * err, axis=-1, keepdims=True), axis=0, keepdims=True)
        dy = err * (1.0 / D)
        dy_ref[...] = dy
        dnw_ref[...] += jnp.sum(dy * nh, axis=0, keepdims=True)
        gdn = dy * nw_v
        dout = (r * (gdn - nh * jnp.mean(gdn * nh, axis=-1, keepdims=True))).astype(BF16)
        dmix = _dot_nt(dout, w)
        dw_ref[...] += _dot_tn(mixv, dout)
        dn_ref[...] = dmix[:, D:]
        dm, g, pre_v = dmix[:, :D], g_ref[...], pre_ref[...]
        sig = _sigmoid(g)
        do = dm * (g * sig)
        do_ref[...] = do
        dg_ref[...] = (dm * pre_v * (sig * (1.0 + g * (1.0 - sig)))).astype(BF16)
        prod = do * pre_v
        same_head = (_iota((LANE, LANE), 0) // 64 == _iota((LANE, LANE), 1) // 64).astype(BF16)
        for cb in range(D // LANE):
            delta_ref[:, cb * LANE:(cb + 1) * LANE] = _pick(prod[:, cb * LANE:(cb + 1) * LANE], same_head)

    row = lambda w: pl.BlockSpec((tm, w), lambda i: (i, 0))
    full = lambda s: pl.BlockSpec(s, lambda i: (0, 0))
    return pl.pallas_call(
        body, name="outproj_loss", grid=(S // tm,),
        in_specs=[row(2 * D), full((2 * D, D)), row(D), row(D), full((1, D)), row(D),
                  pl.BlockSpec((tm, D), lambda i: (i, OFF_G // D))],
        out_specs=[row(D), row(D), row(D), row(D), row(D), full((2 * D, D)), full((1, D)), full((1, LANE))],
        out_shape=[SDS((S, D), F32)] * 4 + [SDS((S, D), BF16), SDS((2 * D, D), F32), SDS((1, D), F32),
                                            SDS((1, LANE), F32)],
        compiler_params=_cp(("arbitrary",)),
    )(mix, w_out, x, tgt, nw, attn_pre, proj)


def _inproj_bwd_dx(srcs, dxbcdt, w_all, x, dy, nw, hosted=None):
    tm = 512
    nk = DP // D
    n_host, n_host_out = (len(hosted.arrays), len(hosted.out_shape)) if hosted else (0, 0)

    def body(*refs):
        src_refs = refs[:nk]
        w_ref, x_ref, dy_ref, nw_ref = refs[nk:nk + 4]
        host_in, refs = refs[nk + 4:nk + 4 + n_host], refs[nk + 4 + n_host:]
        gx_ref, dnw_ref = refs[:2]
        host_out, host_sems = refs[2:2 + n_host_out], refs[2 + n_host_out:]
        i = pl.program_id(0)

        @pl.when(i == 0)
        def _():
            if hosted:
                hosted.start(host_in, host_out, host_sems)
            dnw_ref[...] = jnp.zeros_like(dnw_ref)

        du = None
        for k, ref in enumerate(src_refs):
            width = min(D, 5 * D + X_COLS - k * D)
            part = _dot_nt(ref[:, :width], w_ref[:, k * D:k * D + width])
            du = part if du is None else du + part
        xf, nw_v = x_ref[...], nw_ref[...]
        r = lax.rsqrt(jnp.mean(xf * xf, axis=-1, keepdims=True) + EPS)
        xh = xf * r
        dnw_ref[...] += jnp.sum(du * xh, axis=0, keepdims=True)
        gdu = du * nw_v
        gx_ref[...] = r * (gdu - xh * jnp.mean(gdu * xh, axis=-1, keepdims=True)) + dy_ref[...]

        if hosted:
            pl.when(i == S // tm - 1)(lambda: hosted.finish(host_in, host_out, host_sems))

    row = pl.BlockSpec((tm, D), lambda i: (i, 0))
    row1 = pl.BlockSpec((tm, D), lambda i: (i, 1))
    one = pl.BlockSpec((1, D), lambda i: (0, 0))
    whole_w = pl.BlockSpec((D, DP), lambda i: (0, 0), pipeline_mode=pl.Buffered(1))
    args = [*srcs, dxbcdt, dxbcdt, w_all, x, dy, nw]
    in_specs = [row] * len(srcs) + [row, row1, whole_w, row, row, one]
    out_specs, out_shape, scratch = [row, one], [SDS((S, D), F32), SDS((1, D), F32)], []
    if hosted:
        args += hosted.arrays
        in_specs += [ANY] * n_host
        out_specs += [ANY] * n_host_out
        out_shape += hosted.out_shape
        scratch += hosted.scratch
    outs = pl.pallas_call(
        body, name="inproj_bwd_dx", grid=(S // tm,),
        in_specs=in_specs, out_specs=out_specs, out_shape=out_shape, scratch_shapes=scratch,
        compiler_params=_cp(("arbitrary",)),
    )(*args)
    return (outs[:2], outs[2:]) if hosted else outs


def _dw(u, dsec, name, width=D, hosted=None):
    ts = 1024
    n_host, n_host_out = (len(hosted.arrays), len(hosted.out_shape)) if hosted else (0, 0)

    def body(u_ref, d_ref, *refs):
        host_in, o_ref, refs = refs[:n_host], refs[n_host], refs[n_host + 1:]
        host_out, host_sems = refs[:n_host_out], refs[n_host_out:]
        i = pl.program_id(0)

        @pl.when(i == 0)
        def _():
            if hosted:
                hosted.start(host_in, host_out, host_sems)
            o_ref[...] = jnp.zeros_like(o_ref)

        o_ref[...] += _dot_tn(u_ref[...], d_ref[...])
        if hosted:
            pl.when(i == S // ts - 1)(lambda: hosted.finish(host_in, host_out, host_sems))

    outs = pl.pallas_call(
        body, name=name, grid=(S // ts,),
        in_specs=[pl.BlockSpec((ts, D), lambda i: (i, 0)), pl.BlockSpec((ts, width), lambda i: (i, 0))]
        + [ANY] * n_host,
        out_specs=[pl.BlockSpec((D, width), lambda i: (0, 0))] + [ANY] * n_host_out,
        out_shape=[SDS((D, width), F32)] + (hosted.out_shape if hosted else []),
        scratch_shapes=hosted.scratch if hosted else [],
        compiler_params=_cp(("arbitrary",)),
    )(u, dsec, *(hosted.arrays if hosted else []))
    return (outs[0], outs[1:]) if hosted else outs[0]


def _place():
    x, y, c = lax.axis_index("x"), lax.axis_index("y"), lax.axis_index("c")
    return x, y, c, 2 * x + y


def _chip_of(x, y, k):
    px = 1 - x if k & 2 else x
    py = 1 - y if k & 1 else y
    return px, py, 2 * px + py


def _remote(src, dst, send_sem, recv_sem, dev):
    return pltpu.make_async_remote_copy(src_ref=src, dst_ref=dst, send_sem=send_sem, recv_sem=recv_sem,
                                        device_id=dev, device_id_type=MESH)


def _gather_weights(w_in_b):
    half = w_in_b.shape[0] // 2
    quarter = half // 2

    def body(src, dst, send, recv):
        x, y, c, j = _place()
        me, sib = (x, y, c), (x, y, 1 - c)
        nbr = {"x": _chip_of(x, y, 2), "y": _chip_of(x, y, 1)}
        diag = _chip_of(x, y, 3)[2]
        started, arrivals = [], []

        def rows(n_quarter=None, sibling=False):
            base = (1 - c if sibling else c) * half
            return pl.ds(base, half) if n_quarter is None else pl.ds(base + n_quarter * quarter, quarter)

        def sem(n):
            return send.at[n], recv.at[n]

        def go(cp):
            cp.start()
            started.append(cp)

        own = _remote(src, dst.at[j], *sem(8), sib)
        go(own)
        for n, axis in enumerate("xy"):
            px, py, _ = nbr[axis]
            go(_remote(src.at[rows()], dst.at[j, rows()], *sem(n), (px, py, c)))
        for n, axis in enumerate("xy"):
            ox, oy, _ = nbr["y" if axis == "x" else "x"]
            pj = nbr[axis][2]
            _remote(src.at[rows()], dst.at[pj, rows()], *sem(n), me).wait_recv()
            go(_remote(dst.at[pj, rows(n)], dst.at[pj, rows(n)], *sem(2 + n), (ox, oy, c)))
            go(_remote(dst.at[pj, rows()], dst.at[pj, rows()], *sem(4 + n), sib))
            arrivals.append(_remote(src.at[rows()], dst.at[pj, rows(None, True)], *sem(4 + n), me))
        for n in range(2):
            _remote(dst.at[diag, rows(n)], dst.at[diag, rows(n)], *sem(2 + n), me).wait_recv()
            go(_remote(dst.at[diag, rows(n)], dst.at[diag, rows(n)], *sem(6 + n), sib))
            arrivals.append(_remote(dst.at[diag, rows(n, True)], dst.at[diag, rows(n, True)], *sem(6 + n), me))
        for cp in arrivals + [own]:
            cp.wait_recv()
        for cp in started:
            cp.wait_send()

    return pl.pallas_call(
        body, name="gather_weights", in_specs=[ANY], out_specs=ANY,
        out_shape=SDS((4,) + w_in_b.shape, BF16),
        scratch_shapes=[pltpu.SemaphoreType.DMA((9,)), pltpu.SemaphoreType.DMA((9,))],
        compiler_params=pltpu.CompilerParams(has_side_effects=True),
    )(w_in_b)


class _LateGather:
    def __init__(self, w_out_b, conv_w):
        self.arrays = [w_out_b, conv_w]
        self.out_shape = [SDS((4,) + w_out_b.shape, BF16), SDS((4,) + conv_w.shape, F32)]
        self.scratch = [pltpu.SemaphoreType.DMA((11,)), pltpu.SemaphoreType.DMA((11,))]

    def _plan(self, ins, outs, sems):
        x, y, c, j = _place()
        send, recv = sems
        (wo, cw), (gwo, gcw) = ins, outs
        half = wo.shape[0] // 2
        mine, theirs = pl.ds(c * half, half), pl.ds((1 - c) * half, half)
        me, sib = (x, y, c), (x, y, 1 - c)
        first, arrive, forward, last = [], [], [], []
        for k in (1, 2, 3):
            px, py, pj = _chip_of(x, y, k)
            first += [_remote(wo.at[mine], gwo.at[j, mine], send.at[k - 1], recv.at[k - 1], (px, py, c)),
                      _remote(cw, gcw.at[j], send.at[k + 2], recv.at[k + 2], (px, py, c))]
            arrive.append(_remote(wo.at[mine], gwo.at[pj, mine], send.at[k - 1], recv.at[k - 1], me))
            forward.append(_remote(gwo.at[pj, mine], gwo.at[pj, mine], send.at[k + 5], recv.at[k + 5], sib))
            last += [_remote(cw, gcw.at[pj], send.at[k + 2], recv.at[k + 2], me),
                     _remote(wo.at[theirs], gwo.at[pj, theirs], send.at[k + 5], recv.at[k + 5], me)]
        first += [_remote(wo, gwo.at[j], send.at[9], recv.at[9], sib),
                  _remote(cw, gcw.at[j], send.at[10], recv.at[10], sib)]
        last += first[-2:]
        return first, arrive, forward, last

    def start(self, ins, outs, sems):
        for cp in self._plan(ins, outs, sems)[0]:
            cp.start()

    def pass_on(self, ins, outs, sems):
        _, arrive, forward, _ = self._plan(ins, outs, sems)
        for got, fwd in zip(arrive, forward):
            got.wait_recv()
            fwd.start()

    def finish(self, ins, outs, sems):
        first, _, forward, last = self._plan(ins, outs, sems)
        for cp in last:
            cp.wait_recv()
        for cp in first + forward:
            cp.wait_send()


def _window(s, names):
    lo, hi = TILES * s, TILES * s + TILES + 1
    pieces = []
    for n, name in enumerate(names):
        a, count = SECTION_TILES[name]
        first, last = max(lo, a), min(hi, a + count)
        if first < last:
            pieces.append((n, first - a, last - first, first - lo))
    assert sum(p[2] for p in pieces) == TILES + 1
    return pieces


class _PairExchange:
    def __init__(self, names, sections, shards, more=()):
        self.names, self.shards = names, shards
        self.there = [n for n, a in enumerate(sections) if a is not None]
        self.arrays = [sections[n] for n in self.there] + list(more)
        self.out_shape = [SDS((len(shards), D // 2, WIN), F32)]
        self.out_shape += [SDS((a.shape[0], a.shape[1] // 2, a.shape[2]), F32) for a in more]
        n = sum(p[0] in self.there for s in shards for p in _window(s, names)) + len(more)
        self.scratch = [pltpu.SemaphoreType.DMA((n,)) for _ in range(2)]

    def _copies(self, ins, outs, sems):
        x, y, c, _ = _place()
        sib = (x, y, 1 - c)
        rows = pl.ds((1 - c) * (D // 2), D // 2)
        k = 0
        for i, s in enumerate(self.shards):
            for n, tile, tiles, at in _window(s, self.names):
                if n in self.there:
                    yield _remote(ins[self.there.index(n)].at[rows, pl.ds(tile * LANE, tiles * LANE)],
                                  outs[0].at[i, :, pl.ds(at * LANE, tiles * LANE)], sems[0].at[k], sems[1].at[k], sib)
                    k += 1
        for src, dst in zip(ins[len(self.there):], outs[1:]):
            half = src.shape[1] // 2
            yield _remote(src.at[:, pl.ds((1 - c) * half, half)], dst, sems[0].at[k], sems[1].at[k], sib)
            k += 1

    def start(self, ins, outs, sems):
        for cp in self._copies(ins, outs, sems):
            cp.start()

    def finish(self, ins, outs, sems):
        for cp in self._copies(ins, outs, sems):
            cp.wait()


def _exchange_call(exchange, name, into=None):
    n, n_out = len(exchange.arrays), len(exchange.out_shape)
    given = list(into) if into else []

    def body(*refs):
        ins, outs, sems = refs[:n], refs[n + len(given):n + len(given) + n_out], refs[n + len(given) + n_out:]
        exchange.start(ins, outs, sems)
        exchange.finish(ins, outs, sems)

    return pl.pallas_call(
        body, name=name, in_specs=[ANY] * (n + len(given)), out_specs=[ANY] * n_out, out_shape=exchange.out_shape,
        input_output_aliases={n + k: k for k in range(len(given))},
        scratch_shapes=exchange.scratch, compiler_params=pltpu.CompilerParams(has_side_effects=True),
    )(*exchange.arrays, *given)


def _pair_sum_windows(cidx, names, sections, shards, r, name):
    n, half, _ = r.shape
    tr = min(half, 256)
    nt = half // tr

    def body(c_ref, *refs):
        del c_ref
        secs, r_ref, o_ref = refs[:-2], refs[-2], refs[-1]
        for i, s in enumerate(shards):
            for k, tile, tiles, at in _window(s, names):
                own = secs[k][:, tile * LANE:(tile + tiles) * LANE]
                there = slice(at * LANE, (at + tiles) * LANE)
                o_ref[i, :, there] = (own + r_ref[i, :, there]).astype(BF16)

    window = pl.BlockSpec((n, tr, WIN), lambda t, c: (0, t, 0))
    return pl.pallas_call(
        body, name=name,
        grid_spec=pltpu.PrefetchScalarGridSpec(
            num_scalar_prefetch=1, grid=(nt,),
            in_specs=[pl.BlockSpec((tr, a.shape[1]), lambda t, c: (c[0] * nt + t, 0)) for a in sections] + [window],
            out_specs=window),
        out_shape=SDS(r.shape, BF16),
        compiler_params=_cp(("parallel",)),
    )(cidx, *sections, r)


def _pair_sum(cidx, g, r, name):
    n, half, width = r.shape
    tr = min(half, 256)
    nt = half // tr

    def body(c_ref, g_ref, r_ref, o_ref):
        del c_ref
        o_ref[...] = (g_ref[...] + r_ref[...]).astype(BF16)

    return pl.pallas_call(
        body, name=name,
        grid_spec=pltpu.PrefetchScalarGridSpec(
            num_scalar_prefetch=1, grid=(n, nt),
            in_specs=[pl.BlockSpec((None, tr, width), lambda s, t, c: (s, c[0] * nt + t, 0)),
                      pl.BlockSpec((None, tr, width), lambda s, t, c: (s, t, 0))],
            out_specs=pl.BlockSpec((None, tr, width), lambda s, t, c: (s, t, 0))),
        out_shape=SDS(r.shape, BF16),
        compiler_params=_cp(("parallel", "parallel")),
    )(cidx, g, r)


class _ChipExchange:
    def __init__(self, arrays, rows):
        self.arrays, self.rows = list(arrays), list(rows)
        self.out_shape = [SDS((4,) + a.shape[1:], BF16) for a in self.arrays]
        self.scratch = [pltpu.SemaphoreType.DMA((3 * len(self.arrays),)) for _ in range(2)]

    def _copies(self, ins, outs, sems):
        x, y, c, j = _place()
        send, recv = sems
        for a, (src, dst, row) in enumerate(zip(ins, outs, self.rows)):
            for k in (1, 2, 3):
                px, py, pj = _chip_of(x, y, k)
                n = 3 * a + k - 1
                slot = pj if row is None else py
                yield (None if row is None else px == row, None if row is None else x == row,
                       _remote(src.at[slot], dst.at[j], send.at[n], recv.at[n], (px, py, c)),
                       _remote(src.at[0], dst.at[pj], send.at[n], recv.at[n], (x, y, c)))

    def start(self, ins, outs, sems):
        for sends, _, send, _ in self._copies(ins, outs, sems):
            if sends is None:
                send.start()
            else:
                pl.when(sends)(send.start)

    def finish(self, ins, outs, sems):
        for sends, owns, send, arrival in self._copies(ins, outs, sems):
            if sends is None:
                arrival.wait_recv()
                send.wait_send()
            else:
                pl.when(owns)(arrival.wait_recv)
                pl.when(sends)(send.wait_send)


def _all_gather_rows(src, dst, rows, send, recv, local_sem):
    x, y, c, j = _place()
    me = 2 * j + c
    local = pltpu.make_async_copy(src, dst.at[me, rows], local_sem)
    cps, arrivals = [], []
    for k in range(1, 8):
        px, py, pj = _chip_of(x, y, k >> 1)
        pc = 1 - c if k & 1 else c
        cps.append(_remote(src, dst.at[me, rows], send.at[k - 1], recv.at[k - 1], (px, py, pc)))
        arrivals.append(_remote(src, dst.at[2 * pj + pc, rows], send.at[k - 1], recv.at[k - 1], (x, y, c)))
    starts = [local.start] + [cp.start for cp in cps]
    waits = [cp.wait_recv for cp in arrivals] + [cp.wait_send for cp in cps] + [local.wait]
    return starts, waits


class _SmallExchange:
    def __init__(self, small):
        self.arrays = [small]
        self.out_shape = [SDS((8,) + small.shape, F32)]
        self.scratch = [pltpu.SemaphoreType.DMA((7,)), pltpu.SemaphoreType.DMA((7,)), pltpu.SemaphoreType.DMA]

    def start(self, ins, outs, sems):
        for go in _all_gather_rows(ins[0], outs[0], slice(None), *sems)[0]:
            go()

    def finish(self, ins, outs, sems):
        for wait in _all_gather_rows(ins[0], outs[0], slice(None), *sems)[1]:
            wait()


class _Both:
    def __init__(self, a, b):
        self.parts = (a, b)
        self.arrays, self.out_shape, self.scratch = a.arrays + b.arrays, a.out_shape + b.out_shape, a.scratch + b.scratch

    def _split(self, ins, outs, sems):
        a, b = self.parts
        return ((a, ins[:len(a.arrays)], outs[:len(a.out_shape)], sems[:len(a.scratch)]),
                (b, ins[len(a.arrays):], outs[len(a.out_shape):], sems[len(a.scratch):]))

    def start(self, ins, outs, sems):
        for part, *refs in self._split(ins, outs, sems):
            part.start(*refs)

    def finish(self, ins, outs, sems):
        for part, *refs in self._split(ins, outs, sems):
            part.finish(*refs)


def _slot_sum(r, name):
    n, rows, width = r.shape
    tr = min(rows, 256)

    def body(r_ref, o_ref):
        acc = r_ref[0].astype(F32)
        for s in range(1, n):
            acc = acc + r_ref[s].astype(F32)
        o_ref[...] = acc

    return pl.pallas_call(
        body, name=name, grid=(rows // tr,),
        in_specs=[pl.BlockSpec((n, tr, width), lambda t: (0, t, 0))],
        out_specs=pl.BlockSpec((tr, width), lambda t: (t, 0)),
        out_shape=SDS((rows, width), F32),
        compiler_params=_cp(("parallel",)),
    )(r)


def _chip_sum(where, recv, own, name):
    n, rows, width = recv.shape
    tr = min(rows, 256)
    nt = rows // tr

    def body(j_ref, r_ref, own_ref, o_ref):
        acc = None
        for s in range(n):
            term = jnp.where(j_ref[0] == s, own_ref[...], r_ref[s]).astype(F32)
            acc = term if acc is None else acc + term
        o_ref[...] = acc

    return pl.pallas_call(
        body, name=name,
        grid_spec=pltpu.PrefetchScalarGridSpec(
            num_scalar_prefetch=1, grid=(nt,),
            in_specs=[pl.BlockSpec((n, tr, width), lambda t, j: (0, t, 0)),
                      pl.BlockSpec((None, tr, width), lambda t, j: (j[0], t, 0))],
            out_specs=pl.BlockSpec((tr, width), lambda t, j: (j[1] * nt + t, 0))),
        out_shape=SDS((2 * rows, width), F32),
        compiler_params=_cp(("parallel",)),
    )(where, recv, own)


def _chip_sum_rows(place, recv0, own0, recv1, own1, name):
    n, rows, width = recv0.shape
    tr = min(rows, 256)
    nt = rows // tr

    def body(p_ref, r0_ref, o0_ref, r1_ref, o1_ref, o_ref):
        first_row = p_ref[2] == 0
        own = jnp.where(first_row, o0_ref[...], o1_ref[...])
        acc = None
        for s in range(n):
            term = jnp.where(p_ref[0] == s, own, jnp.where(first_row, r0_ref[s], r1_ref[s])).astype(F32)
            acc = term if acc is None else acc + term
        o_ref[...] = acc

    recv = pl.BlockSpec((n, tr, width), lambda t, p: (0, t, 0))
    own = pl.BlockSpec((None, tr, width), lambda t, p: (p[3], t, 0))
    return pl.pallas_call(
        body, name=name,
        grid_spec=pltpu.PrefetchScalarGridSpec(
            num_scalar_prefetch=1, grid=(nt,), in_specs=[recv, own, recv, own],
            out_specs=pl.BlockSpec((tr, width), lambda t, p: (p[1] * nt + t, 0))),
        out_shape=SDS((2 * rows, width), F32),
        compiler_params=_cp(("parallel",)),
    )(place, recv0, own0, recv1, own1)


def _half_exchange(gw, go, gathered, late, row):
    def body(gw_in, go_in, ga_in, late_ref, gw_ref, go_ref, ga_ref, send, recv, late_send, late_recv, late_local):
        del gw_in, go_in, ga_in
        x, y, c, _ = _place()
        starts, waits = _all_gather_rows(late_ref, ga_ref, pl.ds(row, late.shape[0]), late_send, late_recv,
                                         late_local)
        for go_ in starts:
            go_()
        mine = [pl.ds(c * (r.shape[0] // 2), r.shape[0] // 2) for r in (gw_ref, go_ref)]
        cps = [_remote(r.at[rows], r.at[rows], send.at[k], recv.at[k], (x, y, 1 - c))
               for k, (r, rows) in enumerate(zip((gw_ref, go_ref), mine))]
        for cp in cps:
            cp.start()
        for k, r in enumerate((gw_ref, go_ref)):
            theirs = pl.ds((1 - c) * (r.shape[0] // 2), r.shape[0] // 2)
            _remote(r.at[theirs], r.at[theirs], send.at[k], recv.at[k], (x, y, c)).wait_recv()
        for cp in cps:
            cp.wait_send()
        for wait in waits:
            wait()

    return pl.pallas_call(
        body, name="half_exchange", in_specs=[ANY] * 4, out_specs=[ANY] * 3,
        out_shape=[SDS(gw.shape, F32), SDS(go.shape, F32), SDS(gathered.shape, F32)],
        input_output_aliases={0: 0, 1: 1, 2: 2},
        scratch_shapes=[pltpu.SemaphoreType.DMA((2,)), pltpu.SemaphoreType.DMA((2,)),
                        pltpu.SemaphoreType.DMA((7,)), pltpu.SemaphoreType.DMA((7,)), pltpu.SemaphoreType.DMA],
        compiler_params=pltpu.CompilerParams(has_side_effects=True),
    )(gw, go, gathered, late)


def _adamw(w, g, m, v, name):
    rows, width = w.shape
    tr = min(rows, 256)

    def body(w_ref, g_ref, m_ref, v_ref, d_ref, nm_ref, nv_ref):
        gv = g_ref[...]
        nm = ADAM_B1 * m_ref[...] + (1.0 - ADAM_B1) * gv
        nv = ADAM_B2 * v_ref[...] + (1.0 - ADAM_B2) * (gv * gv)
        m_hat = nm / (1.0 - ADAM_B1 ** ADAM_STEP)
        v_hat = nv / (1.0 - ADAM_B2 ** ADAM_STEP)
        d_ref[...] = -ADAM_LR * (m_hat / (jnp.sqrt(v_hat) + ADAM_EPS) + ADAM_WD * w_ref[...])
        nm_ref[...] = nm
        nv_ref[...] = nv

    t = pl.BlockSpec((tr, width), lambda i: (i, 0))
    return pl.pallas_call(
        body, name=name, grid=(rows // tr,), in_specs=[t] * 4, out_specs=[t] * 3,
        out_shape=[SDS(w.shape, F32)] * 3, compiler_params=_cp(("parallel",)),
    )(w, g, m, v)


def _rowwise(a):
    return jnp.transpose(a, (2, 0, 1)).reshape(SHARD * D // LANE, LANE)


def _columns(ref):
    return jnp.concatenate([ref[pl.ds(c, LANE, stride=8), :].T for c in range(D // LANE)], axis=0)


def _shard_bf16(chip, w_rows):
    def body(j_ref, w_ref, o_ref, prev_ref):
        t = pl.program_id(0)
        cur = _columns(w_ref)

        @pl.when(t == 0)
        def _():
            prev_ref[...] = jnp.zeros_like(prev_ref)

        lane = _iota((D, LANE), 1)
        for s in range(4):
            @pl.when(j_ref[0] == s)
            def _():
                off = SHIFT * s
                moved = cur if s == 0 else jnp.where(lane < off, pltpu.roll(prev_ref[...], off, 1),
                                                     pltpu.roll(cur, off, 1))
                col = t * LANE + lane - off
                o_ref[...] = jnp.where((col >= 0) & (col < SHARD), moved, 0.0).astype(BF16)
        prev_ref[...] = cur

    return pl.pallas_call(
        body, name="shard_bf16",
        grid_spec=pltpu.PrefetchScalarGridSpec(
            num_scalar_prefetch=1, grid=(TILES + 1,),
            in_specs=[pl.BlockSpec((D, LANE), lambda t, j: (t, 0))],
            out_specs=pl.BlockSpec((D, LANE), lambda t, j: (0, t)),
            scratch_shapes=[pltpu.VMEM((D, LANE), F32)]),
        out_shape=SDS((D, WIN), BF16), compiler_params=_cp(("arbitrary",)),
    )(chip, w_rows)


def _whole_w_in(windows):
    tr = 256
    n = windows.shape[0]

    def body(g_ref, o_ref):
        lane = _iota((tr, LANE), 1)
        for s in range(n):
            first = TILES * s
            head = g_ref[s, :, :LANE]
            if s:
                tail = g_ref[s - 1, :, TILES * LANE:]
                head = jnp.where(lane < SHIFT * s, tail.astype(F32), head.astype(F32)).astype(BF16)
            o_ref[:, first * LANE:(first + 1) * LANE] = head
            o_ref[:, (first + 1) * LANE:(first + TILES) * LANE] = g_ref[s, :, LANE:TILES * LANE]
        o_ref[:, n * TILES * LANE:(n * TILES + 1) * LANE] = g_ref[n - 1, :, TILES * LANE:]
        o_ref[:, (n * TILES + 1) * LANE:] = jnp.zeros((tr, DP - (n * TILES + 1) * LANE), BF16)

    return pl.pallas_call(
        body, name="whole_w_in", grid=(D // tr,),
        in_specs=[pl.BlockSpec((n, tr, WIN), lambda t: (0, t, 0))], out_specs=pl.BlockSpec((tr, DP), lambda t: (t, 0)),
        out_shape=SDS((D, DP), BF16), compiler_params=_cp(("parallel",)),
    )(windows)


def _own_buffer(a, name):
    tr = 512
    block = pl.BlockSpec((tr, a.shape[1]), lambda t: (t, 0))

    def body(a_ref, o_ref):
        o_ref[...] = a_ref[...]

    return pl.pallas_call(
        body, name=name, grid=(a.shape[0] // tr,), in_specs=[block], out_specs=block,
        out_shape=SDS(a.shape, a.dtype), compiler_params=_cp(("parallel",)),
    )(a)


def _shard_of_window(chip, g_win):
    def body(j_ref, g_ref, next_ref, grad_ref):
        for s in range(4):
            @pl.when(j_ref[0] == s)
            def _():
                if s == 0:
                    grad_ref[...] = g_ref[...]
                else:
                    back = LANE - SHIFT * s
                    grad_ref[...] = jnp.where(_iota((D, LANE), 1) < back, pltpu.roll(g_ref[...], back, 1),
                                              pltpu.roll(next_ref[...], back, 1))

    tile = pl.BlockSpec((D, LANE), lambda t, j: (0, t))
    next_tile = pl.BlockSpec((D, LANE), lambda t, j: (0, jnp.minimum(t + 1, TILES)))
    return pl.pallas_call(
        body, name="shard_of_window",
        grid_spec=pltpu.PrefetchScalarGridSpec(
            num_scalar_prefetch=1, grid=(TILES + 1,), in_specs=[tile, next_tile], out_specs=tile),
        out_shape=SDS((D, SHARD), F32), compiler_params=_cp(("parallel",)),
    )(chip, g_win, g_win)


def _adamw_in(w_rows, g, m_rows, v_rows):
    def body(w_ref, g_ref, m_ref, v_ref, d_ref, nm_ref, nv_ref):
        columns = _columns
        gv = g_ref[...]
        nm = ADAM_B1 * columns(m_ref) + (1.0 - ADAM_B1) * gv
        nv = ADAM_B2 * columns(v_ref) + (1.0 - ADAM_B2) * (gv * gv)
        m_hat = nm / (1.0 - ADAM_B1 ** ADAM_STEP)
        v_hat = nv / (1.0 - ADAM_B2 ** ADAM_STEP)
        d_ref[...] = -ADAM_LR * (m_hat / (jnp.sqrt(v_hat) + ADAM_EPS) + ADAM_WD * columns(w_ref))
        nm_ref[...] = nm
        nv_ref[...] = nv

    tile = pl.BlockSpec((D, LANE), lambda t: (0, t))
    rows = pl.BlockSpec((D, LANE), lambda t: (t, 0))
    return pl.pallas_call(
        body, name="adamw_in", grid=(TILES + 1,), in_specs=[rows, tile, rows, rows],
        out_specs=[tile] * 3, out_shape=[SDS(g.shape, F32)] * 3, compiler_params=_cp(("parallel",)),
    )(w_rows, g, m_rows, v_rows)


def _rows128(a, rows):
    flat = a.reshape(-1)
    return jnp.pad(flat, (0, rows * LANE - flat.shape[0])).reshape(rows, LANE)


CONV_ROWS = 48


def _pack_small(conv_w, norm_pre, conv_b, ssm_norm, norm_post, dtb, alog, dsk, extra=None):
    cw_rows = CONV_ROWS if conv_w.shape[-1] == 1536 else 16
    extra = jnp.zeros((1, LANE), F32) if extra is None else _rows128(extra, 1)
    vec = jnp.concatenate([_rows128(dtb, 1), _rows128(alog, 1), _rows128(dsk, 1), extra, jnp.zeros((4, LANE), F32)],
                          axis=0)
    return jnp.concatenate([_rows128(conv_w, cw_rows), _rows128(norm_pre, 8), _rows128(conv_b, 16),
                            _rows128(ssm_norm, 8), _rows128(norm_post, 8), vec], axis=0)


def _unpack_small(p, cw_cols):
    cw_rows = CONV_ROWS if cw_cols == 1536 else 16
    o = cw_rows
    conv_w = p[:cw_rows].reshape(-1)[:4 * cw_cols].reshape(1, 4, cw_cols)
    norm_pre = p[o:o + 8].reshape(1, D)
    conv_b = p[o + 8:o + 24].reshape(-1)[:1536].reshape(1, 1536)
    ssm_norm = p[o + 24:o + 32].reshape(1, D)
    norm_post = p[o + 32:o + 40].reshape(1, D)
    vec = p[o + 40:o + 48]
    return conv_w, norm_pre, conv_b, ssm_norm, norm_post, vec[0:1, :NH], vec[1:2, :NH], vec[2:3, :NH], vec[3, 0]


def _pad_lanes(a):
    return jnp.pad(a, ((0, 0), (0, LANE - a.shape[1])))


class _GradReduce:
    LO, HI = ("q", "k", "v", "g"), ("g", "z", "x")

    def __init__(self, xi, yi, ci):
        self.cidx = jnp.reshape(ci, (1,)).astype(jnp.int32)
        self.place = jnp.stack([2 * xi + yi, ci, xi, yi]).astype(jnp.int32)

    def pairs(self, dw_g, dw_z, dw_x, dw_out):
        self.hi = [dw_g, dw_z, dw_x]
        self.go = dw_out.reshape(4, D // 2, D)
        return _PairExchange(self.HI, self.hi, (2, 3), [self.go])

    def first(self, got):
        rw, ro = got
        self.pw_hi = _pair_sum_windows(self.cidx, self.HI, self.hi, (2, 3), rw, "pair_sum_hi")
        self.po = _pair_sum(self.cidx, self.go, ro, "pair_sum_out")
        return _ChipExchange([self.pw_hi, self.po], [1, None])

    def first_done(self, got):
        self.rw_hi, self.ro = got

    def second_pairs(self, dw_q, dw_k, dw_g):
        self.lo = [dw_q, dw_k, None, dw_g]
        return _PairExchange(self.LO, self.lo, (0, 1))

    def second(self, dw_v, got, small):
        rest = _PairExchange(self.LO, [None, None, dw_v, None], (0, 1))
        (rw,) = _exchange_call(rest, "pair_exchange_v", into=got)
        lo = [dw_v if a is None else a for a in self.lo]
        self.pw_lo = _pair_sum_windows(self.cidx, self.LO, lo, (0, 1), rw, "pair_sum_lo")
        return _Both(_ChipExchange([self.pw_lo], [0]), _SmallExchange(small))

    def second_done(self, got):
        self.rw_lo, self.small = got

    def result(self, late, row):
        half_in = _chip_sum_rows(self.place, self.rw_lo, self.pw_lo, self.rw_hi, self.pw_hi, "chip_sum_in")
        half_out = _chip_sum(self.place[0:2], self.ro, self.po, "chip_sum_out")
        return _half_exchange(half_in, half_out, self.small, late, row)


def kernel(x, norm_pre_w, w_in, conv_w, conv_b, dt_bias, a_log, d_skip, ssm_norm_w, w_out, norm_post_w, loss_target, m_norm_pre_w, m_w_in, m_conv_w, m_conv_b, m_dt_bias, m_a_log, m_d_skip, m_ssm_norm_w, m_w_out, m_norm_post_w, v_norm_pre_w, v_w_in, v_conv_w, v_conv_b, v_dt_bias, v_a_log, v_d_skip, v_ssm_norm_w, v_w_out, v_norm_post_w):
    xi, yi, ci = lax.axis_index("x"), lax.axis_index("y"), lax.axis_index("c")
    chip = 2 * xi + yi
    x2, tgt = x[0], loss_target[0]

    chip_idx = jnp.reshape(chip, (1,)).astype(jnp.int32)
    w_rows = _rowwise(w_in)
    w_all = _whole_w_in(_gather_weights(_shard_bf16(chip_idx, w_rows)))
    reduce = _GradReduce(xi, yi, ci)
    grad_x, dnw_pre = _local_step(x2, tgt, w_all, _LateGather(w_out[0].astype(BF16), conv_w[0]), norm_pre_w, conv_b,
                                  dt_bias, a_log, d_skip, ssm_norm_w, norm_post_w, reduce)
    g_win, g_out, small = reduce.result(_rows128(dnw_pre, D // LANE), CONV_ROWS)
    g_small = _slot_sum(small, "small_sum")
    g_cw, g_npre, g_cb, g_nssm, g_npost, g_dtb, g_alog, g_dsk, loss = _unpack_small(g_small, 1536)
    g_cw = lax.dynamic_slice_in_dim(g_cw, chip * 384, 384, axis=2)

    g_in = _shard_of_window(chip_idx, g_win)
    d_in, nm_in, nv_in = _adamw_in(w_rows, g_in, _rowwise(m_w_in), _rowwise(v_w_in))
    grad_x = _own_buffer(grad_x, "grad_x_copy")
    d_out, nm_out, nv_out = _adamw(w_out[0], g_out, m_w_out[0], v_w_out[0], "adamw_out")
    packed = [_pack_small(*t) for t in (
        (conv_w, norm_pre_w, conv_b, ssm_norm_w, norm_post_w, dt_bias, a_log, d_skip),
        (g_cw, g_npre, g_cb, g_nssm, g_npost, g_dtb, g_alog, g_dsk),
        (m_conv_w, m_norm_pre_w, m_conv_b, m_ssm_norm_w, m_norm_post_w, m_dt_bias, m_a_log, m_d_skip),
        (v_conv_w, v_norm_pre_w, v_conv_b, v_ssm_norm_w, v_norm_post_w, v_dt_bias, v_a_log, v_d_skip))]
    small_out = [_unpack_small(p, 384)[:8] for p in _adamw(*packed, "adamw_small")]

    def ordered(cw_, npre, cb_, nssm, npost, dtb_, alog_, dsk_, big_in, big_out):
        return [npre, big_in[None], cw_, cb_, dtb_, alog_, dsk_, nssm, big_out[None], npost]

    grads = ordered(g_cw, g_npre, g_cb, g_nssm, g_npost, g_dtb, g_alog, g_dsk, g_in, g_out)
    deltas = ordered(*small_out[0], d_in, d_out)
    new_m = ordered(*small_out[1], nm_in, nm_out)
    new_v = ordered(*small_out[2], nv_in, nv_out)
    return (loss, grad_x[None], *grads, *deltas, *new_m, *new_v)


def _local_step(x2, tgt, w_all, late, norm_pre_w, conv_b, dt_bias, a_log, d_skip, ssm_norm_w,
                norm_post_w, reduce=None):
    dtb, alog = _pad_lanes(dt_bias), _pad_lanes(a_log)
    d_b = jnp.repeat(d_skip, 64, axis=1)

    if isinstance(late, _LateGather):
        (proj, u), (gout, gcw) = _inproj_fwd(x2, norm_pre_w, w_all, late)
        w_out_all = gout.reshape(2 * D, D)
        cw_all = jnp.concatenate([gcw[0], gcw[1], gcw[2], gcw[3]], axis=1)
    else:
        proj, u = _inproj_fwd(x2, norm_pre_w, w_all)
        w_out_all, cw_all = late
    mix, attn_pre, lse = _attn_fwd(proj, 1, _attn_fwd(proj, 4, _attn_fwd(proj, 16)), final=True)
    mix, y_save, states, conv_out = _ssm_fwd(proj, mix, cw_all, conv_b, dtb, alog, d_b, ssm_norm_w)

    dy, dn_ssm, do, delta, dg, dw_out, dnw_post, loss_part = _outproj_loss(mix, w_out_all, x2, tgt, norm_post_w,
                                                                          attn_pre, proj)
    dz, dxbcdt, dcw, dcb, dvec, dnw_ssm = _ssm_bwd(proj, dn_ssm, y_save, states, conv_out, cw_all, dtb, alog, d_b,
                                                   ssm_norm_w)
    dw_g, dw_z, dw_x = _dw(u, dg, "dw_in_g"), _dw(u, dz, "dw_in_z"), _dw(u, dxbcdt, "dw_in_xbcdt", X_COLS)
    acc = _attn_bwd(proj, do, lse, delta, 16, None, F32, reduce.pairs(dw_g, dw_z, dw_x, dw_out) if reduce else None)
    if reduce:
        acc, got = acc
    acc = _attn_bwd(proj, do, lse, delta, 4, acc, F32, reduce.first(got) if reduce else None)
    if reduce:
        acc, got = acc
        reduce.first_done(got)
    dq, dk, dv = _attn_bwd(proj, do, lse, delta, 1, acc, BF16)
    dw_q, dw_k = _dw(u, dq, "dw_in_q"), _dw(u, dk, "dw_in_k")
    dw_v = _dw(u, dv, "dw_in_v", hosted=reduce.second_pairs(dw_q, dw_k, dw_g) if reduce else None)
    if reduce:
        dw_v, got = dw_v

    def small(dnw_pre):
        return _pack_small(dcw, dnw_pre, dcb, dnw_ssm, dnw_post, dvec[0:1, :NH], dvec[1:2, :NH], dvec[2:3, :NH],
                           loss_part[:, :1])

    res = _inproj_bwd_dx([dq, dk, dv, dg, dz], dxbcdt, w_all, x2, dy, norm_pre_w,
                         reduce.second(dw_v, got, small(jnp.zeros((1, D), F32))) if reduce else None)
    if reduce:
        res, got = res
        reduce.second_done(got)
        return res
    grad_x, dnw_pre = res
    dw_all = jnp.concatenate([dw_q, dw_k, dw_v, dw_g, dw_z, dw_x], axis=1)
    return grad_x, small(dnw_pre), dw_all, dw_out
```

```python
import functools

import jax
import jax.numpy as jnp
from jax import lax
from jax.experimental import pallas as pl
from jax.experimental.pallas import tpu as pltpu

F32 = jnp.float32
BF16 = jnp.bfloat16
MESH = pl.DeviceIdType.MESH
SDS = jax.ShapeDtypeStruct
ANY = pl.BlockSpec(memory_space=pl.ANY)

S = 4096
D = 1024
DP = 7168
SHARD = 1668
OFF_G, OFF_Z = 3072, 4096
NH = 16
CH = 128
NC = S // CH
EPS = 1e-6
NEG = -1e30
LANE = 128
VMEM_LIMIT = 48 * 1024 * 1024

TILES = SHARD // LANE
WIN = (TILES + 1) * LANE
SHIFT = SHARD - TILES * LANE
SECTION_TILES = {"q": (0, 8), "k": (8, 8), "v": (16, 8), "g": (24, 8), "z": (32, 8), "x": (40, 13)}
X_COLS = SECTION_TILES["x"][1] * LANE

ADAM_LR, ADAM_B1, ADAM_B2, ADAM_EPS, ADAM_WD, ADAM_STEP = 0.001, 0.9, 0.999, 1e-08, 0.01, 10


def _cp(sem, **kw):
    return pltpu.CompilerParams(dimension_semantics=sem, vmem_limit_bytes=VMEM_LIMIT, **kw)


def _dot(a, b):
    return jnp.dot(a, b, preferred_element_type=F32)


def _dot_nt(a, b):
    return lax.dot_general(a, b, (((1,), (1,)), ((), ())), preferred_element_type=F32)


def _dot_tn(a, b):
    return lax.dot_general(a, b, (((0,), (0,)), ((), ())), preferred_element_type=F32)


def _pieces(x, n):
    out = []
    for _ in range(n):
        p = x.astype(BF16)
        out.append(p)
        x = x - p.astype(F32)
    return out


def _pick(x, sel, n=2):
    parts = [_dot(p, sel) for p in _pieces(x, n)]
    return functools.reduce(jnp.add, parts)


def _pick_left(sel, x, n=3):
    parts = [_dot(sel, p) for p in _pieces(x, n)]
    return functools.reduce(jnp.add, parts)


def _sigmoid(v):
    return 0.5 * jnp.tanh(0.5 * v) + 0.5


def _iota(shape, dim):
    return lax.broadcasted_iota(jnp.int32, shape, dim)


def _inproj_fwd(x, nw, w_all, hosted=None):
    tm, tn = 1024, 1024
    n_host = len(hosted.arrays) if hosted else 0

    def body(x_ref, nw_ref, w_ref, *refs):
        host_in, (proj_ref, u_ref), refs = refs[:n_host], refs[n_host:n_host + 2], refs[n_host + 2:]
        host_out, host_sems = refs[:n_host], refs[n_host:]
        i, j = pl.program_id(0), pl.program_id(1)
        if hosted:
            pl.when((i == 0) & (j == 0))(lambda: hosted.start(host_in, host_out, host_sems))

        @pl.when(j == 0)
        def _():
            xf = x_ref[...]
            r = lax.rsqrt(jnp.mean(xf * xf, axis=-1, keepdims=True) + EPS)
            u_ref[...] = (xf * r * nw_ref[...]).astype(BF16)

        proj_ref[...] = _dot(u_ref[...], w_ref[...])
        if hosted:
            pl.when((i == S // tm // 2) & (j == 0))(lambda: hosted.pass_on(host_in, host_out, host_sems))
            pl.when((i == S // tm - 1) & (j == DP // tn - 1))(lambda: hosted.finish(host_in, host_out, host_sems))

    outs = pl.pallas_call(
        body, name="inproj_fwd", grid=(S // tm, DP // tn),
        in_specs=[pl.BlockSpec((tm, D), lambda i, j: (i, 0)), pl.BlockSpec((1, D), lambda i, j: (0, 0)),
                  pl.BlockSpec((D, tn), lambda i, j: (0, j))] + [ANY] * n_host,
        out_specs=[pl.BlockSpec((tm, tn), lambda i, j: (i, j)), pl.BlockSpec((tm, D), lambda i, j: (i, 0))]
        + [ANY] * n_host,
        out_shape=[SDS((S, DP), F32), SDS((S, D), BF16)] + (hosted.out_shape if hosted else []),
        scratch_shapes=hosted.scratch if hosted else [],
        compiler_params=_cp(("arbitrary", "arbitrary") if hosted else ("parallel", "arbitrary")),
    )(x, nw, w_all, *(hosted.arrays if hosted else []))
    return (outs[:2], outs[2:]) if hosted else outs


ATTN_QB = {1: 16, 4: 4, 16: 1}


def _unit_rows(r, u, d):
    return pl.ds(r + d * CH * u, CH, stride=d) if d > 1 else pl.ds(CH * u, CH)


def _for_units(d, qb, fn):
    for r in range(d):
        for u in range(qb):
            fn(r, u)


def _attn_mask(has_prev):
    qi, kj = _iota((2 * CH, 2 * CH), 0) & (CH - 1), _iota((2 * CH, 2 * CH), 1)
    cur_ok = (kj >= CH) & (kj - CH <= qi)
    prev_ok = (kj < CH) & (kj >= qi)
    return cur_ok | (prev_ok & has_prev)


def _stack_heads(v, lane_a):
    return jnp.concatenate([jnp.where(lane_a, v, 0.0), jnp.where(lane_a, 0.0, v)], axis=0).astype(BF16)


def _attn_specs(d, qb):
    rows, prows = CH * d * qb, CH * d
    nb = S // rows
    steps = (NH // 2) * nb

    def at(t):
        t = jnp.minimum(t, steps - 1)
        return t % nb, t // nb

    def cur(off):
        return pl.BlockSpec((rows, LANE), lambda t: (at(t)[0], off + at(t)[1]))

    def prev(off):
        return pl.BlockSpec((prows, LANE), lambda t: (jnp.maximum(at(t)[0] * qb - 1, 0), off + at(t)[1]))

    lag = pl.BlockSpec((rows, LANE), lambda t: at(jnp.maximum(t - 1, 0)))
    return nb, steps, cur, prev, lag


def _gather16(src_ref, dense_ref, tmp_ref):
    for a in range(4):
        tmp_ref[...] = src_ref[pl.ds(a, 4 * CH, stride=4), :]
        for b in range(4):
            dense_ref[a + 4 * b] = tmp_ref[pl.ds(b, CH, stride=4), :]


def _scatter16(dense_ref, dst_ref, tmp_ref):
    for a in range(4):
        for b in range(4):
            tmp_ref[pl.ds(b, CH, stride=4), :] = dense_ref[a + 4 * b]
        dst_ref[pl.ds(a, 4 * CH, stride=4), :] = tmp_ref[...]


def _unit_index(r, u, d):
    return (r,) if d == 16 else (_unit_rows(r, u, d), slice(None))


def _unit_kv(p_ref, c_ref, r, u, d):
    prev = p_ref[_unit_index(r, 0, d)] if u == 0 else c_ref[_unit_index(r, u - 1, d)]
    return jnp.concatenate([prev, c_ref[_unit_index(r, u, d)]], axis=0).astype(BF16)


def _dense_scratch(d, n):
    return [pltpu.VMEM((16, CH, LANE), F32)] * n + [pltpu.VMEM((4 * CH, LANE), F32)] if d == 16 else []


def _attn_fwd(proj, d, prior=None, final=False):
    qb = ATTN_QB[d]
    nb, steps, cur, prev, _ = _attn_specs(d, qb)
    n_prior = 2 if prior is not None else 0
    n_in, n_out = 5 + n_prior + final, 2 + final
    assert not (d == 16 and (n_prior or final))

    def body(*refs):
        ins, outs, scratch = refs[:n_in], refs[n_in:n_in + n_out], refs[n_in + n_out:]
        if d == 16:
            tmp_ref = scratch[-1]
            for src, dense in zip(ins, scratch):
                _gather16(src, dense, tmp_ref)
            block_outs, ins, outs = outs, scratch[:n_in], scratch[n_in:n_in + n_out]
        q_ref, kp_ref, kc_ref, vp_ref, vc_ref = ins[:5]
        prior_refs = ins[5:5 + n_prior]
        if final:
            g_ref, (mix_ref, o_ref, l_ref) = ins[-1], outs
        else:
            o_ref, l_ref = outs
        i = pl.program_id(0) % nb
        lane_a = _iota((CH, LANE), 1) < 64
        mask_first, mask_rest = _attn_mask(i > 0), _attn_mask(True)

        def unit(r, u):
            at = _unit_index(r, u, d)
            q2 = _stack_heads(q_ref[at] * 0.125, lane_a)
            k2, v2 = _unit_kv(kp_ref, kc_ref, r, u, d), _unit_kv(vp_ref, vc_ref, r, u, d)
            s = jnp.where(mask_first if u == 0 else mask_rest, _dot_nt(q2, k2), NEG)
            m = jnp.max(s, axis=1, keepdims=True)
            p = jnp.exp(s - m)
            l = jnp.sum(p, axis=1, keepdims=True)
            o2 = _dot(p.astype(BF16), v2) / l
            lse2 = m + jnp.log(l)
            o = jnp.where(lane_a, o2[:CH], o2[CH:])
            lse = jnp.where(lane_a, lse2[:CH], lse2[CH:])
            if n_prior:
                o_a, l_a = prior_refs[0][at], prior_refs[1][at]
                top = jnp.maximum(l_a, lse)
                e_a, e_b = jnp.exp(l_a - top), jnp.exp(lse - top)
                tot = e_a + e_b
                o = (e_a * o_a + e_b * o) / tot
                lse = top + jnp.log(tot)
            o_ref[at] = o
            l_ref[at] = lse
            if final:
                g = g_ref[at]
                mix_ref[at] = (o * (g * _sigmoid(g))).astype(BF16)

        _for_units(d, qb, unit)
        if d == 16:
            for dense, dst in zip(outs, block_outs):
                _scatter16(dense, dst, tmp_ref)

    in_specs = [cur(0), prev(8), cur(8), prev(16), cur(16)] + [cur(0)] * n_prior
    args = [proj] * 5 + (list(prior) if n_prior else [])
    out_specs, out_shape = [cur(0), cur(0)], [SDS((S, D), F32), SDS((S, D), F32)]
    if final:
        assert d == 1
        in_specs.append(cur(OFF_G // LANE))
        args.append(proj)
        out_specs, out_shape = [cur(0)] + out_specs, [SDS((S, 2 * D), BF16)] + out_shape
    return pl.pallas_call(
        body, name=f"attn_fwd_d{d}", grid=(steps,),
        in_specs=in_specs, out_specs=out_specs, out_shape=out_shape,
        scratch_shapes=_dense_scratch(d, n_in + n_out),
        compiler_params=_cp(("parallel",)),
    )(*args)


def _attn_bwd(proj, do, lse, delta, d, acc, out_dtype, hosted=None):
    qb = ATTN_QB[d]
    nb, steps, cur, prev, lag = _attn_specs(d, qb)
    has_acc = acc is not None
    n_in = 11 if has_acc else 8
    n_host, n_host_out = (len(hosted.arrays), len(hosted.out_shape)) if hosted else (0, 0)
    assert not (d == 16 and (has_acc or out_dtype != F32))
    rows = CH * d * qb
    carry = (2, 16, CH, LANE) if d == 16 else (2, rows, LANE)

    def body(*refs):
        ins, host_in, refs = refs[:n_in], refs[n_in:n_in + n_host], refs[n_in + n_host:]
        (dq_ref, dk_ref, dv_ref), host_out, scratch = refs[:3], refs[3:3 + n_host_out], refs[3 + n_host_out:]
        if hosted:
            scratch, host_sems = scratch[:-len(hosted.scratch)], scratch[-len(hosted.scratch):]
        ck_ref, cv_ref = scratch[:2]
        dq_f32 = dq_ref if out_dtype == F32 else scratch[2]
        t = pl.program_id(0)
        i = t % nb
        if hosted:
            pl.when(t == 0)(lambda: hosted.start(host_in, host_out, host_sems))
        if d == 16:
            dense, dq_f32, tmp_ref = scratch[2:2 + n_in], scratch[2 + n_in], scratch[-1]

            @pl.when(t < steps)
            def _():
                for src, dst in zip(ins, dense):
                    _gather16(src, dst, tmp_ref)

            ins = dense
        q_ref, kp_ref, kc_ref, vp_ref, vc_ref, do_ref, lse_ref, dl_ref = ins[:8]
        if has_acc:
            aq_ref, ak_ref, av_ref = ins[8:11]
        slot = t & 1
        now_k, now_v, old_k, old_v = ck_ref.at[slot], cv_ref.at[slot], ck_ref.at[1 - slot], cv_ref.at[1 - slot]
        lane_a = _iota((CH, LANE), 1) < 64
        mask_first, mask_rest = _attn_mask(i > 0), _attn_mask(True)

        @pl.when(t == 0)
        def _():
            ck_ref[1] = jnp.zeros(carry[1:], F32)
            cv_ref[1] = jnp.zeros(carry[1:], F32)

        def unit(r, u):
            at = _unit_index(r, u, d)
            q2 = _stack_heads(q_ref[at] * 0.125, lane_a)
            do2 = _stack_heads(do_ref[at], lane_a)
            k2, v2 = _unit_kv(kp_ref, kc_ref, r, u, d), _unit_kv(vp_ref, vc_ref, r, u, d)
            lsev, dlv = lse_ref[at], dl_ref[at]
            lse2 = jnp.concatenate([lsev[:, 0:1], lsev[:, 64:65]], axis=0)
            dl2 = jnp.concatenate([dlv[:, 0:1], dlv[:, 64:65]], axis=0)
            p = jnp.exp(jnp.where(mask_first if u == 0 else mask_rest, _dot_nt(q2, k2), NEG) - lse2)
            ds = (p * (_dot_nt(do2, v2) - dl2)).astype(BF16)
            dq2 = _dot(ds, k2)
            dk2 = _dot_tn(ds, q2)
            dv2 = _dot_tn(p.astype(BF16), do2)
            dq = jnp.where(lane_a, dq2[:CH], dq2[CH:]) * 0.125
            if has_acc:
                dq = dq + aq_ref[at]
            dq_f32[at] = dq
            if u == 0:
                before = _unit_index(r, qb - 1, d)
                old_k[before] += dk2[:CH]
                old_v[before] += dv2[:CH]
            else:
                before = _unit_index(r, u - 1, d)
                now_k[before] += dk2[:CH]
                now_v[before] += dv2[:CH]
            now_k[at] = dk2[CH:]
            now_v[at] = dv2[CH:]

        @pl.when(t < steps)
        def _():
            _for_units(d, qb, unit)
            if d == 16:
                _scatter16(dq_f32, dq_ref, tmp_ref)
            elif out_dtype != F32:
                dq_ref[...] = dq_f32[...].astype(out_dtype)

        if d == 16:
            _scatter16(old_k, dk_ref, tmp_ref)
            _scatter16(old_v, dv_ref, tmp_ref)
        else:
            dk, dv = old_k[...], old_v[...]
            if has_acc:
                dk, dv = dk + ak_ref[...], dv + av_ref[...]
            dk_ref[...] = dk.astype(out_dtype)
            dv_ref[...] = dv.astype(out_dtype)
        if hosted:
            pl.when(t == steps)(lambda: hosted.finish(host_in, host_out, host_sems))

    in_specs = [cur(0), prev(8), cur(8), prev(16), cur(16), cur(0), cur(0), cur(0)]
    args = [proj, proj, proj, proj, proj, do, lse, delta]
    if has_acc:
        in_specs += [cur(0), lag, lag]
        args += list(acc)
    scratch = [pltpu.VMEM(carry, F32), pltpu.VMEM(carry, F32)]
    if d == 16:
        scratch += _dense_scratch(d, n_in + 1)
    elif out_dtype != F32:
        scratch.append(pltpu.VMEM((rows, LANE), F32))
    out_specs, out_shape = [cur(0), lag, lag], [SDS((S, D), out_dtype)] * 3
    if hosted:
        args += hosted.arrays
        in_specs += [ANY] * n_host
        out_specs += [ANY] * n_host_out
        out_shape += hosted.out_shape
        scratch += hosted.scratch
    outs = pl.pallas_call(
        body, name=f"attn_bwd_d{d}", grid=(steps + 1,),
        in_specs=in_specs, out_specs=out_specs, out_shape=out_shape,
        scratch_shapes=scratch, compiler_params=_cp(("arbitrary",)),
    )(*args)
    return (outs[:3], outs[3:]) if hosted else outs


def _conv_taps(cur, prev8, first):
    row8 = _iota(prev8.shape, 0)
    prev8 = jnp.where(first, 0.0, prev8)
    taps = []
    for s in (3, 2, 1):
        rolled = pltpu.roll(cur, s, 0)
        head = jnp.where(row8 < s, pltpu.roll(prev8, s, 0), rolled[:8])
        taps.append(jnp.concatenate([head, rolled[8:]], axis=0))
    return taps + [cur]


def _conv(taps, w, b):
    acc = b + w[0:1, :] * taps[0]
    for k in (1, 2, 3):
        acc = acc + w[k:k + 1, :] * taps[k]
    return acc


def _expand():
    return (_iota((LANE, D), 1) // 64 == _iota((LANE, D), 0)).astype(BF16)


def _reduce():
    return (_iota((D, LANE), 0) // 64 == _iota((D, LANE), 1)).astype(BF16)


def _ssd_common(xs_c, bc_c, dt_raw, dtb, alog):
    head_lane = _iota((CH, LANE), 1) < NH
    xs = xs_c * _sigmoid(xs_c)
    bc = bc_c * _sigmoid(bc_c)
    pre = dt_raw + dtb
    dt = jnp.where(head_lane, jnp.maximum(pre, 0.0) + jnp.log(1.0 + jnp.exp(-jnp.abs(pre))), 0.0)
    a_row = jnp.where(head_lane[0:1], -jnp.exp(alog), 0.0)
    tri = (_iota((CH, CH), 1) <= _iota((CH, CH), 0)).astype(BF16)
    cs = _pick_left(tri, dt * a_row)
    cs_last = cs[CH - 1:CH, :]
    wide = _pick(jnp.concatenate([dt, jnp.exp(cs), jnp.exp(cs_last - cs)], axis=0), _expand())
    dt_b, e_b, f_b = wide[:CH], wide[CH:2 * CH], wide[2 * CH:]
    return dict(xs=xs, bc=bc, pre=pre, dt=dt, a_row=a_row, cs=cs, cs_t=cs.T, dt_b=dt_b, e_b=e_b, f_b=f_b,
                t_b=e_b[CH - 1:CH, :])


def _groups(bc):
    bcb = bc.astype(BF16)
    return [bcb[:, 0:128], bcb[:, 128:256]], [bcb[:, 256:384], bcb[:, 384:512]]


def _decay(q, h, tril):
    seg = q["cs"][:, h:h + 1] - q["cs_t"][h:h + 1, :]
    return jnp.exp(jnp.where(tril, seg, NEG))


def _ssm_fwd(proj, mix, cw, cb, dtb, alog, d_b, nw):
    def body(xs_ref, xsp_ref, bc_ref, bcp_ref, dt_ref, z_ref, cw_ref, cb_ref, dtb_ref, alog_ref, db_ref, nw_ref,
             mix_in_ref, mix_ref, y_ref, st_ref, conv_ref, h_ref):
        del mix_in_ref
        i = pl.program_id(0)

        @pl.when(i == 0)
        def _():
            h_ref[...] = jnp.zeros_like(h_ref)

        cw, cb = cw_ref[...], cb_ref[...]
        xs_c = _conv(_conv_taps(xs_ref[...], xsp_ref[...], i == 0), cw[:, :D], cb[:, :D])
        bc_c = _conv(_conv_taps(bc_ref[...], bcp_ref[...], i == 0), cw[:, D:], cb[:, D:])
        conv_ref[:, :D] = xs_c
        conv_ref[:, D:] = bc_c
        q = _ssd_common(xs_c, bc_c, dt_ref[...], dtb_ref[...], alog_ref[...])
        bg, cg = _groups(q["bc"])
        xs = q["xs"]
        xdt = xs * q["dt_b"]
        xdt_b = xdt.astype(BF16)
        h_in = h_ref[...]
        st_ref[...] = h_in
        hb = h_in.astype(BF16)
        tril = _iota((CH, CH), 1) <= _iota((CH, CH), 0)
        lane_a = _iota((CH, LANE), 1) < 64
        cbm = [_dot_nt(cg[g], bg[g]) for g in range(2)]
        pairs = []
        for hp in range(NH // 2):
            xp = xdt_b[:, hp * LANE:(hp + 1) * LANE]
            ya = _dot((cbm[hp // 4] * _decay(q, 2 * hp, tril)).astype(BF16), xp)
            yb = _dot((cbm[hp // 4] * _decay(q, 2 * hp + 1, tril)).astype(BF16), xp)
            pairs.append(jnp.where(lane_a, ya, yb))
        y_diag = jnp.concatenate(pairs, axis=1)
        y_off = jnp.concatenate([_dot(cg[g], hb[:, g * 512:(g + 1) * 512]) for g in range(2)], axis=1) * q["e_b"]
        y = y_diag + y_off + db_ref[...] * xs
        y_ref[...] = y
        xf = (xdt * q["f_b"]).astype(BF16)
        h_ref[...] = q["t_b"] * h_in + jnp.concatenate(
            [_dot_tn(bg[g], xf[:, g * 512:(g + 1) * 512]) for g in range(2)], axis=1)
        z = z_ref[...]
        yz = y * (z * _sigmoid(z))
        outs = []
        for g in range(2):
            v = yz[:, g * 512:(g + 1) * 512]
            outs.append(v * lax.rsqrt(jnp.mean(v * v, axis=-1, keepdims=True) + EPS))
        mix_ref[...] = (jnp.concatenate(outs, axis=1) * nw_ref[...]).astype(BF16)

    def col(width, blk, prev=False):
        if prev:
            return pl.BlockSpec((8, width), lambda i: (jnp.maximum(i * (CH // 8) - 1, 0), blk))
        return pl.BlockSpec((CH, width), lambda i: (i, blk))

    def full(a):
        return pl.BlockSpec(a.shape, lambda i: (0,) * a.ndim)

    return pl.pallas_call(
        body, name="ssm_fwd", grid=(NC,),
        in_specs=[col(D, 5), col(D, 5, True), col(512, 12), col(512, 12, True), col(LANE, 52), col(D, 4),
                  full(cw), full(cb), full(dtb), full(alog), full(d_b), full(nw), ANY],
        out_specs=[col(D, 1), col(D, 0), pl.BlockSpec((None, CH, D), lambda i: (i, 0, 0)), col(D + 512, 0)],
        out_shape=[SDS((S, 2 * D), BF16), SDS((S, D), F32), SDS((NC, CH, D), F32), SDS((S, D + 512), F32)],
        scratch_shapes=[pltpu.VMEM((CH, D), F32)],
        input_output_aliases={12: 0},
        compiler_params=_cp(("arbitrary",)),
    )(proj, proj, proj, proj, proj, proj, cw, cb, dtb, alog, d_b, nw, mix)


def _ssm_bwd(proj, dn, y_save, states, conv_out, cw, dtb, alog, d_b, nw):
    def body(xs_ref, bc_ref, dt_ref, z_ref, dn_ref, y_ref, st_ref, conv_ref,
             cw_ref, dtb_ref, alog_ref, db_ref, nw_ref,
             dz_ref, dx_ref, dcw_ref, dcb_ref, dsm_ref, dnw_ref, dh_ref, nxs_ref, nbc_ref):
        i = pl.program_id(0)
        ci = NC - 1 - i

        @pl.when(i == 0)
        def _():
            for ref in (dcw_ref, dcb_ref, dsm_ref, dnw_ref, dh_ref, nxs_ref, nbc_ref):
                ref[...] = jnp.zeros_like(ref)

        cw = cw_ref[...]
        xs_c, bc_c = conv_ref[:, :D], conv_ref[:, D:]
        q = _ssd_common(xs_c, bc_c, dt_ref[...], dtb_ref[...], alog_ref[...])
        bg, cg = _groups(q["bc"])
        xs, dt_b, e_b, f_b, t_b = q["xs"], q["dt_b"], q["e_b"], q["f_b"], q["t_b"]
        xdt = xs * dt_b
        xdt_b = xdt.astype(BF16)
        h_in = st_ref[...]
        hb = h_in.astype(BF16)
        dh_new = dh_ref[...]
        dhb = dh_new.astype(BF16)
        red = _reduce()

        z, y, dn, nw_v = z_ref[...], y_ref[...], dn_ref[...], nw_ref[...]
        sig = _sigmoid(z)
        sz = z * sig
        yz = y * sz
        gdn = dn * nw_v
        dyz, dnw = [], []
        for g in range(2):
            v, gv = yz[:, g * 512:(g + 1) * 512], gdn[:, g * 512:(g + 1) * 512]
            r = lax.rsqrt(jnp.mean(v * v, axis=-1, keepdims=True) + EPS)
            dnw.append(dn[:, g * 512:(g + 1) * 512] * v * r)
            dyz.append(r * (gv - v * (r * r) * jnp.mean(gv * v, axis=-1, keepdims=True)))
        dyz = jnp.concatenate(dyz, axis=1)
        dnw_ref[...] += jnp.sum(jnp.concatenate(dnw, axis=1), axis=0, keepdims=True)
        dy = dyz * sz
        dz_ref[...] = (dyz * y * (sig * (1.0 + z * (1.0 - sig)))).astype(BF16)
        dy_b = dy.astype(BF16)

        tril = _iota((CH, CH), 1) <= _iota((CH, CH), 0)
        lane_a = _iota((CH, LANE), 1) < 64
        cbm = [_dot_nt(cg[g], bg[g]) for g in range(2)]
        dcbm = [jnp.zeros((CH, CH), F32), jnp.zeros((CH, CH), F32)]
        seg_rows = jnp.zeros((CH, LANE), F32)
        seg_cols = jnp.zeros((LANE, CH), F32)
        row_id, col_id = _iota((CH, LANE), 0), _iota((CH, LANE), 1)
        dx_pairs = []
        for hp in range(NH // 2):
            g = hp // 4
            xp = xdt_b[:, hp * LANE:(hp + 1) * LANE]
            dyp_f = dy[:, hp * LANE:(hp + 1) * LANE]
            dyp = dy_b[:, hp * LANE:(hp + 1) * LANE]
            halves = []
            for k in range(2):
                h = 2 * hp + k
                lane = lane_a if k == 0 else jnp.logical_not(lane_a)
                dec = _decay(q, h, tril)
                gm = cbm[g] * dec
                dgm = _dot_nt(jnp.where(lane, dyp_f, 0.0).astype(BF16), xp)
                dcbm[g] = dcbm[g] + dgm * dec
                prod = dgm * gm
                seg_rows = jnp.where(col_id == h, jnp.sum(prod, axis=1, keepdims=True), seg_rows)
                seg_cols = jnp.where(row_id == h, jnp.sum(prod, axis=0, keepdims=True), seg_cols)
                halves.append(_dot_tn(gm.astype(BF16), dyp))
            dx_pairs.append(jnp.where(lane_a, halves[0], halves[1]))
        dxdt_diag = jnp.concatenate(dx_pairs, axis=1)

        qv = jnp.concatenate([_dot(bg[g], dhb[:, g * 512:(g + 1) * 512]) for g in range(2)], axis=1)
        y_off = jnp.concatenate([_dot(cg[g], hb[:, g * 512:(g + 1) * 512]) for g in range(2)], axis=1) * e_b
        xfq = xdt * f_b * qv
        dxdt = dxdt_diag + f_b * qv
        tdt = jnp.sum(dh_new * h_in, axis=0, keepdims=True) * t_b
        per_head = _pick(jnp.concatenate([xfq, dy * y_off, dxdt * xs, dy * xs, jnp.broadcast_to(tdt, (8, D))],
                                         axis=0), red)
        fdf, dyoff_h, dxdtxs_h, dyxs_h = [per_head[k * CH:(k + 1) * CH] for k in range(4)]
        dcs = seg_rows - seg_cols.T + dyoff_h - fdf
        last = per_head[4 * CH:4 * CH + 1] + jnp.sum(fdf, axis=0, keepdims=True)
        dcs = dcs + jnp.where(_iota((CH, LANE), 0) == CH - 1, last, 0.0)
        tri_t = (_iota((CH, CH), 1) >= _iota((CH, CH), 0)).astype(BF16)
        da = _pick_left(tri_t, dcs)
        ddt = da * q["a_row"] + dxdtxs_h
        dxs = dxdt * dt_b + db_ref[...] * dy
        ddt_raw = ddt * _sigmoid(q["pre"])
        dsm_ref[0:1, :] += jnp.sum(ddt_raw, axis=0, keepdims=True)
        dsm_ref[1:2, :] += jnp.sum(da * q["dt"], axis=0, keepdims=True) * q["a_row"]
        dsm_ref[2:3, :] += jnp.sum(dyxs_h, axis=0, keepdims=True)
        edy = (e_b * dy).astype(BF16)
        xf = (xdt * f_b).astype(BF16)
        dbs, dcs_g, dhs = [], [], []
        for g in range(2):
            sl = slice(g * 512, (g + 1) * 512)
            dcb_b = dcbm[g].astype(BF16)
            dcs_g.append(_dot(dcb_b, bg[g]) + _dot_nt(edy[:, sl], hb[:, sl]))
            dbs.append(_dot_tn(dcb_b, cg[g]) + _dot_nt(xf[:, sl], dhb[:, sl]))
            dhs.append(_dot_tn(cg[g], edy[:, sl]))
        dh_ref[...] = t_b * dh_new + jnp.concatenate(dhs, axis=1)
        dbc = jnp.concatenate(dbs + dcs_g, axis=1)

        def conv_bwd(dact, pre, x_raw, w, nxt_ref, lo):
            s = _sigmoid(pre)
            dconv = dact * (s * (1.0 + pre * (1.0 - s)))
            nxt8 = nxt_ref[...]
            row8 = _iota(nxt8.shape, 0)
            hi = lo + dconv.shape[1]
            dcb_ref[:, lo:hi] += jnp.sum(dconv, axis=0, keepdims=True)
            later = [dconv]
            for s_ in (1, 2, 3):
                rolled = pltpu.roll(dconv, CH - s_, 0)
                tail = jnp.where(row8 >= 8 - s_, pltpu.roll(nxt8, 8 - s_, 0), rolled[CH - 8:])
                later.append(jnp.concatenate([rolled[:CH - 8], tail], axis=0))
            dx = None
            for s_, up in enumerate(later):
                k = 3 - s_
                dcw_ref[k:k + 1, lo:hi] += jnp.sum(up * x_raw, axis=0, keepdims=True)
                dx = w[k:k + 1, :] * up if dx is None else dx + w[k:k + 1, :] * up
            nxt_ref[...] = dconv[:8]
            return dx

        dx_ref[:, 0:D] = conv_bwd(dxs, xs_c, xs_ref[...], cw[:, :D], nxs_ref, 0).astype(BF16)
        dx_ref[:, D:D + 512] = conv_bwd(dbc, bc_c, bc_ref[...], cw[:, D:], nbc_ref, D).astype(BF16)
        dx_ref[:, D + 512:D + 640] = ddt_raw.astype(BF16)
        dx_ref[:, D + 640:] = jnp.zeros((CH, D - 640), BF16)

    def col(width, blk):
        return pl.BlockSpec((CH, width), lambda i: (NC - 1 - i, blk))

    def full(a):
        return pl.BlockSpec(a.shape, lambda i: (0,) * len(a.shape))

    acc_shapes = [SDS((4, 1536), F32), SDS((1, 1536), F32), SDS((8, LANE), F32), SDS((1, D), F32)]
    return pl.pallas_call(
        body, name="ssm_bwd", grid=(NC,),
        in_specs=[col(D, 5), col(512, 12), col(LANE, 52), col(D, 4),
                  col(D, 0), col(D, 0), pl.BlockSpec((None, CH, D), lambda i: (NC - 1 - i, 0, 0)), col(D + 512, 0),
                  full(cw), full(dtb), full(alog), full(d_b), full(nw)],
        out_specs=[col(D, 0), col(2 * D, 0)] + [full(a) for a in acc_shapes],
        out_shape=[SDS((S, D), BF16), SDS((S, 2 * D), BF16)] + acc_shapes,
        scratch_shapes=[pltpu.VMEM((CH, D), F32), pltpu.VMEM((8, D), F32), pltpu.VMEM((8, 512), F32)],
        compiler_params=_cp(("arbitrary",)),
    )(proj, proj, proj, proj, dn, y_save, states, conv_out, cw, dtb, alog, d_b, nw)


def _outproj_loss(mix, w_out, x, tgt, nw, attn_pre, proj):
    tm = 256

    def body(mix_ref, w_ref, x_ref, t_ref, nw_ref, pre_ref, g_ref,
             dy_ref, dn_ref, do_ref, delta_ref, dg_ref, dw_ref, dnw_ref, loss_ref):
        @pl.when(pl.program_id(0) == 0)
        def _():
            dw_ref[...] = jnp.zeros_like(dw_ref)
            dnw_ref[...] = jnp.zeros_like(dnw_ref)
            loss_ref[...] = jnp.zeros_like(loss_ref)

        mixv, w = mix_ref[...], w_ref[...]
        out = _dot(mixv, w)
        r = lax.rsqrt(jnp.mean(out * out, axis=-1, keepdims=True) + EPS)
        nh = out * r
        nw_v = nw_ref[...]
        err = x_ref[...] + nh * nw_v - t_ref[...]
        loss_ref[...] += 0.5 * jnp.sum(jnp.mean(err * err, axis=-1, keepdims=True), axis=0, keepdims=True)
        dy = err * (1.0 / D)
        dy_ref[...] = dy
        dnw_ref[...] += jnp.sum(dy * nh, axis=0, keepdims=True)
        gdn = dy * nw_v
        dout = (r * (gdn - nh * jnp.mean(gdn * nh, axis=-1, keepdims=True))).astype(BF16)
        dmix = _dot_nt(dout, w)
        dw_ref[...] += _dot_tn(mixv, dout)
        dn_ref[...] = dmix[:, D:]
        dm, g, pre_v = dmix[:, :D], g_ref[...], pre_ref[...]
        sig = _sigmoid(g)
        do = dm * (g * sig)
        do_ref[...] = do
        dg_ref[...] = (dm * pre_v * (sig * (1.0 + g * (1.0 - sig)))).astype(BF16)
        prod = do * pre_v
        same_head = (_iota((LANE, LANE), 0) // 64 == _iota((LANE, LANE), 1) // 64).astype(BF16)
        for cb in range(D // LANE):
            delta_ref[:, cb * LANE:(cb + 1) * LANE] = _pick(prod[:, cb * LANE:(cb + 1) * LANE], same_head)

    row = lambda w: pl.BlockSpec((tm, w), lambda i: (i, 0))
    full = lambda s: pl.BlockSpec(s, lambda i: (0, 0))
    return pl.pallas_call(
        body, name="outproj_loss", grid=(S // tm,),
        in_specs=[row(2 * D), full((2 * D, D)), row(D), row(D), full((1, D)), row(D),
                  pl.BlockSpec((tm, D), lambda i: (i, OFF_G // D))],
        out_specs=[row(D), row(D), row(D), row(D), row(D), full((2 * D, D)), full((1, D)), full((1, LANE))],
        out_shape=[SDS((S, D), F32)] * 4 + [SDS((S, D), BF16), SDS((2 * D, D), F32), SDS((1, D), F32),
                                            SDS((1, LANE), F32)],
        compiler_params=_cp(("arbitrary",)),
    )(mix, w_out, x, tgt, nw, attn_pre, proj)


def _inproj_bwd_dx(srcs, dxbcdt, w_all, x, dy, nw, hosted=None):
    tm = 512
    nk = DP // D
    n_host, n_host_out = (len(hosted.arrays), len(hosted.out_shape)) if hosted else (0, 0)

    def body(*refs):
        src_refs = refs[:nk]
        w_ref, x_ref, dy_ref, nw_ref = refs[nk:nk + 4]
        host_in, refs = refs[nk + 4:nk + 4 + n_host], refs[nk + 4 + n_host:]
        gx_ref, dnw_ref = refs[:2]
        host_out, host_sems = refs[2:2 + n_host_out], refs[2 + n_host_out:]
        i = pl.program_id(0)

        @pl.when(i == 0)
        def _():
            if hosted:
                hosted.start(host_in, host_out, host_sems)
            dnw_ref[...] = jnp.zeros_like(dnw_ref)

        du = None
        for k, ref in enumerate(src_refs):
            width = min(D, 5 * D + X_COLS - k * D)
            part = _dot_nt(ref[:, :width], w_ref[:, k * D:k * D + width])
            du = part if du is None else du + part
        xf, nw_v = x_ref[...], nw_ref[...]
        r = lax.rsqrt(jnp.mean(xf * xf, axis=-1, keepdims=True) + EPS)
        xh = xf * r
        dnw_ref[...] += jnp.sum(du * xh, axis=0, keepdims=True)
        gdu = du * nw_v
        gx_ref[...] = r * (gdu - xh * jnp.mean(gdu * xh, axis=-1, keepdims=True)) + dy_ref[...]

        if hosted:
            pl.when(i == S // tm - 1)(lambda: hosted.finish(host_in, host_out, host_sems))

    row = pl.BlockSpec((tm, D), lambda i: (i, 0))
    row1 = pl.BlockSpec((tm, D), lambda i: (i, 1))
    one = pl.BlockSpec((1, D), lambda i: (0, 0))
    whole_w = pl.BlockSpec((D, DP), lambda i: (0, 0), pipeline_mode=pl.Buffered(1))
    args = [*srcs, dxbcdt, dxbcdt, w_all, x, dy, nw]
    in_specs = [row] * len(srcs) + [row, row1, whole_w, row, row, one]
    out_specs, out_shape, scratch = [row, one], [SDS((S, D), F32), SDS((1, D), F32)], []
    if hosted:
        args += hosted.arrays
        in_specs += [ANY] * n_host
        out_specs += [ANY] * n_host_out
        out_shape += hosted.out_shape
        scratch += hosted.scratch
    outs = pl.pallas_call(
        body, name="inproj_bwd_dx", grid=(S // tm,),
        in_specs=in_specs, out_specs=out_specs, out_shape=out_shape, scratch_shapes=scratch,
        compiler_params=_cp(("arbitrary",)),
    )(*args)
    return (outs[:2], outs[2:]) if hosted else outs


def _dw(u, dsec, name, width=D, hosted=None):
    ts = 1024
    n_host, n_host_out = (len(hosted.arrays), len(hosted.out_shape)) if hosted else (0, 0)

    def body(u_ref, d_ref, *refs):
        host_in, o_ref, refs = refs[:n_host], refs[n_host], refs[n_host + 1:]
        host_out, host_sems = refs[:n_host_out], refs[n_host_out:]
        i = pl.program_id(0)

        @pl.when(i == 0)
        def _():
            if hosted:
                hosted.start(host_in, host_out, host_sems)
            o_ref[...] = jnp.zeros_like(o_ref)

        o_ref[...] += _dot_tn(u_ref[...], d_ref[...])
        if hosted:
            pl.when(i == S // ts - 1)(lambda: hosted.finish(host_in, host_out, host_sems))

    outs = pl.pallas_call(
        body, name=name, grid=(S // ts,),
        in_specs=[pl.BlockSpec((ts, D), lambda i: (i, 0)), pl.BlockSpec((ts, width), lambda i: (i, 0))]
        + [ANY] * n_host,
        out_specs=[pl.BlockSpec((D, width), lambda i: (0, 0))] + [ANY] * n_host_out,
        out_shape=[SDS((D, width), F32)] + (hosted.out_shape if hosted else []),
        scratch_shapes=hosted.scratch if hosted else [],
        compiler_params=_cp(("arbitrary",)),
    )(u, dsec, *(hosted.arrays if hosted else []))
    return (outs[0], outs[1:]) if hosted else outs[0]


def _place():
    x, y, c = lax.axis_index("x"), lax.axis_index("y"), lax.axis_index("c")
    return x, y, c, 2 * x + y


def _chip_of(x, y, k):
    px = 1 - x if k & 2 else x
    py = 1 - y if k & 1 else y
    return px, py, 2 * px + py


def _remote(src, dst, send_sem, recv_sem, dev):
    return pltpu.make_async_remote_copy(src_ref=src, dst_ref=dst, send_sem=send_sem, recv_sem=recv_sem,
                                        device_id=dev, device_id_type=MESH)


def _gather_weights(w_in_b):
    half = w_in_b.shape[0] // 2
    quarter = half // 2

    def body(src, dst, send, recv):
        x, y, c, j = _place()
        me, sib = (x, y, c), (x, y, 1 - c)
        nbr = {"x": _chip_of(x, y, 2), "y": _chip_of(x, y, 1)}
        diag = _chip_of(x, y, 3)[2]
        started, arrivals = [], []

        def rows(n_quarter=None, sibling=False):
            base = (1 - c if sibling else c) * half
            return pl.ds(base, half) if n_quarter is None else pl.ds(base + n_quarter * quarter, quarter)

        def sem(n):
            return send.at[n], recv.at[n]

        def go(cp):
            cp.start()
            started.append(cp)

        own = _remote(src, dst.at[j], *sem(8), sib)
        go(own)
        for n, axis in enumerate("xy"):
            px, py, _ = nbr[axis]
            go(_remote(src.at[rows()], dst.at[j, rows()], *sem(n), (px, py, c)))
        for n, axis in enumerate("xy"):
            ox, oy, _ = nbr["y" if axis == "x" else "x"]
            pj = nbr[axis][2]
            _remote(src.at[rows()], dst.at[pj, rows()], *sem(n), me).wait_recv()
            go(_remote(dst.at[pj, rows(n)], dst.at[pj, rows(n)], *sem(2 + n), (ox, oy, c)))
            go(_remote(dst.at[pj, rows()], dst.at[pj, rows()], *sem(4 + n), sib))
            arrivals.append(_remote(src.at[rows()], dst.at[pj, rows(None, True)], *sem(4 + n), me))
        for n in range(2):
            _remote(dst.at[diag, rows(n)], dst.at[diag, rows(n)], *sem(2 + n), me).wait_recv()
            go(_remote(dst.at[diag, rows(n)], dst.at[diag, rows(n)], *sem(6 + n), sib))
            arrivals.append(_remote(dst.at[diag, rows(n, True)], dst.at[diag, rows(n, True)], *sem(6 + n), me))
        for cp in arrivals + [own]:
            cp.wait_recv()
        for cp in started:
            cp.wait_send()

    return pl.pallas_call(
        body, name="gather_weights", in_specs=[ANY], out_specs=ANY,
        out_shape=SDS((4,) + w_in_b.shape, BF16),
        scratch_shapes=[pltpu.SemaphoreType.DMA((9,)), pltpu.SemaphoreType.DMA((9,))],
        compiler_params=pltpu.CompilerParams(has_side_effects=True),
    )(w_in_b)


class _LateGather:
    def __init__(self, w_out_b, conv_w):
        self.arrays = [w_out_b, conv_w]
        self.out_shape = [SDS((4,) + w_out_b.shape, BF16), SDS((4,) + conv_w.shape, F32)]
        self.scratch = [pltpu.SemaphoreType.DMA((11,)), pltpu.SemaphoreType.DMA((11,))]

    def _plan(self, ins, outs, sems):
        x, y, c, j = _place()
        send, recv = sems
        (wo, cw), (gwo, gcw) = ins, outs
        half = wo.shape[0] // 2
        mine, theirs = pl.ds(c * half, half), pl.ds((1 - c) * half, half)
        me, sib = (x, y, c), (x, y, 1 - c)
        first, arrive, forward, last = [], [], [], []
        for k in (1, 2, 3):
            px, py, pj = _chip_of(x, y, k)
            first += [_remote(wo.at[mine], gwo.at[j, mine], send.at[k - 1], recv.at[k - 1], (px, py, c)),
                      _remote(cw, gcw.at[j], send.at[k + 2], recv.at[k + 2], (px, py, c))]
            arrive.append(_remote(wo.at[mine], gwo.at[pj, mine], send.at[k - 1], recv.at[k - 1], me))
            forward.append(_remote(gwo.at[pj, mine], gwo.at[pj, mine], send.at[k + 5], recv.at[k + 5], sib))
            last += [_remote(cw, gcw.at[pj], send.at[k + 2], recv.at[k + 2], me),
                     _remote(wo.at[theirs], gwo.at[pj, theirs], send.at[k + 5], recv.at[k + 5], me)]
        first += [_remote(wo, gwo.at[j], send.at[9], recv.at[9], sib),
                  _remote(cw, gcw.at[j], send.at[10], recv.at[10], sib)]
        last += first[-2:]
        return first, arrive, forward, last

    def start(self, ins, outs, sems):
        for cp in self._plan(ins, outs, sems)[0]:
            cp.start()

    def pass_on(self, ins, outs, sems):
        _, arrive, forward, _ = self._plan(ins, outs, sems)
        for got, fwd in zip(arrive, forward):
            got.wait_recv()
            fwd.start()

    def finish(self, ins, outs, sems):
        first, _, forward, last = self._plan(ins, outs, sems)
        for cp in last:
            cp.wait_recv()
        for cp in first + forward:
            cp.wait_send()


def _window(s, names):
    lo, hi = TILES * s, TILES * s + TILES + 1
    pieces = []
    for n, name in enumerate(names):
        a, count = SECTION_TILES[name]
        first, last = max(lo, a), min(hi, a + count)
        if first < last:
            pieces.append((n, first - a, last - first, first - lo))
    assert sum(p[2] for p in pieces) == TILES + 1
    return pieces


class _PairExchange:
    def __init__(self, names, sections, shards, more=()):
        self.names, self.shards = names, shards
        self.there = [n for n, a in enumerate(sections) if a is not None]
        self.arrays = [sections[n] for n in self.there] + list(more)
        self.out_shape = [SDS((len(shards), D // 2, WIN), F32)]
        self.out_shape += [SDS((a.shape[0], a.shape[1] // 2, a.shape[2]), F32) for a in more]
        n = sum(p[0] in self.there for s in shards for p in _window(s, names)) + len(more)
        self.scratch = [pltpu.SemaphoreType.DMA((n,)) for _ in range(2)]

    def _copies(self, ins, outs, sems):
        x, y, c, _ = _place()
        sib = (x, y, 1 - c)
        rows = pl.ds((1 - c) * (D // 2), D // 2)
        k = 0
        for i, s in enumerate(self.shards):
            for n, tile, tiles, at in _window(s, self.names):
                if n in self.there:
                    yield _remote(ins[self.there.index(n)].at[rows, pl.ds(tile * LANE, tiles * LANE)],
                                  outs[0].at[i, :, pl.ds(at * LANE, tiles * LANE)], sems[0].at[k], sems[1].at[k], sib)
                    k += 1
        for src, dst in zip(ins[len(self.there):], outs[1:]):
            half = src.shape[1] // 2
            yield _remote(src.at[:, pl.ds((1 - c) * half, half)], dst, sems[0].at[k], sems[1].at[k], sib)
            k += 1

    def start(self, ins, outs, sems):
        for cp in self._copies(ins, outs, sems):
            cp.start()

    def finish(self, ins, outs, sems):
        for cp in self._copies(ins, outs, sems):
            cp.wait()


def _exchange_call(exchange, name, into=None):
    n, n_out = len(exchange.arrays), len(exchange.out_shape)
    given = list(into) if into else []

    def body(*refs):
        ins, outs, sems = refs[:n], refs[n + len(given):n + len(given) + n_out], refs[n + len(given) + n_out:]
        exchange.start(ins, outs, sems)
        exchange.finish(ins, outs, sems)

    return pl.pallas_call(
        body, name=name, in_specs=[ANY] * (n + len(given)), out_specs=[ANY] * n_out, out_shape=exchange.out_shape,
        input_output_aliases={n + k: k for k in range(len(given))},
        scratch_shapes=exchange.scratch, compiler_params=pltpu.CompilerParams(has_side_effects=True),
    )(*exchange.arrays, *given)


def _pair_sum_windows(cidx, names, sections, shards, r, name):
    n, half, _ = r.shape
    tr = min(half, 256)
    nt = half // tr

    def body(c_ref, *refs):
        del c_ref
        secs, r_ref, o_ref = refs[:-2], refs[-2], refs[-1]
        for i, s in enumerate(shards):
            for k, tile, tiles, at in _window(s, names):
                own = secs[k][:, tile * LANE:(tile + tiles) * LANE]
                there = slice(at * LANE, (at + tiles) * LANE)
                o_ref[i, :, there] = (own + r_ref[i, :, there]).astype(BF16)

    window = pl.BlockSpec((n, tr, WIN), lambda t, c: (0, t, 0))
    return pl.pallas_call(
        body, name=name,
        grid_spec=pltpu.PrefetchScalarGridSpec(
            num_scalar_prefetch=1, grid=(nt,),
            in_specs=[pl.BlockSpec((tr, a.shape[1]), lambda t, c: (c[0] * nt + t, 0)) for a in sections] + [window],
            out_specs=window),
        out_shape=SDS(r.shape, BF16),
        compiler_params=_cp(("parallel",)),
    )(cidx, *sections, r)


def _pair_sum(cidx, g, r, name):
    n, half, width = r.shape
    tr = min(half, 256)
    nt = half // tr

    def body(c_ref, g_ref, r_ref, o_ref):
        del c_ref
        o_ref[...] = (g_ref[...] + r_ref[...]).astype(BF16)

    return pl.pallas_call(
        body, name=name,
        grid_spec=pltpu.PrefetchScalarGridSpec(
            num_scalar_prefetch=1, grid=(n, nt),
            in_specs=[pl.BlockSpec((None, tr, width), lambda s, t, c: (s, c[0] * nt + t, 0)),
                      pl.BlockSpec((None, tr, width), lambda s, t, c: (s, t, 0))],
            out_specs=pl.BlockSpec((None, tr, width), lambda s, t, c: (s, t, 0))),
        out_shape=SDS(r.shape, BF16),
        compiler_params=_cp(("parallel", "parallel")),
    )(cidx, g, r)


class _ChipExchange:
    def __init__(self, arrays, rows):
        self.arrays, self.rows = list(arrays), list(rows)
        self.out_shape = [SDS((4,) + a.shape[1:], BF16) for a in self.arrays]
        self.scratch = [pltpu.SemaphoreType.DMA((3 * len(self.arrays),)) for _ in range(2)]

    def _copies(self, ins, outs, sems):
        x, y, c, j = _place()
        send, recv = sems
        for a, (src, dst, row) in enumerate(zip(ins, outs, self.rows)):
            for k in (1, 2, 3):
                px, py, pj = _chip_of(x, y, k)
                n = 3 * a + k - 1
                slot = pj if row is None else py
                yield (None if row is None else px == row, None if row is None else x == row,
                       _remote(src.at[slot], dst.at[j], send.at[n], recv.at[n], (px, py, c)),
                       _remote(src.at[0], dst.at[pj], send.at[n], recv.at[n], (x, y, c)))

    def start(self, ins, outs, sems):
        for sends, _, send, _ in self._copies(ins, outs, sems):
            if sends is None:
                send.start()
            else:
                pl.when(sends)(send.start)

    def finish(self, ins, outs, sems):
        for sends, owns, send, arrival in self._copies(ins, outs, sems):
            if sends is None:
                arrival.wait_recv()
                send.wait_send()
            else:
                pl.when(owns)(arrival.wait_recv)
                pl.when(sends)(send.wait_send)


def _all_gather_rows(src, dst, rows, send, recv, local_sem):
    x, y, c, j = _place()
    me = 2 * j + c
    local = pltpu.make_async_copy(src, dst.at[me, rows], local_sem)
    cps, arrivals = [], []
    for k in range(1, 8):
        px, py, pj = _chip_of(x, y, k >> 1)
        pc = 1 - c if k & 1 else c
        cps.append(_remote(src, dst.at[me, rows], send.at[k - 1], recv.at[k - 1], (px, py, pc)))
        arrivals.append(_remote(src, dst.at[2 * pj + pc, rows], send.at[k - 1], recv.at[k - 1], (x, y, c)))
    starts = [local.start] + [cp.start for cp in cps]
    waits = [cp.wait_recv for cp in arrivals] + [cp.wait_send for cp in cps] + [local.wait]
    return starts, waits


class _SmallExchange:
    def __init__(self, small):
        self.arrays = [small]
        self.out_shape = [SDS((8,) + small.shape, F32)]
        self.scratch = [pltpu.SemaphoreType.DMA((7,)), pltpu.SemaphoreType.DMA((7,)), pltpu.SemaphoreType.DMA]

    def start(self, ins, outs, sems):
        for go in _all_gather_rows(ins[0], outs[0], slice(None), *sems)[0]:
            go()

    def finish(self, ins, outs, sems):
        for wait in _all_gather_rows(ins[0], outs[0], slice(None), *sems)[1]:
            wait()


class _Both:
    def __init__(self, a, b):
        self.parts = (a, b)
        self.arrays, self.out_shape, self.scratch = a.arrays + b.arrays, a.out_shape + b.out_shape, a.scratch + b.scratch

    def _split(self, ins, outs, sems):
        a, b = self.parts
        return ((a, ins[:len(a.arrays)], outs[:len(a.out_shape)], sems[:len(a.scratch)]),
                (b, ins[len(a.arrays):], outs[len(a.out_shape):], sems[len(a.scratch):]))

    def start(self, ins, outs, sems):
        for part, *refs in self._split(ins, outs, sems):
            part.start(*refs)

    def finish(self, ins, outs, sems):
        for part, *refs in self._split(ins, outs, sems):
            part.finish(*refs)


def _slot_sum(r, name):
    n, rows, width = r.shape
    tr = min(rows, 256)

    def body(r_ref, o_ref):
        acc = r_ref[0].astype(F32)
        for s in range(1, n):
            acc = acc + r_ref[s].astype(F32)
        o_ref[...] = acc

    return pl.pallas_call(
        body, name=name, grid=(rows // tr,),
        in_specs=[pl.BlockSpec((n, tr, width), lambda t: (0, t, 0))],
        out_specs=pl.BlockSpec((tr, width), lambda t: (t, 0)),
        out_shape=SDS((rows, width), F32),
        compiler_params=_cp(("parallel",)),
    )(r)


def _chip_sum(where, recv, own, name):
    n, rows, width = recv.shape
    tr = min(rows, 256)
    nt = rows // tr

    def body(j_ref, r_ref, own_ref, o_ref):
        acc = None
        for s in range(n):
            term = jnp.where(j_ref[0] == s, own_ref[...], r_ref[s]).astype(F32)
            acc = term if acc is None else acc + term
        o_ref[...] = acc

    return pl.pallas_call(
        body, name=name,
        grid_spec=pltpu.PrefetchScalarGridSpec(
            num_scalar_prefetch=1, grid=(nt,),
            in_specs=[pl.BlockSpec((n, tr, width), lambda t, j: (0, t, 0)),
                      pl.BlockSpec((None, tr, width), lambda t, j: (j[0], t, 0))],
            out_specs=pl.BlockSpec((tr, width), lambda t, j: (j[1] * nt + t, 0))),
        out_shape=SDS((2 * rows, width), F32),
        compiler_params=_cp(("parallel",)),
    )(where, recv, own)


def _chip_sum_rows(place, recv0, own0, recv1, own1, name):
    n, rows, width = recv0.shape
    tr = min(rows, 256)
    nt = rows // tr

    def body(p_ref, r0_ref, o0_ref, r1_ref, o1_ref, o_ref):
        first_row = p_ref[2] == 0
        own = jnp.where(first_row, o0_ref[...], o1_ref[...])
        acc = None
        for s in range(n):
            term = jnp.where(p_ref[0] == s, own, jnp.where(first_row, r0_ref[s], r1_ref[s])).astype(F32)
            acc = term if acc is None else acc + term
        o_ref[...] = acc

    recv = pl.BlockSpec((n, tr, width), lambda t, p: (0, t, 0))
    own = pl.BlockSpec((None, tr, width), lambda t, p: (p[3], t, 0))
    return pl.pallas_call(
        body, name=name,
        grid_spec=pltpu.PrefetchScalarGridSpec(
            num_scalar_prefetch=1, grid=(nt,), in_specs=[recv, own, recv, own],
            out_specs=pl.BlockSpec((tr, width), lambda t, p: (p[1] * nt + t, 0))),
        out_shape=SDS((2 * rows, width), F32),
        compiler_params=_cp(("parallel",)),
    )(place, recv0, own0, recv1, own1)


def _half_exchange(gw, go, gathered, late, row):
    def body(gw_in, go_in, ga_in, late_ref, gw_ref, go_ref, ga_ref, send, recv, late_send, late_recv, late_local):
        del gw_in, go_in, ga_in
        x, y, c, _ = _place()
        starts, waits = _all_gather_rows(late_ref, ga_ref, pl.ds(row, late.shape[0]), late_send, late_recv,
                                         late_local)
        for go_ in starts:
            go_()
        mine = [pl.ds(c * (r.shape[0] // 2), r.shape[0] // 2) for r in (gw_ref, go_ref)]
        cps = [_remote(r.at[rows], r.at[rows], send.at[k], recv.at[k], (x, y, 1 - c))
               for k, (r, rows) in enumerate(zip((gw_ref, go_ref), mine))]
        for cp in cps:
            cp.start()
        for k, r in enumerate((gw_ref, go_ref)):
            theirs = pl.ds((1 - c) * (r.shape[0] // 2), r.shape[0] // 2)
            _remote(r.at[theirs], r.at[theirs], send.at[k], recv.at[k], (x, y, c)).wait_recv()
        for cp in cps:
            cp.wait_send()
        for wait in waits:
            wait()

    return pl.pallas_call(
        body, name="half_exchange", in_specs=[ANY] * 4, out_specs=[ANY] * 3,
        out_shape=[SDS(gw.shape, F32), SDS(go.shape, F32), SDS(gathered.shape, F32)],
        input_output_aliases={0: 0, 1: 1, 2: 2},
        scratch_shapes=[pltpu.SemaphoreType.DMA((2,)), pltpu.SemaphoreType.DMA((2,)),
                        pltpu.SemaphoreType.DMA((7,)), pltpu.SemaphoreType.DMA((7,)), pltpu.SemaphoreType.DMA],
        compiler_params=pltpu.CompilerParams(has_side_effects=True),
    )(gw, go, gathered, late)


def _adamw(w, g, m, v, name):
    rows, width = w.shape
    tr = min(rows, 256)

    def body(w_ref, g_ref, m_ref, v_ref, d_ref, nm_ref, nv_ref):
        gv = g_ref[...]
        nm = ADAM_B1 * m_ref[...] + (1.0 - ADAM_B1) * gv
        nv = ADAM_B2 * v_ref[...] + (1.0 - ADAM_B2) * (gv * gv)
        m_hat = nm / (1.0 - ADAM_B1 ** ADAM_STEP)
        v_hat = nv / (1.0 - ADAM_B2 ** ADAM_STEP)
        d_ref[...] = -ADAM_LR * (m_hat / (jnp.sqrt(v_hat) + ADAM_EPS) + ADAM_WD * w_ref[...])
        nm_ref[...] = nm
        nv_ref[...] = nv

    t = pl.BlockSpec((tr, width), lambda i: (i, 0))
    return pl.pallas_call(
        body, name=name, grid=(rows // tr,), in_specs=[t] * 4, out_specs=[t] * 3,
        out_shape=[SDS(w.shape, F32)] * 3, compiler_params=_cp(("parallel",)),
    )(w, g, m, v)


def _rowwise(a):
    return jnp.transpose(a, (2, 0, 1)).reshape(SHARD * D // LANE, LANE)


def _columns(ref):
    return jnp.concatenate([ref[pl.ds(c, LANE, stride=8), :].T for c in range(D // LANE)], axis=0)


def _shard_bf16(chip, w_rows):
    def body(j_ref, w_ref, o_ref, prev_ref):
        t = pl.program_id(0)
        cur = _columns(w_ref)

        @pl.when(t == 0)
        def _():
            prev_ref[...] = jnp.zeros_like(prev_ref)

        lane = _iota((D, LANE), 1)
        for s in range(4):
            @pl.when(j_ref[0] == s)
            def _():
                off = SHIFT * s
                moved = cur if s == 0 else jnp.where(lane < off, pltpu.roll(prev_ref[...], off, 1),
                                                     pltpu.roll(cur, off, 1))
                col = t * LANE + lane - off
                o_ref[...] = jnp.where((col >= 0) & (col < SHARD), moved, 0.0).astype(BF16)
        prev_ref[...] = cur

    return pl.pallas_call(
        body, name="shard_bf16",
        grid_spec=pltpu.PrefetchScalarGridSpec(
            num_scalar_prefetch=1, grid=(TILES + 1,),
            in_specs=[pl.BlockSpec((D, LANE), lambda t, j: (t, 0))],
            out_specs=pl.BlockSpec((D, LANE), lambda t, j: (0, t)),
            scratch_shapes=[pltpu.VMEM((D, LANE), F32)]),
        out_shape=SDS((D, WIN), BF16), compiler_params=_cp(("arbitrary",)),
    )(chip, w_rows)


def _whole_w_in(windows):
    tr = 256
    n = windows.shape[0]

    def body(g_ref, o_ref):
        lane = _iota((tr, LANE), 1)
        for s in range(n):
            first = TILES * s
            head = g_ref[s, :, :LANE]
            if s:
                tail = g_ref[s - 1, :, TILES * LANE:]
                head = jnp.where(lane < SHIFT * s, tail.astype(F32), head.astype(F32)).astype(BF16)
            o_ref[:, first * LANE:(first + 1) * LANE] = head
            o_ref[:, (first + 1) * LANE:(first + TILES) * LANE] = g_ref[s, :, LANE:TILES * LANE]
        o_ref[:, n * TILES * LANE:(n * TILES + 1) * LANE] = g_ref[n - 1, :, TILES * LANE:]
        o_ref[:, (n * TILES + 1) * LANE:] = jnp.zeros((tr, DP - (n * TILES + 1) * LANE), BF16)

    return pl.pallas_call(
        body, name="whole_w_in", grid=(D // tr,),
        in_specs=[pl.BlockSpec((n, tr, WIN), lambda t: (0, t, 0))], out_specs=pl.BlockSpec((tr, DP), lambda t: (t, 0)),
        out_shape=SDS((D, DP), BF16), compiler_params=_cp(("parallel",)),
    )(windows)


def _own_buffer(a, name):
    tr = 512
    block = pl.BlockSpec((tr, a.shape[1]), lambda t: (t, 0))

    def body(a_ref, o_ref):
        o_ref[...] = a_ref[...]

    return pl.pallas_call(
        body, name=name, grid=(a.shape[0] // tr,), in_specs=[block], out_specs=block,
        out_shape=SDS(a.shape, a.dtype), compiler_params=_cp(("parallel",)),
    )(a)


def _shard_of_window(chip, g_win):
    tr = 128

    def body(j_ref, g_ref, grad_ref):
        for s in range(4):
            @pl.when(j_ref[0] == s)
            def _():
                back = LANE - SHIFT * s
                from_this = _iota((tr, LANE), 1) < back

                def moved(t):
                    tile = g_ref[:, t * LANE:(t + 1) * LANE]
                    return pltpu.roll(tile, back, 1) if s else tile

                for t in range(TILES):
                    grad_ref[:, t * LANE:(t + 1) * LANE] = jnp.where(from_this, moved(t), moved(t + 1)) if s else moved(t)
                grad_ref[:, TILES * LANE:] = moved(TILES)[:, :SHARD - TILES * LANE]

    return pl.pallas_call(
        body, name="shard_of_window",
        grid_spec=pltpu.PrefetchScalarGridSpec(
            num_scalar_prefetch=1, grid=(D // tr,), in_specs=[pl.BlockSpec((tr, WIN), lambda t, j: (t, 0))],
            out_specs=pl.BlockSpec((tr, SHARD), lambda t, j: (t, 0))),
        out_shape=SDS((D, SHARD), F32), compiler_params=_cp(("parallel",)),
    )(chip, g_win)


def _adamw_in(w_rows, g, m_rows, v_rows):
    def body(w_ref, g_ref, m_ref, v_ref, d_ref, nm_ref, nv_ref):
        columns = _columns
        gv = g_ref[...]
        nm = ADAM_B1 * columns(m_ref) + (1.0 - ADAM_B1) * gv
        nv = ADAM_B2 * columns(v_ref) + (1.0 - ADAM_B2) * (gv * gv)
        m_hat = nm / (1.0 - ADAM_B1 ** ADAM_STEP)
        v_hat = nv / (1.0 - ADAM_B2 ** ADAM_STEP)
        d_ref[...] = -ADAM_LR * (m_hat / (jnp.sqrt(v_hat) + ADAM_EPS) + ADAM_WD * columns(w_ref))
        nm_ref[...] = nm
        nv_ref[...] = nv

    tile = pl.BlockSpec((D, LANE), lambda t: (0, t))
    rows = pl.BlockSpec((D, LANE), lambda t: (t, 0))
    return pl.pallas_call(
        body, name="adamw_in", grid=(TILES + 1,), in_specs=[rows, tile, rows, rows],
        out_specs=[tile] * 3, out_shape=[SDS(g.shape, F32)] * 3, compiler_params=_cp(("parallel",)),
    )(w_rows, g, m_rows, v_rows)


def _rows128(a, rows):
    flat = a.reshape(-1)
    return jnp.pad(flat, (0, rows * LANE - flat.shape[0])).reshape(rows, LANE)


CONV_ROWS = 48


def _pack_small(conv_w, norm_pre, conv_b, ssm_norm, norm_post, dtb, alog, dsk, extra=None):
    cw_rows = CONV_ROWS if conv_w.shape[-1] == 1536 else 16
    extra = jnp.zeros((1, LANE), F32) if extra is None else _rows128(extra, 1)
    vec = jnp.concatenate([_rows128(dtb, 1), _rows128(alog, 1), _rows128(dsk, 1), extra, jnp.zeros((4, LANE), F32)],
                          axis=0)
    return jnp.concatenate([_rows128(conv_w, cw_rows), _rows128(norm_pre, 8), _rows128(conv_b, 16),
                            _rows128(ssm_norm, 8), _rows128(norm_post, 8), vec], axis=0)


def _unpack_small(p, cw_cols):
    cw_rows = CONV_ROWS if cw_cols == 1536 else 16
    o = cw_rows
    conv_w = p[:cw_rows].reshape(-1)[:4 * cw_cols].reshape(1, 4, cw_cols)
    norm_pre = p[o:o + 8].reshape(1, D)
    conv_b = p[o + 8:o + 24].reshape(-1)[:1536].reshape(1, 1536)
    ssm_norm = p[o + 24:o + 32].reshape(1, D)
    norm_post = p[o + 32:o + 40].reshape(1, D)
    vec = p[o + 40:o + 48]
    return conv_w, norm_pre, conv_b, ssm_norm, norm_post, vec[0:1, :NH], vec[1:2, :NH], vec[2:3, :NH], vec[3, 0]


def _pad_lanes(a):
    return jnp.pad(a, ((0, 0), (0, LANE - a.shape[1])))


class _GradReduce:
    LO, HI = ("q", "k", "v", "g"), ("g", "z", "x")

    def __init__(self, xi, yi, ci):
        self.cidx = jnp.reshape(ci, (1,)).astype(jnp.int32)
        self.place = jnp.stack([2 * xi + yi, ci, xi, yi]).astype(jnp.int32)

    def pairs(self, dw_g, dw_z, dw_x, dw_out):
        self.hi = [dw_g, dw_z, dw_x]
        self.go = dw_out.reshape(4, D // 2, D)
        return _PairExchange(self.HI, self.hi, (2, 3), [self.go])

    def first(self, got):
        rw, ro = got
        self.pw_hi = _pair_sum_windows(self.cidx, self.HI, self.hi, (2, 3), rw, "pair_sum_hi")
        self.po = _pair_sum(self.cidx, self.go, ro, "pair_sum_out")
        return _ChipExchange([self.pw_hi, self.po], [1, None])

    def first_done(self, got):
        self.rw_hi, self.ro = got

    def second_pairs(self, dw_q, dw_k, dw_g):
        self.lo = [dw_q, dw_k, None, dw_g]
        return _PairExchange(self.LO, self.lo, (0, 1))

    def second(self, dw_v, got, small):
        rest = _PairExchange(self.LO, [None, None, dw_v, None], (0, 1))
        (rw,) = _exchange_call(rest, "pair_exchange_v", into=got)
        lo = [dw_v if a is None else a for a in self.lo]
        self.pw_lo = _pair_sum_windows(self.cidx, self.LO, lo, (0, 1), rw, "pair_sum_lo")
        return _Both(_ChipExchange([self.pw_lo], [0]), _SmallExchange(small))

    def second_done(self, got):
        self.rw_lo, self.small = got

    def result(self, late, row):
        half_in = _chip_sum_rows(self.place, self.rw_lo, self.pw_lo, self.rw_hi, self.pw_hi, "chip_sum_in")
        half_out = _chip_sum(self.place[0:2], self.ro, self.po, "chip_sum_out")
        return _half_exchange(half_in, half_out, self.small, late, row)


def kernel(x, norm_pre_w, w_in, conv_w, conv_b, dt_bias, a_log, d_skip, ssm_norm_w, w_out, norm_post_w, loss_target, m_norm_pre_w, m_w_in, m_conv_w, m_conv_b, m_dt_bias, m_a_log, m_d_skip, m_ssm_norm_w, m_w_out, m_norm_post_w, v_norm_pre_w, v_w_in, v_conv_w, v_conv_b, v_dt_bias, v_a_log, v_d_skip, v_ssm_norm_w, v_w_out, v_norm_post_w):
    xi, yi, ci = lax.axis_index("x"), lax.axis_index("y"), lax.axis_index("c")
    chip = 2 * xi + yi
    x2, tgt = x[0], loss_target[0]

    chip_idx = jnp.reshape(chip, (1,)).astype(jnp.int32)
    w_rows = _rowwise(w_in)
    w_all = _whole_w_in(_gather_weights(_shard_bf16(chip_idx, w_rows)))
    reduce = _GradReduce(xi, yi, ci)
    grad_x, dnw_pre = _local_step(x2, tgt, w_all, _LateGather(w_out[0].astype(BF16), conv_w[0]), norm_pre_w, conv_b,
                                  dt_bias, a_log, d_skip, ssm_norm_w, norm_post_w, reduce)
    g_win, g_out, small = reduce.result(_rows128(dnw_pre, D // LANE), CONV_ROWS)
    g_small = _slot_sum(small, "small_sum")
    g_cw, g_npre, g_cb, g_nssm, g_npost, g_dtb, g_alog, g_dsk, loss = _unpack_small(g_small, 1536)
    g_cw = lax.dynamic_slice_in_dim(g_cw, chip * 384, 384, axis=2)

    g_in = _shard_of_window(chip_idx, g_win)
    d_in, nm_in, nv_in = _adamw_in(w_rows, g_in, _rowwise(m_w_in), _rowwise(v_w_in))
    grad_x = _own_buffer(grad_x, "grad_x_copy")
    d_out, nm_out, nv_out = _adamw(w_out[0], g_out, m_w_out[0], v_w_out[0], "adamw_out")
    packed = [_pack_small(*t) for t in (
        (conv_w, norm_pre_w, conv_b, ssm_norm_w, norm_post_w, dt_bias, a_log, d_skip),
        (g_cw, g_npre, g_cb, g_nssm, g_npost, g_dtb, g_alog, g_dsk),
        (m_conv_w, m_norm_pre_w, m_conv_b, m_ssm_norm_w, m_norm_post_w, m_dt_bias, m_a_log, m_d_skip),
        (v_conv_w, v_norm_pre_w, v_conv_b, v_ssm_norm_w, v_norm_post_w, v_dt_bias, v_a_log, v_d_skip))]
    small_out = [_unpack_small(p, 384)[:8] for p in _adamw(*packed, "adamw_small")]

    def ordered(cw_, npre, cb_, nssm, npost, dtb_, alog_, dsk_, big_in, big_out):
        return [npre, big_in[None], cw_, cb_, dtb_, alog_, dsk_, nssm, big_out[None], npost]

    grads = ordered(g_cw, g_npre, g_cb, g_nssm, g_npost, g_dtb, g_alog, g_dsk, g_in, g_out)
    deltas = ordered(*small_out[0], d_in, d_out)
    new_m = ordered(*small_out[1], nm_in, nm_out)
    new_v = ordered(*small_out[2], nv_in, nv_out)
    return (loss, grad_x[None], *grads, *deltas, *new_m, *new_v)


def _local_step(x2, tgt, w_all, late, norm_pre_w, conv_b, dt_bias, a_log, d_skip, ssm_norm_w,
                norm_post_w, reduce=None):
    dtb, alog = _pad_lanes(dt_bias), _pad_lanes(a_log)
    d_b = jnp.repeat(d_skip, 64, axis=1)

    if isinstance(late, _LateGather):
        (proj, u), (gout, gcw) = _inproj_fwd(x2, norm_pre_w, w_all, late)
        w_out_all = gout.reshape(2 * D, D)
        cw_all = jnp.concatenate([gcw[0], gcw[1], gcw[2], gcw[3]], axis=1)
    else:
        proj, u = _inproj_fwd(x2, norm_pre_w, w_all)
        w_out_all, cw_all = late
    mix, attn_pre, lse = _attn_fwd(proj, 1, _attn_fwd(proj, 4, _attn_fwd(proj, 16)), final=True)
    mix, y_save, states, conv_out = _ssm_fwd(proj, mix, cw_all, conv_b, dtb, alog, d_b, ssm_norm_w)

    dy, dn_ssm, do, delta, dg, dw_out, dnw_post, loss_part = _outproj_loss(mix, w_out_all, x2, tgt, norm_post_w,
                                                                          attn_pre, proj)
    dz, dxbcdt, dcw, dcb, dvec, dnw_ssm = _ssm_bwd(proj, dn_ssm, y_save, states, conv_out, cw_all, dtb, alog, d_b,
                                                   ssm_norm_w)
    dw_g, dw_z, dw_x = _dw(u, dg, "dw_in_g"), _dw(u, dz, "dw_in_z"), _dw(u, dxbcdt, "dw_in_xbcdt", X_COLS)
    acc = _attn_bwd(proj, do, lse, delta, 16, None, F32, reduce.pairs(dw_g, dw_z, dw_x, dw_out) if reduce else None)
    if reduce:
        acc, got = acc
    acc = _attn_bwd(proj, do, lse, delta, 4, acc, F32, reduce.first(got) if reduce else None)
    if reduce:
        acc, got = acc
        reduce.first_done(got)
    dq, dk, dv = _attn_bwd(proj, do, lse, delta, 1, acc, BF16)
    dw_q, dw_k = _dw(u, dq, "dw_in_q"), _dw(u, dk, "dw_in_k")
    dw_v = _dw(u, dv, "dw_in_v", hosted=reduce.second_pairs(dw_q, dw_k, dw_g) if reduce else None)
    if reduce:
        dw_v, got = dw_v

    def small(dnw_pre):
        return _pack_small(dcw, dnw_pre, dcb, dnw_ssm, dnw_post, dvec[0:1, :NH], dvec[1:2, :NH], dvec[2:3, :NH],
                           loss_part[:, :1])

    res = _inproj_bwd_dx([dq, dk, dv, dg, dz], dxbcdt, w_all, x2, dy, norm_pre_w,
                         reduce.second(dw_v, got, small(jnp.zeros((1, D), F32))) if reduce else None)
    if reduce:
        res, got = res
        reduce.second_done(got)
        return res
    grad_x, dnw_pre = res
    dw_all = jnp.concatenate([dw_q, dw_k, dw_v, dw_g, dw_z, dw_x], axis=1)
    return grad_x, small(dnw_pre), dw_all, dw_out
```

```python
import functools

import jax
import jax.numpy as jnp
from jax import lax
from jax.experimental import pallas as pl
from jax.experimental.pallas import tpu as pltpu

F32 = jnp.float32
BF16 = jnp.bfloat16
MESH = pl.DeviceIdType.MESH
SDS = jax.ShapeDtypeStruct
ANY = pl.BlockSpec(memory_space=pl.ANY)

S = 4096
D = 1024
DP = 7168
SHARD = 1668
OFF_G, OFF_Z = 3072, 4096
NH = 16
CH = 128
NC = S // CH
EPS = 1e-6
NEG = -1e30
LANE = 128
VMEM_LIMIT = 48 * 1024 * 1024

TILES = SHARD // LANE
WIN = (TILES + 1) * LANE
SHIFT = SHARD - TILES * LANE
SECTION_TILES = {"q": (0, 8), "k": (8, 8), "v": (16, 8), "g": (24, 8), "z": (32, 8), "x": (40, 13)}
X_COLS = SECTION_TILES["x"][1] * LANE

ADAM_LR, ADAM_B1, ADAM_B2, ADAM_EPS, ADAM_WD, ADAM_STEP = 0.001, 0.9, 0.999, 1e-08, 0.01, 10


def _cp(sem, **kw):
    return pltpu.CompilerParams(dimension_semantics=sem, vmem_limit_bytes=VMEM_LIMIT, **kw)


def _dot(a, b):
    return jnp.dot(a, b, preferred_element_type=F32)


def _dot_nt(a, b):
    return lax.dot_general(a, b, (((1,), (1,)), ((), ())), preferred_element_type=F32)


def _dot_tn(a, b):
    return lax.dot_general(a, b, (((0,), (0,)), ((), ())), preferred_element_type=F32)


def _pieces(x, n):
    out = []
    for _ in range(n):
        p = x.astype(BF16)
        out.append(p)
        x = x - p.astype(F32)
    return out


def _pick(x, sel, n=2):
    parts = [_dot(p, sel) for p in _pieces(x, n)]
    return functools.reduce(jnp.add, parts)


def _pick_left(sel, x, n=3):
    parts = [_dot(sel, p) for p in _pieces(x, n)]
    return functools.reduce(jnp.add, parts)


def _sigmoid(v):
    return 0.5 * jnp.tanh(0.5 * v) + 0.5


def _iota(shape, dim):
    return lax.broadcasted_iota(jnp.int32, shape, dim)


def _inproj_fwd(x, nw, w_all, hosted=None):
    tm, tn = 1024, 1024
    n_host = len(hosted.arrays) if hosted else 0

    def body(x_ref, nw_ref, w_ref, *refs):
        host_in, (proj_ref, u_ref), refs = refs[:n_host], refs[n_host:n_host + 2], refs[n_host + 2:]
        host_out, host_sems = refs[:n_host], refs[n_host:]
        i, j = pl.program_id(0), pl.program_id(1)
        if hosted:
            pl.when((i == 0) & (j == 0))(lambda: hosted.start(host_in, host_out, host_sems))

        @pl.when(j == 0)
        def _():
            xf = x_ref[...]
            r = lax.rsqrt(jnp.mean(xf * xf, axis=-1, keepdims=True) + EPS)
            u_ref[...] = (xf * r * nw_ref[...]).astype(BF16)

        proj_ref[...] = _dot(u_ref[...], w_ref[...])
        if hosted:
            pl.when((i == S // tm // 2) & (j == 0))(lambda: hosted.pass_on(host_in, host_out, host_sems))
            pl.when((i == S // tm - 1) & (j == DP // tn - 1))(lambda: hosted.finish(host_in, host_out, host_sems))

    outs = pl.pallas_call(
        body, name="inproj_fwd", grid=(S // tm, DP // tn),
        in_specs=[pl.BlockSpec((tm, D), lambda i, j: (i, 0)), pl.BlockSpec((1, D), lambda i, j: (0, 0)),
                  pl.BlockSpec((D, tn), lambda i, j: (0, j))] + [ANY] * n_host,
        out_specs=[pl.BlockSpec((tm, tn), lambda i, j: (i, j)), pl.BlockSpec((tm, D), lambda i, j: (i, 0))]
        + [ANY] * n_host,
        out_shape=[SDS((S, DP), F32), SDS((S, D), BF16)] + (hosted.out_shape if hosted else []),
        scratch_shapes=hosted.scratch if hosted else [],
        compiler_params=_cp(("arbitrary", "arbitrary") if hosted else ("parallel", "arbitrary")),
    )(x, nw, w_all, *(hosted.arrays if hosted else []))
    return (outs[:2], outs[2:]) if hosted else outs


ATTN_QB = {1: 16, 4: 4, 16: 1}


def _unit_rows(r, u, d):
    return pl.ds(r + d * CH * u, CH, stride=d) if d > 1 else pl.ds(CH * u, CH)


def _for_units(d, qb, fn):
    for r in range(d):
        for u in range(qb):
            fn(r, u)


def _attn_mask(has_prev):
    qi, kj = _iota((2 * CH, 2 * CH), 0) & (CH - 1), _iota((2 * CH, 2 * CH), 1)
    cur_ok = (kj >= CH) & (kj - CH <= qi)
    prev_ok = (kj < CH) & (kj >= qi)
    return cur_ok | (prev_ok & has_prev)


def _stack_heads(v, lane_a):
    return jnp.concatenate([jnp.where(lane_a, v, 0.0), jnp.where(lane_a, 0.0, v)], axis=0).astype(BF16)


def _attn_specs(d, qb):
    rows, prows = CH * d * qb, CH * d
    nb = S // rows
    steps = (NH // 2) * nb

    def at(t):
        t = jnp.minimum(t, steps - 1)
        return t % nb, t // nb

    def cur(off):
        return pl.BlockSpec((rows, LANE), lambda t: (at(t)[0], off + at(t)[1]))

    def prev(off):
        return pl.BlockSpec((prows, LANE), lambda t: (jnp.maximum(at(t)[0] * qb - 1, 0), off + at(t)[1]))

    lag = pl.BlockSpec((rows, LANE), lambda t: at(jnp.maximum(t - 1, 0)))
    return nb, steps, cur, prev, lag


def _gather16(src_ref, dense_ref, tmp_ref):
    for a in range(4):
        tmp_ref[...] = src_ref[pl.ds(a, 4 * CH, stride=4), :]
        for b in range(4):
            dense_ref[a + 4 * b] = tmp_ref[pl.ds(b, CH, stride=4), :]


def _scatter16(dense_ref, dst_ref, tmp_ref):
    for a in range(4):
        for b in range(4):
            tmp_ref[pl.ds(b, CH, stride=4), :] = dense_ref[a + 4 * b]
        dst_ref[pl.ds(a, 4 * CH, stride=4), :] = tmp_ref[...]


def _unit_index(r, u, d):
    return (r,) if d == 16 else (_unit_rows(r, u, d), slice(None))


def _unit_kv(p_ref, c_ref, r, u, d):
    prev = p_ref[_unit_index(r, 0, d)] if u == 0 else c_ref[_unit_index(r, u - 1, d)]
    return jnp.concatenate([prev, c_ref[_unit_index(r, u, d)]], axis=0).astype(BF16)


def _dense_scratch(d, n):
    return [pltpu.VMEM((16, CH, LANE), F32)] * n + [pltpu.VMEM((4 * CH, LANE), F32)] if d == 16 else []


def _attn_fwd(proj, d, prior=None, final=False):
    qb = ATTN_QB[d]
    nb, steps, cur, prev, _ = _attn_specs(d, qb)
    n_prior = 2 if prior is not None else 0
    n_in, n_out = 5 + n_prior + final, 2 + final
    assert not (d == 16 and (n_prior or final))

    def body(*refs):
        ins, outs, scratch = refs[:n_in], refs[n_in:n_in + n_out], refs[n_in + n_out:]
        if d == 16:
            tmp_ref = scratch[-1]
            for src, dense in zip(ins, scratch):
                _gather16(src, dense, tmp_ref)
            block_outs, ins, outs = outs, scratch[:n_in], scratch[n_in:n_in + n_out]
        q_ref, kp_ref, kc_ref, vp_ref, vc_ref = ins[:5]
        prior_refs = ins[5:5 + n_prior]
        if final:
            g_ref, (mix_ref, o_ref, l_ref) = ins[-1], outs
        else:
            o_ref, l_ref = outs
        i = pl.program_id(0) % nb
        lane_a = _iota((CH, LANE), 1) < 64
        mask_first, mask_rest = _attn_mask(i > 0), _attn_mask(True)

        def unit(r, u):
            at = _unit_index(r, u, d)
            q2 = _stack_heads(q_ref[at] * 0.125, lane_a)
            k2, v2 = _unit_kv(kp_ref, kc_ref, r, u, d), _unit_kv(vp_ref, vc_ref, r, u, d)
            s = jnp.where(mask_first if u == 0 else mask_rest, _dot_nt(q2, k2), NEG)
            m = jnp.max(s, axis=1, keepdims=True)
            p = jnp.exp(s - m)
            l = jnp.sum(p, axis=1, keepdims=True)
            o2 = _dot(p.astype(BF16), v2) / l
            lse2 = m + jnp.log(l)
            o = jnp.where(lane_a, o2[:CH], o2[CH:])
            lse = jnp.where(lane_a, lse2[:CH], lse2[CH:])
            if n_prior:
                o_a, l_a = prior_refs[0][at], prior_refs[1][at]
                top = jnp.maximum(l_a, lse)
                e_a, e_b = jnp.exp(l_a - top), jnp.exp(lse - top)
                tot = e_a + e_b
                o = (e_a * o_a + e_b * o) / tot
                lse = top + jnp.log(tot)
            o_ref[at] = o
            l_ref[at] = lse
            if final:
                g = g_ref[at]
                mix_ref[at] = (o * (g * _sigmoid(g))).astype(BF16)

        _for_units(d, qb, unit)
        if d == 16:
            for dense, dst in zip(outs, block_outs):
                _scatter16(dense, dst, tmp_ref)

    in_specs = [cur(0), prev(8), cur(8), prev(16), cur(16)] + [cur(0)] * n_prior
    args = [proj] * 5 + (list(prior) if n_prior else [])
    out_specs, out_shape = [cur(0), cur(0)], [SDS((S, D), F32), SDS((S, D), F32)]
    if final:
        assert d == 1
        in_specs.append(cur(OFF_G // LANE))
        args.append(proj)
        out_specs, out_shape = [cur(0)] + out_specs, [SDS((S, 2 * D), BF16)] + out_shape
    return pl.pallas_call(
        body, name=f"attn_fwd_d{d}", grid=(steps,),
        in_specs=in_specs, out_specs=out_specs, out_shape=out_shape,
        scratch_shapes=_dense_scratch(d, n_in + n_out),
        compiler_params=_cp(("parallel",)),
    )(*args)


def _attn_bwd(proj, do, lse, delta, d, acc, out_dtype, hosted=None):
    qb = ATTN_QB[d]
    nb, steps, cur, prev, lag = _attn_specs(d, qb)
    has_acc = acc is not None
    n_in = 11 if has_acc else 8
    n_host, n_host_out = (len(hosted.arrays), len(hosted.out_shape)) if hosted else (0, 0)
    assert not (d == 16 and (has_acc or out_dtype != F32))
    rows = CH * d * qb
    carry = (2, 16, CH, LANE) if d == 16 else (2, rows, LANE)

    def body(*refs):
        ins, host_in, refs = refs[:n_in], refs[n_in:n_in + n_host], refs[n_in + n_host:]
        (dq_ref, dk_ref, dv_ref), host_out, scratch = refs[:3], refs[3:3 + n_host_out], refs[3 + n_host_out:]
        if hosted:
            scratch, host_sems = scratch[:-len(hosted.scratch)], scratch[-len(hosted.scratch):]
        ck_ref, cv_ref = scratch[:2]
        dq_f32 = dq_ref if out_dtype == F32 else scratch[2]
        t = pl.program_id(0)
        i = t % nb
        if hosted:
            pl.when(t == 0)(lambda: hosted.start(host_in, host_out, host_sems))
        if d == 16:
            dense, dq_f32, tmp_ref = scratch[2:2 + n_in], scratch[2 + n_in], scratch[-1]

            @pl.when(t < steps)
            def _():
                for src, dst in zip(ins, dense):
                    _gather16(src, dst, tmp_ref)

            ins = dense
        q_ref, kp_ref, kc_ref, vp_ref, vc_ref, do_ref, lse_ref, dl_ref = ins[:8]
        if has_acc:
            aq_ref, ak_ref, av_ref = ins[8:11]
        slot = t & 1
        now_k, now_v, old_k, old_v = ck_ref.at[slot], cv_ref.at[slot], ck_ref.at[1 - slot], cv_ref.at[1 - slot]
        lane_a = _iota((CH, LANE), 1) < 64
        mask_first, mask_rest = _attn_mask(i > 0), _attn_mask(True)

        @pl.when(t == 0)
        def _():
            ck_ref[1] = jnp.zeros(carry[1:], F32)
            cv_ref[1] = jnp.zeros(carry[1:], F32)

        def unit(r, u):
            at = _unit_index(r, u, d)
            q2 = _stack_heads(q_ref[at] * 0.125, lane_a)
            do2 = _stack_heads(do_ref[at], lane_a)
            k2, v2 = _unit_kv(kp_ref, kc_ref, r, u, d), _unit_kv(vp_ref, vc_ref, r, u, d)
            lsev, dlv = lse_ref[at], dl_ref[at]
            lse2 = jnp.concatenate([lsev[:, 0:1], lsev[:, 64:65]], axis=0)
            dl2 = jnp.concatenate([dlv[:, 0:1], dlv[:, 64:65]], axis=0)
            p = jnp.exp(jnp.where(mask_first if u == 0 else mask_rest, _dot_nt(q2, k2), NEG) - lse2)
            ds = (p * (_dot_nt(do2, v2) - dl2)).astype(BF16)
            dq2 = _dot(ds, k2)
            dk2 = _dot_tn(ds, q2)
            dv2 = _dot_tn(p.astype(BF16), do2)
            dq = jnp.where(lane_a, dq2[:CH], dq2[CH:]) * 0.125
            if has_acc:
                dq = dq + aq_ref[at]
            dq_f32[at] = dq
            if u == 0:
                before = _unit_index(r, qb - 1, d)
                old_k[before] += dk2[:CH]
                old_v[before] += dv2[:CH]
            else:
                before = _unit_index(r, u - 1, d)
                now_k[before] += dk2[:CH]
                now_v[before] += dv2[:CH]
            now_k[at] = dk2[CH:]
            now_v[at] = dv2[CH:]

        @pl.when(t < steps)
        def _():
            _for_units(d, qb, unit)
            if d == 16:
                _scatter16(dq_f32, dq_ref, tmp_ref)
            elif out_dtype != F32:
                dq_ref[...] = dq_f32[...].astype(out_dtype)

        if d == 16:
            _scatter16(old_k, dk_ref, tmp_ref)
            _scatter16(old_v, dv_ref, tmp_ref)
        else:
            dk, dv = old_k[...], old_v[...]
            if has_acc:
                dk, dv = dk + ak_ref[...], dv + av_ref[...]
            dk_ref[...] = dk.astype(out_dtype)
            dv_ref[...] = dv.astype(out_dtype)
        if hosted:
            pl.when(t == steps)(lambda: hosted.finish(host_in, host_out, host_sems))

    in_specs = [cur(0), prev(8), cur(8), prev(16), cur(16), cur(0), cur(0), cur(0)]
    args = [proj, proj, proj, proj, proj, do, lse, delta]
    if has_acc:
        in_specs += [cur(0), lag, lag]
        args += list(acc)
    scratch = [pltpu.VMEM(carry, F32), pltpu.VMEM(carry, F32)]
    if d == 16:
        scratch += _dense_scratch(d, n_in + 1)
    elif out_dtype != F32:
        scratch.append(pltpu.VMEM((rows, LANE), F32))
    out_specs, out_shape = [cur(0), lag, lag], [SDS((S, D), out_dtype)] * 3
    if hosted:
        args += hosted.arrays
        in_specs += [ANY] * n_host
        out_specs += [ANY] * n_host_out
        out_shape += hosted.out_shape
        scratch += hosted.scratch
    outs = pl.pallas_call(
        body, name=f"attn_bwd_d{d}", grid=(steps + 1,),
        in_specs=in_specs, out_specs=out_specs, out_shape=out_shape,
        scratch_shapes=scratch, compiler_params=_cp(("arbitrary",)),
    )(*args)
    return (outs[:3], outs[3:]) if hosted else outs


def _conv_taps(cur, prev8, first):
    row8 = _iota(prev8.shape, 0)
    prev8 = jnp.where(first, 0.0, prev8)
    taps = []
    for s in (3, 2, 1):
        rolled = pltpu.roll(cur, s, 0)
        head = jnp.where(row8 < s, pltpu.roll(prev8, s, 0), rolled[:8])
        taps.append(jnp.concatenate([head, rolled[8:]], axis=0))
    return taps + [cur]


def _conv(taps, w, b):
    acc = b + w[0:1, :] * taps[0]
    for k in (1, 2, 3):
        acc = acc + w[k:k + 1, :] * taps[k]
    return acc


def _expand():
    return (_iota((LANE, D), 1) // 64 == _iota((LANE, D), 0)).astype(BF16)


def _reduce():
    return (_iota((D, LANE), 0) // 64 == _iota((D, LANE), 1)).astype(BF16)


def _ssd_common(xs_c, bc_c, dt_raw, dtb, alog):
    head_lane = _iota((CH, LANE), 1) < NH
    xs = xs_c * _sigmoid(xs_c)
    bc = bc_c * _sigmoid(bc_c)
    pre = dt_raw + dtb
    dt = jnp.where(head_lane, jnp.maximum(pre, 0.0) + jnp.log(1.0 + jnp.exp(-jnp.abs(pre))), 0.0)
    a_row = jnp.where(head_lane[0:1], -jnp.exp(alog), 0.0)
    tri = (_iota((CH, CH), 1) <= _iota((CH, CH), 0)).astype(BF16)
    cs = _pick_left(tri, dt * a_row)
    cs_last = cs[CH - 1:CH, :]
    wide = _pick(jnp.concatenate([dt, jnp.exp(cs), jnp.exp(cs_last - cs)], axis=0), _expand())
    dt_b, e_b, f_b = wide[:CH], wide[CH:2 * CH], wide[2 * CH:]
    return dict(xs=xs, bc=bc, pre=pre, dt=dt, a_row=a_row, cs=cs, cs_t=cs.T, dt_b=dt_b, e_b=e_b, f_b=f_b,
                t_b=e_b[CH - 1:CH, :])


def _groups(bc):
    bcb = bc.astype(BF16)
    return [bcb[:, 0:128], bcb[:, 128:256]], [bcb[:, 256:384], bcb[:, 384:512]]


def _decay(q, h, tril):
    seg = q["cs"][:, h:h + 1] - q["cs_t"][h:h + 1, :]
    return jnp.exp(jnp.where(tril, seg, NEG))


def _ssm_fwd(proj, mix, cw, cb, dtb, alog, d_b, nw):
    def body(xs_ref, xsp_ref, bc_ref, bcp_ref, dt_ref, z_ref, cw_ref, cb_ref, dtb_ref, alog_ref, db_ref, nw_ref,
             mix_in_ref, mix_ref, y_ref, st_ref, conv_ref, h_ref):
        del mix_in_ref
        i = pl.program_id(0)

        @pl.when(i == 0)
        def _():
            h_ref[...] = jnp.zeros_like(h_ref)

        cw, cb = cw_ref[...], cb_ref[...]
        xs_c = _conv(_conv_taps(xs_ref[...], xsp_ref[...], i == 0), cw[:, :D], cb[:, :D])
        bc_c = _conv(_conv_taps(bc_ref[...], bcp_ref[...], i == 0), cw[:, D:], cb[:, D:])
        conv_ref[:, :D] = xs_c
        conv_ref[:, D:] = bc_c
        q = _ssd_common(xs_c, bc_c, dt_ref[...], dtb_ref[...], alog_ref[...])
        bg, cg = _groups(q["bc"])
        xs = q["xs"]
        xdt = xs * q["dt_b"]
        xdt_b = xdt.astype(BF16)
        h_in = h_ref[...]
        st_ref[...] = h_in
        hb = h_in.astype(BF16)
        tril = _iota((CH, CH), 1) <= _iota((CH, CH), 0)
        lane_a = _iota((CH, LANE), 1) < 64
        cbm = [_dot_nt(cg[g], bg[g]) for g in range(2)]
        pairs = []
        for hp in range(NH // 2):
            xp = xdt_b[:, hp * LANE:(hp + 1) * LANE]
            ya = _dot((cbm[hp // 4] * _decay(q, 2 * hp, tril)).astype(BF16), xp)
            yb = _dot((cbm[hp // 4] * _decay(q, 2 * hp + 1, tril)).astype(BF16), xp)
            pairs.append(jnp.where(lane_a, ya, yb))
        y_diag = jnp.concatenate(pairs, axis=1)
        y_off = jnp.concatenate([_dot(cg[g], hb[:, g * 512:(g + 1) * 512]) for g in range(2)], axis=1) * q["e_b"]
        y = y_diag + y_off + db_ref[...] * xs
        y_ref[...] = y
        xf = (xdt * q["f_b"]).astype(BF16)
        h_ref[...] = q["t_b"] * h_in + jnp.concatenate(
            [_dot_tn(bg[g], xf[:, g * 512:(g + 1) * 512]) for g in range(2)], axis=1)
        z = z_ref[...]
        yz = y * (z * _sigmoid(z))
        outs = []
        for g in range(2):
            v = yz[:, g * 512:(g + 1) * 512]
            outs.append(v * lax.rsqrt(jnp.mean(v * v, axis=-1, keepdims=True) + EPS))
        mix_ref[...] = (jnp.concatenate(outs, axis=1) * nw_ref[...]).astype(BF16)

    def col(width, blk, prev=False):
        if prev:
            return pl.BlockSpec((8, width), lambda i: (jnp.maximum(i * (CH // 8) - 1, 0), blk))
        return pl.BlockSpec((CH, width), lambda i: (i, blk))

    def full(a):
        return pl.BlockSpec(a.shape, lambda i: (0,) * a.ndim)

    return pl.pallas_call(
        body, name="ssm_fwd", grid=(NC,),
        in_specs=[col(D, 5), col(D, 5, True), col(512, 12), col(512, 12, True), col(LANE, 52), col(D, 4),
                  full(cw), full(cb), full(dtb), full(alog), full(d_b), full(nw), ANY],
        out_specs=[col(D, 1), col(D, 0), pl.BlockSpec((None, CH, D), lambda i: (i, 0, 0)), col(D + 512, 0)],
        out_shape=[SDS((S, 2 * D), BF16), SDS((S, D), F32), SDS((NC, CH, D), F32), SDS((S, D + 512), F32)],
        scratch_shapes=[pltpu.VMEM((CH, D), F32)],
        input_output_aliases={12: 0},
        compiler_params=_cp(("arbitrary",)),
    )(proj, proj, proj, proj, proj, proj, cw, cb, dtb, alog, d_b, nw, mix)


def _ssm_bwd(proj, dn, y_save, states, conv_out, cw, dtb, alog, d_b, nw):
    def body(xs_ref, bc_ref, dt_ref, z_ref, dn_ref, y_ref, st_ref, conv_ref,
             cw_ref, dtb_ref, alog_ref, db_ref, nw_ref,
             dz_ref, dx_ref, dcw_ref, dcb_ref, dsm_ref, dnw_ref, dh_ref, nxs_ref, nbc_ref):
        i = pl.program_id(0)
        ci = NC - 1 - i

        @pl.when(i == 0)
        def _():
            for ref in (dcw_ref, dcb_ref, dsm_ref, dnw_ref, dh_ref, nxs_ref, nbc_ref):
                ref[...] = jnp.zeros_like(ref)

        cw = cw_ref[...]
        xs_c, bc_c = conv_ref[:, :D], conv_ref[:, D:]
        q = _ssd_common(xs_c, bc_c, dt_ref[...], dtb_ref[...], alog_ref[...])
        bg, cg = _groups(q["bc"])
        xs, dt_b, e_b, f_b, t_b = q["xs"], q["dt_b"], q["e_b"], q["f_b"], q["t_b"]
        xdt = xs * dt_b
        xdt_b = xdt.astype(BF16)
        h_in = st_ref[...]
        hb = h_in.astype(BF16)
        dh_new = dh_ref[...]
        dhb = dh_new.astype(BF16)
        red = _reduce()

        z, y, dn, nw_v = z_ref[...], y_ref[...], dn_ref[...], nw_ref[...]
        sig = _sigmoid(z)
        sz = z * sig
        yz = y * sz
        gdn = dn * nw_v
        dyz, dnw = [], []
        for g in range(2):
            v, gv = yz[:, g * 512:(g + 1) * 512], gdn[:, g * 512:(g + 1) * 512]
            r = lax.rsqrt(jnp.mean(v * v, axis=-1, keepdims=True) + EPS)
            dnw.append(dn[:, g * 512:(g + 1) * 512] * v * r)
            dyz.append(r * (gv - v * (r * r) * jnp.mean(gv * v, axis=-1, keepdims=True)))
        dyz = jnp.concatenate(dyz, axis=1)
        dnw_ref[...] += jnp.sum(jnp.concatenate(dnw, axis=1), axis=0, keepdims=True)
        dy = dyz * sz
        dz_ref[...] = (dyz * y * (sig * (1.0 + z * (1.0 - sig)))).astype(BF16)
        dy_b = dy.astype(BF16)

        tril = _iota((CH, CH), 1) <= _iota((CH, CH), 0)
        lane_a = _iota((CH, LANE), 1) < 64
        cbm = [_dot_nt(cg[g], bg[g]) for g in range(2)]
        dcbm = [jnp.zeros((CH, CH), F32), jnp.zeros((CH, CH), F32)]
        seg_rows = jnp.zeros((CH, LANE), F32)
        seg_cols = jnp.zeros((LANE, CH), F32)
        row_id, col_id = _iota((CH, LANE), 0), _iota((CH, LANE), 1)
        dx_pairs = []
        for hp in range(NH // 2):
            g = hp // 4
            xp = xdt_b[:, hp * LANE:(hp + 1) * LANE]
            dyp_f = dy[:, hp * LANE:(hp + 1) * LANE]
            dyp = dy_b[:, hp * LANE:(hp + 1) * LANE]
            halves = []
            for k in range(2):
                h = 2 * hp + k
                lane = lane_a if k == 0 else jnp.logical_not(lane_a)
                dec = _decay(q, h, tril)
                gm = cbm[g] * dec
                dgm = _dot_nt(jnp.where(lane, dyp_f, 0.0).astype(BF16), xp)
                dcbm[g] = dcbm[g] + dgm * dec
                prod = dgm * gm
                seg_rows = jnp.where(col_id == h, jnp.sum(prod, axis=1, keepdims=True), seg_rows)
                seg_cols = jnp.where(row_id == h, jnp.sum(prod, axis=0, keepdims=True), seg_cols)
                halves.append(_dot_tn(gm.astype(BF16), dyp))
            dx_pairs.append(jnp.where(lane_a, halves[0], halves[1]))
        dxdt_diag = jnp.concatenate(dx_pairs, axis=1)

        qv = jnp.concatenate([_dot(bg[g], dhb[:, g * 512:(g + 1) * 512]) for g in range(2)], axis=1)
        y_off = jnp.concatenate([_dot(cg[g], hb[:, g * 512:(g + 1) * 512]) for g in range(2)], axis=1) * e_b
        xfq = xdt * f_b * qv
        dxdt = dxdt_diag + f_b * qv
        tdt = jnp.sum(dh_new * h_in, axis=0, keepdims=True) * t_b
        per_head = _pick(jnp.concatenate([xfq, dy * y_off, dxdt * xs, dy * xs, jnp.broadcast_to(tdt, (8, D))],
                                         axis=0), red)
        fdf, dyoff_h, dxdtxs_h, dyxs_h = [per_head[k * CH:(k + 1) * CH] for k in range(4)]
        dcs = seg_rows - seg_cols.T + dyoff_h - fdf
        last = per_head[4 * CH:4 * CH + 1] + jnp.sum(fdf, axis=0, keepdims=True)
        dcs = dcs + jnp.where(_iota((CH, LANE), 0) == CH - 1, last, 0.0)
        tri_t = (_iota((CH, CH), 1) >= _iota((CH, CH), 0)).astype(BF16)
        da = _pick_left(tri_t, dcs)
        ddt = da * q["a_row"] + dxdtxs_h
        dxs = dxdt * dt_b + db_ref[...] * dy
        ddt_raw = ddt * _sigmoid(q["pre"])
        dsm_ref[0:1, :] += jnp.sum(ddt_raw, axis=0, keepdims=True)
        dsm_ref[1:2, :] += jnp.sum(da * q["dt"], axis=0, keepdims=True) * q["a_row"]
        dsm_ref[2:3, :] += jnp.sum(dyxs_h, axis=0, keepdims=True)
        edy = (e_b * dy).astype(BF16)
        xf = (xdt * f_b).astype(BF16)
        dbs, dcs_g, dhs = [], [], []
        for g in range(2):
            sl = slice(g * 512, (g + 1) * 512)
            dcb_b = dcbm[g].astype(BF16)
            dcs_g.append(_dot(dcb_b, bg[g]) + _dot_nt(edy[:, sl], hb[:, sl]))
            dbs.append(_dot_tn(dcb_b, cg[g]) + _dot_nt(xf[:, sl], dhb[:, sl]))
            dhs.append(_dot_tn(cg[g], edy[:, sl]))
        dh_ref[...] = t_b * dh_new + jnp.concatenate(dhs, axis=1)
        dbc = jnp.concatenate(dbs + dcs_g, axis=1)

        def conv_bwd(dact, pre, x_raw, w, nxt_ref, lo):
            s = _sigmoid(pre)
            dconv = dact * (s * (1.0 + pre * (1.0 - s)))
            nxt8 = nxt_ref[...]
            row8 = _iota(nxt8.shape, 0)
            hi = lo + dconv.shape[1]
            dcb_ref[:, lo:hi] += jnp.sum(dconv, axis=0, keepdims=True)
            later = [dconv]
            for s_ in (1, 2, 3):
                rolled = pltpu.roll(dconv, CH - s_, 0)
                tail = jnp.where(row8 >= 8 - s_, pltpu.roll(nxt8, 8 - s_, 0), rolled[CH - 8:])
                later.append(jnp.concatenate([rolled[:CH - 8], tail], axis=0))
            dx = None
            for s_, up in enumerate(later):
                k = 3 - s_
                dcw_ref[k:k + 1, lo:hi] += jnp.sum(up * x_raw, axis=0, keepdims=True)
                dx = w[k:k + 1, :] * up if dx is None else dx + w[k:k + 1, :] * up
            nxt_ref[...] = dconv[:8]
            return dx

        dx_ref[:, 0:D] = conv_bwd(dxs, xs_c, xs_ref[...], cw[:, :D], nxs_ref, 0).astype(BF16)
        dx_ref[:, D:D + 512] = conv_bwd(dbc, bc_c, bc_ref[...], cw[:, D:], nbc_ref, D).astype(BF16)
        dx_ref[:, D + 512:D + 640] = ddt_raw.astype(BF16)
        dx_ref[:, D + 640:] = jnp.zeros((CH, D - 640), BF16)

    def col(width, blk):
        return pl.BlockSpec((CH, width), lambda i: (NC - 1 - i, blk))

    def full(a):
        return pl.BlockSpec(a.shape, lambda i: (0,) * len(a.shape))

    acc_shapes = [SDS((4, 1536), F32), SDS((1, 1536), F32), SDS((8, LANE), F32), SDS((1, D), F32)]
    return pl.pallas_call(
        body, name="ssm_bwd", grid=(NC,),
        in_specs=[col(D, 5), col(512, 12), col(LANE, 52), col(D, 4),
                  col(D, 0), col(D, 0), pl.BlockSpec((None, CH, D), lambda i: (NC - 1 - i, 0, 0)), col(D + 512, 0),
                  full(cw), full(dtb), full(alog), full(d_b), full(nw)],
        out_specs=[col(D, 0), col(2 * D, 0)] + [full(a) for a in acc_shapes],
        out_shape=[SDS((S, D), BF16), SDS((S, 2 * D), BF16)] + acc_shapes,
        scratch_shapes=[pltpu.VMEM((CH, D), F32), pltpu.VMEM((8, D), F32), pltpu.VMEM((8, 512), F32)],
        compiler_params=_cp(("arbitrary",)),
    )(proj, proj, proj, proj, dn, y_save, states, conv_out, cw, dtb, alog, d_b, nw)


def _outproj_loss(mix, w_out, x, tgt, nw, attn_pre, proj):
    tm = 256

    def body(mix_ref, w_ref, x_ref, t_ref, nw_ref, pre_ref, g_ref,
             dy_ref, dn_ref, do_ref, delta_ref, dg_ref, dw_ref, dnw_ref, loss_ref):
        @pl.when(pl.program_id(0) == 0)
        def _():
            dw_ref[...] = jnp.zeros_like(dw_ref)
            dnw_ref[...] = jnp.zeros_like(dnw_ref)
            loss_ref[...] = jnp.zeros_like(loss_ref)

        mixv, w = mix_ref[...], w_ref[...]
        out = _dot(mixv, w)
        r = lax.rsqrt(jnp.mean(out * out, axis=-1, keepdims=True) + EPS)
        nh = out * r
        nw_v = nw_ref[...]
        err = x_ref[...] + nh * nw_v - t_ref[...]
        loss_ref[...] += 0.5 * jnp.sum(jnp.mean(err * err, axis=-1, keepdims=True), axis=0, keepdims=True)
        dy = err * (1.0 / D)
        dy_ref[...] = dy
        dnw_ref[...] += jnp.sum(dy * nh, axis=0, keepdims=True)
        gdn = dy * nw_v
        dout = (r * (gdn - nh * jnp.mean(gdn * nh, axis=-1, keepdims=True))).astype(BF16)
        dmix = _dot_nt(dout, w)
        dw_ref[...] += _dot_tn(mixv, dout)
        dn_ref[...] = dmix[:, D:]
        dm, g, pre_v = dmix[:, :D], g_ref[...], pre_ref[...]
        sig = _sigmoid(g)
        do = dm * (g * sig)
        do_ref[...] = do
        dg_ref[...] = (dm * pre_v * (sig * (1.0 + g * (1.0 - sig)))).astype(BF16)
        prod = do * pre_v
        same_head = (_iota((LANE, LANE), 0) // 64 == _iota((LANE, LANE), 1) // 64).astype(BF16)
        for cb in range(D // LANE):
            delta_ref[:, cb * LANE:(cb + 1) * LANE] = _pick(prod[:, cb * LANE:(cb + 1) * LANE], same_head)

    row = lambda w: pl.BlockSpec((tm, w), lambda i: (i, 0))
    full = lambda s: pl.BlockSpec(s, lambda i: (0, 0))
    return pl.pallas_call(
        body, name="outproj_loss", grid=(S // tm,),
        in_specs=[row(2 * D), full((2 * D, D)), row(D), row(D), full((1, D)), row(D),
                  pl.BlockSpec((tm, D), lambda i: (i, OFF_G // D))],
        out_specs=[row(D), row(D), row(D), row(D), row(D), full((2 * D, D)), full((1, D)), full((1, LANE))],
        out_shape=[SDS((S, D), F32)] * 4 + [SDS((S, D), BF16), SDS((2 * D, D), F32), SDS((1, D), F32),
                                            SDS((1, LANE), F32)],
        compiler_params=_cp(("arbitrary",)),
    )(mix, w_out, x, tgt, nw, attn_pre, proj)


def _inproj_bwd_dx(srcs, dxbcdt, w_all, x, dy, nw, hosted=None):
    tm = 512
    nk = DP // D
    n_host, n_host_out = (len(hosted.arrays), len(hosted.out_shape)) if hosted else (0, 0)

    def body(*refs):
        src_refs = refs[:nk]
        w_ref, x_ref, dy_ref, nw_ref = refs[nk:nk + 4]
        host_in, refs = refs[nk + 4:nk + 4 + n_host], refs[nk + 4 + n_host:]
        gx_ref, dnw_ref = refs[:2]
        host_out, host_sems = refs[2:2 + n_host_out], refs[2 + n_host_out:]
        i = pl.program_id(0)

        @pl.when(i == 0)
        def _():
            if hosted:
                hosted.start(host_in, host_out, host_sems)
            dnw_ref[...] = jnp.zeros_like(dnw_ref)

        du = None
        for k, ref in enumerate(src_refs):
            width = min(D, 5 * D + X_COLS - k * D)
            part = _dot_nt(ref[:, :width], w_ref[:, k * D:k * D + width])
            du = part if du is None else du + part
        xf, nw_v = x_ref[...], nw_ref[...]
        r = lax.rsqrt(jnp.mean(xf * xf, axis=-1, keepdims=True) + EPS)
        xh = xf * r
        dnw_ref[...] += jnp.sum(du * xh, axis=0, keepdims=True)
        gdu = du * nw_v
        gx_ref[...] = r * (gdu - xh * jnp.mean(gdu * xh, axis=-1, keepdims=True)) + dy_ref[...]

        if hosted:
            pl.when(i == S // tm - 1)(lambda: hosted.finish(host_in, host_out, host_sems))

    row = pl.BlockSpec((tm, D), lambda i: (i, 0))
    row1 = pl.BlockSpec((tm, D), lambda i: (i, 1))
    one = pl.BlockSpec((1, D), lambda i: (0, 0))
    whole_w = pl.BlockSpec((D, DP), lambda i: (0, 0), pipeline_mode=pl.Buffered(1))
    args = [*srcs, dxbcdt, dxbcdt, w_all, x, dy, nw]
    in_specs = [row] * len(srcs) + [row, row1, whole_w, row, row, one]
    out_specs, out_shape, scratch = [row, one], [SDS((S, D), F32), SDS((1, D), F32)], []
    if hosted:
        args += hosted.arrays
        in_specs += [ANY] * n_host
        out_specs += [ANY] * n_host_out
        out_shape += hosted.out_shape
        scratch += hosted.scratch
    outs = pl.pallas_call(
        body, name="inproj_bwd_dx", grid=(S // tm,),
        in_specs=in_specs, out_specs=out_specs, out_shape=out_shape, scratch_shapes=scratch,
        compiler_params=_cp(("arbitrary",)),
    )(*args)
    return (outs[:2], outs[2:]) if hosted else outs


def _dw(u, dsec, name, width=D, hosted=None):
    ts = 1024
    n_host, n_host_out = (len(hosted.arrays), len(hosted.out_shape)) if hosted else (0, 0)

    def body(u_ref, d_ref, *refs):
        host_in, o_ref, refs = refs[:n_host], refs[n_host], refs[n_host + 1:]
        host_out, host_sems = refs[:n_host_out], refs[n_host_out:]
        i = pl.program_id(0)

        @pl.when(i == 0)
        def _():
            if hosted:
                hosted.start(host_in, host_out, host_sems)
            o_ref[...] = jnp.zeros_like(o_ref)

        o_ref[...] += _dot_tn(u_ref[...], d_ref[...])
        if hosted:
            pl.when(i == S // ts - 1)(lambda: hosted.finish(host_in, host_out, host_sems))

    outs = pl.pallas_call(
        body, name=name, grid=(S // ts,),
        in_specs=[pl.BlockSpec((ts, D), lambda i: (i, 0)), pl.BlockSpec((ts, width), lambda i: (i, 0))]
        + [ANY] * n_host,
        out_specs=[pl.BlockSpec((D, width), lambda i: (0, 0))] + [ANY] * n_host_out,
        out_shape=[SDS((D, width), F32)] + (hosted.out_shape if hosted else []),
        scratch_shapes=hosted.scratch if hosted else [],
        compiler_params=_cp(("arbitrary",)),
    )(u, dsec, *(hosted.arrays if hosted else []))
    return (outs[0], outs[1:]) if hosted else outs[0]


def _place():
    x, y, c = lax.axis_index("x"), lax.axis_index("y"), lax.axis_index("c")
    return x, y, c, 2 * x + y


def _chip_of(x, y, k):
    px = 1 - x if k & 2 else x
    py = 1 - y if k & 1 else y
    return px, py, 2 * px + py


def _remote(src, dst, send_sem, recv_sem, dev):
    return pltpu.make_async_remote_copy(src_ref=src, dst_ref=dst, send_sem=send_sem, recv_sem=recv_sem,
                                        device_id=dev, device_id_type=MESH)


def _gather_weights(w_in_b):
    half = w_in_b.shape[0] // 2
    quarter = half // 2

    def body(src, dst, send, recv):
        x, y, c, j = _place()
        me, sib = (x, y, c), (x, y, 1 - c)
        nbr = {"x": _chip_of(x, y, 2), "y": _chip_of(x, y, 1)}
        diag = _chip_of(x, y, 3)[2]
        started, arrivals = [], []

        def rows(n_quarter=None, sibling=False):
            base = (1 - c if sibling else c) * half
            return pl.ds(base, half) if n_quarter is None else pl.ds(base + n_quarter * quarter, quarter)

        def sem(n):
            return send.at[n], recv.at[n]

        def go(cp):
            cp.start()
            started.append(cp)

        own = _remote(src, dst.at[j], *sem(8), sib)
        go(own)
        for n, axis in enumerate("xy"):
            px, py, _ = nbr[axis]
            go(_remote(src.at[rows()], dst.at[j, rows()], *sem(n), (px, py, c)))
        for n, axis in enumerate("xy"):
            ox, oy, _ = nbr["y" if axis == "x" else "x"]
            pj = nbr[axis][2]
            _remote(src.at[rows()], dst.at[pj, rows()], *sem(n), me).wait_recv()
            go(_remote(dst.at[pj, rows(n)], dst.at[pj, rows(n)], *sem(2 + n), (ox, oy, c)))
            go(_remote(dst.at[pj, rows()], dst.at[pj, rows()], *sem(4 + n), sib))
            arrivals.append(_remote(src.at[rows()], dst.at[pj, rows(None, True)], *sem(4 + n), me))
        for n in range(2):
            _remote(dst.at[diag, rows(n)], dst.at[diag, rows(n)], *sem(2 + n), me).wait_recv()
            go(_remote(dst.at[diag, rows(n)], dst.at[diag, rows(n)], *sem(6 + n), sib))
            arrivals.append(_remote(dst.at[diag, rows(n, True)], dst.at[diag, rows(n, True)], *sem(6 + n), me))
        for cp in arrivals + [own]:
            cp.wait_recv()
        for cp in started:
            cp.wait_send()

    return pl.pallas_call(
        body, name="gather_weights", in_specs=[ANY], out_specs=ANY,
        out_shape=SDS((4,) + w_in_b.shape, BF16),
        scratch_shapes=[pltpu.SemaphoreType.DMA((9,)), pltpu.SemaphoreType.DMA((9,))],
        compiler_params=pltpu.CompilerParams(has_side_effects=True),
    )(w_in_b)


class _LateGather:
    def __init__(self, w_out_b, conv_w):
        self.arrays = [w_out_b, conv_w]
        self.out_shape = [SDS((4,) + w_out_b.shape, BF16), SDS((4,) + conv_w.shape, F32)]
        self.scratch = [pltpu.SemaphoreType.DMA((11,)), pltpu.SemaphoreType.DMA((11,))]

    def _plan(self, ins, outs, sems):
        x, y, c, j = _place()
        send, recv = sems
        (wo, cw), (gwo, gcw) = ins, outs
        half = wo.shape[0] // 2
        mine, theirs = pl.ds(c * half, half), pl.ds((1 - c) * half, half)
        me, sib = (x, y, c), (x, y, 1 - c)
        first, arrive, forward, last = [], [], [], []
        for k in (1, 2, 3):
            px, py, pj = _chip_of(x, y, k)
            first += [_remote(wo.at[mine], gwo.at[j, mine], send.at[k - 1], recv.at[k - 1], (px, py, c)),
                      _remote(cw, gcw.at[j], send.at[k + 2], recv.at[k + 2], (px, py, c))]
            arrive.append(_remote(wo.at[mine], gwo.at[pj, mine], send.at[k - 1], recv.at[k - 1], me))
            forward.append(_remote(gwo.at[pj, mine], gwo.at[pj, mine], send.at[k + 5], recv.at[k + 5], sib))
            last += [_remote(cw, gcw.at[pj], send.at[k + 2], recv.at[k + 2], me),
                     _remote(wo.at[theirs], gwo.at[pj, theirs], send.at[k + 5], recv.at[k + 5], me)]
        first += [_remote(wo, gwo.at[j], send.at[9], recv.at[9], sib),
                  _remote(cw, gcw.at[j], send.at[10], recv.at[10], sib)]
        last += first[-2:]
        return first, arrive, forward, last

    def start(self, ins, outs, sems):
        for cp in self._plan(ins, outs, sems)[0]:
            cp.start()

    def pass_on(self, ins, outs, sems):
        _, arrive, forward, _ = self._plan(ins, outs, sems)
        for got, fwd in zip(arrive, forward):
            got.wait_recv()
            fwd.start()

    def finish(self, ins, outs, sems):
        first, _, forward, last = self._plan(ins, outs, sems)
        for cp in last:
            cp.wait_recv()
        for cp in first + forward:
            cp.wait_send()


def _window(s, names):
    lo, hi = TILES * s, TILES * s + TILES + 1
    pieces = []
    for n, name in enumerate(names):
        a, count = SECTION_TILES[name]
        first, last = max(lo, a), min(hi, a + count)
        if first < last:
            pieces.append((n, first - a, last - first, first - lo))
    assert sum(p[2] for p in pieces) == TILES + 1
    return pieces


class _PairExchange:
    def __init__(self, names, sections, shards, more=()):
        self.names, self.shards = names, shards
        self.there = [n for n, a in enumerate(sections) if a is not None]
        self.arrays = [sections[n] for n in self.there] + list(more)
        self.out_shape = [SDS((len(shards), D // 2, WIN), F32)]
        self.out_shape += [SDS((a.shape[0], a.shape[1] // 2, a.shape[2]), F32) for a in more]
        n = sum(p[0] in self.there for s in shards for p in _window(s, names)) + len(more)
        self.scratch = [pltpu.SemaphoreType.DMA((n,)) for _ in range(2)]

    def _copies(self, ins, outs, sems):
        x, y, c, _ = _place()
        sib = (x, y, 1 - c)
        rows = pl.ds((1 - c) * (D // 2), D // 2)
        k = 0
        for i, s in enumerate(self.shards):
            for n, tile, tiles, at in _window(s, self.names):
                if n in self.there:
                    yield _remote(ins[self.there.index(n)].at[rows, pl.ds(tile * LANE, tiles * LANE)],
                                  outs[0].at[i, :, pl.ds(at * LANE, tiles * LANE)], sems[0].at[k], sems[1].at[k], sib)
                    k += 1
        for src, dst in zip(ins[len(self.there):], outs[1:]):
            half = src.shape[1] // 2
            yield _remote(src.at[:, pl.ds((1 - c) * half, half)], dst, sems[0].at[k], sems[1].at[k], sib)
            k += 1

    def start(self, ins, outs, sems):
        for cp in self._copies(ins, outs, sems):
            cp.start()

    def finish(self, ins, outs, sems):
        for cp in self._copies(ins, outs, sems):
            cp.wait()


def _exchange_call(exchange, name, into=None):
    n, n_out = len(exchange.arrays), len(exchange.out_shape)
    given = list(into) if into else []

    def body(*refs):
        ins, outs, sems = refs[:n], refs[n + len(given):n + len(given) + n_out], refs[n + len(given) + n_out:]
        exchange.start(ins, outs, sems)
        exchange.finish(ins, outs, sems)

    return pl.pallas_call(
        body, name=name, in_specs=[ANY] * (n + len(given)), out_specs=[ANY] * n_out, out_shape=exchange.out_shape,
        input_output_aliases={n + k: k for k in range(len(given))},
        scratch_shapes=exchange.scratch, compiler_params=pltpu.CompilerParams(has_side_effects=True),
    )(*exchange.arrays, *given)


def _pair_sum_windows(cidx, names, sections, shards, r, name):
    n, half, _ = r.shape
    tr = min(half, 256)
    nt = half // tr

    def body(c_ref, *refs):
        del c_ref
        secs, r_ref, o_ref = refs[:-2], refs[-2], refs[-1]
        for i, s in enumerate(shards):
            for k, tile, tiles, at in _window(s, names):
                own = secs[k][:, tile * LANE:(tile + tiles) * LANE]
                there = slice(at * LANE, (at + tiles) * LANE)
                o_ref[i, :, there] = (own + r_ref[i, :, there]).astype(BF16)

    window = pl.BlockSpec((n, tr, WIN), lambda t, c: (0, t, 0))
    return pl.pallas_call(
        body, name=name,
        grid_spec=pltpu.PrefetchScalarGridSpec(
            num_scalar_prefetch=1, grid=(nt,),
            in_specs=[pl.BlockSpec((tr, a.shape[1]), lambda t, c: (c[0] * nt + t, 0)) for a in sections] + [window],
            out_specs=window),
        out_shape=SDS(r.shape, BF16),
        compiler_params=_cp(("parallel",)),
    )(cidx, *sections, r)


def _pair_sum(cidx, g, r, name):
    n, half, width = r.shape
    tr = min(half, 256)
    nt = half // tr

    def body(c_ref, g_ref, r_ref, o_ref):
        del c_ref
        o_ref[...] = (g_ref[...] + r_ref[...]).astype(BF16)

    return pl.pallas_call(
        body, name=name,
        grid_spec=pltpu.PrefetchScalarGridSpec(
            num_scalar_prefetch=1, grid=(n, nt),
            in_specs=[pl.BlockSpec((None, tr, width), lambda s, t, c: (s, c[0] * nt + t, 0)),
                      pl.BlockSpec((None, tr, width), lambda s, t, c: (s, t, 0))],
            out_specs=pl.BlockSpec((None, tr, width), lambda s, t, c: (s, t, 0))),
        out_shape=SDS(r.shape, BF16),
        compiler_params=_cp(("parallel", "parallel")),
    )(cidx, g, r)


class _ChipExchange:
    def __init__(self, arrays, rows):
        self.arrays, self.rows = list(arrays), list(rows)
        self.out_shape = [SDS((4,) + a.shape[1:], BF16) for a in self.arrays]
        self.scratch = [pltpu.SemaphoreType.DMA((3 * len(self.arrays),)) for _ in range(2)]

    def _copies(self, ins, outs, sems):
        x, y, c, j = _place()
        send, recv = sems
        for a, (src, dst, row) in enumerate(zip(ins, outs, self.rows)):
            for k in (1, 2, 3):
                px, py, pj = _chip_of(x, y, k)
                n = 3 * a + k - 1
                slot = pj if row is None else py
                yield (None if row is None else px == row, None if row is None else x == row,
                       _remote(src.at[slot], dst.at[j], send.at[n], recv.at[n], (px, py, c)),
                       _remote(src.at[0], dst.at[pj], send.at[n], recv.at[n], (x, y, c)))

    def start(self, ins, outs, sems):
        for sends, _, send, _ in self._copies(ins, outs, sems):
            if sends is None:
                send.start()
            else:
                pl.when(sends)(send.start)

    def finish(self, ins, outs, sems):
        for sends, owns, send, arrival in self._copies(ins, outs, sems):
            if sends is None:
                arrival.wait_recv()
                send.wait_send()
            else:
                pl.when(owns)(arrival.wait_recv)
                pl.when(sends)(send.wait_send)


def _all_gather_rows(src, dst, rows, send, recv, local_sem):
    x, y, c, j = _place()
    me = 2 * j + c
    local = pltpu.make_async_copy(src, dst.at[me, rows], local_sem)
    cps, arrivals = [], []
    for k in range(1, 8):
        px, py, pj = _chip_of(x, y, k >> 1)
        pc = 1 - c if k & 1 else c
        cps.append(_remote(src, dst.at[me, rows], send.at[k - 1], recv.at[k - 1], (px, py, pc)))
        arrivals.append(_remote(src, dst.at[2 * pj + pc, rows], send.at[k - 1], recv.at[k - 1], (x, y, c)))
    starts = [local.start] + [cp.start for cp in cps]
    waits = [cp.wait_recv for cp in arrivals] + [cp.wait_send for cp in cps] + [local.wait]
    return starts, waits


class _SmallExchange:
    def __init__(self, small):
        self.arrays = [small]
        self.out_shape = [SDS((8,) + small.shape, F32)]
        self.scratch = [pltpu.SemaphoreType.DMA((7,)), pltpu.SemaphoreType.DMA((7,)), pltpu.SemaphoreType.DMA]

    def start(self, ins, outs, sems):
        for go in _all_gather_rows(ins[0], outs[0], slice(None), *sems)[0]:
            go()

    def finish(self, ins, outs, sems):
        for wait in _all_gather_rows(ins[0], outs[0], slice(None), *sems)[1]:
            wait()


class _Both:
    def __init__(self, a, b):
        self.parts = (a, b)
        self.arrays, self.out_shape, self.scratch = a.arrays + b.arrays, a.out_shape + b.out_shape, a.scratch + b.scratch

    def _split(self, ins, outs, sems):
        a, b = self.parts
        return ((a, ins[:len(a.arrays)], outs[:len(a.out_shape)], sems[:len(a.scratch)]),
                (b, ins[len(a.arrays):], outs[len(a.out_shape):], sems[len(a.scratch):]))

    def start(self, ins, outs, sems):
        for part, *refs in self._split(ins, outs, sems):
            part.start(*refs)

    def finish(self, ins, outs, sems):
        for part, *refs in self._split(ins, outs, sems):
            part.finish(*refs)


def _slot_sum(r, name):
    n, rows, width = r.shape
    tr = min(rows, 256)

    def body(r_ref, o_ref):
        acc = r_ref[0].astype(F32)
        for s in range(1, n):
            acc = acc + r_ref[s].astype(F32)
        o_ref[...] = acc

    return pl.pallas_call(
        body, name=name, grid=(rows // tr,),
        in_specs=[pl.BlockSpec((n, tr, width), lambda t: (0, t, 0))],
        out_specs=pl.BlockSpec((tr, width), lambda t: (t, 0)),
        out_shape=SDS((rows, width), F32),
        compiler_params=_cp(("parallel",)),
    )(r)


def _chip_sum(where, recv, own, name):
    n, rows, width = recv.shape
    tr = min(rows, 256)
    nt = rows // tr

    def body(j_ref, r_ref, own_ref, o_ref):
        acc = None
        for s in range(n):
            term = jnp.where(j_ref[0] == s, own_ref[...], r_ref[s]).astype(F32)
            acc = term if acc is None else acc + term
        o_ref[...] = acc

    return pl.pallas_call(
        body, name=name,
        grid_spec=pltpu.PrefetchScalarGridSpec(
            num_scalar_prefetch=1, grid=(nt,),
            in_specs=[pl.BlockSpec((n, tr, width), lambda t, j: (0, t, 0)),
                      pl.BlockSpec((None, tr, width), lambda t, j: (j[0], t, 0))],
            out_specs=pl.BlockSpec((tr, width), lambda t, j: (j[1] * nt + t, 0))),
        out_shape=SDS((2 * rows, width), F32),
        compiler_params=_cp(("parallel",)),
    )(where, recv, own)


def _chip_sum_rows(place, recv0, own0, recv1, own1, name):
    n, rows, width = recv0.shape
    tr = min(rows, 256)
    nt = rows // tr

    def body(p_ref, r0_ref, o0_ref, r1_ref, o1_ref, o_ref):
        first_row = p_ref[2] == 0
        own = jnp.where(first_row, o0_ref[...], o1_ref[...])
        acc = None
        for s in range(n):
            term = jnp.where(p_ref[0] == s, own, jnp.where(first_row, r0_ref[s], r1_ref[s])).astype(F32)
            acc = term if acc is None else acc + term
        o_ref[...] = acc

    recv = pl.BlockSpec((n, tr, width), lambda t, p: (0, t, 0))
    own = pl.BlockSpec((None, tr, width), lambda t, p: (p[3], t, 0))
    return pl.pallas_call(
        body, name=name,
        grid_spec=pltpu.PrefetchScalarGridSpec(
            num_scalar_prefetch=1, grid=(nt,), in_specs=[recv, own, recv, own],
            out_specs=pl.BlockSpec((tr, width), lambda t, p: (p[1] * nt + t, 0))),
        out_shape=SDS((2 * rows, width), F32),
        compiler_params=_cp(("parallel",)),
    )(place, recv0, own0, recv1, own1)


def _half_exchange(gw, go, gathered, late, row):
    def body(gw_in, go_in, ga_in, late_ref, gw_ref, go_ref, ga_ref, send, recv, late_send, late_recv, late_local):
        del gw_in, go_in, ga_in
        x, y, c, _ = _place()
        starts, waits = _all_gather_rows(late_ref, ga_ref, pl.ds(row, late.shape[0]), late_send, late_recv,
                                         late_local)
        for go_ in starts:
            go_()
        mine = [pl.ds(c * (r.shape[0] // 2), r.shape[0] // 2) for r in (gw_ref, go_ref)]
        cps = [_remote(r.at[rows], r.at[rows], send.at[k], recv.at[k], (x, y, 1 - c))
               for k, (r, rows) in enumerate(zip((gw_ref, go_ref), mine))]
        for cp in cps:
            cp.start()
        for k, r in enumerate((gw_ref, go_ref)):
            theirs = pl.ds((1 - c) * (r.shape[0] // 2), r.shape[0] // 2)
            _remote(r.at[theirs], r.at[theirs], send.at[k], recv.at[k], (x, y, c)).wait_recv()
        for cp in cps:
            cp.wait_send()
        for wait in waits:
            wait()

    return pl.pallas_call(
        body, name="half_exchange", in_specs=[ANY] * 4, out_specs=[ANY] * 3,
        out_shape=[SDS(gw.shape, F32), SDS(go.shape, F32), SDS(gathered.shape, F32)],
        input_output_aliases={0: 0, 1: 1, 2: 2},
        scratch_shapes=[pltpu.SemaphoreType.DMA((2,)), pltpu.SemaphoreType.DMA((2,)),
                        pltpu.SemaphoreType.DMA((7,)), pltpu.SemaphoreType.DMA((7,)), pltpu.SemaphoreType.DMA],
        compiler_params=pltpu.CompilerParams(has_side_effects=True),
    )(gw, go, gathered, late)


def _adamw(w, g, m, v, name):
    rows, width = w.shape
    tr = min(rows, 256)

    def body(w_ref, g_ref, m_ref, v_ref, d_ref, nm_ref, nv_ref):
        gv = g_ref[...]
        nm = ADAM_B1 * m_ref[...] + (1.0 - ADAM_B1) * gv
        nv = ADAM_B2 * v_ref[...] + (1.0 - ADAM_B2) * (gv * gv)
        m_hat = nm / (1.0 - ADAM_B1 ** ADAM_STEP)
        v_hat = nv / (1.0 - ADAM_B2 ** ADAM_STEP)
        d_ref[...] = -ADAM_LR * (m_hat / (jnp.sqrt(v_hat) + ADAM_EPS) + ADAM_WD * w_ref[...])
        nm_ref[...] = nm
        nv_ref[...] = nv

    t = pl.BlockSpec((tr, width), lambda i: (i, 0))
    return pl.pallas_call(
        body, name=name, grid=(rows // tr,), in_specs=[t] * 4, out_specs=[t] * 3,
        out_shape=[SDS(w.shape, F32)] * 3, compiler_params=_cp(("parallel",)),
    )(w, g, m, v)


def _rowwise(a):
    return jnp.transpose(a, (2, 0, 1)).reshape(SHARD * D // LANE, LANE)


def _columns(ref, base=0):
    return jnp.concatenate([ref[pl.ds(base + c, LANE, stride=8), :].T for c in range(D // LANE)], axis=0)


def _shard_bf16(chip, w_rows):
    def body(j_ref, w_ref, o_ref, prev_ref):
        t = pl.program_id(0)
        cur = _columns(w_ref)

        @pl.when(t == 0)
        def _():
            prev_ref[...] = jnp.zeros_like(prev_ref)

        lane = _iota((D, LANE), 1)
        for s in range(4):
            @pl.when(j_ref[0] == s)
            def _():
                off = SHIFT * s
                moved = cur if s == 0 else jnp.where(lane < off, pltpu.roll(prev_ref[...], off, 1),
                                                     pltpu.roll(cur, off, 1))
                col = t * LANE + lane - off
                o_ref[...] = jnp.where((col >= 0) & (col < SHARD), moved, 0.0).astype(BF16)
        prev_ref[...] = cur

    return pl.pallas_call(
        body, name="shard_bf16",
        grid_spec=pltpu.PrefetchScalarGridSpec(
            num_scalar_prefetch=1, grid=(TILES + 1,),
            in_specs=[pl.BlockSpec((D, LANE), lambda t, j: (t, 0))],
            out_specs=pl.BlockSpec((D, LANE), lambda t, j: (0, t)),
            scratch_shapes=[pltpu.VMEM((D, LANE), F32)]),
        out_shape=SDS((D, WIN), BF16), compiler_params=_cp(("arbitrary",)),
    )(chip, w_rows)


def _whole_w_in(windows):
    tr = 256
    n = windows.shape[0]

    def body(g_ref, o_ref):
        lane = _iota((tr, LANE), 1)
        for s in range(n):
            first = TILES * s
            head = g_ref[s, :, :LANE]
            if s:
                tail = g_ref[s - 1, :, TILES * LANE:]
                head = jnp.where(lane < SHIFT * s, tail.astype(F32), head.astype(F32)).astype(BF16)
            o_ref[:, first * LANE:(first + 1) * LANE] = head
            o_ref[:, (first + 1) * LANE:(first + TILES) * LANE] = g_ref[s, :, LANE:TILES * LANE]
        o_ref[:, n * TILES * LANE:(n * TILES + 1) * LANE] = g_ref[n - 1, :, TILES * LANE:]
        o_ref[:, (n * TILES + 1) * LANE:] = jnp.zeros((tr, DP - (n * TILES + 1) * LANE), BF16)

    return pl.pallas_call(
        body, name="whole_w_in", grid=(D // tr,),
        in_specs=[pl.BlockSpec((n, tr, WIN), lambda t: (0, t, 0))], out_specs=pl.BlockSpec((tr, DP), lambda t: (t, 0)),
        out_shape=SDS((D, DP), BF16), compiler_params=_cp(("parallel",)),
    )(windows)


def _own_buffer(a, name):
    tr = 512
    block = pl.BlockSpec((tr, a.shape[1]), lambda t: (t, 0))

    def body(a_ref, o_ref):
        o_ref[...] = a_ref[...]

    return pl.pallas_call(
        body, name=name, grid=(a.shape[0] // tr,), in_specs=[block], out_specs=block,
        out_shape=SDS(a.shape, a.dtype), compiler_params=_cp(("parallel",)),
    )(a)


def _shard_of_window(chip, g_win):
    tr = 128

    def body(j_ref, g_ref, grad_ref):
        for s in range(4):
            @pl.when(j_ref[0] == s)
            def _():
                back = LANE - SHIFT * s
                from_this = _iota((tr, LANE), 1) < back

                def moved(t):
                    tile = g_ref[:, t * LANE:(t + 1) * LANE]
                    return pltpu.roll(tile, back, 1) if s else tile

                for t in range(TILES):
                    grad_ref[:, t * LANE:(t + 1) * LANE] = jnp.where(from_this, moved(t), moved(t + 1)) if s else moved(t)
                grad_ref[:, TILES * LANE:] = moved(TILES)[:, :SHARD - TILES * LANE]

    return pl.pallas_call(
        body, name="shard_of_window",
        grid_spec=pltpu.PrefetchScalarGridSpec(
            num_scalar_prefetch=1, grid=(D // tr,), in_specs=[pl.BlockSpec((tr, WIN), lambda t, j: (t, 0))],
            out_specs=pl.BlockSpec((tr, SHARD), lambda t, j: (t, 0))),
        out_shape=SDS((D, SHARD), F32), compiler_params=_cp(("parallel",)),
    )(chip, g_win)


def _adamw_in(w_rows, g, m_rows, v_rows):
    per_step = 2

    def body(w_ref, g_ref, m_ref, v_ref, d_ref, nm_ref, nv_ref):
        for a in range(per_step):
            cols = slice(a * LANE, (a + 1) * LANE)
            gv = g_ref[:, cols]
            nm = ADAM_B1 * _columns(m_ref, a * D) + (1.0 - ADAM_B1) * gv
            nv = ADAM_B2 * _columns(v_ref, a * D) + (1.0 - ADAM_B2) * (gv * gv)
            m_hat = nm / (1.0 - ADAM_B1 ** ADAM_STEP)
            v_hat = nv / (1.0 - ADAM_B2 ** ADAM_STEP)
            d_ref[:, cols] = -ADAM_LR * (m_hat / (jnp.sqrt(v_hat) + ADAM_EPS) + ADAM_WD * _columns(w_ref, a * D))
            nm_ref[:, cols] = nm
            nv_ref[:, cols] = nv

    tile = pl.BlockSpec((D, per_step * LANE), lambda t: (0, t))
    rows = pl.BlockSpec((per_step * D, LANE), lambda t: (t, 0))
    return pl.pallas_call(
        body, name="adamw_in", grid=(pl.cdiv(TILES + 1, per_step),), in_specs=[rows, tile, rows, rows],
        out_specs=[tile] * 3, out_shape=[SDS(g.shape, F32)] * 3, compiler_params=_cp(("parallel",)),
    )(w_rows, g, m_rows, v_rows)


def _rows128(a, rows):
    flat = a.reshape(-1)
    return jnp.pad(flat, (0, rows * LANE - flat.shape[0])).reshape(rows, LANE)


CONV_ROWS = 48


def _pack_small(conv_w, norm_pre, conv_b, ssm_norm, norm_post, dtb, alog, dsk, extra=None):
    cw_rows = CONV_ROWS if conv_w.shape[-1] == 1536 else 16
    extra = jnp.zeros((1, LANE), F32) if extra is None else _rows128(extra, 1)
    vec = jnp.concatenate([_rows128(dtb, 1), _rows128(alog, 1), _rows128(dsk, 1), extra, jnp.zeros((4, LANE), F32)],
                          axis=0)
    return jnp.concatenate([_rows128(conv_w, cw_rows), _rows128(norm_pre, 8), _rows128(conv_b, 16),
                            _rows128(ssm_norm, 8), _rows128(norm_post, 8), vec], axis=0)


def _unpack_small(p, cw_cols):
    cw_rows = CONV_ROWS if cw_cols == 1536 else 16
    o = cw_rows
    conv_w = p[:cw_rows].reshape(-1)[:4 * cw_cols].reshape(1, 4, cw_cols)
    norm_pre = p[o:o + 8].reshape(1, D)
    conv_b = p[o + 8:o + 24].reshape(-1)[:1536].reshape(1, 1536)
    ssm_norm = p[o + 24:o + 32].reshape(1, D)
    norm_post = p[o + 32:o + 40].reshape(1, D)
    vec = p[o + 40:o + 48]
    return conv_w, norm_pre, conv_b, ssm_norm, norm_post, vec[0:1, :NH], vec[1:2, :NH], vec[2:3, :NH], vec[3, 0]


def _pad_lanes(a):
    return jnp.pad(a, ((0, 0), (0, LANE - a.shape[1])))


class _GradReduce:
    LO, HI = ("q", "k", "v", "g"), ("g", "z", "x")

    def __init__(self, xi, yi, ci):
        self.cidx = jnp.reshape(ci, (1,)).astype(jnp.int32)
        self.place = jnp.stack([2 * xi + yi, ci, xi, yi]).astype(jnp.int32)

    def pairs(self, dw_g, dw_z, dw_x, dw_out):
        self.hi = [dw_g, dw_z, dw_x]
        self.go = dw_out.reshape(4, D // 2, D)
        return _PairExchange(self.HI, self.hi, (2, 3), [self.go])

    def first(self, got):
        rw, ro = got
        self.pw_hi = _pair_sum_windows(self.cidx, self.HI, self.hi, (2, 3), rw, "pair_sum_hi")
        self.po = _pair_sum(self.cidx, self.go, ro, "pair_sum_out")
        return _ChipExchange([self.pw_hi, self.po], [1, None])

    def first_done(self, got):
        self.rw_hi, self.ro = got

    def second_pairs(self, dw_q, dw_k, dw_g):
        self.lo = [dw_q, dw_k, None, dw_g]
        return _PairExchange(self.LO, self.lo, (0, 1))

    def second(self, dw_v, got, small):
        rest = _PairExchange(self.LO, [None, None, dw_v, None], (0, 1))
        (rw,) = _exchange_call(rest, "pair_exchange_v", into=got)
        lo = [dw_v if a is None else a for a in self.lo]
        self.pw_lo = _pair_sum_windows(self.cidx, self.LO, lo, (0, 1), rw, "pair_sum_lo")
        return _Both(_ChipExchange([self.pw_lo], [0]), _SmallExchange(small))

    def second_done(self, got):
        self.rw_lo, self.small = got

    def result(self, late, row):
        half_in = _chip_sum_rows(self.place, self.rw_lo, self.pw_lo, self.rw_hi, self.pw_hi, "chip_sum_in")
        half_out = _chip_sum(self.place[0:2], self.ro, self.po, "chip_sum_out")
        return _half_exchange(half_in, half_out, self.small, late, row)


def kernel(x, norm_pre_w, w_in, conv_w, conv_b, dt_bias, a_log, d_skip, ssm_norm_w, w_out, norm_post_w, loss_target, m_norm_pre_w, m_w_in, m_conv_w, m_conv_b, m_dt_bias, m_a_log, m_d_skip, m_ssm_norm_w, m_w_out, m_norm_post_w, v_norm_pre_w, v_w_in, v_conv_w, v_conv_b, v_dt_bias, v_a_log, v_d_skip, v_ssm_norm_w, v_w_out, v_norm_post_w):
    xi, yi, ci = lax.axis_index("x"), lax.axis_index("y"), lax.axis_index("c")
    chip = 2 * xi + yi
    x2, tgt = x[0], loss_target[0]

    chip_idx = jnp.reshape(chip, (1,)).astype(jnp.int32)
    w_rows = _rowwise(w_in)
    w_all = _whole_w_in(_gather_weights(_shard_bf16(chip_idx, w_rows)))
    reduce = _GradReduce(xi, yi, ci)
    grad_x, dnw_pre = _local_step(x2, tgt, w_all, _LateGather(w_out[0].astype(BF16), conv_w[0]), norm_pre_w, conv_b,
                                  dt_bias, a_log, d_skip, ssm_norm_w, norm_post_w, reduce)
    g_win, g_out, small = reduce.result(_rows128(dnw_pre, D // LANE), CONV_ROWS)
    g_small = _slot_sum(small, "small_sum")
    g_cw, g_npre, g_cb, g_nssm, g_npost, g_dtb, g_alog, g_dsk, loss = _unpack_small(g_small, 1536)
    g_cw = lax.dynamic_slice_in_dim(g_cw, chip * 384, 384, axis=2)

    g_in = _shard_of_window(chip_idx, g_win)
    d_in, nm_in, nv_in = _adamw_in(w_rows, g_in, _rowwise(m_w_in), _rowwise(v_w_in))
    grad_x = _own_buffer(grad_x, "grad_x_copy")
    d_out, nm_out, nv_out = _adamw(w_out[0], g_out, m_w_out[0], v_w_out[0], "adamw_out")
    packed = [_pack_small(*t) for t in (
        (conv_w, norm_pre_w, conv_b, ssm_norm_w, norm_post_w, dt_bias, a_log, d_skip),
        (g_cw, g_npre, g_cb, g_nssm, g_npost, g_dtb, g_alog, g_dsk),
        (m_conv_w, m_norm_pre_w, m_conv_b, m_ssm_norm_w, m_norm_post_w, m_dt_bias, m_a_log, m_d_skip),
        (v_conv_w, v_norm_pre_w, v_conv_b, v_ssm_norm_w, v_norm_post_w, v_dt_bias, v_a_log, v_d_skip))]
    small_out = [_unpack_small(p, 384)[:8] for p in _adamw(*packed, "adamw_small")]

    def ordered(cw_, npre, cb_, nssm, npost, dtb_, alog_, dsk_, big_in, big_out):
        return [npre, big_in[None], cw_, cb_, dtb_, alog_, dsk_, nssm, big_out[None], npost]

    grads = ordered(g_cw, g_npre, g_cb, g_nssm, g_npost, g_dtb, g_alog, g_dsk, g_in, g_out)
    deltas = ordered(*small_out[0], d_in, d_out)
    new_m = ordered(*small_out[1], nm_in, nm_out)
    new_v = ordered(*small_out[2], nv_in, nv_out)
    return (loss, grad_x[None], *grads, *deltas, *new_m, *new_v)


def _local_step(x2, tgt, w_all, late, norm_pre_w, conv_b, dt_bias, a_log, d_skip, ssm_norm_w,
                norm_post_w, reduce=None):
    dtb, alog = _pad_lanes(dt_bias), _pad_lanes(a_log)
    d_b = jnp.repeat(d_skip, 64, axis=1)

    if isinstance(late, _LateGather):
        (proj, u), (gout, gcw) = _inproj_fwd(x2, norm_pre_w, w_all, late)
        w_out_all = gout.reshape(2 * D, D)
        cw_all = jnp.concatenate([gcw[0], gcw[1], gcw[2], gcw[3]], axis=1)
    else:
        proj, u = _inproj_fwd(x2, norm_pre_w, w_all)
        w_out_all, cw_all = late
    mix, attn_pre, lse = _attn_fwd(proj, 1, _attn_fwd(proj, 4, _attn_fwd(proj, 16)), final=True)
    mix, y_save, states, conv_out = _ssm_fwd(proj, mix, cw_all, conv_b, dtb, alog, d_b, ssm_norm_w)

    dy, dn_ssm, do, delta, dg, dw_out, dnw_post, loss_part = _outproj_loss(mix, w_out_all, x2, tgt, norm_post_w,
                                                                          attn_pre, proj)
    dz, dxbcdt, dcw, dcb, dvec, dnw_ssm = _ssm_bwd(proj, dn_ssm, y_save, states, conv_out, cw_all, dtb, alog, d_b,
                                                   ssm_norm_w)
    dw_g, dw_z, dw_x = _dw(u, dg, "dw_in_g"), _dw(u, dz, "dw_in_z"), _dw(u, dxbcdt, "dw_in_xbcdt", X_COLS)
    acc = _attn_bwd(proj, do, lse, delta, 16, None, F32, reduce.pairs(dw_g, dw_z, dw_x, dw_out) if reduce else None)
    if reduce:
        acc, got = acc
    acc = _attn_bwd(proj, do, lse, delta, 4, acc, F32, reduce.first(got) if reduce else None)
    if reduce:
        acc, got = acc
        reduce.first_done(got)
    dq, dk, dv = _attn_bwd(proj, do, lse, delta, 1, acc, BF16)
    dw_q, dw_k = _dw(u, dq, "dw_in_q"), _dw(u, dk, "dw_in_k")
    dw_v = _dw(u, dv, "dw_in_v", hosted=reduce.second_pairs(dw_q, dw_k, dw_g) if reduce else None)
    if reduce:
        dw_v, got = dw_v

    def small(dnw_pre):
        return _pack_small(dcw, dnw_pre, dcb, dnw_ssm, dnw_post, dvec[0:1, :NH], dvec[1:2, :NH], dvec[2:3, :NH],
                           loss_part[:, :1])

    res = _inproj_bwd_dx([dq, dk, dv, dg, dz], dxbcdt, w_all, x2, dy, norm_pre_w,
                         reduce.second(dw_v, got, small(jnp.zeros((1, D), F32))) if reduce else None)
    if reduce:
        res, got = res
        reduce.second_done(got)
        return res
    grad_x, dnw_pre = res
    dw_all = jnp.concatenate([dw_q, dw_k, dw_v, dw_g, dw_z, dw_x], axis=1)
    return grad_x, small(dnw_pre), dw_all, dw_out
```

```python
import functools

import jax
import jax.numpy as jnp
from jax import lax
from jax.experimental import pallas as pl
from jax.experimental.pallas import tpu as pltpu

F32 = jnp.float32
BF16 = jnp.bfloat16
MESH = pl.DeviceIdType.MESH
SDS = jax.ShapeDtypeStruct
ANY = pl.BlockSpec(memory_space=pl.ANY)

S = 4096
D = 1024
DP = 7168
SHARD = 1668
OFF_G, OFF_Z = 3072, 4096
NH = 16
CH = 128
NC = S // CH
EPS = 1e-6
NEG = -1e30
LANE = 128
VMEM_LIMIT = 48 * 1024 * 1024

TILES = SHARD // LANE
WIN = (TILES + 1) * LANE
SHIFT = SHARD - TILES * LANE
SECTION_TILES = {"q": (0, 8), "k": (8, 8), "v": (16, 8), "g": (24, 8), "z": (32, 8), "x": (40, 13)}
X_COLS = SECTION_TILES["x"][1] * LANE

ADAM_LR, ADAM_B1, ADAM_B2, ADAM_EPS, ADAM_WD, ADAM_STEP = 0.001, 0.9, 0.999, 1e-08, 0.01, 10


def _cp(sem, **kw):
    return pltpu.CompilerParams(dimension_semantics=sem, vmem_limit_bytes=VMEM_LIMIT, **kw)


def _dot(a, b):
    return jnp.dot(a, b, preferred_element_type=F32)


def _dot_nt(a, b):
    return lax.dot_general(a, b, (((1,), (1,)), ((), ())), preferred_element_type=F32)


def _dot_tn(a, b):
    return lax.dot_general(a, b, (((0,), (0,)), ((), ())), preferred_element_type=F32)


def _pieces(x, n):
    out = []
    for _ in range(n):
        p = x.astype(BF16)
        out.append(p)
        x = x - p.astype(F32)
    return out


def _pick(x, sel, n=2):
    parts = [_dot(p, sel) for p in _pieces(x, n)]
    return functools.reduce(jnp.add, parts)


def _pick_left(sel, x, n=3):
    parts = [_dot(sel, p) for p in _pieces(x, n)]
    return functools.reduce(jnp.add, parts)


def _sigmoid(v):
    return 0.5 * jnp.tanh(0.5 * v) + 0.5


def _iota(shape, dim):
    return lax.broadcasted_iota(jnp.int32, shape, dim)


def _inproj_fwd(x, nw, w_all, hosted=None):
    tm, tn = 1024, 1024
    n_host = len(hosted.arrays) if hosted else 0

    def body(x_ref, nw_ref, w_ref, *refs):
        host_in, (proj_ref, u_ref), refs = refs[:n_host], refs[n_host:n_host + 2], refs[n_host + 2:]
        host_out, host_sems = refs[:n_host], refs[n_host:]
        i, j = pl.program_id(0), pl.program_id(1)
        if hosted:
            pl.when((i == 0) & (j == 0))(lambda: hosted.start(host_in, host_out, host_sems))

        @pl.when(j == 0)
        def _():
            xf = x_ref[...]
            r = lax.rsqrt(jnp.mean(xf * xf, axis=-1, keepdims=True) + EPS)
            u_ref[...] = (xf * r * nw_ref[...]).astype(BF16)

        proj_ref[...] = _dot(u_ref[...], w_ref[...])
        if hosted:
            pl.when((i == S // tm // 2) & (j == 0))(lambda: hosted.pass_on(host_in, host_out, host_sems))
            pl.when((i == S // tm - 1) & (j == DP // tn - 1))(lambda: hosted.finish(host_in, host_out, host_sems))

    outs = pl.pallas_call(
        body, name="inproj_fwd", grid=(S // tm, DP // tn),
        in_specs=[pl.BlockSpec((tm, D), lambda i, j: (i, 0)), pl.BlockSpec((1, D), lambda i, j: (0, 0)),
                  pl.BlockSpec((D, tn), lambda i, j: (0, j))] + [ANY] * n_host,
        out_specs=[pl.BlockSpec((tm, tn), lambda i, j: (i, j)), pl.BlockSpec((tm, D), lambda i, j: (i, 0))]
        + [ANY] * n_host,
        out_shape=[SDS((S, DP), F32), SDS((S, D), BF16)] + (hosted.out_shape if hosted else []),
        scratch_shapes=hosted.scratch if hosted else [],
        compiler_params=_cp(("arbitrary", "arbitrary") if hosted else ("parallel", "arbitrary")),
    )(x, nw, w_all, *(hosted.arrays if hosted else []))
    return (outs[:2], outs[2:]) if hosted else outs


ATTN_QB = {1: 16, 4: 4, 16: 1}


def _unit_rows(r, u, d):
    return pl.ds(r + d * CH * u, CH, stride=d) if d > 1 else pl.ds(CH * u, CH)


def _for_units(d, qb, fn):
    for r in range(d):
        for u in range(qb):
            fn(r, u)


def _attn_mask(has_prev):
    qi, kj = _iota((2 * CH, 2 * CH), 0) & (CH - 1), _iota((2 * CH, 2 * CH), 1)
    cur_ok = (kj >= CH) & (kj - CH <= qi)
    prev_ok = (kj < CH) & (kj >= qi)
    return cur_ok | (prev_ok & has_prev)


def _stack_heads(v, lane_a):
    return jnp.concatenate([jnp.where(lane_a, v, 0.0), jnp.where(lane_a, 0.0, v)], axis=0).astype(BF16)


def _attn_specs(d, qb):
    rows, prows = CH * d * qb, CH * d
    nb = S // rows
    steps = (NH // 2) * nb

    def at(t):
        t = jnp.minimum(t, steps - 1)
        return t % nb, t // nb

    def cur(off):
        return pl.BlockSpec((rows, LANE), lambda t: (at(t)[0], off + at(t)[1]))

    def prev(off):
        return pl.BlockSpec((prows, LANE), lambda t: (jnp.maximum(at(t)[0] * qb - 1, 0), off + at(t)[1]))

    lag = pl.BlockSpec((rows, LANE), lambda t: at(jnp.maximum(t - 1, 0)))
    return nb, steps, cur, prev, lag


def _gather16(src_ref, dense_ref, tmp_ref):
    for a in range(4):
        tmp_ref[...] = src_ref[pl.ds(a, 4 * CH, stride=4), :]
        for b in range(4):
            dense_ref[a + 4 * b] = tmp_ref[pl.ds(b, CH, stride=4), :]


def _scatter16(dense_ref, dst_ref, tmp_ref):
    for a in range(4):
        for b in range(4):
            tmp_ref[pl.ds(b, CH, stride=4), :] = dense_ref[a + 4 * b]
        dst_ref[pl.ds(a, 4 * CH, stride=4), :] = tmp_ref[...]


def _unit_index(r, u, d):
    return (r,) if d == 16 else (_unit_rows(r, u, d), slice(None))


def _unit_kv(p_ref, c_ref, r, u, d):
    prev = p_ref[_unit_index(r, 0, d)] if u == 0 else c_ref[_unit_index(r, u - 1, d)]
    return jnp.concatenate([prev, c_ref[_unit_index(r, u, d)]], axis=0).astype(BF16)


def _dense_scratch(d, n):
    return [pltpu.VMEM((16, CH, LANE), F32)] * n + [pltpu.VMEM((4 * CH, LANE), F32)] if d == 16 else []


def _attn_fwd(proj, d, prior=None, final=False):
    qb = ATTN_QB[d]
    nb, steps, cur, prev, _ = _attn_specs(d, qb)
    n_prior = 2 if prior is not None else 0
    n_in, n_out = 5 + n_prior + final, 2 + final
    assert not (d == 16 and (n_prior or final))

    def body(*refs):
        ins, outs, scratch = refs[:n_in], refs[n_in:n_in + n_out], refs[n_in + n_out:]
        if d == 16:
            tmp_ref = scratch[-1]
            for src, dense in zip(ins, scratch):
                _gather16(src, dense, tmp_ref)
            block_outs, ins, outs = outs, scratch[:n_in], scratch[n_in:n_in + n_out]
        q_ref, kp_ref, kc_ref, vp_ref, vc_ref = ins[:5]
        prior_refs = ins[5:5 + n_prior]
        if final:
            g_ref, (mix_ref, o_ref, l_ref) = ins[-1], outs
        else:
            o_ref, l_ref = outs
        i = pl.program_id(0) % nb
        lane_a = _iota((CH, LANE), 1) < 64
        mask_first, mask_rest = _attn_mask(i > 0), _attn_mask(True)

        def unit(r, u):
            at = _unit_index(r, u, d)
            q2 = _stack_heads(q_ref[at] * 0.125, lane_a)
            k2, v2 = _unit_kv(kp_ref, kc_ref, r, u, d), _unit_kv(vp_ref, vc_ref, r, u, d)
            s = jnp.where(mask_first if u == 0 else mask_rest, _dot_nt(q2, k2), NEG)
            m = jnp.max(s, axis=1, keepdims=True)
            p = jnp.exp(s - m)
            l = jnp.sum(p, axis=1, keepdims=True)
            o2 = _dot(p.astype(BF16), v2) / l
            lse2 = m + jnp.log(l)
            o = jnp.where(lane_a, o2[:CH], o2[CH:])
            lse = jnp.where(lane_a, lse2[:CH], lse2[CH:])
            if n_prior:
                o_a, l_a = prior_refs[0][at], prior_refs[1][at]
                top = jnp.maximum(l_a, lse)
                e_a, e_b = jnp.exp(l_a - top), jnp.exp(lse - top)
                tot = e_a + e_b
                o = (e_a * o_a + e_b * o) / tot
                lse = top + jnp.log(tot)
            o_ref[at] = o
            l_ref[at] = lse
            if final:
                g = g_ref[at]
                mix_ref[at] = (o * (g * _sigmoid(g))).astype(BF16)

        _for_units(d, qb, unit)
        if d == 16:
            for dense, dst in zip(outs, block_outs):
                _scatter16(dense, dst, tmp_ref)

    in_specs = [cur(0), prev(8), cur(8), prev(16), cur(16)] + [cur(0)] * n_prior
    args = [proj] * 5 + (list(prior) if n_prior else [])
    out_specs, out_shape = [cur(0), cur(0)], [SDS((S, D), F32), SDS((S, D), F32)]
    if final:
        assert d == 1
        in_specs.append(cur(OFF_G // LANE))
        args.append(proj)
        out_specs, out_shape = [cur(0)] + out_specs, [SDS((S, 2 * D), BF16)] + out_shape
    return pl.pallas_call(
        body, name=f"attn_fwd_d{d}", grid=(steps,),
        in_specs=in_specs, out_specs=out_specs, out_shape=out_shape,
        scratch_shapes=_dense_scratch(d, n_in + n_out),
        compiler_params=_cp(("parallel",)),
    )(*args)


def _attn_bwd(proj, do, lse, delta, d, acc, out_dtype, hosted=None):
    qb = ATTN_QB[d]
    nb, steps, cur, prev, lag = _attn_specs(d, qb)
    has_acc = acc is not None
    n_in = 11 if has_acc else 8
    n_host, n_host_out = (len(hosted.arrays), len(hosted.out_shape)) if hosted else (0, 0)
    assert not (d == 16 and (has_acc or out_dtype != F32))
    rows = CH * d * qb
    carry = (2, 16, CH, LANE) if d == 16 else (2, rows, LANE)

    def body(*refs):
        ins, host_in, refs = refs[:n_in], refs[n_in:n_in + n_host], refs[n_in + n_host:]
        (dq_ref, dk_ref, dv_ref), host_out, scratch = refs[:3], refs[3:3 + n_host_out], refs[3 + n_host_out:]
        if hosted:
            scratch, host_sems = scratch[:-len(hosted.scratch)], scratch[-len(hosted.scratch):]
        ck_ref, cv_ref = scratch[:2]
        dq_f32 = dq_ref if out_dtype == F32 else scratch[2]
        t = pl.program_id(0)
        i = t % nb
        if hosted:
            pl.when(t == 0)(lambda: hosted.start(host_in, host_out, host_sems))
        if d == 16:
            dense, dq_f32, tmp_ref = scratch[2:2 + n_in], scratch[2 + n_in], scratch[-1]

            @pl.when(t < steps)
            def _():
                for src, dst in zip(ins, dense):
                    _gather16(src, dst, tmp_ref)

            ins = dense
        q_ref, kp_ref, kc_ref, vp_ref, vc_ref, do_ref, lse_ref, dl_ref = ins[:8]
        if has_acc:
            aq_ref, ak_ref, av_ref = ins[8:11]
        slot = t & 1
        now_k, now_v, old_k, old_v = ck_ref.at[slot], cv_ref.at[slot], ck_ref.at[1 - slot], cv_ref.at[1 - slot]
        lane_a = _iota((CH, LANE), 1) < 64
        mask_first, mask_rest = _attn_mask(i > 0), _attn_mask(True)

        @pl.when(t == 0)
        def _():
            ck_ref[1] = jnp.zeros(carry[1:], F32)
            cv_ref[1] = jnp.zeros(carry[1:], F32)

        def unit(r, u):
            at = _unit_index(r, u, d)
            q2 = _stack_heads(q_ref[at] * 0.125, lane_a)
            do2 = _stack_heads(do_ref[at], lane_a)
            k2, v2 = _unit_kv(kp_ref, kc_ref, r, u, d), _unit_kv(vp_ref, vc_ref, r, u, d)
            lsev, dlv = lse_ref[at], dl_ref[at]
            lse2 = jnp.concatenate([lsev[:, 0:1], lsev[:, 64:65]], axis=0)
            dl2 = jnp.concatenate([dlv[:, 0:1], dlv[:, 64:65]], axis=0)
            p = jnp.exp(jnp.where(mask_first if u == 0 else mask_rest, _dot_nt(q2, k2), NEG) - lse2)
            ds = (p * (_dot_nt(do2, v2) - dl2)).astype(BF16)
            dq2 = _dot(ds, k2)
            dk2 = _dot_tn(ds, q2)
            dv2 = _dot_tn(p.astype(BF16), do2)
            dq = jnp.where(lane_a, dq2[:CH], dq2[CH:]) * 0.125
            if has_acc:
                dq = dq + aq_ref[at]
            dq_f32[at] = dq
            if u == 0:
                before = _unit_index(r, qb - 1, d)
                old_k[before] += dk2[:CH]
                old_v[before] += dv2[:CH]
            else:
                before = _unit_index(r, u - 1, d)
                now_k[before] += dk2[:CH]
                now_v[before] += dv2[:CH]
            now_k[at] = dk2[CH:]
            now_v[at] = dv2[CH:]

        @pl.when(t < steps)
        def _():
            _for_units(d, qb, unit)
            if d == 16:
                _scatter16(dq_f32, dq_ref, tmp_ref)
            elif out_dtype != F32:
                dq_ref[...] = dq_f32[...].astype(out_dtype)

        if d == 16:
            _scatter16(old_k, dk_ref, tmp_ref)
            _scatter16(old_v, dv_ref, tmp_ref)
        else:
            dk, dv = old_k[...], old_v[...]
            if has_acc:
                dk, dv = dk + ak_ref[...], dv + av_ref[...]
            dk_ref[...] = dk.astype(out_dtype)
            dv_ref[...] = dv.astype(out_dtype)
        if hosted:
            pl.when(t == steps)(lambda: hosted.finish(host_in, host_out, host_sems))

    in_specs = [cur(0), prev(8), cur(8), prev(16), cur(16), cur(0), cur(0), cur(0)]
    args = [proj, proj, proj, proj, proj, do, lse, delta]
    if has_acc:
        in_specs += [cur(0), lag, lag]
        args += list(acc)
    scratch = [pltpu.VMEM(carry, F32), pltpu.VMEM(carry, F32)]
    if d == 16:
        scratch += _dense_scratch(d, n_in + 1)
    elif out_dtype != F32:
        scratch.append(pltpu.VMEM((rows, LANE), F32))
    out_specs, out_shape = [cur(0), lag, lag], [SDS((S, D), out_dtype)] * 3
    if hosted:
        args += hosted.arrays
        in_specs += [ANY] * n_host
        out_specs += [ANY] * n_host_out
        out_shape += hosted.out_shape
        scratch += hosted.scratch
    outs = pl.pallas_call(
        body, name=f"attn_bwd_d{d}", grid=(steps + 1,),
        in_specs=in_specs, out_specs=out_specs, out_shape=out_shape,
        scratch_shapes=scratch, compiler_params=_cp(("arbitrary",)),
    )(*args)
    return (outs[:3], outs[3:]) if hosted else outs


def _conv_taps(cur, prev8, first):
    row8 = _iota(prev8.shape, 0)
    prev8 = jnp.where(first, 0.0, prev8)
    taps = []
    for s in (3, 2, 1):
        rolled = pltpu.roll(cur, s, 0)
        head = jnp.where(row8 < s, pltpu.roll(prev8, s, 0), rolled[:8])
        taps.append(jnp.concatenate([head, rolled[8:]], axis=0))
    return taps + [cur]


def _conv(taps, w, b):
    acc = b + w[0:1, :] * taps[0]
    for k in (1, 2, 3):
        acc = acc + w[k:k + 1, :] * taps[k]
    return acc


def _expand():
    return (_iota((LANE, D), 1) // 64 == _iota((LANE, D), 0)).astype(BF16)


def _reduce():
    return (_iota((D, LANE), 0) // 64 == _iota((D, LANE), 1)).astype(BF16)


def _ssd_common(xs_c, bc_c, dt_raw, dtb, alog):
    head_lane = _iota((CH, LANE), 1) < NH
    xs = xs_c * _sigmoid(xs_c)
    bc = bc_c * _sigmoid(bc_c)
    pre = dt_raw + dtb
    dt = jnp.where(head_lane, jnp.maximum(pre, 0.0) + jnp.log(1.0 + jnp.exp(-jnp.abs(pre))), 0.0)
    a_row = jnp.where(head_lane[0:1], -jnp.exp(alog), 0.0)
    tri = (_iota((CH, CH), 1) <= _iota((CH, CH), 0)).astype(BF16)
    cs = _pick_left(tri, dt * a_row)
    cs_last = cs[CH - 1:CH, :]
    wide = _pick(jnp.concatenate([dt, jnp.exp(cs), jnp.exp(cs_last - cs)], axis=0), _expand())
    dt_b, e_b, f_b = wide[:CH], wide[CH:2 * CH], wide[2 * CH:]
    return dict(xs=xs, bc=bc, pre=pre, dt=dt, a_row=a_row, cs=cs, cs_t=cs.T, dt_b=dt_b, e_b=e_b, f_b=f_b,
                t_b=e_b[CH - 1:CH, :])


def _groups(bc):
    bcb = bc.astype(BF16)
    return [bcb[:, 0:128], bcb[:, 128:256]], [bcb[:, 256:384], bcb[:, 384:512]]


def _decay(q, h, tril):
    seg = q["cs"][:, h:h + 1] - q["cs_t"][h:h + 1, :]
    return jnp.exp(jnp.where(tril, seg, NEG))


def _ssm_fwd(proj, mix, cw, cb, dtb, alog, d_b, nw):
    def body(xs_ref, xsp_ref, bc_ref, bcp_ref, dt_ref, z_ref, cw_ref, cb_ref, dtb_ref, alog_ref, db_ref, nw_ref,
             mix_in_ref, mix_ref, y_ref, st_ref, conv_ref, h_ref):
        del mix_in_ref
        i = pl.program_id(0)

        @pl.when(i == 0)
        def _():
            h_ref[...] = jnp.zeros_like(h_ref)

        cw, cb = cw_ref[...], cb_ref[...]
        xs_c = _conv(_conv_taps(xs_ref[...], xsp_ref[...], i == 0), cw[:, :D], cb[:, :D])
        bc_c = _conv(_conv_taps(bc_ref[...], bcp_ref[...], i == 0), cw[:, D:], cb[:, D:])
        conv_ref[:, :D] = xs_c
        conv_ref[:, D:] = bc_c
        q = _ssd_common(xs_c, bc_c, dt_ref[...], dtb_ref[...], alog_ref[...])
        bg, cg = _groups(q["bc"])
        xs = q["xs"]
        xdt = xs * q["dt_b"]
        xdt_b = xdt.astype(BF16)
        h_in = h_ref[...]
        st_ref[...] = h_in
        hb = h_in.astype(BF16)
        tril = _iota((CH, CH), 1) <= _iota((CH, CH), 0)
        lane_a = _iota((CH, LANE), 1) < 64
        cbm = [_dot_nt(cg[g], bg[g]) for g in range(2)]
        pairs = []
        for hp in range(NH // 2):
            xp = xdt_b[:, hp * LANE:(hp + 1) * LANE]
            ya = _dot((cbm[hp // 4] * _decay(q, 2 * hp, tril)).astype(BF16), xp)
            yb = _dot((cbm[hp // 4] * _decay(q, 2 * hp + 1, tril)).astype(BF16), xp)
            pairs.append(jnp.where(lane_a, ya, yb))
        y_diag = jnp.concatenate(pairs, axis=1)
        y_off = jnp.concatenate([_dot(cg[g], hb[:, g * 512:(g + 1) * 512]) for g in range(2)], axis=1) * q["e_b"]
        y = y_diag + y_off + db_ref[...] * xs
        y_ref[...] = y
        xf = (xdt * q["f_b"]).astype(BF16)
        h_ref[...] = q["t_b"] * h_in + jnp.concatenate(
            [_dot_tn(bg[g], xf[:, g * 512:(g + 1) * 512]) for g in range(2)], axis=1)
        z = z_ref[...]
        yz = y * (z * _sigmoid(z))
        outs = []
        for g in range(2):
            v = yz[:, g * 512:(g + 1) * 512]
            outs.append(v * lax.rsqrt(jnp.mean(v * v, axis=-1, keepdims=True) + EPS))
        mix_ref[...] = (jnp.concatenate(outs, axis=1) * nw_ref[...]).astype(BF16)

    def col(width, blk, prev=False):
        if prev:
            return pl.BlockSpec((8, width), lambda i: (jnp.maximum(i * (CH // 8) - 1, 0), blk))
        return pl.BlockSpec((CH, width), lambda i: (i, blk))

    def full(a):
        return pl.BlockSpec(a.shape, lambda i: (0,) * a.ndim)

    return pl.pallas_call(
        body, name="ssm_fwd", grid=(NC,),
        in_specs=[col(D, 5), col(D, 5, True), col(512, 12), col(512, 12, True), col(LANE, 52), col(D, 4),
                  full(cw), full(cb), full(dtb), full(alog), full(d_b), full(nw), ANY],
        out_specs=[col(D, 1), col(D, 0), pl.BlockSpec((None, CH, D), lambda i: (i, 0, 0)), col(D + 512, 0)],
        out_shape=[SDS((S, 2 * D), BF16), SDS((S, D), F32), SDS((NC, CH, D), F32), SDS((S, D + 512), F32)],
        scratch_shapes=[pltpu.VMEM((CH, D), F32)],
        input_output_aliases={12: 0},
        compiler_params=_cp(("arbitrary",)),
    )(proj, proj, proj, proj, proj, proj, cw, cb, dtb, alog, d_b, nw, mix)


def _ssm_bwd(proj, dn, y_save, states, conv_out, cw, dtb, alog, d_b, nw):
    def body(xs_ref, bc_ref, dt_ref, z_ref, dn_ref, y_ref, st_ref, conv_ref,
             cw_ref, dtb_ref, alog_ref, db_ref, nw_ref,
             dz_ref, dx_ref, dcw_ref, dcb_ref, dsm_ref, dnw_ref, dh_ref, nxs_ref, nbc_ref):
        i = pl.program_id(0)
        ci = NC - 1 - i

        @pl.when(i == 0)
        def _():
            for ref in (dcw_ref, dcb_ref, dsm_ref, dnw_ref, dh_ref, nxs_ref, nbc_ref):
                ref[...] = jnp.zeros_like(ref)

        cw = cw_ref[...]
        xs_c, bc_c = conv_ref[:, :D], conv_ref[:, D:]
        q = _ssd_common(xs_c, bc_c, dt_ref[...], dtb_ref[...], alog_ref[...])
        bg, cg = _groups(q["bc"])
        xs, dt_b, e_b, f_b, t_b = q["xs"], q["dt_b"], q["e_b"], q["f_b"], q["t_b"]
        xdt = xs * dt_b
        xdt_b = xdt.astype(BF16)
        h_in = st_ref[...]
        hb = h_in.astype(BF16)
        dh_new = dh_ref[...]
        dhb = dh_new.astype(BF16)
        red = _reduce()

        z, y, dn, nw_v = z_ref[...], y_ref[...], dn_ref[...], nw_ref[...]
        sig = _sigmoid(z)
        sz = z * sig
        yz = y * sz
        gdn = dn * nw_v
        dyz, dnw = [], []
        for g in range(2):
            v, gv = yz[:, g * 512:(g + 1) * 512], gdn[:, g * 512:(g + 1) * 512]
            r = lax.rsqrt(jnp.mean(v * v, axis=-1, keepdims=True) + EPS)
            dnw.append(dn[:, g * 512:(g + 1) * 512] * v * r)
            dyz.append(r * (gv - v * (r * r) * jnp.mean(gv * v, axis=-1, keepdims=True)))
        dyz = jnp.concatenate(dyz, axis=1)
        dnw_ref[...] += jnp.sum(jnp.concatenate(dnw, axis=1), axis=0, keepdims=True)
        dy = dyz * sz
        dz_ref[...] = (dyz * y * (sig * (1.0 + z * (1.0 - sig)))).astype(BF16)
        dy_b = dy.astype(BF16)

        tril = _iota((CH, CH), 1) <= _iota((CH, CH), 0)
        lane_a = _iota((CH, LANE), 1) < 64
        cbm = [_dot_nt(cg[g], bg[g]) for g in range(2)]
        dcbm = [jnp.zeros((CH, CH), F32), jnp.zeros((CH, CH), F32)]
        seg_rows = jnp.zeros((CH, LANE), F32)
        seg_cols = jnp.zeros((LANE, CH), F32)
        row_id, col_id = _iota((CH, LANE), 0), _iota((CH, LANE), 1)
        dx_pairs = []
        for hp in range(NH // 2):
            g = hp // 4
            xp = xdt_b[:, hp * LANE:(hp + 1) * LANE]
            dyp_f = dy[:, hp * LANE:(hp + 1) * LANE]
            dyp = dy_b[:, hp * LANE:(hp + 1) * LANE]
            halves = []
            for k in range(2):
                h = 2 * hp + k
                lane = lane_a if k == 0 else jnp.logical_not(lane_a)
                dec = _decay(q, h, tril)
                gm = cbm[g] * dec
                dgm = _dot_nt(jnp.where(lane, dyp_f, 0.0).astype(BF16), xp)
                dcbm[g] = dcbm[g] + dgm * dec
                prod = dgm * gm
                seg_rows = jnp.where(col_id == h, jnp.sum(prod, axis=1, keepdims=True), seg_rows)
                seg_cols = jnp.where(row_id == h, jnp.sum(prod, axis=0, keepdims=True), seg_cols)
                halves.append(_dot_tn(gm.astype(BF16), dyp))
            dx_pairs.append(jnp.where(lane_a, halves[0], halves[1]))
        dxdt_diag = jnp.concatenate(dx_pairs, axis=1)

        qv = jnp.concatenate([_dot(bg[g], dhb[:, g * 512:(g + 1) * 512]) for g in range(2)], axis=1)
        y_off = jnp.concatenate([_dot(cg[g], hb[:, g * 512:(g + 1) * 512]) for g in range(2)], axis=1) * e_b
        xfq = xdt * f_b * qv
        dxdt = dxdt_diag + f_b * qv
        tdt = jnp.sum(dh_new * h_in, axis=0, keepdims=True) * t_b
        per_head = _pick(jnp.concatenate([xfq, dy * y_off, dxdt * xs, dy * xs, jnp.broadcast_to(tdt, (8, D))],
                                         axis=0), red)
        fdf, dyoff_h, dxdtxs_h, dyxs_h = [per_head[k * CH:(k + 1) * CH] for k in range(4)]
        dcs = seg_rows - seg_cols.T + dyoff_h - fdf
        last = per_head[4 * CH:4 * CH + 1] + jnp.sum(fdf, axis=0, keepdims=True)
        dcs = dcs + jnp.where(_iota((CH, LANE), 0) == CH - 1, last, 0.0)
        tri_t = (_iota((CH, CH), 1) >= _iota((CH, CH), 0)).astype(BF16)
        da = _pick_left(tri_t, dcs)
        ddt = da * q["a_row"] + dxdtxs_h
        dxs = dxdt * dt_b + db_ref[...] * dy
        ddt_raw = ddt * _sigmoid(q["pre"])
        dsm_ref[0:1, :] += jnp.sum(ddt_raw, axis=0, keepdims=True)
        dsm_ref[1:2, :] += jnp.sum(da * q["dt"], axis=0, keepdims=True) * q["a_row"]
        dsm_ref[2:3, :] += jnp.sum(dyxs_h, axis=0, keepdims=True)
        edy = (e_b * dy).astype(BF16)
        xf = (xdt * f_b).astype(BF16)
        dbs, dcs_g, dhs = [], [], []
        for g in range(2):
            sl = slice(g * 512, (g + 1) * 512)
            dcb_b = dcbm[g].astype(BF16)
            dcs_g.append(_dot(dcb_b, bg[g]) + _dot_nt(edy[:, sl], hb[:, sl]))
            dbs.append(_dot_tn(dcb_b, cg[g]) + _dot_nt(xf[:, sl], dhb[:, sl]))
            dhs.append(_dot_tn(cg[g], edy[:, sl]))
        dh_ref[...] = t_b * dh_new + jnp.concatenate(dhs, axis=1)
        dbc = jnp.concatenate(dbs + dcs_g, axis=1)

        def conv_bwd(dact, pre, x_raw, w, nxt_ref, lo):
            s = _sigmoid(pre)
            dconv = dact * (s * (1.0 + pre * (1.0 - s)))
            nxt8 = nxt_ref[...]
            row8 = _iota(nxt8.shape, 0)
            hi = lo + dconv.shape[1]
            dcb_ref[:, lo:hi] += jnp.sum(dconv, axis=0, keepdims=True)
            later = [dconv]
            for s_ in (1, 2, 3):
                rolled = pltpu.roll(dconv, CH - s_, 0)
                tail = jnp.where(row8 >= 8 - s_, pltpu.roll(nxt8, 8 - s_, 0), rolled[CH - 8:])
                later.append(jnp.concatenate([rolled[:CH - 8], tail], axis=0))
            dx = None
            for s_, up in enumerate(later):
                k = 3 - s_
                dcw_ref[k:k + 1, lo:hi] += jnp.sum(up * x_raw, axis=0, keepdims=True)
                dx = w[k:k + 1, :] * up if dx is None else dx + w[k:k + 1, :] * up
            nxt_ref[...] = dconv[:8]
            return dx

        dx_ref[:, 0:D] = conv_bwd(dxs, xs_c, xs_ref[...], cw[:, :D], nxs_ref, 0).astype(BF16)
        dx_ref[:, D:D + 512] = conv_bwd(dbc, bc_c, bc_ref[...], cw[:, D:], nbc_ref, D).astype(BF16)
        dx_ref[:, D + 512:D + 640] = ddt_raw.astype(BF16)
        dx_ref[:, D + 640:] = jnp.zeros((CH, D - 640), BF16)

    def col(width, blk):
        return pl.BlockSpec((CH, width), lambda i: (NC - 1 - i, blk))

    def full(a):
        return pl.BlockSpec(a.shape, lambda i: (0,) * len(a.shape))

    acc_shapes = [SDS((4, 1536), F32), SDS((1, 1536), F32), SDS((8, LANE), F32), SDS((1, D), F32)]
    return pl.pallas_call(
        body, name="ssm_bwd", grid=(NC,),
        in_specs=[col(D, 5), col(512, 12), col(LANE, 52), col(D, 4),
                  col(D, 0), col(D, 0), pl.BlockSpec((None, CH, D), lambda i: (NC - 1 - i, 0, 0)), col(D + 512, 0),
                  full(cw), full(dtb), full(alog), full(d_b), full(nw)],
        out_specs=[col(D, 0), col(2 * D, 0)] + [full(a) for a in acc_shapes],
        out_shape=[SDS((S, D), BF16), SDS((S, 2 * D), BF16)] + acc_shapes,
        scratch_shapes=[pltpu.VMEM((CH, D), F32), pltpu.VMEM((8, D), F32), pltpu.VMEM((8, 512), F32)],
        compiler_params=_cp(("arbitrary",)),
    )(proj, proj, proj, proj, dn, y_save, states, conv_out, cw, dtb, alog, d_b, nw)


def _outproj_loss(mix, w_out, x, tgt, nw, attn_pre, proj):
    tm = 256

    def body(mix_ref, w_ref, x_ref, t_ref, nw_ref, pre_ref, g_ref,
             dy_ref, dn_ref, do_ref, delta_ref, dg_ref, dw_ref, dnw_ref, loss_ref):
        @pl.when(pl.program_id(0) == 0)
        def _():
            dw_ref[...] = jnp.zeros_like(dw_ref)
            dnw_ref[...] = jnp.zeros_like(dnw_ref)
            loss_ref[...] = jnp.zeros_like(loss_ref)

        mixv, w = mix_ref[...], w_ref[...]
        out = _dot(mixv, w)
        r = lax.rsqrt(jnp.mean(out * out, axis=-1, keepdims=True) + EPS)
        nh = out * r
        nw_v = nw_ref[...]
        err = x_ref[...] + nh * nw_v - t_ref[...]
        loss_ref[...] += 0.5 * jnp.sum(jnp.mean(err * err, axis=-1, keepdims=True), axis=0, keepdims=True)
        dy = err * (1.0 / D)
        dy_ref[...] = dy
        dnw_ref[...] += jnp.sum(dy * nh, axis=0, keepdims=True)
        gdn = dy * nw_v
        dout = (r * (gdn - nh * jnp.mean(gdn * nh, axis=-1, keepdims=True))).astype(BF16)
        dmix = _dot_nt(dout, w)
        dw_ref[...] += _dot_tn(mixv, dout)
        dn_ref[...] = dmix[:, D:]
        dm, g, pre_v = dmix[:, :D], g_ref[...], pre_ref[...]
        sig = _sigmoid(g)
        do = dm * (g * sig)
        do_ref[...] = do
        dg_ref[...] = (dm * pre_v * (sig * (1.0 + g * (1.0 - sig)))).astype(BF16)
        prod = do * pre_v
        same_head = (_iota((LANE, LANE), 0) // 64 == _iota((LANE, LANE), 1) // 64).astype(BF16)
        for cb in range(D // LANE):
            delta_ref[:, cb * LANE:(cb + 1) * LANE] = _pick(prod[:, cb * LANE:(cb + 1) * LANE], same_head)

    row = lambda w: pl.BlockSpec((tm, w), lambda i: (i, 0))
    full = lambda s: pl.BlockSpec(s, lambda i: (0, 0))
    return pl.pallas_call(
        body, name="outproj_loss", grid=(S // tm,),
        in_specs=[row(2 * D), full((2 * D, D)), row(D), row(D), full((1, D)), row(D),
                  pl.BlockSpec((tm, D), lambda i: (i, OFF_G // D))],
        out_specs=[row(D), row(D), row(D), row(D), row(D), full((2 * D, D)), full((1, D)), full((1, LANE))],
        out_shape=[SDS((S, D), F32)] * 4 + [SDS((S, D), BF16), SDS((2 * D, D), F32), SDS((1, D), F32),
                                            SDS((1, LANE), F32)],
        compiler_params=_cp(("arbitrary",)),
    )(mix, w_out, x, tgt, nw, attn_pre, proj)


def _inproj_bwd_dx(srcs, dxbcdt, w_all, x, dy, nw, hosted=None):
    tm = 512
    nk = DP // D
    n_host, n_host_out = (len(hosted.arrays), len(hosted.out_shape)) if hosted else (0, 0)

    def body(*refs):
        src_refs = refs[:nk]
        w_ref, x_ref, dy_ref, nw_ref = refs[nk:nk + 4]
        host_in, refs = refs[nk + 4:nk + 4 + n_host], refs[nk + 4 + n_host:]
        gx_ref, dnw_ref = refs[:2]
        host_out, host_sems = refs[2:2 + n_host_out], refs[2 + n_host_out:]
        i = pl.program_id(0)

        @pl.when(i == 0)
        def _():
            if hosted:
                hosted.start(host_in, host_out, host_sems)
            dnw_ref[...] = jnp.zeros_like(dnw_ref)

        du = None
        for k, ref in enumerate(src_refs):
            width = min(D, 5 * D + X_COLS - k * D)
            part = _dot_nt(ref[:, :width], w_ref[:, k * D:k * D + width])
            du = part if du is None else du + part
        xf, nw_v = x_ref[...], nw_ref[...]
        r = lax.rsqrt(jnp.mean(xf * xf, axis=-1, keepdims=True) + EPS)
        xh = xf * r
        dnw_ref[...] += jnp.sum(du * xh, axis=0, keepdims=True)
        gdu = du * nw_v
        gx_ref[...] = r * (gdu - xh * jnp.mean(gdu * xh, axis=-1, keepdims=True)) + dy_ref[...]

        if hosted:
            pl.when(i == S // tm - 1)(lambda: hosted.finish(host_in, host_out, host_sems))

    row = pl.BlockSpec((tm, D), lambda i: (i, 0))
    row1 = pl.BlockSpec((tm, D), lambda i: (i, 1))
    one = pl.BlockSpec((1, D), lambda i: (0, 0))
    whole_w = pl.BlockSpec((D, DP), lambda i: (0, 0), pipeline_mode=pl.Buffered(1))
    args = [*srcs, dxbcdt, dxbcdt, w_all, x, dy, nw]
    in_specs = [row] * len(srcs) + [row, row1, whole_w, row, row, one]
    out_specs, out_shape, scratch = [row, one], [SDS((S, D), F32), SDS((1, D), F32)], []
    if hosted:
        args += hosted.arrays
        in_specs += [ANY] * n_host
        out_specs += [ANY] * n_host_out
        out_shape += hosted.out_shape
        scratch += hosted.scratch
    outs = pl.pallas_call(
        body, name="inproj_bwd_dx", grid=(S // tm,),
        in_specs=in_specs, out_specs=out_specs, out_shape=out_shape, scratch_shapes=scratch,
        compiler_params=_cp(("arbitrary",)),
    )(*args)
    return (outs[:2], outs[2:]) if hosted else outs


def _dw(u, dsec, name, width=D, hosted=None):
    ts = 1024
    n_host, n_host_out = (len(hosted.arrays), len(hosted.out_shape)) if hosted else (0, 0)

    def body(u_ref, d_ref, *refs):
        host_in, o_ref, refs = refs[:n_host], refs[n_host], refs[n_host + 1:]
        host_out, host_sems = refs[:n_host_out], refs[n_host_out:]
        i = pl.program_id(0)

        @pl.when(i == 0)
        def _():
            if hosted:
                hosted.start(host_in, host_out, host_sems)
            o_ref[...] = jnp.zeros_like(o_ref)

        o_ref[...] += _dot_tn(u_ref[...], d_ref[...])
        if hosted:
            pl.when(i == S // ts - 1)(lambda: hosted.finish(host_in, host_out, host_sems))

    outs = pl.pallas_call(
        body, name=name, grid=(S // ts,),
        in_specs=[pl.BlockSpec((ts, D), lambda i: (i, 0)), pl.BlockSpec((ts, width), lambda i: (i, 0))]
        + [ANY] * n_host,
        out_specs=[pl.BlockSpec((D, width), lambda i: (0, 0))] + [ANY] * n_host_out,
        out_shape=[SDS((D, width), F32)] + (hosted.out_shape if hosted else []),
        scratch_shapes=hosted.scratch if hosted else [],
        compiler_params=_cp(("arbitrary",)),
    )(u, dsec, *(hosted.arrays if hosted else []))
    return (outs[0], outs[1:]) if hosted else outs[0]


def _place():
    x, y, c = lax.axis_index("x"), lax.axis_index("y"), lax.axis_index("c")
    return x, y, c, 2 * x + y


def _chip_of(x, y, k):
    px = 1 - x if k & 2 else x
    py = 1 - y if k & 1 else y
    return px, py, 2 * px + py


def _remote(src, dst, send_sem, recv_sem, dev):
    return pltpu.make_async_remote_copy(src_ref=src, dst_ref=dst, send_sem=send_sem, recv_sem=recv_sem,
                                        device_id=dev, device_id_type=MESH)


def _gather_weights(w_in_b):
    half = w_in_b.shape[0] // 2
    quarter = half // 2

    def body(src, dst, send, recv):
        x, y, c, j = _place()
        me, sib = (x, y, c), (x, y, 1 - c)
        nbr = {"x": _chip_of(x, y, 2), "y": _chip_of(x, y, 1)}
        diag = _chip_of(x, y, 3)[2]
        started, arrivals = [], []

        def rows(n_quarter=None, sibling=False):
            base = (1 - c if sibling else c) * half
            return pl.ds(base, half) if n_quarter is None else pl.ds(base + n_quarter * quarter, quarter)

        def sem(n):
            return send.at[n], recv.at[n]

        def go(cp):
            cp.start()
            started.append(cp)

        own = _remote(src, dst.at[j], *sem(8), sib)
        go(own)
        for n, axis in enumerate("xy"):
            px, py, _ = nbr[axis]
            go(_remote(src.at[rows()], dst.at[j, rows()], *sem(n), (px, py, c)))
        for n, axis in enumerate("xy"):
            ox, oy, _ = nbr["y" if axis == "x" else "x"]
            pj = nbr[axis][2]
            _remote(src.at[rows()], dst.at[pj, rows()], *sem(n), me).wait_recv()
            go(_remote(dst.at[pj, rows(n)], dst.at[pj, rows(n)], *sem(2 + n), (ox, oy, c)))
            go(_remote(dst.at[pj, rows()], dst.at[pj, rows()], *sem(4 + n), sib))
            arrivals.append(_remote(src.at[rows()], dst.at[pj, rows(None, True)], *sem(4 + n), me))
        for n in range(2):
            _remote(dst.at[diag, rows(n)], dst.at[diag, rows(n)], *sem(2 + n), me).wait_recv()
            go(_remote(dst.at[diag, rows(n)], dst.at[diag, rows(n)], *sem(6 + n), sib))
            arrivals.append(_remote(dst.at[diag, rows(n, True)], dst.at[diag, rows(n, True)], *sem(6 + n), me))
        for cp in arrivals + [own]:
            cp.wait_recv()
        for cp in started:
            cp.wait_send()

    return pl.pallas_call(
        body, name="gather_weights", in_specs=[ANY], out_specs=ANY,
        out_shape=SDS((4,) + w_in_b.shape, BF16),
        scratch_shapes=[pltpu.SemaphoreType.DMA((9,)), pltpu.SemaphoreType.DMA((9,))],
        compiler_params=pltpu.CompilerParams(has_side_effects=True),
    )(w_in_b)


class _LateGather:
    def __init__(self, w_out_b, conv_w):
        self.arrays = [w_out_b, conv_w]
        self.out_shape = [SDS((4,) + w_out_b.shape, BF16), SDS((4,) + conv_w.shape, F32)]
        self.scratch = [pltpu.SemaphoreType.DMA((11,)), pltpu.SemaphoreType.DMA((11,))]

    def _plan(self, ins, outs, sems):
        x, y, c, j = _place()
        send, recv = sems
        (wo, cw), (gwo, gcw) = ins, outs
        half = wo.shape[0] // 2
        mine, theirs = pl.ds(c * half, half), pl.ds((1 - c) * half, half)
        me, sib = (x, y, c), (x, y, 1 - c)
        first, arrive, forward, last = [], [], [], []
        for k in (1, 2, 3):
            px, py, pj = _chip_of(x, y, k)
            first += [_remote(wo.at[mine], gwo.at[j, mine], send.at[k - 1], recv.at[k - 1], (px, py, c)),
                      _remote(cw, gcw.at[j], send.at[k + 2], recv.at[k + 2], (px, py, c))]
            arrive.append(_remote(wo.at[mine], gwo.at[pj, mine], send.at[k - 1], recv.at[k - 1], me))
            forward.append(_remote(gwo.at[pj, mine], gwo.at[pj, mine], send.at[k + 5], recv.at[k + 5], sib))
            last += [_remote(cw, gcw.at[pj], send.at[k + 2], recv.at[k + 2], me),
                     _remote(wo.at[theirs], gwo.at[pj, theirs], send.at[k + 5], recv.at[k + 5], me)]
        first += [_remote(wo, gwo.at[j], send.at[9], recv.at[9], sib),
                  _remote(cw, gcw.at[j], send.at[10], recv.at[10], sib)]
        last += first[-2:]
        return first, arrive, forward, last

    def start(self, ins, outs, sems):
        for cp in self._plan(ins, outs, sems)[0]:
            cp.start()

    def pass_on(self, ins, outs, sems):
        _, arrive, forward, _ = self._plan(ins, outs, sems)
        for got, fwd in zip(arrive, forward):
            got.wait_recv()
            fwd.start()

    def finish(self, ins, outs, sems):
        first, _, forward, last = self._plan(ins, outs, sems)
        for cp in last:
            cp.wait_recv()
        for cp in first + forward:
            cp.wait_send()


def _window(s, names):
    lo, hi = TILES * s, TILES * s + TILES + 1
    pieces = []
    for n, name in enumerate(names):
        a, count = SECTION_TILES[name]
        first, last = max(lo, a), min(hi, a + count)
        if first < last:
            pieces.append((n, first - a, last - first, first - lo))
    assert sum(p[2] for p in pieces) == TILES + 1
    return pieces


class _PairExchange:
    def __init__(self, names, sections, shards, more=()):
        self.names, self.shards = names, shards
        self.there = [n for n, a in enumerate(sections) if a is not None]
        self.arrays = [sections[n] for n in self.there] + list(more)
        self.out_shape = [SDS((len(shards), D // 2, WIN), F32)]
        self.out_shape += [SDS((a.shape[0], a.shape[1] // 2, a.shape[2]), F32) for a in more]
        n = sum(p[0] in self.there for s in shards for p in _window(s, names)) + len(more)
        self.scratch = [pltpu.SemaphoreType.DMA((n,)) for _ in range(2)]

    def _copies(self, ins, outs, sems):
        x, y, c, _ = _place()
        sib = (x, y, 1 - c)
        rows = pl.ds((1 - c) * (D // 2), D // 2)
        k = 0
        for i, s in enumerate(self.shards):
            for n, tile, tiles, at in _window(s, self.names):
                if n in self.there:
                    yield _remote(ins[self.there.index(n)].at[rows, pl.ds(tile * LANE, tiles * LANE)],
                                  outs[0].at[i, :, pl.ds(at * LANE, tiles * LANE)], sems[0].at[k], sems[1].at[k], sib)
                    k += 1
        for src, dst in zip(ins[len(self.there):], outs[1:]):
            half = src.shape[1] // 2
            yield _remote(src.at[:, pl.ds((1 - c) * half, half)], dst, sems[0].at[k], sems[1].at[k], sib)
            k += 1

    def start(self, ins, outs, sems):
        for cp in self._copies(ins, outs, sems):
            cp.start()

    def finish(self, ins, outs, sems):
        for cp in self._copies(ins, outs, sems):
            cp.wait()


def _exchange_call(exchange, name, into=None):
    n, n_out = len(exchange.arrays), len(exchange.out_shape)
    given = list(into) if into else []

    def body(*refs):
        ins, outs, sems = refs[:n], refs[n + len(given):n + len(given) + n_out], refs[n + len(given) + n_out:]
        exchange.start(ins, outs, sems)
        exchange.finish(ins, outs, sems)

    return pl.pallas_call(
        body, name=name, in_specs=[ANY] * (n + len(given)), out_specs=[ANY] * n_out, out_shape=exchange.out_shape,
        input_output_aliases={n + k: k for k in range(len(given))},
        scratch_shapes=exchange.scratch, compiler_params=pltpu.CompilerParams(has_side_effects=True),
    )(*exchange.arrays, *given)


def _pair_sum_windows(cidx, names, sections, shards, r, name):
    n, half, _ = r.shape
    tr = min(half, 256)
    nt = half // tr

    def body(c_ref, *refs):
        del c_ref
        secs, r_ref, o_ref = refs[:-2], refs[-2], refs[-1]
        for i, s in enumerate(shards):
            for k, tile, tiles, at in _window(s, names):
                own = secs[k][:, tile * LANE:(tile + tiles) * LANE]
                there = slice(at * LANE, (at + tiles) * LANE)
                o_ref[i, :, there] = (own + r_ref[i, :, there]).astype(BF16)

    window = pl.BlockSpec((n, tr, WIN), lambda t, c: (0, t, 0))
    return pl.pallas_call(
        body, name=name,
        grid_spec=pltpu.PrefetchScalarGridSpec(
            num_scalar_prefetch=1, grid=(nt,),
            in_specs=[pl.BlockSpec((tr, a.shape[1]), lambda t, c: (c[0] * nt + t, 0)) for a in sections] + [window],
            out_specs=window),
        out_shape=SDS(r.shape, BF16),
        compiler_params=_cp(("parallel",)),
    )(cidx, *sections, r)


def _pair_sum(cidx, g, r, name):
    n, half, width = r.shape
    tr = min(half, 256)
    nt = half // tr

    def body(c_ref, g_ref, r_ref, o_ref):
        del c_ref
        o_ref[...] = (g_ref[...] + r_ref[...]).astype(BF16)

    return pl.pallas_call(
        body, name=name,
        grid_spec=pltpu.PrefetchScalarGridSpec(
            num_scalar_prefetch=1, grid=(n, nt),
            in_specs=[pl.BlockSpec((None, tr, width), lambda s, t, c: (s, c[0] * nt + t, 0)),
                      pl.BlockSpec((None, tr, width), lambda s, t, c: (s, t, 0))],
            out_specs=pl.BlockSpec((None, tr, width), lambda s, t, c: (s, t, 0))),
        out_shape=SDS(r.shape, BF16),
        compiler_params=_cp(("parallel", "parallel")),
    )(cidx, g, r)


class _ChipExchange:
    def __init__(self, arrays, rows):
        self.arrays, self.rows = list(arrays), list(rows)
        self.out_shape = [SDS((4,) + a.shape[1:], BF16) for a in self.arrays]
        self.scratch = [pltpu.SemaphoreType.DMA((3 * len(self.arrays),)) for _ in range(2)]

    def _copies(self, ins, outs, sems):
        x, y, c, j = _place()
        send, recv = sems
        for a, (src, dst, row) in enumerate(zip(ins, outs, self.rows)):
            for k in (1, 2, 3):
                px, py, pj = _chip_of(x, y, k)
                n = 3 * a + k - 1
                slot = pj if row is None else py
                yield (None if row is None else px == row, None if row is None else x == row,
                       _remote(src.at[slot], dst.at[j], send.at[n], recv.at[n], (px, py, c)),
                       _remote(src.at[0], dst.at[pj], send.at[n], recv.at[n], (x, y, c)))

    def start(self, ins, outs, sems):
        for sends, _, send, _ in self._copies(ins, outs, sems):
            if sends is None:
                send.start()
            else:
                pl.when(sends)(send.start)

    def finish(self, ins, outs, sems):
        for sends, owns, send, arrival in self._copies(ins, outs, sems):
            if sends is None:
                arrival.wait_recv()
                send.wait_send()
            else:
                pl.when(owns)(arrival.wait_recv)
                pl.when(sends)(send.wait_send)


def _all_gather_rows(src, dst, rows, send, recv, local_sem):
    x, y, c, j = _place()
    me = 2 * j + c
    local = pltpu.make_async_copy(src, dst.at[me, rows], local_sem)
    cps, arrivals = [], []
    for k in range(1, 8):
        px, py, pj = _chip_of(x, y, k >> 1)
        pc = 1 - c if k & 1 else c
        cps.append(_remote(src, dst.at[me, rows], send.at[k - 1], recv.at[k - 1], (px, py, pc)))
        arrivals.append(_remote(src, dst.at[2 * pj + pc, rows], send.at[k - 1], recv.at[k - 1], (x, y, c)))
    starts = [local.start] + [cp.start for cp in cps]
    waits = [cp.wait_recv for cp in arrivals] + [cp.wait_send for cp in cps] + [local.wait]
    return starts, waits


class _SmallExchange:
    def __init__(self, small):
        self.arrays = [small]
        self.out_shape = [SDS((8,) + small.shape, F32)]
        self.scratch = [pltpu.SemaphoreType.DMA((7,)), pltpu.SemaphoreType.DMA((7,)), pltpu.SemaphoreType.DMA]

    def start(self, ins, outs, sems):
        for go in _all_gather_rows(ins[0], outs[0], slice(None), *sems)[0]:
            go()

    def finish(self, ins, outs, sems):
        for wait in _all_gather_rows(ins[0], outs[0], slice(None), *sems)[1]:
            wait()


class _Both:
    def __init__(self, a, b):
        self.parts = (a, b)
        self.arrays, self.out_shape, self.scratch = a.arrays + b.arrays, a.out_shape + b.out_shape, a.scratch + b.scratch

    def _split(self, ins, outs, sems):
        a, b = self.parts
        return ((a, ins[:len(a.arrays)], outs[:len(a.out_shape)], sems[:len(a.scratch)]),
                (b, ins[len(a.arrays):], outs[len(a.out_shape):], sems[len(a.scratch):]))

    def start(self, ins, outs, sems):
        for part, *refs in self._split(ins, outs, sems):
            part.start(*refs)

    def finish(self, ins, outs, sems):
        for part, *refs in self._split(ins, outs, sems):
            part.finish(*refs)


def _slot_sum(r, name):
    n, rows, width = r.shape
    tr = min(rows, 256)

    def body(r_ref, o_ref):
        acc = r_ref[0].astype(F32)
        for s in range(1, n):
            acc = acc + r_ref[s].astype(F32)
        o_ref[...] = acc

    return pl.pallas_call(
        body, name=name, grid=(rows // tr,),
        in_specs=[pl.BlockSpec((n, tr, width), lambda t: (0, t, 0))],
        out_specs=pl.BlockSpec((tr, width), lambda t: (t, 0)),
        out_shape=SDS((rows, width), F32),
        compiler_params=_cp(("parallel",)),
    )(r)


def _chip_sum(where, recv, own, name):
    n, rows, width = recv.shape
    tr = min(rows, 256)
    nt = rows // tr

    def body(j_ref, r_ref, own_ref, o_ref):
        acc = None
        for s in range(n):
            term = jnp.where(j_ref[0] == s, own_ref[...], r_ref[s]).astype(F32)
            acc = term if acc is None else acc + term
        o_ref[...] = acc

    return pl.pallas_call(
        body, name=name,
        grid_spec=pltpu.PrefetchScalarGridSpec(
            num_scalar_prefetch=1, grid=(nt,),
            in_specs=[pl.BlockSpec((n, tr, width), lambda t, j: (0, t, 0)),
                      pl.BlockSpec((None, tr, width), lambda t, j: (j[0], t, 0))],
            out_specs=pl.BlockSpec((tr, width), lambda t, j: (j[1] * nt + t, 0))),
        out_shape=SDS((2 * rows, width), F32),
        compiler_params=_cp(("parallel",)),
    )(where, recv, own)


def _chip_sum_rows(place, recv0, own0, recv1, own1, name):
    n, rows, width = recv0.shape
    tr = min(rows, 256)
    nt = rows // tr

    def body(p_ref, r0_ref, o0_ref, r1_ref, o1_ref, o_ref):
        first_row = p_ref[2] == 0
        own = jnp.where(first_row, o0_ref[...], o1_ref[...])
        acc = None
        for s in range(n):
            term = jnp.where(p_ref[0] == s, own, jnp.where(first_row, r0_ref[s], r1_ref[s])).astype(F32)
            acc = term if acc is None else acc + term
        o_ref[...] = acc

    recv = pl.BlockSpec((n, tr, width), lambda t, p: (0, t, 0))
    own = pl.BlockSpec((None, tr, width), lambda t, p: (p[3], t, 0))
    return pl.pallas_call(
        body, name=name,
        grid_spec=pltpu.PrefetchScalarGridSpec(
            num_scalar_prefetch=1, grid=(nt,), in_specs=[recv, own, recv, own],
            out_specs=pl.BlockSpec((tr, width), lambda t, p: (p[1] * nt + t, 0))),
        out_shape=SDS((2 * rows, width), F32),
        compiler_params=_cp(("parallel",)),
    )(place, recv0, own0, recv1, own1)


def _half_exchange(gw, go, gathered, late, row):
    def body(gw_in, go_in, ga_in, late_ref, gw_ref, go_ref, ga_ref, send, recv, late_send, late_recv, late_local):
        del gw_in, go_in, ga_in
        x, y, c, _ = _place()
        starts, waits = _all_gather_rows(late_ref, ga_ref, pl.ds(row, late.shape[0]), late_send, late_recv,
                                         late_local)
        for go_ in starts:
            go_()
        mine = [pl.ds(c * (r.shape[0] // 2), r.shape[0] // 2) for r in (gw_ref, go_ref)]
        cps = [_remote(r.at[rows], r.at[rows], send.at[k], recv.at[k], (x, y, 1 - c))
               for k, (r, rows) in enumerate(zip((gw_ref, go_ref), mine))]
        for cp in cps:
            cp.start()
        for k, r in enumerate((gw_ref, go_ref)):
            theirs = pl.ds((1 - c) * (r.shape[0] // 2), r.shape[0] // 2)
            _remote(r.at[theirs], r.at[theirs], send.at[k], recv.at[k], (x, y, c)).wait_recv()
        for cp in cps:
            cp.wait_send()
        for wait in waits:
            wait()

    return pl.pallas_call(
        body, name="half_exchange", in_specs=[ANY] * 4, out_specs=[ANY] * 3,
        out_shape=[SDS(gw.shape, F32), SDS(go.shape, F32), SDS(gathered.shape, F32)],
        input_output_aliases={0: 0, 1: 1, 2: 2},
        scratch_shapes=[pltpu.SemaphoreType.DMA((2,)), pltpu.SemaphoreType.DMA((2,)),
                        pltpu.SemaphoreType.DMA((7,)), pltpu.SemaphoreType.DMA((7,)), pltpu.SemaphoreType.DMA],
        compiler_params=pltpu.CompilerParams(has_side_effects=True),
    )(gw, go, gathered, late)


def _adamw(w, g, m, v, name):
    rows, width = w.shape
    tr = min(rows, 256)

    def body(w_ref, g_ref, m_ref, v_ref, d_ref, nm_ref, nv_ref):
        gv = g_ref[...]
        nm = ADAM_B1 * m_ref[...] + (1.0 - ADAM_B1) * gv
        nv = ADAM_B2 * v_ref[...] + (1.0 - ADAM_B2) * (gv * gv)
        m_hat = nm / (1.0 - ADAM_B1 ** ADAM_STEP)
        v_hat = nv / (1.0 - ADAM_B2 ** ADAM_STEP)
        d_ref[...] = -ADAM_LR * (m_hat / (jnp.sqrt(v_hat) + ADAM_EPS) + ADAM_WD * w_ref[...])
        nm_ref[...] = nm
        nv_ref[...] = nv

    t = pl.BlockSpec((tr, width), lambda i: (i, 0))
    return pl.pallas_call(
        body, name=name, grid=(rows // tr,), in_specs=[t] * 4, out_specs=[t] * 3,
        out_shape=[SDS(w.shape, F32)] * 3, compiler_params=_cp(("parallel",)),
    )(w, g, m, v)


def _rowwise(a):
    return jnp.transpose(a, (2, 0, 1)).reshape(SHARD * D // LANE, LANE)


def _columns(ref, base=0):
    return jnp.concatenate([ref[pl.ds(base + c, LANE, stride=8), :].T for c in range(D // LANE)], axis=0)


def _shard_bf16(chip, w_rows):
    per_step = 2

    def body(j_ref, w_ref, o_ref, prev_ref):
        t = pl.program_id(0)

        @pl.when(t == 0)
        def _():
            prev_ref[...] = jnp.zeros_like(prev_ref)

        lane = _iota((D, LANE), 1)
        prev = prev_ref[...]
        for a in range(per_step):
            tile = t * per_step + a
            cur = _columns(w_ref, a * D)
            for s in range(4):
                @pl.when(j_ref[0] == s)
                def _():
                    off = SHIFT * s
                    moved = cur if s == 0 else jnp.where(lane < off, pltpu.roll(prev, off, 1), pltpu.roll(cur, off, 1))
                    col = tile * LANE + lane - off
                    o_ref[:, a * LANE:(a + 1) * LANE] = jnp.where((col >= 0) & (col < SHARD), moved, 0.0).astype(BF16)
            prev = cur
        prev_ref[...] = prev

    return pl.pallas_call(
        body, name="shard_bf16",
        grid_spec=pltpu.PrefetchScalarGridSpec(
            num_scalar_prefetch=1, grid=(pl.cdiv(TILES + 1, per_step),),
            in_specs=[pl.BlockSpec((per_step * D, LANE), lambda t, j: (t, 0))],
            out_specs=pl.BlockSpec((D, per_step * LANE), lambda t, j: (0, t)),
            scratch_shapes=[pltpu.VMEM((D, LANE), F32)]),
        out_shape=SDS((D, WIN), BF16), compiler_params=_cp(("arbitrary",)),
    )(chip, w_rows)


def _whole_w_in(windows):
    tr = 256
    n = windows.shape[0]

    def body(g_ref, o_ref):
        lane = _iota((tr, LANE), 1)
        for s in range(n):
            first = TILES * s
            head = g_ref[s, :, :LANE]
            if s:
                tail = g_ref[s - 1, :, TILES * LANE:]
                head = jnp.where(lane < SHIFT * s, tail.astype(F32), head.astype(F32)).astype(BF16)
            o_ref[:, first * LANE:(first + 1) * LANE] = head
            o_ref[:, (first + 1) * LANE:(first + TILES) * LANE] = g_ref[s, :, LANE:TILES * LANE]
        o_ref[:, n * TILES * LANE:(n * TILES + 1) * LANE] = g_ref[n - 1, :, TILES * LANE:]
        o_ref[:, (n * TILES + 1) * LANE:] = jnp.zeros((tr, DP - (n * TILES + 1) * LANE), BF16)

    return pl.pallas_call(
        body, name="whole_w_in", grid=(D // tr,),
        in_specs=[pl.BlockSpec((n, tr, WIN), lambda t: (0, t, 0))], out_specs=pl.BlockSpec((tr, DP), lambda t: (t, 0)),
        out_shape=SDS((D, DP), BF16), compiler_params=_cp(("parallel",)),
    )(windows)


def _own_buffer(a, name):
    tr = 512
    block = pl.BlockSpec((tr, a.shape[1]), lambda t: (t, 0))

    def body(a_ref, o_ref):
        o_ref[...] = a_ref[...]

    return pl.pallas_call(
        body, name=name, grid=(a.shape[0] // tr,), in_specs=[block], out_specs=block,
        out_shape=SDS(a.shape, a.dtype), compiler_params=_cp(("parallel",)),
    )(a)


def _shard_of_window(chip, g_win):
    tr = 128

    def body(j_ref, g_ref, grad_ref):
        for s in range(4):
            @pl.when(j_ref[0] == s)
            def _():
                back = LANE - SHIFT * s
                from_this = _iota((tr, LANE), 1) < back

                def moved(t):
                    tile = g_ref[:, t * LANE:(t + 1) * LANE]
                    return pltpu.roll(tile, back, 1) if s else tile

                for t in range(TILES):
                    grad_ref[:, t * LANE:(t + 1) * LANE] = jnp.where(from_this, moved(t), moved(t + 1)) if s else moved(t)
                grad_ref[:, TILES * LANE:] = moved(TILES)[:, :SHARD - TILES * LANE]

    return pl.pallas_call(
        body, name="shard_of_window",
        grid_spec=pltpu.PrefetchScalarGridSpec(
            num_scalar_prefetch=1, grid=(D // tr,), in_specs=[pl.BlockSpec((tr, WIN), lambda t, j: (t, 0))],
            out_specs=pl.BlockSpec((tr, SHARD), lambda t, j: (t, 0))),
        out_shape=SDS((D, SHARD), F32), compiler_params=_cp(("parallel",)),
    )(chip, g_win)


def _adamw_in(w_rows, g, m_rows, v_rows):
    per_step = 2

    def body(w_ref, g_ref, m_ref, v_ref, d_ref, nm_ref, nv_ref):
        for a in range(per_step):
            cols = slice(a * LANE, (a + 1) * LANE)
            gv = g_ref[:, cols]
            nm = ADAM_B1 * _columns(m_ref, a * D) + (1.0 - ADAM_B1) * gv
            nv = ADAM_B2 * _columns(v_ref, a * D) + (1.0 - ADAM_B2) * (gv * gv)
            m_hat = nm / (1.0 - ADAM_B1 ** ADAM_STEP)
            v_hat = nv / (1.0 - ADAM_B2 ** ADAM_STEP)
            d_ref[:, cols] = -ADAM_LR * (m_hat / (jnp.sqrt(v_hat) + ADAM_EPS) + ADAM_WD * _columns(w_ref, a * D))
            nm_ref[:, cols] = nm
            nv_ref[:, cols] = nv

    tile = pl.BlockSpec((D, per_step * LANE), lambda t: (0, t))
    rows = pl.BlockSpec((per_step * D, LANE), lambda t: (t, 0))
    return pl.pallas_call(
        body, name="adamw_in", grid=(pl.cdiv(TILES + 1, per_step),), in_specs=[rows, tile, rows, rows],
        out_specs=[tile] * 3, out_shape=[SDS(g.shape, F32)] * 3, compiler_params=_cp(("parallel",)),
    )(w_rows, g, m_rows, v_rows)


def _rows128(a, rows):
    flat = a.reshape(-1)
    return jnp.pad(flat, (0, rows * LANE - flat.shape[0])).reshape(rows, LANE)


CONV_ROWS = 48


def _pack_small(conv_w, norm_pre, conv_b, ssm_norm, norm_post, dtb, alog, dsk, extra=None):
    cw_rows = CONV_ROWS if conv_w.shape[-1] == 1536 else 16
    extra = jnp.zeros((1, LANE), F32) if extra is None else _rows128(extra, 1)
    vec = jnp.concatenate([_rows128(dtb, 1), _rows128(alog, 1), _rows128(dsk, 1), extra, jnp.zeros((4, LANE), F32)],
                          axis=0)
    return jnp.concatenate([_rows128(conv_w, cw_rows), _rows128(norm_pre, 8), _rows128(conv_b, 16),
                            _rows128(ssm_norm, 8), _rows128(norm_post, 8), vec], axis=0)


def _unpack_small(p, cw_cols):
    cw_rows = CONV_ROWS if cw_cols == 1536 else 16
    o = cw_rows
    conv_w = p[:cw_rows].reshape(-1)[:4 * cw_cols].reshape(1, 4, cw_cols)
    norm_pre = p[o:o + 8].reshape(1, D)
    conv_b = p[o + 8:o + 24].reshape(-1)[:1536].reshape(1, 1536)
    ssm_norm = p[o + 24:o + 32].reshape(1, D)
    norm_post = p[o + 32:o + 40].reshape(1, D)
    vec = p[o + 40:o + 48]
    return conv_w, norm_pre, conv_b, ssm_norm, norm_post, vec[0:1, :NH], vec[1:2, :NH], vec[2:3, :NH], vec[3, 0]


def _pad_lanes(a):
    return jnp.pad(a, ((0, 0), (0, LANE - a.shape[1])))


class _GradReduce:
    LO, HI = ("q", "k", "v", "g"), ("g", "z", "x")

    def __init__(self, xi, yi, ci):
        self.cidx = jnp.reshape(ci, (1,)).astype(jnp.int32)
        self.place = jnp.stack([2 * xi + yi, ci, xi, yi]).astype(jnp.int32)

    def pairs(self, dw_g, dw_z, dw_x, dw_out):
        self.hi = [dw_g, dw_z, dw_x]
        self.go = dw_out.reshape(4, D // 2, D)
        return _PairExchange(self.HI, self.hi, (2, 3), [self.go])

    def first(self, got):
        rw, ro = got
        self.pw_hi = _pair_sum_windows(self.cidx, self.HI, self.hi, (2, 3), rw, "pair_sum_hi")
        self.po = _pair_sum(self.cidx, self.go, ro, "pair_sum_out")
        return _ChipExchange([self.pw_hi, self.po], [1, None])

    def first_done(self, got):
        self.rw_hi, self.ro = got

    def second_pairs(self, dw_q, dw_k, dw_g):
        self.lo = [dw_q, dw_k, None, dw_g]
        return _PairExchange(self.LO, self.lo, (0, 1))

    def second(self, dw_v, got, small):
        rest = _PairExchange(self.LO, [None, None, dw_v, None], (0, 1))
        (rw,) = _exchange_call(rest, "pair_exchange_v", into=got)
        lo = [dw_v if a is None else a for a in self.lo]
        self.pw_lo = _pair_sum_windows(self.cidx, self.LO, lo, (0, 1), rw, "pair_sum_lo")
        return _Both(_ChipExchange([self.pw_lo], [0]), _SmallExchange(small))

    def second_done(self, got):
        self.rw_lo, self.small = got

    def result(self, late, row):
        half_in = _chip_sum_rows(self.place, self.rw_lo, self.pw_lo, self.rw_hi, self.pw_hi, "chip_sum_in")
        half_out = _chip_sum(self.place[0:2], self.ro, self.po, "chip_sum_out")
        return _half_exchange(half_in, half_out, self.small, late, row)


def kernel(x, norm_pre_w, w_in, conv_w, conv_b, dt_bias, a_log, d_skip, ssm_norm_w, w_out, norm_post_w, loss_target, m_norm_pre_w, m_w_in, m_conv_w, m_conv_b, m_dt_bias, m_a_log, m_d_skip, m_ssm_norm_w, m_w_out, m_norm_post_w, v_norm_pre_w, v_w_in, v_conv_w, v_conv_b, v_dt_bias, v_a_log, v_d_skip, v_ssm_norm_w, v_w_out, v_norm_post_w):
    xi, yi, ci = lax.axis_index("x"), lax.axis_index("y"), lax.axis_index("c")
    chip = 2 * xi + yi
    x2, tgt = x[0], loss_target[0]

    chip_idx = jnp.reshape(chip, (1,)).astype(jnp.int32)
    w_rows = _rowwise(w_in)
    w_all = _whole_w_in(_gather_weights(_shard_bf16(chip_idx, w_rows)))
    reduce = _GradReduce(xi, yi, ci)
    grad_x, dnw_pre = _local_step(x2, tgt, w_all, _LateGather(w_out[0].astype(BF16), conv_w[0]), norm_pre_w, conv_b,
                                  dt_bias, a_log, d_skip, ssm_norm_w, norm_post_w, reduce)
    g_win, g_out, small = reduce.result(_rows128(dnw_pre, D // LANE), CONV_ROWS)
    g_small = _slot_sum(small, "small_sum")
    g_cw, g_npre, g_cb, g_nssm, g_npost, g_dtb, g_alog, g_dsk, loss = _unpack_small(g_small, 1536)
    g_cw = lax.dynamic_slice_in_dim(g_cw, chip * 384, 384, axis=2)

    g_in = _shard_of_window(chip_idx, g_win)
    d_in, nm_in, nv_in = _adamw_in(w_rows, g_in, _rowwise(m_w_in), _rowwise(v_w_in))
    grad_x = _own_buffer(grad_x, "grad_x_copy")
    d_out, nm_out, nv_out = _adamw(w_out[0], g_out, m_w_out[0], v_w_out[0], "adamw_out")
    packed = [_pack_small(*t) for t in (
        (conv_w, norm_pre_w, conv_b, ssm_norm_w, norm_post_w, dt_bias, a_log, d_skip),
        (g_cw, g_npre, g_cb, g_nssm, g_npost, g_dtb, g_alog, g_dsk),
        (m_conv_w, m_norm_pre_w, m_conv_b, m_ssm_norm_w, m_norm_post_w, m_dt_bias, m_a_log, m_d_skip),
        (v_conv_w, v_norm_pre_w, v_conv_b, v_ssm_norm_w, v_norm_post_w, v_dt_bias, v_a_log, v_d_skip))]
    small_out = [_unpack_small(p, 384)[:8] for p in _adamw(*packed, "adamw_small")]

    def ordered(cw_, npre, cb_, nssm, npost, dtb_, alog_, dsk_, big_in, big_out):
        return [npre, big_in[None], cw_, cb_, dtb_, alog_, dsk_, nssm, big_out[None], npost]

    grads = ordered(g_cw, g_npre, g_cb, g_nssm, g_npost, g_dtb, g_alog, g_dsk, g_in, g_out)
    deltas = ordered(*small_out[0], d_in, d_out)
    new_m = ordered(*small_out[1], nm_in, nm_out)
    new_v = ordered(*small_out[2], nv_in, nv_out)
    return (loss, grad_x[None], *grads, *deltas, *new_m, *new_v)


def _local_step(x2, tgt, w_all, late, norm_pre_w, conv_b, dt_bias, a_log, d_skip, ssm_norm_w,
                norm_post_w, reduce=None):
    dtb, alog = _pad_lanes(dt_bias), _pad_lanes(a_log)
    d_b = jnp.repeat(d_skip, 64, axis=1)

    if isinstance(late, _LateGather):
        (proj, u), (gout, gcw) = _inproj_fwd(x2, norm_pre_w, w_all, late)
        w_out_all = gout.reshape(2 * D, D)
        cw_all = jnp.concatenate([gcw[0], gcw[1], gcw[2], gcw[3]], axis=1)
    else:
        proj, u = _inproj_fwd(x2, norm_pre_w, w_all)
        w_out_all, cw_all = late
    mix, attn_pre, lse = _attn_fwd(proj, 1, _attn_fwd(proj, 4, _attn_fwd(proj, 16)), final=True)
    mix, y_save, states, conv_out = _ssm_fwd(proj, mix, cw_all, conv_b, dtb, alog, d_b, ssm_norm_w)

    dy, dn_ssm, do, delta, dg, dw_out, dnw_post, loss_part = _outproj_loss(mix, w_out_all, x2, tgt, norm_post_w,
                                                                          attn_pre, proj)
    dz, dxbcdt, dcw, dcb, dvec, dnw_ssm = _ssm_bwd(proj, dn_ssm, y_save, states, conv_out, cw_all, dtb, alog, d_b,
                                                   ssm_norm_w)
    dw_g, dw_z, dw_x = _dw(u, dg, "dw_in_g"), _dw(u, dz, "dw_in_z"), _dw(u, dxbcdt, "dw_in_xbcdt", X_COLS)
    acc = _attn_bwd(proj, do, lse, delta, 16, None, F32, reduce.pairs(dw_g, dw_z, dw_x, dw_out) if reduce else None)
    if reduce:
        acc, got = acc
    acc = _attn_bwd(proj, do, lse, delta, 4, acc, F32, reduce.first(got) if reduce else None)
    if reduce:
        acc, got = acc
        reduce.first_done(got)
    dq, dk, dv = _attn_bwd(proj, do, lse, delta, 1, acc, BF16)
    dw_q, dw_k = _dw(u, dq, "dw_in_q"), _dw(u, dk, "dw_in_k")
    dw_v = _dw(u, dv, "dw_in_v", hosted=reduce.second_pairs(dw_q, dw_k, dw_g) if reduce else None)
    if reduce:
        dw_v, got = dw_v

    def small(dnw_pre):
        return _pack_small(dcw, dnw_pre, dcb, dnw_ssm, dnw_post, dvec[0:1, :NH], dvec[1:2, :NH], dvec[2:3, :NH],
                           loss_part[:, :1])

    res = _inproj_bwd_dx([dq, dk, dv, dg, dz], dxbcdt, w_all, x2, dy, norm_pre_w,
                         reduce.second(dw_v, got, small(jnp.zeros((1, D), F32))) if reduce else None)
    if reduce:
        res, got = res
        reduce.second_done(got)
        return res
    grad_x, dnw_pre = res
    dw_all = jnp.concatenate([dw_q, dw_k, dw_v, dw_g, dw_z, dw_x], axis=1)
    return grad_x, small(dnw_pre), dw_all, dw_out
```

```python
import functools

import jax
import jax.numpy as jnp
from jax import lax
from jax.experimental import pallas as pl
from jax.experimental.pallas import tpu as pltpu

F32 = jnp.float32
BF16 = jnp.bfloat16
MESH = pl.DeviceIdType.MESH
SDS = jax.ShapeDtypeStruct
ANY = pl.BlockSpec(memory_space=pl.ANY)

S = 4096
D = 1024
DP = 7168
SHARD = 1668
OFF_G, OFF_Z = 3072, 4096
NH = 16
CH = 128
NC = S // CH
EPS = 1e-6
NEG = -1e30
LANE = 128
VMEM_LIMIT = 48 * 1024 * 1024

TILES = SHARD // LANE
WIN = (TILES + 1) * LANE
SHIFT = SHARD - TILES * LANE
SECTION_TILES = {"qk": (0, 16), "v": (16, 8), "gz": (24, 16), "x": (40, 13)}
X_COLS = SECTION_TILES["x"][1] * LANE

ADAM_LR, ADAM_B1, ADAM_B2, ADAM_EPS, ADAM_WD, ADAM_STEP = 0.001, 0.9, 0.999, 1e-08, 0.01, 10


def _cp(sem, **kw):
    return pltpu.CompilerParams(dimension_semantics=sem, vmem_limit_bytes=VMEM_LIMIT, **kw)


def _dot(a, b):
    return jnp.dot(a, b, preferred_element_type=F32)


def _dot_nt(a, b):
    return lax.dot_general(a, b, (((1,), (1,)), ((), ())), preferred_element_type=F32)


def _dot_tn(a, b):
    return lax.dot_general(a, b, (((0,), (0,)), ((), ())), preferred_element_type=F32)


def _pieces(x, n):
    out = []
    for _ in range(n):
        p = x.astype(BF16)
        out.append(p)
        x = x - p.astype(F32)
    return out


def _pick(x, sel, n=2):
    parts = [_dot(p, sel) for p in _pieces(x, n)]
    return functools.reduce(jnp.add, parts)


def _pick_left(sel, x, n=3):
    parts = [_dot(sel, p) for p in _pieces(x, n)]
    return functools.reduce(jnp.add, parts)


def _sigmoid(v):
    return 0.5 * jnp.tanh(0.5 * v) + 0.5


def _iota(shape, dim):
    return lax.broadcasted_iota(jnp.int32, shape, dim)


def _inproj_fwd(x, nw, w_all, hosted=None):
    tm, tn = 1024, 1024
    n_host = len(hosted.arrays) if hosted else 0

    def body(x_ref, nw_ref, w_ref, *refs):
        host_in, (proj_ref, u_ref), refs = refs[:n_host], refs[n_host:n_host + 2], refs[n_host + 2:]
        host_out, host_sems = refs[:n_host], refs[n_host:]
        i, j = pl.program_id(0), pl.program_id(1)
        if hosted:
            pl.when((i == 0) & (j == 0))(lambda: hosted.start(host_in, host_out, host_sems))

        @pl.when(j == 0)
        def _():
            xf = x_ref[...]
            r = lax.rsqrt(jnp.mean(xf * xf, axis=-1, keepdims=True) + EPS)
            u_ref[...] = (xf * r * nw_ref[...]).astype(BF16)

        proj_ref[...] = _dot(u_ref[...], w_ref[...])
        if hosted:
            pl.when((i == S // tm // 2) & (j == 0))(lambda: hosted.pass_on(host_in, host_out, host_sems))
            pl.when((i == S // tm - 1) & (j == DP // tn - 1))(lambda: hosted.finish(host_in, host_out, host_sems))

    outs = pl.pallas_call(
        body, name="inproj_fwd", grid=(S // tm, DP // tn),
        in_specs=[pl.BlockSpec((tm, D), lambda i, j: (i, 0)), pl.BlockSpec((1, D), lambda i, j: (0, 0)),
                  pl.BlockSpec((D, tn), lambda i, j: (0, j))] + [ANY] * n_host,
        out_specs=[pl.BlockSpec((tm, tn), lambda i, j: (i, j)), pl.BlockSpec((tm, D), lambda i, j: (i, 0))]
        + [ANY] * n_host,
        out_shape=[SDS((S, DP), F32), SDS((S, D), BF16)] + (hosted.out_shape if hosted else []),
        scratch_shapes=hosted.scratch if hosted else [],
        compiler_params=_cp(("arbitrary", "arbitrary") if hosted else ("parallel", "arbitrary")),
    )(x, nw, w_all, *(hosted.arrays if hosted else []))
    return (outs[:2], outs[2:]) if hosted else outs


ATTN_QB = {1: 16, 4: 4, 16: 1}


def _unit_rows(r, u, d):
    return pl.ds(r + d * CH * u, CH, stride=d) if d > 1 else pl.ds(CH * u, CH)


def _for_units(d, qb, fn):
    for r in range(d):
        for u in range(qb):
            fn(r, u)


def _attn_mask(has_prev):
    qi, kj = _iota((2 * CH, 2 * CH), 0) & (CH - 1), _iota((2 * CH, 2 * CH), 1)
    cur_ok = (kj >= CH) & (kj - CH <= qi)
    prev_ok = (kj < CH) & (kj >= qi)
    return cur_ok | (prev_ok & has_prev)


def _stack_heads(v, lane_a):
    return jnp.concatenate([jnp.where(lane_a, v, 0.0), jnp.where(lane_a, 0.0, v)], axis=0).astype(BF16)


def _attn_specs(d, qb):
    rows, prows = CH * d * qb, CH * d
    nb = S // rows
    steps = (NH // 2) * nb

    def at(t):
        t = jnp.minimum(t, steps - 1)
        return t % nb, t // nb

    def cur(off):
        return pl.BlockSpec((rows, LANE), lambda t: (at(t)[0], off + at(t)[1]))

    def prev(off):
        return pl.BlockSpec((prows, LANE), lambda t: (jnp.maximum(at(t)[0] * qb - 1, 0), off + at(t)[1]))

    lag = pl.BlockSpec((rows, LANE), lambda t: at(jnp.maximum(t - 1, 0)))
    return nb, steps, cur, prev, lag


def _gather16(src_ref, dense_ref, tmp_ref):
    for a in range(4):
        tmp_ref[...] = src_ref[pl.ds(a, 4 * CH, stride=4), :]
        for b in range(4):
            dense_ref[a + 4 * b] = tmp_ref[pl.ds(b, CH, stride=4), :]


def _scatter16(dense_ref, dst_ref, tmp_ref):
    for a in range(4):
        for b in range(4):
            tmp_ref[pl.ds(b, CH, stride=4), :] = dense_ref[a + 4 * b]
        dst_ref[pl.ds(a, 4 * CH, stride=4), :] = tmp_ref[...]


def _unit_index(r, u, d):
    return (r,) if d == 16 else (_unit_rows(r, u, d), slice(None))


def _unit_kv(p_ref, c_ref, r, u, d):
    prev = p_ref[_unit_index(r, 0, d)] if u == 0 else c_ref[_unit_index(r, u - 1, d)]
    return jnp.concatenate([prev, c_ref[_unit_index(r, u, d)]], axis=0).astype(BF16)


def _dense_scratch(d, n):
    return [pltpu.VMEM((16, CH, LANE), F32)] * n + [pltpu.VMEM((4 * CH, LANE), F32)] if d == 16 else []


def _attn_fwd(proj, d, prior=None, final=False):
    qb = ATTN_QB[d]
    nb, steps, cur, prev, _ = _attn_specs(d, qb)
    n_prior = 2 if prior is not None else 0
    n_in, n_out = 5 + n_prior + final, 2 + final
    assert not (d == 16 and (n_prior or final))

    def body(*refs):
        ins, outs, scratch = refs[:n_in], refs[n_in:n_in + n_out], refs[n_in + n_out:]
        if d == 16:
            tmp_ref = scratch[-1]
            for src, dense in zip(ins, scratch):
                _gather16(src, dense, tmp_ref)
            block_outs, ins, outs = outs, scratch[:n_in], scratch[n_in:n_in + n_out]
        q_ref, kp_ref, kc_ref, vp_ref, vc_ref = ins[:5]
        prior_refs = ins[5:5 + n_prior]
        if final:
            g_ref, (mix_ref, o_ref, l_ref) = ins[-1], outs
        else:
            o_ref, l_ref = outs
        i = pl.program_id(0) % nb
        lane_a = _iota((CH, LANE), 1) < 64
        mask_first, mask_rest = _attn_mask(i > 0), _attn_mask(True)

        def unit(r, u):
            at = _unit_index(r, u, d)
            q2 = _stack_heads(q_ref[at] * 0.125, lane_a)
            k2, v2 = _unit_kv(kp_ref, kc_ref, r, u, d), _unit_kv(vp_ref, vc_ref, r, u, d)
            s = jnp.where(mask_first if u == 0 else mask_rest, _dot_nt(q2, k2), NEG)
            m = jnp.max(s, axis=1, keepdims=True)
            p = jnp.exp(s - m)
            l = jnp.sum(p, axis=1, keepdims=True)
            o2 = _dot(p.astype(BF16), v2) / l
            lse2 = m + jnp.log(l)
            o = jnp.where(lane_a, o2[:CH], o2[CH:])
            lse = jnp.where(lane_a, lse2[:CH], lse2[CH:])
            if n_prior:
                o_a, l_a = prior_refs[0][at], prior_refs[1][at]
                top = jnp.maximum(l_a, lse)
                e_a, e_b = jnp.exp(l_a - top), jnp.exp(lse - top)
                tot = e_a + e_b
                o = (e_a * o_a + e_b * o) / tot
                lse = top + jnp.log(tot)
            o_ref[at] = o
            l_ref[at] = lse
            if final:
                g = g_ref[at]
                mix_ref[at] = (o * (g * _sigmoid(g))).astype(BF16)

        _for_units(d, qb, unit)
        if d == 16:
            for dense, dst in zip(outs, block_outs):
                _scatter16(dense, dst, tmp_ref)

    in_specs = [cur(0), prev(8), cur(8), prev(16), cur(16)] + [cur(0)] * n_prior
    args = [proj] * 5 + (list(prior) if n_prior else [])
    out_specs, out_shape = [cur(0), cur(0)], [SDS((S, D), F32), SDS((S, D), F32)]
    if final:
        assert d == 1
        in_specs.append(cur(OFF_G // LANE))
        args.append(proj)
        out_specs, out_shape = [cur(0)] + out_specs, [SDS((S, 2 * D), BF16)] + out_shape
    return pl.pallas_call(
        body, name=f"attn_fwd_d{d}", grid=(steps,),
        in_specs=in_specs, out_specs=out_specs, out_shape=out_shape,
        scratch_shapes=_dense_scratch(d, n_in + n_out),
        compiler_params=_cp(("parallel",)),
    )(*args)


def _attn_bwd(proj, do, lse, delta, d, acc, out_dtype, hosted=None):
    qb = ATTN_QB[d]
    nb, steps, cur, prev, lag = _attn_specs(d, qb)
    has_acc = acc is not None
    n_in = 11 if has_acc else 8
    n_host, n_host_out = (len(hosted.arrays), len(hosted.out_shape)) if hosted else (0, 0)
    assert not (d == 16 and (has_acc or out_dtype != F32))
    rows = CH * d * qb
    carry = (2, 16, CH, LANE) if d == 16 else (2, rows, LANE)

    def body(*refs):
        ins, host_in, refs = refs[:n_in], refs[n_in:n_in + n_host], refs[n_in + n_host:]
        (dq_ref, dk_ref, dv_ref), host_out, scratch = refs[:3], refs[3:3 + n_host_out], refs[3 + n_host_out:]
        if hosted:
            scratch, host_sems = scratch[:-len(hosted.scratch)], scratch[-len(hosted.scratch):]
        ck_ref, cv_ref = scratch[:2]
        dq_f32 = dq_ref if out_dtype == F32 else scratch[2]
        t = pl.program_id(0)
        i = t % nb
        if hosted:
            pl.when(t == 0)(lambda: hosted.start(host_in, host_out, host_sems))
        if d == 16:
            dense, dq_f32, tmp_ref = scratch[2:2 + n_in], scratch[2 + n_in], scratch[-1]

            @pl.when(t < steps)
            def _():
                for src, dst in zip(ins, dense):
                    _gather16(src, dst, tmp_ref)

            ins = dense
        q_ref, kp_ref, kc_ref, vp_ref, vc_ref, do_ref, lse_ref, dl_ref = ins[:8]
        if has_acc:
            aq_ref, ak_ref, av_ref = ins[8:11]
        slot = t & 1
        now_k, now_v, old_k, old_v = ck_ref.at[slot], cv_ref.at[slot], ck_ref.at[1 - slot], cv_ref.at[1 - slot]
        lane_a = _iota((CH, LANE), 1) < 64
        mask_first, mask_rest = _attn_mask(i > 0), _attn_mask(True)

        @pl.when(t == 0)
        def _():
            ck_ref[1] = jnp.zeros(carry[1:], F32)
            cv_ref[1] = jnp.zeros(carry[1:], F32)

        def unit(r, u):
            at = _unit_index(r, u, d)
            q2 = _stack_heads(q_ref[at] * 0.125, lane_a)
            do2 = _stack_heads(do_ref[at], lane_a)
            k2, v2 = _unit_kv(kp_ref, kc_ref, r, u, d), _unit_kv(vp_ref, vc_ref, r, u, d)
            lsev, dlv = lse_ref[at], dl_ref[at]
            lse2 = jnp.concatenate([lsev[:, 0:1], lsev[:, 64:65]], axis=0)
            dl2 = jnp.concatenate([dlv[:, 0:1], dlv[:, 64:65]], axis=0)
            p = jnp.exp(jnp.where(mask_first if u == 0 else mask_rest, _dot_nt(q2, k2), NEG) - lse2)
            ds = (p * (_dot_nt(do2, v2) - dl2)).astype(BF16)
            dq2 = _dot(ds, k2)
            dk2 = _dot_tn(ds, q2)
            dv2 = _dot_tn(p.astype(BF16), do2)
            dq = jnp.where(lane_a, dq2[:CH], dq2[CH:]) * 0.125
            if has_acc:
                dq = dq + aq_ref[at]
            dq_f32[at] = dq
            if u == 0:
                before = _unit_index(r, qb - 1, d)
                old_k[before] += dk2[:CH]
                old_v[before] += dv2[:CH]
            else:
                before = _unit_index(r, u - 1, d)
                now_k[before] += dk2[:CH]
                now_v[before] += dv2[:CH]
            now_k[at] = dk2[CH:]
            now_v[at] = dv2[CH:]

        @pl.when(t < steps)
        def _():
            _for_units(d, qb, unit)
            if d == 16:
                _scatter16(dq_f32, dq_ref, tmp_ref)
            elif out_dtype != F32:
                dq_ref[...] = dq_f32[...].astype(out_dtype)

        if d == 16:
            _scatter16(old_k, dk_ref, tmp_ref)
            _scatter16(old_v, dv_ref, tmp_ref)
        else:
            dk, dv = old_k[...], old_v[...]
            if has_acc:
                dk, dv = dk + ak_ref[...], dv + av_ref[...]
            dk_ref[...] = dk.astype(out_dtype)
            dv_ref[...] = dv.astype(out_dtype)
        if hosted:
            pl.when(t == steps)(lambda: hosted.finish(host_in, host_out, host_sems))

    in_specs = [cur(0), prev(8), cur(8), prev(16), cur(16), cur(0), cur(0), cur(0)]
    args = [proj, proj, proj, proj, proj, do, lse, delta]
    if has_acc:
        in_specs += [cur(0), lag, lag]
        args += list(acc)
    scratch = [pltpu.VMEM(carry, F32), pltpu.VMEM(carry, F32)]
    if d == 16:
        scratch += _dense_scratch(d, n_in + 1)
    elif out_dtype != F32:
        scratch.append(pltpu.VMEM((rows, LANE), F32))
    out_specs, out_shape = [cur(0), lag, lag], [SDS((S, D), out_dtype)] * 3
    if hosted:
        args += hosted.arrays
        in_specs += [ANY] * n_host
        out_specs += [ANY] * n_host_out
        out_shape += hosted.out_shape
        scratch += hosted.scratch
    outs = pl.pallas_call(
        body, name=f"attn_bwd_d{d}", grid=(steps + 1,),
        in_specs=in_specs, out_specs=out_specs, out_shape=out_shape,
        scratch_shapes=scratch, compiler_params=_cp(("arbitrary",)),
    )(*args)
    return (outs[:3], outs[3:]) if hosted else outs


def _conv_taps(cur, prev8, first):
    row8 = _iota(prev8.shape, 0)
    prev8 = jnp.where(first, 0.0, prev8)
    taps = []
    for s in (3, 2, 1):
        rolled = pltpu.roll(cur, s, 0)
        head = jnp.where(row8 < s, pltpu.roll(prev8, s, 0), rolled[:8])
        taps.append(jnp.concatenate([head, rolled[8:]], axis=0))
    return taps + [cur]


def _conv(taps, w, b):
    acc = b + w[0:1, :] * taps[0]
    for k in (1, 2, 3):
        acc = acc + w[k:k + 1, :] * taps[k]
    return acc


def _expand():
    return (_iota((LANE, D), 1) // 64 == _iota((LANE, D), 0)).astype(BF16)


def _reduce():
    return (_iota((D, LANE), 0) // 64 == _iota((D, LANE), 1)).astype(BF16)


def _ssd_common(xs_c, bc_c, dt_raw, dtb, alog):
    head_lane = _iota((CH, LANE), 1) < NH
    xs = xs_c * _sigmoid(xs_c)
    bc = bc_c * _sigmoid(bc_c)
    pre = dt_raw + dtb
    dt = jnp.where(head_lane, jnp.maximum(pre, 0.0) + jnp.log(1.0 + jnp.exp(-jnp.abs(pre))), 0.0)
    a_row = jnp.where(head_lane[0:1], -jnp.exp(alog), 0.0)
    tri = (_iota((CH, CH), 1) <= _iota((CH, CH), 0)).astype(BF16)
    cs = _pick_left(tri, dt * a_row)
    cs_last = cs[CH - 1:CH, :]
    wide = _pick(jnp.concatenate([dt, jnp.exp(cs), jnp.exp(cs_last - cs)], axis=0), _expand())
    dt_b, e_b, f_b = wide[:CH], wide[CH:2 * CH], wide[2 * CH:]
    return dict(xs=xs, bc=bc, pre=pre, dt=dt, a_row=a_row, cs=cs, cs_t=cs.T, dt_b=dt_b, e_b=e_b, f_b=f_b,
                t_b=e_b[CH - 1:CH, :])


def _groups(bc):
    bcb = bc.astype(BF16)
    return [bcb[:, 0:128], bcb[:, 128:256]], [bcb[:, 256:384], bcb[:, 384:512]]


def _decay(q, h, tril):
    seg = q["cs"][:, h:h + 1] - q["cs_t"][h:h + 1, :]
    return jnp.exp(jnp.where(tril, seg, NEG))


def _ssm_fwd(proj, mix, cw, cb, dtb, alog, d_b, nw):
    def body(xs_ref, xsp_ref, bc_ref, bcp_ref, dt_ref, z_ref, cw_ref, cb_ref, dtb_ref, alog_ref, db_ref, nw_ref,
             mix_in_ref, mix_ref, y_ref, st_ref, conv_ref, h_ref):
        del mix_in_ref
        i = pl.program_id(0)

        @pl.when(i == 0)
        def _():
            h_ref[...] = jnp.zeros_like(h_ref)

        cw, cb = cw_ref[...], cb_ref[...]
        xs_c = _conv(_conv_taps(xs_ref[...], xsp_ref[...], i == 0), cw[:, :D], cb[:, :D])
        bc_c = _conv(_conv_taps(bc_ref[...], bcp_ref[...], i == 0), cw[:, D:], cb[:, D:])
        conv_ref[:, :D] = xs_c
        conv_ref[:, D:] = bc_c
        q = _ssd_common(xs_c, bc_c, dt_ref[...], dtb_ref[...], alog_ref[...])
        bg, cg = _groups(q["bc"])
        xs = q["xs"]
        xdt = xs * q["dt_b"]
        xdt_b = xdt.astype(BF16)
        h_in = h_ref[...]
        st_ref[...] = h_in
        hb = h_in.astype(BF16)
        tril = _iota((CH, CH), 1) <= _iota((CH, CH), 0)
        lane_a = _iota((CH, LANE), 1) < 64
        cbm = [_dot_nt(cg[g], bg[g]) for g in range(2)]
        pairs = []
        for hp in range(NH // 2):
            xp = xdt_b[:, hp * LANE:(hp + 1) * LANE]
            ya = _dot((cbm[hp // 4] * _decay(q, 2 * hp, tril)).astype(BF16), xp)
            yb = _dot((cbm[hp // 4] * _decay(q, 2 * hp + 1, tril)).astype(BF16), xp)
            pairs.append(jnp.where(lane_a, ya, yb))
        y_diag = jnp.concatenate(pairs, axis=1)
        y_off = jnp.concatenate([_dot(cg[g], hb[:, g * 512:(g + 1) * 512]) for g in range(2)], axis=1) * q["e_b"]
        y = y_diag + y_off + db_ref[...] * xs
        y_ref[...] = y
        xf = (xdt * q["f_b"]).astype(BF16)
        h_ref[...] = q["t_b"] * h_in + jnp.concatenate(
            [_dot_tn(bg[g], xf[:, g * 512:(g + 1) * 512]) for g in range(2)], axis=1)
        z = z_ref[...]
        yz = y * (z * _sigmoid(z))
        outs = []
        for g in range(2):
            v = yz[:, g * 512:(g + 1) * 512]
            outs.append(v * lax.rsqrt(jnp.mean(v * v, axis=-1, keepdims=True) + EPS))
        mix_ref[...] = (jnp.concatenate(outs, axis=1) * nw_ref[...]).astype(BF16)

    def col(width, blk, prev=False):
        if prev:
            return pl.BlockSpec((8, width), lambda i: (jnp.maximum(i * (CH // 8) - 1, 0), blk))
        return pl.BlockSpec((CH, width), lambda i: (i, blk))

    def full(a):
        return pl.BlockSpec(a.shape, lambda i: (0,) * a.ndim)

    return pl.pallas_call(
        body, name="ssm_fwd", grid=(NC,),
        in_specs=[col(D, 5), col(D, 5, True), col(512, 12), col(512, 12, True), col(LANE, 52), col(D, 4),
                  full(cw), full(cb), full(dtb), full(alog), full(d_b), full(nw), ANY],
        out_specs=[col(D, 1), col(D, 0), pl.BlockSpec((None, CH, D), lambda i: (i, 0, 0)), col(D + 512, 0)],
        out_shape=[SDS((S, 2 * D), BF16), SDS((S, D), F32), SDS((NC, CH, D), F32), SDS((S, D + 512), F32)],
        scratch_shapes=[pltpu.VMEM((CH, D), F32)],
        input_output_aliases={12: 0},
        compiler_params=_cp(("arbitrary",)),
    )(proj, proj, proj, proj, proj, proj, cw, cb, dtb, alog, d_b, nw, mix)


def _ssm_bwd(proj, dn, y_save, states, conv_out, cw, dtb, alog, d_b, nw):
    def body(xs_ref, bc_ref, dt_ref, z_ref, dn_ref, y_ref, st_ref, conv_ref,
             cw_ref, dtb_ref, alog_ref, db_ref, nw_ref,
             dz_ref, dx_ref, dcw_ref, dcb_ref, dsm_ref, dnw_ref, dh_ref, nxs_ref, nbc_ref):
        i = pl.program_id(0)
        ci = NC - 1 - i

        @pl.when(i == 0)
        def _():
            for ref in (dcw_ref, dcb_ref, dsm_ref, dnw_ref, dh_ref, nxs_ref, nbc_ref):
                ref[...] = jnp.zeros_like(ref)

        cw = cw_ref[...]
        xs_c, bc_c = conv_ref[:, :D], conv_ref[:, D:]
        q = _ssd_common(xs_c, bc_c, dt_ref[...], dtb_ref[...], alog_ref[...])
        bg, cg = _groups(q["bc"])
        xs, dt_b, e_b, f_b, t_b = q["xs"], q["dt_b"], q["e_b"], q["f_b"], q["t_b"]
        xdt = xs * dt_b
        xdt_b = xdt.astype(BF16)
        h_in = st_ref[...]
        hb = h_in.astype(BF16)
        dh_new = dh_ref[...]
        dhb = dh_new.astype(BF16)
        red = _reduce()

        z, y, dn, nw_v = z_ref[...], y_ref[...], dn_ref[...], nw_ref[...]
        sig = _sigmoid(z)
        sz = z * sig
        yz = y * sz
        gdn = dn * nw_v
        dyz, dnw = [], []
        for g in range(2):
            v, gv = yz[:, g * 512:(g + 1) * 512], gdn[:, g * 512:(g + 1) * 512]
            r = lax.rsqrt(jnp.mean(v * v, axis=-1, keepdims=True) + EPS)
            dnw.append(dn[:, g * 512:(g + 1) * 512] * v * r)
            dyz.append(r * (gv - v * (r * r) * jnp.mean(gv * v, axis=-1, keepdims=True)))
        dyz = jnp.concatenate(dyz, axis=1)
        dnw_ref[...] += jnp.sum(jnp.concatenate(dnw, axis=1), axis=0, keepdims=True)
        dy = dyz * sz
        dz_ref[...] = (dyz * y * (sig * (1.0 + z * (1.0 - sig)))).astype(BF16)
        dy_b = dy.astype(BF16)

        tril = _iota((CH, CH), 1) <= _iota((CH, CH), 0)
        lane_a = _iota((CH, LANE), 1) < 64
        cbm = [_dot_nt(cg[g], bg[g]) for g in range(2)]
        dcbm = [jnp.zeros((CH, CH), F32), jnp.zeros((CH, CH), F32)]
        seg_rows = jnp.zeros((CH, LANE), F32)
        seg_cols = jnp.zeros((LANE, CH), F32)
        row_id, col_id = _iota((CH, LANE), 0), _iota((CH, LANE), 1)
        dx_pairs = []
        for hp in range(NH // 2):
            g = hp // 4
            xp = xdt_b[:, hp * LANE:(hp + 1) * LANE]
            dyp_f = dy[:, hp * LANE:(hp + 1) * LANE]
            dyp = dy_b[:, hp * LANE:(hp + 1) * LANE]
            halves = []
            for k in range(2):
                h = 2 * hp + k
                lane = lane_a if k == 0 else jnp.logical_not(lane_a)
                dec = _decay(q, h, tril)
                gm = cbm[g] * dec
                dgm = _dot_nt(jnp.where(lane, dyp_f, 0.0).astype(BF16), xp)
                dcbm[g] = dcbm[g] + dgm * dec
                prod = dgm * gm
                seg_rows = jnp.where(col_id == h, jnp.sum(prod, axis=1, keepdims=True), seg_rows)
                seg_cols = jnp.where(row_id == h, jnp.sum(prod, axis=0, keepdims=True), seg_cols)
                halves.append(_dot_tn(gm.astype(BF16), dyp))
            dx_pairs.append(jnp.where(lane_a, halves[0], halves[1]))
        dxdt_diag = jnp.concatenate(dx_pairs, axis=1)

        qv = jnp.concatenate([_dot(bg[g], dhb[:, g * 512:(g + 1) * 512]) for g in range(2)], axis=1)
        y_off = jnp.concatenate([_dot(cg[g], hb[:, g * 512:(g + 1) * 512]) for g in range(2)], axis=1) * e_b
        xfq = xdt * f_b * qv
        dxdt = dxdt_diag + f_b * qv
        tdt = jnp.sum(dh_new * h_in, axis=0, keepdims=True) * t_b
        per_head = _pick(jnp.concatenate([xfq, dy * y_off, dxdt * xs, dy * xs, jnp.broadcast_to(tdt, (8, D))],
                                         axis=0), red)
        fdf, dyoff_h, dxdtxs_h, dyxs_h = [per_head[k * CH:(k + 1) * CH] for k in range(4)]
        dcs = seg_rows - seg_cols.T + dyoff_h - fdf
        last = per_head[4 * CH:4 * CH + 1] + jnp.sum(fdf, axis=0, keepdims=True)
        dcs = dcs + jnp.where(_iota((CH, LANE), 0) == CH - 1, last, 0.0)
        tri_t = (_iota((CH, CH), 1) >= _iota((CH, CH), 0)).astype(BF16)
        da = _pick_left(tri_t, dcs)
        ddt = da * q["a_row"] + dxdtxs_h
        dxs = dxdt * dt_b + db_ref[...] * dy
        ddt_raw = ddt * _sigmoid(q["pre"])
        dsm_ref[0:1, :] += jnp.sum(ddt_raw, axis=0, keepdims=True)
        dsm_ref[1:2, :] += jnp.sum(da * q["dt"], axis=0, keepdims=True) * q["a_row"]
        dsm_ref[2:3, :] += jnp.sum(dyxs_h, axis=0, keepdims=True)
        edy = (e_b * dy).astype(BF16)
        xf = (xdt * f_b).astype(BF16)
        dbs, dcs_g, dhs = [], [], []
        for g in range(2):
            sl = slice(g * 512, (g + 1) * 512)
            dcb_b = dcbm[g].astype(BF16)
            dcs_g.append(_dot(dcb_b, bg[g]) + _dot_nt(edy[:, sl], hb[:, sl]))
            dbs.append(_dot_tn(dcb_b, cg[g]) + _dot_nt(xf[:, sl], dhb[:, sl]))
            dhs.append(_dot_tn(cg[g], edy[:, sl]))
        dh_ref[...] = t_b * dh_new + jnp.concatenate(dhs, axis=1)
        dbc = jnp.concatenate(dbs + dcs_g, axis=1)

        def conv_bwd(dact, pre, x_raw, w, nxt_ref, lo):
            s = _sigmoid(pre)
            dconv = dact * (s * (1.0 + pre * (1.0 - s)))
            nxt8 = nxt_ref[...]
            row8 = _iota(nxt8.shape, 0)
            hi = lo + dconv.shape[1]
            dcb_ref[:, lo:hi] += jnp.sum(dconv, axis=0, keepdims=True)
            later = [dconv]
            for s_ in (1, 2, 3):
                rolled = pltpu.roll(dconv, CH - s_, 0)
                tail = jnp.where(row8 >= 8 - s_, pltpu.roll(nxt8, 8 - s_, 0), rolled[CH - 8:])
                later.append(jnp.concatenate([rolled[:CH - 8], tail], axis=0))
            dx = None
            for s_, up in enumerate(later):
                k = 3 - s_
                dcw_ref[k:k + 1, lo:hi] += jnp.sum(up * x_raw, axis=0, keepdims=True)
                dx = w[k:k + 1, :] * up if dx is None else dx + w[k:k + 1, :] * up
            nxt_ref[...] = dconv[:8]
            return dx

        dx_ref[:, 0:D] = conv_bwd(dxs, xs_c, xs_ref[...], cw[:, :D], nxs_ref, 0).astype(BF16)
        dx_ref[:, D:D + 512] = conv_bwd(dbc, bc_c, bc_ref[...], cw[:, D:], nbc_ref, D).astype(BF16)
        dx_ref[:, D + 512:D + 640] = ddt_raw.astype(BF16)
        dx_ref[:, D + 640:] = jnp.zeros((CH, D - 640), BF16)

    def col(width, blk):
        return pl.BlockSpec((CH, width), lambda i: (NC - 1 - i, blk))

    def full(a):
        return pl.BlockSpec(a.shape, lambda i: (0,) * len(a.shape))

    acc_shapes = [SDS((4, 1536), F32), SDS((1, 1536), F32), SDS((8, LANE), F32), SDS((1, D), F32)]
    return pl.pallas_call(
        body, name="ssm_bwd", grid=(NC,),
        in_specs=[col(D, 5), col(512, 12), col(LANE, 52), col(D, 4),
                  col(D, 0), col(D, 0), pl.BlockSpec((None, CH, D), lambda i: (NC - 1 - i, 0, 0)), col(D + 512, 0),
                  full(cw), full(dtb), full(alog), full(d_b), full(nw)],
        out_specs=[col(D, 0), col(2 * D, 0)] + [full(a) for a in acc_shapes],
        out_shape=[SDS((S, D), BF16), SDS((S, 2 * D), BF16)] + acc_shapes,
        scratch_shapes=[pltpu.VMEM((CH, D), F32), pltpu.VMEM((8, D), F32), pltpu.VMEM((8, 512), F32)],
        compiler_params=_cp(("arbitrary",)),
    )(proj, proj, proj, proj, dn, y_save, states, conv_out, cw, dtb, alog, d_b, nw)


def _outproj_loss(mix, w_out, x, tgt, nw, attn_pre, proj):
    tm = 256

    def body(mix_ref, w_ref, x_ref, t_ref, nw_ref, pre_ref, g_ref,
             dy_ref, dn_ref, do_ref, delta_ref, dg_ref, dw_ref, dnw_ref, loss_ref):
        @pl.when(pl.program_id(0) == 0)
        def _():
            dw_ref[...] = jnp.zeros_like(dw_ref)
            dnw_ref[...] = jnp.zeros_like(dnw_ref)
            loss_ref[...] = jnp.zeros_like(loss_ref)

        mixv, w = mix_ref[...], w_ref[...]
        out = _dot(mixv, w)
        r = lax.rsqrt(jnp.mean(out * out, axis=-1, keepdims=True) + EPS)
        nh = out * r
        nw_v = nw_ref[...]
        err = x_ref[...] + nh * nw_v - t_ref[...]
        loss_ref[...] += 0.5 * jnp.sum(jnp.mean(err * err, axis=-1, keepdims=True), axis=0, keepdims=True)
        dy = err * (1.0 / D)
        dy_ref[...] = dy
        dnw_ref[...] += jnp.sum(dy * nh, axis=0, keepdims=True)
        gdn = dy * nw_v
        dout = (r * (gdn - nh * jnp.mean(gdn * nh, axis=-1, keepdims=True))).astype(BF16)
        dmix = _dot_nt(dout, w)
        dw_ref[...] += _dot_tn(mixv, dout)
        dn_ref[...] = dmix[:, D:]
        dm, g, pre_v = dmix[:, :D], g_ref[...], pre_ref[...]
        sig = _sigmoid(g)
        do = dm * (g * sig)
        do_ref[...] = do
        dg_ref[...] = (dm * pre_v * (sig * (1.0 + g * (1.0 - sig)))).astype(BF16)
        prod = do * pre_v
        same_head = (_iota((LANE, LANE), 0) // 64 == _iota((LANE, LANE), 1) // 64).astype(BF16)
        for cb in range(D // LANE):
            delta_ref[:, cb * LANE:(cb + 1) * LANE] = _pick(prod[:, cb * LANE:(cb + 1) * LANE], same_head)

    row = lambda w: pl.BlockSpec((tm, w), lambda i: (i, 0))
    full = lambda s: pl.BlockSpec(s, lambda i: (0, 0))
    return pl.pallas_call(
        body, name="outproj_loss", grid=(S // tm,),
        in_specs=[row(2 * D), full((2 * D, D)), row(D), row(D), full((1, D)), row(D),
                  pl.BlockSpec((tm, D), lambda i: (i, OFF_G // D))],
        out_specs=[row(D), row(D), row(D), row(D), row(D), full((2 * D, D)), full((1, D)), full((1, LANE))],
        out_shape=[SDS((S, D), F32)] * 4 + [SDS((S, D), BF16), SDS((2 * D, D), F32), SDS((1, D), F32),
                                            SDS((1, LANE), F32)],
        compiler_params=_cp(("arbitrary",)),
    )(mix, w_out, x, tgt, nw, attn_pre, proj)


def _inproj_bwd_dx(srcs, dxbcdt, w_all, x, dy, nw, hosted=None):
    tm = 512
    nk = DP // D
    n_host, n_host_out = (len(hosted.arrays), len(hosted.out_shape)) if hosted else (0, 0)

    def body(*refs):
        src_refs = refs[:nk]
        w_ref, x_ref, dy_ref, nw_ref = refs[nk:nk + 4]
        host_in, refs = refs[nk + 4:nk + 4 + n_host], refs[nk + 4 + n_host:]
        gx_ref, dnw_ref = refs[:2]
        host_out, host_sems = refs[2:2 + n_host_out], refs[2 + n_host_out:]
        i = pl.program_id(0)

        @pl.when(i == 0)
        def _():
            if hosted:
                hosted.start(host_in, host_out, host_sems)
            dnw_ref[...] = jnp.zeros_like(dnw_ref)

        du = None
        for k, ref in enumerate(src_refs):
            width = min(D, 5 * D + X_COLS - k * D)
            part = _dot_nt(ref[:, :width], w_ref[:, k * D:k * D + width])
            du = part if du is None else du + part
        xf, nw_v = x_ref[...], nw_ref[...]
        r = lax.rsqrt(jnp.mean(xf * xf, axis=-1, keepdims=True) + EPS)
        xh = xf * r
        dnw_ref[...] += jnp.sum(du * xh, axis=0, keepdims=True)
        gdu = du * nw_v
        gx_ref[...] = r * (gdu - xh * jnp.mean(gdu * xh, axis=-1, keepdims=True)) + dy_ref[...]

        if hosted:
            pl.when(i == S // tm - 1)(lambda: hosted.finish(host_in, host_out, host_sems))

    row = pl.BlockSpec((tm, D), lambda i: (i, 0))
    row1 = pl.BlockSpec((tm, D), lambda i: (i, 1))
    one = pl.BlockSpec((1, D), lambda i: (0, 0))
    whole_w = pl.BlockSpec((D, DP), lambda i: (0, 0), pipeline_mode=pl.Buffered(1))
    args = [*srcs, dxbcdt, dxbcdt, w_all, x, dy, nw]
    in_specs = [row] * len(srcs) + [row, row1, whole_w, row, row, one]
    out_specs, out_shape, scratch = [row, one], [SDS((S, D), F32), SDS((1, D), F32)], []
    if hosted:
        args += hosted.arrays
        in_specs += [ANY] * n_host
        out_specs += [ANY] * n_host_out
        out_shape += hosted.out_shape
        scratch += hosted.scratch
    outs = pl.pallas_call(
        body, name="inproj_bwd_dx", grid=(S // tm,),
        in_specs=in_specs, out_specs=out_specs, out_shape=out_shape, scratch_shapes=scratch,
        compiler_params=_cp(("arbitrary",)),
    )(*args)
    return (outs[:2], outs[2:]) if hosted else outs


def _dw(u, dsec, name, width=D, hosted=None):
    ts = 1024
    n_host, n_host_out = (len(hosted.arrays), len(hosted.out_shape)) if hosted else (0, 0)

    def body(u_ref, d_ref, *refs):
        host_in, o_ref, refs = refs[:n_host], refs[n_host], refs[n_host + 1:]
        host_out, host_sems = refs[:n_host_out], refs[n_host_out:]
        i = pl.program_id(0)

        @pl.when(i == 0)
        def _():
            if hosted:
                hosted.start(host_in, host_out, host_sems)
            o_ref[...] = jnp.zeros_like(o_ref)

        o_ref[...] += _dot_tn(u_ref[...], d_ref[...])
        if hosted:
            pl.when(i == S // ts - 1)(lambda: hosted.finish(host_in, host_out, host_sems))

    outs = pl.pallas_call(
        body, name=name, grid=(S // ts,),
        in_specs=[pl.BlockSpec((ts, D), lambda i: (i, 0)), pl.BlockSpec((ts, width), lambda i: (i, 0))]
        + [ANY] * n_host,
        out_specs=[pl.BlockSpec((D, width), lambda i: (0, 0))] + [ANY] * n_host_out,
        out_shape=[SDS((D, width), F32)] + (hosted.out_shape if hosted else []),
        scratch_shapes=hosted.scratch if hosted else [],
        compiler_params=_cp(("arbitrary",)),
    )(u, dsec, *(hosted.arrays if hosted else []))
    return (outs[0], outs[1:]) if hosted else outs[0]


def _dw_pair(u, da, db, name):
    ts = 1024
    last = S // ts - 1

    def body(u_ref, a_ref, b_ref, o_ref):
        j = pl.program_id(0)

        @pl.when(pl.program_id(1) == 0)
        def _():
            o_ref[...] = jnp.zeros_like(o_ref)

        for k, d_ref in enumerate((a_ref, b_ref)):
            @pl.when(j == k)
            def _():
                o_ref[...] += _dot_tn(u_ref[...], d_ref[...])

    return pl.pallas_call(
        body, name=name, grid=(2, S // ts),
        in_specs=[pl.BlockSpec((ts, D), lambda j, i: (i, 0)),
                  pl.BlockSpec((ts, D), lambda j, i: (jnp.where(j == 0, i, last), 0)),
                  pl.BlockSpec((ts, D), lambda j, i: (jnp.where(j == 1, i, 0), 0))],
        out_specs=pl.BlockSpec((D, D), lambda j, i: (0, j)),
        out_shape=SDS((D, 2 * D), F32),
        compiler_params=_cp(("arbitrary", "arbitrary")),
    )(u, da, db)


def _place():
    x, y, c = lax.axis_index("x"), lax.axis_index("y"), lax.axis_index("c")
    return x, y, c, 2 * x + y


def _chip_of(x, y, k):
    px = 1 - x if k & 2 else x
    py = 1 - y if k & 1 else y
    return px, py, 2 * px + py


def _remote(src, dst, send_sem, recv_sem, dev):
    return pltpu.make_async_remote_copy(src_ref=src, dst_ref=dst, send_sem=send_sem, recv_sem=recv_sem,
                                        device_id=dev, device_id_type=MESH)


def _gather_weights(w_in_b):
    half = w_in_b.shape[0] // 2
    quarter = half // 2

    def body(src, dst, send, recv):
        x, y, c, j = _place()
        me, sib = (x, y, c), (x, y, 1 - c)
        nbr = {"x": _chip_of(x, y, 2), "y": _chip_of(x, y, 1)}
        diag = _chip_of(x, y, 3)[2]
        started, arrivals = [], []

        def rows(n_quarter=None, sibling=False):
            base = (1 - c if sibling else c) * half
            return pl.ds(base, half) if n_quarter is None else pl.ds(base + n_quarter * quarter, quarter)

        def sem(n):
            return send.at[n], recv.at[n]

        def go(cp):
            cp.start()
            started.append(cp)

        own = _remote(src, dst.at[j], *sem(8), sib)
        go(own)
        for n, axis in enumerate("xy"):
            px, py, _ = nbr[axis]
            go(_remote(src.at[rows()], dst.at[j, rows()], *sem(n), (px, py, c)))
        for n, axis in enumerate("xy"):
            ox, oy, _ = nbr["y" if axis == "x" else "x"]
            pj = nbr[axis][2]
            _remote(src.at[rows()], dst.at[pj, rows()], *sem(n), me).wait_recv()
            go(_remote(dst.at[pj, rows(n)], dst.at[pj, rows(n)], *sem(2 + n), (ox, oy, c)))
            go(_remote(dst.at[pj, rows()], dst.at[pj, rows()], *sem(4 + n), sib))
            arrivals.append(_remote(src.at[rows()], dst.at[pj, rows(None, True)], *sem(4 + n), me))
        for n in range(2):
            _remote(dst.at[diag, rows(n)], dst.at[diag, rows(n)], *sem(2 + n), me).wait_recv()
            go(_remote(dst.at[diag, rows(n)], dst.at[diag, rows(n)], *sem(6 + n), sib))
            arrivals.append(_remote(dst.at[diag, rows(n, True)], dst.at[diag, rows(n, True)], *sem(6 + n), me))
        for cp in arrivals + [own]:
            cp.wait_recv()
        for cp in started:
            cp.wait_send()

    return pl.pallas_call(
        body, name="gather_weights", in_specs=[ANY], out_specs=ANY,
        out_shape=SDS((4,) + w_in_b.shape, BF16),
        scratch_shapes=[pltpu.SemaphoreType.DMA((9,)), pltpu.SemaphoreType.DMA((9,))],
        compiler_params=pltpu.CompilerParams(has_side_effects=True),
    )(w_in_b)


class _LateGather:
    def __init__(self, w_out_b, conv_w):
        self.arrays = [w_out_b, conv_w]
        self.out_shape = [SDS((4,) + w_out_b.shape, BF16), SDS((4,) + conv_w.shape, F32)]
        self.scratch = [pltpu.SemaphoreType.DMA((11,)), pltpu.SemaphoreType.DMA((11,))]

    def _plan(self, ins, outs, sems):
        x, y, c, j = _place()
        send, recv = sems
        (wo, cw), (gwo, gcw) = ins, outs
        half = wo.shape[0] // 2
        mine, theirs = pl.ds(c * half, half), pl.ds((1 - c) * half, half)
        me, sib = (x, y, c), (x, y, 1 - c)
        first, arrive, forward, last = [], [], [], []
        for k in (1, 2, 3):
            px, py, pj = _chip_of(x, y, k)
            first += [_remote(wo.at[mine], gwo.at[j, mine], send.at[k - 1], recv.at[k - 1], (px, py, c)),
                      _remote(cw, gcw.at[j], send.at[k + 2], recv.at[k + 2], (px, py, c))]
            arrive.append(_remote(wo.at[mine], gwo.at[pj, mine], send.at[k - 1], recv.at[k - 1], me))
            forward.append(_remote(gwo.at[pj, mine], gwo.at[pj, mine], send.at[k + 5], recv.at[k + 5], sib))
            last += [_remote(cw, gcw.at[pj], send.at[k + 2], recv.at[k + 2], me),
                     _remote(wo.at[theirs], gwo.at[pj, theirs], send.at[k + 5], recv.at[k + 5], me)]
        first += [_remote(wo, gwo.at[j], send.at[9], recv.at[9], sib),
                  _remote(cw, gcw.at[j], send.at[10], recv.at[10], sib)]
        last += first[-2:]
        return first, arrive, forward, last

    def start(self, ins, outs, sems):
        for cp in self._plan(ins, outs, sems)[0]:
            cp.start()

    def pass_on(self, ins, outs, sems):
        _, arrive, forward, _ = self._plan(ins, outs, sems)
        for got, fwd in zip(arrive, forward):
            got.wait_recv()
            fwd.start()

    def finish(self, ins, outs, sems):
        first, _, forward, last = self._plan(ins, outs, sems)
        for cp in last:
            cp.wait_recv()
        for cp in first + forward:
            cp.wait_send()


def _window(s, names):
    lo, hi = TILES * s, TILES * s + TILES + 1
    pieces = []
    for n, name in enumerate(names):
        a, count = SECTION_TILES[name]
        first, last = max(lo, a), min(hi, a + count)
        if first < last:
            pieces.append((n, first - a, last - first, first - lo))
    assert sum(p[2] for p in pieces) == TILES + 1
    return pieces


class _PairExchange:
    def __init__(self, names, sections, shards, more=()):
        self.names, self.shards = names, shards
        self.there = [n for n, a in enumerate(sections) if a is not None]
        self.arrays = [sections[n] for n in self.there] + list(more)
        self.out_shape = [SDS((len(shards), D // 2, WIN), F32)]
        self.out_shape += [SDS((a.shape[0], a.shape[1] // 2, a.shape[2]), F32) for a in more]
        n = sum(p[0] in self.there for s in shards for p in _window(s, names)) + len(more)
        self.scratch = [pltpu.SemaphoreType.DMA((n,)) for _ in range(2)]

    def _copies(self, ins, outs, sems):
        x, y, c, _ = _place()
        sib = (x, y, 1 - c)
        rows = pl.ds((1 - c) * (D // 2), D // 2)
        k = 0
        for i, s in enumerate(self.shards):
            for n, tile, tiles, at in _window(s, self.names):
                if n in self.there:
                    yield _remote(ins[self.there.index(n)].at[rows, pl.ds(tile * LANE, tiles * LANE)],
                                  outs[0].at[i, :, pl.ds(at * LANE, tiles * LANE)], sems[0].at[k], sems[1].at[k], sib)
                    k += 1
        for src, dst in zip(ins[len(self.there):], outs[1:]):
            half = src.shape[1] // 2
            yield _remote(src.at[:, pl.ds((1 - c) * half, half)], dst, sems[0].at[k], sems[1].at[k], sib)
            k += 1

    def start(self, ins, outs, sems):
        for cp in self._copies(ins, outs, sems):
            cp.start()

    def finish(self, ins, outs, sems):
        for cp in self._copies(ins, outs, sems):
            cp.wait()


def _exchange_call(exchange, name, into=None):
    n, n_out = len(exchange.arrays), len(exchange.out_shape)
    given = list(into) if into else []

    def body(*refs):
        ins, outs, sems = refs[:n], refs[n + len(given):n + len(given) + n_out], refs[n + len(given) + n_out:]
        exchange.start(ins, outs, sems)
        exchange.finish(ins, outs, sems)

    return pl.pallas_call(
        body, name=name, in_specs=[ANY] * (n + len(given)), out_specs=[ANY] * n_out, out_shape=exchange.out_shape,
        input_output_aliases={n + k: k for k in range(len(given))},
        scratch_shapes=exchange.scratch, compiler_params=pltpu.CompilerParams(has_side_effects=True),
    )(*exchange.arrays, *given)


def _pair_sum_windows(cidx, names, sections, shards, r, name):
    n, half, _ = r.shape
    tr = min(half, 256)
    nt = half // tr

    def body(c_ref, *refs):
        del c_ref
        secs, r_ref, o_ref = refs[:-2], refs[-2], refs[-1]
        for i, s in enumerate(shards):
            for k, tile, tiles, at in _window(s, names):
                own = secs[k][:, tile * LANE:(tile + tiles) * LANE]
                there = slice(at * LANE, (at + tiles) * LANE)
                o_ref[i, :, there] = (own + r_ref[i, :, there]).astype(BF16)

    window = pl.BlockSpec((n, tr, WIN), lambda t, c: (0, t, 0))
    return pl.pallas_call(
        body, name=name,
        grid_spec=pltpu.PrefetchScalarGridSpec(
            num_scalar_prefetch=1, grid=(nt,),
            in_specs=[pl.BlockSpec((tr, a.shape[1]), lambda t, c: (c[0] * nt + t, 0)) for a in sections] + [window],
            out_specs=window),
        out_shape=SDS(r.shape, BF16),
        compiler_params=_cp(("parallel",)),
    )(cidx, *sections, r)


def _pair_sum(cidx, g, r, name):
    n, half, width = r.shape
    tr = min(half, 256)
    nt = half // tr

    def body(c_ref, g_ref, r_ref, o_ref):
        del c_ref
        o_ref[...] = (g_ref[...] + r_ref[...]).astype(BF16)

    return pl.pallas_call(
        body, name=name,
        grid_spec=pltpu.PrefetchScalarGridSpec(
            num_scalar_prefetch=1, grid=(n, nt),
            in_specs=[pl.BlockSpec((None, tr, width), lambda s, t, c: (s, c[0] * nt + t, 0)),
                      pl.BlockSpec((None, tr, width), lambda s, t, c: (s, t, 0))],
            out_specs=pl.BlockSpec((None, tr, width), lambda s, t, c: (s, t, 0))),
        out_shape=SDS(r.shape, BF16),
        compiler_params=_cp(("parallel", "parallel")),
    )(cidx, g, r)


class _ChipExchange:
    def __init__(self, arrays, rows):
        self.arrays, self.rows = list(arrays), list(rows)
        self.out_shape = [SDS((4,) + a.shape[1:], BF16) for a in self.arrays]
        self.scratch = [pltpu.SemaphoreType.DMA((3 * len(self.arrays),)) for _ in range(2)]

    def _copies(self, ins, outs, sems):
        x, y, c, j = _place()
        send, recv = sems
        for a, (src, dst, row) in enumerate(zip(ins, outs, self.rows)):
            for k in (1, 2, 3):
                px, py, pj = _chip_of(x, y, k)
                n = 3 * a + k - 1
                slot = pj if row is None else py
                yield (None if row is None else px == row, None if row is None else x == row,
                       _remote(src.at[slot], dst.at[j], send.at[n], recv.at[n], (px, py, c)),
                       _remote(src.at[0], dst.at[pj], send.at[n], recv.at[n], (x, y, c)))

    def start(self, ins, outs, sems):
        for sends, _, send, _ in self._copies(ins, outs, sems):
            if sends is None:
                send.start()
            else:
                pl.when(sends)(send.start)

    def finish(self, ins, outs, sems):
        for sends, owns, send, arrival in self._copies(ins, outs, sems):
            if sends is None:
                arrival.wait_recv()
                send.wait_send()
            else:
                pl.when(owns)(arrival.wait_recv)
                pl.when(sends)(send.wait_send)


def _all_gather_rows(src, dst, rows, send, recv, local_sem):
    x, y, c, j = _place()
    me = 2 * j + c
    local = pltpu.make_async_copy(src, dst.at[me, rows], local_sem)
    cps, arrivals = [], []
    for k in range(1, 8):
        px, py, pj = _chip_of(x, y, k >> 1)
        pc = 1 - c if k & 1 else c
        cps.append(_remote(src, dst.at[me, rows], send.at[k - 1], recv.at[k - 1], (px, py, pc)))
        arrivals.append(_remote(src, dst.at[2 * pj + pc, rows], send.at[k - 1], recv.at[k - 1], (x, y, c)))
    starts = [local.start] + [cp.start for cp in cps]
    waits = [cp.wait_recv for cp in arrivals] + [cp.wait_send for cp in cps] + [local.wait]
    return starts, waits


class _SmallExchange:
    def __init__(self, small):
        self.arrays = [small]
        self.out_shape = [SDS((8,) + small.shape, F32)]
        self.scratch = [pltpu.SemaphoreType.DMA((7,)), pltpu.SemaphoreType.DMA((7,)), pltpu.SemaphoreType.DMA]

    def start(self, ins, outs, sems):
        for go in _all_gather_rows(ins[0], outs[0], slice(None), *sems)[0]:
            go()

    def finish(self, ins, outs, sems):
        for wait in _all_gather_rows(ins[0], outs[0], slice(None), *sems)[1]:
            wait()


class _Both:
    def __init__(self, a, b):
        self.parts = (a, b)
        self.arrays, self.out_shape, self.scratch = a.arrays + b.arrays, a.out_shape + b.out_shape, a.scratch + b.scratch

    def _split(self, ins, outs, sems):
        a, b = self.parts
        return ((a, ins[:len(a.arrays)], outs[:len(a.out_shape)], sems[:len(a.scratch)]),
                (b, ins[len(a.arrays):], outs[len(a.out_shape):], sems[len(a.scratch):]))

    def start(self, ins, outs, sems):
        for part, *refs in self._split(ins, outs, sems):
            part.start(*refs)

    def finish(self, ins, outs, sems):
        for part, *refs in self._split(ins, outs, sems):
            part.finish(*refs)


def _slot_sum(r, name):
    n, rows, width = r.shape
    tr = min(rows, 256)

    def body(r_ref, o_ref):
        acc = r_ref[0].astype(F32)
        for s in range(1, n):
            acc = acc + r_ref[s].astype(F32)
        o_ref[...] = acc

    return pl.pallas_call(
        body, name=name, grid=(rows // tr,),
        in_specs=[pl.BlockSpec((n, tr, width), lambda t: (0, t, 0))],
        out_specs=pl.BlockSpec((tr, width), lambda t: (t, 0)),
        out_shape=SDS((rows, width), F32),
        compiler_params=_cp(("parallel",)),
    )(r)


def _chip_sum(where, recv, own, name):
    n, rows, width = recv.shape
    tr = min(rows, 256)
    nt = rows // tr

    def body(j_ref, r_ref, own_ref, o_ref):
        acc = None
        for s in range(n):
            term = jnp.where(j_ref[0] == s, own_ref[...], r_ref[s]).astype(F32)
            acc = term if acc is None else acc + term
        o_ref[...] = acc

    return pl.pallas_call(
        body, name=name,
        grid_spec=pltpu.PrefetchScalarGridSpec(
            num_scalar_prefetch=1, grid=(nt,),
            in_specs=[pl.BlockSpec((n, tr, width), lambda t, j: (0, t, 0)),
                      pl.BlockSpec((None, tr, width), lambda t, j: (j[0], t, 0))],
            out_specs=pl.BlockSpec((tr, width), lambda t, j: (j[1] * nt + t, 0))),
        out_shape=SDS((2 * rows, width), F32),
        compiler_params=_cp(("parallel",)),
    )(where, recv, own)


def _chip_sum_rows(place, recv0, own0, recv1, own1, name):
    n, rows, width = recv0.shape
    tr = min(rows, 256)
    nt = rows // tr

    def body(p_ref, r0_ref, o0_ref, r1_ref, o1_ref, o_ref):
        first_row = p_ref[2] == 0
        own = jnp.where(first_row, o0_ref[...], o1_ref[...])
        acc = None
        for s in range(n):
            term = jnp.where(p_ref[0] == s, own, jnp.where(first_row, r0_ref[s], r1_ref[s])).astype(F32)
            acc = term if acc is None else acc + term
        o_ref[...] = acc

    recv = pl.BlockSpec((n, tr, width), lambda t, p: (0, t, 0))
    own = pl.BlockSpec((None, tr, width), lambda t, p: (p[3], t, 0))
    return pl.pallas_call(
        body, name=name,
        grid_spec=pltpu.PrefetchScalarGridSpec(
            num_scalar_prefetch=1, grid=(nt,), in_specs=[recv, own, recv, own],
            out_specs=pl.BlockSpec((tr, width), lambda t, p: (p[1] * nt + t, 0))),
        out_shape=SDS((2 * rows, width), F32),
        compiler_params=_cp(("parallel",)),
    )(place, recv0, own0, recv1, own1)


def _half_exchange(gw, go, gathered, late, row):
    def body(gw_in, go_in, ga_in, late_ref, gw_ref, go_ref, ga_ref, send, recv, late_send, late_recv, late_local):
        del gw_in, go_in, ga_in
        x, y, c, _ = _place()
        starts, waits = _all_gather_rows(late_ref, ga_ref, pl.ds(row, late.shape[0]), late_send, late_recv,
                                         late_local)
        for go_ in starts:
            go_()
        mine = [pl.ds(c * (r.shape[0] // 2), r.shape[0] // 2) for r in (gw_ref, go_ref)]
        cps = [_remote(r.at[rows], r.at[rows], send.at[k], recv.at[k], (x, y, 1 - c))
               for k, (r, rows) in enumerate(zip((gw_ref, go_ref), mine))]
        for cp in cps:
            cp.start()
        for k, r in enumerate((gw_ref, go_ref)):
            theirs = pl.ds((1 - c) * (r.shape[0] // 2), r.shape[0] // 2)
            _remote(r.at[theirs], r.at[theirs], send.at[k], recv.at[k], (x, y, c)).wait_recv()
        for cp in cps:
            cp.wait_send()
        for wait in waits:
            wait()

    return pl.pallas_call(
        body, name="half_exchange", in_specs=[ANY] * 4, out_specs=[ANY] * 3,
        out_shape=[SDS(gw.shape, F32), SDS(go.shape, F32), SDS(gathered.shape, F32)],
        input_output_aliases={0: 0, 1: 1, 2: 2},
        scratch_shapes=[pltpu.SemaphoreType.DMA((2,)), pltpu.SemaphoreType.DMA((2,)),
                        pltpu.SemaphoreType.DMA((7,)), pltpu.SemaphoreType.DMA((7,)), pltpu.SemaphoreType.DMA],
        compiler_params=pltpu.CompilerParams(has_side_effects=True),
    )(gw, go, gathered, late)


def _adamw(w, g, m, v, name):
    rows, width = w.shape
    tr = min(rows, 256)

    def body(w_ref, g_ref, m_ref, v_ref, d_ref, nm_ref, nv_ref):
        gv = g_ref[...]
        nm = ADAM_B1 * m_ref[...] + (1.0 - ADAM_B1) * gv
        nv = ADAM_B2 * v_ref[...] + (1.0 - ADAM_B2) * (gv * gv)
        m_hat = nm / (1.0 - ADAM_B1 ** ADAM_STEP)
        v_hat = nv / (1.0 - ADAM_B2 ** ADAM_STEP)
        d_ref[...] = -ADAM_LR * (m_hat / (jnp.sqrt(v_hat) + ADAM_EPS) + ADAM_WD * w_ref[...])
        nm_ref[...] = nm
        nv_ref[...] = nv

    t = pl.BlockSpec((tr, width), lambda i: (i, 0))
    return pl.pallas_call(
        body, name=name, grid=(rows // tr,), in_specs=[t] * 4, out_specs=[t] * 3,
        out_shape=[SDS(w.shape, F32)] * 3, compiler_params=_cp(("parallel",)),
    )(w, g, m, v)


def _rowwise(a):
    return jnp.transpose(a, (2, 0, 1)).reshape(SHARD * D // LANE, LANE)


def _columns(ref, base=0):
    return jnp.concatenate([ref[pl.ds(base + c, LANE, stride=8), :].T for c in range(D // LANE)], axis=0)


def _shard_bf16(chip, w_rows):
    def body(j_ref, w_ref, o_ref, prev_ref):
        t = pl.program_id(0)
        cur = _columns(w_ref)

        @pl.when(t == 0)
        def _():
            prev_ref[...] = jnp.zeros_like(prev_ref)

        lane = _iota((D, LANE), 1)
        for s in range(4):
            @pl.when(j_ref[0] == s)
            def _():
                off = SHIFT * s
                moved = cur if s == 0 else jnp.where(lane < off, pltpu.roll(prev_ref[...], off, 1),
                                                     pltpu.roll(cur, off, 1))
                col = t * LANE + lane - off
                o_ref[...] = jnp.where((col >= 0) & (col < SHARD), moved, 0.0).astype(BF16)
        prev_ref[...] = cur

    return pl.pallas_call(
        body, name="shard_bf16",
        grid_spec=pltpu.PrefetchScalarGridSpec(
            num_scalar_prefetch=1, grid=(TILES + 1,),
            in_specs=[pl.BlockSpec((D, LANE), lambda t, j: (t, 0))],
            out_specs=pl.BlockSpec((D, LANE), lambda t, j: (0, t)),
            scratch_shapes=[pltpu.VMEM((D, LANE), F32)]),
        out_shape=SDS((D, WIN), BF16), compiler_params=_cp(("arbitrary",)),
    )(chip, w_rows)


def _whole_w_in(windows):
    tr = 256
    n = windows.shape[0]

    def body(g_ref, o_ref):
        lane = _iota((tr, LANE), 1)
        for s in range(n):
            first = TILES * s
            head = g_ref[s, :, :LANE]
            if s:
                tail = g_ref[s - 1, :, TILES * LANE:]
                head = jnp.where(lane < SHIFT * s, tail.astype(F32), head.astype(F32)).astype(BF16)
            o_ref[:, first * LANE:(first + 1) * LANE] = head
            o_ref[:, (first + 1) * LANE:(first + TILES) * LANE] = g_ref[s, :, LANE:TILES * LANE]
        o_ref[:, n * TILES * LANE:(n * TILES + 1) * LANE] = g_ref[n - 1, :, TILES * LANE:]
        o_ref[:, (n * TILES + 1) * LANE:] = jnp.zeros((tr, DP - (n * TILES + 1) * LANE), BF16)

    return pl.pallas_call(
        body, name="whole_w_in", grid=(D // tr,),
        in_specs=[pl.BlockSpec((n, tr, WIN), lambda t: (0, t, 0))], out_specs=pl.BlockSpec((tr, DP), lambda t: (t, 0)),
        out_shape=SDS((D, DP), BF16), compiler_params=_cp(("parallel",)),
    )(windows)


def _own_buffer(a, name):
    tr = 512
    block = pl.BlockSpec((tr, a.shape[1]), lambda t: (t, 0))

    def body(a_ref, o_ref):
        o_ref[...] = a_ref[...]

    return pl.pallas_call(
        body, name=name, grid=(a.shape[0] // tr,), in_specs=[block], out_specs=block,
        out_shape=SDS(a.shape, a.dtype), compiler_params=_cp(("parallel",)),
    )(a)


def _shard_of_window(chip, g_win):
    tr = 128

    def body(j_ref, g_ref, grad_ref):
        for s in range(4):
            @pl.when(j_ref[0] == s)
            def _():
                back = LANE - SHIFT * s
                from_this = _iota((tr, LANE), 1) < back

                def moved(t):
                    tile = g_ref[:, t * LANE:(t + 1) * LANE]
                    return pltpu.roll(tile, back, 1) if s else tile

                for t in range(TILES):
                    grad_ref[:, t * LANE:(t + 1) * LANE] = jnp.where(from_this, moved(t), moved(t + 1)) if s else moved(t)
                grad_ref[:, TILES * LANE:] = moved(TILES)[:, :SHARD - TILES * LANE]

    return pl.pallas_call(
        body, name="shard_of_window",
        grid_spec=pltpu.PrefetchScalarGridSpec(
            num_scalar_prefetch=1, grid=(D // tr,), in_specs=[pl.BlockSpec((tr, WIN), lambda t, j: (t, 0))],
            out_specs=pl.BlockSpec((tr, SHARD), lambda t, j: (t, 0))),
        out_shape=SDS((D, SHARD), F32), compiler_params=_cp(("parallel",)),
    )(chip, g_win)


def _adamw_in(w_rows, g, m_rows, v_rows):
    per_step = 2

    def body(w_ref, g_ref, m_ref, v_ref, d_ref, nm_ref, nv_ref):
        for a in range(per_step):
            cols = slice(a * LANE, (a + 1) * LANE)
            gv = g_ref[:, cols]
            nm = ADAM_B1 * _columns(m_ref, a * D) + (1.0 - ADAM_B1) * gv
            nv = ADAM_B2 * _columns(v_ref, a * D) + (1.0 - ADAM_B2) * (gv * gv)
            m_hat = nm / (1.0 - ADAM_B1 ** ADAM_STEP)
            v_hat = nv / (1.0 - ADAM_B2 ** ADAM_STEP)
            d_ref[:, cols] = -ADAM_LR * (m_hat / (jnp.sqrt(v_hat) + ADAM_EPS) + ADAM_WD * _columns(w_ref, a * D))
            nm_ref[:, cols] = nm
            nv_ref[:, cols] = nv

    tile = pl.BlockSpec((D, per_step * LANE), lambda t: (0, t))
    rows = pl.BlockSpec((per_step * D, LANE), lambda t: (t, 0))
    return pl.pallas_call(
        body, name="adamw_in", grid=(pl.cdiv(TILES + 1, per_step),), in_specs=[rows, tile, rows, rows],
        out_specs=[tile] * 3, out_shape=[SDS(g.shape, F32)] * 3, compiler_params=_cp(("parallel",)),
    )(w_rows, g, m_rows, v_rows)


def _rows128(a, rows):
    flat = a.reshape(-1)
    return jnp.pad(flat, (0, rows * LANE - flat.shape[0])).reshape(rows, LANE)


CONV_ROWS = 48


def _pack_small(conv_w, norm_pre, conv_b, ssm_norm, norm_post, dtb, alog, dsk, extra=None):
    cw_rows = CONV_ROWS if conv_w.shape[-1] == 1536 else 16
    extra = jnp.zeros((1, LANE), F32) if extra is None else _rows128(extra, 1)
    vec = jnp.concatenate([_rows128(dtb, 1), _rows128(alog, 1), _rows128(dsk, 1), extra, jnp.zeros((4, LANE), F32)],
                          axis=0)
    return jnp.concatenate([_rows128(conv_w, cw_rows), _rows128(norm_pre, 8), _rows128(conv_b, 16),
                            _rows128(ssm_norm, 8), _rows128(norm_post, 8), vec], axis=0)


def _unpack_small(p, cw_cols):
    cw_rows = CONV_ROWS if cw_cols == 1536 else 16
    o = cw_rows
    conv_w = p[:cw_rows].reshape(-1)[:4 * cw_cols].reshape(1, 4, cw_cols)
    norm_pre = p[o:o + 8].reshape(1, D)
    conv_b = p[o + 8:o + 24].reshape(-1)[:1536].reshape(1, 1536)
    ssm_norm = p[o + 24:o + 32].reshape(1, D)
    norm_post = p[o + 32:o + 40].reshape(1, D)
    vec = p[o + 40:o + 48]
    return conv_w, norm_pre, conv_b, ssm_norm, norm_post, vec[0:1, :NH], vec[1:2, :NH], vec[2:3, :NH], vec[3, 0]


def _pad_lanes(a):
    return jnp.pad(a, ((0, 0), (0, LANE - a.shape[1])))


class _GradReduce:
    LO, HI = ("qk", "v", "gz"), ("gz", "x")

    def __init__(self, xi, yi, ci):
        self.cidx = jnp.reshape(ci, (1,)).astype(jnp.int32)
        self.place = jnp.stack([2 * xi + yi, ci, xi, yi]).astype(jnp.int32)

    def pairs(self, dw_gz, dw_x, dw_out):
        self.hi = [dw_gz, dw_x]
        self.go = dw_out.reshape(4, D // 2, D)
        return _PairExchange(self.HI, self.hi, (2, 3), [self.go])

    def first(self, got):
        rw, ro = got
        self.pw_hi = _pair_sum_windows(self.cidx, self.HI, self.hi, (2, 3), rw, "pair_sum_hi")
        self.po = _pair_sum(self.cidx, self.go, ro, "pair_sum_out")
        return _ChipExchange([self.pw_hi, self.po], [1, None])

    def first_done(self, got):
        self.rw_hi, self.ro = got

    def second_pairs(self, dw_qk, dw_gz):
        self.lo = [dw_qk, None, dw_gz]
        return _PairExchange(self.LO, self.lo, (0, 1))

    def second(self, dw_v, got, small):
        rest = _PairExchange(self.LO, [None, dw_v, None], (0, 1))
        (rw,) = _exchange_call(rest, "pair_exchange_v", into=got)
        lo = [dw_v if a is None else a for a in self.lo]
        self.pw_lo = _pair_sum_windows(self.cidx, self.LO, lo, (0, 1), rw, "pair_sum_lo")
        return _Both(_ChipExchange([self.pw_lo], [0]), _SmallExchange(small))

    def second_done(self, got):
        self.rw_lo, self.small = got

    def result(self, late, row):
        half_in = _chip_sum_rows(self.place, self.rw_lo, self.pw_lo, self.rw_hi, self.pw_hi, "chip_sum_in")
        half_out = _chip_sum(self.place[0:2], self.ro, self.po, "chip_sum_out")
        return _half_exchange(half_in, half_out, self.small, late, row)


def kernel(x, norm_pre_w, w_in, conv_w, conv_b, dt_bias, a_log, d_skip, ssm_norm_w, w_out, norm_post_w, loss_target, m_norm_pre_w, m_w_in, m_conv_w, m_conv_b, m_dt_bias, m_a_log, m_d_skip, m_ssm_norm_w, m_w_out, m_norm_post_w, v_norm_pre_w, v_w_in, v_conv_w, v_conv_b, v_dt_bias, v_a_log, v_d_skip, v_ssm_norm_w, v_w_out, v_norm_post_w):
    xi, yi, ci = lax.axis_index("x"), lax.axis_index("y"), lax.axis_index("c")
    chip = 2 * xi + yi
    x2, tgt = x[0], loss_target[0]

    chip_idx = jnp.reshape(chip, (1,)).astype(jnp.int32)
    w_rows = _rowwise(w_in)
    w_all = _whole_w_in(_gather_weights(_shard_bf16(chip_idx, w_rows)))
    reduce = _GradReduce(xi, yi, ci)
    grad_x, dnw_pre = _local_step(x2, tgt, w_all, _LateGather(w_out[0].astype(BF16), conv_w[0]), norm_pre_w, conv_b,
                                  dt_bias, a_log, d_skip, ssm_norm_w, norm_post_w, reduce)
    g_win, g_out, small = reduce.result(_rows128(dnw_pre, D // LANE), CONV_ROWS)
    g_small = _slot_sum(small, "small_sum")
    g_cw, g_npre, g_cb, g_nssm, g_npost, g_dtb, g_alog, g_dsk, loss = _unpack_small(g_small, 1536)
    g_cw = lax.dynamic_slice_in_dim(g_cw, chip * 384, 384, axis=2)

    g_in = _shard_of_window(chip_idx, g_win)
    d_in, nm_in, nv_in = _adamw_in(w_rows, g_in, _rowwise(m_w_in), _rowwise(v_w_in))
    grad_x = _own_buffer(grad_x, "grad_x_copy")
    d_out, nm_out, nv_out = _adamw(w_out[0], g_out, m_w_out[0], v_w_out[0], "adamw_out")
    packed = [_pack_small(*t) for t in (
        (conv_w, norm_pre_w, conv_b, ssm_norm_w, norm_post_w, dt_bias, a_log, d_skip),
        (g_cw, g_npre, g_cb, g_nssm, g_npost, g_dtb, g_alog, g_dsk),
        (m_conv_w, m_norm_pre_w, m_conv_b, m_ssm_norm_w, m_norm_post_w, m_dt_bias, m_a_log, m_d_skip),
        (v_conv_w, v_norm_pre_w, v_conv_b, v_ssm_norm_w, v_norm_post_w, v_dt_bias, v_a_log, v_d_skip))]
    small_out = [_unpack_small(p, 384)[:8] for p in _adamw(*packed, "adamw_small")]

    def ordered(cw_, npre, cb_, nssm, npost, dtb_, alog_, dsk_, big_in, big_out):
        return [npre, big_in[None], cw_, cb_, dtb_, alog_, dsk_, nssm, big_out[None], npost]

    grads = ordered(g_cw, g_npre, g_cb, g_nssm, g_npost, g_dtb, g_alog, g_dsk, g_in, g_out)
    deltas = ordered(*small_out[0], d_in, d_out)
    new_m = ordered(*small_out[1], nm_in, nm_out)
    new_v = ordered(*small_out[2], nv_in, nv_out)
    return (loss, grad_x[None], *grads, *deltas, *new_m, *new_v)


def _local_step(x2, tgt, w_all, late, norm_pre_w, conv_b, dt_bias, a_log, d_skip, ssm_norm_w,
                norm_post_w, reduce=None):
    dtb, alog = _pad_lanes(dt_bias), _pad_lanes(a_log)
    d_b = jnp.repeat(d_skip, 64, axis=1)

    if isinstance(late, _LateGather):
        (proj, u), (gout, gcw) = _inproj_fwd(x2, norm_pre_w, w_all, late)
        w_out_all = gout.reshape(2 * D, D)
        cw_all = jnp.concatenate([gcw[0], gcw[1], gcw[2], gcw[3]], axis=1)
    else:
        proj, u = _inproj_fwd(x2, norm_pre_w, w_all)
        w_out_all, cw_all = late
    mix, attn_pre, lse = _attn_fwd(proj, 1, _attn_fwd(proj, 4, _attn_fwd(proj, 16)), final=True)
    mix, y_save, states, conv_out = _ssm_fwd(proj, mix, cw_all, conv_b, dtb, alog, d_b, ssm_norm_w)

    dy, dn_ssm, do, delta, dg, dw_out, dnw_post, loss_part = _outproj_loss(mix, w_out_all, x2, tgt, norm_post_w,
                                                                          attn_pre, proj)
    dz, dxbcdt, dcw, dcb, dvec, dnw_ssm = _ssm_bwd(proj, dn_ssm, y_save, states, conv_out, cw_all, dtb, alog, d_b,
                                                   ssm_norm_w)
    dw_gz, dw_x = _dw_pair(u, dg, dz, "dw_in_gz"), _dw(u, dxbcdt, "dw_in_xbcdt", X_COLS)
    acc = _attn_bwd(proj, do, lse, delta, 16, None, F32, reduce.pairs(dw_gz, dw_x, dw_out) if reduce else None)
    if reduce:
        acc, got = acc
    acc = _attn_bwd(proj, do, lse, delta, 4, acc, F32, reduce.first(got) if reduce else None)
    if reduce:
        acc, got = acc
        reduce.first_done(got)
    dq, dk, dv = _attn_bwd(proj, do, lse, delta, 1, acc, BF16)
    dw_qk = _dw_pair(u, dq, dk, "dw_in_qk")
    dw_v = _dw(u, dv, "dw_in_v", hosted=reduce.second_pairs(dw_qk, dw_gz) if reduce else None)
    if reduce:
        dw_v, got = dw_v

    def small(dnw_pre):
        return _pack_small(dcw, dnw_pre, dcb, dnw_ssm, dnw_post, dvec[0:1, :NH], dvec[1:2, :NH], dvec[2:3, :NH],
                           loss_part[:, :1])

    res = _inproj_bwd_dx([dq, dk, dv, dg, dz], dxbcdt, w_all, x2, dy, norm_pre_w,
                         reduce.second(dw_v, got, small(jnp.zeros((1, D), F32))) if reduce else None)
    if reduce:
        res, got = res
        reduce.second_done(got)
        return res
    grad_x, dnw_pre = res
    dw_all = jnp.concatenate([dw_qk, dw_v, dw_gz, dw_x], axis=1)
    return grad_x, small(dnw_pre), dw_all, dw_out
```

```python
import functools

import jax
import jax.numpy as jnp
from jax import lax
from jax.experimental import pallas as pl
from jax.experimental.pallas import tpu as pltpu

F32 = jnp.float32
BF16 = jnp.bfloat16
MESH = pl.DeviceIdType.MESH
SDS = jax.ShapeDtypeStruct
ANY = pl.BlockSpec(memory_space=pl.ANY)

S = 4096
D = 1024
DP = 7168
SHARD = 1668
OFF_G, OFF_Z = 3072, 4096
NH = 16
CH = 128
NC = S // CH
EPS = 1e-6
NEG = -1e30
LANE = 128
VMEM_LIMIT = 48 * 1024 * 1024

TILES = SHARD // LANE
WIN = (TILES + 1) * LANE
SHIFT = SHARD - TILES * LANE
SECTION_TILES = {"qk": (0, 16), "v": (16, 8), "gz": (24, 16), "x": (40, 13)}
X_COLS = SECTION_TILES["x"][1] * LANE

ADAM_LR, ADAM_B1, ADAM_B2, ADAM_EPS, ADAM_WD, ADAM_STEP = 0.001, 0.9, 0.999, 1e-08, 0.01, 10


def _cp(sem, **kw):
    return pltpu.CompilerParams(dimension_semantics=sem, vmem_limit_bytes=VMEM_LIMIT, **kw)


def _dot(a, b):
    return jnp.dot(a, b, preferred_element_type=F32)


def _dot_nt(a, b):
    return lax.dot_general(a, b, (((1,), (1,)), ((), ())), preferred_element_type=F32)


def _dot_tn(a, b):
    return lax.dot_general(a, b, (((0,), (0,)), ((), ())), preferred_element_type=F32)


def _pieces(x, n):
    out = []
    for _ in range(n):
        p = x.astype(BF16)
        out.append(p)
        x = x - p.astype(F32)
    return out


def _pick(x, sel, n=2):
    parts = [_dot(p, sel) for p in _pieces(x, n)]
    return functools.reduce(jnp.add, parts)


def _pick_left(sel, x, n=3):
    parts = [_dot(sel, p) for p in _pieces(x, n)]
    return functools.reduce(jnp.add, parts)


def _sigmoid(v):
    return 0.5 * jnp.tanh(0.5 * v) + 0.5


def _iota(shape, dim):
    return lax.broadcasted_iota(jnp.int32, shape, dim)


def _inproj_fwd(x, nw, w_all, hosted=None):
    tm, tn = 1024, 1024
    n_host = len(hosted.arrays) if hosted else 0

    def body(x_ref, nw_ref, w_ref, *refs):
        host_in, (proj_ref, u_ref), refs = refs[:n_host], refs[n_host:n_host + 2], refs[n_host + 2:]
        host_out, host_sems = refs[:n_host], refs[n_host:]
        i, j = pl.program_id(0), pl.program_id(1)
        if hosted:
            pl.when((i == 0) & (j == 0))(lambda: hosted.start(host_in, host_out, host_sems))

        @pl.when(j == 0)
        def _():
            xf = x_ref[...]
            r = lax.rsqrt(jnp.mean(xf * xf, axis=-1, keepdims=True) + EPS)
            u_ref[...] = (xf * r * nw_ref[...]).astype(BF16)

        proj_ref[...] = _dot(u_ref[...], w_ref[:, pl.ds(pl.multiple_of(j * tn, tn), tn)])
        if hosted:
            pl.when((i == S // tm // 2) & (j == 0))(lambda: hosted.pass_on(host_in, host_out, host_sems))
            pl.when((i == S // tm - 1) & (j == DP // tn - 1))(lambda: hosted.finish(host_in, host_out, host_sems))

    outs = pl.pallas_call(
        body, name="inproj_fwd", grid=(S // tm, DP // tn),
        in_specs=[pl.BlockSpec((tm, D), lambda i, j: (i, 0)), pl.BlockSpec((1, D), lambda i, j: (0, 0)),
                  pl.BlockSpec((D, DP), lambda i, j: (0, 0), pipeline_mode=pl.Buffered(1))] + [ANY] * n_host,
        out_specs=[pl.BlockSpec((tm, tn), lambda i, j: (i, j)), pl.BlockSpec((tm, D), lambda i, j: (i, 0))]
        + [ANY] * n_host,
        out_shape=[SDS((S, DP), F32), SDS((S, D), BF16)] + (hosted.out_shape if hosted else []),
        scratch_shapes=hosted.scratch if hosted else [],
        compiler_params=_cp(("arbitrary", "arbitrary") if hosted else ("parallel", "arbitrary")),
    )(x, nw, w_all, *(hosted.arrays if hosted else []))
    return (outs[:2], outs[2:]) if hosted else outs


ATTN_QB = {1: 16, 4: 4, 16: 1}


def _unit_rows(r, u, d):
    return pl.ds(r + d * CH * u, CH, stride=d) if d > 1 else pl.ds(CH * u, CH)


def _for_units(d, qb, fn):
    for r in range(d):
        for u in range(qb):
            fn(r, u)


def _attn_mask(has_prev):
    qi, kj = _iota((2 * CH, 2 * CH), 0) & (CH - 1), _iota((2 * CH, 2 * CH), 1)
    cur_ok = (kj >= CH) & (kj - CH <= qi)
    prev_ok = (kj < CH) & (kj >= qi)
    return cur_ok | (prev_ok & has_prev)


def _stack_heads(v, lane_a):
    return jnp.concatenate([jnp.where(lane_a, v, 0.0), jnp.where(lane_a, 0.0, v)], axis=0).astype(BF16)


def _attn_specs(d, qb):
    rows, prows = CH * d * qb, CH * d
    nb = S // rows
    steps = (NH // 2) * nb

    def at(t):
        t = jnp.minimum(t, steps - 1)
        return t % nb, t // nb

    def cur(off):
        return pl.BlockSpec((rows, LANE), lambda t: (at(t)[0], off + at(t)[1]))

    def prev(off):
        return pl.BlockSpec((prows, LANE), lambda t: (jnp.maximum(at(t)[0] * qb - 1, 0), off + at(t)[1]))

    lag = pl.BlockSpec((rows, LANE), lambda t: at(jnp.maximum(t - 1, 0)))
    return nb, steps, cur, prev, lag


def _gather16(src_ref, dense_ref, tmp_ref):
    for a in range(4):
        tmp_ref[...] = src_ref[pl.ds(a, 4 * CH, stride=4), :]
        for b in range(4):
            dense_ref[a + 4 * b] = tmp_ref[pl.ds(b, CH, stride=4), :]


def _scatter16(dense_ref, dst_ref, tmp_ref):
    for a in range(4):
        for b in range(4):
            tmp_ref[pl.ds(b, CH, stride=4), :] = dense_ref[a + 4 * b]
        dst_ref[pl.ds(a, 4 * CH, stride=4), :] = tmp_ref[...]


def _unit_index(r, u, d):
    return (r,) if d == 16 else (_unit_rows(r, u, d), slice(None))


def _unit_kv(p_ref, c_ref, r, u, d):
    prev = p_ref[_unit_index(r, 0, d)] if u == 0 else c_ref[_unit_index(r, u - 1, d)]
    return jnp.concatenate([prev, c_ref[_unit_index(r, u, d)]], axis=0).astype(BF16)


def _dense_scratch(d, n):
    return [pltpu.VMEM((16, CH, LANE), F32)] * n + [pltpu.VMEM((4 * CH, LANE), F32)] if d == 16 else []


def _attn_fwd(proj, d, prior=None, final=False):
    qb = ATTN_QB[d]
    nb, steps, cur, prev, _ = _attn_specs(d, qb)
    n_prior = 2 if prior is not None else 0
    n_in, n_out = 5 + n_prior + final, 2 + final
    assert not (d == 16 and (n_prior or final))

    def body(*refs):
        ins, outs, scratch = refs[:n_in], refs[n_in:n_in + n_out], refs[n_in + n_out:]
        if d == 16:
            tmp_ref = scratch[-1]
            for src, dense in zip(ins, scratch):
                _gather16(src, dense, tmp_ref)
            block_outs, ins, outs = outs, scratch[:n_in], scratch[n_in:n_in + n_out]
        q_ref, kp_ref, kc_ref, vp_ref, vc_ref = ins[:5]
        prior_refs = ins[5:5 + n_prior]
        if final:
            g_ref, (mix_ref, o_ref, l_ref) = ins[-1], outs
        else:
            o_ref, l_ref = outs
        i = pl.program_id(0) % nb
        lane_a = _iota((CH, LANE), 1) < 64
        mask_first, mask_rest = _attn_mask(i > 0), _attn_mask(True)

        def unit(r, u):
            at = _unit_index(r, u, d)
            q2 = _stack_heads(q_ref[at] * 0.125, lane_a)
            k2, v2 = _unit_kv(kp_ref, kc_ref, r, u, d), _unit_kv(vp_ref, vc_ref, r, u, d)
            s = jnp.where(mask_first if u == 0 else mask_rest, _dot_nt(q2, k2), NEG)
            m = jnp.max(s, axis=1, keepdims=True)
            p = jnp.exp(s - m)
            l = jnp.sum(p, axis=1, keepdims=True)
            o2 = _dot(p.astype(BF16), v2) / l
            lse2 = m + jnp.log(l)
            o = jnp.where(lane_a, o2[:CH], o2[CH:])
            lse = jnp.where(lane_a, lse2[:CH], lse2[CH:])
            if n_prior:
                o_a, l_a = prior_refs[0][at], prior_refs[1][at]
                top = jnp.maximum(l_a, lse)
                e_a, e_b = jnp.exp(l_a - top), jnp.exp(lse - top)
                tot = e_a + e_b
                o = (e_a * o_a + e_b * o) / tot
                lse = top + jnp.log(tot)
            o_ref[at] = o
            l_ref[at] = lse
            if final:
                g = g_ref[at]
                mix_ref[at] = (o * (g * _sigmoid(g))).astype(BF16)

        _for_units(d, qb, unit)
        if d == 16:
            for dense, dst in zip(outs, block_outs):
                _scatter16(dense, dst, tmp_ref)

    in_specs = [cur(0), prev(8), cur(8), prev(16), cur(16)] + [cur(0)] * n_prior
    args = [proj] * 5 + (list(prior) if n_prior else [])
    out_specs, out_shape = [cur(0), cur(0)], [SDS((S, D), F32), SDS((S, D), F32)]
    if final:
        assert d == 1
        in_specs.append(cur(OFF_G // LANE))
        args.append(proj)
        out_specs, out_shape = [cur(0)] + out_specs, [SDS((S, 2 * D), BF16)] + out_shape
    return pl.pallas_call(
        body, name=f"attn_fwd_d{d}", grid=(steps,),
        in_specs=in_specs, out_specs=out_specs, out_shape=out_shape,
        scratch_shapes=_dense_scratch(d, n_in + n_out),
        compiler_params=_cp(("parallel",)),
    )(*args)


def _attn_bwd(proj, do, lse, delta, d, acc, out_dtype, hosted=None):
    qb = ATTN_QB[d]
    nb, steps, cur, prev, lag = _attn_specs(d, qb)
    has_acc = acc is not None
    n_in = 11 if has_acc else 8
    n_host, n_host_out = (len(hosted.arrays), len(hosted.out_shape)) if hosted else (0, 0)
    assert not (d == 16 and (has_acc or out_dtype != F32))
    rows = CH * d * qb
    carry = (2, 16, CH, LANE) if d == 16 else (2, rows, LANE)

    def body(*refs):
        ins, host_in, refs = refs[:n_in], refs[n_in:n_in + n_host], refs[n_in + n_host:]
        (dq_ref, dk_ref, dv_ref), host_out, scratch = refs[:3], refs[3:3 + n_host_out], refs[3 + n_host_out:]
        if hosted:
            scratch, host_sems = scratch[:-len(hosted.scratch)], scratch[-len(hosted.scratch):]
        ck_ref, cv_ref = scratch[:2]
        dq_f32 = dq_ref if out_dtype == F32 else scratch[2]
        t = pl.program_id(0)
        i = t % nb
        if hosted:
            pl.when(t == 0)(lambda: hosted.start(host_in, host_out, host_sems))
        if d == 16:
            dense, dq_f32, tmp_ref = scratch[2:2 + n_in], scratch[2 + n_in], scratch[-1]

            @pl.when(t < steps)
            def _():
                for src, dst in zip(ins, dense):
                    _gather16(src, dst, tmp_ref)

            ins = dense
        q_ref, kp_ref, kc_ref, vp_ref, vc_ref, do_ref, lse_ref, dl_ref = ins[:8]
        if has_acc:
            aq_ref, ak_ref, av_ref = ins[8:11]
        slot = t & 1
        now_k, now_v, old_k, old_v = ck_ref.at[slot], cv_ref.at[slot], ck_ref.at[1 - slot], cv_ref.at[1 - slot]
        lane_a = _iota((CH, LANE), 1) < 64
        mask_first, mask_rest = _attn_mask(i > 0), _attn_mask(True)

        @pl.when(t == 0)
        def _():
            ck_ref[1] = jnp.zeros(carry[1:], F32)
            cv_ref[1] = jnp.zeros(carry[1:], F32)

        def unit(r, u):
            at = _unit_index(r, u, d)
            q2 = _stack_heads(q_ref[at] * 0.125, lane_a)
            do2 = _stack_heads(do_ref[at], lane_a)
            k2, v2 = _unit_kv(kp_ref, kc_ref, r, u, d), _unit_kv(vp_ref, vc_ref, r, u, d)
            lsev, dlv = lse_ref[at], dl_ref[at]
            lse2 = jnp.concatenate([lsev[:, 0:1], lsev[:, 64:65]], axis=0)
            dl2 = jnp.concatenate([dlv[:, 0:1], dlv[:, 64:65]], axis=0)
            p = jnp.exp(jnp.where(mask_first if u == 0 else mask_rest, _dot_nt(q2, k2), NEG) - lse2)
            ds = (p * (_dot_nt(do2, v2) - dl2)).astype(BF16)
            dq2 = _dot(ds, k2)
            dk2 = _dot_tn(ds, q2)
            dv2 = _dot_tn(p.astype(BF16), do2)
            dq = jnp.where(lane_a, dq2[:CH], dq2[CH:]) * 0.125
            if has_acc:
                dq = dq + aq_ref[at]
            dq_f32[at] = dq
            if u == 0:
                before = _unit_index(r, qb - 1, d)
                old_k[before] += dk2[:CH]
                old_v[before] += dv2[:CH]
            else:
                before = _unit_index(r, u - 1, d)
                now_k[before] += dk2[:CH]
                now_v[before] += dv2[:CH]
            now_k[at] = dk2[CH:]
            now_v[at] = dv2[CH:]

        @pl.when(t < steps)
        def _():
            _for_units(d, qb, unit)
            if d == 16:
                _scatter16(dq_f32, dq_ref, tmp_ref)
            elif out_dtype != F32:
                dq_ref[...] = dq_f32[...].astype(out_dtype)

        if d == 16:
            _scatter16(old_k, dk_ref, tmp_ref)
            _scatter16(old_v, dv_ref, tmp_ref)
        else:
            dk, dv = old_k[...], old_v[...]
            if has_acc:
                dk, dv = dk + ak_ref[...], dv + av_ref[...]
            dk_ref[...] = dk.astype(out_dtype)
            dv_ref[...] = dv.astype(out_dtype)
        if hosted:
            pl.when(t == steps)(lambda: hosted.finish(host_in, host_out, host_sems))

    in_specs = [cur(0), prev(8), cur(8), prev(16), cur(16), cur(0), cur(0), cur(0)]
    args = [proj, proj, proj, proj, proj, do, lse, delta]
    if has_acc:
        in_specs += [cur(0), lag, lag]
        args += list(acc)
    scratch = [pltpu.VMEM(carry, F32), pltpu.VMEM(carry, F32)]
    if d == 16:
        scratch += _dense_scratch(d, n_in + 1)
    elif out_dtype != F32:
        scratch.append(pltpu.VMEM((rows, LANE), F32))
    out_specs, out_shape = [cur(0), lag, lag], [SDS((S, D), out_dtype)] * 3
    if hosted:
        args += hosted.arrays
        in_specs += [ANY] * n_host
        out_specs += [ANY] * n_host_out
        out_shape += hosted.out_shape
        scratch += hosted.scratch
    outs = pl.pallas_call(
        body, name=f"attn_bwd_d{d}", grid=(steps + 1,),
        in_specs=in_specs, out_specs=out_specs, out_shape=out_shape,
        scratch_shapes=scratch, compiler_params=_cp(("arbitrary",)),
    )(*args)
    return (outs[:3], outs[3:]) if hosted else outs


def _conv_taps(cur, prev8, first):
    row8 = _iota(prev8.shape, 0)
    prev8 = jnp.where(first, 0.0, prev8)
    taps = []
    for s in (3, 2, 1):
        rolled = pltpu.roll(cur, s, 0)
        head = jnp.where(row8 < s, pltpu.roll(prev8, s, 0), rolled[:8])
        taps.append(jnp.concatenate([head, rolled[8:]], axis=0))
    return taps + [cur]


def _conv(taps, w, b):
    acc = b + w[0:1, :] * taps[0]
    for k in (1, 2, 3):
        acc = acc + w[k:k + 1, :] * taps[k]
    return acc


def _expand():
    return (_iota((LANE, D), 1) // 64 == _iota((LANE, D), 0)).astype(BF16)


def _reduce():
    return (_iota((D, LANE), 0) // 64 == _iota((D, LANE), 1)).astype(BF16)


def _ssd_common(xs_c, bc_c, dt_raw, dtb, alog):
    head_lane = _iota((CH, LANE), 1) < NH
    xs = xs_c * _sigmoid(xs_c)
    bc = bc_c * _sigmoid(bc_c)
    pre = dt_raw + dtb
    dt = jnp.where(head_lane, jnp.maximum(pre, 0.0) + jnp.log(1.0 + jnp.exp(-jnp.abs(pre))), 0.0)
    a_row = jnp.where(head_lane[0:1], -jnp.exp(alog), 0.0)
    tri = (_iota((CH, CH), 1) <= _iota((CH, CH), 0)).astype(BF16)
    cs = _pick_left(tri, dt * a_row)
    cs_last = cs[CH - 1:CH, :]
    wide = _pick(jnp.concatenate([dt, jnp.exp(cs), jnp.exp(cs_last - cs)], axis=0), _expand())
    dt_b, e_b, f_b = wide[:CH], wide[CH:2 * CH], wide[2 * CH:]
    return dict(xs=xs, bc=bc, pre=pre, dt=dt, a_row=a_row, cs=cs, cs_t=cs.T, dt_b=dt_b, e_b=e_b, f_b=f_b,
                t_b=e_b[CH - 1:CH, :])


def _groups(bc):
    bcb = bc.astype(BF16)
    return [bcb[:, 0:128], bcb[:, 128:256]], [bcb[:, 256:384], bcb[:, 384:512]]


def _decay(q, h, tril):
    seg = q["cs"][:, h:h + 1] - q["cs_t"][h:h + 1, :]
    return jnp.exp(jnp.where(tril, seg, NEG))


def _ssm_fwd(proj, mix, cw, cb, dtb, alog, d_b, nw):
    def body(xs_ref, xsp_ref, bc_ref, bcp_ref, dt_ref, z_ref, cw_ref, cb_ref, dtb_ref, alog_ref, db_ref, nw_ref,
             mix_in_ref, mix_ref, y_ref, st_ref, conv_ref, h_ref):
        del mix_in_ref
        i = pl.program_id(0)

        @pl.when(i == 0)
        def _():
            h_ref[...] = jnp.zeros_like(h_ref)

        cw, cb = cw_ref[...], cb_ref[...]
        xs_c = _conv(_conv_taps(xs_ref[...], xsp_ref[...], i == 0), cw[:, :D], cb[:, :D])
        bc_c = _conv(_conv_taps(bc_ref[...], bcp_ref[...], i == 0), cw[:, D:], cb[:, D:])
        conv_ref[:, :D] = xs_c
        conv_ref[:, D:] = bc_c
        q = _ssd_common(xs_c, bc_c, dt_ref[...], dtb_ref[...], alog_ref[...])
        bg, cg = _groups(q["bc"])
        xs = q["xs"]
        xdt = xs * q["dt_b"]
        xdt_b = xdt.astype(BF16)
        h_in = h_ref[...]
        st_ref[...] = h_in
        hb = h_in.astype(BF16)
        tril = _iota((CH, CH), 1) <= _iota((CH, CH), 0)
        lane_a = _iota((CH, LANE), 1) < 64
        cbm = [_dot_nt(cg[g], bg[g]) for g in range(2)]
        pairs = []
        for hp in range(NH // 2):
            xp = xdt_b[:, hp * LANE:(hp + 1) * LANE]
            ya = _dot((cbm[hp // 4] * _decay(q, 2 * hp, tril)).astype(BF16), xp)
            yb = _dot((cbm[hp // 4] * _decay(q, 2 * hp + 1, tril)).astype(BF16), xp)
            pairs.append(jnp.where(lane_a, ya, yb))
        y_diag = jnp.concatenate(pairs, axis=1)
        y_off = jnp.concatenate([_dot(cg[g], hb[:, g * 512:(g + 1) * 512]) for g in range(2)], axis=1) * q["e_b"]
        y = y_diag + y_off + db_ref[...] * xs
        y_ref[...] = y
        xf = (xdt * q["f_b"]).astype(BF16)
        h_ref[...] = q["t_b"] * h_in + jnp.concatenate(
            [_dot_tn(bg[g], xf[:, g * 512:(g + 1) * 512]) for g in range(2)], axis=1)
        z = z_ref[...]
        yz = y * (z * _sigmoid(z))
        outs = []
        for g in range(2):
            v = yz[:, g * 512:(g + 1) * 512]
            outs.append(v * lax.rsqrt(jnp.mean(v * v, axis=-1, keepdims=True) + EPS))
        mix_ref[...] = (jnp.concatenate(outs, axis=1) * nw_ref[...]).astype(BF16)

    def col(width, blk, prev=False):
        if prev:
            return pl.BlockSpec((8, width), lambda i: (jnp.maximum(i * (CH // 8) - 1, 0), blk))
        return pl.BlockSpec((CH, width), lambda i: (i, blk))

    def full(a):
        return pl.BlockSpec(a.shape, lambda i: (0,) * a.ndim)

    return pl.pallas_call(
        body, name="ssm_fwd", grid=(NC,),
        in_specs=[col(D, 5), col(D, 5, True), col(512, 12), col(512, 12, True), col(LANE, 52), col(D, 4),
                  full(cw), full(cb), full(dtb), full(alog), full(d_b), full(nw), ANY],
        out_specs=[col(D, 1), col(D, 0), pl.BlockSpec((None, CH, D), lambda i: (i, 0, 0)), col(D + 512, 0)],
        out_shape=[SDS((S, 2 * D), BF16), SDS((S, D), F32), SDS((NC, CH, D), F32), SDS((S, D + 512), F32)],
        scratch_shapes=[pltpu.VMEM((CH, D), F32)],
        input_output_aliases={12: 0},
        compiler_params=_cp(("arbitrary",)),
    )(proj, proj, proj, proj, proj, proj, cw, cb, dtb, alog, d_b, nw, mix)


def _ssm_bwd(proj, dn, y_save, states, conv_out, cw, dtb, alog, d_b, nw):
    def body(xs_ref, bc_ref, dt_ref, z_ref, dn_ref, y_ref, st_ref, conv_ref,
             cw_ref, dtb_ref, alog_ref, db_ref, nw_ref,
             dz_ref, dx_ref, dcw_ref, dcb_ref, dsm_ref, dnw_ref, dh_ref, nxs_ref, nbc_ref):
        i = pl.program_id(0)
        ci = NC - 1 - i

        @pl.when(i == 0)
        def _():
            for ref in (dcw_ref, dcb_ref, dsm_ref, dnw_ref, dh_ref, nxs_ref, nbc_ref):
                ref[...] = jnp.zeros_like(ref)

        cw = cw_ref[...]
        xs_c, bc_c = conv_ref[:, :D], conv_ref[:, D:]
        q = _ssd_common(xs_c, bc_c, dt_ref[...], dtb_ref[...], alog_ref[...])
        bg, cg = _groups(q["bc"])
        xs, dt_b, e_b, f_b, t_b = q["xs"], q["dt_b"], q["e_b"], q["f_b"], q["t_b"]
        xdt = xs * dt_b
        xdt_b = xdt.astype(BF16)
        h_in = st_ref[...]
        hb = h_in.astype(BF16)
        dh_new = dh_ref[...]
        dhb = dh_new.astype(BF16)
        red = _reduce()

        z, y, dn, nw_v = z_ref[...], y_ref[...], dn_ref[...], nw_ref[...]
        sig = _sigmoid(z)
        sz = z * sig
        yz = y * sz
        gdn = dn * nw_v
        dyz, dnw = [], []
        for g in range(2):
            v, gv = yz[:, g * 512:(g + 1) * 512], gdn[:, g * 512:(g + 1) * 512]
            r = lax.rsqrt(jnp.mean(v * v, axis=-1, keepdims=True) + EPS)
            dnw.append(dn[:, g * 512:(g + 1) * 512] * v * r)
            dyz.append(r * (gv - v * (r * r) * jnp.mean(gv * v, axis=-1, keepdims=True)))
        dyz = jnp.concatenate(dyz, axis=1)
        dnw_ref[...] += jnp.sum(jnp.concatenate(dnw, axis=1), axis=0, keepdims=True)
        dy = dyz * sz
        dz_ref[...] = (dyz * y * (sig * (1.0 + z * (1.0 - sig)))).astype(BF16)
        dy_b = dy.astype(BF16)

        tril = _iota((CH, CH), 1) <= _iota((CH, CH), 0)
        lane_a = _iota((CH, LANE), 1) < 64
        cbm = [_dot_nt(cg[g], bg[g]) for g in range(2)]
        dcbm = [jnp.zeros((CH, CH), F32), jnp.zeros((CH, CH), F32)]
        seg_rows = jnp.zeros((CH, LANE), F32)
        seg_cols = jnp.zeros((LANE, CH), F32)
        row_id, col_id = _iota((CH, LANE), 0), _iota((CH, LANE), 1)
        dx_pairs = []
        for hp in range(NH // 2):
            g = hp // 4
            xp = xdt_b[:, hp * LANE:(hp + 1) * LANE]
            dyp_f = dy[:, hp * LANE:(hp + 1) * LANE]
            dyp = dy_b[:, hp * LANE:(hp + 1) * LANE]
            halves = []
            for k in range(2):
                h = 2 * hp + k
                lane = lane_a if k == 0 else jnp.logical_not(lane_a)
                dec = _decay(q, h, tril)
                gm = cbm[g] * dec
                dgm = _dot_nt(jnp.where(lane, dyp_f, 0.0).astype(BF16), xp)
                dcbm[g] = dcbm[g] + dgm * dec
                prod = dgm * gm
                seg_rows = jnp.where(col_id == h, jnp.sum(prod, axis=1, keepdims=True), seg_rows)
                seg_cols = jnp.where(row_id == h, jnp.sum(prod, axis=0, keepdims=True), seg_cols)
                halves.append(_dot_tn(gm.astype(BF16), dyp))
            dx_pairs.append(jnp.where(lane_a, halves[0], halves[1]))
        dxdt_diag = jnp.concatenate(dx_pairs, axis=1)

        qv = jnp.concatenate([_dot(bg[g], dhb[:, g * 512:(g + 1) * 512]) for g in range(2)], axis=1)
        y_off = jnp.concatenate([_dot(cg[g], hb[:, g * 512:(g + 1) * 512]) for g in range(2)], axis=1) * e_b
        xfq = xdt * f_b * qv
        dxdt = dxdt_diag + f_b * qv
        tdt = jnp.sum(dh_new * h_in, axis=0, keepdims=True) * t_b
        per_head = _pick(jnp.concatenate([xfq, dy * y_off, dxdt * xs, dy * xs, jnp.broadcast_to(tdt, (8, D))],
                                         axis=0), red)
        fdf, dyoff_h, dxdtxs_h, dyxs_h = [per_head[k * CH:(k + 1) * CH] for k in range(4)]
        dcs = seg_rows - seg_cols.T + dyoff_h - fdf
        last = per_head[4 * CH:4 * CH + 1] + jnp.sum(fdf, axis=0, keepdims=True)
        dcs = dcs + jnp.where(_iota((CH, LANE), 0) == CH - 1, last, 0.0)
        tri_t = (_iota((CH, CH), 1) >= _iota((CH, CH), 0)).astype(BF16)
        da = _pick_left(tri_t, dcs)
        ddt = da * q["a_row"] + dxdtxs_h
        dxs = dxdt * dt_b + db_ref[...] * dy
        ddt_raw = ddt * _sigmoid(q["pre"])
        dsm_ref[0:1, :] += jnp.sum(ddt_raw, axis=0, keepdims=True)
        dsm_ref[1:2, :] += jnp.sum(da * q["dt"], axis=0, keepdims=True) * q["a_row"]
        dsm_ref[2:3, :] += jnp.sum(dyxs_h, axis=0, keepdims=True)
        edy = (e_b * dy).astype(BF16)
        xf = (xdt * f_b).astype(BF16)
        dbs, dcs_g, dhs = [], [], []
        for g in range(2):
            sl = slice(g * 512, (g + 1) * 512)
            dcb_b = dcbm[g].astype(BF16)
            dcs_g.append(_dot(dcb_b, bg[g]) + _dot_nt(edy[:, sl], hb[:, sl]))
            dbs.append(_dot_tn(dcb_b, cg[g]) + _dot_nt(xf[:, sl], dhb[:, sl]))
            dhs.append(_dot_tn(cg[g], edy[:, sl]))
        dh_ref[...] = t_b * dh_new + jnp.concatenate(dhs, axis=1)
        dbc = jnp.concatenate(dbs + dcs_g, axis=1)

        def conv_bwd(dact, pre, x_raw, w, nxt_ref, lo):
            s = _sigmoid(pre)
            dconv = dact * (s * (1.0 + pre * (1.0 - s)))
            nxt8 = nxt_ref[...]
            row8 = _iota(nxt8.shape, 0)
            hi = lo + dconv.shape[1]
            dcb_ref[:, lo:hi] += jnp.sum(dconv, axis=0, keepdims=True)
            later = [dconv]
            for s_ in (1, 2, 3):
                rolled = pltpu.roll(dconv, CH - s_, 0)
                tail = jnp.where(row8 >= 8 - s_, pltpu.roll(nxt8, 8 - s_, 0), rolled[CH - 8:])
                later.append(jnp.concatenate([rolled[:CH - 8], tail], axis=0))
            dx = None
            for s_, up in enumerate(later):
                k = 3 - s_
                dcw_ref[k:k + 1, lo:hi] += jnp.sum(up * x_raw, axis=0, keepdims=True)
                dx = w[k:k + 1, :] * up if dx is None else dx + w[k:k + 1, :] * up
            nxt_ref[...] = dconv[:8]
            return dx

        dx_ref[:, 0:D] = conv_bwd(dxs, xs_c, xs_ref[...], cw[:, :D], nxs_ref, 0).astype(BF16)
        dx_ref[:, D:D + 512] = conv_bwd(dbc, bc_c, bc_ref[...], cw[:, D:], nbc_ref, D).astype(BF16)
        dx_ref[:, D + 512:D + 640] = ddt_raw.astype(BF16)
        dx_ref[:, D + 640:] = jnp.zeros((CH, D - 640), BF16)

    def col(width, blk):
        return pl.BlockSpec((CH, width), lambda i: (NC - 1 - i, blk))

    def full(a):
        return pl.BlockSpec(a.shape, lambda i: (0,) * len(a.shape))

    acc_shapes = [SDS((4, 1536), F32), SDS((1, 1536), F32), SDS((8, LANE), F32), SDS((1, D), F32)]
    return pl.pallas_call(
        body, name="ssm_bwd", grid=(NC,),
        in_specs=[col(D, 5), col(512, 12), col(LANE, 52), col(D, 4),
                  col(D, 0), col(D, 0), pl.BlockSpec((None, CH, D), lambda i: (NC - 1 - i, 0, 0)), col(D + 512, 0),
                  full(cw), full(dtb), full(alog), full(d_b), full(nw)],
        out_specs=[col(D, 0), col(2 * D, 0)] + [full(a) for a in acc_shapes],
        out_shape=[SDS((S, D), BF16), SDS((S, 2 * D), BF16)] + acc_shapes,
        scratch_shapes=[pltpu.VMEM((CH, D), F32), pltpu.VMEM((8, D), F32), pltpu.VMEM((8, 512), F32)],
        compiler_params=_cp(("arbitrary",)),
    )(proj, proj, proj, proj, dn, y_save, states, conv_out, cw, dtb, alog, d_b, nw)


def _outproj_loss(mix, w_out, x, tgt, nw, attn_pre, proj):
    tm = 256

    def body(mix_ref, w_ref, x_ref, t_ref, nw_ref, pre_ref, g_ref,
             dy_ref, dn_ref, do_ref, delta_ref, dg_ref, dw_ref, dnw_ref, loss_ref):
        @pl.when(pl.program_id(0) == 0)
        def _():
            dw_ref[...] = jnp.zeros_like(dw_ref)
            dnw_ref[...] = jnp.zeros_like(dnw_ref)
            loss_ref[...] = jnp.zeros_like(loss_ref)

        mixv, w = mix_ref[...], w_ref[...]
        out = _dot(mixv, w)
        r = lax.rsqrt(jnp.mean(out * out, axis=-1, keepdims=True) + EPS)
        nh = out * r
        nw_v = nw_ref[...]
        err = x_ref[...] + nh * nw_v - t_ref[...]
        loss_ref[...] += 0.5 * jnp.sum(jnp.mean(err * err, axis=-1, keepdims=True), axis=0, keepdims=True)
        dy = err * (1.0 / D)
        dy_ref[...] = dy
        dnw_ref[...] += jnp.sum(dy * nh, axis=0, keepdims=True)
        gdn = dy * nw_v
        dout = (r * (gdn - nh * jnp.mean(gdn * nh, axis=-1, keepdims=True))).astype(BF16)
        dmix = _dot_nt(dout, w)
        dw_ref[...] += _dot_tn(mixv, dout)
        dn_ref[...] = dmix[:, D:]
        dm, g, pre_v = dmix[:, :D], g_ref[...], pre_ref[...]
        sig = _sigmoid(g)
        do = dm * (g * sig)
        do_ref[...] = do
        dg_ref[...] = (dm * pre_v * (sig * (1.0 + g * (1.0 - sig)))).astype(BF16)
        prod = do * pre_v
        same_head = (_iota((LANE, LANE), 0) // 64 == _iota((LANE, LANE), 1) // 64).astype(BF16)
        for cb in range(D // LANE):
            delta_ref[:, cb * LANE:(cb + 1) * LANE] = _pick(prod[:, cb * LANE:(cb + 1) * LANE], same_head)

    row = lambda w: pl.BlockSpec((tm, w), lambda i: (i, 0))
    full = lambda s: pl.BlockSpec(s, lambda i: (0, 0))
    return pl.pallas_call(
        body, name="outproj_loss", grid=(S // tm,),
        in_specs=[row(2 * D), full((2 * D, D)), row(D), row(D), full((1, D)), row(D),
                  pl.BlockSpec((tm, D), lambda i: (i, OFF_G // D))],
        out_specs=[row(D), row(D), row(D), row(D), row(D), full((2 * D, D)), full((1, D)), full((1, LANE))],
        out_shape=[SDS((S, D), F32)] * 4 + [SDS((S, D), BF16), SDS((2 * D, D), F32), SDS((1, D), F32),
                                            SDS((1, LANE), F32)],
        compiler_params=_cp(("arbitrary",)),
    )(mix, w_out, x, tgt, nw, attn_pre, proj)


def _inproj_bwd_dx(srcs, dxbcdt, w_all, x, dy, nw, hosted=None):
    tm = 512
    nk = DP // D
    n_host, n_host_out = (len(hosted.arrays), len(hosted.out_shape)) if hosted else (0, 0)

    def body(*refs):
        src_refs = refs[:nk]
        w_ref, x_ref, dy_ref, nw_ref = refs[nk:nk + 4]
        host_in, refs = refs[nk + 4:nk + 4 + n_host], refs[nk + 4 + n_host:]
        gx_ref, dnw_ref = refs[:2]
        host_out, host_sems = refs[2:2 + n_host_out], refs[2 + n_host_out:]
        i = pl.program_id(0)

        @pl.when(i == 0)
        def _():
            if hosted:
                hosted.start(host_in, host_out, host_sems)
            dnw_ref[...] = jnp.zeros_like(dnw_ref)

        du = None
        for k, ref in enumerate(src_refs):
            width = min(D, 5 * D + X_COLS - k * D)
            part = _dot_nt(ref[:, :width], w_ref[:, k * D:k * D + width])
            du = part if du is None else du + part
        xf, nw_v = x_ref[...], nw_ref[...]
        r = lax.rsqrt(jnp.mean(xf * xf, axis=-1, keepdims=True) + EPS)
        xh = xf * r
        dnw_ref[...] += jnp.sum(du * xh, axis=0, keepdims=True)
        gdu = du * nw_v
        gx_ref[...] = r * (gdu - xh * jnp.mean(gdu * xh, axis=-1, keepdims=True)) + dy_ref[...]

        if hosted:
            pl.when(i == S // tm - 1)(lambda: hosted.finish(host_in, host_out, host_sems))

    row = pl.BlockSpec((tm, D), lambda i: (i, 0))
    row1 = pl.BlockSpec((tm, D), lambda i: (i, 1))
    one = pl.BlockSpec((1, D), lambda i: (0, 0))
    whole_w = pl.BlockSpec((D, DP), lambda i: (0, 0), pipeline_mode=pl.Buffered(1))
    args = [*srcs, dxbcdt, dxbcdt, w_all, x, dy, nw]
    in_specs = [row] * len(srcs) + [row, row1, whole_w, row, row, one]
    out_specs, out_shape, scratch = [row, one], [SDS((S, D), F32), SDS((1, D), F32)], []
    if hosted:
        args += hosted.arrays
        in_specs += [ANY] * n_host
        out_specs += [ANY] * n_host_out
        out_shape += hosted.out_shape
        scratch += hosted.scratch
    outs = pl.pallas_call(
        body, name="inproj_bwd_dx", grid=(S // tm,),
        in_specs=in_specs, out_specs=out_specs, out_shape=out_shape, scratch_shapes=scratch,
        compiler_params=_cp(("arbitrary",)),
    )(*args)
    return (outs[:2], outs[2:]) if hosted else outs


def _dw(u, dsec, name, width=D, hosted=None):
    ts = 1024
    n_host, n_host_out = (len(hosted.arrays), len(hosted.out_shape)) if hosted else (0, 0)

    def body(u_ref, d_ref, *refs):
        host_in, o_ref, refs = refs[:n_host], refs[n_host], refs[n_host + 1:]
        host_out, host_sems = refs[:n_host_out], refs[n_host_out:]
        i = pl.program_id(0)

        @pl.when(i == 0)
        def _():
            if hosted:
                hosted.start(host_in, host_out, host_sems)
            o_ref[...] = jnp.zeros_like(o_ref)

        o_ref[...] += _dot_tn(u_ref[...], d_ref[...])
        if hosted:
            pl.when(i == S // ts - 1)(lambda: hosted.finish(host_in, host_out, host_sems))

    outs = pl.pallas_call(
        body, name=name, grid=(S // ts,),
        in_specs=[pl.BlockSpec((ts, D), lambda i: (i, 0)), pl.BlockSpec((ts, width), lambda i: (i, 0))]
        + [ANY] * n_host,
        out_specs=[pl.BlockSpec((D, width), lambda i: (0, 0))] + [ANY] * n_host_out,
        out_shape=[SDS((D, width), F32)] + (hosted.out_shape if hosted else []),
        scratch_shapes=hosted.scratch if hosted else [],
        compiler_params=_cp(("arbitrary",)),
    )(u, dsec, *(hosted.arrays if hosted else []))
    return (outs[0], outs[1:]) if hosted else outs[0]


def _dw_pair(u, da, db, name):
    ts = 1024
    last = S // ts - 1

    def body(u_ref, a_ref, b_ref, o_ref):
        j = pl.program_id(0)

        @pl.when(pl.program_id(1) == 0)
        def _():
            o_ref[...] = jnp.zeros_like(o_ref)

        for k, d_ref in enumerate((a_ref, b_ref)):
            @pl.when(j == k)
            def _():
                o_ref[...] += _dot_tn(u_ref[...], d_ref[...])

    return pl.pallas_call(
        body, name=name, grid=(2, S // ts),
        in_specs=[pl.BlockSpec((ts, D), lambda j, i: (i, 0)),
                  pl.BlockSpec((ts, D), lambda j, i: (jnp.where(j == 0, i, last), 0)),
                  pl.BlockSpec((ts, D), lambda j, i: (jnp.where(j == 1, i, 0), 0))],
        out_specs=pl.BlockSpec((D, D), lambda j, i: (0, j)),
        out_shape=SDS((D, 2 * D), F32),
        compiler_params=_cp(("arbitrary", "arbitrary")),
    )(u, da, db)


def _place():
    x, y, c = lax.axis_index("x"), lax.axis_index("y"), lax.axis_index("c")
    return x, y, c, 2 * x + y


def _chip_of(x, y, k):
    px = 1 - x if k & 2 else x
    py = 1 - y if k & 1 else y
    return px, py, 2 * px + py


def _remote(src, dst, send_sem, recv_sem, dev):
    return pltpu.make_async_remote_copy(src_ref=src, dst_ref=dst, send_sem=send_sem, recv_sem=recv_sem,
                                        device_id=dev, device_id_type=MESH)


def _gather_weights(w_in_b):
    half = w_in_b.shape[0] // 2
    quarter = half // 2

    def body(src, dst, send, recv):
        x, y, c, j = _place()
        me, sib = (x, y, c), (x, y, 1 - c)
        nbr = {"x": _chip_of(x, y, 2), "y": _chip_of(x, y, 1)}
        diag = _chip_of(x, y, 3)[2]
        started, arrivals = [], []

        def rows(n_quarter=None, sibling=False):
            base = (1 - c if sibling else c) * half
            return pl.ds(base, half) if n_quarter is None else pl.ds(base + n_quarter * quarter, quarter)

        def sem(n):
            return send.at[n], recv.at[n]

        def go(cp):
            cp.start()
            started.append(cp)

        own = _remote(src, dst.at[j], *sem(8), sib)
        go(own)
        for n, axis in enumerate("xy"):
            px, py, _ = nbr[axis]
            go(_remote(src.at[rows()], dst.at[j, rows()], *sem(n), (px, py, c)))
        for n, axis in enumerate("xy"):
            ox, oy, _ = nbr["y" if axis == "x" else "x"]
            pj = nbr[axis][2]
            _remote(src.at[rows()], dst.at[pj, rows()], *sem(n), me).wait_recv()
            go(_remote(dst.at[pj, rows(n)], dst.at[pj, rows(n)], *sem(2 + n), (ox, oy, c)))
            go(_remote(dst.at[pj, rows()], dst.at[pj, rows()], *sem(4 + n), sib))
            arrivals.append(_remote(src.at[rows()], dst.at[pj, rows(None, True)], *sem(4 + n), me))
        for n in range(2):
            _remote(dst.at[diag, rows(n)], dst.at[diag, rows(n)], *sem(2 + n), me).wait_recv()
            go(_remote(dst.at[diag, rows(n)], dst.at[diag, rows(n)], *sem(6 + n), sib))
            arrivals.append(_remote(dst.at[diag, rows(n, True)], dst.at[diag, rows(n, True)], *sem(6 + n), me))
        for cp in arrivals + [own]:
            cp.wait_recv()
        for cp in started:
            cp.wait_send()

    return pl.pallas_call(
        body, name="gather_weights", in_specs=[ANY], out_specs=ANY,
        out_shape=SDS((4,) + w_in_b.shape, BF16),
        scratch_shapes=[pltpu.SemaphoreType.DMA((9,)), pltpu.SemaphoreType.DMA((9,))],
        compiler_params=pltpu.CompilerParams(has_side_effects=True),
    )(w_in_b)


class _LateGather:
    def __init__(self, w_out_b, conv_w):
        self.arrays = [w_out_b, conv_w]
        self.out_shape = [SDS((4,) + w_out_b.shape, BF16), SDS((4,) + conv_w.shape, F32)]
        self.scratch = [pltpu.SemaphoreType.DMA((11,)), pltpu.SemaphoreType.DMA((11,))]

    def _plan(self, ins, outs, sems):
        x, y, c, j = _place()
        send, recv = sems
        (wo, cw), (gwo, gcw) = ins, outs
        half = wo.shape[0] // 2
        mine, theirs = pl.ds(c * half, half), pl.ds((1 - c) * half, half)
        me, sib = (x, y, c), (x, y, 1 - c)
        first, arrive, forward, last = [], [], [], []
        for k in (1, 2, 3):
            px, py, pj = _chip_of(x, y, k)
            first += [_remote(wo.at[mine], gwo.at[j, mine], send.at[k - 1], recv.at[k - 1], (px, py, c)),
                      _remote(cw, gcw.at[j], send.at[k + 2], recv.at[k + 2], (px, py, c))]
            arrive.append(_remote(wo.at[mine], gwo.at[pj, mine], send.at[k - 1], recv.at[k - 1], me))
            forward.append(_remote(gwo.at[pj, mine], gwo.at[pj, mine], send.at[k + 5], recv.at[k + 5], sib))
            last += [_remote(cw, gcw.at[pj], send.at[k + 2], recv.at[k + 2], me),
                     _remote(wo.at[theirs], gwo.at[pj, theirs], send.at[k + 5], recv.at[k + 5], me)]
        first += [_remote(wo, gwo.at[j], send.at[9], recv.at[9], sib),
                  _remote(cw, gcw.at[j], send.at[10], recv.at[10], sib)]
        last += first[-2:]
        return first, arrive, forward, last

    def start(self, ins, outs, sems):
        for cp in self._plan(ins, outs, sems)[0]:
            cp.start()

    def pass_on(self, ins, outs, sems):
        _, arrive, forward, _ = self._plan(ins, outs, sems)
        for got, fwd in zip(arrive, forward):
            got.wait_recv()
            fwd.start()

    def finish(self, ins, outs, sems):
        first, _, forward, last = self._plan(ins, outs, sems)
        for cp in last:
            cp.wait_recv()
        for cp in first + forward:
            cp.wait_send()


def _window(s, names):
    lo, hi = TILES * s, TILES * s + TILES + 1
    pieces = []
    for n, name in enumerate(names):
        a, count = SECTION_TILES[name]
        first, last = max(lo, a), min(hi, a + count)
        if first < last:
            pieces.append((n, first - a, last - first, first - lo))
    assert sum(p[2] for p in pieces) == TILES + 1
    return pieces


class _PairExchange:
    def __init__(self, names, sections, shards, more=()):
        self.names, self.shards = names, shards
        self.there = [n for n, a in enumerate(sections) if a is not None]
        self.arrays = [sections[n] for n in self.there] + list(more)
        self.out_shape = [SDS((len(shards), D // 2, WIN), F32)]
        self.out_shape += [SDS((a.shape[0], a.shape[1] // 2, a.shape[2]), F32) for a in more]
        n = sum(p[0] in self.there for s in shards for p in _window(s, names)) + len(more)
        self.scratch = [pltpu.SemaphoreType.DMA((n,)) for _ in range(2)]

    def _copies(self, ins, outs, sems):
        x, y, c, _ = _place()
        sib = (x, y, 1 - c)
        rows = pl.ds((1 - c) * (D // 2), D // 2)
        k = 0
        for i, s in enumerate(self.shards):
            for n, tile, tiles, at in _window(s, self.names):
                if n in self.there:
                    yield _remote(ins[self.there.index(n)].at[rows, pl.ds(tile * LANE, tiles * LANE)],
                                  outs[0].at[i, :, pl.ds(at * LANE, tiles * LANE)], sems[0].at[k], sems[1].at[k], sib)
                    k += 1
        for src, dst in zip(ins[len(self.there):], outs[1:]):
            half = src.shape[1] // 2
            yield _remote(src.at[:, pl.ds((1 - c) * half, half)], dst, sems[0].at[k], sems[1].at[k], sib)
            k += 1

    def start(self, ins, outs, sems):
        for cp in self._copies(ins, outs, sems):
            cp.start()

    def finish(self, ins, outs, sems):
        for cp in self._copies(ins, outs, sems):
            cp.wait()


def _exchange_call(exchange, name, into=None):
    n, n_out = len(exchange.arrays), len(exchange.out_shape)
    given = list(into) if into else []

    def body(*refs):
        ins, outs, sems = refs[:n], refs[n + len(given):n + len(given) + n_out], refs[n + len(given) + n_out:]
        exchange.start(ins, outs, sems)
        exchange.finish(ins, outs, sems)

    return pl.pallas_call(
        body, name=name, in_specs=[ANY] * (n + len(given)), out_specs=[ANY] * n_out, out_shape=exchange.out_shape,
        input_output_aliases={n + k: k for k in range(len(given))},
        scratch_shapes=exchange.scratch, compiler_params=pltpu.CompilerParams(has_side_effects=True),
    )(*exchange.arrays, *given)


def _pair_sum_windows(cidx, names, sections, shards, r, name):
    n, half, _ = r.shape
    tr = min(half, 256)
    nt = half // tr

    def body(c_ref, *refs):
        del c_ref
        secs, r_ref, o_ref = refs[:-2], refs[-2], refs[-1]
        for i, s in enumerate(shards):
            for k, tile, tiles, at in _window(s, names):
                own = secs[k][:, tile * LANE:(tile + tiles) * LANE]
                there = slice(at * LANE, (at + tiles) * LANE)
                o_ref[i, :, there] = (own + r_ref[i, :, there]).astype(BF16)

    window = pl.BlockSpec((n, tr, WIN), lambda t, c: (0, t, 0))
    return pl.pallas_call(
        body, name=name,
        grid_spec=pltpu.PrefetchScalarGridSpec(
            num_scalar_prefetch=1, grid=(nt,),
            in_specs=[pl.BlockSpec((tr, a.shape[1]), lambda t, c: (c[0] * nt + t, 0)) for a in sections] + [window],
            out_specs=window),
        out_shape=SDS(r.shape, BF16),
        compiler_params=_cp(("parallel",)),
    )(cidx, *sections, r)


def _pair_sum(cidx, g, r, name):
    n, half, width = r.shape
    tr = min(half, 256)
    nt = half // tr

    def body(c_ref, g_ref, r_ref, o_ref):
        del c_ref
        o_ref[...] = (g_ref[...] + r_ref[...]).astype(BF16)

    return pl.pallas_call(
        body, name=name,
        grid_spec=pltpu.PrefetchScalarGridSpec(
            num_scalar_prefetch=1, grid=(n, nt),
            in_specs=[pl.BlockSpec((None, tr, width), lambda s, t, c: (s, c[0] * nt + t, 0)),
                      pl.BlockSpec((None, tr, width), lambda s, t, c: (s, t, 0))],
            out_specs=pl.BlockSpec((None, tr, width), lambda s, t, c: (s, t, 0))),
        out_shape=SDS(r.shape, BF16),
        compiler_params=_cp(("parallel", "parallel")),
    )(cidx, g, r)


class _ChipExchange:
    def __init__(self, arrays, rows):
        self.arrays, self.rows = list(arrays), list(rows)
        self.out_shape = [SDS((4,) + a.shape[1:], BF16) for a in self.arrays]
        self.scratch = [pltpu.SemaphoreType.DMA((3 * len(self.arrays),)) for _ in range(2)]

    def _copies(self, ins, outs, sems):
        x, y, c, j = _place()
        send, recv = sems
        for a, (src, dst, row) in enumerate(zip(ins, outs, self.rows)):
            for k in (1, 2, 3):
                px, py, pj = _chip_of(x, y, k)
                n = 3 * a + k - 1
                slot = pj if row is None else py
                yield (None if row is None else px == row, None if row is None else x == row,
                       _remote(src.at[slot], dst.at[j], send.at[n], recv.at[n], (px, py, c)),
                       _remote(src.at[0], dst.at[pj], send.at[n], recv.at[n], (x, y, c)))

    def start(self, ins, outs, sems):
        for sends, _, send, _ in self._copies(ins, outs, sems):
            if sends is None:
                send.start()
            else:
                pl.when(sends)(send.start)

    def finish(self, ins, outs, sems):
        for sends, owns, send, arrival in self._copies(ins, outs, sems):
            if sends is None:
                arrival.wait_recv()
                send.wait_send()
            else:
                pl.when(owns)(arrival.wait_recv)
                pl.when(sends)(send.wait_send)


def _all_gather_rows(src, dst, rows, send, recv, local_sem):
    x, y, c, j = _place()
    me = 2 * j + c
    local = pltpu.make_async_copy(src, dst.at[me, rows], local_sem)
    cps, arrivals = [], []
    for k in range(1, 8):
        px, py, pj = _chip_of(x, y, k >> 1)
        pc = 1 - c if k & 1 else c
        cps.append(_remote(src, dst.at[me, rows], send.at[k - 1], recv.at[k - 1], (px, py, pc)))
        arrivals.append(_remote(src, dst.at[2 * pj + pc, rows], send.at[k - 1], recv.at[k - 1], (x, y, c)))
    starts = [local.start] + [cp.start for cp in cps]
    waits = [cp.wait_recv for cp in arrivals] + [cp.wait_send for cp in cps] + [local.wait]
    return starts, waits


class _SmallExchange:
    def __init__(self, small):
        self.arrays = [small]
        self.out_shape = [SDS((8,) + small.shape, F32)]
        self.scratch = [pltpu.SemaphoreType.DMA((7,)), pltpu.SemaphoreType.DMA((7,)), pltpu.SemaphoreType.DMA]

    def start(self, ins, outs, sems):
        for go in _all_gather_rows(ins[0], outs[0], slice(None), *sems)[0]:
            go()

    def finish(self, ins, outs, sems):
        for wait in _all_gather_rows(ins[0], outs[0], slice(None), *sems)[1]:
            wait()


class _Both:
    def __init__(self, a, b):
        self.parts = (a, b)
        self.arrays, self.out_shape, self.scratch = a.arrays + b.arrays, a.out_shape + b.out_shape, a.scratch + b.scratch

    def _split(self, ins, outs, sems):
        a, b = self.parts
        return ((a, ins[:len(a.arrays)], outs[:len(a.out_shape)], sems[:len(a.scratch)]),
                (b, ins[len(a.arrays):], outs[len(a.out_shape):], sems[len(a.scratch):]))

    def start(self, ins, outs, sems):
        for part, *refs in self._split(ins, outs, sems):
            part.start(*refs)

    def finish(self, ins, outs, sems):
        for part, *refs in self._split(ins, outs, sems):
            part.finish(*refs)


def _slot_sum(r, name):
    n, rows, width = r.shape
    tr = min(rows, 256)

    def body(r_ref, o_ref):
        acc = r_ref[0].astype(F32)
        for s in range(1, n):
            acc = acc + r_ref[s].astype(F32)
        o_ref[...] = acc

    return pl.pallas_call(
        body, name=name, grid=(rows // tr,),
        in_specs=[pl.BlockSpec((n, tr, width), lambda t: (0, t, 0))],
        out_specs=pl.BlockSpec((tr, width), lambda t: (t, 0)),
        out_shape=SDS((rows, width), F32),
        compiler_params=_cp(("parallel",)),
    )(r)


def _chip_sum(where, recv, own, name):
    n, rows, width = recv.shape
    tr = min(rows, 256)
    nt = rows // tr

    def body(j_ref, r_ref, own_ref, o_ref):
        acc = None
        for s in range(n):
            term = jnp.where(j_ref[0] == s, own_ref[...], r_ref[s]).astype(F32)
            acc = term if acc is None else acc + term
        o_ref[...] = acc

    return pl.pallas_call(
        body, name=name,
        grid_spec=pltpu.PrefetchScalarGridSpec(
            num_scalar_prefetch=1, grid=(nt,),
            in_specs=[pl.BlockSpec((n, tr, width), lambda t, j: (0, t, 0)),
                      pl.BlockSpec((None, tr, width), lambda t, j: (j[0], t, 0))],
            out_specs=pl.BlockSpec((tr, width), lambda t, j: (j[1] * nt + t, 0))),
        out_shape=SDS((2 * rows, width), F32),
        compiler_params=_cp(("parallel",)),
    )(where, recv, own)


def _chip_sum_rows(place, recv0, own0, recv1, own1, name):
    n, rows, width = recv0.shape
    tr = min(rows, 256)
    nt = rows // tr

    def body(p_ref, r0_ref, o0_ref, r1_ref, o1_ref, o_ref):
        first_row = p_ref[2] == 0
        own = jnp.where(first_row, o0_ref[...], o1_ref[...])
        acc = None
        for s in range(n):
            term = jnp.where(p_ref[0] == s, own, jnp.where(first_row, r0_ref[s], r1_ref[s])).astype(F32)
            acc = term if acc is None else acc + term
        o_ref[...] = acc

    recv = pl.BlockSpec((n, tr, width), lambda t, p: (0, t, 0))
    own = pl.BlockSpec((None, tr, width), lambda t, p: (p[3], t, 0))
    return pl.pallas_call(
        body, name=name,
        grid_spec=pltpu.PrefetchScalarGridSpec(
            num_scalar_prefetch=1, grid=(nt,), in_specs=[recv, own, recv, own],
            out_specs=pl.BlockSpec((tr, width), lambda t, p: (p[1] * nt + t, 0))),
        out_shape=SDS((2 * rows, width), F32),
        compiler_params=_cp(("parallel",)),
    )(place, recv0, own0, recv1, own1)


def _half_exchange(gw, go, gathered, late, row):
    def body(gw_in, go_in, ga_in, late_ref, gw_ref, go_ref, ga_ref, send, recv, late_send, late_recv, late_local):
        del gw_in, go_in, ga_in
        x, y, c, _ = _place()
        starts, waits = _all_gather_rows(late_ref, ga_ref, pl.ds(row, late.shape[0]), late_send, late_recv,
                                         late_local)
        for go_ in starts:
            go_()
        mine = [pl.ds(c * (r.shape[0] // 2), r.shape[0] // 2) for r in (gw_ref, go_ref)]
        cps = [_remote(r.at[rows], r.at[rows], send.at[k], recv.at[k], (x, y, 1 - c))
               for k, (r, rows) in enumerate(zip((gw_ref, go_ref), mine))]
        for cp in cps:
            cp.start()
        for k, r in enumerate((gw_ref, go_ref)):
            theirs = pl.ds((1 - c) * (r.shape[0] // 2), r.shape[0] // 2)
            _remote(r.at[theirs], r.at[theirs], send.at[k], recv.at[k], (x, y, c)).wait_recv()
        for cp in cps:
            cp.wait_send()
        for wait in waits:
            wait()

    return pl.pallas_call(
        body, name="half_exchange", in_specs=[ANY] * 4, out_specs=[ANY] * 3,
        out_shape=[SDS(gw.shape, F32), SDS(go.shape, F32), SDS(gathered.shape, F32)],
        input_output_aliases={0: 0, 1: 1, 2: 2},
        scratch_shapes=[pltpu.SemaphoreType.DMA((2,)), pltpu.SemaphoreType.DMA((2,)),
                        pltpu.SemaphoreType.DMA((7,)), pltpu.SemaphoreType.DMA((7,)), pltpu.SemaphoreType.DMA],
        compiler_params=pltpu.CompilerParams(has_side_effects=True),
    )(gw, go, gathered, late)


def _adamw(w, g, m, v, name):
    rows, width = w.shape
    tr = min(rows, 256)

    def body(w_ref, g_ref, m_ref, v_ref, d_ref, nm_ref, nv_ref):
        gv = g_ref[...]
        nm = ADAM_B1 * m_ref[...] + (1.0 - ADAM_B1) * gv
        nv = ADAM_B2 * v_ref[...] + (1.0 - ADAM_B2) * (gv * gv)
        m_hat = nm / (1.0 - ADAM_B1 ** ADAM_STEP)
        v_hat = nv / (1.0 - ADAM_B2 ** ADAM_STEP)
        d_ref[...] = -ADAM_LR * (m_hat / (jnp.sqrt(v_hat) + ADAM_EPS) + ADAM_WD * w_ref[...])
        nm_ref[...] = nm
        nv_ref[...] = nv

    t = pl.BlockSpec((tr, width), lambda i: (i, 0))
    return pl.pallas_call(
        body, name=name, grid=(rows // tr,), in_specs=[t] * 4, out_specs=[t] * 3,
        out_shape=[SDS(w.shape, F32)] * 3, compiler_params=_cp(("parallel",)),
    )(w, g, m, v)


def _rowwise(a):
    return jnp.transpose(a, (2, 0, 1)).reshape(SHARD * D // LANE, LANE)


def _columns(ref, base=0):
    return jnp.concatenate([ref[pl.ds(base + c, LANE, stride=8), :].T for c in range(D // LANE)], axis=0)


def _shard_bf16(chip, w_rows):
    def body(j_ref, w_ref, o_ref, prev_ref):
        t = pl.program_id(0)
        cur = _columns(w_ref)

        @pl.when(t == 0)
        def _():
            prev_ref[...] = jnp.zeros_like(prev_ref)

        lane = _iota((D, LANE), 1)
        for s in range(4):
            @pl.when(j_ref[0] == s)
            def _():
                off = SHIFT * s
                moved = cur if s == 0 else jnp.where(lane < off, pltpu.roll(prev_ref[...], off, 1),
                                                     pltpu.roll(cur, off, 1))
                col = t * LANE + lane - off
                o_ref[...] = jnp.where((col >= 0) & (col < SHARD), moved, 0.0).astype(BF16)
        prev_ref[...] = cur

    return pl.pallas_call(
        body, name="shard_bf16",
        grid_spec=pltpu.PrefetchScalarGridSpec(
            num_scalar_prefetch=1, grid=(TILES + 1,),
            in_specs=[pl.BlockSpec((D, LANE), lambda t, j: (t, 0))],
            out_specs=pl.BlockSpec((D, LANE), lambda t, j: (0, t)),
            scratch_shapes=[pltpu.VMEM((D, LANE), F32)]),
        out_shape=SDS((D, WIN), BF16), compiler_params=_cp(("arbitrary",)),
    )(chip, w_rows)


def _whole_w_in(windows):
    tr = 256
    n = windows.shape[0]

    def body(g_ref, o_ref):
        lane = _iota((tr, LANE), 1)
        for s in range(n):
            first = TILES * s
            head = g_ref[s, :, :LANE]
            if s:
                tail = g_ref[s - 1, :, TILES * LANE:]
                head = jnp.where(lane < SHIFT * s, tail.astype(F32), head.astype(F32)).astype(BF16)
            o_ref[:, first * LANE:(first + 1) * LANE] = head
            o_ref[:, (first + 1) * LANE:(first + TILES) * LANE] = g_ref[s, :, LANE:TILES * LANE]
        o_ref[:, n * TILES * LANE:(n * TILES + 1) * LANE] = g_ref[n - 1, :, TILES * LANE:]
        o_ref[:, (n * TILES + 1) * LANE:] = jnp.zeros((tr, DP - (n * TILES + 1) * LANE), BF16)

    return pl.pallas_call(
        body, name="whole_w_in", grid=(D // tr,),
        in_specs=[pl.BlockSpec((n, tr, WIN), lambda t: (0, t, 0))], out_specs=pl.BlockSpec((tr, DP), lambda t: (t, 0)),
        out_shape=SDS((D, DP), BF16), compiler_params=_cp(("parallel",)),
    )(windows)


def _own_buffer(a, name):
    tr = 512
    block = pl.BlockSpec((tr, a.shape[1]), lambda t: (t, 0))

    def body(a_ref, o_ref):
        o_ref[...] = a_ref[...]

    return pl.pallas_call(
        body, name=name, grid=(a.shape[0] // tr,), in_specs=[block], out_specs=block,
        out_shape=SDS(a.shape, a.dtype), compiler_params=_cp(("parallel",)),
    )(a)


def _shard_of_window(chip, g_win):
    tr = 128

    def body(j_ref, g_ref, grad_ref):
        for s in range(4):
            @pl.when(j_ref[0] == s)
            def _():
                back = LANE - SHIFT * s
                from_this = _iota((tr, LANE), 1) < back

                def moved(t):
                    tile = g_ref[:, t * LANE:(t + 1) * LANE]
                    return pltpu.roll(tile, back, 1) if s else tile

                for t in range(TILES):
                    grad_ref[:, t * LANE:(t + 1) * LANE] = jnp.where(from_this, moved(t), moved(t + 1)) if s else moved(t)
                grad_ref[:, TILES * LANE:] = moved(TILES)[:, :SHARD - TILES * LANE]

    return pl.pallas_call(
        body, name="shard_of_window",
        grid_spec=pltpu.PrefetchScalarGridSpec(
            num_scalar_prefetch=1, grid=(D // tr,), in_specs=[pl.BlockSpec((tr, WIN), lambda t, j: (t, 0))],
            out_specs=pl.BlockSpec((tr, SHARD), lambda t, j: (t, 0))),
        out_shape=SDS((D, SHARD), F32), compiler_params=_cp(("parallel",)),
    )(chip, g_win)


def _adamw_in(w_rows, g, m_rows, v_rows):
    per_step = 2

    def body(w_ref, g_ref, m_ref, v_ref, d_ref, nm_ref, nv_ref):
        for a in range(per_step):
            cols = slice(a * LANE, (a + 1) * LANE)
            gv = g_ref[:, cols]
            nm = ADAM_B1 * _columns(m_ref, a * D) + (1.0 - ADAM_B1) * gv
            nv = ADAM_B2 * _columns(v_ref, a * D) + (1.0 - ADAM_B2) * (gv * gv)
            m_hat = nm / (1.0 - ADAM_B1 ** ADAM_STEP)
            v_hat = nv / (1.0 - ADAM_B2 ** ADAM_STEP)
            d_ref[:, cols] = -ADAM_LR * (m_hat / (jnp.sqrt(v_hat) + ADAM_EPS) + ADAM_WD * _columns(w_ref, a * D))
            nm_ref[:, cols] = nm
            nv_ref[:, cols] = nv

    tile = pl.BlockSpec((D, per_step * LANE), lambda t: (0, t))
    rows = pl.BlockSpec((per_step * D, LANE), lambda t: (t, 0))
    return pl.pallas_call(
        body, name="adamw_in", grid=(pl.cdiv(TILES + 1, per_step),), in_specs=[rows, tile, rows, rows],
        out_specs=[tile] * 3, out_shape=[SDS(g.shape, F32)] * 3, compiler_params=_cp(("parallel",)),
    )(w_rows, g, m_rows, v_rows)


def _rows128(a, rows):
    flat = a.reshape(-1)
    return jnp.pad(flat, (0, rows * LANE - flat.shape[0])).reshape(rows, LANE)


CONV_ROWS = 48


def _pack_small(conv_w, norm_pre, conv_b, ssm_norm, norm_post, dtb, alog, dsk, extra=None):
    cw_rows = CONV_ROWS if conv_w.shape[-1] == 1536 else 16
    extra = jnp.zeros((1, LANE), F32) if extra is None else _rows128(extra, 1)
    vec = jnp.concatenate([_rows128(dtb, 1), _rows128(alog, 1), _rows128(dsk, 1), extra, jnp.zeros((4, LANE), F32)],
                          axis=0)
    return jnp.concatenate([_rows128(conv_w, cw_rows), _rows128(norm_pre, 8), _rows128(conv_b, 16),
                            _rows128(ssm_norm, 8), _rows128(norm_post, 8), vec], axis=0)


def _unpack_small(p, cw_cols):
    cw_rows = CONV_ROWS if cw_cols == 1536 else 16
    o = cw_rows
    conv_w = p[:cw_rows].reshape(-1)[:4 * cw_cols].reshape(1, 4, cw_cols)
    norm_pre = p[o:o + 8].reshape(1, D)
    conv_b = p[o + 8:o + 24].reshape(-1)[:1536].reshape(1, 1536)
    ssm_norm = p[o + 24:o + 32].reshape(1, D)
    norm_post = p[o + 32:o + 40].reshape(1, D)
    vec = p[o + 40:o + 48]
    return conv_w, norm_pre, conv_b, ssm_norm, norm_post, vec[0:1, :NH], vec[1:2, :NH], vec[2:3, :NH], vec[3, 0]


def _pad_lanes(a):
    return jnp.pad(a, ((0, 0), (0, LANE - a.shape[1])))


class _GradReduce:
    LO, HI = ("qk", "v", "gz"), ("gz", "x")

    def __init__(self, xi, yi, ci):
        self.cidx = jnp.reshape(ci, (1,)).astype(jnp.int32)
        self.place = jnp.stack([2 * xi + yi, ci, xi, yi]).astype(jnp.int32)

    def pairs(self, dw_gz, dw_x, dw_out):
        self.hi = [dw_gz, dw_x]
        self.go = dw_out.reshape(4, D // 2, D)
        return _PairExchange(self.HI, self.hi, (2, 3), [self.go])

    def first(self, got):
        rw, ro = got
        self.pw_hi = _pair_sum_windows(self.cidx, self.HI, self.hi, (2, 3), rw, "pair_sum_hi")
        self.po = _pair_sum(self.cidx, self.go, ro, "pair_sum_out")
        return _ChipExchange([self.pw_hi, self.po], [1, None])

    def first_done(self, got):
        self.rw_hi, self.ro = got

    def second_pairs(self, dw_qk, dw_gz):
        self.lo = [dw_qk, None, dw_gz]
        return _PairExchange(self.LO, self.lo, (0, 1))

    def second(self, dw_v, got, small):
        rest = _PairExchange(self.LO, [None, dw_v, None], (0, 1))
        (rw,) = _exchange_call(rest, "pair_exchange_v", into=got)
        lo = [dw_v if a is None else a for a in self.lo]
        self.pw_lo = _pair_sum_windows(self.cidx, self.LO, lo, (0, 1), rw, "pair_sum_lo")
        return _Both(_ChipExchange([self.pw_lo], [0]), _SmallExchange(small))

    def second_done(self, got):
        self.rw_lo, self.small = got

    def result(self, late, row):
        half_in = _chip_sum_rows(self.place, self.rw_lo, self.pw_lo, self.rw_hi, self.pw_hi, "chip_sum_in")
        half_out = _chip_sum(self.place[0:2], self.ro, self.po, "chip_sum_out")
        return _half_exchange(half_in, half_out, self.small, late, row)


def kernel(x, norm_pre_w, w_in, conv_w, conv_b, dt_bias, a_log, d_skip, ssm_norm_w, w_out, norm_post_w, loss_target, m_norm_pre_w, m_w_in, m_conv_w, m_conv_b, m_dt_bias, m_a_log, m_d_skip, m_ssm_norm_w, m_w_out, m_norm_post_w, v_norm_pre_w, v_w_in, v_conv_w, v_conv_b, v_dt_bias, v_a_log, v_d_skip, v_ssm_norm_w, v_w_out, v_norm_post_w):
    xi, yi, ci = lax.axis_index("x"), lax.axis_index("y"), lax.axis_index("c")
    chip = 2 * xi + yi
    x2, tgt = x[0], loss_target[0]

    chip_idx = jnp.reshape(chip, (1,)).astype(jnp.int32)
    w_rows = _rowwise(w_in)
    w_all = _whole_w_in(_gather_weights(_shard_bf16(chip_idx, w_rows)))
    reduce = _GradReduce(xi, yi, ci)
    grad_x, dnw_pre = _local_step(x2, tgt, w_all, _LateGather(w_out[0].astype(BF16), conv_w[0]), norm_pre_w, conv_b,
                                  dt_bias, a_log, d_skip, ssm_norm_w, norm_post_w, reduce)
    g_win, g_out, small = reduce.result(_rows128(dnw_pre, D // LANE), CONV_ROWS)
    g_small = _slot_sum(small, "small_sum")
    g_cw, g_npre, g_cb, g_nssm, g_npost, g_dtb, g_alog, g_dsk, loss = _unpack_small(g_small, 1536)
    g_cw = lax.dynamic_slice_in_dim(g_cw, chip * 384, 384, axis=2)

    g_in = _shard_of_window(chip_idx, g_win)
    d_in, nm_in, nv_in = _adamw_in(w_rows, g_in, _rowwise(m_w_in), _rowwise(v_w_in))
    grad_x = _own_buffer(grad_x, "grad_x_copy")
    d_out, nm_out, nv_out = _adamw(w_out[0], g_out, m_w_out[0], v_w_out[0], "adamw_out")
    packed = [_pack_small(*t) for t in (
        (conv_w, norm_pre_w, conv_b, ssm_norm_w, norm_post_w, dt_bias, a_log, d_skip),
        (g_cw, g_npre, g_cb, g_nssm, g_npost, g_dtb, g_alog, g_dsk),
        (m_conv_w, m_norm_pre_w, m_conv_b, m_ssm_norm_w, m_norm_post_w, m_dt_bias, m_a_log, m_d_skip),
        (v_conv_w, v_norm_pre_w, v_conv_b, v_ssm_norm_w, v_norm_post_w, v_dt_bias, v_a_log, v_d_skip))]
    small_out = [_unpack_small(p, 384)[:8] for p in _adamw(*packed, "adamw_small")]

    def ordered(cw_, npre, cb_, nssm, npost, dtb_, alog_, dsk_, big_in, big_out):
        return [npre, big_in[None], cw_, cb_, dtb_, alog_, dsk_, nssm, big_out[None], npost]

    grads = ordered(g_cw, g_npre, g_cb, g_nssm, g_npost, g_dtb, g_alog, g_dsk, g_in, g_out)
    deltas = ordered(*small_out[0], d_in, d_out)
    new_m = ordered(*small_out[1], nm_in, nm_out)
    new_v = ordered(*small_out[2], nv_in, nv_out)
    return (loss, grad_x[None], *grads, *deltas, *new_m, *new_v)


def _local_step(x2, tgt, w_all, late, norm_pre_w, conv_b, dt_bias, a_log, d_skip, ssm_norm_w,
                norm_post_w, reduce=None):
    dtb, alog = _pad_lanes(dt_bias), _pad_lanes(a_log)
    d_b = jnp.repeat(d_skip, 64, axis=1)

    if isinstance(late, _LateGather):
        (proj, u), (gout, gcw) = _inproj_fwd(x2, norm_pre_w, w_all, late)
        w_out_all = gout.reshape(2 * D, D)
        cw_all = jnp.concatenate([gcw[0], gcw[1], gcw[2], gcw[3]], axis=1)
    else:
        proj, u = _inproj_fwd(x2, norm_pre_w, w_all)
        w_out_all, cw_all = late
    mix, attn_pre, lse = _attn_fwd(proj, 1, _attn_fwd(proj, 4, _attn_fwd(proj, 16)), final=True)
    mix, y_save, states, conv_out = _ssm_fwd(proj, mix, cw_all, conv_b, dtb, alog, d_b, ssm_norm_w)

    dy, dn_ssm, do, delta, dg, dw_out, dnw_post, loss_part = _outproj_loss(mix, w_out_all, x2, tgt, norm_post_w,
                                                                          attn_pre, proj)
    dz, dxbcdt, dcw, dcb, dvec, dnw_ssm = _ssm_bwd(proj, dn_ssm, y_save, states, conv_out, cw_all, dtb, alog, d_b,
                                                   ssm_norm_w)
    dw_gz, dw_x = _dw_pair(u, dg, dz, "dw_in_gz"), _dw(u, dxbcdt, "dw_in_xbcdt", X_COLS)
    acc = _attn_bwd(proj, do, lse, delta, 16, None, F32, reduce.pairs(dw_gz, dw_x, dw_out) if reduce else None)
    if reduce:
        acc, got = acc
    acc = _attn_bwd(proj, do, lse, delta, 4, acc, F32, reduce.first(got) if reduce else None)
    if reduce:
        acc, got = acc
        reduce.first_done(got)
    dq, dk, dv = _attn_bwd(proj, do, lse, delta, 1, acc, BF16)
    dw_qk = _dw_pair(u, dq, dk, "dw_in_qk")
    dw_v = _dw(u, dv, "dw_in_v", hosted=reduce.second_pairs(dw_qk, dw_gz) if reduce else None)
    if reduce:
        dw_v, got = dw_v

    def small(dnw_pre):
        return _pack_small(dcw, dnw_pre, dcb, dnw_ssm, dnw_post, dvec[0:1, :NH], dvec[1:2, :NH], dvec[2:3, :NH],
                           loss_part[:, :1])

    res = _inproj_bwd_dx([dq, dk, dv, dg, dz], dxbcdt, w_all, x2, dy, norm_pre_w,
                         reduce.second(dw_v, got, small(jnp.zeros((1, D), F32))) if reduce else None)
    if reduce:
        res, got = res
        reduce.second_done(got)
        return res
    grad_x, dnw_pre = res
    dw_all = jnp.concatenate([dw_qk, dw_v, dw_gz, dw_x], axis=1)
    return grad_x, small(dnw_pre), dw_all, dw_out
```

```python
import functools

import jax
import jax.numpy as jnp
from jax import lax
from jax.experimental import pallas as pl
from jax.experimental.pallas import tpu as pltpu

F32 = jnp.float32
BF16 = jnp.bfloat16
MESH = pl.DeviceIdType.MESH
SDS = jax.ShapeDtypeStruct
ANY = pl.BlockSpec(memory_space=pl.ANY)

S = 4096
D = 1024
DP = 7168
SHARD = 1668
OFF_G, OFF_Z = 3072, 4096
NH = 16
CH = 128
NC = S // CH
EPS = 1e-6
NEG = -1e30
LANE = 128
VMEM_LIMIT = 48 * 1024 * 1024

TILES = SHARD // LANE
WIN = (TILES + 1) * LANE
SHIFT = SHARD - TILES * LANE
SECTION_TILES = {"qk": (0, 16), "v": (16, 8), "gz": (24, 16), "x": (40, 13)}
X_COLS = SECTION_TILES["x"][1] * LANE

ADAM_LR, ADAM_B1, ADAM_B2, ADAM_EPS, ADAM_WD, ADAM_STEP = 0.001, 0.9, 0.999, 1e-08, 0.01, 10


def _cp(sem, **kw):
    return pltpu.CompilerParams(dimension_semantics=sem, vmem_limit_bytes=VMEM_LIMIT, **kw)


def _dot(a, b):
    return jnp.dot(a, b, preferred_element_type=F32)


def _dot_nt(a, b):
    return lax.dot_general(a, b, (((1,), (1,)), ((), ())), preferred_element_type=F32)


def _dot_tn(a, b):
    return lax.dot_general(a, b, (((0,), (0,)), ((), ())), preferred_element_type=F32)


def _pieces(x, n):
    out = []
    for _ in range(n):
        p = x.astype(BF16)
        out.append(p)
        x = x - p.astype(F32)
    return out


def _pick(x, sel, n=2):
    parts = [_dot(p, sel) for p in _pieces(x, n)]
    return functools.reduce(jnp.add, parts)


def _pick_left(sel, x, n=3):
    parts = [_dot(sel, p) for p in _pieces(x, n)]
    return functools.reduce(jnp.add, parts)


def _sigmoid(v):
    return 0.5 * jnp.tanh(0.5 * v) + 0.5


def _iota(shape, dim):
    return lax.broadcasted_iota(jnp.int32, shape, dim)


def _inproj_fwd(x, nw, w_all, hosted=None):
    tm, tn = 1024, 1024
    ni, nj = S // tm, DP // tn
    n_host = len(hosted.arrays) if hosted else 0

    def body(x_hbm, nw_ref, w_hbm, *refs):
        host_in, (proj_ref, u_ref), refs = refs[:n_host], refs[n_host:n_host + 2], refs[n_host + 2:]
        host_out, host_sems, (xbuf, wbuf, xsem, wsem) = refs[:n_host], refs[n_host:-4], refs[-4:]
        i, j = pl.program_id(0), pl.program_id(1)
        s = i * nj + j

        def x_copy(k):
            return pltpu.make_async_copy(x_hbm.at[pl.ds(pl.multiple_of(k * tm, tm), tm)], xbuf.at[k % 2], xsem.at[k % 2])

        def w_copy(t):
            cols = pl.ds(pl.multiple_of((t % nj) * tn, tn), tn)
            return pltpu.make_async_copy(w_hbm.at[:, cols], wbuf.at[t % 3], wsem.at[t % 3])

        @pl.when(s == 0)
        def _():
            x_copy(0).start()
            w_copy(0).start()
            w_copy(1).start()
            if hosted:
                hosted.start(host_in, host_out, host_sems)

        @pl.when(j == 0)
        def _():
            x_copy(i).wait()
            pl.when(i + 1 < ni)(lambda: x_copy(i + 1).start())
            xf = xbuf[i % 2]
            r = lax.rsqrt(jnp.mean(xf * xf, axis=-1, keepdims=True) + EPS)
            u_ref[...] = (xf * r * nw_ref[...]).astype(BF16)

        w_copy(s).wait()
        pl.when(s + 2 < ni * nj)(lambda: w_copy(s + 2).start())
        proj_ref[...] = _dot(u_ref[...], wbuf[s % 3])
        if hosted:
            pl.when((i == ni // 2) & (j == 0))(lambda: hosted.pass_on(host_in, host_out, host_sems))
            pl.when((i == ni - 1) & (j == nj - 1))(lambda: hosted.finish(host_in, host_out, host_sems))

    rings = [pltpu.VMEM((2, tm, D), F32), pltpu.VMEM((3, D, tn), BF16),
             pltpu.SemaphoreType.DMA((2,)), pltpu.SemaphoreType.DMA((3,))]
    outs = pl.pallas_call(
        body, name="inproj_fwd", grid=(ni, nj),
        in_specs=[ANY, pl.BlockSpec((1, D), lambda i, j: (0, 0)), ANY] + [ANY] * n_host,
        out_specs=[pl.BlockSpec((tm, tn), lambda i, j: (i, j)), pl.BlockSpec((tm, D), lambda i, j: (i, 0))]
        + [ANY] * n_host,
        out_shape=[SDS((S, DP), F32), SDS((S, D), BF16)] + (hosted.out_shape if hosted else []),
        scratch_shapes=(hosted.scratch if hosted else []) + rings,
        compiler_params=_cp(("arbitrary", "arbitrary")),
    )(x, nw, w_all, *(hosted.arrays if hosted else []))
    return (outs[:2], outs[2:]) if hosted else outs


ATTN_QB = {1: 16, 4: 4, 16: 1}


def _unit_rows(r, u, d):
    return pl.ds(r + d * CH * u, CH, stride=d) if d > 1 else pl.ds(CH * u, CH)


def _for_units(d, qb, fn):
    for r in range(d):
        for u in range(qb):
            fn(r, u)


def _attn_mask(has_prev):
    qi, kj = _iota((2 * CH, 2 * CH), 0) & (CH - 1), _iota((2 * CH, 2 * CH), 1)
    cur_ok = (kj >= CH) & (kj - CH <= qi)
    prev_ok = (kj < CH) & (kj >= qi)
    return cur_ok | (prev_ok & has_prev)


def _stack_heads(v, lane_a):
    return jnp.concatenate([jnp.where(lane_a, v, 0.0), jnp.where(lane_a, 0.0, v)], axis=0).astype(BF16)


def _attn_specs(d, qb):
    rows, prows = CH * d * qb, CH * d
    nb = S // rows
    steps = (NH // 2) * nb

    def at(t):
        t = jnp.minimum(t, steps - 1)
        return t % nb, t // nb

    def cur(off):
        return pl.BlockSpec((rows, LANE), lambda t: (at(t)[0], off + at(t)[1]))

    def prev(off):
        return pl.BlockSpec((prows, LANE), lambda t: (jnp.maximum(at(t)[0] * qb - 1, 0), off + at(t)[1]))

    lag = pl.BlockSpec((rows, LANE), lambda t: at(jnp.maximum(t - 1, 0)))
    return nb, steps, cur, prev, lag


def _gather16(src_ref, dense_ref, tmp_ref):
    for a in range(4):
        tmp_ref[...] = src_ref[pl.ds(a, 4 * CH, stride=4), :]
        for b in range(4):
            dense_ref[a + 4 * b] = tmp_ref[pl.ds(b, CH, stride=4), :]


def _scatter16(dense_ref, dst_ref, tmp_ref):
    for a in range(4):
        for b in range(4):
            tmp_ref[pl.ds(b, CH, stride=4), :] = dense_ref[a + 4 * b]
        dst_ref[pl.ds(a, 4 * CH, stride=4), :] = tmp_ref[...]


def _unit_index(r, u, d):
    return (r,) if d == 16 else (_unit_rows(r, u, d), slice(None))


def _unit_kv(p_ref, c_ref, r, u, d):
    prev = p_ref[_unit_index(r, 0, d)] if u == 0 else c_ref[_unit_index(r, u - 1, d)]
    return jnp.concatenate([prev, c_ref[_unit_index(r, u, d)]], axis=0).astype(BF16)


def _dense_scratch(d, n):
    return [pltpu.VMEM((16, CH, LANE), F32)] * n + [pltpu.VMEM((4 * CH, LANE), F32)] if d == 16 else []


def _attn_fwd(proj, d, prior=None, final=False):
    qb = ATTN_QB[d]
    nb, steps, cur, prev, _ = _attn_specs(d, qb)
    n_prior = 2 if prior is not None else 0
    n_in, n_out = 5 + n_prior + final, 2 + final
    assert not (d == 16 and (n_prior or final))

    def body(*refs):
        ins, outs, scratch = refs[:n_in], refs[n_in:n_in + n_out], refs[n_in + n_out:]
        if d == 16:
            tmp_ref = scratch[-1]
            for src, dense in zip(ins, scratch):
                _gather16(src, dense, tmp_ref)
            block_outs, ins, outs = outs, scratch[:n_in], scratch[n_in:n_in + n_out]
        q_ref, kp_ref, kc_ref, vp_ref, vc_ref = ins[:5]
        prior_refs = ins[5:5 + n_prior]
        if final:
            g_ref, (mix_ref, o_ref, l_ref) = ins[-1], outs
        else:
            o_ref, l_ref = outs
        i = pl.program_id(0) % nb
        lane_a = _iota((CH, LANE), 1) < 64
        mask_first, mask_rest = _attn_mask(i > 0), _attn_mask(True)

        def unit(r, u):
            at = _unit_index(r, u, d)
            q2 = _stack_heads(q_ref[at] * 0.125, lane_a)
            k2, v2 = _unit_kv(kp_ref, kc_ref, r, u, d), _unit_kv(vp_ref, vc_ref, r, u, d)
            s = jnp.where(mask_first if u == 0 else mask_rest, _dot_nt(q2, k2), NEG)
            m = jnp.max(s, axis=1, keepdims=True)
            p = jnp.exp(s - m)
            l = jnp.sum(p, axis=1, keepdims=True)
            o2 = _dot(p.astype(BF16), v2) / l
            lse2 = m + jnp.log(l)
            o = jnp.where(lane_a, o2[:CH], o2[CH:])
            lse = jnp.where(lane_a, lse2[:CH], lse2[CH:])
            if n_prior:
                o_a, l_a = prior_refs[0][at], prior_refs[1][at]
                top = jnp.maximum(l_a, lse)
                e_a, e_b = jnp.exp(l_a - top), jnp.exp(lse - top)
                tot = e_a + e_b
                o = (e_a * o_a + e_b * o) / tot
                lse = top + jnp.log(tot)
            o_ref[at] = o
            l_ref[at] = lse
            if final:
                g = g_ref[at]
                mix_ref[at] = (o * (g * _sigmoid(g))).astype(BF16)

        _for_units(d, qb, unit)
        if d == 16:
            for dense, dst in zip(outs, block_outs):
                _scatter16(dense, dst, tmp_ref)

    in_specs = [cur(0), prev(8), cur(8), prev(16), cur(16)] + [cur(0)] * n_prior
    args = [proj] * 5 + (list(prior) if n_prior else [])
    out_specs, out_shape = [cur(0), cur(0)], [SDS((S, D), F32), SDS((S, D), F32)]
    if final:
        assert d == 1
        in_specs.append(cur(OFF_G // LANE))
        args.append(proj)
        out_specs, out_shape = [cur(0)] + out_specs, [SDS((S, 2 * D), BF16)] + out_shape
    return pl.pallas_call(
        body, name=f"attn_fwd_d{d}", grid=(steps,),
        in_specs=in_specs, out_specs=out_specs, out_shape=out_shape,
        scratch_shapes=_dense_scratch(d, n_in + n_out),
        compiler_params=_cp(("parallel",)),
    )(*args)


def _attn_bwd(proj, do, lse, delta, d, acc, out_dtype, hosted=None):
    qb = ATTN_QB[d]
    nb, steps, cur, prev, lag = _attn_specs(d, qb)
    has_acc = acc is not None
    n_in = 11 if has_acc else 8
    n_host, n_host_out = (len(hosted.arrays), len(hosted.out_shape)) if hosted else (0, 0)
    assert not (d == 16 and (has_acc or out_dtype != F32))
    rows = CH * d * qb
    carry = (2, 16, CH, LANE) if d == 16 else (2, rows, LANE)

    def body(*refs):
        ins, host_in, refs = refs[:n_in], refs[n_in:n_in + n_host], refs[n_in + n_host:]
        (dq_ref, dk_ref, dv_ref), host_out, scratch = refs[:3], refs[3:3 + n_host_out], refs[3 + n_host_out:]
        if hosted:
            scratch, host_sems = scratch[:-len(hosted.scratch)], scratch[-len(hosted.scratch):]
        ck_ref, cv_ref = scratch[:2]
        dq_f32 = dq_ref if out_dtype == F32 else scratch[2]
        t = pl.program_id(0)
        i = t % nb
        if hosted:
            pl.when(t == 0)(lambda: hosted.start(host_in, host_out, host_sems))
        if d == 16:
            dense, dq_f32, tmp_ref = scratch[2:2 + n_in], scratch[2 + n_in], scratch[-1]

            @pl.when(t < steps)
            def _():
                for src, dst in zip(ins, dense):
                    _gather16(src, dst, tmp_ref)

            ins = dense
        q_ref, kp_ref, kc_ref, vp_ref, vc_ref, do_ref, lse_ref, dl_ref = ins[:8]
        if has_acc:
            aq_ref, ak_ref, av_ref = ins[8:11]
        slot = t & 1
        now_k, now_v, old_k, old_v = ck_ref.at[slot], cv_ref.at[slot], ck_ref.at[1 - slot], cv_ref.at[1 - slot]
        lane_a = _iota((CH, LANE), 1) < 64
        mask_first, mask_rest = _attn_mask(i > 0), _attn_mask(True)

        @pl.when(t == 0)
        def _():
            ck_ref[1] = jnp.zeros(carry[1:], F32)
            cv_ref[1] = jnp.zeros(carry[1:], F32)

        def unit(r, u):
            at = _unit_index(r, u, d)
            q2 = _stack_heads(q_ref[at] * 0.125, lane_a)
            do2 = _stack_heads(do_ref[at], lane_a)
            k2, v2 = _unit_kv(kp_ref, kc_ref, r, u, d), _unit_kv(vp_ref, vc_ref, r, u, d)
            lsev, dlv = lse_ref[at], dl_ref[at]
            lse2 = jnp.concatenate([lsev[:, 0:1], lsev[:, 64:65]], axis=0)
            dl2 = jnp.concatenate([dlv[:, 0:1], dlv[:, 64:65]], axis=0)
            p = jnp.exp(jnp.where(mask_first if u == 0 else mask_rest, _dot_nt(q2, k2), NEG) - lse2)
            ds = (p * (_dot_nt(do2, v2) - dl2)).astype(BF16)
            dq2 = _dot(ds, k2)
            dk2 = _dot_tn(ds, q2)
            dv2 = _dot_tn(p.astype(BF16), do2)
            dq = jnp.where(lane_a, dq2[:CH], dq2[CH:]) * 0.125
            if has_acc:
                dq = dq + aq_ref[at]
            dq_f32[at] = dq
            if u == 0:
                before = _unit_index(r, qb - 1, d)
                old_k[before] += dk2[:CH]
                old_v[before] += dv2[:CH]
            else:
                before = _unit_index(r, u - 1, d)
                now_k[before] += dk2[:CH]
                now_v[before] += dv2[:CH]
            now_k[at] = dk2[CH:]
            now_v[at] = dv2[CH:]

        @pl.when(t < steps)
        def _():
            _for_units(d, qb, unit)
            if d == 16:
                _scatter16(dq_f32, dq_ref, tmp_ref)
            elif out_dtype != F32:
                dq_ref[...] = dq_f32[...].astype(out_dtype)

        if d == 16:
            _scatter16(old_k, dk_ref, tmp_ref)
            _scatter16(old_v, dv_ref, tmp_ref)
        else:
            dk, dv = old_k[...], old_v[...]
            if has_acc:
                dk, dv = dk + ak_ref[...], dv + av_ref[...]
            dk_ref[...] = dk.astype(out_dtype)
            dv_ref[...] = dv.astype(out_dtype)
        if hosted:
            pl.when(t == steps)(lambda: hosted.finish(host_in, host_out, host_sems))

    in_specs = [cur(0), prev(8), cur(8), prev(16), cur(16), cur(0), cur(0), cur(0)]
    args = [proj, proj, proj, proj, proj, do, lse, delta]
    if has_acc:
        in_specs += [cur(0), lag, lag]
        args += list(acc)
    scratch = [pltpu.VMEM(carry, F32), pltpu.VMEM(carry, F32)]
    if d == 16:
        scratch += _dense_scratch(d, n_in + 1)
    elif out_dtype != F32:
        scratch.append(pltpu.VMEM((rows, LANE), F32))
    out_specs, out_shape = [cur(0), lag, lag], [SDS((S, D), out_dtype)] * 3
    if hosted:
        args += hosted.arrays
        in_specs += [ANY] * n_host
        out_specs += [ANY] * n_host_out
        out_shape += hosted.out_shape
        scratch += hosted.scratch
    outs = pl.pallas_call(
        body, name=f"attn_bwd_d{d}", grid=(steps + 1,),
        in_specs=in_specs, out_specs=out_specs, out_shape=out_shape,
        scratch_shapes=scratch, compiler_params=_cp(("arbitrary",)),
    )(*args)
    return (outs[:3], outs[3:]) if hosted else outs


def _conv_taps(cur, prev8, first):
    row8 = _iota(prev8.shape, 0)
    prev8 = jnp.where(first, 0.0, prev8)
    taps = []
    for s in (3, 2, 1):
        rolled = pltpu.roll(cur, s, 0)
        head = jnp.where(row8 < s, pltpu.roll(prev8, s, 0), rolled[:8])
        taps.append(jnp.concatenate([head, rolled[8:]], axis=0))
    return taps + [cur]


def _conv(taps, w, b):
    acc = b + w[0:1, :] * taps[0]
    for k in (1, 2, 3):
        acc = acc + w[k:k + 1, :] * taps[k]
    return acc


def _expand():
    return (_iota((LANE, D), 1) // 64 == _iota((LANE, D), 0)).astype(BF16)


def _reduce():
    return (_iota((D, LANE), 0) // 64 == _iota((D, LANE), 1)).astype(BF16)


def _ssd_common(xs_c, bc_c, dt_raw, dtb, alog):
    head_lane = _iota((CH, LANE), 1) < NH
    xs = xs_c * _sigmoid(xs_c)
    bc = bc_c * _sigmoid(bc_c)
    pre = dt_raw + dtb
    dt = jnp.where(head_lane, jnp.maximum(pre, 0.0) + jnp.log(1.0 + jnp.exp(-jnp.abs(pre))), 0.0)
    a_row = jnp.where(head_lane[0:1], -jnp.exp(alog), 0.0)
    tri = (_iota((CH, CH), 1) <= _iota((CH, CH), 0)).astype(BF16)
    cs = _pick_left(tri, dt * a_row)
    cs_last = cs[CH - 1:CH, :]
    wide = _pick(jnp.concatenate([dt, jnp.exp(cs), jnp.exp(cs_last - cs)], axis=0), _expand())
    dt_b, e_b, f_b = wide[:CH], wide[CH:2 * CH], wide[2 * CH:]
    return dict(xs=xs, bc=bc, pre=pre, dt=dt, a_row=a_row, cs=cs, cs_t=cs.T, dt_b=dt_b, e_b=e_b, f_b=f_b,
                t_b=e_b[CH - 1:CH, :])


def _groups(bc):
    bcb = bc.astype(BF16)
    return [bcb[:, 0:128], bcb[:, 128:256]], [bcb[:, 256:384], bcb[:, 384:512]]


def _decay(q, h, tril):
    seg = q["cs"][:, h:h + 1] - q["cs_t"][h:h + 1, :]
    return jnp.exp(jnp.where(tril, seg, NEG))


def _ssm_fwd(proj, mix, cw, cb, dtb, alog, d_b, nw):
    def body(xs_ref, xsp_ref, bc_ref, bcp_ref, dt_ref, z_ref, cw_ref, cb_ref, dtb_ref, alog_ref, db_ref, nw_ref,
             mix_in_ref, mix_ref, y_ref, st_ref, conv_ref, h_ref):
        del mix_in_ref
        i = pl.program_id(0)

        @pl.when(i == 0)
        def _():
            h_ref[...] = jnp.zeros_like(h_ref)

        cw, cb = cw_ref[...], cb_ref[...]
        xs_c = _conv(_conv_taps(xs_ref[...], xsp_ref[...], i == 0), cw[:, :D], cb[:, :D])
        bc_c = _conv(_conv_taps(bc_ref[...], bcp_ref[...], i == 0), cw[:, D:], cb[:, D:])
        conv_ref[:, :D] = xs_c
        conv_ref[:, D:] = bc_c
        q = _ssd_common(xs_c, bc_c, dt_ref[...], dtb_ref[...], alog_ref[...])
        bg, cg = _groups(q["bc"])
        xs = q["xs"]
        xdt = xs * q["dt_b"]
        xdt_b = xdt.astype(BF16)
        h_in = h_ref[...]
        st_ref[...] = h_in
        hb = h_in.astype(BF16)
        tril = _iota((CH, CH), 1) <= _iota((CH, CH), 0)
        lane_a = _iota((CH, LANE), 1) < 64
        cbm = [_dot_nt(cg[g], bg[g]) for g in range(2)]
        pairs = []
        for hp in range(NH // 2):
            xp = xdt_b[:, hp * LANE:(hp + 1) * LANE]
            ya = _dot((cbm[hp // 4] * _decay(q, 2 * hp, tril)).astype(BF16), xp)
            yb = _dot((cbm[hp // 4] * _decay(q, 2 * hp + 1, tril)).astype(BF16), xp)
            pairs.append(jnp.where(lane_a, ya, yb))
        y_diag = jnp.concatenate(pairs, axis=1)
        y_off = jnp.concatenate([_dot(cg[g], hb[:, g * 512:(g + 1) * 512]) for g in range(2)], axis=1) * q["e_b"]
        y = y_diag + y_off + db_ref[...] * xs
        y_ref[...] = y
        xf = (xdt * q["f_b"]).astype(BF16)
        h_ref[...] = q["t_b"] * h_in + jnp.concatenate(
            [_dot_tn(bg[g], xf[:, g * 512:(g + 1) * 512]) for g in range(2)], axis=1)
        z = z_ref[...]
        yz = y * (z * _sigmoid(z))
        outs = []
        for g in range(2):
            v = yz[:, g * 512:(g + 1) * 512]
            outs.append(v * lax.rsqrt(jnp.mean(v * v, axis=-1, keepdims=True) + EPS))
        mix_ref[...] = (jnp.concatenate(outs, axis=1) * nw_ref[...]).astype(BF16)

    def col(width, blk, prev=False):
        if prev:
            return pl.BlockSpec((8, width), lambda i: (jnp.maximum(i * (CH // 8) - 1, 0), blk))
        return pl.BlockSpec((CH, width), lambda i: (i, blk))

    def full(a):
        return pl.BlockSpec(a.shape, lambda i: (0,) * a.ndim)

    return pl.pallas_call(
        body, name="ssm_fwd", grid=(NC,),
        in_specs=[col(D, 5), col(D, 5, True), col(512, 12), col(512, 12, True), col(LANE, 52), col(D, 4),
                  full(cw), full(cb), full(dtb), full(alog), full(d_b), full(nw), ANY],
        out_specs=[col(D, 1), col(D, 0), pl.BlockSpec((None, CH, D), lambda i: (i, 0, 0)), col(D + 512, 0)],
        out_shape=[SDS((S, 2 * D), BF16), SDS((S, D), F32), SDS((NC, CH, D), F32), SDS((S, D + 512), F32)],
        scratch_shapes=[pltpu.VMEM((CH, D), F32)],
        input_output_aliases={12: 0},
        compiler_params=_cp(("arbitrary",)),
    )(proj, proj, proj, proj, proj, proj, cw, cb, dtb, alog, d_b, nw, mix)


def _ssm_bwd(proj, dn, y_save, states, conv_out, cw, dtb, alog, d_b, nw):
    def body(xs_ref, bc_ref, dt_ref, z_ref, dn_ref, y_ref, st_ref, conv_ref,
             cw_ref, dtb_ref, alog_ref, db_ref, nw_ref,
             dz_ref, dx_ref, dcw_ref, dcb_ref, dsm_ref, dnw_ref, dh_ref, nxs_ref, nbc_ref):
        i = pl.program_id(0)
        ci = NC - 1 - i

        @pl.when(i == 0)
        def _():
            for ref in (dcw_ref, dcb_ref, dsm_ref, dnw_ref, dh_ref, nxs_ref, nbc_ref):
                ref[...] = jnp.zeros_like(ref)

        cw = cw_ref[...]
        xs_c, bc_c = conv_ref[:, :D], conv_ref[:, D:]
        q = _ssd_common(xs_c, bc_c, dt_ref[...], dtb_ref[...], alog_ref[...])
        bg, cg = _groups(q["bc"])
        xs, dt_b, e_b, f_b, t_b = q["xs"], q["dt_b"], q["e_b"], q["f_b"], q["t_b"]
        xdt = xs * dt_b
        xdt_b = xdt.astype(BF16)
        h_in = st_ref[...]
        hb = h_in.astype(BF16)
        dh_new = dh_ref[...]
        dhb = dh_new.astype(BF16)
        red = _reduce()

        z, y, dn, nw_v = z_ref[...], y_ref[...], dn_ref[...], nw_ref[...]
        sig = _sigmoid(z)
        sz = z * sig
        yz = y * sz
        gdn = dn * nw_v
        dyz, dnw = [], []
        for g in range(2):
            v, gv = yz[:, g * 512:(g + 1) * 512], gdn[:, g * 512:(g + 1) * 512]
            r = lax.rsqrt(jnp.mean(v * v, axis=-1, keepdims=True) + EPS)
            dnw.append(dn[:, g * 512:(g + 1) * 512] * v * r)
            dyz.append(r * (gv - v * (r * r) * jnp.mean(gv * v, axis=-1, keepdims=True)))
        dyz = jnp.concatenate(dyz, axis=1)
        dnw_ref[...] += jnp.sum(jnp.concatenate(dnw, axis=1), axis=0, keepdims=True)
        dy = dyz * sz
        dz_ref[...] = (dyz * y * (sig * (1.0 + z * (1.0 - sig)))).astype(BF16)
        dy_b = dy.astype(BF16)

        tril = _iota((CH, CH), 1) <= _iota((CH, CH), 0)
        lane_a = _iota((CH, LANE), 1) < 64
        cbm = [_dot_nt(cg[g], bg[g]) for g in range(2)]
        dcbm = [jnp.zeros((CH, CH), F32), jnp.zeros((CH, CH), F32)]
        seg_rows = jnp.zeros((CH, LANE), F32)
        seg_cols = jnp.zeros((LANE, CH), F32)
        row_id, col_id = _iota((CH, LANE), 0), _iota((CH, LANE), 1)
        dx_pairs = []
        for hp in range(NH // 2):
            g = hp // 4
            xp = xdt_b[:, hp * LANE:(hp + 1) * LANE]
            dyp_f = dy[:, hp * LANE:(hp + 1) * LANE]
            dyp = dy_b[:, hp * LANE:(hp + 1) * LANE]
            halves = []
            for k in range(2):
                h = 2 * hp + k
                lane = lane_a if k == 0 else jnp.logical_not(lane_a)
                dec = _decay(q, h, tril)
                gm = cbm[g] * dec
                dgm = _dot_nt(jnp.where(lane, dyp_f, 0.0).astype(BF16), xp)
                dcbm[g] = dcbm[g] + dgm * dec
                prod = dgm * gm
                seg_rows = jnp.where(col_id == h, jnp.sum(prod, axis=1, keepdims=True), seg_rows)
                seg_cols = jnp.where(row_id == h, jnp.sum(prod, axis=0, keepdims=True), seg_cols)
                halves.append(_dot_tn(gm.astype(BF16), dyp))
            dx_pairs.append(jnp.where(lane_a, halves[0], halves[1]))
        dxdt_diag = jnp.concatenate(dx_pairs, axis=1)

        qv = jnp.concatenate([_dot(bg[g], dhb[:, g * 512:(g + 1) * 512]) for g in range(2)], axis=1)
        y_off = jnp.concatenate([_dot(cg[g], hb[:, g * 512:(g + 1) * 512]) for g in range(2)], axis=1) * e_b
        xfq = xdt * f_b * qv
        dxdt = dxdt_diag + f_b * qv
        tdt = jnp.sum(dh_new * h_in, axis=0, keepdims=True) * t_b
        per_head = _pick(jnp.concatenate([xfq, dy * y_off, dxdt * xs, dy * xs, jnp.broadcast_to(tdt, (8, D))],
                                         axis=0), red)
        fdf, dyoff_h, dxdtxs_h, dyxs_h = [per_head[k * CH:(k + 1) * CH] for k in range(4)]
        dcs = seg_rows - seg_cols.T + dyoff_h - fdf
        last = per_head[4 * CH:4 * CH + 1] + jnp.sum(fdf, axis=0, keepdims=True)
        dcs = dcs + jnp.where(_iota((CH, LANE), 0) == CH - 1, last, 0.0)
        tri_t = (_iota((CH, CH), 1) >= _iota((CH, CH), 0)).astype(BF16)
        da = _pick_left(tri_t, dcs)
        ddt = da * q["a_row"] + dxdtxs_h
        dxs = dxdt * dt_b + db_ref[...] * dy
        ddt_raw = ddt * _sigmoid(q["pre"])
        dsm_ref[0:1, :] += jnp.sum(ddt_raw, axis=0, keepdims=True)
        dsm_ref[1:2, :] += jnp.sum(da * q["dt"], axis=0, keepdims=True) * q["a_row"]
        dsm_ref[2:3, :] += jnp.sum(dyxs_h, axis=0, keepdims=True)
        edy = (e_b * dy).astype(BF16)
        xf = (xdt * f_b).astype(BF16)
        dbs, dcs_g, dhs = [], [], []
        for g in range(2):
            sl = slice(g * 512, (g + 1) * 512)
            dcb_b = dcbm[g].astype(BF16)
            dcs_g.append(_dot(dcb_b, bg[g]) + _dot_nt(edy[:, sl], hb[:, sl]))
            dbs.append(_dot_tn(dcb_b, cg[g]) + _dot_nt(xf[:, sl], dhb[:, sl]))
            dhs.append(_dot_tn(cg[g], edy[:, sl]))
        dh_ref[...] = t_b * dh_new + jnp.concatenate(dhs, axis=1)
        dbc = jnp.concatenate(dbs + dcs_g, axis=1)

        def conv_bwd(dact, pre, x_raw, w, nxt_ref, lo):
            s = _sigmoid(pre)
            dconv = dact * (s * (1.0 + pre * (1.0 - s)))
            nxt8 = nxt_ref[...]
            row8 = _iota(nxt8.shape, 0)
            hi = lo + dconv.shape[1]
            dcb_ref[:, lo:hi] += jnp.sum(dconv, axis=0, keepdims=True)
            later = [dconv]
            for s_ in (1, 2, 3):
                rolled = pltpu.roll(dconv, CH - s_, 0)
                tail = jnp.where(row8 >= 8 - s_, pltpu.roll(nxt8, 8 - s_, 0), rolled[CH - 8:])
                later.append(jnp.concatenate([rolled[:CH - 8], tail], axis=0))
            dx = None
            for s_, up in enumerate(later):
                k = 3 - s_
                dcw_ref[k:k + 1, lo:hi] += jnp.sum(up * x_raw, axis=0, keepdims=True)
                dx = w[k:k + 1, :] * up if dx is None else dx + w[k:k + 1, :] * up
            nxt_ref[...] = dconv[:8]
            return dx

        dx_ref[:, 0:D] = conv_bwd(dxs, xs_c, xs_ref[...], cw[:, :D], nxs_ref, 0).astype(BF16)
        dx_ref[:, D:D + 512] = conv_bwd(dbc, bc_c, bc_ref[...], cw[:, D:], nbc_ref, D).astype(BF16)
        dx_ref[:, D + 512:D + 640] = ddt_raw.astype(BF16)
        dx_ref[:, D + 640:] = jnp.zeros((CH, D - 640), BF16)

    def col(width, blk):
        return pl.BlockSpec((CH, width), lambda i: (NC - 1 - i, blk))

    def full(a):
        return pl.BlockSpec(a.shape, lambda i: (0,) * len(a.shape))

    acc_shapes = [SDS((4, 1536), F32), SDS((1, 1536), F32), SDS((8, LANE), F32), SDS((1, D), F32)]
    return pl.pallas_call(
        body, name="ssm_bwd", grid=(NC,),
        in_specs=[col(D, 5), col(512, 12), col(LANE, 52), col(D, 4),
                  col(D, 0), col(D, 0), pl.BlockSpec((None, CH, D), lambda i: (NC - 1 - i, 0, 0)), col(D + 512, 0),
                  full(cw), full(dtb), full(alog), full(d_b), full(nw)],
        out_specs=[col(D, 0), col(2 * D, 0)] + [full(a) for a in acc_shapes],
        out_shape=[SDS((S, D), BF16), SDS((S, 2 * D), BF16)] + acc_shapes,
        scratch_shapes=[pltpu.VMEM((CH, D), F32), pltpu.VMEM((8, D), F32), pltpu.VMEM((8, 512), F32)],
        compiler_params=_cp(("arbitrary",)),
    )(proj, proj, proj, proj, dn, y_save, states, conv_out, cw, dtb, alog, d_b, nw)


def _outproj_loss(mix, w_out, x, tgt, nw, attn_pre, proj):
    tm = 256

    def body(mix_ref, w_ref, x_ref, t_ref, nw_ref, pre_ref, g_ref,
             dy_ref, dn_ref, do_ref, delta_ref, dg_ref, dw_ref, dnw_ref, loss_ref):
        @pl.when(pl.program_id(0) == 0)
        def _():
            dw_ref[...] = jnp.zeros_like(dw_ref)
            dnw_ref[...] = jnp.zeros_like(dnw_ref)
            loss_ref[...] = jnp.zeros_like(loss_ref)

        mixv, w = mix_ref[...], w_ref[...]
        out = _dot(mixv, w)
        r = lax.rsqrt(jnp.mean(out * out, axis=-1, keepdims=True) + EPS)
        nh = out * r
        nw_v = nw_ref[...]
        err = x_ref[...] + nh * nw_v - t_ref[...]
        loss_ref[...] += 0.5 * jnp.sum(jnp.mean(err * err, axis=-1, keepdims=True), axis=0, keepdims=True)
        dy = err * (1.0 / D)
        dy_ref[...] = dy
        dnw_ref[...] += jnp.sum(dy * nh, axis=0, keepdims=True)
        gdn = dy * nw_v
        dout = (r * (gdn - nh * jnp.mean(gdn * nh, axis=-1, keepdims=True))).astype(BF16)
        dmix = _dot_nt(dout, w)
        dw_ref[...] += _dot_tn(mixv, dout)
        dn_ref[...] = dmix[:, D:]
        dm, g, pre_v = dmix[:, :D], g_ref[...], pre_ref[...]
        sig = _sigmoid(g)
        do = dm * (g * sig)
        do_ref[...] = do
        dg_ref[...] = (dm * pre_v * (sig * (1.0 + g * (1.0 - sig)))).astype(BF16)
        prod = do * pre_v
        same_head = (_iota((LANE, LANE), 0) // 64 == _iota((LANE, LANE), 1) // 64).astype(BF16)
        for cb in range(D // LANE):
            delta_ref[:, cb * LANE:(cb + 1) * LANE] = _pick(prod[:, cb * LANE:(cb + 1) * LANE], same_head)

    row = lambda w: pl.BlockSpec((tm, w), lambda i: (i, 0))
    full = lambda s: pl.BlockSpec(s, lambda i: (0, 0))
    return pl.pallas_call(
        body, name="outproj_loss", grid=(S // tm,),
        in_specs=[row(2 * D), full((2 * D, D)), row(D), row(D), full((1, D)), row(D),
                  pl.BlockSpec((tm, D), lambda i: (i, OFF_G // D))],
        out_specs=[row(D), row(D), row(D), row(D), row(D), full((2 * D, D)), full((1, D)), full((1, LANE))],
        out_shape=[SDS((S, D), F32)] * 4 + [SDS((S, D), BF16), SDS((2 * D, D), F32), SDS((1, D), F32),
                                            SDS((1, LANE), F32)],
        compiler_params=_cp(("arbitrary",)),
    )(mix, w_out, x, tgt, nw, attn_pre, proj)


def _inproj_bwd_dx(srcs, dxbcdt, w_all, x, dy, nw, hosted=None):
    tm = 512
    nk = DP // D
    n_host, n_host_out = (len(hosted.arrays), len(hosted.out_shape)) if hosted else (0, 0)

    def body(*refs):
        src_refs = refs[:nk]
        w_ref, x_ref, dy_ref, nw_ref = refs[nk:nk + 4]
        host_in, refs = refs[nk + 4:nk + 4 + n_host], refs[nk + 4 + n_host:]
        gx_ref, dnw_ref = refs[:2]
        host_out, host_sems = refs[2:2 + n_host_out], refs[2 + n_host_out:]
        i = pl.program_id(0)

        @pl.when(i == 0)
        def _():
            if hosted:
                hosted.start(host_in, host_out, host_sems)
            dnw_ref[...] = jnp.zeros_like(dnw_ref)

        du = None
        for k, ref in enumerate(src_refs):
            width = min(D, 5 * D + X_COLS - k * D)
            part = _dot_nt(ref[:, :width], w_ref[:, k * D:k * D + width])
            du = part if du is None else du + part
        xf, nw_v = x_ref[...], nw_ref[...]
        r = lax.rsqrt(jnp.mean(xf * xf, axis=-1, keepdims=True) + EPS)
        xh = xf * r
        dnw_ref[...] += jnp.sum(du * xh, axis=0, keepdims=True)
        gdu = du * nw_v
        gx_ref[...] = r * (gdu - xh * jnp.mean(gdu * xh, axis=-1, keepdims=True)) + dy_ref[...]

        if hosted:
            pl.when(i == S // tm - 1)(lambda: hosted.finish(host_in, host_out, host_sems))

    row = pl.BlockSpec((tm, D), lambda i: (i, 0))
    row1 = pl.BlockSpec((tm, D), lambda i: (i, 1))
    one = pl.BlockSpec((1, D), lambda i: (0, 0))
    whole_w = pl.BlockSpec((D, DP), lambda i: (0, 0), pipeline_mode=pl.Buffered(1))
    args = [*srcs, dxbcdt, dxbcdt, w_all, x, dy, nw]
    in_specs = [row] * len(srcs) + [row, row1, whole_w, row, row, one]
    out_specs, out_shape, scratch = [row, one], [SDS((S, D), F32), SDS((1, D), F32)], []
    if hosted:
        args += hosted.arrays
        in_specs += [ANY] * n_host
        out_specs += [ANY] * n_host_out
        out_shape += hosted.out_shape
        scratch += hosted.scratch
    outs = pl.pallas_call(
        body, name="inproj_bwd_dx", grid=(S // tm,),
        in_specs=in_specs, out_specs=out_specs, out_shape=out_shape, scratch_shapes=scratch,
        compiler_params=_cp(("arbitrary",)),
    )(*args)
    return (outs[:2], outs[2:]) if hosted else outs


def _dw(u, dsec, name, width=D, hosted=None):
    ts = 1024
    n_host, n_host_out = (len(hosted.arrays), len(hosted.out_shape)) if hosted else (0, 0)

    def body(u_ref, d_ref, *refs):
        host_in, o_ref, refs = refs[:n_host], refs[n_host], refs[n_host + 1:]
        host_out, host_sems = refs[:n_host_out], refs[n_host_out:]
        i = pl.program_id(0)

        @pl.when(i == 0)
        def _():
            if hosted:
                hosted.start(host_in, host_out, host_sems)
            o_ref[...] = jnp.zeros_like(o_ref)

        o_ref[...] += _dot_tn(u_ref[...], d_ref[...])
        if hosted:
            pl.when(i == S // ts - 1)(lambda: hosted.finish(host_in, host_out, host_sems))

    outs = pl.pallas_call(
        body, name=name, grid=(S // ts,),
        in_specs=[pl.BlockSpec((ts, D), lambda i: (i, 0)), pl.BlockSpec((ts, width), lambda i: (i, 0))]
        + [ANY] * n_host,
        out_specs=[pl.BlockSpec((D, width), lambda i: (0, 0))] + [ANY] * n_host_out,
        out_shape=[SDS((D, width), F32)] + (hosted.out_shape if hosted else []),
        scratch_shapes=hosted.scratch if hosted else [],
        compiler_params=_cp(("arbitrary",)),
    )(u, dsec, *(hosted.arrays if hosted else []))
    return (outs[0], outs[1:]) if hosted else outs[0]


def _dw_pair(u, da, db, name):
    ts = 1024
    last = S // ts - 1

    def body(u_ref, a_ref, b_ref, o_ref):
        j = pl.program_id(0)

        @pl.when(pl.program_id(1) == 0)
        def _():
            o_ref[...] = jnp.zeros_like(o_ref)

        for k, d_ref in enumerate((a_ref, b_ref)):
            @pl.when(j == k)
            def _():
                o_ref[...] += _dot_tn(u_ref[...], d_ref[...])

    return pl.pallas_call(
        body, name=name, grid=(2, S // ts),
        in_specs=[pl.BlockSpec((ts, D), lambda j, i: (i, 0)),
                  pl.BlockSpec((ts, D), lambda j, i: (jnp.where(j == 0, i, last), 0)),
                  pl.BlockSpec((ts, D), lambda j, i: (jnp.where(j == 1, i, 0), 0))],
        out_specs=pl.BlockSpec((D, D), lambda j, i: (0, j)),
        out_shape=SDS((D, 2 * D), F32),
        compiler_params=_cp(("arbitrary", "arbitrary")),
    )(u, da, db)


def _place():
    x, y, c = lax.axis_index("x"), lax.axis_index("y"), lax.axis_index("c")
    return x, y, c, 2 * x + y


def _chip_of(x, y, k):
    px = 1 - x if k & 2 else x
    py = 1 - y if k & 1 else y
    return px, py, 2 * px + py


def _remote(src, dst, send_sem, recv_sem, dev):
    return pltpu.make_async_remote_copy(src_ref=src, dst_ref=dst, send_sem=send_sem, recv_sem=recv_sem,
                                        device_id=dev, device_id_type=MESH)


def _gather_weights(w_in_b):
    half = w_in_b.shape[0] // 2
    quarter = half // 2

    def body(src, dst, send, recv):
        x, y, c, j = _place()
        me, sib = (x, y, c), (x, y, 1 - c)
        nbr = {"x": _chip_of(x, y, 2), "y": _chip_of(x, y, 1)}
        diag = _chip_of(x, y, 3)[2]
        started, arrivals = [], []

        def rows(n_quarter=None, sibling=False):
            base = (1 - c if sibling else c) * half
            return pl.ds(base, half) if n_quarter is None else pl.ds(base + n_quarter * quarter, quarter)

        def sem(n):
            return send.at[n], recv.at[n]

        def go(cp):
            cp.start()
            started.append(cp)

        own = _remote(src, dst.at[j], *sem(8), sib)
        go(own)
        for n, axis in enumerate("xy"):
            px, py, _ = nbr[axis]
            go(_remote(src.at[rows()], dst.at[j, rows()], *sem(n), (px, py, c)))
        for n, axis in enumerate("xy"):
            ox, oy, _ = nbr["y" if axis == "x" else "x"]
            pj = nbr[axis][2]
            _remote(src.at[rows()], dst.at[pj, rows()], *sem(n), me).wait_recv()
            go(_remote(dst.at[pj, rows(n)], dst.at[pj, rows(n)], *sem(2 + n), (ox, oy, c)))
            go(_remote(dst.at[pj, rows()], dst.at[pj, rows()], *sem(4 + n), sib))
            arrivals.append(_remote(src.at[rows()], dst.at[pj, rows(None, True)], *sem(4 + n), me))
        for n in range(2):
            _remote(dst.at[diag, rows(n)], dst.at[diag, rows(n)], *sem(2 + n), me).wait_recv()
            go(_remote(dst.at[diag, rows(n)], dst.at[diag, rows(n)], *sem(6 + n), sib))
            arrivals.append(_remote(dst.at[diag, rows(n, True)], dst.at[diag, rows(n, True)], *sem(6 + n), me))
        for cp in arrivals + [own]:
            cp.wait_recv()
        for cp in started:
            cp.wait_send()

    return pl.pallas_call(
        body, name="gather_weights", in_specs=[ANY], out_specs=ANY,
        out_shape=SDS((4,) + w_in_b.shape, BF16),
        scratch_shapes=[pltpu.SemaphoreType.DMA((9,)), pltpu.SemaphoreType.DMA((9,))],
        compiler_params=pltpu.CompilerParams(has_side_effects=True),
    )(w_in_b)


class _LateGather:
    def __init__(self, w_out_b, conv_w):
        self.arrays = [w_out_b, conv_w]
        self.out_shape = [SDS((4,) + w_out_b.shape, BF16), SDS((4,) + conv_w.shape, F32)]
        self.scratch = [pltpu.SemaphoreType.DMA((11,)), pltpu.SemaphoreType.DMA((11,))]

    def _plan(self, ins, outs, sems):
        x, y, c, j = _place()
        send, recv = sems
        (wo, cw), (gwo, gcw) = ins, outs
        half = wo.shape[0] // 2
        mine, theirs = pl.ds(c * half, half), pl.ds((1 - c) * half, half)
        me, sib = (x, y, c), (x, y, 1 - c)
        first, arrive, forward, last = [], [], [], []
        for k in (1, 2, 3):
            px, py, pj = _chip_of(x, y, k)
            first += [_remote(wo.at[mine], gwo.at[j, mine], send.at[k - 1], recv.at[k - 1], (px, py, c)),
                      _remote(cw, gcw.at[j], send.at[k + 2], recv.at[k + 2], (px, py, c))]
            arrive.append(_remote(wo.at[mine], gwo.at[pj, mine], send.at[k - 1], recv.at[k - 1], me))
            forward.append(_remote(gwo.at[pj, mine], gwo.at[pj, mine], send.at[k + 5], recv.at[k + 5], sib))
            last += [_remote(cw, gcw.at[pj], send.at[k + 2], recv.at[k + 2], me),
                     _remote(wo.at[theirs], gwo.at[pj, theirs], send.at[k + 5], recv.at[k + 5], me)]
        first += [_remote(wo, gwo.at[j], send.at[9], recv.at[9], sib),
                  _remote(cw, gcw.at[j], send.at[10], recv.at[10], sib)]
        last += first[-2:]
        return first, arrive, forward, last

    def start(self, ins, outs, sems):
        for cp in self._plan(ins, outs, sems)[0]:
            cp.start()

    def pass_on(self, ins, outs, sems):
        _, arrive, forward, _ = self._plan(ins, outs, sems)
        for got, fwd in zip(arrive, forward):
            got.wait_recv()
            fwd.start()

    def finish(self, ins, outs, sems):
        first, _, forward, last = self._plan(ins, outs, sems)
        for cp in last:
            cp.wait_recv()
        for cp in first + forward:
            cp.wait_send()


def _window(s, names):
    lo, hi = TILES * s, TILES * s + TILES + 1
    pieces = []
    for n, name in enumerate(names):
        a, count = SECTION_TILES[name]
        first, last = max(lo, a), min(hi, a + count)
        if first < last:
            pieces.append((n, first - a, last - first, first - lo))
    assert sum(p[2] for p in pieces) == TILES + 1
    return pieces


class _PairExchange:
    def __init__(self, names, sections, shards, more=()):
        self.names, self.shards = names, shards
        self.there = [n for n, a in enumerate(sections) if a is not None]
        self.arrays = [sections[n] for n in self.there] + list(more)
        self.out_shape = [SDS((len(shards), D // 2, WIN), F32)]
        self.out_shape += [SDS((a.shape[0], a.shape[1] // 2, a.shape[2]), F32) for a in more]
        n = sum(p[0] in self.there for s in shards for p in _window(s, names)) + len(more)
        self.scratch = [pltpu.SemaphoreType.DMA((n,)) for _ in range(2)]

    def _copies(self, ins, outs, sems):
        x, y, c, _ = _place()
        sib = (x, y, 1 - c)
        rows = pl.ds((1 - c) * (D // 2), D // 2)
        k = 0
        for i, s in enumerate(self.shards):
            for n, tile, tiles, at in _window(s, self.names):
                if n in self.there:
                    yield _remote(ins[self.there.index(n)].at[rows, pl.ds(tile * LANE, tiles * LANE)],
                                  outs[0].at[i, :, pl.ds(at * LANE, tiles * LANE)], sems[0].at[k], sems[1].at[k], sib)
                    k += 1
        for src, dst in zip(ins[len(self.there):], outs[1:]):
            half = src.shape[1] // 2
            yield _remote(src.at[:, pl.ds((1 - c) * half, half)], dst, sems[0].at[k], sems[1].at[k], sib)
            k += 1

    def start(self, ins, outs, sems):
        for cp in self._copies(ins, outs, sems):
            cp.start()

    def finish(self, ins, outs, sems):
        for cp in self._copies(ins, outs, sems):
            cp.wait()


def _exchange_call(exchange, name, into=None):
    n, n_out = len(exchange.arrays), len(exchange.out_shape)
    given = list(into) if into else []

    def body(*refs):
        ins, outs, sems = refs[:n], refs[n + len(given):n + len(given) + n_out], refs[n + len(given) + n_out:]
        exchange.start(ins, outs, sems)
        exchange.finish(ins, outs, sems)

    return pl.pallas_call(
        body, name=name, in_specs=[ANY] * (n + len(given)), out_specs=[ANY] * n_out, out_shape=exchange.out_shape,
        input_output_aliases={n + k: k for k in range(len(given))},
        scratch_shapes=exchange.scratch, compiler_params=pltpu.CompilerParams(has_side_effects=True),
    )(*exchange.arrays, *given)


def _pair_sum_windows(cidx, names, sections, shards, r, name):
    n, half, _ = r.shape
    tr = min(half, 256)
    nt = half // tr

    def body(c_ref, *refs):
        del c_ref
        secs, r_ref, o_ref = refs[:-2], refs[-2], refs[-1]
        for i, s in enumerate(shards):
            for k, tile, tiles, at in _window(s, names):
                own = secs[k][:, tile * LANE:(tile + tiles) * LANE]
                there = slice(at * LANE, (at + tiles) * LANE)
                o_ref[i, :, there] = (own + r_ref[i, :, there]).astype(BF16)

    window = pl.BlockSpec((n, tr, WIN), lambda t, c: (0, t, 0))
    return pl.pallas_call(
        body, name=name,
        grid_spec=pltpu.PrefetchScalarGridSpec(
            num_scalar_prefetch=1, grid=(nt,),
            in_specs=[pl.BlockSpec((tr, a.shape[1]), lambda t, c: (c[0] * nt + t, 0)) for a in sections] + [window],
            out_specs=window),
        out_shape=SDS(r.shape, BF16),
        compiler_params=_cp(("parallel",)),
    )(cidx, *sections, r)


def _pair_sum(cidx, g, r, name):
    n, half, width = r.shape
    tr = min(half, 256)
    nt = half // tr

    def body(c_ref, g_ref, r_ref, o_ref):
        del c_ref
        o_ref[...] = (g_ref[...] + r_ref[...]).astype(BF16)

    return pl.pallas_call(
        body, name=name,
        grid_spec=pltpu.PrefetchScalarGridSpec(
            num_scalar_prefetch=1, grid=(n, nt),
            in_specs=[pl.BlockSpec((None, tr, width), lambda s, t, c: (s, c[0] * nt + t, 0)),
                      pl.BlockSpec((None, tr, width), lambda s, t, c: (s, t, 0))],
            out_specs=pl.BlockSpec((None, tr, width), lambda s, t, c: (s, t, 0))),
        out_shape=SDS(r.shape, BF16),
        compiler_params=_cp(("parallel", "parallel")),
    )(cidx, g, r)


class _ChipExchange:
    def __init__(self, arrays, rows):
        self.arrays, self.rows = list(arrays), list(rows)
        self.out_shape = [SDS((4,) + a.shape[1:], BF16) for a in self.arrays]
        self.scratch = [pltpu.SemaphoreType.DMA((3 * len(self.arrays),)) for _ in range(2)]

    def _copies(self, ins, outs, sems):
        x, y, c, j = _place()
        send, recv = sems
        for a, (src, dst, row) in enumerate(zip(ins, outs, self.rows)):
            for k in (1, 2, 3):
                px, py, pj = _chip_of(x, y, k)
                n = 3 * a + k - 1
                slot = pj if row is None else py
                yield (None if row is None else px == row, None if row is None else x == row,
                       _remote(src.at[slot], dst.at[j], send.at[n], recv.at[n], (px, py, c)),
                       _remote(src.at[0], dst.at[pj], send.at[n], recv.at[n], (x, y, c)))

    def start(self, ins, outs, sems):
        for sends, _, send, _ in self._copies(ins, outs, sems):
            if sends is None:
                send.start()
            else:
                pl.when(sends)(send.start)

    def finish(self, ins, outs, sems):
        for sends, owns, send, arrival in self._copies(ins, outs, sems):
            if sends is None:
                arrival.wait_recv()
                send.wait_send()
            else:
                pl.when(owns)(arrival.wait_recv)
                pl.when(sends)(send.wait_send)


def _all_gather_rows(src, dst, rows, send, recv, local_sem):
    x, y, c, j = _place()
    me = 2 * j + c
    local = pltpu.make_async_copy(src, dst.at[me, rows], local_sem)
    cps, arrivals = [], []
    for k in range(1, 8):
        px, py, pj = _chip_of(x, y, k >> 1)
        pc = 1 - c if k & 1 else c
        cps.append(_remote(src, dst.at[me, rows], send.at[k - 1], recv.at[k - 1], (px, py, pc)))
        arrivals.append(_remote(src, dst.at[2 * pj + pc, rows], send.at[k - 1], recv.at[k - 1], (x, y, c)))
    starts = [local.start] + [cp.start for cp in cps]
    waits = [cp.wait_recv for cp in arrivals] + [cp.wait_send for cp in cps] + [local.wait]
    return starts, waits


class _SmallExchange:
    def __init__(self, small):
        self.arrays = [small]
        self.out_shape = [SDS((8,) + small.shape, F32)]
        self.scratch = [pltpu.SemaphoreType.DMA((7,)), pltpu.SemaphoreType.DMA((7,)), pltpu.SemaphoreType.DMA]

    def start(self, ins, outs, sems):
        for go in _all_gather_rows(ins[0], outs[0], slice(None), *sems)[0]:
            go()

    def finish(self, ins, outs, sems):
        for wait in _all_gather_rows(ins[0], outs[0], slice(None), *sems)[1]:
            wait()


class _Both:
    def __init__(self, a, b):
        self.parts = (a, b)
        self.arrays, self.out_shape, self.scratch = a.arrays + b.arrays, a.out_shape + b.out_shape, a.scratch + b.scratch

    def _split(self, ins, outs, sems):
        a, b = self.parts
        return ((a, ins[:len(a.arrays)], outs[:len(a.out_shape)], sems[:len(a.scratch)]),
                (b, ins[len(a.arrays):], outs[len(a.out_shape):], sems[len(a.scratch):]))

    def start(self, ins, outs, sems):
        for part, *refs in self._split(ins, outs, sems):
            part.start(*refs)

    def finish(self, ins, outs, sems):
        for part, *refs in self._split(ins, outs, sems):
            part.finish(*refs)


def _slot_sum(r, name):
    n, rows, width = r.shape
    tr = min(rows, 256)

    def body(r_ref, o_ref):
        acc = r_ref[0].astype(F32)
        for s in range(1, n):
            acc = acc + r_ref[s].astype(F32)
        o_ref[...] = acc

    return pl.pallas_call(
        body, name=name, grid=(rows // tr,),
        in_specs=[pl.BlockSpec((n, tr, width), lambda t: (0, t, 0))],
        out_specs=pl.BlockSpec((tr, width), lambda t: (t, 0)),
        out_shape=SDS((rows, width), F32),
        compiler_params=_cp(("parallel",)),
    )(r)


def _chip_sum(where, recv, own, name):
    n, rows, width = recv.shape
    tr = min(rows, 256)
    nt = rows // tr

    def body(j_ref, r_ref, own_ref, o_ref):
        acc = None
        for s in range(n):
            term = jnp.where(j_ref[0] == s, own_ref[...], r_ref[s]).astype(F32)
            acc = term if acc is None else acc + term
        o_ref[...] = acc

    return pl.pallas_call(
        body, name=name,
        grid_spec=pltpu.PrefetchScalarGridSpec(
            num_scalar_prefetch=1, grid=(nt,),
            in_specs=[pl.BlockSpec((n, tr, width), lambda t, j: (0, t, 0)),
                      pl.BlockSpec((None, tr, width), lambda t, j: (j[0], t, 0))],
            out_specs=pl.BlockSpec((tr, width), lambda t, j: (j[1] * nt + t, 0))),
        out_shape=SDS((2 * rows, width), F32),
        compiler_params=_cp(("parallel",)),
    )(where, recv, own)


def _chip_sum_rows(place, recv0, own0, recv1, own1, name):
    n, rows, width = recv0.shape
    tr = min(rows, 256)
    nt = rows // tr

    def body(p_ref, r0_ref, o0_ref, r1_ref, o1_ref, o_ref):
        first_row = p_ref[2] == 0
        own = jnp.where(first_row, o0_ref[...], o1_ref[...])
        acc = None
        for s in range(n):
            term = jnp.where(p_ref[0] == s, own, jnp.where(first_row, r0_ref[s], r1_ref[s])).astype(F32)
            acc = term if acc is None else acc + term
        o_ref[...] = acc

    recv = pl.BlockSpec((n, tr, width), lambda t, p: (0, t, 0))
    own = pl.BlockSpec((None, tr, width), lambda t, p: (p[3], t, 0))
    return pl.pallas_call(
        body, name=name,
        grid_spec=pltpu.PrefetchScalarGridSpec(
            num_scalar_prefetch=1, grid=(nt,), in_specs=[recv, own, recv, own],
            out_specs=pl.BlockSpec((tr, width), lambda t, p: (p[1] * nt + t, 0))),
        out_shape=SDS((2 * rows, width), F32),
        compiler_params=_cp(("parallel",)),
    )(place, recv0, own0, recv1, own1)


def _half_exchange(gw, go, gathered, late, row):
    def body(gw_in, go_in, ga_in, late_ref, gw_ref, go_ref, ga_ref, send, recv, late_send, late_recv, late_local):
        del gw_in, go_in, ga_in
        x, y, c, _ = _place()
        starts, waits = _all_gather_rows(late_ref, ga_ref, pl.ds(row, late.shape[0]), late_send, late_recv,
                                         late_local)
        for go_ in starts:
            go_()
        mine = [pl.ds(c * (r.shape[0] // 2), r.shape[0] // 2) for r in (gw_ref, go_ref)]
        cps = [_remote(r.at[rows], r.at[rows], send.at[k], recv.at[k], (x, y, 1 - c))
               for k, (r, rows) in enumerate(zip((gw_ref, go_ref), mine))]
        for cp in cps:
            cp.start()
        for k, r in enumerate((gw_ref, go_ref)):
            theirs = pl.ds((1 - c) * (r.shape[0] // 2), r.shape[0] // 2)
            _remote(r.at[theirs], r.at[theirs], send.at[k], recv.at[k], (x, y, c)).wait_recv()
        for cp in cps:
            cp.wait_send()
        for wait in waits:
            wait()

    return pl.pallas_call(
        body, name="half_exchange", in_specs=[ANY] * 4, out_specs=[ANY] * 3,
        out_shape=[SDS(gw.shape, F32), SDS(go.shape, F32), SDS(gathered.shape, F32)],
        input_output_aliases={0: 0, 1: 1, 2: 2},
        scratch_shapes=[pltpu.SemaphoreType.DMA((2,)), pltpu.SemaphoreType.DMA((2,)),
                        pltpu.SemaphoreType.DMA((7,)), pltpu.SemaphoreType.DMA((7,)), pltpu.SemaphoreType.DMA],
        compiler_params=pltpu.CompilerParams(has_side_effects=True),
    )(gw, go, gathered, late)


def _adamw(w, g, m, v, name):
    rows, width = w.shape
    tr = min(rows, 256)

    def body(w_ref, g_ref, m_ref, v_ref, d_ref, nm_ref, nv_ref):
        gv = g_ref[...]
        nm = ADAM_B1 * m_ref[...] + (1.0 - ADAM_B1) * gv
        nv = ADAM_B2 * v_ref[...] + (1.0 - ADAM_B2) * (gv * gv)
        m_hat = nm / (1.0 - ADAM_B1 ** ADAM_STEP)
        v_hat = nv / (1.0 - ADAM_B2 ** ADAM_STEP)
        d_ref[...] = -ADAM_LR * (m_hat / (jnp.sqrt(v_hat) + ADAM_EPS) + ADAM_WD * w_ref[...])
        nm_ref[...] = nm
        nv_ref[...] = nv

    t = pl.BlockSpec((tr, width), lambda i: (i, 0))
    return pl.pallas_call(
        body, name=name, grid=(rows // tr,), in_specs=[t] * 4, out_specs=[t] * 3,
        out_shape=[SDS(w.shape, F32)] * 3, compiler_params=_cp(("parallel",)),
    )(w, g, m, v)


def _rowwise(a):
    return jnp.transpose(a, (2, 0, 1)).reshape(SHARD * D // LANE, LANE)


def _columns(ref, base=0):
    return jnp.concatenate([ref[pl.ds(base + c, LANE, stride=8), :].T for c in range(D // LANE)], axis=0)


def _shard_bf16(chip, w_rows):
    def body(j_ref, w_ref, o_ref, prev_ref):
        t = pl.program_id(0)
        cur = _columns(w_ref)

        @pl.when(t == 0)
        def _():
            prev_ref[...] = jnp.zeros_like(prev_ref)

        lane = _iota((D, LANE), 1)
        for s in range(4):
            @pl.when(j_ref[0] == s)
            def _():
                off = SHIFT * s
                moved = cur if s == 0 else jnp.where(lane < off, pltpu.roll(prev_ref[...], off, 1),
                                                     pltpu.roll(cur, off, 1))
                col = t * LANE + lane - off
                o_ref[...] = jnp.where((col >= 0) & (col < SHARD), moved, 0.0).astype(BF16)
        prev_ref[...] = cur

    return pl.pallas_call(
        body, name="shard_bf16",
        grid_spec=pltpu.PrefetchScalarGridSpec(
            num_scalar_prefetch=1, grid=(TILES + 1,),
            in_specs=[pl.BlockSpec((D, LANE), lambda t, j: (t, 0))],
            out_specs=pl.BlockSpec((D, LANE), lambda t, j: (0, t)),
            scratch_shapes=[pltpu.VMEM((D, LANE), F32)]),
        out_shape=SDS((D, WIN), BF16), compiler_params=_cp(("arbitrary",)),
    )(chip, w_rows)


def _whole_w_in(windows):
    tr = 256
    n = windows.shape[0]

    def body(g_ref, o_ref):
        lane = _iota((tr, LANE), 1)
        for s in range(n):
            first = TILES * s
            head = g_ref[s, :, :LANE]
            if s:
                tail = g_ref[s - 1, :, TILES * LANE:]
                head = jnp.where(lane < SHIFT * s, tail.astype(F32), head.astype(F32)).astype(BF16)
            o_ref[:, first * LANE:(first + 1) * LANE] = head
            o_ref[:, (first + 1) * LANE:(first + TILES) * LANE] = g_ref[s, :, LANE:TILES * LANE]
        o_ref[:, n * TILES * LANE:(n * TILES + 1) * LANE] = g_ref[n - 1, :, TILES * LANE:]
        o_ref[:, (n * TILES + 1) * LANE:] = jnp.zeros((tr, DP - (n * TILES + 1) * LANE), BF16)

    return pl.pallas_call(
        body, name="whole_w_in", grid=(D // tr,),
        in_specs=[pl.BlockSpec((n, tr, WIN), lambda t: (0, t, 0))], out_specs=pl.BlockSpec((tr, DP), lambda t: (t, 0)),
        out_shape=SDS((D, DP), BF16), compiler_params=_cp(("parallel",)),
    )(windows)


def _own_buffer(a, name):
    tr = 512
    block = pl.BlockSpec((tr, a.shape[1]), lambda t: (t, 0))

    def body(a_ref, o_ref):
        o_ref[...] = a_ref[...]

    return pl.pallas_call(
        body, name=name, grid=(a.shape[0] // tr,), in_specs=[block], out_specs=block,
        out_shape=SDS(a.shape, a.dtype), compiler_params=_cp(("parallel",)),
    )(a)


def _shard_of_window(chip, g_win):
    tr = 128

    def body(j_ref, g_ref, grad_ref):
        for s in range(4):
            @pl.when(j_ref[0] == s)
            def _():
                back = LANE - SHIFT * s
                from_this = _iota((tr, LANE), 1) < back

                def moved(t):
                    tile = g_ref[:, t * LANE:(t + 1) * LANE]
                    return pltpu.roll(tile, back, 1) if s else tile

                for t in range(TILES):
                    grad_ref[:, t * LANE:(t + 1) * LANE] = jnp.where(from_this, moved(t), moved(t + 1)) if s else moved(t)
                grad_ref[:, TILES * LANE:] = moved(TILES)[:, :SHARD - TILES * LANE]

    return pl.pallas_call(
        body, name="shard_of_window",
        grid_spec=pltpu.PrefetchScalarGridSpec(
            num_scalar_prefetch=1, grid=(D // tr,), in_specs=[pl.BlockSpec((tr, WIN), lambda t, j: (t, 0))],
            out_specs=pl.BlockSpec((tr, SHARD), lambda t, j: (t, 0))),
        out_shape=SDS((D, SHARD), F32), compiler_params=_cp(("parallel",)),
    )(chip, g_win)


def _adamw_in(w_rows, g, m_rows, v_rows):
    per_step = 2

    def body(w_ref, g_ref, m_ref, v_ref, d_ref, nm_ref, nv_ref):
        for a in range(per_step):
            cols = slice(a * LANE, (a + 1) * LANE)
            gv = g_ref[:, cols]
            nm = ADAM_B1 * _columns(m_ref, a * D) + (1.0 - ADAM_B1) * gv
            nv = ADAM_B2 * _columns(v_ref, a * D) + (1.0 - ADAM_B2) * (gv * gv)
            m_hat = nm / (1.0 - ADAM_B1 ** ADAM_STEP)
            v_hat = nv / (1.0 - ADAM_B2 ** ADAM_STEP)
            d_ref[:, cols] = -ADAM_LR * (m_hat / (jnp.sqrt(v_hat) + ADAM_EPS) + ADAM_WD * _columns(w_ref, a * D))
            nm_ref[:, cols] = nm
            nv_ref[:, cols] = nv

    tile = pl.BlockSpec((D, per_step * LANE), lambda t: (0, t))
    rows = pl.BlockSpec((per_step * D, LANE), lambda t: (t, 0))
    return pl.pallas_call(
        body, name="adamw_in", grid=(pl.cdiv(TILES + 1, per_step),), in_specs=[rows, tile, rows, rows],
        out_specs=[tile] * 3, out_shape=[SDS(g.shape, F32)] * 3, compiler_params=_cp(("parallel",)),
    )(w_rows, g, m_rows, v_rows)


def _rows128(a, rows):
    flat = a.reshape(-1)
    return jnp.pad(flat, (0, rows * LANE - flat.shape[0])).reshape(rows, LANE)


CONV_ROWS = 48


def _pack_small(conv_w, norm_pre, conv_b, ssm_norm, norm_post, dtb, alog, dsk, extra=None):
    cw_rows = CONV_ROWS if conv_w.shape[-1] == 1536 else 16
    extra = jnp.zeros((1, LANE), F32) if extra is None else _rows128(extra, 1)
    vec = jnp.concatenate([_rows128(dtb, 1), _rows128(alog, 1), _rows128(dsk, 1), extra, jnp.zeros((4, LANE), F32)],
                          axis=0)
    return jnp.concatenate([_rows128(conv_w, cw_rows), _rows128(norm_pre, 8), _rows128(conv_b, 16),
                            _rows128(ssm_norm, 8), _rows128(norm_post, 8), vec], axis=0)


def _unpack_small(p, cw_cols):
    cw_rows = CONV_ROWS if cw_cols == 1536 else 16
    o = cw_rows
    conv_w = p[:cw_rows].reshape(-1)[:4 * cw_cols].reshape(1, 4, cw_cols)
    norm_pre = p[o:o + 8].reshape(1, D)
    conv_b = p[o + 8:o + 24].reshape(-1)[:1536].reshape(1, 1536)
    ssm_norm = p[o + 24:o + 32].reshape(1, D)
    norm_post = p[o + 32:o + 40].reshape(1, D)
    vec = p[o + 40:o + 48]
    return conv_w, norm_pre, conv_b, ssm_norm, norm_post, vec[0:1, :NH], vec[1:2, :NH], vec[2:3, :NH], vec[3, 0]


def _pad_lanes(a):
    return jnp.pad(a, ((0, 0), (0, LANE - a.shape[1])))


class _GradReduce:
    LO, HI = ("qk", "v", "gz"), ("gz", "x")

    def __init__(self, xi, yi, ci):
        self.cidx = jnp.reshape(ci, (1,)).astype(jnp.int32)
        self.place = jnp.stack([2 * xi + yi, ci, xi, yi]).astype(jnp.int32)

    def pairs(self, dw_gz, dw_x, dw_out):
        self.hi = [dw_gz, dw_x]
        self.go = dw_out.reshape(4, D // 2, D)
        return _PairExchange(self.HI, self.hi, (2, 3), [self.go])

    def first(self, got):
        rw, ro = got
        self.pw_hi = _pair_sum_windows(self.cidx, self.HI, self.hi, (2, 3), rw, "pair_sum_hi")
        self.po = _pair_sum(self.cidx, self.go, ro, "pair_sum_out")
        return _ChipExchange([self.pw_hi, self.po], [1, None])

    def first_done(self, got):
        self.rw_hi, self.ro = got

    def second_pairs(self, dw_qk, dw_gz):
        self.lo = [dw_qk, None, dw_gz]
        return _PairExchange(self.LO, self.lo, (0, 1))

    def second(self, dw_v, got, small):
        rest = _PairExchange(self.LO, [None, dw_v, None], (0, 1))
        (rw,) = _exchange_call(rest, "pair_exchange_v", into=got)
        lo = [dw_v if a is None else a for a in self.lo]
        self.pw_lo = _pair_sum_windows(self.cidx, self.LO, lo, (0, 1), rw, "pair_sum_lo")
        return _Both(_ChipExchange([self.pw_lo], [0]), _SmallExchange(small))

    def second_done(self, got):
        self.rw_lo, self.small = got

    def result(self, late, row):
        half_in = _chip_sum_rows(self.place, self.rw_lo, self.pw_lo, self.rw_hi, self.pw_hi, "chip_sum_in")
        half_out = _chip_sum(self.place[0:2], self.ro, self.po, "chip_sum_out")
        return _half_exchange(half_in, half_out, self.small, late, row)


def kernel(x, norm_pre_w, w_in, conv_w, conv_b, dt_bias, a_log, d_skip, ssm_norm_w, w_out, norm_post_w, loss_target, m_norm_pre_w, m_w_in, m_conv_w, m_conv_b, m_dt_bias, m_a_log, m_d_skip, m_ssm_norm_w, m_w_out, m_norm_post_w, v_norm_pre_w, v_w_in, v_conv_w, v_conv_b, v_dt_bias, v_a_log, v_d_skip, v_ssm_norm_w, v_w_out, v_norm_post_w):
    xi, yi, ci = lax.axis_index("x"), lax.axis_index("y"), lax.axis_index("c")
    chip = 2 * xi + yi
    x2, tgt = x[0], loss_target[0]

    chip_idx = jnp.reshape(chip, (1,)).astype(jnp.int32)
    w_rows = _rowwise(w_in)
    w_all = _whole_w_in(_gather_weights(_shard_bf16(chip_idx, w_rows)))
    reduce = _GradReduce(xi, yi, ci)
    grad_x, dnw_pre = _local_step(x2, tgt, w_all, _LateGather(w_out[0].astype(BF16), conv_w[0]), norm_pre_w, conv_b,
                                  dt_bias, a_log, d_skip, ssm_norm_w, norm_post_w, reduce)
    g_win, g_out, small = reduce.result(_rows128(dnw_pre, D // LANE), CONV_ROWS)
    g_small = _slot_sum(small, "small_sum")
    g_cw, g_npre, g_cb, g_nssm, g_npost, g_dtb, g_alog, g_dsk, loss = _unpack_small(g_small, 1536)
    g_cw = lax.dynamic_slice_in_dim(g_cw, chip * 384, 384, axis=2)

    g_in = _shard_of_window(chip_idx, g_win)
    d_in, nm_in, nv_in = _adamw_in(w_rows, g_in, _rowwise(m_w_in), _rowwise(v_w_in))
    grad_x = _own_buffer(grad_x, "grad_x_copy")
    d_out, nm_out, nv_out = _adamw(w_out[0], g_out, m_w_out[0], v_w_out[0], "adamw_out")
    packed = [_pack_small(*t) for t in (
        (conv_w, norm_pre_w, conv_b, ssm_norm_w, norm_post_w, dt_bias, a_log, d_skip),
        (g_cw, g_npre, g_cb, g_nssm, g_npost, g_dtb, g_alog, g_dsk),
        (m_conv_w, m_norm_pre_w, m_conv_b, m_ssm_norm_w, m_norm_post_w, m_dt_bias, m_a_log, m_d_skip),
        (v_conv_w, v_norm_pre_w, v_conv_b, v_ssm_norm_w, v_norm_post_w, v_dt_bias, v_a_log, v_d_skip))]
    small_out = [_unpack_small(p, 384)[:8] for p in _adamw(*packed, "adamw_small")]

    def ordered(cw_, npre, cb_, nssm, npost, dtb_, alog_, dsk_, big_in, big_out):
        return [npre, big_in[None], cw_, cb_, dtb_, alog_, dsk_, nssm, big_out[None], npost]

    grads = ordered(g_cw, g_npre, g_cb, g_nssm, g_npost, g_dtb, g_alog, g_dsk, g_in, g_out)
    deltas = ordered(*small_out[0], d_in, d_out)
    new_m = ordered(*small_out[1], nm_in, nm_out)
    new_v = ordered(*small_out[2], nv_in, nv_out)
    return (loss, grad_x[None], *grads, *deltas, *new_m, *new_v)


def _local_step(x2, tgt, w_all, late, norm_pre_w, conv_b, dt_bias, a_log, d_skip, ssm_norm_w,
                norm_post_w, reduce=None):
    dtb, alog = _pad_lanes(dt_bias), _pad_lanes(a_log)
    d_b = jnp.repeat(d_skip, 64, axis=1)

    if isinstance(late, _LateGather):
        (proj, u), (gout, gcw) = _inproj_fwd(x2, norm_pre_w, w_all, late)
        w_out_all = gout.reshape(2 * D, D)
        cw_all = jnp.concatenate([gcw[0], gcw[1], gcw[2], gcw[3]], axis=1)
    else:
        proj, u = _inproj_fwd(x2, norm_pre_w, w_all)
        w_out_all, cw_all = late
    mix, attn_pre, lse = _attn_fwd(proj, 1, _attn_fwd(proj, 4, _attn_fwd(proj, 16)), final=True)
    mix, y_save, states, conv_out = _ssm_fwd(proj, mix, cw_all, conv_b, dtb, alog, d_b, ssm_norm_w)

    dy, dn_ssm, do, delta, dg, dw_out, dnw_post, loss_part = _outproj_loss(mix, w_out_all, x2, tgt, norm_post_w,
                                                                          attn_pre, proj)
    dz, dxbcdt, dcw, dcb, dvec, dnw_ssm = _ssm_bwd(proj, dn_ssm, y_save, states, conv_out, cw_all, dtb, alog, d_b,
                                                   ssm_norm_w)
    dw_gz, dw_x = _dw_pair(u, dg, dz, "dw_in_gz"), _dw(u, dxbcdt, "dw_in_xbcdt", X_COLS)
    acc = _attn_bwd(proj, do, lse, delta, 16, None, F32, reduce.pairs(dw_gz, dw_x, dw_out) if reduce else None)
    if reduce:
        acc, got = acc
    acc = _attn_bwd(proj, do, lse, delta, 4, acc, F32, reduce.first(got) if reduce else None)
    if reduce:
        acc, got = acc
        reduce.first_done(got)
    dq, dk, dv = _attn_bwd(proj, do, lse, delta, 1, acc, BF16)
    dw_qk = _dw_pair(u, dq, dk, "dw_in_qk")
    dw_v = _dw(u, dv, "dw_in_v", hosted=reduce.second_pairs(dw_qk, dw_gz) if reduce else None)
    if reduce:
        dw_v, got = dw_v

    def small(dnw_pre):
        return _pack_small(dcw, dnw_pre, dcb, dnw_ssm, dnw_post, dvec[0:1, :NH], dvec[1:2, :NH], dvec[2:3, :NH],
                           loss_part[:, :1])

    res = _inproj_bwd_dx([dq, dk, dv, dg, dz], dxbcdt, w_all, x2, dy, norm_pre_w,
                         reduce.second(dw_v, got, small(jnp.zeros((1, D), F32))) if reduce else None)
    if reduce:
        res, got = res
        reduce.second_done(got)
        return res
    grad_x, dnw_pre = res
    dw_all = jnp.concatenate([dw_qk, dw_v, dw_gz, dw_x], axis=1)
    return grad_x, small(dnw_pre), dw_all, dw_out
```

```python
import functools

import jax
import jax.numpy as jnp
from jax import lax
from jax.experimental import pallas as pl
from jax.experimental.pallas import tpu as pltpu

F32 = jnp.float32
BF16 = jnp.bfloat16
MESH = pl.DeviceIdType.MESH
SDS = jax.ShapeDtypeStruct
ANY = pl.BlockSpec(memory_space=pl.ANY)

S = 4096
D = 1024
DP = 7168
SHARD = 1668
OFF_G, OFF_Z = 3072, 4096
NH = 16
CH = 128
NC = S // CH
EPS = 1e-6
NEG = -1e30
LANE = 128
VMEM_LIMIT = 48 * 1024 * 1024

TILES = SHARD // LANE
WIN = (TILES + 1) * LANE
SHIFT = SHARD - TILES * LANE
SECTION_TILES = {"qk": (0, 16), "v": (16, 8), "gz": (24, 16), "x": (40, 13)}
X_COLS = SECTION_TILES["x"][1] * LANE

ADAM_LR, ADAM_B1, ADAM_B2, ADAM_EPS, ADAM_WD, ADAM_STEP = 0.001, 0.9, 0.999, 1e-08, 0.01, 10


def _cp(sem, **kw):
    return pltpu.CompilerParams(dimension_semantics=sem, vmem_limit_bytes=VMEM_LIMIT, **kw)


def _dot(a, b):
    return jnp.dot(a, b, preferred_element_type=F32)


def _dot_nt(a, b):
    return lax.dot_general(a, b, (((1,), (1,)), ((), ())), preferred_element_type=F32)


def _dot_tn(a, b):
    return lax.dot_general(a, b, (((0,), (0,)), ((), ())), preferred_element_type=F32)


def _pieces(x, n):
    out = []
    for _ in range(n):
        p = x.astype(BF16)
        out.append(p)
        x = x - p.astype(F32)
    return out


def _pick(x, sel, n=2):
    parts = [_dot(p, sel) for p in _pieces(x, n)]
    return functools.reduce(jnp.add, parts)


def _pick_left(sel, x, n=3):
    parts = [_dot(sel, p) for p in _pieces(x, n)]
    return functools.reduce(jnp.add, parts)


def _sigmoid(v):
    return 0.5 * jnp.tanh(0.5 * v) + 0.5


def _iota(shape, dim):
    return lax.broadcasted_iota(jnp.int32, shape, dim)


def _inproj_fwd(x, nw, w_all, hosted=None):
    tm, tn = 1024, 1024
    ni, nj = S // tm, DP // tn
    n_host = len(hosted.arrays) if hosted else 0

    def body(x_hbm, nw_ref, w_hbm, *refs):
        host_in, (proj_ref, u_ref), refs = refs[:n_host], refs[n_host:n_host + 2], refs[n_host + 2:]
        host_out, host_sems, (xbuf, wbuf, xsem, wsem) = refs[:n_host], refs[n_host:-4], refs[-4:]
        i, j = pl.program_id(0), pl.program_id(1)
        s = i * nj + j

        def x_copy(k):
            return pltpu.make_async_copy(x_hbm.at[pl.ds(pl.multiple_of(k * tm, tm), tm)], xbuf.at[k % 2], xsem.at[k % 2])

        def w_copy(t):
            cols = pl.ds(pl.multiple_of((t % nj) * tn, tn), tn)
            return pltpu.make_async_copy(w_hbm.at[:, cols], wbuf.at[t % 3], wsem.at[t % 3])

        @pl.when(s == 0)
        def _():
            x_copy(0).start()
            w_copy(0).start()
            w_copy(1).start()
            if hosted:
                hosted.start(host_in, host_out, host_sems)

        @pl.when(j == 0)
        def _():
            x_copy(i).wait()
            pl.when(i + 1 < ni)(lambda: x_copy(i + 1).start())
            xf = xbuf[i % 2]
            r = lax.rsqrt(jnp.mean(xf * xf, axis=-1, keepdims=True) + EPS)
            u_ref[...] = (xf * r * nw_ref[...]).astype(BF16)

        w_copy(s).wait()
        pl.when(s + 2 < ni * nj)(lambda: w_copy(s + 2).start())
        proj_ref[...] = _dot(u_ref[...], wbuf[s % 3])
        if hosted:
            pl.when((i == ni // 2) & (j == 0))(lambda: hosted.pass_on(host_in, host_out, host_sems))
            pl.when((i == ni - 1) & (j == nj - 1))(lambda: hosted.finish(host_in, host_out, host_sems))

    rings = [pltpu.VMEM((2, tm, D), F32), pltpu.VMEM((3, D, tn), BF16),
             pltpu.SemaphoreType.DMA((2,)), pltpu.SemaphoreType.DMA((3,))]
    outs = pl.pallas_call(
        body, name="inproj_fwd", grid=(ni, nj),
        in_specs=[ANY, pl.BlockSpec((1, D), lambda i, j: (0, 0)), ANY] + [ANY] * n_host,
        out_specs=[pl.BlockSpec((tm, tn), lambda i, j: (i, j)), pl.BlockSpec((tm, D), lambda i, j: (i, 0))]
        + [ANY] * n_host,
        out_shape=[SDS((S, DP), F32), SDS((S, D), BF16)] + (hosted.out_shape if hosted else []),
        scratch_shapes=(hosted.scratch if hosted else []) + rings,
        compiler_params=_cp(("arbitrary", "arbitrary")),
    )(x, nw, w_all, *(hosted.arrays if hosted else []))
    return (outs[:2], outs[2:]) if hosted else outs


ATTN_QB = {1: 16, 4: 4, 16: 1}


def _unit_rows(r, u, d):
    return pl.ds(r + d * CH * u, CH, stride=d) if d > 1 else pl.ds(CH * u, CH)


def _for_units(d, qb, fn):
    for r in range(d):
        for u in range(qb):
            fn(r, u)


def _attn_mask(has_prev):
    qi, kj = _iota((2 * CH, 2 * CH), 0) & (CH - 1), _iota((2 * CH, 2 * CH), 1)
    cur_ok = (kj >= CH) & (kj - CH <= qi)
    prev_ok = (kj < CH) & (kj >= qi)
    return cur_ok | (prev_ok & has_prev)


def _stack_heads(v, lane_a):
    return jnp.concatenate([jnp.where(lane_a, v, 0.0), jnp.where(lane_a, 0.0, v)], axis=0).astype(BF16)


def _attn_specs(d, qb):
    rows, prows = CH * d * qb, CH * d
    nb = S // rows
    steps = (NH // 2) * nb

    def at(t):
        t = jnp.minimum(t, steps - 1)
        return t % nb, t // nb

    def cur(off):
        return pl.BlockSpec((rows, LANE), lambda t: (at(t)[0], off + at(t)[1]))

    def prev(off):
        return pl.BlockSpec((prows, LANE), lambda t: (jnp.maximum(at(t)[0] * qb - 1, 0), off + at(t)[1]))

    lag = pl.BlockSpec((rows, LANE), lambda t: at(jnp.maximum(t - 1, 0)))
    return nb, steps, cur, prev, lag


def _gather16(src_ref, dense_ref, tmp_ref):
    for a in range(4):
        tmp_ref[...] = src_ref[pl.ds(a, 4 * CH, stride=4), :]
        for b in range(4):
            dense_ref[a + 4 * b] = tmp_ref[pl.ds(b, CH, stride=4), :]


def _scatter16(dense_ref, dst_ref, tmp_ref):
    for a in range(4):
        for b in range(4):
            tmp_ref[pl.ds(b, CH, stride=4), :] = dense_ref[a + 4 * b]
        dst_ref[pl.ds(a, 4 * CH, stride=4), :] = tmp_ref[...]


def _unit_index(r, u, d):
    return (r,) if d == 16 else (_unit_rows(r, u, d), slice(None))


def _unit_kv(p_ref, c_ref, r, u, d):
    prev = p_ref[_unit_index(r, 0, d)] if u == 0 else c_ref[_unit_index(r, u - 1, d)]
    return jnp.concatenate([prev, c_ref[_unit_index(r, u, d)]], axis=0).astype(BF16)


def _dense_scratch(d, n):
    return [pltpu.VMEM((16, CH, LANE), F32)] * n + [pltpu.VMEM((4 * CH, LANE), F32)] if d == 16 else []


def _attn_fwd(proj, d, prior=None, final=False):
    qb = ATTN_QB[d]
    nb, steps, cur, prev, _ = _attn_specs(d, qb)
    n_prior = 2 if prior is not None else 0
    n_in, n_out = 5 + n_prior + final, 2 + final
    assert not (d == 16 and (n_prior or final))

    def body(*refs):
        ins, outs, scratch = refs[:n_in], refs[n_in:n_in + n_out], refs[n_in + n_out:]
        if d == 16:
            tmp_ref = scratch[-1]
            for src, dense in zip(ins, scratch):
                _gather16(src, dense, tmp_ref)
            block_outs, ins, outs = outs, scratch[:n_in], scratch[n_in:n_in + n_out]
        q_ref, kp_ref, kc_ref, vp_ref, vc_ref = ins[:5]
        prior_refs = ins[5:5 + n_prior]
        if final:
            g_ref, (mix_ref, o_ref, l_ref) = ins[-1], outs
        else:
            o_ref, l_ref = outs
        i = pl.program_id(0) % nb
        lane_a = _iota((CH, LANE), 1) < 64
        mask_first, mask_rest = _attn_mask(i > 0), _attn_mask(True)

        def unit(r, u):
            at = _unit_index(r, u, d)
            q2 = _stack_heads(q_ref[at] * 0.125, lane_a)
            k2, v2 = _unit_kv(kp_ref, kc_ref, r, u, d), _unit_kv(vp_ref, vc_ref, r, u, d)
            s = jnp.where(mask_first if u == 0 else mask_rest, _dot_nt(q2, k2), NEG)
            m = jnp.max(s, axis=1, keepdims=True)
            p = jnp.exp(s - m)
            l = jnp.sum(p, axis=1, keepdims=True)
            o2 = _dot(p.astype(BF16), v2) / l
            lse2 = m + jnp.log(l)
            o = jnp.where(lane_a, o2[:CH], o2[CH:])
            lse = jnp.where(lane_a, lse2[:CH], lse2[CH:])
            if n_prior:
                o_a, l_a = prior_refs[0][at], prior_refs[1][at]
                top = jnp.maximum(l_a, lse)
                e_a, e_b = jnp.exp(l_a - top), jnp.exp(lse - top)
                tot = e_a + e_b
                o = (e_a * o_a + e_b * o) / tot
                lse = top + jnp.log(tot)
            o_ref[at] = o
            l_ref[at] = lse
            if final:
                g = g_ref[at]
                mix_ref[at] = (o * (g * _sigmoid(g))).astype(BF16)

        _for_units(d, qb, unit)
        if d == 16:
            for dense, dst in zip(outs, block_outs):
                _scatter16(dense, dst, tmp_ref)

    in_specs = [cur(0), prev(8), cur(8), prev(16), cur(16)] + [cur(0)] * n_prior
    args = [proj] * 5 + (list(prior) if n_prior else [])
    out_specs, out_shape = [cur(0), cur(0)], [SDS((S, D), F32), SDS((S, D), F32)]
    if final:
        assert d == 1
        in_specs.append(cur(OFF_G // LANE))
        args.append(proj)
        out_specs, out_shape = [cur(0)] + out_specs, [SDS((S, 2 * D), BF16)] + out_shape
    return pl.pallas_call(
        body, name=f"attn_fwd_d{d}", grid=(steps,),
        in_specs=in_specs, out_specs=out_specs, out_shape=out_shape,
        scratch_shapes=_dense_scratch(d, n_in + n_out),
        compiler_params=_cp(("parallel",)),
    )(*args)


def _attn_bwd(proj, do, lse, delta, d, acc, out_dtype, hosted=None):
    qb = ATTN_QB[d]
    nb, steps, cur, prev, lag = _attn_specs(d, qb)
    has_acc = acc is not None
    n_in = 11 if has_acc else 8
    n_host, n_host_out = (len(hosted.arrays), len(hosted.out_shape)) if hosted else (0, 0)
    assert not (d == 16 and (has_acc or out_dtype != F32))
    rows = CH * d * qb
    carry = (2, 16, CH, LANE) if d == 16 else (2, rows, LANE)

    def body(*refs):
        ins, host_in, refs = refs[:n_in], refs[n_in:n_in + n_host], refs[n_in + n_host:]
        (dq_ref, dk_ref, dv_ref), host_out, scratch = refs[:3], refs[3:3 + n_host_out], refs[3 + n_host_out:]
        if hosted:
            scratch, host_sems = scratch[:-len(hosted.scratch)], scratch[-len(hosted.scratch):]
        ck_ref, cv_ref = scratch[:2]
        dq_f32 = dq_ref if out_dtype == F32 else scratch[2]
        t = pl.program_id(0)
        i = t % nb
        if hosted:
            pl.when(t == 0)(lambda: hosted.start(host_in, host_out, host_sems))
        if d == 16:
            dense, dq_f32, tmp_ref = scratch[2:2 + n_in], scratch[2 + n_in], scratch[-1]

            @pl.when(t < steps)
            def _():
                for src, dst in zip(ins, dense):
                    _gather16(src, dst, tmp_ref)

            ins = dense
        q_ref, kp_ref, kc_ref, vp_ref, vc_ref, do_ref, lse_ref, dl_ref = ins[:8]
        if has_acc:
            aq_ref, ak_ref, av_ref = ins[8:11]
        slot = t & 1
        now_k, now_v, old_k, old_v = ck_ref.at[slot], cv_ref.at[slot], ck_ref.at[1 - slot], cv_ref.at[1 - slot]
        lane_a = _iota((CH, LANE), 1) < 64
        mask_first, mask_rest = _attn_mask(i > 0), _attn_mask(True)

        @pl.when(t == 0)
        def _():
            ck_ref[1] = jnp.zeros(carry[1:], F32)
            cv_ref[1] = jnp.zeros(carry[1:], F32)

        def unit(r, u):
            at = _unit_index(r, u, d)
            q2 = _stack_heads(q_ref[at] * 0.125, lane_a)
            do2 = _stack_heads(do_ref[at], lane_a)
            k2, v2 = _unit_kv(kp_ref, kc_ref, r, u, d), _unit_kv(vp_ref, vc_ref, r, u, d)
            lsev, dlv = lse_ref[at], dl_ref[at]
            lse2 = jnp.concatenate([lsev[:, 0:1], lsev[:, 64:65]], axis=0)
            dl2 = jnp.concatenate([dlv[:, 0:1], dlv[:, 64:65]], axis=0)
            p = jnp.exp(jnp.where(mask_first if u == 0 else mask_rest, _dot_nt(q2, k2), NEG) - lse2)
            ds = (p * (_dot_nt(do2, v2) - dl2)).astype(BF16)
            dq2 = _dot(ds, k2)
            dk2 = _dot_tn(ds, q2)
            dv2 = _dot_tn(p.astype(BF16), do2)
            dq = jnp.where(lane_a, dq2[:CH], dq2[CH:]) * 0.125
            if has_acc:
                dq = dq + aq_ref[at]
            dq_f32[at] = dq
            if u == 0:
                before = _unit_index(r, qb - 1, d)
                old_k[before] += dk2[:CH]
                old_v[before] += dv2[:CH]
            else:
                before = _unit_index(r, u - 1, d)
                now_k[before] += dk2[:CH]
                now_v[before] += dv2[:CH]
            now_k[at] = dk2[CH:]
            now_v[at] = dv2[CH:]

        @pl.when(t < steps)
        def _():
            _for_units(d, qb, unit)
            if d == 16:
                _scatter16(dq_f32, dq_ref, tmp_ref)
            elif out_dtype != F32:
                dq_ref[...] = dq_f32[...].astype(out_dtype)

        if d == 16:
            _scatter16(old_k, dk_ref, tmp_ref)
            _scatter16(old_v, dv_ref, tmp_ref)
        else:
            dk, dv = old_k[...], old_v[...]
            if has_acc:
                dk, dv = dk + ak_ref[...], dv + av_ref[...]
            dk_ref[...] = dk.astype(out_dtype)
            dv_ref[...] = dv.astype(out_dtype)
        if hosted:
            pl.when(t == steps)(lambda: hosted.finish(host_in, host_out, host_sems))

    in_specs = [cur(0), prev(8), cur(8), prev(16), cur(16), cur(0), cur(0), cur(0)]
    args = [proj, proj, proj, proj, proj, do, lse, delta]
    if has_acc:
        in_specs += [cur(0), lag, lag]
        args += list(acc)
    scratch = [pltpu.VMEM(carry, F32), pltpu.VMEM(carry, F32)]
    if d == 16:
        scratch += _dense_scratch(d, n_in + 1)
    elif out_dtype != F32:
        scratch.append(pltpu.VMEM((rows, LANE), F32))
    out_specs, out_shape = [cur(0), lag, lag], [SDS((S, D), out_dtype)] * 3
    if hosted:
        args += hosted.arrays
        in_specs += [ANY] * n_host
        out_specs += [ANY] * n_host_out
        out_shape += hosted.out_shape
        scratch += hosted.scratch
    outs = pl.pallas_call(
        body, name=f"attn_bwd_d{d}", grid=(steps + 1,),
        in_specs=in_specs, out_specs=out_specs, out_shape=out_shape,
        scratch_shapes=scratch, compiler_params=_cp(("arbitrary",)),
    )(*args)
    return (outs[:3], outs[3:]) if hosted else outs


def _conv_taps(cur, prev8, first):
    row8 = _iota(prev8.shape, 0)
    prev8 = jnp.where(first, 0.0, prev8)
    taps = []
    for s in (3, 2, 1):
        rolled = pltpu.roll(cur, s, 0)
        head = jnp.where(row8 < s, pltpu.roll(prev8, s, 0), rolled[:8])
        taps.append(jnp.concatenate([head, rolled[8:]], axis=0))
    return taps + [cur]


def _conv(taps, w, b):
    acc = b + w[0:1, :] * taps[0]
    for k in (1, 2, 3):
        acc = acc + w[k:k + 1, :] * taps[k]
    return acc


def _expand():
    return (_iota((LANE, D), 1) // 64 == _iota((LANE, D), 0)).astype(BF16)


def _reduce():
    return (_iota((D, LANE), 0) // 64 == _iota((D, LANE), 1)).astype(BF16)


def _ssd_common(xs_c, bc_c, dt_raw, dtb, alog):
    head_lane = _iota((CH, LANE), 1) < NH
    xs = xs_c * _sigmoid(xs_c)
    bc = bc_c * _sigmoid(bc_c)
    pre = dt_raw + dtb
    dt = jnp.where(head_lane, jnp.maximum(pre, 0.0) + jnp.log(1.0 + jnp.exp(-jnp.abs(pre))), 0.0)
    a_row = jnp.where(head_lane[0:1], -jnp.exp(alog), 0.0)
    tri = (_iota((CH, CH), 1) <= _iota((CH, CH), 0)).astype(BF16)
    cs = _pick_left(tri, dt * a_row)
    cs_last = cs[CH - 1:CH, :]
    wide = _pick(jnp.concatenate([dt, jnp.exp(cs), jnp.exp(cs_last - cs)], axis=0), _expand())
    dt_b, e_b, f_b = wide[:CH], wide[CH:2 * CH], wide[2 * CH:]
    return dict(xs=xs, bc=bc, pre=pre, dt=dt, a_row=a_row, cs=cs, cs_t=cs.T, dt_b=dt_b, e_b=e_b, f_b=f_b,
                t_b=e_b[CH - 1:CH, :])


def _groups(bc):
    bcb = bc.astype(BF16)
    return [bcb[:, 0:128], bcb[:, 128:256]], [bcb[:, 256:384], bcb[:, 384:512]]


def _decay(q, h, tril):
    seg = q["cs"][:, h:h + 1] - q["cs_t"][h:h + 1, :]
    return jnp.exp(jnp.where(tril, seg, NEG))


def _ssm_fwd(proj, mix, cw, cb, dtb, alog, d_b, nw):
    def body(xs_ref, xsp_ref, bc_ref, bcp_ref, dt_ref, z_ref, cw_ref, cb_ref, dtb_ref, alog_ref, db_ref, nw_ref,
             mix_in_ref, mix_ref, y_ref, st_ref, conv_ref, h_ref):
        del mix_in_ref
        i = pl.program_id(0)

        @pl.when(i == 0)
        def _():
            h_ref[...] = jnp.zeros_like(h_ref)

        cw, cb = cw_ref[...], cb_ref[...]
        xs_c = _conv(_conv_taps(xs_ref[...], xsp_ref[...], i == 0), cw[:, :D], cb[:, :D])
        bc_c = _conv(_conv_taps(bc_ref[...], bcp_ref[...], i == 0), cw[:, D:], cb[:, D:])
        conv_ref[:, :D] = xs_c
        conv_ref[:, D:] = bc_c
        q = _ssd_common(xs_c, bc_c, dt_ref[...], dtb_ref[...], alog_ref[...])
        bg, cg = _groups(q["bc"])
        xs = q["xs"]
        xdt = xs * q["dt_b"]
        xdt_b = xdt.astype(BF16)
        h_in = h_ref[...]
        st_ref[...] = h_in
        hb = h_in.astype(BF16)
        tril = _iota((CH, CH), 1) <= _iota((CH, CH), 0)
        lane_a = _iota((CH, LANE), 1) < 64
        cbm = [_dot_nt(cg[g], bg[g]) for g in range(2)]
        pairs = []
        for hp in range(NH // 2):
            xp = xdt_b[:, hp * LANE:(hp + 1) * LANE]
            ya = _dot((cbm[hp // 4] * _decay(q, 2 * hp, tril)).astype(BF16), xp)
            yb = _dot((cbm[hp // 4] * _decay(q, 2 * hp + 1, tril)).astype(BF16), xp)
            pairs.append(jnp.where(lane_a, ya, yb))
        y_diag = jnp.concatenate(pairs, axis=1)
        y_off = jnp.concatenate([_dot(cg[g], hb[:, g * 512:(g + 1) * 512]) for g in range(2)], axis=1) * q["e_b"]
        y = y_diag + y_off + db_ref[...] * xs
        y_ref[...] = y
        xf = (xdt * q["f_b"]).astype(BF16)
        h_ref[...] = q["t_b"] * h_in + jnp.concatenate(
            [_dot_tn(bg[g], xf[:, g * 512:(g + 1) * 512]) for g in range(2)], axis=1)
        z = z_ref[...]
        yz = y * (z * _sigmoid(z))
        outs = []
        for g in range(2):
            v = yz[:, g * 512:(g + 1) * 512]
            outs.append(v * lax.rsqrt(jnp.mean(v * v, axis=-1, keepdims=True) + EPS))
        mix_ref[...] = (jnp.concatenate(outs, axis=1) * nw_ref[...]).astype(BF16)

    def col(width, blk, prev=False):
        if prev:
            return pl.BlockSpec((8, width), lambda i: (jnp.maximum(i * (CH // 8) - 1, 0), blk))
        return pl.BlockSpec((CH, width), lambda i: (i, blk))

    def full(a):
        return pl.BlockSpec(a.shape, lambda i: (0,) * a.ndim)

    return pl.pallas_call(
        body, name="ssm_fwd", grid=(NC,),
        in_specs=[col(D, 5), col(D, 5, True), col(512, 12), col(512, 12, True), col(LANE, 52), col(D, 4),
                  full(cw), full(cb), full(dtb), full(alog), full(d_b), full(nw), ANY],
        out_specs=[col(D, 1), col(D, 0), pl.BlockSpec((None, CH, D), lambda i: (i, 0, 0)), col(D + 512, 0)],
        out_shape=[SDS((S, 2 * D), BF16), SDS((S, D), F32), SDS((NC, CH, D), F32), SDS((S, D + 512), F32)],
        scratch_shapes=[pltpu.VMEM((CH, D), F32)],
        input_output_aliases={12: 0},
        compiler_params=_cp(("arbitrary",)),
    )(proj, proj, proj, proj, proj, proj, cw, cb, dtb, alog, d_b, nw, mix)


def _ssm_bwd(proj, dn, y_save, states, conv_out, cw, dtb, alog, d_b, nw):
    def body(xs_ref, bc_ref, dt_ref, z_ref, dn_ref, y_ref, st_ref, conv_ref,
             cw_ref, dtb_ref, alog_ref, db_ref, nw_ref,
             dz_ref, dx_ref, dcw_ref, dcb_ref, dsm_ref, dnw_ref, dh_ref, nxs_ref, nbc_ref):
        i = pl.program_id(0)
        ci = NC - 1 - i

        @pl.when(i == 0)
        def _():
            for ref in (dcw_ref, dcb_ref, dsm_ref, dnw_ref, dh_ref, nxs_ref, nbc_ref):
                ref[...] = jnp.zeros_like(ref)

        cw = cw_ref[...]
        xs_c, bc_c = conv_ref[:, :D], conv_ref[:, D:]
        q = _ssd_common(xs_c, bc_c, dt_ref[...], dtb_ref[...], alog_ref[...])
        bg, cg = _groups(q["bc"])
        xs, dt_b, e_b, f_b, t_b = q["xs"], q["dt_b"], q["e_b"], q["f_b"], q["t_b"]
        xdt = xs * dt_b
        xdt_b = xdt.astype(BF16)
        h_in = st_ref[...]
        hb = h_in.astype(BF16)
        dh_new = dh_ref[...]
        dhb = dh_new.astype(BF16)
        red = _reduce()

        z, y, dn, nw_v = z_ref[...], y_ref[...], dn_ref[...], nw_ref[...]
        sig = _sigmoid(z)
        sz = z * sig
        yz = y * sz
        gdn = dn * nw_v
        dyz, dnw = [], []
        for g in range(2):
            v, gv = yz[:, g * 512:(g + 1) * 512], gdn[:, g * 512:(g + 1) * 512]
            r = lax.rsqrt(jnp.mean(v * v, axis=-1, keepdims=True) + EPS)
            dnw.append(dn[:, g * 512:(g + 1) * 512] * v * r)
            dyz.append(r * (gv - v * (r * r) * jnp.mean(gv * v, axis=-1, keepdims=True)))
        dyz = jnp.concatenate(dyz, axis=1)
        dnw_ref[...] += jnp.sum(jnp.concatenate(dnw, axis=1), axis=0, keepdims=True)
        dy = dyz * sz
        dz_ref[...] = (dyz * y * (sig * (1.0 + z * (1.0 - sig)))).astype(BF16)
        dy_b = dy.astype(BF16)

        tril = _iota((CH, CH), 1) <= _iota((CH, CH), 0)
        lane_a = _iota((CH, LANE), 1) < 64
        cbm = [_dot_nt(cg[g], bg[g]) for g in range(2)]
        dcbm = [jnp.zeros((CH, CH), F32), jnp.zeros((CH, CH), F32)]
        seg_rows = jnp.zeros((CH, LANE), F32)
        seg_cols = jnp.zeros((LANE, CH), F32)
        row_id, col_id = _iota((CH, LANE), 0), _iota((CH, LANE), 1)
        dx_pairs = []
        for hp in range(NH // 2):
            g = hp // 4
            xp = xdt_b[:, hp * LANE:(hp + 1) * LANE]
            dyp_f = dy[:, hp * LANE:(hp + 1) * LANE]
            dyp = dy_b[:, hp * LANE:(hp + 1) * LANE]
            halves = []
            for k in range(2):
                h = 2 * hp + k
                lane = lane_a if k == 0 else jnp.logical_not(lane_a)
                dec = _decay(q, h, tril)
                gm = cbm[g] * dec
                dgm = _dot_nt(jnp.where(lane, dyp_f, 0.0).astype(BF16), xp)
                dcbm[g] = dcbm[g] + dgm * dec
                prod = dgm * gm
                seg_rows = jnp.where(col_id == h, jnp.sum(prod, axis=1, keepdims=True), seg_rows)
                seg_cols = jnp.where(row_id == h, jnp.sum(prod, axis=0, keepdims=True), seg_cols)
                halves.append(_dot_tn(gm.astype(BF16), dyp))
            dx_pairs.append(jnp.where(lane_a, halves[0], halves[1]))
        dxdt_diag = jnp.concatenate(dx_pairs, axis=1)

        qv = jnp.concatenate([_dot(bg[g], dhb[:, g * 512:(g + 1) * 512]) for g in range(2)], axis=1)
        y_off = jnp.concatenate([_dot(cg[g], hb[:, g * 512:(g + 1) * 512]) for g in range(2)], axis=1) * e_b
        xfq = xdt * f_b * qv
        dxdt = dxdt_diag + f_b * qv
        tdt = jnp.sum(dh_new * h_in, axis=0, keepdims=True) * t_b
        per_head = _pick(jnp.concatenate([xfq, dy * y_off, dxdt * xs, dy * xs, jnp.broadcast_to(tdt, (8, D))],
                                         axis=0), red)
        fdf, dyoff_h, dxdtxs_h, dyxs_h = [per_head[k * CH:(k + 1) * CH] for k in range(4)]
        dcs = seg_rows - seg_cols.T + dyoff_h - fdf
        last = per_head[4 * CH:4 * CH + 1] + jnp.sum(fdf, axis=0, keepdims=True)
        dcs = dcs + jnp.where(_iota((CH, LANE), 0) == CH - 1, last, 0.0)
        tri_t = (_iota((CH, CH), 1) >= _iota((CH, CH), 0)).astype(BF16)
        da = _pick_left(tri_t, dcs)
        ddt = da * q["a_row"] + dxdtxs_h
        dxs = dxdt * dt_b + db_ref[...] * dy
        ddt_raw = ddt * _sigmoid(q["pre"])
        dsm_ref[0:1, :] += jnp.sum(ddt_raw, axis=0, keepdims=True)
        dsm_ref[1:2, :] += jnp.sum(da * q["dt"], axis=0, keepdims=True) * q["a_row"]
        dsm_ref[2:3, :] += jnp.sum(dyxs_h, axis=0, keepdims=True)
        edy = (e_b * dy).astype(BF16)
        xf = (xdt * f_b).astype(BF16)
        dbs, dcs_g, dhs = [], [], []
        for g in range(2):
            sl = slice(g * 512, (g + 1) * 512)
            dcb_b = dcbm[g].astype(BF16)
            dcs_g.append(_dot(dcb_b, bg[g]) + _dot_nt(edy[:, sl], hb[:, sl]))
            dbs.append(_dot_tn(dcb_b, cg[g]) + _dot_nt(xf[:, sl], dhb[:, sl]))
            dhs.append(_dot_tn(cg[g], edy[:, sl]))
        dh_ref[...] = t_b * dh_new + jnp.concatenate(dhs, axis=1)
        dbc = jnp.concatenate(dbs + dcs_g, axis=1)

        def conv_bwd(dact, pre, x_raw, w, nxt_ref, lo):
            s = _sigmoid(pre)
            dconv = dact * (s * (1.0 + pre * (1.0 - s)))
            nxt8 = nxt_ref[...]
            row8 = _iota(nxt8.shape, 0)
            hi = lo + dconv.shape[1]
            dcb_ref[:, lo:hi] += jnp.sum(dconv, axis=0, keepdims=True)
            later = [dconv]
            for s_ in (1, 2, 3):
                rolled = pltpu.roll(dconv, CH - s_, 0)
                tail = jnp.where(row8 >= 8 - s_, pltpu.roll(nxt8, 8 - s_, 0), rolled[CH - 8:])
                later.append(jnp.concatenate([rolled[:CH - 8], tail], axis=0))
            dx = None
            for s_, up in enumerate(later):
                k = 3 - s_
                dcw_ref[k:k + 1, lo:hi] += jnp.sum(up * x_raw, axis=0, keepdims=True)
                dx = w[k:k + 1, :] * up if dx is None else dx + w[k:k + 1, :] * up
            nxt_ref[...] = dconv[:8]
            return dx

        dx_ref[:, 0:D] = conv_bwd(dxs, xs_c, xs_ref[...], cw[:, :D], nxs_ref, 0).astype(BF16)
        dx_ref[:, D:D + 512] = conv_bwd(dbc, bc_c, bc_ref[...], cw[:, D:], nbc_ref, D).astype(BF16)
        dx_ref[:, D + 512:D + 640] = ddt_raw.astype(BF16)
        dx_ref[:, D + 640:] = jnp.zeros((CH, D - 640), BF16)

    def col(width, blk):
        return pl.BlockSpec((CH, width), lambda i: (NC - 1 - i, blk))

    def full(a):
        return pl.BlockSpec(a.shape, lambda i: (0,) * len(a.shape))

    acc_shapes = [SDS((4, 1536), F32), SDS((1, 1536), F32), SDS((8, LANE), F32), SDS((1, D), F32)]
    return pl.pallas_call(
        body, name="ssm_bwd", grid=(NC,),
        in_specs=[col(D, 5), col(512, 12), col(LANE, 52), col(D, 4),
                  col(D, 0), col(D, 0), pl.BlockSpec((None, CH, D), lambda i: (NC - 1 - i, 0, 0)), col(D + 512, 0),
                  full(cw), full(dtb), full(alog), full(d_b), full(nw)],
        out_specs=[col(D, 0), col(2 * D, 0)] + [full(a) for a in acc_shapes],
        out_shape=[SDS((S, D), BF16), SDS((S, 2 * D), BF16)] + acc_shapes,
        scratch_shapes=[pltpu.VMEM((CH, D), F32), pltpu.VMEM((8, D), F32), pltpu.VMEM((8, 512), F32)],
        compiler_params=_cp(("arbitrary",)),
    )(proj, proj, proj, proj, dn, y_save, states, conv_out, cw, dtb, alog, d_b, nw)


def _outproj_loss(mix, w_out, x, tgt, nw, attn_pre, proj):
    tm = 256

    def body(mix_ref, w_ref, x_ref, t_ref, nw_ref, pre_ref, g_ref,
             dy_ref, dn_ref, do_ref, delta_ref, dg_ref, dw_ref, dnw_ref, loss_ref):
        @pl.when(pl.program_id(0) == 0)
        def _():
            dw_ref[...] = jnp.zeros_like(dw_ref)
            dnw_ref[...] = jnp.zeros_like(dnw_ref)
            loss_ref[...] = jnp.zeros_like(loss_ref)

        mixv, w = mix_ref[...], w_ref[...]
        out = _dot(mixv, w)
        r = lax.rsqrt(jnp.mean(out * out, axis=-1, keepdims=True) + EPS)
        nh = out * r
        nw_v = nw_ref[...]
        err = x_ref[...] + nh * nw_v - t_ref[...]
        loss_ref[...] += 0.5 * jnp.sum(jnp.mean(err * err, axis=-1, keepdims=True), axis=0, keepdims=True)
        dy = err * (1.0 / D)
        dy_ref[...] = dy
        dnw_ref[...] += jnp.sum(dy * nh, axis=0, keepdims=True)
        gdn = dy * nw_v
        dout = (r * (gdn - nh * jnp.mean(gdn * nh, axis=-1, keepdims=True))).astype(BF16)
        dmix = _dot_nt(dout, w)
        dw_ref[...] += _dot_tn(mixv, dout)
        dn_ref[...] = dmix[:, D:]
        dm, g, pre_v = dmix[:, :D], g_ref[...], pre_ref[...]
        sig = _sigmoid(g)
        do = dm * (g * sig)
        do_ref[...] = do
        dg_ref[...] = (dm * pre_v * (sig * (1.0 + g * (1.0 - sig)))).astype(BF16)
        prod = do * pre_v
        same_head = (_iota((LANE, LANE), 0) // 64 == _iota((LANE, LANE), 1) // 64).astype(BF16)
        for cb in range(D // LANE):
            delta_ref[:, cb * LANE:(cb + 1) * LANE] = _pick(prod[:, cb * LANE:(cb + 1) * LANE], same_head)

    row = lambda w: pl.BlockSpec((tm, w), lambda i: (i, 0))
    full = lambda s: pl.BlockSpec(s, lambda i: (0, 0))
    return pl.pallas_call(
        body, name="outproj_loss", grid=(S // tm,),
        in_specs=[row(2 * D), full((2 * D, D)), row(D), row(D), full((1, D)), row(D),
                  pl.BlockSpec((tm, D), lambda i: (i, OFF_G // D))],
        out_specs=[row(D), row(D), row(D), row(D), row(D), full((2 * D, D)), full((1, D)), full((1, LANE))],
        out_shape=[SDS((S, D), F32)] * 4 + [SDS((S, D), BF16), SDS((2 * D, D), F32), SDS((1, D), F32),
                                            SDS((1, LANE), F32)],
        compiler_params=_cp(("arbitrary",)),
    )(mix, w_out, x, tgt, nw, attn_pre, proj)


def _inproj_bwd_dx(srcs, dxbcdt, w_all, x, dy, nw, hosted=None):
    tm = 512
    nk = DP // D
    n_host, n_host_out = (len(hosted.arrays), len(hosted.out_shape)) if hosted else (0, 0)

    def body(*refs):
        src_refs = refs[:nk]
        w_hbm, x_ref, dy_ref, nw_ref = refs[nk:nk + 4]
        host_in, refs = refs[nk + 4:nk + 4 + n_host], refs[nk + 4 + n_host:]
        gx_ref, dnw_ref = refs[:2]
        host_out, host_sems, (w_ref, w_sem) = refs[2:2 + n_host_out], refs[2 + n_host_out:-2], refs[-2:]
        i = pl.program_id(0)

        def w_copy(k):
            cols = slice(k * D, (k + 1) * D)
            return pltpu.make_async_copy(w_hbm.at[:, cols], w_ref.at[:, cols], w_sem.at[k])

        @pl.when(i == 0)
        def _():
            for k in range(nk):
                w_copy(k).start()
            if hosted:
                hosted.start(host_in, host_out, host_sems)
            dnw_ref[...] = jnp.zeros_like(dnw_ref)

        du = None
        for k, ref in enumerate(src_refs):
            pl.when(i == 0)(w_copy(k).wait)
            width = min(D, 5 * D + X_COLS - k * D)
            part = _dot_nt(ref[:, :width], w_ref[:, k * D:k * D + width])
            du = part if du is None else du + part
        xf, nw_v = x_ref[...], nw_ref[...]
        r = lax.rsqrt(jnp.mean(xf * xf, axis=-1, keepdims=True) + EPS)
        xh = xf * r
        dnw_ref[...] += jnp.sum(du * xh, axis=0, keepdims=True)
        gdu = du * nw_v
        gx_ref[...] = r * (gdu - xh * jnp.mean(gdu * xh, axis=-1, keepdims=True)) + dy_ref[...]

        if hosted:
            pl.when(i == S // tm - 1)(lambda: hosted.finish(host_in, host_out, host_sems))

    row = pl.BlockSpec((tm, D), lambda i: (i, 0))
    row1 = pl.BlockSpec((tm, D), lambda i: (i, 1))
    one = pl.BlockSpec((1, D), lambda i: (0, 0))
    args = [*srcs, dxbcdt, dxbcdt, w_all, x, dy, nw]
    in_specs = [row] * len(srcs) + [row, row1, ANY, row, row, one]
    out_specs, out_shape, scratch = [row, one], [SDS((S, D), F32), SDS((1, D), F32)], []
    if hosted:
        args += hosted.arrays
        in_specs += [ANY] * n_host
        out_specs += [ANY] * n_host_out
        out_shape += hosted.out_shape
        scratch += hosted.scratch
    scratch += [pltpu.VMEM((D, DP), BF16), pltpu.SemaphoreType.DMA((nk,))]
    outs = pl.pallas_call(
        body, name="inproj_bwd_dx", grid=(S // tm,),
        in_specs=in_specs, out_specs=out_specs, out_shape=out_shape, scratch_shapes=scratch,
        compiler_params=_cp(("arbitrary",)),
    )(*args)
    return (outs[:2], outs[2:]) if hosted else outs


def _dw(u, dsec, name, width=D, hosted=None):
    ts = 1024
    n_host, n_host_out = (len(hosted.arrays), len(hosted.out_shape)) if hosted else (0, 0)

    def body(u_ref, d_ref, *refs):
        host_in, o_ref, refs = refs[:n_host], refs[n_host], refs[n_host + 1:]
        host_out, host_sems = refs[:n_host_out], refs[n_host_out:]
        i = pl.program_id(0)

        @pl.when(i == 0)
        def _():
            if hosted:
                hosted.start(host_in, host_out, host_sems)
            o_ref[...] = jnp.zeros_like(o_ref)

        o_ref[...] += _dot_tn(u_ref[...], d_ref[...])
        if hosted:
            pl.when(i == S // ts - 1)(lambda: hosted.finish(host_in, host_out, host_sems))

    outs = pl.pallas_call(
        body, name=name, grid=(S // ts,),
        in_specs=[pl.BlockSpec((ts, D), lambda i: (i, 0)), pl.BlockSpec((ts, width), lambda i: (i, 0))]
        + [ANY] * n_host,
        out_specs=[pl.BlockSpec((D, width), lambda i: (0, 0))] + [ANY] * n_host_out,
        out_shape=[SDS((D, width), F32)] + (hosted.out_shape if hosted else []),
        scratch_shapes=hosted.scratch if hosted else [],
        compiler_params=_cp(("arbitrary",)),
    )(u, dsec, *(hosted.arrays if hosted else []))
    return (outs[0], outs[1:]) if hosted else outs[0]


def _dw_pair(u, da, db, name):
    ts = 1024
    last = S // ts - 1

    def body(u_ref, a_ref, b_ref, o_ref):
        j = pl.program_id(0)

        @pl.when(pl.program_id(1) == 0)
        def _():
            o_ref[...] = jnp.zeros_like(o_ref)

        for k, d_ref in enumerate((a_ref, b_ref)):
            @pl.when(j == k)
            def _():
                o_ref[...] += _dot_tn(u_ref[...], d_ref[...])

    return pl.pallas_call(
        body, name=name, grid=(2, S // ts),
        in_specs=[pl.BlockSpec((ts, D), lambda j, i: (i, 0)),
                  pl.BlockSpec((ts, D), lambda j, i: (jnp.where(j == 0, i, last), 0)),
                  pl.BlockSpec((ts, D), lambda j, i: (jnp.where(j == 1, i, 0), 0))],
        out_specs=pl.BlockSpec((D, D), lambda j, i: (0, j)),
        out_shape=SDS((D, 2 * D), F32),
        compiler_params=_cp(("arbitrary", "arbitrary")),
    )(u, da, db)


def _place():
    x, y, c = lax.axis_index("x"), lax.axis_index("y"), lax.axis_index("c")
    return x, y, c, 2 * x + y


def _chip_of(x, y, k):
    px = 1 - x if k & 2 else x
    py = 1 - y if k & 1 else y
    return px, py, 2 * px + py


def _remote(src, dst, send_sem, recv_sem, dev):
    return pltpu.make_async_remote_copy(src_ref=src, dst_ref=dst, send_sem=send_sem, recv_sem=recv_sem,
                                        device_id=dev, device_id_type=MESH)


def _gather_weights(w_in_b):
    half = w_in_b.shape[0] // 2
    quarter = half // 2

    def body(src, dst, send, recv):
        x, y, c, j = _place()
        me, sib = (x, y, c), (x, y, 1 - c)
        nbr = {"x": _chip_of(x, y, 2), "y": _chip_of(x, y, 1)}
        diag = _chip_of(x, y, 3)[2]
        started, arrivals = [], []

        def rows(n_quarter=None, sibling=False):
            base = (1 - c if sibling else c) * half
            return pl.ds(base, half) if n_quarter is None else pl.ds(base + n_quarter * quarter, quarter)

        def sem(n):
            return send.at[n], recv.at[n]

        def go(cp):
            cp.start()
            started.append(cp)

        own = _remote(src, dst.at[j], *sem(8), sib)
        go(own)
        for n, axis in enumerate("xy"):
            px, py, _ = nbr[axis]
            go(_remote(src.at[rows()], dst.at[j, rows()], *sem(n), (px, py, c)))
        for n, axis in enumerate("xy"):
            ox, oy, _ = nbr["y" if axis == "x" else "x"]
            pj = nbr[axis][2]
            _remote(src.at[rows()], dst.at[pj, rows()], *sem(n), me).wait_recv()
            go(_remote(dst.at[pj, rows(n)], dst.at[pj, rows(n)], *sem(2 + n), (ox, oy, c)))
            go(_remote(dst.at[pj, rows()], dst.at[pj, rows()], *sem(4 + n), sib))
            arrivals.append(_remote(src.at[rows()], dst.at[pj, rows(None, True)], *sem(4 + n), me))
        for n in range(2):
            _remote(dst.at[diag, rows(n)], dst.at[diag, rows(n)], *sem(2 + n), me).wait_recv()
            go(_remote(dst.at[diag, rows(n)], dst.at[diag, rows(n)], *sem(6 + n), sib))
            arrivals.append(_remote(dst.at[diag, rows(n, True)], dst.at[diag, rows(n, True)], *sem(6 + n), me))
        for cp in arrivals + [own]:
            cp.wait_recv()
        for cp in started:
            cp.wait_send()

    return pl.pallas_call(
        body, name="gather_weights", in_specs=[ANY], out_specs=ANY,
        out_shape=SDS((4,) + w_in_b.shape, BF16),
        scratch_shapes=[pltpu.SemaphoreType.DMA((9,)), pltpu.SemaphoreType.DMA((9,))],
        compiler_params=pltpu.CompilerParams(has_side_effects=True),
    )(w_in_b)


class _LateGather:
    def __init__(self, w_out_b, conv_w):
        self.arrays = [w_out_b, conv_w]
        self.out_shape = [SDS((4,) + w_out_b.shape, BF16), SDS((4,) + conv_w.shape, F32)]
        self.scratch = [pltpu.SemaphoreType.DMA((11,)), pltpu.SemaphoreType.DMA((11,))]

    def _plan(self, ins, outs, sems):
        x, y, c, j = _place()
        send, recv = sems
        (wo, cw), (gwo, gcw) = ins, outs
        half = wo.shape[0] // 2
        mine, theirs = pl.ds(c * half, half), pl.ds((1 - c) * half, half)
        me, sib = (x, y, c), (x, y, 1 - c)
        first, arrive, forward, last = [], [], [], []
        for k in (1, 2, 3):
            px, py, pj = _chip_of(x, y, k)
            first += [_remote(wo.at[mine], gwo.at[j, mine], send.at[k - 1], recv.at[k - 1], (px, py, c)),
                      _remote(cw, gcw.at[j], send.at[k + 2], recv.at[k + 2], (px, py, c))]
            arrive.append(_remote(wo.at[mine], gwo.at[pj, mine], send.at[k - 1], recv.at[k - 1], me))
            forward.append(_remote(gwo.at[pj, mine], gwo.at[pj, mine], send.at[k + 5], recv.at[k + 5], sib))
            last += [_remote(cw, gcw.at[pj], send.at[k + 2], recv.at[k + 2], me),
                     _remote(wo.at[theirs], gwo.at[pj, theirs], send.at[k + 5], recv.at[k + 5], me)]
        first += [_remote(wo, gwo.at[j], send.at[9], recv.at[9], sib),
                  _remote(cw, gcw.at[j], send.at[10], recv.at[10], sib)]
        last += first[-2:]
        return first, arrive, forward, last

    def start(self, ins, outs, sems):
        for cp in self._plan(ins, outs, sems)[0]:
            cp.start()

    def pass_on(self, ins, outs, sems):
        _, arrive, forward, _ = self._plan(ins, outs, sems)
        for got, fwd in zip(arrive, forward):
            got.wait_recv()
            fwd.start()

    def finish(self, ins, outs, sems):
        first, _, forward, last = self._plan(ins, outs, sems)
        for cp in last:
            cp.wait_recv()
        for cp in first + forward:
            cp.wait_send()


def _window(s, names):
    lo, hi = TILES * s, TILES * s + TILES + 1
    pieces = []
    for n, name in enumerate(names):
        a, count = SECTION_TILES[name]
        first, last = max(lo, a), min(hi, a + count)
        if first < last:
            pieces.append((n, first - a, last - first, first - lo))
    assert sum(p[2] for p in pieces) == TILES + 1
    return pieces


class _PairExchange:
    def __init__(self, names, sections, shards, more=()):
        self.names, self.shards = names, shards
        self.there = [n for n, a in enumerate(sections) if a is not None]
        self.arrays = [sections[n] for n in self.there] + list(more)
        self.out_shape = [SDS((len(shards), D // 2, WIN), F32)]
        self.out_shape += [SDS((a.shape[0], a.shape[1] // 2, a.shape[2]), F32) for a in more]
        n = sum(p[0] in self.there for s in shards for p in _window(s, names)) + len(more)
        self.scratch = [pltpu.SemaphoreType.DMA((n,)) for _ in range(2)]

    def _copies(self, ins, outs, sems):
        x, y, c, _ = _place()
        sib = (x, y, 1 - c)
        rows = pl.ds((1 - c) * (D // 2), D // 2)
        k = 0
        for i, s in enumerate(self.shards):
            for n, tile, tiles, at in _window(s, self.names):
                if n in self.there:
                    yield _remote(ins[self.there.index(n)].at[rows, pl.ds(tile * LANE, tiles * LANE)],
                                  outs[0].at[i, :, pl.ds(at * LANE, tiles * LANE)], sems[0].at[k], sems[1].at[k], sib)
                    k += 1
        for src, dst in zip(ins[len(self.there):], outs[1:]):
            half = src.shape[1] // 2
            yield _remote(src.at[:, pl.ds((1 - c) * half, half)], dst, sems[0].at[k], sems[1].at[k], sib)
            k += 1

    def start(self, ins, outs, sems):
        for cp in self._copies(ins, outs, sems):
            cp.start()

    def finish(self, ins, outs, sems):
        for cp in self._copies(ins, outs, sems):
            cp.wait()


def _exchange_call(exchange, name, into=None):
    n, n_out = len(exchange.arrays), len(exchange.out_shape)
    given = list(into) if into else []

    def body(*refs):
        ins, outs, sems = refs[:n], refs[n + len(given):n + len(given) + n_out], refs[n + len(given) + n_out:]
        exchange.start(ins, outs, sems)
        exchange.finish(ins, outs, sems)

    return pl.pallas_call(
        body, name=name, in_specs=[ANY] * (n + len(given)), out_specs=[ANY] * n_out, out_shape=exchange.out_shape,
        input_output_aliases={n + k: k for k in range(len(given))},
        scratch_shapes=exchange.scratch, compiler_params=pltpu.CompilerParams(has_side_effects=True),
    )(*exchange.arrays, *given)


def _pair_sum_windows(cidx, names, sections, shards, r, name):
    n, half, _ = r.shape
    tr = min(half, 256)
    nt = half // tr

    def body(c_ref, *refs):
        del c_ref
        secs, r_ref, o_ref = refs[:-2], refs[-2], refs[-1]
        for i, s in enumerate(shards):
            for k, tile, tiles, at in _window(s, names):
                own = secs[k][:, tile * LANE:(tile + tiles) * LANE]
                there = slice(at * LANE, (at + tiles) * LANE)
                o_ref[i, :, there] = (own + r_ref[i, :, there]).astype(BF16)

    window = pl.BlockSpec((n, tr, WIN), lambda t, c: (0, t, 0))
    return pl.pallas_call(
        body, name=name,
        grid_spec=pltpu.PrefetchScalarGridSpec(
            num_scalar_prefetch=1, grid=(nt,),
            in_specs=[pl.BlockSpec((tr, a.shape[1]), lambda t, c: (c[0] * nt + t, 0)) for a in sections] + [window],
            out_specs=window),
        out_shape=SDS(r.shape, BF16),
        compiler_params=_cp(("parallel",)),
    )(cidx, *sections, r)


def _pair_sum(cidx, g, r, name):
    n, half, width = r.shape
    tr = min(half, 256)
    nt = half // tr

    def body(c_ref, g_ref, r_ref, o_ref):
        del c_ref
        o_ref[...] = (g_ref[...] + r_ref[...]).astype(BF16)

    return pl.pallas_call(
        body, name=name,
        grid_spec=pltpu.PrefetchScalarGridSpec(
            num_scalar_prefetch=1, grid=(n, nt),
            in_specs=[pl.BlockSpec((None, tr, width), lambda s, t, c: (s, c[0] * nt + t, 0)),
                      pl.BlockSpec((None, tr, width), lambda s, t, c: (s, t, 0))],
            out_specs=pl.BlockSpec((None, tr, width), lambda s, t, c: (s, t, 0))),
        out_shape=SDS(r.shape, BF16),
        compiler_params=_cp(("parallel", "parallel")),
    )(cidx, g, r)


class _ChipExchange:
    def __init__(self, arrays, rows):
        self.arrays, self.rows = list(arrays), list(rows)
        self.out_shape = [SDS((4,) + a.shape[1:], BF16) for a in self.arrays]
        self.scratch = [pltpu.SemaphoreType.DMA((3 * len(self.arrays),)) for _ in range(2)]

    def _copies(self, ins, outs, sems):
        x, y, c, j = _place()
        send, recv = sems
        for a, (src, dst, row) in enumerate(zip(ins, outs, self.rows)):
            for k in (1, 2, 3):
                px, py, pj = _chip_of(x, y, k)
                n = 3 * a + k - 1
                slot = pj if row is None else py
                yield (None if row is None else px == row, None if row is None else x == row,
                       _remote(src.at[slot], dst.at[j], send.at[n], recv.at[n], (px, py, c)),
                       _remote(src.at[0], dst.at[pj], send.at[n], recv.at[n], (x, y, c)))

    def start(self, ins, outs, sems):
        for sends, _, send, _ in self._copies(ins, outs, sems):
            if sends is None:
                send.start()
            else:
                pl.when(sends)(send.start)

    def finish(self, ins, outs, sems):
        for sends, owns, send, arrival in self._copies(ins, outs, sems):
            if sends is None:
                arrival.wait_recv()
                send.wait_send()
            else:
                pl.when(owns)(arrival.wait_recv)
                pl.when(sends)(send.wait_send)


def _all_gather_rows(src, dst, rows, send, recv, local_sem):
    x, y, c, j = _place()
    me = 2 * j + c
    local = pltpu.make_async_copy(src, dst.at[me, rows], local_sem)
    cps, arrivals = [], []
    for k in range(1, 8):
        px, py, pj = _chip_of(x, y, k >> 1)
        pc = 1 - c if k & 1 else c
        cps.append(_remote(src, dst.at[me, rows], send.at[k - 1], recv.at[k - 1], (px, py, pc)))
        arrivals.append(_remote(src, dst.at[2 * pj + pc, rows], send.at[k - 1], recv.at[k - 1], (x, y, c)))
    starts = [local.start] + [cp.start for cp in cps]
    waits = [cp.wait_recv for cp in arrivals] + [cp.wait_send for cp in cps] + [local.wait]
    return starts, waits


class _SmallExchange:
    def __init__(self, small):
        self.arrays = [small]
        self.out_shape = [SDS((8,) + small.shape, F32)]
        self.scratch = [pltpu.SemaphoreType.DMA((7,)), pltpu.SemaphoreType.DMA((7,)), pltpu.SemaphoreType.DMA]

    def start(self, ins, outs, sems):
        for go in _all_gather_rows(ins[0], outs[0], slice(None), *sems)[0]:
            go()

    def finish(self, ins, outs, sems):
        for wait in _all_gather_rows(ins[0], outs[0], slice(None), *sems)[1]:
            wait()


class _Both:
    def __init__(self, a, b):
        self.parts = (a, b)
        self.arrays, self.out_shape, self.scratch = a.arrays + b.arrays, a.out_shape + b.out_shape, a.scratch + b.scratch

    def _split(self, ins, outs, sems):
        a, b = self.parts
        return ((a, ins[:len(a.arrays)], outs[:len(a.out_shape)], sems[:len(a.scratch)]),
                (b, ins[len(a.arrays):], outs[len(a.out_shape):], sems[len(a.scratch):]))

    def start(self, ins, outs, sems):
        for part, *refs in self._split(ins, outs, sems):
            part.start(*refs)

    def finish(self, ins, outs, sems):
        for part, *refs in self._split(ins, outs, sems):
            part.finish(*refs)


def _slot_sum(r, name):
    n, rows, width = r.shape
    tr = min(rows, 256)

    def body(r_ref, o_ref):
        acc = r_ref[0].astype(F32)
        for s in range(1, n):
            acc = acc + r_ref[s].astype(F32)
        o_ref[...] = acc

    return pl.pallas_call(
        body, name=name, grid=(rows // tr,),
        in_specs=[pl.BlockSpec((n, tr, width), lambda t: (0, t, 0))],
        out_specs=pl.BlockSpec((tr, width), lambda t: (t, 0)),
        out_shape=SDS((rows, width), F32),
        compiler_params=_cp(("parallel",)),
    )(r)


def _chip_sum(where, recv, own, name):
    n, rows, width = recv.shape
    tr = min(rows, 256)
    nt = rows // tr

    def body(j_ref, r_ref, own_ref, o_ref):
        acc = None
        for s in range(n):
            term = jnp.where(j_ref[0] == s, own_ref[...], r_ref[s]).astype(F32)
            acc = term if acc is None else acc + term
        o_ref[...] = acc

    return pl.pallas_call(
        body, name=name,
        grid_spec=pltpu.PrefetchScalarGridSpec(
            num_scalar_prefetch=1, grid=(nt,),
            in_specs=[pl.BlockSpec((n, tr, width), lambda t, j: (0, t, 0)),
                      pl.BlockSpec((None, tr, width), lambda t, j: (j[0], t, 0))],
            out_specs=pl.BlockSpec((tr, width), lambda t, j: (j[1] * nt + t, 0))),
        out_shape=SDS((2 * rows, width), F32),
        compiler_params=_cp(("parallel",)),
    )(where, recv, own)


def _chip_sum_rows(place, recv0, own0, recv1, own1, name):
    n, rows, width = recv0.shape
    tr = min(rows, 256)
    nt = rows // tr

    def body(p_ref, r0_ref, o0_ref, r1_ref, o1_ref, o_ref):
        first_row = p_ref[2] == 0
        own = jnp.where(first_row, o0_ref[...], o1_ref[...])
        acc = None
        for s in range(n):
            term = jnp.where(p_ref[0] == s, own, jnp.where(first_row, r0_ref[s], r1_ref[s])).astype(F32)
            acc = term if acc is None else acc + term
        o_ref[...] = acc

    recv = pl.BlockSpec((n, tr, width), lambda t, p: (0, t, 0))
    own = pl.BlockSpec((None, tr, width), lambda t, p: (p[3], t, 0))
    return pl.pallas_call(
        body, name=name,
        grid_spec=pltpu.PrefetchScalarGridSpec(
            num_scalar_prefetch=1, grid=(nt,), in_specs=[recv, own, recv, own],
            out_specs=pl.BlockSpec((tr, width), lambda t, p: (p[1] * nt + t, 0))),
        out_shape=SDS((2 * rows, width), F32),
        compiler_params=_cp(("parallel",)),
    )(place, recv0, own0, recv1, own1)


def _half_exchange(gw, go, gathered, late, row):
    def body(gw_in, go_in, ga_in, late_ref, gw_ref, go_ref, ga_ref, send, recv, late_send, late_recv, late_local):
        del gw_in, go_in, ga_in
        x, y, c, _ = _place()
        starts, waits = _all_gather_rows(late_ref, ga_ref, pl.ds(row, late.shape[0]), late_send, late_recv,
                                         late_local)
        for go_ in starts:
            go_()
        mine = [pl.ds(c * (r.shape[0] // 2), r.shape[0] // 2) for r in (gw_ref, go_ref)]
        cps = [_remote(r.at[rows], r.at[rows], send.at[k], recv.at[k], (x, y, 1 - c))
               for k, (r, rows) in enumerate(zip((gw_ref, go_ref), mine))]
        for cp in cps:
            cp.start()
        for k, r in enumerate((gw_ref, go_ref)):
            theirs = pl.ds((1 - c) * (r.shape[0] // 2), r.shape[0] // 2)
            _remote(r.at[theirs], r.at[theirs], send.at[k], recv.at[k], (x, y, c)).wait_recv()
        for cp in cps:
            cp.wait_send()
        for wait in waits:
            wait()

    return pl.pallas_call(
        body, name="half_exchange", in_specs=[ANY] * 4, out_specs=[ANY] * 3,
        out_shape=[SDS(gw.shape, F32), SDS(go.shape, F32), SDS(gathered.shape, F32)],
        input_output_aliases={0: 0, 1: 1, 2: 2},
        scratch_shapes=[pltpu.SemaphoreType.DMA((2,)), pltpu.SemaphoreType.DMA((2,)),
                        pltpu.SemaphoreType.DMA((7,)), pltpu.SemaphoreType.DMA((7,)), pltpu.SemaphoreType.DMA],
        compiler_params=pltpu.CompilerParams(has_side_effects=True),
    )(gw, go, gathered, late)


def _adamw(w, g, m, v, name):
    rows, width = w.shape
    tr = min(rows, 256)

    def body(w_ref, g_ref, m_ref, v_ref, d_ref, nm_ref, nv_ref):
        gv = g_ref[...]
        nm = ADAM_B1 * m_ref[...] + (1.0 - ADAM_B1) * gv
        nv = ADAM_B2 * v_ref[...] + (1.0 - ADAM_B2) * (gv * gv)
        m_hat = nm / (1.0 - ADAM_B1 ** ADAM_STEP)
        v_hat = nv / (1.0 - ADAM_B2 ** ADAM_STEP)
        d_ref[...] = -ADAM_LR * (m_hat / (jnp.sqrt(v_hat) + ADAM_EPS) + ADAM_WD * w_ref[...])
        nm_ref[...] = nm
        nv_ref[...] = nv

    t = pl.BlockSpec((tr, width), lambda i: (i, 0))
    return pl.pallas_call(
        body, name=name, grid=(rows // tr,), in_specs=[t] * 4, out_specs=[t] * 3,
        out_shape=[SDS(w.shape, F32)] * 3, compiler_params=_cp(("parallel",)),
    )(w, g, m, v)


def _rowwise(a):
    return jnp.transpose(a, (2, 0, 1)).reshape(SHARD * D // LANE, LANE)


def _columns(ref, base=0):
    return jnp.concatenate([ref[pl.ds(base + c, LANE, stride=8), :].T for c in range(D // LANE)], axis=0)


def _shard_bf16(chip, w_rows):
    def body(j_ref, w_ref, o_ref, prev_ref):
        t = pl.program_id(0)
        cur = _columns(w_ref)

        @pl.when(t == 0)
        def _():
            prev_ref[...] = jnp.zeros_like(prev_ref)

        lane = _iota((D, LANE), 1)
        for s in range(4):
            @pl.when(j_ref[0] == s)
            def _():
                off = SHIFT * s
                moved = cur if s == 0 else jnp.where(lane < off, pltpu.roll(prev_ref[...], off, 1),
                                                     pltpu.roll(cur, off, 1))
                col = t * LANE + lane - off
                o_ref[...] = jnp.where((col >= 0) & (col < SHARD), moved, 0.0).astype(BF16)
        prev_ref[...] = cur

    return pl.pallas_call(
        body, name="shard_bf16",
        grid_spec=pltpu.PrefetchScalarGridSpec(
            num_scalar_prefetch=1, grid=(TILES + 1,),
            in_specs=[pl.BlockSpec((D, LANE), lambda t, j: (t, 0))],
            out_specs=pl.BlockSpec((D, LANE), lambda t, j: (0, t)),
            scratch_shapes=[pltpu.VMEM((D, LANE), F32)]),
        out_shape=SDS((D, WIN), BF16), compiler_params=_cp(("arbitrary",)),
    )(chip, w_rows)


def _whole_w_in(windows):
    tr = 256
    n = windows.shape[0]

    def body(g_ref, o_ref):
        lane = _iota((tr, LANE), 1)
        for s in range(n):
            first = TILES * s
            head = g_ref[s, :, :LANE]
            if s:
                tail = g_ref[s - 1, :, TILES * LANE:]
                head = jnp.where(lane < SHIFT * s, tail.astype(F32), head.astype(F32)).astype(BF16)
            o_ref[:, first * LANE:(first + 1) * LANE] = head
            o_ref[:, (first + 1) * LANE:(first + TILES) * LANE] = g_ref[s, :, LANE:TILES * LANE]
        o_ref[:, n * TILES * LANE:(n * TILES + 1) * LANE] = g_ref[n - 1, :, TILES * LANE:]
        o_ref[:, (n * TILES + 1) * LANE:] = jnp.zeros((tr, DP - (n * TILES + 1) * LANE), BF16)

    return pl.pallas_call(
        body, name="whole_w_in", grid=(D // tr,),
        in_specs=[pl.BlockSpec((n, tr, WIN), lambda t: (0, t, 0))], out_specs=pl.BlockSpec((tr, DP), lambda t: (t, 0)),
        out_shape=SDS((D, DP), BF16), compiler_params=_cp(("parallel",)),
    )(windows)


def _own_buffer(a, name):
    tr = 512
    block = pl.BlockSpec((tr, a.shape[1]), lambda t: (t, 0))

    def body(a_ref, o_ref):
        o_ref[...] = a_ref[...]

    return pl.pallas_call(
        body, name=name, grid=(a.shape[0] // tr,), in_specs=[block], out_specs=block,
        out_shape=SDS(a.shape, a.dtype), compiler_params=_cp(("parallel",)),
    )(a)


def _shard_of_window(chip, g_win):
    tr = 128

    def body(j_ref, g_ref, grad_ref):
        for s in range(4):
            @pl.when(j_ref[0] == s)
            def _():
                back = LANE - SHIFT * s
                from_this = _iota((tr, LANE), 1) < back

                def moved(t):
                    tile = g_ref[:, t * LANE:(t + 1) * LANE]
                    return pltpu.roll(tile, back, 1) if s else tile

                for t in range(TILES):
                    grad_ref[:, t * LANE:(t + 1) * LANE] = jnp.where(from_this, moved(t), moved(t + 1)) if s else moved(t)
                grad_ref[:, TILES * LANE:] = moved(TILES)[:, :SHARD - TILES * LANE]

    return pl.pallas_call(
        body, name="shard_of_window",
        grid_spec=pltpu.PrefetchScalarGridSpec(
            num_scalar_prefetch=1, grid=(D // tr,), in_specs=[pl.BlockSpec((tr, WIN), lambda t, j: (t, 0))],
            out_specs=pl.BlockSpec((tr, SHARD), lambda t, j: (t, 0))),
        out_shape=SDS((D, SHARD), F32), compiler_params=_cp(("parallel",)),
    )(chip, g_win)


def _adamw_in(w_rows, g, m_rows, v_rows):
    per_step = 2

    def body(w_ref, g_ref, m_ref, v_ref, d_ref, nm_ref, nv_ref):
        for a in range(per_step):
            cols = slice(a * LANE, (a + 1) * LANE)
            gv = g_ref[:, cols]
            nm = ADAM_B1 * _columns(m_ref, a * D) + (1.0 - ADAM_B1) * gv
            nv = ADAM_B2 * _columns(v_ref, a * D) + (1.0 - ADAM_B2) * (gv * gv)
            m_hat = nm / (1.0 - ADAM_B1 ** ADAM_STEP)
            v_hat = nv / (1.0 - ADAM_B2 ** ADAM_STEP)
            d_ref[:, cols] = -ADAM_LR * (m_hat / (jnp.sqrt(v_hat) + ADAM_EPS) + ADAM_WD * _columns(w_ref, a * D))
            nm_ref[:, cols] = nm
            nv_ref[:, cols] = nv

    tile = pl.BlockSpec((D, per_step * LANE), lambda t: (0, t))
    rows = pl.BlockSpec((per_step * D, LANE), lambda t: (t, 0))
    return pl.pallas_call(
        body, name="adamw_in", grid=(pl.cdiv(TILES + 1, per_step),), in_specs=[rows, tile, rows, rows],
        out_specs=[tile] * 3, out_shape=[SDS(g.shape, F32)] * 3, compiler_params=_cp(("parallel",)),
    )(w_rows, g, m_rows, v_rows)


def _rows128(a, rows):
    flat = a.reshape(-1)
    return jnp.pad(flat, (0, rows * LANE - flat.shape[0])).reshape(rows, LANE)


CONV_ROWS = 48


def _pack_small(conv_w, norm_pre, conv_b, ssm_norm, norm_post, dtb, alog, dsk, extra=None):
    cw_rows = CONV_ROWS if conv_w.shape[-1] == 1536 else 16
    extra = jnp.zeros((1, LANE), F32) if extra is None else _rows128(extra, 1)
    vec = jnp.concatenate([_rows128(dtb, 1), _rows128(alog, 1), _rows128(dsk, 1), extra, jnp.zeros((4, LANE), F32)],
                          axis=0)
    return jnp.concatenate([_rows128(conv_w, cw_rows), _rows128(norm_pre, 8), _rows128(conv_b, 16),
                            _rows128(ssm_norm, 8), _rows128(norm_post, 8), vec], axis=0)


def _unpack_small(p, cw_cols):
    cw_rows = CONV_ROWS if cw_cols == 1536 else 16
    o = cw_rows
    conv_w = p[:cw_rows].reshape(-1)[:4 * cw_cols].reshape(1, 4, cw_cols)
    norm_pre = p[o:o + 8].reshape(1, D)
    conv_b = p[o + 8:o + 24].reshape(-1)[:1536].reshape(1, 1536)
    ssm_norm = p[o + 24:o + 32].reshape(1, D)
    norm_post = p[o + 32:o + 40].reshape(1, D)
    vec = p[o + 40:o + 48]
    return conv_w, norm_pre, conv_b, ssm_norm, norm_post, vec[0:1, :NH], vec[1:2, :NH], vec[2:3, :NH], vec[3, 0]


def _pad_lanes(a):
    return jnp.pad(a, ((0, 0), (0, LANE - a.shape[1])))


class _GradReduce:
    LO, HI = ("qk", "v", "gz"), ("gz", "x")

    def __init__(self, xi, yi, ci):
        self.cidx = jnp.reshape(ci, (1,)).astype(jnp.int32)
        self.place = jnp.stack([2 * xi + yi, ci, xi, yi]).astype(jnp.int32)

    def pairs(self, dw_gz, dw_x, dw_out):
        self.hi = [dw_gz, dw_x]
        self.go = dw_out.reshape(4, D // 2, D)
        return _PairExchange(self.HI, self.hi, (2, 3), [self.go])

    def first(self, got):
        rw, ro = got
        self.pw_hi = _pair_sum_windows(self.cidx, self.HI, self.hi, (2, 3), rw, "pair_sum_hi")
        self.po = _pair_sum(self.cidx, self.go, ro, "pair_sum_out")
        return _ChipExchange([self.pw_hi, self.po], [1, None])

    def first_done(self, got):
        self.rw_hi, self.ro = got

    def second_pairs(self, dw_qk, dw_gz):
        self.lo = [dw_qk, None, dw_gz]
        return _PairExchange(self.LO, self.lo, (0, 1))

    def second(self, dw_v, got, small):
        rest = _PairExchange(self.LO, [None, dw_v, None], (0, 1))
        (rw,) = _exchange_call(rest, "pair_exchange_v", into=got)
        lo = [dw_v if a is None else a for a in self.lo]
        self.pw_lo = _pair_sum_windows(self.cidx, self.LO, lo, (0, 1), rw, "pair_sum_lo")
        return _Both(_ChipExchange([self.pw_lo], [0]), _SmallExchange(small))

    def second_done(self, got):
        self.rw_lo, self.small = got

    def result(self, late, row):
        half_in = _chip_sum_rows(self.place, self.rw_lo, self.pw_lo, self.rw_hi, self.pw_hi, "chip_sum_in")
        half_out = _chip_sum(self.place[0:2], self.ro, self.po, "chip_sum_out")
        return _half_exchange(half_in, half_out, self.small, late, row)


def kernel(x, norm_pre_w, w_in, conv_w, conv_b, dt_bias, a_log, d_skip, ssm_norm_w, w_out, norm_post_w, loss_target, m_norm_pre_w, m_w_in, m_conv_w, m_conv_b, m_dt_bias, m_a_log, m_d_skip, m_ssm_norm_w, m_w_out, m_norm_post_w, v_norm_pre_w, v_w_in, v_conv_w, v_conv_b, v_dt_bias, v_a_log, v_d_skip, v_ssm_norm_w, v_w_out, v_norm_post_w):
    xi, yi, ci = lax.axis_index("x"), lax.axis_index("y"), lax.axis_index("c")
    chip = 2 * xi + yi
    x2, tgt = x[0], loss_target[0]

    chip_idx = jnp.reshape(chip, (1,)).astype(jnp.int32)
    w_rows = _rowwise(w_in)
    w_all = _whole_w_in(_gather_weights(_shard_bf16(chip_idx, w_rows)))
    reduce = _GradReduce(xi, yi, ci)
    grad_x, dnw_pre = _local_step(x2, tgt, w_all, _LateGather(w_out[0].astype(BF16), conv_w[0]), norm_pre_w, conv_b,
                                  dt_bias, a_log, d_skip, ssm_norm_w, norm_post_w, reduce)
    g_win, g_out, small = reduce.result(_rows128(dnw_pre, D // LANE), CONV_ROWS)
    g_small = _slot_sum(small, "small_sum")
    g_cw, g_npre, g_cb, g_nssm, g_npost, g_dtb, g_alog, g_dsk, loss = _unpack_small(g_small, 1536)
    g_cw = lax.dynamic_slice_in_dim(g_cw, chip * 384, 384, axis=2)

    g_in = _shard_of_window(chip_idx, g_win)
    d_in, nm_in, nv_in = _adamw_in(w_rows, g_in, _rowwise(m_w_in), _rowwise(v_w_in))
    grad_x = _own_buffer(grad_x, "grad_x_copy")
    d_out, nm_out, nv_out = _adamw(w_out[0], g_out, m_w_out[0], v_w_out[0], "adamw_out")
    packed = [_pack_small(*t) for t in (
        (conv_w, norm_pre_w, conv_b, ssm_norm_w, norm_post_w, dt_bias, a_log, d_skip),
        (g_cw, g_npre, g_cb, g_nssm, g_npost, g_dtb, g_alog, g_dsk),
        (m_conv_w, m_norm_pre_w, m_conv_b, m_ssm_norm_w, m_norm_post_w, m_dt_bias, m_a_log, m_d_skip),
        (v_conv_w, v_norm_pre_w, v_conv_b, v_ssm_norm_w, v_norm_post_w, v_dt_bias, v_a_log, v_d_skip))]
    small_out = [_unpack_small(p, 384)[:8] for p in _adamw(*packed, "adamw_small")]

    def ordered(cw_, npre, cb_, nssm, npost, dtb_, alog_, dsk_, big_in, big_out):
        return [npre, big_in[None], cw_, cb_, dtb_, alog_, dsk_, nssm, big_out[None], npost]

    grads = ordered(g_cw, g_npre, g_cb, g_nssm, g_npost, g_dtb, g_alog, g_dsk, g_in, g_out)
    deltas = ordered(*small_out[0], d_in, d_out)
    new_m = ordered(*small_out[1], nm_in, nm_out)
    new_v = ordered(*small_out[2], nv_in, nv_out)
    return (loss, grad_x[None], *grads, *deltas, *new_m, *new_v)


def _local_step(x2, tgt, w_all, late, norm_pre_w, conv_b, dt_bias, a_log, d_skip, ssm_norm_w,
                norm_post_w, reduce=None):
    dtb, alog = _pad_lanes(dt_bias), _pad_lanes(a_log)
    d_b = jnp.repeat(d_skip, 64, axis=1)

    if isinstance(late, _LateGather):
        (proj, u), (gout, gcw) = _inproj_fwd(x2, norm_pre_w, w_all, late)
        w_out_all = gout.reshape(2 * D, D)
        cw_all = jnp.concatenate([gcw[0], gcw[1], gcw[2], gcw[3]], axis=1)
    else:
        proj, u = _inproj_fwd(x2, norm_pre_w, w_all)
        w_out_all, cw_all = late
    mix, attn_pre, lse = _attn_fwd(proj, 1, _attn_fwd(proj, 4, _attn_fwd(proj, 16)), final=True)
    mix, y_save, states, conv_out = _ssm_fwd(proj, mix, cw_all, conv_b, dtb, alog, d_b, ssm_norm_w)

    dy, dn_ssm, do, delta, dg, dw_out, dnw_post, loss_part = _outproj_loss(mix, w_out_all, x2, tgt, norm_post_w,
                                                                          attn_pre, proj)
    dz, dxbcdt, dcw, dcb, dvec, dnw_ssm = _ssm_bwd(proj, dn_ssm, y_save, states, conv_out, cw_all, dtb, alog, d_b,
                                                   ssm_norm_w)
    dw_gz, dw_x = _dw_pair(u, dg, dz, "dw_in_gz"), _dw(u, dxbcdt, "dw_in_xbcdt", X_COLS)
    acc = _attn_bwd(proj, do, lse, delta, 16, None, F32, reduce.pairs(dw_gz, dw_x, dw_out) if reduce else None)
    if reduce:
        acc, got = acc
    acc = _attn_bwd(proj, do, lse, delta, 4, acc, F32, reduce.first(got) if reduce else None)
    if reduce:
        acc, got = acc
        reduce.first_done(got)
    dq, dk, dv = _attn_bwd(proj, do, lse, delta, 1, acc, BF16)
    dw_qk = _dw_pair(u, dq, dk, "dw_in_qk")
    dw_v = _dw(u, dv, "dw_in_v", hosted=reduce.second_pairs(dw_qk, dw_gz) if reduce else None)
    if reduce:
        dw_v, got = dw_v

    def small(dnw_pre):
        return _pack_small(dcw, dnw_pre, dcb, dnw_ssm, dnw_post, dvec[0:1, :NH], dvec[1:2, :NH], dvec[2:3, :NH],
                           loss_part[:, :1])

    res = _inproj_bwd_dx([dq, dk, dv, dg, dz], dxbcdt, w_all, x2, dy, norm_pre_w,
                         reduce.second(dw_v, got, small(jnp.zeros((1, D), F32))) if reduce else None)
    if reduce:
        res, got = res
        reduce.second_done(got)
        return res
    grad_x, dnw_pre = res
    dw_all = jnp.concatenate([dw_qk, dw_v, dw_gz, dw_x], axis=1)
    return grad_x, small(dnw_pre), dw_all, dw_out
```

```python
import functools

import jax
import jax.numpy as jnp
from jax import lax
from jax.experimental import pallas as pl
from jax.experimental.pallas import tpu as pltpu

F32 = jnp.float32
BF16 = jnp.bfloat16
MESH = pl.DeviceIdType.MESH
SDS = jax.ShapeDtypeStruct
ANY = pl.BlockSpec(memory_space=pl.ANY)

S = 4096
D = 1024
DP = 7168
SHARD = 1668
OFF_G, OFF_Z = 3072, 4096
NH = 16
CH = 128
NC = S // CH
EPS = 1e-6
NEG = -1e30
LANE = 128
VMEM_LIMIT = 48 * 1024 * 1024

TILES = SHARD // LANE
WIN = (TILES + 1) * LANE
SHIFT = SHARD - TILES * LANE
SECTION_TILES = {"qk": (0, 16), "v": (16, 8), "gz": (24, 16), "x": (40, 13)}
X_COLS = SECTION_TILES["x"][1] * LANE

ADAM_LR, ADAM_B1, ADAM_B2, ADAM_EPS, ADAM_WD, ADAM_STEP = 0.001, 0.9, 0.999, 1e-08, 0.01, 10


def _cp(sem, **kw):
    return pltpu.CompilerParams(dimension_semantics=sem, vmem_limit_bytes=VMEM_LIMIT, **kw)


def _dot(a, b):
    return jnp.dot(a, b, preferred_element_type=F32)


def _dot_nt(a, b):
    return lax.dot_general(a, b, (((1,), (1,)), ((), ())), preferred_element_type=F32)


def _dot_tn(a, b):
    return lax.dot_general(a, b, (((0,), (0,)), ((), ())), preferred_element_type=F32)


def _pieces(x, n):
    out = []
    for _ in range(n):
        p = x.astype(BF16)
        out.append(p)
        x = x - p.astype(F32)
    return out


def _pick(x, sel, n=2):
    parts = [_dot(p, sel) for p in _pieces(x, n)]
    return functools.reduce(jnp.add, parts)


def _pick_left(sel, x, n=3):
    parts = [_dot(sel, p) for p in _pieces(x, n)]
    return functools.reduce(jnp.add, parts)


def _sigmoid(v):
    return 0.5 * jnp.tanh(0.5 * v) + 0.5


def _iota(shape, dim):
    return lax.broadcasted_iota(jnp.int32, shape, dim)


def _inproj_fwd(x, nw, w_all, hosted=None):
    tm, tn = 1024, 1024
    ni, nj = S // tm, DP // tn
    n_host = len(hosted.arrays) if hosted else 0

    def body(x_hbm, nw_ref, w_hbm, *refs):
        host_in, (proj_ref, u_ref), refs = refs[:n_host], refs[n_host:n_host + 2], refs[n_host + 2:]
        host_out, host_sems, (xbuf, wbuf, xsem, wsem) = refs[:n_host], refs[n_host:-4], refs[-4:]
        i, j = pl.program_id(0), pl.program_id(1)
        s = i * nj + j

        def x_copy(k):
            return pltpu.make_async_copy(x_hbm.at[pl.ds(pl.multiple_of(k * tm, tm), tm)], xbuf.at[k % 2], xsem.at[k % 2])

        def w_copy(t):
            cols = pl.ds(pl.multiple_of((t % nj) * tn, tn), tn)
            return pltpu.make_async_copy(w_hbm.at[:, cols], wbuf.at[t % 3], wsem.at[t % 3])

        @pl.when(s == 0)
        def _():
            x_copy(0).start()
            w_copy(0).start()
            w_copy(1).start()
            if hosted:
                hosted.start(host_in, host_out, host_sems)

        @pl.when(j == 0)
        def _():
            x_copy(i).wait()
            pl.when(i + 1 < ni)(lambda: x_copy(i + 1).start())
            xf = xbuf[i % 2]
            r = lax.rsqrt(jnp.mean(xf * xf, axis=-1, keepdims=True) + EPS)
            u_ref[...] = (xf * r * nw_ref[...]).astype(BF16)

        w_copy(s).wait()
        pl.when(s + 2 < ni * nj)(lambda: w_copy(s + 2).start())
        proj_ref[...] = _dot(u_ref[...], wbuf[s % 3])
        if hosted:
            pl.when((i == ni // 2) & (j == 0))(lambda: hosted.pass_on(host_in, host_out, host_sems))
            pl.when((i == ni - 1) & (j == nj - 1))(lambda: hosted.finish(host_in, host_out, host_sems))

    rings = [pltpu.VMEM((2, tm, D), F32), pltpu.VMEM((3, D, tn), BF16),
             pltpu.SemaphoreType.DMA((2,)), pltpu.SemaphoreType.DMA((3,))]
    outs = pl.pallas_call(
        body, name="inproj_fwd", grid=(ni, nj),
        in_specs=[ANY, pl.BlockSpec((1, D), lambda i, j: (0, 0)), ANY] + [ANY] * n_host,
        out_specs=[pl.BlockSpec((tm, tn), lambda i, j: (i, j)), pl.BlockSpec((tm, D), lambda i, j: (i, 0))]
        + [ANY] * n_host,
        out_shape=[SDS((S, DP), F32), SDS((S, D), BF16)] + (hosted.out_shape if hosted else []),
        scratch_shapes=(hosted.scratch if hosted else []) + rings,
        compiler_params=_cp(("arbitrary", "arbitrary")),
    )(x, nw, w_all, *(hosted.arrays if hosted else []))
    return (outs[:2], outs[2:]) if hosted else outs


ATTN_QB = {1: 16, 4: 4, 16: 1}


def _unit_rows(r, u, d):
    return pl.ds(r + d * CH * u, CH, stride=d) if d > 1 else pl.ds(CH * u, CH)


def _for_units(d, qb, fn):
    for r in range(d):
        for u in range(qb):
            fn(r, u)


def _attn_mask(has_prev):
    qi, kj = _iota((2 * CH, 2 * CH), 0) & (CH - 1), _iota((2 * CH, 2 * CH), 1)
    cur_ok = (kj >= CH) & (kj - CH <= qi)
    prev_ok = (kj < CH) & (kj >= qi)
    return cur_ok | (prev_ok & has_prev)


def _stack_heads(v, lane_a):
    return jnp.concatenate([jnp.where(lane_a, v, 0.0), jnp.where(lane_a, 0.0, v)], axis=0).astype(BF16)


def _attn_specs(d, qb):
    rows, prows = CH * d * qb, CH * d
    nb = S // rows
    steps = (NH // 2) * nb

    def at(t):
        t = jnp.minimum(t, steps - 1)
        return t % nb, t // nb

    def cur(off):
        return pl.BlockSpec((rows, LANE), lambda t: (at(t)[0], off + at(t)[1]))

    def prev(off):
        return pl.BlockSpec((prows, LANE), lambda t: (jnp.maximum(at(t)[0] * qb - 1, 0), off + at(t)[1]))

    lag = pl.BlockSpec((rows, LANE), lambda t: at(jnp.maximum(t - 1, 0)))
    return nb, steps, cur, prev, lag


def _gather16(src_ref, dense_ref, tmp_ref):
    for a in range(4):
        tmp_ref[...] = src_ref[pl.ds(a, 4 * CH, stride=4), :]
        for b in range(4):
            dense_ref[a + 4 * b] = tmp_ref[pl.ds(b, CH, stride=4), :]


def _scatter16(dense_ref, dst_ref, tmp_ref):
    for a in range(4):
        for b in range(4):
            tmp_ref[pl.ds(b, CH, stride=4), :] = dense_ref[a + 4 * b]
        dst_ref[pl.ds(a, 4 * CH, stride=4), :] = tmp_ref[...]


def _unit_index(r, u, d):
    return (r,) if d == 16 else (_unit_rows(r, u, d), slice(None))


def _unit_kv(p_ref, c_ref, r, u, d):
    prev = p_ref[_unit_index(r, 0, d)] if u == 0 else c_ref[_unit_index(r, u - 1, d)]
    return jnp.concatenate([prev, c_ref[_unit_index(r, u, d)]], axis=0).astype(BF16)


def _dense_scratch(d, n):
    return [pltpu.VMEM((16, CH, LANE), F32)] * n + [pltpu.VMEM((4 * CH, LANE), F32)] if d == 16 else []


def _attn_fwd(proj, d, prior=None, final=False):
    qb = ATTN_QB[d]
    nb, steps, cur, prev, _ = _attn_specs(d, qb)
    n_prior = 2 if prior is not None else 0
    n_in, n_out = 5 + n_prior + final, 2 + final
    assert not (d == 16 and (n_prior or final))

    rows = CH * d * qb
    streams = [] if d == 16 else ([(0, 0), (2, 8), (4, 16)] + [(5 + k, 0) for k in range(n_prior)]
                                  + ([(n_in - 1, OFF_G // LANE)] if final else []))

    def body(*refs):
        ins, outs, scratch = list(refs[:n_in]), refs[n_in:n_in + n_out], refs[n_in + n_out:]
        if streams:
            t = pl.program_id(0)
            bufs, sems = scratch[:len(streams)], scratch[len(streams):]

            def copy(n, step):
                at, off = streams[n]
                src = ins[at].at[pl.ds(pl.multiple_of((step % nb) * rows, rows), rows),
                                 pl.ds(pl.multiple_of((off + step // nb) * LANE, LANE), LANE)]
                return pltpu.make_async_copy(src, bufs[n].at[step % 3], sems[n].at[step % 3])

            @pl.when(t == 0)
            def _():
                for n in range(len(streams)):
                    copy(n, 0).start()
                    copy(n, 1).start()

            for n in range(len(streams)):
                copy(n, t).wait()

            @pl.when(t + 2 < steps)
            def _():
                for n in range(len(streams)):
                    copy(n, t + 2).start()

            views = {streams[n][0]: bufs[n].at[t % 3] for n in range(len(streams))}
            ins = [views.get(k, ref) for k, ref in enumerate(ins)]
        if d == 16:
            tmp_ref = scratch[-1]
            for src, dense in zip(ins, scratch):
                _gather16(src, dense, tmp_ref)
            block_outs, ins, outs = outs, scratch[:n_in], scratch[n_in:n_in + n_out]
        q_ref, kp_ref, kc_ref, vp_ref, vc_ref = ins[:5]
        prior_refs = ins[5:5 + n_prior]
        if final:
            g_ref, (mix_ref, o_ref, l_ref) = ins[-1], outs
        else:
            o_ref, l_ref = outs
        i = pl.program_id(0) % nb
        lane_a = _iota((CH, LANE), 1) < 64
        mask_first, mask_rest = _attn_mask(i > 0), _attn_mask(True)

        def unit(r, u):
            at = _unit_index(r, u, d)
            q2 = _stack_heads(q_ref[at] * 0.125, lane_a)
            k2, v2 = _unit_kv(kp_ref, kc_ref, r, u, d), _unit_kv(vp_ref, vc_ref, r, u, d)
            s = jnp.where(mask_first if u == 0 else mask_rest, _dot_nt(q2, k2), NEG)
            m = jnp.max(s, axis=1, keepdims=True)
            p = jnp.exp(s - m)
            l = jnp.sum(p, axis=1, keepdims=True)
            o2 = _dot(p.astype(BF16), v2) / l
            lse2 = m + jnp.log(l)
            o = jnp.where(lane_a, o2[:CH], o2[CH:])
            lse = jnp.where(lane_a, lse2[:CH], lse2[CH:])
            if n_prior:
                o_a, l_a = prior_refs[0][at], prior_refs[1][at]
                top = jnp.maximum(l_a, lse)
                e_a, e_b = jnp.exp(l_a - top), jnp.exp(lse - top)
                tot = e_a + e_b
                o = (e_a * o_a + e_b * o) / tot
                lse = top + jnp.log(tot)
            o_ref[at] = o
            l_ref[at] = lse
            if final:
                g = g_ref[at]
                mix_ref[at] = (o * (g * _sigmoid(g))).astype(BF16)

        _for_units(d, qb, unit)
        if d == 16:
            for dense, dst in zip(outs, block_outs):
                _scatter16(dense, dst, tmp_ref)

    in_specs = [cur(0), prev(8), cur(8), prev(16), cur(16)] + [cur(0)] * n_prior
    args = [proj] * 5 + (list(prior) if n_prior else [])
    out_specs, out_shape = [cur(0), cur(0)], [SDS((S, D), F32), SDS((S, D), F32)]
    if final:
        assert d == 1
        in_specs.append(cur(OFF_G // LANE))
        args.append(proj)
        out_specs, out_shape = [cur(0)] + out_specs, [SDS((S, 2 * D), BF16)] + out_shape
    for at, _ in streams:
        in_specs[at] = ANY
    rings = [pltpu.VMEM((3, rows, LANE), F32)] * len(streams) + [pltpu.SemaphoreType.DMA((3,))] * len(streams)
    return pl.pallas_call(
        body, name=f"attn_fwd_d{d}", grid=(steps,),
        in_specs=in_specs, out_specs=out_specs, out_shape=out_shape,
        scratch_shapes=_dense_scratch(d, n_in + n_out) + rings,
        compiler_params=_cp(("arbitrary",)),
    )(*args)


def _attn_bwd(proj, do, lse, delta, d, acc, out_dtype, hosted=None):
    qb = ATTN_QB[d]
    nb, steps, cur, prev, lag = _attn_specs(d, qb)
    has_acc = acc is not None
    n_in = 11 if has_acc else 8
    n_host, n_host_out = (len(hosted.arrays), len(hosted.out_shape)) if hosted else (0, 0)
    assert not (d == 16 and (has_acc or out_dtype != F32))
    rows = CH * d * qb
    carry = (2, 16, CH, LANE) if d == 16 else (2, rows, LANE)

    def body(*refs):
        ins, host_in, refs = refs[:n_in], refs[n_in:n_in + n_host], refs[n_in + n_host:]
        (dq_ref, dk_ref, dv_ref), host_out, scratch = refs[:3], refs[3:3 + n_host_out], refs[3 + n_host_out:]
        if hosted:
            scratch, host_sems = scratch[:-len(hosted.scratch)], scratch[-len(hosted.scratch):]
        ck_ref, cv_ref = scratch[:2]
        dq_f32 = dq_ref if out_dtype == F32 else scratch[2]
        t = pl.program_id(0)
        i = t % nb
        if hosted:
            pl.when(t == 0)(lambda: hosted.start(host_in, host_out, host_sems))
        if d == 16:
            dense, dq_f32, tmp_ref = scratch[2:2 + n_in], scratch[2 + n_in], scratch[-1]

            @pl.when(t < steps)
            def _():
                for src, dst in zip(ins, dense):
                    _gather16(src, dst, tmp_ref)

            ins = dense
        q_ref, kp_ref, kc_ref, vp_ref, vc_ref, do_ref, lse_ref, dl_ref = ins[:8]
        if has_acc:
            aq_ref, ak_ref, av_ref = ins[8:11]
        slot = t & 1
        now_k, now_v, old_k, old_v = ck_ref.at[slot], cv_ref.at[slot], ck_ref.at[1 - slot], cv_ref.at[1 - slot]
        lane_a = _iota((CH, LANE), 1) < 64
        mask_first, mask_rest = _attn_mask(i > 0), _attn_mask(True)

        @pl.when(t == 0)
        def _():
            ck_ref[1] = jnp.zeros(carry[1:], F32)
            cv_ref[1] = jnp.zeros(carry[1:], F32)

        def unit(r, u):
            at = _unit_index(r, u, d)
            q2 = _stack_heads(q_ref[at] * 0.125, lane_a)
            do2 = _stack_heads(do_ref[at], lane_a)
            k2, v2 = _unit_kv(kp_ref, kc_ref, r, u, d), _unit_kv(vp_ref, vc_ref, r, u, d)
            lsev, dlv = lse_ref[at], dl_ref[at]
            lse2 = jnp.concatenate([lsev[:, 0:1], lsev[:, 64:65]], axis=0)
            dl2 = jnp.concatenate([dlv[:, 0:1], dlv[:, 64:65]], axis=0)
            p = jnp.exp(jnp.where(mask_first if u == 0 else mask_rest, _dot_nt(q2, k2), NEG) - lse2)
            ds = (p * (_dot_nt(do2, v2) - dl2)).astype(BF16)
            dq2 = _dot(ds, k2)
            dk2 = _dot_tn(ds, q2)
            dv2 = _dot_tn(p.astype(BF16), do2)
            dq = jnp.where(lane_a, dq2[:CH], dq2[CH:]) * 0.125
            if has_acc:
                dq = dq + aq_ref[at]
            dq_f32[at] = dq
            if u == 0:
                before = _unit_index(r, qb - 1, d)
                old_k[before] += dk2[:CH]
                old_v[before] += dv2[:CH]
            else:
                before = _unit_index(r, u - 1, d)
                now_k[before] += dk2[:CH]
                now_v[before] += dv2[:CH]
            now_k[at] = dk2[CH:]
            now_v[at] = dv2[CH:]

        @pl.when(t < steps)
        def _():
            _for_units(d, qb, unit)
            if d == 16:
                _scatter16(dq_f32, dq_ref, tmp_ref)
            elif out_dtype != F32:
                dq_ref[...] = dq_f32[...].astype(out_dtype)

        if d == 16:
            _scatter16(old_k, dk_ref, tmp_ref)
            _scatter16(old_v, dv_ref, tmp_ref)
        else:
            dk, dv = old_k[...], old_v[...]
            if has_acc:
                dk, dv = dk + ak_ref[...], dv + av_ref[...]
            dk_ref[...] = dk.astype(out_dtype)
            dv_ref[...] = dv.astype(out_dtype)
        if hosted:
            pl.when(t == steps)(lambda: hosted.finish(host_in, host_out, host_sems))

    in_specs = [cur(0), prev(8), cur(8), prev(16), cur(16), cur(0), cur(0), cur(0)]
    args = [proj, proj, proj, proj, proj, do, lse, delta]
    if has_acc:
        in_specs += [cur(0), lag, lag]
        args += list(acc)
    scratch = [pltpu.VMEM(carry, F32), pltpu.VMEM(carry, F32)]
    if d == 16:
        scratch += _dense_scratch(d, n_in + 1)
    elif out_dtype != F32:
        scratch.append(pltpu.VMEM((rows, LANE), F32))
    out_specs, out_shape = [cur(0), lag, lag], [SDS((S, D), out_dtype)] * 3
    if hosted:
        args += hosted.arrays
        in_specs += [ANY] * n_host
        out_specs += [ANY] * n_host_out
        out_shape += hosted.out_shape
        scratch += hosted.scratch
    outs = pl.pallas_call(
        body, name=f"attn_bwd_d{d}", grid=(steps + 1,),
        in_specs=in_specs, out_specs=out_specs, out_shape=out_shape,
        scratch_shapes=scratch, compiler_params=_cp(("arbitrary",)),
    )(*args)
    return (outs[:3], outs[3:]) if hosted else outs


def _conv_taps(cur, prev8, first):
    row8 = _iota(prev8.shape, 0)
    prev8 = jnp.where(first, 0.0, prev8)
    taps = []
    for s in (3, 2, 1):
        rolled = pltpu.roll(cur, s, 0)
        head = jnp.where(row8 < s, pltpu.roll(prev8, s, 0), rolled[:8])
        taps.append(jnp.concatenate([head, rolled[8:]], axis=0))
    return taps + [cur]


def _conv(taps, w, b):
    acc = b + w[0:1, :] * taps[0]
    for k in (1, 2, 3):
        acc = acc + w[k:k + 1, :] * taps[k]
    return acc


def _expand():
    return (_iota((LANE, D), 1) // 64 == _iota((LANE, D), 0)).astype(BF16)


def _reduce():
    return (_iota((D, LANE), 0) // 64 == _iota((D, LANE), 1)).astype(BF16)


def _ssd_common(xs_c, bc_c, dt_raw, dtb, alog):
    head_lane = _iota((CH, LANE), 1) < NH
    xs = xs_c * _sigmoid(xs_c)
    bc = bc_c * _sigmoid(bc_c)
    pre = dt_raw + dtb
    dt = jnp.where(head_lane, jnp.maximum(pre, 0.0) + jnp.log(1.0 + jnp.exp(-jnp.abs(pre))), 0.0)
    a_row = jnp.where(head_lane[0:1], -jnp.exp(alog), 0.0)
    tri = (_iota((CH, CH), 1) <= _iota((CH, CH), 0)).astype(BF16)
    cs = _pick_left(tri, dt * a_row)
    cs_last = cs[CH - 1:CH, :]
    wide = _pick(jnp.concatenate([dt, jnp.exp(cs), jnp.exp(cs_last - cs)], axis=0), _expand())
    dt_b, e_b, f_b = wide[:CH], wide[CH:2 * CH], wide[2 * CH:]
    return dict(xs=xs, bc=bc, pre=pre, dt=dt, a_row=a_row, cs=cs, cs_t=cs.T, dt_b=dt_b, e_b=e_b, f_b=f_b,
                t_b=e_b[CH - 1:CH, :])


def _groups(bc):
    bcb = bc.astype(BF16)
    return [bcb[:, 0:128], bcb[:, 128:256]], [bcb[:, 256:384], bcb[:, 384:512]]


def _decay(q, h, tril):
    seg = q["cs"][:, h:h + 1] - q["cs_t"][h:h + 1, :]
    return jnp.exp(jnp.where(tril, seg, NEG))


def _ssm_fwd(proj, mix, cw, cb, dtb, alog, d_b, nw):
    def body(xs_ref, xsp_ref, bc_ref, bcp_ref, dt_ref, z_ref, cw_ref, cb_ref, dtb_ref, alog_ref, db_ref, nw_ref,
             mix_in_ref, mix_ref, y_ref, st_ref, conv_ref, h_ref):
        del mix_in_ref
        i = pl.program_id(0)

        @pl.when(i == 0)
        def _():
            h_ref[...] = jnp.zeros_like(h_ref)

        cw, cb = cw_ref[...], cb_ref[...]
        xs_c = _conv(_conv_taps(xs_ref[...], xsp_ref[...], i == 0), cw[:, :D], cb[:, :D])
        bc_c = _conv(_conv_taps(bc_ref[...], bcp_ref[...], i == 0), cw[:, D:], cb[:, D:])
        conv_ref[:, :D] = xs_c
        conv_ref[:, D:] = bc_c
        q = _ssd_common(xs_c, bc_c, dt_ref[...], dtb_ref[...], alog_ref[...])
        bg, cg = _groups(q["bc"])
        xs = q["xs"]
        xdt = xs * q["dt_b"]
        xdt_b = xdt.astype(BF16)
        h_in = h_ref[...]
        st_ref[...] = h_in
        hb = h_in.astype(BF16)
        tril = _iota((CH, CH), 1) <= _iota((CH, CH), 0)
        lane_a = _iota((CH, LANE), 1) < 64
        cbm = [_dot_nt(cg[g], bg[g]) for g in range(2)]
        pairs = []
        for hp in range(NH // 2):
            xp = xdt_b[:, hp * LANE:(hp + 1) * LANE]
            ya = _dot((cbm[hp // 4] * _decay(q, 2 * hp, tril)).astype(BF16), xp)
            yb = _dot((cbm[hp // 4] * _decay(q, 2 * hp + 1, tril)).astype(BF16), xp)
            pairs.append(jnp.where(lane_a, ya, yb))
        y_diag = jnp.concatenate(pairs, axis=1)
        y_off = jnp.concatenate([_dot(cg[g], hb[:, g * 512:(g + 1) * 512]) for g in range(2)], axis=1) * q["e_b"]
        y = y_diag + y_off + db_ref[...] * xs
        y_ref[...] = y
        xf = (xdt * q["f_b"]).astype(BF16)
        h_ref[...] = q["t_b"] * h_in + jnp.concatenate(
            [_dot_tn(bg[g], xf[:, g * 512:(g + 1) * 512]) for g in range(2)], axis=1)
        z = z_ref[...]
        yz = y * (z * _sigmoid(z))
        outs = []
        for g in range(2):
            v = yz[:, g * 512:(g + 1) * 512]
            outs.append(v * lax.rsqrt(jnp.mean(v * v, axis=-1, keepdims=True) + EPS))
        mix_ref[...] = (jnp.concatenate(outs, axis=1) * nw_ref[...]).astype(BF16)

    def col(width, blk, prev=False):
        if prev:
            return pl.BlockSpec((8, width), lambda i: (jnp.maximum(i * (CH // 8) - 1, 0), blk))
        return pl.BlockSpec((CH, width), lambda i: (i, blk))

    def full(a):
        return pl.BlockSpec(a.shape, lambda i: (0,) * a.ndim)

    return pl.pallas_call(
        body, name="ssm_fwd", grid=(NC,),
        in_specs=[col(D, 5), col(D, 5, True), col(512, 12), col(512, 12, True), col(LANE, 52), col(D, 4),
                  full(cw), full(cb), full(dtb), full(alog), full(d_b), full(nw), ANY],
        out_specs=[col(D, 1), col(D, 0), pl.BlockSpec((None, CH, D), lambda i: (i, 0, 0)), col(D + 512, 0)],
        out_shape=[SDS((S, 2 * D), BF16), SDS((S, D), F32), SDS((NC, CH, D), F32), SDS((S, D + 512), F32)],
        scratch_shapes=[pltpu.VMEM((CH, D), F32)],
        input_output_aliases={12: 0},
        compiler_params=_cp(("arbitrary",)),
    )(proj, proj, proj, proj, proj, proj, cw, cb, dtb, alog, d_b, nw, mix)


def _ssm_bwd(proj, dn, y_save, states, conv_out, cw, dtb, alog, d_b, nw):
    def body(xs_ref, bc_ref, dt_ref, z_ref, dn_ref, y_ref, st_ref, conv_ref,
             cw_ref, dtb_ref, alog_ref, db_ref, nw_ref,
             dz_ref, dx_ref, dcw_ref, dcb_ref, dsm_ref, dnw_ref, dh_ref, nxs_ref, nbc_ref):
        i = pl.program_id(0)
        ci = NC - 1 - i

        @pl.when(i == 0)
        def _():
            for ref in (dcw_ref, dcb_ref, dsm_ref, dnw_ref, dh_ref, nxs_ref, nbc_ref):
                ref[...] = jnp.zeros_like(ref)

        cw = cw_ref[...]
        xs_c, bc_c = conv_ref[:, :D], conv_ref[:, D:]
        q = _ssd_common(xs_c, bc_c, dt_ref[...], dtb_ref[...], alog_ref[...])
        bg, cg = _groups(q["bc"])
        xs, dt_b, e_b, f_b, t_b = q["xs"], q["dt_b"], q["e_b"], q["f_b"], q["t_b"]
        xdt = xs * dt_b
        xdt_b = xdt.astype(BF16)
        h_in = st_ref[...]
        hb = h_in.astype(BF16)
        dh_new = dh_ref[...]
        dhb = dh_new.astype(BF16)
        red = _reduce()

        z, y, dn, nw_v = z_ref[...], y_ref[...], dn_ref[...], nw_ref[...]
        sig = _sigmoid(z)
        sz = z * sig
        yz = y * sz
        gdn = dn * nw_v
        dyz, dnw = [], []
        for g in range(2):
            v, gv = yz[:, g * 512:(g + 1) * 512], gdn[:, g * 512:(g + 1) * 512]
            r = lax.rsqrt(jnp.mean(v * v, axis=-1, keepdims=True) + EPS)
            dnw.append(dn[:, g * 512:(g + 1) * 512] * v * r)
            dyz.append(r * (gv - v * (r * r) * jnp.mean(gv * v, axis=-1, keepdims=True)))
        dyz = jnp.concatenate(dyz, axis=1)
        dnw_ref[...] += jnp.sum(jnp.concatenate(dnw, axis=1), axis=0, keepdims=True)
        dy = dyz * sz
        dz_ref[...] = (dyz * y * (sig * (1.0 + z * (1.0 - sig)))).astype(BF16)
        dy_b = dy.astype(BF16)

        tril = _iota((CH, CH), 1) <= _iota((CH, CH), 0)
        lane_a = _iota((CH, LANE), 1) < 64
        cbm = [_dot_nt(cg[g], bg[g]) for g in range(2)]
        dcbm = [jnp.zeros((CH, CH), F32), jnp.zeros((CH, CH), F32)]
        seg_rows = jnp.zeros((CH, LANE), F32)
        seg_cols = jnp.zeros((LANE, CH), F32)
        row_id, col_id = _iota((CH, LANE), 0), _iota((CH, LANE), 1)
        dx_pairs = []
        for hp in range(NH // 2):
            g = hp // 4
            xp = xdt_b[:, hp * LANE:(hp + 1) * LANE]
            dyp_f = dy[:, hp * LANE:(hp + 1) * LANE]
            dyp = dy_b[:, hp * LANE:(hp + 1) * LANE]
            halves = []
            for k in range(2):
                h = 2 * hp + k
                lane = lane_a if k == 0 else jnp.logical_not(lane_a)
                dec = _decay(q, h, tril)
                gm = cbm[g] * dec
                dgm = _dot_nt(jnp.where(lane, dyp_f, 0.0).astype(BF16), xp)
                dcbm[g] = dcbm[g] + dgm * dec
                prod = dgm * gm
                seg_rows = jnp.where(col_id == h, jnp.sum(prod, axis=1, keepdims=True), seg_rows)
                seg_cols = jnp.where(row_id == h, jnp.sum(prod, axis=0, keepdims=True), seg_cols)
                halves.append(_dot_tn(gm.astype(BF16), dyp))
            dx_pairs.append(jnp.where(lane_a, halves[0], halves[1]))
        dxdt_diag = jnp.concatenate(dx_pairs, axis=1)

        qv = jnp.concatenate([_dot(bg[g], dhb[:, g * 512:(g + 1) * 512]) for g in range(2)], axis=1)
        y_off = jnp.concatenate([_dot(cg[g], hb[:, g * 512:(g + 1) * 512]) for g in range(2)], axis=1) * e_b
        xfq = xdt * f_b * qv
        dxdt = dxdt_diag + f_b * qv
        tdt = jnp.sum(dh_new * h_in, axis=0, keepdims=True) * t_b
        per_head = _pick(jnp.concatenate([xfq, dy * y_off, dxdt * xs, dy * xs, jnp.broadcast_to(tdt, (8, D))],
                                         axis=0), red)
        fdf, dyoff_h, dxdtxs_h, dyxs_h = [per_head[k * CH:(k + 1) * CH] for k in range(4)]
        dcs = seg_rows - seg_cols.T + dyoff_h - fdf
        last = per_head[4 * CH:4 * CH + 1] + jnp.sum(fdf, axis=0, keepdims=True)
        dcs = dcs + jnp.where(_iota((CH, LANE), 0) == CH - 1, last, 0.0)
        tri_t = (_iota((CH, CH), 1) >= _iota((CH, CH), 0)).astype(BF16)
        da = _pick_left(tri_t, dcs)
        ddt = da * q["a_row"] + dxdtxs_h
        dxs = dxdt * dt_b + db_ref[...] * dy
        ddt_raw = ddt * _sigmoid(q["pre"])
        dsm_ref[0:1, :] += jnp.sum(ddt_raw, axis=0, keepdims=True)
        dsm_ref[1:2, :] += jnp.sum(da * q["dt"], axis=0, keepdims=True) * q["a_row"]
        dsm_ref[2:3, :] += jnp.sum(dyxs_h, axis=0, keepdims=True)
        edy = (e_b * dy).astype(BF16)
        xf = (xdt * f_b).astype(BF16)
        dbs, dcs_g, dhs = [], [], []
        for g in range(2):
            sl = slice(g * 512, (g + 1) * 512)
            dcb_b = dcbm[g].astype(BF16)
            dcs_g.append(_dot(dcb_b, bg[g]) + _dot_nt(edy[:, sl], hb[:, sl]))
            dbs.append(_dot_tn(dcb_b, cg[g]) + _dot_nt(xf[:, sl], dhb[:, sl]))
            dhs.append(_dot_tn(cg[g], edy[:, sl]))
        dh_ref[...] = t_b * dh_new + jnp.concatenate(dhs, axis=1)
        dbc = jnp.concatenate(dbs + dcs_g, axis=1)

        def conv_bwd(dact, pre, x_raw, w, nxt_ref, lo):
            s = _sigmoid(pre)
            dconv = dact * (s * (1.0 + pre * (1.0 - s)))
            nxt8 = nxt_ref[...]
            row8 = _iota(nxt8.shape, 0)
            hi = lo + dconv.shape[1]
            dcb_ref[:, lo:hi] += jnp.sum(dconv, axis=0, keepdims=True)
            later = [dconv]
            for s_ in (1, 2, 3):
                rolled = pltpu.roll(dconv, CH - s_, 0)
                tail = jnp.where(row8 >= 8 - s_, pltpu.roll(nxt8, 8 - s_, 0), rolled[CH - 8:])
                later.append(jnp.concatenate([rolled[:CH - 8], tail], axis=0))
            dx = None
            for s_, up in enumerate(later):
                k = 3 - s_
                dcw_ref[k:k + 1, lo:hi] += jnp.sum(up * x_raw, axis=0, keepdims=True)
                dx = w[k:k + 1, :] * up if dx is None else dx + w[k:k + 1, :] * up
            nxt_ref[...] = dconv[:8]
            return dx

        dx_ref[:, 0:D] = conv_bwd(dxs, xs_c, xs_ref[...], cw[:, :D], nxs_ref, 0).astype(BF16)
        dx_ref[:, D:D + 512] = conv_bwd(dbc, bc_c, bc_ref[...], cw[:, D:], nbc_ref, D).astype(BF16)
        dx_ref[:, D + 512:D + 640] = ddt_raw.astype(BF16)
        dx_ref[:, D + 640:] = jnp.zeros((CH, D - 640), BF16)

    def col(width, blk):
        return pl.BlockSpec((CH, width), lambda i: (NC - 1 - i, blk))

    def full(a):
        return pl.BlockSpec(a.shape, lambda i: (0,) * len(a.shape))

    acc_shapes = [SDS((4, 1536), F32), SDS((1, 1536), F32), SDS((8, LANE), F32), SDS((1, D), F32)]
    return pl.pallas_call(
        body, name="ssm_bwd", grid=(NC,),
        in_specs=[col(D, 5), col(512, 12), col(LANE, 52), col(D, 4),
                  col(D, 0), col(D, 0), pl.BlockSpec((None, CH, D), lambda i: (NC - 1 - i, 0, 0)), col(D + 512, 0),
                  full(cw), full(dtb), full(alog), full(d_b), full(nw)],
        out_specs=[col(D, 0), col(2 * D, 0)] + [full(a) for a in acc_shapes],
        out_shape=[SDS((S, D), BF16), SDS((S, 2 * D), BF16)] + acc_shapes,
        scratch_shapes=[pltpu.VMEM((CH, D), F32), pltpu.VMEM((8, D), F32), pltpu.VMEM((8, 512), F32)],
        compiler_params=_cp(("arbitrary",)),
    )(proj, proj, proj, proj, dn, y_save, states, conv_out, cw, dtb, alog, d_b, nw)


def _outproj_loss(mix, w_out, x, tgt, nw, attn_pre, proj):
    tm = 256

    def body(mix_ref, w_ref, x_ref, t_ref, nw_ref, pre_ref, g_ref,
             dy_ref, dn_ref, do_ref, delta_ref, dg_ref, dw_ref, dnw_ref, loss_ref):
        @pl.when(pl.program_id(0) == 0)
        def _():
            dw_ref[...] = jnp.zeros_like(dw_ref)
            dnw_ref[...] = jnp.zeros_like(dnw_ref)
            loss_ref[...] = jnp.zeros_like(loss_ref)

        mixv, w = mix_ref[...], w_ref[...]
        out = _dot(mixv, w)
        r = lax.rsqrt(jnp.mean(out * out, axis=-1, keepdims=True) + EPS)
        nh = out * r
        nw_v = nw_ref[...]
        err = x_ref[...] + nh * nw_v - t_ref[...]
        loss_ref[...] += 0.5 * jnp.sum(jnp.mean(err * err, axis=-1, keepdims=True), axis=0, keepdims=True)
        dy = err * (1.0 / D)
        dy_ref[...] = dy
        dnw_ref[...] += jnp.sum(dy * nh, axis=0, keepdims=True)
        gdn = dy * nw_v
        dout = (r * (gdn - nh * jnp.mean(gdn * nh, axis=-1, keepdims=True))).astype(BF16)
        dmix = _dot_nt(dout, w)
        dw_ref[...] += _dot_tn(mixv, dout)
        dn_ref[...] = dmix[:, D:]
        dm, g, pre_v = dmix[:, :D], g_ref[...], pre_ref[...]
        sig = _sigmoid(g)
        do = dm * (g * sig)
        do_ref[...] = do
        dg_ref[...] = (dm * pre_v * (sig * (1.0 + g * (1.0 - sig)))).astype(BF16)
        prod = do * pre_v
        same_head = (_iota((LANE, LANE), 0) // 64 == _iota((LANE, LANE), 1) // 64).astype(BF16)
        for cb in range(D // LANE):
            delta_ref[:, cb * LANE:(cb + 1) * LANE] = _pick(prod[:, cb * LANE:(cb + 1) * LANE], same_head)

    row = lambda w: pl.BlockSpec((tm, w), lambda i: (i, 0))
    full = lambda s: pl.BlockSpec(s, lambda i: (0, 0))
    return pl.pallas_call(
        body, name="outproj_loss", grid=(S // tm,),
        in_specs=[row(2 * D), full((2 * D, D)), row(D), row(D), full((1, D)), row(D),
                  pl.BlockSpec((tm, D), lambda i: (i, OFF_G // D))],
        out_specs=[row(D), row(D), row(D), row(D), row(D), full((2 * D, D)), full((1, D)), full((1, LANE))],
        out_shape=[SDS((S, D), F32)] * 4 + [SDS((S, D), BF16), SDS((2 * D, D), F32), SDS((1, D), F32),
                                            SDS((1, LANE), F32)],
        compiler_params=_cp(("arbitrary",)),
    )(mix, w_out, x, tgt, nw, attn_pre, proj)


def _inproj_bwd_dx(srcs, dxbcdt, w_all, x, dy, nw, hosted=None):
    tm = 512
    nk = DP // D
    n_host, n_host_out = (len(hosted.arrays), len(hosted.out_shape)) if hosted else (0, 0)

    def body(*refs):
        src_refs = refs[:nk]
        w_hbm, x_ref, dy_ref, nw_ref = refs[nk:nk + 4]
        host_in, refs = refs[nk + 4:nk + 4 + n_host], refs[nk + 4 + n_host:]
        gx_ref, dnw_ref = refs[:2]
        host_out, host_sems, (w_ref, w_sem) = refs[2:2 + n_host_out], refs[2 + n_host_out:-2], refs[-2:]
        i = pl.program_id(0)

        def w_copy(k):
            cols = slice(k * D, (k + 1) * D)
            return pltpu.make_async_copy(w_hbm.at[:, cols], w_ref.at[:, cols], w_sem.at[k])

        @pl.when(i == 0)
        def _():
            for k in range(nk):
                w_copy(k).start()
            if hosted:
                hosted.start(host_in, host_out, host_sems)
            dnw_ref[...] = jnp.zeros_like(dnw_ref)

        du = None
        for k, ref in enumerate(src_refs):
            pl.when(i == 0)(w_copy(k).wait)
            width = min(D, 5 * D + X_COLS - k * D)
            part = _dot_nt(ref[:, :width], w_ref[:, k * D:k * D + width])
            du = part if du is None else du + part
        xf, nw_v = x_ref[...], nw_ref[...]
        r = lax.rsqrt(jnp.mean(xf * xf, axis=-1, keepdims=True) + EPS)
        xh = xf * r
        dnw_ref[...] += jnp.sum(du * xh, axis=0, keepdims=True)
        gdu = du * nw_v
        gx_ref[...] = r * (gdu - xh * jnp.mean(gdu * xh, axis=-1, keepdims=True)) + dy_ref[...]

        if hosted:
            pl.when(i == S // tm - 1)(lambda: hosted.finish(host_in, host_out, host_sems))

    row = pl.BlockSpec((tm, D), lambda i: (i, 0))
    row1 = pl.BlockSpec((tm, D), lambda i: (i, 1))
    one = pl.BlockSpec((1, D), lambda i: (0, 0))
    args = [*srcs, dxbcdt, dxbcdt, w_all, x, dy, nw]
    in_specs = [row] * len(srcs) + [row, row1, ANY, row, row, one]
    out_specs, out_shape, scratch = [row, one], [SDS((S, D), F32), SDS((1, D), F32)], []
    if hosted:
        args += hosted.arrays
        in_specs += [ANY] * n_host
        out_specs += [ANY] * n_host_out
        out_shape += hosted.out_shape
        scratch += hosted.scratch
    scratch += [pltpu.VMEM((D, DP), BF16), pltpu.SemaphoreType.DMA((nk,))]
    outs = pl.pallas_call(
        body, name="inproj_bwd_dx", grid=(S // tm,),
        in_specs=in_specs, out_specs=out_specs, out_shape=out_shape, scratch_shapes=scratch,
        compiler_params=_cp(("arbitrary",)),
    )(*args)
    return (outs[:2], outs[2:]) if hosted else outs


def _dw(u, dsec, name, width=D, hosted=None):
    ts = 1024
    n_host, n_host_out = (len(hosted.arrays), len(hosted.out_shape)) if hosted else (0, 0)

    def body(u_ref, d_ref, *refs):
        host_in, o_ref, refs = refs[:n_host], refs[n_host], refs[n_host + 1:]
        host_out, host_sems = refs[:n_host_out], refs[n_host_out:]
        i = pl.program_id(0)

        @pl.when(i == 0)
        def _():
            if hosted:
                hosted.start(host_in, host_out, host_sems)
            o_ref[...] = jnp.zeros_like(o_ref)

        o_ref[...] += _dot_tn(u_ref[...], d_ref[...])
        if hosted:
            pl.when(i == S // ts - 1)(lambda: hosted.finish(host_in, host_out, host_sems))

    outs = pl.pallas_call(
        body, name=name, grid=(S // ts,),
        in_specs=[pl.BlockSpec((ts, D), lambda i: (i, 0)), pl.BlockSpec((ts, width), lambda i: (i, 0))]
        + [ANY] * n_host,
        out_specs=[pl.BlockSpec((D, width), lambda i: (0, 0))] + [ANY] * n_host_out,
        out_shape=[SDS((D, width), F32)] + (hosted.out_shape if hosted else []),
        scratch_shapes=hosted.scratch if hosted else [],
        compiler_params=_cp(("arbitrary",)),
    )(u, dsec, *(hosted.arrays if hosted else []))
    return (outs[0], outs[1:]) if hosted else outs[0]


def _dw_pair(u, da, db, name):
    ts = 1024
    last = S // ts - 1

    def body(u_ref, a_ref, b_ref, o_ref):
        j = pl.program_id(0)

        @pl.when(pl.program_id(1) == 0)
        def _():
            o_ref[...] = jnp.zeros_like(o_ref)

        for k, d_ref in enumerate((a_ref, b_ref)):
            @pl.when(j == k)
            def _():
                o_ref[...] += _dot_tn(u_ref[...], d_ref[...])

    return pl.pallas_call(
        body, name=name, grid=(2, S // ts),
        in_specs=[pl.BlockSpec((ts, D), lambda j, i: (i, 0)),
                  pl.BlockSpec((ts, D), lambda j, i: (jnp.where(j == 0, i, last), 0)),
                  pl.BlockSpec((ts, D), lambda j, i: (jnp.where(j == 1, i, 0), 0))],
        out_specs=pl.BlockSpec((D, D), lambda j, i: (0, j)),
        out_shape=SDS((D, 2 * D), F32),
        compiler_params=_cp(("arbitrary", "arbitrary")),
    )(u, da, db)


def _place():
    x, y, c = lax.axis_index("x"), lax.axis_index("y"), lax.axis_index("c")
    return x, y, c, 2 * x + y


def _chip_of(x, y, k):
    px = 1 - x if k & 2 else x
    py = 1 - y if k & 1 else y
    return px, py, 2 * px + py


def _remote(src, dst, send_sem, recv_sem, dev):
    return pltpu.make_async_remote_copy(src_ref=src, dst_ref=dst, send_sem=send_sem, recv_sem=recv_sem,
                                        device_id=dev, device_id_type=MESH)


def _gather_weights(w_in_b):
    half = w_in_b.shape[0] // 2
    quarter = half // 2

    def body(src, dst, send, recv):
        x, y, c, j = _place()
        me, sib = (x, y, c), (x, y, 1 - c)
        nbr = {"x": _chip_of(x, y, 2), "y": _chip_of(x, y, 1)}
        diag = _chip_of(x, y, 3)[2]
        started, arrivals = [], []

        def rows(n_quarter=None, sibling=False):
            base = (1 - c if sibling else c) * half
            return pl.ds(base, half) if n_quarter is None else pl.ds(base + n_quarter * quarter, quarter)

        def sem(n):
            return send.at[n], recv.at[n]

        def go(cp):
            cp.start()
            started.append(cp)

        own = _remote(src, dst.at[j], *sem(8), sib)
        go(own)
        for n, axis in enumerate("xy"):
            px, py, _ = nbr[axis]
            go(_remote(src.at[rows()], dst.at[j, rows()], *sem(n), (px, py, c)))
        for n, axis in enumerate("xy"):
            ox, oy, _ = nbr["y" if axis == "x" else "x"]
            pj = nbr[axis][2]
            _remote(src.at[rows()], dst.at[pj, rows()], *sem(n), me).wait_recv()
            go(_remote(dst.at[pj, rows(n)], dst.at[pj, rows(n)], *sem(2 + n), (ox, oy, c)))
            go(_remote(dst.at[pj, rows()], dst.at[pj, rows()], *sem(4 + n), sib))
            arrivals.append(_remote(src.at[rows()], dst.at[pj, rows(None, True)], *sem(4 + n), me))
        for n in range(2):
            _remote(dst.at[diag, rows(n)], dst.at[diag, rows(n)], *sem(2 + n), me).wait_recv()
            go(_remote(dst.at[diag, rows(n)], dst.at[diag, rows(n)], *sem(6 + n), sib))
            arrivals.append(_remote(dst.at[diag, rows(n, True)], dst.at[diag, rows(n, True)], *sem(6 + n), me))
        for cp in arrivals + [own]:
            cp.wait_recv()
        for cp in started:
            cp.wait_send()

    return pl.pallas_call(
        body, name="gather_weights", in_specs=[ANY], out_specs=ANY,
        out_shape=SDS((4,) + w_in_b.shape, BF16),
        scratch_shapes=[pltpu.SemaphoreType.DMA((9,)), pltpu.SemaphoreType.DMA((9,))],
        compiler_params=pltpu.CompilerParams(has_side_effects=True),
    )(w_in_b)


class _LateGather:
    def __init__(self, w_out_b, conv_w):
        self.arrays = [w_out_b, conv_w]
        self.out_shape = [SDS((4,) + w_out_b.shape, BF16), SDS((4,) + conv_w.shape, F32)]
        self.scratch = [pltpu.SemaphoreType.DMA((11,)), pltpu.SemaphoreType.DMA((11,))]

    def _plan(self, ins, outs, sems):
        x, y, c, j = _place()
        send, recv = sems
        (wo, cw), (gwo, gcw) = ins, outs
        half = wo.shape[0] // 2
        mine, theirs = pl.ds(c * half, half), pl.ds((1 - c) * half, half)
        me, sib = (x, y, c), (x, y, 1 - c)
        first, arrive, forward, last = [], [], [], []
        for k in (1, 2, 3):
            px, py, pj = _chip_of(x, y, k)
            first += [_remote(wo.at[mine], gwo.at[j, mine], send.at[k - 1], recv.at[k - 1], (px, py, c)),
                      _remote(cw, gcw.at[j], send.at[k + 2], recv.at[k + 2], (px, py, c))]
            arrive.append(_remote(wo.at[mine], gwo.at[pj, mine], send.at[k - 1], recv.at[k - 1], me))
            forward.append(_remote(gwo.at[pj, mine], gwo.at[pj, mine], send.at[k + 5], recv.at[k + 5], sib))
            last += [_remote(cw, gcw.at[pj], send.at[k + 2], recv.at[k + 2], me),
                     _remote(wo.at[theirs], gwo.at[pj, theirs], send.at[k + 5], recv.at[k + 5], me)]
        first += [_remote(wo, gwo.at[j], send.at[9], recv.at[9], sib),
                  _remote(cw, gcw.at[j], send.at[10], recv.at[10], sib)]
        last += first[-2:]
        return first, arrive, forward, last

    def start(self, ins, outs, sems):
        for cp in self._plan(ins, outs, sems)[0]:
            cp.start()

    def pass_on(self, ins, outs, sems):
        _, arrive, forward, _ = self._plan(ins, outs, sems)
        for got, fwd in zip(arrive, forward):
            got.wait_recv()
            fwd.start()

    def finish(self, ins, outs, sems):
        first, _, forward, last = self._plan(ins, outs, sems)
        for cp in last:
            cp.wait_recv()
        for cp in first + forward:
            cp.wait_send()


def _window(s, names):
    lo, hi = TILES * s, TILES * s + TILES + 1
    pieces = []
    for n, name in enumerate(names):
        a, count = SECTION_TILES[name]
        first, last = max(lo, a), min(hi, a + count)
        if first < last:
            pieces.append((n, first - a, last - first, first - lo))
    assert sum(p[2] for p in pieces) == TILES + 1
    return pieces


class _PairExchange:
    def __init__(self, names, sections, shards, more=()):
        self.names, self.shards = names, shards
        self.there = [n for n, a in enumerate(sections) if a is not None]
        self.arrays = [sections[n] for n in self.there] + list(more)
        self.out_shape = [SDS((len(shards), D // 2, WIN), F32)]
        self.out_shape += [SDS((a.shape[0], a.shape[1] // 2, a.shape[2]), F32) for a in more]
        n = sum(p[0] in self.there for s in shards for p in _window(s, names)) + len(more)
        self.scratch = [pltpu.SemaphoreType.DMA((n,)) for _ in range(2)]

    def _copies(self, ins, outs, sems):
        x, y, c, _ = _place()
        sib = (x, y, 1 - c)
        rows = pl.ds((1 - c) * (D // 2), D // 2)
        k = 0
        for i, s in enumerate(self.shards):
            for n, tile, tiles, at in _window(s, self.names):
                if n in self.there:
                    yield _remote(ins[self.there.index(n)].at[rows, pl.ds(tile * LANE, tiles * LANE)],
                                  outs[0].at[i, :, pl.ds(at * LANE, tiles * LANE)], sems[0].at[k], sems[1].at[k], sib)
                    k += 1
        for src, dst in zip(ins[len(self.there):], outs[1:]):
            half = src.shape[1] // 2
            yield _remote(src.at[:, pl.ds((1 - c) * half, half)], dst, sems[0].at[k], sems[1].at[k], sib)
            k += 1

    def start(self, ins, outs, sems):
        for cp in self._copies(ins, outs, sems):
            cp.start()

    def finish(self, ins, outs, sems):
        for cp in self._copies(ins, outs, sems):
            cp.wait()


def _exchange_call(exchange, name, into=None):
    n, n_out = len(exchange.arrays), len(exchange.out_shape)
    given = list(into) if into else []

    def body(*refs):
        ins, outs, sems = refs[:n], refs[n + len(given):n + len(given) + n_out], refs[n + len(given) + n_out:]
        exchange.start(ins, outs, sems)
        exchange.finish(ins, outs, sems)

    return pl.pallas_call(
        body, name=name, in_specs=[ANY] * (n + len(given)), out_specs=[ANY] * n_out, out_shape=exchange.out_shape,
        input_output_aliases={n + k: k for k in range(len(given))},
        scratch_shapes=exchange.scratch, compiler_params=pltpu.CompilerParams(has_side_effects=True),
    )(*exchange.arrays, *given)


def _pair_sum_windows(cidx, names, sections, shards, r, name):
    n, half, _ = r.shape
    tr = min(half, 256)
    nt = half // tr

    def body(c_ref, *refs):
        del c_ref
        secs, r_ref, o_ref = refs[:-2], refs[-2], refs[-1]
        for i, s in enumerate(shards):
            for k, tile, tiles, at in _window(s, names):
                own = secs[k][:, tile * LANE:(tile + tiles) * LANE]
                there = slice(at * LANE, (at + tiles) * LANE)
                o_ref[i, :, there] = (own + r_ref[i, :, there]).astype(BF16)

    window = pl.BlockSpec((n, tr, WIN), lambda t, c: (0, t, 0))
    return pl.pallas_call(
        body, name=name,
        grid_spec=pltpu.PrefetchScalarGridSpec(
            num_scalar_prefetch=1, grid=(nt,),
            in_specs=[pl.BlockSpec((tr, a.shape[1]), lambda t, c: (c[0] * nt + t, 0)) for a in sections] + [window],
            out_specs=window),
        out_shape=SDS(r.shape, BF16),
        compiler_params=_cp(("parallel",)),
    )(cidx, *sections, r)


def _pair_sum(cidx, g, r, name):
    n, half, width = r.shape
    tr = min(half, 256)
    nt = half // tr

    def body(c_ref, g_ref, r_ref, o_ref):
        del c_ref
        o_ref[...] = (g_ref[...] + r_ref[...]).astype(BF16)

    return pl.pallas_call(
        body, name=name,
        grid_spec=pltpu.PrefetchScalarGridSpec(
            num_scalar_prefetch=1, grid=(n, nt),
            in_specs=[pl.BlockSpec((None, tr, width), lambda s, t, c: (s, c[0] * nt + t, 0)),
                      pl.BlockSpec((None, tr, width), lambda s, t, c: (s, t, 0))],
            out_specs=pl.BlockSpec((None, tr, width), lambda s, t, c: (s, t, 0))),
        out_shape=SDS(r.shape, BF16),
        compiler_params=_cp(("parallel", "parallel")),
    )(cidx, g, r)


class _ChipExchange:
    def __init__(self, arrays, rows):
        self.arrays, self.rows = list(arrays), list(rows)
        self.out_shape = [SDS((4,) + a.shape[1:], BF16) for a in self.arrays]
        self.scratch = [pltpu.SemaphoreType.DMA((3 * len(self.arrays),)) for _ in range(2)]

    def _copies(self, ins, outs, sems):
        x, y, c, j = _place()
        send, recv = sems
        for a, (src, dst, row) in enumerate(zip(ins, outs, self.rows)):
            for k in (1, 2, 3):
                px, py, pj = _chip_of(x, y, k)
                n = 3 * a + k - 1
                slot = pj if row is None else py
                yield (None if row is None else px == row, None if row is None else x == row,
                       _remote(src.at[slot], dst.at[j], send.at[n], recv.at[n], (px, py, c)),
                       _remote(src.at[0], dst.at[pj], send.at[n], recv.at[n], (x, y, c)))

    def start(self, ins, outs, sems):
        for sends, _, send, _ in self._copies(ins, outs, sems):
            if sends is None:
                send.start()
            else:
                pl.when(sends)(send.start)

    def finish(self, ins, outs, sems):
        for sends, owns, send, arrival in self._copies(ins, outs, sems):
            if sends is None:
                arrival.wait_recv()
                send.wait_send()
            else:
                pl.when(owns)(arrival.wait_recv)
                pl.when(sends)(send.wait_send)


def _all_gather_rows(src, dst, rows, send, recv, local_sem):
    x, y, c, j = _place()
    me = 2 * j + c
    local = pltpu.make_async_copy(src, dst.at[me, rows], local_sem)
    cps, arrivals = [], []
    for k in range(1, 8):
        px, py, pj = _chip_of(x, y, k >> 1)
        pc = 1 - c if k & 1 else c
        cps.append(_remote(src, dst.at[me, rows], send.at[k - 1], recv.at[k - 1], (px, py, pc)))
        arrivals.append(_remote(src, dst.at[2 * pj + pc, rows], send.at[k - 1], recv.at[k - 1], (x, y, c)))
    starts = [local.start] + [cp.start for cp in cps]
    waits = [cp.wait_recv for cp in arrivals] + [cp.wait_send for cp in cps] + [local.wait]
    return starts, waits


class _SmallExchange:
    def __init__(self, small):
        self.arrays = [small]
        self.out_shape = [SDS((8,) + small.shape, F32)]
        self.scratch = [pltpu.SemaphoreType.DMA((7,)), pltpu.SemaphoreType.DMA((7,)), pltpu.SemaphoreType.DMA]

    def start(self, ins, outs, sems):
        for go in _all_gather_rows(ins[0], outs[0], slice(None), *sems)[0]:
            go()

    def finish(self, ins, outs, sems):
        for wait in _all_gather_rows(ins[0], outs[0], slice(None), *sems)[1]:
            wait()


class _Both:
    def __init__(self, a, b):
        self.parts = (a, b)
        self.arrays, self.out_shape, self.scratch = a.arrays + b.arrays, a.out_shape + b.out_shape, a.scratch + b.scratch

    def _split(self, ins, outs, sems):
        a, b = self.parts
        return ((a, ins[:len(a.arrays)], outs[:len(a.out_shape)], sems[:len(a.scratch)]),
                (b, ins[len(a.arrays):], outs[len(a.out_shape):], sems[len(a.scratch):]))

    def start(self, ins, outs, sems):
        for part, *refs in self._split(ins, outs, sems):
            part.start(*refs)

    def finish(self, ins, outs, sems):
        for part, *refs in self._split(ins, outs, sems):
            part.finish(*refs)


def _slot_sum(r, name):
    n, rows, width = r.shape
    tr = min(rows, 256)

    def body(r_ref, o_ref):
        acc = r_ref[0].astype(F32)
        for s in range(1, n):
            acc = acc + r_ref[s].astype(F32)
        o_ref[...] = acc

    return pl.pallas_call(
        body, name=name, grid=(rows // tr,),
        in_specs=[pl.BlockSpec((n, tr, width), lambda t: (0, t, 0))],
        out_specs=pl.BlockSpec((tr, width), lambda t: (t, 0)),
        out_shape=SDS((rows, width), F32),
        compiler_params=_cp(("parallel",)),
    )(r)


def _chip_sum(where, recv, own, name):
    n, rows, width = recv.shape
    tr = min(rows, 256)
    nt = rows // tr

    def body(j_ref, r_ref, own_ref, o_ref):
        acc = None
        for s in range(n):
            term = jnp.where(j_ref[0] == s, own_ref[...], r_ref[s]).astype(F32)
            acc = term if acc is None else acc + term
        o_ref[...] = acc

    return pl.pallas_call(
        body, name=name,
        grid_spec=pltpu.PrefetchScalarGridSpec(
            num_scalar_prefetch=1, grid=(nt,),
            in_specs=[pl.BlockSpec((n, tr, width), lambda t, j: (0, t, 0)),
                      pl.BlockSpec((None, tr, width), lambda t, j: (j[0], t, 0))],
            out_specs=pl.BlockSpec((tr, width), lambda t, j: (j[1] * nt + t, 0))),
        out_shape=SDS((2 * rows, width), F32),
        compiler_params=_cp(("parallel",)),
    )(where, recv, own)


def _chip_sum_rows(place, recv0, own0, recv1, own1, name):
    n, rows, width = recv0.shape
    tr = min(rows, 256)
    nt = rows // tr

    def body(p_ref, r0_ref, o0_ref, r1_ref, o1_ref, o_ref):
        first_row = p_ref[2] == 0
        own = jnp.where(first_row, o0_ref[...], o1_ref[...])
        acc = None
        for s in range(n):
            term = jnp.where(p_ref[0] == s, own, jnp.where(first_row, r0_ref[s], r1_ref[s])).astype(F32)
            acc = term if acc is None else acc + term
        o_ref[...] = acc

    recv = pl.BlockSpec((n, tr, width), lambda t, p: (0, t, 0))
    own = pl.BlockSpec((None, tr, width), lambda t, p: (p[3], t, 0))
    return pl.pallas_call(
        body, name=name,
        grid_spec=pltpu.PrefetchScalarGridSpec(
            num_scalar_prefetch=1, grid=(nt,), in_specs=[recv, own, recv, own],
            out_specs=pl.BlockSpec((tr, width), lambda t, p: (p[1] * nt + t, 0))),
        out_shape=SDS((2 * rows, width), F32),
        compiler_params=_cp(("parallel",)),
    )(place, recv0, own0, recv1, own1)


def _half_exchange(gw, go, gathered, late, row):
    def body(gw_in, go_in, ga_in, late_ref, gw_ref, go_ref, ga_ref, send, recv, late_send, late_recv, late_local):
        del gw_in, go_in, ga_in
        x, y, c, _ = _place()
        starts, waits = _all_gather_rows(late_ref, ga_ref, pl.ds(row, late.shape[0]), late_send, late_recv,
                                         late_local)
        for go_ in starts:
            go_()
        mine = [pl.ds(c * (r.shape[0] // 2), r.shape[0] // 2) for r in (gw_ref, go_ref)]
        cps = [_remote(r.at[rows], r.at[rows], send.at[k], recv.at[k], (x, y, 1 - c))
               for k, (r, rows) in enumerate(zip((gw_ref, go_ref), mine))]
        for cp in cps:
            cp.start()
        for k, r in enumerate((gw_ref, go_ref)):
            theirs = pl.ds((1 - c) * (r.shape[0] // 2), r.shape[0] // 2)
            _remote(r.at[theirs], r.at[theirs], send.at[k], recv.at[k], (x, y, c)).wait_recv()
        for cp in cps:
            cp.wait_send()
        for wait in waits:
            wait()

    return pl.pallas_call(
        body, name="half_exchange", in_specs=[ANY] * 4, out_specs=[ANY] * 3,
        out_shape=[SDS(gw.shape, F32), SDS(go.shape, F32), SDS(gathered.shape, F32)],
        input_output_aliases={0: 0, 1: 1, 2: 2},
        scratch_shapes=[pltpu.SemaphoreType.DMA((2,)), pltpu.SemaphoreType.DMA((2,)),
                        pltpu.SemaphoreType.DMA((7,)), pltpu.SemaphoreType.DMA((7,)), pltpu.SemaphoreType.DMA],
        compiler_params=pltpu.CompilerParams(has_side_effects=True),
    )(gw, go, gathered, late)


def _adamw(w, g, m, v, name):
    rows, width = w.shape
    tr = min(rows, 256)

    def body(w_ref, g_ref, m_ref, v_ref, d_ref, nm_ref, nv_ref):
        gv = g_ref[...]
        nm = ADAM_B1 * m_ref[...] + (1.0 - ADAM_B1) * gv
        nv = ADAM_B2 * v_ref[...] + (1.0 - ADAM_B2) * (gv * gv)
        m_hat = nm / (1.0 - ADAM_B1 ** ADAM_STEP)
        v_hat = nv / (1.0 - ADAM_B2 ** ADAM_STEP)
        d_ref[...] = -ADAM_LR * (m_hat / (jnp.sqrt(v_hat) + ADAM_EPS) + ADAM_WD * w_ref[...])
        nm_ref[...] = nm
        nv_ref[...] = nv

    t = pl.BlockSpec((tr, width), lambda i: (i, 0))
    return pl.pallas_call(
        body, name=name, grid=(rows // tr,), in_specs=[t] * 4, out_specs=[t] * 3,
        out_shape=[SDS(w.shape, F32)] * 3, compiler_params=_cp(("parallel",)),
    )(w, g, m, v)


def _rowwise(a):
    return jnp.transpose(a, (2, 0, 1)).reshape(SHARD * D // LANE, LANE)


def _columns(ref, base=0):
    return jnp.concatenate([ref[pl.ds(base + c, LANE, stride=8), :].T for c in range(D // LANE)], axis=0)


def _shard_bf16(chip, w_rows):
    def body(j_ref, w_ref, o_ref, prev_ref):
        t = pl.program_id(0)
        cur = _columns(w_ref)

        @pl.when(t == 0)
        def _():
            prev_ref[...] = jnp.zeros_like(prev_ref)

        lane = _iota((D, LANE), 1)
        for s in range(4):
            @pl.when(j_ref[0] == s)
            def _():
                off = SHIFT * s
                moved = cur if s == 0 else jnp.where(lane < off, pltpu.roll(prev_ref[...], off, 1),
                                                     pltpu.roll(cur, off, 1))
                col = t * LANE + lane - off
                o_ref[...] = jnp.where((col >= 0) & (col < SHARD), moved, 0.0).astype(BF16)
        prev_ref[...] = cur

    return pl.pallas_call(
        body, name="shard_bf16",
        grid_spec=pltpu.PrefetchScalarGridSpec(
            num_scalar_prefetch=1, grid=(TILES + 1,),
            in_specs=[pl.BlockSpec((D, LANE), lambda t, j: (t, 0))],
            out_specs=pl.BlockSpec((D, LANE), lambda t, j: (0, t)),
            scratch_shapes=[pltpu.VMEM((D, LANE), F32)]),
        out_shape=SDS((D, WIN), BF16), compiler_params=_cp(("arbitrary",)),
    )(chip, w_rows)


def _whole_w_in(windows):
    tr = 256
    n = windows.shape[0]

    def body(g_ref, o_ref):
        lane = _iota((tr, LANE), 1)
        for s in range(n):
            first = TILES * s
            head = g_ref[s, :, :LANE]
            if s:
                tail = g_ref[s - 1, :, TILES * LANE:]
                head = jnp.where(lane < SHIFT * s, tail.astype(F32), head.astype(F32)).astype(BF16)
            o_ref[:, first * LANE:(first + 1) * LANE] = head
            o_ref[:, (first + 1) * LANE:(first + TILES) * LANE] = g_ref[s, :, LANE:TILES * LANE]
        o_ref[:, n * TILES * LANE:(n * TILES + 1) * LANE] = g_ref[n - 1, :, TILES * LANE:]
        o_ref[:, (n * TILES + 1) * LANE:] = jnp.zeros((tr, DP - (n * TILES + 1) * LANE), BF16)

    return pl.pallas_call(
        body, name="whole_w_in", grid=(D // tr,),
        in_specs=[pl.BlockSpec((n, tr, WIN), lambda t: (0, t, 0))], out_specs=pl.BlockSpec((tr, DP), lambda t: (t, 0)),
        out_shape=SDS((D, DP), BF16), compiler_params=_cp(("parallel",)),
    )(windows)


def _own_buffer(a, name):
    tr = 512
    block = pl.BlockSpec((tr, a.shape[1]), lambda t: (t, 0))

    def body(a_ref, o_ref):
        o_ref[...] = a_ref[...]

    return pl.pallas_call(
        body, name=name, grid=(a.shape[0] // tr,), in_specs=[block], out_specs=block,
        out_shape=SDS(a.shape, a.dtype), compiler_params=_cp(("parallel",)),
    )(a)


def _shard_of_window(chip, g_win):
    tr = 128

    def body(j_ref, g_ref, grad_ref):
        for s in range(4):
            @pl.when(j_ref[0] == s)
            def _():
                back = LANE - SHIFT * s
                from_this = _iota((tr, LANE), 1) < back

                def moved(t):
                    tile = g_ref[:, t * LANE:(t + 1) * LANE]
                    return pltpu.roll(tile, back, 1) if s else tile

                for t in range(TILES):
                    grad_ref[:, t * LANE:(t + 1) * LANE] = jnp.where(from_this, moved(t), moved(t + 1)) if s else moved(t)
                grad_ref[:, TILES * LANE:] = moved(TILES)[:, :SHARD - TILES * LANE]

    return pl.pallas_call(
        body, name="shard_of_window",
        grid_spec=pltpu.PrefetchScalarGridSpec(
            num_scalar_prefetch=1, grid=(D // tr,), in_specs=[pl.BlockSpec((tr, WIN), lambda t, j: (t, 0))],
            out_specs=pl.BlockSpec((tr, SHARD), lambda t, j: (t, 0))),
        out_shape=SDS((D, SHARD), F32), compiler_params=_cp(("parallel",)),
    )(chip, g_win)


def _adamw_in(w_rows, g, m_rows, v_rows):
    per_step = 2

    def body(w_ref, g_ref, m_ref, v_ref, d_ref, nm_ref, nv_ref):
        for a in range(per_step):
            cols = slice(a * LANE, (a + 1) * LANE)
            gv = g_ref[:, cols]
            nm = ADAM_B1 * _columns(m_ref, a * D) + (1.0 - ADAM_B1) * gv
            nv = ADAM_B2 * _columns(v_ref, a * D) + (1.0 - ADAM_B2) * (gv * gv)
            m_hat = nm / (1.0 - ADAM_B1 ** ADAM_STEP)
            v_hat = nv / (1.0 - ADAM_B2 ** ADAM_STEP)
            d_ref[:, cols] = -ADAM_LR * (m_hat / (jnp.sqrt(v_hat) + ADAM_EPS) + ADAM_WD * _columns(w_ref, a * D))
            nm_ref[:, cols] = nm
            nv_ref[:, cols] = nv

    tile = pl.BlockSpec((D, per_step * LANE), lambda t: (0, t))
    rows = pl.BlockSpec((per_step * D, LANE), lambda t: (t, 0))
    return pl.pallas_call(
        body, name="adamw_in", grid=(pl.cdiv(TILES + 1, per_step),), in_specs=[rows, tile, rows, rows],
        out_specs=[tile] * 3, out_shape=[SDS(g.shape, F32)] * 3, compiler_params=_cp(("parallel",)),
    )(w_rows, g, m_rows, v_rows)


def _rows128(a, rows):
    flat = a.reshape(-1)
    return jnp.pad(flat, (0, rows * LANE - flat.shape[0])).reshape(rows, LANE)


CONV_ROWS = 48


def _pack_small(conv_w, norm_pre, conv_b, ssm_norm, norm_post, dtb, alog, dsk, extra=None):
    cw_rows = CONV_ROWS if conv_w.shape[-1] == 1536 else 16
    extra = jnp.zeros((1, LANE), F32) if extra is None else _rows128(extra, 1)
    vec = jnp.concatenate([_rows128(dtb, 1), _rows128(alog, 1), _rows128(dsk, 1), extra, jnp.zeros((4, LANE), F32)],
                          axis=0)
    return jnp.concatenate([_rows128(conv_w, cw_rows), _rows128(norm_pre, 8), _rows128(conv_b, 16),
                            _rows128(ssm_norm, 8), _rows128(norm_post, 8), vec], axis=0)


def _unpack_small(p, cw_cols):
    cw_rows = CONV_ROWS if cw_cols == 1536 else 16
    o = cw_rows
    conv_w = p[:cw_rows].reshape(-1)[:4 * cw_cols].reshape(1, 4, cw_cols)
    norm_pre = p[o:o + 8].reshape(1, D)
    conv_b = p[o + 8:o + 24].reshape(-1)[:1536].reshape(1, 1536)
    ssm_norm = p[o + 24:o + 32].reshape(1, D)
    norm_post = p[o + 32:o + 40].reshape(1, D)
    vec = p[o + 40:o + 48]
    return conv_w, norm_pre, conv_b, ssm_norm, norm_post, vec[0:1, :NH], vec[1:2, :NH], vec[2:3, :NH], vec[3, 0]


def _pad_lanes(a):
    return jnp.pad(a, ((0, 0), (0, LANE - a.shape[1])))


class _GradReduce:
    LO, HI = ("qk", "v", "gz"), ("gz", "x")

    def __init__(self, xi, yi, ci):
        self.cidx = jnp.reshape(ci, (1,)).astype(jnp.int32)
        self.place = jnp.stack([2 * xi + yi, ci, xi, yi]).astype(jnp.int32)

    def pairs(self, dw_gz, dw_x, dw_out):
        self.hi = [dw_gz, dw_x]
        self.go = dw_out.reshape(4, D // 2, D)
        return _PairExchange(self.HI, self.hi, (2, 3), [self.go])

    def first(self, got):
        rw, ro = got
        self.pw_hi = _pair_sum_windows(self.cidx, self.HI, self.hi, (2, 3), rw, "pair_sum_hi")
        self.po = _pair_sum(self.cidx, self.go, ro, "pair_sum_out")
        return _ChipExchange([self.pw_hi, self.po], [1, None])

    def first_done(self, got):
        self.rw_hi, self.ro = got

    def second_pairs(self, dw_qk, dw_gz):
        self.lo = [dw_qk, None, dw_gz]
        return _PairExchange(self.LO, self.lo, (0, 1))

    def second(self, dw_v, got, small):
        rest = _PairExchange(self.LO, [None, dw_v, None], (0, 1))
        (rw,) = _exchange_call(rest, "pair_exchange_v", into=got)
        lo = [dw_v if a is None else a for a in self.lo]
        self.pw_lo = _pair_sum_windows(self.cidx, self.LO, lo, (0, 1), rw, "pair_sum_lo")
        return _Both(_ChipExchange([self.pw_lo], [0]), _SmallExchange(small))

    def second_done(self, got):
        self.rw_lo, self.small = got

    def result(self, late, row):
        half_in = _chip_sum_rows(self.place, self.rw_lo, self.pw_lo, self.rw_hi, self.pw_hi, "chip_sum_in")
        half_out = _chip_sum(self.place[0:2], self.ro, self.po, "chip_sum_out")
        return _half_exchange(half_in, half_out, self.small, late, row)


def kernel(x, norm_pre_w, w_in, conv_w, conv_b, dt_bias, a_log, d_skip, ssm_norm_w, w_out, norm_post_w, loss_target, m_norm_pre_w, m_w_in, m_conv_w, m_conv_b, m_dt_bias, m_a_log, m_d_skip, m_ssm_norm_w, m_w_out, m_norm_post_w, v_norm_pre_w, v_w_in, v_conv_w, v_conv_b, v_dt_bias, v_a_log, v_d_skip, v_ssm_norm_w, v_w_out, v_norm_post_w):
    xi, yi, ci = lax.axis_index("x"), lax.axis_index("y"), lax.axis_index("c")
    chip = 2 * xi + yi
    x2, tgt = x[0], loss_target[0]

    chip_idx = jnp.reshape(chip, (1,)).astype(jnp.int32)
    w_rows = _rowwise(w_in)
    w_all = _whole_w_in(_gather_weights(_shard_bf16(chip_idx, w_rows)))
    reduce = _GradReduce(xi, yi, ci)
    grad_x, dnw_pre = _local_step(x2, tgt, w_all, _LateGather(w_out[0].astype(BF16), conv_w[0]), norm_pre_w, conv_b,
                                  dt_bias, a_log, d_skip, ssm_norm_w, norm_post_w, reduce)
    g_win, g_out, small = reduce.result(_rows128(dnw_pre, D // LANE), CONV_ROWS)
    g_small = _slot_sum(small, "small_sum")
    g_cw, g_npre, g_cb, g_nssm, g_npost, g_dtb, g_alog, g_dsk, loss = _unpack_small(g_small, 1536)
    g_cw = lax.dynamic_slice_in_dim(g_cw, chip * 384, 384, axis=2)

    g_in = _shard_of_window(chip_idx, g_win)
    d_in, nm_in, nv_in = _adamw_in(w_rows, g_in, _rowwise(m_w_in), _rowwise(v_w_in))
    grad_x = _own_buffer(grad_x, "grad_x_copy")
    d_out, nm_out, nv_out = _adamw(w_out[0], g_out, m_w_out[0], v_w_out[0], "adamw_out")
    packed = [_pack_small(*t) for t in (
        (conv_w, norm_pre_w, conv_b, ssm_norm_w, norm_post_w, dt_bias, a_log, d_skip),
        (g_cw, g_npre, g_cb, g_nssm, g_npost, g_dtb, g_alog, g_dsk),
        (m_conv_w, m_norm_pre_w, m_conv_b, m_ssm_norm_w, m_norm_post_w, m_dt_bias, m_a_log, m_d_skip),
        (v_conv_w, v_norm_pre_w, v_conv_b, v_ssm_norm_w, v_norm_post_w, v_dt_bias, v_a_log, v_d_skip))]
    small_out = [_unpack_small(p, 384)[:8] for p in _adamw(*packed, "adamw_small")]

    def ordered(cw_, npre, cb_, nssm, npost, dtb_, alog_, dsk_, big_in, big_out):
        return [npre, big_in[None], cw_, cb_, dtb_, alog_, dsk_, nssm, big_out[None], npost]

    grads = ordered(g_cw, g_npre, g_cb, g_nssm, g_npost, g_dtb, g_alog, g_dsk, g_in, g_out)
    deltas = ordered(*small_out[0], d_in, d_out)
    new_m = ordered(*small_out[1], nm_in, nm_out)
    new_v = ordered(*small_out[2], nv_in, nv_out)
    return (loss, grad_x[None], *grads, *deltas, *new_m, *new_v)


def _local_step(x2, tgt, w_all, late, norm_pre_w, conv_b, dt_bias, a_log, d_skip, ssm_norm_w,
                norm_post_w, reduce=None):
    dtb, alog = _pad_lanes(dt_bias), _pad_lanes(a_log)
    d_b = jnp.repeat(d_skip, 64, axis=1)

    if isinstance(late, _LateGather):
        (proj, u), (gout, gcw) = _inproj_fwd(x2, norm_pre_w, w_all, late)
        w_out_all = gout.reshape(2 * D, D)
        cw_all = jnp.concatenate([gcw[0], gcw[1], gcw[2], gcw[3]], axis=1)
    else:
        proj, u = _inproj_fwd(x2, norm_pre_w, w_all)
        w_out_all, cw_all = late
    mix, attn_pre, lse = _attn_fwd(proj, 1, _attn_fwd(proj, 4, _attn_fwd(proj, 16)), final=True)
    mix, y_save, states, conv_out = _ssm_fwd(proj, mix, cw_all, conv_b, dtb, alog, d_b, ssm_norm_w)

    dy, dn_ssm, do, delta, dg, dw_out, dnw_post, loss_part = _outproj_loss(mix, w_out_all, x2, tgt, norm_post_w,
                                                                          attn_pre, proj)
    dz, dxbcdt, dcw, dcb, dvec, dnw_ssm = _ssm_bwd(proj, dn_ssm, y_save, states, conv_out, cw_all, dtb, alog, d_b,
                                                   ssm_norm_w)
    dw_gz, dw_x = _dw_pair(u, dg, dz, "dw_in_gz"), _dw(u, dxbcdt, "dw_in_xbcdt", X_COLS)
    acc = _attn_bwd(proj, do, lse, delta, 16, None, F32, reduce.pairs(dw_gz, dw_x, dw_out) if reduce else None)
    if reduce:
        acc, got = acc
    acc = _attn_bwd(proj, do, lse, delta, 4, acc, F32, reduce.first(got) if reduce else None)
    if reduce:
        acc, got = acc
        reduce.first_done(got)
    dq, dk, dv = _attn_bwd(proj, do, lse, delta, 1, acc, BF16)
    dw_qk = _dw_pair(u, dq, dk, "dw_in_qk")
    dw_v = _dw(u, dv, "dw_in_v", hosted=reduce.second_pairs(dw_qk, dw_gz) if reduce else None)
    if reduce:
        dw_v, got = dw_v

    def small(dnw_pre):
        return _pack_small(dcw, dnw_pre, dcb, dnw_ssm, dnw_post, dvec[0:1, :NH], dvec[1:2, :NH], dvec[2:3, :NH],
                           loss_part[:, :1])

    res = _inproj_bwd_dx([dq, dk, dv, dg, dz], dxbcdt, w_all, x2, dy, norm_pre_w,
                         reduce.second(dw_v, got, small(jnp.zeros((1, D), F32))) if reduce else None)
    if reduce:
        res, got = res
        reduce.second_done(got)
        return res
    grad_x, dnw_pre = res
    dw_all = jnp.concatenate([dw_qk, dw_v, dw_gz, dw_x], axis=1)
    return grad_x, small(dnw_pre), dw_all, dw_out
```

```python
import functools

import jax
import jax.numpy as jnp
from jax import lax
from jax.experimental import pallas as pl
from jax.experimental.pallas import tpu as pltpu

F32 = jnp.float32
BF16 = jnp.bfloat16
MESH = pl.DeviceIdType.MESH
SDS = jax.ShapeDtypeStruct
ANY = pl.BlockSpec(memory_space=pl.ANY)

S = 4096
D = 1024
DP = 7168
SHARD = 1668
OFF_G, OFF_Z = 3072, 4096
NH = 16
CH = 128
NC = S // CH
EPS = 1e-6
NEG = -1e30
LANE = 128
VMEM_LIMIT = 48 * 1024 * 1024

TILES = SHARD // LANE
WIN = (TILES + 1) * LANE
SHIFT = SHARD - TILES * LANE
SECTION_TILES = {"qk": (0, 16), "v": (16, 8), "gz": (24, 16), "x": (40, 13)}
X_COLS = SECTION_TILES["x"][1] * LANE

ADAM_LR, ADAM_B1, ADAM_B2, ADAM_EPS, ADAM_WD, ADAM_STEP = 0.001, 0.9, 0.999, 1e-08, 0.01, 10


def _cp(sem, **kw):
    return pltpu.CompilerParams(dimension_semantics=sem, vmem_limit_bytes=VMEM_LIMIT, **kw)


def _dot(a, b):
    return jnp.dot(a, b, preferred_element_type=F32)


def _dot_nt(a, b):
    return lax.dot_general(a, b, (((1,), (1,)), ((), ())), preferred_element_type=F32)


def _dot_tn(a, b):
    return lax.dot_general(a, b, (((0,), (0,)), ((), ())), preferred_element_type=F32)


def _pieces(x, n):
    out = []
    for _ in range(n):
        p = x.astype(BF16)
        out.append(p)
        x = x - p.astype(F32)
    return out


def _pick(x, sel, n=2):
    parts = [_dot(p, sel) for p in _pieces(x, n)]
    return functools.reduce(jnp.add, parts)


def _pick_left(sel, x, n=3):
    parts = [_dot(sel, p) for p in _pieces(x, n)]
    return functools.reduce(jnp.add, parts)


def _sigmoid(v):
    return 0.5 * jnp.tanh(0.5 * v) + 0.5


def _iota(shape, dim):
    return lax.broadcasted_iota(jnp.int32, shape, dim)


def _inproj_fwd(x, nw, w_all, hosted=None):
    tm, tn = 1024, 1024
    ni, nj = S // tm, DP // tn
    n_host = len(hosted.arrays) if hosted else 0

    def body(x_hbm, nw_ref, w_hbm, *refs):
        host_in, (proj_ref, u_ref), refs = refs[:n_host], refs[n_host:n_host + 2], refs[n_host + 2:]
        host_out, host_sems, (xbuf, wbuf, xsem, wsem) = refs[:n_host], refs[n_host:-4], refs[-4:]
        i, j = pl.program_id(0), pl.program_id(1)
        s = i * nj + j

        def x_copy(k):
            return pltpu.make_async_copy(x_hbm.at[pl.ds(pl.multiple_of(k * tm, tm), tm)], xbuf.at[k % 2], xsem.at[k % 2])

        def w_copy(t):
            cols = pl.ds(pl.multiple_of((t % nj) * tn, tn), tn)
            return pltpu.make_async_copy(w_hbm.at[:, cols], wbuf.at[t % 3], wsem.at[t % 3])

        @pl.when(s == 0)
        def _():
            x_copy(0).start()
            w_copy(0).start()
            w_copy(1).start()
            if hosted:
                hosted.start(host_in, host_out, host_sems)

        @pl.when(j == 0)
        def _():
            x_copy(i).wait()
            pl.when(i + 1 < ni)(lambda: x_copy(i + 1).start())
            xf = xbuf[i % 2]
            r = lax.rsqrt(jnp.mean(xf * xf, axis=-1, keepdims=True) + EPS)
            u_ref[...] = (xf * r * nw_ref[...]).astype(BF16)

        w_copy(s).wait()
        pl.when(s + 2 < ni * nj)(lambda: w_copy(s + 2).start())
        proj_ref[...] = _dot(u_ref[...], wbuf[s % 3])
        if hosted:
            pl.when((i == ni // 2) & (j == 0))(lambda: hosted.pass_on(host_in, host_out, host_sems))
            pl.when((i == ni - 1) & (j == nj - 1))(lambda: hosted.finish(host_in, host_out, host_sems))

    rings = [pltpu.VMEM((2, tm, D), F32), pltpu.VMEM((3, D, tn), BF16),
             pltpu.SemaphoreType.DMA((2,)), pltpu.SemaphoreType.DMA((3,))]
    outs = pl.pallas_call(
        body, name="inproj_fwd", grid=(ni, nj),
        in_specs=[ANY, pl.BlockSpec((1, D), lambda i, j: (0, 0)), ANY] + [ANY] * n_host,
        out_specs=[pl.BlockSpec((tm, tn), lambda i, j: (i, j)), pl.BlockSpec((tm, D), lambda i, j: (i, 0))]
        + [ANY] * n_host,
        out_shape=[SDS((S, DP), F32), SDS((S, D), BF16)] + (hosted.out_shape if hosted else []),
        scratch_shapes=(hosted.scratch if hosted else []) + rings,
        compiler_params=_cp(("arbitrary", "arbitrary")),
    )(x, nw, w_all, *(hosted.arrays if hosted else []))
    return (outs[:2], outs[2:]) if hosted else outs


ATTN_QB = {1: 16, 4: 4, 16: 1}


def _unit_rows(r, u, d):
    return pl.ds(r + d * CH * u, CH, stride=d) if d > 1 else pl.ds(CH * u, CH)


def _for_units(d, qb, fn):
    for r in range(d):
        for u in range(qb):
            fn(r, u)


def _attn_mask(has_prev):
    qi, kj = _iota((2 * CH, 2 * CH), 0) & (CH - 1), _iota((2 * CH, 2 * CH), 1)
    cur_ok = (kj >= CH) & (kj - CH <= qi)
    prev_ok = (kj < CH) & (kj >= qi)
    return cur_ok | (prev_ok & has_prev)


def _stack_heads(v, lane_a):
    return jnp.concatenate([jnp.where(lane_a, v, 0.0), jnp.where(lane_a, 0.0, v)], axis=0).astype(BF16)


def _attn_specs(d, qb):
    rows, prows = CH * d * qb, CH * d
    nb = S // rows
    steps = (NH // 2) * nb

    def at(t):
        t = jnp.minimum(t, steps - 1)
        return t % nb, t // nb

    def cur(off):
        return pl.BlockSpec((rows, LANE), lambda t: (at(t)[0], off + at(t)[1]))

    def prev(off):
        return pl.BlockSpec((prows, LANE), lambda t: (jnp.maximum(at(t)[0] * qb - 1, 0), off + at(t)[1]))

    lag = pl.BlockSpec((rows, LANE), lambda t: at(jnp.maximum(t - 1, 0)))
    return nb, steps, cur, prev, lag


def _gather16(src_ref, dense_ref, tmp_ref):
    for a in range(4):
        tmp_ref[...] = src_ref[pl.ds(a, 4 * CH, stride=4), :]
        for b in range(4):
            dense_ref[a + 4 * b] = tmp_ref[pl.ds(b, CH, stride=4), :]


def _scatter16(dense_ref, dst_ref, tmp_ref):
    for a in range(4):
        for b in range(4):
            tmp_ref[pl.ds(b, CH, stride=4), :] = dense_ref[a + 4 * b]
        dst_ref[pl.ds(a, 4 * CH, stride=4), :] = tmp_ref[...]


def _unit_index(r, u, d):
    return (r,) if d == 16 else (_unit_rows(r, u, d), slice(None))


def _unit_kv(p_ref, c_ref, r, u, d):
    prev = p_ref[_unit_index(r, 0, d)] if u == 0 else c_ref[_unit_index(r, u - 1, d)]
    return jnp.concatenate([prev, c_ref[_unit_index(r, u, d)]], axis=0).astype(BF16)


def _dense_scratch(d, n):
    return [pltpu.VMEM((16, CH, LANE), F32)] * n + [pltpu.VMEM((4 * CH, LANE), F32)] if d == 16 else []


def _attn_fwd(proj, d, prior=None, final=False):
    qb = ATTN_QB[d]
    nb, steps, cur, prev, _ = _attn_specs(d, qb)
    n_prior = 2 if prior is not None else 0
    n_in, n_out = 5 + n_prior + final, 2 + final
    assert not (d == 16 and (n_prior or final))

    rows = CH * d * qb
    streams = ([(0, 0), (2, 8), (4, 16)] + [(5 + k, 0) for k in range(n_prior)]
               + ([(n_in - 1, OFF_G // LANE)] if final else []))
    n_dense = len(_dense_scratch(d, n_in + n_out))

    def body(*refs):
        ins, outs, scratch = list(refs[:n_in]), refs[n_in:n_in + n_out], refs[n_in + n_out:]
        scratch, ring = scratch[:n_dense], scratch[n_dense:]
        if streams:
            t = pl.program_id(0)
            bufs, sems = ring[:len(streams)], ring[len(streams):]

            def copy(n, step):
                at, off = streams[n]
                src = ins[at].at[pl.ds(pl.multiple_of((step % nb) * rows, rows), rows),
                                 pl.ds(pl.multiple_of((off + step // nb) * LANE, LANE), LANE)]
                return pltpu.make_async_copy(src, bufs[n].at[step % 3], sems[n].at[step % 3])

            @pl.when(t == 0)
            def _():
                for n in range(len(streams)):
                    copy(n, 0).start()
                    copy(n, 1).start()

            for n in range(len(streams)):
                copy(n, t).wait()

            @pl.when(t + 2 < steps)
            def _():
                for n in range(len(streams)):
                    copy(n, t + 2).start()

            views = {streams[n][0]: bufs[n].at[t % 3] for n in range(len(streams))}
            ins = [views.get(k, ref) for k, ref in enumerate(ins)]
        if d == 16:
            tmp_ref = scratch[-1]
            for src, dense in zip(ins, scratch):
                _gather16(src, dense, tmp_ref)
            block_outs, ins, outs = outs, scratch[:n_in], scratch[n_in:n_in + n_out]
        q_ref, kp_ref, kc_ref, vp_ref, vc_ref = ins[:5]
        prior_refs = ins[5:5 + n_prior]
        if final:
            g_ref, (mix_ref, o_ref, l_ref) = ins[-1], outs
        else:
            o_ref, l_ref = outs
        i = pl.program_id(0) % nb
        lane_a = _iota((CH, LANE), 1) < 64
        mask_first, mask_rest = _attn_mask(i > 0), _attn_mask(True)

        def unit(r, u):
            at = _unit_index(r, u, d)
            q2 = _stack_heads(q_ref[at] * 0.125, lane_a)
            k2, v2 = _unit_kv(kp_ref, kc_ref, r, u, d), _unit_kv(vp_ref, vc_ref, r, u, d)
            s = jnp.where(mask_first if u == 0 else mask_rest, _dot_nt(q2, k2), NEG)
            m = jnp.max(s, axis=1, keepdims=True)
            p = jnp.exp(s - m)
            l = jnp.sum(p, axis=1, keepdims=True)
            o2 = _dot(p.astype(BF16), v2) / l
            lse2 = m + jnp.log(l)
            o = jnp.where(lane_a, o2[:CH], o2[CH:])
            lse = jnp.where(lane_a, lse2[:CH], lse2[CH:])
            if n_prior:
                o_a, l_a = prior_refs[0][at], prior_refs[1][at]
                top = jnp.maximum(l_a, lse)
                e_a, e_b = jnp.exp(l_a - top), jnp.exp(lse - top)
                tot = e_a + e_b
                o = (e_a * o_a + e_b * o) / tot
                lse = top + jnp.log(tot)
            o_ref[at] = o
            l_ref[at] = lse
            if final:
                g = g_ref[at]
                mix_ref[at] = (o * (g * _sigmoid(g))).astype(BF16)

        _for_units(d, qb, unit)
        if d == 16:
            for dense, dst in zip(outs, block_outs):
                _scatter16(dense, dst, tmp_ref)

    in_specs = [cur(0), prev(8), cur(8), prev(16), cur(16)] + [cur(0)] * n_prior
    args = [proj] * 5 + (list(prior) if n_prior else [])
    out_specs, out_shape = [cur(0), cur(0)], [SDS((S, D), F32), SDS((S, D), F32)]
    if final:
        assert d == 1
        in_specs.append(cur(OFF_G // LANE))
        args.append(proj)
        out_specs, out_shape = [cur(0)] + out_specs, [SDS((S, 2 * D), BF16)] + out_shape
    for at, _ in streams:
        in_specs[at] = ANY
    rings = [pltpu.VMEM((3, rows, LANE), F32)] * len(streams) + [pltpu.SemaphoreType.DMA((3,))] * len(streams)
    return pl.pallas_call(
        body, name=f"attn_fwd_d{d}", grid=(steps,),
        in_specs=in_specs, out_specs=out_specs, out_shape=out_shape,
        scratch_shapes=_dense_scratch(d, n_in + n_out) + rings,
        compiler_params=_cp(("arbitrary",)),
    )(*args)


def _attn_bwd(proj, do, lse, delta, d, acc, out_dtype, hosted=None):
    qb = ATTN_QB[d]
    nb, steps, cur, prev, lag = _attn_specs(d, qb)
    has_acc = acc is not None
    n_in = 11 if has_acc else 8
    n_host, n_host_out = (len(hosted.arrays), len(hosted.out_shape)) if hosted else (0, 0)
    assert not (d == 16 and (has_acc or out_dtype != F32))
    rows = CH * d * qb
    carry = (2, 16, CH, LANE) if d == 16 else (2, rows, LANE)

    def body(*refs):
        ins, host_in, refs = refs[:n_in], refs[n_in:n_in + n_host], refs[n_in + n_host:]
        (dq_ref, dk_ref, dv_ref), host_out, scratch = refs[:3], refs[3:3 + n_host_out], refs[3 + n_host_out:]
        if hosted:
            scratch, host_sems = scratch[:-len(hosted.scratch)], scratch[-len(hosted.scratch):]
        ck_ref, cv_ref = scratch[:2]
        dq_f32 = dq_ref if out_dtype == F32 else scratch[2]
        t = pl.program_id(0)
        i = t % nb
        if hosted:
            pl.when(t == 0)(lambda: hosted.start(host_in, host_out, host_sems))
        if d == 16:
            dense, dq_f32, tmp_ref = scratch[2:2 + n_in], scratch[2 + n_in], scratch[-1]

            @pl.when(t < steps)
            def _():
                for src, dst in zip(ins, dense):
                    _gather16(src, dst, tmp_ref)

            ins = dense
        q_ref, kp_ref, kc_ref, vp_ref, vc_ref, do_ref, lse_ref, dl_ref = ins[:8]
        if has_acc:
            aq_ref, ak_ref, av_ref = ins[8:11]
        slot = t & 1
        now_k, now_v, old_k, old_v = ck_ref.at[slot], cv_ref.at[slot], ck_ref.at[1 - slot], cv_ref.at[1 - slot]
        lane_a = _iota((CH, LANE), 1) < 64
        mask_first, mask_rest = _attn_mask(i > 0), _attn_mask(True)

        @pl.when(t == 0)
        def _():
            ck_ref[1] = jnp.zeros(carry[1:], F32)
            cv_ref[1] = jnp.zeros(carry[1:], F32)

        def unit(r, u):
            at = _unit_index(r, u, d)
            q2 = _stack_heads(q_ref[at] * 0.125, lane_a)
            do2 = _stack_heads(do_ref[at], lane_a)
            k2, v2 = _unit_kv(kp_ref, kc_ref, r, u, d), _unit_kv(vp_ref, vc_ref, r, u, d)
            lsev, dlv = lse_ref[at], dl_ref[at]
            lse2 = jnp.concatenate([lsev[:, 0:1], lsev[:, 64:65]], axis=0)
            dl2 = jnp.concatenate([dlv[:, 0:1], dlv[:, 64:65]], axis=0)
            p = jnp.exp(jnp.where(mask_first if u == 0 else mask_rest, _dot_nt(q2, k2), NEG) - lse2)
            ds = (p * (_dot_nt(do2, v2) - dl2)).astype(BF16)
            dq2 = _dot(ds, k2)
            dk2 = _dot_tn(ds, q2)
            dv2 = _dot_tn(p.astype(BF16), do2)
            dq = jnp.where(lane_a, dq2[:CH], dq2[CH:]) * 0.125
            if has_acc:
                dq = dq + aq_ref[at]
            dq_f32[at] = dq
            if u == 0:
                before = _unit_index(r, qb - 1, d)
                old_k[before] += dk2[:CH]
                old_v[before] += dv2[:CH]
            else:
                before = _unit_index(r, u - 1, d)
                now_k[before] += dk2[:CH]
                now_v[before] += dv2[:CH]
            now_k[at] = dk2[CH:]
            now_v[at] = dv2[CH:]

        @pl.when(t < steps)
        def _():
            _for_units(d, qb, unit)
            if d == 16:
                _scatter16(dq_f32, dq_ref, tmp_ref)
            elif out_dtype != F32:
                dq_ref[...] = dq_f32[...].astype(out_dtype)

        if d == 16:
            _scatter16(old_k, dk_ref, tmp_ref)
            _scatter16(old_v, dv_ref, tmp_ref)
        else:
            dk, dv = old_k[...], old_v[...]
            if has_acc:
                dk, dv = dk + ak_ref[...], dv + av_ref[...]
            dk_ref[...] = dk.astype(out_dtype)
            dv_ref[...] = dv.astype(out_dtype)
        if hosted:
            pl.when(t == steps)(lambda: hosted.finish(host_in, host_out, host_sems))

    in_specs = [cur(0), prev(8), cur(8), prev(16), cur(16), cur(0), cur(0), cur(0)]
    args = [proj, proj, proj, proj, proj, do, lse, delta]
    if has_acc:
        in_specs += [cur(0), lag, lag]
        args += list(acc)
    scratch = [pltpu.VMEM(carry, F32), pltpu.VMEM(carry, F32)]
    if d == 16:
        scratch += _dense_scratch(d, n_in + 1)
    elif out_dtype != F32:
        scratch.append(pltpu.VMEM((rows, LANE), F32))
    out_specs, out_shape = [cur(0), lag, lag], [SDS((S, D), out_dtype)] * 3
    if hosted:
        args += hosted.arrays
        in_specs += [ANY] * n_host
        out_specs += [ANY] * n_host_out
        out_shape += hosted.out_shape
        scratch += hosted.scratch
    outs = pl.pallas_call(
        body, name=f"attn_bwd_d{d}", grid=(steps + 1,),
        in_specs=in_specs, out_specs=out_specs, out_shape=out_shape,
        scratch_shapes=scratch, compiler_params=_cp(("arbitrary",)),
    )(*args)
    return (outs[:3], outs[3:]) if hosted else outs


def _conv_taps(cur, prev8, first):
    row8 = _iota(prev8.shape, 0)
    prev8 = jnp.where(first, 0.0, prev8)
    taps = []
    for s in (3, 2, 1):
        rolled = pltpu.roll(cur, s, 0)
        head = jnp.where(row8 < s, pltpu.roll(prev8, s, 0), rolled[:8])
        taps.append(jnp.concatenate([head, rolled[8:]], axis=0))
    return taps + [cur]


def _conv(taps, w, b):
    acc = b + w[0:1, :] * taps[0]
    for k in (1, 2, 3):
        acc = acc + w[k:k + 1, :] * taps[k]
    return acc


def _expand():
    return (_iota((LANE, D), 1) // 64 == _iota((LANE, D), 0)).astype(BF16)


def _reduce():
    return (_iota((D, LANE), 0) // 64 == _iota((D, LANE), 1)).astype(BF16)


def _ssd_common(xs_c, bc_c, dt_raw, dtb, alog):
    head_lane = _iota((CH, LANE), 1) < NH
    xs = xs_c * _sigmoid(xs_c)
    bc = bc_c * _sigmoid(bc_c)
    pre = dt_raw + dtb
    dt = jnp.where(head_lane, jnp.maximum(pre, 0.0) + jnp.log(1.0 + jnp.exp(-jnp.abs(pre))), 0.0)
    a_row = jnp.where(head_lane[0:1], -jnp.exp(alog), 0.0)
    tri = (_iota((CH, CH), 1) <= _iota((CH, CH), 0)).astype(BF16)
    cs = _pick_left(tri, dt * a_row)
    cs_last = cs[CH - 1:CH, :]
    wide = _pick(jnp.concatenate([dt, jnp.exp(cs), jnp.exp(cs_last - cs)], axis=0), _expand())
    dt_b, e_b, f_b = wide[:CH], wide[CH:2 * CH], wide[2 * CH:]
    return dict(xs=xs, bc=bc, pre=pre, dt=dt, a_row=a_row, cs=cs, cs_t=cs.T, dt_b=dt_b, e_b=e_b, f_b=f_b,
                t_b=e_b[CH - 1:CH, :])


def _groups(bc):
    bcb = bc.astype(BF16)
    return [bcb[:, 0:128], bcb[:, 128:256]], [bcb[:, 256:384], bcb[:, 384:512]]


def _decay(q, h, tril):
    seg = q["cs"][:, h:h + 1] - q["cs_t"][h:h + 1, :]
    return jnp.exp(jnp.where(tril, seg, NEG))


def _ssm_fwd(proj, mix, cw, cb, dtb, alog, d_b, nw):
    def body(xs_ref, xsp_ref, bc_ref, bcp_ref, dt_ref, z_ref, cw_ref, cb_ref, dtb_ref, alog_ref, db_ref, nw_ref,
             mix_in_ref, mix_ref, y_ref, st_ref, conv_ref, h_ref):
        del mix_in_ref
        i = pl.program_id(0)

        @pl.when(i == 0)
        def _():
            h_ref[...] = jnp.zeros_like(h_ref)

        cw, cb = cw_ref[...], cb_ref[...]
        xs_c = _conv(_conv_taps(xs_ref[...], xsp_ref[...], i == 0), cw[:, :D], cb[:, :D])
        bc_c = _conv(_conv_taps(bc_ref[...], bcp_ref[...], i == 0), cw[:, D:], cb[:, D:])
        conv_ref[:, :D] = xs_c
        conv_ref[:, D:] = bc_c
        q = _ssd_common(xs_c, bc_c, dt_ref[...], dtb_ref[...], alog_ref[...])
        bg, cg = _groups(q["bc"])
        xs = q["xs"]
        xdt = xs * q["dt_b"]
        xdt_b = xdt.astype(BF16)
        h_in = h_ref[...]
        st_ref[...] = h_in
        hb = h_in.astype(BF16)
        tril = _iota((CH, CH), 1) <= _iota((CH, CH), 0)
        lane_a = _iota((CH, LANE), 1) < 64
        cbm = [_dot_nt(cg[g], bg[g]) for g in range(2)]
        pairs = []
        for hp in range(NH // 2):
            xp = xdt_b[:, hp * LANE:(hp + 1) * LANE]
            ya = _dot((cbm[hp // 4] * _decay(q, 2 * hp, tril)).astype(BF16), xp)
            yb = _dot((cbm[hp // 4] * _decay(q, 2 * hp + 1, tril)).astype(BF16), xp)
            pairs.append(jnp.where(lane_a, ya, yb))
        y_diag = jnp.concatenate(pairs, axis=1)
        y_off = jnp.concatenate([_dot(cg[g], hb[:, g * 512:(g + 1) * 512]) for g in range(2)], axis=1) * q["e_b"]
        y = y_diag + y_off + db_ref[...] * xs
        y_ref[...] = y
        xf = (xdt * q["f_b"]).astype(BF16)
        h_ref[...] = q["t_b"] * h_in + jnp.concatenate(
            [_dot_tn(bg[g], xf[:, g * 512:(g + 1) * 512]) for g in range(2)], axis=1)
        z = z_ref[...]
        yz = y * (z * _sigmoid(z))
        outs = []
        for g in range(2):
            v = yz[:, g * 512:(g + 1) * 512]
            outs.append(v * lax.rsqrt(jnp.mean(v * v, axis=-1, keepdims=True) + EPS))
        mix_ref[...] = (jnp.concatenate(outs, axis=1) * nw_ref[...]).astype(BF16)

    def col(width, blk, prev=False):
        if prev:
            return pl.BlockSpec((8, width), lambda i: (jnp.maximum(i * (CH // 8) - 1, 0), blk))
        return pl.BlockSpec((CH, width), lambda i: (i, blk))

    def full(a):
        return pl.BlockSpec(a.shape, lambda i: (0,) * a.ndim)

    return pl.pallas_call(
        body, name="ssm_fwd", grid=(NC,),
        in_specs=[col(D, 5), col(D, 5, True), col(512, 12), col(512, 12, True), col(LANE, 52), col(D, 4),
                  full(cw), full(cb), full(dtb), full(alog), full(d_b), full(nw), ANY],
        out_specs=[col(D, 1), col(D, 0), pl.BlockSpec((None, CH, D), lambda i: (i, 0, 0)), col(D + 512, 0)],
        out_shape=[SDS((S, 2 * D), BF16), SDS((S, D), F32), SDS((NC, CH, D), F32), SDS((S, D + 512), F32)],
        scratch_shapes=[pltpu.VMEM((CH, D), F32)],
        input_output_aliases={12: 0},
        compiler_params=_cp(("arbitrary",)),
    )(proj, proj, proj, proj, proj, proj, cw, cb, dtb, alog, d_b, nw, mix)


def _ssm_bwd(proj, dn, y_save, states, conv_out, cw, dtb, alog, d_b, nw):
    def body(xs_ref, bc_ref, dt_ref, z_ref, dn_ref, y_ref, st_ref, conv_ref,
             cw_ref, dtb_ref, alog_ref, db_ref, nw_ref,
             dz_ref, dx_ref, dcw_ref, dcb_ref, dsm_ref, dnw_ref, dh_ref, nxs_ref, nbc_ref):
        i = pl.program_id(0)
        ci = NC - 1 - i

        @pl.when(i == 0)
        def _():
            for ref in (dcw_ref, dcb_ref, dsm_ref, dnw_ref, dh_ref, nxs_ref, nbc_ref):
                ref[...] = jnp.zeros_like(ref)

        cw = cw_ref[...]
        xs_c, bc_c = conv_ref[:, :D], conv_ref[:, D:]
        q = _ssd_common(xs_c, bc_c, dt_ref[...], dtb_ref[...], alog_ref[...])
        bg, cg = _groups(q["bc"])
        xs, dt_b, e_b, f_b, t_b = q["xs"], q["dt_b"], q["e_b"], q["f_b"], q["t_b"]
        xdt = xs * dt_b
        xdt_b = xdt.astype(BF16)
        h_in = st_ref[...]
        hb = h_in.astype(BF16)
        dh_new = dh_ref[...]
        dhb = dh_new.astype(BF16)
        red = _reduce()

        z, y, dn, nw_v = z_ref[...], y_ref[...], dn_ref[...], nw_ref[...]
        sig = _sigmoid(z)
        sz = z * sig
        yz = y * sz
        gdn = dn * nw_v
        dyz, dnw = [], []
        for g in range(2):
            v, gv = yz[:, g * 512:(g + 1) * 512], gdn[:, g * 512:(g + 1) * 512]
            r = lax.rsqrt(jnp.mean(v * v, axis=-1, keepdims=True) + EPS)
            dnw.append(dn[:, g * 512:(g + 1) * 512] * v * r)
            dyz.append(r * (gv - v * (r * r) * jnp.mean(gv * v, axis=-1, keepdims=True)))
        dyz = jnp.concatenate(dyz, axis=1)
        dnw_ref[...] += jnp.sum(jnp.concatenate(dnw, axis=1), axis=0, keepdims=True)
        dy = dyz * sz
        dz_ref[...] = (dyz * y * (sig * (1.0 + z * (1.0 - sig)))).astype(BF16)
        dy_b = dy.astype(BF16)

        tril = _iota((CH, CH), 1) <= _iota((CH, CH), 0)
        lane_a = _iota((CH, LANE), 1) < 64
        cbm = [_dot_nt(cg[g], bg[g]) for g in range(2)]
        dcbm = [jnp.zeros((CH, CH), F32), jnp.zeros((CH, CH), F32)]
        seg_rows = jnp.zeros((CH, LANE), F32)
        seg_cols = jnp.zeros((LANE, CH), F32)
        row_id, col_id = _iota((CH, LANE), 0), _iota((CH, LANE), 1)
        dx_pairs = []
        for hp in range(NH // 2):
            g = hp // 4
            xp = xdt_b[:, hp * LANE:(hp + 1) * LANE]
            dyp_f = dy[:, hp * LANE:(hp + 1) * LANE]
            dyp = dy_b[:, hp * LANE:(hp + 1) * LANE]
            halves = []
            for k in range(2):
                h = 2 * hp + k
                lane = lane_a if k == 0 else jnp.logical_not(lane_a)
                dec = _decay(q, h, tril)
                gm = cbm[g] * dec
                dgm = _dot_nt(jnp.where(lane, dyp_f, 0.0).astype(BF16), xp)
                dcbm[g] = dcbm[g] + dgm * dec
                prod = dgm * gm
                seg_rows = jnp.where(col_id == h, jnp.sum(prod, axis=1, keepdims=True), seg_rows)
                seg_cols = jnp.where(row_id == h, jnp.sum(prod, axis=0, keepdims=True), seg_cols)
                halves.append(_dot_tn(gm.astype(BF16), dyp))
            dx_pairs.append(jnp.where(lane_a, halves[0], halves[1]))
        dxdt_diag = jnp.concatenate(dx_pairs, axis=1)

        qv = jnp.concatenate([_dot(bg[g], dhb[:, g * 512:(g + 1) * 512]) for g in range(2)], axis=1)
        y_off = jnp.concatenate([_dot(cg[g], hb[:, g * 512:(g + 1) * 512]) for g in range(2)], axis=1) * e_b
        xfq = xdt * f_b * qv
        dxdt = dxdt_diag + f_b * qv
        tdt = jnp.sum(dh_new * h_in, axis=0, keepdims=True) * t_b
        per_head = _pick(jnp.concatenate([xfq, dy * y_off, dxdt * xs, dy * xs, jnp.broadcast_to(tdt, (8, D))],
                                         axis=0), red)
        fdf, dyoff_h, dxdtxs_h, dyxs_h = [per_head[k * CH:(k + 1) * CH] for k in range(4)]
        dcs = seg_rows - seg_cols.T + dyoff_h - fdf
        last = per_head[4 * CH:4 * CH + 1] + jnp.sum(fdf, axis=0, keepdims=True)
        dcs = dcs + jnp.where(_iota((CH, LANE), 0) == CH - 1, last, 0.0)
        tri_t = (_iota((CH, CH), 1) >= _iota((CH, CH), 0)).astype(BF16)
        da = _pick_left(tri_t, dcs)
        ddt = da * q["a_row"] + dxdtxs_h
        dxs = dxdt * dt_b + db_ref[...] * dy
        ddt_raw = ddt * _sigmoid(q["pre"])
        dsm_ref[0:1, :] += jnp.sum(ddt_raw, axis=0, keepdims=True)
        dsm_ref[1:2, :] += jnp.sum(da * q["dt"], axis=0, keepdims=True) * q["a_row"]
        dsm_ref[2:3, :] += jnp.sum(dyxs_h, axis=0, keepdims=True)
        edy = (e_b * dy).astype(BF16)
        xf = (xdt * f_b).astype(BF16)
        dbs, dcs_g, dhs = [], [], []
        for g in range(2):
            sl = slice(g * 512, (g + 1) * 512)
            dcb_b = dcbm[g].astype(BF16)
            dcs_g.append(_dot(dcb_b, bg[g]) + _dot_nt(edy[:, sl], hb[:, sl]))
            dbs.append(_dot_tn(dcb_b, cg[g]) + _dot_nt(xf[:, sl], dhb[:, sl]))
            dhs.append(_dot_tn(cg[g], edy[:, sl]))
        dh_ref[...] = t_b * dh_new + jnp.concatenate(dhs, axis=1)
        dbc = jnp.concatenate(dbs + dcs_g, axis=1)

        def conv_bwd(dact, pre, x_raw, w, nxt_ref, lo):
            s = _sigmoid(pre)
            dconv = dact * (s * (1.0 + pre * (1.0 - s)))
            nxt8 = nxt_ref[...]
            row8 = _iota(nxt8.shape, 0)
            hi = lo + dconv.shape[1]
            dcb_ref[:, lo:hi] += jnp.sum(dconv, axis=0, keepdims=True)
            later = [dconv]
            for s_ in (1, 2, 3):
                rolled = pltpu.roll(dconv, CH - s_, 0)
                tail = jnp.where(row8 >= 8 - s_, pltpu.roll(nxt8, 8 - s_, 0), rolled[CH - 8:])
                later.append(jnp.concatenate([rolled[:CH - 8], tail], axis=0))
            dx = None
            for s_, up in enumerate(later):
                k = 3 - s_
                dcw_ref[k:k + 1, lo:hi] += jnp.sum(up * x_raw, axis=0, keepdims=True)
                dx = w[k:k + 1, :] * up if dx is None else dx + w[k:k + 1, :] * up
            nxt_ref[...] = dconv[:8]
            return dx

        dx_ref[:, 0:D] = conv_bwd(dxs, xs_c, xs_ref[...], cw[:, :D], nxs_ref, 0).astype(BF16)
        dx_ref[:, D:D + 512] = conv_bwd(dbc, bc_c, bc_ref[...], cw[:, D:], nbc_ref, D).astype(BF16)
        dx_ref[:, D + 512:D + 640] = ddt_raw.astype(BF16)
        dx_ref[:, D + 640:] = jnp.zeros((CH, D - 640), BF16)

    def col(width, blk):
        return pl.BlockSpec((CH, width), lambda i: (NC - 1 - i, blk))

    def full(a):
        return pl.BlockSpec(a.shape, lambda i: (0,) * len(a.shape))

    acc_shapes = [SDS((4, 1536), F32), SDS((1, 1536), F32), SDS((8, LANE), F32), SDS((1, D), F32)]
    return pl.pallas_call(
        body, name="ssm_bwd", grid=(NC,),
        in_specs=[col(D, 5), col(512, 12), col(LANE, 52), col(D, 4),
                  col(D, 0), col(D, 0), pl.BlockSpec((None, CH, D), lambda i: (NC - 1 - i, 0, 0)), col(D + 512, 0),
                  full(cw), full(dtb), full(alog), full(d_b), full(nw)],
        out_specs=[col(D, 0), col(2 * D, 0)] + [full(a) for a in acc_shapes],
        out_shape=[SDS((S, D), BF16), SDS((S, 2 * D), BF16)] + acc_shapes,
        scratch_shapes=[pltpu.VMEM((CH, D), F32), pltpu.VMEM((8, D), F32), pltpu.VMEM((8, 512), F32)],
        compiler_params=_cp(("arbitrary",)),
    )(proj, proj, proj, proj, dn, y_save, states, conv_out, cw, dtb, alog, d_b, nw)


def _outproj_loss(mix, w_out, x, tgt, nw, attn_pre, proj):
    tm = 256

    def body(mix_ref, w_ref, x_ref, t_ref, nw_ref, pre_ref, g_ref,
             dy_ref, dn_ref, do_ref, delta_ref, dg_ref, dw_ref, dnw_ref, loss_ref):
        @pl.when(pl.program_id(0) == 0)
        def _():
            dw_ref[...] = jnp.zeros_like(dw_ref)
            dnw_ref[...] = jnp.zeros_like(dnw_ref)
            loss_ref[...] = jnp.zeros_like(loss_ref)

        mixv, w = mix_ref[...], w_ref[...]
        out = _dot(mixv, w)
        r = lax.rsqrt(jnp.mean(out * out, axis=-1, keepdims=True) + EPS)
        nh = out * r
        nw_v = nw_ref[...]
        err = x_ref[...] + nh * nw_v - t_ref[...]
        loss_ref[...] += 0.5 * jnp.sum(jnp.mean(err * err, axis=-1, keepdims=True), axis=0, keepdims=True)
        dy = err * (1.0 / D)
        dy_ref[...] = dy
        dnw_ref[...] += jnp.sum(dy * nh, axis=0, keepdims=True)
        gdn = dy * nw_v
        dout = (r * (gdn - nh * jnp.mean(gdn * nh, axis=-1, keepdims=True))).astype(BF16)
        dmix = _dot_nt(dout, w)
        dw_ref[...] += _dot_tn(mixv, dout)
        dn_ref[...] = dmix[:, D:]
        dm, g, pre_v = dmix[:, :D], g_ref[...], pre_ref[...]
        sig = _sigmoid(g)
        do = dm * (g * sig)
        do_ref[...] = do
        dg_ref[...] = (dm * pre_v * (sig * (1.0 + g * (1.0 - sig)))).astype(BF16)
        prod = do * pre_v
        same_head = (_iota((LANE, LANE), 0) // 64 == _iota((LANE, LANE), 1) // 64).astype(BF16)
        for cb in range(D // LANE):
            delta_ref[:, cb * LANE:(cb + 1) * LANE] = _pick(prod[:, cb * LANE:(cb + 1) * LANE], same_head)

    row = lambda w: pl.BlockSpec((tm, w), lambda i: (i, 0))
    full = lambda s: pl.BlockSpec(s, lambda i: (0, 0))
    return pl.pallas_call(
        body, name="outproj_loss", grid=(S // tm,),
        in_specs=[row(2 * D), full((2 * D, D)), row(D), row(D), full((1, D)), row(D),
                  pl.BlockSpec((tm, D), lambda i: (i, OFF_G // D))],
        out_specs=[row(D), row(D), row(D), row(D), row(D), full((2 * D, D)), full((1, D)), full((1, LANE))],
        out_shape=[SDS((S, D), F32)] * 4 + [SDS((S, D), BF16), SDS((2 * D, D), F32), SDS((1, D), F32),
                                            SDS((1, LANE), F32)],
        compiler_params=_cp(("arbitrary",)),
    )(mix, w_out, x, tgt, nw, attn_pre, proj)


def _inproj_bwd_dx(srcs, dxbcdt, w_all, x, dy, nw, hosted=None):
    tm = 512
    nk = DP // D
    n_host, n_host_out = (len(hosted.arrays), len(hosted.out_shape)) if hosted else (0, 0)

    def body(*refs):
        src_refs = refs[:nk]
        w_hbm, x_ref, dy_ref, nw_ref = refs[nk:nk + 4]
        host_in, refs = refs[nk + 4:nk + 4 + n_host], refs[nk + 4 + n_host:]
        gx_ref, dnw_ref = refs[:2]
        host_out, host_sems, (w_ref, w_sem) = refs[2:2 + n_host_out], refs[2 + n_host_out:-2], refs[-2:]
        i = pl.program_id(0)

        def w_copy(k):
            cols = slice(k * D, (k + 1) * D)
            return pltpu.make_async_copy(w_hbm.at[:, cols], w_ref.at[:, cols], w_sem.at[k])

        @pl.when(i == 0)
        def _():
            for k in range(nk):
                w_copy(k).start()
            if hosted:
                hosted.start(host_in, host_out, host_sems)
            dnw_ref[...] = jnp.zeros_like(dnw_ref)

        du = None
        for k, ref in enumerate(src_refs):
            pl.when(i == 0)(w_copy(k).wait)
            width = min(D, 5 * D + X_COLS - k * D)
            part = _dot_nt(ref[:, :width], w_ref[:, k * D:k * D + width])
            du = part if du is None else du + part
        xf, nw_v = x_ref[...], nw_ref[...]
        r = lax.rsqrt(jnp.mean(xf * xf, axis=-1, keepdims=True) + EPS)
        xh = xf * r
        dnw_ref[...] += jnp.sum(du * xh, axis=0, keepdims=True)
        gdu = du * nw_v
        gx_ref[...] = r * (gdu - xh * jnp.mean(gdu * xh, axis=-1, keepdims=True)) + dy_ref[...]

        if hosted:
            pl.when(i == S // tm - 1)(lambda: hosted.finish(host_in, host_out, host_sems))

    row = pl.BlockSpec((tm, D), lambda i: (i, 0))
    row1 = pl.BlockSpec((tm, D), lambda i: (i, 1))
    one = pl.BlockSpec((1, D), lambda i: (0, 0))
    args = [*srcs, dxbcdt, dxbcdt, w_all, x, dy, nw]
    in_specs = [row] * len(srcs) + [row, row1, ANY, row, row, one]
    out_specs, out_shape, scratch = [row, one], [SDS((S, D), F32), SDS((1, D), F32)], []
    if hosted:
        args += hosted.arrays
        in_specs += [ANY] * n_host
        out_specs += [ANY] * n_host_out
        out_shape += hosted.out_shape
        scratch += hosted.scratch
    scratch += [pltpu.VMEM((D, DP), BF16), pltpu.SemaphoreType.DMA((nk,))]
    outs = pl.pallas_call(
        body, name="inproj_bwd_dx", grid=(S // tm,),
        in_specs=in_specs, out_specs=out_specs, out_shape=out_shape, scratch_shapes=scratch,
        compiler_params=_cp(("arbitrary",)),
    )(*args)
    return (outs[:2], outs[2:]) if hosted else outs


def _dw(u, dsec, name, width=D, hosted=None):
    ts = 1024
    n_host, n_host_out = (len(hosted.arrays), len(hosted.out_shape)) if hosted else (0, 0)

    def body(u_ref, d_ref, *refs):
        host_in, o_ref, refs = refs[:n_host], refs[n_host], refs[n_host + 1:]
        host_out, host_sems = refs[:n_host_out], refs[n_host_out:]
        i = pl.program_id(0)

        @pl.when(i == 0)
        def _():
            if hosted:
                hosted.start(host_in, host_out, host_sems)
            o_ref[...] = jnp.zeros_like(o_ref)

        o_ref[...] += _dot_tn(u_ref[...], d_ref[...])
        if hosted:
            pl.when(i == S // ts - 1)(lambda: hosted.finish(host_in, host_out, host_sems))

    outs = pl.pallas_call(
        body, name=name, grid=(S // ts,),
        in_specs=[pl.BlockSpec((ts, D), lambda i: (i, 0)), pl.BlockSpec((ts, width), lambda i: (i, 0))]
        + [ANY] * n_host,
        out_specs=[pl.BlockSpec((D, width), lambda i: (0, 0))] + [ANY] * n_host_out,
        out_shape=[SDS((D, width), F32)] + (hosted.out_shape if hosted else []),
        scratch_shapes=hosted.scratch if hosted else [],
        compiler_params=_cp(("arbitrary",)),
    )(u, dsec, *(hosted.arrays if hosted else []))
    return (outs[0], outs[1:]) if hosted else outs[0]


def _dw_pair(u, da, db, name):
    ts = 1024
    last = S // ts - 1

    def body(u_ref, a_ref, b_ref, o_ref):
        j = pl.program_id(0)

        @pl.when(pl.program_id(1) == 0)
        def _():
            o_ref[...] = jnp.zeros_like(o_ref)

        for k, d_ref in enumerate((a_ref, b_ref)):
            @pl.when(j == k)
            def _():
                o_ref[...] += _dot_tn(u_ref[...], d_ref[...])

    return pl.pallas_call(
        body, name=name, grid=(2, S // ts),
        in_specs=[pl.BlockSpec((ts, D), lambda j, i: (i, 0)),
                  pl.BlockSpec((ts, D), lambda j, i: (jnp.where(j == 0, i, last), 0)),
                  pl.BlockSpec((ts, D), lambda j, i: (jnp.where(j == 1, i, 0), 0))],
        out_specs=pl.BlockSpec((D, D), lambda j, i: (0, j)),
        out_shape=SDS((D, 2 * D), F32),
        compiler_params=_cp(("arbitrary", "arbitrary")),
    )(u, da, db)


def _place():
    x, y, c = lax.axis_index("x"), lax.axis_index("y"), lax.axis_index("c")
    return x, y, c, 2 * x + y


def _chip_of(x, y, k):
    px = 1 - x if k & 2 else x
    py = 1 - y if k & 1 else y
    return px, py, 2 * px + py


def _remote(src, dst, send_sem, recv_sem, dev):
    return pltpu.make_async_remote_copy(src_ref=src, dst_ref=dst, send_sem=send_sem, recv_sem=recv_sem,
                                        device_id=dev, device_id_type=MESH)


def _gather_weights(w_in_b):
    half = w_in_b.shape[0] // 2
    quarter = half // 2

    def body(src, dst, send, recv):
        x, y, c, j = _place()
        me, sib = (x, y, c), (x, y, 1 - c)
        nbr = {"x": _chip_of(x, y, 2), "y": _chip_of(x, y, 1)}
        diag = _chip_of(x, y, 3)[2]
        started, arrivals = [], []

        def rows(n_quarter=None, sibling=False):
            base = (1 - c if sibling else c) * half
            return pl.ds(base, half) if n_quarter is None else pl.ds(base + n_quarter * quarter, quarter)

        def sem(n):
            return send.at[n], recv.at[n]

        def go(cp):
            cp.start()
            started.append(cp)

        own = _remote(src, dst.at[j], *sem(8), sib)
        go(own)
        for n, axis in enumerate("xy"):
            px, py, _ = nbr[axis]
            go(_remote(src.at[rows()], dst.at[j, rows()], *sem(n), (px, py, c)))
        for n, axis in enumerate("xy"):
            ox, oy, _ = nbr["y" if axis == "x" else "x"]
            pj = nbr[axis][2]
            _remote(src.at[rows()], dst.at[pj, rows()], *sem(n), me).wait_recv()
            go(_remote(dst.at[pj, rows(n)], dst.at[pj, rows(n)], *sem(2 + n), (ox, oy, c)))
            go(_remote(dst.at[pj, rows()], dst.at[pj, rows()], *sem(4 + n), sib))
            arrivals.append(_remote(src.at[rows()], dst.at[pj, rows(None, True)], *sem(4 + n), me))
        for n in range(2):
            _remote(dst.at[diag, rows(n)], dst.at[diag, rows(n)], *sem(2 + n), me).wait_recv()
            go(_remote(dst.at[diag, rows(n)], dst.at[diag, rows(n)], *sem(6 + n), sib))
            arrivals.append(_remote(dst.at[diag, rows(n, True)], dst.at[diag, rows(n, True)], *sem(6 + n), me))
        for cp in arrivals + [own]:
            cp.wait_recv()
        for cp in started:
            cp.wait_send()

    return pl.pallas_call(
        body, name="gather_weights", in_specs=[ANY], out_specs=ANY,
        out_shape=SDS((4,) + w_in_b.shape, BF16),
        scratch_shapes=[pltpu.SemaphoreType.DMA((9,)), pltpu.SemaphoreType.DMA((9,))],
        compiler_params=pltpu.CompilerParams(has_side_effects=True),
    )(w_in_b)


class _LateGather:
    def __init__(self, w_out_b, conv_w):
        self.arrays = [w_out_b, conv_w]
        self.out_shape = [SDS((4,) + w_out_b.shape, BF16), SDS((4,) + conv_w.shape, F32)]
        self.scratch = [pltpu.SemaphoreType.DMA((11,)), pltpu.SemaphoreType.DMA((11,))]

    def _plan(self, ins, outs, sems):
        x, y, c, j = _place()
        send, recv = sems
        (wo, cw), (gwo, gcw) = ins, outs
        half = wo.shape[0] // 2
        mine, theirs = pl.ds(c * half, half), pl.ds((1 - c) * half, half)
        me, sib = (x, y, c), (x, y, 1 - c)
        first, arrive, forward, last = [], [], [], []
        for k in (1, 2, 3):
            px, py, pj = _chip_of(x, y, k)
            first += [_remote(wo.at[mine], gwo.at[j, mine], send.at[k - 1], recv.at[k - 1], (px, py, c)),
                      _remote(cw, gcw.at[j], send.at[k + 2], recv.at[k + 2], (px, py, c))]
            arrive.append(_remote(wo.at[mine], gwo.at[pj, mine], send.at[k - 1], recv.at[k - 1], me))
            forward.append(_remote(gwo.at[pj, mine], gwo.at[pj, mine], send.at[k + 5], recv.at[k + 5], sib))
            last += [_remote(cw, gcw.at[pj], send.at[k + 2], recv.at[k + 2], me),
                     _remote(wo.at[theirs], gwo.at[pj, theirs], send.at[k + 5], recv.at[k + 5], me)]
        first += [_remote(wo, gwo.at[j], send.at[9], recv.at[9], sib),
                  _remote(cw, gcw.at[j], send.at[10], recv.at[10], sib)]
        last += first[-2:]
        return first, arrive, forward, last

    def start(self, ins, outs, sems):
        for cp in self._plan(ins, outs, sems)[0]:
            cp.start()

    def pass_on(self, ins, outs, sems):
        _, arrive, forward, _ = self._plan(ins, outs, sems)
        for got, fwd in zip(arrive, forward):
            got.wait_recv()
            fwd.start()

    def finish(self, ins, outs, sems):
        first, _, forward, last = self._plan(ins, outs, sems)
        for cp in last:
            cp.wait_recv()
        for cp in first + forward:
            cp.wait_send()


def _window(s, names):
    lo, hi = TILES * s, TILES * s + TILES + 1
    pieces = []
    for n, name in enumerate(names):
        a, count = SECTION_TILES[name]
        first, last = max(lo, a), min(hi, a + count)
        if first < last:
            pieces.append((n, first - a, last - first, first - lo))
    assert sum(p[2] for p in pieces) == TILES + 1
    return pieces


class _PairExchange:
    def __init__(self, names, sections, shards, more=()):
        self.names, self.shards = names, shards
        self.there = [n for n, a in enumerate(sections) if a is not None]
        self.arrays = [sections[n] for n in self.there] + list(more)
        self.out_shape = [SDS((len(shards), D // 2, WIN), F32)]
        self.out_shape += [SDS((a.shape[0], a.shape[1] // 2, a.shape[2]), F32) for a in more]
        n = sum(p[0] in self.there for s in shards for p in _window(s, names)) + len(more)
        self.scratch = [pltpu.SemaphoreType.DMA((n,)) for _ in range(2)]

    def _copies(self, ins, outs, sems):
        x, y, c, _ = _place()
        sib = (x, y, 1 - c)
        rows = pl.ds((1 - c) * (D // 2), D // 2)
        k = 0
        for i, s in enumerate(self.shards):
            for n, tile, tiles, at in _window(s, self.names):
                if n in self.there:
                    yield _remote(ins[self.there.index(n)].at[rows, pl.ds(tile * LANE, tiles * LANE)],
                                  outs[0].at[i, :, pl.ds(at * LANE, tiles * LANE)], sems[0].at[k], sems[1].at[k], sib)
                    k += 1
        for src, dst in zip(ins[len(self.there):], outs[1:]):
            half = src.shape[1] // 2
            yield _remote(src.at[:, pl.ds((1 - c) * half, half)], dst, sems[0].at[k], sems[1].at[k], sib)
            k += 1

    def start(self, ins, outs, sems):
        for cp in self._copies(ins, outs, sems):
            cp.start()

    def finish(self, ins, outs, sems):
        for cp in self._copies(ins, outs, sems):
            cp.wait()


def _exchange_call(exchange, name, into=None):
    n, n_out = len(exchange.arrays), len(exchange.out_shape)
    given = list(into) if into else []

    def body(*refs):
        ins, outs, sems = refs[:n], refs[n + len(given):n + len(given) + n_out], refs[n + len(given) + n_out:]
        exchange.start(ins, outs, sems)
        exchange.finish(ins, outs, sems)

    return pl.pallas_call(
        body, name=name, in_specs=[ANY] * (n + len(given)), out_specs=[ANY] * n_out, out_shape=exchange.out_shape,
        input_output_aliases={n + k: k for k in range(len(given))},
        scratch_shapes=exchange.scratch, compiler_params=pltpu.CompilerParams(has_side_effects=True),
    )(*exchange.arrays, *given)


def _pair_sum_windows(cidx, names, sections, shards, r, name):
    n, half, _ = r.shape
    tr = min(half, 256)
    nt = half // tr

    def body(c_ref, *refs):
        del c_ref
        secs, r_ref, o_ref = refs[:-2], refs[-2], refs[-1]
        for i, s in enumerate(shards):
            for k, tile, tiles, at in _window(s, names):
                own = secs[k][:, tile * LANE:(tile + tiles) * LANE]
                there = slice(at * LANE, (at + tiles) * LANE)
                o_ref[i, :, there] = (own + r_ref[i, :, there]).astype(BF16)

    window = pl.BlockSpec((n, tr, WIN), lambda t, c: (0, t, 0))
    return pl.pallas_call(
        body, name=name,
        grid_spec=pltpu.PrefetchScalarGridSpec(
            num_scalar_prefetch=1, grid=(nt,),
            in_specs=[pl.BlockSpec((tr, a.shape[1]), lambda t, c: (c[0] * nt + t, 0)) for a in sections] + [window],
            out_specs=window),
        out_shape=SDS(r.shape, BF16),
        compiler_params=_cp(("parallel",)),
    )(cidx, *sections, r)


def _pair_sum(cidx, g, r, name):
    n, half, width = r.shape
    tr = min(half, 256)
    nt = half // tr

    def body(c_ref, g_ref, r_ref, o_ref):
        del c_ref
        o_ref[...] = (g_ref[...] + r_ref[...]).astype(BF16)

    return pl.pallas_call(
        body, name=name,
        grid_spec=pltpu.PrefetchScalarGridSpec(
            num_scalar_prefetch=1, grid=(n, nt),
            in_specs=[pl.BlockSpec((None, tr, width), lambda s, t, c: (s, c[0] * nt + t, 0)),
                      pl.BlockSpec((None, tr, width), lambda s, t, c: (s, t, 0))],
            out_specs=pl.BlockSpec((None, tr, width), lambda s, t, c: (s, t, 0))),
        out_shape=SDS(r.shape, BF16),
        compiler_params=_cp(("parallel", "parallel")),
    )(cidx, g, r)


class _ChipExchange:
    def __init__(self, arrays, rows):
        self.arrays, self.rows = list(arrays), list(rows)
        self.out_shape = [SDS((4,) + a.shape[1:], BF16) for a in self.arrays]
        self.scratch = [pltpu.SemaphoreType.DMA((3 * len(self.arrays),)) for _ in range(2)]

    def _copies(self, ins, outs, sems):
        x, y, c, j = _place()
        send, recv = sems
        for a, (src, dst, row) in enumerate(zip(ins, outs, self.rows)):
            for k in (1, 2, 3):
                px, py, pj = _chip_of(x, y, k)
                n = 3 * a + k - 1
                slot = pj if row is None else py
                yield (None if row is None else px == row, None if row is None else x == row,
                       _remote(src.at[slot], dst.at[j], send.at[n], recv.at[n], (px, py, c)),
                       _remote(src.at[0], dst.at[pj], send.at[n], recv.at[n], (x, y, c)))

    def start(self, ins, outs, sems):
        for sends, _, send, _ in self._copies(ins, outs, sems):
            if sends is None:
                send.start()
            else:
                pl.when(sends)(send.start)

    def finish(self, ins, outs, sems):
        for sends, owns, send, arrival in self._copies(ins, outs, sems):
            if sends is None:
                arrival.wait_recv()
                send.wait_send()
            else:
                pl.when(owns)(arrival.wait_recv)
                pl.when(sends)(send.wait_send)


def _all_gather_rows(src, dst, rows, send, recv, local_sem):
    x, y, c, j = _place()
    me = 2 * j + c
    local = pltpu.make_async_copy(src, dst.at[me, rows], local_sem)
    cps, arrivals = [], []
    for k in range(1, 8):
        px, py, pj = _chip_of(x, y, k >> 1)
        pc = 1 - c if k & 1 else c
        cps.append(_remote(src, dst.at[me, rows], send.at[k - 1], recv.at[k - 1], (px, py, pc)))
        arrivals.append(_remote(src, dst.at[2 * pj + pc, rows], send.at[k - 1], recv.at[k - 1], (x, y, c)))
    starts = [local.start] + [cp.start for cp in cps]
    waits = [cp.wait_recv for cp in arrivals] + [cp.wait_send for cp in cps] + [local.wait]
    return starts, waits


class _SmallExchange:
    def __init__(self, small):
        self.arrays = [small]
        self.out_shape = [SDS((8,) + small.shape, F32)]
        self.scratch = [pltpu.SemaphoreType.DMA((7,)), pltpu.SemaphoreType.DMA((7,)), pltpu.SemaphoreType.DMA]

    def start(self, ins, outs, sems):
        for go in _all_gather_rows(ins[0], outs[0], slice(None), *sems)[0]:
            go()

    def finish(self, ins, outs, sems):
        for wait in _all_gather_rows(ins[0], outs[0], slice(None), *sems)[1]:
            wait()


class _Both:
    def __init__(self, a, b):
        self.parts = (a, b)
        self.arrays, self.out_shape, self.scratch = a.arrays + b.arrays, a.out_shape + b.out_shape, a.scratch + b.scratch

    def _split(self, ins, outs, sems):
        a, b = self.parts
        return ((a, ins[:len(a.arrays)], outs[:len(a.out_shape)], sems[:len(a.scratch)]),
                (b, ins[len(a.arrays):], outs[len(a.out_shape):], sems[len(a.scratch):]))

    def start(self, ins, outs, sems):
        for part, *refs in self._split(ins, outs, sems):
            part.start(*refs)

    def finish(self, ins, outs, sems):
        for part, *refs in self._split(ins, outs, sems):
            part.finish(*refs)


def _slot_sum(r, name):
    n, rows, width = r.shape
    tr = min(rows, 256)

    def body(r_ref, o_ref):
        acc = r_ref[0].astype(F32)
        for s in range(1, n):
            acc = acc + r_ref[s].astype(F32)
        o_ref[...] = acc

    return pl.pallas_call(
        body, name=name, grid=(rows // tr,),
        in_specs=[pl.BlockSpec((n, tr, width), lambda t: (0, t, 0))],
        out_specs=pl.BlockSpec((tr, width), lambda t: (t, 0)),
        out_shape=SDS((rows, width), F32),
        compiler_params=_cp(("parallel",)),
    )(r)


def _chip_sum(where, recv, own, name):
    n, rows, width = recv.shape
    tr = min(rows, 256)
    nt = rows // tr

    def body(j_ref, r_ref, own_ref, o_ref):
        acc = None
        for s in range(n):
            term = jnp.where(j_ref[0] == s, own_ref[...], r_ref[s]).astype(F32)
            acc = term if acc is None else acc + term
        o_ref[...] = acc

    return pl.pallas_call(
        body, name=name,
        grid_spec=pltpu.PrefetchScalarGridSpec(
            num_scalar_prefetch=1, grid=(nt,),
            in_specs=[pl.BlockSpec((n, tr, width), lambda t, j: (0, t, 0)),
                      pl.BlockSpec((None, tr, width), lambda t, j: (j[0], t, 0))],
            out_specs=pl.BlockSpec((tr, width), lambda t, j: (j[1] * nt + t, 0))),
        out_shape=SDS((2 * rows, width), F32),
        compiler_params=_cp(("parallel",)),
    )(where, recv, own)


def _chip_sum_rows(place, recv0, own0, recv1, own1, name):
    n, rows, width = recv0.shape
    tr = min(rows, 256)
    nt = rows // tr

    def body(p_ref, r0_ref, o0_ref, r1_ref, o1_ref, o_ref):
        first_row = p_ref[2] == 0
        own = jnp.where(first_row, o0_ref[...], o1_ref[...])
        acc = None
        for s in range(n):
            term = jnp.where(p_ref[0] == s, own, jnp.where(first_row, r0_ref[s], r1_ref[s])).astype(F32)
            acc = term if acc is None else acc + term
        o_ref[...] = acc

    recv = pl.BlockSpec((n, tr, width), lambda t, p: (0, t, 0))
    own = pl.BlockSpec((None, tr, width), lambda t, p: (p[3], t, 0))
    return pl.pallas_call(
        body, name=name,
        grid_spec=pltpu.PrefetchScalarGridSpec(
            num_scalar_prefetch=1, grid=(nt,), in_specs=[recv, own, recv, own],
            out_specs=pl.BlockSpec((tr, width), lambda t, p: (p[1] * nt + t, 0))),
        out_shape=SDS((2 * rows, width), F32),
        compiler_params=_cp(("parallel",)),
    )(place, recv0, own0, recv1, own1)


def _half_exchange(gw, go, gathered, late, row):
    def body(gw_in, go_in, ga_in, late_ref, gw_ref, go_ref, ga_ref, send, recv, late_send, late_recv, late_local):
        del gw_in, go_in, ga_in
        x, y, c, _ = _place()
        starts, waits = _all_gather_rows(late_ref, ga_ref, pl.ds(row, late.shape[0]), late_send, late_recv,
                                         late_local)
        for go_ in starts:
            go_()
        mine = [pl.ds(c * (r.shape[0] // 2), r.shape[0] // 2) for r in (gw_ref, go_ref)]
        cps = [_remote(r.at[rows], r.at[rows], send.at[k], recv.at[k], (x, y, 1 - c))
               for k, (r, rows) in enumerate(zip((gw_ref, go_ref), mine))]
        for cp in cps:
            cp.start()
        for k, r in enumerate((gw_ref, go_ref)):
            theirs = pl.ds((1 - c) * (r.shape[0] // 2), r.shape[0] // 2)
            _remote(r.at[theirs], r.at[theirs], send.at[k], recv.at[k], (x, y, c)).wait_recv()
        for cp in cps:
            cp.wait_send()
        for wait in waits:
            wait()

    return pl.pallas_call(
        body, name="half_exchange", in_specs=[ANY] * 4, out_specs=[ANY] * 3,
        out_shape=[SDS(gw.shape, F32), SDS(go.shape, F32), SDS(gathered.shape, F32)],
        input_output_aliases={0: 0, 1: 1, 2: 2},
        scratch_shapes=[pltpu.SemaphoreType.DMA((2,)), pltpu.SemaphoreType.DMA((2,)),
                        pltpu.SemaphoreType.DMA((7,)), pltpu.SemaphoreType.DMA((7,)), pltpu.SemaphoreType.DMA],
        compiler_params=pltpu.CompilerParams(has_side_effects=True),
    )(gw, go, gathered, late)


def _adamw(w, g, m, v, name):
    rows, width = w.shape
    tr = min(rows, 256)

    def body(w_ref, g_ref, m_ref, v_ref, d_ref, nm_ref, nv_ref):
        gv = g_ref[...]
        nm = ADAM_B1 * m_ref[...] + (1.0 - ADAM_B1) * gv
        nv = ADAM_B2 * v_ref[...] + (1.0 - ADAM_B2) * (gv * gv)
        m_hat = nm / (1.0 - ADAM_B1 ** ADAM_STEP)
        v_hat = nv / (1.0 - ADAM_B2 ** ADAM_STEP)
        d_ref[...] = -ADAM_LR * (m_hat / (jnp.sqrt(v_hat) + ADAM_EPS) + ADAM_WD * w_ref[...])
        nm_ref[...] = nm
        nv_ref[...] = nv

    t = pl.BlockSpec((tr, width), lambda i: (i, 0))
    return pl.pallas_call(
        body, name=name, grid=(rows // tr,), in_specs=[t] * 4, out_specs=[t] * 3,
        out_shape=[SDS(w.shape, F32)] * 3, compiler_params=_cp(("parallel",)),
    )(w, g, m, v)


def _rowwise(a):
    return jnp.transpose(a, (2, 0, 1)).reshape(SHARD * D // LANE, LANE)


def _columns(ref, base=0):
    return jnp.concatenate([ref[pl.ds(base + c, LANE, stride=8), :].T for c in range(D // LANE)], axis=0)


def _shard_bf16(chip, w_rows):
    def body(j_ref, w_ref, o_ref, prev_ref):
        t = pl.program_id(0)
        cur = _columns(w_ref)

        @pl.when(t == 0)
        def _():
            prev_ref[...] = jnp.zeros_like(prev_ref)

        lane = _iota((D, LANE), 1)
        for s in range(4):
            @pl.when(j_ref[0] == s)
            def _():
                off = SHIFT * s
                moved = cur if s == 0 else jnp.where(lane < off, pltpu.roll(prev_ref[...], off, 1),
                                                     pltpu.roll(cur, off, 1))
                col = t * LANE + lane - off
                o_ref[...] = jnp.where((col >= 0) & (col < SHARD), moved, 0.0).astype(BF16)
        prev_ref[...] = cur

    return pl.pallas_call(
        body, name="shard_bf16",
        grid_spec=pltpu.PrefetchScalarGridSpec(
            num_scalar_prefetch=1, grid=(TILES + 1,),
            in_specs=[pl.BlockSpec((D, LANE), lambda t, j: (t, 0))],
            out_specs=pl.BlockSpec((D, LANE), lambda t, j: (0, t)),
            scratch_shapes=[pltpu.VMEM((D, LANE), F32)]),
        out_shape=SDS((D, WIN), BF16), compiler_params=_cp(("arbitrary",)),
    )(chip, w_rows)


def _whole_w_in(windows):
    tr = 256
    n = windows.shape[0]

    def body(g_ref, o_ref):
        lane = _iota((tr, LANE), 1)
        for s in range(n):
            first = TILES * s
            head = g_ref[s, :, :LANE]
            if s:
                tail = g_ref[s - 1, :, TILES * LANE:]
                head = jnp.where(lane < SHIFT * s, tail.astype(F32), head.astype(F32)).astype(BF16)
            o_ref[:, first * LANE:(first + 1) * LANE] = head
            o_ref[:, (first + 1) * LANE:(first + TILES) * LANE] = g_ref[s, :, LANE:TILES * LANE]
        o_ref[:, n * TILES * LANE:(n * TILES + 1) * LANE] = g_ref[n - 1, :, TILES * LANE:]
        o_ref[:, (n * TILES + 1) * LANE:] = jnp.zeros((tr, DP - (n * TILES + 1) * LANE), BF16)

    return pl.pallas_call(
        body, name="whole_w_in", grid=(D // tr,),
        in_specs=[pl.BlockSpec((n, tr, WIN), lambda t: (0, t, 0))], out_specs=pl.BlockSpec((tr, DP), lambda t: (t, 0)),
        out_shape=SDS((D, DP), BF16), compiler_params=_cp(("parallel",)),
    )(windows)


def _own_buffer(a, name):
    tr = 512
    block = pl.BlockSpec((tr, a.shape[1]), lambda t: (t, 0))

    def body(a_ref, o_ref):
        o_ref[...] = a_ref[...]

    return pl.pallas_call(
        body, name=name, grid=(a.shape[0] // tr,), in_specs=[block], out_specs=block,
        out_shape=SDS(a.shape, a.dtype), compiler_params=_cp(("parallel",)),
    )(a)


def _shard_of_window(chip, g_win):
    tr = 128

    def body(j_ref, g_ref, grad_ref):
        for s in range(4):
            @pl.when(j_ref[0] == s)
            def _():
                back = LANE - SHIFT * s
                from_this = _iota((tr, LANE), 1) < back

                def moved(t):
                    tile = g_ref[:, t * LANE:(t + 1) * LANE]
                    return pltpu.roll(tile, back, 1) if s else tile

                for t in range(TILES):
                    grad_ref[:, t * LANE:(t + 1) * LANE] = jnp.where(from_this, moved(t), moved(t + 1)) if s else moved(t)
                grad_ref[:, TILES * LANE:] = moved(TILES)[:, :SHARD - TILES * LANE]

    return pl.pallas_call(
        body, name="shard_of_window",
        grid_spec=pltpu.PrefetchScalarGridSpec(
            num_scalar_prefetch=1, grid=(D // tr,), in_specs=[pl.BlockSpec((tr, WIN), lambda t, j: (t, 0))],
            out_specs=pl.BlockSpec((tr, SHARD), lambda t, j: (t, 0))),
        out_shape=SDS((D, SHARD), F32), compiler_params=_cp(("parallel",)),
    )(chip, g_win)


def _adamw_in(w_rows, g, m_rows, v_rows):
    per_step = 2

    def body(w_ref, g_ref, m_ref, v_ref, d_ref, nm_ref, nv_ref):
        for a in range(per_step):
            cols = slice(a * LANE, (a + 1) * LANE)
            gv = g_ref[:, cols]
            nm = ADAM_B1 * _columns(m_ref, a * D) + (1.0 - ADAM_B1) * gv
            nv = ADAM_B2 * _columns(v_ref, a * D) + (1.0 - ADAM_B2) * (gv * gv)
            m_hat = nm / (1.0 - ADAM_B1 ** ADAM_STEP)
            v_hat = nv / (1.0 - ADAM_B2 ** ADAM_STEP)
            d_ref[:, cols] = -ADAM_LR * (m_hat / (jnp.sqrt(v_hat) + ADAM_EPS) + ADAM_WD * _columns(w_ref, a * D))
            nm_ref[:, cols] = nm
            nv_ref[:, cols] = nv

    tile = pl.BlockSpec((D, per_step * LANE), lambda t: (0, t))
    rows = pl.BlockSpec((per_step * D, LANE), lambda t: (t, 0))
    return pl.pallas_call(
        body, name="adamw_in", grid=(pl.cdiv(TILES + 1, per_step),), in_specs=[rows, tile, rows, rows],
        out_specs=[tile] * 3, out_shape=[SDS(g.shape, F32)] * 3, compiler_params=_cp(("parallel",)),
    )(w_rows, g, m_rows, v_rows)


def _rows128(a, rows):
    flat = a.reshape(-1)
    return jnp.pad(flat, (0, rows * LANE - flat.shape[0])).reshape(rows, LANE)


CONV_ROWS = 48


def _pack_small(conv_w, norm_pre, conv_b, ssm_norm, norm_post, dtb, alog, dsk, extra=None):
    cw_rows = CONV_ROWS if conv_w.shape[-1] == 1536 else 16
    extra = jnp.zeros((1, LANE), F32) if extra is None else _rows128(extra, 1)
    vec = jnp.concatenate([_rows128(dtb, 1), _rows128(alog, 1), _rows128(dsk, 1), extra, jnp.zeros((4, LANE), F32)],
                          axis=0)
    return jnp.concatenate([_rows128(conv_w, cw_rows), _rows128(norm_pre, 8), _rows128(conv_b, 16),
                            _rows128(ssm_norm, 8), _rows128(norm_post, 8), vec], axis=0)


def _unpack_small(p, cw_cols):
    cw_rows = CONV_ROWS if cw_cols == 1536 else 16
    o = cw_rows
    conv_w = p[:cw_rows].reshape(-1)[:4 * cw_cols].reshape(1, 4, cw_cols)
    norm_pre = p[o:o + 8].reshape(1, D)
    conv_b = p[o + 8:o + 24].reshape(-1)[:1536].reshape(1, 1536)
    ssm_norm = p[o + 24:o + 32].reshape(1, D)
    norm_post = p[o + 32:o + 40].reshape(1, D)
    vec = p[o + 40:o + 48]
    return conv_w, norm_pre, conv_b, ssm_norm, norm_post, vec[0:1, :NH], vec[1:2, :NH], vec[2:3, :NH], vec[3, 0]


def _pad_lanes(a):
    return jnp.pad(a, ((0, 0), (0, LANE - a.shape[1])))


class _GradReduce:
    LO, HI = ("qk", "v", "gz"), ("gz", "x")

    def __init__(self, xi, yi, ci):
        self.cidx = jnp.reshape(ci, (1,)).astype(jnp.int32)
        self.place = jnp.stack([2 * xi + yi, ci, xi, yi]).astype(jnp.int32)

    def pairs(self, dw_gz, dw_x, dw_out):
        self.hi = [dw_gz, dw_x]
        self.go = dw_out.reshape(4, D // 2, D)
        return _PairExchange(self.HI, self.hi, (2, 3), [self.go])

    def first(self, got):
        rw, ro = got
        self.pw_hi = _pair_sum_windows(self.cidx, self.HI, self.hi, (2, 3), rw, "pair_sum_hi")
        self.po = _pair_sum(self.cidx, self.go, ro, "pair_sum_out")
        return _ChipExchange([self.pw_hi, self.po], [1, None])

    def first_done(self, got):
        self.rw_hi, self.ro = got

    def second_pairs(self, dw_qk, dw_gz):
        self.lo = [dw_qk, None, dw_gz]
        return _PairExchange(self.LO, self.lo, (0, 1))

    def second(self, dw_v, got, small):
        rest = _PairExchange(self.LO, [None, dw_v, None], (0, 1))
        (rw,) = _exchange_call(rest, "pair_exchange_v", into=got)
        lo = [dw_v if a is None else a for a in self.lo]
        self.pw_lo = _pair_sum_windows(self.cidx, self.LO, lo, (0, 1), rw, "pair_sum_lo")
        return _Both(_ChipExchange([self.pw_lo], [0]), _SmallExchange(small))

    def second_done(self, got):
        self.rw_lo, self.small = got

    def result(self, late, row):
        half_in = _chip_sum_rows(self.place, self.rw_lo, self.pw_lo, self.rw_hi, self.pw_hi, "chip_sum_in")
        half_out = _chip_sum(self.place[0:2], self.ro, self.po, "chip_sum_out")
        return _half_exchange(half_in, half_out, self.small, late, row)


def kernel(x, norm_pre_w, w_in, conv_w, conv_b, dt_bias, a_log, d_skip, ssm_norm_w, w_out, norm_post_w, loss_target, m_norm_pre_w, m_w_in, m_conv_w, m_conv_b, m_dt_bias, m_a_log, m_d_skip, m_ssm_norm_w, m_w_out, m_norm_post_w, v_norm_pre_w, v_w_in, v_conv_w, v_conv_b, v_dt_bias, v_a_log, v_d_skip, v_ssm_norm_w, v_w_out, v_norm_post_w):
    xi, yi, ci = lax.axis_index("x"), lax.axis_index("y"), lax.axis_index("c")
    chip = 2 * xi + yi
    x2, tgt = x[0], loss_target[0]

    chip_idx = jnp.reshape(chip, (1,)).astype(jnp.int32)
    w_rows = _rowwise(w_in)
    w_all = _whole_w_in(_gather_weights(_shard_bf16(chip_idx, w_rows)))
    reduce = _GradReduce(xi, yi, ci)
    grad_x, dnw_pre = _local_step(x2, tgt, w_all, _LateGather(w_out[0].astype(BF16), conv_w[0]), norm_pre_w, conv_b,
                                  dt_bias, a_log, d_skip, ssm_norm_w, norm_post_w, reduce)
    g_win, g_out, small = reduce.result(_rows128(dnw_pre, D // LANE), CONV_ROWS)
    g_small = _slot_sum(small, "small_sum")
    g_cw, g_npre, g_cb, g_nssm, g_npost, g_dtb, g_alog, g_dsk, loss = _unpack_small(g_small, 1536)
    g_cw = lax.dynamic_slice_in_dim(g_cw, chip * 384, 384, axis=2)

    g_in = _shard_of_window(chip_idx, g_win)
    d_in, nm_in, nv_in = _adamw_in(w_rows, g_in, _rowwise(m_w_in), _rowwise(v_w_in))
    grad_x = _own_buffer(grad_x, "grad_x_copy")
    d_out, nm_out, nv_out = _adamw(w_out[0], g_out, m_w_out[0], v_w_out[0], "adamw_out")
    packed = [_pack_small(*t) for t in (
        (conv_w, norm_pre_w, conv_b, ssm_norm_w, norm_post_w, dt_bias, a_log, d_skip),
        (g_cw, g_npre, g_cb, g_nssm, g_npost, g_dtb, g_alog, g_dsk),
        (m_conv_w, m_norm_pre_w, m_conv_b, m_ssm_norm_w, m_norm_post_w, m_dt_bias, m_a_log, m_d_skip),
        (v_conv_w, v_norm_pre_w, v_conv_b, v_ssm_norm_w, v_norm_post_w, v_dt_bias, v_a_log, v_d_skip))]
    small_out = [_unpack_small(p, 384)[:8] for p in _adamw(*packed, "adamw_small")]

    def ordered(cw_, npre, cb_, nssm, npost, dtb_, alog_, dsk_, big_in, big_out):
        return [npre, big_in[None], cw_, cb_, dtb_, alog_, dsk_, nssm, big_out[None], npost]

    grads = ordered(g_cw, g_npre, g_cb, g_nssm, g_npost, g_dtb, g_alog, g_dsk, g_in, g_out)
    deltas = ordered(*small_out[0], d_in, d_out)
    new_m = ordered(*small_out[1], nm_in, nm_out)
    new_v = ordered(*small_out[2], nv_in, nv_out)
    return (loss, grad_x[None], *grads, *deltas, *new_m, *new_v)


def _local_step(x2, tgt, w_all, late, norm_pre_w, conv_b, dt_bias, a_log, d_skip, ssm_norm_w,
                norm_post_w, reduce=None):
    dtb, alog = _pad_lanes(dt_bias), _pad_lanes(a_log)
    d_b = jnp.repeat(d_skip, 64, axis=1)

    if isinstance(late, _LateGather):
        (proj, u), (gout, gcw) = _inproj_fwd(x2, norm_pre_w, w_all, late)
        w_out_all = gout.reshape(2 * D, D)
        cw_all = jnp.concatenate([gcw[0], gcw[1], gcw[2], gcw[3]], axis=1)
    else:
        proj, u = _inproj_fwd(x2, norm_pre_w, w_all)
        w_out_all, cw_all = late
    mix, attn_pre, lse = _attn_fwd(proj, 1, _attn_fwd(proj, 4, _attn_fwd(proj, 16)), final=True)
    mix, y_save, states, conv_out = _ssm_fwd(proj, mix, cw_all, conv_b, dtb, alog, d_b, ssm_norm_w)

    dy, dn_ssm, do, delta, dg, dw_out, dnw_post, loss_part = _outproj_loss(mix, w_out_all, x2, tgt, norm_post_w,
                                                                          attn_pre, proj)
    dz, dxbcdt, dcw, dcb, dvec, dnw_ssm = _ssm_bwd(proj, dn_ssm, y_save, states, conv_out, cw_all, dtb, alog, d_b,
                                                   ssm_norm_w)
    dw_gz, dw_x = _dw_pair(u, dg, dz, "dw_in_gz"), _dw(u, dxbcdt, "dw_in_xbcdt", X_COLS)
    acc = _attn_bwd(proj, do, lse, delta, 16, None, F32, reduce.pairs(dw_gz, dw_x, dw_out) if reduce else None)
    if reduce:
        acc, got = acc
    acc = _attn_bwd(proj, do, lse, delta, 4, acc, F32, reduce.first(got) if reduce else None)
    if reduce:
        acc, got = acc
        reduce.first_done(got)
    dq, dk, dv = _attn_bwd(proj, do, lse, delta, 1, acc, BF16)
    dw_qk = _dw_pair(u, dq, dk, "dw_in_qk")
    dw_v = _dw(u, dv, "dw_in_v", hosted=reduce.second_pairs(dw_qk, dw_gz) if reduce else None)
    if reduce:
        dw_v, got = dw_v

    def small(dnw_pre):
        return _pack_small(dcw, dnw_pre, dcb, dnw_ssm, dnw_post, dvec[0:1, :NH], dvec[1:2, :NH], dvec[2:3, :NH],
                           loss_part[:, :1])

    res = _inproj_bwd_dx([dq, dk, dv, dg, dz], dxbcdt, w_all, x2, dy, norm_pre_w,
                         reduce.second(dw_v, got, small(jnp.zeros((1, D), F32))) if reduce else None)
    if reduce:
        res, got = res
        reduce.second_done(got)
        return res
    grad_x, dnw_pre = res
    dw_all = jnp.concatenate([dw_qk, dw_v, dw_gz, dw_x], axis=1)
    return grad_x, small(dnw_pre), dw_all, dw_out
```
